```python
import jax, jax.numpy as jnp
from jax import lax
import numpy as np

D_MODEL = 1024
BATCH = 8
SEQ = 8192
DEPTH = 1

HEAD_DIM = 64
ATTN_WIDTH = 3 * D_MODEL // 4
N_ATTN_HEADS = ATTN_WIDTH // HEAD_DIM
CONV_WIDTH = D_MODEL - ATTN_WIDTH
MIX_WIDTH = ATTN_WIDTH + CONV_WIDTH
IN_WIDTH = 4 * ATTN_WIDTH + 4 * CONV_WIDTH
CONV_K = 3
DILATED_PATTERNS = ((128, 1), (512, 4), (2048, 16))
BLOCK = 128
ROPE_THETA = 10000.0
NORM_EPS = 1e-6

kernel_name = "hybrid_dilated_attn_shortconv_layer"


def rmsnorm(x, g):
    xf = x.astype(jnp.float32)
    y = xf * lax.rsqrt(jnp.mean(xf * xf, axis=-1, keepdims=True) + NORM_EPS)
    return (y * g.astype(jnp.float32)).astype(x.dtype)


def rope(t, pos):
    half = t.shape[-1] // 2
    inv_freq = ROPE_THETA ** (-jnp.arange(half, dtype=jnp.float32) * 2.0 / t.shape[-1])
    ang = pos.astype(jnp.float32)[:, None] * inv_freq[None, :]
    cos = jnp.cos(ang)[None, :, None, :]
    sin = jnp.sin(ang)[None, :, None, :]
    tf = t.astype(jnp.float32)
    t1, t2 = tf[..., :half], tf[..., half:]
    out = jnp.concatenate([t1 * cos - t2 * sin, t2 * cos + t1 * sin], axis=-1)
    return out.astype(t.dtype)


def banded_window_attn(q, k, v, window_keys):
    N, L, H, hd = q.shape
    Lp = -(-L // BLOCK) * BLOCK
    padw = ((0, 0), (0, Lp - L), (0, 0), (0, 0))
    q, k, v = jnp.pad(q, padw), jnp.pad(k, padw), jnp.pad(v, padw)
    nb = Lp // BLOCK
    qb = q.reshape(N, nb, BLOCK, H, hd)
    kb = k.reshape(N, nb, BLOCK, H, hd)
    vb = v.reshape(N, nb, BLOCK, H, hd)
    shift = ((0, 0), (1, 0), (0, 0), (0, 0), (0, 0))
    kk = jnp.concatenate([jnp.pad(kb, shift)[:, :-1], kb], axis=2)
    vv = jnp.concatenate([jnp.pad(vb, shift)[:, :-1], vb], axis=2)
    s = jnp.einsum('nbqhd,nbkhd->nbhqk', qb, kk).astype(jnp.float32) * (hd ** -0.5)
    qi = jnp.arange(BLOCK)[:, None]
    kj = jnp.arange(2 * BLOCK)[None, :]
    dist = qi + BLOCK - kj
    key_abs = jnp.arange(nb)[:, None, None] * BLOCK + kj[None] - BLOCK
    valid = (dist >= 0)[None] & (dist <= window_keys)[None] & (key_abs >= 0)
    s = jnp.where(valid[None, :, None], s, -jnp.inf)
    m = jnp.max(s, axis=-1, keepdims=True)
    p = jnp.exp(s - m)
    den = jnp.sum(p, axis=-1, keepdims=True)
    lse = (m + jnp.log(den))[..., 0]
    o = jnp.einsum('nbhqk,nbkhd->nbqhd', (p / den).astype(v.dtype), vv)
    o = o.reshape(N, Lp, H, hd)[:, :L]
    lse = lse.transpose(0, 1, 3, 2).reshape(N, Lp, H)[:, :L]
    return o, lse


def dilated_window_attn(q, k, v, window, dilation):
    B, S, H, hd = q.shape
    L = S // dilation

    def to_res(t):
        return t.reshape(B, L, dilation, H, hd).transpose(0, 2, 1, 3, 4).reshape(B * dilation, L, H, hd)

    o, lse = banded_window_attn(to_res(q), to_res(k), to_res(v), window // dilation)
    o = o.reshape(B, dilation, L, H, hd).transpose(0, 2, 1, 3, 4).reshape(B, S, H, hd)
    lse = lse.reshape(B, dilation, L, H).transpose(0, 2, 1, 3).reshape(B, S, H)
    return o, lse


def causal_depthwise_conv(u, w):
    S = u.shape[1]
    up = jnp.pad(u, ((0, 0), (CONV_K - 1, 0), (0, 0)))
    y = up[:, 0:S] * w[0]
    for j in range(1, CONV_K):
        y = y + up[:, j:j + S] * w[j]
    return y


def _fwd_setup_inputs(seed: int = 0) -> dict:
    key = jax.random.key(seed)
    ks = jax.random.split(key, 6)
    x = jax.random.normal(ks[0], (BATCH, SEQ, D_MODEL), jnp.float32)
    norm_pre_g = 1.0 + 0.01 * jax.random.normal(ks[1], (D_MODEL,), jnp.float32)
    w_in = jax.random.normal(ks[2], (D_MODEL, IN_WIDTH), jnp.float32) * D_MODEL ** -0.5
    conv_w = jax.random.normal(ks[3], (CONV_K, CONV_WIDTH), jnp.float32) * CONV_K ** -0.5
    w_out = jax.random.normal(ks[4], (MIX_WIDTH, D_MODEL), jnp.float32) * MIX_WIDTH ** -0.5
    norm_post_g = 1.0 + 0.01 * jax.random.normal(ks[5], (D_MODEL,), jnp.float32)
    return {"x": x, "norm_pre_g": norm_pre_g, "w_in": w_in, "conv_w": conv_w,
            "w_out": w_out, "norm_post_g": norm_post_g}


def _fwd_reference(x, norm_pre_g, w_in, conv_w, w_out, norm_post_g):
    B, S, _ = x.shape
    pos = jnp.arange(S)
    for _layer in range(DEPTH):
        h = rmsnorm(x, norm_pre_g)
        z = jnp.einsum('bsd,de->bse', h, w_in)
        A, C = ATTN_WIDTH, CONV_WIDTH
        cuts = np.cumsum([A, A, A, A, C, C, C])
        q, k, v, g_attn, c_h, c_b, c_c, g_conv = jnp.split(z, cuts, axis=-1)

        q = rope(q.reshape(B, S, N_ATTN_HEADS, HEAD_DIM), pos)
        k = rope(k.reshape(B, S, N_ATTN_HEADS, HEAD_DIM), pos)
        v = v.reshape(B, S, N_ATTN_HEADS, HEAD_DIM)
        outs, lses = [], []
        for window, dilation in DILATED_PATTERNS:
            o_i, lse_i = dilated_window_attn(q, k, v, window, dilation)
            outs.append(o_i)
            lses.append(lse_i)
        mix_w = jax.nn.softmax(jnp.stack(lses, axis=0), axis=0)
        o = jnp.sum(mix_w[..., None] * jnp.stack(outs, axis=0).astype(jnp.float32), axis=0)
        attn_out = o.astype(x.dtype).reshape(B, S, ATTN_WIDTH) * jax.nn.silu(g_attn)

        conv_out = c_b * causal_depthwise_conv(c_c * c_h, conv_w.astype(x.dtype))
        conv_out = conv_out * jax.nn.silu(g_conv)

        mixed = jnp.concatenate([attn_out, conv_out], axis=-1)
        y = jnp.einsum('bse,ed->bsd', mixed, w_out)
        x = x + rmsnorm(y, norm_post_g)
    return x


import jax as _jax
import jax.numpy as _jnp

TWIN_FORMAT = 'train_step'
FWD_PARAMS = ['x', 'norm_pre_g', 'w_in', 'conv_w', 'w_out', 'norm_post_g']
TWIN_WEIGHTS = ['norm_pre_g', 'w_in', 'conv_w', 'w_out', 'norm_post_g']
TWIN_DIFF_INPUT = 'x'
TWIN_INPUTS = ['x', 'norm_pre_g', 'w_in', 'conv_w', 'w_out', 'norm_post_g', 'loss_target', 'm_norm_pre_g', 'm_w_in', 'm_conv_w', 'm_w_out', 'm_norm_post_g', 'v_norm_pre_g', 'v_w_in', 'v_conv_w', 'v_w_out', 'v_norm_post_g']
TWIN_OUTPUTS = ['loss', 'grad_x', 'grad_norm_pre_g', 'grad_w_in', 'grad_conv_w', 'grad_w_out', 'grad_norm_post_g', 'delta_norm_pre_g', 'delta_w_in', 'delta_conv_w', 'delta_w_out', 'delta_norm_post_g', 'new_m_norm_pre_g', 'new_m_w_in', 'new_m_conv_w', 'new_m_w_out', 'new_m_norm_post_g', 'new_v_norm_pre_g', 'new_v_w_in', 'new_v_conv_w', 'new_v_w_out', 'new_v_norm_post_g']
TWIN_LEAF_KINDS = {'loss': 'loss', 'grad_x': 'grad_x', 'grad_norm_pre_g': 'grad_w', 'grad_w_in': 'grad_w', 'grad_conv_w': 'grad_w', 'grad_w_out': 'grad_w', 'grad_norm_post_g': 'grad_w', 'delta_norm_pre_g': 'delta_w', 'delta_w_in': 'delta_w', 'delta_conv_w': 'delta_w', 'delta_w_out': 'delta_w', 'delta_norm_post_g': 'delta_w', 'new_m_norm_pre_g': 'new_m', 'new_m_w_in': 'new_m', 'new_m_conv_w': 'new_m', 'new_m_w_out': 'new_m', 'new_m_norm_post_g': 'new_m', 'new_v_norm_pre_g': 'new_v', 'new_v_w_in': 'new_v', 'new_v_conv_w': 'new_v', 'new_v_w_out': 'new_v', 'new_v_norm_post_g': 'new_v'}


def _forward(args):
    return _fwd_reference(*[args[k] for k in FWD_PARAMS])


def _output_shape():
    def fwd():
        inp = _fwd_setup_inputs(0)
        return _fwd_reference(*[inp[k] for k in FWD_PARAMS])
    out = _jax.eval_shape(fwd)
    return out.shape, out.dtype

N_MICROBATCH = 1
ADAM_LR = 0.001
ADAM_B1 = 0.9
ADAM_B2 = 0.999
ADAM_EPS = 1e-08
ADAM_WD = 0.01
ADAM_STEP = 10
PER_EXAMPLE_BATCH_AXIS = {'x': 0, 'loss_target': 0}
SHARED_INPUTS = []
_WEIGHT_DTYPES = {'norm_pre_g': _jnp.float32, 'w_in': _jnp.float32, 'conv_w': _jnp.float32, 'w_out': _jnp.float32, 'norm_post_g': _jnp.float32}
MOMENT_SCALE = {'norm_pre_g': 8.613104e-01, 'w_in': 3.855561e-01, 'conv_w': 7.505597e-01, 'w_out': 3.577796e-01, 'norm_post_g': 6.427942e+01}


def _to_microbatches(a, axis):
    t = _jnp.moveaxis(a, axis, 0)
    t = t.reshape((N_MICROBATCH, t.shape[0] // N_MICROBATCH) + t.shape[1:])
    return _jnp.moveaxis(t, 1, axis + 1)


def setup_inputs(seed: int = 0) -> dict:
    inp = _fwd_setup_inputs(seed)
    key = _jax.random.fold_in(_jax.random.key(seed), 7919)
    shape, _ = _output_shape()
    out = dict(inp)
    out["loss_target"] = _jax.random.normal(_jax.random.fold_in(key, 0), shape, _jnp.float32)
    for i, name in enumerate(TWIN_WEIGHTS):
        w = inp[name].astype(_jnp.float32)
        if MOMENT_SCALE is None:
            s = _jnp.sqrt(_jnp.mean(_jnp.square(w)) + 1e-30)
        else:
            s = MOMENT_SCALE[name]
        km, kv = _jax.random.split(_jax.random.fold_in(key, i + 1))
        out[name] = w
        out["m_" + name] = s * _jax.random.normal(km, w.shape, _jnp.float32)
        out["v_" + name] = (s * s) * _jax.random.uniform(kv, w.shape, _jnp.float32, 0.5, 1.5)
    if N_MICROBATCH > 1:
        for name, axis in PER_EXAMPLE_BATCH_AXIS.items():
            out[name] = _to_microbatches(out[name], axis)
    return {'x': out['x'], 'norm_pre_g': out['norm_pre_g'], 'w_in': out['w_in'], 'conv_w': out['conv_w'], 'w_out': out['w_out'], 'norm_post_g': out['norm_post_g'], 'loss_target': out['loss_target'], 'm_norm_pre_g': out['m_norm_pre_g'], 'm_w_in': out['m_w_in'], 'm_conv_w': out['m_conv_w'], 'm_w_out': out['m_w_out'], 'm_norm_post_g': out['m_norm_post_g'], 'v_norm_pre_g': out['v_norm_pre_g'], 'v_w_in': out['v_w_in'], 'v_conv_w': out['v_conv_w'], 'v_w_out': out['v_w_out'], 'v_norm_post_g': out['v_norm_post_g']}


def _loss(weights, diff, rest, loss_target):
    with _jax.named_scope("forward"):
        args = {**rest, TWIN_DIFF_INPUT: diff, **{k: w.astype(_WEIGHT_DTYPES[k]) for k, w in weights.items()}}
        y = _forward(args)
    with _jax.named_scope("loss_head"):
        err = _jnp.square(y.astype(_jnp.float32) - loss_target)
        return 0.5 * _jnp.sum(_jnp.mean(err, axis=-1)) if err.ndim else 0.5 * err


def _adamw(w, g, m, v):
    m = ADAM_B1 * m + (1.0 - ADAM_B1) * g
    v = ADAM_B2 * v + (1.0 - ADAM_B2) * _jnp.square(g)
    m_hat = m / (1.0 - ADAM_B1 ** ADAM_STEP)
    v_hat = v / (1.0 - ADAM_B2 ** ADAM_STEP)
    delta = -ADAM_LR * (m_hat / (_jnp.sqrt(v_hat) + ADAM_EPS) + ADAM_WD * w)
    return delta, m, v


def reference(x, norm_pre_g, w_in, conv_w, w_out, norm_post_g, loss_target, m_norm_pre_g, m_w_in, m_conv_w, m_w_out, m_norm_post_g, v_norm_pre_g, v_w_in, v_conv_w, v_w_out, v_norm_post_g):
    given = dict(x=x, norm_pre_g=norm_pre_g, w_in=w_in, conv_w=conv_w, w_out=w_out, norm_post_g=norm_post_g, loss_target=loss_target, m_norm_pre_g=m_norm_pre_g, m_w_in=m_w_in, m_conv_w=m_conv_w, m_w_out=m_w_out, m_norm_post_g=m_norm_post_g, v_norm_pre_g=v_norm_pre_g, v_w_in=v_w_in, v_conv_w=v_conv_w, v_w_out=v_w_out, v_norm_post_g=v_norm_post_g)
    weights = {n: given[n] for n in TWIN_WEIGHTS}
    shared = {n: given[n] for n in SHARED_INPUTS}
    per_example = {n: given[n] for n in ['x']}
    grad_fn = _jax.value_and_grad(_loss, argnums=(0, 1))

    def one_microbatch(ex, loss_target):
        ex = dict(ex)
        diff = ex.pop(TWIN_DIFF_INPUT)
        return grad_fn(weights, diff, {**shared, **ex}, loss_target)

    if N_MICROBATCH == 1:
        loss, (grad_w, grad_x) = one_microbatch(per_example, given["loss_target"])
    else:
        def body(carry, xs):
            loss_sum, grad_sum = carry
            l_k, (gw_k, gx_k) = one_microbatch(xs[0], xs[1])
            with _jax.named_scope("update"):
                return (loss_sum + l_k, _jax.tree.map(_jnp.add, grad_sum, gw_k)), gx_k

        init = (_jnp.zeros((), _jnp.float32), _jax.tree.map(_jnp.zeros_like, weights))
        (loss, grad_w), grad_x = _jax.lax.scan(body, init, (per_example, given["loss_target"]))
    with _jax.named_scope("update"):
        delta_w, new_m, new_v = {}, {}, {}
        for n in TWIN_WEIGHTS:
            delta_w[n], new_m[n], new_v[n] = _adamw(weights[n], grad_w[n], given["m_" + n], given["v_" + n])
    return (loss, grad_x, *[grad_w[n] for n in TWIN_WEIGHTS], *[delta_w[n] for n in TWIN_WEIGHTS],
            *[new_m[n] for n in TWIN_WEIGHTS], *[new_v[n] for n in TWIN_WEIGHTS])
```

```python
import functools

import jax
import jax.numpy as jnp
from jax import lax
from jax.experimental import pallas as pl
from jax.experimental.pallas import tpu as pltpu

HEAD_DIM = 64
LANES = 128
SUBLANES = 8
BLOCK = 128
WINDOW_KEYS = 128
DILATIONS = (1, 4, 16)
CONV_K = 3
ROPE_THETA = 10000.0
NORM_EPS = 1e-6
ATTN_SCALE = HEAD_DIM ** -0.5
NEG = -1e30
N_CHIPS = 4
N_DEV = 8
MESH = pl.DeviceIdType.MESH
ADAM_LR = 0.001
ADAM_B1 = 0.9
ADAM_B2 = 0.999
ADAM_EPS = 1e-08
ADAM_WD = 0.01
ADAM_STEP = 10
VMEM_LIMIT = 52 * 1024 * 1024

F32 = jnp.float32
BF16 = jnp.bfloat16


def _params(sem=None, **kw):
    return pltpu.CompilerParams(dimension_semantics=sem, vmem_limit_bytes=VMEM_LIMIT, **kw)


def _sigmoid(z):
    return 1.0 / (1.0 + jnp.exp(-z))


def _rowgroup_sum(a):
    rows, n = a.shape
    return a.reshape(rows // SUBLANES, SUBLANES, n).sum(axis=0)


def _nt(a, b):
    return lax.dot_general(a, b, (((1,), (1,)), ((), ())), preferred_element_type=F32)


def _tn(a, b):
    return lax.dot_general(a, b, (((0,), (0,)), ((), ())), preferred_element_type=F32)


def _col_pieces(a, b, width):
    out = []
    while a < b:
        j = a // width
        e = min(b, (j + 1) * width)
        out.append((j, a - j * width, e - j * width))
        a = e
    return out


def _flip(a, f):
    return 1 - a if f else a


def _chip_peers(x, y):
    return [(1 - x, y), (x, 1 - y), (1 - x, 1 - y)]


def gather_weights(w_in, w_out, conv_w):
    d_model, width = w_in.shape
    rows = w_out.shape[0]
    cw = jnp.zeros((SUBLANES, LANES), F32).at[:CONV_K, :conv_w.shape[1]].set(conv_w)

    def body(win_ref, wout_ref, cw_ref, winf_ref, woutf_ref, cwf_ref, st_in, st_out, send_sems, recv_sems):
        x, y, c = lax.axis_index("x"), lax.axis_index("y"), lax.axis_index("c")
        me = 2 * x + y
        st_in[...] = win_ref[...].astype(BF16)
        st_out[...] = wout_ref[...].astype(BF16)
        winf_ref[me] = st_in[...]
        woutf_ref[me] = st_out[...]
        cwf_ref[me] = cw_ref[...]
        srcs = (st_in, st_out, cw_ref)
        dsts = (winf_ref, woutf_ref, cwf_ref)

        def copy(k, t, slot, to):
            return pltpu.make_async_remote_copy(
                src_ref=srcs[t], dst_ref=dsts[t].at[slot], send_sem=send_sems.at[k, t], recv_sem=recv_sems.at[k, t],
                device_id=to, device_id_type=MESH)

        peers = _chip_peers(x, y)
        sends = [copy(k, t, me, (px, py, c)) for k, (px, py) in enumerate(peers) for t in range(3)]
        for cp in sends:
            cp.start()
        for k, (px, py) in enumerate(peers):
            for t in range(3):
                copy(k, t, 2 * px + py, (px, py, c)).wait_recv()
        for cp in sends:
            cp.wait_send()

    vm = pl.BlockSpec(memory_space=pltpu.VMEM)
    return pl.pallas_call(
        body, name="gather_weights",
        out_shape=(jax.ShapeDtypeStruct((N_CHIPS, d_model, width), BF16),
                   jax.ShapeDtypeStruct((N_CHIPS, rows, d_model), BF16),
                   jax.ShapeDtypeStruct((N_CHIPS, SUBLANES, LANES), F32)),
        in_specs=[vm, vm, vm], out_specs=(vm, vm, vm),
        scratch_shapes=[pltpu.VMEM((d_model, width), BF16), pltpu.VMEM((rows, d_model), BF16),
                        pltpu.SemaphoreType.DMA((3, 3)), pltpu.SemaphoreType.DMA((3, 3))],
        compiler_params=_params(),
    )(w_in, w_out, cw)


def _rope_tables(seq):
    half = HEAD_DIM // 2
    inv_freq = ROPE_THETA ** (-jnp.arange(half, dtype=F32) * 2.0 / HEAD_DIM)
    ang = jnp.arange(seq).astype(F32)[:, None] * inv_freq[None, :]
    cos, sin = jnp.cos(ang), jnp.sin(ang)
    zero = jnp.zeros_like(sin)
    return (jnp.concatenate([cos, cos, cos, cos], axis=1),
            jnp.concatenate([-sin, zero, -sin, zero], axis=1),
            jnp.concatenate([zero, sin, zero, sin], axis=1))


def _rope(t, cos, s1, s2):
    return t * cos + pltpu.roll(t, LANES - HEAD_DIM // 2, 1) * s1 + pltpu.roll(t, HEAD_DIM // 2, 1) * s2


def _rope_transposed(g, cos, s1, s2):
    return g * cos + pltpu.roll(g * s1, HEAD_DIM // 2, 1) + pltpu.roll(g * s2, LANES - HEAD_DIM // 2, 1)


def inproj(x, g1, w_full, tables, attn_w, conv_w):
    seq, d_model = x.shape
    width = w_full.shape[2]
    tm = 256

    def body(x_ref, g_ref, w_ref, cos_ref, s1_ref, s2_ref, h_ref, q_ref, k_ref, v_ref, ga_ref, cz_ref):
        xv = x_ref[...]
        hb = ((xv * lax.rsqrt(jnp.mean(xv * xv, axis=-1, keepdims=True) + NORM_EPS)) * g_ref[...]).astype(BF16)
        h_ref[...] = hb
        cos, s1, s2 = cos_ref[...], s1_ref[...], s2_ref[...]

        def proj(a, b):
            parts = [jnp.dot(hb, w_ref[j, :, lo:hi], preferred_element_type=F32) for j, lo, hi in _col_pieces(a, b, width)]
            return parts[0] if len(parts) == 1 else jnp.concatenate(parts, axis=1)

        zq = proj(0, attn_w)
        for g in range(attn_w // LANES):
            sl = slice(g * LANES, (g + 1) * LANES)
            q_ref[:, sl] = (_rope(zq[:, sl], cos, s1, s2) * ATTN_SCALE).astype(BF16)
        zk = proj(attn_w, 2 * attn_w)
        for g in range(attn_w // LANES):
            sl = slice(g * LANES, (g + 1) * LANES)
            k_ref[:, sl] = _rope(zk[:, sl], cos, s1, s2).astype(BF16)
        v_ref[...] = proj(2 * attn_w, 3 * attn_w).astype(BF16)
        ga_ref[...] = proj(3 * attn_w, 4 * attn_w)
        cz_ref[...] = proj(4 * attn_w, 4 * attn_w + 4 * conv_w)

    row = lambda n: pl.BlockSpec((tm, n), lambda i: (i, 0))
    return pl.pallas_call(
        body, name="inproj", grid=(seq // tm,),
        out_shape=(jax.ShapeDtypeStruct((seq, d_model), BF16),
                   jax.ShapeDtypeStruct((seq, attn_w), BF16), jax.ShapeDtypeStruct((seq, attn_w), BF16),
                   jax.ShapeDtypeStruct((seq, attn_w), BF16), jax.ShapeDtypeStruct((seq, attn_w), F32),
                   jax.ShapeDtypeStruct((seq, 4 * conv_w), F32)),
        in_specs=[row(d_model), pl.BlockSpec((1, d_model), lambda i: (0, 0)),
                  pl.BlockSpec(w_full.shape, lambda i: (0, 0, 0)), row(LANES), row(LANES), row(LANES)],
        out_specs=(row(d_model), row(attn_w), row(attn_w), row(attn_w), row(attn_w), row(4 * conv_w)),
        compiler_params=_params(("arbitrary",)),
    )(x, g1, w_full, *tables)


def _head_masks():
    lane = lax.broadcasted_iota(jnp.int32, (BLOCK, LANES), 1)
    lo = lane < HEAD_DIM
    return lane, lo, jnp.where(lo, 1.0, 0.0).astype(BF16), jnp.where(lo, 0.0, 1.0).astype(BF16)


def _column(blk, lane, h):
    return jnp.sum(jnp.where(lane == h, blk, 0.0), axis=1, keepdims=True)


def attn_fwd(q, k, v, dil, run):
    seq, attn_w = q.shape
    rows = seq // dil
    nb = rows // BLOCK
    pairs = attn_w // LANES
    first = run is None
    view = lambda a: a.reshape(rows, dil * a.shape[1])

    def body(*refs):
        if first:
            q_ref, kp_ref, kc_ref, vp_ref, vc_ref, o_ref, l_ref = refs
        else:
            q_ref, kp_ref, kc_ref, vp_ref, vc_ref, oin_ref, lin_ref, o_ref, l_ref = refs
        n = pl.program_id(1)
        qi = lax.broadcasted_iota(jnp.int32, (BLOCK, 2 * BLOCK), 0)
        kj = lax.broadcasted_iota(jnp.int32, (BLOCK, 2 * BLOCK), 1)
        rel = kj - qi
        valid = (rel >= 0) & (rel <= WINDOW_KEYS) & ((kj >= BLOCK) | (n > 0))
        bias = jnp.where(valid, 0.0, NEG)
        lane, lo, m_lo, m_hi = _head_masks()
        lblk = jnp.zeros((BLOCK, LANES), F32)
        lin = None if first else lin_ref[...]
        for p in range(pairs):
            sl = slice(p * LANES, (p + 1) * LANES)
            q2 = q_ref[:, sl]
            kcat = jnp.concatenate([kp_ref[:, sl], kc_ref[:, sl]], axis=0)
            vcat = jnp.concatenate([vp_ref[:, sl], vc_ref[:, sl]], axis=0)
            halves = []
            for e, msk in enumerate((m_lo, m_hi)):
                h = 2 * p + e
                s = _nt(q2 * msk, kcat) + bias
                m = jnp.max(s, axis=1, keepdims=True)
                pe = jnp.exp(s - m)
                l = jnp.sum(pe, axis=1, keepdims=True)
                o_e = jnp.dot(pe.astype(BF16), vcat, preferred_element_type=F32) / l
                lse = m + jnp.log(l)
                if not first:
                    lp = _column(lin, lane, h)
                    mx = jnp.maximum(lp, lse)
                    new = mx + jnp.log(jnp.exp(lp - mx) + jnp.exp(lse - mx))
                    o_e = jnp.exp(lp - new) * oin_ref[:, sl] + jnp.exp(lse - new) * o_e
                    lse = new
                halves.append(o_e)
                lblk = jnp.where(lane == h, lse, lblk)
            o_ref[:, sl] = jnp.where(lo, halves[0], halves[1])
        l_ref[...] = lblk

    cur = lambda w: pl.BlockSpec((BLOCK, w), lambda r, n: (n, r))
    prev = lambda w: pl.BlockSpec((BLOCK, w), lambda r, n: (jnp.maximum(n - 1, 0), r))
    ins = [view(q), view(k), view(k), view(v), view(v)]
    specs = [cur(attn_w), prev(attn_w), cur(attn_w), prev(attn_w), cur(attn_w)]
    if not first:
        ins += [view(run[0]), view(run[1])]
        specs += [cur(attn_w), cur(LANES)]
    o, lse = pl.pallas_call(
        body, name=f"attn_fwd_d{dil}", grid=(dil, nb),
        out_shape=(jax.ShapeDtypeStruct((rows, dil * attn_w), F32), jax.ShapeDtypeStruct((rows, dil * LANES), F32)),
        in_specs=specs, out_specs=(cur(attn_w), cur(LANES)),
        compiler_params=_params(("arbitrary", "arbitrary")),
    )(*ins)
    return o.reshape(seq, attn_w), lse.reshape(seq, LANES)


def attn_bwd(q, k, v, d_o, lse, delta, dil, run):
    seq, attn_w = q.shape
    rows = seq // dil
    nb = rows // BLOCK
    pairs = attn_w // LANES
    first = run is None
    view = lambda a: a.reshape(rows, dil * a.shape[1])

    def body(*refs):
        q_ref, kp_ref, kc_ref, vp_ref, vc_ref, do_ref, lse_ref, dl_ref = refs[:8]
        if first:
            dq_ref, dk_ref, dv_ref, ck, cv = refs[8:]
        else:
            dqi_ref, dki_ref, dvi_ref, dq_ref, dk_ref, dv_ref, ck, cv = refs[8:]
        n = pl.program_id(1)

        @pl.when(n == 0)
        def _():
            ck[...] = jnp.zeros_like(ck)
            cv[...] = jnp.zeros_like(cv)

        @pl.when(n < nb)
        def _():
            kj = lax.broadcasted_iota(jnp.int32, (2 * BLOCK, BLOCK), 0)
            qi = lax.broadcasted_iota(jnp.int32, (2 * BLOCK, BLOCK), 1)
            rel = kj - qi
            valid = (rel >= 0) & (rel <= WINDOW_KEYS) & ((kj >= BLOCK) | (n > 0))
            bias = jnp.where(valid, 0.0, NEG)
            _, lo, m_lo, m_hi = _head_masks()
            lse_t = jnp.transpose(lse_ref[...])
            dl_t = jnp.transpose(dl_ref[...])
            for p in range(pairs):
                sl = slice(p * LANES, (p + 1) * LANES)
                q2, do2 = q_ref[:, sl], do_ref[:, sl]
                kcat = jnp.concatenate([kp_ref[:, sl], kc_ref[:, sl]], axis=0)
                vcat = jnp.concatenate([vp_ref[:, sl], vc_ref[:, sl]], axis=0)
                dkc = jnp.zeros((2 * BLOCK, LANES), F32)
                dvc = jnp.zeros((2 * BLOCK, LANES), F32)
                dqs = []
                for e, msk in enumerate((m_lo, m_hi)):
                    h = 2 * p + e
                    qe, doe = q2 * msk, do2 * msk
                    p_t = jnp.exp(_nt(kcat, qe) + bias - lse_t[h:h + 1, :])
                    ds_t = p_t * (_nt(vcat, doe) - dl_t[h:h + 1, :])
                    dsb = ds_t.astype(BF16)
                    dkc = dkc + jnp.dot(dsb, qe, preferred_element_type=F32)
                    dvc = dvc + jnp.dot(p_t.astype(BF16), doe, preferred_element_type=F32)
                    dqs.append(_tn(dsb, kcat))
                dq2 = jnp.where(lo, dqs[0], dqs[1]) * ATTN_SCALE
                dk2 = ck[:, sl] + dkc[:BLOCK]
                dv2 = cv[:, sl] + dvc[:BLOCK]
                if not first:
                    dq2 = dq2 + dqi_ref[:, sl]
                    dk2 = dk2 + dki_ref[:, sl]
                    dv2 = dv2 + dvi_ref[:, sl]
                dq_ref[:, sl] = dq2
                dk_ref[:, sl] = dk2
                dv_ref[:, sl] = dv2
                ck[:, sl] = dkc[BLOCK:]
                cv[:, sl] = dvc[BLOCK:]

        @pl.when(n == nb)
        def _():
            if first:
                dk_ref[...] = ck[...]
                dv_ref[...] = cv[...]
            else:
                dk_ref[...] = ck[...] + dki_ref[...]
                dv_ref[...] = cv[...] + dvi_ref[...]

    last = nb - 1
    cur = lambda w: pl.BlockSpec((BLOCK, w), lambda r, n: (jnp.minimum(n, last), r))
    prev = lambda w: pl.BlockSpec((BLOCK, w), lambda r, n: (jnp.maximum(n - 1, 0), r))
    ins = [view(q), view(k), view(k), view(v), view(v), view(d_o), view(lse), view(delta)]
    specs = [cur(attn_w), prev(attn_w), cur(attn_w), prev(attn_w), cur(attn_w), cur(attn_w), cur(LANES), cur(LANES)]
    if not first:
        ins += [view(a) for a in run]
        specs += [cur(attn_w), prev(attn_w), prev(attn_w)]
    shp = jax.ShapeDtypeStruct((rows, dil * attn_w), F32)
    outs = pl.pallas_call(
        body, name=f"attn_bwd_d{dil}", grid=(dil, nb + 1),
        out_shape=(shp, shp, shp), in_specs=specs, out_specs=(cur(attn_w), prev(attn_w), prev(attn_w)),
        scratch_shapes=[pltpu.VMEM((BLOCK, attn_w), F32), pltpu.VMEM((BLOCK, attn_w), F32)],
        compiler_params=_params(("arbitrary", "arbitrary")),
    )(*ins)
    return tuple(a.reshape(seq, attn_w) for a in outs)


def _shift_down(u, halo, k):
    rolled = pltpu.roll(u, k, 0)
    row = lax.broadcasted_iota(jnp.int32, halo.shape, 0)
    top = jnp.where(row < k, pltpu.roll(halo, k, 0), rolled[:SUBLANES])
    return jnp.concatenate([top, rolled[SUBLANES:]], axis=0)


def _shift_up(u, halo, k):
    rows = u.shape[0]
    rolled = pltpu.roll(u, rows - k, 0)
    row = lax.broadcasted_iota(jnp.int32, halo.shape, 0)
    bot = jnp.where(row >= SUBLANES - k, pltpu.roll(halo, SUBLANES - k, 0), rolled[rows - SUBLANES:])
    return jnp.concatenate([rolled[:rows - SUBLANES], bot], axis=0)


def tail(o, ga, cz, x, tgt, w_out, g2, cw):
    seq, d_model = x.shape
    attn_w = o.shape[1]
    conv_w = cz.shape[1] // 4
    mix = attn_w + conv_w
    pairs = attn_w // LANES
    tm = 256
    nt = seq // tm
    hb = tm // SUBLANES

    def body(o_ref, ga_ref, cz_ref, hz_ref, x_ref, t_ref, w_ref, g_ref, cw_ref,
             do_ref, dl_ref, dga_ref, dcb_ref, dgc_ref, dcv_ref, e_ref, dw_ref, dg_ref, dcw_ref, loss_ref):
        i = pl.program_id(0)

        @pl.when(i == 0)
        def _():
            dw_ref[...] = jnp.zeros_like(dw_ref)
            dg_ref[...] = jnp.zeros_like(dg_ref)
            dcw_ref[...] = jnp.zeros_like(dcw_ref)
            loss_ref[...] = jnp.zeros_like(loss_ref)

        ov, gav = o_ref[...], ga_ref[...]
        sig_a = _sigmoid(gav)
        silu_a = gav * sig_a
        attn_out = ov * silu_a
        ch, cb = cz_ref[:, 0:conv_w], cz_ref[:, conv_w:2 * conv_w]
        cc, gc = cz_ref[:, 2 * conv_w:3 * conv_w], cz_ref[:, 3 * conv_w:4 * conv_w]
        u = cc * ch
        uh = hz_ref[:, 2 * conv_w:3 * conv_w] * hz_ref[:, 0:conv_w]
        uh = jnp.where(i > 0, uh, 0.0)
        u1 = _shift_down(u, uh, 1)
        u2 = _shift_down(u, uh, 2)
        w0, w1, w2 = cw_ref[0:1, :], cw_ref[1:2, :], cw_ref[2:3, :]
        cvv = u2 * w0 + u1 * w1 + u * w2
        sig_c = _sigmoid(gc)
        silu_c = gc * sig_c
        bc = cb * cvv
        conv_out = bc * silu_c
        mixed = jnp.concatenate([attn_out, conv_out], axis=1).astype(BF16)

        yv = jnp.dot(mixed, w_ref[...], preferred_element_type=F32)
        r2 = lax.rsqrt(jnp.mean(yv * yv, axis=-1, keepdims=True) + NORM_EPS)
        yhat = yv * r2
        gv = g_ref[...]
        diff = (x_ref[...] + yhat * gv) - t_ref[...]
        loss_ref[...] += _rowgroup_sum(diff * diff)
        ev = diff * (1.0 / d_model)
        e_ref[...] = ev
        dg_ref[...] += _rowgroup_sum(ev * yhat)
        eg = ev * gv
        dy = (r2 * (eg - yhat * jnp.mean(eg * yhat, axis=-1, keepdims=True))).astype(BF16)
        dw_ref[...] += _tn(mixed, dy)
        dm = _nt(dy, w_ref[...])
        dma, dmc = dm[:, :attn_w], dm[:, attn_w:]

        dov = dma * silu_a
        do_ref[...] = dov.astype(BF16)
        dga_ref[...] = (dma * ov * (sig_a * (1.0 + gav * (1.0 - sig_a)))).astype(BF16)
        prod = dov * ov
        lane = lax.broadcasted_iota(jnp.int32, (tm, LANES), 1)
        lo = lane < HEAD_DIM
        dblk = jnp.zeros((tm, LANES), F32)
        for p in range(pairs):
            pr = prod[:, p * LANES:(p + 1) * LANES]
            dblk = jnp.where(lane == 2 * p, jnp.sum(jnp.where(lo, pr, 0.0), axis=1, keepdims=True), dblk)
            dblk = jnp.where(lane == 2 * p + 1, jnp.sum(jnp.where(lo, 0.0, pr), axis=1, keepdims=True), dblk)
        dl_ref[...] = dblk

        dsc = dmc * silu_c
        dcb_ref[...] = (dsc * cvv).astype(BF16)
        dgc_ref[...] = (dmc * bc * (sig_c * (1.0 + gc * (1.0 - sig_c)))).astype(BF16)
        dcv = dsc * cb
        dcv_ref[...] = dcv
        dcw_ref[0:SUBLANES, :] += _rowgroup_sum(dcv * u2)
        dcw_ref[SUBLANES:2 * SUBLANES, :] += _rowgroup_sum(dcv * u1)
        dcw_ref[2 * SUBLANES:3 * SUBLANES, :] += _rowgroup_sum(dcv * u)

    row = lambda n: pl.BlockSpec((tm, n), lambda i: (i, 0))
    whole = lambda a, b: pl.BlockSpec((a, b), lambda i: (0, 0))
    return pl.pallas_call(
        body, name="tail", grid=(nt,),
        out_shape=(jax.ShapeDtypeStruct((seq, attn_w), BF16), jax.ShapeDtypeStruct((seq, LANES), F32),
                   jax.ShapeDtypeStruct((seq, attn_w), BF16), jax.ShapeDtypeStruct((seq, conv_w), BF16),
                   jax.ShapeDtypeStruct((seq, conv_w), BF16), jax.ShapeDtypeStruct((seq, conv_w), F32),
                   jax.ShapeDtypeStruct((seq, d_model), F32), jax.ShapeDtypeStruct((mix, d_model), F32),
                   jax.ShapeDtypeStruct((SUBLANES, d_model), F32), jax.ShapeDtypeStruct((CONV_K * SUBLANES, conv_w), F32),
                   jax.ShapeDtypeStruct((SUBLANES, d_model), F32)),
        in_specs=[row(attn_w), row(attn_w), row(4 * conv_w),
                  pl.BlockSpec((SUBLANES, 4 * conv_w), lambda i: (jnp.maximum(i * hb - 1, 0), 0)),
                  row(d_model), row(d_model), whole(mix, d_model), whole(1, d_model), whole(SUBLANES, conv_w)],
        out_specs=(row(attn_w), row(LANES), row(attn_w), row(conv_w), row(conv_w), row(conv_w), row(d_model),
                   whole(mix, d_model), whole(SUBLANES, d_model), whole(CONV_K * SUBLANES, conv_w),
                   whole(SUBLANES, d_model)),
        compiler_params=_params(("arbitrary",)),
    )(o, ga, cz, cz, x, tgt, w_out, g2, cw)


def dz_dx(dq, dk, dv, dga, dcb, dgc, dcv, cz, tables, x, g1, e, w_full, cw):
    seq, d_model = x.shape
    attn_w = dq.shape[1]
    conv_w = dcv.shape[1]
    width = w_full.shape[2]
    in_w = 4 * attn_w + 4 * conv_w
    tm = 256
    nt = seq // tm
    hb = tm // SUBLANES

    def body(dq_ref, dk_ref, dv_ref, dga_ref, dcb_ref, dgc_ref, dcv_ref, nh_ref, cz_ref, cos_ref, s1_ref, s2_ref,
             x_ref, g_ref, e_ref, w_ref, cw_ref, gx_ref, dz_ref, dg_ref):
        i = pl.program_id(0)

        @pl.when(i == 0)
        def _():
            dg_ref[...] = jnp.zeros_like(dg_ref)

        cos, s1, s2 = cos_ref[...], s1_ref[...], s2_ref[...]
        for g in range(attn_w // LANES):
            sl = slice(g * LANES, (g + 1) * LANES)
            dz_ref[:, sl] = _rope_transposed(dq_ref[:, sl], cos, s1, s2).astype(BF16)
            dz_ref[:, attn_w + g * LANES:attn_w + (g + 1) * LANES] = _rope_transposed(dk_ref[:, sl], cos, s1, s2).astype(BF16)
        dz_ref[:, 2 * attn_w:3 * attn_w] = dv_ref[...].astype(BF16)
        dz_ref[:, 3 * attn_w:4 * attn_w] = dga_ref[...]
        dcv = dcv_ref[...]
        nh = jnp.where(i < nt - 1, nh_ref[...], 0.0)
        w0, w1, w2 = cw_ref[0:1, :], cw_ref[1:2, :], cw_ref[2:3, :]
        du = dcv * w2 + _shift_up(dcv, nh, 1) * w1 + _shift_up(dcv, nh, 2) * w0
        base = 4 * attn_w
        dz_ref[:, base:base + conv_w] = (du * cz_ref[:, 2 * conv_w:3 * conv_w]).astype(BF16)
        dz_ref[:, base + conv_w:base + 2 * conv_w] = dcb_ref[...]
        dz_ref[:, base + 2 * conv_w:base + 3 * conv_w] = (du * cz_ref[:, 0:conv_w]).astype(BF16)
        dz_ref[:, base + 3 * conv_w:base + 4 * conv_w] = dgc_ref[...]

        dh = _nt(dz_ref[:, 0:width], w_ref[0])
        for j in range(1, N_CHIPS):
            dh = dh + _nt(dz_ref[:, j * width:(j + 1) * width], w_ref[j])
        xv = x_ref[...]
        r1 = lax.rsqrt(jnp.mean(xv * xv, axis=-1, keepdims=True) + NORM_EPS)
        xhat = xv * r1
        dg_ref[...] += _rowgroup_sum(dh * xhat)
        dhg = dh * g_ref[...]
        gx_ref[...] = r1 * (dhg - xhat * jnp.mean(dhg * xhat, axis=-1, keepdims=True)) + e_ref[...]

    row = lambda n: pl.BlockSpec((tm, n), lambda i: (i, 0))
    whole = lambda a, b: pl.BlockSpec((a, b), lambda i: (0, 0))
    return pl.pallas_call(
        body, name="dz_dx", grid=(nt,),
        out_shape=(jax.ShapeDtypeStruct((seq, d_model), F32), jax.ShapeDtypeStruct((seq, in_w), BF16),
                   jax.ShapeDtypeStruct((SUBLANES, d_model), F32)),
        in_specs=[row(attn_w), row(attn_w), row(attn_w), row(attn_w), row(conv_w), row(conv_w), row(conv_w),
                  pl.BlockSpec((SUBLANES, conv_w), lambda i: (jnp.minimum((i + 1) * hb, seq // SUBLANES - 1), 0)),
                  row(4 * conv_w), row(LANES), row(LANES), row(LANES), row(d_model), whole(1, d_model), row(d_model),
                  pl.BlockSpec(w_full.shape, lambda i: (0, 0, 0)), whole(SUBLANES, conv_w)],
        out_specs=(row(d_model), row(in_w), whole(SUBLANES, d_model)),
        compiler_params=_params(("arbitrary",)),
    )(dq, dk, dv, dga, dcb, dgc, dcv, dcv, cz, *tables, x, g1, e, w_full, cw)


def dw_in(h, dz):
    seq, d_model = h.shape
    half = dz.shape[1] // N_DEV
    ts = 512
    steps = seq // ts

    def body(h_ref, dz_ref, o_ref):
        @pl.when(pl.program_id(1) == 0)
        def _():
            o_ref[...] = jnp.zeros_like(o_ref)

        o_ref[0] += _tn(h_ref[...], dz_ref[...])

    return pl.pallas_call(
        body, name="dw_in", grid=(N_DEV, steps),
        out_shape=jax.ShapeDtypeStruct((N_DEV, d_model, half), F32),
        in_specs=[pl.BlockSpec((ts, d_model), lambda j, s: (s, 0)), pl.BlockSpec((ts, half), lambda j, s: (s, j))],
        out_specs=pl.BlockSpec((1, d_model, half), lambda j, s: (j, 0, 0)),
        compiler_params=_params(("arbitrary", "arbitrary")),
    )(h, dz)


def grad_reduce(g_in, g_out, small):
    gi = g_in.reshape(N_CHIPS, 2, *g_in.shape[1:])
    go = g_out.reshape(N_CHIPS, 2, *g_out.shape[1:])
    shp_i, shp_o = gi.shape[2:], go.shape[2:]

    def body(gi_ref, go_ref, sm_ref, ri_ref, ro_ref, rs_ref, a_i, b_i, c_i, a_o, b_o, c_o, sbuf,
             loc_sems, sa, ra, sb, rb, sc, rc, ss, rs):
        x, y, c = lax.axis_index("x"), lax.axis_index("y"), lax.axis_index("c")
        me = 2 * x + y
        sib = (x, y, 1 - c)
        srcs, mine, theirs, contrib, res = (gi_ref, go_ref), (a_i, a_o), (b_i, b_o), (c_i, c_o), (ri_ref, ro_ref)

        flips = [(fx, fy, fc) for fx in (0, 1) for fy in (0, 1) for fc in (0, 1)][1:]
        my8 = 4 * x + 2 * y + c
        sbuf[my8] = sm_ref[...]

        def small_copy(k, slot, to):
            return pltpu.make_async_remote_copy(src_ref=sm_ref, dst_ref=sbuf.at[slot], send_sem=ss.at[k], recv_sem=rs.at[k],
                                                device_id=to, device_id_type=MESH)

        small_sends = []
        for k, (fx, fy, fc) in enumerate(flips):
            px, py, pc = _flip(x, fx), _flip(y, fy), _flip(c, fc)
            small_sends.append(small_copy(k, my8, (px, py, pc)))
            small_sends[-1].start()

        def a_copy(t):
            return pltpu.make_async_remote_copy(src_ref=srcs[t].at[:, 1 - c], dst_ref=theirs[t], send_sem=sa.at[t],
                                                recv_sem=ra.at[t], device_id=sib, device_id_type=MESH)

        loads = [pltpu.make_async_copy(srcs[t].at[:, c], mine[t], loc_sems.at[t]) for t in range(2)]
        a_sends = [a_copy(t) for t in range(2)]
        for cp in loads + a_sends:
            cp.start()
        for t in range(2):
            loads[t].wait()
            a_copy(t).wait_recv()
            mine[t][...] = mine[t][...] + theirs[t][...]

        def b_copy(k, t, piece, slot, to):
            return pltpu.make_async_remote_copy(src_ref=mine[t].at[piece], dst_ref=contrib[t].at[slot], send_sem=sb.at[k, t],
                                                recv_sem=rb.at[k, t], device_id=to, device_id_type=MESH)

        peers = _chip_peers(x, y)
        b_sends = [b_copy(k, t, 2 * px + py, me, (px, py, c)) for k, (px, py) in enumerate(peers) for t in range(2)]
        for cp in b_sends:
            cp.start()
        for t in range(2):
            contrib[t][me] = mine[t][me]
        for k, (px, py) in enumerate(peers):
            for t in range(2):
                b_copy(k, t, me, 2 * px + py, (px, py, c)).wait_recv()

        def c_copy(t, half):
            return pltpu.make_async_remote_copy(src_ref=res[t].at[half], dst_ref=res[t].at[half], send_sem=sc.at[t],
                                                recv_sem=rc.at[t], device_id=sib, device_id_type=MESH)

        c_sends = []
        for t in range(2):
            res[t][c] = ((contrib[t][0] + contrib[t][1]) + contrib[t][2]) + contrib[t][3]
            c_sends.append(c_copy(t, c))
            c_sends[-1].start()
        for t in range(2):
            c_copy(t, 1 - c).wait_recv()

        for k, (fx, fy, fc) in enumerate(flips):
            px, py, pc = _flip(x, fx), _flip(y, fy), _flip(c, fc)
            small_copy(k, 4 * px + 2 * py + pc, (px, py, pc)).wait_recv()
        tot = sbuf[0]
        for d in range(1, N_DEV):
            tot = tot + sbuf[d]
        rs_ref[...] = tot
        for cp in small_sends + a_sends + b_sends + c_sends:
            cp.wait_send()

    vm = pl.BlockSpec(memory_space=pltpu.VMEM)
    anyspace = pl.BlockSpec(memory_space=pl.ANY)
    dma = pltpu.SemaphoreType.DMA
    return pl.pallas_call(
        body, name="grad_reduce",
        out_shape=(jax.ShapeDtypeStruct((2, *shp_i), F32), jax.ShapeDtypeStruct((2, *shp_o), F32),
                   jax.ShapeDtypeStruct(small.shape, F32)),
        in_specs=[anyspace, anyspace, vm], out_specs=(vm, vm, vm),
        scratch_shapes=[pltpu.VMEM((N_CHIPS, *shp_i), F32), pltpu.VMEM((N_CHIPS, *shp_i), F32), pltpu.VMEM((N_CHIPS, *shp_i), F32),
                        pltpu.VMEM((N_CHIPS, *shp_o), F32), pltpu.VMEM((N_CHIPS, *shp_o), F32), pltpu.VMEM((N_CHIPS, *shp_o), F32),
                        pltpu.VMEM((N_DEV, *small.shape), F32),
                        dma((2,)), dma((2,)), dma((2,)), dma((3, 2)), dma((3, 2)), dma((2,)), dma((2,)),
                        dma((N_DEV - 1,)), dma((N_DEV - 1,))],
        compiler_params=_params(),
    )(gi, go, small)


def _adam_math(w, g, m, v):
    m = ADAM_B1 * m + (1.0 - ADAM_B1) * g
    v = ADAM_B2 * v + (1.0 - ADAM_B2) * (g * g)
    m_hat = m / (1.0 - ADAM_B1 ** ADAM_STEP)
    v_hat = v / (1.0 - ADAM_B2 ** ADAM_STEP)
    delta = -ADAM_LR * (m_hat / (jnp.sqrt(v_hat) + ADAM_EPS) + ADAM_WD * w)
    return delta, m, v


def adam_shard(name, w, g2, m, v, block, grid, w_map, g_map):
    def body(w_ref, g_ref, m_ref, v_ref, go_ref, d_ref, mo_ref, vo_ref):
        g = g_ref[0]
        delta, mn, vn = _adam_math(w_ref[...], g, m_ref[...], v_ref[...])
        go_ref[...] = g
        d_ref[...] = delta
        mo_ref[...] = mn
        vo_ref[...] = vn

    ws = pl.BlockSpec(block, w_map)
    shp = jax.ShapeDtypeStruct(w.shape, F32)
    return pl.pallas_call(
        body, name=name, grid=grid, out_shape=(shp, shp, shp, shp),
        in_specs=[ws, pl.BlockSpec((1, *block), g_map), ws, ws], out_specs=(ws, ws, ws, ws),
        compiler_params=_params(("arbitrary",) * len(grid)),
    )(w, g2, m, v)


def adam_small(ws, gs, ms, vs):
    n = len(ws)

    def body(*refs):
        ins, outs = refs[:4 * n], refs[4 * n:]
        for t in range(n):
            delta, mn, vn = _adam_math(ins[t][...], ins[n + t][...], ins[2 * n + t][...], ins[3 * n + t][...])
            outs[3 * t][...] = delta
            outs[3 * t + 1][...] = mn
            outs[3 * t + 2][...] = vn

    vm = pl.BlockSpec(memory_space=pltpu.VMEM)
    outs = pl.pallas_call(
        body, name="adam_small",
        out_shape=tuple(jax.ShapeDtypeStruct(w.shape, F32) for w in ws for _ in range(3)),
        in_specs=[vm] * (4 * n), out_specs=tuple([vm] * (3 * n)),
        compiler_params=_params(),
    )(*ws, *gs, *ms, *vs)
    return [outs[3 * t:3 * t + 3] for t in range(n)]


def kernel(x, norm_pre_g, w_in, conv_w, w_out, norm_post_g, loss_target, m_norm_pre_g, m_w_in, m_conv_w, m_w_out, m_norm_post_g, v_norm_pre_g, v_w_in, v_conv_w, v_w_out, v_norm_post_g):
    _, seq, d_model = x.shape
    width = w_in.shape[1]
    conv_q = conv_w.shape[1]
    conv_width = N_CHIPS * conv_q
    attn_width = d_model - conv_width
    xs, tg = x[0], loss_target[0]
    g1, g2 = norm_pre_g.reshape(1, d_model), norm_post_g.reshape(1, d_model)

    w_full, wout_full, cw_full = gather_weights(w_in, w_out, conv_w)
    wout2 = wout_full.reshape(attn_width + conv_width, d_model)
    cw = jnp.zeros((SUBLANES, conv_width), F32).at[:CONV_K].set(
        cw_full[:, :CONV_K, :conv_q].transpose(1, 0, 2).reshape(CONV_K, conv_width))
    tables = _rope_tables(seq)

    h, q, k, v, ga, cz = inproj(xs, g1, w_full, tables, attn_width, conv_width)
    run = None
    for dil in DILATIONS:
        run = attn_fwd(q, k, v, dil, run)
    o, lse = run
    d_o, delta, dga, dcb, dgc, dcv, e, dwout, dg2, dcw, loss_acc = tail(o, ga, cz, xs, tg, wout2, g2, cw)
    grads = None
    for dil in DILATIONS:
        grads = attn_bwd(q, k, v, d_o, lse, delta, dil, grads)
    grad_x, dz, dg1 = dz_dx(*grads, dga, dcb, dgc, dcv, cz, tables, xs, g1, e, w_full, cw)
    dwin = dw_in(h, dz)

    small = jnp.zeros((SUBLANES, d_model), F32)
    small = small.at[0].set(dg1.sum(axis=0)).at[1].set(dg2.sum(axis=0))
    small = small.at[2:2 + CONV_K, :conv_width].set(dcw.reshape(CONV_K, SUBLANES, conv_width).sum(axis=1))
    rin, rout, rsmall = grad_reduce(dwin, dwout.reshape(N_DEV, -1, d_model), small)

    half = width // 2
    tr = 256
    gw_in, d_in, m_in, v_in = adam_shard(
        "adam_w_in", w_in, rin, m_w_in, v_w_in, (tr, half), (2, d_model // tr),
        lambda hf, i: (i, hf), lambda hf, i: (hf, i, 0))
    rq = w_out.shape[0] // 2
    gw_out, d_out, m_out, v_out = adam_shard(
        "adam_w_out", w_out, rout, m_w_out, v_w_out, (rq, d_model), (2,),
        lambda hf: (hf, 0), lambda hf: (hf, 0, 0))

    chip = 2 * lax.axis_index("x") + lax.axis_index("y")
    g_pre, g_post = rsmall[0:1], rsmall[1:2]
    g_conv = lax.dynamic_slice(rsmall[2:2 + CONV_K, :conv_width], (0, chip * conv_q), (CONV_K, conv_q))
    (d_pre, m_pre, v_pre), (d_post, m_post, v_post), (d_cv, m_cv, v_cv) = adam_small(
        [g1, g2, conv_w], [g_pre, g_post, g_conv],
        [m_norm_pre_g.reshape(1, d_model), m_norm_post_g.reshape(1, d_model), m_conv_w],
        [v_norm_pre_g.reshape(1, d_model), v_norm_post_g.reshape(1, d_model), v_conv_w])

    loss = lax.psum(0.5 * jnp.sum(loss_acc) / d_model, ("x", "y", "c"))
    vec = lambda a: a.reshape(d_model)
    return (loss, grad_x.reshape(1, seq, d_model),
            vec(g_pre), gw_in, g_conv, gw_out, vec(g_post),
            vec(d_pre), d_in, d_cv, d_out, vec(d_post),
            vec(m_pre), m_in, m_cv, m_out, vec(m_post),
            vec(v_pre), v_in, v_cv, v_out, vec(v_post))
```

```python
import jax
import jax.numpy as jnp
from jax import lax
from jax.experimental import pallas as pl
from jax.experimental.pallas import tpu as pltpu

HEAD_DIM = 64
LANES = 128
SUBLANES = 8
BLOCK = 128
WINDOW_KEYS = 128
PERM = 16
PJ = 4
P4_ROWS = BLOCK // PJ
ROW_TILE = 256
CONV_K = 3
ROPE_THETA = 10000.0
NORM_EPS = 1e-6
ATTN_SCALE = HEAD_DIM ** -0.5
NEG = -1e30
N_CHIPS = 4
N_DEV = 8
MESH = pl.DeviceIdType.MESH
ADAM_LR = 0.001
ADAM_B1 = 0.9
ADAM_B2 = 0.999
ADAM_EPS = 1e-08
ADAM_WD = 0.01
ADAM_STEP = 10
VMEM_LIMIT = 52 * 1024 * 1024

F32 = jnp.float32
BF16 = jnp.bfloat16


def _params(sem=None, **kw):
    return pltpu.CompilerParams(dimension_semantics=sem, vmem_limit_bytes=VMEM_LIMIT, **kw)


def _sigmoid(z):
    return 1.0 / (1.0 + jnp.exp(-z))


def _rowgroup_sum(a):
    rows, n = a.shape
    return a.reshape(rows // SUBLANES, SUBLANES, n).sum(axis=0)


def _nt(a, b):
    return lax.dot_general(a, b, (((1,), (1,)), ((), ())), preferred_element_type=F32)


def _tn(a, b):
    return lax.dot_general(a, b, (((0,), (0,)), ((), ())), preferred_element_type=F32)


def _col_pieces(a, b, width):
    out = []
    while a < b:
        j = a // width
        e = min(b, (j + 1) * width)
        out.append((j, a - j * width, e - j * width))
        a = e
    return out


def _lane_groups(width):
    return [slice(g * LANES, (g + 1) * LANES) for g in range(width // LANES)]


def _perm_shape(seq, width):
    return (PJ, PJ, seq // PERM, width)


def _perm_tile_spec(width):
    return pl.BlockSpec((PJ, PJ, ROW_TILE // PERM, width), lambda i: (0, 0, i, 0))


def _to_perm(stage, g, dst_ref, sl, dtype):
    rows = stage.shape[1] // PERM
    for b in range(PERM):
        dst_ref[b // PJ, b % PJ, :, sl] = stage[g, pl.ds(b, rows, stride=PERM), :].astype(dtype)


def _from_perm(src_ref, sl, stage, g):
    rows = stage.shape[1] // PERM
    for b in range(PERM):
        stage[g, pl.ds(b, rows, stride=PERM), :] = src_ref[b // PJ, b % PJ, :, sl]


def _flip(a, f):
    return 1 - a if f else a


def _chip_peers(x, y):
    return [(1 - x, y), (x, 1 - y), (1 - x, 1 - y)]


def gather_weights(w_in, w_out, conv_w):
    d_model, width = w_in.shape
    rows = w_out.shape[0]
    cw = jnp.zeros((SUBLANES, LANES), F32).at[:CONV_K, :conv_w.shape[1]].set(conv_w)

    def body(win_ref, wout_ref, cw_ref, winf_ref, woutf_ref, cwf_ref, st_in, st_out, send_sems, recv_sems):
        x, y, c = lax.axis_index("x"), lax.axis_index("y"), lax.axis_index("c")
        me = 2 * x + y
        st_in[...] = win_ref[...].astype(BF16)
        st_out[...] = wout_ref[...].astype(BF16)
        winf_ref[me] = st_in[...]
        woutf_ref[me] = st_out[...]
        cwf_ref[me] = cw_ref[...]
        srcs = (st_in, st_out, cw_ref)
        dsts = (winf_ref, woutf_ref, cwf_ref)

        def copy(k, t, slot, to):
            return pltpu.make_async_remote_copy(
                src_ref=srcs[t], dst_ref=dsts[t].at[slot], send_sem=send_sems.at[k, t], recv_sem=recv_sems.at[k, t],
                device_id=to, device_id_type=MESH)

        peers = _chip_peers(x, y)
        sends = [copy(k, t, me, (px, py, c)) for k, (px, py) in enumerate(peers) for t in range(3)]
        for cp in sends:
            cp.start()
        for k, (px, py) in enumerate(peers):
            for t in range(3):
                copy(k, t, 2 * px + py, (px, py, c)).wait_recv()
        for cp in sends:
            cp.wait_send()

    vm = pl.BlockSpec(memory_space=pltpu.VMEM)
    return pl.pallas_call(
        body, name="gather_weights",
        out_shape=(jax.ShapeDtypeStruct((N_CHIPS, d_model, width), BF16),
                   jax.ShapeDtypeStruct((N_CHIPS, rows, d_model), BF16),
                   jax.ShapeDtypeStruct((N_CHIPS, SUBLANES, LANES), F32)),
        in_specs=[vm, vm, vm], out_specs=(vm, vm, vm),
        scratch_shapes=[pltpu.VMEM((d_model, width), BF16), pltpu.VMEM((rows, d_model), BF16),
                        pltpu.SemaphoreType.DMA((3, 3)), pltpu.SemaphoreType.DMA((3, 3))],
        compiler_params=_params(),
    )(w_in, w_out, cw)


def _rope_tables(seq):
    half = HEAD_DIM // 2
    inv_freq = ROPE_THETA ** (-jnp.arange(half, dtype=F32) * 2.0 / HEAD_DIM)
    ang = jnp.arange(seq).astype(F32)[:, None] * inv_freq[None, :]
    cos, sin = jnp.cos(ang), jnp.sin(ang)
    zero = jnp.zeros_like(sin)
    return (jnp.concatenate([cos, cos, cos, cos], axis=1),
            jnp.concatenate([-sin, zero, -sin, zero], axis=1),
            jnp.concatenate([zero, sin, zero, sin], axis=1))


def _rope(t, cos, s1, s2):
    return t * cos + pltpu.roll(t, LANES - HEAD_DIM // 2, 1) * s1 + pltpu.roll(t, HEAD_DIM // 2, 1) * s2


def _rope_transposed(g, cos, s1, s2):
    return g * cos + pltpu.roll(g * s1, HEAD_DIM // 2, 1) + pltpu.roll(g * s2, LANES - HEAD_DIM // 2, 1)


def inproj(x, g1, w_full, tables, attn_w, conv_w):
    seq, d_model = x.shape
    width = w_full.shape[2]
    tm = ROW_TILE
    groups = _lane_groups(attn_w)

    def body(x_ref, g_ref, w_ref, cos_ref, s1_ref, s2_ref,
             h_ref, q_ref, k_ref, v_ref, qp_ref, kp_ref, vp_ref, ga_ref, cz_ref, stage):
        xv = x_ref[...]
        hb = ((xv * lax.rsqrt(jnp.mean(xv * xv, axis=-1, keepdims=True) + NORM_EPS)) * g_ref[...]).astype(BF16)
        h_ref[...] = hb
        cos, s1, s2 = cos_ref[...], s1_ref[...], s2_ref[...]

        def proj(a, b):
            parts = [jnp.dot(hb, w_ref[j, :, lo:hi], preferred_element_type=F32) for j, lo, hi in _col_pieces(a, b, width)]
            return parts[0] if len(parts) == 1 else jnp.concatenate(parts, axis=1)

        def emit(z, nat_ref, perm_ref, fn):
            for g, sl in enumerate(groups):
                val = fn(z[:, sl])
                nat_ref[:, sl] = val.astype(BF16)
                stage[g] = val
            for g, sl in enumerate(groups):
                _to_perm(stage, g, perm_ref, sl, BF16)

        emit(proj(0, attn_w), q_ref, qp_ref, lambda t: _rope(t, cos, s1, s2) * ATTN_SCALE)
        emit(proj(attn_w, 2 * attn_w), k_ref, kp_ref, lambda t: _rope(t, cos, s1, s2))
        emit(proj(2 * attn_w, 3 * attn_w), v_ref, vp_ref, lambda t: t)
        ga_ref[...] = proj(3 * attn_w, 4 * attn_w)
        cz_ref[...] = proj(4 * attn_w, 4 * attn_w + 4 * conv_w)

    row = lambda n: pl.BlockSpec((tm, n), lambda i: (i, 0))
    nat = jax.ShapeDtypeStruct((seq, attn_w), BF16)
    perm = jax.ShapeDtypeStruct(_perm_shape(seq, attn_w), BF16)
    return pl.pallas_call(
        body, name="inproj", grid=(seq // tm,),
        out_shape=(jax.ShapeDtypeStruct((seq, d_model), BF16), nat, nat, nat, perm, perm, perm,
                   jax.ShapeDtypeStruct((seq, attn_w), F32), jax.ShapeDtypeStruct((seq, 4 * conv_w), F32)),
        in_specs=[row(d_model), pl.BlockSpec((1, d_model), lambda i: (0, 0)),
                  pl.BlockSpec(w_full.shape, lambda i: (0, 0, 0)), row(LANES), row(LANES), row(LANES)],
        out_specs=(row(d_model), row(attn_w), row(attn_w), row(attn_w),
                   _perm_tile_spec(attn_w), _perm_tile_spec(attn_w), _perm_tile_spec(attn_w),
                   row(attn_w), row(4 * conv_w)),
        scratch_shapes=[pltpu.VMEM((len(groups), tm, LANES), F32)],
        compiler_params=_params(("arbitrary",)),
    )(x, g1, w_full, *tables)


class _Mode:
    def __init__(self, name, seq):
        self.name = name
        if name == "nat":
            self.residues, self.nb = 1, seq // BLOCK
        elif name == "p16":
            self.residues, self.nb = PERM, seq // PERM // BLOCK
        else:
            self.residues, self.nb = PJ, seq // PERM // P4_ROWS

    def spec(self, width, which, last=None):
        if which == "prev":
            blk = lambda n: jnp.maximum(n - 1, 0)
        elif last is None:
            blk = lambda n: n
        else:
            blk = lambda n: jnp.minimum(n, last)
        if self.name == "nat":
            return pl.BlockSpec((BLOCK, width), lambda r, n: (blk(n), 0))
        if self.name == "p16":
            return pl.BlockSpec((1, 1, BLOCK, width), lambda r, n: (r // PJ, r % PJ, blk(n), 0))
        return pl.BlockSpec((PJ, 1, P4_ROWS, width), lambda r, n: (0, r, blk(n), 0))

    def get(self, ref, sl):
        if self.name == "nat":
            return ref[:, sl]
        if self.name == "p16":
            return ref[0, 0, :, sl]
        return jnp.concatenate([ref[j, 0, :, sl] for j in range(PJ)], axis=0)

    def put(self, ref, sl, val):
        if self.name == "nat":
            ref[:, sl] = val
        elif self.name == "p16":
            ref[0, 0, :, sl] = val
        else:
            for j in range(PJ):
                ref[j, 0, :, sl] = val[j * P4_ROWS:(j + 1) * P4_ROWS]

    def index(self, idx, is_key):
        if self.name != "p4":
            return idx - BLOCK if is_key else idx
        within = jnp.bitwise_and(idx, BLOCK - 1)
        m = PJ * jnp.bitwise_and(within, P4_ROWS - 1) + jnp.right_shift(within, P4_ROWS.bit_length() - 1)
        return m + BLOCK * (jnp.right_shift(idx, BLOCK.bit_length() - 1) - 1) if is_key else m

    def bias(self, n, keys_major):
        shape = (2 * BLOCK, BLOCK) if keys_major else (BLOCK, 2 * BLOCK)
        kdim = 0 if keys_major else 1
        kidx = lax.broadcasted_iota(jnp.int32, shape, kdim)
        qidx = lax.broadcasted_iota(jnp.int32, shape, 1 - kdim)
        rel = self.index(qidx, False) - self.index(kidx, True)
        valid = (rel >= 0) & (rel <= WINDOW_KEYS) & ((kidx >= BLOCK) | (n > 0))
        return jnp.where(valid, 0.0, NEG)


def _head_masks():
    lane = lax.broadcasted_iota(jnp.int32, (BLOCK, LANES), 1)
    lo = lane < HEAD_DIM
    return lane, lo, jnp.where(lo, 1.0, 0.0).astype(BF16), jnp.where(lo, 0.0, 1.0).astype(BF16)


def _column(blk, lane, h):
    return jnp.sum(jnp.where(lane == h, blk, 0.0), axis=1, keepdims=True)


def attn_fwd(name, q, k, v, run):
    nat = name == "nat"
    seq = q.shape[0] if nat else q.shape[2] * PERM
    attn_w = q.shape[-1]
    mode = _Mode(name, seq)
    groups = _lane_groups(attn_w)
    first = run is None
    all_lanes = slice(0, LANES)

    def body(*refs):
        q_ref, kp_ref, kc_ref, vp_ref, vc_ref = refs[:5]
        if first:
            o_ref, l_ref = refs[5:]
        elif nat:
            oin_ref, lin_ref, o_ref, l_ref, ostage, lstage = refs[5:]
        else:
            oin_ref, lin_ref, o_ref, l_ref = refs[5:]
        n = pl.program_id(1)
        bias = mode.bias(n, False)
        lane, lo, m_lo, m_hi = _head_masks()
        lblk = jnp.zeros((BLOCK, LANES), F32)
        if first:
            lin = None
        elif nat:
            for g, sl in enumerate(groups):
                _from_perm(oin_ref, sl, ostage, g)
            _from_perm(lin_ref, all_lanes, lstage, 0)
            lin = lstage[0]
        else:
            lin = mode.get(lin_ref, all_lanes)
        for p, sl in enumerate(groups):
            q2 = mode.get(q_ref, sl)
            kcat = jnp.concatenate([mode.get(kp_ref, sl), mode.get(kc_ref, sl)], axis=0)
            vcat = jnp.concatenate([mode.get(vp_ref, sl), mode.get(vc_ref, sl)], axis=0)
            if first:
                o_prev = None
            else:
                o_prev = ostage[p] if nat else mode.get(oin_ref, sl)
            halves = []
            for e, msk in enumerate((m_lo, m_hi)):
                h = 2 * p + e
                s = _nt(q2 * msk, kcat) + bias
                m = jnp.max(s, axis=1, keepdims=True)
                pe = jnp.exp(s - m)
                l = jnp.sum(pe, axis=1, keepdims=True)
                o_e = jnp.dot(pe.astype(BF16), vcat, preferred_element_type=F32) / l
                lse = m + jnp.log(l)
                if not first:
                    lp = _column(lin, lane, h)
                    mx = jnp.maximum(lp, lse)
                    new = mx + jnp.log(jnp.exp(lp - mx) + jnp.exp(lse - mx))
                    o_e = jnp.exp(lp - new) * o_prev + jnp.exp(lse - new) * o_e
                    lse = new
                halves.append(o_e)
                lblk = jnp.where(lane == h, lse, lblk)
            mode.put(o_ref, sl, jnp.where(lo, halves[0], halves[1]))
        mode.put(l_ref, all_lanes, lblk)

    ins = [q, k, k, v, v]
    specs = [mode.spec(attn_w, "cur"), mode.spec(attn_w, "prev"), mode.spec(attn_w, "cur"),
             mode.spec(attn_w, "prev"), mode.spec(attn_w, "cur")]
    scratch = []
    if not first:
        ins += list(run)
        if nat:
            rows8 = BLOCK // PERM
            specs += [pl.BlockSpec((PJ, PJ, rows8, attn_w), lambda r, n: (0, 0, n, 0)),
                      pl.BlockSpec((PJ, PJ, rows8, LANES), lambda r, n: (0, 0, n, 0))]
            scratch = [pltpu.VMEM((len(groups), BLOCK, LANES), F32), pltpu.VMEM((1, BLOCK, LANES), F32)]
        else:
            specs += [mode.spec(attn_w, "cur"), mode.spec(LANES, "cur")]
    if nat:
        out_shape = (jax.ShapeDtypeStruct((seq, attn_w), F32), jax.ShapeDtypeStruct((seq, LANES), F32))
    else:
        out_shape = (jax.ShapeDtypeStruct(_perm_shape(seq, attn_w), F32), jax.ShapeDtypeStruct(_perm_shape(seq, LANES), F32))
    return pl.pallas_call(
        body, name=f"attn_fwd_{name}", grid=(mode.residues, mode.nb),
        out_shape=out_shape, in_specs=specs, out_specs=(mode.spec(attn_w, "cur"), mode.spec(LANES, "cur")),
        scratch_shapes=scratch,
        compiler_params=_params(("arbitrary", "arbitrary")),
    )(*ins)


def attn_bwd(name, q, k, v, d_o, lse, delta, run):
    nat = name == "nat"
    seq = q.shape[0] if nat else q.shape[2] * PERM
    attn_w = q.shape[-1]
    mode = _Mode(name, seq)
    nb = mode.nb
    groups = _lane_groups(attn_w)
    first = run is None
    all_lanes = slice(0, LANES)

    def body(*refs):
        q_ref, kp_ref, kc_ref, vp_ref, vc_ref, do_ref, lse_ref, dl_ref = refs[:8]
        if first:
            dq_ref, dk_ref, dv_ref, ck, cv = refs[8:]
        else:
            dqi_ref, dki_ref, dvi_ref, dq_ref, dk_ref, dv_ref, ck, cv = refs[8:]
        n = pl.program_id(1)

        @pl.when(n == 0)
        def _():
            ck[...] = jnp.zeros_like(ck)
            cv[...] = jnp.zeros_like(cv)

        @pl.when(n < nb)
        def _():
            bias = mode.bias(n, True)
            _, lo, m_lo, m_hi = _head_masks()
            lse_t = jnp.transpose(mode.get(lse_ref, all_lanes))
            dl_t = jnp.transpose(mode.get(dl_ref, all_lanes))
            for p, sl in enumerate(groups):
                q2, do2 = mode.get(q_ref, sl), mode.get(do_ref, sl)
                kcat = jnp.concatenate([mode.get(kp_ref, sl), mode.get(kc_ref, sl)], axis=0)
                vcat = jnp.concatenate([mode.get(vp_ref, sl), mode.get(vc_ref, sl)], axis=0)
                dkc = jnp.zeros((2 * BLOCK, LANES), F32)
                dvc = jnp.zeros((2 * BLOCK, LANES), F32)
                dqs = []
                for e, msk in enumerate((m_lo, m_hi)):
                    h = 2 * p + e
                    qe, doe = q2 * msk, do2 * msk
                    p_t = jnp.exp(_nt(kcat, qe) + bias - lse_t[h:h + 1, :])
                    ds_t = p_t * (_nt(vcat, doe) - dl_t[h:h + 1, :])
                    dsb = ds_t.astype(BF16)
                    dkc = dkc + jnp.dot(dsb, qe, preferred_element_type=F32)
                    dvc = dvc + jnp.dot(p_t.astype(BF16), doe, preferred_element_type=F32)
                    dqs.append(_tn(dsb, kcat))
                dq2 = jnp.where(lo, dqs[0], dqs[1]) * ATTN_SCALE
                dk2 = ck[:, sl] + dkc[:BLOCK]
                dv2 = cv[:, sl] + dvc[:BLOCK]
                if not first:
                    dq2 = dq2 + mode.get(dqi_ref, sl)
                    dk2 = dk2 + mode.get(dki_ref, sl)
                    dv2 = dv2 + mode.get(dvi_ref, sl)
                mode.put(dq_ref, sl, dq2)
                mode.put(dk_ref, sl, dk2)
                mode.put(dv_ref, sl, dv2)
                ck[:, sl] = dkc[BLOCK:]
                cv[:, sl] = dvc[BLOCK:]

        @pl.when(n == nb)
        def _():
            for sl in groups:
                if first:
                    mode.put(dk_ref, sl, ck[:, sl])
                    mode.put(dv_ref, sl, cv[:, sl])
                else:
                    mode.put(dk_ref, sl, ck[:, sl] + mode.get(dki_ref, sl))
                    mode.put(dv_ref, sl, cv[:, sl] + mode.get(dvi_ref, sl))

    last = nb - 1
    cur = lambda w: mode.spec(w, "cur", last)
    prev = lambda w: mode.spec(w, "prev")
    ins = [q, k, k, v, v, d_o, lse, delta]
    specs = [cur(attn_w), prev(attn_w), cur(attn_w), prev(attn_w), cur(attn_w), cur(attn_w), cur(LANES), cur(LANES)]
    if not first:
        ins += list(run)
        specs += [cur(attn_w), prev(attn_w), prev(attn_w)]
    shp = jax.ShapeDtypeStruct((seq, attn_w) if nat else _perm_shape(seq, attn_w), F32)
    return pl.pallas_call(
        body, name=f"attn_bwd_{name}", grid=(mode.residues, nb + 1),
        out_shape=(shp, shp, shp), in_specs=specs, out_specs=(cur(attn_w), prev(attn_w), prev(attn_w)),
        scratch_shapes=[pltpu.VMEM((BLOCK, attn_w), F32), pltpu.VMEM((BLOCK, attn_w), F32)],
        compiler_params=_params(("arbitrary", "arbitrary")),
    )(*ins)


def _shift_down(u, halo, k):
    rolled = pltpu.roll(u, k, 0)
    row = lax.broadcasted_iota(jnp.int32, halo.shape, 0)
    top = jnp.where(row < k, pltpu.roll(halo, k, 0), rolled[:SUBLANES])
    return jnp.concatenate([top, rolled[SUBLANES:]], axis=0)


def _shift_up(u, halo, k):
    rows = u.shape[0]
    rolled = pltpu.roll(u, rows - k, 0)
    row = lax.broadcasted_iota(jnp.int32, halo.shape, 0)
    bot = jnp.where(row >= SUBLANES - k, pltpu.roll(halo, SUBLANES - k, 0), rolled[rows - SUBLANES:])
    return jnp.concatenate([rolled[:rows - SUBLANES], bot], axis=0)


def tail(o, lse, ga, cz, x, tgt, w_out, g2, cw):
    seq, d_model = x.shape
    attn_w = o.shape[1]
    conv_w = cz.shape[1] // 4
    mix = attn_w + conv_w
    groups = _lane_groups(attn_w)
    tm = ROW_TILE
    nt = seq // tm
    hb = tm // SUBLANES

    def body(o_ref, l_ref, ga_ref, cz_ref, hz_ref, x_ref, t_ref, w_ref, g_ref, cw_ref,
             do_ref, dl_ref, dop_ref, dlp_ref, lp_ref, dga_ref, dcb_ref, dgc_ref, dcv_ref, e_ref,
             dw_ref, dg_ref, dcw_ref, loss_ref, stage):
        i = pl.program_id(0)

        @pl.when(i == 0)
        def _():
            dw_ref[...] = jnp.zeros_like(dw_ref)
            dg_ref[...] = jnp.zeros_like(dg_ref)
            dcw_ref[...] = jnp.zeros_like(dcw_ref)
            loss_ref[...] = jnp.zeros_like(loss_ref)

        ov, gav = o_ref[...], ga_ref[...]
        sig_a = _sigmoid(gav)
        silu_a = gav * sig_a
        attn_out = ov * silu_a
        ch, cb = cz_ref[:, 0:conv_w], cz_ref[:, conv_w:2 * conv_w]
        cc, gc = cz_ref[:, 2 * conv_w:3 * conv_w], cz_ref[:, 3 * conv_w:4 * conv_w]
        u = cc * ch
        uh = hz_ref[:, 2 * conv_w:3 * conv_w] * hz_ref[:, 0:conv_w]
        uh = jnp.where(i > 0, uh, 0.0)
        u1 = _shift_down(u, uh, 1)
        u2 = _shift_down(u, uh, 2)
        w0, w1, w2 = cw_ref[0:1, :], cw_ref[1:2, :], cw_ref[2:3, :]
        cvv = u2 * w0 + u1 * w1 + u * w2
        sig_c = _sigmoid(gc)
        silu_c = gc * sig_c
        bc = cb * cvv
        conv_out = bc * silu_c
        mixed = jnp.concatenate([attn_out, conv_out], axis=1).astype(BF16)

        yv = jnp.dot(mixed, w_ref[...], preferred_element_type=F32)
        r2 = lax.rsqrt(jnp.mean(yv * yv, axis=-1, keepdims=True) + NORM_EPS)
        yhat = yv * r2
        gv = g_ref[...]
        diff = (x_ref[...] + yhat * gv) - t_ref[...]
        loss_ref[...] += _rowgroup_sum(diff * diff)
        ev = diff * (1.0 / d_model)
        e_ref[...] = ev
        dg_ref[...] += _rowgroup_sum(ev * yhat)
        eg = ev * gv
        dy = (r2 * (eg - yhat * jnp.mean(eg * yhat, axis=-1, keepdims=True))).astype(BF16)
        dw_ref[...] += _tn(mixed, dy)
        dm = _nt(dy, w_ref[...])
        dma, dmc = dm[:, :attn_w], dm[:, attn_w:]

        dov = dma * silu_a
        do_ref[...] = dov.astype(BF16)
        dga_ref[...] = (dma * ov * (sig_a * (1.0 + gav * (1.0 - sig_a)))).astype(BF16)
        prod = dov * ov
        lane = lax.broadcasted_iota(jnp.int32, (tm, LANES), 1)
        lo = lane < HEAD_DIM
        dblk = jnp.zeros((tm, LANES), F32)
        for p, sl in enumerate(groups):
            pr = prod[:, sl]
            dblk = jnp.where(lane == 2 * p, jnp.sum(jnp.where(lo, pr, 0.0), axis=1, keepdims=True), dblk)
            dblk = jnp.where(lane == 2 * p + 1, jnp.sum(jnp.where(lo, 0.0, pr), axis=1, keepdims=True), dblk)
            stage[p] = dov[:, sl]
        dl_ref[...] = dblk
        for p, sl in enumerate(groups):
            _to_perm(stage, p, dop_ref, sl, BF16)
        all_lanes = slice(0, LANES)
        stage[0] = dblk
        stage[1] = l_ref[...]
        _to_perm(stage, 0, dlp_ref, all_lanes, F32)
        _to_perm(stage, 1, lp_ref, all_lanes, F32)

        dsc = dmc * silu_c
        dcb_ref[...] = (dsc * cvv).astype(BF16)
        dgc_ref[...] = (dmc * bc * (sig_c * (1.0 + gc * (1.0 - sig_c)))).astype(BF16)
        dcv = dsc * cb
        dcv_ref[...] = dcv
        dcw_ref[0:SUBLANES, :] += _rowgroup_sum(dcv * u2)
        dcw_ref[SUBLANES:2 * SUBLANES, :] += _rowgroup_sum(dcv * u1)
        dcw_ref[2 * SUBLANES:3 * SUBLANES, :] += _rowgroup_sum(dcv * u)

    row = lambda n: pl.BlockSpec((tm, n), lambda i: (i, 0))
    whole = lambda a, b: pl.BlockSpec((a, b), lambda i: (0, 0))
    return pl.pallas_call(
        body, name="tail", grid=(nt,),
        out_shape=(jax.ShapeDtypeStruct((seq, attn_w), BF16), jax.ShapeDtypeStruct((seq, LANES), F32),
                   jax.ShapeDtypeStruct(_perm_shape(seq, attn_w), BF16), jax.ShapeDtypeStruct(_perm_shape(seq, LANES), F32),
                   jax.ShapeDtypeStruct(_perm_shape(seq, LANES), F32),
                   jax.ShapeDtypeStruct((seq, attn_w), BF16), jax.ShapeDtypeStruct((seq, conv_w), BF16),
                   jax.ShapeDtypeStruct((seq, conv_w), BF16), jax.ShapeDtypeStruct((seq, conv_w), F32),
                   jax.ShapeDtypeStruct((seq, d_model), F32), jax.ShapeDtypeStruct((mix, d_model), F32),
                   jax.ShapeDtypeStruct((SUBLANES, d_model), F32), jax.ShapeDtypeStruct((CONV_K * SUBLANES, conv_w), F32),
                   jax.ShapeDtypeStruct((SUBLANES, d_model), F32)),
        in_specs=[row(attn_w), row(LANES), row(attn_w), row(4 * conv_w),
                  pl.BlockSpec((SUBLANES, 4 * conv_w), lambda i: (jnp.maximum(i * hb - 1, 0), 0)),
                  row(d_model), row(d_model), whole(mix, d_model), whole(1, d_model), whole(SUBLANES, conv_w)],
        out_specs=(row(attn_w), row(LANES), _perm_tile_spec(attn_w), _perm_tile_spec(LANES), _perm_tile_spec(LANES),
                   row(attn_w), row(conv_w), row(conv_w), row(conv_w), row(d_model),
                   whole(mix, d_model), whole(SUBLANES, d_model), whole(CONV_K * SUBLANES, conv_w),
                   whole(SUBLANES, d_model)),
        scratch_shapes=[pltpu.VMEM((max(len(groups), 2), tm, LANES), F32)],
        compiler_params=_params(("arbitrary",)),
    )(o, lse, ga, cz, cz, x, tgt, w_out, g2, cw)


def dz_dx(nat_grads, perm_grads, dga, dcb, dgc, dcv, cz, tables, x, g1, e, w_full, cw):
    seq, d_model = x.shape
    attn_w = dga.shape[1]
    conv_w = dcv.shape[1]
    width = w_full.shape[2]
    in_w = 4 * attn_w + 4 * conv_w
    groups = _lane_groups(attn_w)
    tm = ROW_TILE
    nt = seq // tm
    hb = tm // SUBLANES

    def body(dq_ref, dk_ref, dv_ref, dqp_ref, dkp_ref, dvp_ref, dga_ref, dcb_ref, dgc_ref, dcv_ref, nh_ref, cz_ref,
             cos_ref, s1_ref, s2_ref, x_ref, g_ref, e_ref, w_ref, cw_ref, gx_ref, dz_ref, dg_ref, stage):
        i = pl.program_id(0)

        @pl.when(i == 0)
        def _():
            dg_ref[...] = jnp.zeros_like(dg_ref)

        cos, s1, s2 = cos_ref[...], s1_ref[...], s2_ref[...]
        for t, (nat_ref, perm_ref) in enumerate(((dq_ref, dqp_ref), (dk_ref, dkp_ref), (dv_ref, dvp_ref))):
            for g, sl in enumerate(groups):
                _from_perm(perm_ref, sl, stage, g)
            for g, sl in enumerate(groups):
                tot = nat_ref[:, sl] + stage[g]
                if t < 2:
                    tot = _rope_transposed(tot, cos, s1, s2)
                dz_ref[:, t * attn_w + g * LANES:t * attn_w + (g + 1) * LANES] = tot.astype(BF16)
        dz_ref[:, 3 * attn_w:4 * attn_w] = dga_ref[...]
        dcv = dcv_ref[...]
        nh = jnp.where(i < nt - 1, nh_ref[...], 0.0)
        w0, w1, w2 = cw_ref[0:1, :], cw_ref[1:2, :], cw_ref[2:3, :]
        du = dcv * w2 + _shift_up(dcv, nh, 1) * w1 + _shift_up(dcv, nh, 2) * w0
        base = 4 * attn_w
        dz_ref[:, base:base + conv_w] = (du * cz_ref[:, 2 * conv_w:3 * conv_w]).astype(BF16)
        dz_ref[:, base + conv_w:base + 2 * conv_w] = dcb_ref[...]
        dz_ref[:, base + 2 * conv_w:base + 3 * conv_w] = (du * cz_ref[:, 0:conv_w]).astype(BF16)
        dz_ref[:, base + 3 * conv_w:base + 4 * conv_w] = dgc_ref[...]

        dh = _nt(dz_ref[:, 0:width], w_ref[0])
        for j in range(1, N_CHIPS):
            dh = dh + _nt(dz_ref[:, j * width:(j + 1) * width], w_ref[j])
        xv = x_ref[...]
        r1 = lax.rsqrt(jnp.mean(xv * xv, axis=-1, keepdims=True) + NORM_EPS)
        xhat = xv * r1
        dg_ref[...] += _rowgroup_sum(dh * xhat)
        dhg = dh * g_ref[...]
        gx_ref[...] = r1 * (dhg - xhat * jnp.mean(dhg * xhat, axis=-1, keepdims=True)) + e_ref[...]

    row = lambda n: pl.BlockSpec((tm, n), lambda i: (i, 0))
    whole = lambda a, b: pl.BlockSpec((a, b), lambda i: (0, 0))
    pt = _perm_tile_spec(attn_w)
    return pl.pallas_call(
        body, name="dz_dx", grid=(nt,),
        out_shape=(jax.ShapeDtypeStruct((seq, d_model), F32), jax.ShapeDtypeStruct((seq, in_w), BF16),
                   jax.ShapeDtypeStruct((SUBLANES, d_model), F32)),
        in_specs=[row(attn_w), row(attn_w), row(attn_w), pt, pt, pt, row(attn_w), row(conv_w), row(conv_w), row(conv_w),
                  pl.BlockSpec((SUBLANES, conv_w), lambda i: (jnp.minimum((i + 1) * hb, seq // SUBLANES - 1), 0)),
                  row(4 * conv_w), row(LANES), row(LANES), row(LANES), row(d_model), whole(1, d_model), row(d_model),
                  pl.BlockSpec(w_full.shape, lambda i: (0, 0, 0)), whole(SUBLANES, conv_w)],
        out_specs=(row(d_model), row(in_w), whole(SUBLANES, d_model)),
        scratch_shapes=[pltpu.VMEM((len(groups), tm, LANES), F32)],
        compiler_params=_params(("arbitrary",)),
    )(*nat_grads, *perm_grads, dga, dcb, dgc, dcv, dcv, cz, *tables, x, g1, e, w_full, cw)


def dw_in(h, dz):
    seq, d_model = h.shape
    half = dz.shape[1] // N_DEV
    ts = 512
    steps = seq // ts

    def body(h_ref, dz_ref, o_ref):
        @pl.when(pl.program_id(1) == 0)
        def _():
            o_ref[...] = jnp.zeros_like(o_ref)

        o_ref[0] += _tn(h_ref[...], dz_ref[...])

    return pl.pallas_call(
        body, name="dw_in", grid=(N_DEV, steps),
        out_shape=jax.ShapeDtypeStruct((N_DEV, d_model, half), F32),
        in_specs=[pl.BlockSpec((ts, d_model), lambda j, s: (s, 0)), pl.BlockSpec((ts, half), lambda j, s: (s, j))],
        out_specs=pl.BlockSpec((1, d_model, half), lambda j, s: (j, 0, 0)),
        compiler_params=_params(("arbitrary", "arbitrary")),
    )(h, dz)


def grad_reduce(g_in, g_out, small):
    gi = g_in.reshape(N_CHIPS, 2, *g_in.shape[1:])
    go = g_out.reshape(N_CHIPS, 2, *g_out.shape[1:])
    shp_i, shp_o = gi.shape[2:], go.shape[2:]

    def body(gi_ref, go_ref, sm_ref, ri_ref, ro_ref, rs_ref, a_i, b_i, c_i, a_o, b_o, c_o, sbuf,
             loc_sems, sa, ra, sb, rb, sc, rc, ss, rs):
        x, y, c = lax.axis_index("x"), lax.axis_index("y"), lax.axis_index("c")
        me = 2 * x + y
        sib = (x, y, 1 - c)
        srcs, mine, theirs, contrib, res = (gi_ref, go_ref), (a_i, a_o), (b_i, b_o), (c_i, c_o), (ri_ref, ro_ref)

        flips = [(fx, fy, fc) for fx in (0, 1) for fy in (0, 1) for fc in (0, 1)][1:]
        my8 = 4 * x + 2 * y + c
        sbuf[my8] = sm_ref[...]

        def small_copy(k, slot, to):
            return pltpu.make_async_remote_copy(src_ref=sm_ref, dst_ref=sbuf.at[slot], send_sem=ss.at[k], recv_sem=rs.at[k],
                                                device_id=to, device_id_type=MESH)

        small_sends = []
        for k, (fx, fy, fc) in enumerate(flips):
            px, py, pc = _flip(x, fx), _flip(y, fy), _flip(c, fc)
            small_sends.append(small_copy(k, my8, (px, py, pc)))
            small_sends[-1].start()

        def a_copy(t):
            return pltpu.make_async_remote_copy(src_ref=srcs[t].at[:, 1 - c], dst_ref=theirs[t], send_sem=sa.at[t],
                                                recv_sem=ra.at[t], device_id=sib, device_id_type=MESH)

        loads = [pltpu.make_async_copy(srcs[t].at[:, c], mine[t], loc_sems.at[t]) for t in range(2)]
        a_sends = [a_copy(t) for t in range(2)]
        for cp in loads + a_sends:
            cp.start()
        for t in range(2):
            loads[t].wait()
            a_copy(t).wait_recv()
            mine[t][...] = mine[t][...] + theirs[t][...]

        def b_copy(k, t, piece, slot, to):
            return pltpu.make_async_remote_copy(src_ref=mine[t].at[piece], dst_ref=contrib[t].at[slot], send_sem=sb.at[k, t],
                                                recv_sem=rb.at[k, t], device_id=to, device_id_type=MESH)

        peers = _chip_peers(x, y)
        b_sends = [b_copy(k, t, 2 * px + py, me, (px, py, c)) for k, (px, py) in enumerate(peers) for t in range(2)]
        for cp in b_sends:
            cp.start()
        for t in range(2):
            contrib[t][me] = mine[t][me]
        for k, (px, py) in enumerate(peers):
            for t in range(2):
                b_copy(k, t, me, 2 * px + py, (px, py, c)).wait_recv()

        def c_copy(t, half):
            return pltpu.make_async_remote_copy(src_ref=res[t].at[half], dst_ref=res[t].at[half], send_sem=sc.at[t],
                                                recv_sem=rc.at[t], device_id=sib, device_id_type=MESH)

        c_sends = []
        for t in range(2):
            res[t][c] = ((contrib[t][0] + contrib[t][1]) + contrib[t][2]) + contrib[t][3]
            c_sends.append(c_copy(t, c))
            c_sends[-1].start()
        for t in range(2):
            c_copy(t, 1 - c).wait_recv()

        for k, (fx, fy, fc) in enumerate(flips):
            px, py, pc = _flip(x, fx), _flip(y, fy), _flip(c, fc)
            small_copy(k, 4 * px + 2 * py + pc, (px, py, pc)).wait_recv()
        tot = sbuf[0]
        for d in range(1, N_DEV):
            tot = tot + sbuf[d]
        rs_ref[...] = tot
        for cp in small_sends + a_sends + b_sends + c_sends:
            cp.wait_send()

    vm = pl.BlockSpec(memory_space=pltpu.VMEM)
    anyspace = pl.BlockSpec(memory_space=pl.ANY)
    dma = pltpu.SemaphoreType.DMA
    return pl.pallas_call(
        body, name="grad_reduce",
        out_shape=(jax.ShapeDtypeStruct((2, *shp_i), F32), jax.ShapeDtypeStruct((2, *shp_o), F32),
                   jax.ShapeDtypeStruct(small.shape, F32)),
        in_specs=[anyspace, anyspace, vm], out_specs=(vm, vm, vm),
        scratch_shapes=[pltpu.VMEM((N_CHIPS, *shp_i), F32), pltpu.VMEM((N_CHIPS, *shp_i), F32), pltpu.VMEM((N_CHIPS, *shp_i), F32),
                        pltpu.VMEM((N_CHIPS, *shp_o), F32), pltpu.VMEM((N_CHIPS, *shp_o), F32), pltpu.VMEM((N_CHIPS, *shp_o), F32),
                        pltpu.VMEM((N_DEV, *small.shape), F32),
                        dma((2,)), dma((2,)), dma((2,)), dma((3, 2)), dma((3, 2)), dma((2,)), dma((2,)),
                        dma((N_DEV - 1,)), dma((N_DEV - 1,))],
        compiler_params=_params(),
    )(gi, go, small)


def _adam_math(w, g, m, v):
    m = ADAM_B1 * m + (1.0 - ADAM_B1) * g
    v = ADAM_B2 * v + (1.0 - ADAM_B2) * (g * g)
    m_hat = m / (1.0 - ADAM_B1 ** ADAM_STEP)
    v_hat = v / (1.0 - ADAM_B2 ** ADAM_STEP)
    delta = -ADAM_LR * (m_hat / (jnp.sqrt(v_hat) + ADAM_EPS) + ADAM_WD * w)
    return delta, m, v


def adam_shard(name, w, g2, m, v, block, grid, w_map, g_map):
    def body(w_ref, g_ref, m_ref, v_ref, go_ref, d_ref, mo_ref, vo_ref):
        g = g_ref[0]
        delta, mn, vn = _adam_math(w_ref[...], g, m_ref[...], v_ref[...])
        go_ref[...] = g
        d_ref[...] = delta
        mo_ref[...] = mn
        vo_ref[...] = vn

    ws = pl.BlockSpec(block, w_map)
    shp = jax.ShapeDtypeStruct(w.shape, F32)
    return pl.pallas_call(
        body, name=name, grid=grid, out_shape=(shp, shp, shp, shp),
        in_specs=[ws, pl.BlockSpec((1, *block), g_map), ws, ws], out_specs=(ws, ws, ws, ws),
        compiler_params=_params(("arbitrary",) * len(grid)),
    )(w, g2, m, v)


def adam_small(ws, gs, ms, vs):
    n = len(ws)

    def body(*refs):
        ins, outs = refs[:4 * n], refs[4 * n:]
        for t in range(n):
            delta, mn, vn = _adam_math(ins[t][...], ins[n + t][...], ins[2 * n + t][...], ins[3 * n + t][...])
            outs[3 * t][...] = delta
            outs[3 * t + 1][...] = mn
            outs[3 * t + 2][...] = vn

    vm = pl.BlockSpec(memory_space=pltpu.VMEM)
    outs = pl.pallas_call(
        body, name="adam_small",
        out_shape=tuple(jax.ShapeDtypeStruct(w.shape, F32) for w in ws for _ in range(3)),
        in_specs=[vm] * (4 * n), out_specs=tuple([vm] * (3 * n)),
        compiler_params=_params(),
    )(*ws, *gs, *ms, *vs)
    return [outs[3 * t:3 * t + 3] for t in range(n)]


def kernel(x, norm_pre_g, w_in, conv_w, w_out, norm_post_g, loss_target, m_norm_pre_g, m_w_in, m_conv_w, m_w_out, m_norm_post_g, v_norm_pre_g, v_w_in, v_conv_w, v_w_out, v_norm_post_g):
    _, seq, d_model = x.shape
    width = w_in.shape[1]
    conv_q = conv_w.shape[1]
    conv_width = N_CHIPS * conv_q
    attn_width = d_model - conv_width
    xs, tg = x[0], loss_target[0]
    g1, g2 = norm_pre_g.reshape(1, d_model), norm_post_g.reshape(1, d_model)

    w_full, wout_full, cw_full = gather_weights(w_in, w_out, conv_w)
    wout2 = wout_full.reshape(attn_width + conv_width, d_model)
    cw = jnp.zeros((SUBLANES, conv_width), F32).at[:CONV_K].set(
        cw_full[:, :CONV_K, :conv_q].transpose(1, 0, 2).reshape(CONV_K, conv_width))
    tables = _rope_tables(seq)

    h, q, k, v, qp, kp, vp, ga, cz = inproj(xs, g1, w_full, tables, attn_width, conv_width)
    run = attn_fwd("p4", qp, kp, vp, None)
    run = attn_fwd("p16", qp, kp, vp, run)
    o, lse = attn_fwd("nat", q, k, v, run)
    (d_o, delta, d_op, delta_p, lse_p, dga, dcb, dgc, dcv, e, dwout, dg2, dcw, loss_acc) = tail(
        o, lse, ga, cz, xs, tg, wout2, g2, cw)
    nat_grads = attn_bwd("nat", q, k, v, d_o, lse, delta, None)
    perm_grads = attn_bwd("p4", qp, kp, vp, d_op, lse_p, delta_p, None)
    perm_grads = attn_bwd("p16", qp, kp, vp, d_op, lse_p, delta_p, perm_grads)
    grad_x, dz, dg1 = dz_dx(nat_grads, perm_grads, dga, dcb, dgc, dcv, cz, tables, xs, g1, e, w_full, cw)
    dwin = dw_in(h, dz)

    small = jnp.zeros((SUBLANES, d_model), F32)
    small = small.at[0].set(dg1.sum(axis=0)).at[1].set(dg2.sum(axis=0))
    small = small.at[2:2 + CONV_K, :conv_width].set(dcw.reshape(CONV_K, SUBLANES, conv_width).sum(axis=1))
    rin, rout, rsmall = grad_reduce(dwin, dwout.reshape(N_DEV, -1, d_model), small)

    half = width // 2
    tr = 256
    gw_in, d_in, m_in, v_in = adam_shard(
        "adam_w_in", w_in, rin, m_w_in, v_w_in, (tr, half), (2, d_model // tr),
        lambda hf, i: (i, hf), lambda hf, i: (hf, i, 0))
    rq = w_out.shape[0] // 2
    gw_out, d_out, m_out, v_out = adam_shard(
        "adam_w_out", w_out, rout, m_w_out, v_w_out, (rq, d_model), (2,),
        lambda hf: (hf, 0), lambda hf: (hf, 0, 0))

    chip = 2 * lax.axis_index("x") + lax.axis_index("y")
    g_pre, g_post = rsmall[0:1], rsmall[1:2]
    g_conv = lax.dynamic_slice(rsmall[2:2 + CONV_K, :conv_width], (0, chip * conv_q), (CONV_K, conv_q))
    (d_pre, m_pre, v_pre), (d_post, m_post, v_post), (d_cv, m_cv, v_cv) = adam_small(
        [g1, g2, conv_w], [g_pre, g_post, g_conv],
        [m_norm_pre_g.reshape(1, d_model), m_norm_post_g.reshape(1, d_model), m_conv_w],
        [v_norm_pre_g.reshape(1, d_model), v_norm_post_g.reshape(1, d_model), v_conv_w])

    loss = lax.psum(0.5 * jnp.sum(loss_acc) / d_model, ("x", "y", "c"))
    vec = lambda a: a.reshape(d_model)
    return (loss, grad_x.reshape(1, seq, d_model),
            vec(g_pre), gw_in, g_conv, gw_out, vec(g_post),
            vec(d_pre), d_in, d_cv, d_out, vec(d_post),
            vec(m_pre), m_in, m_cv, m_out, vec(m_post),
            vec(v_pre), v_in, v_cv, v_out, vec(v_post))
```

```python
import jax
import jax.numpy as jnp
from jax import lax
from jax.experimental import pallas as pl
from jax.experimental.pallas import tpu as pltpu

HEAD_DIM = 64
LANES = 128
SUBLANES = 8
BLOCK = 128
WINDOW_KEYS = 128
PERM = 16
PJ = 4
P4_ROWS = BLOCK // PJ
ROW_TILE = 256
CONV_K = 3
ROPE_THETA = 10000.0
NORM_EPS = 1e-6
ATTN_SCALE = HEAD_DIM ** -0.5
NEG = -1e30
N_CHIPS = 4
N_DEV = 8
MESH = pl.DeviceIdType.MESH
ADAM_LR = 0.001
ADAM_B1 = 0.9
ADAM_B2 = 0.999
ADAM_EPS = 1e-08
ADAM_WD = 0.01
ADAM_STEP = 10
VMEM_LIMIT = 52 * 1024 * 1024

F32 = jnp.float32
BF16 = jnp.bfloat16


def _params(sem=None, **kw):
    return pltpu.CompilerParams(dimension_semantics=sem, vmem_limit_bytes=VMEM_LIMIT, **kw)


def _sigmoid(z):
    return 1.0 / (1.0 + jnp.exp(-z))


def _rowgroup_sum(a):
    rows, n = a.shape
    return a.reshape(rows // SUBLANES, SUBLANES, n).sum(axis=0)


def _nt(a, b):
    return lax.dot_general(a, b, (((1,), (1,)), ((), ())), preferred_element_type=F32)


def _tn(a, b):
    return lax.dot_general(a, b, (((0,), (0,)), ((), ())), preferred_element_type=F32)


def _col_pieces(a, b, width):
    out = []
    while a < b:
        j = a // width
        e = min(b, (j + 1) * width)
        out.append((j, a - j * width, e - j * width))
        a = e
    return out


def _lane_groups(width):
    return [slice(g * LANES, (g + 1) * LANES) for g in range(width // LANES)]


def _perm_shape(seq, width):
    return (PJ, PJ, seq // PERM, width)


def _perm_tile_spec(width):
    return pl.BlockSpec((PJ, PJ, ROW_TILE // PERM, width), lambda i: (0, 0, i, 0))


def _to_perm(stage, g, dst_ref, sl, dtype):
    rows = stage.shape[1] // PERM
    for b in range(PERM):
        dst_ref[b // PJ, b % PJ, :, sl] = stage[g, pl.ds(b, rows, stride=PERM), :].astype(dtype)


def _from_perm(src_ref, sl, stage, g):
    rows = stage.shape[1] // PERM
    for b in range(PERM):
        stage[g, pl.ds(b, rows, stride=PERM), :] = src_ref[b // PJ, b % PJ, :, sl]


def _flip(a, f):
    return 1 - a if f else a


def _chip_peers(x, y):
    return [(1 - x, y), (x, 1 - y), (1 - x, 1 - y)]


def gather_weights(w_in, w_out, conv_w):
    d_model, width = w_in.shape
    rows = w_out.shape[0]
    cw = jnp.zeros((SUBLANES, LANES), F32).at[:CONV_K, :conv_w.shape[1]].set(conv_w)

    def body(win_ref, wout_ref, cw_ref, winf_ref, woutf_ref, cwf_ref, st_in, st_out, send_sems, recv_sems):
        x, y, c = lax.axis_index("x"), lax.axis_index("y"), lax.axis_index("c")
        me = 2 * x + y
        st_in[...] = win_ref[...].astype(BF16)
        st_out[...] = wout_ref[...].astype(BF16)
        winf_ref[me] = st_in[...]
        woutf_ref[me] = st_out[...]
        cwf_ref[me] = cw_ref[...]
        srcs = (st_in, st_out, cw_ref)
        dsts = (winf_ref, woutf_ref, cwf_ref)

        def copy(k, t, slot, to):
            return pltpu.make_async_remote_copy(
                src_ref=srcs[t], dst_ref=dsts[t].at[slot], send_sem=send_sems.at[k, t], recv_sem=recv_sems.at[k, t],
                device_id=to, device_id_type=MESH)

        peers = _chip_peers(x, y)
        sends = [copy(k, t, me, (px, py, c)) for k, (px, py) in enumerate(peers) for t in range(3)]
        for cp in sends:
            cp.start()
        for k, (px, py) in enumerate(peers):
            for t in range(3):
                copy(k, t, 2 * px + py, (px, py, c)).wait_recv()
        for cp in sends:
            cp.wait_send()

    vm = pl.BlockSpec(memory_space=pltpu.VMEM)
    return pl.pallas_call(
        body, name="gather_weights",
        out_shape=(jax.ShapeDtypeStruct((N_CHIPS, d_model, width), BF16),
                   jax.ShapeDtypeStruct((N_CHIPS, rows, d_model), BF16),
                   jax.ShapeDtypeStruct((N_CHIPS, SUBLANES, LANES), F32)),
        in_specs=[vm, vm, vm], out_specs=(vm, vm, vm),
        scratch_shapes=[pltpu.VMEM((d_model, width), BF16), pltpu.VMEM((rows, d_model), BF16),
                        pltpu.SemaphoreType.DMA((3, 3)), pltpu.SemaphoreType.DMA((3, 3))],
        compiler_params=_params(),
    )(w_in, w_out, cw)


def _rope_tables(seq):
    half = HEAD_DIM // 2
    inv_freq = ROPE_THETA ** (-jnp.arange(half, dtype=F32) * 2.0 / HEAD_DIM)
    ang = jnp.arange(seq).astype(F32)[:, None] * inv_freq[None, :]
    cos, sin = jnp.cos(ang), jnp.sin(ang)
    zero = jnp.zeros_like(sin)
    return (jnp.concatenate([cos, cos, cos, cos], axis=1),
            jnp.concatenate([-sin, zero, -sin, zero], axis=1),
            jnp.concatenate([zero, sin, zero, sin], axis=1))


def _rope(t, cos, s1, s2):
    return t * cos + pltpu.roll(t, LANES - HEAD_DIM // 2, 1) * s1 + pltpu.roll(t, HEAD_DIM // 2, 1) * s2


def _rope_transposed(g, cos, s1, s2):
    return g * cos + pltpu.roll(g * s1, HEAD_DIM // 2, 1) + pltpu.roll(g * s2, LANES - HEAD_DIM // 2, 1)


def inproj(x, g1, w_full, tables, attn_w, conv_w):
    seq, d_model = x.shape
    width = w_full.shape[2]
    tm = ROW_TILE
    groups = _lane_groups(attn_w)

    def body(x_ref, g_ref, w_ref, cos_ref, s1_ref, s2_ref,
             h_ref, q_ref, k_ref, v_ref, qp_ref, kp_ref, vp_ref, ga_ref, cz_ref, stage):
        xv = x_ref[...]
        hb = ((xv * lax.rsqrt(jnp.mean(xv * xv, axis=-1, keepdims=True) + NORM_EPS)) * g_ref[...]).astype(BF16)
        h_ref[...] = hb
        cos, s1, s2 = cos_ref[...], s1_ref[...], s2_ref[...]

        def proj(a, b):
            parts = [jnp.dot(hb, w_ref[j, :, lo:hi], preferred_element_type=F32) for j, lo, hi in _col_pieces(a, b, width)]
            return parts[0] if len(parts) == 1 else jnp.concatenate(parts, axis=1)

        def emit(z, nat_ref, perm_ref, fn):
            for g, sl in enumerate(groups):
                val = fn(z[:, sl])
                nat_ref[:, sl] = val.astype(BF16)
                stage[g] = val
            for g, sl in enumerate(groups):
                _to_perm(stage, g, perm_ref, sl, BF16)

        emit(proj(0, attn_w), q_ref, qp_ref, lambda t: _rope(t, cos, s1, s2) * ATTN_SCALE)
        emit(proj(attn_w, 2 * attn_w), k_ref, kp_ref, lambda t: _rope(t, cos, s1, s2))
        emit(proj(2 * attn_w, 3 * attn_w), v_ref, vp_ref, lambda t: t)
        ga_ref[...] = proj(3 * attn_w, 4 * attn_w)
        cz_ref[...] = proj(4 * attn_w, 4 * attn_w + 4 * conv_w)

    row = lambda n: pl.BlockSpec((tm, n), lambda i: (i, 0))
    nat = jax.ShapeDtypeStruct((seq, attn_w), BF16)
    perm = jax.ShapeDtypeStruct(_perm_shape(seq, attn_w), BF16)
    return pl.pallas_call(
        body, name="inproj", grid=(seq // tm,),
        out_shape=(jax.ShapeDtypeStruct((seq, d_model), BF16), nat, nat, nat, perm, perm, perm,
                   jax.ShapeDtypeStruct((seq, attn_w), F32), jax.ShapeDtypeStruct((seq, 4 * conv_w), F32)),
        in_specs=[row(d_model), pl.BlockSpec((1, d_model), lambda i: (0, 0)),
                  pl.BlockSpec(w_full.shape, lambda i: (0, 0, 0)), row(LANES), row(LANES), row(LANES)],
        out_specs=(row(d_model), row(attn_w), row(attn_w), row(attn_w),
                   _perm_tile_spec(attn_w), _perm_tile_spec(attn_w), _perm_tile_spec(attn_w),
                   row(attn_w), row(4 * conv_w)),
        scratch_shapes=[pltpu.VMEM((len(groups), tm, LANES), F32)],
        compiler_params=_params(("arbitrary",)),
    )(x, g1, w_full, *tables)


class _Mode:
    def __init__(self, name, seq):
        self.name = name
        if name == "nat":
            self.residues, self.nb = 1, seq // BLOCK
        elif name == "p16":
            self.residues, self.nb = PERM, seq // PERM // BLOCK
        else:
            self.residues, self.nb = PJ, seq // PERM // P4_ROWS

    def spec(self, width, which, last=None):
        if which == "prev":
            blk = lambda n: jnp.maximum(n - 1, 0)
        elif last is None:
            blk = lambda n: n
        else:
            blk = lambda n: jnp.minimum(n, last)
        if self.name == "nat":
            return pl.BlockSpec((BLOCK, width), lambda r, n: (blk(n), 0))
        if self.name == "p16":
            return pl.BlockSpec((1, 1, BLOCK, width), lambda r, n: (r // PJ, r % PJ, blk(n), 0))
        return pl.BlockSpec((PJ, 1, P4_ROWS, width), lambda r, n: (0, r, blk(n), 0))

    def get(self, ref, sl):
        if self.name == "nat":
            return ref[:, sl]
        if self.name == "p16":
            return ref[0, 0, :, sl]
        return jnp.concatenate([ref[j, 0, :, sl] for j in range(PJ)], axis=0)

    def put(self, ref, sl, val):
        if self.name == "nat":
            ref[:, sl] = val
        elif self.name == "p16":
            ref[0, 0, :, sl] = val
        else:
            for j in range(PJ):
                ref[j, 0, :, sl] = val[j * P4_ROWS:(j + 1) * P4_ROWS]

    def index(self, idx, is_key):
        if self.name != "p4":
            return idx - BLOCK if is_key else idx
        within = jnp.bitwise_and(idx, BLOCK - 1)
        m = PJ * jnp.bitwise_and(within, P4_ROWS - 1) + jnp.right_shift(within, P4_ROWS.bit_length() - 1)
        return m + BLOCK * (jnp.right_shift(idx, BLOCK.bit_length() - 1) - 1) if is_key else m

    def bias(self, n, keys_major):
        shape = (2 * BLOCK, BLOCK) if keys_major else (BLOCK, 2 * BLOCK)
        kdim = 0 if keys_major else 1
        kidx = lax.broadcasted_iota(jnp.int32, shape, kdim)
        qidx = lax.broadcasted_iota(jnp.int32, shape, 1 - kdim)
        rel = self.index(qidx, False) - self.index(kidx, True)
        valid = (rel >= 0) & (rel <= WINDOW_KEYS) & ((kidx >= BLOCK) | (n > 0))
        return jnp.where(valid, 0.0, NEG)


def _head_masks():
    lane = lax.broadcasted_iota(jnp.int32, (BLOCK, LANES), 1)
    lo = lane < HEAD_DIM
    return lane, lo, jnp.where(lo, 1.0, 0.0).astype(BF16), jnp.where(lo, 0.0, 1.0).astype(BF16)


def _column(blk, lane, h):
    return jnp.sum(jnp.where(lane == h, blk, 0.0), axis=1, keepdims=True)


def attn_fwd(name, q, k, v, run):
    nat = name == "nat"
    seq = q.shape[0] if nat else q.shape[2] * PERM
    attn_w = q.shape[-1]
    mode = _Mode(name, seq)
    groups = _lane_groups(attn_w)
    first = run is None
    all_lanes = slice(0, LANES)

    def body(*refs):
        q_ref, kp_ref, kc_ref, vp_ref, vc_ref = refs[:5]
        if first:
            o_ref, l_ref = refs[5:]
        elif nat:
            oin_ref, lin_ref, o_ref, l_ref, ostage, lstage = refs[5:]
        else:
            oin_ref, lin_ref, o_ref, l_ref = refs[5:]
        n = pl.program_id(1)
        bias = mode.bias(n, False)
        bias2 = jnp.concatenate([bias, bias], axis=0)
        lane, lo, m_lo, m_hi = _head_masks()
        lblk = jnp.zeros((BLOCK, LANES), F32)
        if first:
            lin = None
        elif nat:
            for g, sl in enumerate(groups):
                _from_perm(oin_ref, sl, ostage, g)
            _from_perm(lin_ref, all_lanes, lstage, 0)
            lin = lstage[0]
        else:
            lin = mode.get(lin_ref, all_lanes)
        for p, sl in enumerate(groups):
            q2 = mode.get(q_ref, sl)
            kcat = jnp.concatenate([mode.get(kp_ref, sl), mode.get(kc_ref, sl)], axis=0)
            vcat = jnp.concatenate([mode.get(vp_ref, sl), mode.get(vc_ref, sl)], axis=0)
            if first:
                o_prev = None
            else:
                o_prev = ostage[p] if nat else mode.get(oin_ref, sl)
            qq = jnp.concatenate([q2 * m_lo, q2 * m_hi], axis=0)
            s = _nt(qq, kcat) + bias2
            m = jnp.max(s, axis=1, keepdims=True)
            pe = jnp.exp(s - m)
            l = jnp.sum(pe, axis=1, keepdims=True)
            o_new = jnp.dot(pe.astype(BF16), vcat, preferred_element_type=F32) / l
            lse = m + jnp.log(l)
            if not first:
                lp = jnp.concatenate([_column(lin, lane, 2 * p), _column(lin, lane, 2 * p + 1)], axis=0)
                mx = jnp.maximum(lp, lse)
                new = mx + jnp.log(jnp.exp(lp - mx) + jnp.exp(lse - mx))
                o_new = jnp.exp(lp - new) * jnp.concatenate([o_prev, o_prev], axis=0) + jnp.exp(lse - new) * o_new
                lse = new
            mode.put(o_ref, sl, jnp.where(lo, o_new[:BLOCK], o_new[BLOCK:]))
            lblk = jnp.where(lane == 2 * p, lse[:BLOCK], lblk)
            lblk = jnp.where(lane == 2 * p + 1, lse[BLOCK:], lblk)
        mode.put(l_ref, all_lanes, lblk)

    ins = [q, k, k, v, v]
    specs = [mode.spec(attn_w, "cur"), mode.spec(attn_w, "prev"), mode.spec(attn_w, "cur"),
             mode.spec(attn_w, "prev"), mode.spec(attn_w, "cur")]
    scratch = []
    if not first:
        ins += list(run)
        if nat:
            rows8 = BLOCK // PERM
            specs += [pl.BlockSpec((PJ, PJ, rows8, attn_w), lambda r, n: (0, 0, n, 0)),
                      pl.BlockSpec((PJ, PJ, rows8, LANES), lambda r, n: (0, 0, n, 0))]
            scratch = [pltpu.VMEM((len(groups), BLOCK, LANES), F32), pltpu.VMEM((1, BLOCK, LANES), F32)]
        else:
            specs += [mode.spec(attn_w, "cur"), mode.spec(LANES, "cur")]
    if nat:
        out_shape = (jax.ShapeDtypeStruct((seq, attn_w), F32), jax.ShapeDtypeStruct((seq, LANES), F32))
    else:
        out_shape = (jax.ShapeDtypeStruct(_perm_shape(seq, attn_w), F32), jax.ShapeDtypeStruct(_perm_shape(seq, LANES), F32))
    return pl.pallas_call(
        body, name=f"attn_fwd_{name}", grid=(mode.residues, mode.nb),
        out_shape=out_shape, in_specs=specs, out_specs=(mode.spec(attn_w, "cur"), mode.spec(LANES, "cur")),
        scratch_shapes=scratch,
        compiler_params=_params(("arbitrary", "arbitrary")),
    )(*ins)


def attn_bwd(name, q, k, v, d_o, lse, delta, run):
    nat = name == "nat"
    seq = q.shape[0] if nat else q.shape[2] * PERM
    attn_w = q.shape[-1]
    mode = _Mode(name, seq)
    nb = mode.nb
    groups = _lane_groups(attn_w)
    first = run is None
    all_lanes = slice(0, LANES)

    def body(*refs):
        q_ref, kp_ref, kc_ref, vp_ref, vc_ref, do_ref, lse_ref, dl_ref = refs[:8]
        if first:
            dq_ref, dk_ref, dv_ref, ck, cv = refs[8:]
        else:
            dqi_ref, dki_ref, dvi_ref, dq_ref, dk_ref, dv_ref, ck, cv = refs[8:]
        n = pl.program_id(1)

        @pl.when(n == 0)
        def _():
            ck[...] = jnp.zeros_like(ck)
            cv[...] = jnp.zeros_like(cv)

        @pl.when(n < nb)
        def _():
            bias = mode.bias(n, True)
            bias2 = jnp.concatenate([bias, bias], axis=1)
            _, lo, m_lo, m_hi = _head_masks()
            lse_t = jnp.transpose(mode.get(lse_ref, all_lanes))
            dl_t = jnp.transpose(mode.get(dl_ref, all_lanes))
            for p, sl in enumerate(groups):
                q2, do2 = mode.get(q_ref, sl), mode.get(do_ref, sl)
                kcat = jnp.concatenate([mode.get(kp_ref, sl), mode.get(kc_ref, sl)], axis=0)
                vcat = jnp.concatenate([mode.get(vp_ref, sl), mode.get(vc_ref, sl)], axis=0)
                qq = jnp.concatenate([q2 * m_lo, q2 * m_hi], axis=0)
                dd = jnp.concatenate([do2 * m_lo, do2 * m_hi], axis=0)
                h0 = 2 * p
                lse2 = jnp.concatenate([lse_t[h0:h0 + 1, :], lse_t[h0 + 1:h0 + 2, :]], axis=1)
                dl2 = jnp.concatenate([dl_t[h0:h0 + 1, :], dl_t[h0 + 1:h0 + 2, :]], axis=1)
                p_t = jnp.exp(_nt(kcat, qq) + (bias2 - lse2))
                ds_t = p_t * (_nt(vcat, dd) - dl2)
                dsb = ds_t.astype(BF16)
                dkc = jnp.dot(dsb, qq, preferred_element_type=F32)
                dvc = jnp.dot(p_t.astype(BF16), dd, preferred_element_type=F32)
                dqb = _tn(dsb, kcat)
                dq2 = jnp.where(lo, dqb[:BLOCK], dqb[BLOCK:]) * ATTN_SCALE
                dk2 = ck[:, sl] + dkc[:BLOCK]
                dv2 = cv[:, sl] + dvc[:BLOCK]
                if not first:
                    dq2 = dq2 + mode.get(dqi_ref, sl)
                    dk2 = dk2 + mode.get(dki_ref, sl)
                    dv2 = dv2 + mode.get(dvi_ref, sl)
                mode.put(dq_ref, sl, dq2)
                mode.put(dk_ref, sl, dk2)
                mode.put(dv_ref, sl, dv2)
                ck[:, sl] = dkc[BLOCK:]
                cv[:, sl] = dvc[BLOCK:]

        @pl.when(n == nb)
        def _():
            for sl in groups:
                if first:
                    mode.put(dk_ref, sl, ck[:, sl])
                    mode.put(dv_ref, sl, cv[:, sl])
                else:
                    mode.put(dk_ref, sl, ck[:, sl] + mode.get(dki_ref, sl))
                    mode.put(dv_ref, sl, cv[:, sl] + mode.get(dvi_ref, sl))

    last = nb - 1
    cur = lambda w: mode.spec(w, "cur", last)
    prev = lambda w: mode.spec(w, "prev")
    ins = [q, k, k, v, v, d_o, lse, delta]
    specs = [cur(attn_w), prev(attn_w), cur(attn_w), prev(attn_w), cur(attn_w), cur(attn_w), cur(LANES), cur(LANES)]
    if not first:
        ins += list(run)
        specs += [cur(attn_w), prev(attn_w), prev(attn_w)]
    shp = jax.ShapeDtypeStruct((seq, attn_w) if nat else _perm_shape(seq, attn_w), F32)
    return pl.pallas_call(
        body, name=f"attn_bwd_{name}", grid=(mode.residues, nb + 1),
        out_shape=(shp, shp, shp), in_specs=specs, out_specs=(cur(attn_w), prev(attn_w), prev(attn_w)),
        scratch_shapes=[pltpu.VMEM((BLOCK, attn_w), F32), pltpu.VMEM((BLOCK, attn_w), F32)],
        compiler_params=_params(("arbitrary", "arbitrary")),
    )(*ins)


def _shift_down(u, halo, k):
    rolled = pltpu.roll(u, k, 0)
    row = lax.broadcasted_iota(jnp.int32, halo.shape, 0)
    top = jnp.where(row < k, pltpu.roll(halo, k, 0), rolled[:SUBLANES])
    return jnp.concatenate([top, rolled[SUBLANES:]], axis=0)


def _shift_up(u, halo, k):
    rows = u.shape[0]
    rolled = pltpu.roll(u, rows - k, 0)
    row = lax.broadcasted_iota(jnp.int32, halo.shape, 0)
    bot = jnp.where(row >= SUBLANES - k, pltpu.roll(halo, SUBLANES - k, 0), rolled[rows - SUBLANES:])
    return jnp.concatenate([rolled[:rows - SUBLANES], bot], axis=0)


def tail(o, lse, ga, cz, x, tgt, w_out, g2, cw):
    seq, d_model = x.shape
    attn_w = o.shape[1]
    conv_w = cz.shape[1] // 4
    mix = attn_w + conv_w
    groups = _lane_groups(attn_w)
    tm = ROW_TILE
    nt = seq // tm
    hb = tm // SUBLANES

    def body(o_ref, l_ref, ga_ref, cz_ref, hz_ref, x_ref, t_ref, w_ref, g_ref, cw_ref,
             do_ref, dl_ref, dop_ref, dlp_ref, lp_ref, dga_ref, dcb_ref, dgc_ref, dcv_ref, e_ref,
             dw_ref, dg_ref, dcw_ref, loss_ref, stage):
        i = pl.program_id(0)

        @pl.when(i == 0)
        def _():
            dw_ref[...] = jnp.zeros_like(dw_ref)
            dg_ref[...] = jnp.zeros_like(dg_ref)
            dcw_ref[...] = jnp.zeros_like(dcw_ref)
            loss_ref[...] = jnp.zeros_like(loss_ref)

        ov, gav = o_ref[...], ga_ref[...]
        sig_a = _sigmoid(gav)
        silu_a = gav * sig_a
        attn_out = ov * silu_a
        ch, cb = cz_ref[:, 0:conv_w], cz_ref[:, conv_w:2 * conv_w]
        cc, gc = cz_ref[:, 2 * conv_w:3 * conv_w], cz_ref[:, 3 * conv_w:4 * conv_w]
        u = cc * ch
        uh = hz_ref[:, 2 * conv_w:3 * conv_w] * hz_ref[:, 0:conv_w]
        uh = jnp.where(i > 0, uh, 0.0)
        u1 = _shift_down(u, uh, 1)
        u2 = _shift_down(u, uh, 2)
        w0, w1, w2 = cw_ref[0:1, :], cw_ref[1:2, :], cw_ref[2:3, :]
        cvv = u2 * w0 + u1 * w1 + u * w2
        sig_c = _sigmoid(gc)
        silu_c = gc * sig_c
        bc = cb * cvv
        conv_out = bc * silu_c
        mixed = jnp.concatenate([attn_out, conv_out], axis=1).astype(BF16)

        yv = jnp.dot(mixed, w_ref[...], preferred_element_type=F32)
        r2 = lax.rsqrt(jnp.mean(yv * yv, axis=-1, keepdims=True) + NORM_EPS)
        yhat = yv * r2
        gv = g_ref[...]
        diff = (x_ref[...] + yhat * gv) - t_ref[...]
        loss_ref[...] += _rowgroup_sum(diff * diff)
        ev = diff * (1.0 / d_model)
        e_ref[...] = ev
        dg_ref[...] += _rowgroup_sum(ev * yhat)
        eg = ev * gv
        dy = (r2 * (eg - yhat * jnp.mean(eg * yhat, axis=-1, keepdims=True))).astype(BF16)
        dw_ref[...] += _tn(mixed, dy)
        dm = _nt(dy, w_ref[...])
        dma, dmc = dm[:, :attn_w], dm[:, attn_w:]

        dov = dma * silu_a
        do_ref[...] = dov.astype(BF16)
        dga_ref[...] = (dma * ov * (sig_a * (1.0 + gav * (1.0 - sig_a)))).astype(BF16)
        prod = dov * ov
        lane = lax.broadcasted_iota(jnp.int32, (tm, LANES), 1)
        lo = lane < HEAD_DIM
        dblk = jnp.zeros((tm, LANES), F32)
        for p, sl in enumerate(groups):
            pr = prod[:, sl]
            dblk = jnp.where(lane == 2 * p, jnp.sum(jnp.where(lo, pr, 0.0), axis=1, keepdims=True), dblk)
            dblk = jnp.where(lane == 2 * p + 1, jnp.sum(jnp.where(lo, 0.0, pr), axis=1, keepdims=True), dblk)
            stage[p] = dov[:, sl]
        dl_ref[...] = dblk
        for p, sl in enumerate(groups):
            _to_perm(stage, p, dop_ref, sl, BF16)
        all_lanes = slice(0, LANES)
        stage[0] = dblk
        stage[1] = l_ref[...]
        _to_perm(stage, 0, dlp_ref, all_lanes, F32)
        _to_perm(stage, 1, lp_ref, all_lanes, F32)

        dsc = dmc * silu_c
        dcb_ref[...] = (dsc * cvv).astype(BF16)
        dgc_ref[...] = (dmc * bc * (sig_c * (1.0 + gc * (1.0 - sig_c)))).astype(BF16)
        dcv = dsc * cb
        dcv_ref[...] = dcv
        dcw_ref[0:SUBLANES, :] += _rowgroup_sum(dcv * u2)
        dcw_ref[SUBLANES:2 * SUBLANES, :] += _rowgroup_sum(dcv * u1)
        dcw_ref[2 * SUBLANES:3 * SUBLANES, :] += _rowgroup_sum(dcv * u)

    row = lambda n: pl.BlockSpec((tm, n), lambda i: (i, 0))
    whole = lambda a, b: pl.BlockSpec((a, b), lambda i: (0, 0))
    return pl.pallas_call(
        body, name="tail", grid=(nt,),
        out_shape=(jax.ShapeDtypeStruct((seq, attn_w), BF16), jax.ShapeDtypeStruct((seq, LANES), F32),
                   jax.ShapeDtypeStruct(_perm_shape(seq, attn_w), BF16), jax.ShapeDtypeStruct(_perm_shape(seq, LANES), F32),
                   jax.ShapeDtypeStruct(_perm_shape(seq, LANES), F32),
                   jax.ShapeDtypeStruct((seq, attn_w), BF16), jax.ShapeDtypeStruct((seq, conv_w), BF16),
                   jax.ShapeDtypeStruct((seq, conv_w), BF16), jax.ShapeDtypeStruct((seq, conv_w), F32),
                   jax.ShapeDtypeStruct((seq, d_model), F32), jax.ShapeDtypeStruct((mix, d_model), F32),
                   jax.ShapeDtypeStruct((SUBLANES, d_model), F32), jax.ShapeDtypeStruct((CONV_K * SUBLANES, conv_w), F32),
                   jax.ShapeDtypeStruct((SUBLANES, d_model), F32)),
        in_specs=[row(attn_w), row(LANES), row(attn_w), row(4 * conv_w),
                  pl.BlockSpec((SUBLANES, 4 * conv_w), lambda i: (jnp.maximum(i * hb - 1, 0), 0)),
                  row(d_model), row(d_model), whole(mix, d_model), whole(1, d_model), whole(SUBLANES, conv_w)],
        out_specs=(row(attn_w), row(LANES), _perm_tile_spec(attn_w), _perm_tile_spec(LANES), _perm_tile_spec(LANES),
                   row(attn_w), row(conv_w), row(conv_w), row(conv_w), row(d_model),
                   whole(mix, d_model), whole(SUBLANES, d_model), whole(CONV_K * SUBLANES, conv_w),
                   whole(SUBLANES, d_model)),
        scratch_shapes=[pltpu.VMEM((max(len(groups), 2), tm, LANES), F32)],
        compiler_params=_params(("arbitrary",)),
    )(o, lse, ga, cz, cz, x, tgt, w_out, g2, cw)


def dz_dx(nat_grads, perm_grads, dga, dcb, dgc, dcv, cz, tables, x, g1, e, w_full, cw):
    seq, d_model = x.shape
    attn_w = dga.shape[1]
    conv_w = dcv.shape[1]
    width = w_full.shape[2]
    in_w = 4 * attn_w + 4 * conv_w
    groups = _lane_groups(attn_w)
    tm = ROW_TILE
    nt = seq // tm
    hb = tm // SUBLANES

    def body(dq_ref, dk_ref, dv_ref, dqp_ref, dkp_ref, dvp_ref, dga_ref, dcb_ref, dgc_ref, dcv_ref, nh_ref, cz_ref,
             cos_ref, s1_ref, s2_ref, x_ref, g_ref, e_ref, w_ref, cw_ref, gx_ref, dz_ref, dg_ref, stage):
        i = pl.program_id(0)

        @pl.when(i == 0)
        def _():
            dg_ref[...] = jnp.zeros_like(dg_ref)

        cos, s1, s2 = cos_ref[...], s1_ref[...], s2_ref[...]
        for t, (nat_ref, perm_ref) in enumerate(((dq_ref, dqp_ref), (dk_ref, dkp_ref), (dv_ref, dvp_ref))):
            for g, sl in enumerate(groups):
                _from_perm(perm_ref, sl, stage, g)
            for g, sl in enumerate(groups):
                tot = nat_ref[:, sl] + stage[g]
                if t < 2:
                    tot = _rope_transposed(tot, cos, s1, s2)
                dz_ref[:, t * attn_w + g * LANES:t * attn_w + (g + 1) * LANES] = tot.astype(BF16)
        dz_ref[:, 3 * attn_w:4 * attn_w] = dga_ref[...]
        dcv = dcv_ref[...]
        nh = jnp.where(i < nt - 1, nh_ref[...], 0.0)
        w0, w1, w2 = cw_ref[0:1, :], cw_ref[1:2, :], cw_ref[2:3, :]
        du = dcv * w2 + _shift_up(dcv, nh, 1) * w1 + _shift_up(dcv, nh, 2) * w0
        base = 4 * attn_w
        dz_ref[:, base:base + conv_w] = (du * cz_ref[:, 2 * conv_w:3 * conv_w]).astype(BF16)
        dz_ref[:, base + conv_w:base + 2 * conv_w] = dcb_ref[...]
        dz_ref[:, base + 2 * conv_w:base + 3 * conv_w] = (du * cz_ref[:, 0:conv_w]).astype(BF16)
        dz_ref[:, base + 3 * conv_w:base + 4 * conv_w] = dgc_ref[...]

        dh = _nt(dz_ref[:, 0:width], w_ref[0])
        for j in range(1, N_CHIPS):
            dh = dh + _nt(dz_ref[:, j * width:(j + 1) * width], w_ref[j])
        xv = x_ref[...]
        r1 = lax.rsqrt(jnp.mean(xv * xv, axis=-1, keepdims=True) + NORM_EPS)
        xhat = xv * r1
        dg_ref[...] += _rowgroup_sum(dh * xhat)
        dhg = dh * g_ref[...]
        gx_ref[...] = r1 * (dhg - xhat * jnp.mean(dhg * xhat, axis=-1, keepdims=True)) + e_ref[...]

    row = lambda n: pl.BlockSpec((tm, n), lambda i: (i, 0))
    whole = lambda a, b: pl.BlockSpec((a, b), lambda i: (0, 0))
    pt = _perm_tile_spec(attn_w)
    return pl.pallas_call(
        body, name="dz_dx", grid=(nt,),
        out_shape=(jax.ShapeDtypeStruct((seq, d_model), F32), jax.ShapeDtypeStruct((seq, in_w), BF16),
                   jax.ShapeDtypeStruct((SUBLANES, d_model), F32)),
        in_specs=[row(attn_w), row(attn_w), row(attn_w), pt, pt, pt, row(attn_w), row(conv_w), row(conv_w), row(conv_w),
                  pl.BlockSpec((SUBLANES, conv_w), lambda i: (jnp.minimum((i + 1) * hb, seq // SUBLANES - 1), 0)),
                  row(4 * conv_w), row(LANES), row(LANES), row(LANES), row(d_model), whole(1, d_model), row(d_model),
                  pl.BlockSpec(w_full.shape, lambda i: (0, 0, 0)), whole(SUBLANES, conv_w)],
        out_specs=(row(d_model), row(in_w), whole(SUBLANES, d_model)),
        scratch_shapes=[pltpu.VMEM((len(groups), tm, LANES), F32)],
        compiler_params=_params(("arbitrary",)),
    )(*nat_grads, *perm_grads, dga, dcb, dgc, dcv, dcv, cz, *tables, x, g1, e, w_full, cw)


def dw_in(h, dz):
    seq, d_model = h.shape
    half = dz.shape[1] // N_DEV
    ts = 512
    steps = seq // ts

    def body(h_ref, dz_ref, o_ref):
        @pl.when(pl.program_id(1) == 0)
        def _():
            o_ref[...] = jnp.zeros_like(o_ref)

        o_ref[0] += _tn(h_ref[...], dz_ref[...])

    return pl.pallas_call(
        body, name="dw_in", grid=(N_DEV, steps),
        out_shape=jax.ShapeDtypeStruct((N_DEV, d_model, half), F32),
        in_specs=[pl.BlockSpec((ts, d_model), lambda j, s: (s, 0)), pl.BlockSpec((ts, half), lambda j, s: (s, j))],
        out_specs=pl.BlockSpec((1, d_model, half), lambda j, s: (j, 0, 0)),
        compiler_params=_params(("arbitrary", "arbitrary")),
    )(h, dz)


def grad_reduce(g_in, g_out, small):
    gi = g_in.reshape(N_CHIPS, 2, *g_in.shape[1:])
    go = g_out.reshape(N_CHIPS, 2, *g_out.shape[1:])
    shp_i, shp_o = gi.shape[2:], go.shape[2:]

    def body(gi_ref, go_ref, sm_ref, ri_ref, ro_ref, rs_ref, a_i, b_i, c_i, a_o, b_o, c_o, sbuf,
             loc_sems, sa, ra, sb, rb, sc, rc, ss, rs):
        x, y, c = lax.axis_index("x"), lax.axis_index("y"), lax.axis_index("c")
        me = 2 * x + y
        sib = (x, y, 1 - c)
        srcs, mine, theirs, contrib, res = (gi_ref, go_ref), (a_i, a_o), (b_i, b_o), (c_i, c_o), (ri_ref, ro_ref)

        flips = [(fx, fy, fc) for fx in (0, 1) for fy in (0, 1) for fc in (0, 1)][1:]
        my8 = 4 * x + 2 * y + c
        sbuf[my8] = sm_ref[...]

        def small_copy(k, slot, to):
            return pltpu.make_async_remote_copy(src_ref=sm_ref, dst_ref=sbuf.at[slot], send_sem=ss.at[k], recv_sem=rs.at[k],
                                                device_id=to, device_id_type=MESH)

        small_sends = []
        for k, (fx, fy, fc) in enumerate(flips):
            px, py, pc = _flip(x, fx), _flip(y, fy), _flip(c, fc)
            small_sends.append(small_copy(k, my8, (px, py, pc)))
            small_sends[-1].start()

        def a_copy(t):
            return pltpu.make_async_remote_copy(src_ref=srcs[t].at[:, 1 - c], dst_ref=theirs[t], send_sem=sa.at[t],
                                                recv_sem=ra.at[t], device_id=sib, device_id_type=MESH)

        loads = [pltpu.make_async_copy(srcs[t].at[:, c], mine[t], loc_sems.at[t]) for t in range(2)]
        a_sends = [a_copy(t) for t in range(2)]
        for cp in loads + a_sends:
            cp.start()
        for t in range(2):
            loads[t].wait()
            a_copy(t).wait_recv()
            mine[t][...] = mine[t][...] + theirs[t][...]

        def b_copy(k, t, piece, slot, to):
            return pltpu.make_async_remote_copy(src_ref=mine[t].at[piece], dst_ref=contrib[t].at[slot], send_sem=sb.at[k, t],
                                                recv_sem=rb.at[k, t], device_id=to, device_id_type=MESH)

        peers = _chip_peers(x, y)
        b_sends = [b_copy(k, t, 2 * px + py, me, (px, py, c)) for k, (px, py) in enumerate(peers) for t in range(2)]
        for cp in b_sends:
            cp.start()
        for t in range(2):
            contrib[t][me] = mine[t][me]
        for k, (px, py) in enumerate(peers):
            for t in range(2):
                b_copy(k, t, me, 2 * px + py, (px, py, c)).wait_recv()

        def c_copy(t, half):
            return pltpu.make_async_remote_copy(src_ref=res[t].at[half], dst_ref=res[t].at[half], send_sem=sc.at[t],
                                                recv_sem=rc.at[t], device_id=sib, device_id_type=MESH)

        c_sends = []
        for t in range(2):
            res[t][c] = ((contrib[t][0] + contrib[t][1]) + contrib[t][2]) + contrib[t][3]
            c_sends.append(c_copy(t, c))
            c_sends[-1].start()
        for t in range(2):
            c_copy(t, 1 - c).wait_recv()

        for k, (fx, fy, fc) in enumerate(flips):
            px, py, pc = _flip(x, fx), _flip(y, fy), _flip(c, fc)
            small_copy(k, 4 * px + 2 * py + pc, (px, py, pc)).wait_recv()
        tot = sbuf[0]
        for d in range(1, N_DEV):
            tot = tot + sbuf[d]
        rs_ref[...] = tot
        for cp in small_sends + a_sends + b_sends + c_sends:
            cp.wait_send()

    vm = pl.BlockSpec(memory_space=pltpu.VMEM)
    anyspace = pl.BlockSpec(memory_space=pl.ANY)
    dma = pltpu.SemaphoreType.DMA
    return pl.pallas_call(
        body, name="grad_reduce",
        out_shape=(jax.ShapeDtypeStruct((2, *shp_i), F32), jax.ShapeDtypeStruct((2, *shp_o), F32),
                   jax.ShapeDtypeStruct(small.shape, F32)),
        in_specs=[anyspace, anyspace, vm], out_specs=(vm, vm, vm),
        scratch_shapes=[pltpu.VMEM((N_CHIPS, *shp_i), F32), pltpu.VMEM((N_CHIPS, *shp_i), F32), pltpu.VMEM((N_CHIPS, *shp_i), F32),
                        pltpu.VMEM((N_CHIPS, *shp_o), F32), pltpu.VMEM((N_CHIPS, *shp_o), F32), pltpu.VMEM((N_CHIPS, *shp_o), F32),
                        pltpu.VMEM((N_DEV, *small.shape), F32),
                        dma((2,)), dma((2,)), dma((2,)), dma((3, 2)), dma((3, 2)), dma((2,)), dma((2,)),
                        dma((N_DEV - 1,)), dma((N_DEV - 1,))],
        compiler_params=_params(),
    )(gi, go, small)


def _adam_math(w, g, m, v):
    m = ADAM_B1 * m + (1.0 - ADAM_B1) * g
    v = ADAM_B2 * v + (1.0 - ADAM_B2) * (g * g)
    m_hat = m / (1.0 - ADAM_B1 ** ADAM_STEP)
    v_hat = v / (1.0 - ADAM_B2 ** ADAM_STEP)
    delta = -ADAM_LR * (m_hat / (jnp.sqrt(v_hat) + ADAM_EPS) + ADAM_WD * w)
    return delta, m, v


def adam_shard(name, w, g2, m, v, block, grid, w_map, g_map):
    def body(w_ref, g_ref, m_ref, v_ref, go_ref, d_ref, mo_ref, vo_ref):
        g = g_ref[0]
        delta, mn, vn = _adam_math(w_ref[...], g, m_ref[...], v_ref[...])
        go_ref[...] = g
        d_ref[...] = delta
        mo_ref[...] = mn
        vo_ref[...] = vn

    ws = pl.BlockSpec(block, w_map)
    shp = jax.ShapeDtypeStruct(w.shape, F32)
    return pl.pallas_call(
        body, name=name, grid=grid, out_shape=(shp, shp, shp, shp),
        in_specs=[ws, pl.BlockSpec((1, *block), g_map), ws, ws], out_specs=(ws, ws, ws, ws),
        compiler_params=_params(("arbitrary",) * len(grid)),
    )(w, g2, m, v)


def adam_small(ws, gs, ms, vs):
    n = len(ws)

    def body(*refs):
        ins, outs = refs[:4 * n], refs[4 * n:]
        for t in range(n):
            delta, mn, vn = _adam_math(ins[t][...], ins[n + t][...], ins[2 * n + t][...], ins[3 * n + t][...])
            outs[3 * t][...] = delta
            outs[3 * t + 1][...] = mn
            outs[3 * t + 2][...] = vn

    vm = pl.BlockSpec(memory_space=pltpu.VMEM)
    outs = pl.pallas_call(
        body, name="adam_small",
        out_shape=tuple(jax.ShapeDtypeStruct(w.shape, F32) for w in ws for _ in range(3)),
        in_specs=[vm] * (4 * n), out_specs=tuple([vm] * (3 * n)),
        compiler_params=_params(),
    )(*ws, *gs, *ms, *vs)
    return [outs[3 * t:3 * t + 3] for t in range(n)]


def kernel(x, norm_pre_g, w_in, conv_w, w_out, norm_post_g, loss_target, m_norm_pre_g, m_w_in, m_conv_w, m_w_out, m_norm_post_g, v_norm_pre_g, v_w_in, v_conv_w, v_w_out, v_norm_post_g):
    _, seq, d_model = x.shape
    width = w_in.shape[1]
    conv_q = conv_w.shape[1]
    conv_width = N_CHIPS * conv_q
    attn_width = d_model - conv_width
    xs, tg = x[0], loss_target[0]
    g1, g2 = norm_pre_g.reshape(1, d_model), norm_post_g.reshape(1, d_model)

    w_full, wout_full, cw_full = gather_weights(w_in, w_out, conv_w)
    wout2 = wout_full.reshape(attn_width + conv_width, d_model)
    cw = jnp.zeros((SUBLANES, conv_width), F32).at[:CONV_K].set(
        cw_full[:, :CONV_K, :conv_q].transpose(1, 0, 2).reshape(CONV_K, conv_width))
    tables = _rope_tables(seq)

    h, q, k, v, qp, kp, vp, ga, cz = inproj(xs, g1, w_full, tables, attn_width, conv_width)
    run = attn_fwd("p4", qp, kp, vp, None)
    run = attn_fwd("p16", qp, kp, vp, run)
    o, lse = attn_fwd("nat", q, k, v, run)
    (d_o, delta, d_op, delta_p, lse_p, dga, dcb, dgc, dcv, e, dwout, dg2, dcw, loss_acc) = tail(
        o, lse, ga, cz, xs, tg, wout2, g2, cw)
    nat_grads = attn_bwd("nat", q, k, v, d_o, lse, delta, None)
    perm_grads = attn_bwd("p4", qp, kp, vp, d_op, lse_p, delta_p, None)
    perm_grads = attn_bwd("p16", qp, kp, vp, d_op, lse_p, delta_p, perm_grads)
    grad_x, dz, dg1 = dz_dx(nat_grads, perm_grads, dga, dcb, dgc, dcv, cz, tables, xs, g1, e, w_full, cw)
    dwin = dw_in(h, dz)

    small = jnp.zeros((SUBLANES, d_model), F32)
    small = small.at[0].set(dg1.sum(axis=0)).at[1].set(dg2.sum(axis=0))
    small = small.at[2:2 + CONV_K, :conv_width].set(dcw.reshape(CONV_K, SUBLANES, conv_width).sum(axis=1))
    rin, rout, rsmall = grad_reduce(dwin, dwout.reshape(N_DEV, -1, d_model), small)

    half = width // 2
    tr = 256
    gw_in, d_in, m_in, v_in = adam_shard(
        "adam_w_in", w_in, rin, m_w_in, v_w_in, (tr, half), (2, d_model // tr),
        lambda hf, i: (i, hf), lambda hf, i: (hf, i, 0))
    rq = w_out.shape[0] // 2
    gw_out, d_out, m_out, v_out = adam_shard(
        "adam_w_out", w_out, rout, m_w_out, v_w_out, (rq, d_model), (2,),
        lambda hf: (hf, 0), lambda hf: (hf, 0, 0))

    chip = 2 * lax.axis_index("x") + lax.axis_index("y")
    g_pre, g_post = rsmall[0:1], rsmall[1:2]
    g_conv = lax.dynamic_slice(rsmall[2:2 + CONV_K, :conv_width], (0, chip * conv_q), (CONV_K, conv_q))
    (d_pre, m_pre, v_pre), (d_post, m_post, v_post), (d_cv, m_cv, v_cv) = adam_small(
        [g1, g2, conv_w], [g_pre, g_post, g_conv],
        [m_norm_pre_g.reshape(1, d_model), m_norm_post_g.reshape(1, d_model), m_conv_w],
        [v_norm_pre_g.reshape(1, d_model), v_norm_post_g.reshape(1, d_model), v_conv_w])

    loss = lax.psum(0.5 * jnp.sum(loss_acc) / d_model, ("x", "y", "c"))
    vec = lambda a: a.reshape(d_model)
    return (loss, grad_x.reshape(1, seq, d_model),
            vec(g_pre), gw_in, g_conv, gw_out, vec(g_post),
            vec(d_pre), d_in, d_cv, d_out, vec(d_post),
            vec(m_pre), m_in, m_cv, m_out, vec(m_post),
            vec(v_pre), v_in, v_cv, v_out, vec(v_post))
```

```python
import jax
import jax.numpy as jnp
from jax import lax
from jax.experimental import pallas as pl
from jax.experimental.pallas import tpu as pltpu

HEAD_DIM = 64
LANES = 128
SUBLANES = 8
BLOCK = 128
WINDOW_KEYS = 128
PERM = 16
PJ = 4
P4_ROWS = BLOCK // PJ
ROW_TILE = 512
DZ_ROW_TILE = 256
CONV_K = 3
ROPE_THETA = 10000.0
NORM_EPS = 1e-6
ATTN_SCALE = HEAD_DIM ** -0.5
NEG = -1e30
N_CHIPS = 4
N_DEV = 8
MESH = pl.DeviceIdType.MESH
ADAM_LR = 0.001
ADAM_B1 = 0.9
ADAM_B2 = 0.999
ADAM_EPS = 1e-08
ADAM_WD = 0.01
ADAM_STEP = 10
VMEM_LIMIT = 52 * 1024 * 1024

F32 = jnp.float32
BF16 = jnp.bfloat16


def _params(sem=None, **kw):
    return pltpu.CompilerParams(dimension_semantics=sem, vmem_limit_bytes=VMEM_LIMIT, **kw)


def _const_spec(shape):
    return pl.BlockSpec(shape, lambda *_: (0,) * len(shape), pipeline_mode=pl.Buffered(1))


def _sigmoid(z):
    return 1.0 / (1.0 + jnp.exp(-z))


def _rowgroup_sum(a):
    rows, n = a.shape
    return a.reshape(rows // SUBLANES, SUBLANES, n).sum(axis=0)


def _nt(a, b):
    return lax.dot_general(a, b, (((1,), (1,)), ((), ())), preferred_element_type=F32)


def _tn(a, b):
    return lax.dot_general(a, b, (((0,), (0,)), ((), ())), preferred_element_type=F32)


def _col_pieces(a, b, width):
    out = []
    while a < b:
        j = a // width
        e = min(b, (j + 1) * width)
        out.append((j, a - j * width, e - j * width))
        a = e
    return out


def _lane_groups(width):
    return [slice(g * LANES, (g + 1) * LANES) for g in range(width // LANES)]


def _perm_shape(seq, width):
    return (PJ, PJ, seq // PERM, width)


def _perm_tile_spec(width, tm):
    return pl.BlockSpec((PJ, PJ, tm // PERM, width), lambda i: (0, 0, i, 0))


def _to_perm(stage, g, dst_ref, sl, dtype):
    rows = stage.shape[1] // PERM
    for b in range(PERM):
        dst_ref[b // PJ, b % PJ, :, sl] = stage[g, pl.ds(b, rows, stride=PERM), :].astype(dtype)


def _from_perm(src_ref, sl, stage, g):
    rows = stage.shape[1] // PERM
    for b in range(PERM):
        stage[g, pl.ds(b, rows, stride=PERM), :] = src_ref[b // PJ, b % PJ, :, sl]


def _flip(a, f):
    return 1 - a if f else a


def _chip_peers(x, y):
    return [(1 - x, y), (x, 1 - y), (1 - x, 1 - y)]


def gather_weights(w_in, w_out, conv_w):
    d_model, width = w_in.shape
    rows = w_out.shape[0]
    cw = jnp.zeros((SUBLANES, LANES), F32).at[:CONV_K, :conv_w.shape[1]].set(conv_w)

    def body(win_ref, wout_ref, cw_ref, winf_ref, woutf_ref, cwf_ref, st_in, st_out, send_sems, recv_sems):
        x, y, c = lax.axis_index("x"), lax.axis_index("y"), lax.axis_index("c")
        me = 2 * x + y
        st_in[...] = win_ref[...].astype(BF16)
        st_out[...] = wout_ref[...].astype(BF16)
        winf_ref[me] = st_in[...]
        woutf_ref[me] = st_out[...]
        cwf_ref[me] = cw_ref[...]
        srcs = (st_in, st_out, cw_ref)
        dsts = (winf_ref, woutf_ref, cwf_ref)

        def copy(k, t, slot, to):
            return pltpu.make_async_remote_copy(
                src_ref=srcs[t], dst_ref=dsts[t].at[slot], send_sem=send_sems.at[k, t], recv_sem=recv_sems.at[k, t],
                device_id=to, device_id_type=MESH)

        peers = _chip_peers(x, y)
        sends = [copy(k, t, me, (px, py, c)) for k, (px, py) in enumerate(peers) for t in range(3)]
        for cp in sends:
            cp.start()
        for k, (px, py) in enumerate(peers):
            for t in range(3):
                copy(k, t, 2 * px + py, (px, py, c)).wait_recv()
        for cp in sends:
            cp.wait_send()

    vm = pl.BlockSpec(memory_space=pltpu.VMEM)
    return pl.pallas_call(
        body, name="gather_weights",
        out_shape=(jax.ShapeDtypeStruct((N_CHIPS, d_model, width), BF16),
                   jax.ShapeDtypeStruct((N_CHIPS, rows, d_model), BF16),
                   jax.ShapeDtypeStruct((N_CHIPS, SUBLANES, LANES), F32)),
        in_specs=[vm, vm, vm], out_specs=(vm, vm, vm),
        scratch_shapes=[pltpu.VMEM((d_model, width), BF16), pltpu.VMEM((rows, d_model), BF16),
                        pltpu.SemaphoreType.DMA((3, 3)), pltpu.SemaphoreType.DMA((3, 3))],
        compiler_params=_params(),
    )(w_in, w_out, cw)


def _rope_tables(seq):
    half = HEAD_DIM // 2
    inv_freq = ROPE_THETA ** (-jnp.arange(half, dtype=F32) * 2.0 / HEAD_DIM)
    ang = jnp.arange(seq).astype(F32)[:, None] * inv_freq[None, :]
    cos, sin = jnp.cos(ang), jnp.sin(ang)
    zero = jnp.zeros_like(sin)
    return (jnp.concatenate([cos, cos, cos, cos], axis=1),
            jnp.concatenate([-sin, zero, -sin, zero], axis=1),
            jnp.concatenate([zero, sin, zero, sin], axis=1))


def _rope(t, cos, s1, s2):
    return t * cos + pltpu.roll(t, LANES - HEAD_DIM // 2, 1) * s1 + pltpu.roll(t, HEAD_DIM // 2, 1) * s2


def _rope_transposed(g, cos, s1, s2):
    return g * cos + pltpu.roll(g * s1, HEAD_DIM // 2, 1) + pltpu.roll(g * s2, LANES - HEAD_DIM // 2, 1)


def inproj(x, g1, w_full, tables, attn_w, conv_w):
    seq, d_model = x.shape
    width = w_full.shape[2]
    tm = ROW_TILE
    groups = _lane_groups(attn_w)

    def body(x_ref, g_ref, w_ref, cos_ref, s1_ref, s2_ref,
             h_ref, q_ref, k_ref, v_ref, qp_ref, kp_ref, vp_ref, ga_ref, cz_ref, stage):
        xv = x_ref[...]
        hb = ((xv * lax.rsqrt(jnp.mean(xv * xv, axis=-1, keepdims=True) + NORM_EPS)) * g_ref[...]).astype(BF16)
        h_ref[...] = hb
        cos, s1, s2 = cos_ref[...], s1_ref[...], s2_ref[...]

        def proj(a, b):
            parts = [jnp.dot(hb, w_ref[j, :, lo:hi], preferred_element_type=F32) for j, lo, hi in _col_pieces(a, b, width)]
            return parts[0] if len(parts) == 1 else jnp.concatenate(parts, axis=1)

        def emit(z, nat_ref, perm_ref, fn):
            for g, sl in enumerate(groups):
                val = fn(z[:, sl])
                nat_ref[:, sl] = val.astype(BF16)
                stage[g] = val
            for g, sl in enumerate(groups):
                _to_perm(stage, g, perm_ref, sl, BF16)

        emit(proj(0, attn_w), q_ref, qp_ref, lambda t: _rope(t, cos, s1, s2) * ATTN_SCALE)
        emit(proj(attn_w, 2 * attn_w), k_ref, kp_ref, lambda t: _rope(t, cos, s1, s2))
        emit(proj(2 * attn_w, 3 * attn_w), v_ref, vp_ref, lambda t: t)
        ga_ref[...] = proj(3 * attn_w, 4 * attn_w)
        cz_ref[...] = proj(4 * attn_w, 4 * attn_w + 4 * conv_w)

    row = lambda n: pl.BlockSpec((tm, n), lambda i: (i, 0))
    nat = jax.ShapeDtypeStruct((seq, attn_w), BF16)
    perm = jax.ShapeDtypeStruct(_perm_shape(seq, attn_w), BF16)
    return pl.pallas_call(
        body, name="inproj", grid=(seq // tm,),
        out_shape=(jax.ShapeDtypeStruct((seq, d_model), BF16), nat, nat, nat, perm, perm, perm,
                   jax.ShapeDtypeStruct((seq, attn_w), F32), jax.ShapeDtypeStruct((seq, 4 * conv_w), F32)),
        in_specs=[row(d_model), _const_spec((1, d_model)), _const_spec(w_full.shape), row(LANES), row(LANES), row(LANES)],
        out_specs=(row(d_model), row(attn_w), row(attn_w), row(attn_w),
                   _perm_tile_spec(attn_w, tm), _perm_tile_spec(attn_w, tm), _perm_tile_spec(attn_w, tm),
                   row(attn_w), row(4 * conv_w)),
        scratch_shapes=[pltpu.VMEM((len(groups), tm, LANES), F32)],
        compiler_params=_params(("arbitrary",)),
    )(x, g1, w_full, *tables)


class _Mode:
    def __init__(self, name, seq):
        self.name = name
        if name == "nat":
            self.residues, self.nb = 1, seq // BLOCK
        elif name == "p16":
            self.residues, self.nb = PERM, seq // PERM // BLOCK
        else:
            self.residues, self.nb = PJ, seq // PERM // P4_ROWS

    def spec(self, width, which, last=None):
        if which == "prev":
            blk = lambda n: jnp.maximum(n - 1, 0)
        elif last is None:
            blk = lambda n: n
        else:
            blk = lambda n: jnp.minimum(n, last)
        if self.name == "nat":
            return pl.BlockSpec((BLOCK, width), lambda r, n: (blk(n), 0))
        if self.name == "p16":
            return pl.BlockSpec((1, 1, BLOCK, width), lambda r, n: (r // PJ, r % PJ, blk(n), 0))
        return pl.BlockSpec((PJ, 1, P4_ROWS, width), lambda r, n: (0, r, blk(n), 0))

    def get(self, ref, sl):
        if self.name == "nat":
            return ref[:, sl]
        if self.name == "p16":
            return ref[0, 0, :, sl]
        return jnp.concatenate([ref[j, 0, :, sl] for j in range(PJ)], axis=0)

    def put(self, ref, sl, val):
        if self.name == "nat":
            ref[:, sl] = val
        elif self.name == "p16":
            ref[0, 0, :, sl] = val
        else:
            for j in range(PJ):
                ref[j, 0, :, sl] = val[j * P4_ROWS:(j + 1) * P4_ROWS]

    def index(self, idx, is_key):
        if self.name != "p4":
            return idx - BLOCK if is_key else idx
        within = jnp.bitwise_and(idx, BLOCK - 1)
        m = PJ * jnp.bitwise_and(within, P4_ROWS - 1) + jnp.right_shift(within, P4_ROWS.bit_length() - 1)
        return m + BLOCK * (jnp.right_shift(idx, BLOCK.bit_length() - 1) - 1) if is_key else m

    def bias(self, n, keys_major):
        shape = (2 * BLOCK, BLOCK) if keys_major else (BLOCK, 2 * BLOCK)
        kdim = 0 if keys_major else 1
        kidx = lax.broadcasted_iota(jnp.int32, shape, kdim)
        qidx = lax.broadcasted_iota(jnp.int32, shape, 1 - kdim)
        rel = self.index(qidx, False) - self.index(kidx, True)
        valid = (rel >= 0) & (rel <= WINDOW_KEYS) & ((kidx >= BLOCK) | (n > 0))
        return jnp.where(valid, 0.0, NEG)


def _head_masks():
    lane = lax.broadcasted_iota(jnp.int32, (BLOCK, LANES), 1)
    lo = lane < HEAD_DIM
    return lane, lo, jnp.where(lo, 1.0, 0.0).astype(BF16), jnp.where(lo, 0.0, 1.0).astype(BF16)


def _column(blk, lane, h):
    return jnp.sum(jnp.where(lane == h, blk, 0.0), axis=1, keepdims=True)


def attn_fwd(name, q, k, v, run):
    nat = name == "nat"
    seq = q.shape[0] if nat else q.shape[2] * PERM
    attn_w = q.shape[-1]
    mode = _Mode(name, seq)
    groups = _lane_groups(attn_w)
    first = run is None
    all_lanes = slice(0, LANES)

    def body(*refs):
        q_ref, kp_ref, kc_ref, vp_ref, vc_ref = refs[:5]
        if first:
            o_ref, l_ref = refs[5:]
        elif nat:
            oin_ref, lin_ref, ex_ref, o_ref, l_ref, ostage, lstage = refs[5:]
        else:
            oin_ref, lin_ref, ex_ref, o_ref, l_ref = refs[5:]
        n = pl.program_id(1)
        bias = mode.bias(n, True)
        bias2 = jnp.concatenate([bias, bias], axis=1)
        _, lo, m_lo, m_hi = _head_masks()
        head_row = lax.broadcasted_iota(jnp.int32, (BLOCK, LANES), 0)
        lrows = jnp.zeros((BLOCK, LANES), F32)
        for p, sl in enumerate(groups):
            q2 = mode.get(q_ref, sl)
            kcat = jnp.concatenate([mode.get(kp_ref, sl), mode.get(kc_ref, sl)], axis=0)
            vcat = jnp.concatenate([mode.get(vp_ref, sl), mode.get(vc_ref, sl)], axis=0)
            qq = jnp.concatenate([q2 * m_lo, q2 * m_hi], axis=0)
            s_t = _nt(kcat, qq) + bias2
            m = jnp.max(s_t, axis=0, keepdims=True)
            pe = jnp.exp(s_t - m)
            l = jnp.sum(pe, axis=0, keepdims=True)
            o_new = _tn((pe * (1.0 / l)).astype(BF16), vcat)
            mode.put(o_ref, sl, jnp.where(lo, o_new[:BLOCK], o_new[BLOCK:]))
            lse = m + jnp.log(l)
            lrows = jnp.where(head_row == 2 * p, lse[:, :BLOCK], lrows)
            lrows = jnp.where(head_row == 2 * p + 1, lse[:, BLOCK:], lrows)
        lblk = jnp.transpose(lrows)
        if first:
            mode.put(l_ref, all_lanes, lblk)
        else:
            if nat:
                for g, sl in enumerate(groups):
                    _from_perm(oin_ref, sl, ostage, g)
                _from_perm(lin_ref, all_lanes, lstage, 0)
                lin = lstage[0]
            else:
                lin = mode.get(lin_ref, all_lanes)
            mx = jnp.maximum(lin, lblk)
            new = mx + jnp.log(jnp.exp(lin - mx) + jnp.exp(lblk - mx))
            mode.put(l_ref, all_lanes, new)

            def expand(w):
                hi = w.astype(BF16)
                rest = (w - hi.astype(F32)).astype(BF16)
                ex = ex_ref[...]
                return jnp.dot(hi, ex, preferred_element_type=F32) + jnp.dot(rest, ex, preferred_element_type=F32)

            w_prev, w_cur = expand(jnp.exp(lin - new)), expand(jnp.exp(lblk - new))
            for p, sl in enumerate(groups):
                o_prev = ostage[p] if nat else mode.get(oin_ref, sl)
                mode.put(o_ref, sl, w_prev[:, sl] * o_prev + w_cur[:, sl] * mode.get(o_ref, sl))

    ins = [q, k, k, v, v]
    specs = [mode.spec(attn_w, "cur"), mode.spec(attn_w, "prev"), mode.spec(attn_w, "cur"),
             mode.spec(attn_w, "prev"), mode.spec(attn_w, "cur")]
    scratch = []
    if not first:
        ins += list(run)
        if nat:
            rows8 = BLOCK // PERM
            specs += [pl.BlockSpec((PJ, PJ, rows8, attn_w), lambda r, n: (0, 0, n, 0)),
                      pl.BlockSpec((PJ, PJ, rows8, LANES), lambda r, n: (0, 0, n, 0))]
            scratch = [pltpu.VMEM((len(groups), BLOCK, LANES), F32), pltpu.VMEM((1, BLOCK, LANES), F32)]
        else:
            specs += [mode.spec(attn_w, "cur"), mode.spec(LANES, "cur")]
        head_of_lane = jnp.arange(attn_w, dtype=jnp.int32) // HEAD_DIM
        ins.append((jnp.arange(LANES, dtype=jnp.int32)[:, None] == head_of_lane[None, :]).astype(BF16))
        specs.append(_const_spec((LANES, attn_w)))
    if nat:
        out_shape = (jax.ShapeDtypeStruct((seq, attn_w), F32), jax.ShapeDtypeStruct((seq, LANES), F32))
    else:
        out_shape = (jax.ShapeDtypeStruct(_perm_shape(seq, attn_w), F32), jax.ShapeDtypeStruct(_perm_shape(seq, LANES), F32))
    return pl.pallas_call(
        body, name=f"attn_fwd_{name}", grid=(mode.residues, mode.nb),
        out_shape=out_shape, in_specs=specs, out_specs=(mode.spec(attn_w, "cur"), mode.spec(LANES, "cur")),
        scratch_shapes=scratch,
        compiler_params=_params(("arbitrary", "arbitrary")),
    )(*ins)


def attn_bwd(name, q, k, v, d_o, lse, delta, run):
    nat = name == "nat"
    seq = q.shape[0] if nat else q.shape[2] * PERM
    attn_w = q.shape[-1]
    mode = _Mode(name, seq)
    nb = mode.nb
    groups = _lane_groups(attn_w)
    first = run is None
    all_lanes = slice(0, LANES)

    def body(*refs):
        q_ref, kp_ref, kc_ref, vp_ref, vc_ref, do_ref, lse_ref, dl_ref = refs[:8]
        if first:
            dq_ref, dk_ref, dv_ref, ck, cv = refs[8:]
        else:
            dqi_ref, dki_ref, dvi_ref, dq_ref, dk_ref, dv_ref, ck, cv = refs[8:]
        n = pl.program_id(1)

        @pl.when(n == 0)
        def _():
            ck[...] = jnp.zeros_like(ck)
            cv[...] = jnp.zeros_like(cv)

        @pl.when(n < nb)
        def _():
            bias = mode.bias(n, True)
            bias2 = jnp.concatenate([bias, bias], axis=1)
            _, lo, m_lo, m_hi = _head_masks()
            lse_t = jnp.transpose(mode.get(lse_ref, all_lanes))
            dl_t = jnp.transpose(mode.get(dl_ref, all_lanes))
            for p, sl in enumerate(groups):
                q2, do2 = mode.get(q_ref, sl), mode.get(do_ref, sl)
                kcat = jnp.concatenate([mode.get(kp_ref, sl), mode.get(kc_ref, sl)], axis=0)
                vcat = jnp.concatenate([mode.get(vp_ref, sl), mode.get(vc_ref, sl)], axis=0)
                qq = jnp.concatenate([q2 * m_lo, q2 * m_hi], axis=0)
                dd = jnp.concatenate([do2 * m_lo, do2 * m_hi], axis=0)
                h0 = 2 * p
                lse2 = jnp.concatenate([lse_t[h0:h0 + 1, :], lse_t[h0 + 1:h0 + 2, :]], axis=1)
                dl2 = jnp.concatenate([dl_t[h0:h0 + 1, :], dl_t[h0 + 1:h0 + 2, :]], axis=1)
                p_t = jnp.exp(_nt(kcat, qq) + (bias2 - lse2))
                ds_t = p_t * (_nt(vcat, dd) - dl2)
                dsb = ds_t.astype(BF16)
                dkc = jnp.dot(dsb, qq, preferred_element_type=F32)
                dvc = jnp.dot(p_t.astype(BF16), dd, preferred_element_type=F32)
                dqb = _tn(dsb, kcat)
                dq2 = jnp.where(lo, dqb[:BLOCK], dqb[BLOCK:]) * ATTN_SCALE
                dk2 = ck[:, sl] + dkc[:BLOCK]
                dv2 = cv[:, sl] + dvc[:BLOCK]
                if not first:
                    dq2 = dq2 + mode.get(dqi_ref, sl)
                    dk2 = dk2 + mode.get(dki_ref, sl)
                    dv2 = dv2 + mode.get(dvi_ref, sl)
                mode.put(dq_ref, sl, dq2)
                mode.put(dk_ref, sl, dk2)
                mode.put(dv_ref, sl, dv2)
                ck[:, sl] = dkc[BLOCK:]
                cv[:, sl] = dvc[BLOCK:]

        @pl.when(n == nb)
        def _():
            for sl in groups:
                if first:
                    mode.put(dk_ref, sl, ck[:, sl])
                    mode.put(dv_ref, sl, cv[:, sl])
                else:
                    mode.put(dk_ref, sl, ck[:, sl] + mode.get(dki_ref, sl))
                    mode.put(dv_ref, sl, cv[:, sl] + mode.get(dvi_ref, sl))

    last = nb - 1
    cur = lambda w: mode.spec(w, "cur", last)
    prev = lambda w: mode.spec(w, "prev")
    ins = [q, k, k, v, v, d_o, lse, delta]
    specs = [cur(attn_w), prev(attn_w), cur(attn_w), prev(attn_w), cur(attn_w), cur(attn_w), cur(LANES), cur(LANES)]
    if not first:
        ins += list(run)
        specs += [cur(attn_w), prev(attn_w), prev(attn_w)]
    shp = jax.ShapeDtypeStruct((seq, attn_w) if nat else _perm_shape(seq, attn_w), F32)
    return pl.pallas_call(
        body, name=f"attn_bwd_{name}", grid=(mode.residues, nb + 1),
        out_shape=(shp, shp, shp), in_specs=specs, out_specs=(cur(attn_w), prev(attn_w), prev(attn_w)),
        scratch_shapes=[pltpu.VMEM((BLOCK, attn_w), F32), pltpu.VMEM((BLOCK, attn_w), F32)],
        compiler_params=_params(("arbitrary", "arbitrary")),
    )(*ins)


def _shift_down(u, halo, k):
    rolled = pltpu.roll(u, k, 0)
    row = lax.broadcasted_iota(jnp.int32, halo.shape, 0)
    top = jnp.where(row < k, pltpu.roll(halo, k, 0), rolled[:SUBLANES])
    return jnp.concatenate([top, rolled[SUBLANES:]], axis=0)


def _shift_up(u, halo, k):
    rows = u.shape[0]
    rolled = pltpu.roll(u, rows - k, 0)
    row = lax.broadcasted_iota(jnp.int32, halo.shape, 0)
    bot = jnp.where(row >= SUBLANES - k, pltpu.roll(halo, SUBLANES - k, 0), rolled[rows - SUBLANES:])
    return jnp.concatenate([rolled[:rows - SUBLANES], bot], axis=0)


def tail(o, lse, ga, cz, x, tgt, w_out, g2, cw):
    seq, d_model = x.shape
    attn_w = o.shape[1]
    conv_w = cz.shape[1] // 4
    mix = attn_w + conv_w
    groups = _lane_groups(attn_w)
    tm = ROW_TILE
    nt = seq // tm
    hb = tm // SUBLANES

    def body(o_ref, l_ref, ga_ref, cz_ref, hz_ref, x_ref, t_ref, w_ref, g_ref, cw_ref,
             do_ref, dl_ref, dop_ref, dlp_ref, lp_ref, dga_ref, dcb_ref, dgc_ref, dcv_ref, e_ref,
             dw_ref, dg_ref, dcw_ref, loss_ref, stage):
        i = pl.program_id(0)

        @pl.when(i == 0)
        def _():
            dw_ref[...] = jnp.zeros_like(dw_ref)
            dg_ref[...] = jnp.zeros_like(dg_ref)
            dcw_ref[...] = jnp.zeros_like(dcw_ref)
            loss_ref[...] = jnp.zeros_like(loss_ref)

        ov, gav = o_ref[...], ga_ref[...]
        sig_a = _sigmoid(gav)
        silu_a = gav * sig_a
        attn_out = ov * silu_a
        ch, cb = cz_ref[:, 0:conv_w], cz_ref[:, conv_w:2 * conv_w]
        cc, gc = cz_ref[:, 2 * conv_w:3 * conv_w], cz_ref[:, 3 * conv_w:4 * conv_w]
        u = cc * ch
        uh = hz_ref[:, 2 * conv_w:3 * conv_w] * hz_ref[:, 0:conv_w]
        uh = jnp.where(i > 0, uh, 0.0)
        u1 = _shift_down(u, uh, 1)
        u2 = _shift_down(u, uh, 2)
        w0, w1, w2 = cw_ref[0:1, :], cw_ref[1:2, :], cw_ref[2:3, :]
        cvv = u2 * w0 + u1 * w1 + u * w2
        sig_c = _sigmoid(gc)
        silu_c = gc * sig_c
        bc = cb * cvv
        conv_out = bc * silu_c
        mixed = jnp.concatenate([attn_out, conv_out], axis=1).astype(BF16)

        yv = jnp.dot(mixed, w_ref[...], preferred_element_type=F32)
        r2 = lax.rsqrt(jnp.mean(yv * yv, axis=-1, keepdims=True) + NORM_EPS)
        yhat = yv * r2
        gv = g_ref[...]
        diff = (x_ref[...] + yhat * gv) - t_ref[...]
        loss_ref[...] += _rowgroup_sum(diff * diff)
        ev = diff * (1.0 / d_model)
        e_ref[...] = ev
        dg_ref[...] += _rowgroup_sum(ev * yhat)
        eg = ev * gv
        dy = (r2 * (eg - yhat * jnp.mean(eg * yhat, axis=-1, keepdims=True))).astype(BF16)
        dw_ref[...] += _tn(mixed, dy)
        dm = _nt(dy, w_ref[...])
        dma, dmc = dm[:, :attn_w], dm[:, attn_w:]

        dov = dma * silu_a
        do_ref[...] = dov.astype(BF16)
        dga_ref[...] = (dma * ov * (sig_a * (1.0 + gav * (1.0 - sig_a)))).astype(BF16)
        prod = dov * ov
        lane = lax.broadcasted_iota(jnp.int32, (tm, LANES), 1)
        lo = lane < HEAD_DIM
        dblk = jnp.zeros((tm, LANES), F32)
        for p, sl in enumerate(groups):
            pr = prod[:, sl]
            dblk = jnp.where(lane == 2 * p, jnp.sum(jnp.where(lo, pr, 0.0), axis=1, keepdims=True), dblk)
            dblk = jnp.where(lane == 2 * p + 1, jnp.sum(jnp.where(lo, 0.0, pr), axis=1, keepdims=True), dblk)
            stage[p] = dov[:, sl]
        dl_ref[...] = dblk
        for p, sl in enumerate(groups):
            _to_perm(stage, p, dop_ref, sl, BF16)
        all_lanes = slice(0, LANES)
        stage[0] = dblk
        stage[1] = l_ref[...]
        _to_perm(stage, 0, dlp_ref, all_lanes, F32)
        _to_perm(stage, 1, lp_ref, all_lanes, F32)

        dsc = dmc * silu_c
        dcb_ref[...] = (dsc * cvv).astype(BF16)
        dgc_ref[...] = (dmc * bc * (sig_c * (1.0 + gc * (1.0 - sig_c)))).astype(BF16)
        dcv = dsc * cb
        dcv_ref[...] = dcv
        dcw_ref[0:SUBLANES, :] += _rowgroup_sum(dcv * u2)
        dcw_ref[SUBLANES:2 * SUBLANES, :] += _rowgroup_sum(dcv * u1)
        dcw_ref[2 * SUBLANES:3 * SUBLANES, :] += _rowgroup_sum(dcv * u)

    row = lambda n: pl.BlockSpec((tm, n), lambda i: (i, 0))
    whole = lambda a, b: pl.BlockSpec((a, b), lambda i: (0, 0))
    return pl.pallas_call(
        body, name="tail", grid=(nt,),
        out_shape=(jax.ShapeDtypeStruct((seq, attn_w), BF16), jax.ShapeDtypeStruct((seq, LANES), F32),
                   jax.ShapeDtypeStruct(_perm_shape(seq, attn_w), BF16), jax.ShapeDtypeStruct(_perm_shape(seq, LANES), F32),
                   jax.ShapeDtypeStruct(_perm_shape(seq, LANES), F32),
                   jax.ShapeDtypeStruct((seq, attn_w), BF16), jax.ShapeDtypeStruct((seq, conv_w), BF16),
                   jax.ShapeDtypeStruct((seq, conv_w), BF16), jax.ShapeDtypeStruct((seq, conv_w), F32),
                   jax.ShapeDtypeStruct((seq, d_model), F32), jax.ShapeDtypeStruct((mix, d_model), F32),
                   jax.ShapeDtypeStruct((SUBLANES, d_model), F32), jax.ShapeDtypeStruct((CONV_K * SUBLANES, conv_w), F32),
                   jax.ShapeDtypeStruct((SUBLANES, d_model), F32)),
        in_specs=[row(attn_w), row(LANES), row(attn_w), row(4 * conv_w),
                  pl.BlockSpec((SUBLANES, 4 * conv_w), lambda i: (jnp.maximum(i * hb - 1, 0), 0)),
                  row(d_model), row(d_model), _const_spec((mix, d_model)), _const_spec((1, d_model)),
                  _const_spec((SUBLANES, conv_w))],
        out_specs=(row(attn_w), row(LANES), _perm_tile_spec(attn_w, tm), _perm_tile_spec(LANES, tm), _perm_tile_spec(LANES, tm),
                   row(attn_w), row(conv_w), row(conv_w), row(conv_w), row(d_model),
                   whole(mix, d_model), whole(SUBLANES, d_model), whole(CONV_K * SUBLANES, conv_w),
                   whole(SUBLANES, d_model)),
        scratch_shapes=[pltpu.VMEM((max(len(groups), 2), tm, LANES), F32)],
        compiler_params=_params(("arbitrary",)),
    )(o, lse, ga, cz, cz, x, tgt, w_out, g2, cw)


def dz_dx(nat_grads, perm_grads, dga, dcb, dgc, dcv, cz, tables, x, g1, e, w_full, cw):
    seq, d_model = x.shape
    attn_w = dga.shape[1]
    conv_w = dcv.shape[1]
    width = w_full.shape[2]
    in_w = 4 * attn_w + 4 * conv_w
    groups = _lane_groups(attn_w)
    tm = DZ_ROW_TILE
    nt = seq // tm
    hb = tm // SUBLANES

    def body(dq_ref, dk_ref, dv_ref, dqp_ref, dkp_ref, dvp_ref, dga_ref, dcb_ref, dgc_ref, dcv_ref, nh_ref, cz_ref,
             cos_ref, s1_ref, s2_ref, x_ref, g_ref, e_ref, w_ref, cw_ref, gx_ref, dz_ref, dg_ref, stage):
        i = pl.program_id(0)

        @pl.when(i == 0)
        def _():
            dg_ref[...] = jnp.zeros_like(dg_ref)

        cos, s1, s2 = cos_ref[...], s1_ref[...], s2_ref[...]
        for t, (nat_ref, perm_ref) in enumerate(((dq_ref, dqp_ref), (dk_ref, dkp_ref), (dv_ref, dvp_ref))):
            for g, sl in enumerate(groups):
                _from_perm(perm_ref, sl, stage, g)
            for g, sl in enumerate(groups):
                tot = nat_ref[:, sl] + stage[g]
                if t < 2:
                    tot = _rope_transposed(tot, cos, s1, s2)
                dz_ref[:, t * attn_w + g * LANES:t * attn_w + (g + 1) * LANES] = tot.astype(BF16)
        dz_ref[:, 3 * attn_w:4 * attn_w] = dga_ref[...]
        dcv = dcv_ref[...]
        nh = jnp.where(i < nt - 1, nh_ref[...], 0.0)
        w0, w1, w2 = cw_ref[0:1, :], cw_ref[1:2, :], cw_ref[2:3, :]
        du = dcv * w2 + _shift_up(dcv, nh, 1) * w1 + _shift_up(dcv, nh, 2) * w0
        base = 4 * attn_w
        dz_ref[:, base:base + conv_w] = (du * cz_ref[:, 2 * conv_w:3 * conv_w]).astype(BF16)
        dz_ref[:, base + conv_w:base + 2 * conv_w] = dcb_ref[...]
        dz_ref[:, base + 2 * conv_w:base + 3 * conv_w] = (du * cz_ref[:, 0:conv_w]).astype(BF16)
        dz_ref[:, base + 3 * conv_w:base + 4 * conv_w] = dgc_ref[...]

        dh = _nt(dz_ref[:, 0:width], w_ref[0])
        for j in range(1, N_CHIPS):
            dh = dh + _nt(dz_ref[:, j * width:(j + 1) * width], w_ref[j])
        xv = x_ref[...]
        r1 = lax.rsqrt(jnp.mean(xv * xv, axis=-1, keepdims=True) + NORM_EPS)
        xhat = xv * r1
        dg_ref[...] += _rowgroup_sum(dh * xhat)
        dhg = dh * g_ref[...]
        gx_ref[...] = r1 * (dhg - xhat * jnp.mean(dhg * xhat, axis=-1, keepdims=True)) + e_ref[...]

    row = lambda n: pl.BlockSpec((tm, n), lambda i: (i, 0))
    whole = lambda a, b: pl.BlockSpec((a, b), lambda i: (0, 0))
    pt = _perm_tile_spec(attn_w, tm)
    return pl.pallas_call(
        body, name="dz_dx", grid=(nt,),
        out_shape=(jax.ShapeDtypeStruct((seq, d_model), F32), jax.ShapeDtypeStruct((seq, in_w), BF16),
                   jax.ShapeDtypeStruct((SUBLANES, d_model), F32)),
        in_specs=[row(attn_w), row(attn_w), row(attn_w), pt, pt, pt, row(attn_w), row(conv_w), row(conv_w), row(conv_w),
                  pl.BlockSpec((SUBLANES, conv_w), lambda i: (jnp.minimum((i + 1) * hb, seq // SUBLANES - 1), 0)),
                  row(4 * conv_w), row(LANES), row(LANES), row(LANES), row(d_model), _const_spec((1, d_model)), row(d_model),
                  _const_spec(w_full.shape), _const_spec((SUBLANES, conv_w))],
        out_specs=(row(d_model), row(in_w), whole(SUBLANES, d_model)),
        scratch_shapes=[pltpu.VMEM((len(groups), tm, LANES), F32)],
        compiler_params=_params(("arbitrary",)),
    )(*nat_grads, *perm_grads, dga, dcb, dgc, dcv, dcv, cz, *tables, x, g1, e, w_full, cw)


def dw_in(h, dz):
    seq, d_model = h.shape
    half = dz.shape[1] // N_DEV
    ts = 512
    steps = seq // ts

    def body(h_ref, dz_ref, o_ref):
        @pl.when(pl.program_id(1) == 0)
        def _():
            o_ref[...] = jnp.zeros_like(o_ref)

        o_ref[0] += _tn(h_ref[...], dz_ref[...])

    return pl.pallas_call(
        body, name="dw_in", grid=(N_DEV, steps),
        out_shape=jax.ShapeDtypeStruct((N_DEV, d_model, half), F32),
        in_specs=[pl.BlockSpec((ts, d_model), lambda j, s: (s, 0)), pl.BlockSpec((ts, half), lambda j, s: (s, j))],
        out_specs=pl.BlockSpec((1, d_model, half), lambda j, s: (j, 0, 0)),
        compiler_params=_params(("arbitrary", "arbitrary")),
    )(h, dz)


def grad_reduce(g_in, g_out, small):
    gi = g_in.reshape(N_CHIPS, 2, *g_in.shape[1:])
    go = g_out.reshape(N_CHIPS, 2, *g_out.shape[1:])
    shp_i, shp_o = gi.shape[2:], go.shape[2:]

    def body(gi_ref, go_ref, sm_ref, ri_ref, ro_ref, rs_ref, a_i, b_i, c_i, a_o, b_o, c_o, sbuf,
             loc_sems, sa, ra, sb, rb, sc, rc, ss, rs):
        x, y, c = lax.axis_index("x"), lax.axis_index("y"), lax.axis_index("c")
        me = 2 * x + y
        sib = (x, y, 1 - c)
        srcs, mine, theirs, contrib, res = (gi_ref, go_ref), (a_i, a_o), (b_i, b_o), (c_i, c_o), (ri_ref, ro_ref)

        flips = [(fx, fy, fc) for fx in (0, 1) for fy in (0, 1) for fc in (0, 1)][1:]
        my8 = 4 * x + 2 * y + c
        sbuf[my8] = sm_ref[...]

        def small_copy(k, slot, to):
            return pltpu.make_async_remote_copy(src_ref=sm_ref, dst_ref=sbuf.at[slot], send_sem=ss.at[k], recv_sem=rs.at[k],
                                                device_id=to, device_id_type=MESH)

        small_sends = []
        for k, (fx, fy, fc) in enumerate(flips):
            px, py, pc = _flip(x, fx), _flip(y, fy), _flip(c, fc)
            small_sends.append(small_copy(k, my8, (px, py, pc)))
            small_sends[-1].start()

        def a_copy(t):
            return pltpu.make_async_remote_copy(src_ref=srcs[t].at[:, 1 - c], dst_ref=theirs[t], send_sem=sa.at[t],
                                                recv_sem=ra.at[t], device_id=sib, device_id_type=MESH)

        loads = [pltpu.make_async_copy(srcs[t].at[:, c], mine[t], loc_sems.at[t]) for t in range(2)]
        a_sends = [a_copy(t) for t in range(2)]
        for cp in loads + a_sends:
            cp.start()
        for t in range(2):
            loads[t].wait()
            a_copy(t).wait_recv()
            mine[t][...] = mine[t][...] + theirs[t][...]

        def b_copy(k, t, piece, slot, to):
            return pltpu.make_async_remote_copy(src_ref=mine[t].at[piece], dst_ref=contrib[t].at[slot], send_sem=sb.at[k, t],
                                                recv_sem=rb.at[k, t], device_id=to, device_id_type=MESH)

        peers = _chip_peers(x, y)
        b_sends = [b_copy(k, t, 2 * px + py, me, (px, py, c)) for k, (px, py) in enumerate(peers) for t in range(2)]
        for cp in b_sends:
            cp.start()
        for t in range(2):
            contrib[t][me] = mine[t][me]
        for k, (px, py) in enumerate(peers):
            for t in range(2):
                b_copy(k, t, me, 2 * px + py, (px, py, c)).wait_recv()

        def c_copy(t, half):
            return pltpu.make_async_remote_copy(src_ref=res[t].at[half], dst_ref=res[t].at[half], send_sem=sc.at[t],
                                                recv_sem=rc.at[t], device_id=sib, device_id_type=MESH)

        c_sends = []
        for t in range(2):
            res[t][c] = ((contrib[t][0] + contrib[t][1]) + contrib[t][2]) + contrib[t][3]
            c_sends.append(c_copy(t, c))
            c_sends[-1].start()
        for t in range(2):
            c_copy(t, 1 - c).wait_recv()

        for k, (fx, fy, fc) in enumerate(flips):
            px, py, pc = _flip(x, fx), _flip(y, fy), _flip(c, fc)
            small_copy(k, 4 * px + 2 * py + pc, (px, py, pc)).wait_recv()
        tot = sbuf[0]
        for d in range(1, N_DEV):
            tot = tot + sbuf[d]
        rs_ref[...] = tot
        for cp in small_sends + a_sends + b_sends + c_sends:
            cp.wait_send()

    vm = pl.BlockSpec(memory_space=pltpu.VMEM)
    anyspace = pl.BlockSpec(memory_space=pl.ANY)
    dma = pltpu.SemaphoreType.DMA
    return pl.pallas_call(
        body, name="grad_reduce",
        out_shape=(jax.ShapeDtypeStruct((2, *shp_i), F32), jax.ShapeDtypeStruct((2, *shp_o), F32),
                   jax.ShapeDtypeStruct(small.shape, F32)),
        in_specs=[anyspace, anyspace, vm], out_specs=(vm, vm, vm),
        scratch_shapes=[pltpu.VMEM((N_CHIPS, *shp_i), F32), pltpu.VMEM((N_CHIPS, *shp_i), F32), pltpu.VMEM((N_CHIPS, *shp_i), F32),
                        pltpu.VMEM((N_CHIPS, *shp_o), F32), pltpu.VMEM((N_CHIPS, *shp_o), F32), pltpu.VMEM((N_CHIPS, *shp_o), F32),
                        pltpu.VMEM((N_DEV, *small.shape), F32),
                        dma((2,)), dma((2,)), dma((2,)), dma((3, 2)), dma((3, 2)), dma((2,)), dma((2,)),
                        dma((N_DEV - 1,)), dma((N_DEV - 1,))],
        compiler_params=_params(),
    )(gi, go, small)


def _adam_math(w, g, m, v):
    m = ADAM_B1 * m + (1.0 - ADAM_B1) * g
    v = ADAM_B2 * v + (1.0 - ADAM_B2) * (g * g)
    m_hat = m / (1.0 - ADAM_B1 ** ADAM_STEP)
    v_hat = v / (1.0 - ADAM_B2 ** ADAM_STEP)
    delta = -ADAM_LR * (m_hat / (jnp.sqrt(v_hat) + ADAM_EPS) + ADAM_WD * w)
    return delta, m, v


def adam_shard(name, w, g2, m, v, block, grid, w_map, g_map):
    def body(w_ref, g_ref, m_ref, v_ref, go_ref, d_ref, mo_ref, vo_ref):
        g = g_ref[0]
        delta, mn, vn = _adam_math(w_ref[...], g, m_ref[...], v_ref[...])
        go_ref[...] = g
        d_ref[...] = delta
        mo_ref[...] = mn
        vo_ref[...] = vn

    ws = pl.BlockSpec(block, w_map)
    shp = jax.ShapeDtypeStruct(w.shape, F32)
    return pl.pallas_call(
        body, name=name, grid=grid, out_shape=(shp, shp, shp, shp),
        in_specs=[ws, pl.BlockSpec((1, *block), g_map), ws, ws], out_specs=(ws, ws, ws, ws),
        compiler_params=_params(("arbitrary",) * len(grid)),
    )(w, g2, m, v)


def adam_small(ws, gs, ms, vs):
    n = len(ws)

    def body(*refs):
        ins, outs = refs[:4 * n], refs[4 * n:]
        for t in range(n):
            delta, mn, vn = _adam_math(ins[t][...], ins[n + t][...], ins[2 * n + t][...], ins[3 * n + t][...])
            outs[3 * t][...] = delta
            outs[3 * t + 1][...] = mn
            outs[3 * t + 2][...] = vn

    vm = pl.BlockSpec(memory_space=pltpu.VMEM)
    outs = pl.pallas_call(
        body, name="adam_small",
        out_shape=tuple(jax.ShapeDtypeStruct(w.shape, F32) for w in ws for _ in range(3)),
        in_specs=[vm] * (4 * n), out_specs=tuple([vm] * (3 * n)),
        compiler_params=_params(),
    )(*ws, *gs, *ms, *vs)
    return [outs[3 * t:3 * t + 3] for t in range(n)]


def kernel(x, norm_pre_g, w_in, conv_w, w_out, norm_post_g, loss_target, m_norm_pre_g, m_w_in, m_conv_w, m_w_out, m_norm_post_g, v_norm_pre_g, v_w_in, v_conv_w, v_w_out, v_norm_post_g):
    _, seq, d_model = x.shape
    width = w_in.shape[1]
    conv_q = conv_w.shape[1]
    conv_width = N_CHIPS * conv_q
    attn_width = d_model - conv_width
    xs, tg = x[0], loss_target[0]
    g1, g2 = norm_pre_g.reshape(1, d_model), norm_post_g.reshape(1, d_model)

    w_full, wout_full, cw_full = gather_weights(w_in, w_out, conv_w)
    wout2 = wout_full.reshape(attn_width + conv_width, d_model)
    cw = jnp.zeros((SUBLANES, conv_width), F32).at[:CONV_K].set(
        cw_full[:, :CONV_K, :conv_q].transpose(1, 0, 2).reshape(CONV_K, conv_width))
    tables = _rope_tables(seq)

    h, q, k, v, qp, kp, vp, ga, cz = inproj(xs, g1, w_full, tables, attn_width, conv_width)
    run = attn_fwd("p4", qp, kp, vp, None)
    run = attn_fwd("p16", qp, kp, vp, run)
    o, lse = attn_fwd("nat", q, k, v, run)
    (d_o, delta, d_op, delta_p, lse_p, dga, dcb, dgc, dcv, e, dwout, dg2, dcw, loss_acc) = tail(
        o, lse, ga, cz, xs, tg, wout2, g2, cw)
    nat_grads = attn_bwd("nat", q, k, v, d_o, lse, delta, None)
    perm_grads = attn_bwd("p4", qp, kp, vp, d_op, lse_p, delta_p, None)
    perm_grads = attn_bwd("p16", qp, kp, vp, d_op, lse_p, delta_p, perm_grads)
    grad_x, dz, dg1 = dz_dx(nat_grads, perm_grads, dga, dcb, dgc, dcv, cz, tables, xs, g1, e, w_full, cw)
    dwin = dw_in(h, dz)

    small = jnp.zeros((SUBLANES, d_model), F32)
    small = small.at[0].set(dg1.sum(axis=0)).at[1].set(dg2.sum(axis=0))
    small = small.at[2:2 + CONV_K, :conv_width].set(dcw.reshape(CONV_K, SUBLANES, conv_width).sum(axis=1))
    rin, rout, rsmall = grad_reduce(dwin, dwout.reshape(N_DEV, -1, d_model), small)

    half = width // 2
    tr = 256
    gw_in, d_in, m_in, v_in = adam_shard(
        "adam_w_in", w_in, rin, m_w_in, v_w_in, (tr, half), (2, d_model // tr),
        lambda hf, i: (i, hf), lambda hf, i: (hf, i, 0))
    rq = w_out.shape[0] // 2
    gw_out, d_out, m_out, v_out = adam_shard(
        "adam_w_out", w_out, rout, m_w_out, v_w_out, (rq, d_model), (2,),
        lambda hf: (hf, 0), lambda hf: (hf, 0, 0))

    chip = 2 * lax.axis_index("x") + lax.axis_index("y")
    g_pre, g_post = rsmall[0:1], rsmall[1:2]
    g_conv = lax.dynamic_slice(rsmall[2:2 + CONV_K, :conv_width], (0, chip * conv_q), (CONV_K, conv_q))
    (d_pre, m_pre, v_pre), (d_post, m_post, v_post), (d_cv, m_cv, v_cv) = adam_small(
        [g1, g2, conv_w], [g_pre, g_post, g_conv],
        [m_norm_pre_g.reshape(1, d_model), m_norm_post_g.reshape(1, d_model), m_conv_w],
        [v_norm_pre_g.reshape(1, d_model), v_norm_post_g.reshape(1, d_model), v_conv_w])

    loss = lax.psum(0.5 * jnp.sum(loss_acc) / d_model, ("x", "y", "c"))
    vec = lambda a: a.reshape(d_model)
    return (loss, grad_x.reshape(1, seq, d_model),
            vec(g_pre), gw_in, g_conv, gw_out, vec(g_post),
            vec(d_pre), d_in, d_cv, d_out, vec(d_post),
            vec(m_pre), m_in, m_cv, m_out, vec(m_post),
            vec(v_pre), v_in, v_cv, v_out, vec(v_post))
```

```python
import jax
import jax.numpy as jnp
from jax import lax
from jax.experimental import pallas as pl
from jax.experimental.pallas import tpu as pltpu

HEAD_DIM = 64
LANES = 128
SUBLANES = 8
BLOCK = 128
WINDOW_KEYS = 128
PERM = 16
PJ = 4
P4_ROWS = BLOCK // PJ
ROW_TILE = 512
DZ_ROW_TILE = 256
CONV_K = 3
ROPE_THETA = 10000.0
NORM_EPS = 1e-6
ATTN_SCALE = HEAD_DIM ** -0.5
NEG = -1e30
N_CHIPS = 4
N_DEV = 8
MESH = pl.DeviceIdType.MESH
ADAM_LR = 0.001
ADAM_B1 = 0.9
ADAM_B2 = 0.999
ADAM_EPS = 1e-08
ADAM_WD = 0.01
ADAM_STEP = 10
VMEM_LIMIT = 52 * 1024 * 1024

F32 = jnp.float32
BF16 = jnp.bfloat16


def _params(sem=None, **kw):
    return pltpu.CompilerParams(dimension_semantics=sem, vmem_limit_bytes=VMEM_LIMIT, **kw)


def _const_spec(shape):
    return pl.BlockSpec(shape, lambda *_: (0,) * len(shape), pipeline_mode=pl.Buffered(1))


def _sigmoid(z):
    return 1.0 / (1.0 + jnp.exp(-z))


def _rowgroup_sum(a):
    rows, n = a.shape
    return a.reshape(rows // SUBLANES, SUBLANES, n).sum(axis=0)


def _nt(a, b):
    return lax.dot_general(a, b, (((1,), (1,)), ((), ())), preferred_element_type=F32)


def _tn(a, b):
    return lax.dot_general(a, b, (((0,), (0,)), ((), ())), preferred_element_type=F32)


def _col_pieces(a, b, width):
    out = []
    while a < b:
        j = a // width
        e = min(b, (j + 1) * width)
        out.append((j, a - j * width, e - j * width))
        a = e
    return out


def _lane_groups(width):
    return [slice(g * LANES, (g + 1) * LANES) for g in range(width // LANES)]


def _perm_shape(seq, width):
    return (PJ, PJ, seq // PERM, width)


def _perm_tile_spec(width, tm):
    return pl.BlockSpec((PJ, PJ, tm // PERM, width), lambda i: (0, 0, i, 0))


def _to_perm(stage, g, dst_ref, sl, dtype):
    rows = stage.shape[1] // PERM
    for b in range(PERM):
        dst_ref[b // PJ, b % PJ, :, sl] = stage[g, pl.ds(b, rows, stride=PERM), :].astype(dtype)


def _from_perm(src_ref, sl, stage, g):
    rows = stage.shape[1] // PERM
    for b in range(PERM):
        stage[g, pl.ds(b, rows, stride=PERM), :] = src_ref[b // PJ, b % PJ, :, sl]


def _flip(a, f):
    return 1 - a if f else a


def _chip_peers(x, y):
    return [(1 - x, y), (x, 1 - y), (1 - x, 1 - y)]


def gather_weights(w_in, w_out, conv_w):
    d_model, width = w_in.shape
    rows = w_out.shape[0]
    cw = jnp.zeros((SUBLANES, LANES), F32).at[:CONV_K, :conv_w.shape[1]].set(conv_w)

    def body(win_ref, wout_ref, cw_ref, winf_ref, woutf_ref, cwf_ref, st_in, st_out, ici_send, ici_recv, d2d_send, d2d_recv):
        x, y, c = lax.axis_index("x"), lax.axis_index("y"), lax.axis_index("c")
        me = 2 * x + y
        sib = (x, y, 1 - c)
        st_in[...] = win_ref[...].astype(BF16)
        st_out[...] = wout_ref[...].astype(BF16)
        winf_ref[me] = st_in[...]
        woutf_ref[me] = st_out[...]
        cwf_ref[me] = cw_ref[...]
        stages = (st_in, st_out)
        fulls = (winf_ref, woutf_ref)
        halves = (d_model // 2, rows // 2)

        def half(t, core):
            return pl.ds(pl.multiple_of(core * halves[t], halves[t]), halves[t])

        def ici(k, t, slot, to, core):
            src = stages[t].at[half(t, core)] if t < 2 else cw_ref
            dst = fulls[t].at[slot, half(t, core)] if t < 2 else cwf_ref.at[slot]
            return pltpu.make_async_remote_copy(src_ref=src, dst_ref=dst, send_sem=ici_send.at[k, t], recv_sem=ici_recv.at[k, t],
                                                device_id=to, device_id_type=MESH)

        def d2d(k, t, slot, core):
            ref = fulls[t].at[slot, half(t, core)]
            return pltpu.make_async_remote_copy(src_ref=ref, dst_ref=ref, send_sem=d2d_send.at[k, t], recv_sem=d2d_recv.at[k, t],
                                                device_id=sib, device_id_type=MESH)

        peers = _chip_peers(x, y)
        sends = [ici(k, t, me, (px, py, c), c) for k, (px, py) in enumerate(peers) for t in range(3)]
        for cp in sends:
            cp.start()
        for k, (px, py) in enumerate(peers):
            for t in range(2):
                ici(k, t, 2 * px + py, (px, py, c), c).wait_recv()
                fwd = d2d(k, t, 2 * px + py, c)
                fwd.start()
                sends.append(fwd)
            ici(k, 2, 2 * px + py, (px, py, c), c).wait_recv()
        for k, (px, py) in enumerate(peers):
            for t in range(2):
                d2d(k, t, 2 * px + py, 1 - c).wait_recv()
        for cp in sends:
            cp.wait_send()

    vm = pl.BlockSpec(memory_space=pltpu.VMEM)
    dma = pltpu.SemaphoreType.DMA
    return pl.pallas_call(
        body, name="gather_weights",
        out_shape=(jax.ShapeDtypeStruct((N_CHIPS, d_model, width), BF16),
                   jax.ShapeDtypeStruct((N_CHIPS, rows, d_model), BF16),
                   jax.ShapeDtypeStruct((N_CHIPS, SUBLANES, LANES), F32)),
        in_specs=[vm, vm, vm], out_specs=(vm, vm, vm),
        scratch_shapes=[pltpu.VMEM((d_model, width), BF16), pltpu.VMEM((rows, d_model), BF16),
                        dma((3, 3)), dma((3, 3)), dma((3, 2)), dma((3, 2))],
        compiler_params=_params(),
    )(w_in, w_out, cw)


def _rope_tables(seq):
    half = HEAD_DIM // 2
    inv_freq = ROPE_THETA ** (-jnp.arange(half, dtype=F32) * 2.0 / HEAD_DIM)
    ang = jnp.arange(seq).astype(F32)[:, None] * inv_freq[None, :]
    cos, sin = jnp.cos(ang), jnp.sin(ang)
    zero = jnp.zeros_like(sin)
    return (jnp.concatenate([cos, cos, cos, cos], axis=1),
            jnp.concatenate([-sin, zero, -sin, zero], axis=1),
            jnp.concatenate([zero, sin, zero, sin], axis=1))


def _rope(t, cos, s1, s2):
    return t * cos + pltpu.roll(t, LANES - HEAD_DIM // 2, 1) * s1 + pltpu.roll(t, HEAD_DIM // 2, 1) * s2


def _rope_transposed(g, cos, s1, s2):
    return g * cos + pltpu.roll(g * s1, HEAD_DIM // 2, 1) + pltpu.roll(g * s2, LANES - HEAD_DIM // 2, 1)


def inproj(x, g1, w_full, tables, attn_w, conv_w):
    seq, d_model = x.shape
    width = w_full.shape[2]
    tm = ROW_TILE
    groups = _lane_groups(attn_w)

    def body(x_ref, g_ref, w_ref, cos_ref, s1_ref, s2_ref,
             h_ref, q_ref, k_ref, v_ref, qp_ref, kp_ref, vp_ref, ga_ref, cz_ref, stage):
        xv = x_ref[...]
        hb = ((xv * lax.rsqrt(jnp.mean(xv * xv, axis=-1, keepdims=True) + NORM_EPS)) * g_ref[...]).astype(BF16)
        h_ref[...] = hb
        cos, s1, s2 = cos_ref[...], s1_ref[...], s2_ref[...]

        def proj(a, b):
            parts = [jnp.dot(hb, w_ref[j, :, lo:hi], preferred_element_type=F32) for j, lo, hi in _col_pieces(a, b, width)]
            return parts[0] if len(parts) == 1 else jnp.concatenate(parts, axis=1)

        def emit(z, nat_ref, perm_ref, fn):
            for g, sl in enumerate(groups):
                val = fn(z[:, sl])
                nat_ref[:, sl] = val.astype(BF16)
                stage[g] = val
            for g, sl in enumerate(groups):
                _to_perm(stage, g, perm_ref, sl, BF16)

        emit(proj(0, attn_w), q_ref, qp_ref, lambda t: _rope(t, cos, s1, s2) * ATTN_SCALE)
        emit(proj(attn_w, 2 * attn_w), k_ref, kp_ref, lambda t: _rope(t, cos, s1, s2))
        emit(proj(2 * attn_w, 3 * attn_w), v_ref, vp_ref, lambda t: t)
        ga_ref[...] = proj(3 * attn_w, 4 * attn_w)
        cz_ref[...] = proj(4 * attn_w, 4 * attn_w + 4 * conv_w)

    row = lambda n: pl.BlockSpec((tm, n), lambda i: (i, 0))
    nat = jax.ShapeDtypeStruct((seq, attn_w), BF16)
    perm = jax.ShapeDtypeStruct(_perm_shape(seq, attn_w), BF16)
    return pl.pallas_call(
        body, name="inproj", grid=(seq // tm,),
        out_shape=(jax.ShapeDtypeStruct((seq, d_model), BF16), nat, nat, nat, perm, perm, perm,
                   jax.ShapeDtypeStruct((seq, attn_w), F32), jax.ShapeDtypeStruct((seq, 4 * conv_w), F32)),
        in_specs=[row(d_model), _const_spec((1, d_model)), _const_spec(w_full.shape), row(LANES), row(LANES), row(LANES)],
        out_specs=(row(d_model), row(attn_w), row(attn_w), row(attn_w),
                   _perm_tile_spec(attn_w, tm), _perm_tile_spec(attn_w, tm), _perm_tile_spec(attn_w, tm),
                   row(attn_w), row(4 * conv_w)),
        scratch_shapes=[pltpu.VMEM((len(groups), tm, LANES), F32)],
        compiler_params=_params(("arbitrary",)),
    )(x, g1, w_full, *tables)


class _Mode:
    def __init__(self, name, seq):
        self.name = name
        if name == "nat":
            self.residues, self.nb = 1, seq // BLOCK
        elif name == "p16":
            self.residues, self.nb = PERM, seq // PERM // BLOCK
        else:
            self.residues, self.nb = PJ, seq // PERM // P4_ROWS

    def spec(self, width, which, last=None):
        if which == "prev":
            blk = lambda n: jnp.maximum(n - 1, 0)
        elif last is None:
            blk = lambda n: n
        else:
            blk = lambda n: jnp.minimum(n, last)
        if self.name == "nat":
            return pl.BlockSpec((BLOCK, width), lambda r, n: (blk(n), 0))
        if self.name == "p16":
            return pl.BlockSpec((1, 1, BLOCK, width), lambda r, n: (r // PJ, r % PJ, blk(n), 0))
        return pl.BlockSpec((PJ, 1, P4_ROWS, width), lambda r, n: (0, r, blk(n), 0))

    def get(self, ref, sl):
        if self.name == "nat":
            return ref[:, sl]
        if self.name == "p16":
            return ref[0, 0, :, sl]
        return jnp.concatenate([ref[j, 0, :, sl] for j in range(PJ)], axis=0)

    def put(self, ref, sl, val):
        if self.name == "nat":
            ref[:, sl] = val
        elif self.name == "p16":
            ref[0, 0, :, sl] = val
        else:
            for j in range(PJ):
                ref[j, 0, :, sl] = val[j * P4_ROWS:(j + 1) * P4_ROWS]

    def index(self, idx, is_key):
        if self.name != "p4":
            return idx - BLOCK if is_key else idx
        within = jnp.bitwise_and(idx, BLOCK - 1)
        m = PJ * jnp.bitwise_and(within, P4_ROWS - 1) + jnp.right_shift(within, P4_ROWS.bit_length() - 1)
        return m + BLOCK * (jnp.right_shift(idx, BLOCK.bit_length() - 1) - 1) if is_key else m

    def bias(self, n, keys_major):
        shape = (2 * BLOCK, BLOCK) if keys_major else (BLOCK, 2 * BLOCK)
        kdim = 0 if keys_major else 1
        kidx = lax.broadcasted_iota(jnp.int32, shape, kdim)
        qidx = lax.broadcasted_iota(jnp.int32, shape, 1 - kdim)
        rel = self.index(qidx, False) - self.index(kidx, True)
        valid = (rel >= 0) & (rel <= WINDOW_KEYS) & ((kidx >= BLOCK) | (n > 0))
        return jnp.where(valid, 0.0, NEG)


def _head_masks():
    lane = lax.broadcasted_iota(jnp.int32, (BLOCK, LANES), 1)
    lo = lane < HEAD_DIM
    return lane, lo, jnp.where(lo, 1.0, 0.0).astype(BF16), jnp.where(lo, 0.0, 1.0).astype(BF16)


def _column(blk, lane, h):
    return jnp.sum(jnp.where(lane == h, blk, 0.0), axis=1, keepdims=True)


def attn_fwd(name, q, k, v, run):
    nat = name == "nat"
    seq = q.shape[0] if nat else q.shape[2] * PERM
    attn_w = q.shape[-1]
    mode = _Mode(name, seq)
    groups = _lane_groups(attn_w)
    first = run is None
    all_lanes = slice(0, LANES)

    def body(*refs):
        q_ref, kp_ref, kc_ref, vp_ref, vc_ref = refs[:5]
        if first:
            o_ref, l_ref = refs[5:]
        elif nat:
            oin_ref, lin_ref, ex_ref, o_ref, l_ref, ostage, lstage = refs[5:]
        else:
            oin_ref, lin_ref, ex_ref, o_ref, l_ref = refs[5:]
        n = pl.program_id(1)
        bias = mode.bias(n, True)
        bias2 = jnp.concatenate([bias, bias], axis=1)
        _, lo, m_lo, m_hi = _head_masks()
        head_row = lax.broadcasted_iota(jnp.int32, (BLOCK, LANES), 0)
        lrows = jnp.zeros((BLOCK, LANES), F32)
        for p, sl in enumerate(groups):
            q2 = mode.get(q_ref, sl)
            kcat = jnp.concatenate([mode.get(kp_ref, sl), mode.get(kc_ref, sl)], axis=0)
            vcat = jnp.concatenate([mode.get(vp_ref, sl), mode.get(vc_ref, sl)], axis=0)
            qq = jnp.concatenate([q2 * m_lo, q2 * m_hi], axis=0)
            s_t = _nt(kcat, qq) + bias2
            m = jnp.max(s_t, axis=0, keepdims=True)
            pe = jnp.exp(s_t - m)
            l = jnp.sum(pe, axis=0, keepdims=True)
            o_new = _tn((pe * (1.0 / l)).astype(BF16), vcat)
            mode.put(o_ref, sl, jnp.where(lo, o_new[:BLOCK], o_new[BLOCK:]))
            lse = m + jnp.log(l)
            lrows = jnp.where(head_row == 2 * p, lse[:, :BLOCK], lrows)
            lrows = jnp.where(head_row == 2 * p + 1, lse[:, BLOCK:], lrows)
        lblk = jnp.transpose(lrows)
        if first:
            mode.put(l_ref, all_lanes, lblk)
        else:
            if nat:
                for g, sl in enumerate(groups):
                    _from_perm(oin_ref, sl, ostage, g)
                _from_perm(lin_ref, all_lanes, lstage, 0)
                lin = lstage[0]
            else:
                lin = mode.get(lin_ref, all_lanes)
            mx = jnp.maximum(lin, lblk)
            new = mx + jnp.log(jnp.exp(lin - mx) + jnp.exp(lblk - mx))
            mode.put(l_ref, all_lanes, new)

            def expand(w):
                hi = w.astype(BF16)
                rest = (w - hi.astype(F32)).astype(BF16)
                ex = ex_ref[...]
                return jnp.dot(hi, ex, preferred_element_type=F32) + jnp.dot(rest, ex, preferred_element_type=F32)

            w_prev, w_cur = expand(jnp.exp(lin - new)), expand(jnp.exp(lblk - new))
            for p, sl in enumerate(groups):
                o_prev = ostage[p] if nat else mode.get(oin_ref, sl)
                mode.put(o_ref, sl, w_prev[:, sl] * o_prev + w_cur[:, sl] * mode.get(o_ref, sl))

    ins = [q, k, k, v, v]
    specs = [mode.spec(attn_w, "cur"), mode.spec(attn_w, "prev"), mode.spec(attn_w, "cur"),
             mode.spec(attn_w, "prev"), mode.spec(attn_w, "cur")]
    scratch = []
    if not first:
        ins += list(run)
        if nat:
            rows8 = BLOCK // PERM
            specs += [pl.BlockSpec((PJ, PJ, rows8, attn_w), lambda r, n: (0, 0, n, 0)),
                      pl.BlockSpec((PJ, PJ, rows8, LANES), lambda r, n: (0, 0, n, 0))]
            scratch = [pltpu.VMEM((len(groups), BLOCK, LANES), F32), pltpu.VMEM((1, BLOCK, LANES), F32)]
        else:
            specs += [mode.spec(attn_w, "cur"), mode.spec(LANES, "cur")]
        head_of_lane = jnp.arange(attn_w, dtype=jnp.int32) // HEAD_DIM
        ins.append((jnp.arange(LANES, dtype=jnp.int32)[:, None] == head_of_lane[None, :]).astype(BF16))
        specs.append(_const_spec((LANES, attn_w)))
    if nat:
        out_shape = (jax.ShapeDtypeStruct((seq, attn_w), F32), jax.ShapeDtypeStruct((seq, LANES), F32))
    else:
        out_shape = (jax.ShapeDtypeStruct(_perm_shape(seq, attn_w), F32), jax.ShapeDtypeStruct(_perm_shape(seq, LANES), F32))
    return pl.pallas_call(
        body, name=f"attn_fwd_{name}", grid=(mode.residues, mode.nb),
        out_shape=out_shape, in_specs=specs, out_specs=(mode.spec(attn_w, "cur"), mode.spec(LANES, "cur")),
        scratch_shapes=scratch,
        compiler_params=_params(("arbitrary", "arbitrary")),
    )(*ins)


def attn_bwd(name, q, k, v, d_o, lse, delta, run):
    nat = name == "nat"
    seq = q.shape[0] if nat else q.shape[2] * PERM
    attn_w = q.shape[-1]
    mode = _Mode(name, seq)
    nb = mode.nb
    groups = _lane_groups(attn_w)
    first = run is None
    all_lanes = slice(0, LANES)

    def body(*refs):
        q_ref, kp_ref, kc_ref, vp_ref, vc_ref, do_ref, lse_ref, dl_ref = refs[:8]
        if first:
            dq_ref, dk_ref, dv_ref, ck, cv = refs[8:]
        else:
            dqi_ref, dki_ref, dvi_ref, dq_ref, dk_ref, dv_ref, ck, cv = refs[8:]
        n = pl.program_id(1)

        @pl.when(n == 0)
        def _():
            ck[...] = jnp.zeros_like(ck)
            cv[...] = jnp.zeros_like(cv)

        @pl.when(n < nb)
        def _():
            bias = mode.bias(n, True)
            bias2 = jnp.concatenate([bias, bias], axis=1)
            _, lo, m_lo, m_hi = _head_masks()
            lse_t = jnp.transpose(mode.get(lse_ref, all_lanes))
            dl_t = jnp.transpose(mode.get(dl_ref, all_lanes))
            for p, sl in enumerate(groups):
                q2, do2 = mode.get(q_ref, sl), mode.get(do_ref, sl)
                kcat = jnp.concatenate([mode.get(kp_ref, sl), mode.get(kc_ref, sl)], axis=0)
                vcat = jnp.concatenate([mode.get(vp_ref, sl), mode.get(vc_ref, sl)], axis=0)
                qq = jnp.concatenate([q2 * m_lo, q2 * m_hi], axis=0)
                dd = jnp.concatenate([do2 * m_lo, do2 * m_hi], axis=0)
                h0 = 2 * p
                lse2 = jnp.concatenate([lse_t[h0:h0 + 1, :], lse_t[h0 + 1:h0 + 2, :]], axis=1)
                dl2 = jnp.concatenate([dl_t[h0:h0 + 1, :], dl_t[h0 + 1:h0 + 2, :]], axis=1)
                p_t = jnp.exp(_nt(kcat, qq) + (bias2 - lse2))
                ds_t = p_t * (_nt(vcat, dd) - dl2)
                dsb = ds_t.astype(BF16)
                dkc = jnp.dot(dsb, qq, preferred_element_type=F32)
                dvc = jnp.dot(p_t.astype(BF16), dd, preferred_element_type=F32)
                dqb = _tn(dsb, kcat)
                dq2 = jnp.where(lo, dqb[:BLOCK], dqb[BLOCK:]) * ATTN_SCALE
                dk2 = ck[:, sl] + dkc[:BLOCK]
                dv2 = cv[:, sl] + dvc[:BLOCK]
                if not first:
                    dq2 = dq2 + mode.get(dqi_ref, sl)
                    dk2 = dk2 + mode.get(dki_ref, sl)
                    dv2 = dv2 + mode.get(dvi_ref, sl)
                mode.put(dq_ref, sl, dq2)
                mode.put(dk_ref, sl, dk2)
                mode.put(dv_ref, sl, dv2)
                ck[:, sl] = dkc[BLOCK:]
                cv[:, sl] = dvc[BLOCK:]

        @pl.when(n == nb)
        def _():
            for sl in groups:
                if first:
                    mode.put(dk_ref, sl, ck[:, sl])
                    mode.put(dv_ref, sl, cv[:, sl])
                else:
                    mode.put(dk_ref, sl, ck[:, sl] + mode.get(dki_ref, sl))
                    mode.put(dv_ref, sl, cv[:, sl] + mode.get(dvi_ref, sl))

    last = nb - 1
    cur = lambda w: mode.spec(w, "cur", last)
    prev = lambda w: mode.spec(w, "prev")
    ins = [q, k, k, v, v, d_o, lse, delta]
    specs = [cur(attn_w), prev(attn_w), cur(attn_w), prev(attn_w), cur(attn_w), cur(attn_w), cur(LANES), cur(LANES)]
    if not first:
        ins += list(run)
        specs += [cur(attn_w), prev(attn_w), prev(attn_w)]
    shp = jax.ShapeDtypeStruct((seq, attn_w) if nat else _perm_shape(seq, attn_w), F32)
    return pl.pallas_call(
        body, name=f"attn_bwd_{name}", grid=(mode.residues, nb + 1),
        out_shape=(shp, shp, shp), in_specs=specs, out_specs=(cur(attn_w), prev(attn_w), prev(attn_w)),
        scratch_shapes=[pltpu.VMEM((BLOCK, attn_w), F32), pltpu.VMEM((BLOCK, attn_w), F32)],
        compiler_params=_params(("arbitrary", "arbitrary")),
    )(*ins)


def _shift_down(u, halo, k):
    rolled = pltpu.roll(u, k, 0)
    row = lax.broadcasted_iota(jnp.int32, halo.shape, 0)
    top = jnp.where(row < k, pltpu.roll(halo, k, 0), rolled[:SUBLANES])
    return jnp.concatenate([top, rolled[SUBLANES:]], axis=0)


def _shift_up(u, halo, k):
    rows = u.shape[0]
    rolled = pltpu.roll(u, rows - k, 0)
    row = lax.broadcasted_iota(jnp.int32, halo.shape, 0)
    bot = jnp.where(row >= SUBLANES - k, pltpu.roll(halo, SUBLANES - k, 0), rolled[rows - SUBLANES:])
    return jnp.concatenate([rolled[:rows - SUBLANES], bot], axis=0)


def tail(o, lse, ga, cz, x, tgt, w_out, g2, cw):
    seq, d_model = x.shape
    attn_w = o.shape[1]
    conv_w = cz.shape[1] // 4
    mix = attn_w + conv_w
    groups = _lane_groups(attn_w)
    tm = ROW_TILE
    nt = seq // tm
    hb = tm // SUBLANES

    def body(o_ref, l_ref, ga_ref, cz_ref, hz_ref, x_ref, t_ref, w_ref, g_ref, cw_ref,
             do_ref, dl_ref, dop_ref, dlp_ref, lp_ref, dga_ref, dcb_ref, dgc_ref, dcv_ref, e_ref,
             dw_ref, dg_ref, dcw_ref, loss_ref, stage):
        i = pl.program_id(0)

        @pl.when(i == 0)
        def _():
            dw_ref[...] = jnp.zeros_like(dw_ref)
            dg_ref[...] = jnp.zeros_like(dg_ref)
            dcw_ref[...] = jnp.zeros_like(dcw_ref)
            loss_ref[...] = jnp.zeros_like(loss_ref)

        ov, gav = o_ref[...], ga_ref[...]
        sig_a = _sigmoid(gav)
        silu_a = gav * sig_a
        attn_out = ov * silu_a
        ch, cb = cz_ref[:, 0:conv_w], cz_ref[:, conv_w:2 * conv_w]
        cc, gc = cz_ref[:, 2 * conv_w:3 * conv_w], cz_ref[:, 3 * conv_w:4 * conv_w]
        u = cc * ch
        uh = hz_ref[:, 2 * conv_w:3 * conv_w] * hz_ref[:, 0:conv_w]
        uh = jnp.where(i > 0, uh, 0.0)
        u1 = _shift_down(u, uh, 1)
        u2 = _shift_down(u, uh, 2)
        w0, w1, w2 = cw_ref[0:1, :], cw_ref[1:2, :], cw_ref[2:3, :]
        cvv = u2 * w0 + u1 * w1 + u * w2
        sig_c = _sigmoid(gc)
        silu_c = gc * sig_c
        bc = cb * cvv
        conv_out = bc * silu_c
        mixed = jnp.concatenate([attn_out, conv_out], axis=1).astype(BF16)

        yv = jnp.dot(mixed, w_ref[...], preferred_element_type=F32)
        r2 = lax.rsqrt(jnp.mean(yv * yv, axis=-1, keepdims=True) + NORM_EPS)
        yhat = yv * r2
        gv = g_ref[...]
        diff = (x_ref[...] + yhat * gv) - t_ref[...]
        loss_ref[...] += _rowgroup_sum(diff * diff)
        ev = diff * (1.0 / d_model)
        e_ref[...] = ev
        dg_ref[...] += _rowgroup_sum(ev * yhat)
        eg = ev * gv
        dy = (r2 * (eg - yhat * jnp.mean(eg * yhat, axis=-1, keepdims=True))).astype(BF16)
        dw_ref[...] += _tn(mixed, dy)
        dm = _nt(dy, w_ref[...])
        dma, dmc = dm[:, :attn_w], dm[:, attn_w:]

        dov = dma * silu_a
        do_ref[...] = dov.astype(BF16)
        dga_ref[...] = (dma * ov * (sig_a * (1.0 + gav * (1.0 - sig_a)))).astype(BF16)
        prod = dov * ov
        lane = lax.broadcasted_iota(jnp.int32, (tm, LANES), 1)
        lo = lane < HEAD_DIM
        dblk = jnp.zeros((tm, LANES), F32)
        for p, sl in enumerate(groups):
            pr = prod[:, sl]
            dblk = jnp.where(lane == 2 * p, jnp.sum(jnp.where(lo, pr, 0.0), axis=1, keepdims=True), dblk)
            dblk = jnp.where(lane == 2 * p + 1, jnp.sum(jnp.where(lo, 0.0, pr), axis=1, keepdims=True), dblk)
            stage[p] = dov[:, sl]
        dl_ref[...] = dblk
        for p, sl in enumerate(groups):
            _to_perm(stage, p, dop_ref, sl, BF16)
        all_lanes = slice(0, LANES)
        stage[0] = dblk
        stage[1] = l_ref[...]
        _to_perm(stage, 0, dlp_ref, all_lanes, F32)
        _to_perm(stage, 1, lp_ref, all_lanes, F32)

        dsc = dmc * silu_c
        dcb_ref[...] = (dsc * cvv).astype(BF16)
        dgc_ref[...] = (dmc * bc * (sig_c * (1.0 + gc * (1.0 - sig_c)))).astype(BF16)
        dcv = dsc * cb
        dcv_ref[...] = dcv
        dcw_ref[0:SUBLANES, :] += _rowgroup_sum(dcv * u2)
        dcw_ref[SUBLANES:2 * SUBLANES, :] += _rowgroup_sum(dcv * u1)
        dcw_ref[2 * SUBLANES:3 * SUBLANES, :] += _rowgroup_sum(dcv * u)

    row = lambda n: pl.BlockSpec((tm, n), lambda i: (i, 0))
    whole = lambda a, b: pl.BlockSpec((a, b), lambda i: (0, 0))
    return pl.pallas_call(
        body, name="tail", grid=(nt,),
        out_shape=(jax.ShapeDtypeStruct((seq, attn_w), BF16), jax.ShapeDtypeStruct((seq, LANES), F32),
                   jax.ShapeDtypeStruct(_perm_shape(seq, attn_w), BF16), jax.ShapeDtypeStruct(_perm_shape(seq, LANES), F32),
                   jax.ShapeDtypeStruct(_perm_shape(seq, LANES), F32),
                   jax.ShapeDtypeStruct((seq, attn_w), BF16), jax.ShapeDtypeStruct((seq, conv_w), BF16),
                   jax.ShapeDtypeStruct((seq, conv_w), BF16), jax.ShapeDtypeStruct((seq, conv_w), F32),
                   jax.ShapeDtypeStruct((seq, d_model), F32), jax.ShapeDtypeStruct((mix, d_model), F32),
                   jax.ShapeDtypeStruct((SUBLANES, d_model), F32), jax.ShapeDtypeStruct((CONV_K * SUBLANES, conv_w), F32),
                   jax.ShapeDtypeStruct((SUBLANES, d_model), F32)),
        in_specs=[row(attn_w), row(LANES), row(attn_w), row(4 * conv_w),
                  pl.BlockSpec((SUBLANES, 4 * conv_w), lambda i: (jnp.maximum(i * hb - 1, 0), 0)),
                  row(d_model), row(d_model), _const_spec((mix, d_model)), _const_spec((1, d_model)),
                  _const_spec((SUBLANES, conv_w))],
        out_specs=(row(attn_w), row(LANES), _perm_tile_spec(attn_w, tm), _perm_tile_spec(LANES, tm), _perm_tile_spec(LANES, tm),
                   row(attn_w), row(conv_w), row(conv_w), row(conv_w), row(d_model),
                   whole(mix, d_model), whole(SUBLANES, d_model), whole(CONV_K * SUBLANES, conv_w),
                   whole(SUBLANES, d_model)),
        scratch_shapes=[pltpu.VMEM((max(len(groups), 2), tm, LANES), F32)],
        compiler_params=_params(("arbitrary",)),
    )(o, lse, ga, cz, cz, x, tgt, w_out, g2, cw)


def dz_dx(nat_grads, perm_grads, dga, dcb, dgc, dcv, cz, tables, x, g1, e, w_full, cw):
    seq, d_model = x.shape
    attn_w = dga.shape[1]
    conv_w = dcv.shape[1]
    width = w_full.shape[2]
    in_w = 4 * attn_w + 4 * conv_w
    groups = _lane_groups(attn_w)
    tm = DZ_ROW_TILE
    nt = seq // tm
    hb = tm // SUBLANES

    def body(dq_ref, dk_ref, dv_ref, dqp_ref, dkp_ref, dvp_ref, dga_ref, dcb_ref, dgc_ref, dcv_ref, nh_ref, cz_ref,
             cos_ref, s1_ref, s2_ref, x_ref, g_ref, e_ref, w_ref, cw_ref, gx_ref, dz_ref, dg_ref, stage):
        i = pl.program_id(0)

        @pl.when(i == 0)
        def _():
            dg_ref[...] = jnp.zeros_like(dg_ref)

        cos, s1, s2 = cos_ref[...], s1_ref[...], s2_ref[...]
        for t, (nat_ref, perm_ref) in enumerate(((dq_ref, dqp_ref), (dk_ref, dkp_ref), (dv_ref, dvp_ref))):
            for g, sl in enumerate(groups):
                _from_perm(perm_ref, sl, stage, g)
            for g, sl in enumerate(groups):
                tot = nat_ref[:, sl] + stage[g]
                if t < 2:
                    tot = _rope_transposed(tot, cos, s1, s2)
                dz_ref[:, t * attn_w + g * LANES:t * attn_w + (g + 1) * LANES] = tot.astype(BF16)
        dz_ref[:, 3 * attn_w:4 * attn_w] = dga_ref[...]
        dcv = dcv_ref[...]
        nh = jnp.where(i < nt - 1, nh_ref[...], 0.0)
        w0, w1, w2 = cw_ref[0:1, :], cw_ref[1:2, :], cw_ref[2:3, :]
        du = dcv * w2 + _shift_up(dcv, nh, 1) * w1 + _shift_up(dcv, nh, 2) * w0
        base = 4 * attn_w
        dz_ref[:, base:base + conv_w] = (du * cz_ref[:, 2 * conv_w:3 * conv_w]).astype(BF16)
        dz_ref[:, base + conv_w:base + 2 * conv_w] = dcb_ref[...]
        dz_ref[:, base + 2 * conv_w:base + 3 * conv_w] = (du * cz_ref[:, 0:conv_w]).astype(BF16)
        dz_ref[:, base + 3 * conv_w:base + 4 * conv_w] = dgc_ref[...]

        dh = _nt(dz_ref[:, 0:width], w_ref[0])
        for j in range(1, N_CHIPS):
            dh = dh + _nt(dz_ref[:, j * width:(j + 1) * width], w_ref[j])
        xv = x_ref[...]
        r1 = lax.rsqrt(jnp.mean(xv * xv, axis=-1, keepdims=True) + NORM_EPS)
        xhat = xv * r1
        dg_ref[...] += _rowgroup_sum(dh * xhat)
        dhg = dh * g_ref[...]
        gx_ref[...] = r1 * (dhg - xhat * jnp.mean(dhg * xhat, axis=-1, keepdims=True)) + e_ref[...]

    row = lambda n: pl.BlockSpec((tm, n), lambda i: (i, 0))
    whole = lambda a, b: pl.BlockSpec((a, b), lambda i: (0, 0))
    pt = _perm_tile_spec(attn_w, tm)
    return pl.pallas_call(
        body, name="dz_dx", grid=(nt,),
        out_shape=(jax.ShapeDtypeStruct((seq, d_model), F32), jax.ShapeDtypeStruct((seq, in_w), BF16),
                   jax.ShapeDtypeStruct((SUBLANES, d_model), F32)),
        in_specs=[row(attn_w), row(attn_w), row(attn_w), pt, pt, pt, row(attn_w), row(conv_w), row(conv_w), row(conv_w),
                  pl.BlockSpec((SUBLANES, conv_w), lambda i: (jnp.minimum((i + 1) * hb, seq // SUBLANES - 1), 0)),
                  row(4 * conv_w), row(LANES), row(LANES), row(LANES), row(d_model), _const_spec((1, d_model)), row(d_model),
                  _const_spec(w_full.shape), _const_spec((SUBLANES, conv_w))],
        out_specs=(row(d_model), row(in_w), whole(SUBLANES, d_model)),
        scratch_shapes=[pltpu.VMEM((len(groups), tm, LANES), F32)],
        compiler_params=_params(("arbitrary",)),
    )(*nat_grads, *perm_grads, dga, dcb, dgc, dcv, dcv, cz, *tables, x, g1, e, w_full, cw)


def dw_in(h, dz):
    seq, d_model = h.shape
    half = dz.shape[1] // N_DEV
    ts = 512
    steps = seq // ts

    def body(h_ref, dz_ref, o_ref):
        @pl.when(pl.program_id(1) == 0)
        def _():
            o_ref[...] = jnp.zeros_like(o_ref)

        o_ref[0] += _tn(h_ref[...], dz_ref[...])

    return pl.pallas_call(
        body, name="dw_in", grid=(N_DEV, steps),
        out_shape=jax.ShapeDtypeStruct((N_DEV, d_model, half), F32),
        in_specs=[pl.BlockSpec((ts, d_model), lambda j, s: (s, 0)), pl.BlockSpec((ts, half), lambda j, s: (s, j))],
        out_specs=pl.BlockSpec((1, d_model, half), lambda j, s: (j, 0, 0)),
        compiler_params=_params(("arbitrary", "arbitrary")),
    )(h, dz)


def grad_reduce(g_in, g_out, small):
    gi = g_in.reshape(N_CHIPS, 2, *g_in.shape[1:])
    go = g_out.reshape(N_CHIPS, 2, *g_out.shape[1:])
    shp_i, shp_o = gi.shape[2:], go.shape[2:]

    def body(gi_ref, go_ref, sm_ref, ri_ref, ro_ref, rs_ref, a_i, b_i, s_i, c_i, a_o, b_o, s_o, c_o, sbuf,
             loc_sems, sa, ra, sb, rb, sc, rc, ss, rs):
        x, y, c = lax.axis_index("x"), lax.axis_index("y"), lax.axis_index("c")
        me = 2 * x + y
        sib = (x, y, 1 - c)
        srcs, mine, theirs, staged, contrib, res = ((gi_ref, go_ref), (a_i, a_o), (b_i, b_o), (s_i, s_o), (c_i, c_o),
                                                    (ri_ref, ro_ref))

        flips = [(fx, fy, fc) for fx in (0, 1) for fy in (0, 1) for fc in (0, 1)][1:]
        my8 = 4 * x + 2 * y + c
        sbuf[my8] = sm_ref[...]

        def small_copy(k, slot, to):
            return pltpu.make_async_remote_copy(src_ref=sm_ref, dst_ref=sbuf.at[slot], send_sem=ss.at[k], recv_sem=rs.at[k],
                                                device_id=to, device_id_type=MESH)

        sends = []
        for k, (fx, fy, fc) in enumerate(flips):
            px, py, pc = _flip(x, fx), _flip(y, fy), _flip(c, fc)
            sends.append(small_copy(k, my8, (px, py, pc)))
            sends[-1].start()

        def a_copy(t, j):
            return pltpu.make_async_remote_copy(src_ref=srcs[t].at[j, 1 - c], dst_ref=theirs[t].at[j], send_sem=sa.at[t, j],
                                                recv_sem=ra.at[t, j], device_id=sib, device_id_type=MESH)

        peers = _chip_peers(x, y)
        order = [2 * px + py for px, py in peers] + [me]
        loads = [[pltpu.make_async_copy(srcs[t].at[j, c], mine[t].at[j], loc_sems.at[t, j]) for t in range(2)] for j in order]
        for pos, j in enumerate(order):
            for t in range(2):
                loads[pos][t].start()
                sends.append(a_copy(t, j))
                sends[-1].start()

        def b_copy(k, t, piece, slot, to):
            return pltpu.make_async_remote_copy(src_ref=staged[t].at[piece], dst_ref=contrib[t].at[slot], send_sem=sb.at[k, t],
                                                recv_sem=rb.at[k, t], device_id=to, device_id_type=MESH)

        for k, (px, py) in enumerate(peers):
            j = 2 * px + py
            for t in range(2):
                loads[k][t].wait()
                a_copy(t, j).wait_recv()
                staged[t][j] = (mine[t][j] + theirs[t][j]).astype(BF16)
                sends.append(b_copy(k, t, j, me, (px, py, c)))
                sends[-1].start()
        for t in range(2):
            loads[len(peers)][t].wait()
            a_copy(t, me).wait_recv()
            mine[t][me] = mine[t][me] + theirs[t][me]
            contrib[t][me] = mine[t][me].astype(BF16)
        for k, (px, py) in enumerate(peers):
            for t in range(2):
                b_copy(k, t, me, 2 * px + py, (px, py, c)).wait_recv()

        def c_copy(t, half):
            return pltpu.make_async_remote_copy(src_ref=res[t].at[half], dst_ref=res[t].at[half], send_sem=sc.at[t],
                                                recv_sem=rc.at[t], device_id=sib, device_id_type=MESH)

        for t in range(2):
            own = mine[t][me]
            term = lambda j: jnp.where(me == j, own, contrib[t][j].astype(F32))
            res[t][c] = ((term(0) + term(1)) + term(2)) + term(3)
            sends.append(c_copy(t, c))
            sends[-1].start()
        for t in range(2):
            c_copy(t, 1 - c).wait_recv()

        for k, (fx, fy, fc) in enumerate(flips):
            px, py, pc = _flip(x, fx), _flip(y, fy), _flip(c, fc)
            small_copy(k, 4 * px + 2 * py + pc, (px, py, pc)).wait_recv()
        tot = sbuf[0]
        for d in range(1, N_DEV):
            tot = tot + sbuf[d]
        rs_ref[...] = tot
        for cp in sends:
            cp.wait_send()

    vm = pl.BlockSpec(memory_space=pltpu.VMEM)
    anyspace = pl.BlockSpec(memory_space=pl.ANY)
    dma = pltpu.SemaphoreType.DMA
    bufs = lambda shp: [pltpu.VMEM((N_CHIPS, *shp), F32), pltpu.VMEM((N_CHIPS, *shp), F32),
                        pltpu.VMEM((N_CHIPS, *shp), BF16), pltpu.VMEM((N_CHIPS, *shp), BF16)]
    return pl.pallas_call(
        body, name="grad_reduce",
        out_shape=(jax.ShapeDtypeStruct((2, *shp_i), F32), jax.ShapeDtypeStruct((2, *shp_o), F32),
                   jax.ShapeDtypeStruct(small.shape, F32)),
        in_specs=[anyspace, anyspace, vm], out_specs=(vm, vm, vm),
        scratch_shapes=[*bufs(shp_i), *bufs(shp_o), pltpu.VMEM((N_DEV, *small.shape), F32),
                        dma((2, N_CHIPS)), dma((2, N_CHIPS)), dma((2, N_CHIPS)), dma((3, 2)), dma((3, 2)), dma((2,)), dma((2,)),
                        dma((N_DEV - 1,)), dma((N_DEV - 1,))],
        compiler_params=_params(),
    )(gi, go, small)


def _adam_math(w, g, m, v):
    m = ADAM_B1 * m + (1.0 - ADAM_B1) * g
    v = ADAM_B2 * v + (1.0 - ADAM_B2) * (g * g)
    m_hat = m / (1.0 - ADAM_B1 ** ADAM_STEP)
    v_hat = v / (1.0 - ADAM_B2 ** ADAM_STEP)
    delta = -ADAM_LR * (m_hat / (jnp.sqrt(v_hat) + ADAM_EPS) + ADAM_WD * w)
    return delta, m, v


def adam_shard(name, w, g2, m, v, block, grid, w_map, g_map):
    def body(w_ref, g_ref, m_ref, v_ref, go_ref, d_ref, mo_ref, vo_ref):
        g = g_ref[0]
        delta, mn, vn = _adam_math(w_ref[...], g, m_ref[...], v_ref[...])
        go_ref[...] = g
        d_ref[...] = delta
        mo_ref[...] = mn
        vo_ref[...] = vn

    ws = pl.BlockSpec(block, w_map)
    shp = jax.ShapeDtypeStruct(w.shape, F32)
    return pl.pallas_call(
        body, name=name, grid=grid, out_shape=(shp, shp, shp, shp),
        in_specs=[ws, pl.BlockSpec((1, *block), g_map), ws, ws], out_specs=(ws, ws, ws, ws),
        compiler_params=_params(("arbitrary",) * len(grid)),
    )(w, g2, m, v)


def adam_small(ws, gs, ms, vs):
    n = len(ws)

    def body(*refs):
        ins, outs = refs[:4 * n], refs[4 * n:]
        for t in range(n):
            delta, mn, vn = _adam_math(ins[t][...], ins[n + t][...], ins[2 * n + t][...], ins[3 * n + t][...])
            outs[3 * t][...] = delta
            outs[3 * t + 1][...] = mn
            outs[3 * t + 2][...] = vn

    vm = pl.BlockSpec(memory_space=pltpu.VMEM)
    outs = pl.pallas_call(
        body, name="adam_small",
        out_shape=tuple(jax.ShapeDtypeStruct(w.shape, F32) for w in ws for _ in range(3)),
        in_specs=[vm] * (4 * n), out_specs=tuple([vm] * (3 * n)),
        compiler_params=_params(),
    )(*ws, *gs, *ms, *vs)
    return [outs[3 * t:3 * t + 3] for t in range(n)]


def kernel(x, norm_pre_g, w_in, conv_w, w_out, norm_post_g, loss_target, m_norm_pre_g, m_w_in, m_conv_w, m_w_out, m_norm_post_g, v_norm_pre_g, v_w_in, v_conv_w, v_w_out, v_norm_post_g):
    _, seq, d_model = x.shape
    width = w_in.shape[1]
    conv_q = conv_w.shape[1]
    conv_width = N_CHIPS * conv_q
    attn_width = d_model - conv_width
    xs, tg = x[0], loss_target[0]
    g1, g2 = norm_pre_g.reshape(1, d_model), norm_post_g.reshape(1, d_model)

    w_full, wout_full, cw_full = gather_weights(w_in, w_out, conv_w)
    wout2 = wout_full.reshape(attn_width + conv_width, d_model)
    cw = jnp.zeros((SUBLANES, conv_width), F32).at[:CONV_K].set(
        cw_full[:, :CONV_K, :conv_q].transpose(1, 0, 2).reshape(CONV_K, conv_width))
    tables = _rope_tables(seq)

    h, q, k, v, qp, kp, vp, ga, cz = inproj(xs, g1, w_full, tables, attn_width, conv_width)
    run = attn_fwd("p4", qp, kp, vp, None)
    run = attn_fwd("p16", qp, kp, vp, run)
    o, lse = attn_fwd("nat", q, k, v, run)
    (d_o, delta, d_op, delta_p, lse_p, dga, dcb, dgc, dcv, e, dwout, dg2, dcw, loss_acc) = tail(
        o, lse, ga, cz, xs, tg, wout2, g2, cw)
    nat_grads = attn_bwd("nat", q, k, v, d_o, lse, delta, None)
    perm_grads = attn_bwd("p4", qp, kp, vp, d_op, lse_p, delta_p, None)
    perm_grads = attn_bwd("p16", qp, kp, vp, d_op, lse_p, delta_p, perm_grads)
    grad_x, dz, dg1 = dz_dx(nat_grads, perm_grads, dga, dcb, dgc, dcv, cz, tables, xs, g1, e, w_full, cw)
    dwin = dw_in(h, dz)

    small = jnp.zeros((SUBLANES, d_model), F32)
    small = small.at[0].set(dg1.sum(axis=0)).at[1].set(dg2.sum(axis=0))
    small = small.at[2:2 + CONV_K, :conv_width].set(dcw.reshape(CONV_K, SUBLANES, conv_width).sum(axis=1))
    rin, rout, rsmall = grad_reduce(dwin, dwout.reshape(N_DEV, -1, d_model), small)

    half = width // 2
    tr = 256
    gw_in, d_in, m_in, v_in = adam_shard(
        "adam_w_in", w_in, rin, m_w_in, v_w_in, (tr, half), (2, d_model // tr),
        lambda hf, i: (i, hf), lambda hf, i: (hf, i, 0))
    rq = w_out.shape[0] // 2
    gw_out, d_out, m_out, v_out = adam_shard(
        "adam_w_out", w_out, rout, m_w_out, v_w_out, (rq, d_model), (2,),
        lambda hf: (hf, 0), lambda hf: (hf, 0, 0))

    chip = 2 * lax.axis_index("x") + lax.axis_index("y")
    g_pre, g_post = rsmall[0:1], rsmall[1:2]
    g_conv = lax.dynamic_slice(rsmall[2:2 + CONV_K, :conv_width], (0, chip * conv_q), (CONV_K, conv_q))
    (d_pre, m_pre, v_pre), (d_post, m_post, v_post), (d_cv, m_cv, v_cv) = adam_small(
        [g1, g2, conv_w], [g_pre, g_post, g_conv],
        [m_norm_pre_g.reshape(1, d_model), m_norm_post_g.reshape(1, d_model), m_conv_w],
        [v_norm_pre_g.reshape(1, d_model), v_norm_post_g.reshape(1, d_model), v_conv_w])

    loss = lax.psum(0.5 * jnp.sum(loss_acc) / d_model, ("x", "y", "c"))
    vec = lambda a: a.reshape(d_model)
    return (loss, grad_x.reshape(1, seq, d_model),
            vec(g_pre), gw_in, g_conv, gw_out, vec(g_post),
            vec(d_pre), d_in, d_cv, d_out, vec(d_post),
            vec(m_pre), m_in, m_cv, m_out, vec(m_post),
            vec(v_pre), v_in, v_cv, v_out, vec(v_post))
```

```python
import jax
import jax.numpy as jnp
from jax import lax
from jax.experimental import pallas as pl
from jax.experimental.pallas import tpu as pltpu

HEAD_DIM = 64
LANES = 128
SUBLANES = 8
BLOCK = 128
WINDOW_KEYS = 128
PERM = 16
PJ = 4
P4_ROWS = BLOCK // PJ
ROW_TILE = 512
DZ_ROW_TILE = 256
CONV_K = 3
ROPE_THETA = 10000.0
NORM_EPS = 1e-6
ATTN_SCALE = HEAD_DIM ** -0.5
NEG = -1e30
N_CHIPS = 4
N_DEV = 8
MESH = pl.DeviceIdType.MESH
ADAM_LR = 0.001
ADAM_B1 = 0.9
ADAM_B2 = 0.999
ADAM_EPS = 1e-08
ADAM_WD = 0.01
ADAM_STEP = 10
VMEM_LIMIT = 52 * 1024 * 1024

F32 = jnp.float32
BF16 = jnp.bfloat16


def _params(sem=None, **kw):
    return pltpu.CompilerParams(dimension_semantics=sem, vmem_limit_bytes=VMEM_LIMIT, **kw)


def _const_spec(shape):
    return pl.BlockSpec(shape, lambda *_: (0,) * len(shape), pipeline_mode=pl.Buffered(1))


def _sigmoid(z):
    return 1.0 / (1.0 + jnp.exp(-z))


def _rowgroup_sum(a):
    rows, n = a.shape
    return a.reshape(rows // SUBLANES, SUBLANES, n).sum(axis=0)


def _nt(a, b):
    return lax.dot_general(a, b, (((1,), (1,)), ((), ())), preferred_element_type=F32)


def _tn(a, b):
    return lax.dot_general(a, b, (((0,), (0,)), ((), ())), preferred_element_type=F32)


def _col_pieces(a, b, width):
    out = []
    while a < b:
        j = a // width
        e = min(b, (j + 1) * width)
        out.append((j, a - j * width, e - j * width))
        a = e
    return out


def _lane_groups(width):
    return [slice(g * LANES, (g + 1) * LANES) for g in range(width // LANES)]


def _perm_shape(seq, width):
    return (PJ, PJ, seq // PERM, width)


def _perm_tile_spec(width, tm):
    return pl.BlockSpec((PJ, PJ, tm // PERM, width), lambda i: (0, 0, i, 0))


def _to_perm(stage, g, dst_ref, sl, dtype):
    rows = stage.shape[1] // PERM
    for b in range(PERM):
        dst_ref[b // PJ, b % PJ, :, sl] = stage[g, pl.ds(b, rows, stride=PERM), :].astype(dtype)


def _from_perm(src_ref, sl, stage, g):
    rows = stage.shape[1] // PERM
    for b in range(PERM):
        stage[g, pl.ds(b, rows, stride=PERM), :] = src_ref[b // PJ, b % PJ, :, sl]


def _flip(a, f):
    return 1 - a if f else a


def _chip_peers(x, y):
    return [(1 - x, y), (x, 1 - y), (1 - x, 1 - y)]


def gather_weights(w_in, w_out, conv_w):
    d_model, width = w_in.shape
    rows = w_out.shape[0]
    cw = jnp.zeros((SUBLANES, LANES), F32).at[:CONV_K, :conv_w.shape[1]].set(conv_w)

    def body(win_ref, wout_ref, cw_ref, winf_ref, woutf_ref, cwf_ref, st_in, st_out, ici_send, ici_recv, d2d_send, d2d_recv):
        x, y, c = lax.axis_index("x"), lax.axis_index("y"), lax.axis_index("c")
        me = 2 * x + y
        sib = (x, y, 1 - c)
        st_in[...] = win_ref[...].astype(BF16)
        st_out[...] = wout_ref[...].astype(BF16)
        winf_ref[me] = st_in[...]
        woutf_ref[me] = st_out[...]
        cwf_ref[me] = cw_ref[...]
        stages = (st_in, st_out)
        fulls = (winf_ref, woutf_ref)
        halves = (d_model // 2, rows // 2)

        def half(t, core):
            return pl.ds(pl.multiple_of(core * halves[t], halves[t]), halves[t])

        def ici(k, t, slot, to, core):
            src = stages[t].at[half(t, core)] if t < 2 else cw_ref
            dst = fulls[t].at[slot, half(t, core)] if t < 2 else cwf_ref.at[slot]
            return pltpu.make_async_remote_copy(src_ref=src, dst_ref=dst, send_sem=ici_send.at[k, t], recv_sem=ici_recv.at[k, t],
                                                device_id=to, device_id_type=MESH)

        def d2d(k, t, slot, core):
            ref = fulls[t].at[slot, half(t, core)]
            return pltpu.make_async_remote_copy(src_ref=ref, dst_ref=ref, send_sem=d2d_send.at[k, t], recv_sem=d2d_recv.at[k, t],
                                                device_id=sib, device_id_type=MESH)

        peers = _chip_peers(x, y)
        sends = [ici(k, t, me, (px, py, c), c) for k, (px, py) in enumerate(peers) for t in range(3)]
        for cp in sends:
            cp.start()
        for k, (px, py) in enumerate(peers):
            for t in range(2):
                ici(k, t, 2 * px + py, (px, py, c), c).wait_recv()
                fwd = d2d(k, t, 2 * px + py, c)
                fwd.start()
                sends.append(fwd)
            ici(k, 2, 2 * px + py, (px, py, c), c).wait_recv()
        for k, (px, py) in enumerate(peers):
            for t in range(2):
                d2d(k, t, 2 * px + py, 1 - c).wait_recv()
        for cp in sends:
            cp.wait_send()

    vm = pl.BlockSpec(memory_space=pltpu.VMEM)
    dma = pltpu.SemaphoreType.DMA
    return pl.pallas_call(
        body, name="gather_weights",
        out_shape=(jax.ShapeDtypeStruct((N_CHIPS, d_model, width), BF16),
                   jax.ShapeDtypeStruct((N_CHIPS, rows, d_model), BF16),
                   jax.ShapeDtypeStruct((N_CHIPS, SUBLANES, LANES), F32)),
        in_specs=[vm, vm, vm], out_specs=(vm, vm, vm),
        scratch_shapes=[pltpu.VMEM((d_model, width), BF16), pltpu.VMEM((rows, d_model), BF16),
                        dma((3, 3)), dma((3, 3)), dma((3, 2)), dma((3, 2))],
        compiler_params=_params(),
    )(w_in, w_out, cw)


def _rope_tables(seq):
    half = HEAD_DIM // 2
    inv_freq = ROPE_THETA ** (-jnp.arange(half, dtype=F32) * 2.0 / HEAD_DIM)
    ang = jnp.arange(seq).astype(F32)[:, None] * inv_freq[None, :]
    cos, sin = jnp.cos(ang), jnp.sin(ang)
    zero = jnp.zeros_like(sin)
    return (jnp.concatenate([cos, cos, cos, cos], axis=1),
            jnp.concatenate([-sin, zero, -sin, zero], axis=1),
            jnp.concatenate([zero, sin, zero, sin], axis=1))


def _rope(t, cos, s1, s2):
    return t * cos + pltpu.roll(t, LANES - HEAD_DIM // 2, 1) * s1 + pltpu.roll(t, HEAD_DIM // 2, 1) * s2


def _rope_transposed(g, cos, s1, s2):
    return g * cos + pltpu.roll(g * s1, HEAD_DIM // 2, 1) + pltpu.roll(g * s2, LANES - HEAD_DIM // 2, 1)


def inproj(x, g1, w_full, tables, attn_w, conv_w):
    seq, d_model = x.shape
    width = w_full.shape[2]
    tm = ROW_TILE
    groups = _lane_groups(attn_w)

    def body(x_ref, g_ref, w_ref, cos_ref, s1_ref, s2_ref,
             ht_ref, q_ref, k_ref, v_ref, qp_ref, kp_ref, vp_ref, ga_ref, cz_ref, stage):
        xv = x_ref[...]
        hb = ((xv * lax.rsqrt(jnp.mean(xv * xv, axis=-1, keepdims=True) + NORM_EPS)) * g_ref[...]).astype(BF16)
        ht_ref[...] = jnp.transpose(hb)
        cos, s1, s2 = cos_ref[...], s1_ref[...], s2_ref[...]

        def proj(a, b):
            parts = [jnp.dot(hb, w_ref[j, :, lo:hi], preferred_element_type=F32) for j, lo, hi in _col_pieces(a, b, width)]
            return parts[0] if len(parts) == 1 else jnp.concatenate(parts, axis=1)

        def emit(z, nat_ref, perm_ref, fn):
            for g, sl in enumerate(groups):
                val = fn(z[:, sl])
                nat_ref[:, sl] = val.astype(BF16)
                stage[g] = val
            for g, sl in enumerate(groups):
                _to_perm(stage, g, perm_ref, sl, BF16)

        emit(proj(0, attn_w), q_ref, qp_ref, lambda t: _rope(t, cos, s1, s2) * ATTN_SCALE)
        emit(proj(attn_w, 2 * attn_w), k_ref, kp_ref, lambda t: _rope(t, cos, s1, s2))
        emit(proj(2 * attn_w, 3 * attn_w), v_ref, vp_ref, lambda t: t)
        ga_ref[...] = proj(3 * attn_w, 4 * attn_w)
        cz_ref[...] = proj(4 * attn_w, 4 * attn_w + 4 * conv_w)

    row = lambda n: pl.BlockSpec((tm, n), lambda i: (i, 0))
    nat = jax.ShapeDtypeStruct((seq, attn_w), BF16)
    perm = jax.ShapeDtypeStruct(_perm_shape(seq, attn_w), BF16)
    return pl.pallas_call(
        body, name="inproj", grid=(seq // tm,),
        out_shape=(jax.ShapeDtypeStruct((d_model, seq), BF16), nat, nat, nat, perm, perm, perm,
                   jax.ShapeDtypeStruct((seq, attn_w), F32), jax.ShapeDtypeStruct((seq, 4 * conv_w), F32)),
        in_specs=[row(d_model), _const_spec((1, d_model)), _const_spec(w_full.shape), row(LANES), row(LANES), row(LANES)],
        out_specs=(pl.BlockSpec((d_model, tm), lambda i: (0, i)), row(attn_w), row(attn_w), row(attn_w),
                   _perm_tile_spec(attn_w, tm), _perm_tile_spec(attn_w, tm), _perm_tile_spec(attn_w, tm),
                   row(attn_w), row(4 * conv_w)),
        scratch_shapes=[pltpu.VMEM((len(groups), tm, LANES), F32)],
        compiler_params=_params(("arbitrary",)),
    )(x, g1, w_full, *tables)


class _Mode:
    def __init__(self, name, seq):
        self.name = name
        if name == "nat":
            self.residues, self.nb = 1, seq // BLOCK
        elif name == "p16":
            self.residues, self.nb = PERM, seq // PERM // BLOCK
        else:
            self.residues, self.nb = PJ, seq // PERM // P4_ROWS

    def spec(self, width, which, last=None):
        if which == "prev":
            blk = lambda n: jnp.maximum(n - 1, 0)
        elif last is None:
            blk = lambda n: n
        else:
            blk = lambda n: jnp.minimum(n, last)
        if self.name == "nat":
            return pl.BlockSpec((BLOCK, width), lambda r, n: (blk(n), 0))
        if self.name == "p16":
            return pl.BlockSpec((1, 1, BLOCK, width), lambda r, n: (r // PJ, r % PJ, blk(n), 0))
        return pl.BlockSpec((PJ, 1, P4_ROWS, width), lambda r, n: (0, r, blk(n), 0))

    def get(self, ref, sl):
        if self.name == "nat":
            return ref[:, sl]
        if self.name == "p16":
            return ref[0, 0, :, sl]
        return jnp.concatenate([ref[j, 0, :, sl] for j in range(PJ)], axis=0)

    def put(self, ref, sl, val):
        if self.name == "nat":
            ref[:, sl] = val
        elif self.name == "p16":
            ref[0, 0, :, sl] = val
        else:
            for j in range(PJ):
                ref[j, 0, :, sl] = val[j * P4_ROWS:(j + 1) * P4_ROWS]

    def index(self, idx, is_key):
        if self.name != "p4":
            return idx - BLOCK if is_key else idx
        within = jnp.bitwise_and(idx, BLOCK - 1)
        m = PJ * jnp.bitwise_and(within, P4_ROWS - 1) + jnp.right_shift(within, P4_ROWS.bit_length() - 1)
        return m + BLOCK * (jnp.right_shift(idx, BLOCK.bit_length() - 1) - 1) if is_key else m

    def bias(self, n, keys_major):
        shape = (2 * BLOCK, BLOCK) if keys_major else (BLOCK, 2 * BLOCK)
        kdim = 0 if keys_major else 1
        kidx = lax.broadcasted_iota(jnp.int32, shape, kdim)
        qidx = lax.broadcasted_iota(jnp.int32, shape, 1 - kdim)
        rel = self.index(qidx, False) - self.index(kidx, True)
        valid = (rel >= 0) & (rel <= WINDOW_KEYS) & ((kidx >= BLOCK) | (n > 0))
        return jnp.where(valid, 0.0, NEG)


def _head_masks():
    lane = lax.broadcasted_iota(jnp.int32, (BLOCK, LANES), 1)
    lo = lane < HEAD_DIM
    return lane, lo, jnp.where(lo, 1.0, 0.0).astype(BF16), jnp.where(lo, 0.0, 1.0).astype(BF16)


def _column(blk, lane, h):
    return jnp.sum(jnp.where(lane == h, blk, 0.0), axis=1, keepdims=True)


def attn_fwd(name, q, k, v, run):
    nat = name == "nat"
    seq = q.shape[0] if nat else q.shape[2] * PERM
    attn_w = q.shape[-1]
    mode = _Mode(name, seq)
    groups = _lane_groups(attn_w)
    first = run is None
    all_lanes = slice(0, LANES)

    def body(*refs):
        q_ref, kp_ref, kc_ref, vp_ref, vc_ref = refs[:5]
        if first:
            o_ref, l_ref = refs[5:]
        elif nat:
            oin_ref, lin_ref, ex_ref, o_ref, l_ref, ostage, lstage = refs[5:]
        else:
            oin_ref, lin_ref, ex_ref, o_ref, l_ref = refs[5:]
        n = pl.program_id(1)
        bias = mode.bias(n, True)
        bias2 = jnp.concatenate([bias, bias], axis=1)
        _, lo, m_lo, m_hi = _head_masks()
        head_row = lax.broadcasted_iota(jnp.int32, (BLOCK, LANES), 0)
        lrows = jnp.zeros((BLOCK, LANES), F32)
        def probs(sl):
            q2 = mode.get(q_ref, sl)
            kcat = jnp.concatenate([mode.get(kp_ref, sl), mode.get(kc_ref, sl)], axis=0)
            vcat = jnp.concatenate([mode.get(vp_ref, sl), mode.get(vc_ref, sl)], axis=0)
            qq = jnp.concatenate([q2 * m_lo, q2 * m_hi], axis=0)
            s_t = _nt(kcat, qq) + bias2
            m = jnp.max(s_t, axis=0, keepdims=True)
            pe = jnp.exp(s_t - m)
            l = jnp.sum(pe, axis=0, keepdims=True)
            return vcat, (pe * (1.0 / l)).astype(BF16), m + jnp.log(l)

        def output(p, sl, vcat, pn, lse, lrows):
            o_new = _tn(pn, vcat)
            mode.put(o_ref, sl, jnp.where(lo, o_new[:BLOCK], o_new[BLOCK:]))
            lrows = jnp.where(head_row == 2 * p, lse[:, :BLOCK], lrows)
            return jnp.where(head_row == 2 * p + 1, lse[:, BLOCK:], lrows)

        pending = None
        for p, sl in enumerate(groups):
            nxt = probs(sl)
            if pending is not None:
                lrows = output(*pending, lrows)
            pending = (p, sl, *nxt)
        lrows = output(*pending, lrows)
        lblk = jnp.transpose(lrows)
        if first:
            mode.put(l_ref, all_lanes, lblk)
        else:
            if nat:
                for g, sl in enumerate(groups):
                    _from_perm(oin_ref, sl, ostage, g)
                _from_perm(lin_ref, all_lanes, lstage, 0)
                lin = lstage[0]
            else:
                lin = mode.get(lin_ref, all_lanes)
            mx = jnp.maximum(lin, lblk)
            new = mx + jnp.log(jnp.exp(lin - mx) + jnp.exp(lblk - mx))
            mode.put(l_ref, all_lanes, new)

            def expand(w):
                hi = w.astype(BF16)
                rest = (w - hi.astype(F32)).astype(BF16)
                ex = ex_ref[...]
                return jnp.dot(hi, ex, preferred_element_type=F32) + jnp.dot(rest, ex, preferred_element_type=F32)

            w_prev, w_cur = expand(jnp.exp(lin - new)), expand(jnp.exp(lblk - new))
            for p, sl in enumerate(groups):
                o_prev = ostage[p] if nat else mode.get(oin_ref, sl)
                mode.put(o_ref, sl, w_prev[:, sl] * o_prev + w_cur[:, sl] * mode.get(o_ref, sl))

    ins = [q, k, k, v, v]
    specs = [mode.spec(attn_w, "cur"), mode.spec(attn_w, "prev"), mode.spec(attn_w, "cur"),
             mode.spec(attn_w, "prev"), mode.spec(attn_w, "cur")]
    scratch = []
    if not first:
        ins += list(run)
        if nat:
            rows8 = BLOCK // PERM
            specs += [pl.BlockSpec((PJ, PJ, rows8, attn_w), lambda r, n: (0, 0, n, 0)),
                      pl.BlockSpec((PJ, PJ, rows8, LANES), lambda r, n: (0, 0, n, 0))]
            scratch = [pltpu.VMEM((len(groups), BLOCK, LANES), F32), pltpu.VMEM((1, BLOCK, LANES), F32)]
        else:
            specs += [mode.spec(attn_w, "cur"), mode.spec(LANES, "cur")]
        head_of_lane = jnp.arange(attn_w, dtype=jnp.int32) // HEAD_DIM
        ins.append((jnp.arange(LANES, dtype=jnp.int32)[:, None] == head_of_lane[None, :]).astype(BF16))
        specs.append(_const_spec((LANES, attn_w)))
    if nat:
        out_shape = (jax.ShapeDtypeStruct((seq, attn_w), F32), jax.ShapeDtypeStruct((seq, LANES), F32))
    else:
        out_shape = (jax.ShapeDtypeStruct(_perm_shape(seq, attn_w), F32), jax.ShapeDtypeStruct(_perm_shape(seq, LANES), F32))
    return pl.pallas_call(
        body, name=f"attn_fwd_{name}", grid=(mode.residues, mode.nb),
        out_shape=out_shape, in_specs=specs, out_specs=(mode.spec(attn_w, "cur"), mode.spec(LANES, "cur")),
        scratch_shapes=scratch,
        compiler_params=_params(("arbitrary", "arbitrary")),
    )(*ins)


def attn_bwd(name, q, k, v, d_o, lse, delta, run):
    nat = name == "nat"
    seq = q.shape[0] if nat else q.shape[2] * PERM
    attn_w = q.shape[-1]
    mode = _Mode(name, seq)
    nb = mode.nb
    groups = _lane_groups(attn_w)
    first = run is None
    all_lanes = slice(0, LANES)

    def body(*refs):
        q_ref, kp_ref, kc_ref, vp_ref, vc_ref, do_ref, lse_ref, dl_ref = refs[:8]
        if first:
            dq_ref, dk_ref, dv_ref, ck, cv = refs[8:]
        else:
            dqi_ref, dki_ref, dvi_ref, dq_ref, dk_ref, dv_ref, ck, cv = refs[8:]
        n = pl.program_id(1)

        @pl.when(n == 0)
        def _():
            ck[...] = jnp.zeros_like(ck)
            cv[...] = jnp.zeros_like(cv)

        @pl.when(n < nb)
        def _():
            bias = mode.bias(n, True)
            bias2 = jnp.concatenate([bias, bias], axis=1)
            _, lo, m_lo, m_hi = _head_masks()
            lse_t = jnp.transpose(mode.get(lse_ref, all_lanes))
            dl_t = jnp.transpose(mode.get(dl_ref, all_lanes))
            def scores(p, sl):
                q2, do2 = mode.get(q_ref, sl), mode.get(do_ref, sl)
                kcat = jnp.concatenate([mode.get(kp_ref, sl), mode.get(kc_ref, sl)], axis=0)
                vcat = jnp.concatenate([mode.get(vp_ref, sl), mode.get(vc_ref, sl)], axis=0)
                qq = jnp.concatenate([q2 * m_lo, q2 * m_hi], axis=0)
                dd = jnp.concatenate([do2 * m_lo, do2 * m_hi], axis=0)
                h0 = 2 * p
                lse2 = jnp.concatenate([lse_t[h0:h0 + 1, :], lse_t[h0 + 1:h0 + 2, :]], axis=1)
                dl2 = jnp.concatenate([dl_t[h0:h0 + 1, :], dl_t[h0 + 1:h0 + 2, :]], axis=1)
                p_t = jnp.exp(_nt(kcat, qq) + (bias2 - lse2))
                ds_t = p_t * (_nt(vcat, dd) - dl2)
                return qq, dd, kcat, p_t.astype(BF16), ds_t.astype(BF16)

            def grads(sl, qq, dd, kcat, pb, dsb):
                dkc = jnp.dot(dsb, qq, preferred_element_type=F32)
                dvc = jnp.dot(pb, dd, preferred_element_type=F32)
                dqb = _tn(dsb, kcat)
                dq2 = jnp.where(lo, dqb[:BLOCK], dqb[BLOCK:]) * ATTN_SCALE
                dk2 = ck[:, sl] + dkc[:BLOCK]
                dv2 = cv[:, sl] + dvc[:BLOCK]
                if not first:
                    dq2 = dq2 + mode.get(dqi_ref, sl)
                    dk2 = dk2 + mode.get(dki_ref, sl)
                    dv2 = dv2 + mode.get(dvi_ref, sl)
                mode.put(dq_ref, sl, dq2)
                mode.put(dk_ref, sl, dk2)
                mode.put(dv_ref, sl, dv2)
                ck[:, sl] = dkc[BLOCK:]
                cv[:, sl] = dvc[BLOCK:]

            pending = None
            for p, sl in enumerate(groups):
                nxt = scores(p, sl)
                if pending is not None:
                    grads(*pending)
                pending = (sl, *nxt)
            grads(*pending)

        @pl.when(n == nb)
        def _():
            for sl in groups:
                if first:
                    mode.put(dk_ref, sl, ck[:, sl])
                    mode.put(dv_ref, sl, cv[:, sl])
                else:
                    mode.put(dk_ref, sl, ck[:, sl] + mode.get(dki_ref, sl))
                    mode.put(dv_ref, sl, cv[:, sl] + mode.get(dvi_ref, sl))

    last = nb - 1
    cur = lambda w: mode.spec(w, "cur", last)
    prev = lambda w: mode.spec(w, "prev")
    ins = [q, k, k, v, v, d_o, lse, delta]
    specs = [cur(attn_w), prev(attn_w), cur(attn_w), prev(attn_w), cur(attn_w), cur(attn_w), cur(LANES), cur(LANES)]
    if not first:
        ins += list(run)
        specs += [cur(attn_w), prev(attn_w), prev(attn_w)]
    shp = jax.ShapeDtypeStruct((seq, attn_w) if nat else _perm_shape(seq, attn_w), F32)
    return pl.pallas_call(
        body, name=f"attn_bwd_{name}", grid=(mode.residues, nb + 1),
        out_shape=(shp, shp, shp), in_specs=specs, out_specs=(cur(attn_w), prev(attn_w), prev(attn_w)),
        scratch_shapes=[pltpu.VMEM((BLOCK, attn_w), F32), pltpu.VMEM((BLOCK, attn_w), F32)],
        compiler_params=_params(("arbitrary", "arbitrary")),
    )(*ins)


def _shift_down(u, halo, k):
    rolled = pltpu.roll(u, k, 0)
    row = lax.broadcasted_iota(jnp.int32, halo.shape, 0)
    top = jnp.where(row < k, pltpu.roll(halo, k, 0), rolled[:SUBLANES])
    return jnp.concatenate([top, rolled[SUBLANES:]], axis=0)


def _shift_up(u, halo, k):
    rows = u.shape[0]
    rolled = pltpu.roll(u, rows - k, 0)
    row = lax.broadcasted_iota(jnp.int32, halo.shape, 0)
    bot = jnp.where(row >= SUBLANES - k, pltpu.roll(halo, SUBLANES - k, 0), rolled[rows - SUBLANES:])
    return jnp.concatenate([rolled[:rows - SUBLANES], bot], axis=0)


def tail(o, lse, ga, cz, x, tgt, w_out, g2, cw):
    seq, d_model = x.shape
    attn_w = o.shape[1]
    conv_w = cz.shape[1] // 4
    mix = attn_w + conv_w
    groups = _lane_groups(attn_w)
    tm = ROW_TILE
    nt = seq // tm
    hb = tm // SUBLANES

    def body(o_ref, l_ref, ga_ref, cz_ref, hz_ref, x_ref, t_ref, w_ref, g_ref, cw_ref,
             do_ref, dl_ref, dop_ref, dlp_ref, lp_ref, dga_ref, dcb_ref, dgc_ref, dcv_ref, e_ref,
             dw_ref, dg_ref, dcw_ref, loss_ref, stage):
        i = pl.program_id(0)

        @pl.when(i == 0)
        def _():
            dw_ref[...] = jnp.zeros_like(dw_ref)
            dg_ref[...] = jnp.zeros_like(dg_ref)
            dcw_ref[...] = jnp.zeros_like(dcw_ref)
            loss_ref[...] = jnp.zeros_like(loss_ref)

        ov, gav = o_ref[...], ga_ref[...]
        sig_a = _sigmoid(gav)
        silu_a = gav * sig_a
        attn_out = ov * silu_a
        ch, cb = cz_ref[:, 0:conv_w], cz_ref[:, conv_w:2 * conv_w]
        cc, gc = cz_ref[:, 2 * conv_w:3 * conv_w], cz_ref[:, 3 * conv_w:4 * conv_w]
        u = cc * ch
        uh = hz_ref[:, 2 * conv_w:3 * conv_w] * hz_ref[:, 0:conv_w]
        uh = jnp.where(i > 0, uh, 0.0)
        u1 = _shift_down(u, uh, 1)
        u2 = _shift_down(u, uh, 2)
        w0, w1, w2 = cw_ref[0:1, :], cw_ref[1:2, :], cw_ref[2:3, :]
        cvv = u2 * w0 + u1 * w1 + u * w2
        sig_c = _sigmoid(gc)
        silu_c = gc * sig_c
        bc = cb * cvv
        conv_out = bc * silu_c
        mixed = jnp.concatenate([attn_out, conv_out], axis=1).astype(BF16)

        yv = jnp.dot(mixed, w_ref[...], preferred_element_type=F32)
        r2 = lax.rsqrt(jnp.mean(yv * yv, axis=-1, keepdims=True) + NORM_EPS)
        yhat = yv * r2
        gv = g_ref[...]
        diff = (x_ref[...] + yhat * gv) - t_ref[...]
        loss_ref[...] += _rowgroup_sum(diff * diff)
        ev = diff * (1.0 / d_model)
        e_ref[...] = ev
        dg_ref[...] += _rowgroup_sum(ev * yhat)
        eg = ev * gv
        dy = (r2 * (eg - yhat * jnp.mean(eg * yhat, axis=-1, keepdims=True))).astype(BF16)
        dw_ref[...] += _tn(mixed, dy)
        dm = _nt(dy, w_ref[...])
        dma, dmc = dm[:, :attn_w], dm[:, attn_w:]

        dov = dma * silu_a
        do_ref[...] = dov.astype(BF16)
        dga_ref[...] = (dma * ov * (sig_a * (1.0 + gav * (1.0 - sig_a)))).astype(BF16)
        prod = dov * ov
        lane = lax.broadcasted_iota(jnp.int32, (tm, LANES), 1)
        lo = lane < HEAD_DIM
        dblk = jnp.zeros((tm, LANES), F32)
        for p, sl in enumerate(groups):
            pr = prod[:, sl]
            dblk = jnp.where(lane == 2 * p, jnp.sum(jnp.where(lo, pr, 0.0), axis=1, keepdims=True), dblk)
            dblk = jnp.where(lane == 2 * p + 1, jnp.sum(jnp.where(lo, 0.0, pr), axis=1, keepdims=True), dblk)
            stage[p] = dov[:, sl]
        dl_ref[...] = dblk
        for p, sl in enumerate(groups):
            _to_perm(stage, p, dop_ref, sl, BF16)
        all_lanes = slice(0, LANES)
        stage[0] = dblk
        stage[1] = l_ref[...]
        _to_perm(stage, 0, dlp_ref, all_lanes, F32)
        _to_perm(stage, 1, lp_ref, all_lanes, F32)

        dsc = dmc * silu_c
        dcb_ref[...] = (dsc * cvv).astype(BF16)
        dgc_ref[...] = (dmc * bc * (sig_c * (1.0 + gc * (1.0 - sig_c)))).astype(BF16)
        dcv = dsc * cb
        dcv_ref[...] = dcv
        dcw_ref[0:SUBLANES, :] += _rowgroup_sum(dcv * u2)
        dcw_ref[SUBLANES:2 * SUBLANES, :] += _rowgroup_sum(dcv * u1)
        dcw_ref[2 * SUBLANES:3 * SUBLANES, :] += _rowgroup_sum(dcv * u)

    row = lambda n: pl.BlockSpec((tm, n), lambda i: (i, 0))
    whole = lambda a, b: pl.BlockSpec((a, b), lambda i: (0, 0))
    return pl.pallas_call(
        body, name="tail", grid=(nt,),
        out_shape=(jax.ShapeDtypeStruct((seq, attn_w), BF16), jax.ShapeDtypeStruct((seq, LANES), F32),
                   jax.ShapeDtypeStruct(_perm_shape(seq, attn_w), BF16), jax.ShapeDtypeStruct(_perm_shape(seq, LANES), F32),
                   jax.ShapeDtypeStruct(_perm_shape(seq, LANES), F32),
                   jax.ShapeDtypeStruct((seq, attn_w), BF16), jax.ShapeDtypeStruct((seq, conv_w), BF16),
                   jax.ShapeDtypeStruct((seq, conv_w), BF16), jax.ShapeDtypeStruct((seq, conv_w), F32),
                   jax.ShapeDtypeStruct((seq, d_model), F32), jax.ShapeDtypeStruct((mix, d_model), F32),
                   jax.ShapeDtypeStruct((SUBLANES, d_model), F32), jax.ShapeDtypeStruct((CONV_K * SUBLANES, conv_w), F32),
                   jax.ShapeDtypeStruct((SUBLANES, d_model), F32)),
        in_specs=[row(attn_w), row(LANES), row(attn_w), row(4 * conv_w),
                  pl.BlockSpec((SUBLANES, 4 * conv_w), lambda i: (jnp.maximum(i * hb - 1, 0), 0)),
                  row(d_model), row(d_model), _const_spec((mix, d_model)), _const_spec((1, d_model)),
                  _const_spec((SUBLANES, conv_w))],
        out_specs=(row(attn_w), row(LANES), _perm_tile_spec(attn_w, tm), _perm_tile_spec(LANES, tm), _perm_tile_spec(LANES, tm),
                   row(attn_w), row(conv_w), row(conv_w), row(conv_w), row(d_model),
                   whole(mix, d_model), whole(SUBLANES, d_model), whole(CONV_K * SUBLANES, conv_w),
                   whole(SUBLANES, d_model)),
        scratch_shapes=[pltpu.VMEM((max(len(groups), 2), tm, LANES), F32)],
        compiler_params=_params(("arbitrary",)),
    )(o, lse, ga, cz, cz, x, tgt, w_out, g2, cw)


def dz_dx(nat_grads, perm_grads, dga, dcb, dgc, dcv, cz, tables, x, g1, e, w_full, cw):
    seq, d_model = x.shape
    attn_w = dga.shape[1]
    conv_w = dcv.shape[1]
    width = w_full.shape[2]
    in_w = 4 * attn_w + 4 * conv_w
    groups = _lane_groups(attn_w)
    tm = DZ_ROW_TILE
    nt = seq // tm
    hb = tm // SUBLANES

    def body(dq_ref, dk_ref, dv_ref, dqp_ref, dkp_ref, dvp_ref, dga_ref, dcb_ref, dgc_ref, dcv_ref, nh_ref, cz_ref,
             cos_ref, s1_ref, s2_ref, x_ref, g_ref, e_ref, w_ref, cw_ref, gx_ref, dz_ref, dg_ref, stage):
        i = pl.program_id(0)

        @pl.when(i == 0)
        def _():
            dg_ref[...] = jnp.zeros_like(dg_ref)

        cos, s1, s2 = cos_ref[...], s1_ref[...], s2_ref[...]
        for t, (nat_ref, perm_ref) in enumerate(((dq_ref, dqp_ref), (dk_ref, dkp_ref), (dv_ref, dvp_ref))):
            for g, sl in enumerate(groups):
                _from_perm(perm_ref, sl, stage, g)
            for g, sl in enumerate(groups):
                tot = nat_ref[:, sl] + stage[g]
                if t < 2:
                    tot = _rope_transposed(tot, cos, s1, s2)
                dz_ref[:, t * attn_w + g * LANES:t * attn_w + (g + 1) * LANES] = tot.astype(BF16)
        dz_ref[:, 3 * attn_w:4 * attn_w] = dga_ref[...]
        dcv = dcv_ref[...]
        nh = jnp.where(i < nt - 1, nh_ref[...], 0.0)
        w0, w1, w2 = cw_ref[0:1, :], cw_ref[1:2, :], cw_ref[2:3, :]
        du = dcv * w2 + _shift_up(dcv, nh, 1) * w1 + _shift_up(dcv, nh, 2) * w0
        base = 4 * attn_w
        dz_ref[:, base:base + conv_w] = (du * cz_ref[:, 2 * conv_w:3 * conv_w]).astype(BF16)
        dz_ref[:, base + conv_w:base + 2 * conv_w] = dcb_ref[...]
        dz_ref[:, base + 2 * conv_w:base + 3 * conv_w] = (du * cz_ref[:, 0:conv_w]).astype(BF16)
        dz_ref[:, base + 3 * conv_w:base + 4 * conv_w] = dgc_ref[...]

        dh = _nt(dz_ref[:, 0:width], w_ref[0])
        for j in range(1, N_CHIPS):
            dh = dh + _nt(dz_ref[:, j * width:(j + 1) * width], w_ref[j])
        xv = x_ref[...]
        r1 = lax.rsqrt(jnp.mean(xv * xv, axis=-1, keepdims=True) + NORM_EPS)
        xhat = xv * r1
        dg_ref[...] += _rowgroup_sum(dh * xhat)
        dhg = dh * g_ref[...]
        gx_ref[...] = r1 * (dhg - xhat * jnp.mean(dhg * xhat, axis=-1, keepdims=True)) + e_ref[...]

    row = lambda n: pl.BlockSpec((tm, n), lambda i: (i, 0))
    whole = lambda a, b: pl.BlockSpec((a, b), lambda i: (0, 0))
    pt = _perm_tile_spec(attn_w, tm)
    return pl.pallas_call(
        body, name="dz_dx", grid=(nt,),
        out_shape=(jax.ShapeDtypeStruct((seq, d_model), F32), jax.ShapeDtypeStruct((seq, in_w), BF16),
                   jax.ShapeDtypeStruct((SUBLANES, d_model), F32)),
        in_specs=[row(attn_w), row(attn_w), row(attn_w), pt, pt, pt, row(attn_w), row(conv_w), row(conv_w), row(conv_w),
                  pl.BlockSpec((SUBLANES, conv_w), lambda i: (jnp.minimum((i + 1) * hb, seq // SUBLANES - 1), 0)),
                  row(4 * conv_w), row(LANES), row(LANES), row(LANES), row(d_model), _const_spec((1, d_model)), row(d_model),
                  _const_spec(w_full.shape), _const_spec((SUBLANES, conv_w))],
        out_specs=(row(d_model), row(in_w), whole(SUBLANES, d_model)),
        scratch_shapes=[pltpu.VMEM((len(groups), tm, LANES), F32)],
        compiler_params=_params(("arbitrary",)),
    )(*nat_grads, *perm_grads, dga, dcb, dgc, dcv, dcv, cz, *tables, x, g1, e, w_full, cw)


def dw_in(ht, dz):
    d_model, seq = ht.shape
    in_w = dz.shape[1]
    half = in_w // N_DEV
    ts = 512

    def body(ht_ref, dz_ref, o_ref):
        @pl.when(pl.program_id(0) == 0)
        def _():
            o_ref[...] = jnp.zeros_like(o_ref)

        for jb in range(N_DEV):
            o_ref[jb] += jnp.dot(ht_ref[...], dz_ref[:, jb * half:(jb + 1) * half], preferred_element_type=F32)

    return pl.pallas_call(
        body, name="dw_in", grid=(seq // ts,),
        out_shape=jax.ShapeDtypeStruct((N_DEV, d_model, half), F32),
        in_specs=[pl.BlockSpec((d_model, ts), lambda s: (0, s)), pl.BlockSpec((ts, in_w), lambda s: (s, 0))],
        out_specs=pl.BlockSpec((N_DEV, d_model, half), lambda s: (0, 0, 0)),
        compiler_params=_params(("arbitrary",)),
    )(ht, dz)


def grad_reduce(g_in, g_out, small):
    gi = g_in.reshape(N_CHIPS, 2, *g_in.shape[1:])
    go = g_out.reshape(N_CHIPS, 2, *g_out.shape[1:])
    shp_i, shp_o = gi.shape[2:], go.shape[2:]

    def body(gi_ref, go_ref, sm_ref, ri_ref, ro_ref, rs_ref, a_i, b_i, s_i, c_i, a_o, b_o, s_o, c_o, sbuf,
             loc_sems, sa, ra, sb, rb, sc, rc, ss, rs):
        x, y, c = lax.axis_index("x"), lax.axis_index("y"), lax.axis_index("c")
        me = 2 * x + y
        sib = (x, y, 1 - c)
        srcs, mine, theirs, staged, contrib, res = ((gi_ref, go_ref), (a_i, a_o), (b_i, b_o), (s_i, s_o), (c_i, c_o),
                                                    (ri_ref, ro_ref))

        flips = [(fx, fy, fc) for fx in (0, 1) for fy in (0, 1) for fc in (0, 1)][1:]
        my8 = 4 * x + 2 * y + c
        sbuf[my8] = sm_ref[...]

        def small_copy(k, slot, to):
            return pltpu.make_async_remote_copy(src_ref=sm_ref, dst_ref=sbuf.at[slot], send_sem=ss.at[k], recv_sem=rs.at[k],
                                                device_id=to, device_id_type=MESH)

        sends = []
        for k, (fx, fy, fc) in enumerate(flips):
            px, py, pc = _flip(x, fx), _flip(y, fy), _flip(c, fc)
            sends.append(small_copy(k, my8, (px, py, pc)))
            sends[-1].start()

        def a_copy(t, j):
            return pltpu.make_async_remote_copy(src_ref=srcs[t].at[j, 1 - c], dst_ref=theirs[t].at[j], send_sem=sa.at[t, j],
                                                recv_sem=ra.at[t, j], device_id=sib, device_id_type=MESH)

        peers = _chip_peers(x, y)
        order = [2 * px + py for px, py in peers] + [me]
        loads = [[pltpu.make_async_copy(srcs[t].at[j, c], mine[t].at[j], loc_sems.at[t, j]) for t in range(2)] for j in order]
        for pos, j in enumerate(order):
            for t in range(2):
                loads[pos][t].start()
                sends.append(a_copy(t, j))
                sends[-1].start()

        def b_copy(k, t, piece, slot, to):
            return pltpu.make_async_remote_copy(src_ref=staged[t].at[piece], dst_ref=contrib[t].at[slot], send_sem=sb.at[k, t],
                                                recv_sem=rb.at[k, t], device_id=to, device_id_type=MESH)

        for k, (px, py) in enumerate(peers):
            j = 2 * px + py
            for t in range(2):
                loads[k][t].wait()
                a_copy(t, j).wait_recv()
                staged[t][j] = (mine[t][j] + theirs[t][j]).astype(BF16)
                sends.append(b_copy(k, t, j, me, (px, py, c)))
                sends[-1].start()
        for t in range(2):
            loads[len(peers)][t].wait()
            a_copy(t, me).wait_recv()
            mine[t][me] = mine[t][me] + theirs[t][me]
            contrib[t][me] = mine[t][me].astype(BF16)
        for k, (px, py) in enumerate(peers):
            for t in range(2):
                b_copy(k, t, me, 2 * px + py, (px, py, c)).wait_recv()

        def c_copy(t, half):
            return pltpu.make_async_remote_copy(src_ref=res[t].at[half], dst_ref=res[t].at[half], send_sem=sc.at[t],
                                                recv_sem=rc.at[t], device_id=sib, device_id_type=MESH)

        for t in range(2):
            own = mine[t][me]
            term = lambda j: jnp.where(me == j, own, contrib[t][j].astype(F32))
            res[t][c] = ((term(0) + term(1)) + term(2)) + term(3)
            sends.append(c_copy(t, c))
            sends[-1].start()
        for t in range(2):
            c_copy(t, 1 - c).wait_recv()

        for k, (fx, fy, fc) in enumerate(flips):
            px, py, pc = _flip(x, fx), _flip(y, fy), _flip(c, fc)
            small_copy(k, 4 * px + 2 * py + pc, (px, py, pc)).wait_recv()
        tot = sbuf[0]
        for d in range(1, N_DEV):
            tot = tot + sbuf[d]
        rs_ref[...] = tot
        for cp in sends:
            cp.wait_send()

    vm = pl.BlockSpec(memory_space=pltpu.VMEM)
    anyspace = pl.BlockSpec(memory_space=pl.ANY)
    dma = pltpu.SemaphoreType.DMA
    bufs = lambda shp: [pltpu.VMEM((N_CHIPS, *shp), F32), pltpu.VMEM((N_CHIPS, *shp), F32),
                        pltpu.VMEM((N_CHIPS, *shp), BF16), pltpu.VMEM((N_CHIPS, *shp), BF16)]
    return pl.pallas_call(
        body, name="grad_reduce",
        out_shape=(jax.ShapeDtypeStruct((2, *shp_i), F32), jax.ShapeDtypeStruct((2, *shp_o), F32),
                   jax.ShapeDtypeStruct(small.shape, F32)),
        in_specs=[anyspace, anyspace, vm], out_specs=(vm, vm, vm),
        scratch_shapes=[*bufs(shp_i), *bufs(shp_o), pltpu.VMEM((N_DEV, *small.shape), F32),
                        dma((2, N_CHIPS)), dma((2, N_CHIPS)), dma((2, N_CHIPS)), dma((3, 2)), dma((3, 2)), dma((2,)), dma((2,)),
                        dma((N_DEV - 1,)), dma((N_DEV - 1,))],
        compiler_params=_params(),
    )(gi, go, small)


def _adam_math(w, g, m, v):
    m = ADAM_B1 * m + (1.0 - ADAM_B1) * g
    v = ADAM_B2 * v + (1.0 - ADAM_B2) * (g * g)
    m_hat = m / (1.0 - ADAM_B1 ** ADAM_STEP)
    v_hat = v / (1.0 - ADAM_B2 ** ADAM_STEP)
    delta = -ADAM_LR * (m_hat / (jnp.sqrt(v_hat) + ADAM_EPS) + ADAM_WD * w)
    return delta, m, v


def adam_shard(name, w, g2, m, v, block, grid, w_map, g_map):
    def body(w_ref, g_ref, m_ref, v_ref, go_ref, d_ref, mo_ref, vo_ref):
        g = g_ref[0]
        delta, mn, vn = _adam_math(w_ref[...], g, m_ref[...], v_ref[...])
        go_ref[...] = g
        d_ref[...] = delta
        mo_ref[...] = mn
        vo_ref[...] = vn

    ws = pl.BlockSpec(block, w_map)
    shp = jax.ShapeDtypeStruct(w.shape, F32)
    return pl.pallas_call(
        body, name=name, grid=grid, out_shape=(shp, shp, shp, shp),
        in_specs=[ws, pl.BlockSpec((1, *block), g_map), ws, ws], out_specs=(ws, ws, ws, ws),
        compiler_params=_params(("arbitrary",) * len(grid)),
    )(w, g2, m, v)


def adam_small(ws, gs, ms, vs):
    n = len(ws)

    def body(*refs):
        ins, outs = refs[:4 * n], refs[4 * n:]
        for t in range(n):
            delta, mn, vn = _adam_math(ins[t][...], ins[n + t][...], ins[2 * n + t][...], ins[3 * n + t][...])
            outs[3 * t][...] = delta
            outs[3 * t + 1][...] = mn
            outs[3 * t + 2][...] = vn

    vm = pl.BlockSpec(memory_space=pltpu.VMEM)
    outs = pl.pallas_call(
        body, name="adam_small",
        out_shape=tuple(jax.ShapeDtypeStruct(w.shape, F32) for w in ws for _ in range(3)),
        in_specs=[vm] * (4 * n), out_specs=tuple([vm] * (3 * n)),
        compiler_params=_params(),
    )(*ws, *gs, *ms, *vs)
    return [outs[3 * t:3 * t + 3] for t in range(n)]


def kernel(x, norm_pre_g, w_in, conv_w, w_out, norm_post_g, loss_target, m_norm_pre_g, m_w_in, m_conv_w, m_w_out, m_norm_post_g, v_norm_pre_g, v_w_in, v_conv_w, v_w_out, v_norm_post_g):
    _, seq, d_model = x.shape
    width = w_in.shape[1]
    conv_q = conv_w.shape[1]
    conv_width = N_CHIPS * conv_q
    attn_width = d_model - conv_width
    xs, tg = x[0], loss_target[0]
    g1, g2 = norm_pre_g.reshape(1, d_model), norm_post_g.reshape(1, d_model)

    w_full, wout_full, cw_full = gather_weights(w_in, w_out, conv_w)
    wout2 = wout_full.reshape(attn_width + conv_width, d_model)
    cw = jnp.zeros((SUBLANES, conv_width), F32).at[:CONV_K].set(
        cw_full[:, :CONV_K, :conv_q].transpose(1, 0, 2).reshape(CONV_K, conv_width))
    tables = _rope_tables(seq)

    ht, q, k, v, qp, kp, vp, ga, cz = inproj(xs, g1, w_full, tables, attn_width, conv_width)
    run = attn_fwd("p4", qp, kp, vp, None)
    run = attn_fwd("p16", qp, kp, vp, run)
    o, lse = attn_fwd("nat", q, k, v, run)
    (d_o, delta, d_op, delta_p, lse_p, dga, dcb, dgc, dcv, e, dwout, dg2, dcw, loss_acc) = tail(
        o, lse, ga, cz, xs, tg, wout2, g2, cw)
    nat_grads = attn_bwd("nat", q, k, v, d_o, lse, delta, None)
    perm_grads = attn_bwd("p4", qp, kp, vp, d_op, lse_p, delta_p, None)
    perm_grads = attn_bwd("p16", qp, kp, vp, d_op, lse_p, delta_p, perm_grads)
    grad_x, dz, dg1 = dz_dx(nat_grads, perm_grads, dga, dcb, dgc, dcv, cz, tables, xs, g1, e, w_full, cw)
    dwin = dw_in(ht, dz)

    small = jnp.zeros((SUBLANES, d_model), F32)
    small = small.at[0].set(dg1.sum(axis=0)).at[1].set(dg2.sum(axis=0))
    small = small.at[2:2 + CONV_K, :conv_width].set(dcw.reshape(CONV_K, SUBLANES, conv_width).sum(axis=1))
    rin, rout, rsmall = grad_reduce(dwin, dwout.reshape(N_DEV, -1, d_model), small)

    half = width // 2
    tr = 256
    gw_in, d_in, m_in, v_in = adam_shard(
        "adam_w_in", w_in, rin, m_w_in, v_w_in, (tr, half), (2, d_model // tr),
        lambda hf, i: (i, hf), lambda hf, i: (hf, i, 0))
    rq = w_out.shape[0] // 2
    gw_out, d_out, m_out, v_out = adam_shard(
        "adam_w_out", w_out, rout, m_w_out, v_w_out, (rq, d_model), (2,),
        lambda hf: (hf, 0), lambda hf: (hf, 0, 0))

    chip = 2 * lax.axis_index("x") + lax.axis_index("y")
    g_pre, g_post = rsmall[0:1], rsmall[1:2]
    g_conv = lax.dynamic_slice(rsmall[2:2 + CONV_K, :conv_width], (0, chip * conv_q), (CONV_K, conv_q))
    (d_pre, m_pre, v_pre), (d_post, m_post, v_post), (d_cv, m_cv, v_cv) = adam_small(
        [g1, g2, conv_w], [g_pre, g_post, g_conv],
        [m_norm_pre_g.reshape(1, d_model), m_norm_post_g.reshape(1, d_model), m_conv_w],
        [v_norm_pre_g.reshape(1, d_model), v_norm_post_g.reshape(1, d_model), v_conv_w])

    loss = lax.psum(0.5 * jnp.sum(loss_acc) / d_model, ("x", "y", "c"))
    vec = lambda a: a.reshape(d_model)
    return (loss, grad_x.reshape(1, seq, d_model),
            vec(g_pre), gw_in, g_conv, gw_out, vec(g_post),
            vec(d_pre), d_in, d_cv, d_out, vec(d_post),
            vec(m_pre), m_in, m_cv, m_out, vec(m_post),
            vec(v_pre), v_in, v_cv, v_out, vec(v_post))
```

```python
import jax
import jax.numpy as jnp
from jax import lax
from jax.experimental import pallas as pl
from jax.experimental.pallas import tpu as pltpu

HEAD_DIM = 64
LANES = 128
SUBLANES = 8
BLOCK = 128
WINDOW_KEYS = 128
PERM = 16
PJ = 4
P4_ROWS = BLOCK // PJ
ROW_TILE = 512
DZ_ROW_TILE = 256
CONV_K = 3
ROPE_THETA = 10000.0
NORM_EPS = 1e-6
ATTN_SCALE = HEAD_DIM ** -0.5
NEG = -1e30
N_CHIPS = 4
N_DEV = 8
MESH = pl.DeviceIdType.MESH
ADAM_LR = 0.001
ADAM_B1 = 0.9
ADAM_B2 = 0.999
ADAM_EPS = 1e-08
ADAM_WD = 0.01
ADAM_STEP = 10
VMEM_LIMIT = 52 * 1024 * 1024

F32 = jnp.float32
BF16 = jnp.bfloat16


def _params(sem=None, **kw):
    return pltpu.CompilerParams(dimension_semantics=sem, vmem_limit_bytes=VMEM_LIMIT, **kw)


def _const_spec(shape):
    return pl.BlockSpec(shape, lambda *_: (0,) * len(shape), pipeline_mode=pl.Buffered(1))


def _sigmoid(z):
    return 1.0 / (1.0 + jnp.exp(-z))


def _rowgroup_sum(a):
    rows, n = a.shape
    return a.reshape(rows // SUBLANES, SUBLANES, n).sum(axis=0)


def _nt(a, b):
    return lax.dot_general(a, b, (((1,), (1,)), ((), ())), preferred_element_type=F32)


def _tn(a, b):
    return lax.dot_general(a, b, (((0,), (0,)), ((), ())), preferred_element_type=F32)


def _col_pieces(a, b, width):
    out = []
    while a < b:
        j = a // width
        e = min(b, (j + 1) * width)
        out.append((j, a - j * width, e - j * width))
        a = e
    return out


def _lane_groups(width):
    return [slice(g * LANES, (g + 1) * LANES) for g in range(width // LANES)]


def _perm_shape(seq, width):
    return (PJ, PJ, seq // PERM, width)


def _perm_tile_spec(width, tm):
    return pl.BlockSpec((PJ, PJ, tm // PERM, width), lambda i: (0, 0, i, 0))


STAGE_PITCH = 24


def _stage_shape(groups, rows):
    return (groups, rows // PERM * STAGE_PITCH, LANES)


def _stage_put(stage, g, val):
    for a in range(val.shape[0] // PERM):
        stage[g, a * STAGE_PITCH:a * STAGE_PITCH + PERM, :] = val[a * PERM:(a + 1) * PERM]


def _stage_get(stage, g):
    return jnp.concatenate([stage[g, a * STAGE_PITCH:a * STAGE_PITCH + PERM, :]
                            for a in range(stage.shape[1] // STAGE_PITCH)], axis=0)


def _to_perm(stage, g, dst_ref, sl, dtype):
    rows = stage.shape[1] // STAGE_PITCH
    for b in range(PERM):
        dst_ref[b // PJ, b % PJ, :, sl] = stage[g, pl.ds(b, rows, stride=STAGE_PITCH), :].astype(dtype)


def _from_perm(src_ref, sl, stage, g):
    rows = stage.shape[1] // STAGE_PITCH
    for b in range(PERM):
        stage[g, pl.ds(b, rows, stride=STAGE_PITCH), :] = src_ref[b // PJ, b % PJ, :, sl]


def _flip(a, f):
    return 1 - a if f else a


def _chip_peers(x, y):
    return [(1 - x, y), (x, 1 - y), (1 - x, 1 - y)]


def gather_weights(w_in, w_out, conv_w):
    d_model, width = w_in.shape
    rows = w_out.shape[0]
    cw = jnp.zeros((SUBLANES, LANES), F32).at[:CONV_K, :conv_w.shape[1]].set(conv_w)

    def body(win_ref, wout_ref, cw_ref, winf_ref, woutf_ref, cwf_ref, st_in, st_out, ici_send, ici_recv, d2d_send, d2d_recv):
        x, y, c = lax.axis_index("x"), lax.axis_index("y"), lax.axis_index("c")
        me = 2 * x + y
        sib = (x, y, 1 - c)
        st_in[...] = win_ref[...].astype(BF16)
        st_out[...] = wout_ref[...].astype(BF16)
        winf_ref[me] = st_in[...]
        woutf_ref[me] = st_out[...]
        cwf_ref[me] = cw_ref[...]
        stages = (st_in, st_out)
        fulls = (winf_ref, woutf_ref)
        halves = (d_model // 2, rows // 2)

        def half(t, core):
            return pl.ds(pl.multiple_of(core * halves[t], halves[t]), halves[t])

        def ici(k, t, slot, to, core):
            src = stages[t].at[half(t, core)] if t < 2 else cw_ref
            dst = fulls[t].at[slot, half(t, core)] if t < 2 else cwf_ref.at[slot]
            return pltpu.make_async_remote_copy(src_ref=src, dst_ref=dst, send_sem=ici_send.at[k, t], recv_sem=ici_recv.at[k, t],
                                                device_id=to, device_id_type=MESH)

        def d2d(k, t, slot, core):
            ref = fulls[t].at[slot, half(t, core)]
            return pltpu.make_async_remote_copy(src_ref=ref, dst_ref=ref, send_sem=d2d_send.at[k, t], recv_sem=d2d_recv.at[k, t],
                                                device_id=sib, device_id_type=MESH)

        peers = _chip_peers(x, y)
        sends = [ici(k, t, me, (px, py, c), c) for k, (px, py) in enumerate(peers) for t in range(3)]
        for cp in sends:
            cp.start()
        for k, (px, py) in enumerate(peers):
            for t in range(2):
                ici(k, t, 2 * px + py, (px, py, c), c).wait_recv()
                fwd = d2d(k, t, 2 * px + py, c)
                fwd.start()
                sends.append(fwd)
            ici(k, 2, 2 * px + py, (px, py, c), c).wait_recv()
        for k, (px, py) in enumerate(peers):
            for t in range(2):
                d2d(k, t, 2 * px + py, 1 - c).wait_recv()
        for cp in sends:
            cp.wait_send()

    vm = pl.BlockSpec(memory_space=pltpu.VMEM)
    dma = pltpu.SemaphoreType.DMA
    return pl.pallas_call(
        body, name="gather_weights",
        out_shape=(jax.ShapeDtypeStruct((N_CHIPS, d_model, width), BF16),
                   jax.ShapeDtypeStruct((N_CHIPS, rows, d_model), BF16),
                   jax.ShapeDtypeStruct((N_CHIPS, SUBLANES, LANES), F32)),
        in_specs=[vm, vm, vm], out_specs=(vm, vm, vm),
        scratch_shapes=[pltpu.VMEM((d_model, width), BF16), pltpu.VMEM((rows, d_model), BF16),
                        dma((3, 3)), dma((3, 3)), dma((3, 2)), dma((3, 2))],
        compiler_params=_params(),
    )(w_in, w_out, cw)


def _rope_tables(seq):
    half = HEAD_DIM // 2
    inv_freq = ROPE_THETA ** (-jnp.arange(half, dtype=F32) * 2.0 / HEAD_DIM)
    ang = jnp.arange(seq).astype(F32)[:, None] * jnp.tile(inv_freq, LANES // half)[None, :]
    first_half = (jnp.arange(LANES) % HEAD_DIM < half)[None, :]
    sin = jnp.sin(ang)
    return jnp.cos(ang), jnp.where(first_half, -sin, 0.0), jnp.where(first_half, 0.0, sin)


def _rope(t, cos, s1, s2):
    return t * cos + pltpu.roll(t, LANES - HEAD_DIM // 2, 1) * s1 + pltpu.roll(t, HEAD_DIM // 2, 1) * s2


def _rope_transposed(g, cos, s1, s2):
    return g * cos + pltpu.roll(g * s1, HEAD_DIM // 2, 1) + pltpu.roll(g * s2, LANES - HEAD_DIM // 2, 1)


def inproj(x, g1, w_full, tables, attn_w, conv_w):
    seq, d_model = x.shape
    width = w_full.shape[2]
    tm = ROW_TILE
    groups = _lane_groups(attn_w)

    def body(x_ref, g_ref, w_ref, cos_ref, s1_ref, s2_ref,
             ht_ref, q_ref, k_ref, v_ref, qp_ref, kp_ref, vp_ref, ga_ref, cz_ref, stage):
        xv = x_ref[...]
        hb = ((xv * lax.rsqrt(jnp.mean(xv * xv, axis=-1, keepdims=True) + NORM_EPS)) * g_ref[...]).astype(BF16)
        ht_ref[...] = jnp.transpose(hb)
        cos, s1, s2 = cos_ref[...], s1_ref[...], s2_ref[...]

        def proj(a, b):
            parts = [jnp.dot(hb, w_ref[j, :, lo:hi], preferred_element_type=F32) for j, lo, hi in _col_pieces(a, b, width)]
            return parts[0] if len(parts) == 1 else jnp.concatenate(parts, axis=1)

        def emit(z, nat_ref, perm_ref, fn):
            for g, sl in enumerate(groups):
                val = fn(z[:, sl])
                nat_ref[:, sl] = val.astype(BF16)
                _stage_put(stage, g, val)
            for g, sl in enumerate(groups):
                _to_perm(stage, g, perm_ref, sl, BF16)

        emit(proj(0, attn_w), q_ref, qp_ref, lambda t: _rope(t, cos, s1, s2) * ATTN_SCALE)
        emit(proj(attn_w, 2 * attn_w), k_ref, kp_ref, lambda t: _rope(t, cos, s1, s2))
        emit(proj(2 * attn_w, 3 * attn_w), v_ref, vp_ref, lambda t: t)
        ga_ref[...] = proj(3 * attn_w, 4 * attn_w)
        cz_ref[...] = proj(4 * attn_w, 4 * attn_w + 4 * conv_w)

    row = lambda n: pl.BlockSpec((tm, n), lambda i: (i, 0))
    nat = jax.ShapeDtypeStruct((seq, attn_w), BF16)
    perm = jax.ShapeDtypeStruct(_perm_shape(seq, attn_w), BF16)
    return pl.pallas_call(
        body, name="inproj", grid=(seq // tm,),
        out_shape=(jax.ShapeDtypeStruct((d_model, seq), BF16), nat, nat, nat, perm, perm, perm,
                   jax.ShapeDtypeStruct((seq, attn_w), F32), jax.ShapeDtypeStruct((seq, 4 * conv_w), F32)),
        in_specs=[row(d_model), _const_spec((1, d_model)), _const_spec(w_full.shape), row(LANES), row(LANES), row(LANES)],
        out_specs=(pl.BlockSpec((d_model, tm), lambda i: (0, i)), row(attn_w), row(attn_w), row(attn_w),
                   _perm_tile_spec(attn_w, tm), _perm_tile_spec(attn_w, tm), _perm_tile_spec(attn_w, tm),
                   row(attn_w), row(4 * conv_w)),
        scratch_shapes=[pltpu.VMEM(_stage_shape(len(groups), tm), F32)],
        compiler_params=_params(("arbitrary",)),
    )(x, g1, w_full, *tables)


class _Mode:
    def __init__(self, name, seq):
        self.name = name
        if name == "nat":
            self.residues, self.nb = 1, seq // BLOCK
        elif name == "p16":
            self.residues, self.nb = PERM, seq // PERM // BLOCK
        else:
            self.residues, self.nb = PJ, seq // PERM // P4_ROWS

    def spec(self, width, which, last=None):
        if which == "prev":
            blk = lambda n: jnp.maximum(n - 1, 0)
        elif last is None:
            blk = lambda n: n
        else:
            blk = lambda n: jnp.minimum(n, last)
        if self.name == "nat":
            return pl.BlockSpec((BLOCK, width), lambda r, n: (blk(n), 0))
        if self.name == "p16":
            return pl.BlockSpec((1, 1, BLOCK, width), lambda r, n: (r // PJ, r % PJ, blk(n), 0))
        return pl.BlockSpec((PJ, 1, P4_ROWS, width), lambda r, n: (0, r, blk(n), 0))

    def get(self, ref, sl):
        if self.name == "nat":
            return ref[:, sl]
        if self.name == "p16":
            return ref[0, 0, :, sl]
        return jnp.concatenate([ref[j, 0, :, sl] for j in range(PJ)], axis=0)

    def put(self, ref, sl, val):
        if self.name == "nat":
            ref[:, sl] = val
        elif self.name == "p16":
            ref[0, 0, :, sl] = val
        else:
            for j in range(PJ):
                ref[j, 0, :, sl] = val[j * P4_ROWS:(j + 1) * P4_ROWS]

    def index(self, idx, is_key):
        if self.name != "p4":
            return idx - BLOCK if is_key else idx
        within = jnp.bitwise_and(idx, BLOCK - 1)
        m = PJ * jnp.bitwise_and(within, P4_ROWS - 1) + jnp.right_shift(within, P4_ROWS.bit_length() - 1)
        return m + BLOCK * (jnp.right_shift(idx, BLOCK.bit_length() - 1) - 1) if is_key else m

    def bias(self, n, keys_major):
        shape = (2 * BLOCK, BLOCK) if keys_major else (BLOCK, 2 * BLOCK)
        kdim = 0 if keys_major else 1
        kidx = lax.broadcasted_iota(jnp.int32, shape, kdim)
        qidx = lax.broadcasted_iota(jnp.int32, shape, 1 - kdim)
        rel = self.index(qidx, False) - self.index(kidx, True)
        valid = (rel >= 0) & (rel <= WINDOW_KEYS) & ((kidx >= BLOCK) | (n > 0))
        return jnp.where(valid, 0.0, NEG)


def _head_masks():
    lane = lax.broadcasted_iota(jnp.int32, (BLOCK, LANES), 1)
    lo = lane < HEAD_DIM
    return lane, lo, jnp.where(lo, 1.0, 0.0).astype(BF16), jnp.where(lo, 0.0, 1.0).astype(BF16)


def _column(blk, lane, h):
    return jnp.sum(jnp.where(lane == h, blk, 0.0), axis=1, keepdims=True)


def attn_fwd(name, q, k, v, run):
    nat = name == "nat"
    seq = q.shape[0] if nat else q.shape[2] * PERM
    attn_w = q.shape[-1]
    mode = _Mode(name, seq)
    groups = _lane_groups(attn_w)
    first = run is None
    all_lanes = slice(0, LANES)

    def body(*refs):
        q_ref, kp_ref, kc_ref, vp_ref, vc_ref = refs[:5]
        if first:
            o_ref, l_ref = refs[5:]
        elif nat:
            oin_ref, lin_ref, ex_ref, o_ref, l_ref, ostage, lstage = refs[5:]
        else:
            oin_ref, lin_ref, ex_ref, o_ref, l_ref = refs[5:]
        n = pl.program_id(1)
        bias = mode.bias(n, True)
        bias2 = jnp.concatenate([bias, bias], axis=1)
        _, lo, m_lo, m_hi = _head_masks()
        head_row = lax.broadcasted_iota(jnp.int32, (BLOCK, LANES), 0)
        lrows = jnp.zeros((BLOCK, LANES), F32)
        def probs(sl):
            q2 = mode.get(q_ref, sl)
            kcat = jnp.concatenate([mode.get(kp_ref, sl), mode.get(kc_ref, sl)], axis=0)
            vcat = jnp.concatenate([mode.get(vp_ref, sl), mode.get(vc_ref, sl)], axis=0)
            qq = jnp.concatenate([q2 * m_lo, q2 * m_hi], axis=0)
            s_t = _nt(kcat, qq) + bias2
            m = jnp.max(s_t, axis=0, keepdims=True)
            pe = jnp.exp(s_t - m)
            l = jnp.sum(pe, axis=0, keepdims=True)
            return vcat, (pe * (1.0 / l)).astype(BF16), m + jnp.log(l)

        def output(p, sl, vcat, pn, lse, lrows):
            o_new = _tn(pn, vcat)
            mode.put(o_ref, sl, jnp.where(lo, o_new[:BLOCK], o_new[BLOCK:]))
            lrows = jnp.where(head_row == 2 * p, lse[:, :BLOCK], lrows)
            return jnp.where(head_row == 2 * p + 1, lse[:, BLOCK:], lrows)

        pending = None
        for p, sl in enumerate(groups):
            nxt = probs(sl)
            if pending is not None:
                lrows = output(*pending, lrows)
            pending = (p, sl, *nxt)
        lrows = output(*pending, lrows)
        lblk = jnp.transpose(lrows)
        if first:
            mode.put(l_ref, all_lanes, lblk)
        else:
            if nat:
                for g, sl in enumerate(groups):
                    _from_perm(oin_ref, sl, ostage, g)
                _from_perm(lin_ref, all_lanes, lstage, 0)
                lin = _stage_get(lstage, 0)
            else:
                lin = mode.get(lin_ref, all_lanes)
            mx = jnp.maximum(lin, lblk)
            new = mx + jnp.log(jnp.exp(lin - mx) + jnp.exp(lblk - mx))
            mode.put(l_ref, all_lanes, new)

            def expand(w):
                hi = w.astype(BF16)
                rest = (w - hi.astype(F32)).astype(BF16)
                ex = ex_ref[...]
                return jnp.dot(hi, ex, preferred_element_type=F32) + jnp.dot(rest, ex, preferred_element_type=F32)

            w_prev, w_cur = expand(jnp.exp(lin - new)), expand(jnp.exp(lblk - new))
            for p, sl in enumerate(groups):
                o_prev = _stage_get(ostage, p) if nat else mode.get(oin_ref, sl)
                mode.put(o_ref, sl, w_prev[:, sl] * o_prev + w_cur[:, sl] * mode.get(o_ref, sl))

    ins = [q, k, k, v, v]
    specs = [mode.spec(attn_w, "cur"), mode.spec(attn_w, "prev"), mode.spec(attn_w, "cur"),
             mode.spec(attn_w, "prev"), mode.spec(attn_w, "cur")]
    scratch = []
    if not first:
        ins += list(run)
        if nat:
            rows8 = BLOCK // PERM
            specs += [pl.BlockSpec((PJ, PJ, rows8, attn_w), lambda r, n: (0, 0, n, 0)),
                      pl.BlockSpec((PJ, PJ, rows8, LANES), lambda r, n: (0, 0, n, 0))]
            scratch = [pltpu.VMEM(_stage_shape(len(groups), BLOCK), F32), pltpu.VMEM(_stage_shape(1, BLOCK), F32)]
        else:
            specs += [mode.spec(attn_w, "cur"), mode.spec(LANES, "cur")]
        head_of_lane = jnp.arange(attn_w, dtype=jnp.int32) // HEAD_DIM
        ins.append((jnp.arange(LANES, dtype=jnp.int32)[:, None] == head_of_lane[None, :]).astype(BF16))
        specs.append(_const_spec((LANES, attn_w)))
    if nat:
        out_shape = (jax.ShapeDtypeStruct((seq, attn_w), F32), jax.ShapeDtypeStruct((seq, LANES), F32))
    else:
        out_shape = (jax.ShapeDtypeStruct(_perm_shape(seq, attn_w), F32), jax.ShapeDtypeStruct(_perm_shape(seq, LANES), F32))
    return pl.pallas_call(
        body, name=f"attn_fwd_{name}", grid=(mode.residues, mode.nb),
        out_shape=out_shape, in_specs=specs, out_specs=(mode.spec(attn_w, "cur"), mode.spec(LANES, "cur")),
        scratch_shapes=scratch,
        compiler_params=_params(("arbitrary", "arbitrary")),
    )(*ins)


def attn_bwd(name, q, k, v, d_o, lse, delta, run):
    nat = name == "nat"
    seq = q.shape[0] if nat else q.shape[2] * PERM
    attn_w = q.shape[-1]
    mode = _Mode(name, seq)
    nb = mode.nb
    groups = _lane_groups(attn_w)
    first = run is None
    all_lanes = slice(0, LANES)

    def body(*refs):
        q_ref, kp_ref, kc_ref, vp_ref, vc_ref, do_ref, lse_ref, dl_ref = refs[:8]
        if first:
            dq_ref, dk_ref, dv_ref, ck, cv = refs[8:]
        else:
            dqi_ref, dki_ref, dvi_ref, dq_ref, dk_ref, dv_ref, ck, cv = refs[8:]
        n = pl.program_id(1)

        @pl.when(n == 0)
        def _():
            ck[...] = jnp.zeros_like(ck)
            cv[...] = jnp.zeros_like(cv)

        @pl.when(n < nb)
        def _():
            bias = mode.bias(n, True)
            bias2 = jnp.concatenate([bias, bias], axis=1)
            _, lo, m_lo, m_hi = _head_masks()
            lse_t = jnp.transpose(mode.get(lse_ref, all_lanes))
            dl_t = jnp.transpose(mode.get(dl_ref, all_lanes))
            def scores(p, sl):
                q2, do2 = mode.get(q_ref, sl), mode.get(do_ref, sl)
                kcat = jnp.concatenate([mode.get(kp_ref, sl), mode.get(kc_ref, sl)], axis=0)
                vcat = jnp.concatenate([mode.get(vp_ref, sl), mode.get(vc_ref, sl)], axis=0)
                qq = jnp.concatenate([q2 * m_lo, q2 * m_hi], axis=0)
                dd = jnp.concatenate([do2 * m_lo, do2 * m_hi], axis=0)
                h0 = 2 * p
                lse2 = jnp.concatenate([lse_t[h0:h0 + 1, :], lse_t[h0 + 1:h0 + 2, :]], axis=1)
                dl2 = jnp.concatenate([dl_t[h0:h0 + 1, :], dl_t[h0 + 1:h0 + 2, :]], axis=1)
                p_t = jnp.exp(_nt(kcat, qq) + (bias2 - lse2))
                ds_t = p_t * (_nt(vcat, dd) - dl2)
                return qq, dd, kcat, p_t.astype(BF16), ds_t.astype(BF16)

            def grads(sl, qq, dd, kcat, pb, dsb):
                dkc = jnp.dot(dsb, qq, preferred_element_type=F32)
                dvc = jnp.dot(pb, dd, preferred_element_type=F32)
                dqb = _tn(dsb, kcat)
                dq2 = jnp.where(lo, dqb[:BLOCK], dqb[BLOCK:]) * ATTN_SCALE
                dk2 = ck[:, sl] + dkc[:BLOCK]
                dv2 = cv[:, sl] + dvc[:BLOCK]
                if not first:
                    dq2 = dq2 + mode.get(dqi_ref, sl)
                    dk2 = dk2 + mode.get(dki_ref, sl)
                    dv2 = dv2 + mode.get(dvi_ref, sl)
                mode.put(dq_ref, sl, dq2)
                mode.put(dk_ref, sl, dk2)
                mode.put(dv_ref, sl, dv2)
                ck[:, sl] = dkc[BLOCK:]
                cv[:, sl] = dvc[BLOCK:]

            pending = None
            for p, sl in enumerate(groups):
                nxt = scores(p, sl)
                if pending is not None:
                    grads(*pending)
                pending = (sl, *nxt)
            grads(*pending)

        @pl.when(n == nb)
        def _():
            for sl in groups:
                if first:
                    mode.put(dk_ref, sl, ck[:, sl])
                    mode.put(dv_ref, sl, cv[:, sl])
                else:
                    mode.put(dk_ref, sl, ck[:, sl] + mode.get(dki_ref, sl))
                    mode.put(dv_ref, sl, cv[:, sl] + mode.get(dvi_ref, sl))

    last = nb - 1
    cur = lambda w: mode.spec(w, "cur", last)
    prev = lambda w: mode.spec(w, "prev")
    ins = [q, k, k, v, v, d_o, lse, delta]
    specs = [cur(attn_w), prev(attn_w), cur(attn_w), prev(attn_w), cur(attn_w), cur(attn_w), cur(LANES), cur(LANES)]
    if not first:
        ins += list(run)
        specs += [cur(attn_w), prev(attn_w), prev(attn_w)]
    shp = jax.ShapeDtypeStruct((seq, attn_w) if nat else _perm_shape(seq, attn_w), F32)
    return pl.pallas_call(
        body, name=f"attn_bwd_{name}", grid=(mode.residues, nb + 1),
        out_shape=(shp, shp, shp), in_specs=specs, out_specs=(cur(attn_w), prev(attn_w), prev(attn_w)),
        scratch_shapes=[pltpu.VMEM((BLOCK, attn_w), F32), pltpu.VMEM((BLOCK, attn_w), F32)],
        compiler_params=_params(("arbitrary", "arbitrary")),
    )(*ins)


def _shift_down(u, halo, k):
    rolled = pltpu.roll(u, k, 0)
    row = lax.broadcasted_iota(jnp.int32, halo.shape, 0)
    top = jnp.where(row < k, pltpu.roll(halo, k, 0), rolled[:SUBLANES])
    return jnp.concatenate([top, rolled[SUBLANES:]], axis=0)


def _shift_up(u, halo, k):
    rows = u.shape[0]
    rolled = pltpu.roll(u, rows - k, 0)
    row = lax.broadcasted_iota(jnp.int32, halo.shape, 0)
    bot = jnp.where(row >= SUBLANES - k, pltpu.roll(halo, SUBLANES - k, 0), rolled[rows - SUBLANES:])
    return jnp.concatenate([rolled[:rows - SUBLANES], bot], axis=0)


def tail(o, lse, ga, cz, x, tgt, w_out, g2, cw):
    seq, d_model = x.shape
    attn_w = o.shape[1]
    conv_w = cz.shape[1] // 4
    mix = attn_w + conv_w
    groups = _lane_groups(attn_w)
    tm = ROW_TILE
    nt = seq // tm
    hb = tm // SUBLANES

    def body(o_ref, l_ref, ga_ref, cz_ref, hz_ref, x_ref, t_ref, w_ref, g_ref, cw_ref,
             do_ref, dl_ref, dop_ref, dlp_ref, lp_ref, dga_ref, dcb_ref, dgc_ref, dcv_ref, e_ref,
             dw_ref, dg_ref, dcw_ref, loss_ref, stage):
        i = pl.program_id(0)

        @pl.when(i == 0)
        def _():
            dw_ref[...] = jnp.zeros_like(dw_ref)
            dg_ref[...] = jnp.zeros_like(dg_ref)
            dcw_ref[...] = jnp.zeros_like(dcw_ref)
            loss_ref[...] = jnp.zeros_like(loss_ref)

        ov, gav = o_ref[...], ga_ref[...]
        sig_a = _sigmoid(gav)
        silu_a = gav * sig_a
        attn_out = ov * silu_a
        ch, cb = cz_ref[:, 0:conv_w], cz_ref[:, conv_w:2 * conv_w]
        cc, gc = cz_ref[:, 2 * conv_w:3 * conv_w], cz_ref[:, 3 * conv_w:4 * conv_w]
        u = cc * ch
        uh = hz_ref[:, 2 * conv_w:3 * conv_w] * hz_ref[:, 0:conv_w]
        uh = jnp.where(i > 0, uh, 0.0)
        u1 = _shift_down(u, uh, 1)
        u2 = _shift_down(u, uh, 2)
        w0, w1, w2 = cw_ref[0:1, :], cw_ref[1:2, :], cw_ref[2:3, :]
        cvv = u2 * w0 + u1 * w1 + u * w2
        sig_c = _sigmoid(gc)
        silu_c = gc * sig_c
        bc = cb * cvv
        conv_out = bc * silu_c
        mixed = jnp.concatenate([attn_out, conv_out], axis=1).astype(BF16)

        yv = jnp.dot(mixed, w_ref[...], preferred_element_type=F32)
        r2 = lax.rsqrt(jnp.mean(yv * yv, axis=-1, keepdims=True) + NORM_EPS)
        yhat = yv * r2
        gv = g_ref[...]
        diff = (x_ref[...] + yhat * gv) - t_ref[...]
        loss_ref[...] += _rowgroup_sum(diff * diff)
        ev = diff * (1.0 / d_model)
        e_ref[...] = ev
        dg_ref[...] += _rowgroup_sum(ev * yhat)
        eg = ev * gv
        dy = (r2 * (eg - yhat * jnp.mean(eg * yhat, axis=-1, keepdims=True))).astype(BF16)
        dw_ref[...] += _tn(mixed, dy)
        dm = _nt(dy, w_ref[...])
        dma, dmc = dm[:, :attn_w], dm[:, attn_w:]

        dov = dma * silu_a
        do_ref[...] = dov.astype(BF16)
        dga_ref[...] = (dma * ov * (sig_a * (1.0 + gav * (1.0 - sig_a)))).astype(BF16)
        prod = dov * ov
        lane = lax.broadcasted_iota(jnp.int32, (tm, LANES), 1)
        lo = lane < HEAD_DIM
        dblk = jnp.zeros((tm, LANES), F32)
        for p, sl in enumerate(groups):
            pr = prod[:, sl]
            dblk = jnp.where(lane == 2 * p, jnp.sum(jnp.where(lo, pr, 0.0), axis=1, keepdims=True), dblk)
            dblk = jnp.where(lane == 2 * p + 1, jnp.sum(jnp.where(lo, 0.0, pr), axis=1, keepdims=True), dblk)
            _stage_put(stage, p, dov[:, sl])
        dl_ref[...] = dblk
        for p, sl in enumerate(groups):
            _to_perm(stage, p, dop_ref, sl, BF16)
        all_lanes = slice(0, LANES)
        _stage_put(stage, 0, dblk)
        _stage_put(stage, 1, l_ref[...])
        _to_perm(stage, 0, dlp_ref, all_lanes, F32)
        _to_perm(stage, 1, lp_ref, all_lanes, F32)

        dsc = dmc * silu_c
        dcb_ref[...] = (dsc * cvv).astype(BF16)
        dgc_ref[...] = (dmc * bc * (sig_c * (1.0 + gc * (1.0 - sig_c)))).astype(BF16)
        dcv = dsc * cb
        dcv_ref[...] = dcv
        dcw_ref[0:SUBLANES, :] += _rowgroup_sum(dcv * u2)
        dcw_ref[SUBLANES:2 * SUBLANES, :] += _rowgroup_sum(dcv * u1)
        dcw_ref[2 * SUBLANES:3 * SUBLANES, :] += _rowgroup_sum(dcv * u)

    row = lambda n: pl.BlockSpec((tm, n), lambda i: (i, 0))
    whole = lambda a, b: pl.BlockSpec((a, b), lambda i: (0, 0))
    return pl.pallas_call(
        body, name="tail", grid=(nt,),
        out_shape=(jax.ShapeDtypeStruct((seq, attn_w), BF16), jax.ShapeDtypeStruct((seq, LANES), F32),
                   jax.ShapeDtypeStruct(_perm_shape(seq, attn_w), BF16), jax.ShapeDtypeStruct(_perm_shape(seq, LANES), F32),
                   jax.ShapeDtypeStruct(_perm_shape(seq, LANES), F32),
                   jax.ShapeDtypeStruct((seq, attn_w), BF16), jax.ShapeDtypeStruct((seq, conv_w), BF16),
                   jax.ShapeDtypeStruct((seq, conv_w), BF16), jax.ShapeDtypeStruct((seq, conv_w), F32),
                   jax.ShapeDtypeStruct((seq, d_model), F32), jax.ShapeDtypeStruct((mix, d_model), F32),
                   jax.ShapeDtypeStruct((SUBLANES, d_model), F32), jax.ShapeDtypeStruct((CONV_K * SUBLANES, conv_w), F32),
                   jax.ShapeDtypeStruct((SUBLANES, d_model), F32)),
        in_specs=[row(attn_w), row(LANES), row(attn_w), row(4 * conv_w),
                  pl.BlockSpec((SUBLANES, 4 * conv_w), lambda i: (jnp.maximum(i * hb - 1, 0), 0)),
                  row(d_model), row(d_model), _const_spec((mix, d_model)), _const_spec((1, d_model)),
                  _const_spec((SUBLANES, conv_w))],
        out_specs=(row(attn_w), row(LANES), _perm_tile_spec(attn_w, tm), _perm_tile_spec(LANES, tm), _perm_tile_spec(LANES, tm),
                   row(attn_w), row(conv_w), row(conv_w), row(conv_w), row(d_model),
                   whole(mix, d_model), whole(SUBLANES, d_model), whole(CONV_K * SUBLANES, conv_w),
                   whole(SUBLANES, d_model)),
        scratch_shapes=[pltpu.VMEM(_stage_shape(max(len(groups), 2), tm), F32)],
        compiler_params=_params(("arbitrary",)),
    )(o, lse, ga, cz, cz, x, tgt, w_out, g2, cw)


def dz_dx(nat_grads, perm_grads, dga, dcb, dgc, dcv, cz, tables, x, g1, e, w_full, cw):
    seq, d_model = x.shape
    attn_w = dga.shape[1]
    conv_w = dcv.shape[1]
    width = w_full.shape[2]
    in_w = 4 * attn_w + 4 * conv_w
    groups = _lane_groups(attn_w)
    tm = DZ_ROW_TILE
    nt = seq // tm
    hb = tm // SUBLANES

    def body(dq_ref, dk_ref, dv_ref, dqp_ref, dkp_ref, dvp_ref, dga_ref, dcb_ref, dgc_ref, dcv_ref, nh_ref, cz_ref,
             cos_ref, s1_ref, s2_ref, x_ref, g_ref, e_ref, w_ref, cw_ref, gx_ref, dz_ref, dg_ref, stage):
        i = pl.program_id(0)

        @pl.when(i == 0)
        def _():
            dg_ref[...] = jnp.zeros_like(dg_ref)

        cos, s1, s2 = cos_ref[...], s1_ref[...], s2_ref[...]
        for t, (nat_ref, perm_ref) in enumerate(((dq_ref, dqp_ref), (dk_ref, dkp_ref), (dv_ref, dvp_ref))):
            for g, sl in enumerate(groups):
                _from_perm(perm_ref, sl, stage, g)
            for g, sl in enumerate(groups):
                tot = nat_ref[:, sl] + _stage_get(stage, g)
                if t < 2:
                    tot = _rope_transposed(tot, cos, s1, s2)
                dz_ref[:, t * attn_w + g * LANES:t * attn_w + (g + 1) * LANES] = tot.astype(BF16)
        dz_ref[:, 3 * attn_w:4 * attn_w] = dga_ref[...]
        dcv = dcv_ref[...]
        nh = jnp.where(i < nt - 1, nh_ref[...], 0.0)
        w0, w1, w2 = cw_ref[0:1, :], cw_ref[1:2, :], cw_ref[2:3, :]
        du = dcv * w2 + _shift_up(dcv, nh, 1) * w1 + _shift_up(dcv, nh, 2) * w0
        base = 4 * attn_w
        dz_ref[:, base:base + conv_w] = (du * cz_ref[:, 2 * conv_w:3 * conv_w]).astype(BF16)
        dz_ref[:, base + conv_w:base + 2 * conv_w] = dcb_ref[...]
        dz_ref[:, base + 2 * conv_w:base + 3 * conv_w] = (du * cz_ref[:, 0:conv_w]).astype(BF16)
        dz_ref[:, base + 3 * conv_w:base + 4 * conv_w] = dgc_ref[...]

        dh = _nt(dz_ref[:, 0:width], w_ref[0])
        for j in range(1, N_CHIPS):
            dh = dh + _nt(dz_ref[:, j * width:(j + 1) * width], w_ref[j])
        xv = x_ref[...]
        r1 = lax.rsqrt(jnp.mean(xv * xv, axis=-1, keepdims=True) + NORM_EPS)
        xhat = xv * r1
        dg_ref[...] += _rowgroup_sum(dh * xhat)
        dhg = dh * g_ref[...]
        gx_ref[...] = r1 * (dhg - xhat * jnp.mean(dhg * xhat, axis=-1, keepdims=True)) + e_ref[...]

    row = lambda n: pl.BlockSpec((tm, n), lambda i: (i, 0))
    whole = lambda a, b: pl.BlockSpec((a, b), lambda i: (0, 0))
    pt = _perm_tile_spec(attn_w, tm)
    return pl.pallas_call(
        body, name="dz_dx", grid=(nt,),
        out_shape=(jax.ShapeDtypeStruct((seq, d_model), F32), jax.ShapeDtypeStruct((seq, in_w), BF16),
                   jax.ShapeDtypeStruct((SUBLANES, d_model), F32)),
        in_specs=[row(attn_w), row(attn_w), row(attn_w), pt, pt, pt, row(attn_w), row(conv_w), row(conv_w), row(conv_w),
                  pl.BlockSpec((SUBLANES, conv_w), lambda i: (jnp.minimum((i + 1) * hb, seq // SUBLANES - 1), 0)),
                  row(4 * conv_w), row(LANES), row(LANES), row(LANES), row(d_model), _const_spec((1, d_model)), row(d_model),
                  _const_spec(w_full.shape), _const_spec((SUBLANES, conv_w))],
        out_specs=(row(d_model), row(in_w), whole(SUBLANES, d_model)),
        scratch_shapes=[pltpu.VMEM(_stage_shape(len(groups), tm), F32)],
        compiler_params=_params(("arbitrary",)),
    )(*nat_grads, *perm_grads, dga, dcb, dgc, dcv, dcv, cz, *tables, x, g1, e, w_full, cw)


def dw_in(ht, dz):
    d_model, seq = ht.shape
    in_w = dz.shape[1]
    half = in_w // N_DEV
    ts = 512

    def body(ht_ref, dz_ref, o_ref):
        @pl.when(pl.program_id(0) == 0)
        def _():
            o_ref[...] = jnp.zeros_like(o_ref)

        for jb in range(N_DEV):
            o_ref[jb] += jnp.dot(ht_ref[...], dz_ref[:, jb * half:(jb + 1) * half], preferred_element_type=F32)

    return pl.pallas_call(
        body, name="dw_in", grid=(seq // ts,),
        out_shape=jax.ShapeDtypeStruct((N_DEV, d_model, half), F32),
        in_specs=[pl.BlockSpec((d_model, ts), lambda s: (0, s)), pl.BlockSpec((ts, in_w), lambda s: (s, 0))],
        out_specs=pl.BlockSpec((N_DEV, d_model, half), lambda s: (0, 0, 0)),
        compiler_params=_params(("arbitrary",)),
    )(ht, dz)


def grad_reduce(g_in, g_out, small):
    gi = g_in.reshape(N_CHIPS, 2, *g_in.shape[1:])
    go = g_out.reshape(N_CHIPS, 2, *g_out.shape[1:])
    shp_i, shp_o = gi.shape[2:], go.shape[2:]

    def body(gi_ref, go_ref, sm_ref, ri_ref, ro_ref, rs_ref, a_i, b_i, s_i, c_i, a_o, b_o, s_o, c_o, sbuf,
             loc_sems, sa, ra, sb, rb, sc, rc, ss, rs):
        x, y, c = lax.axis_index("x"), lax.axis_index("y"), lax.axis_index("c")
        me = 2 * x + y
        sib = (x, y, 1 - c)
        srcs, mine, theirs, staged, contrib, res = ((gi_ref, go_ref), (a_i, a_o), (b_i, b_o), (s_i, s_o), (c_i, c_o),
                                                    (ri_ref, ro_ref))

        flips = [(fx, fy, fc) for fx in (0, 1) for fy in (0, 1) for fc in (0, 1)][1:]
        my8 = 4 * x + 2 * y + c
        sbuf[my8] = sm_ref[...]

        def small_copy(k, slot, to):
            return pltpu.make_async_remote_copy(src_ref=sm_ref, dst_ref=sbuf.at[slot], send_sem=ss.at[k], recv_sem=rs.at[k],
                                                device_id=to, device_id_type=MESH)

        sends = []
        for k, (fx, fy, fc) in enumerate(flips):
            px, py, pc = _flip(x, fx), _flip(y, fy), _flip(c, fc)
            sends.append(small_copy(k, my8, (px, py, pc)))
            sends[-1].start()

        def a_copy(t, j):
            return pltpu.make_async_remote_copy(src_ref=srcs[t].at[j, 1 - c], dst_ref=theirs[t].at[j], send_sem=sa.at[t, j],
                                                recv_sem=ra.at[t, j], device_id=sib, device_id_type=MESH)

        peers = _chip_peers(x, y)
        order = [2 * px + py for px, py in peers] + [me]
        loads = [[pltpu.make_async_copy(srcs[t].at[j, c], mine[t].at[j], loc_sems.at[t, j]) for t in range(2)] for j in order]
        for pos, j in enumerate(order):
            for t in range(2):
                loads[pos][t].start()
                sends.append(a_copy(t, j))
                sends[-1].start()

        def b_copy(k, t, piece, slot, to):
            return pltpu.make_async_remote_copy(src_ref=staged[t].at[piece], dst_ref=contrib[t].at[slot], send_sem=sb.at[k, t],
                                                recv_sem=rb.at[k, t], device_id=to, device_id_type=MESH)

        for k, (px, py) in enumerate(peers):
            j = 2 * px + py
            for t in range(2):
                loads[k][t].wait()
                a_copy(t, j).wait_recv()
                staged[t][j] = (mine[t][j] + theirs[t][j]).astype(BF16)
                sends.append(b_copy(k, t, j, me, (px, py, c)))
                sends[-1].start()
        for t in range(2):
            loads[len(peers)][t].wait()
            a_copy(t, me).wait_recv()
            mine[t][me] = mine[t][me] + theirs[t][me]
            contrib[t][me] = mine[t][me].astype(BF16)
        for k, (px, py) in enumerate(peers):
            for t in range(2):
                b_copy(k, t, me, 2 * px + py, (px, py, c)).wait_recv()

        def c_copy(t, half):
            return pltpu.make_async_remote_copy(src_ref=res[t].at[half], dst_ref=res[t].at[half], send_sem=sc.at[t],
                                                recv_sem=rc.at[t], device_id=sib, device_id_type=MESH)

        for t in range(2):
            own = mine[t][me]
            term = lambda j: jnp.where(me == j, own, contrib[t][j].astype(F32))
            res[t][c] = ((term(0) + term(1)) + term(2)) + term(3)
            sends.append(c_copy(t, c))
            sends[-1].start()
        for t in range(2):
            c_copy(t, 1 - c).wait_recv()

        for k, (fx, fy, fc) in enumerate(flips):
            px, py, pc = _flip(x, fx), _flip(y, fy), _flip(c, fc)
            small_copy(k, 4 * px + 2 * py + pc, (px, py, pc)).wait_recv()
        tot = sbuf[0]
        for d in range(1, N_DEV):
            tot = tot + sbuf[d]
        rs_ref[...] = tot
        for cp in sends:
            cp.wait_send()

    vm = pl.BlockSpec(memory_space=pltpu.VMEM)
    anyspace = pl.BlockSpec(memory_space=pl.ANY)
    dma = pltpu.SemaphoreType.DMA
    bufs = lambda shp: [pltpu.VMEM((N_CHIPS, *shp), F32), pltpu.VMEM((N_CHIPS, *shp), F32),
                        pltpu.VMEM((N_CHIPS, *shp), BF16), pltpu.VMEM((N_CHIPS, *shp), BF16)]
    return pl.pallas_call(
        body, name="grad_reduce",
        out_shape=(jax.ShapeDtypeStruct((2, *shp_i), F32), jax.ShapeDtypeStruct((2, *shp_o), F32),
                   jax.ShapeDtypeStruct(small.shape, F32)),
        in_specs=[anyspace, anyspace, vm], out_specs=(vm, vm, vm),
        scratch_shapes=[*bufs(shp_i), *bufs(shp_o), pltpu.VMEM((N_DEV, *small.shape), F32),
                        dma((2, N_CHIPS)), dma((2, N_CHIPS)), dma((2, N_CHIPS)), dma((3, 2)), dma((3, 2)), dma((2,)), dma((2,)),
                        dma((N_DEV - 1,)), dma((N_DEV - 1,))],
        compiler_params=_params(),
    )(gi, go, small)


def _adam_math(w, g, m, v):
    m = ADAM_B1 * m + (1.0 - ADAM_B1) * g
    v = ADAM_B2 * v + (1.0 - ADAM_B2) * (g * g)
    m_hat = m / (1.0 - ADAM_B1 ** ADAM_STEP)
    v_hat = v / (1.0 - ADAM_B2 ** ADAM_STEP)
    delta = -ADAM_LR * (m_hat / (jnp.sqrt(v_hat) + ADAM_EPS) + ADAM_WD * w)
    return delta, m, v


def adam_shard(name, w, g2, m, v, block, grid, w_map, g_map):
    def body(w_ref, g_ref, m_ref, v_ref, go_ref, d_ref, mo_ref, vo_ref):
        g = g_ref[0]
        delta, mn, vn = _adam_math(w_ref[...], g, m_ref[...], v_ref[...])
        go_ref[...] = g
        d_ref[...] = delta
        mo_ref[...] = mn
        vo_ref[...] = vn

    ws = pl.BlockSpec(block, w_map)
    shp = jax.ShapeDtypeStruct(w.shape, F32)
    return pl.pallas_call(
        body, name=name, grid=grid, out_shape=(shp, shp, shp, shp),
        in_specs=[ws, pl.BlockSpec((1, *block), g_map), ws, ws], out_specs=(ws, ws, ws, ws),
        compiler_params=_params(("arbitrary",) * len(grid)),
    )(w, g2, m, v)


def adam_small(ws, gs, ms, vs):
    n = len(ws)

    def body(*refs):
        ins, outs = refs[:4 * n], refs[4 * n:]
        for t in range(n):
            delta, mn, vn = _adam_math(ins[t][...], ins[n + t][...], ins[2 * n + t][...], ins[3 * n + t][...])
            outs[3 * t][...] = delta
            outs[3 * t + 1][...] = mn
            outs[3 * t + 2][...] = vn

    vm = pl.BlockSpec(memory_space=pltpu.VMEM)
    outs = pl.pallas_call(
        body, name="adam_small",
        out_shape=tuple(jax.ShapeDtypeStruct(w.shape, F32) for w in ws for _ in range(3)),
        in_specs=[vm] * (4 * n), out_specs=tuple([vm] * (3 * n)),
        compiler_params=_params(),
    )(*ws, *gs, *ms, *vs)
    return [outs[3 * t:3 * t + 3] for t in range(n)]


def kernel(x, norm_pre_g, w_in, conv_w, w_out, norm_post_g, loss_target, m_norm_pre_g, m_w_in, m_conv_w, m_w_out, m_norm_post_g, v_norm_pre_g, v_w_in, v_conv_w, v_w_out, v_norm_post_g):
    _, seq, d_model = x.shape
    width = w_in.shape[1]
    conv_q = conv_w.shape[1]
    conv_width = N_CHIPS * conv_q
    attn_width = d_model - conv_width
    xs, tg = x[0], loss_target[0]
    g1, g2 = norm_pre_g.reshape(1, d_model), norm_post_g.reshape(1, d_model)

    w_full, wout_full, cw_full = gather_weights(w_in, w_out, conv_w)
    wout2 = wout_full.reshape(attn_width + conv_width, d_model)
    cw = jnp.zeros((SUBLANES, conv_width), F32).at[:CONV_K].set(
        cw_full[:, :CONV_K, :conv_q].transpose(1, 0, 2).reshape(CONV_K, conv_width))
    tables = _rope_tables(seq)

    ht, q, k, v, qp, kp, vp, ga, cz = inproj(xs, g1, w_full, tables, attn_width, conv_width)
    run = attn_fwd("p4", qp, kp, vp, None)
    run = attn_fwd("p16", qp, kp, vp, run)
    o, lse = attn_fwd("nat", q, k, v, run)
    (d_o, delta, d_op, delta_p, lse_p, dga, dcb, dgc, dcv, e, dwout, dg2, dcw, loss_acc) = tail(
        o, lse, ga, cz, xs, tg, wout2, g2, cw)
    nat_grads = attn_bwd("nat", q, k, v, d_o, lse, delta, None)
    perm_grads = attn_bwd("p4", qp, kp, vp, d_op, lse_p, delta_p, None)
    perm_grads = attn_bwd("p16", qp, kp, vp, d_op, lse_p, delta_p, perm_grads)
    grad_x, dz, dg1 = dz_dx(nat_grads, perm_grads, dga, dcb, dgc, dcv, cz, tables, xs, g1, e, w_full, cw)
    dwin = dw_in(ht, dz)

    small = jnp.zeros((SUBLANES, d_model), F32)
    small = small.at[0].set(dg1.sum(axis=0)).at[1].set(dg2.sum(axis=0))
    small = small.at[2:2 + CONV_K, :conv_width].set(dcw.reshape(CONV_K, SUBLANES, conv_width).sum(axis=1))
    small = small.at[2 + CONV_K, 0].set(jnp.sum(loss_acc))
    rin, rout, rsmall = grad_reduce(dwin, dwout.reshape(N_DEV, -1, d_model), small)

    half = width // 2
    tr = 256
    gw_in, d_in, m_in, v_in = adam_shard(
        "adam_w_in", w_in, rin, m_w_in, v_w_in, (tr, half), (2, d_model // tr),
        lambda hf, i: (i, hf), lambda hf, i: (hf, i, 0))
    rq = w_out.shape[0] // 2
    gw_out, d_out, m_out, v_out = adam_shard(
        "adam_w_out", w_out, rout, m_w_out, v_w_out, (rq, d_model), (2,),
        lambda hf: (hf, 0), lambda hf: (hf, 0, 0))

    chip = 2 * lax.axis_index("x") + lax.axis_index("y")
    g_pre, g_post = rsmall[0:1], rsmall[1:2]
    g_conv = lax.dynamic_slice(rsmall[2:2 + CONV_K, :conv_width], (0, chip * conv_q), (CONV_K, conv_q))
    (d_pre, m_pre, v_pre), (d_post, m_post, v_post), (d_cv, m_cv, v_cv) = adam_small(
        [g1, g2, conv_w], [g_pre, g_post, g_conv],
        [m_norm_pre_g.reshape(1, d_model), m_norm_post_g.reshape(1, d_model), m_conv_w],
        [v_norm_pre_g.reshape(1, d_model), v_norm_post_g.reshape(1, d_model), v_conv_w])

    loss = 0.5 * rsmall[2 + CONV_K, 0] / d_model
    vec = lambda a: a.reshape(d_model)
    return (loss, grad_x.reshape(1, seq, d_model),
            vec(g_pre), gw_in, g_conv, gw_out, vec(g_post),
            vec(d_pre), d_in, d_cv, d_out, vec(d_post),
            vec(m_pre), m_in, m_cv, m_out, vec(m_post),
            vec(v_pre), v_in, v_cv, v_out, vec(v_post))
```

```python
import jax
import jax.numpy as jnp
from jax import lax
from jax.experimental import pallas as pl
from jax.experimental.pallas import tpu as pltpu

HEAD_DIM = 64
LANES = 128
SUBLANES = 8
BLOCK = 128
WINDOW_KEYS = 128
PERM = 16
PJ = 4
P4_ROWS = BLOCK // PJ
ROW_TILE = 512
DZ_ROW_TILE = 512
CONV_K = 3
ROPE_THETA = 10000.0
NORM_EPS = 1e-6
ATTN_SCALE = HEAD_DIM ** -0.5
NEG = -1e30
N_CHIPS = 4
N_DEV = 8
MESH = pl.DeviceIdType.MESH
ADAM_LR = 0.001
ADAM_B1 = 0.9
ADAM_B2 = 0.999
ADAM_EPS = 1e-08
ADAM_WD = 0.01
ADAM_STEP = 10
VMEM_LIMIT = 52 * 1024 * 1024

F32 = jnp.float32
BF16 = jnp.bfloat16


def _params(sem=None, **kw):
    return pltpu.CompilerParams(dimension_semantics=sem, vmem_limit_bytes=VMEM_LIMIT, **kw)


def _const_spec(shape):
    return pl.BlockSpec(shape, lambda *_: (0,) * len(shape), pipeline_mode=pl.Buffered(1))


def _sigmoid(z):
    return 1.0 / (1.0 + jnp.exp(-z))


def _rowgroup_sum(a):
    rows, n = a.shape
    return a.reshape(rows // SUBLANES, SUBLANES, n).sum(axis=0)


def _nt(a, b):
    return lax.dot_general(a, b, (((1,), (1,)), ((), ())), preferred_element_type=F32)


def _tn(a, b):
    return lax.dot_general(a, b, (((0,), (0,)), ((), ())), preferred_element_type=F32)


def _col_pieces(a, b, width):
    out = []
    while a < b:
        j = a // width
        e = min(b, (j + 1) * width)
        out.append((j, a - j * width, e - j * width))
        a = e
    return out


def _lane_groups(width):
    return [slice(g * LANES, (g + 1) * LANES) for g in range(width // LANES)]


def _perm_shape(seq, width):
    return (PJ, PJ, seq // PERM, width)


def _perm_tile_spec(width, tm):
    return pl.BlockSpec((PJ, PJ, tm // PERM, width), lambda i: (0, 0, i, 0))


STAGE_PITCH = 24


def _stage_shape(groups, rows):
    return (groups, rows // PERM * STAGE_PITCH, LANES)


def _stage_put(stage, g, val):
    for a in range(val.shape[0] // PERM):
        stage[g, a * STAGE_PITCH:a * STAGE_PITCH + PERM, :] = val[a * PERM:(a + 1) * PERM]


def _stage_get(stage, g):
    return jnp.concatenate([stage[g, a * STAGE_PITCH:a * STAGE_PITCH + PERM, :]
                            for a in range(stage.shape[1] // STAGE_PITCH)], axis=0)


def _to_perm(stage, g, dst_ref, sl, dtype):
    rows = stage.shape[1] // STAGE_PITCH
    for b in range(PERM):
        dst_ref[b // PJ, b % PJ, :, sl] = stage[g, pl.ds(b, rows, stride=STAGE_PITCH), :].astype(dtype)


def _from_perm(src_ref, sl, stage, g):
    rows = stage.shape[1] // STAGE_PITCH
    for b in range(PERM):
        stage[g, pl.ds(b, rows, stride=STAGE_PITCH), :] = src_ref[b // PJ, b % PJ, :, sl].astype(F32)


def _flip(a, f):
    return 1 - a if f else a


def _chip_peers(x, y):
    return [(1 - x, y), (x, 1 - y), (1 - x, 1 - y)]


def gather_weights(w_in, w_out, conv_w):
    d_model, width = w_in.shape
    rows = w_out.shape[0]
    cw = jnp.zeros((SUBLANES, LANES), F32).at[:CONV_K, :conv_w.shape[1]].set(conv_w)

    def body(win_ref, wout_ref, cw_ref, winf_ref, woutf_ref, cwf_ref, st_in, st_out, ici_send, ici_recv, d2d_send, d2d_recv):
        x, y, c = lax.axis_index("x"), lax.axis_index("y"), lax.axis_index("c")
        me = 2 * x + y
        sib = (x, y, 1 - c)
        st_in[...] = win_ref[...].astype(BF16)
        st_out[...] = wout_ref[...].astype(BF16)
        winf_ref[me] = st_in[...]
        woutf_ref[me] = st_out[...]
        cwf_ref[me] = cw_ref[...]
        stages = (st_in, st_out)
        fulls = (winf_ref, woutf_ref)
        halves = (d_model // 2, rows // 2)

        def half(t, core):
            return pl.ds(pl.multiple_of(core * halves[t], halves[t]), halves[t])

        def ici(k, t, slot, to, core):
            src = stages[t].at[half(t, core)] if t < 2 else cw_ref
            dst = fulls[t].at[slot, half(t, core)] if t < 2 else cwf_ref.at[slot]
            return pltpu.make_async_remote_copy(src_ref=src, dst_ref=dst, send_sem=ici_send.at[k, t], recv_sem=ici_recv.at[k, t],
                                                device_id=to, device_id_type=MESH)

        def d2d(k, t, slot, core):
            ref = fulls[t].at[slot, half(t, core)]
            return pltpu.make_async_remote_copy(src_ref=ref, dst_ref=ref, send_sem=d2d_send.at[k, t], recv_sem=d2d_recv.at[k, t],
                                                device_id=sib, device_id_type=MESH)

        peers = _chip_peers(x, y)
        sends = [ici(k, t, me, (px, py, c), c) for k, (px, py) in enumerate(peers) for t in range(3)]
        for cp in sends:
            cp.start()
        for k, (px, py) in enumerate(peers):
            for t in range(2):
                ici(k, t, 2 * px + py, (px, py, c), c).wait_recv()
                fwd = d2d(k, t, 2 * px + py, c)
                fwd.start()
                sends.append(fwd)
            ici(k, 2, 2 * px + py, (px, py, c), c).wait_recv()
        for k, (px, py) in enumerate(peers):
            for t in range(2):
                d2d(k, t, 2 * px + py, 1 - c).wait_recv()
        for cp in sends:
            cp.wait_send()

    vm = pl.BlockSpec(memory_space=pltpu.VMEM)
    dma = pltpu.SemaphoreType.DMA
    return pl.pallas_call(
        body, name="gather_weights",
        out_shape=(jax.ShapeDtypeStruct((N_CHIPS, d_model, width), BF16),
                   jax.ShapeDtypeStruct((N_CHIPS, rows, d_model), BF16),
                   jax.ShapeDtypeStruct((N_CHIPS, SUBLANES, LANES), F32)),
        in_specs=[vm, vm, vm], out_specs=(vm, vm, vm),
        scratch_shapes=[pltpu.VMEM((d_model, width), BF16), pltpu.VMEM((rows, d_model), BF16),
                        dma((3, 3)), dma((3, 3)), dma((3, 2)), dma((3, 2))],
        compiler_params=_params(),
    )(w_in, w_out, cw)


def _rope_tables(seq):
    half = HEAD_DIM // 2
    inv_freq = ROPE_THETA ** (-jnp.arange(half, dtype=F32) * 2.0 / HEAD_DIM)
    ang = jnp.arange(seq).astype(F32)[:, None] * jnp.tile(inv_freq, LANES // half)[None, :]
    first_half = (jnp.arange(LANES) % HEAD_DIM < half)[None, :]
    sin = jnp.sin(ang)
    return jnp.cos(ang), jnp.where(first_half, -sin, 0.0), jnp.where(first_half, 0.0, sin)


def _rope(t, cos, s1, s2):
    return t * cos + pltpu.roll(t, LANES - HEAD_DIM // 2, 1) * s1 + pltpu.roll(t, HEAD_DIM // 2, 1) * s2


def _rope_transposed(g, cos, s1, s2):
    return g * cos + pltpu.roll(g * s1, HEAD_DIM // 2, 1) + pltpu.roll(g * s2, LANES - HEAD_DIM // 2, 1)


def inproj(x, g1, w_full, tables, attn_w, conv_w):
    seq, d_model = x.shape
    width = w_full.shape[2]
    tm = ROW_TILE
    groups = _lane_groups(attn_w)

    def body(x_ref, g_ref, w_ref, cos_ref, s1_ref, s2_ref,
             ht_ref, q_ref, k_ref, v_ref, qp_ref, kp_ref, vp_ref, ga_ref, cz_ref, stage):
        xv = x_ref[...]
        hb = ((xv * lax.rsqrt(jnp.mean(xv * xv, axis=-1, keepdims=True) + NORM_EPS)) * g_ref[...]).astype(BF16)
        ht_ref[...] = jnp.transpose(hb)
        cos, s1, s2 = cos_ref[...], s1_ref[...], s2_ref[...]

        def proj(a, b):
            parts = [jnp.dot(hb, w_ref[j, :, lo:hi], preferred_element_type=F32) for j, lo, hi in _col_pieces(a, b, width)]
            return parts[0] if len(parts) == 1 else jnp.concatenate(parts, axis=1)

        def emit(z, nat_ref, perm_ref, fn):
            for g, sl in enumerate(groups):
                val = fn(z[:, sl])
                nat_ref[:, sl] = val.astype(BF16)
                _stage_put(stage, g, val)
            for g, sl in enumerate(groups):
                _to_perm(stage, g, perm_ref, sl, BF16)

        emit(proj(0, attn_w), q_ref, qp_ref, lambda t: _rope(t, cos, s1, s2) * ATTN_SCALE)
        emit(proj(attn_w, 2 * attn_w), k_ref, kp_ref, lambda t: _rope(t, cos, s1, s2))
        emit(proj(2 * attn_w, 3 * attn_w), v_ref, vp_ref, lambda t: t)
        ga_ref[...] = proj(3 * attn_w, 4 * attn_w)
        cz_ref[...] = proj(4 * attn_w, 4 * attn_w + 4 * conv_w)

    row = lambda n: pl.BlockSpec((tm, n), lambda i: (i, 0))
    nat = jax.ShapeDtypeStruct((seq, attn_w), BF16)
    perm = jax.ShapeDtypeStruct(_perm_shape(seq, attn_w), BF16)
    return pl.pallas_call(
        body, name="inproj", grid=(seq // tm,),
        out_shape=(jax.ShapeDtypeStruct((d_model, seq), BF16), nat, nat, nat, perm, perm, perm,
                   jax.ShapeDtypeStruct((seq, attn_w), F32), jax.ShapeDtypeStruct((seq, 4 * conv_w), F32)),
        in_specs=[row(d_model), _const_spec((1, d_model)), _const_spec(w_full.shape), row(LANES), row(LANES), row(LANES)],
        out_specs=(pl.BlockSpec((d_model, tm), lambda i: (0, i)), row(attn_w), row(attn_w), row(attn_w),
                   _perm_tile_spec(attn_w, tm), _perm_tile_spec(attn_w, tm), _perm_tile_spec(attn_w, tm),
                   row(attn_w), row(4 * conv_w)),
        scratch_shapes=[pltpu.VMEM(_stage_shape(len(groups), tm), F32)],
        compiler_params=_params(("arbitrary",)),
    )(x, g1, w_full, *tables)


class _Mode:
    def __init__(self, name, seq):
        self.name = name
        if name == "nat":
            self.residues, self.nb = 1, seq // BLOCK
        elif name == "p16":
            self.residues, self.nb = PERM, seq // PERM // BLOCK
        else:
            self.residues, self.nb = PJ, seq // PERM // P4_ROWS

    def spec(self, width, which, last=None):
        if which == "prev":
            blk = lambda n: jnp.maximum(n - 1, 0)
        elif last is None:
            blk = lambda n: n
        else:
            blk = lambda n: jnp.minimum(n, last)
        if self.name == "nat":
            return pl.BlockSpec((BLOCK, width), lambda r, n: (blk(n), 0))
        if self.name == "p16":
            return pl.BlockSpec((1, 1, BLOCK, width), lambda r, n: (r // PJ, r % PJ, blk(n), 0))
        return pl.BlockSpec((PJ, 1, P4_ROWS, width), lambda r, n: (0, r, blk(n), 0))

    def get(self, ref, sl):
        if self.name == "nat":
            return ref[:, sl]
        if self.name == "p16":
            return ref[0, 0, :, sl]
        return jnp.concatenate([ref[j, 0, :, sl] for j in range(PJ)], axis=0)

    def put(self, ref, sl, val):
        val = val.astype(ref.dtype)
        if self.name == "nat":
            ref[:, sl] = val
        elif self.name == "p16":
            ref[0, 0, :, sl] = val
        else:
            for j in range(PJ):
                ref[j, 0, :, sl] = val[j * P4_ROWS:(j + 1) * P4_ROWS]

    def index(self, idx, is_key):
        if self.name != "p4":
            return idx - BLOCK if is_key else idx
        within = jnp.bitwise_and(idx, BLOCK - 1)
        m = PJ * jnp.bitwise_and(within, P4_ROWS - 1) + jnp.right_shift(within, P4_ROWS.bit_length() - 1)
        return m + BLOCK * (jnp.right_shift(idx, BLOCK.bit_length() - 1) - 1) if is_key else m

    def bias(self, n, keys_major):
        shape = (2 * BLOCK, BLOCK) if keys_major else (BLOCK, 2 * BLOCK)
        kdim = 0 if keys_major else 1
        kidx = lax.broadcasted_iota(jnp.int32, shape, kdim)
        qidx = lax.broadcasted_iota(jnp.int32, shape, 1 - kdim)
        rel = self.index(qidx, False) - self.index(kidx, True)
        valid = (rel >= 0) & (rel <= WINDOW_KEYS) & ((kidx >= BLOCK) | (n > 0))
        return jnp.where(valid, 0.0, NEG)


def _head_masks():
    lane = lax.broadcasted_iota(jnp.int32, (BLOCK, LANES), 1)
    lo = lane < HEAD_DIM
    return lane, lo, jnp.where(lo, 1.0, 0.0).astype(BF16), jnp.where(lo, 0.0, 1.0).astype(BF16)


def _column(blk, lane, h):
    return jnp.sum(jnp.where(lane == h, blk, 0.0), axis=1, keepdims=True)


def attn_fwd(name, q, k, v, run):
    nat = name == "nat"
    seq = q.shape[0] if nat else q.shape[2] * PERM
    attn_w = q.shape[-1]
    mode = _Mode(name, seq)
    groups = _lane_groups(attn_w)
    first = run is None
    all_lanes = slice(0, LANES)

    def body(*refs):
        q_ref, kp_ref, kc_ref, vp_ref, vc_ref = refs[:5]
        if first:
            o_ref, l_ref = refs[5:]
        elif nat:
            oin_ref, lin_ref, ex_ref, o_ref, l_ref, ostage, lstage = refs[5:]
        else:
            oin_ref, lin_ref, ex_ref, o_ref, l_ref = refs[5:]
        n = pl.program_id(1)
        bias = mode.bias(n, True)
        bias2 = jnp.concatenate([bias, bias], axis=1)
        _, lo, m_lo, m_hi = _head_masks()
        head_row = lax.broadcasted_iota(jnp.int32, (BLOCK, LANES), 0)
        lrows = jnp.zeros((BLOCK, LANES), F32)
        def probs(sl):
            q2 = mode.get(q_ref, sl)
            kcat = jnp.concatenate([mode.get(kp_ref, sl), mode.get(kc_ref, sl)], axis=0)
            vcat = jnp.concatenate([mode.get(vp_ref, sl), mode.get(vc_ref, sl)], axis=0)
            qq = jnp.concatenate([q2 * m_lo, q2 * m_hi], axis=0)
            s_t = _nt(kcat, qq) + bias2
            m = jnp.max(s_t, axis=0, keepdims=True)
            pe = jnp.exp(s_t - m)
            l = jnp.sum(pe, axis=0, keepdims=True)
            return vcat, (pe * (1.0 / l)).astype(BF16), m + jnp.log(l)

        def output(p, sl, vcat, pn, lse, lrows):
            o_new = _tn(pn, vcat)
            mode.put(o_ref, sl, jnp.where(lo, o_new[:BLOCK], o_new[BLOCK:]))
            lrows = jnp.where(head_row == 2 * p, lse[:, :BLOCK], lrows)
            return jnp.where(head_row == 2 * p + 1, lse[:, BLOCK:], lrows)

        pending = None
        for p, sl in enumerate(groups):
            nxt = probs(sl)
            if pending is not None:
                lrows = output(*pending, lrows)
            pending = (p, sl, *nxt)
        lrows = output(*pending, lrows)
        lblk = jnp.transpose(lrows)
        if first:
            mode.put(l_ref, all_lanes, lblk)
        else:
            if nat:
                for g, sl in enumerate(groups):
                    _from_perm(oin_ref, sl, ostage, g)
                _from_perm(lin_ref, all_lanes, lstage, 0)
                lin = _stage_get(lstage, 0)
            else:
                lin = mode.get(lin_ref, all_lanes)
            mx = jnp.maximum(lin, lblk)
            new = mx + jnp.log(jnp.exp(lin - mx) + jnp.exp(lblk - mx))
            mode.put(l_ref, all_lanes, new)

            def expand(w):
                hi = w.astype(BF16)
                rest = (w - hi.astype(F32)).astype(BF16)
                ex = ex_ref[...]
                return jnp.dot(hi, ex, preferred_element_type=F32) + jnp.dot(rest, ex, preferred_element_type=F32)

            w_prev, w_cur = expand(jnp.exp(lin - new)), expand(jnp.exp(lblk - new))
            for p, sl in enumerate(groups):
                o_prev = _stage_get(ostage, p) if nat else mode.get(oin_ref, sl)
                mode.put(o_ref, sl, w_prev[:, sl] * o_prev + w_cur[:, sl] * mode.get(o_ref, sl))

    ins = [q, k, k, v, v]
    specs = [mode.spec(attn_w, "cur"), mode.spec(attn_w, "prev"), mode.spec(attn_w, "cur"),
             mode.spec(attn_w, "prev"), mode.spec(attn_w, "cur")]
    scratch = []
    if not first:
        ins += list(run)
        if nat:
            rows8 = BLOCK // PERM
            specs += [pl.BlockSpec((PJ, PJ, rows8, attn_w), lambda r, n: (0, 0, n, 0)),
                      pl.BlockSpec((PJ, PJ, rows8, LANES), lambda r, n: (0, 0, n, 0))]
            scratch = [pltpu.VMEM(_stage_shape(len(groups), BLOCK), F32), pltpu.VMEM(_stage_shape(1, BLOCK), F32)]
        else:
            specs += [mode.spec(attn_w, "cur"), mode.spec(LANES, "cur")]
        head_of_lane = jnp.arange(attn_w, dtype=jnp.int32) // HEAD_DIM
        ins.append((jnp.arange(LANES, dtype=jnp.int32)[:, None] == head_of_lane[None, :]).astype(BF16))
        specs.append(_const_spec((LANES, attn_w)))
    if nat:
        out_shape = (jax.ShapeDtypeStruct((seq, attn_w), F32), jax.ShapeDtypeStruct((seq, LANES), F32))
    else:
        out_shape = (jax.ShapeDtypeStruct(_perm_shape(seq, attn_w), F32), jax.ShapeDtypeStruct(_perm_shape(seq, LANES), F32))
    return pl.pallas_call(
        body, name=f"attn_fwd_{name}", grid=(mode.residues, mode.nb),
        out_shape=out_shape, in_specs=specs, out_specs=(mode.spec(attn_w, "cur"), mode.spec(LANES, "cur")),
        scratch_shapes=scratch,
        compiler_params=_params(("arbitrary", "arbitrary")),
    )(*ins)


def attn_bwd(name, q, k, v, d_o, lse, delta, run):
    nat = name == "nat"
    seq = q.shape[0] if nat else q.shape[2] * PERM
    attn_w = q.shape[-1]
    mode = _Mode(name, seq)
    nb = mode.nb
    groups = _lane_groups(attn_w)
    first = run is None
    all_lanes = slice(0, LANES)

    def body(*refs):
        q_ref, kp_ref, kc_ref, vp_ref, vc_ref, do_ref, lse_ref, dl_ref = refs[:8]
        if first:
            dq_ref, dk_ref, dv_ref, ck, cv = refs[8:]
        else:
            dqi_ref, dki_ref, dvi_ref, dq_ref, dk_ref, dv_ref, ck, cv = refs[8:]
        n = pl.program_id(1)

        @pl.when(n == 0)
        def _():
            ck[...] = jnp.zeros_like(ck)
            cv[...] = jnp.zeros_like(cv)

        @pl.when(n < nb)
        def _():
            bias = mode.bias(n, True)
            bias2 = jnp.concatenate([bias, bias], axis=1)
            _, lo, m_lo, m_hi = _head_masks()
            lse_t = jnp.transpose(mode.get(lse_ref, all_lanes))
            dl_t = jnp.transpose(mode.get(dl_ref, all_lanes))
            def scores(p, sl):
                q2, do2 = mode.get(q_ref, sl), mode.get(do_ref, sl)
                kcat = jnp.concatenate([mode.get(kp_ref, sl), mode.get(kc_ref, sl)], axis=0)
                vcat = jnp.concatenate([mode.get(vp_ref, sl), mode.get(vc_ref, sl)], axis=0)
                qq = jnp.concatenate([q2 * m_lo, q2 * m_hi], axis=0)
                dd = jnp.concatenate([do2 * m_lo, do2 * m_hi], axis=0)
                h0 = 2 * p
                lse2 = jnp.concatenate([lse_t[h0:h0 + 1, :], lse_t[h0 + 1:h0 + 2, :]], axis=1)
                dl2 = jnp.concatenate([dl_t[h0:h0 + 1, :], dl_t[h0 + 1:h0 + 2, :]], axis=1)
                p_t = jnp.exp(_nt(kcat, qq) + (bias2 - lse2))
                ds_t = p_t * (_nt(vcat, dd) - dl2)
                return qq, dd, kcat, p_t.astype(BF16), ds_t.astype(BF16)

            def grads(sl, qq, dd, kcat, pb, dsb):
                dkc = jnp.dot(dsb, qq, preferred_element_type=F32)
                dvc = jnp.dot(pb, dd, preferred_element_type=F32)
                dqb = _tn(dsb, kcat)
                dq2 = jnp.where(lo, dqb[:BLOCK], dqb[BLOCK:]) * ATTN_SCALE
                dk2 = ck[:, sl] + dkc[:BLOCK]
                dv2 = cv[:, sl] + dvc[:BLOCK]
                if not first:
                    dq2 = dq2 + mode.get(dqi_ref, sl).astype(F32)
                    dk2 = dk2 + mode.get(dki_ref, sl).astype(F32)
                    dv2 = dv2 + mode.get(dvi_ref, sl).astype(F32)
                mode.put(dq_ref, sl, dq2)
                mode.put(dk_ref, sl, dk2)
                mode.put(dv_ref, sl, dv2)
                ck[:, sl] = dkc[BLOCK:]
                cv[:, sl] = dvc[BLOCK:]

            pending = None
            for p, sl in enumerate(groups):
                nxt = scores(p, sl)
                if pending is not None:
                    grads(*pending)
                pending = (sl, *nxt)
            grads(*pending)

        @pl.when(n == nb)
        def _():
            for sl in groups:
                if first:
                    mode.put(dk_ref, sl, ck[:, sl])
                    mode.put(dv_ref, sl, cv[:, sl])
                else:
                    mode.put(dk_ref, sl, ck[:, sl] + mode.get(dki_ref, sl).astype(F32))
                    mode.put(dv_ref, sl, cv[:, sl] + mode.get(dvi_ref, sl).astype(F32))

    last = nb - 1
    cur = lambda w: mode.spec(w, "cur", last)
    prev = lambda w: mode.spec(w, "prev")
    ins = [q, k, k, v, v, d_o, lse, delta]
    specs = [cur(attn_w), prev(attn_w), cur(attn_w), prev(attn_w), cur(attn_w), cur(attn_w), cur(LANES), cur(LANES)]
    if not first:
        ins += list(run)
        specs += [cur(attn_w), prev(attn_w), prev(attn_w)]
    shp = jax.ShapeDtypeStruct((seq, attn_w) if nat else _perm_shape(seq, attn_w), BF16)
    return pl.pallas_call(
        body, name=f"attn_bwd_{name}", grid=(mode.residues, nb + 1),
        out_shape=(shp, shp, shp), in_specs=specs, out_specs=(cur(attn_w), prev(attn_w), prev(attn_w)),
        scratch_shapes=[pltpu.VMEM((BLOCK, attn_w), F32), pltpu.VMEM((BLOCK, attn_w), F32)],
        compiler_params=_params(("arbitrary", "arbitrary")),
    )(*ins)


def _shift_down(u, halo, k):
    rolled = pltpu.roll(u, k, 0)
    row = lax.broadcasted_iota(jnp.int32, halo.shape, 0)
    top = jnp.where(row < k, pltpu.roll(halo, k, 0), rolled[:SUBLANES])
    return jnp.concatenate([top, rolled[SUBLANES:]], axis=0)


def _shift_up(u, halo, k):
    rows = u.shape[0]
    rolled = pltpu.roll(u, rows - k, 0)
    row = lax.broadcasted_iota(jnp.int32, halo.shape, 0)
    bot = jnp.where(row >= SUBLANES - k, pltpu.roll(halo, SUBLANES - k, 0), rolled[rows - SUBLANES:])
    return jnp.concatenate([rolled[:rows - SUBLANES], bot], axis=0)


def tail(o, lse, ga, cz, x, tgt, w_out, g2, cw):
    seq, d_model = x.shape
    attn_w = o.shape[1]
    conv_w = cz.shape[1] // 4
    mix = attn_w + conv_w
    groups = _lane_groups(attn_w)
    tm = ROW_TILE
    nt = seq // tm
    hb = tm // SUBLANES

    def body(o_ref, l_ref, ga_ref, cz_ref, hz_ref, x_ref, t_ref, w_ref, g_ref, cw_ref,
             do_ref, dl_ref, dop_ref, dlp_ref, lp_ref, dga_ref, dcb_ref, dgc_ref, dcv_ref, e_ref,
             dw_ref, dg_ref, dcw_ref, loss_ref, stage):
        i = pl.program_id(0)

        @pl.when(i == 0)
        def _():
            dw_ref[...] = jnp.zeros_like(dw_ref)
            dg_ref[...] = jnp.zeros_like(dg_ref)
            dcw_ref[...] = jnp.zeros_like(dcw_ref)
            loss_ref[...] = jnp.zeros_like(loss_ref)

        ov, gav = o_ref[...], ga_ref[...]
        sig_a = _sigmoid(gav)
        silu_a = gav * sig_a
        attn_out = ov * silu_a
        ch, cb = cz_ref[:, 0:conv_w], cz_ref[:, conv_w:2 * conv_w]
        cc, gc = cz_ref[:, 2 * conv_w:3 * conv_w], cz_ref[:, 3 * conv_w:4 * conv_w]
        u = cc * ch
        uh = hz_ref[:, 2 * conv_w:3 * conv_w] * hz_ref[:, 0:conv_w]
        uh = jnp.where(i > 0, uh, 0.0)
        u1 = _shift_down(u, uh, 1)
        u2 = _shift_down(u, uh, 2)
        w0, w1, w2 = cw_ref[0:1, :], cw_ref[1:2, :], cw_ref[2:3, :]
        cvv = u2 * w0 + u1 * w1 + u * w2
        sig_c = _sigmoid(gc)
        silu_c = gc * sig_c
        bc = cb * cvv
        conv_out = bc * silu_c
        mixed = jnp.concatenate([attn_out, conv_out], axis=1).astype(BF16)

        yv = jnp.dot(mixed, w_ref[...], preferred_element_type=F32)
        r2 = lax.rsqrt(jnp.mean(yv * yv, axis=-1, keepdims=True) + NORM_EPS)
        yhat = yv * r2
        gv = g_ref[...]
        diff = (x_ref[...] + yhat * gv) - t_ref[...]
        loss_ref[...] += _rowgroup_sum(diff * diff)
        ev = diff * (1.0 / d_model)
        e_ref[...] = ev
        dg_ref[...] += _rowgroup_sum(ev * yhat)
        eg = ev * gv
        dy = (r2 * (eg - yhat * jnp.mean(eg * yhat, axis=-1, keepdims=True))).astype(BF16)
        dw_ref[...] += _tn(mixed, dy)
        dm = _nt(dy, w_ref[...])
        dma, dmc = dm[:, :attn_w], dm[:, attn_w:]

        dov = dma * silu_a
        do_ref[...] = dov.astype(BF16)
        dga_ref[...] = (dma * ov * (sig_a * (1.0 + gav * (1.0 - sig_a)))).astype(BF16)
        prod = dov * ov
        lane = lax.broadcasted_iota(jnp.int32, (tm, LANES), 1)
        lo = lane < HEAD_DIM
        dblk = jnp.zeros((tm, LANES), F32)
        for p, sl in enumerate(groups):
            pr = prod[:, sl]
            dblk = jnp.where(lane == 2 * p, jnp.sum(jnp.where(lo, pr, 0.0), axis=1, keepdims=True), dblk)
            dblk = jnp.where(lane == 2 * p + 1, jnp.sum(jnp.where(lo, 0.0, pr), axis=1, keepdims=True), dblk)
            _stage_put(stage, p, dov[:, sl])
        dl_ref[...] = dblk
        for p, sl in enumerate(groups):
            _to_perm(stage, p, dop_ref, sl, BF16)
        all_lanes = slice(0, LANES)
        _stage_put(stage, 0, dblk)
        _stage_put(stage, 1, l_ref[...])
        _to_perm(stage, 0, dlp_ref, all_lanes, F32)
        _to_perm(stage, 1, lp_ref, all_lanes, F32)

        dsc = dmc * silu_c
        dcb_ref[...] = (dsc * cvv).astype(BF16)
        dgc_ref[...] = (dmc * bc * (sig_c * (1.0 + gc * (1.0 - sig_c)))).astype(BF16)
        dcv = dsc * cb
        dcv_ref[...] = dcv
        dcw_ref[0:SUBLANES, :] += _rowgroup_sum(dcv * u2)
        dcw_ref[SUBLANES:2 * SUBLANES, :] += _rowgroup_sum(dcv * u1)
        dcw_ref[2 * SUBLANES:3 * SUBLANES, :] += _rowgroup_sum(dcv * u)

    row = lambda n: pl.BlockSpec((tm, n), lambda i: (i, 0))
    whole = lambda a, b: pl.BlockSpec((a, b), lambda i: (0, 0))
    return pl.pallas_call(
        body, name="tail", grid=(nt,),
        out_shape=(jax.ShapeDtypeStruct((seq, attn_w), BF16), jax.ShapeDtypeStruct((seq, LANES), F32),
                   jax.ShapeDtypeStruct(_perm_shape(seq, attn_w), BF16), jax.ShapeDtypeStruct(_perm_shape(seq, LANES), F32),
                   jax.ShapeDtypeStruct(_perm_shape(seq, LANES), F32),
                   jax.ShapeDtypeStruct((seq, attn_w), BF16), jax.ShapeDtypeStruct((seq, conv_w), BF16),
                   jax.ShapeDtypeStruct((seq, conv_w), BF16), jax.ShapeDtypeStruct((seq, conv_w), F32),
                   jax.ShapeDtypeStruct((seq, d_model), F32), jax.ShapeDtypeStruct((mix, d_model), F32),
                   jax.ShapeDtypeStruct((SUBLANES, d_model), F32), jax.ShapeDtypeStruct((CONV_K * SUBLANES, conv_w), F32),
                   jax.ShapeDtypeStruct((SUBLANES, d_model), F32)),
        in_specs=[row(attn_w), row(LANES), row(attn_w), row(4 * conv_w),
                  pl.BlockSpec((SUBLANES, 4 * conv_w), lambda i: (jnp.maximum(i * hb - 1, 0), 0)),
                  row(d_model), row(d_model), _const_spec((mix, d_model)), _const_spec((1, d_model)),
                  _const_spec((SUBLANES, conv_w))],
        out_specs=(row(attn_w), row(LANES), _perm_tile_spec(attn_w, tm), _perm_tile_spec(LANES, tm), _perm_tile_spec(LANES, tm),
                   row(attn_w), row(conv_w), row(conv_w), row(conv_w), row(d_model),
                   whole(mix, d_model), whole(SUBLANES, d_model), whole(CONV_K * SUBLANES, conv_w),
                   whole(SUBLANES, d_model)),
        scratch_shapes=[pltpu.VMEM(_stage_shape(max(len(groups), 2), tm), F32)],
        compiler_params=_params(("arbitrary",)),
    )(o, lse, ga, cz, cz, x, tgt, w_out, g2, cw)


def dz_dx(nat_grads, perm_grads, dga, dcb, dgc, dcv, cz, tables, x, g1, e, w_full, cw):
    seq, d_model = x.shape
    attn_w = dga.shape[1]
    conv_w = dcv.shape[1]
    width = w_full.shape[2]
    in_w = 4 * attn_w + 4 * conv_w
    groups = _lane_groups(attn_w)
    tm = DZ_ROW_TILE
    nt = seq // tm
    hb = tm // SUBLANES

    def body(dq_ref, dk_ref, dv_ref, dqp_ref, dkp_ref, dvp_ref, dga_ref, dcb_ref, dgc_ref, dcv_ref, nh_ref, cz_ref,
             cos_ref, s1_ref, s2_ref, x_ref, g_ref, e_ref, w_ref, cw_ref, gx_ref, dz_ref, dg_ref, stage):
        i = pl.program_id(0)

        @pl.when(i == 0)
        def _():
            dg_ref[...] = jnp.zeros_like(dg_ref)

        cos, s1, s2 = cos_ref[...], s1_ref[...], s2_ref[...]
        for t, (nat_ref, perm_ref) in enumerate(((dq_ref, dqp_ref), (dk_ref, dkp_ref), (dv_ref, dvp_ref))):
            for g, sl in enumerate(groups):
                _from_perm(perm_ref, sl, stage, g)
            for g, sl in enumerate(groups):
                tot = nat_ref[:, sl].astype(F32) + _stage_get(stage, g)
                if t < 2:
                    tot = _rope_transposed(tot, cos, s1, s2)
                dz_ref[:, t * attn_w + g * LANES:t * attn_w + (g + 1) * LANES] = tot.astype(BF16)
        dz_ref[:, 3 * attn_w:4 * attn_w] = dga_ref[...]
        dcv = dcv_ref[...]
        nh = jnp.where(i < nt - 1, nh_ref[...], 0.0)
        w0, w1, w2 = cw_ref[0:1, :], cw_ref[1:2, :], cw_ref[2:3, :]
        du = dcv * w2 + _shift_up(dcv, nh, 1) * w1 + _shift_up(dcv, nh, 2) * w0
        base = 4 * attn_w
        dz_ref[:, base:base + conv_w] = (du * cz_ref[:, 2 * conv_w:3 * conv_w]).astype(BF16)
        dz_ref[:, base + conv_w:base + 2 * conv_w] = dcb_ref[...]
        dz_ref[:, base + 2 * conv_w:base + 3 * conv_w] = (du * cz_ref[:, 0:conv_w]).astype(BF16)
        dz_ref[:, base + 3 * conv_w:base + 4 * conv_w] = dgc_ref[...]

        dh = _nt(dz_ref[:, 0:width], w_ref[0])
        for j in range(1, N_CHIPS):
            dh = dh + _nt(dz_ref[:, j * width:(j + 1) * width], w_ref[j])
        xv = x_ref[...]
        r1 = lax.rsqrt(jnp.mean(xv * xv, axis=-1, keepdims=True) + NORM_EPS)
        xhat = xv * r1
        dg_ref[...] += _rowgroup_sum(dh * xhat)
        dhg = dh * g_ref[...]
        gx_ref[...] = r1 * (dhg - xhat * jnp.mean(dhg * xhat, axis=-1, keepdims=True)) + e_ref[...]

    row = lambda n: pl.BlockSpec((tm, n), lambda i: (i, 0))
    whole = lambda a, b: pl.BlockSpec((a, b), lambda i: (0, 0))
    pt = _perm_tile_spec(attn_w, tm)
    return pl.pallas_call(
        body, name="dz_dx", grid=(nt,),
        out_shape=(jax.ShapeDtypeStruct((seq, d_model), F32), jax.ShapeDtypeStruct((seq, in_w), BF16),
                   jax.ShapeDtypeStruct((SUBLANES, d_model), F32)),
        in_specs=[row(attn_w), row(attn_w), row(attn_w), pt, pt, pt, row(attn_w), row(conv_w), row(conv_w), row(conv_w),
                  pl.BlockSpec((SUBLANES, conv_w), lambda i: (jnp.minimum((i + 1) * hb, seq // SUBLANES - 1), 0)),
                  row(4 * conv_w), row(LANES), row(LANES), row(LANES), row(d_model), _const_spec((1, d_model)), row(d_model),
                  _const_spec(w_full.shape), _const_spec((SUBLANES, conv_w))],
        out_specs=(row(d_model), row(in_w), whole(SUBLANES, d_model)),
        scratch_shapes=[pltpu.VMEM(_stage_shape(len(groups), tm), F32)],
        compiler_params=_params(("arbitrary",)),
    )(*nat_grads, *perm_grads, dga, dcb, dgc, dcv, dcv, cz, *tables, x, g1, e, w_full, cw)


def dw_in(ht, dz):
    d_model, seq = ht.shape
    in_w = dz.shape[1]
    half = in_w // N_DEV
    ts = 512

    def body(ht_ref, dz_ref, o_ref):
        @pl.when(pl.program_id(0) == 0)
        def _():
            o_ref[...] = jnp.zeros_like(o_ref)

        for jb in range(N_DEV):
            o_ref[jb] += jnp.dot(ht_ref[...], dz_ref[:, jb * half:(jb + 1) * half], preferred_element_type=F32)

    return pl.pallas_call(
        body, name="dw_in", grid=(seq // ts,),
        out_shape=jax.ShapeDtypeStruct((N_DEV, d_model, half), F32),
        in_specs=[pl.BlockSpec((d_model, ts), lambda s: (0, s)), pl.BlockSpec((ts, in_w), lambda s: (s, 0))],
        out_specs=pl.BlockSpec((N_DEV, d_model, half), lambda s: (0, 0, 0)),
        compiler_params=_params(("arbitrary",)),
    )(ht, dz)


def grad_reduce(g_in, g_out, small):
    gi = g_in.reshape(N_CHIPS, 2, *g_in.shape[1:])
    go = g_out.reshape(N_CHIPS, 2, *g_out.shape[1:])
    shp_i, shp_o = gi.shape[2:], go.shape[2:]

    def body(gi_ref, go_ref, sm_ref, ri_ref, ro_ref, rs_ref, a_i, b_i, s_i, c_i, a_o, b_o, s_o, c_o, sbuf,
             loc_sems, sa, ra, sb, rb, sc, rc, ss, rs):
        x, y, c = lax.axis_index("x"), lax.axis_index("y"), lax.axis_index("c")
        me = 2 * x + y
        sib = (x, y, 1 - c)
        srcs, mine, theirs, staged, contrib, res = ((gi_ref, go_ref), (a_i, a_o), (b_i, b_o), (s_i, s_o), (c_i, c_o),
                                                    (ri_ref, ro_ref))

        flips = [(fx, fy, fc) for fx in (0, 1) for fy in (0, 1) for fc in (0, 1)][1:]
        my8 = 4 * x + 2 * y + c
        sbuf[my8] = sm_ref[...]

        def small_copy(k, slot, to):
            return pltpu.make_async_remote_copy(src_ref=sm_ref, dst_ref=sbuf.at[slot], send_sem=ss.at[k], recv_sem=rs.at[k],
                                                device_id=to, device_id_type=MESH)

        sends = []
        for k, (fx, fy, fc) in enumerate(flips):
            px, py, pc = _flip(x, fx), _flip(y, fy), _flip(c, fc)
            sends.append(small_copy(k, my8, (px, py, pc)))
            sends[-1].start()

        def a_copy(t, j):
            return pltpu.make_async_remote_copy(src_ref=srcs[t].at[j, 1 - c], dst_ref=theirs[t].at[j], send_sem=sa.at[t, j],
                                                recv_sem=ra.at[t, j], device_id=sib, device_id_type=MESH)

        peers = _chip_peers(x, y)
        order = [2 * px + py for px, py in peers] + [me]
        loads = [[pltpu.make_async_copy(srcs[t].at[j, c], mine[t].at[j], loc_sems.at[t, j]) for t in range(2)] for j in order]
        for pos, j in enumerate(order):
            for t in range(2):
                loads[pos][t].start()
                sends.append(a_copy(t, j))
                sends[-1].start()

        def b_copy(k, t, piece, slot, to):
            return pltpu.make_async_remote_copy(src_ref=staged[t].at[piece], dst_ref=contrib[t].at[slot], send_sem=sb.at[k, t],
                                                recv_sem=rb.at[k, t], device_id=to, device_id_type=MESH)

        for k, (px, py) in enumerate(peers):
            j = 2 * px + py
            for t in range(2):
                loads[k][t].wait()
                a_copy(t, j).wait_recv()
                staged[t][j] = (mine[t][j] + theirs[t][j]).astype(BF16)
                sends.append(b_copy(k, t, j, me, (px, py, c)))
                sends[-1].start()
        for t in range(2):
            loads[len(peers)][t].wait()
            a_copy(t, me).wait_recv()
            mine[t][me] = mine[t][me] + theirs[t][me]
            contrib[t][me] = mine[t][me].astype(BF16)
        for k, (px, py) in enumerate(peers):
            for t in range(2):
                b_copy(k, t, me, 2 * px + py, (px, py, c)).wait_recv()

        def c_copy(t, half):
            return pltpu.make_async_remote_copy(src_ref=res[t].at[half], dst_ref=res[t].at[half], send_sem=sc.at[t],
                                                recv_sem=rc.at[t], device_id=sib, device_id_type=MESH)

        for t in range(2):
            own = mine[t][me]
            term = lambda j: jnp.where(me == j, own, contrib[t][j].astype(F32))
            res[t][c] = ((term(0) + term(1)) + term(2)) + term(3)
            sends.append(c_copy(t, c))
            sends[-1].start()
        for t in range(2):
            c_copy(t, 1 - c).wait_recv()

        for k, (fx, fy, fc) in enumerate(flips):
            px, py, pc = _flip(x, fx), _flip(y, fy), _flip(c, fc)
            small_copy(k, 4 * px + 2 * py + pc, (px, py, pc)).wait_recv()
        tot = sbuf[0]
        for d in range(1, N_DEV):
            tot = tot + sbuf[d]
        rs_ref[...] = tot
        for cp in sends:
            cp.wait_send()

    vm = pl.BlockSpec(memory_space=pltpu.VMEM)
    anyspace = pl.BlockSpec(memory_space=pl.ANY)
    dma = pltpu.SemaphoreType.DMA
    bufs = lambda shp: [pltpu.VMEM((N_CHIPS, *shp), F32), pltpu.VMEM((N_CHIPS, *shp), F32),
                        pltpu.VMEM((N_CHIPS, *shp), BF16), pltpu.VMEM((N_CHIPS, *shp), BF16)]
    return pl.pallas_call(
        body, name="grad_reduce",
        out_shape=(jax.ShapeDtypeStruct((2, *shp_i), F32), jax.ShapeDtypeStruct((2, *shp_o), F32),
                   jax.ShapeDtypeStruct(small.shape, F32)),
        in_specs=[anyspace, anyspace, vm], out_specs=(vm, vm, vm),
        scratch_shapes=[*bufs(shp_i), *bufs(shp_o), pltpu.VMEM((N_DEV, *small.shape), F32),
                        dma((2, N_CHIPS)), dma((2, N_CHIPS)), dma((2, N_CHIPS)), dma((3, 2)), dma((3, 2)), dma((2,)), dma((2,)),
                        dma((N_DEV - 1,)), dma((N_DEV - 1,))],
        compiler_params=_params(),
    )(gi, go, small)


def _adam_math(w, g, m, v):
    m = ADAM_B1 * m + (1.0 - ADAM_B1) * g
    v = ADAM_B2 * v + (1.0 - ADAM_B2) * (g * g)
    m_hat = m / (1.0 - ADAM_B1 ** ADAM_STEP)
    v_hat = v / (1.0 - ADAM_B2 ** ADAM_STEP)
    delta = -ADAM_LR * (m_hat / (jnp.sqrt(v_hat) + ADAM_EPS) + ADAM_WD * w)
    return delta, m, v


def adam_shard(name, w, g2, m, v, block, grid, w_map, g_map):
    def body(w_ref, g_ref, m_ref, v_ref, go_ref, d_ref, mo_ref, vo_ref):
        g = g_ref[0]
        delta, mn, vn = _adam_math(w_ref[...], g, m_ref[...], v_ref[...])
        go_ref[...] = g
        d_ref[...] = delta
        mo_ref[...] = mn
        vo_ref[...] = vn

    ws = pl.BlockSpec(block, w_map)
    shp = jax.ShapeDtypeStruct(w.shape, F32)
    return pl.pallas_call(
        body, name=name, grid=grid, out_shape=(shp, shp, shp, shp),
        in_specs=[ws, pl.BlockSpec((1, *block), g_map), ws, ws], out_specs=(ws, ws, ws, ws),
        compiler_params=_params(("arbitrary",) * len(grid)),
    )(w, g2, m, v)


def adam_small(ws, gs, ms, vs):
    n = len(ws)

    def body(*refs):
        ins, outs = refs[:4 * n], refs[4 * n:]
        for t in range(n):
            delta, mn, vn = _adam_math(ins[t][...], ins[n + t][...], ins[2 * n + t][...], ins[3 * n + t][...])
            outs[3 * t][...] = delta
            outs[3 * t + 1][...] = mn
            outs[3 * t + 2][...] = vn

    vm = pl.BlockSpec(memory_space=pltpu.VMEM)
    outs = pl.pallas_call(
        body, name="adam_small",
        out_shape=tuple(jax.ShapeDtypeStruct(w.shape, F32) for w in ws for _ in range(3)),
        in_specs=[vm] * (4 * n), out_specs=tuple([vm] * (3 * n)),
        compiler_params=_params(),
    )(*ws, *gs, *ms, *vs)
    return [outs[3 * t:3 * t + 3] for t in range(n)]


def kernel(x, norm_pre_g, w_in, conv_w, w_out, norm_post_g, loss_target, m_norm_pre_g, m_w_in, m_conv_w, m_w_out, m_norm_post_g, v_norm_pre_g, v_w_in, v_conv_w, v_w_out, v_norm_post_g):
    _, seq, d_model = x.shape
    width = w_in.shape[1]
    conv_q = conv_w.shape[1]
    conv_width = N_CHIPS * conv_q
    attn_width = d_model - conv_width
    xs, tg = x[0], loss_target[0]
    g1, g2 = norm_pre_g.reshape(1, d_model), norm_post_g.reshape(1, d_model)

    w_full, wout_full, cw_full = gather_weights(w_in, w_out, conv_w)
    wout2 = wout_full.reshape(attn_width + conv_width, d_model)
    cw = jnp.zeros((SUBLANES, conv_width), F32).at[:CONV_K].set(
        cw_full[:, :CONV_K, :conv_q].transpose(1, 0, 2).reshape(CONV_K, conv_width))
    tables = _rope_tables(seq)

    ht, q, k, v, qp, kp, vp, ga, cz = inproj(xs, g1, w_full, tables, attn_width, conv_width)
    run = attn_fwd("p4", qp, kp, vp, None)
    run = attn_fwd("p16", qp, kp, vp, run)
    o, lse = attn_fwd("nat", q, k, v, run)
    (d_o, delta, d_op, delta_p, lse_p, dga, dcb, dgc, dcv, e, dwout, dg2, dcw, loss_acc) = tail(
        o, lse, ga, cz, xs, tg, wout2, g2, cw)
    nat_grads = attn_bwd("nat", q, k, v, d_o, lse, delta, None)
    perm_grads = attn_bwd("p4", qp, kp, vp, d_op, lse_p, delta_p, None)
    perm_grads = attn_bwd("p16", qp, kp, vp, d_op, lse_p, delta_p, perm_grads)
    grad_x, dz, dg1 = dz_dx(nat_grads, perm_grads, dga, dcb, dgc, dcv, cz, tables, xs, g1, e, w_full, cw)
    dwin = dw_in(ht, dz)

    small = jnp.zeros((SUBLANES, d_model), F32)
    small = small.at[0].set(dg1.sum(axis=0)).at[1].set(dg2.sum(axis=0))
    small = small.at[2:2 + CONV_K, :conv_width].set(dcw.reshape(CONV_K, SUBLANES, conv_width).sum(axis=1))
    small = small.at[2 + CONV_K, 0].set(jnp.sum(loss_acc))
    rin, rout, rsmall = grad_reduce(dwin, dwout.reshape(N_DEV, -1, d_model), small)

    half = width // 2
    tr = 256
    gw_in, d_in, m_in, v_in = adam_shard(
        "adam_w_in", w_in, rin, m_w_in, v_w_in, (tr, half), (2, d_model // tr),
        lambda hf, i: (i, hf), lambda hf, i: (hf, i, 0))
    rq = w_out.shape[0] // 2
    gw_out, d_out, m_out, v_out = adam_shard(
        "adam_w_out", w_out, rout, m_w_out, v_w_out, (rq, d_model), (2,),
        lambda hf: (hf, 0), lambda hf: (hf, 0, 0))

    chip = 2 * lax.axis_index("x") + lax.axis_index("y")
    g_pre, g_post = rsmall[0:1], rsmall[1:2]
    g_conv = lax.dynamic_slice(rsmall[2:2 + CONV_K, :conv_width], (0, chip * conv_q), (CONV_K, conv_q))
    (d_pre, m_pre, v_pre), (d_post, m_post, v_post), (d_cv, m_cv, v_cv) = adam_small(
        [g1, g2, conv_w], [g_pre, g_post, g_conv],
        [m_norm_pre_g.reshape(1, d_model), m_norm_post_g.reshape(1, d_model), m_conv_w],
        [v_norm_pre_g.reshape(1, d_model), v_norm_post_g.reshape(1, d_model), v_conv_w])

    loss = 0.5 * rsmall[2 + CONV_K, 0] / d_model
    vec = lambda a: a.reshape(d_model)
    return (loss, grad_x.reshape(1, seq, d_model),
            vec(g_pre), gw_in, g_conv, gw_out, vec(g_post),
            vec(d_pre), d_in, d_cv, d_out, vec(d_post),
            vec(m_pre), m_in, m_cv, m_out, vec(m_post),
            vec(v_pre), v_in, v_cv, v_out, vec(v_post))
```

```python
import jax
import jax.numpy as jnp
from jax import lax
from jax.experimental import pallas as pl
from jax.experimental.pallas import tpu as pltpu

HEAD_DIM = 64
LANES = 128
SUBLANES = 8
BLOCK = 128
WINDOW_KEYS = 128
PERM = 16
PJ = 4
P4_ROWS = BLOCK // PJ
ROW_TILE = 512
DZ_ROW_TILE = 512
CONV_K = 3
ROPE_THETA = 10000.0
NORM_EPS = 1e-6
ATTN_SCALE = HEAD_DIM ** -0.5
NEG = -1e30
N_CHIPS = 4
N_DEV = 8
MESH = pl.DeviceIdType.MESH
ADAM_LR = 0.001
ADAM_B1 = 0.9
ADAM_B2 = 0.999
ADAM_EPS = 1e-08
ADAM_WD = 0.01
ADAM_STEP = 10
VMEM_LIMIT = 52 * 1024 * 1024

F32 = jnp.float32
BF16 = jnp.bfloat16


def _params(sem=None, **kw):
    return pltpu.CompilerParams(dimension_semantics=sem, vmem_limit_bytes=VMEM_LIMIT, **kw)


def _const_spec(shape):
    return pl.BlockSpec(shape, lambda *_: (0,) * len(shape), pipeline_mode=pl.Buffered(1))


def _sigmoid(z):
    return 1.0 / (1.0 + jnp.exp(-z))


def _rowgroup_sum(a):
    rows, n = a.shape
    return a.reshape(rows // SUBLANES, SUBLANES, n).sum(axis=0)


def _nt(a, b):
    return lax.dot_general(a, b, (((1,), (1,)), ((), ())), preferred_element_type=F32)


def _tn(a, b):
    return lax.dot_general(a, b, (((0,), (0,)), ((), ())), preferred_element_type=F32)


def _col_pieces(a, b, width):
    out = []
    while a < b:
        j = a // width
        e = min(b, (j + 1) * width)
        out.append((j, a - j * width, e - j * width))
        a = e
    return out


def _lane_groups(width):
    return [slice(g * LANES, (g + 1) * LANES) for g in range(width // LANES)]


def _perm_shape(seq, width):
    return (PJ, PJ, seq // PERM, width)


def _perm_tile_spec(width, tm):
    return pl.BlockSpec((PJ, PJ, tm // PERM, width), lambda i: (0, 0, i, 0))


STAGE_PITCH = 24


def _stage_shape(groups, rows):
    return (groups, rows // PERM * STAGE_PITCH, LANES)


def _stage_put(stage, g, val, row0=0):
    for a in range(val.shape[0] // PERM):
        at = (row0 // PERM + a) * STAGE_PITCH
        stage[g, at:at + PERM, :] = val[a * PERM:(a + 1) * PERM]


def _stage_get(stage, g):
    return jnp.concatenate([stage[g, a * STAGE_PITCH:a * STAGE_PITCH + PERM, :]
                            for a in range(stage.shape[1] // STAGE_PITCH)], axis=0)


def _to_perm(stage, g, dst_ref, sl, dtype):
    rows = stage.shape[1] // STAGE_PITCH
    for b in range(PERM):
        dst_ref[b // PJ, b % PJ, :, sl] = stage[g, pl.ds(b, rows, stride=STAGE_PITCH), :].astype(dtype)


def _from_perm(src_ref, sl, stage, g):
    rows = stage.shape[1] // STAGE_PITCH
    for b in range(PERM):
        stage[g, pl.ds(b, rows, stride=STAGE_PITCH), :] = src_ref[b // PJ, b % PJ, :, sl].astype(F32)


def _flip(a, f):
    return 1 - a if f else a


def _chip_peers(x, y):
    return [(1 - x, y), (x, 1 - y), (1 - x, 1 - y)]


def gather_weights(w_in, w_out, conv_w):
    d_model, width = w_in.shape
    rows = w_out.shape[0]
    cw = jnp.zeros((SUBLANES, LANES), F32).at[:CONV_K, :conv_w.shape[1]].set(conv_w)

    def body(win_ref, wout_ref, cw_ref, winf_ref, woutf_ref, cwf_ref, st_in, st_out, ici_send, ici_recv, d2d_send, d2d_recv):
        x, y, c = lax.axis_index("x"), lax.axis_index("y"), lax.axis_index("c")
        me = 2 * x + y
        sib = (x, y, 1 - c)
        st_in[...] = win_ref[...].astype(BF16)
        st_out[...] = wout_ref[...].astype(BF16)
        winf_ref[me] = st_in[...]
        woutf_ref[me] = st_out[...]
        cwf_ref[me] = cw_ref[...]
        stages = (st_in, st_out)
        fulls = (winf_ref, woutf_ref)
        halves = (d_model // 2, rows // 2)

        def half(t, core):
            return pl.ds(pl.multiple_of(core * halves[t], halves[t]), halves[t])

        def ici(k, t, slot, to, core):
            src = stages[t].at[half(t, core)] if t < 2 else cw_ref
            dst = fulls[t].at[slot, half(t, core)] if t < 2 else cwf_ref.at[slot]
            return pltpu.make_async_remote_copy(src_ref=src, dst_ref=dst, send_sem=ici_send.at[k, t], recv_sem=ici_recv.at[k, t],
                                                device_id=to, device_id_type=MESH)

        def d2d(k, t, slot, core):
            ref = fulls[t].at[slot, half(t, core)]
            return pltpu.make_async_remote_copy(src_ref=ref, dst_ref=ref, send_sem=d2d_send.at[k, t], recv_sem=d2d_recv.at[k, t],
                                                device_id=sib, device_id_type=MESH)

        peers = _chip_peers(x, y)
        sends = [ici(k, t, me, (px, py, c), c) for k, (px, py) in enumerate(peers) for t in range(3)]
        for cp in sends:
            cp.start()
        for k, (px, py) in enumerate(peers):
            for t in range(2):
                ici(k, t, 2 * px + py, (px, py, c), c).wait_recv()
                fwd = d2d(k, t, 2 * px + py, c)
                fwd.start()
                sends.append(fwd)
            ici(k, 2, 2 * px + py, (px, py, c), c).wait_recv()
        for k, (px, py) in enumerate(peers):
            for t in range(2):
                d2d(k, t, 2 * px + py, 1 - c).wait_recv()
        for cp in sends:
            cp.wait_send()

    vm = pl.BlockSpec(memory_space=pltpu.VMEM)
    dma = pltpu.SemaphoreType.DMA
    return pl.pallas_call(
        body, name="gather_weights",
        out_shape=(jax.ShapeDtypeStruct((N_CHIPS, d_model, width), BF16),
                   jax.ShapeDtypeStruct((N_CHIPS, rows, d_model), BF16),
                   jax.ShapeDtypeStruct((N_CHIPS, SUBLANES, LANES), F32)),
        in_specs=[vm, vm, vm], out_specs=(vm, vm, vm),
        scratch_shapes=[pltpu.VMEM((d_model, width), BF16), pltpu.VMEM((rows, d_model), BF16),
                        dma((3, 3)), dma((3, 3)), dma((3, 2)), dma((3, 2))],
        compiler_params=_params(),
    )(w_in, w_out, cw)


def _rope_tables(seq):
    half = HEAD_DIM // 2
    inv_freq = ROPE_THETA ** (-jnp.arange(half, dtype=F32) * 2.0 / HEAD_DIM)
    ang = jnp.arange(seq).astype(F32)[:, None] * jnp.tile(inv_freq, LANES // half)[None, :]
    first_half = (jnp.arange(LANES) % HEAD_DIM < half)[None, :]
    sin = jnp.sin(ang)
    return jnp.cos(ang), jnp.where(first_half, -sin, 0.0), jnp.where(first_half, 0.0, sin)


def _rope(t, cos, s1, s2):
    return t * cos + pltpu.roll(t, LANES - HEAD_DIM // 2, 1) * s1 + pltpu.roll(t, HEAD_DIM // 2, 1) * s2


def _rope_transposed(g, cos, s1, s2):
    return g * cos + pltpu.roll(g * s1, HEAD_DIM // 2, 1) + pltpu.roll(g * s2, LANES - HEAD_DIM // 2, 1)


def inproj(x, g1, w_full, tables, attn_w, conv_w):
    seq, d_model = x.shape
    width = w_full.shape[2]
    tm = ROW_TILE
    groups = _lane_groups(attn_w)

    def body(x_ref, g_ref, w_ref, cos_ref, s1_ref, s2_ref,
             ht_ref, q_ref, k_ref, v_ref, qp_ref, kp_ref, vp_ref, ga_ref, cz_ref, stage):
        xv = x_ref[...]
        hb = ((xv * lax.rsqrt(jnp.mean(xv * xv, axis=-1, keepdims=True) + NORM_EPS)) * g_ref[...]).astype(BF16)
        ht_ref[...] = jnp.transpose(hb)
        cos, s1, s2 = cos_ref[...], s1_ref[...], s2_ref[...]

        def proj(a, b):
            parts = [jnp.dot(hb, w_ref[j, :, lo:hi], preferred_element_type=F32) for j, lo, hi in _col_pieces(a, b, width)]
            return parts[0] if len(parts) == 1 else jnp.concatenate(parts, axis=1)

        def emit(z, nat_ref, perm_ref, fn):
            for g, sl in enumerate(groups):
                val = fn(z[:, sl])
                nat_ref[:, sl] = val.astype(BF16)
                _stage_put(stage, g, val)
            for g, sl in enumerate(groups):
                _to_perm(stage, g, perm_ref, sl, BF16)

        emit(proj(0, attn_w), q_ref, qp_ref, lambda t: _rope(t, cos, s1, s2) * ATTN_SCALE)
        emit(proj(attn_w, 2 * attn_w), k_ref, kp_ref, lambda t: _rope(t, cos, s1, s2))
        emit(proj(2 * attn_w, 3 * attn_w), v_ref, vp_ref, lambda t: t)
        ga_ref[...] = proj(3 * attn_w, 4 * attn_w)
        cz_ref[...] = proj(4 * attn_w, 4 * attn_w + 4 * conv_w)

    row = lambda n: pl.BlockSpec((tm, n), lambda i: (i, 0))
    nat = jax.ShapeDtypeStruct((seq, attn_w), BF16)
    perm = jax.ShapeDtypeStruct(_perm_shape(seq, attn_w), BF16)
    return pl.pallas_call(
        body, name="inproj", grid=(seq // tm,),
        out_shape=(jax.ShapeDtypeStruct((d_model, seq), BF16), nat, nat, nat, perm, perm, perm,
                   jax.ShapeDtypeStruct((seq, attn_w), F32), jax.ShapeDtypeStruct((seq, 4 * conv_w), F32)),
        in_specs=[row(d_model), _const_spec((1, d_model)), _const_spec(w_full.shape), row(LANES), row(LANES), row(LANES)],
        out_specs=(pl.BlockSpec((d_model, tm), lambda i: (0, i)), row(attn_w), row(attn_w), row(attn_w),
                   _perm_tile_spec(attn_w, tm), _perm_tile_spec(attn_w, tm), _perm_tile_spec(attn_w, tm),
                   row(attn_w), row(4 * conv_w)),
        scratch_shapes=[pltpu.VMEM(_stage_shape(len(groups), tm), F32)],
        compiler_params=_params(("arbitrary",)),
    )(x, g1, w_full, *tables)


class _Mode:
    def __init__(self, name, seq):
        self.name = name
        if name == "nat":
            self.residues, self.nb = 1, seq // BLOCK
        elif name == "p16":
            self.residues, self.nb = PERM, seq // PERM // BLOCK
        else:
            self.residues, self.nb = PJ, seq // PERM // P4_ROWS

    def spec(self, width, which, last=None):
        if which == "prev":
            blk = lambda n: jnp.maximum(n - 1, 0)
        elif last is None:
            blk = lambda n: n
        else:
            blk = lambda n: jnp.minimum(n, last)
        if self.name == "nat":
            return pl.BlockSpec((BLOCK, width), lambda r, n: (blk(n), 0))
        if self.name == "p16":
            return pl.BlockSpec((1, 1, BLOCK, width), lambda r, n: (r // PJ, r % PJ, blk(n), 0))
        return pl.BlockSpec((PJ, 1, P4_ROWS, width), lambda r, n: (0, r, blk(n), 0))

    def get(self, ref, sl):
        if self.name == "nat":
            return ref[:, sl]
        if self.name == "p16":
            return ref[0, 0, :, sl]
        return jnp.concatenate([ref[j, 0, :, sl] for j in range(PJ)], axis=0)

    def put(self, ref, sl, val):
        val = val.astype(ref.dtype)
        if self.name == "nat":
            ref[:, sl] = val
        elif self.name == "p16":
            ref[0, 0, :, sl] = val
        else:
            for j in range(PJ):
                ref[j, 0, :, sl] = val[j * P4_ROWS:(j + 1) * P4_ROWS]

    def index(self, idx, is_key):
        if self.name != "p4":
            return idx - BLOCK if is_key else idx
        within = jnp.bitwise_and(idx, BLOCK - 1)
        m = PJ * jnp.bitwise_and(within, P4_ROWS - 1) + jnp.right_shift(within, P4_ROWS.bit_length() - 1)
        return m + BLOCK * (jnp.right_shift(idx, BLOCK.bit_length() - 1) - 1) if is_key else m

    def bias(self, n, keys_major):
        shape = (2 * BLOCK, BLOCK) if keys_major else (BLOCK, 2 * BLOCK)
        kdim = 0 if keys_major else 1
        kidx = lax.broadcasted_iota(jnp.int32, shape, kdim)
        qidx = lax.broadcasted_iota(jnp.int32, shape, 1 - kdim)
        rel = self.index(qidx, False) - self.index(kidx, True)
        valid = (rel >= 0) & (rel <= WINDOW_KEYS) & ((kidx >= BLOCK) | (n > 0))
        return jnp.where(valid, 0.0, NEG)


def _head_masks():
    lane = lax.broadcasted_iota(jnp.int32, (BLOCK, LANES), 1)
    lo = lane < HEAD_DIM
    return lane, lo, jnp.where(lo, 1.0, 0.0).astype(BF16), jnp.where(lo, 0.0, 1.0).astype(BF16)


def _column(blk, lane, h):
    return jnp.sum(jnp.where(lane == h, blk, 0.0), axis=1, keepdims=True)


def attn_fwd(name, q, k, v, run):
    nat = name == "nat"
    seq = q.shape[0] if nat else q.shape[2] * PERM
    attn_w = q.shape[-1]
    mode = _Mode(name, seq)
    groups = _lane_groups(attn_w)
    first = run is None
    all_lanes = slice(0, LANES)

    def body(*refs):
        q_ref, kp_ref, kc_ref, vp_ref, vc_ref = refs[:5]
        if first:
            o_ref, l_ref = refs[5:]
        elif nat:
            oin_ref, lin_ref, ex_ref, o_ref, l_ref, ostage, lstage = refs[5:]
        else:
            oin_ref, lin_ref, ex_ref, o_ref, l_ref = refs[5:]
        n = pl.program_id(1)
        bias = mode.bias(n, True)
        bias2 = jnp.concatenate([bias, bias], axis=1)
        _, lo, m_lo, m_hi = _head_masks()
        head_row = lax.broadcasted_iota(jnp.int32, (BLOCK, LANES), 0)
        lrows = jnp.zeros((BLOCK, LANES), F32)
        def probs(sl):
            q2 = mode.get(q_ref, sl)
            kcat = jnp.concatenate([mode.get(kp_ref, sl), mode.get(kc_ref, sl)], axis=0)
            vcat = jnp.concatenate([mode.get(vp_ref, sl), mode.get(vc_ref, sl)], axis=0)
            qq = jnp.concatenate([q2 * m_lo, q2 * m_hi], axis=0)
            s_t = _nt(kcat, qq) + bias2
            m = jnp.max(s_t, axis=0, keepdims=True)
            pe = jnp.exp(s_t - m)
            l = jnp.sum(pe, axis=0, keepdims=True)
            return vcat, (pe * (1.0 / l)).astype(BF16), m + jnp.log(l)

        def output(p, sl, vcat, pn, lse, lrows):
            o_new = _tn(pn, vcat)
            mode.put(o_ref, sl, jnp.where(lo, o_new[:BLOCK], o_new[BLOCK:]))
            lrows = jnp.where(head_row == 2 * p, lse[:, :BLOCK], lrows)
            return jnp.where(head_row == 2 * p + 1, lse[:, BLOCK:], lrows)

        pending = None
        for p, sl in enumerate(groups):
            nxt = probs(sl)
            if pending is not None:
                lrows = output(*pending, lrows)
            pending = (p, sl, *nxt)
        lrows = output(*pending, lrows)
        lblk = jnp.transpose(lrows)
        if first:
            mode.put(l_ref, all_lanes, lblk)
        else:
            if nat:
                for g, sl in enumerate(groups):
                    _from_perm(oin_ref, sl, ostage, g)
                _from_perm(lin_ref, all_lanes, lstage, 0)
                lin = _stage_get(lstage, 0)
            else:
                lin = mode.get(lin_ref, all_lanes)
            mx = jnp.maximum(lin, lblk)
            new = mx + jnp.log(jnp.exp(lin - mx) + jnp.exp(lblk - mx))
            mode.put(l_ref, all_lanes, new)

            def expand(w):
                hi = w.astype(BF16)
                rest = (w - hi.astype(F32)).astype(BF16)
                ex = ex_ref[...]
                return jnp.dot(hi, ex, preferred_element_type=F32) + jnp.dot(rest, ex, preferred_element_type=F32)

            w_prev, w_cur = expand(jnp.exp(lin - new)), expand(jnp.exp(lblk - new))
            for p, sl in enumerate(groups):
                o_prev = _stage_get(ostage, p) if nat else mode.get(oin_ref, sl)
                mode.put(o_ref, sl, w_prev[:, sl] * o_prev + w_cur[:, sl] * mode.get(o_ref, sl))

    ins = [q, k, k, v, v]
    specs = [mode.spec(attn_w, "cur"), mode.spec(attn_w, "prev"), mode.spec(attn_w, "cur"),
             mode.spec(attn_w, "prev"), mode.spec(attn_w, "cur")]
    scratch = []
    if not first:
        ins += list(run)
        if nat:
            rows8 = BLOCK // PERM
            specs += [pl.BlockSpec((PJ, PJ, rows8, attn_w), lambda r, n: (0, 0, n, 0)),
                      pl.BlockSpec((PJ, PJ, rows8, LANES), lambda r, n: (0, 0, n, 0))]
            scratch = [pltpu.VMEM(_stage_shape(len(groups), BLOCK), F32), pltpu.VMEM(_stage_shape(1, BLOCK), F32)]
        else:
            specs += [mode.spec(attn_w, "cur"), mode.spec(LANES, "cur")]
        head_of_lane = jnp.arange(attn_w, dtype=jnp.int32) // HEAD_DIM
        ins.append((jnp.arange(LANES, dtype=jnp.int32)[:, None] == head_of_lane[None, :]).astype(BF16))
        specs.append(_const_spec((LANES, attn_w)))
    if nat:
        out_shape = (jax.ShapeDtypeStruct((seq, attn_w), F32), jax.ShapeDtypeStruct((seq, LANES), F32))
    else:
        out_shape = (jax.ShapeDtypeStruct(_perm_shape(seq, attn_w), F32), jax.ShapeDtypeStruct(_perm_shape(seq, LANES), F32))
    return pl.pallas_call(
        body, name=f"attn_fwd_{name}", grid=(mode.residues, mode.nb),
        out_shape=out_shape, in_specs=specs, out_specs=(mode.spec(attn_w, "cur"), mode.spec(LANES, "cur")),
        scratch_shapes=scratch,
        compiler_params=_params(("arbitrary", "arbitrary")),
    )(*ins)


def attn_bwd(name, q, k, v, d_o, lse, delta, run):
    nat = name == "nat"
    seq = q.shape[0] if nat else q.shape[2] * PERM
    attn_w = q.shape[-1]
    mode = _Mode(name, seq)
    nb = mode.nb
    groups = _lane_groups(attn_w)
    first = run is None
    all_lanes = slice(0, LANES)

    def body(*refs):
        q_ref, kp_ref, kc_ref, vp_ref, vc_ref, do_ref, lse_ref, dl_ref = refs[:8]
        if first:
            dq_ref, dk_ref, dv_ref, ck, cv = refs[8:]
        else:
            dqi_ref, dki_ref, dvi_ref, dq_ref, dk_ref, dv_ref, ck, cv = refs[8:]
        n = pl.program_id(1)

        @pl.when(n == 0)
        def _():
            ck[...] = jnp.zeros_like(ck)
            cv[...] = jnp.zeros_like(cv)

        @pl.when(n < nb)
        def _():
            bias = mode.bias(n, True)
            bias2 = jnp.concatenate([bias, bias], axis=1)
            _, lo, m_lo, m_hi = _head_masks()
            lse_t = jnp.transpose(mode.get(lse_ref, all_lanes))
            dl_t = jnp.transpose(mode.get(dl_ref, all_lanes))
            def scores(p, sl):
                q2, do2 = mode.get(q_ref, sl), mode.get(do_ref, sl)
                kcat = jnp.concatenate([mode.get(kp_ref, sl), mode.get(kc_ref, sl)], axis=0)
                vcat = jnp.concatenate([mode.get(vp_ref, sl), mode.get(vc_ref, sl)], axis=0)
                qq = jnp.concatenate([q2 * m_lo, q2 * m_hi], axis=0)
                dd = jnp.concatenate([do2 * m_lo, do2 * m_hi], axis=0)
                h0 = 2 * p
                lse2 = jnp.concatenate([lse_t[h0:h0 + 1, :], lse_t[h0 + 1:h0 + 2, :]], axis=1)
                dl2 = jnp.concatenate([dl_t[h0:h0 + 1, :], dl_t[h0 + 1:h0 + 2, :]], axis=1)
                p_t = jnp.exp(_nt(kcat, qq) + (bias2 - lse2))
                ds_t = p_t * (_nt(vcat, dd) - dl2)
                return qq, dd, kcat, p_t.astype(BF16), ds_t.astype(BF16)

            def grads(sl, qq, dd, kcat, pb, dsb):
                dkc = jnp.dot(dsb, qq, preferred_element_type=F32)
                dvc = jnp.dot(pb, dd, preferred_element_type=F32)
                dqb = _tn(dsb, kcat)
                dq2 = jnp.where(lo, dqb[:BLOCK], dqb[BLOCK:]) * ATTN_SCALE
                dk2 = ck[:, sl] + dkc[:BLOCK]
                dv2 = cv[:, sl] + dvc[:BLOCK]
                if not first:
                    dq2 = dq2 + mode.get(dqi_ref, sl).astype(F32)
                    dk2 = dk2 + mode.get(dki_ref, sl).astype(F32)
                    dv2 = dv2 + mode.get(dvi_ref, sl).astype(F32)
                mode.put(dq_ref, sl, dq2)
                mode.put(dk_ref, sl, dk2)
                mode.put(dv_ref, sl, dv2)
                ck[:, sl] = dkc[BLOCK:]
                cv[:, sl] = dvc[BLOCK:]

            pending = None
            for p, sl in enumerate(groups):
                nxt = scores(p, sl)
                if pending is not None:
                    grads(*pending)
                pending = (sl, *nxt)
            grads(*pending)

        @pl.when(n == nb)
        def _():
            for sl in groups:
                if first:
                    mode.put(dk_ref, sl, ck[:, sl])
                    mode.put(dv_ref, sl, cv[:, sl])
                else:
                    mode.put(dk_ref, sl, ck[:, sl] + mode.get(dki_ref, sl).astype(F32))
                    mode.put(dv_ref, sl, cv[:, sl] + mode.get(dvi_ref, sl).astype(F32))

    last = nb - 1
    cur = lambda w: mode.spec(w, "cur", last)
    prev = lambda w: mode.spec(w, "prev")
    ins = [q, k, k, v, v, d_o, lse, delta]
    specs = [cur(attn_w), prev(attn_w), cur(attn_w), prev(attn_w), cur(attn_w), cur(attn_w), cur(LANES), cur(LANES)]
    if not first:
        ins += list(run)
        specs += [cur(attn_w), prev(attn_w), prev(attn_w)]
    shp = jax.ShapeDtypeStruct((seq, attn_w) if nat else _perm_shape(seq, attn_w), BF16)
    return pl.pallas_call(
        body, name=f"attn_bwd_{name}", grid=(mode.residues, nb + 1),
        out_shape=(shp, shp, shp), in_specs=specs, out_specs=(cur(attn_w), prev(attn_w), prev(attn_w)),
        scratch_shapes=[pltpu.VMEM((BLOCK, attn_w), F32), pltpu.VMEM((BLOCK, attn_w), F32)],
        compiler_params=_params(("arbitrary", "arbitrary")),
    )(*ins)


def _shift_down(u, halo, k):
    rolled = pltpu.roll(u, k, 0)
    row = lax.broadcasted_iota(jnp.int32, halo.shape, 0)
    top = jnp.where(row < k, pltpu.roll(halo, k, 0), rolled[:SUBLANES])
    return jnp.concatenate([top, rolled[SUBLANES:]], axis=0)


def _shift_up(u, halo, k):
    rows = u.shape[0]
    rolled = pltpu.roll(u, rows - k, 0)
    row = lax.broadcasted_iota(jnp.int32, halo.shape, 0)
    bot = jnp.where(row >= SUBLANES - k, pltpu.roll(halo, SUBLANES - k, 0), rolled[rows - SUBLANES:])
    return jnp.concatenate([rolled[:rows - SUBLANES], bot], axis=0)


def tail(o, lse, ga, cz, x, tgt, w_out, g2, cw):
    seq, d_model = x.shape
    attn_w = o.shape[1]
    conv_w = cz.shape[1] // 4
    mix = attn_w + conv_w
    groups = _lane_groups(attn_w)
    tm = ROW_TILE
    nt = seq // tm
    hb = tm // SUBLANES

    def body(o_ref, l_ref, ga_ref, cz_ref, hz_ref, x_ref, t_ref, w_ref, g_ref, cw_ref,
             do_ref, dl_ref, dop_ref, dlp_ref, lp_ref, dga_ref, dcb_ref, dgc_ref, dcv_ref, e_ref,
             dw_ref, dg_ref, dcw_ref, loss_ref, stage):
        i = pl.program_id(0)

        @pl.when(i == 0)
        def _():
            dw_ref[...] = jnp.zeros_like(dw_ref)
            dg_ref[...] = jnp.zeros_like(dg_ref)
            dcw_ref[...] = jnp.zeros_like(dcw_ref)
            loss_ref[...] = jnp.zeros_like(loss_ref)

        u = cz_ref[:, 2 * conv_w:3 * conv_w] * cz_ref[:, 0:conv_w]
        uh = hz_ref[:, 2 * conv_w:3 * conv_w] * hz_ref[:, 0:conv_w]
        uh = jnp.where(i > 0, uh, 0.0)
        u1 = _shift_down(u, uh, 1)
        u2 = _shift_down(u, uh, 2)
        w0, w1, w2 = cw_ref[0:1, :], cw_ref[1:2, :], cw_ref[2:3, :]
        cvv = u2 * w0 + u1 * w1 + u * w2
        gv = g_ref[...]
        all_lanes = slice(0, LANES)

        def forward(rs):
            ov, gav = o_ref[rs, :], ga_ref[rs, :]
            sig_a = _sigmoid(gav)
            silu_a = gav * sig_a
            cb, gc = cz_ref[rs, conv_w:2 * conv_w], cz_ref[rs, 3 * conv_w:4 * conv_w]
            sig_c = _sigmoid(gc)
            silu_c = gc * sig_c
            bc = cb * cvv[rs]
            mixed = jnp.concatenate([ov * silu_a, bc * silu_c], axis=1).astype(BF16)
            yv = jnp.dot(mixed, w_ref[...], preferred_element_type=F32)
            return ov, gav, sig_a, silu_a, cb, gc, sig_c, silu_c, bc, mixed, yv

        def loss_and_dy(rs, mixed, yv):
            r2 = lax.rsqrt(jnp.mean(yv * yv, axis=-1, keepdims=True) + NORM_EPS)
            yhat = yv * r2
            diff = (x_ref[rs, :] + yhat * gv) - t_ref[rs, :]
            loss_ref[...] += _rowgroup_sum(diff * diff)
            ev = diff * (1.0 / d_model)
            e_ref[rs, :] = ev
            dg_ref[...] += _rowgroup_sum(ev * yhat)
            eg = ev * gv
            dy = (r2 * (eg - yhat * jnp.mean(eg * yhat, axis=-1, keepdims=True))).astype(BF16)
            dw_ref[...] += _tn(mixed, dy)
            return _nt(dy, w_ref[...])

        def backward(rs, ov, gav, sig_a, silu_a, cb, gc, sig_c, silu_c, bc, dm):
            rows = rs.stop - rs.start
            dma, dmc = dm[:, :attn_w], dm[:, attn_w:]
            dov = dma * silu_a
            do_ref[rs, :] = dov.astype(BF16)
            dga_ref[rs, :] = (dma * ov * (sig_a * (1.0 + gav * (1.0 - sig_a)))).astype(BF16)
            prod = dov * ov
            lane = lax.broadcasted_iota(jnp.int32, (rows, LANES), 1)
            lo = lane < HEAD_DIM
            dblk = jnp.zeros((rows, LANES), F32)
            for p, sl in enumerate(groups):
                pr = prod[:, sl]
                dblk = jnp.where(lane == 2 * p, jnp.sum(jnp.where(lo, pr, 0.0), axis=1, keepdims=True), dblk)
                dblk = jnp.where(lane == 2 * p + 1, jnp.sum(jnp.where(lo, 0.0, pr), axis=1, keepdims=True), dblk)
                _stage_put(stage, p, dov[:, sl], rs.start)
            dl_ref[rs, :] = dblk
            _stage_put(stage, len(groups), dblk, rs.start)
            _stage_put(stage, len(groups) + 1, l_ref[rs, :], rs.start)
            dsc = dmc * silu_c
            cv_rows = cvv[rs]
            dcb_ref[rs, :] = (dsc * cv_rows).astype(BF16)
            dgc_ref[rs, :] = (dmc * bc * (sig_c * (1.0 + gc * (1.0 - sig_c)))).astype(BF16)
            dcv = dsc * cb
            dcv_ref[rs, :] = dcv
            dcw_ref[0:SUBLANES, :] += _rowgroup_sum(dcv * u2[rs])
            dcw_ref[SUBLANES:2 * SUBLANES, :] += _rowgroup_sum(dcv * u1[rs])
            dcw_ref[2 * SUBLANES:3 * SUBLANES, :] += _rowgroup_sum(dcv * u[rs])

        halves = [slice(0, tm // 2), slice(tm // 2, tm)]
        fwd = [forward(rs) for rs in halves]
        dms = [loss_and_dy(rs, f[9], f[10]) for rs, f in zip(halves, fwd)]
        for rs, f, dm in zip(halves, fwd, dms):
            backward(rs, *f[:9], dm)
        for p, sl in enumerate(groups):
            _to_perm(stage, p, dop_ref, sl, BF16)
        _to_perm(stage, len(groups), dlp_ref, all_lanes, F32)
        _to_perm(stage, len(groups) + 1, lp_ref, all_lanes, F32)

    row = lambda n: pl.BlockSpec((tm, n), lambda i: (i, 0))
    whole = lambda a, b: pl.BlockSpec((a, b), lambda i: (0, 0))
    return pl.pallas_call(
        body, name="tail", grid=(nt,),
        out_shape=(jax.ShapeDtypeStruct((seq, attn_w), BF16), jax.ShapeDtypeStruct((seq, LANES), F32),
                   jax.ShapeDtypeStruct(_perm_shape(seq, attn_w), BF16), jax.ShapeDtypeStruct(_perm_shape(seq, LANES), F32),
                   jax.ShapeDtypeStruct(_perm_shape(seq, LANES), F32),
                   jax.ShapeDtypeStruct((seq, attn_w), BF16), jax.ShapeDtypeStruct((seq, conv_w), BF16),
                   jax.ShapeDtypeStruct((seq, conv_w), BF16), jax.ShapeDtypeStruct((seq, conv_w), F32),
                   jax.ShapeDtypeStruct((seq, d_model), F32), jax.ShapeDtypeStruct((mix, d_model), F32),
                   jax.ShapeDtypeStruct((SUBLANES, d_model), F32), jax.ShapeDtypeStruct((CONV_K * SUBLANES, conv_w), F32),
                   jax.ShapeDtypeStruct((SUBLANES, d_model), F32)),
        in_specs=[row(attn_w), row(LANES), row(attn_w), row(4 * conv_w),
                  pl.BlockSpec((SUBLANES, 4 * conv_w), lambda i: (jnp.maximum(i * hb - 1, 0), 0)),
                  row(d_model), row(d_model), _const_spec((mix, d_model)), _const_spec((1, d_model)),
                  _const_spec((SUBLANES, conv_w))],
        out_specs=(row(attn_w), row(LANES), _perm_tile_spec(attn_w, tm), _perm_tile_spec(LANES, tm), _perm_tile_spec(LANES, tm),
                   row(attn_w), row(conv_w), row(conv_w), row(conv_w), row(d_model),
                   whole(mix, d_model), whole(SUBLANES, d_model), whole(CONV_K * SUBLANES, conv_w),
                   whole(SUBLANES, d_model)),
        scratch_shapes=[pltpu.VMEM(_stage_shape(len(groups) + 2, tm), F32)],
        compiler_params=_params(("arbitrary",)),
    )(o, lse, ga, cz, cz, x, tgt, w_out, g2, cw)


def dz_dx(nat_grads, perm_grads, dga, dcb, dgc, dcv, cz, tables, x, g1, e, w_full, cw):
    seq, d_model = x.shape
    attn_w = dga.shape[1]
    conv_w = dcv.shape[1]
    width = w_full.shape[2]
    in_w = 4 * attn_w + 4 * conv_w
    groups = _lane_groups(attn_w)
    tm = DZ_ROW_TILE
    nt = seq // tm
    hb = tm // SUBLANES

    def body(dq_ref, dk_ref, dv_ref, dqp_ref, dkp_ref, dvp_ref, dga_ref, dcb_ref, dgc_ref, dcv_ref, nh_ref, cz_ref,
             cos_ref, s1_ref, s2_ref, x_ref, g_ref, e_ref, w_ref, cw_ref, gx_ref, dz_ref, dg_ref, stage):
        i = pl.program_id(0)

        @pl.when(i == 0)
        def _():
            dg_ref[...] = jnp.zeros_like(dg_ref)

        cos, s1, s2 = cos_ref[...], s1_ref[...], s2_ref[...]
        for t, (nat_ref, perm_ref) in enumerate(((dq_ref, dqp_ref), (dk_ref, dkp_ref), (dv_ref, dvp_ref))):
            for g, sl in enumerate(groups):
                _from_perm(perm_ref, sl, stage, g)
            for g, sl in enumerate(groups):
                tot = nat_ref[:, sl].astype(F32) + _stage_get(stage, g)
                if t < 2:
                    tot = _rope_transposed(tot, cos, s1, s2)
                dz_ref[:, t * attn_w + g * LANES:t * attn_w + (g + 1) * LANES] = tot.astype(BF16)
        dz_ref[:, 3 * attn_w:4 * attn_w] = dga_ref[...]
        dcv = dcv_ref[...]
        nh = jnp.where(i < nt - 1, nh_ref[...], 0.0)
        w0, w1, w2 = cw_ref[0:1, :], cw_ref[1:2, :], cw_ref[2:3, :]
        du = dcv * w2 + _shift_up(dcv, nh, 1) * w1 + _shift_up(dcv, nh, 2) * w0
        base = 4 * attn_w
        dz_ref[:, base:base + conv_w] = (du * cz_ref[:, 2 * conv_w:3 * conv_w]).astype(BF16)
        dz_ref[:, base + conv_w:base + 2 * conv_w] = dcb_ref[...]
        dz_ref[:, base + 2 * conv_w:base + 3 * conv_w] = (du * cz_ref[:, 0:conv_w]).astype(BF16)
        dz_ref[:, base + 3 * conv_w:base + 4 * conv_w] = dgc_ref[...]

        dh = _nt(dz_ref[:, 0:width], w_ref[0])
        for j in range(1, N_CHIPS):
            dh = dh + _nt(dz_ref[:, j * width:(j + 1) * width], w_ref[j])
        xv = x_ref[...]
        r1 = lax.rsqrt(jnp.mean(xv * xv, axis=-1, keepdims=True) + NORM_EPS)
        xhat = xv * r1
        dg_ref[...] += _rowgroup_sum(dh * xhat)
        dhg = dh * g_ref[...]
        gx_ref[...] = r1 * (dhg - xhat * jnp.mean(dhg * xhat, axis=-1, keepdims=True)) + e_ref[...]

    row = lambda n: pl.BlockSpec((tm, n), lambda i: (i, 0))
    whole = lambda a, b: pl.BlockSpec((a, b), lambda i: (0, 0))
    pt = _perm_tile_spec(attn_w, tm)
    return pl.pallas_call(
        body, name="dz_dx", grid=(nt,),
        out_shape=(jax.ShapeDtypeStruct((seq, d_model), F32), jax.ShapeDtypeStruct((seq, in_w), BF16),
                   jax.ShapeDtypeStruct((SUBLANES, d_model), F32)),
        in_specs=[row(attn_w), row(attn_w), row(attn_w), pt, pt, pt, row(attn_w), row(conv_w), row(conv_w), row(conv_w),
                  pl.BlockSpec((SUBLANES, conv_w), lambda i: (jnp.minimum((i + 1) * hb, seq // SUBLANES - 1), 0)),
                  row(4 * conv_w), row(LANES), row(LANES), row(LANES), row(d_model), _const_spec((1, d_model)), row(d_model),
                  _const_spec(w_full.shape), _const_spec((SUBLANES, conv_w))],
        out_specs=(row(d_model), row(in_w), whole(SUBLANES, d_model)),
        scratch_shapes=[pltpu.VMEM(_stage_shape(len(groups), tm), F32)],
        compiler_params=_params(("arbitrary",)),
    )(*nat_grads, *perm_grads, dga, dcb, dgc, dcv, dcv, cz, *tables, x, g1, e, w_full, cw)


def dw_in(ht, dz):
    d_model, seq = ht.shape
    in_w = dz.shape[1]
    half = in_w // N_DEV
    ts = 512

    def body(ht_ref, dz_ref, o_ref):
        @pl.when(pl.program_id(0) == 0)
        def _():
            o_ref[...] = jnp.zeros_like(o_ref)

        for jb in range(N_DEV):
            o_ref[jb] += jnp.dot(ht_ref[...], dz_ref[:, jb * half:(jb + 1) * half], preferred_element_type=F32)

    return pl.pallas_call(
        body, name="dw_in", grid=(seq // ts,),
        out_shape=jax.ShapeDtypeStruct((N_DEV, d_model, half), F32),
        in_specs=[pl.BlockSpec((d_model, ts), lambda s: (0, s)), pl.BlockSpec((ts, in_w), lambda s: (s, 0))],
        out_specs=pl.BlockSpec((N_DEV, d_model, half), lambda s: (0, 0, 0)),
        compiler_params=_params(("arbitrary",)),
    )(ht, dz)


def grad_reduce(g_in, g_out, small):
    gi = g_in.reshape(N_CHIPS, 2, *g_in.shape[1:])
    go = g_out.reshape(N_CHIPS, 2, *g_out.shape[1:])
    shp_i, shp_o = gi.shape[2:], go.shape[2:]

    def body(gi_ref, go_ref, sm_ref, ri_ref, ro_ref, rs_ref, a_i, b_i, s_i, c_i, a_o, b_o, s_o, c_o, sbuf,
             loc_sems, sa, ra, sb, rb, sc, rc, ss, rs):
        x, y, c = lax.axis_index("x"), lax.axis_index("y"), lax.axis_index("c")
        me = 2 * x + y
        sib = (x, y, 1 - c)
        srcs, mine, theirs, staged, contrib, res = ((gi_ref, go_ref), (a_i, a_o), (b_i, b_o), (s_i, s_o), (c_i, c_o),
                                                    (ri_ref, ro_ref))

        flips = [(fx, fy, fc) for fx in (0, 1) for fy in (0, 1) for fc in (0, 1)][1:]
        my8 = 4 * x + 2 * y + c
        sbuf[my8] = sm_ref[...]

        def small_copy(k, slot, to):
            return pltpu.make_async_remote_copy(src_ref=sm_ref, dst_ref=sbuf.at[slot], send_sem=ss.at[k], recv_sem=rs.at[k],
                                                device_id=to, device_id_type=MESH)

        sends = []
        for k, (fx, fy, fc) in enumerate(flips):
            px, py, pc = _flip(x, fx), _flip(y, fy), _flip(c, fc)
            sends.append(small_copy(k, my8, (px, py, pc)))
            sends[-1].start()

        def a_copy(t, j):
            return pltpu.make_async_remote_copy(src_ref=srcs[t].at[j, 1 - c], dst_ref=theirs[t].at[j], send_sem=sa.at[t, j],
                                                recv_sem=ra.at[t, j], device_id=sib, device_id_type=MESH)

        peers = _chip_peers(x, y)
        order = [2 * px + py for px, py in peers] + [me]
        loads = [[pltpu.make_async_copy(srcs[t].at[j, c], mine[t].at[j], loc_sems.at[t, j]) for t in range(2)] for j in order]
        for pos, j in enumerate(order):
            for t in range(2):
                loads[pos][t].start()
                sends.append(a_copy(t, j))
                sends[-1].start()

        def b_copy(k, t, piece, slot, to):
            return pltpu.make_async_remote_copy(src_ref=staged[t].at[piece], dst_ref=contrib[t].at[slot], send_sem=sb.at[k, t],
                                                recv_sem=rb.at[k, t], device_id=to, device_id_type=MESH)

        for k, (px, py) in enumerate(peers):
            j = 2 * px + py
            for t in range(2):
                loads[k][t].wait()
                a_copy(t, j).wait_recv()
                staged[t][j] = (mine[t][j] + theirs[t][j]).astype(BF16)
                sends.append(b_copy(k, t, j, me, (px, py, c)))
                sends[-1].start()
        for t in range(2):
            loads[len(peers)][t].wait()
            a_copy(t, me).wait_recv()
            mine[t][me] = mine[t][me] + theirs[t][me]
            contrib[t][me] = mine[t][me].astype(BF16)
        for k, (px, py) in enumerate(peers):
            for t in range(2):
                b_copy(k, t, me, 2 * px + py, (px, py, c)).wait_recv()

        def c_copy(t, half):
            return pltpu.make_async_remote_copy(src_ref=res[t].at[half], dst_ref=res[t].at[half], send_sem=sc.at[t],
                                                recv_sem=rc.at[t], device_id=sib, device_id_type=MESH)

        for t in range(2):
            own = mine[t][me]
            term = lambda j: jnp.where(me == j, own, contrib[t][j].astype(F32))
            res[t][c] = ((term(0) + term(1)) + term(2)) + term(3)
            sends.append(c_copy(t, c))
            sends[-1].start()
        for t in range(2):
            c_copy(t, 1 - c).wait_recv()

        for k, (fx, fy, fc) in enumerate(flips):
            px, py, pc = _flip(x, fx), _flip(y, fy), _flip(c, fc)
            small_copy(k, 4 * px + 2 * py + pc, (px, py, pc)).wait_recv()
        tot = sbuf[0]
        for d in range(1, N_DEV):
            tot = tot + sbuf[d]
        rs_ref[...] = tot
        for cp in sends:
            cp.wait_send()

    vm = pl.BlockSpec(memory_space=pltpu.VMEM)
    anyspace = pl.BlockSpec(memory_space=pl.ANY)
    dma = pltpu.SemaphoreType.DMA
    bufs = lambda shp: [pltpu.VMEM((N_CHIPS, *shp), F32), pltpu.VMEM((N_CHIPS, *shp), F32),
                        pltpu.VMEM((N_CHIPS, *shp), BF16), pltpu.VMEM((N_CHIPS, *shp), BF16)]
    return pl.pallas_call(
        body, name="grad_reduce",
        out_shape=(jax.ShapeDtypeStruct((2, *shp_i), F32), jax.ShapeDtypeStruct((2, *shp_o), F32),
                   jax.ShapeDtypeStruct(small.shape, F32)),
        in_specs=[anyspace, anyspace, vm], out_specs=(vm, vm, vm),
        scratch_shapes=[*bufs(shp_i), *bufs(shp_o), pltpu.VMEM((N_DEV, *small.shape), F32),
                        dma((2, N_CHIPS)), dma((2, N_CHIPS)), dma((2, N_CHIPS)), dma((3, 2)), dma((3, 2)), dma((2,)), dma((2,)),
                        dma((N_DEV - 1,)), dma((N_DEV - 1,))],
        compiler_params=_params(),
    )(gi, go, small)


def _adam_math(w, g, m, v):
    m = ADAM_B1 * m + (1.0 - ADAM_B1) * g
    v = ADAM_B2 * v + (1.0 - ADAM_B2) * (g * g)
    m_hat = m / (1.0 - ADAM_B1 ** ADAM_STEP)
    v_hat = v / (1.0 - ADAM_B2 ** ADAM_STEP)
    delta = -ADAM_LR * (m_hat / (jnp.sqrt(v_hat) + ADAM_EPS) + ADAM_WD * w)
    return delta, m, v


def adam_shard(name, w, g2, m, v, block, grid, w_map, g_map):
    def body(w_ref, g_ref, m_ref, v_ref, go_ref, d_ref, mo_ref, vo_ref):
        g = g_ref[0]
        delta, mn, vn = _adam_math(w_ref[...], g, m_ref[...], v_ref[...])
        go_ref[...] = g
        d_ref[...] = delta
        mo_ref[...] = mn
        vo_ref[...] = vn

    ws = pl.BlockSpec(block, w_map)
    shp = jax.ShapeDtypeStruct(w.shape, F32)
    return pl.pallas_call(
        body, name=name, grid=grid, out_shape=(shp, shp, shp, shp),
        in_specs=[ws, pl.BlockSpec((1, *block), g_map), ws, ws], out_specs=(ws, ws, ws, ws),
        compiler_params=_params(("arbitrary",) * len(grid)),
    )(w, g2, m, v)


def adam_small(ws, gs, ms, vs):
    n = len(ws)

    def body(*refs):
        ins, outs = refs[:4 * n], refs[4 * n:]
        for t in range(n):
            delta, mn, vn = _adam_math(ins[t][...], ins[n + t][...], ins[2 * n + t][...], ins[3 * n + t][...])
            outs[3 * t][...] = delta
            outs[3 * t + 1][...] = mn
            outs[3 * t + 2][...] = vn

    vm = pl.BlockSpec(memory_space=pltpu.VMEM)
    outs = pl.pallas_call(
        body, name="adam_small",
        out_shape=tuple(jax.ShapeDtypeStruct(w.shape, F32) for w in ws for _ in range(3)),
        in_specs=[vm] * (4 * n), out_specs=tuple([vm] * (3 * n)),
        compiler_params=_params(),
    )(*ws, *gs, *ms, *vs)
    return [outs[3 * t:3 * t + 3] for t in range(n)]


def kernel(x, norm_pre_g, w_in, conv_w, w_out, norm_post_g, loss_target, m_norm_pre_g, m_w_in, m_conv_w, m_w_out, m_norm_post_g, v_norm_pre_g, v_w_in, v_conv_w, v_w_out, v_norm_post_g):
    _, seq, d_model = x.shape
    width = w_in.shape[1]
    conv_q = conv_w.shape[1]
    conv_width = N_CHIPS * conv_q
    attn_width = d_model - conv_width
    xs, tg = x[0], loss_target[0]
    g1, g2 = norm_pre_g.reshape(1, d_model), norm_post_g.reshape(1, d_model)

    w_full, wout_full, cw_full = gather_weights(w_in, w_out, conv_w)
    wout2 = wout_full.reshape(attn_width + conv_width, d_model)
    cw = jnp.zeros((SUBLANES, conv_width), F32).at[:CONV_K].set(
        cw_full[:, :CONV_K, :conv_q].transpose(1, 0, 2).reshape(CONV_K, conv_width))
    tables = _rope_tables(seq)

    ht, q, k, v, qp, kp, vp, ga, cz = inproj(xs, g1, w_full, tables, attn_width, conv_width)
    run = attn_fwd("p4", qp, kp, vp, None)
    run = attn_fwd("p16", qp, kp, vp, run)
    o, lse = attn_fwd("nat", q, k, v, run)
    (d_o, delta, d_op, delta_p, lse_p, dga, dcb, dgc, dcv, e, dwout, dg2, dcw, loss_acc) = tail(
        o, lse, ga, cz, xs, tg, wout2, g2, cw)
    nat_grads = attn_bwd("nat", q, k, v, d_o, lse, delta, None)
    perm_grads = attn_bwd("p4", qp, kp, vp, d_op, lse_p, delta_p, None)
    perm_grads = attn_bwd("p16", qp, kp, vp, d_op, lse_p, delta_p, perm_grads)
    grad_x, dz, dg1 = dz_dx(nat_grads, perm_grads, dga, dcb, dgc, dcv, cz, tables, xs, g1, e, w_full, cw)
    dwin = dw_in(ht, dz)

    small = jnp.zeros((SUBLANES, d_model), F32)
    small = small.at[0].set(dg1.sum(axis=0)).at[1].set(dg2.sum(axis=0))
    small = small.at[2:2 + CONV_K, :conv_width].set(dcw.reshape(CONV_K, SUBLANES, conv_width).sum(axis=1))
    small = small.at[2 + CONV_K, 0].set(jnp.sum(loss_acc))
    rin, rout, rsmall = grad_reduce(dwin, dwout.reshape(N_DEV, -1, d_model), small)

    half = width // 2
    tr = 256
    gw_in, d_in, m_in, v_in = adam_shard(
        "adam_w_in", w_in, rin, m_w_in, v_w_in, (tr, half), (2, d_model // tr),
        lambda hf, i: (i, hf), lambda hf, i: (hf, i, 0))
    rq = w_out.shape[0] // 2
    gw_out, d_out, m_out, v_out = adam_shard(
        "adam_w_out", w_out, rout, m_w_out, v_w_out, (rq, d_model), (2,),
        lambda hf: (hf, 0), lambda hf: (hf, 0, 0))

    chip = 2 * lax.axis_index("x") + lax.axis_index("y")
    g_pre, g_post = rsmall[0:1], rsmall[1:2]
    g_conv = lax.dynamic_slice(rsmall[2:2 + CONV_K, :conv_width], (0, chip * conv_q), (CONV_K, conv_q))
    (d_pre, m_pre, v_pre), (d_post, m_post, v_post), (d_cv, m_cv, v_cv) = adam_small(
        [g1, g2, conv_w], [g_pre, g_post, g_conv],
        [m_norm_pre_g.reshape(1, d_model), m_norm_post_g.reshape(1, d_model), m_conv_w],
        [v_norm_pre_g.reshape(1, d_model), v_norm_post_g.reshape(1, d_model), v_conv_w])

    loss = 0.5 * rsmall[2 + CONV_K, 0] / d_model
    vec = lambda a: a.reshape(d_model)
    return (loss, grad_x.reshape(1, seq, d_model),
            vec(g_pre), gw_in, g_conv, gw_out, vec(g_post),
            vec(d_pre), d_in, d_cv, d_out, vec(d_post),
            vec(m_pre), m_in, m_cv, m_out, vec(m_post),
            vec(v_pre), v_in, v_cv, v_out, vec(v_post))
```

```python
import jax
import jax.numpy as jnp
from jax import lax
from jax.experimental import pallas as pl
from jax.experimental.pallas import tpu as pltpu

HEAD_DIM = 64
LANES = 128
SUBLANES = 8
BLOCK = 128
WINDOW_KEYS = 128
PERM = 16
PJ = 4
P4_ROWS = BLOCK // PJ
ROW_TILE = 512
DZ_ROW_TILE = 512
CONV_K = 3
ROPE_THETA = 10000.0
NORM_EPS = 1e-6
ATTN_SCALE = HEAD_DIM ** -0.5
NEG = -1e30
N_CHIPS = 4
N_DEV = 8
MESH = pl.DeviceIdType.MESH
ADAM_LR = 0.001
ADAM_B1 = 0.9
ADAM_B2 = 0.999
ADAM_EPS = 1e-08
ADAM_WD = 0.01
ADAM_STEP = 10
VMEM_LIMIT = 52 * 1024 * 1024

F32 = jnp.float32
BF16 = jnp.bfloat16


def _params(sem=None, **kw):
    return pltpu.CompilerParams(dimension_semantics=sem, vmem_limit_bytes=VMEM_LIMIT, **kw)


def _const_spec(shape):
    return pl.BlockSpec(shape, lambda *_: (0,) * len(shape), pipeline_mode=pl.Buffered(1))


def _sigmoid(z):
    return 1.0 / (1.0 + jnp.exp(-z))


def _rowgroup_sum(a):
    rows, n = a.shape
    return a.reshape(rows // SUBLANES, SUBLANES, n).sum(axis=0)


def _nt(a, b):
    return lax.dot_general(a, b, (((1,), (1,)), ((), ())), preferred_element_type=F32)


def _tn(a, b):
    return lax.dot_general(a, b, (((0,), (0,)), ((), ())), preferred_element_type=F32)


def _col_pieces(a, b, width):
    out = []
    while a < b:
        j = a // width
        e = min(b, (j + 1) * width)
        out.append((j, a - j * width, e - j * width))
        a = e
    return out


def _lane_groups(width):
    return [slice(g * LANES, (g + 1) * LANES) for g in range(width // LANES)]


def _perm_shape(seq, width):
    return (PJ, PJ, seq // PERM, width)


def _perm_tile_spec(width, tm):
    return pl.BlockSpec((PJ, PJ, tm // PERM, width), lambda i: (0, 0, i, 0))


STAGE_PITCH = 24


def _stage_shape(groups, rows):
    return (groups, rows // PERM * STAGE_PITCH, LANES)


def _stage_put(stage, g, val, row0=0):
    for a in range(val.shape[0] // PERM):
        at = (row0 // PERM + a) * STAGE_PITCH
        stage[g, at:at + PERM, :] = val[a * PERM:(a + 1) * PERM]


def _stage_get(stage, g):
    return jnp.concatenate([stage[g, a * STAGE_PITCH:a * STAGE_PITCH + PERM, :]
                            for a in range(stage.shape[1] // STAGE_PITCH)], axis=0)


def _to_perm(stage, g, dst_ref, sl, dtype):
    rows = stage.shape[1] // STAGE_PITCH
    for b in range(PERM):
        dst_ref[b // PJ, b % PJ, :, sl] = stage[g, pl.ds(b, rows, stride=STAGE_PITCH), :].astype(dtype)


def _from_perm(src_ref, sl, stage, g):
    rows = stage.shape[1] // STAGE_PITCH
    for b in range(PERM):
        stage[g, pl.ds(b, rows, stride=STAGE_PITCH), :] = src_ref[b // PJ, b % PJ, :, sl].astype(F32)


def _flip(a, f):
    return 1 - a if f else a


def _chip_peers(x, y):
    return [(1 - x, y), (x, 1 - y), (1 - x, 1 - y)]


def gather_weights(w_in, w_out, conv_w):
    d_model, width = w_in.shape
    rows = w_out.shape[0]
    cw = jnp.zeros((SUBLANES, LANES), F32).at[:CONV_K, :conv_w.shape[1]].set(conv_w)

    def body(win_ref, wout_ref, cw_ref, winf_ref, woutf_ref, cwf_ref, st_in, st_out, ici_send, ici_recv, d2d_send, d2d_recv):
        x, y, c = lax.axis_index("x"), lax.axis_index("y"), lax.axis_index("c")
        me = 2 * x + y
        sib = (x, y, 1 - c)
        st_in[...] = win_ref[...].astype(BF16)
        st_out[...] = wout_ref[...].astype(BF16)
        winf_ref[me] = st_in[...]
        woutf_ref[me] = st_out[...]
        cwf_ref[me] = cw_ref[...]
        stages = (st_in, st_out)
        fulls = (winf_ref, woutf_ref)
        halves = (d_model // 2, rows // 2)

        def half(t, core):
            return pl.ds(pl.multiple_of(core * halves[t], halves[t]), halves[t])

        def ici(k, t, slot, to, core):
            src = stages[t].at[half(t, core)] if t < 2 else cw_ref
            dst = fulls[t].at[slot, half(t, core)] if t < 2 else cwf_ref.at[slot]
            return pltpu.make_async_remote_copy(src_ref=src, dst_ref=dst, send_sem=ici_send.at[k, t], recv_sem=ici_recv.at[k, t],
                                                device_id=to, device_id_type=MESH)

        def d2d(k, t, slot, core):
            ref = fulls[t].at[slot, half(t, core)]
            return pltpu.make_async_remote_copy(src_ref=ref, dst_ref=ref, send_sem=d2d_send.at[k, t], recv_sem=d2d_recv.at[k, t],
                                                device_id=sib, device_id_type=MESH)

        peers = _chip_peers(x, y)
        sends = [ici(k, t, me, (px, py, c), c) for k, (px, py) in enumerate(peers) for t in range(3)]
        for cp in sends:
            cp.start()
        for k, (px, py) in enumerate(peers):
            for t in range(2):
                ici(k, t, 2 * px + py, (px, py, c), c).wait_recv()
                fwd = d2d(k, t, 2 * px + py, c)
                fwd.start()
                sends.append(fwd)
            ici(k, 2, 2 * px + py, (px, py, c), c).wait_recv()
        for k, (px, py) in enumerate(peers):
            for t in range(2):
                d2d(k, t, 2 * px + py, 1 - c).wait_recv()
        for cp in sends:
            cp.wait_send()

    vm = pl.BlockSpec(memory_space=pltpu.VMEM)
    dma = pltpu.SemaphoreType.DMA
    return pl.pallas_call(
        body, name="gather_weights",
        out_shape=(jax.ShapeDtypeStruct((N_CHIPS, d_model, width), BF16),
                   jax.ShapeDtypeStruct((N_CHIPS, rows, d_model), BF16),
                   jax.ShapeDtypeStruct((N_CHIPS, SUBLANES, LANES), F32)),
        in_specs=[vm, vm, vm], out_specs=(vm, vm, vm),
        scratch_shapes=[pltpu.VMEM((d_model, width), BF16), pltpu.VMEM((rows, d_model), BF16),
                        dma((3, 3)), dma((3, 3)), dma((3, 2)), dma((3, 2))],
        compiler_params=_params(),
    )(w_in, w_out, cw)


def _rope_tables(seq):
    half = HEAD_DIM // 2
    inv_freq = ROPE_THETA ** (-jnp.arange(half, dtype=F32) * 2.0 / HEAD_DIM)
    ang = jnp.arange(seq).astype(F32)[:, None] * jnp.tile(inv_freq, LANES // half)[None, :]
    first_half = (jnp.arange(LANES) % HEAD_DIM < half)[None, :]
    sin = jnp.sin(ang)
    return jnp.cos(ang), jnp.where(first_half, -sin, 0.0), jnp.where(first_half, 0.0, sin)


def _rope(t, cos, s1, s2):
    return t * cos + pltpu.roll(t, LANES - HEAD_DIM // 2, 1) * s1 + pltpu.roll(t, HEAD_DIM // 2, 1) * s2


def _rope_transposed(g, cos, s1, s2):
    return g * cos + pltpu.roll(g * s1, HEAD_DIM // 2, 1) + pltpu.roll(g * s2, LANES - HEAD_DIM // 2, 1)


def inproj(x, g1, w_full, tables, attn_w, conv_w):
    seq, d_model = x.shape
    width = w_full.shape[2]
    tm = ROW_TILE
    groups = _lane_groups(attn_w)

    def body(x_ref, g_ref, w_ref, cos_ref, s1_ref, s2_ref,
             ht_ref, q_ref, k_ref, v_ref, qp_ref, kp_ref, vp_ref, ga_ref, cz_ref, stage):
        xv = x_ref[...]
        hb = ((xv * lax.rsqrt(jnp.mean(xv * xv, axis=-1, keepdims=True) + NORM_EPS)) * g_ref[...]).astype(BF16)
        ht_ref[...] = jnp.transpose(hb)
        cos, s1, s2 = cos_ref[...], s1_ref[...], s2_ref[...]

        def proj(a, b):
            parts = [jnp.dot(hb, w_ref[j, :, lo:hi], preferred_element_type=F32) for j, lo, hi in _col_pieces(a, b, width)]
            return parts[0] if len(parts) == 1 else jnp.concatenate(parts, axis=1)

        def emit(z, nat_ref, perm_ref, fn):
            for g, sl in enumerate(groups):
                val = fn(z[:, sl])
                nat_ref[:, sl] = val.astype(BF16)
                _stage_put(stage, g, val)
            for g, sl in enumerate(groups):
                _to_perm(stage, g, perm_ref, sl, BF16)

        emit(proj(0, attn_w), q_ref, qp_ref, lambda t: _rope(t, cos, s1, s2) * ATTN_SCALE)
        emit(proj(attn_w, 2 * attn_w), k_ref, kp_ref, lambda t: _rope(t, cos, s1, s2))
        emit(proj(2 * attn_w, 3 * attn_w), v_ref, vp_ref, lambda t: t)
        ga_ref[...] = proj(3 * attn_w, 4 * attn_w)
        cz_ref[...] = proj(4 * attn_w, 4 * attn_w + 4 * conv_w)

    row = lambda n: pl.BlockSpec((tm, n), lambda i: (i, 0))
    nat = jax.ShapeDtypeStruct((seq, attn_w), BF16)
    perm = jax.ShapeDtypeStruct(_perm_shape(seq, attn_w), BF16)
    return pl.pallas_call(
        body, name="inproj", grid=(seq // tm,),
        out_shape=(jax.ShapeDtypeStruct((d_model, seq), BF16), nat, nat, nat, perm, perm, perm,
                   jax.ShapeDtypeStruct((seq, attn_w), F32), jax.ShapeDtypeStruct((seq, 4 * conv_w), F32)),
        in_specs=[row(d_model), _const_spec((1, d_model)), _const_spec(w_full.shape), row(LANES), row(LANES), row(LANES)],
        out_specs=(pl.BlockSpec((d_model, tm), lambda i: (0, i)), row(attn_w), row(attn_w), row(attn_w),
                   _perm_tile_spec(attn_w, tm), _perm_tile_spec(attn_w, tm), _perm_tile_spec(attn_w, tm),
                   row(attn_w), row(4 * conv_w)),
        scratch_shapes=[pltpu.VMEM(_stage_shape(len(groups), tm), F32)],
        compiler_params=_params(("arbitrary",)),
    )(x, g1, w_full, *tables)


class _Mode:
    def __init__(self, name, seq):
        self.name = name
        if name == "nat":
            self.residues, self.nb = 1, seq // BLOCK
        elif name == "p16":
            self.residues, self.nb = PERM, seq // PERM // BLOCK
        else:
            self.residues, self.nb = PJ, seq // PERM // P4_ROWS

    def spec(self, width, which, last=None):
        if which == "prev":
            blk = lambda n: jnp.maximum(n - 1, 0)
        elif last is None:
            blk = lambda n: n
        else:
            blk = lambda n: jnp.minimum(n, last)
        if self.name == "nat":
            return pl.BlockSpec((BLOCK, width), lambda r, n: (blk(n), 0))
        if self.name == "p16":
            return pl.BlockSpec((1, 1, BLOCK, width), lambda r, n: (r // PJ, r % PJ, blk(n), 0))
        return pl.BlockSpec((PJ, 1, P4_ROWS, width), lambda r, n: (0, r, blk(n), 0))

    def get(self, ref, sl):
        if self.name == "nat":
            return ref[:, sl]
        if self.name == "p16":
            return ref[0, 0, :, sl]
        return jnp.concatenate([ref[j, 0, :, sl] for j in range(PJ)], axis=0)

    def put(self, ref, sl, val):
        val = val.astype(ref.dtype)
        if self.name == "nat":
            ref[:, sl] = val
        elif self.name == "p16":
            ref[0, 0, :, sl] = val
        else:
            for j in range(PJ):
                ref[j, 0, :, sl] = val[j * P4_ROWS:(j + 1) * P4_ROWS]

    def index(self, idx, is_key):
        if self.name != "p4":
            return idx - BLOCK if is_key else idx
        within = jnp.bitwise_and(idx, BLOCK - 1)
        m = PJ * jnp.bitwise_and(within, P4_ROWS - 1) + jnp.right_shift(within, P4_ROWS.bit_length() - 1)
        return m + BLOCK * (jnp.right_shift(idx, BLOCK.bit_length() - 1) - 1) if is_key else m

    def bias(self, n, keys_major):
        shape = (2 * BLOCK, BLOCK) if keys_major else (BLOCK, 2 * BLOCK)
        kdim = 0 if keys_major else 1
        kidx = lax.broadcasted_iota(jnp.int32, shape, kdim)
        qidx = lax.broadcasted_iota(jnp.int32, shape, 1 - kdim)
        rel = self.index(qidx, False) - self.index(kidx, True)
        valid = (rel >= 0) & (rel <= WINDOW_KEYS) & ((kidx >= BLOCK) | (n > 0))
        return jnp.where(valid, 0.0, NEG)


def _head_masks():
    lane = lax.broadcasted_iota(jnp.int32, (BLOCK, LANES), 1)
    lo = lane < HEAD_DIM
    return lane, lo, jnp.where(lo, 1.0, 0.0).astype(BF16), jnp.where(lo, 0.0, 1.0).astype(BF16)


def _column(blk, lane, h):
    return jnp.sum(jnp.where(lane == h, blk, 0.0), axis=1, keepdims=True)


def attn_fwd(name, q, k, v, run):
    nat = name == "nat"
    seq = q.shape[0] if nat else q.shape[2] * PERM
    attn_w = q.shape[-1]
    mode = _Mode(name, seq)
    groups = _lane_groups(attn_w)
    first = run is None
    all_lanes = slice(0, LANES)

    def body(*refs):
        q_ref, kp_ref, kc_ref, vp_ref, vc_ref = refs[:5]
        if first:
            o_ref, l_ref = refs[5:]
        elif nat:
            oin_ref, lin_ref, ex_ref, o_ref, l_ref, ostage, lstage = refs[5:]
        else:
            oin_ref, lin_ref, ex_ref, o_ref, l_ref = refs[5:]
        n = pl.program_id(1)
        bias = mode.bias(n, True)
        bias2 = jnp.concatenate([bias, bias], axis=1)
        _, lo, m_lo, m_hi = _head_masks()
        head_row = lax.broadcasted_iota(jnp.int32, (BLOCK, LANES), 0)
        lrows = jnp.zeros((BLOCK, LANES), F32)
        def probs(sl):
            q2 = mode.get(q_ref, sl)
            kcat = jnp.concatenate([mode.get(kp_ref, sl), mode.get(kc_ref, sl)], axis=0)
            vcat = jnp.concatenate([mode.get(vp_ref, sl), mode.get(vc_ref, sl)], axis=0)
            qq = jnp.concatenate([q2 * m_lo, q2 * m_hi], axis=0)
            s_t = _nt(kcat, qq) + bias2
            m = jnp.max(s_t, axis=0, keepdims=True)
            pe = jnp.exp(s_t - m)
            l = jnp.sum(pe, axis=0, keepdims=True)
            return vcat, (pe * (1.0 / l)).astype(BF16), m + jnp.log(l)

        def output(p, sl, vcat, pn, lse, lrows):
            o_new = _tn(pn, vcat)
            mode.put(o_ref, sl, jnp.where(lo, o_new[:BLOCK], o_new[BLOCK:]))
            lrows = jnp.where(head_row == 2 * p, lse[:, :BLOCK], lrows)
            return jnp.where(head_row == 2 * p + 1, lse[:, BLOCK:], lrows)

        pending = None
        for p, sl in enumerate(groups):
            nxt = probs(sl)
            if pending is not None:
                lrows = output(*pending, lrows)
            pending = (p, sl, *nxt)
        lrows = output(*pending, lrows)
        lblk = jnp.transpose(lrows)
        if first:
            mode.put(l_ref, all_lanes, lblk)
        else:
            if nat:
                for g, sl in enumerate(groups):
                    _from_perm(oin_ref, sl, ostage, g)
                _from_perm(lin_ref, all_lanes, lstage, 0)
                lin = _stage_get(lstage, 0)
            else:
                lin = mode.get(lin_ref, all_lanes)
            mx = jnp.maximum(lin, lblk)
            new = mx + jnp.log(jnp.exp(lin - mx) + jnp.exp(lblk - mx))
            mode.put(l_ref, all_lanes, new)

            def expand(w):
                hi = w.astype(BF16)
                rest = (w - hi.astype(F32)).astype(BF16)
                ex = ex_ref[...]
                return jnp.dot(hi, ex, preferred_element_type=F32) + jnp.dot(rest, ex, preferred_element_type=F32)

            w_prev, w_cur = expand(jnp.exp(lin - new)), expand(jnp.exp(lblk - new))
            for p, sl in enumerate(groups):
                o_prev = _stage_get(ostage, p) if nat else mode.get(oin_ref, sl)
                mode.put(o_ref, sl, w_prev[:, sl] * o_prev + w_cur[:, sl] * mode.get(o_ref, sl))

    ins = [q, k, k, v, v]
    specs = [mode.spec(attn_w, "cur"), mode.spec(attn_w, "prev"), mode.spec(attn_w, "cur"),
             mode.spec(attn_w, "prev"), mode.spec(attn_w, "cur")]
    scratch = []
    if not first:
        ins += list(run)
        if nat:
            rows8 = BLOCK // PERM
            specs += [pl.BlockSpec((PJ, PJ, rows8, attn_w), lambda r, n: (0, 0, n, 0)),
                      pl.BlockSpec((PJ, PJ, rows8, LANES), lambda r, n: (0, 0, n, 0))]
            scratch = [pltpu.VMEM(_stage_shape(len(groups), BLOCK), F32), pltpu.VMEM(_stage_shape(1, BLOCK), F32)]
        else:
            specs += [mode.spec(attn_w, "cur"), mode.spec(LANES, "cur")]
        head_of_lane = jnp.arange(attn_w, dtype=jnp.int32) // HEAD_DIM
        ins.append((jnp.arange(LANES, dtype=jnp.int32)[:, None] == head_of_lane[None, :]).astype(BF16))
        specs.append(_const_spec((LANES, attn_w)))
    if nat:
        out_shape = (jax.ShapeDtypeStruct((seq, attn_w), F32), jax.ShapeDtypeStruct((seq, LANES), F32))
    else:
        out_shape = (jax.ShapeDtypeStruct(_perm_shape(seq, attn_w), F32), jax.ShapeDtypeStruct(_perm_shape(seq, LANES), F32))
    return pl.pallas_call(
        body, name=f"attn_fwd_{name}", grid=(mode.residues, mode.nb),
        out_shape=out_shape, in_specs=specs, out_specs=(mode.spec(attn_w, "cur"), mode.spec(LANES, "cur")),
        scratch_shapes=scratch,
        compiler_params=_params(("arbitrary", "arbitrary")),
    )(*ins)


def attn_bwd(name, q, k, v, d_o, lse, delta, run):
    nat = name == "nat"
    seq = q.shape[0] if nat else q.shape[2] * PERM
    attn_w = q.shape[-1]
    mode = _Mode(name, seq)
    nb = mode.nb
    groups = _lane_groups(attn_w)
    first = run is None
    all_lanes = slice(0, LANES)

    def body(*refs):
        q_ref, kp_ref, kc_ref, vp_ref, vc_ref, do_ref, lse_ref, dl_ref = refs[:8]
        if first:
            dq_ref, dk_ref, dv_ref, ck, cv = refs[8:]
        else:
            dqi_ref, dki_ref, dvi_ref, dq_ref, dk_ref, dv_ref, ck, cv = refs[8:]
        n = pl.program_id(1)

        @pl.when(n == 0)
        def _():
            ck[...] = jnp.zeros_like(ck)
            cv[...] = jnp.zeros_like(cv)

        @pl.when(n < nb)
        def _():
            bias = mode.bias(n, True)
            bias2 = jnp.concatenate([bias, bias], axis=1)
            _, lo, m_lo, m_hi = _head_masks()
            lse_t = jnp.transpose(mode.get(lse_ref, all_lanes))
            dl_t = jnp.transpose(mode.get(dl_ref, all_lanes))
            def scores(p, sl):
                q2, do2 = mode.get(q_ref, sl), mode.get(do_ref, sl)
                kcat = jnp.concatenate([mode.get(kp_ref, sl), mode.get(kc_ref, sl)], axis=0)
                vcat = jnp.concatenate([mode.get(vp_ref, sl), mode.get(vc_ref, sl)], axis=0)
                qq = jnp.concatenate([q2 * m_lo, q2 * m_hi], axis=0)
                dd = jnp.concatenate([do2 * m_lo, do2 * m_hi], axis=0)
                h0 = 2 * p
                lse2 = jnp.concatenate([lse_t[h0:h0 + 1, :], lse_t[h0 + 1:h0 + 2, :]], axis=1)
                dl2 = jnp.concatenate([dl_t[h0:h0 + 1, :], dl_t[h0 + 1:h0 + 2, :]], axis=1)
                p_t = jnp.exp(_nt(kcat, qq) + (bias2 - lse2))
                ds_t = p_t * (_nt(vcat, dd) - dl2)
                return qq, dd, kcat, p_t.astype(BF16), ds_t.astype(BF16)

            def grads(sl, qq, dd, kcat, pb, dsb):
                dkc = jnp.dot(dsb, qq, preferred_element_type=F32)
                dvc = jnp.dot(pb, dd, preferred_element_type=F32)
                dqb = _tn(dsb, kcat)
                dq2 = jnp.where(lo, dqb[:BLOCK], dqb[BLOCK:]) * ATTN_SCALE
                dk2 = ck[:, sl] + dkc[:BLOCK]
                dv2 = cv[:, sl] + dvc[:BLOCK]
                if not first:
                    dq2 = dq2 + mode.get(dqi_ref, sl).astype(F32)
                    dk2 = dk2 + mode.get(dki_ref, sl).astype(F32)
                    dv2 = dv2 + mode.get(dvi_ref, sl).astype(F32)
                mode.put(dq_ref, sl, dq2)
                mode.put(dk_ref, sl, dk2)
                mode.put(dv_ref, sl, dv2)
                ck[:, sl] = dkc[BLOCK:]
                cv[:, sl] = dvc[BLOCK:]

            pending = None
            for p, sl in enumerate(groups):
                nxt = scores(p, sl)
                if pending is not None:
                    grads(*pending)
                pending = (sl, *nxt)
            grads(*pending)

        @pl.when(n == nb)
        def _():
            for sl in groups:
                if first:
                    mode.put(dk_ref, sl, ck[:, sl])
                    mode.put(dv_ref, sl, cv[:, sl])
                else:
                    mode.put(dk_ref, sl, ck[:, sl] + mode.get(dki_ref, sl).astype(F32))
                    mode.put(dv_ref, sl, cv[:, sl] + mode.get(dvi_ref, sl).astype(F32))

    last = nb - 1
    cur = lambda w: mode.spec(w, "cur", last)
    prev = lambda w: mode.spec(w, "prev")
    ins = [q, k, k, v, v, d_o, lse, delta]
    specs = [cur(attn_w), prev(attn_w), cur(attn_w), prev(attn_w), cur(attn_w), cur(attn_w), cur(LANES), cur(LANES)]
    if not first:
        ins += list(run)
        specs += [cur(attn_w), prev(attn_w), prev(attn_w)]
    shp = jax.ShapeDtypeStruct((seq, attn_w) if nat else _perm_shape(seq, attn_w), BF16)
    return pl.pallas_call(
        body, name=f"attn_bwd_{name}", grid=(mode.residues, nb + 1),
        out_shape=(shp, shp, shp), in_specs=specs, out_specs=(cur(attn_w), prev(attn_w), prev(attn_w)),
        scratch_shapes=[pltpu.VMEM((BLOCK, attn_w), F32), pltpu.VMEM((BLOCK, attn_w), F32)],
        compiler_params=_params(("arbitrary", "arbitrary")),
    )(*ins)


def _shift_down(u, halo, k):
    rolled = pltpu.roll(u, k, 0)
    row = lax.broadcasted_iota(jnp.int32, halo.shape, 0)
    top = jnp.where(row < k, pltpu.roll(halo, k, 0), rolled[:SUBLANES])
    return jnp.concatenate([top, rolled[SUBLANES:]], axis=0)


def _shift_up(u, halo, k):
    rows = u.shape[0]
    rolled = pltpu.roll(u, rows - k, 0)
    row = lax.broadcasted_iota(jnp.int32, halo.shape, 0)
    bot = jnp.where(row >= SUBLANES - k, pltpu.roll(halo, SUBLANES - k, 0), rolled[rows - SUBLANES:])
    return jnp.concatenate([rolled[:rows - SUBLANES], bot], axis=0)


def tail(o, lse, ga, cz, x, tgt, w_out, g2, cw):
    seq, d_model = x.shape
    attn_w = o.shape[1]
    conv_w = cz.shape[1] // 4
    mix = attn_w + conv_w
    groups = _lane_groups(attn_w)
    tm = ROW_TILE
    nt = seq // tm
    hb = tm // SUBLANES

    def body(o_ref, l_ref, ga_ref, cz_ref, hz_ref, x_ref, t_ref, w_ref, g_ref, cw_ref,
             do_ref, dl_ref, dop_ref, dlp_ref, lp_ref, dga_ref, dcb_ref, dgc_ref, dcv_ref, e_ref,
             dw_ref, dg_ref, dcw_ref, loss_ref, stage):
        i = pl.program_id(0)

        @pl.when(i == 0)
        def _():
            dw_ref[...] = jnp.zeros_like(dw_ref)
            dg_ref[...] = jnp.zeros_like(dg_ref)
            dcw_ref[...] = jnp.zeros_like(dcw_ref)
            loss_ref[...] = jnp.zeros_like(loss_ref)

        u = cz_ref[:, 2 * conv_w:3 * conv_w] * cz_ref[:, 0:conv_w]
        uh = hz_ref[:, 2 * conv_w:3 * conv_w] * hz_ref[:, 0:conv_w]
        uh = jnp.where(i > 0, uh, 0.0)
        u1 = _shift_down(u, uh, 1)
        u2 = _shift_down(u, uh, 2)
        w0, w1, w2 = cw_ref[0:1, :], cw_ref[1:2, :], cw_ref[2:3, :]
        cvv = u2 * w0 + u1 * w1 + u * w2
        gv = g_ref[...]
        all_lanes = slice(0, LANES)

        def forward(rs):
            ov, gav = o_ref[rs, :], ga_ref[rs, :]
            sig_a = _sigmoid(gav)
            silu_a = gav * sig_a
            cb, gc = cz_ref[rs, conv_w:2 * conv_w], cz_ref[rs, 3 * conv_w:4 * conv_w]
            sig_c = _sigmoid(gc)
            silu_c = gc * sig_c
            bc = cb * cvv[rs]
            mixed = jnp.concatenate([ov * silu_a, bc * silu_c], axis=1).astype(BF16)
            yv = jnp.dot(mixed, w_ref[...], preferred_element_type=F32)
            return ov, gav, sig_a, silu_a, cb, gc, sig_c, silu_c, bc, mixed, yv

        def loss_and_dy(rs, mixed, yv):
            r2 = lax.rsqrt(jnp.mean(yv * yv, axis=-1, keepdims=True) + NORM_EPS)
            yhat = yv * r2
            diff = (x_ref[rs, :] + yhat * gv) - t_ref[rs, :]
            loss_ref[...] += _rowgroup_sum(diff * diff)
            ev = diff * (1.0 / d_model)
            e_ref[rs, :] = ev
            dg_ref[...] += _rowgroup_sum(ev * yhat)
            eg = ev * gv
            dy = (r2 * (eg - yhat * jnp.mean(eg * yhat, axis=-1, keepdims=True))).astype(BF16)
            dw_ref[...] += _tn(mixed, dy)
            return _nt(dy, w_ref[...])

        def backward(rs, ov, gav, sig_a, silu_a, cb, gc, sig_c, silu_c, bc, dm):
            rows = rs.stop - rs.start
            dma, dmc = dm[:, :attn_w], dm[:, attn_w:]
            dov = dma * silu_a
            do_ref[rs, :] = dov.astype(BF16)
            dga_ref[rs, :] = (dma * ov * (sig_a * (1.0 + gav * (1.0 - sig_a)))).astype(BF16)
            prod = dov * ov
            lane = lax.broadcasted_iota(jnp.int32, (rows, LANES), 1)
            lo = lane < HEAD_DIM
            dblk = jnp.zeros((rows, LANES), F32)
            for p, sl in enumerate(groups):
                pr = prod[:, sl]
                dblk = jnp.where(lane == 2 * p, jnp.sum(jnp.where(lo, pr, 0.0), axis=1, keepdims=True), dblk)
                dblk = jnp.where(lane == 2 * p + 1, jnp.sum(jnp.where(lo, 0.0, pr), axis=1, keepdims=True), dblk)
                _stage_put(stage, p, dov[:, sl], rs.start)
            dl_ref[rs, :] = dblk
            _stage_put(stage, len(groups), dblk, rs.start)
            _stage_put(stage, len(groups) + 1, l_ref[rs, :], rs.start)
            dsc = dmc * silu_c
            cv_rows = cvv[rs]
            dcb_ref[rs, :] = (dsc * cv_rows).astype(BF16)
            dgc_ref[rs, :] = (dmc * bc * (sig_c * (1.0 + gc * (1.0 - sig_c)))).astype(BF16)
            dcv = dsc * cb
            dcv_ref[rs, :] = dcv
            dcw_ref[0:SUBLANES, :] += _rowgroup_sum(dcv * u2[rs])
            dcw_ref[SUBLANES:2 * SUBLANES, :] += _rowgroup_sum(dcv * u1[rs])
            dcw_ref[2 * SUBLANES:3 * SUBLANES, :] += _rowgroup_sum(dcv * u[rs])

        halves = [slice(0, tm // 2), slice(tm // 2, tm)]
        fwd = [forward(rs) for rs in halves]
        dms = [loss_and_dy(rs, f[9], f[10]) for rs, f in zip(halves, fwd)]
        for rs, f, dm in zip(halves, fwd, dms):
            backward(rs, *f[:9], dm)
        for p, sl in enumerate(groups):
            _to_perm(stage, p, dop_ref, sl, BF16)
        _to_perm(stage, len(groups), dlp_ref, all_lanes, F32)
        _to_perm(stage, len(groups) + 1, lp_ref, all_lanes, F32)

    row = lambda n: pl.BlockSpec((tm, n), lambda i: (i, 0))
    whole = lambda a, b: pl.BlockSpec((a, b), lambda i: (0, 0))
    return pl.pallas_call(
        body, name="tail", grid=(nt,),
        out_shape=(jax.ShapeDtypeStruct((seq, attn_w), BF16), jax.ShapeDtypeStruct((seq, LANES), F32),
                   jax.ShapeDtypeStruct(_perm_shape(seq, attn_w), BF16), jax.ShapeDtypeStruct(_perm_shape(seq, LANES), F32),
                   jax.ShapeDtypeStruct(_perm_shape(seq, LANES), F32),
                   jax.ShapeDtypeStruct((seq, attn_w), BF16), jax.ShapeDtypeStruct((seq, conv_w), BF16),
                   jax.ShapeDtypeStruct((seq, conv_w), BF16), jax.ShapeDtypeStruct((seq, conv_w), F32),
                   jax.ShapeDtypeStruct((seq, d_model), F32), jax.ShapeDtypeStruct((mix, d_model), F32),
                   jax.ShapeDtypeStruct((SUBLANES, d_model), F32), jax.ShapeDtypeStruct((CONV_K * SUBLANES, conv_w), F32),
                   jax.ShapeDtypeStruct((SUBLANES, d_model), F32)),
        in_specs=[row(attn_w), row(LANES), row(attn_w), row(4 * conv_w),
                  pl.BlockSpec((SUBLANES, 4 * conv_w), lambda i: (jnp.maximum(i * hb - 1, 0), 0)),
                  row(d_model), row(d_model), _const_spec((mix, d_model)), _const_spec((1, d_model)),
                  _const_spec((SUBLANES, conv_w))],
        out_specs=(row(attn_w), row(LANES), _perm_tile_spec(attn_w, tm), _perm_tile_spec(LANES, tm), _perm_tile_spec(LANES, tm),
                   row(attn_w), row(conv_w), row(conv_w), row(conv_w), row(d_model),
                   whole(mix, d_model), whole(SUBLANES, d_model), whole(CONV_K * SUBLANES, conv_w),
                   whole(SUBLANES, d_model)),
        scratch_shapes=[pltpu.VMEM(_stage_shape(len(groups) + 2, tm), F32)],
        compiler_params=_params(("arbitrary",)),
    )(o, lse, ga, cz, cz, x, tgt, w_out, g2, cw)


def dz_dx(nat_grads, perm_grads, dga, dcb, dgc, dcv, cz, tables, x, g1, e, w_full, cw):
    seq, d_model = x.shape
    attn_w = dga.shape[1]
    conv_w = dcv.shape[1]
    width = w_full.shape[2]
    in_w = 4 * attn_w + 4 * conv_w
    groups = _lane_groups(attn_w)
    tm = DZ_ROW_TILE
    nt = seq // tm
    hb = tm // SUBLANES

    def body(dq_ref, dk_ref, dv_ref, dqp_ref, dkp_ref, dvp_ref, dga_ref, dcb_ref, dgc_ref, dcv_ref, nh_ref, cz_ref,
             cos_ref, s1_ref, s2_ref, x_ref, g_ref, e_ref, w_ref, cw_ref, gx_ref, dz_ref, dg_ref, stage):
        i = pl.program_id(0)

        @pl.when(i == 0)
        def _():
            dg_ref[...] = jnp.zeros_like(dg_ref)

        cos, s1, s2 = cos_ref[...], s1_ref[...], s2_ref[...]
        for t, (nat_ref, perm_ref) in enumerate(((dq_ref, dqp_ref), (dk_ref, dkp_ref), (dv_ref, dvp_ref))):
            for g, sl in enumerate(groups):
                _from_perm(perm_ref, sl, stage, g)
            for g, sl in enumerate(groups):
                tot = nat_ref[:, sl].astype(F32) + _stage_get(stage, g)
                if t < 2:
                    tot = _rope_transposed(tot, cos, s1, s2)
                dz_ref[:, t * attn_w + g * LANES:t * attn_w + (g + 1) * LANES] = tot.astype(BF16)
        dz_ref[:, 3 * attn_w:4 * attn_w] = dga_ref[...]
        dcv = dcv_ref[...]
        nh = jnp.where(i < nt - 1, nh_ref[...], 0.0)
        w0, w1, w2 = cw_ref[0:1, :], cw_ref[1:2, :], cw_ref[2:3, :]
        du = dcv * w2 + _shift_up(dcv, nh, 1) * w1 + _shift_up(dcv, nh, 2) * w0
        base = 4 * attn_w
        dz_ref[:, base:base + conv_w] = (du * cz_ref[:, 2 * conv_w:3 * conv_w]).astype(BF16)
        dz_ref[:, base + conv_w:base + 2 * conv_w] = dcb_ref[...]
        dz_ref[:, base + 2 * conv_w:base + 3 * conv_w] = (du * cz_ref[:, 0:conv_w]).astype(BF16)
        dz_ref[:, base + 3 * conv_w:base + 4 * conv_w] = dgc_ref[...]

        dh = _nt(dz_ref[:, 0:width], w_ref[0])
        for j in range(1, N_CHIPS):
            dh = dh + _nt(dz_ref[:, j * width:(j + 1) * width], w_ref[j])
        xv = x_ref[...]
        r1 = lax.rsqrt(jnp.mean(xv * xv, axis=-1, keepdims=True) + NORM_EPS)
        xhat = xv * r1
        dg_ref[...] += _rowgroup_sum(dh * xhat)
        dhg = dh * g_ref[...]
        gx_ref[...] = r1 * (dhg - xhat * jnp.mean(dhg * xhat, axis=-1, keepdims=True)) + e_ref[...]

    row = lambda n: pl.BlockSpec((tm, n), lambda i: (i, 0))
    whole = lambda a, b: pl.BlockSpec((a, b), lambda i: (0, 0))
    pt = _perm_tile_spec(attn_w, tm)
    return pl.pallas_call(
        body, name="dz_dx", grid=(nt,),
        out_shape=(jax.ShapeDtypeStruct((seq, d_model), F32), jax.ShapeDtypeStruct((seq, in_w), BF16),
                   jax.ShapeDtypeStruct((SUBLANES, d_model), F32)),
        in_specs=[row(attn_w), row(attn_w), row(attn_w), pt, pt, pt, row(attn_w), row(conv_w), row(conv_w), row(conv_w),
                  pl.BlockSpec((SUBLANES, conv_w), lambda i: (jnp.minimum((i + 1) * hb, seq // SUBLANES - 1), 0)),
                  row(4 * conv_w), row(LANES), row(LANES), row(LANES), row(d_model), _const_spec((1, d_model)), row(d_model),
                  _const_spec(w_full.shape), _const_spec((SUBLANES, conv_w))],
        out_specs=(row(d_model), row(in_w), whole(SUBLANES, d_model)),
        scratch_shapes=[pltpu.VMEM(_stage_shape(len(groups), tm), F32)],
        compiler_params=_params(("arbitrary",)),
    )(*nat_grads, *perm_grads, dga, dcb, dgc, dcv, dcv, cz, *tables, x, g1, e, w_full, cw)


def dw_in_reduce(ht, dz):
    d_model, seq = ht.shape
    half = dz.shape[1] // N_DEV
    ts = min(2048, seq)
    steps = seq // ts
    x, y, c = lax.axis_index("x"), lax.axis_index("y"), lax.axis_index("c")
    chips = jnp.stack([2 * px + py for px, py in _chip_peers(x, y)] + [2 * x + y]).astype(jnp.int32)
    order = jnp.concatenate([2 * chips + (1 - c), 2 * chips + c])

    def body(order_ref, ht_ref, dz_ref, out_ref, acc, theirs, staged, contrib, resbuf, out_sem, sa, ra, sb, rb, sc, rc):
        del order_ref
        p, s = pl.program_id(0), pl.program_id(1)
        x, y, c = lax.axis_index("x"), lax.axis_index("y"), lax.axis_index("c")
        sib = (x, y, 1 - c)
        peers = _chip_peers(x, y)
        slot = p % 2

        def a_copy(k):
            return pltpu.make_async_remote_copy(src_ref=acc.at[k % 2], dst_ref=theirs.at[k], send_sem=sa.at[k], recv_sem=ra.at[k],
                                                device_id=sib, device_id_type=MESH)

        def b_copy(k):
            px, py = peers[k]
            return pltpu.make_async_remote_copy(src_ref=staged.at[k], dst_ref=contrib.at[k], send_sem=sb.at[k], recv_sem=rb.at[k],
                                                device_id=(px, py, c), device_id_type=MESH)

        def c_copy(which):
            return pltpu.make_async_remote_copy(src_ref=resbuf.at[which], dst_ref=resbuf.at[which], send_sem=sc, recv_sem=rc,
                                                device_id=sib, device_id_type=MESH)

        @pl.when(s == 0)
        def _():
            for k in range(N_CHIPS):
                @pl.when(p == k + 2)
                def _():
                    a_copy(k).wait_send()
            acc[slot] = jnp.zeros((d_model, half), F32)

        acc[slot] += jnp.dot(ht_ref[...], dz_ref[...], preferred_element_type=F32)

        @pl.when(s == steps - 1)
        def _():
            for k in range(N_CHIPS):
                @pl.when(p == k)
                def _():
                    a_copy(k).start()
            for k in range(N_CHIPS - 1):
                @pl.when(p == N_CHIPS + k)
                def _():
                    a_copy(k).wait_recv()
                    staged[k] = (acc[slot] + theirs[k]).astype(BF16)
                    b_copy(k).start()

            @pl.when(p == N_DEV - 1)
            def _():
                a_copy(N_CHIPS - 1).wait_recv()
                tot = acc[slot] + theirs[N_CHIPS - 1]
                for k in range(N_CHIPS - 1):
                    b_copy(k).wait_recv()
                    tot = tot + contrib[k].astype(F32)
                resbuf[c] = tot
                c_copy(c).start()
                c_copy(1 - c).wait_recv()
                done = pltpu.make_async_copy(resbuf, out_ref, out_sem)
                done.start()
                for k in range(N_CHIPS - 1):
                    b_copy(k).wait_send()
                c_copy(c).wait_send()
                done.wait()

    dma = pltpu.SemaphoreType.DMA
    grid_spec = pltpu.PrefetchScalarGridSpec(
        num_scalar_prefetch=1, grid=(N_DEV, steps),
        in_specs=[pl.BlockSpec((d_model, ts), lambda p, s, order_ref: (0, s)),
                  pl.BlockSpec((ts, half), lambda p, s, order_ref: (s, order_ref[p]))],
        out_specs=pl.BlockSpec(memory_space=pl.ANY),
        scratch_shapes=[pltpu.VMEM((2, d_model, half), F32), pltpu.VMEM((N_CHIPS, d_model, half), F32),
                        pltpu.VMEM((N_CHIPS - 1, d_model, half), BF16), pltpu.VMEM((N_CHIPS - 1, d_model, half), BF16),
                        pltpu.VMEM((2, d_model, half), F32), dma,
                        dma((N_CHIPS,)), dma((N_CHIPS,)), dma((N_CHIPS - 1,)), dma((N_CHIPS - 1,)), dma, dma])
    return pl.pallas_call(
        body, name="dw_in_reduce", grid_spec=grid_spec,
        out_shape=jax.ShapeDtypeStruct((2, d_model, half), F32),
        compiler_params=_params(("arbitrary", "arbitrary")),
    )(order, ht, dz)


def grad_reduce(tensors, small):
    nt = len(tensors)
    split = [g.reshape(N_CHIPS, 2, *g.shape[1:]) for g in tensors]
    shapes = [g.shape[2:] for g in split]

    def body(*refs):
        srcs, sm_ref = refs[:nt], refs[nt]
        res, rs_ref = refs[nt + 1:2 * nt + 1], refs[2 * nt + 1]
        scratch = refs[2 * nt + 2:]
        mine, theirs, staged, contrib = (scratch[k * nt:(k + 1) * nt] for k in range(4))
        sbuf, loc_sems, sa, ra, sb, rb, sc, rc, ss, rs = scratch[4 * nt:]
        x, y, c = lax.axis_index("x"), lax.axis_index("y"), lax.axis_index("c")
        me = 2 * x + y
        sib = (x, y, 1 - c)

        flips = [(fx, fy, fc) for fx in (0, 1) for fy in (0, 1) for fc in (0, 1)][1:]
        my8 = 4 * x + 2 * y + c
        sbuf[my8] = sm_ref[...]

        def small_copy(k, slot, to):
            return pltpu.make_async_remote_copy(src_ref=sm_ref, dst_ref=sbuf.at[slot], send_sem=ss.at[k], recv_sem=rs.at[k],
                                                device_id=to, device_id_type=MESH)

        sends = []
        for k, (fx, fy, fc) in enumerate(flips):
            px, py, pc = _flip(x, fx), _flip(y, fy), _flip(c, fc)
            sends.append(small_copy(k, my8, (px, py, pc)))
            sends[-1].start()

        def a_copy(t, j):
            return pltpu.make_async_remote_copy(src_ref=srcs[t].at[j, 1 - c], dst_ref=theirs[t].at[j], send_sem=sa.at[t, j],
                                                recv_sem=ra.at[t, j], device_id=sib, device_id_type=MESH)

        peers = _chip_peers(x, y)
        order = [2 * px + py for px, py in peers] + [me]
        loads = [[pltpu.make_async_copy(srcs[t].at[j, c], mine[t].at[j], loc_sems.at[t, j]) for t in range(nt)] for j in order]
        for pos, j in enumerate(order):
            for t in range(nt):
                loads[pos][t].start()
                sends.append(a_copy(t, j))
                sends[-1].start()

        def b_copy(k, t, piece, slot, to):
            return pltpu.make_async_remote_copy(src_ref=staged[t].at[piece], dst_ref=contrib[t].at[slot], send_sem=sb.at[k, t],
                                                recv_sem=rb.at[k, t], device_id=to, device_id_type=MESH)

        for k, (px, py) in enumerate(peers):
            j = 2 * px + py
            for t in range(nt):
                loads[k][t].wait()
                a_copy(t, j).wait_recv()
                staged[t][j] = (mine[t][j] + theirs[t][j]).astype(BF16)
                sends.append(b_copy(k, t, j, me, (px, py, c)))
                sends[-1].start()
        for t in range(nt):
            loads[len(peers)][t].wait()
            a_copy(t, me).wait_recv()
            mine[t][me] = mine[t][me] + theirs[t][me]
            contrib[t][me] = mine[t][me].astype(BF16)
        for k, (px, py) in enumerate(peers):
            for t in range(nt):
                b_copy(k, t, me, 2 * px + py, (px, py, c)).wait_recv()

        def c_copy(t, half):
            return pltpu.make_async_remote_copy(src_ref=res[t].at[half], dst_ref=res[t].at[half], send_sem=sc.at[t],
                                                recv_sem=rc.at[t], device_id=sib, device_id_type=MESH)

        for t in range(nt):
            own = mine[t][me]
            term = lambda j: jnp.where(me == j, own, contrib[t][j].astype(F32))
            res[t][c] = ((term(0) + term(1)) + term(2)) + term(3)
            sends.append(c_copy(t, c))
            sends[-1].start()
        for t in range(nt):
            c_copy(t, 1 - c).wait_recv()

        for k, (fx, fy, fc) in enumerate(flips):
            px, py, pc = _flip(x, fx), _flip(y, fy), _flip(c, fc)
            small_copy(k, 4 * px + 2 * py + pc, (px, py, pc)).wait_recv()
        tot = sbuf[0]
        for d in range(1, N_DEV):
            tot = tot + sbuf[d]
        rs_ref[...] = tot
        for cp in sends:
            cp.wait_send()

    vm = pl.BlockSpec(memory_space=pltpu.VMEM)
    anyspace = pl.BlockSpec(memory_space=pl.ANY)
    dma = pltpu.SemaphoreType.DMA
    bufs = [pltpu.VMEM((N_CHIPS, *shp), dt) for dt in (F32, F32, BF16, BF16) for shp in shapes]
    outs = pl.pallas_call(
        body, name="grad_reduce",
        out_shape=(*[jax.ShapeDtypeStruct((2, *shp), F32) for shp in shapes], jax.ShapeDtypeStruct(small.shape, F32)),
        in_specs=[anyspace] * nt + [vm], out_specs=tuple([vm] * (nt + 1)),
        scratch_shapes=[*bufs, pltpu.VMEM((N_DEV, *small.shape), F32),
                        dma((nt, N_CHIPS)), dma((nt, N_CHIPS)), dma((nt, N_CHIPS)), dma((3, nt)), dma((3, nt)), dma((nt,)), dma((nt,)),
                        dma((N_DEV - 1,)), dma((N_DEV - 1,))],
        compiler_params=_params(),
    )(*split, small)
    return outs[:nt], outs[nt]


def _adam_math(w, g, m, v):
    m = ADAM_B1 * m + (1.0 - ADAM_B1) * g
    v = ADAM_B2 * v + (1.0 - ADAM_B2) * (g * g)
    m_hat = m / (1.0 - ADAM_B1 ** ADAM_STEP)
    v_hat = v / (1.0 - ADAM_B2 ** ADAM_STEP)
    delta = -ADAM_LR * (m_hat / (jnp.sqrt(v_hat) + ADAM_EPS) + ADAM_WD * w)
    return delta, m, v


def adam_shard(name, w, g2, m, v, block, grid, w_map, g_map):
    def body(w_ref, g_ref, m_ref, v_ref, go_ref, d_ref, mo_ref, vo_ref):
        g = g_ref[0]
        delta, mn, vn = _adam_math(w_ref[...], g, m_ref[...], v_ref[...])
        go_ref[...] = g
        d_ref[...] = delta
        mo_ref[...] = mn
        vo_ref[...] = vn

    ws = pl.BlockSpec(block, w_map)
    shp = jax.ShapeDtypeStruct(w.shape, F32)
    return pl.pallas_call(
        body, name=name, grid=grid, out_shape=(shp, shp, shp, shp),
        in_specs=[ws, pl.BlockSpec((1, *block), g_map), ws, ws], out_specs=(ws, ws, ws, ws),
        compiler_params=_params(("arbitrary",) * len(grid)),
    )(w, g2, m, v)


def adam_small(ws, gs, ms, vs):
    n = len(ws)

    def body(*refs):
        ins, outs = refs[:4 * n], refs[4 * n:]
        for t in range(n):
            delta, mn, vn = _adam_math(ins[t][...], ins[n + t][...], ins[2 * n + t][...], ins[3 * n + t][...])
            outs[3 * t][...] = delta
            outs[3 * t + 1][...] = mn
            outs[3 * t + 2][...] = vn

    vm = pl.BlockSpec(memory_space=pltpu.VMEM)
    outs = pl.pallas_call(
        body, name="adam_small",
        out_shape=tuple(jax.ShapeDtypeStruct(w.shape, F32) for w in ws for _ in range(3)),
        in_specs=[vm] * (4 * n), out_specs=tuple([vm] * (3 * n)),
        compiler_params=_params(),
    )(*ws, *gs, *ms, *vs)
    return [outs[3 * t:3 * t + 3] for t in range(n)]


def kernel(x, norm_pre_g, w_in, conv_w, w_out, norm_post_g, loss_target, m_norm_pre_g, m_w_in, m_conv_w, m_w_out, m_norm_post_g, v_norm_pre_g, v_w_in, v_conv_w, v_w_out, v_norm_post_g):
    _, seq, d_model = x.shape
    width = w_in.shape[1]
    conv_q = conv_w.shape[1]
    conv_width = N_CHIPS * conv_q
    attn_width = d_model - conv_width
    xs, tg = x[0], loss_target[0]
    g1, g2 = norm_pre_g.reshape(1, d_model), norm_post_g.reshape(1, d_model)

    w_full, wout_full, cw_full = gather_weights(w_in, w_out, conv_w)
    wout2 = wout_full.reshape(attn_width + conv_width, d_model)
    cw = jnp.zeros((SUBLANES, conv_width), F32).at[:CONV_K].set(
        cw_full[:, :CONV_K, :conv_q].transpose(1, 0, 2).reshape(CONV_K, conv_width))
    tables = _rope_tables(seq)

    ht, q, k, v, qp, kp, vp, ga, cz = inproj(xs, g1, w_full, tables, attn_width, conv_width)
    run = attn_fwd("p4", qp, kp, vp, None)
    run = attn_fwd("p16", qp, kp, vp, run)
    o, lse = attn_fwd("nat", q, k, v, run)
    (d_o, delta, d_op, delta_p, lse_p, dga, dcb, dgc, dcv, e, dwout, dg2, dcw, loss_acc) = tail(
        o, lse, ga, cz, xs, tg, wout2, g2, cw)
    nat_grads = attn_bwd("nat", q, k, v, d_o, lse, delta, None)
    perm_grads = attn_bwd("p4", qp, kp, vp, d_op, lse_p, delta_p, None)
    perm_grads = attn_bwd("p16", qp, kp, vp, d_op, lse_p, delta_p, perm_grads)
    grad_x, dz, dg1 = dz_dx(nat_grads, perm_grads, dga, dcb, dgc, dcv, cz, tables, xs, g1, e, w_full, cw)

    small = jnp.zeros((SUBLANES, d_model), F32)
    small = small.at[0].set(dg1.sum(axis=0)).at[1].set(dg2.sum(axis=0))
    small = small.at[2:2 + CONV_K, :conv_width].set(dcw.reshape(CONV_K, SUBLANES, conv_width).sum(axis=1))
    small = small.at[2 + CONV_K, 0].set(jnp.sum(loss_acc))
    (rout,), rsmall = grad_reduce([dwout.reshape(N_DEV, -1, d_model)], small)
    rin = dw_in_reduce(ht, dz)

    half = width // 2
    tr = 256
    gw_in, d_in, m_in, v_in = adam_shard(
        "adam_w_in", w_in, rin, m_w_in, v_w_in, (tr, half), (2, d_model // tr),
        lambda hf, i: (i, hf), lambda hf, i: (hf, i, 0))
    rq = w_out.shape[0] // 2
    gw_out, d_out, m_out, v_out = adam_shard(
        "adam_w_out", w_out, rout, m_w_out, v_w_out, (rq, d_model), (2,),
        lambda hf: (hf, 0), lambda hf: (hf, 0, 0))

    chip = 2 * lax.axis_index("x") + lax.axis_index("y")
    g_pre, g_post = rsmall[0:1], rsmall[1:2]
    g_conv = lax.dynamic_slice(rsmall[2:2 + CONV_K, :conv_width], (0, chip * conv_q), (CONV_K, conv_q))
    (d_pre, m_pre, v_pre), (d_post, m_post, v_post), (d_cv, m_cv, v_cv) = adam_small(
        [g1, g2, conv_w], [g_pre, g_post, g_conv],
        [m_norm_pre_g.reshape(1, d_model), m_norm_post_g.reshape(1, d_model), m_conv_w],
        [v_norm_pre_g.reshape(1, d_model), v_norm_post_g.reshape(1, d_model), v_conv_w])

    loss = 0.5 * rsmall[2 + CONV_K, 0] / d_model
    vec = lambda a: a.reshape(d_model)
    return (loss, grad_x.reshape(1, seq, d_model),
            vec(g_pre), gw_in, g_conv, gw_out, vec(g_post),
            vec(d_pre), d_in, d_cv, d_out, vec(d_post),
            vec(m_pre), m_in, m_cv, m_out, vec(m_post),
            vec(v_pre), v_in, v_cv, v_out, vec(v_post))
```

```python
import jax
import jax.numpy as jnp
from jax import lax
from jax.experimental import pallas as pl
from jax.experimental.pallas import tpu as pltpu

HEAD_DIM = 64
LANES = 128
SUBLANES = 8
BLOCK = 128
WINDOW_KEYS = 128
PERM = 16
PJ = 4
P4_ROWS = BLOCK // PJ
ROW_TILE = 512
DZ_ROW_TILE = 512
CONV_K = 3
ROPE_THETA = 10000.0
NORM_EPS = 1e-6
ATTN_SCALE = HEAD_DIM ** -0.5
NEG = -1e30
N_CHIPS = 4
N_DEV = 8
MESH = pl.DeviceIdType.MESH
ADAM_LR = 0.001
ADAM_B1 = 0.9
ADAM_B2 = 0.999
ADAM_EPS = 1e-08
ADAM_WD = 0.01
ADAM_STEP = 10
VMEM_LIMIT = 52 * 1024 * 1024

F32 = jnp.float32
BF16 = jnp.bfloat16


def _params(sem=None, **kw):
    return pltpu.CompilerParams(dimension_semantics=sem, vmem_limit_bytes=VMEM_LIMIT, **kw)


def _const_spec(shape):
    return pl.BlockSpec(shape, lambda *_: (0,) * len(shape), pipeline_mode=pl.Buffered(1))


def _sigmoid(z):
    return 1.0 / (1.0 + jnp.exp(-z))


def _rowgroup_sum(a):
    rows, n = a.shape
    return a.reshape(rows // SUBLANES, SUBLANES, n).sum(axis=0)


def _nt(a, b):
    return lax.dot_general(a, b, (((1,), (1,)), ((), ())), preferred_element_type=F32)


def _tn(a, b):
    return lax.dot_general(a, b, (((0,), (0,)), ((), ())), preferred_element_type=F32)


def _col_pieces(a, b, width):
    out = []
    while a < b:
        j = a // width
        e = min(b, (j + 1) * width)
        out.append((j, a - j * width, e - j * width))
        a = e
    return out


def _lane_groups(width):
    return [slice(g * LANES, (g + 1) * LANES) for g in range(width // LANES)]


def _perm_shape(seq, width):
    return (PJ, PJ, seq // PERM, width)


def _perm_tile_spec(width, tm):
    return pl.BlockSpec((PJ, PJ, tm // PERM, width), lambda i: (0, 0, i, 0))


STAGE_PITCH = 24


def _stage_shape(groups, rows):
    return (groups, rows // PERM * STAGE_PITCH, LANES)


def _stage_put(stage, g, val, row0=0):
    for a in range(val.shape[0] // PERM):
        at = (row0 // PERM + a) * STAGE_PITCH
        stage[g, at:at + PERM, :] = val[a * PERM:(a + 1) * PERM]


def _stage_get(stage, g):
    return jnp.concatenate([stage[g, a * STAGE_PITCH:a * STAGE_PITCH + PERM, :]
                            for a in range(stage.shape[1] // STAGE_PITCH)], axis=0)


def _to_perm(stage, g, dst_ref, sl, dtype):
    rows = stage.shape[1] // STAGE_PITCH
    for b in range(PERM):
        dst_ref[b // PJ, b % PJ, :, sl] = stage[g, pl.ds(b, rows, stride=STAGE_PITCH), :].astype(dtype)


def _from_perm(src_ref, sl, stage, g):
    rows = stage.shape[1] // STAGE_PITCH
    for b in range(PERM):
        stage[g, pl.ds(b, rows, stride=STAGE_PITCH), :] = src_ref[b // PJ, b % PJ, :, sl].astype(F32)


def _flip(a, f):
    return 1 - a if f else a


def _chip_peers(x, y):
    return [(1 - x, y), (x, 1 - y), (1 - x, 1 - y)]


def gather_weights(w_in, w_out, conv_w):
    d_model, width = w_in.shape
    rows = w_out.shape[0]
    cw = jnp.zeros((SUBLANES, LANES), F32).at[:CONV_K, :conv_w.shape[1]].set(conv_w)

    def body(win_ref, wout_ref, cw_ref, winf_ref, woutf_ref, cwf_ref, st_in, st_out, ici_send, ici_recv, d2d_send, d2d_recv):
        x, y, c = lax.axis_index("x"), lax.axis_index("y"), lax.axis_index("c")
        me = 2 * x + y
        sib = (x, y, 1 - c)
        st_in[...] = win_ref[...].astype(BF16)
        st_out[...] = wout_ref[...].astype(BF16)
        winf_ref[me] = st_in[...]
        woutf_ref[me] = st_out[...]
        cwf_ref[me] = cw_ref[...]
        stages = (st_in, st_out)
        fulls = (winf_ref, woutf_ref)
        halves = (d_model // 2, rows // 2)

        def half(t, core):
            return pl.ds(pl.multiple_of(core * halves[t], halves[t]), halves[t])

        def ici(k, t, slot, to, core):
            src = stages[t].at[half(t, core)] if t < 2 else cw_ref
            dst = fulls[t].at[slot, half(t, core)] if t < 2 else cwf_ref.at[slot]
            return pltpu.make_async_remote_copy(src_ref=src, dst_ref=dst, send_sem=ici_send.at[k, t], recv_sem=ici_recv.at[k, t],
                                                device_id=to, device_id_type=MESH)

        def d2d(k, t, slot, core):
            ref = fulls[t].at[slot, half(t, core)]
            return pltpu.make_async_remote_copy(src_ref=ref, dst_ref=ref, send_sem=d2d_send.at[k, t], recv_sem=d2d_recv.at[k, t],
                                                device_id=sib, device_id_type=MESH)

        peers = _chip_peers(x, y)
        sends = [ici(k, t, me, (px, py, c), c) for k, (px, py) in enumerate(peers) for t in range(3)]
        for cp in sends:
            cp.start()
        for k, (px, py) in enumerate(peers):
            for t in range(2):
                ici(k, t, 2 * px + py, (px, py, c), c).wait_recv()
                fwd = d2d(k, t, 2 * px + py, c)
                fwd.start()
                sends.append(fwd)
            ici(k, 2, 2 * px + py, (px, py, c), c).wait_recv()
        for k, (px, py) in enumerate(peers):
            for t in range(2):
                d2d(k, t, 2 * px + py, 1 - c).wait_recv()
        for cp in sends:
            cp.wait_send()

    vm = pl.BlockSpec(memory_space=pltpu.VMEM)
    dma = pltpu.SemaphoreType.DMA
    return pl.pallas_call(
        body, name="gather_weights",
        out_shape=(jax.ShapeDtypeStruct((N_CHIPS, d_model, width), BF16),
                   jax.ShapeDtypeStruct((N_CHIPS, rows, d_model), BF16),
                   jax.ShapeDtypeStruct((N_CHIPS, SUBLANES, LANES), F32)),
        in_specs=[vm, vm, vm], out_specs=(vm, vm, vm),
        scratch_shapes=[pltpu.VMEM((d_model, width), BF16), pltpu.VMEM((rows, d_model), BF16),
                        dma((3, 3)), dma((3, 3)), dma((3, 2)), dma((3, 2))],
        compiler_params=_params(),
    )(w_in, w_out, cw)


def _rope_tables(seq):
    half = HEAD_DIM // 2
    inv_freq = ROPE_THETA ** (-jnp.arange(half, dtype=F32) * 2.0 / HEAD_DIM)
    ang = jnp.arange(seq).astype(F32)[:, None] * jnp.tile(inv_freq, LANES // half)[None, :]
    first_half = (jnp.arange(LANES) % HEAD_DIM < half)[None, :]
    sin = jnp.sin(ang)
    return jnp.cos(ang), jnp.where(first_half, -sin, 0.0), jnp.where(first_half, 0.0, sin)


def _rope(t, cos, s1, s2):
    return t * cos + pltpu.roll(t, LANES - HEAD_DIM // 2, 1) * s1 + pltpu.roll(t, HEAD_DIM // 2, 1) * s2


def _rope_transposed(g, cos, s1, s2):
    return g * cos + pltpu.roll(g * s1, HEAD_DIM // 2, 1) + pltpu.roll(g * s2, LANES - HEAD_DIM // 2, 1)


def inproj(x, g1, w_full, tables, attn_w, conv_w):
    seq, d_model = x.shape
    width = w_full.shape[2]
    tm = ROW_TILE
    groups = _lane_groups(attn_w)

    def body(x_ref, g_ref, w_ref, cos_ref, s1_ref, s2_ref,
             ht_ref, q_ref, k_ref, v_ref, qp_ref, kp_ref, vp_ref, ga_ref, cz_ref, stage):
        xv = x_ref[...]
        hb = ((xv * lax.rsqrt(jnp.mean(xv * xv, axis=-1, keepdims=True) + NORM_EPS)) * g_ref[...]).astype(BF16)
        ht_ref[...] = jnp.transpose(hb)
        cos, s1, s2 = cos_ref[...], s1_ref[...], s2_ref[...]

        def proj(a, b):
            parts = [jnp.dot(hb, w_ref[j, :, lo:hi], preferred_element_type=F32) for j, lo, hi in _col_pieces(a, b, width)]
            return parts[0] if len(parts) == 1 else jnp.concatenate(parts, axis=1)

        def emit(z, nat_ref, perm_ref, fn):
            for g, sl in enumerate(groups):
                val = fn(z[:, sl])
                nat_ref[:, sl] = val.astype(BF16)
                _stage_put(stage, g, val)
            for g, sl in enumerate(groups):
                _to_perm(stage, g, perm_ref, sl, BF16)

        emit(proj(0, attn_w), q_ref, qp_ref, lambda t: _rope(t, cos, s1, s2) * ATTN_SCALE)
        emit(proj(attn_w, 2 * attn_w), k_ref, kp_ref, lambda t: _rope(t, cos, s1, s2))
        emit(proj(2 * attn_w, 3 * attn_w), v_ref, vp_ref, lambda t: t)
        ga_ref[...] = proj(3 * attn_w, 4 * attn_w)
        cz_ref[...] = proj(4 * attn_w, 4 * attn_w + 4 * conv_w)

    row = lambda n: pl.BlockSpec((tm, n), lambda i: (i, 0))
    nat = jax.ShapeDtypeStruct((seq, attn_w), BF16)
    perm = jax.ShapeDtypeStruct(_perm_shape(seq, attn_w), BF16)
    return pl.pallas_call(
        body, name="inproj", grid=(seq // tm,),
        out_shape=(jax.ShapeDtypeStruct((d_model, seq), BF16), nat, nat, nat, perm, perm, perm,
                   jax.ShapeDtypeStruct((seq, attn_w), F32), jax.ShapeDtypeStruct((seq, 4 * conv_w), F32)),
        in_specs=[row(d_model), _const_spec((1, d_model)), _const_spec(w_full.shape), row(LANES), row(LANES), row(LANES)],
        out_specs=(pl.BlockSpec((d_model, tm), lambda i: (0, i)), row(attn_w), row(attn_w), row(attn_w),
                   _perm_tile_spec(attn_w, tm), _perm_tile_spec(attn_w, tm), _perm_tile_spec(attn_w, tm),
                   row(attn_w), row(4 * conv_w)),
        scratch_shapes=[pltpu.VMEM(_stage_shape(len(groups), tm), F32)],
        compiler_params=_params(("arbitrary",)),
    )(x, g1, w_full, *tables)


class _Mode:
    def __init__(self, name, seq):
        self.name = name
        if name == "nat":
            self.residues, self.nb = 1, seq // BLOCK
        elif name == "p16":
            self.residues, self.nb = PERM, seq // PERM // BLOCK
        else:
            self.residues, self.nb = PJ, seq // PERM // P4_ROWS

    def spec(self, width, which, last=None):
        if which == "prev":
            blk = lambda n: jnp.maximum(n - 1, 0)
        elif last is None:
            blk = lambda n: n
        else:
            blk = lambda n: jnp.minimum(n, last)
        if self.name == "nat":
            return pl.BlockSpec((BLOCK, width), lambda r, n: (blk(n), 0))
        if self.name == "p16":
            return pl.BlockSpec((1, 1, BLOCK, width), lambda r, n: (r // PJ, r % PJ, blk(n), 0))
        return pl.BlockSpec((PJ, 1, P4_ROWS, width), lambda r, n: (0, r, blk(n), 0))

    def get(self, ref, sl):
        if self.name == "nat":
            return ref[:, sl]
        if self.name == "p16":
            return ref[0, 0, :, sl]
        return jnp.concatenate([ref[j, 0, :, sl] for j in range(PJ)], axis=0)

    def put(self, ref, sl, val):
        val = val.astype(ref.dtype)
        if self.name == "nat":
            ref[:, sl] = val
        elif self.name == "p16":
            ref[0, 0, :, sl] = val
        else:
            for j in range(PJ):
                ref[j, 0, :, sl] = val[j * P4_ROWS:(j + 1) * P4_ROWS]

    def index(self, idx, is_key):
        if self.name != "p4":
            return idx - BLOCK if is_key else idx
        within = jnp.bitwise_and(idx, BLOCK - 1)
        m = PJ * jnp.bitwise_and(within, P4_ROWS - 1) + jnp.right_shift(within, P4_ROWS.bit_length() - 1)
        return m + BLOCK * (jnp.right_shift(idx, BLOCK.bit_length() - 1) - 1) if is_key else m

    def bias(self, n, keys_major):
        shape = (2 * BLOCK, BLOCK) if keys_major else (BLOCK, 2 * BLOCK)
        kdim = 0 if keys_major else 1
        kidx = lax.broadcasted_iota(jnp.int32, shape, kdim)
        qidx = lax.broadcasted_iota(jnp.int32, shape, 1 - kdim)
        rel = self.index(qidx, False) - self.index(kidx, True)
        valid = (rel >= 0) & (rel <= WINDOW_KEYS) & ((kidx >= BLOCK) | (n > 0))
        return jnp.where(valid, 0.0, NEG)


def _head_masks():
    lane = lax.broadcasted_iota(jnp.int32, (BLOCK, LANES), 1)
    lo = lane < HEAD_DIM
    return lane, lo, jnp.where(lo, 1.0, 0.0).astype(BF16), jnp.where(lo, 0.0, 1.0).astype(BF16)


def _column(blk, lane, h):
    return jnp.sum(jnp.where(lane == h, blk, 0.0), axis=1, keepdims=True)


def attn_fwd(name, q, k, v, run):
    nat = name == "nat"
    seq = q.shape[0] if nat else q.shape[2] * PERM
    attn_w = q.shape[-1]
    mode = _Mode(name, seq)
    groups = _lane_groups(attn_w)
    first = run is None
    all_lanes = slice(0, LANES)

    def body(*refs):
        q_ref, kp_ref, kc_ref, vp_ref, vc_ref = refs[:5]
        if first:
            o_ref, l_ref = refs[5:]
        elif nat:
            oin_ref, lin_ref, ex_ref, o_ref, l_ref, ostage, lstage = refs[5:]
        else:
            oin_ref, lin_ref, ex_ref, o_ref, l_ref = refs[5:]
        n = pl.program_id(1)
        bias = mode.bias(n, True)
        bias2 = jnp.concatenate([bias, bias], axis=1)
        _, lo, m_lo, m_hi = _head_masks()
        head_row = lax.broadcasted_iota(jnp.int32, (BLOCK, LANES), 0)
        lrows = jnp.zeros((BLOCK, LANES), F32)
        def probs(sl):
            q2 = mode.get(q_ref, sl)
            kcat = jnp.concatenate([mode.get(kp_ref, sl), mode.get(kc_ref, sl)], axis=0)
            vcat = jnp.concatenate([mode.get(vp_ref, sl), mode.get(vc_ref, sl)], axis=0)
            qq = jnp.concatenate([q2 * m_lo, q2 * m_hi], axis=0)
            s_t = _nt(kcat, qq) + bias2
            m = jnp.max(s_t, axis=0, keepdims=True)
            pe = jnp.exp(s_t - m)
            l = jnp.sum(pe, axis=0, keepdims=True)
            return vcat, (pe * (1.0 / l)).astype(BF16), m + jnp.log(l)

        def output(p, sl, vcat, pn, lse, lrows):
            o_new = _tn(pn, vcat)
            mode.put(o_ref, sl, jnp.where(lo, o_new[:BLOCK], o_new[BLOCK:]))
            lrows = jnp.where(head_row == 2 * p, lse[:, :BLOCK], lrows)
            return jnp.where(head_row == 2 * p + 1, lse[:, BLOCK:], lrows)

        pending = None
        for p, sl in enumerate(groups):
            nxt = probs(sl)
            if pending is not None:
                lrows = output(*pending, lrows)
            pending = (p, sl, *nxt)
        lrows = output(*pending, lrows)
        lblk = jnp.transpose(lrows)
        if first:
            mode.put(l_ref, all_lanes, lblk)
        else:
            if nat:
                for g, sl in enumerate(groups):
                    _from_perm(oin_ref, sl, ostage, g)
                _from_perm(lin_ref, all_lanes, lstage, 0)
                lin = _stage_get(lstage, 0)
            else:
                lin = mode.get(lin_ref, all_lanes)
            mx = jnp.maximum(lin, lblk)
            new = mx + jnp.log(jnp.exp(lin - mx) + jnp.exp(lblk - mx))
            mode.put(l_ref, all_lanes, new)

            def expand(w):
                hi = w.astype(BF16)
                rest = (w - hi.astype(F32)).astype(BF16)
                ex = ex_ref[...]
                return jnp.dot(hi, ex, preferred_element_type=F32) + jnp.dot(rest, ex, preferred_element_type=F32)

            w_prev, w_cur = expand(jnp.exp(lin - new)), expand(jnp.exp(lblk - new))
            for p, sl in enumerate(groups):
                o_prev = _stage_get(ostage, p) if nat else mode.get(oin_ref, sl)
                mode.put(o_ref, sl, w_prev[:, sl] * o_prev + w_cur[:, sl] * mode.get(o_ref, sl))

    ins = [q, k, k, v, v]
    specs = [mode.spec(attn_w, "cur"), mode.spec(attn_w, "prev"), mode.spec(attn_w, "cur"),
             mode.spec(attn_w, "prev"), mode.spec(attn_w, "cur")]
    scratch = []
    if not first:
        ins += list(run)
        if nat:
            rows8 = BLOCK // PERM
            specs += [pl.BlockSpec((PJ, PJ, rows8, attn_w), lambda r, n: (0, 0, n, 0)),
                      pl.BlockSpec((PJ, PJ, rows8, LANES), lambda r, n: (0, 0, n, 0))]
            scratch = [pltpu.VMEM(_stage_shape(len(groups), BLOCK), F32), pltpu.VMEM(_stage_shape(1, BLOCK), F32)]
        else:
            specs += [mode.spec(attn_w, "cur"), mode.spec(LANES, "cur")]
        head_of_lane = jnp.arange(attn_w, dtype=jnp.int32) // HEAD_DIM
        ins.append((jnp.arange(LANES, dtype=jnp.int32)[:, None] == head_of_lane[None, :]).astype(BF16))
        specs.append(_const_spec((LANES, attn_w)))
    if nat:
        out_shape = (jax.ShapeDtypeStruct((seq, attn_w), F32), jax.ShapeDtypeStruct((seq, LANES), F32))
    else:
        out_shape = (jax.ShapeDtypeStruct(_perm_shape(seq, attn_w), F32), jax.ShapeDtypeStruct(_perm_shape(seq, LANES), F32))
    return pl.pallas_call(
        body, name=f"attn_fwd_{name}", grid=(mode.residues, mode.nb),
        out_shape=out_shape, in_specs=specs, out_specs=(mode.spec(attn_w, "cur"), mode.spec(LANES, "cur")),
        scratch_shapes=scratch,
        compiler_params=_params(("arbitrary", "arbitrary")),
    )(*ins)


def attn_bwd(name, q, k, v, d_o, lse, delta, run):
    nat = name == "nat"
    seq = q.shape[0] if nat else q.shape[2] * PERM
    attn_w = q.shape[-1]
    mode = _Mode(name, seq)
    nb = mode.nb
    groups = _lane_groups(attn_w)
    first = run is None
    all_lanes = slice(0, LANES)

    def body(*refs):
        q_ref, kp_ref, kc_ref, vp_ref, vc_ref, do_ref, lse_ref, dl_ref = refs[:8]
        if first:
            dq_ref, dk_ref, dv_ref, ck, cv = refs[8:]
        else:
            dqi_ref, dki_ref, dvi_ref, dq_ref, dk_ref, dv_ref, ck, cv = refs[8:]
        n = pl.program_id(1)

        @pl.when(n == 0)
        def _():
            ck[...] = jnp.zeros_like(ck)
            cv[...] = jnp.zeros_like(cv)

        @pl.when(n < nb)
        def _():
            bias = mode.bias(n, True)
            bias2 = jnp.concatenate([bias, bias], axis=1)
            _, lo, m_lo, m_hi = _head_masks()
            lse_t = jnp.transpose(mode.get(lse_ref, all_lanes))
            dl_t = jnp.transpose(mode.get(dl_ref, all_lanes))
            def scores(p, sl):
                q2, do2 = mode.get(q_ref, sl), mode.get(do_ref, sl)
                kcat = jnp.concatenate([mode.get(kp_ref, sl), mode.get(kc_ref, sl)], axis=0)
                vcat = jnp.concatenate([mode.get(vp_ref, sl), mode.get(vc_ref, sl)], axis=0)
                qq = jnp.concatenate([q2 * m_lo, q2 * m_hi], axis=0)
                dd = jnp.concatenate([do2 * m_lo, do2 * m_hi], axis=0)
                h0 = 2 * p
                lse2 = jnp.concatenate([lse_t[h0:h0 + 1, :], lse_t[h0 + 1:h0 + 2, :]], axis=1)
                dl2 = jnp.concatenate([dl_t[h0:h0 + 1, :], dl_t[h0 + 1:h0 + 2, :]], axis=1)
                p_t = jnp.exp(_nt(kcat, qq) + (bias2 - lse2))
                ds_t = p_t * (_nt(vcat, dd) - dl2)
                return qq, dd, kcat, p_t.astype(BF16), ds_t.astype(BF16)

            def grads(sl, qq, dd, kcat, pb, dsb):
                dkc = jnp.dot(dsb, qq, preferred_element_type=F32)
                dvc = jnp.dot(pb, dd, preferred_element_type=F32)
                dqb = _tn(dsb, kcat)
                dq2 = jnp.where(lo, dqb[:BLOCK], dqb[BLOCK:]) * ATTN_SCALE
                dk2 = ck[:, sl] + dkc[:BLOCK]
                dv2 = cv[:, sl] + dvc[:BLOCK]
                if not first:
                    dq2 = dq2 + mode.get(dqi_ref, sl).astype(F32)
                    dk2 = dk2 + mode.get(dki_ref, sl).astype(F32)
                    dv2 = dv2 + mode.get(dvi_ref, sl).astype(F32)
                mode.put(dq_ref, sl, dq2)
                mode.put(dk_ref, sl, dk2)
                mode.put(dv_ref, sl, dv2)
                ck[:, sl] = dkc[BLOCK:]
                cv[:, sl] = dvc[BLOCK:]

            pending = None
            for p, sl in enumerate(groups):
                nxt = scores(p, sl)
                if pending is not None:
                    grads(*pending)
                pending = (sl, *nxt)
            grads(*pending)

        @pl.when(n == nb)
        def _():
            for sl in groups:
                if first:
                    mode.put(dk_ref, sl, ck[:, sl])
                    mode.put(dv_ref, sl, cv[:, sl])
                else:
                    mode.put(dk_ref, sl, ck[:, sl] + mode.get(dki_ref, sl).astype(F32))
                    mode.put(dv_ref, sl, cv[:, sl] + mode.get(dvi_ref, sl).astype(F32))

    last = nb - 1
    cur = lambda w: mode.spec(w, "cur", last)
    prev = lambda w: mode.spec(w, "prev")
    ins = [q, k, k, v, v, d_o, lse, delta]
    specs = [cur(attn_w), prev(attn_w), cur(attn_w), prev(attn_w), cur(attn_w), cur(attn_w), cur(LANES), cur(LANES)]
    if not first:
        ins += list(run)
        specs += [cur(attn_w), prev(attn_w), prev(attn_w)]
    shp = jax.ShapeDtypeStruct((seq, attn_w) if nat else _perm_shape(seq, attn_w), BF16)
    return pl.pallas_call(
        body, name=f"attn_bwd_{name}", grid=(mode.residues, nb + 1),
        out_shape=(shp, shp, shp), in_specs=specs, out_specs=(cur(attn_w), prev(attn_w), prev(attn_w)),
        scratch_shapes=[pltpu.VMEM((BLOCK, attn_w), F32), pltpu.VMEM((BLOCK, attn_w), F32)],
        compiler_params=_params(("arbitrary", "arbitrary")),
    )(*ins)


def _shift_down(u, halo, k):
    rolled = pltpu.roll(u, k, 0)
    row = lax.broadcasted_iota(jnp.int32, halo.shape, 0)
    top = jnp.where(row < k, pltpu.roll(halo, k, 0), rolled[:SUBLANES])
    return jnp.concatenate([top, rolled[SUBLANES:]], axis=0)


def _shift_up(u, halo, k):
    rows = u.shape[0]
    rolled = pltpu.roll(u, rows - k, 0)
    row = lax.broadcasted_iota(jnp.int32, halo.shape, 0)
    bot = jnp.where(row >= SUBLANES - k, pltpu.roll(halo, SUBLANES - k, 0), rolled[rows - SUBLANES:])
    return jnp.concatenate([rolled[:rows - SUBLANES], bot], axis=0)


def tail(o, lse, ga, cz, x, tgt, w_out, g2, cw):
    seq, d_model = x.shape
    attn_w = o.shape[1]
    conv_w = cz.shape[1] // 4
    mix = attn_w + conv_w
    groups = _lane_groups(attn_w)
    tm = ROW_TILE
    nt = seq // tm
    hb = tm // SUBLANES

    def body(o_ref, l_ref, ga_ref, cz_ref, hz_ref, x_ref, t_ref, w_ref, g_ref, cw_ref,
             do_ref, dl_ref, dop_ref, dlp_ref, lp_ref, dga_ref, dcb_ref, dgc_ref, dcv_ref, e_ref,
             dw_ref, dg_ref, dcw_ref, loss_ref, stage):
        i = pl.program_id(0)

        @pl.when(i == 0)
        def _():
            dw_ref[...] = jnp.zeros_like(dw_ref)
            dg_ref[...] = jnp.zeros_like(dg_ref)
            dcw_ref[...] = jnp.zeros_like(dcw_ref)
            loss_ref[...] = jnp.zeros_like(loss_ref)

        u = cz_ref[:, 2 * conv_w:3 * conv_w] * cz_ref[:, 0:conv_w]
        uh = hz_ref[:, 2 * conv_w:3 * conv_w] * hz_ref[:, 0:conv_w]
        uh = jnp.where(i > 0, uh, 0.0)
        u1 = _shift_down(u, uh, 1)
        u2 = _shift_down(u, uh, 2)
        w0, w1, w2 = cw_ref[0:1, :], cw_ref[1:2, :], cw_ref[2:3, :]
        cvv = u2 * w0 + u1 * w1 + u * w2
        gv = g_ref[...]
        all_lanes = slice(0, LANES)

        def forward(rs):
            ov, gav = o_ref[rs, :], ga_ref[rs, :]
            sig_a = _sigmoid(gav)
            silu_a = gav * sig_a
            cb, gc = cz_ref[rs, conv_w:2 * conv_w], cz_ref[rs, 3 * conv_w:4 * conv_w]
            sig_c = _sigmoid(gc)
            silu_c = gc * sig_c
            bc = cb * cvv[rs]
            mixed = jnp.concatenate([ov * silu_a, bc * silu_c], axis=1).astype(BF16)
            yv = jnp.dot(mixed, w_ref[...], preferred_element_type=F32)
            return ov, gav, sig_a, silu_a, cb, gc, sig_c, silu_c, bc, mixed, yv

        def loss_and_dy(rs, mixed, yv):
            r2 = lax.rsqrt(jnp.mean(yv * yv, axis=-1, keepdims=True) + NORM_EPS)
            yhat = yv * r2
            diff = (x_ref[rs, :] + yhat * gv) - t_ref[rs, :]
            loss_ref[...] += _rowgroup_sum(diff * diff)
            ev = diff * (1.0 / d_model)
            e_ref[rs, :] = ev
            dg_ref[...] += _rowgroup_sum(ev * yhat)
            eg = ev * gv
            dy = (r2 * (eg - yhat * jnp.mean(eg * yhat, axis=-1, keepdims=True))).astype(BF16)
            dw_ref[...] += _tn(mixed, dy)
            return _nt(dy, w_ref[...])

        def backward(rs, ov, gav, sig_a, silu_a, cb, gc, sig_c, silu_c, bc, dm):
            rows = rs.stop - rs.start
            dma, dmc = dm[:, :attn_w], dm[:, attn_w:]
            dov = dma * silu_a
            do_ref[rs, :] = dov.astype(BF16)
            dga_ref[rs, :] = (dma * ov * (sig_a * (1.0 + gav * (1.0 - sig_a)))).astype(BF16)
            prod = dov * ov
            lane = lax.broadcasted_iota(jnp.int32, (rows, LANES), 1)
            lo = lane < HEAD_DIM
            dblk = jnp.zeros((rows, LANES), F32)
            for p, sl in enumerate(groups):
                pr = prod[:, sl]
                dblk = jnp.where(lane == 2 * p, jnp.sum(jnp.where(lo, pr, 0.0), axis=1, keepdims=True), dblk)
                dblk = jnp.where(lane == 2 * p + 1, jnp.sum(jnp.where(lo, 0.0, pr), axis=1, keepdims=True), dblk)
                _stage_put(stage, p, dov[:, sl], rs.start)
            dl_ref[rs, :] = dblk
            _stage_put(stage, len(groups), dblk, rs.start)
            _stage_put(stage, len(groups) + 1, l_ref[rs, :], rs.start)
            dsc = dmc * silu_c
            cv_rows = cvv[rs]
            dcb_ref[rs, :] = (dsc * cv_rows).astype(BF16)
            dgc_ref[rs, :] = (dmc * bc * (sig_c * (1.0 + gc * (1.0 - sig_c)))).astype(BF16)
            dcv = dsc * cb
            dcv_ref[rs, :] = dcv
            dcw_ref[0:SUBLANES, :] += _rowgroup_sum(dcv * u2[rs])
            dcw_ref[SUBLANES:2 * SUBLANES, :] += _rowgroup_sum(dcv * u1[rs])
            dcw_ref[2 * SUBLANES:3 * SUBLANES, :] += _rowgroup_sum(dcv * u[rs])

        halves = [slice(0, tm // 2), slice(tm // 2, tm)]
        fwd = [forward(rs) for rs in halves]
        dms = [loss_and_dy(rs, f[9], f[10]) for rs, f in zip(halves, fwd)]
        for rs, f, dm in zip(halves, fwd, dms):
            backward(rs, *f[:9], dm)
        for p, sl in enumerate(groups):
            _to_perm(stage, p, dop_ref, sl, BF16)
        _to_perm(stage, len(groups), dlp_ref, all_lanes, F32)
        _to_perm(stage, len(groups) + 1, lp_ref, all_lanes, F32)

    row = lambda n: pl.BlockSpec((tm, n), lambda i: (i, 0))
    whole = lambda a, b: pl.BlockSpec((a, b), lambda i: (0, 0))
    return pl.pallas_call(
        body, name="tail", grid=(nt,),
        out_shape=(jax.ShapeDtypeStruct((seq, attn_w), BF16), jax.ShapeDtypeStruct((seq, LANES), F32),
                   jax.ShapeDtypeStruct(_perm_shape(seq, attn_w), BF16), jax.ShapeDtypeStruct(_perm_shape(seq, LANES), F32),
                   jax.ShapeDtypeStruct(_perm_shape(seq, LANES), F32),
                   jax.ShapeDtypeStruct((seq, attn_w), BF16), jax.ShapeDtypeStruct((seq, conv_w), BF16),
                   jax.ShapeDtypeStruct((seq, conv_w), BF16), jax.ShapeDtypeStruct((seq, conv_w), F32),
                   jax.ShapeDtypeStruct((seq, d_model), F32), jax.ShapeDtypeStruct((mix, d_model), F32),
                   jax.ShapeDtypeStruct((SUBLANES, d_model), F32), jax.ShapeDtypeStruct((CONV_K * SUBLANES, conv_w), F32),
                   jax.ShapeDtypeStruct((SUBLANES, d_model), F32)),
        in_specs=[row(attn_w), row(LANES), row(attn_w), row(4 * conv_w),
                  pl.BlockSpec((SUBLANES, 4 * conv_w), lambda i: (jnp.maximum(i * hb - 1, 0), 0)),
                  row(d_model), row(d_model), _const_spec((mix, d_model)), _const_spec((1, d_model)),
                  _const_spec((SUBLANES, conv_w))],
        out_specs=(row(attn_w), row(LANES), _perm_tile_spec(attn_w, tm), _perm_tile_spec(LANES, tm), _perm_tile_spec(LANES, tm),
                   row(attn_w), row(conv_w), row(conv_w), row(conv_w), row(d_model),
                   whole(mix, d_model), whole(SUBLANES, d_model), whole(CONV_K * SUBLANES, conv_w),
                   whole(SUBLANES, d_model)),
        scratch_shapes=[pltpu.VMEM(_stage_shape(len(groups) + 2, tm), F32)],
        compiler_params=_params(("arbitrary",)),
    )(o, lse, ga, cz, cz, x, tgt, w_out, g2, cw)


def dz_dx(nat_grads, perm_grads, dga, dcb, dgc, dcv, cz, tables, x, g1, e, w_full, cw):
    seq, d_model = x.shape
    attn_w = dga.shape[1]
    conv_w = dcv.shape[1]
    width = w_full.shape[2]
    in_w = 4 * attn_w + 4 * conv_w
    groups = _lane_groups(attn_w)
    tm = DZ_ROW_TILE
    nt = seq // tm
    hb = tm // SUBLANES

    def body(dq_ref, dk_ref, dv_ref, dqp_ref, dkp_ref, dvp_ref, dga_ref, dcb_ref, dgc_ref, dcv_ref, nh_ref, cz_ref,
             cos_ref, s1_ref, s2_ref, x_ref, g_ref, e_ref, w_ref, cw_ref, gx_ref, dz_ref, dg_ref, stage):
        i = pl.program_id(0)

        @pl.when(i == 0)
        def _():
            dg_ref[...] = jnp.zeros_like(dg_ref)

        cos, s1, s2 = cos_ref[...], s1_ref[...], s2_ref[...]
        for t, (nat_ref, perm_ref) in enumerate(((dq_ref, dqp_ref), (dk_ref, dkp_ref), (dv_ref, dvp_ref))):
            for g, sl in enumerate(groups):
                _from_perm(perm_ref, sl, stage, g)
            for g, sl in enumerate(groups):
                tot = nat_ref[:, sl].astype(F32) + _stage_get(stage, g)
                if t < 2:
                    tot = _rope_transposed(tot, cos, s1, s2)
                dz_ref[:, t * attn_w + g * LANES:t * attn_w + (g + 1) * LANES] = tot.astype(BF16)
        dz_ref[:, 3 * attn_w:4 * attn_w] = dga_ref[...]
        dcv = dcv_ref[...]
        nh = jnp.where(i < nt - 1, nh_ref[...], 0.0)
        w0, w1, w2 = cw_ref[0:1, :], cw_ref[1:2, :], cw_ref[2:3, :]
        du = dcv * w2 + _shift_up(dcv, nh, 1) * w1 + _shift_up(dcv, nh, 2) * w0
        base = 4 * attn_w
        dz_ref[:, base:base + conv_w] = (du * cz_ref[:, 2 * conv_w:3 * conv_w]).astype(BF16)
        dz_ref[:, base + conv_w:base + 2 * conv_w] = dcb_ref[...]
        dz_ref[:, base + 2 * conv_w:base + 3 * conv_w] = (du * cz_ref[:, 0:conv_w]).astype(BF16)
        dz_ref[:, base + 3 * conv_w:base + 4 * conv_w] = dgc_ref[...]

        dh = _nt(dz_ref[:, 0:width], w_ref[0])
        for j in range(1, N_CHIPS):
            dh = dh + _nt(dz_ref[:, j * width:(j + 1) * width], w_ref[j])
        xv = x_ref[...]
        r1 = lax.rsqrt(jnp.mean(xv * xv, axis=-1, keepdims=True) + NORM_EPS)
        xhat = xv * r1
        dg_ref[...] += _rowgroup_sum(dh * xhat)
        dhg = dh * g_ref[...]
        gx_ref[...] = r1 * (dhg - xhat * jnp.mean(dhg * xhat, axis=-1, keepdims=True)) + e_ref[...]

    row = lambda n: pl.BlockSpec((tm, n), lambda i: (i, 0))
    whole = lambda a, b: pl.BlockSpec((a, b), lambda i: (0, 0))
    pt = _perm_tile_spec(attn_w, tm)
    return pl.pallas_call(
        body, name="dz_dx", grid=(nt,),
        out_shape=(jax.ShapeDtypeStruct((seq, d_model), F32), jax.ShapeDtypeStruct((seq, in_w), BF16),
                   jax.ShapeDtypeStruct((SUBLANES, d_model), F32)),
        in_specs=[row(attn_w), row(attn_w), row(attn_w), pt, pt, pt, row(attn_w), row(conv_w), row(conv_w), row(conv_w),
                  pl.BlockSpec((SUBLANES, conv_w), lambda i: (jnp.minimum((i + 1) * hb, seq // SUBLANES - 1), 0)),
                  row(4 * conv_w), row(LANES), row(LANES), row(LANES), row(d_model), _const_spec((1, d_model)), row(d_model),
                  _const_spec(w_full.shape), _const_spec((SUBLANES, conv_w))],
        out_specs=(row(d_model), row(in_w), whole(SUBLANES, d_model)),
        scratch_shapes=[pltpu.VMEM(_stage_shape(len(groups), tm), F32)],
        compiler_params=_params(("arbitrary",)),
    )(*nat_grads, *perm_grads, dga, dcb, dgc, dcv, dcv, cz, *tables, x, g1, e, w_full, cw)


def dw_in_reduce(ht, dz):
    d_model, seq = ht.shape
    half = dz.shape[1] // N_DEV
    ts = min(2048, seq)
    steps = seq // ts
    x, y, c = lax.axis_index("x"), lax.axis_index("y"), lax.axis_index("c")
    far_first = lambda x, y: [(1 - x, 1 - y), (1 - x, y), (x, 1 - y)]
    chips = jnp.stack([2 * px + py for px, py in far_first(x, y)] + [2 * x + y]).astype(jnp.int32)
    order = jnp.stack([2 * chips + (1 - c), 2 * chips + c], axis=1).reshape(N_DEV)

    def body(order_ref, ht_ref, dz_ref, out_ref, acc, theirs, staged, contrib, resbuf, out_sem, sa, ra, sb, rb, sc, rc):
        del order_ref
        p, s = pl.program_id(0), pl.program_id(1)
        x, y, c = lax.axis_index("x"), lax.axis_index("y"), lax.axis_index("c")
        sib = (x, y, 1 - c)
        peers = far_first(x, y)
        slot = p % 2

        def a_copy(k):
            return pltpu.make_async_remote_copy(src_ref=acc.at[0], dst_ref=theirs.at[k], send_sem=sa.at[k], recv_sem=ra.at[k],
                                                device_id=sib, device_id_type=MESH)

        def b_copy(k):
            px, py = peers[k]
            return pltpu.make_async_remote_copy(src_ref=staged.at[k], dst_ref=contrib.at[k], send_sem=sb.at[k], recv_sem=rb.at[k],
                                                device_id=(px, py, c), device_id_type=MESH)

        def c_copy(which):
            return pltpu.make_async_remote_copy(src_ref=resbuf.at[which], dst_ref=resbuf.at[which], send_sem=sc, recv_sem=rc,
                                                device_id=sib, device_id_type=MESH)

        @pl.when(s == 0)
        def _():
            for k in range(N_CHIPS - 1):
                @pl.when(p == 2 * k + 2)
                def _():
                    a_copy(k).wait_send()
            acc[slot] = jnp.zeros((d_model, half), F32)

        acc[slot] += jnp.dot(ht_ref[...], dz_ref[...], preferred_element_type=F32)

        @pl.when(s == steps - 1)
        def _():
            for k in range(N_CHIPS):
                @pl.when(p == 2 * k)
                def _():
                    a_copy(k).start()
            for k in range(N_CHIPS - 1):
                @pl.when(p == 2 * k + 1)
                def _():
                    a_copy(k).wait_recv()
                    staged[k] = (acc[1] + theirs[k]).astype(BF16)
                    b_copy(k).start()

            @pl.when(p == N_DEV - 1)
            def _():
                a_copy(N_CHIPS - 1).wait_recv()
                tot = acc[1] + theirs[N_CHIPS - 1]
                for k in range(N_CHIPS - 1):
                    b_copy(k).wait_recv()
                    tot = tot + contrib[k].astype(F32)
                resbuf[c] = tot
                c_copy(c).start()
                c_copy(1 - c).wait_recv()
                done = pltpu.make_async_copy(resbuf, out_ref, out_sem)
                done.start()
                a_copy(N_CHIPS - 1).wait_send()
                for k in range(N_CHIPS - 1):
                    b_copy(k).wait_send()
                c_copy(c).wait_send()
                done.wait()

    dma = pltpu.SemaphoreType.DMA
    grid_spec = pltpu.PrefetchScalarGridSpec(
        num_scalar_prefetch=1, grid=(N_DEV, steps),
        in_specs=[pl.BlockSpec((d_model, ts), lambda p, s, order_ref: (0, s)),
                  pl.BlockSpec((ts, half), lambda p, s, order_ref: (s, order_ref[p]))],
        out_specs=pl.BlockSpec(memory_space=pl.ANY),
        scratch_shapes=[pltpu.VMEM((2, d_model, half), F32), pltpu.VMEM((N_CHIPS, d_model, half), F32),
                        pltpu.VMEM((N_CHIPS - 1, d_model, half), BF16), pltpu.VMEM((N_CHIPS - 1, d_model, half), BF16),
                        pltpu.VMEM((2, d_model, half), F32), dma,
                        dma((N_CHIPS,)), dma((N_CHIPS,)), dma((N_CHIPS - 1,)), dma((N_CHIPS - 1,)), dma, dma])
    return pl.pallas_call(
        body, name="dw_in_reduce", grid_spec=grid_spec,
        out_shape=jax.ShapeDtypeStruct((2, d_model, half), F32),
        compiler_params=_params(("arbitrary", "arbitrary")),
    )(order, ht, dz)


def grad_reduce(tensors, small):
    nt = len(tensors)
    split = [g.reshape(N_CHIPS, 2, *g.shape[1:]) for g in tensors]
    shapes = [g.shape[2:] for g in split]

    def body(*refs):
        srcs, sm_ref = refs[:nt], refs[nt]
        res, rs_ref = refs[nt + 1:2 * nt + 1], refs[2 * nt + 1]
        scratch = refs[2 * nt + 2:]
        mine, theirs, staged, contrib = (scratch[k * nt:(k + 1) * nt] for k in range(4))
        sbuf, loc_sems, sa, ra, sb, rb, sc, rc, ss, rs = scratch[4 * nt:]
        x, y, c = lax.axis_index("x"), lax.axis_index("y"), lax.axis_index("c")
        me = 2 * x + y
        sib = (x, y, 1 - c)

        flips = [(fx, fy, fc) for fx in (0, 1) for fy in (0, 1) for fc in (0, 1)][1:]
        my8 = 4 * x + 2 * y + c
        sbuf[my8] = sm_ref[...]

        def small_copy(k, slot, to):
            return pltpu.make_async_remote_copy(src_ref=sm_ref, dst_ref=sbuf.at[slot], send_sem=ss.at[k], recv_sem=rs.at[k],
                                                device_id=to, device_id_type=MESH)

        sends = []
        for k, (fx, fy, fc) in enumerate(flips):
            px, py, pc = _flip(x, fx), _flip(y, fy), _flip(c, fc)
            sends.append(small_copy(k, my8, (px, py, pc)))
            sends[-1].start()

        def a_copy(t, j):
            return pltpu.make_async_remote_copy(src_ref=srcs[t].at[j, 1 - c], dst_ref=theirs[t].at[j], send_sem=sa.at[t, j],
                                                recv_sem=ra.at[t, j], device_id=sib, device_id_type=MESH)

        peers = _chip_peers(x, y)
        order = [2 * px + py for px, py in peers] + [me]
        loads = [[pltpu.make_async_copy(srcs[t].at[j, c], mine[t].at[j], loc_sems.at[t, j]) for t in range(nt)] for j in order]
        for pos, j in enumerate(order):
            for t in range(nt):
                loads[pos][t].start()
                sends.append(a_copy(t, j))
                sends[-1].start()

        def b_copy(k, t, piece, slot, to):
            return pltpu.make_async_remote_copy(src_ref=staged[t].at[piece], dst_ref=contrib[t].at[slot], send_sem=sb.at[k, t],
                                                recv_sem=rb.at[k, t], device_id=to, device_id_type=MESH)

        for k, (px, py) in enumerate(peers):
            j = 2 * px + py
            for t in range(nt):
                loads[k][t].wait()
                a_copy(t, j).wait_recv()
                staged[t][j] = (mine[t][j] + theirs[t][j]).astype(BF16)
                sends.append(b_copy(k, t, j, me, (px, py, c)))
                sends[-1].start()
        for t in range(nt):
            loads[len(peers)][t].wait()
            a_copy(t, me).wait_recv()
            mine[t][me] = mine[t][me] + theirs[t][me]
            contrib[t][me] = mine[t][me].astype(BF16)
        for k, (px, py) in enumerate(peers):
            for t in range(nt):
                b_copy(k, t, me, 2 * px + py, (px, py, c)).wait_recv()

        def c_copy(t, half):
            return pltpu.make_async_remote_copy(src_ref=res[t].at[half], dst_ref=res[t].at[half], send_sem=sc.at[t],
                                                recv_sem=rc.at[t], device_id=sib, device_id_type=MESH)

        for t in range(nt):
            own = mine[t][me]
            term = lambda j: jnp.where(me == j, own, contrib[t][j].astype(F32))
            res[t][c] = ((term(0) + term(1)) + term(2)) + term(3)
            sends.append(c_copy(t, c))
            sends[-1].start()
        for t in range(nt):
            c_copy(t, 1 - c).wait_recv()

        for k, (fx, fy, fc) in enumerate(flips):
            px, py, pc = _flip(x, fx), _flip(y, fy), _flip(c, fc)
            small_copy(k, 4 * px + 2 * py + pc, (px, py, pc)).wait_recv()
        tot = sbuf[0]
        for d in range(1, N_DEV):
            tot = tot + sbuf[d]
        rs_ref[...] = tot
        for cp in sends:
            cp.wait_send()

    vm = pl.BlockSpec(memory_space=pltpu.VMEM)
    anyspace = pl.BlockSpec(memory_space=pl.ANY)
    dma = pltpu.SemaphoreType.DMA
    bufs = [pltpu.VMEM((N_CHIPS, *shp), dt) for dt in (F32, F32, BF16, BF16) for shp in shapes]
    outs = pl.pallas_call(
        body, name="grad_reduce",
        out_shape=(*[jax.ShapeDtypeStruct((2, *shp), F32) for shp in shapes], jax.ShapeDtypeStruct(small.shape, F32)),
        in_specs=[anyspace] * nt + [vm], out_specs=tuple([vm] * (nt + 1)),
        scratch_shapes=[*bufs, pltpu.VMEM((N_DEV, *small.shape), F32),
                        dma((nt, N_CHIPS)), dma((nt, N_CHIPS)), dma((nt, N_CHIPS)), dma((3, nt)), dma((3, nt)), dma((nt,)), dma((nt,)),
                        dma((N_DEV - 1,)), dma((N_DEV - 1,))],
        compiler_params=_params(),
    )(*split, small)
    return outs[:nt], outs[nt]


def _adam_math(w, g, m, v):
    m = ADAM_B1 * m + (1.0 - ADAM_B1) * g
    v = ADAM_B2 * v + (1.0 - ADAM_B2) * (g * g)
    m_hat = m / (1.0 - ADAM_B1 ** ADAM_STEP)
    v_hat = v / (1.0 - ADAM_B2 ** ADAM_STEP)
    delta = -ADAM_LR * (m_hat / (jnp.sqrt(v_hat) + ADAM_EPS) + ADAM_WD * w)
    return delta, m, v


def adam_shard(name, w, g2, m, v, block, grid, w_map, g_map):
    def body(w_ref, g_ref, m_ref, v_ref, go_ref, d_ref, mo_ref, vo_ref):
        g = g_ref[0]
        delta, mn, vn = _adam_math(w_ref[...], g, m_ref[...], v_ref[...])
        go_ref[...] = g
        d_ref[...] = delta
        mo_ref[...] = mn
        vo_ref[...] = vn

    ws = pl.BlockSpec(block, w_map)
    shp = jax.ShapeDtypeStruct(w.shape, F32)
    return pl.pallas_call(
        body, name=name, grid=grid, out_shape=(shp, shp, shp, shp),
        in_specs=[ws, pl.BlockSpec((1, *block), g_map), ws, ws], out_specs=(ws, ws, ws, ws),
        compiler_params=_params(("arbitrary",) * len(grid)),
    )(w, g2, m, v)


def adam_small(ws, gs, ms, vs):
    n = len(ws)

    def body(*refs):
        ins, outs = refs[:4 * n], refs[4 * n:]
        for t in range(n):
            delta, mn, vn = _adam_math(ins[t][...], ins[n + t][...], ins[2 * n + t][...], ins[3 * n + t][...])
            outs[3 * t][...] = delta
            outs[3 * t + 1][...] = mn
            outs[3 * t + 2][...] = vn

    vm = pl.BlockSpec(memory_space=pltpu.VMEM)
    outs = pl.pallas_call(
        body, name="adam_small",
        out_shape=tuple(jax.ShapeDtypeStruct(w.shape, F32) for w in ws for _ in range(3)),
        in_specs=[vm] * (4 * n), out_specs=tuple([vm] * (3 * n)),
        compiler_params=_params(),
    )(*ws, *gs, *ms, *vs)
    return [outs[3 * t:3 * t + 3] for t in range(n)]


def kernel(x, norm_pre_g, w_in, conv_w, w_out, norm_post_g, loss_target, m_norm_pre_g, m_w_in, m_conv_w, m_w_out, m_norm_post_g, v_norm_pre_g, v_w_in, v_conv_w, v_w_out, v_norm_post_g):
    _, seq, d_model = x.shape
    width = w_in.shape[1]
    conv_q = conv_w.shape[1]
    conv_width = N_CHIPS * conv_q
    attn_width = d_model - conv_width
    xs, tg = x[0], loss_target[0]
    g1, g2 = norm_pre_g.reshape(1, d_model), norm_post_g.reshape(1, d_model)

    w_full, wout_full, cw_full = gather_weights(w_in, w_out, conv_w)
    wout2 = wout_full.reshape(attn_width + conv_width, d_model)
    cw = jnp.zeros((SUBLANES, conv_width), F32).at[:CONV_K].set(
        cw_full[:, :CONV_K, :conv_q].transpose(1, 0, 2).reshape(CONV_K, conv_width))
    tables = _rope_tables(seq)

    ht, q, k, v, qp, kp, vp, ga, cz = inproj(xs, g1, w_full, tables, attn_width, conv_width)
    run = attn_fwd("p4", qp, kp, vp, None)
    run = attn_fwd("p16", qp, kp, vp, run)
    o, lse = attn_fwd("nat", q, k, v, run)
    (d_o, delta, d_op, delta_p, lse_p, dga, dcb, dgc, dcv, e, dwout, dg2, dcw, loss_acc) = tail(
        o, lse, ga, cz, xs, tg, wout2, g2, cw)
    nat_grads = attn_bwd("nat", q, k, v, d_o, lse, delta, None)
    perm_grads = attn_bwd("p4", qp, kp, vp, d_op, lse_p, delta_p, None)
    perm_grads = attn_bwd("p16", qp, kp, vp, d_op, lse_p, delta_p, perm_grads)
    grad_x, dz, dg1 = dz_dx(nat_grads, perm_grads, dga, dcb, dgc, dcv, cz, tables, xs, g1, e, w_full, cw)

    small = jnp.zeros((SUBLANES, d_model), F32)
    small = small.at[0].set(dg1.sum(axis=0)).at[1].set(dg2.sum(axis=0))
    small = small.at[2:2 + CONV_K, :conv_width].set(dcw.reshape(CONV_K, SUBLANES, conv_width).sum(axis=1))
    small = small.at[2 + CONV_K, 0].set(jnp.sum(loss_acc))
    (rout,), rsmall = grad_reduce([dwout.reshape(N_DEV, -1, d_model)], small)
    rin = dw_in_reduce(ht, dz)

    half = width // 2
    tr = 256
    gw_in, d_in, m_in, v_in = adam_shard(
        "adam_w_in", w_in, rin, m_w_in, v_w_in, (tr, half), (2, d_model // tr),
        lambda hf, i: (i, hf), lambda hf, i: (hf, i, 0))
    rq = w_out.shape[0] // 2
    gw_out, d_out, m_out, v_out = adam_shard(
        "adam_w_out", w_out, rout, m_w_out, v_w_out, (rq, d_model), (2,),
        lambda hf: (hf, 0), lambda hf: (hf, 0, 0))

    chip = 2 * lax.axis_index("x") + lax.axis_index("y")
    g_pre, g_post = rsmall[0:1], rsmall[1:2]
    g_conv = lax.dynamic_slice(rsmall[2:2 + CONV_K, :conv_width], (0, chip * conv_q), (CONV_K, conv_q))
    (d_pre, m_pre, v_pre), (d_post, m_post, v_post), (d_cv, m_cv, v_cv) = adam_small(
        [g1, g2, conv_w], [g_pre, g_post, g_conv],
        [m_norm_pre_g.reshape(1, d_model), m_norm_post_g.reshape(1, d_model), m_conv_w],
        [v_norm_pre_g.reshape(1, d_model), v_norm_post_g.reshape(1, d_model), v_conv_w])

    loss = 0.5 * rsmall[2 + CONV_K, 0] / d_model
    vec = lambda a: a.reshape(d_model)
    return (loss, grad_x.reshape(1, seq, d_model),
            vec(g_pre), gw_in, g_conv, gw_out, vec(g_post),
            vec(d_pre), d_in, d_cv, d_out, vec(d_post),
            vec(m_pre), m_in, m_cv, m_out, vec(m_post),
            vec(v_pre), v_in, v_cv, v_out, vec(v_post))
```

```python
import jax
import jax.numpy as jnp
from jax import lax
from jax.experimental import pallas as pl
from jax.experimental.pallas import tpu as pltpu

HEAD_DIM = 64
LANES = 128
SUBLANES = 8
BLOCK = 128
WINDOW_KEYS = 128
PERM = 16
PJ = 4
P4_ROWS = BLOCK // PJ
ROW_TILE = 512
DZ_ROW_TILE = 512
CONV_K = 3
ROPE_THETA = 10000.0
NORM_EPS = 1e-6
ATTN_SCALE = HEAD_DIM ** -0.5
NEG = -1e30
N_CHIPS = 4
N_DEV = 8
MESH = pl.DeviceIdType.MESH
ADAM_LR = 0.001
ADAM_B1 = 0.9
ADAM_B2 = 0.999
ADAM_EPS = 1e-08
ADAM_WD = 0.01
ADAM_STEP = 10
VMEM_LIMIT = 52 * 1024 * 1024

F32 = jnp.float32
BF16 = jnp.bfloat16


def _params(sem=None, **kw):
    return pltpu.CompilerParams(dimension_semantics=sem, vmem_limit_bytes=VMEM_LIMIT, **kw)


def _const_spec(shape):
    return pl.BlockSpec(shape, lambda *_: (0,) * len(shape), pipeline_mode=pl.Buffered(1))


def _sigmoid(z):
    return 1.0 / (1.0 + jnp.exp(-z))


def _rowgroup_sum(a):
    rows, n = a.shape
    return a.reshape(rows // SUBLANES, SUBLANES, n).sum(axis=0)


def _nt(a, b):
    return lax.dot_general(a, b, (((1,), (1,)), ((), ())), preferred_element_type=F32)


def _tn(a, b):
    return lax.dot_general(a, b, (((0,), (0,)), ((), ())), preferred_element_type=F32)


def _col_pieces(a, b, width):
    out = []
    while a < b:
        j = a // width
        e = min(b, (j + 1) * width)
        out.append((j, a - j * width, e - j * width))
        a = e
    return out


def _lane_groups(width):
    return [slice(g * LANES, (g + 1) * LANES) for g in range(width // LANES)]


def _perm_shape(seq, width):
    return (PJ, PJ, seq // PERM, width)


def _perm_tile_spec(width, tm):
    return pl.BlockSpec((PJ, PJ, tm // PERM, width), lambda i: (0, 0, i, 0))


STAGE_PITCH = 24


def _stage_shape(groups, rows):
    return (groups, rows // PERM * STAGE_PITCH, LANES)


def _stage_put(stage, g, val, row0=0):
    for a in range(val.shape[0] // PERM):
        at = (row0 // PERM + a) * STAGE_PITCH
        stage[g, at:at + PERM, :] = val[a * PERM:(a + 1) * PERM]


def _stage_get(stage, g):
    return jnp.concatenate([stage[g, a * STAGE_PITCH:a * STAGE_PITCH + PERM, :]
                            for a in range(stage.shape[1] // STAGE_PITCH)], axis=0)


def _to_perm(stage, g, dst_ref, sl, dtype):
    rows = stage.shape[1] // STAGE_PITCH
    for b in range(PERM):
        dst_ref[b // PJ, b % PJ, :, sl] = stage[g, pl.ds(b, rows, stride=STAGE_PITCH), :].astype(dtype)


def _from_perm(src_ref, sl, stage, g):
    rows = stage.shape[1] // STAGE_PITCH
    for b in range(PERM):
        stage[g, pl.ds(b, rows, stride=STAGE_PITCH), :] = src_ref[b // PJ, b % PJ, :, sl].astype(F32)


def _flip(a, f):
    return 1 - a if f else a


def _chip_peers(x, y):
    return [(1 - x, y), (x, 1 - y), (1 - x, 1 - y)]


def gather_weights(w_in, w_out, conv_w, seq):
    d_model, width = w_in.shape
    rows = w_out.shape[0]
    cw = jnp.zeros((SUBLANES, LANES), F32).at[:CONV_K, :conv_w.shape[1]].set(conv_w)
    half_dim = HEAD_DIM // 2
    inv_freq = ROPE_THETA ** (-jnp.arange(half_dim, dtype=F32) * 2.0 / HEAD_DIM)
    inv_freq = jnp.tile(inv_freq, LANES // half_dim).reshape(1, LANES)
    chunk = min(ROW_TILE, seq)

    def body(win_ref, wout_ref, cw_ref, freq_ref, winf_ref, woutf_ref, cwf_ref, cos_ref, s1_ref, s2_ref,
             st_in, st_out, ici_send, ici_recv, d2d_send, d2d_recv):
        x, y, c = lax.axis_index("x"), lax.axis_index("y"), lax.axis_index("c")
        me = 2 * x + y
        sib = (x, y, 1 - c)
        st_in[...] = win_ref[...].astype(BF16)
        st_out[...] = wout_ref[...].astype(BF16)
        winf_ref[me] = st_in[...]
        woutf_ref[me] = st_out[...]
        cwf_ref[me] = cw_ref[...]
        stages = (st_in, st_out)
        fulls = (winf_ref, woutf_ref)
        halves = (d_model // 2, rows // 2)

        def half(t, core):
            return pl.ds(pl.multiple_of(core * halves[t], halves[t]), halves[t])

        def ici(k, t, slot, to, core):
            src = stages[t].at[half(t, core)] if t < 2 else cw_ref
            dst = fulls[t].at[slot, half(t, core)] if t < 2 else cwf_ref.at[slot]
            return pltpu.make_async_remote_copy(src_ref=src, dst_ref=dst, send_sem=ici_send.at[k, t], recv_sem=ici_recv.at[k, t],
                                                device_id=to, device_id_type=MESH)

        def d2d(k, t, slot, core):
            ref = fulls[t].at[slot, half(t, core)]
            return pltpu.make_async_remote_copy(src_ref=ref, dst_ref=ref, send_sem=d2d_send.at[k, t], recv_sem=d2d_recv.at[k, t],
                                                device_id=sib, device_id_type=MESH)

        peers = _chip_peers(x, y)
        sends = [ici(k, t, me, (px, py, c), c) for k, (px, py) in enumerate(peers) for t in range(3)]
        for cp in sends:
            cp.start()

        first_half = lax.broadcasted_iota(jnp.int32, (chunk, LANES), 1) % HEAD_DIM < half_dim
        row = lax.broadcasted_iota(jnp.int32, (chunk, LANES), 0)

        def table_rows(i, carry):
            at = pl.multiple_of(i * chunk, chunk)
            ang = (row + at).astype(F32) * freq_ref[...]
            sin = jnp.sin(ang)
            cos_ref[pl.ds(at, chunk), :] = jnp.cos(ang)
            s1_ref[pl.ds(at, chunk), :] = jnp.where(first_half, -sin, 0.0)
            s2_ref[pl.ds(at, chunk), :] = jnp.where(first_half, 0.0, sin)
            return carry

        lax.fori_loop(0, seq // chunk, table_rows, 0)

        for k, (px, py) in enumerate(peers):
            for t in range(2):
                ici(k, t, 2 * px + py, (px, py, c), c).wait_recv()
                fwd = d2d(k, t, 2 * px + py, c)
                fwd.start()
                sends.append(fwd)
            ici(k, 2, 2 * px + py, (px, py, c), c).wait_recv()
        for k, (px, py) in enumerate(peers):
            for t in range(2):
                d2d(k, t, 2 * px + py, 1 - c).wait_recv()
        for cp in sends:
            cp.wait_send()

    vm = pl.BlockSpec(memory_space=pltpu.VMEM)
    dma = pltpu.SemaphoreType.DMA
    return pl.pallas_call(
        body, name="gather_weights",
        out_shape=(jax.ShapeDtypeStruct((N_CHIPS, d_model, width), BF16),
                   jax.ShapeDtypeStruct((N_CHIPS, rows, d_model), BF16),
                   jax.ShapeDtypeStruct((N_CHIPS, SUBLANES, LANES), F32),
                   *[jax.ShapeDtypeStruct((seq, LANES), F32)] * 3),
        in_specs=[vm, vm, vm, vm], out_specs=(vm,) * 6,
        scratch_shapes=[pltpu.VMEM((d_model, width), BF16), pltpu.VMEM((rows, d_model), BF16),
                        dma((3, 3)), dma((3, 3)), dma((3, 2)), dma((3, 2))],
        compiler_params=_params(),
    )(w_in, w_out, cw, inv_freq)


def _rope(t, cos, s1, s2):
    return t * cos + pltpu.roll(t, LANES - HEAD_DIM // 2, 1) * s1 + pltpu.roll(t, HEAD_DIM // 2, 1) * s2


def _rope_transposed(g, cos, s1, s2):
    return g * cos + pltpu.roll(g * s1, HEAD_DIM // 2, 1) + pltpu.roll(g * s2, LANES - HEAD_DIM // 2, 1)


def inproj(x, g1, w_full, tables, attn_w, conv_w):
    seq, d_model = x.shape
    width = w_full.shape[2]
    tm = ROW_TILE
    groups = _lane_groups(attn_w)

    def body(x_ref, g_ref, w_ref, cos_ref, s1_ref, s2_ref,
             ht_ref, q_ref, k_ref, v_ref, qp_ref, kp_ref, vp_ref, ga_ref, cz_ref, stage):
        xv = x_ref[...]
        hb = ((xv * lax.rsqrt(jnp.mean(xv * xv, axis=-1, keepdims=True) + NORM_EPS)) * g_ref[...]).astype(BF16)
        ht_ref[...] = jnp.transpose(hb)
        cos, s1, s2 = cos_ref[...], s1_ref[...], s2_ref[...]

        def proj(a, b):
            parts = [jnp.dot(hb, w_ref[j, :, lo:hi], preferred_element_type=F32) for j, lo, hi in _col_pieces(a, b, width)]
            return parts[0] if len(parts) == 1 else jnp.concatenate(parts, axis=1)

        def emit(z, nat_ref, perm_ref, fn):
            for g, sl in enumerate(groups):
                val = fn(z[:, sl])
                nat_ref[:, sl] = val.astype(BF16)
                _stage_put(stage, g, val)
            for g, sl in enumerate(groups):
                _to_perm(stage, g, perm_ref, sl, BF16)

        emit(proj(0, attn_w), q_ref, qp_ref, lambda t: _rope(t, cos, s1, s2) * ATTN_SCALE)
        emit(proj(attn_w, 2 * attn_w), k_ref, kp_ref, lambda t: _rope(t, cos, s1, s2))
        emit(proj(2 * attn_w, 3 * attn_w), v_ref, vp_ref, lambda t: t)
        ga_ref[...] = proj(3 * attn_w, 4 * attn_w)
        cz_ref[...] = proj(4 * attn_w, 4 * attn_w + 4 * conv_w)

    row = lambda n: pl.BlockSpec((tm, n), lambda i: (i, 0))
    nat = jax.ShapeDtypeStruct((seq, attn_w), BF16)
    perm = jax.ShapeDtypeStruct(_perm_shape(seq, attn_w), BF16)
    return pl.pallas_call(
        body, name="inproj", grid=(seq // tm,),
        out_shape=(jax.ShapeDtypeStruct((d_model, seq), BF16), nat, nat, nat, perm, perm, perm,
                   jax.ShapeDtypeStruct((seq, attn_w), F32), jax.ShapeDtypeStruct((seq, 4 * conv_w), F32)),
        in_specs=[row(d_model), _const_spec((1, d_model)), _const_spec(w_full.shape), row(LANES), row(LANES), row(LANES)],
        out_specs=(pl.BlockSpec((d_model, tm), lambda i: (0, i)), row(attn_w), row(attn_w), row(attn_w),
                   _perm_tile_spec(attn_w, tm), _perm_tile_spec(attn_w, tm), _perm_tile_spec(attn_w, tm),
                   row(attn_w), row(4 * conv_w)),
        scratch_shapes=[pltpu.VMEM(_stage_shape(len(groups), tm), F32)],
        compiler_params=_params(("arbitrary",)),
    )(x, g1, w_full, *tables)


class _Mode:
    def __init__(self, name, seq):
        self.name = name
        if name == "nat":
            self.residues, self.nb = 1, seq // BLOCK
        elif name == "p16":
            self.residues, self.nb = PERM, seq // PERM // BLOCK
        else:
            self.residues, self.nb = PJ, seq // PERM // P4_ROWS

    def spec(self, width, which, last=None):
        if which == "prev":
            blk = lambda n: jnp.maximum(n - 1, 0)
        elif last is None:
            blk = lambda n: n
        else:
            blk = lambda n: jnp.minimum(n, last)
        if self.name == "nat":
            return pl.BlockSpec((BLOCK, width), lambda r, n: (blk(n), 0))
        if self.name == "p16":
            return pl.BlockSpec((1, 1, BLOCK, width), lambda r, n: (r // PJ, r % PJ, blk(n), 0))
        return pl.BlockSpec((PJ, 1, P4_ROWS, width), lambda r, n: (0, r, blk(n), 0))

    def get(self, ref, sl):
        if self.name == "nat":
            return ref[:, sl]
        if self.name == "p16":
            return ref[0, 0, :, sl]
        return jnp.concatenate([ref[j, 0, :, sl] for j in range(PJ)], axis=0)

    def put(self, ref, sl, val):
        val = val.astype(ref.dtype)
        if self.name == "nat":
            ref[:, sl] = val
        elif self.name == "p16":
            ref[0, 0, :, sl] = val
        else:
            for j in range(PJ):
                ref[j, 0, :, sl] = val[j * P4_ROWS:(j + 1) * P4_ROWS]

    def index(self, idx, is_key):
        if self.name != "p4":
            return idx - BLOCK if is_key else idx
        within = jnp.bitwise_and(idx, BLOCK - 1)
        m = PJ * jnp.bitwise_and(within, P4_ROWS - 1) + jnp.right_shift(within, P4_ROWS.bit_length() - 1)
        return m + BLOCK * (jnp.right_shift(idx, BLOCK.bit_length() - 1) - 1) if is_key else m

    def bias(self, n, keys_major):
        shape = (2 * BLOCK, BLOCK) if keys_major else (BLOCK, 2 * BLOCK)
        kdim = 0 if keys_major else 1
        kidx = lax.broadcasted_iota(jnp.int32, shape, kdim)
        qidx = lax.broadcasted_iota(jnp.int32, shape, 1 - kdim)
        rel = self.index(qidx, False) - self.index(kidx, True)
        valid = (rel >= 0) & (rel <= WINDOW_KEYS) & ((kidx >= BLOCK) | (n > 0))
        return jnp.where(valid, 0.0, NEG)


def _head_masks():
    lane = lax.broadcasted_iota(jnp.int32, (BLOCK, LANES), 1)
    lo = lane < HEAD_DIM
    return lane, lo, jnp.where(lo, 1.0, 0.0).astype(BF16), jnp.where(lo, 0.0, 1.0).astype(BF16)


def _column(blk, lane, h):
    return jnp.sum(jnp.where(lane == h, blk, 0.0), axis=1, keepdims=True)


def attn_fwd(name, q, k, v, run):
    nat = name == "nat"
    seq = q.shape[0] if nat else q.shape[2] * PERM
    attn_w = q.shape[-1]
    mode = _Mode(name, seq)
    groups = _lane_groups(attn_w)
    first = run is None
    all_lanes = slice(0, LANES)

    def body(*refs):
        q_ref, kp_ref, kc_ref, vp_ref, vc_ref = refs[:5]
        if first:
            o_ref, l_ref = refs[5:]
        elif nat:
            oin_ref, lin_ref, ex_ref, o_ref, l_ref, ostage, lstage = refs[5:]
        else:
            oin_ref, lin_ref, ex_ref, o_ref, l_ref = refs[5:]
        n = pl.program_id(1)
        bias = mode.bias(n, True)
        bias2 = jnp.concatenate([bias, bias], axis=1)
        _, lo, m_lo, m_hi = _head_masks()
        head_row = lax.broadcasted_iota(jnp.int32, (BLOCK, LANES), 0)
        lrows = jnp.zeros((BLOCK, LANES), F32)
        def probs(sl):
            q2 = mode.get(q_ref, sl)
            kcat = jnp.concatenate([mode.get(kp_ref, sl), mode.get(kc_ref, sl)], axis=0)
            vcat = jnp.concatenate([mode.get(vp_ref, sl), mode.get(vc_ref, sl)], axis=0)
            qq = jnp.concatenate([q2 * m_lo, q2 * m_hi], axis=0)
            s_t = _nt(kcat, qq) + bias2
            m = jnp.max(s_t, axis=0, keepdims=True)
            pe = jnp.exp(s_t - m)
            l = jnp.sum(pe, axis=0, keepdims=True)
            return vcat, (pe * (1.0 / l)).astype(BF16), m + jnp.log(l)

        def output(p, sl, vcat, pn, lse, lrows):
            o_new = _tn(pn, vcat)
            mode.put(o_ref, sl, jnp.where(lo, o_new[:BLOCK], o_new[BLOCK:]))
            lrows = jnp.where(head_row == 2 * p, lse[:, :BLOCK], lrows)
            return jnp.where(head_row == 2 * p + 1, lse[:, BLOCK:], lrows)

        pending = None
        for p, sl in enumerate(groups):
            nxt = probs(sl)
            if pending is not None:
                lrows = output(*pending, lrows)
            pending = (p, sl, *nxt)
        lrows = output(*pending, lrows)
        lblk = jnp.transpose(lrows)
        if first:
            mode.put(l_ref, all_lanes, lblk)
        else:
            if nat:
                for g, sl in enumerate(groups):
                    _from_perm(oin_ref, sl, ostage, g)
                _from_perm(lin_ref, all_lanes, lstage, 0)
                lin = _stage_get(lstage, 0)
            else:
                lin = mode.get(lin_ref, all_lanes)
            mx = jnp.maximum(lin, lblk)
            new = mx + jnp.log(jnp.exp(lin - mx) + jnp.exp(lblk - mx))
            mode.put(l_ref, all_lanes, new)

            def expand(w):
                hi = w.astype(BF16)
                rest = (w - hi.astype(F32)).astype(BF16)
                ex = ex_ref[...]
                return jnp.dot(hi, ex, preferred_element_type=F32) + jnp.dot(rest, ex, preferred_element_type=F32)

            w_prev, w_cur = expand(jnp.exp(lin - new)), expand(jnp.exp(lblk - new))
            for p, sl in enumerate(groups):
                o_prev = _stage_get(ostage, p) if nat else mode.get(oin_ref, sl)
                mode.put(o_ref, sl, w_prev[:, sl] * o_prev + w_cur[:, sl] * mode.get(o_ref, sl))

    ins = [q, k, k, v, v]
    specs = [mode.spec(attn_w, "cur"), mode.spec(attn_w, "prev"), mode.spec(attn_w, "cur"),
             mode.spec(attn_w, "prev"), mode.spec(attn_w, "cur")]
    scratch = []
    if not first:
        ins += list(run)
        if nat:
            rows8 = BLOCK // PERM
            specs += [pl.BlockSpec((PJ, PJ, rows8, attn_w), lambda r, n: (0, 0, n, 0)),
                      pl.BlockSpec((PJ, PJ, rows8, LANES), lambda r, n: (0, 0, n, 0))]
            scratch = [pltpu.VMEM(_stage_shape(len(groups), BLOCK), F32), pltpu.VMEM(_stage_shape(1, BLOCK), F32)]
        else:
            specs += [mode.spec(attn_w, "cur"), mode.spec(LANES, "cur")]
        head_of_lane = jnp.arange(attn_w, dtype=jnp.int32) // HEAD_DIM
        ins.append((jnp.arange(LANES, dtype=jnp.int32)[:, None] == head_of_lane[None, :]).astype(BF16))
        specs.append(_const_spec((LANES, attn_w)))
    if nat:
        out_shape = (jax.ShapeDtypeStruct((seq, attn_w), F32), jax.ShapeDtypeStruct((seq, LANES), F32))
    else:
        out_shape = (jax.ShapeDtypeStruct(_perm_shape(seq, attn_w), F32), jax.ShapeDtypeStruct(_perm_shape(seq, LANES), F32))
    return pl.pallas_call(
        body, name=f"attn_fwd_{name}", grid=(mode.residues, mode.nb),
        out_shape=out_shape, in_specs=specs, out_specs=(mode.spec(attn_w, "cur"), mode.spec(LANES, "cur")),
        scratch_shapes=scratch,
        compiler_params=_params(("arbitrary", "arbitrary")),
    )(*ins)


def attn_bwd(name, q, k, v, d_o, lse, delta, run):
    nat = name == "nat"
    seq = q.shape[0] if nat else q.shape[2] * PERM
    attn_w = q.shape[-1]
    mode = _Mode(name, seq)
    nb = mode.nb
    groups = _lane_groups(attn_w)
    first = run is None
    all_lanes = slice(0, LANES)

    def body(*refs):
        q_ref, kp_ref, kc_ref, vp_ref, vc_ref, do_ref, lse_ref, dl_ref = refs[:8]
        if first:
            dq_ref, dk_ref, dv_ref, ck, cv = refs[8:]
        else:
            dqi_ref, dki_ref, dvi_ref, dq_ref, dk_ref, dv_ref, ck, cv = refs[8:]
        n = pl.program_id(1)

        @pl.when(n == 0)
        def _():
            ck[...] = jnp.zeros_like(ck)
            cv[...] = jnp.zeros_like(cv)

        @pl.when(n < nb)
        def _():
            bias = mode.bias(n, True)
            bias2 = jnp.concatenate([bias, bias], axis=1)
            _, lo, m_lo, m_hi = _head_masks()
            lse_t = jnp.transpose(mode.get(lse_ref, all_lanes))
            dl_t = jnp.transpose(mode.get(dl_ref, all_lanes))
            def scores(p, sl):
                q2, do2 = mode.get(q_ref, sl), mode.get(do_ref, sl)
                kcat = jnp.concatenate([mode.get(kp_ref, sl), mode.get(kc_ref, sl)], axis=0)
                vcat = jnp.concatenate([mode.get(vp_ref, sl), mode.get(vc_ref, sl)], axis=0)
                qq = jnp.concatenate([q2 * m_lo, q2 * m_hi], axis=0)
                dd = jnp.concatenate([do2 * m_lo, do2 * m_hi], axis=0)
                h0 = 2 * p
                lse2 = jnp.concatenate([lse_t[h0:h0 + 1, :], lse_t[h0 + 1:h0 + 2, :]], axis=1)
                dl2 = jnp.concatenate([dl_t[h0:h0 + 1, :], dl_t[h0 + 1:h0 + 2, :]], axis=1)
                p_t = jnp.exp(_nt(kcat, qq) + (bias2 - lse2))
                ds_t = p_t * (_nt(vcat, dd) - dl2)
                return qq, dd, kcat, p_t.astype(BF16), ds_t.astype(BF16)

            def grads(sl, qq, dd, kcat, pb, dsb):
                dkc = jnp.dot(dsb, qq, preferred_element_type=F32)
                dvc = jnp.dot(pb, dd, preferred_element_type=F32)
                dqb = _tn(dsb, kcat)
                dq2 = jnp.where(lo, dqb[:BLOCK], dqb[BLOCK:]) * ATTN_SCALE
                dk2 = ck[:, sl] + dkc[:BLOCK]
                dv2 = cv[:, sl] + dvc[:BLOCK]
                if not first:
                    dq2 = dq2 + mode.get(dqi_ref, sl).astype(F32)
                    dk2 = dk2 + mode.get(dki_ref, sl).astype(F32)
                    dv2 = dv2 + mode.get(dvi_ref, sl).astype(F32)
                mode.put(dq_ref, sl, dq2)
                mode.put(dk_ref, sl, dk2)
                mode.put(dv_ref, sl, dv2)
                ck[:, sl] = dkc[BLOCK:]
                cv[:, sl] = dvc[BLOCK:]

            pending = None
            for p, sl in enumerate(groups):
                nxt = scores(p, sl)
                if pending is not None:
                    grads(*pending)
                pending = (sl, *nxt)
            grads(*pending)

        @pl.when(n == nb)
        def _():
            for sl in groups:
                if first:
                    mode.put(dk_ref, sl, ck[:, sl])
                    mode.put(dv_ref, sl, cv[:, sl])
                else:
                    mode.put(dk_ref, sl, ck[:, sl] + mode.get(dki_ref, sl).astype(F32))
                    mode.put(dv_ref, sl, cv[:, sl] + mode.get(dvi_ref, sl).astype(F32))

    last = nb - 1
    cur = lambda w: mode.spec(w, "cur", last)
    prev = lambda w: mode.spec(w, "prev")
    ins = [q, k, k, v, v, d_o, lse, delta]
    specs = [cur(attn_w), prev(attn_w), cur(attn_w), prev(attn_w), cur(attn_w), cur(attn_w), cur(LANES), cur(LANES)]
    if not first:
        ins += list(run)
        specs += [cur(attn_w), prev(attn_w), prev(attn_w)]
    shp = jax.ShapeDtypeStruct((seq, attn_w) if nat else _perm_shape(seq, attn_w), BF16)
    return pl.pallas_call(
        body, name=f"attn_bwd_{name}", grid=(mode.residues, nb + 1),
        out_shape=(shp, shp, shp), in_specs=specs, out_specs=(cur(attn_w), prev(attn_w), prev(attn_w)),
        scratch_shapes=[pltpu.VMEM((BLOCK, attn_w), F32), pltpu.VMEM((BLOCK, attn_w), F32)],
        compiler_params=_params(("arbitrary", "arbitrary")),
    )(*ins)


def _shift_down(u, halo, k):
    rolled = pltpu.roll(u, k, 0)
    row = lax.broadcasted_iota(jnp.int32, halo.shape, 0)
    top = jnp.where(row < k, pltpu.roll(halo, k, 0), rolled[:SUBLANES])
    return jnp.concatenate([top, rolled[SUBLANES:]], axis=0)


def _shift_up(u, halo, k):
    rows = u.shape[0]
    rolled = pltpu.roll(u, rows - k, 0)
    row = lax.broadcasted_iota(jnp.int32, halo.shape, 0)
    bot = jnp.where(row >= SUBLANES - k, pltpu.roll(halo, SUBLANES - k, 0), rolled[rows - SUBLANES:])
    return jnp.concatenate([rolled[:rows - SUBLANES], bot], axis=0)


def tail(o, lse, ga, cz, x, tgt, w_out, g2, cw):
    seq, d_model = x.shape
    attn_w = o.shape[1]
    conv_w = cz.shape[1] // 4
    mix = attn_w + conv_w
    groups = _lane_groups(attn_w)
    tm = ROW_TILE
    nt = seq // tm
    hb = tm // SUBLANES

    def body(o_ref, l_ref, ga_ref, cz_ref, hz_ref, x_ref, t_ref, w_ref, g_ref, cw_ref,
             do_ref, dl_ref, dop_ref, dlp_ref, lp_ref, dga_ref, dcb_ref, dgc_ref, dcv_ref, e_ref,
             dw_ref, dg_ref, dcw_ref, loss_ref, stage):
        i = pl.program_id(0)

        @pl.when(i == 0)
        def _():
            dw_ref[...] = jnp.zeros_like(dw_ref)
            dg_ref[...] = jnp.zeros_like(dg_ref)
            dcw_ref[...] = jnp.zeros_like(dcw_ref)
            loss_ref[...] = jnp.zeros_like(loss_ref)

        u = cz_ref[:, 2 * conv_w:3 * conv_w] * cz_ref[:, 0:conv_w]
        uh = hz_ref[:, 2 * conv_w:3 * conv_w] * hz_ref[:, 0:conv_w]
        uh = jnp.where(i > 0, uh, 0.0)
        u1 = _shift_down(u, uh, 1)
        u2 = _shift_down(u, uh, 2)
        w0, w1, w2 = cw_ref[0:1, :], cw_ref[1:2, :], cw_ref[2:3, :]
        cvv = u2 * w0 + u1 * w1 + u * w2
        gv = g_ref[...]
        all_lanes = slice(0, LANES)

        def forward(rs):
            ov, gav = o_ref[rs, :], ga_ref[rs, :]
            sig_a = _sigmoid(gav)
            silu_a = gav * sig_a
            cb, gc = cz_ref[rs, conv_w:2 * conv_w], cz_ref[rs, 3 * conv_w:4 * conv_w]
            sig_c = _sigmoid(gc)
            silu_c = gc * sig_c
            bc = cb * cvv[rs]
            mixed = jnp.concatenate([ov * silu_a, bc * silu_c], axis=1).astype(BF16)
            yv = jnp.dot(mixed, w_ref[...], preferred_element_type=F32)
            return ov, gav, sig_a, silu_a, cb, gc, sig_c, silu_c, bc, mixed, yv

        def loss_and_dy(rs, mixed, yv):
            r2 = lax.rsqrt(jnp.mean(yv * yv, axis=-1, keepdims=True) + NORM_EPS)
            yhat = yv * r2
            diff = (x_ref[rs, :] + yhat * gv) - t_ref[rs, :]
            loss_ref[...] += _rowgroup_sum(diff * diff)
            ev = diff * (1.0 / d_model)
            e_ref[rs, :] = ev
            dg_ref[...] += _rowgroup_sum(ev * yhat)
            eg = ev * gv
            dy = (r2 * (eg - yhat * jnp.mean(eg * yhat, axis=-1, keepdims=True))).astype(BF16)
            dw_ref[...] += _tn(mixed, dy)
            return _nt(dy, w_ref[...])

        def backward(rs, ov, gav, sig_a, silu_a, cb, gc, sig_c, silu_c, bc, dm):
            rows = rs.stop - rs.start
            dma, dmc = dm[:, :attn_w], dm[:, attn_w:]
            dov = dma * silu_a
            do_ref[rs, :] = dov.astype(BF16)
            dga_ref[rs, :] = (dma * ov * (sig_a * (1.0 + gav * (1.0 - sig_a)))).astype(BF16)
            prod = dov * ov
            lane = lax.broadcasted_iota(jnp.int32, (rows, LANES), 1)
            lo = lane < HEAD_DIM
            dblk = jnp.zeros((rows, LANES), F32)
            for p, sl in enumerate(groups):
                pr = prod[:, sl]
                dblk = jnp.where(lane == 2 * p, jnp.sum(jnp.where(lo, pr, 0.0), axis=1, keepdims=True), dblk)
                dblk = jnp.where(lane == 2 * p + 1, jnp.sum(jnp.where(lo, 0.0, pr), axis=1, keepdims=True), dblk)
                _stage_put(stage, p, dov[:, sl], rs.start)
            dl_ref[rs, :] = dblk
            _stage_put(stage, len(groups), dblk, rs.start)
            _stage_put(stage, len(groups) + 1, l_ref[rs, :], rs.start)
            dsc = dmc * silu_c
            cv_rows = cvv[rs]
            dcb_ref[rs, :] = (dsc * cv_rows).astype(BF16)
            dgc_ref[rs, :] = (dmc * bc * (sig_c * (1.0 + gc * (1.0 - sig_c)))).astype(BF16)
            dcv = dsc * cb
            dcv_ref[rs, :] = dcv
            dcw_ref[0:SUBLANES, :] += _rowgroup_sum(dcv * u2[rs])
            dcw_ref[SUBLANES:2 * SUBLANES, :] += _rowgroup_sum(dcv * u1[rs])
            dcw_ref[2 * SUBLANES:3 * SUBLANES, :] += _rowgroup_sum(dcv * u[rs])

        halves = [slice(0, tm // 2), slice(tm // 2, tm)]
        fwd = [forward(rs) for rs in halves]
        dms = [loss_and_dy(rs, f[9], f[10]) for rs, f in zip(halves, fwd)]
        for rs, f, dm in zip(halves, fwd, dms):
            backward(rs, *f[:9], dm)
        for p, sl in enumerate(groups):
            _to_perm(stage, p, dop_ref, sl, BF16)
        _to_perm(stage, len(groups), dlp_ref, all_lanes, F32)
        _to_perm(stage, len(groups) + 1, lp_ref, all_lanes, F32)

    row = lambda n: pl.BlockSpec((tm, n), lambda i: (i, 0))
    whole = lambda a, b: pl.BlockSpec((a, b), lambda i: (0, 0))
    return pl.pallas_call(
        body, name="tail", grid=(nt,),
        out_shape=(jax.ShapeDtypeStruct((seq, attn_w), BF16), jax.ShapeDtypeStruct((seq, LANES), F32),
                   jax.ShapeDtypeStruct(_perm_shape(seq, attn_w), BF16), jax.ShapeDtypeStruct(_perm_shape(seq, LANES), F32),
                   jax.ShapeDtypeStruct(_perm_shape(seq, LANES), F32),
                   jax.ShapeDtypeStruct((seq, attn_w), BF16), jax.ShapeDtypeStruct((seq, conv_w), BF16),
                   jax.ShapeDtypeStruct((seq, conv_w), BF16), jax.ShapeDtypeStruct((seq, conv_w), F32),
                   jax.ShapeDtypeStruct((seq, d_model), F32), jax.ShapeDtypeStruct((mix, d_model), F32),
                   jax.ShapeDtypeStruct((SUBLANES, d_model), F32), jax.ShapeDtypeStruct((CONV_K * SUBLANES, conv_w), F32),
                   jax.ShapeDtypeStruct((SUBLANES, d_model), F32)),
        in_specs=[row(attn_w), row(LANES), row(attn_w), row(4 * conv_w),
                  pl.BlockSpec((SUBLANES, 4 * conv_w), lambda i: (jnp.maximum(i * hb - 1, 0), 0)),
                  row(d_model), row(d_model), _const_spec((mix, d_model)), _const_spec((1, d_model)),
                  _const_spec((SUBLANES, conv_w))],
        out_specs=(row(attn_w), row(LANES), _perm_tile_spec(attn_w, tm), _perm_tile_spec(LANES, tm), _perm_tile_spec(LANES, tm),
                   row(attn_w), row(conv_w), row(conv_w), row(conv_w), row(d_model),
                   whole(mix, d_model), whole(SUBLANES, d_model), whole(CONV_K * SUBLANES, conv_w),
                   whole(SUBLANES, d_model)),
        scratch_shapes=[pltpu.VMEM(_stage_shape(len(groups) + 2, tm), F32)],
        compiler_params=_params(("arbitrary",)),
    )(o, lse, ga, cz, cz, x, tgt, w_out, g2, cw)


def dz_dx(nat_grads, perm_grads, dga, dcb, dgc, dcv, cz, tables, x, g1, e, w_full, cw):
    seq, d_model = x.shape
    attn_w = dga.shape[1]
    conv_w = dcv.shape[1]
    width = w_full.shape[2]
    in_w = 4 * attn_w + 4 * conv_w
    groups = _lane_groups(attn_w)
    tm = DZ_ROW_TILE
    nt = seq // tm
    hb = tm // SUBLANES

    def body(dq_ref, dk_ref, dv_ref, dqp_ref, dkp_ref, dvp_ref, dga_ref, dcb_ref, dgc_ref, dcv_ref, nh_ref, cz_ref,
             cos_ref, s1_ref, s2_ref, x_ref, g_ref, e_ref, w_ref, cw_ref, gx_ref, dz_ref, dg_ref, stage):
        i = pl.program_id(0)

        @pl.when(i == 0)
        def _():
            dg_ref[...] = jnp.zeros_like(dg_ref)

        cos, s1, s2 = cos_ref[...], s1_ref[...], s2_ref[...]
        for t, (nat_ref, perm_ref) in enumerate(((dq_ref, dqp_ref), (dk_ref, dkp_ref), (dv_ref, dvp_ref))):
            for g, sl in enumerate(groups):
                _from_perm(perm_ref, sl, stage, g)
            for g, sl in enumerate(groups):
                tot = nat_ref[:, sl].astype(F32) + _stage_get(stage, g)
                if t < 2:
                    tot = _rope_transposed(tot, cos, s1, s2)
                dz_ref[:, t * attn_w + g * LANES:t * attn_w + (g + 1) * LANES] = tot.astype(BF16)
        dz_ref[:, 3 * attn_w:4 * attn_w] = dga_ref[...]
        dcv = dcv_ref[...]
        nh = jnp.where(i < nt - 1, nh_ref[...], 0.0)
        w0, w1, w2 = cw_ref[0:1, :], cw_ref[1:2, :], cw_ref[2:3, :]
        du = dcv * w2 + _shift_up(dcv, nh, 1) * w1 + _shift_up(dcv, nh, 2) * w0
        base = 4 * attn_w
        dz_ref[:, base:base + conv_w] = (du * cz_ref[:, 2 * conv_w:3 * conv_w]).astype(BF16)
        dz_ref[:, base + conv_w:base + 2 * conv_w] = dcb_ref[...]
        dz_ref[:, base + 2 * conv_w:base + 3 * conv_w] = (du * cz_ref[:, 0:conv_w]).astype(BF16)
        dz_ref[:, base + 3 * conv_w:base + 4 * conv_w] = dgc_ref[...]

        dh = _nt(dz_ref[:, 0:width], w_ref[0])
        for j in range(1, N_CHIPS):
            dh = dh + _nt(dz_ref[:, j * width:(j + 1) * width], w_ref[j])
        xv = x_ref[...]
        r1 = lax.rsqrt(jnp.mean(xv * xv, axis=-1, keepdims=True) + NORM_EPS)
        xhat = xv * r1
        dg_ref[...] += _rowgroup_sum(dh * xhat)
        dhg = dh * g_ref[...]
        gx_ref[...] = r1 * (dhg - xhat * jnp.mean(dhg * xhat, axis=-1, keepdims=True)) + e_ref[...]

    row = lambda n: pl.BlockSpec((tm, n), lambda i: (i, 0))
    whole = lambda a, b: pl.BlockSpec((a, b), lambda i: (0, 0))
    pt = _perm_tile_spec(attn_w, tm)
    return pl.pallas_call(
        body, name="dz_dx", grid=(nt,),
        out_shape=(jax.ShapeDtypeStruct((seq, d_model), F32), jax.ShapeDtypeStruct((seq, in_w), BF16),
                   jax.ShapeDtypeStruct((SUBLANES, d_model), F32)),
        in_specs=[row(attn_w), row(attn_w), row(attn_w), pt, pt, pt, row(attn_w), row(conv_w), row(conv_w), row(conv_w),
                  pl.BlockSpec((SUBLANES, conv_w), lambda i: (jnp.minimum((i + 1) * hb, seq // SUBLANES - 1), 0)),
                  row(4 * conv_w), row(LANES), row(LANES), row(LANES), row(d_model), _const_spec((1, d_model)), row(d_model),
                  _const_spec(w_full.shape), _const_spec((SUBLANES, conv_w))],
        out_specs=(row(d_model), row(in_w), whole(SUBLANES, d_model)),
        scratch_shapes=[pltpu.VMEM(_stage_shape(len(groups), tm), F32)],
        compiler_params=_params(("arbitrary",)),
    )(*nat_grads, *perm_grads, dga, dcb, dgc, dcv, dcv, cz, *tables, x, g1, e, w_full, cw)


def dw_in_reduce(ht, dz):
    d_model, seq = ht.shape
    half = dz.shape[1] // N_DEV
    ts = min(2048, seq)
    steps = seq // ts
    x, y, c = lax.axis_index("x"), lax.axis_index("y"), lax.axis_index("c")
    far_first = lambda x, y: [(1 - x, 1 - y), (1 - x, y), (x, 1 - y)]
    chips = jnp.stack([2 * px + py for px, py in far_first(x, y)] + [2 * x + y]).astype(jnp.int32)
    order = jnp.stack([2 * chips + (1 - c), 2 * chips + c], axis=1).reshape(N_DEV)

    def body(order_ref, ht_ref, dz_ref, out_ref, acc, theirs, staged, contrib, resbuf, out_sem, sa, ra, sb, rb, sc, rc):
        del order_ref
        p, s = pl.program_id(0), pl.program_id(1)
        x, y, c = lax.axis_index("x"), lax.axis_index("y"), lax.axis_index("c")
        sib = (x, y, 1 - c)
        peers = far_first(x, y)
        slot = p % 2

        def a_copy(k):
            return pltpu.make_async_remote_copy(src_ref=acc.at[0], dst_ref=theirs.at[k], send_sem=sa.at[k], recv_sem=ra.at[k],
                                                device_id=sib, device_id_type=MESH)

        def b_copy(k):
            px, py = peers[k]
            return pltpu.make_async_remote_copy(src_ref=staged.at[k], dst_ref=contrib.at[k], send_sem=sb.at[k], recv_sem=rb.at[k],
                                                device_id=(px, py, c), device_id_type=MESH)

        def c_copy(which):
            return pltpu.make_async_remote_copy(src_ref=resbuf.at[which], dst_ref=resbuf.at[which], send_sem=sc, recv_sem=rc,
                                                device_id=sib, device_id_type=MESH)

        @pl.when(s == 0)
        def _():
            for k in range(N_CHIPS - 1):
                @pl.when(p == 2 * k + 2)
                def _():
                    a_copy(k).wait_send()
            acc[slot] = jnp.zeros((d_model, half), F32)

        acc[slot] += jnp.dot(ht_ref[...], dz_ref[...], preferred_element_type=F32)

        @pl.when(s == steps - 1)
        def _():
            for k in range(N_CHIPS):
                @pl.when(p == 2 * k)
                def _():
                    a_copy(k).start()
            for k in range(N_CHIPS - 1):
                @pl.when(p == 2 * k + 1)
                def _():
                    a_copy(k).wait_recv()
                    staged[k] = (acc[1] + theirs[k]).astype(BF16)
                    b_copy(k).start()

            @pl.when(p == N_DEV - 1)
            def _():
                a_copy(N_CHIPS - 1).wait_recv()
                tot = acc[1] + theirs[N_CHIPS - 1]
                for k in range(N_CHIPS - 1):
                    b_copy(k).wait_recv()
                    tot = tot + contrib[k].astype(F32)
                resbuf[c] = tot
                c_copy(c).start()
                c_copy(1 - c).wait_recv()
                done = pltpu.make_async_copy(resbuf, out_ref, out_sem)
                done.start()
                a_copy(N_CHIPS - 1).wait_send()
                for k in range(N_CHIPS - 1):
                    b_copy(k).wait_send()
                c_copy(c).wait_send()
                done.wait()

    dma = pltpu.SemaphoreType.DMA
    grid_spec = pltpu.PrefetchScalarGridSpec(
        num_scalar_prefetch=1, grid=(N_DEV, steps),
        in_specs=[pl.BlockSpec((d_model, ts), lambda p, s, order_ref: (0, s)),
                  pl.BlockSpec((ts, half), lambda p, s, order_ref: (s, order_ref[p]))],
        out_specs=pl.BlockSpec(memory_space=pl.ANY),
        scratch_shapes=[pltpu.VMEM((2, d_model, half), F32), pltpu.VMEM((N_CHIPS, d_model, half), F32),
                        pltpu.VMEM((N_CHIPS - 1, d_model, half), BF16), pltpu.VMEM((N_CHIPS - 1, d_model, half), BF16),
                        pltpu.VMEM((2, d_model, half), F32), dma,
                        dma((N_CHIPS,)), dma((N_CHIPS,)), dma((N_CHIPS - 1,)), dma((N_CHIPS - 1,)), dma, dma])
    return pl.pallas_call(
        body, name="dw_in_reduce", grid_spec=grid_spec,
        out_shape=jax.ShapeDtypeStruct((2, d_model, half), F32),
        compiler_params=_params(("arbitrary", "arbitrary")),
    )(order, ht, dz)


def grad_reduce(tensors, small):
    nt = len(tensors)
    split = [g.reshape(N_CHIPS, 2, *g.shape[1:]) for g in tensors]
    shapes = [g.shape[2:] for g in split]

    def body(*refs):
        srcs, sm_ref = refs[:nt], refs[nt]
        res, rs_ref = refs[nt + 1:2 * nt + 1], refs[2 * nt + 1]
        scratch = refs[2 * nt + 2:]
        mine, theirs, staged, contrib = (scratch[k * nt:(k + 1) * nt] for k in range(4))
        sbuf, loc_sems, sa, ra, sb, rb, sc, rc, ss, rs = scratch[4 * nt:]
        x, y, c = lax.axis_index("x"), lax.axis_index("y"), lax.axis_index("c")
        me = 2 * x + y
        sib = (x, y, 1 - c)

        flips = [(fx, fy, fc) for fx in (0, 1) for fy in (0, 1) for fc in (0, 1)][1:]
        my8 = 4 * x + 2 * y + c
        sbuf[my8] = sm_ref[...]

        def small_copy(k, slot, to):
            return pltpu.make_async_remote_copy(src_ref=sm_ref, dst_ref=sbuf.at[slot], send_sem=ss.at[k], recv_sem=rs.at[k],
                                                device_id=to, device_id_type=MESH)

        sends = []
        for k, (fx, fy, fc) in enumerate(flips):
            px, py, pc = _flip(x, fx), _flip(y, fy), _flip(c, fc)
            sends.append(small_copy(k, my8, (px, py, pc)))
            sends[-1].start()

        def a_copy(t, j):
            return pltpu.make_async_remote_copy(src_ref=srcs[t].at[j, 1 - c], dst_ref=theirs[t].at[j], send_sem=sa.at[t, j],
                                                recv_sem=ra.at[t, j], device_id=sib, device_id_type=MESH)

        peers = _chip_peers(x, y)
        order = [2 * px + py for px, py in peers] + [me]
        loads = [[pltpu.make_async_copy(srcs[t].at[j, c], mine[t].at[j], loc_sems.at[t, j]) for t in range(nt)] for j in order]
        for pos, j in enumerate(order):
            for t in range(nt):
                loads[pos][t].start()
                sends.append(a_copy(t, j))
                sends[-1].start()

        def b_copy(k, t, piece, slot, to):
            return pltpu.make_async_remote_copy(src_ref=staged[t].at[piece], dst_ref=contrib[t].at[slot], send_sem=sb.at[k, t],
                                                recv_sem=rb.at[k, t], device_id=to, device_id_type=MESH)

        for k, (px, py) in enumerate(peers):
            j = 2 * px + py
            for t in range(nt):
                loads[k][t].wait()
                a_copy(t, j).wait_recv()
                staged[t][j] = (mine[t][j] + theirs[t][j]).astype(BF16)
                sends.append(b_copy(k, t, j, me, (px, py, c)))
                sends[-1].start()
        for t in range(nt):
            loads[len(peers)][t].wait()
            a_copy(t, me).wait_recv()
            mine[t][me] = mine[t][me] + theirs[t][me]
            contrib[t][me] = mine[t][me].astype(BF16)
        for k, (px, py) in enumerate(peers):
            for t in range(nt):
                b_copy(k, t, me, 2 * px + py, (px, py, c)).wait_recv()

        def c_copy(t, half):
            return pltpu.make_async_remote_copy(src_ref=res[t].at[half], dst_ref=res[t].at[half], send_sem=sc.at[t],
                                                recv_sem=rc.at[t], device_id=sib, device_id_type=MESH)

        for t in range(nt):
            own = mine[t][me]
            term = lambda j: jnp.where(me == j, own, contrib[t][j].astype(F32))
            res[t][c] = ((term(0) + term(1)) + term(2)) + term(3)
            sends.append(c_copy(t, c))
            sends[-1].start()
        for t in range(nt):
            c_copy(t, 1 - c).wait_recv()

        for k, (fx, fy, fc) in enumerate(flips):
            px, py, pc = _flip(x, fx), _flip(y, fy), _flip(c, fc)
            small_copy(k, 4 * px + 2 * py + pc, (px, py, pc)).wait_recv()
        tot = sbuf[0]
        for d in range(1, N_DEV):
            tot = tot + sbuf[d]
        rs_ref[...] = tot
        for cp in sends:
            cp.wait_send()

    vm = pl.BlockSpec(memory_space=pltpu.VMEM)
    anyspace = pl.BlockSpec(memory_space=pl.ANY)
    dma = pltpu.SemaphoreType.DMA
    bufs = [pltpu.VMEM((N_CHIPS, *shp), dt) for dt in (F32, F32, BF16, BF16) for shp in shapes]
    outs = pl.pallas_call(
        body, name="grad_reduce",
        out_shape=(*[jax.ShapeDtypeStruct((2, *shp), F32) for shp in shapes], jax.ShapeDtypeStruct(small.shape, F32)),
        in_specs=[anyspace] * nt + [vm], out_specs=tuple([vm] * (nt + 1)),
        scratch_shapes=[*bufs, pltpu.VMEM((N_DEV, *small.shape), F32),
                        dma((nt, N_CHIPS)), dma((nt, N_CHIPS)), dma((nt, N_CHIPS)), dma((3, nt)), dma((3, nt)), dma((nt,)), dma((nt,)),
                        dma((N_DEV - 1,)), dma((N_DEV - 1,))],
        compiler_params=_params(),
    )(*split, small)
    return outs[:nt], outs[nt]


def _adam_math(w, g, m, v):
    m = ADAM_B1 * m + (1.0 - ADAM_B1) * g
    v = ADAM_B2 * v + (1.0 - ADAM_B2) * (g * g)
    m_hat = m / (1.0 - ADAM_B1 ** ADAM_STEP)
    v_hat = v / (1.0 - ADAM_B2 ** ADAM_STEP)
    delta = -ADAM_LR * (m_hat / (jnp.sqrt(v_hat) + ADAM_EPS) + ADAM_WD * w)
    return delta, m, v


def adam_shard(name, w, g2, m, v, block, grid, w_map, g_map):
    def body(w_ref, g_ref, m_ref, v_ref, go_ref, d_ref, mo_ref, vo_ref):
        g = g_ref[0]
        delta, mn, vn = _adam_math(w_ref[...], g, m_ref[...], v_ref[...])
        go_ref[...] = g
        d_ref[...] = delta
        mo_ref[...] = mn
        vo_ref[...] = vn

    ws = pl.BlockSpec(block, w_map)
    shp = jax.ShapeDtypeStruct(w.shape, F32)
    return pl.pallas_call(
        body, name=name, grid=grid, out_shape=(shp, shp, shp, shp),
        in_specs=[ws, pl.BlockSpec((1, *block), g_map), ws, ws], out_specs=(ws, ws, ws, ws),
        compiler_params=_params(("arbitrary",) * len(grid)),
    )(w, g2, m, v)


def adam_small(ws, gs, ms, vs):
    n = len(ws)

    def body(*refs):
        ins, outs = refs[:4 * n], refs[4 * n:]
        for t in range(n):
            delta, mn, vn = _adam_math(ins[t][...], ins[n + t][...], ins[2 * n + t][...], ins[3 * n + t][...])
            outs[3 * t][...] = delta
            outs[3 * t + 1][...] = mn
            outs[3 * t + 2][...] = vn

    vm = pl.BlockSpec(memory_space=pltpu.VMEM)
    outs = pl.pallas_call(
        body, name="adam_small",
        out_shape=tuple(jax.ShapeDtypeStruct(w.shape, F32) for w in ws for _ in range(3)),
        in_specs=[vm] * (4 * n), out_specs=tuple([vm] * (3 * n)),
        compiler_params=_params(),
    )(*ws, *gs, *ms, *vs)
    return [outs[3 * t:3 * t + 3] for t in range(n)]


def kernel(x, norm_pre_g, w_in, conv_w, w_out, norm_post_g, loss_target, m_norm_pre_g, m_w_in, m_conv_w, m_w_out, m_norm_post_g, v_norm_pre_g, v_w_in, v_conv_w, v_w_out, v_norm_post_g):
    _, seq, d_model = x.shape
    width = w_in.shape[1]
    conv_q = conv_w.shape[1]
    conv_width = N_CHIPS * conv_q
    attn_width = d_model - conv_width
    xs, tg = x[0], loss_target[0]
    g1, g2 = norm_pre_g.reshape(1, d_model), norm_post_g.reshape(1, d_model)

    w_full, wout_full, cw_full, *tables = gather_weights(w_in, w_out, conv_w, seq)
    wout2 = wout_full.reshape(attn_width + conv_width, d_model)
    cw = jnp.zeros((SUBLANES, conv_width), F32).at[:CONV_K].set(
        cw_full[:, :CONV_K, :conv_q].transpose(1, 0, 2).reshape(CONV_K, conv_width))

    ht, q, k, v, qp, kp, vp, ga, cz = inproj(xs, g1, w_full, tables, attn_width, conv_width)
    run = attn_fwd("p4", qp, kp, vp, None)
    run = attn_fwd("p16", qp, kp, vp, run)
    o, lse = attn_fwd("nat", q, k, v, run)
    (d_o, delta, d_op, delta_p, lse_p, dga, dcb, dgc, dcv, e, dwout, dg2, dcw, loss_acc) = tail(
        o, lse, ga, cz, xs, tg, wout2, g2, cw)
    nat_grads = attn_bwd("nat", q, k, v, d_o, lse, delta, None)
    perm_grads = attn_bwd("p4", qp, kp, vp, d_op, lse_p, delta_p, None)
    perm_grads = attn_bwd("p16", qp, kp, vp, d_op, lse_p, delta_p, perm_grads)
    grad_x, dz, dg1 = dz_dx(nat_grads, perm_grads, dga, dcb, dgc, dcv, cz, tables, xs, g1, e, w_full, cw)

    small = jnp.zeros((SUBLANES, d_model), F32)
    small = small.at[0].set(dg1.sum(axis=0)).at[1].set(dg2.sum(axis=0))
    small = small.at[2:2 + CONV_K, :conv_width].set(dcw.reshape(CONV_K, SUBLANES, conv_width).sum(axis=1))
    small = small.at[2 + CONV_K, 0].set(jnp.sum(loss_acc))
    (rout,), rsmall = grad_reduce([dwout.reshape(N_DEV, -1, d_model)], small)
    rin = dw_in_reduce(ht, dz)

    half = width // 2
    tr = 256
    gw_in, d_in, m_in, v_in = adam_shard(
        "adam_w_in", w_in, rin, m_w_in, v_w_in, (tr, half), (2, d_model // tr),
        lambda hf, i: (i, hf), lambda hf, i: (hf, i, 0))
    rq = w_out.shape[0] // 2
    gw_out, d_out, m_out, v_out = adam_shard(
        "adam_w_out", w_out, rout, m_w_out, v_w_out, (rq, d_model), (2,),
        lambda hf: (hf, 0), lambda hf: (hf, 0, 0))

    chip = 2 * lax.axis_index("x") + lax.axis_index("y")
    g_pre, g_post = rsmall[0:1], rsmall[1:2]
    g_conv = lax.dynamic_slice(rsmall[2:2 + CONV_K, :conv_width], (0, chip * conv_q), (CONV_K, conv_q))
    (d_pre, m_pre, v_pre), (d_post, m_post, v_post), (d_cv, m_cv, v_cv) = adam_small(
        [g1, g2, conv_w], [g_pre, g_post, g_conv],
        [m_norm_pre_g.reshape(1, d_model), m_norm_post_g.reshape(1, d_model), m_conv_w],
        [v_norm_pre_g.reshape(1, d_model), v_norm_post_g.reshape(1, d_model), v_conv_w])

    loss = 0.5 * rsmall[2 + CONV_K, 0] / d_model
    vec = lambda a: a.reshape(d_model)
    return (loss, grad_x.reshape(1, seq, d_model),
            vec(g_pre), gw_in, g_conv, gw_out, vec(g_post),
            vec(d_pre), d_in, d_cv, d_out, vec(d_post),
            vec(m_pre), m_in, m_cv, m_out, vec(m_post),
            vec(v_pre), v_in, v_cv, v_out, vec(v_post))
```

```python
import jax
import jax.numpy as jnp
from jax import lax
from jax.experimental import pallas as pl
from jax.experimental.pallas import tpu as pltpu

HEAD_DIM = 64
LANES = 128
SUBLANES = 8
BLOCK = 128
WINDOW_KEYS = 128
PERM = 16
PJ = 4
P4_ROWS = BLOCK // PJ
ROW_TILE = 512
DZ_ROW_TILE = 512
CONV_K = 3
ROPE_THETA = 10000.0
NORM_EPS = 1e-6
ATTN_SCALE = HEAD_DIM ** -0.5
NEG = -1e30
N_CHIPS = 4
N_DEV = 8
MESH = pl.DeviceIdType.MESH
ADAM_LR = 0.001
ADAM_B1 = 0.9
ADAM_B2 = 0.999
ADAM_EPS = 1e-08
ADAM_WD = 0.01
ADAM_STEP = 10
VMEM_LIMIT = 52 * 1024 * 1024

F32 = jnp.float32
BF16 = jnp.bfloat16


def _params(sem=None, **kw):
    return pltpu.CompilerParams(dimension_semantics=sem, vmem_limit_bytes=VMEM_LIMIT, **kw)


def _const_spec(shape):
    return pl.BlockSpec(shape, lambda *_: (0,) * len(shape), pipeline_mode=pl.Buffered(1))


def _sigmoid(z):
    return 1.0 / (1.0 + jnp.exp(-z))


def _rowgroup_sum(a):
    rows, n = a.shape
    return a.reshape(rows // SUBLANES, SUBLANES, n).sum(axis=0)


def _nt(a, b):
    return lax.dot_general(a, b, (((1,), (1,)), ((), ())), preferred_element_type=F32)


def _tn(a, b):
    return lax.dot_general(a, b, (((0,), (0,)), ((), ())), preferred_element_type=F32)


def _col_pieces(a, b, width):
    out = []
    while a < b:
        j = a // width
        e = min(b, (j + 1) * width)
        out.append((j, a - j * width, e - j * width))
        a = e
    return out


def _lane_groups(width):
    return [slice(g * LANES, (g + 1) * LANES) for g in range(width // LANES)]


def _perm_shape(seq, width):
    return (PJ, PJ, seq // PERM, width)


def _perm_tile_spec(width, tm):
    return pl.BlockSpec((PJ, PJ, tm // PERM, width), lambda i: (0, 0, i, 0))


STAGE_PITCH = 24


def _stage_shape(groups, rows):
    return (groups, rows // PERM * STAGE_PITCH, LANES)


def _stage_put(stage, g, val, row0=0):
    for a in range(val.shape[0] // PERM):
        at = (row0 // PERM + a) * STAGE_PITCH
        stage[g, at:at + PERM, :] = val[a * PERM:(a + 1) * PERM]


def _stage_get(stage, g):
    return jnp.concatenate([stage[g, a * STAGE_PITCH:a * STAGE_PITCH + PERM, :]
                            for a in range(stage.shape[1] // STAGE_PITCH)], axis=0)


def _to_perm(stage, g, dst_ref, sl, dtype):
    rows = stage.shape[1] // STAGE_PITCH
    for b in range(PERM):
        dst_ref[b // PJ, b % PJ, :, sl] = stage[g, pl.ds(b, rows, stride=STAGE_PITCH), :].astype(dtype)


def _from_perm(src_ref, sl, stage, g):
    rows = stage.shape[1] // STAGE_PITCH
    for b in range(PERM):
        stage[g, pl.ds(b, rows, stride=STAGE_PITCH), :] = src_ref[b // PJ, b % PJ, :, sl].astype(F32)


def _flip(a, f):
    return 1 - a if f else a


def _chip_peers(x, y):
    return [(1 - x, y), (x, 1 - y), (1 - x, 1 - y)]


def gather_weights(w_in, w_out, conv_w, seq):
    d_model, width = w_in.shape
    rows = w_out.shape[0]
    cw = jnp.zeros((SUBLANES, LANES), F32).at[:CONV_K, :conv_w.shape[1]].set(conv_w)
    half_dim = HEAD_DIM // 2
    inv_freq = ROPE_THETA ** (-jnp.arange(half_dim, dtype=F32) * 2.0 / HEAD_DIM)
    inv_freq = jnp.tile(inv_freq, LANES // half_dim).reshape(1, LANES)
    chunk = min(ROW_TILE, seq)

    def body(win_ref, wout_ref, cw_ref, freq_ref, winf_ref, woutf_ref, cwf_ref, cos_ref, s1_ref, s2_ref,
             st_in, st_out, ici_send, ici_recv, d2d_send, d2d_recv):
        x, y, c = lax.axis_index("x"), lax.axis_index("y"), lax.axis_index("c")
        me = 2 * x + y
        sib = (x, y, 1 - c)
        st_in[...] = win_ref[...].astype(BF16)
        st_out[...] = wout_ref[...].astype(BF16)
        winf_ref[me] = st_in[...]
        woutf_ref[me] = st_out[...]
        cwf_ref[me] = cw_ref[...]
        stages = (st_in, st_out)
        fulls = (winf_ref, woutf_ref)
        halves = (d_model // 2, rows // 2)

        def half(t, core):
            return pl.ds(pl.multiple_of(core * halves[t], halves[t]), halves[t])

        def ici(k, t, slot, to, core):
            src = stages[t].at[half(t, core)] if t < 2 else cw_ref
            dst = fulls[t].at[slot, half(t, core)] if t < 2 else cwf_ref.at[slot]
            return pltpu.make_async_remote_copy(src_ref=src, dst_ref=dst, send_sem=ici_send.at[k, t], recv_sem=ici_recv.at[k, t],
                                                device_id=to, device_id_type=MESH)

        def d2d(k, t, slot, core):
            ref = fulls[t].at[slot, half(t, core)]
            return pltpu.make_async_remote_copy(src_ref=ref, dst_ref=ref, send_sem=d2d_send.at[k, t], recv_sem=d2d_recv.at[k, t],
                                                device_id=sib, device_id_type=MESH)

        peers = _chip_peers(x, y)
        sends = [ici(k, t, me, (px, py, c), c) for k, (px, py) in enumerate(peers) for t in range(3)]
        for cp in sends:
            cp.start()

        first_half = lax.broadcasted_iota(jnp.int32, (chunk, LANES), 1) % HEAD_DIM < half_dim
        row = lax.broadcasted_iota(jnp.int32, (chunk, LANES), 0)

        def table_rows(i, carry):
            at = pl.multiple_of(i * chunk, chunk)
            ang = (row + at).astype(F32) * freq_ref[...]
            sin = jnp.sin(ang)
            cos_ref[pl.ds(at, chunk), :] = jnp.cos(ang)
            s1_ref[pl.ds(at, chunk), :] = jnp.where(first_half, -sin, 0.0)
            s2_ref[pl.ds(at, chunk), :] = jnp.where(first_half, 0.0, sin)
            return carry

        lax.fori_loop(0, seq // chunk, table_rows, 0)

        for k, (px, py) in enumerate(peers):
            for t in range(2):
                ici(k, t, 2 * px + py, (px, py, c), c).wait_recv()
                fwd = d2d(k, t, 2 * px + py, c)
                fwd.start()
                sends.append(fwd)
            ici(k, 2, 2 * px + py, (px, py, c), c).wait_recv()
        for k, (px, py) in enumerate(peers):
            for t in range(2):
                d2d(k, t, 2 * px + py, 1 - c).wait_recv()
        for cp in sends:
            cp.wait_send()

    vm = pl.BlockSpec(memory_space=pltpu.VMEM)
    dma = pltpu.SemaphoreType.DMA
    return pl.pallas_call(
        body, name="gather_weights",
        out_shape=(jax.ShapeDtypeStruct((N_CHIPS, d_model, width), BF16),
                   jax.ShapeDtypeStruct((N_CHIPS, rows, d_model), BF16),
                   jax.ShapeDtypeStruct((N_CHIPS, SUBLANES, LANES), F32),
                   *[jax.ShapeDtypeStruct((seq, LANES), F32)] * 3),
        in_specs=[vm, vm, vm, vm], out_specs=(vm,) * 6,
        scratch_shapes=[pltpu.VMEM((d_model, width), BF16), pltpu.VMEM((rows, d_model), BF16),
                        dma((3, 3)), dma((3, 3)), dma((3, 2)), dma((3, 2))],
        compiler_params=_params(),
    )(w_in, w_out, cw, inv_freq)


def _rope(t, cos, s1, s2):
    return t * cos + pltpu.roll(t, LANES - HEAD_DIM // 2, 1) * s1 + pltpu.roll(t, HEAD_DIM // 2, 1) * s2


def _rope_transposed(g, cos, s1, s2):
    return g * cos + pltpu.roll(g * s1, HEAD_DIM // 2, 1) + pltpu.roll(g * s2, LANES - HEAD_DIM // 2, 1)


def inproj(x, g1, w_full, tables, attn_w, conv_w):
    seq, d_model = x.shape
    width = w_full.shape[2]
    tm = ROW_TILE
    groups = _lane_groups(attn_w)

    def body(x_ref, g_ref, w_ref, cos_ref, s1_ref, s2_ref,
             ht_ref, q_ref, k_ref, v_ref, qp_ref, kp_ref, vp_ref, ga_ref, cz_ref, stage):
        xv = x_ref[...]
        hb = ((xv * lax.rsqrt(jnp.mean(xv * xv, axis=-1, keepdims=True) + NORM_EPS)) * g_ref[...]).astype(BF16)
        ht_ref[...] = jnp.transpose(hb)
        cos, s1, s2 = cos_ref[...], s1_ref[...], s2_ref[...]

        def proj(a, b):
            parts = [jnp.dot(hb, w_ref[j, :, lo:hi], preferred_element_type=F32) for j, lo, hi in _col_pieces(a, b, width)]
            return parts[0] if len(parts) == 1 else jnp.concatenate(parts, axis=1)

        def emit(z, nat_ref, perm_ref, fn):
            for g, sl in enumerate(groups):
                val = fn(z[:, sl])
                nat_ref[:, sl] = val.astype(BF16)
                _stage_put(stage, g, val)
            for g, sl in enumerate(groups):
                _to_perm(stage, g, perm_ref, sl, BF16)

        emit(proj(0, attn_w), q_ref, qp_ref, lambda t: _rope(t, cos, s1, s2) * ATTN_SCALE)
        emit(proj(attn_w, 2 * attn_w), k_ref, kp_ref, lambda t: _rope(t, cos, s1, s2))
        emit(proj(2 * attn_w, 3 * attn_w), v_ref, vp_ref, lambda t: t)
        ga_ref[...] = proj(3 * attn_w, 4 * attn_w)
        cz_ref[...] = proj(4 * attn_w, 4 * attn_w + 4 * conv_w)

    row = lambda n: pl.BlockSpec((tm, n), lambda i: (i, 0))
    nat = jax.ShapeDtypeStruct((seq, attn_w), BF16)
    perm = jax.ShapeDtypeStruct(_perm_shape(seq, attn_w), BF16)
    return pl.pallas_call(
        body, name="inproj", grid=(seq // tm,),
        out_shape=(jax.ShapeDtypeStruct((d_model, seq), BF16), nat, nat, nat, perm, perm, perm,
                   jax.ShapeDtypeStruct((seq, attn_w), F32), jax.ShapeDtypeStruct((seq, 4 * conv_w), F32)),
        in_specs=[row(d_model), _const_spec((1, d_model)), _const_spec(w_full.shape), row(LANES), row(LANES), row(LANES)],
        out_specs=(pl.BlockSpec((d_model, tm), lambda i: (0, i)), row(attn_w), row(attn_w), row(attn_w),
                   _perm_tile_spec(attn_w, tm), _perm_tile_spec(attn_w, tm), _perm_tile_spec(attn_w, tm),
                   row(attn_w), row(4 * conv_w)),
        scratch_shapes=[pltpu.VMEM(_stage_shape(len(groups), tm), F32)],
        compiler_params=_params(("arbitrary",)),
    )(x, g1, w_full, *tables)


class _Mode:
    def __init__(self, name, seq):
        self.name = name
        if name == "nat":
            self.residues, blocks = 1, seq // BLOCK
        elif name == "p16":
            self.residues, blocks = PERM, seq // PERM // BLOCK
        else:
            self.residues, blocks = PJ, seq // PERM // P4_ROWS
        self.qb = 2 if blocks % 2 == 0 else 1
        self.steps = blocks // self.qb

    def _spec(self, blocks, width, index):
        if self.name == "nat":
            return pl.BlockSpec((blocks * BLOCK, width), lambda r, n: (index(n), 0))
        if self.name == "p16":
            return pl.BlockSpec((1, 1, blocks * BLOCK, width), lambda r, n: (r // PJ, r % PJ, index(n), 0))
        return pl.BlockSpec((PJ, 1, blocks * P4_ROWS, width), lambda r, n: (0, r, index(n), 0))

    def wide(self, width, last=None):
        return self._spec(self.qb, width, (lambda n: n) if last is None else (lambda n: jnp.minimum(n, last)))

    def wide_before(self, width):
        return self._spec(self.qb, width, lambda n: jnp.maximum(n - 1, 0))

    def block_before(self, width, last=None):
        step = (lambda n: n) if last is None else (lambda n: jnp.minimum(n, last))
        return self._spec(1, width, lambda n: jnp.maximum(self.qb * step(n) - 1, 0))

    def get(self, ref, sl, sub=0):
        if self.name == "nat":
            return ref[sub * BLOCK:(sub + 1) * BLOCK, sl]
        if self.name == "p16":
            return ref[0, 0, sub * BLOCK:(sub + 1) * BLOCK, sl]
        return jnp.concatenate([ref[j, 0, sub * P4_ROWS:(sub + 1) * P4_ROWS, sl] for j in range(PJ)], axis=0)

    def put(self, ref, sl, val, sub=0):
        val = val.astype(ref.dtype)
        if self.name == "nat":
            ref[sub * BLOCK:(sub + 1) * BLOCK, sl] = val
        elif self.name == "p16":
            ref[0, 0, sub * BLOCK:(sub + 1) * BLOCK, sl] = val
        else:
            for j in range(PJ):
                ref[j, 0, sub * P4_ROWS:(sub + 1) * P4_ROWS, sl] = val[j * P4_ROWS:(j + 1) * P4_ROWS]

    def keys(self, before_ref, wide_ref, sl, sub):
        older = self.get(before_ref, sl) if sub == 0 else self.get(wide_ref, sl, sub - 1)
        return jnp.concatenate([older, self.get(wide_ref, sl, sub)], axis=0)

    def index(self, idx, is_key):
        if self.name != "p4":
            return idx - BLOCK if is_key else idx
        within = jnp.bitwise_and(idx, BLOCK - 1)
        m = PJ * jnp.bitwise_and(within, P4_ROWS - 1) + jnp.right_shift(within, P4_ROWS.bit_length() - 1)
        return m + BLOCK * (jnp.right_shift(idx, BLOCK.bit_length() - 1) - 1) if is_key else m

    def bias(self, has_before):
        shape = (2 * BLOCK, BLOCK)
        kidx = lax.broadcasted_iota(jnp.int32, shape, 0)
        qidx = lax.broadcasted_iota(jnp.int32, shape, 1)
        rel = self.index(qidx, False) - self.index(kidx, True)
        valid = (rel >= 0) & (rel <= WINDOW_KEYS)
        if has_before is not True:
            valid = valid & ((kidx >= BLOCK) | has_before)
        one = jnp.where(valid, 0.0, NEG)
        return jnp.concatenate([one, one], axis=1)


def _head_masks():
    lane = lax.broadcasted_iota(jnp.int32, (BLOCK, LANES), 1)
    lo = lane < HEAD_DIM
    return lane, lo, jnp.where(lo, 1.0, 0.0).astype(BF16), jnp.where(lo, 0.0, 1.0).astype(BF16)


def _column(blk, lane, h):
    return jnp.sum(jnp.where(lane == h, blk, 0.0), axis=1, keepdims=True)


def attn_fwd(name, q, k, v, run):
    nat = name == "nat"
    seq = q.shape[0] if nat else q.shape[2] * PERM
    attn_w = q.shape[-1]
    mode = _Mode(name, seq)
    groups = _lane_groups(attn_w)
    first = run is None
    all_lanes = slice(0, LANES)

    def body(*refs):
        q_ref, kp_ref, kc_ref, vp_ref, vc_ref = refs[:5]
        if first:
            o_ref, l_ref = refs[5:]
        elif nat:
            oin_ref, lin_ref, ex_ref, o_ref, l_ref, ostage, lstage = refs[5:]
        else:
            oin_ref, lin_ref, ex_ref, o_ref, l_ref = refs[5:]
        n = pl.program_id(1)
        subs = range(mode.qb)
        biases = [mode.bias(n > 0)] + [mode.bias(True)] * (mode.qb - 1)
        _, lo, m_lo, m_hi = _head_masks()
        head_row = lax.broadcasted_iota(jnp.int32, (BLOCK, LANES), 0)
        ones = jnp.ones((2 * BLOCK, LANES), BF16)
        lrows = [jnp.zeros((BLOCK, LANES), F32) for _ in subs]

        def probs(sub, sl):
            q2 = mode.get(q_ref, sl, sub)
            kcat = mode.keys(kp_ref, kc_ref, sl, sub)
            vcat = mode.keys(vp_ref, vc_ref, sl, sub)
            qq = jnp.concatenate([q2 * m_lo, q2 * m_hi], axis=0)
            s_t = _nt(kcat, qq) + biases[sub]
            m = jnp.max(s_t, axis=0, keepdims=True)
            pe = jnp.exp(s_t - m)
            l = jnp.sum(pe, axis=0, keepdims=True)
            return jnp.concatenate([vcat, ones], axis=1), pe.astype(BF16), m + jnp.log(l)

        def output(sub, p, sl, vext, pb, lse):
            o_ext = _tn(pb, vext)
            o_new = o_ext[:, :LANES] / o_ext[:, LANES:]
            mode.put(o_ref, sl, jnp.where(lo, o_new[:BLOCK], o_new[BLOCK:]), sub)
            rows = jnp.where(head_row == 2 * p, lse[:, :BLOCK], lrows[sub])
            lrows[sub] = jnp.where(head_row == 2 * p + 1, lse[:, BLOCK:], rows)

        pending = None
        for sub in subs:
            for p, sl in enumerate(groups):
                nxt = probs(sub, sl)
                if pending is not None:
                    output(*pending)
                pending = (sub, p, sl, *nxt)
        output(*pending)
        if not first and nat:
            for g, sl in enumerate(groups):
                _from_perm(oin_ref, sl, ostage, g)
            _from_perm(lin_ref, all_lanes, lstage, 0)
        for sub in subs:
            rows = slice(sub * BLOCK, (sub + 1) * BLOCK)
            lblk = jnp.transpose(lrows[sub])
            if first:
                mode.put(l_ref, all_lanes, lblk, sub)
                continue
            lin = _stage_get(lstage, 0)[rows] if nat else mode.get(lin_ref, all_lanes, sub)
            mx = jnp.maximum(lin, lblk)
            new = mx + jnp.log(jnp.exp(lin - mx) + jnp.exp(lblk - mx))
            mode.put(l_ref, all_lanes, new, sub)

            def expand(w):
                hi = w.astype(BF16)
                rest = (w - hi.astype(F32)).astype(BF16)
                ex = ex_ref[...]
                return jnp.dot(hi, ex, preferred_element_type=F32) + jnp.dot(rest, ex, preferred_element_type=F32)

            w_prev, w_cur = expand(jnp.exp(lin - new)), expand(jnp.exp(lblk - new))
            for p, sl in enumerate(groups):
                o_prev = _stage_get(ostage, p)[rows] if nat else mode.get(oin_ref, sl, sub)
                mode.put(o_ref, sl, w_prev[:, sl] * o_prev + w_cur[:, sl] * mode.get(o_ref, sl, sub), sub)

    ins = [q, k, k, v, v]
    specs = [mode.wide(attn_w), mode.block_before(attn_w), mode.wide(attn_w), mode.block_before(attn_w), mode.wide(attn_w)]
    scratch = []
    if not first:
        ins += list(run)
        if nat:
            rows_a = mode.qb * BLOCK // PERM
            specs += [pl.BlockSpec((PJ, PJ, rows_a, attn_w), lambda r, n: (0, 0, n, 0)),
                      pl.BlockSpec((PJ, PJ, rows_a, LANES), lambda r, n: (0, 0, n, 0))]
            scratch = [pltpu.VMEM(_stage_shape(len(groups), mode.qb * BLOCK), F32),
                       pltpu.VMEM(_stage_shape(1, mode.qb * BLOCK), F32)]
        else:
            specs += [mode.wide(attn_w), mode.wide(LANES)]
        head_of_lane = jnp.arange(attn_w, dtype=jnp.int32) // HEAD_DIM
        ins.append((jnp.arange(LANES, dtype=jnp.int32)[:, None] == head_of_lane[None, :]).astype(BF16))
        specs.append(_const_spec((LANES, attn_w)))
    if nat:
        out_shape = (jax.ShapeDtypeStruct((seq, attn_w), F32), jax.ShapeDtypeStruct((seq, LANES), F32))
    else:
        out_shape = (jax.ShapeDtypeStruct(_perm_shape(seq, attn_w), F32), jax.ShapeDtypeStruct(_perm_shape(seq, LANES), F32))
    return pl.pallas_call(
        body, name=f"attn_fwd_{name}", grid=(mode.residues, mode.steps),
        out_shape=out_shape, in_specs=specs, out_specs=(mode.wide(attn_w), mode.wide(LANES)),
        scratch_shapes=scratch,
        compiler_params=_params(("arbitrary", "arbitrary")),
    )(*ins)


def attn_bwd(name, q, k, v, d_o, lse, delta, run):
    nat = name == "nat"
    seq = q.shape[0] if nat else q.shape[2] * PERM
    attn_w = q.shape[-1]
    mode = _Mode(name, seq)
    steps, qb = mode.steps, mode.qb
    groups = _lane_groups(attn_w)
    first = run is None
    all_lanes = slice(0, LANES)

    def body(*refs):
        q_ref, kp_ref, kc_ref, vp_ref, vc_ref, do_ref, lse_ref, dl_ref = refs[:8]
        if first:
            dq_ref, dk_ref, dv_ref, ck, cv = refs[8:]
        else:
            dqi_ref, dki_ref, dvi_ref, dq_ref, dk_ref, dv_ref, ck, cv = refs[8:]
        n = pl.program_id(1)
        carries = ((ck, dk_ref, None if first else dki_ref), (cv, dv_ref, None if first else dvi_ref))

        def emit(out_ref, acc_ref, sl, sub, val):
            if acc_ref is not None:
                val = val + mode.get(acc_ref, sl, sub).astype(F32)
            mode.put(out_ref, sl, val, sub)

        @pl.when(n == 0)
        def _():
            ck[...] = jnp.zeros_like(ck)
            cv[...] = jnp.zeros_like(cv)

        @pl.when(n < steps)
        def _():
            biases = [mode.bias(n > 0)] + [mode.bias(True)] * (qb - 1)
            _, lo, m_lo, m_hi = _head_masks()

            def scores(sub, p, sl, lse_t, dl_t):
                q2, do2 = mode.get(q_ref, sl, sub), mode.get(do_ref, sl, sub)
                kcat = mode.keys(kp_ref, kc_ref, sl, sub)
                vcat = mode.keys(vp_ref, vc_ref, sl, sub)
                qq = jnp.concatenate([q2 * m_lo, q2 * m_hi], axis=0)
                dd = jnp.concatenate([do2 * m_lo, do2 * m_hi], axis=0)
                h0 = 2 * p
                lse2 = jnp.concatenate([lse_t[h0:h0 + 1, :], lse_t[h0 + 1:h0 + 2, :]], axis=1)
                dl2 = jnp.concatenate([dl_t[h0:h0 + 1, :], dl_t[h0 + 1:h0 + 2, :]], axis=1)
                p_t = jnp.exp(_nt(kcat, qq) + (biases[sub] - lse2))
                ds_t = p_t * (_nt(vcat, dd) - dl2)
                return qq, dd, kcat, p_t.astype(BF16), ds_t.astype(BF16)

            def grads(sub, sl, qq, dd, kcat, pb, dsb):
                dqb = _tn(dsb, kcat)
                dq2 = jnp.where(lo, dqb[:BLOCK], dqb[BLOCK:]) * ATTN_SCALE
                if not first:
                    dq2 = dq2 + mode.get(dqi_ref, sl, sub).astype(F32)
                mode.put(dq_ref, sl, dq2, sub)
                for (carry, out_ref, acc_ref), lhs, rhs in zip(carries, (dsb, pb), (qq, dd)):
                    both = jnp.dot(lhs, rhs, preferred_element_type=F32)
                    if sub == 0:
                        for s in range(qb - 1):
                            emit(out_ref, acc_ref, sl, s, carry[s, :, sl])
                        emit(out_ref, acc_ref, sl, qb - 1, carry[qb - 1, :, sl] + both[:BLOCK])
                        carry[0, :, sl] = both[BLOCK:]
                    else:
                        carry[sub - 1, :, sl] += both[:BLOCK]
                        carry[sub, :, sl] = both[BLOCK:]

            stats = [(jnp.transpose(mode.get(lse_ref, all_lanes, sub)),
                      jnp.transpose(mode.get(dl_ref, all_lanes, sub))) for sub in range(qb)]
            pending = None
            for p, sl in enumerate(groups):
                for sub in range(qb):
                    nxt = scores(sub, p, sl, *stats[sub])
                    if pending is not None:
                        grads(*pending)
                    pending = (sub, sl, *nxt)
            grads(*pending)

        @pl.when(n == steps)
        def _():
            for carry, out_ref, acc_ref in carries:
                for sl in groups:
                    for s in range(qb):
                        emit(out_ref, acc_ref, sl, s, carry[s, :, sl])

    last = steps - 1
    wide = lambda w: mode.wide(w, last)
    ins = [q, k, k, v, v, d_o, lse, delta]
    specs = [wide(attn_w), mode.block_before(attn_w, last), wide(attn_w), mode.block_before(attn_w, last), wide(attn_w),
             wide(attn_w), wide(LANES), wide(LANES)]
    if not first:
        ins += list(run)
        specs += [wide(attn_w), mode.wide_before(attn_w), mode.wide_before(attn_w)]
    shp = jax.ShapeDtypeStruct((seq, attn_w) if nat else _perm_shape(seq, attn_w), BF16)
    return pl.pallas_call(
        body, name=f"attn_bwd_{name}", grid=(mode.residues, steps + 1),
        out_shape=(shp, shp, shp), in_specs=specs,
        out_specs=(wide(attn_w), mode.wide_before(attn_w), mode.wide_before(attn_w)),
        scratch_shapes=[pltpu.VMEM((qb, BLOCK, attn_w), F32), pltpu.VMEM((qb, BLOCK, attn_w), F32)],
        compiler_params=_params(("arbitrary", "arbitrary")),
    )(*ins)


def _shift_down(u, halo, k):
    rolled = pltpu.roll(u, k, 0)
    row = lax.broadcasted_iota(jnp.int32, halo.shape, 0)
    top = jnp.where(row < k, pltpu.roll(halo, k, 0), rolled[:SUBLANES])
    return jnp.concatenate([top, rolled[SUBLANES:]], axis=0)


def _shift_up(u, halo, k):
    rows = u.shape[0]
    rolled = pltpu.roll(u, rows - k, 0)
    row = lax.broadcasted_iota(jnp.int32, halo.shape, 0)
    bot = jnp.where(row >= SUBLANES - k, pltpu.roll(halo, SUBLANES - k, 0), rolled[rows - SUBLANES:])
    return jnp.concatenate([rolled[:rows - SUBLANES], bot], axis=0)


def tail(o, lse, ga, cz, x, tgt, w_out, g2, cw):
    seq, d_model = x.shape
    attn_w = o.shape[1]
    conv_w = cz.shape[1] // 4
    mix = attn_w + conv_w
    groups = _lane_groups(attn_w)
    tm = ROW_TILE
    nt = seq // tm
    hb = tm // SUBLANES

    def body(o_ref, l_ref, ga_ref, cz_ref, hz_ref, x_ref, t_ref, w_ref, g_ref, cw_ref,
             do_ref, dl_ref, dop_ref, dlp_ref, lp_ref, dga_ref, dcb_ref, dgc_ref, dcv_ref, e_ref,
             dw_ref, dg_ref, dcw_ref, loss_ref, stage):
        i = pl.program_id(0)

        @pl.when(i == 0)
        def _():
            dw_ref[...] = jnp.zeros_like(dw_ref)
            dg_ref[...] = jnp.zeros_like(dg_ref)
            dcw_ref[...] = jnp.zeros_like(dcw_ref)
            loss_ref[...] = jnp.zeros_like(loss_ref)

        u = cz_ref[:, 2 * conv_w:3 * conv_w] * cz_ref[:, 0:conv_w]
        uh = hz_ref[:, 2 * conv_w:3 * conv_w] * hz_ref[:, 0:conv_w]
        uh = jnp.where(i > 0, uh, 0.0)
        u1 = _shift_down(u, uh, 1)
        u2 = _shift_down(u, uh, 2)
        w0, w1, w2 = cw_ref[0:1, :], cw_ref[1:2, :], cw_ref[2:3, :]
        cvv = u2 * w0 + u1 * w1 + u * w2
        gv = g_ref[...]
        all_lanes = slice(0, LANES)

        def forward(rs):
            ov, gav = o_ref[rs, :], ga_ref[rs, :]
            sig_a = _sigmoid(gav)
            silu_a = gav * sig_a
            cb, gc = cz_ref[rs, conv_w:2 * conv_w], cz_ref[rs, 3 * conv_w:4 * conv_w]
            sig_c = _sigmoid(gc)
            silu_c = gc * sig_c
            bc = cb * cvv[rs]
            mixed = jnp.concatenate([ov * silu_a, bc * silu_c], axis=1).astype(BF16)
            yv = jnp.dot(mixed, w_ref[...], preferred_element_type=F32)
            return ov, gav, sig_a, silu_a, cb, gc, sig_c, silu_c, bc, mixed, yv

        def loss_and_dy(rs, mixed, yv):
            r2 = lax.rsqrt(jnp.mean(yv * yv, axis=-1, keepdims=True) + NORM_EPS)
            yhat = yv * r2
            diff = (x_ref[rs, :] + yhat * gv) - t_ref[rs, :]
            loss_ref[...] += _rowgroup_sum(diff * diff)
            ev = diff * (1.0 / d_model)
            e_ref[rs, :] = ev
            dg_ref[...] += _rowgroup_sum(ev * yhat)
            eg = ev * gv
            dy = (r2 * (eg - yhat * jnp.mean(eg * yhat, axis=-1, keepdims=True))).astype(BF16)
            dw_ref[...] += _tn(mixed, dy)
            return _nt(dy, w_ref[...])

        def backward(rs, ov, gav, sig_a, silu_a, cb, gc, sig_c, silu_c, bc, dm):
            rows = rs.stop - rs.start
            dma, dmc = dm[:, :attn_w], dm[:, attn_w:]
            dov = dma * silu_a
            do_ref[rs, :] = dov.astype(BF16)
            dga_ref[rs, :] = (dma * ov * (sig_a * (1.0 + gav * (1.0 - sig_a)))).astype(BF16)
            prod = dov * ov
            lane = lax.broadcasted_iota(jnp.int32, (rows, LANES), 1)
            lo = lane < HEAD_DIM
            dblk = jnp.zeros((rows, LANES), F32)
            for p, sl in enumerate(groups):
                pr = prod[:, sl]
                dblk = jnp.where(lane == 2 * p, jnp.sum(jnp.where(lo, pr, 0.0), axis=1, keepdims=True), dblk)
                dblk = jnp.where(lane == 2 * p + 1, jnp.sum(jnp.where(lo, 0.0, pr), axis=1, keepdims=True), dblk)
                _stage_put(stage, p, dov[:, sl], rs.start)
            dl_ref[rs, :] = dblk
            _stage_put(stage, len(groups), dblk, rs.start)
            _stage_put(stage, len(groups) + 1, l_ref[rs, :], rs.start)
            dsc = dmc * silu_c
            cv_rows = cvv[rs]
            dcb_ref[rs, :] = (dsc * cv_rows).astype(BF16)
            dgc_ref[rs, :] = (dmc * bc * (sig_c * (1.0 + gc * (1.0 - sig_c)))).astype(BF16)
            dcv = dsc * cb
            dcv_ref[rs, :] = dcv
            dcw_ref[0:SUBLANES, :] += _rowgroup_sum(dcv * u2[rs])
            dcw_ref[SUBLANES:2 * SUBLANES, :] += _rowgroup_sum(dcv * u1[rs])
            dcw_ref[2 * SUBLANES:3 * SUBLANES, :] += _rowgroup_sum(dcv * u[rs])

        halves = [slice(0, tm // 2), slice(tm // 2, tm)]
        fwd = [forward(rs) for rs in halves]
        dms = [loss_and_dy(rs, f[9], f[10]) for rs, f in zip(halves, fwd)]
        for rs, f, dm in zip(halves, fwd, dms):
            backward(rs, *f[:9], dm)
        for p, sl in enumerate(groups):
            _to_perm(stage, p, dop_ref, sl, BF16)
        _to_perm(stage, len(groups), dlp_ref, all_lanes, F32)
        _to_perm(stage, len(groups) + 1, lp_ref, all_lanes, F32)

    row = lambda n: pl.BlockSpec((tm, n), lambda i: (i, 0))
    whole = lambda a, b: pl.BlockSpec((a, b), lambda i: (0, 0))
    return pl.pallas_call(
        body, name="tail", grid=(nt,),
        out_shape=(jax.ShapeDtypeStruct((seq, attn_w), BF16), jax.ShapeDtypeStruct((seq, LANES), F32),
                   jax.ShapeDtypeStruct(_perm_shape(seq, attn_w), BF16), jax.ShapeDtypeStruct(_perm_shape(seq, LANES), F32),
                   jax.ShapeDtypeStruct(_perm_shape(seq, LANES), F32),
                   jax.ShapeDtypeStruct((seq, attn_w), BF16), jax.ShapeDtypeStruct((seq, conv_w), BF16),
                   jax.ShapeDtypeStruct((seq, conv_w), BF16), jax.ShapeDtypeStruct((seq, conv_w), F32),
                   jax.ShapeDtypeStruct((seq, d_model), F32), jax.ShapeDtypeStruct((mix, d_model), F32),
                   jax.ShapeDtypeStruct((SUBLANES, d_model), F32), jax.ShapeDtypeStruct((CONV_K * SUBLANES, conv_w), F32),
                   jax.ShapeDtypeStruct((SUBLANES, d_model), F32)),
        in_specs=[row(attn_w), row(LANES), row(attn_w), row(4 * conv_w),
                  pl.BlockSpec((SUBLANES, 4 * conv_w), lambda i: (jnp.maximum(i * hb - 1, 0), 0)),
                  row(d_model), row(d_model), _const_spec((mix, d_model)), _const_spec((1, d_model)),
                  _const_spec((SUBLANES, conv_w))],
        out_specs=(row(attn_w), row(LANES), _perm_tile_spec(attn_w, tm), _perm_tile_spec(LANES, tm), _perm_tile_spec(LANES, tm),
                   row(attn_w), row(conv_w), row(conv_w), row(conv_w), row(d_model),
                   whole(mix, d_model), whole(SUBLANES, d_model), whole(CONV_K * SUBLANES, conv_w),
                   whole(SUBLANES, d_model)),
        scratch_shapes=[pltpu.VMEM(_stage_shape(len(groups) + 2, tm), F32)],
        compiler_params=_params(("arbitrary",)),
    )(o, lse, ga, cz, cz, x, tgt, w_out, g2, cw)


def dz_dx(nat_grads, perm_grads, dga, dcb, dgc, dcv, cz, tables, x, g1, e, w_full, cw):
    seq, d_model = x.shape
    attn_w = dga.shape[1]
    conv_w = dcv.shape[1]
    width = w_full.shape[2]
    in_w = 4 * attn_w + 4 * conv_w
    groups = _lane_groups(attn_w)
    tm = DZ_ROW_TILE
    nt = seq // tm
    hb = tm // SUBLANES

    def body(dq_ref, dk_ref, dv_ref, dqp_ref, dkp_ref, dvp_ref, dga_ref, dcb_ref, dgc_ref, dcv_ref, nh_ref, cz_ref,
             cos_ref, s1_ref, s2_ref, x_ref, g_ref, e_ref, w_ref, cw_ref, gx_ref, dz_ref, dg_ref, stage):
        i = pl.program_id(0)

        @pl.when(i == 0)
        def _():
            dg_ref[...] = jnp.zeros_like(dg_ref)

        cos, s1, s2 = cos_ref[...], s1_ref[...], s2_ref[...]
        for t, (nat_ref, perm_ref) in enumerate(((dq_ref, dqp_ref), (dk_ref, dkp_ref), (dv_ref, dvp_ref))):
            for g, sl in enumerate(groups):
                _from_perm(perm_ref, sl, stage, g)
            for g, sl in enumerate(groups):
                tot = nat_ref[:, sl].astype(F32) + _stage_get(stage, g)
                if t < 2:
                    tot = _rope_transposed(tot, cos, s1, s2)
                dz_ref[:, t * attn_w + g * LANES:t * attn_w + (g + 1) * LANES] = tot.astype(BF16)
        dz_ref[:, 3 * attn_w:4 * attn_w] = dga_ref[...]
        dcv = dcv_ref[...]
        nh = jnp.where(i < nt - 1, nh_ref[...], 0.0)
        w0, w1, w2 = cw_ref[0:1, :], cw_ref[1:2, :], cw_ref[2:3, :]
        du = dcv * w2 + _shift_up(dcv, nh, 1) * w1 + _shift_up(dcv, nh, 2) * w0
        base = 4 * attn_w
        dz_ref[:, base:base + conv_w] = (du * cz_ref[:, 2 * conv_w:3 * conv_w]).astype(BF16)
        dz_ref[:, base + conv_w:base + 2 * conv_w] = dcb_ref[...]
        dz_ref[:, base + 2 * conv_w:base + 3 * conv_w] = (du * cz_ref[:, 0:conv_w]).astype(BF16)
        dz_ref[:, base + 3 * conv_w:base + 4 * conv_w] = dgc_ref[...]

        dh = _nt(dz_ref[:, 0:width], w_ref[0])
        for j in range(1, N_CHIPS):
            dh = dh + _nt(dz_ref[:, j * width:(j + 1) * width], w_ref[j])
        xv = x_ref[...]
        r1 = lax.rsqrt(jnp.mean(xv * xv, axis=-1, keepdims=True) + NORM_EPS)
        xhat = xv * r1
        dg_ref[...] += _rowgroup_sum(dh * xhat)
        dhg = dh * g_ref[...]
        gx_ref[...] = r1 * (dhg - xhat * jnp.mean(dhg * xhat, axis=-1, keepdims=True)) + e_ref[...]

    row = lambda n: pl.BlockSpec((tm, n), lambda i: (i, 0))
    whole = lambda a, b: pl.BlockSpec((a, b), lambda i: (0, 0))
    pt = _perm_tile_spec(attn_w, tm)
    return pl.pallas_call(
        body, name="dz_dx", grid=(nt,),
        out_shape=(jax.ShapeDtypeStruct((seq, d_model), F32), jax.ShapeDtypeStruct((seq, in_w), BF16),
                   jax.ShapeDtypeStruct((SUBLANES, d_model), F32)),
        in_specs=[row(attn_w), row(attn_w), row(attn_w), pt, pt, pt, row(attn_w), row(conv_w), row(conv_w), row(conv_w),
                  pl.BlockSpec((SUBLANES, conv_w), lambda i: (jnp.minimum((i + 1) * hb, seq // SUBLANES - 1), 0)),
                  row(4 * conv_w), row(LANES), row(LANES), row(LANES), row(d_model), _const_spec((1, d_model)), row(d_model),
                  _const_spec(w_full.shape), _const_spec((SUBLANES, conv_w))],
        out_specs=(row(d_model), row(in_w), whole(SUBLANES, d_model)),
        scratch_shapes=[pltpu.VMEM(_stage_shape(len(groups), tm), F32)],
        compiler_params=_params(("arbitrary",)),
    )(*nat_grads, *perm_grads, dga, dcb, dgc, dcv, dcv, cz, *tables, x, g1, e, w_full, cw)


def dw_in_reduce(ht, dz):
    d_model, seq = ht.shape
    half = dz.shape[1] // N_DEV
    ts = min(2048, seq)
    steps = seq // ts
    x, y, c = lax.axis_index("x"), lax.axis_index("y"), lax.axis_index("c")
    far_first = lambda x, y: [(1 - x, 1 - y), (1 - x, y), (x, 1 - y)]
    chips = jnp.stack([2 * px + py for px, py in far_first(x, y)] + [2 * x + y]).astype(jnp.int32)
    order = jnp.stack([2 * chips + (1 - c), 2 * chips + c], axis=1).reshape(N_DEV)

    def body(order_ref, ht_ref, dz_ref, out_ref, acc, theirs, staged, contrib, resbuf, out_sem, sa, ra, sb, rb, sc, rc):
        del order_ref
        p, s = pl.program_id(0), pl.program_id(1)
        x, y, c = lax.axis_index("x"), lax.axis_index("y"), lax.axis_index("c")
        sib = (x, y, 1 - c)
        peers = far_first(x, y)
        slot = p % 2

        def a_copy(k):
            return pltpu.make_async_remote_copy(src_ref=acc.at[0], dst_ref=theirs.at[k], send_sem=sa.at[k], recv_sem=ra.at[k],
                                                device_id=sib, device_id_type=MESH)

        def b_copy(k):
            px, py = peers[k]
            return pltpu.make_async_remote_copy(src_ref=staged.at[k], dst_ref=contrib.at[k], send_sem=sb.at[k], recv_sem=rb.at[k],
                                                device_id=(px, py, c), device_id_type=MESH)

        def c_copy(which):
            return pltpu.make_async_remote_copy(src_ref=resbuf.at[which], dst_ref=resbuf.at[which], send_sem=sc, recv_sem=rc,
                                                device_id=sib, device_id_type=MESH)

        @pl.when(s == 0)
        def _():
            for k in range(N_CHIPS - 1):
                @pl.when(p == 2 * k + 2)
                def _():
                    a_copy(k).wait_send()
            acc[slot] = jnp.zeros((d_model, half), F32)

        acc[slot] += jnp.dot(ht_ref[...], dz_ref[...], preferred_element_type=F32)

        @pl.when(s == steps - 1)
        def _():
            for k in range(N_CHIPS):
                @pl.when(p == 2 * k)
                def _():
                    a_copy(k).start()
            for k in range(N_CHIPS - 1):
                @pl.when(p == 2 * k + 1)
                def _():
                    a_copy(k).wait_recv()
                    staged[k] = (acc[1] + theirs[k]).astype(BF16)
                    b_copy(k).start()

            @pl.when(p == N_DEV - 1)
            def _():
                a_copy(N_CHIPS - 1).wait_recv()
                tot = acc[1] + theirs[N_CHIPS - 1]
                for k in range(N_CHIPS - 1):
                    b_copy(k).wait_recv()
                    tot = tot + contrib[k].astype(F32)
                resbuf[c] = tot
                c_copy(c).start()
                c_copy(1 - c).wait_recv()
                done = pltpu.make_async_copy(resbuf, out_ref, out_sem)
                done.start()
                a_copy(N_CHIPS - 1).wait_send()
                for k in range(N_CHIPS - 1):
                    b_copy(k).wait_send()
                c_copy(c).wait_send()
                done.wait()

    dma = pltpu.SemaphoreType.DMA
    grid_spec = pltpu.PrefetchScalarGridSpec(
        num_scalar_prefetch=1, grid=(N_DEV, steps),
        in_specs=[pl.BlockSpec((d_model, ts), lambda p, s, order_ref: (0, s)),
                  pl.BlockSpec((ts, half), lambda p, s, order_ref: (s, order_ref[p]))],
        out_specs=pl.BlockSpec(memory_space=pl.ANY),
        scratch_shapes=[pltpu.VMEM((2, d_model, half), F32), pltpu.VMEM((N_CHIPS, d_model, half), F32),
                        pltpu.VMEM((N_CHIPS - 1, d_model, half), BF16), pltpu.VMEM((N_CHIPS - 1, d_model, half), BF16),
                        pltpu.VMEM((2, d_model, half), F32), dma,
                        dma((N_CHIPS,)), dma((N_CHIPS,)), dma((N_CHIPS - 1,)), dma((N_CHIPS - 1,)), dma, dma])
    return pl.pallas_call(
        body, name="dw_in_reduce", grid_spec=grid_spec,
        out_shape=jax.ShapeDtypeStruct((2, d_model, half), F32),
        compiler_params=_params(("arbitrary", "arbitrary")),
    )(order, ht, dz)


def grad_reduce(tensors, small):
    nt = len(tensors)
    split = [g.reshape(N_CHIPS, 2, *g.shape[1:]) for g in tensors]
    shapes = [g.shape[2:] for g in split]

    def body(*refs):
        srcs, sm_ref = refs[:nt], refs[nt]
        res, rs_ref = refs[nt + 1:2 * nt + 1], refs[2 * nt + 1]
        scratch = refs[2 * nt + 2:]
        mine, theirs, staged, contrib = (scratch[k * nt:(k + 1) * nt] for k in range(4))
        sbuf, loc_sems, sa, ra, sb, rb, sc, rc, ss, rs = scratch[4 * nt:]
        x, y, c = lax.axis_index("x"), lax.axis_index("y"), lax.axis_index("c")
        me = 2 * x + y
        sib = (x, y, 1 - c)

        flips = [(fx, fy, fc) for fx in (0, 1) for fy in (0, 1) for fc in (0, 1)][1:]
        my8 = 4 * x + 2 * y + c
        sbuf[my8] = sm_ref[...]

        def small_copy(k, slot, to):
            return pltpu.make_async_remote_copy(src_ref=sm_ref, dst_ref=sbuf.at[slot], send_sem=ss.at[k], recv_sem=rs.at[k],
                                                device_id=to, device_id_type=MESH)

        sends = []
        for k, (fx, fy, fc) in enumerate(flips):
            px, py, pc = _flip(x, fx), _flip(y, fy), _flip(c, fc)
            sends.append(small_copy(k, my8, (px, py, pc)))
            sends[-1].start()

        def a_copy(t, j):
            return pltpu.make_async_remote_copy(src_ref=srcs[t].at[j, 1 - c], dst_ref=theirs[t].at[j], send_sem=sa.at[t, j],
                                                recv_sem=ra.at[t, j], device_id=sib, device_id_type=MESH)

        peers = _chip_peers(x, y)
        order = [2 * px + py for px, py in peers] + [me]
        loads = [[pltpu.make_async_copy(srcs[t].at[j, c], mine[t].at[j], loc_sems.at[t, j]) for t in range(nt)] for j in order]
        for pos, j in enumerate(order):
            for t in range(nt):
                loads[pos][t].start()
                sends.append(a_copy(t, j))
                sends[-1].start()

        def b_copy(k, t, piece, slot, to):
            return pltpu.make_async_remote_copy(src_ref=staged[t].at[piece], dst_ref=contrib[t].at[slot], send_sem=sb.at[k, t],
                                                recv_sem=rb.at[k, t], device_id=to, device_id_type=MESH)

        for k, (px, py) in enumerate(peers):
            j = 2 * px + py
            for t in range(nt):
                loads[k][t].wait()
                a_copy(t, j).wait_recv()
                staged[t][j] = (mine[t][j] + theirs[t][j]).astype(BF16)
                sends.append(b_copy(k, t, j, me, (px, py, c)))
                sends[-1].start()
        for t in range(nt):
            loads[len(peers)][t].wait()
            a_copy(t, me).wait_recv()
            mine[t][me] = mine[t][me] + theirs[t][me]
            contrib[t][me] = mine[t][me].astype(BF16)
        for k, (px, py) in enumerate(peers):
            for t in range(nt):
                b_copy(k, t, me, 2 * px + py, (px, py, c)).wait_recv()

        def c_copy(t, half):
            return pltpu.make_async_remote_copy(src_ref=res[t].at[half], dst_ref=res[t].at[half], send_sem=sc.at[t],
                                                recv_sem=rc.at[t], device_id=sib, device_id_type=MESH)

        for t in range(nt):
            own = mine[t][me]
            term = lambda j: jnp.where(me == j, own, contrib[t][j].astype(F32))
            res[t][c] = ((term(0) + term(1)) + term(2)) + term(3)
            sends.append(c_copy(t, c))
            sends[-1].start()
        for t in range(nt):
            c_copy(t, 1 - c).wait_recv()

        for k, (fx, fy, fc) in enumerate(flips):
            px, py, pc = _flip(x, fx), _flip(y, fy), _flip(c, fc)
            small_copy(k, 4 * px + 2 * py + pc, (px, py, pc)).wait_recv()
        tot = sbuf[0]
        for d in range(1, N_DEV):
            tot = tot + sbuf[d]
        rs_ref[...] = tot
        for cp in sends:
            cp.wait_send()

    vm = pl.BlockSpec(memory_space=pltpu.VMEM)
    anyspace = pl.BlockSpec(memory_space=pl.ANY)
    dma = pltpu.SemaphoreType.DMA
    bufs = [pltpu.VMEM((N_CHIPS, *shp), dt) for dt in (F32, F32, BF16, BF16) for shp in shapes]
    outs = pl.pallas_call(
        body, name="grad_reduce",
        out_shape=(*[jax.ShapeDtypeStruct((2, *shp), F32) for shp in shapes], jax.ShapeDtypeStruct(small.shape, F32)),
        in_specs=[anyspace] * nt + [vm], out_specs=tuple([vm] * (nt + 1)),
        scratch_shapes=[*bufs, pltpu.VMEM((N_DEV, *small.shape), F32),
                        dma((nt, N_CHIPS)), dma((nt, N_CHIPS)), dma((nt, N_CHIPS)), dma((3, nt)), dma((3, nt)), dma((nt,)), dma((nt,)),
                        dma((N_DEV - 1,)), dma((N_DEV - 1,))],
        compiler_params=_params(),
    )(*split, small)
    return outs[:nt], outs[nt]


def _adam_math(w, g, m, v):
    m = ADAM_B1 * m + (1.0 - ADAM_B1) * g
    v = ADAM_B2 * v + (1.0 - ADAM_B2) * (g * g)
    m_hat = m / (1.0 - ADAM_B1 ** ADAM_STEP)
    v_hat = v / (1.0 - ADAM_B2 ** ADAM_STEP)
    delta = -ADAM_LR * (m_hat / (jnp.sqrt(v_hat) + ADAM_EPS) + ADAM_WD * w)
    return delta, m, v


def adam_shard(name, w, g2, m, v, block, grid, w_map, g_map):
    def body(w_ref, g_ref, m_ref, v_ref, go_ref, d_ref, mo_ref, vo_ref):
        g = g_ref[0]
        delta, mn, vn = _adam_math(w_ref[...], g, m_ref[...], v_ref[...])
        go_ref[...] = g
        d_ref[...] = delta
        mo_ref[...] = mn
        vo_ref[...] = vn

    ws = pl.BlockSpec(block, w_map)
    shp = jax.ShapeDtypeStruct(w.shape, F32)
    return pl.pallas_call(
        body, name=name, grid=grid, out_shape=(shp, shp, shp, shp),
        in_specs=[ws, pl.BlockSpec((1, *block), g_map), ws, ws], out_specs=(ws, ws, ws, ws),
        compiler_params=_params(("arbitrary",) * len(grid)),
    )(w, g2, m, v)


def adam_small(ws, gs, ms, vs):
    n = len(ws)

    def body(*refs):
        ins, outs = refs[:4 * n], refs[4 * n:]
        for t in range(n):
            delta, mn, vn = _adam_math(ins[t][...], ins[n + t][...], ins[2 * n + t][...], ins[3 * n + t][...])
            outs[3 * t][...] = delta
            outs[3 * t + 1][...] = mn
            outs[3 * t + 2][...] = vn

    vm = pl.BlockSpec(memory_space=pltpu.VMEM)
    outs = pl.pallas_call(
        body, name="adam_small",
        out_shape=tuple(jax.ShapeDtypeStruct(w.shape, F32) for w in ws for _ in range(3)),
        in_specs=[vm] * (4 * n), out_specs=tuple([vm] * (3 * n)),
        compiler_params=_params(),
    )(*ws, *gs, *ms, *vs)
    return [outs[3 * t:3 * t + 3] for t in range(n)]


def kernel(x, norm_pre_g, w_in, conv_w, w_out, norm_post_g, loss_target, m_norm_pre_g, m_w_in, m_conv_w, m_w_out, m_norm_post_g, v_norm_pre_g, v_w_in, v_conv_w, v_w_out, v_norm_post_g):
    _, seq, d_model = x.shape
    width = w_in.shape[1]
    conv_q = conv_w.shape[1]
    conv_width = N_CHIPS * conv_q
    attn_width = d_model - conv_width
    xs, tg = x[0], loss_target[0]
    g1, g2 = norm_pre_g.reshape(1, d_model), norm_post_g.reshape(1, d_model)

    w_full, wout_full, cw_full, *tables = gather_weights(w_in, w_out, conv_w, seq)
    wout2 = wout_full.reshape(attn_width + conv_width, d_model)
    cw = jnp.zeros((SUBLANES, conv_width), F32).at[:CONV_K].set(
        cw_full[:, :CONV_K, :conv_q].transpose(1, 0, 2).reshape(CONV_K, conv_width))

    ht, q, k, v, qp, kp, vp, ga, cz = inproj(xs, g1, w_full, tables, attn_width, conv_width)
    run = attn_fwd("p4", qp, kp, vp, None)
    run = attn_fwd("p16", qp, kp, vp, run)
    o, lse = attn_fwd("nat", q, k, v, run)
    (d_o, delta, d_op, delta_p, lse_p, dga, dcb, dgc, dcv, e, dwout, dg2, dcw, loss_acc) = tail(
        o, lse, ga, cz, xs, tg, wout2, g2, cw)
    nat_grads = attn_bwd("nat", q, k, v, d_o, lse, delta, None)
    perm_grads = attn_bwd("p4", qp, kp, vp, d_op, lse_p, delta_p, None)
    perm_grads = attn_bwd("p16", qp, kp, vp, d_op, lse_p, delta_p, perm_grads)
    grad_x, dz, dg1 = dz_dx(nat_grads, perm_grads, dga, dcb, dgc, dcv, cz, tables, xs, g1, e, w_full, cw)

    small = jnp.zeros((SUBLANES, d_model), F32)
    small = small.at[0].set(dg1.sum(axis=0)).at[1].set(dg2.sum(axis=0))
    small = small.at[2:2 + CONV_K, :conv_width].set(dcw.reshape(CONV_K, SUBLANES, conv_width).sum(axis=1))
    small = small.at[2 + CONV_K, 0].set(jnp.sum(loss_acc))
    (rout,), rsmall = grad_reduce([dwout.reshape(N_DEV, -1, d_model)], small)
    rin = dw_in_reduce(ht, dz)

    half = width // 2
    tr = 256
    gw_in, d_in, m_in, v_in = adam_shard(
        "adam_w_in", w_in, rin, m_w_in, v_w_in, (tr, half), (2, d_model // tr),
        lambda hf, i: (i, hf), lambda hf, i: (hf, i, 0))
    rq = w_out.shape[0] // 2
    gw_out, d_out, m_out, v_out = adam_shard(
        "adam_w_out", w_out, rout, m_w_out, v_w_out, (rq, d_model), (2,),
        lambda hf: (hf, 0), lambda hf: (hf, 0, 0))

    chip = 2 * lax.axis_index("x") + lax.axis_index("y")
    g_pre, g_post = rsmall[0:1], rsmall[1:2]
    g_conv = lax.dynamic_slice(rsmall[2:2 + CONV_K, :conv_width], (0, chip * conv_q), (CONV_K, conv_q))
    (d_pre, m_pre, v_pre), (d_post, m_post, v_post), (d_cv, m_cv, v_cv) = adam_small(
        [g1, g2, conv_w], [g_pre, g_post, g_conv],
        [m_norm_pre_g.reshape(1, d_model), m_norm_post_g.reshape(1, d_model), m_conv_w],
        [v_norm_pre_g.reshape(1, d_model), v_norm_post_g.reshape(1, d_model), v_conv_w])

    loss = 0.5 * rsmall[2 + CONV_K, 0] / d_model
    vec = lambda a: a.reshape(d_model)
    return (loss, grad_x.reshape(1, seq, d_model),
            vec(g_pre), gw_in, g_conv, gw_out, vec(g_post),
            vec(d_pre), d_in, d_cv, d_out, vec(d_post),
            vec(m_pre), m_in, m_cv, m_out, vec(m_post),
            vec(v_pre), v_in, v_cv, v_out, vec(v_post))
```

```python
import jax
import jax.numpy as jnp
from jax import lax
from jax.experimental import pallas as pl
from jax.experimental.pallas import tpu as pltpu

HEAD_DIM = 64
LANES = 128
SUBLANES = 8
BLOCK = 128
WINDOW_KEYS = 128
PERM = 16
PJ = 4
P4_ROWS = BLOCK // PJ
MAX_QUERY_BLOCKS = 4
ROW_TILE = 512
DZ_ROW_TILE = 512
CONV_K = 3
ROPE_THETA = 10000.0
NORM_EPS = 1e-6
ATTN_SCALE = HEAD_DIM ** -0.5
NEG = -1e30
N_CHIPS = 4
N_DEV = 8
MESH = pl.DeviceIdType.MESH
ADAM_LR = 0.001
ADAM_B1 = 0.9
ADAM_B2 = 0.999
ADAM_EPS = 1e-08
ADAM_WD = 0.01
ADAM_STEP = 10
VMEM_LIMIT = 52 * 1024 * 1024

F32 = jnp.float32
BF16 = jnp.bfloat16


def _params(sem=None, **kw):
    return pltpu.CompilerParams(dimension_semantics=sem, vmem_limit_bytes=VMEM_LIMIT, **kw)


def _const_spec(shape):
    return pl.BlockSpec(shape, lambda *_: (0,) * len(shape), pipeline_mode=pl.Buffered(1))


def _sigmoid(z):
    return 1.0 / (1.0 + jnp.exp(-z))


def _rowgroup_sum(a):
    rows, n = a.shape
    return a.reshape(rows // SUBLANES, SUBLANES, n).sum(axis=0)


def _nt(a, b):
    return lax.dot_general(a, b, (((1,), (1,)), ((), ())), preferred_element_type=F32)


def _tn(a, b):
    return lax.dot_general(a, b, (((0,), (0,)), ((), ())), preferred_element_type=F32)


def _col_pieces(a, b, width):
    out = []
    while a < b:
        j = a // width
        e = min(b, (j + 1) * width)
        out.append((j, a - j * width, e - j * width))
        a = e
    return out


def _lane_groups(width):
    return [slice(g * LANES, (g + 1) * LANES) for g in range(width // LANES)]


def _perm_shape(seq, width):
    return (PJ, PJ, seq // PERM, width)


def _perm_tile_spec(width, tm):
    return pl.BlockSpec((PJ, PJ, tm // PERM, width), lambda i: (0, 0, i, 0))


STAGE_PITCH = 24


def _stage_shape(groups, rows):
    return (groups, rows // PERM * STAGE_PITCH, LANES)


def _stage_put(stage, g, val, row0=0):
    for a in range(val.shape[0] // PERM):
        at = (row0 // PERM + a) * STAGE_PITCH
        stage[g, at:at + PERM, :] = val[a * PERM:(a + 1) * PERM]


def _stage_get(stage, g):
    return jnp.concatenate([stage[g, a * STAGE_PITCH:a * STAGE_PITCH + PERM, :]
                            for a in range(stage.shape[1] // STAGE_PITCH)], axis=0)


def _to_perm(stage, g, dst_ref, sl, dtype):
    rows = stage.shape[1] // STAGE_PITCH
    for b in range(PERM):
        dst_ref[b // PJ, b % PJ, :, sl] = stage[g, pl.ds(b, rows, stride=STAGE_PITCH), :].astype(dtype)


def _from_perm(src_ref, sl, stage, g):
    rows = stage.shape[1] // STAGE_PITCH
    for b in range(PERM):
        stage[g, pl.ds(b, rows, stride=STAGE_PITCH), :] = src_ref[b // PJ, b % PJ, :, sl].astype(F32)


def _flip(a, f):
    return 1 - a if f else a


def _chip_peers(x, y):
    return [(1 - x, y), (x, 1 - y), (1 - x, 1 - y)]


def gather_weights(w_in, w_out, conv_w, seq):
    d_model, width = w_in.shape
    rows = w_out.shape[0]
    cw = jnp.zeros((SUBLANES, LANES), F32).at[:CONV_K, :conv_w.shape[1]].set(conv_w)
    half_dim = HEAD_DIM // 2
    inv_freq = ROPE_THETA ** (-jnp.arange(half_dim, dtype=F32) * 2.0 / HEAD_DIM)
    inv_freq = jnp.tile(inv_freq, LANES // half_dim).reshape(1, LANES)
    chunk = min(ROW_TILE, seq)

    def body(win_ref, wout_ref, cw_ref, freq_ref, winf_ref, woutf_ref, cwf_ref, cos_ref, s1_ref, s2_ref,
             st_in, st_out, ici_send, ici_recv, d2d_send, d2d_recv):
        x, y, c = lax.axis_index("x"), lax.axis_index("y"), lax.axis_index("c")
        me = 2 * x + y
        sib = (x, y, 1 - c)
        st_in[...] = win_ref[...].astype(BF16)
        st_out[...] = wout_ref[...].astype(BF16)
        winf_ref[me] = st_in[...]
        woutf_ref[me] = st_out[...]
        cwf_ref[me] = cw_ref[...]
        stages = (st_in, st_out)
        fulls = (winf_ref, woutf_ref)
        halves = (d_model // 2, rows // 2)

        def half(t, core):
            return pl.ds(pl.multiple_of(core * halves[t], halves[t]), halves[t])

        def ici(k, t, slot, to, core):
            src = stages[t].at[half(t, core)] if t < 2 else cw_ref
            dst = fulls[t].at[slot, half(t, core)] if t < 2 else cwf_ref.at[slot]
            return pltpu.make_async_remote_copy(src_ref=src, dst_ref=dst, send_sem=ici_send.at[k, t], recv_sem=ici_recv.at[k, t],
                                                device_id=to, device_id_type=MESH)

        def d2d(k, t, slot, core):
            ref = fulls[t].at[slot, half(t, core)]
            return pltpu.make_async_remote_copy(src_ref=ref, dst_ref=ref, send_sem=d2d_send.at[k, t], recv_sem=d2d_recv.at[k, t],
                                                device_id=sib, device_id_type=MESH)

        peers = _chip_peers(x, y)
        sends = [ici(k, t, me, (px, py, c), c) for k, (px, py) in enumerate(peers) for t in range(3)]
        for cp in sends:
            cp.start()

        first_half = lax.broadcasted_iota(jnp.int32, (chunk, LANES), 1) % HEAD_DIM < half_dim
        row = lax.broadcasted_iota(jnp.int32, (chunk, LANES), 0)

        def table_rows(i, carry):
            at = pl.multiple_of(i * chunk, chunk)
            ang = (row + at).astype(F32) * freq_ref[...]
            sin = jnp.sin(ang)
            cos_ref[pl.ds(at, chunk), :] = jnp.cos(ang)
            s1_ref[pl.ds(at, chunk), :] = jnp.where(first_half, -sin, 0.0)
            s2_ref[pl.ds(at, chunk), :] = jnp.where(first_half, 0.0, sin)
            return carry

        lax.fori_loop(0, seq // chunk, table_rows, 0)

        for k, (px, py) in enumerate(peers):
            for t in range(2):
                ici(k, t, 2 * px + py, (px, py, c), c).wait_recv()
                fwd = d2d(k, t, 2 * px + py, c)
                fwd.start()
                sends.append(fwd)
            ici(k, 2, 2 * px + py, (px, py, c), c).wait_recv()
        for k, (px, py) in enumerate(peers):
            for t in range(2):
                d2d(k, t, 2 * px + py, 1 - c).wait_recv()
        for cp in sends:
            cp.wait_send()

    vm = pl.BlockSpec(memory_space=pltpu.VMEM)
    dma = pltpu.SemaphoreType.DMA
    return pl.pallas_call(
        body, name="gather_weights",
        out_shape=(jax.ShapeDtypeStruct((N_CHIPS, d_model, width), BF16),
                   jax.ShapeDtypeStruct((N_CHIPS, rows, d_model), BF16),
                   jax.ShapeDtypeStruct((N_CHIPS, SUBLANES, LANES), F32),
                   *[jax.ShapeDtypeStruct((seq, LANES), F32)] * 3),
        in_specs=[vm, vm, vm, vm], out_specs=(vm,) * 6,
        scratch_shapes=[pltpu.VMEM((d_model, width), BF16), pltpu.VMEM((rows, d_model), BF16),
                        dma((3, 3)), dma((3, 3)), dma((3, 2)), dma((3, 2))],
        compiler_params=_params(),
    )(w_in, w_out, cw, inv_freq)


def _rope(t, cos, s1, s2):
    return t * cos + pltpu.roll(t, LANES - HEAD_DIM // 2, 1) * s1 + pltpu.roll(t, HEAD_DIM // 2, 1) * s2


def _rope_transposed(g, cos, s1, s2):
    return g * cos + pltpu.roll(g * s1, HEAD_DIM // 2, 1) + pltpu.roll(g * s2, LANES - HEAD_DIM // 2, 1)


def inproj(x, g1, w_full, tables, attn_w, conv_w):
    seq, d_model = x.shape
    width = w_full.shape[2]
    tm = ROW_TILE
    groups = _lane_groups(attn_w)

    def body(x_ref, g_ref, w_ref, cos_ref, s1_ref, s2_ref,
             ht_ref, q_ref, k_ref, v_ref, qp_ref, kp_ref, vp_ref, ga_ref, cz_ref, stage):
        xv = x_ref[...]
        hb = ((xv * lax.rsqrt(jnp.mean(xv * xv, axis=-1, keepdims=True) + NORM_EPS)) * g_ref[...]).astype(BF16)
        ht_ref[...] = jnp.transpose(hb)
        cos, s1, s2 = cos_ref[...], s1_ref[...], s2_ref[...]

        def proj(a, b):
            parts = [jnp.dot(hb, w_ref[j, :, lo:hi], preferred_element_type=F32) for j, lo, hi in _col_pieces(a, b, width)]
            return parts[0] if len(parts) == 1 else jnp.concatenate(parts, axis=1)

        def emit(z, nat_ref, perm_ref, fn):
            for g, sl in enumerate(groups):
                val = fn(z[:, sl])
                nat_ref[:, sl] = val.astype(BF16)
                _stage_put(stage, g, val)
            for g, sl in enumerate(groups):
                _to_perm(stage, g, perm_ref, sl, BF16)

        emit(proj(0, attn_w), q_ref, qp_ref, lambda t: _rope(t, cos, s1, s2) * ATTN_SCALE)
        emit(proj(attn_w, 2 * attn_w), k_ref, kp_ref, lambda t: _rope(t, cos, s1, s2))
        emit(proj(2 * attn_w, 3 * attn_w), v_ref, vp_ref, lambda t: t)
        ga_ref[...] = proj(3 * attn_w, 4 * attn_w)
        cz_ref[...] = proj(4 * attn_w, 4 * attn_w + 4 * conv_w)

    row = lambda n: pl.BlockSpec((tm, n), lambda i: (i, 0))
    nat = jax.ShapeDtypeStruct((seq, attn_w), BF16)
    perm = jax.ShapeDtypeStruct(_perm_shape(seq, attn_w), BF16)
    return pl.pallas_call(
        body, name="inproj", grid=(seq // tm,),
        out_shape=(jax.ShapeDtypeStruct((d_model, seq), BF16), nat, nat, nat, perm, perm, perm,
                   jax.ShapeDtypeStruct((seq, attn_w), F32), jax.ShapeDtypeStruct((seq, 4 * conv_w), F32)),
        in_specs=[row(d_model), _const_spec((1, d_model)), _const_spec(w_full.shape), row(LANES), row(LANES), row(LANES)],
        out_specs=(pl.BlockSpec((d_model, tm), lambda i: (0, i)), row(attn_w), row(attn_w), row(attn_w),
                   _perm_tile_spec(attn_w, tm), _perm_tile_spec(attn_w, tm), _perm_tile_spec(attn_w, tm),
                   row(attn_w), row(4 * conv_w)),
        scratch_shapes=[pltpu.VMEM(_stage_shape(len(groups), tm), F32)],
        compiler_params=_params(("arbitrary",)),
    )(x, g1, w_full, *tables)


class _Mode:
    def __init__(self, name, seq):
        self.name = name
        if name == "nat":
            self.residues, blocks = 1, seq // BLOCK
        elif name == "p16":
            self.residues, blocks = PERM, seq // PERM // BLOCK
        else:
            self.residues, blocks = PJ, seq // PERM // P4_ROWS
        self.qb = blocks if blocks <= MAX_QUERY_BLOCKS else (2 if blocks % 2 == 0 else 1)
        self.steps = blocks // self.qb

    def _spec(self, blocks, width, index):
        if self.name == "nat":
            return pl.BlockSpec((blocks * BLOCK, width), lambda r, n: (index(n), 0))
        if self.name == "p16":
            return pl.BlockSpec((1, 1, blocks * BLOCK, width), lambda r, n: (r // PJ, r % PJ, index(n), 0))
        return pl.BlockSpec((PJ, 1, blocks * P4_ROWS, width), lambda r, n: (0, r, index(n), 0))

    def wide(self, width, last=None):
        return self._spec(self.qb, width, (lambda n: n) if last is None else (lambda n: jnp.minimum(n, last)))

    def wide_before(self, width):
        return self._spec(self.qb, width, lambda n: jnp.maximum(n - 1, 0))

    def block_before(self, width, last=None):
        step = (lambda n: n) if last is None else (lambda n: jnp.minimum(n, last))
        return self._spec(1, width, lambda n: jnp.maximum(self.qb * step(n) - 1, 0))

    def get(self, ref, sl, sub=0):
        if self.name == "nat":
            return ref[sub * BLOCK:(sub + 1) * BLOCK, sl]
        if self.name == "p16":
            return ref[0, 0, sub * BLOCK:(sub + 1) * BLOCK, sl]
        return jnp.concatenate([ref[j, 0, sub * P4_ROWS:(sub + 1) * P4_ROWS, sl] for j in range(PJ)], axis=0)

    def put(self, ref, sl, val, sub=0):
        val = val.astype(ref.dtype)
        if self.name == "nat":
            ref[sub * BLOCK:(sub + 1) * BLOCK, sl] = val
        elif self.name == "p16":
            ref[0, 0, sub * BLOCK:(sub + 1) * BLOCK, sl] = val
        else:
            for j in range(PJ):
                ref[j, 0, sub * P4_ROWS:(sub + 1) * P4_ROWS, sl] = val[j * P4_ROWS:(j + 1) * P4_ROWS]

    def keys(self, before_ref, wide_ref, sl, sub):
        older = self.get(before_ref, sl) if sub == 0 else self.get(wide_ref, sl, sub - 1)
        return jnp.concatenate([older, self.get(wide_ref, sl, sub)], axis=0)

    def index(self, idx, is_key):
        if self.name != "p4":
            return idx - BLOCK if is_key else idx
        within = jnp.bitwise_and(idx, BLOCK - 1)
        m = PJ * jnp.bitwise_and(within, P4_ROWS - 1) + jnp.right_shift(within, P4_ROWS.bit_length() - 1)
        return m + BLOCK * (jnp.right_shift(idx, BLOCK.bit_length() - 1) - 1) if is_key else m

    def bias(self, has_before):
        shape = (2 * BLOCK, BLOCK)
        kidx = lax.broadcasted_iota(jnp.int32, shape, 0)
        qidx = lax.broadcasted_iota(jnp.int32, shape, 1)
        rel = self.index(qidx, False) - self.index(kidx, True)
        valid = (rel >= 0) & (rel <= WINDOW_KEYS)
        if has_before is not True:
            valid = valid & ((kidx >= BLOCK) | has_before)
        one = jnp.where(valid, 0.0, NEG)
        return jnp.concatenate([one, one], axis=1)


def _head_masks():
    lane = lax.broadcasted_iota(jnp.int32, (BLOCK, LANES), 1)
    lo = lane < HEAD_DIM
    return lane, lo, jnp.where(lo, 1.0, 0.0).astype(BF16), jnp.where(lo, 0.0, 1.0).astype(BF16)


def _column(blk, lane, h):
    return jnp.sum(jnp.where(lane == h, blk, 0.0), axis=1, keepdims=True)


def attn_fwd(name, q, k, v, run):
    nat = name == "nat"
    seq = q.shape[0] if nat else q.shape[2] * PERM
    attn_w = q.shape[-1]
    mode = _Mode(name, seq)
    groups = _lane_groups(attn_w)
    first = run is None
    all_lanes = slice(0, LANES)

    def body(*refs):
        q_ref, kp_ref, kc_ref, vp_ref, vc_ref = refs[:5]
        if first:
            o_ref, l_ref = refs[5:]
        elif nat:
            oin_ref, lin_ref, ex_ref, o_ref, l_ref, ostage, lstage = refs[5:]
        else:
            oin_ref, lin_ref, ex_ref, o_ref, l_ref = refs[5:]
        n = pl.program_id(1)
        subs = range(mode.qb)
        biases = [mode.bias(n > 0)] + [mode.bias(True)] * (mode.qb - 1)
        _, lo, m_lo, m_hi = _head_masks()
        head_row = lax.broadcasted_iota(jnp.int32, (BLOCK, LANES), 0)
        ones = jnp.ones((2 * BLOCK, LANES), BF16)
        lrows = [jnp.zeros((BLOCK, LANES), F32) for _ in subs]

        def probs(sub, sl):
            q2 = mode.get(q_ref, sl, sub)
            kcat = mode.keys(kp_ref, kc_ref, sl, sub)
            vcat = mode.keys(vp_ref, vc_ref, sl, sub)
            qq = jnp.concatenate([q2 * m_lo, q2 * m_hi], axis=0)
            s_t = _nt(kcat, qq) + biases[sub]
            m = jnp.max(s_t, axis=0, keepdims=True)
            pe = jnp.exp(s_t - m)
            l = jnp.sum(pe, axis=0, keepdims=True)
            return jnp.concatenate([vcat, ones], axis=1), pe.astype(BF16), m + jnp.log(l)

        def output(sub, p, sl, vext, pb, lse):
            o_ext = _tn(pb, vext)
            o_new = o_ext[:, :LANES] / o_ext[:, LANES:]
            mode.put(o_ref, sl, jnp.where(lo, o_new[:BLOCK], o_new[BLOCK:]), sub)
            rows = jnp.where(head_row == 2 * p, lse[:, :BLOCK], lrows[sub])
            lrows[sub] = jnp.where(head_row == 2 * p + 1, lse[:, BLOCK:], rows)

        pending = None
        for sub in subs:
            for p, sl in enumerate(groups):
                nxt = probs(sub, sl)
                if pending is not None:
                    output(*pending)
                pending = (sub, p, sl, *nxt)
        output(*pending)
        if not first and nat:
            for g, sl in enumerate(groups):
                _from_perm(oin_ref, sl, ostage, g)
            _from_perm(lin_ref, all_lanes, lstage, 0)
        for sub in subs:
            rows = slice(sub * BLOCK, (sub + 1) * BLOCK)
            lblk = jnp.transpose(lrows[sub])
            if first:
                mode.put(l_ref, all_lanes, lblk, sub)
                continue
            lin = _stage_get(lstage, 0)[rows] if nat else mode.get(lin_ref, all_lanes, sub)
            mx = jnp.maximum(lin, lblk)
            new = mx + jnp.log(jnp.exp(lin - mx) + jnp.exp(lblk - mx))
            mode.put(l_ref, all_lanes, new, sub)

            def expand(w):
                hi = w.astype(BF16)
                rest = (w - hi.astype(F32)).astype(BF16)
                ex = ex_ref[...]
                return jnp.dot(hi, ex, preferred_element_type=F32) + jnp.dot(rest, ex, preferred_element_type=F32)

            w_prev, w_cur = expand(jnp.exp(lin - new)), expand(jnp.exp(lblk - new))
            for p, sl in enumerate(groups):
                o_prev = _stage_get(ostage, p)[rows] if nat else mode.get(oin_ref, sl, sub)
                mode.put(o_ref, sl, w_prev[:, sl] * o_prev + w_cur[:, sl] * mode.get(o_ref, sl, sub), sub)

    ins = [q, k, k, v, v]
    specs = [mode.wide(attn_w), mode.block_before(attn_w), mode.wide(attn_w), mode.block_before(attn_w), mode.wide(attn_w)]
    scratch = []
    if not first:
        ins += list(run)
        if nat:
            rows_a = mode.qb * BLOCK // PERM
            specs += [pl.BlockSpec((PJ, PJ, rows_a, attn_w), lambda r, n: (0, 0, n, 0)),
                      pl.BlockSpec((PJ, PJ, rows_a, LANES), lambda r, n: (0, 0, n, 0))]
            scratch = [pltpu.VMEM(_stage_shape(len(groups), mode.qb * BLOCK), F32),
                       pltpu.VMEM(_stage_shape(1, mode.qb * BLOCK), F32)]
        else:
            specs += [mode.wide(attn_w), mode.wide(LANES)]
        head_of_lane = jnp.arange(attn_w, dtype=jnp.int32) // HEAD_DIM
        ins.append((jnp.arange(LANES, dtype=jnp.int32)[:, None] == head_of_lane[None, :]).astype(BF16))
        specs.append(_const_spec((LANES, attn_w)))
    if nat:
        out_shape = (jax.ShapeDtypeStruct((seq, attn_w), F32), jax.ShapeDtypeStruct((seq, LANES), F32))
    else:
        out_shape = (jax.ShapeDtypeStruct(_perm_shape(seq, attn_w), F32), jax.ShapeDtypeStruct(_perm_shape(seq, LANES), F32))
    return pl.pallas_call(
        body, name=f"attn_fwd_{name}", grid=(mode.residues, mode.steps),
        out_shape=out_shape, in_specs=specs, out_specs=(mode.wide(attn_w), mode.wide(LANES)),
        scratch_shapes=scratch,
        compiler_params=_params(("arbitrary", "arbitrary")),
    )(*ins)


def attn_bwd(name, q, k, v, d_o, lse, delta, run):
    nat = name == "nat"
    seq = q.shape[0] if nat else q.shape[2] * PERM
    attn_w = q.shape[-1]
    mode = _Mode(name, seq)
    steps, qb = mode.steps, mode.qb
    single = steps == 1
    groups = _lane_groups(attn_w)
    first = run is None
    all_lanes = slice(0, LANES)

    def body(*refs):
        q_ref, kp_ref, kc_ref, vp_ref, vc_ref, do_ref, lse_ref, dl_ref = refs[:8]
        if first:
            dq_ref, dk_ref, dv_ref, ck, cv = refs[8:]
        else:
            dqi_ref, dki_ref, dvi_ref, dq_ref, dk_ref, dv_ref, ck, cv = refs[8:]
        n = pl.program_id(1)
        carries = ((ck, dk_ref, None if first else dki_ref), (cv, dv_ref, None if first else dvi_ref))

        def emit(out_ref, acc_ref, sl, sub, val):
            if acc_ref is not None:
                val = val + mode.get(acc_ref, sl, sub).astype(F32)
            mode.put(out_ref, sl, val, sub)

        if not single:
            @pl.when(n == 0)
            def _():
                ck[...] = jnp.zeros_like(ck)
                cv[...] = jnp.zeros_like(cv)

        @pl.when(n < steps)
        def _():
            biases = [mode.bias(n > 0)] + [mode.bias(True)] * (qb - 1)
            _, lo, m_lo, m_hi = _head_masks()

            def scores(sub, p, sl, lse_t, dl_t):
                q2, do2 = mode.get(q_ref, sl, sub), mode.get(do_ref, sl, sub)
                kcat = mode.keys(kp_ref, kc_ref, sl, sub)
                vcat = mode.keys(vp_ref, vc_ref, sl, sub)
                qq = jnp.concatenate([q2 * m_lo, q2 * m_hi], axis=0)
                dd = jnp.concatenate([do2 * m_lo, do2 * m_hi], axis=0)
                h0 = 2 * p
                lse2 = jnp.concatenate([lse_t[h0:h0 + 1, :], lse_t[h0 + 1:h0 + 2, :]], axis=1)
                dl2 = jnp.concatenate([dl_t[h0:h0 + 1, :], dl_t[h0 + 1:h0 + 2, :]], axis=1)
                p_t = jnp.exp(_nt(kcat, qq) + (biases[sub] - lse2))
                ds_t = p_t * (_nt(vcat, dd) - dl2)
                return qq, dd, kcat, p_t.astype(BF16), ds_t.astype(BF16)

            def grads(sub, sl, qq, dd, kcat, pb, dsb):
                dqb = _tn(dsb, kcat)
                dq2 = jnp.where(lo, dqb[:BLOCK], dqb[BLOCK:]) * ATTN_SCALE
                if not first:
                    dq2 = dq2 + mode.get(dqi_ref, sl, sub).astype(F32)
                mode.put(dq_ref, sl, dq2, sub)
                for (carry, out_ref, acc_ref), lhs, rhs in zip(carries, (dsb, pb), (qq, dd)):
                    both = jnp.dot(lhs, rhs, preferred_element_type=F32)
                    if sub == 0:
                        if not single:
                            for s in range(qb - 1):
                                emit(out_ref, acc_ref, sl, s, carry[s, :, sl])
                            emit(out_ref, acc_ref, sl, qb - 1, carry[qb - 1, :, sl] + both[:BLOCK])
                        carry[0, :, sl] = both[BLOCK:]
                    else:
                        carry[sub - 1, :, sl] += both[:BLOCK]
                        carry[sub, :, sl] = both[BLOCK:]
                    if single and sub == qb - 1:
                        for s in range(qb):
                            emit(out_ref, acc_ref, sl, s, carry[s, :, sl])

            stats = [(jnp.transpose(mode.get(lse_ref, all_lanes, sub)),
                      jnp.transpose(mode.get(dl_ref, all_lanes, sub))) for sub in range(qb)]
            pending = None
            for p, sl in enumerate(groups):
                for sub in range(qb):
                    nxt = scores(sub, p, sl, *stats[sub])
                    if pending is not None:
                        grads(*pending)
                    pending = (sub, sl, *nxt)
            grads(*pending)

        if not single:
            @pl.when(n == steps)
            def _():
                for carry, out_ref, acc_ref in carries:
                    for sl in groups:
                        for s in range(qb):
                            emit(out_ref, acc_ref, sl, s, carry[s, :, sl])

    last = steps - 1
    wide = lambda w: mode.wide(w, last)
    ins = [q, k, k, v, v, d_o, lse, delta]
    specs = [wide(attn_w), mode.block_before(attn_w, last), wide(attn_w), mode.block_before(attn_w, last), wide(attn_w),
             wide(attn_w), wide(LANES), wide(LANES)]
    if not first:
        ins += list(run)
        specs += [wide(attn_w), mode.wide_before(attn_w), mode.wide_before(attn_w)]
    shp = jax.ShapeDtypeStruct((seq, attn_w) if nat else _perm_shape(seq, attn_w), BF16)
    return pl.pallas_call(
        body, name=f"attn_bwd_{name}", grid=(mode.residues, steps if single else steps + 1),
        out_shape=(shp, shp, shp), in_specs=specs,
        out_specs=(wide(attn_w), mode.wide_before(attn_w), mode.wide_before(attn_w)),
        scratch_shapes=[pltpu.VMEM((qb, BLOCK, attn_w), F32), pltpu.VMEM((qb, BLOCK, attn_w), F32)],
        compiler_params=_params(("arbitrary", "arbitrary")),
    )(*ins)


def _shift_down(u, halo, k):
    rolled = pltpu.roll(u, k, 0)
    row = lax.broadcasted_iota(jnp.int32, halo.shape, 0)
    top = jnp.where(row < k, pltpu.roll(halo, k, 0), rolled[:SUBLANES])
    return jnp.concatenate([top, rolled[SUBLANES:]], axis=0)


def _shift_up(u, halo, k):
    rows = u.shape[0]
    rolled = pltpu.roll(u, rows - k, 0)
    row = lax.broadcasted_iota(jnp.int32, halo.shape, 0)
    bot = jnp.where(row >= SUBLANES - k, pltpu.roll(halo, SUBLANES - k, 0), rolled[rows - SUBLANES:])
    return jnp.concatenate([rolled[:rows - SUBLANES], bot], axis=0)


def tail(o, lse, ga, cz, x, tgt, w_out, g2, cw):
    seq, d_model = x.shape
    attn_w = o.shape[1]
    conv_w = cz.shape[1] // 4
    mix = attn_w + conv_w
    groups = _lane_groups(attn_w)
    tm = ROW_TILE
    nt = seq // tm
    hb = tm // SUBLANES

    def body(o_ref, l_ref, ga_ref, cz_ref, hz_ref, x_ref, t_ref, w_ref, g_ref, cw_ref,
             do_ref, dl_ref, dop_ref, dlp_ref, lp_ref, dga_ref, dcb_ref, dgc_ref, dcv_ref, e_ref,
             dw_ref, dg_ref, dcw_ref, loss_ref, stage):
        i = pl.program_id(0)

        @pl.when(i == 0)
        def _():
            dw_ref[...] = jnp.zeros_like(dw_ref)
            dg_ref[...] = jnp.zeros_like(dg_ref)
            dcw_ref[...] = jnp.zeros_like(dcw_ref)
            loss_ref[...] = jnp.zeros_like(loss_ref)

        u = cz_ref[:, 2 * conv_w:3 * conv_w] * cz_ref[:, 0:conv_w]
        uh = hz_ref[:, 2 * conv_w:3 * conv_w] * hz_ref[:, 0:conv_w]
        uh = jnp.where(i > 0, uh, 0.0)
        u1 = _shift_down(u, uh, 1)
        u2 = _shift_down(u, uh, 2)
        w0, w1, w2 = cw_ref[0:1, :], cw_ref[1:2, :], cw_ref[2:3, :]
        cvv = u2 * w0 + u1 * w1 + u * w2
        gv = g_ref[...]
        all_lanes = slice(0, LANES)

        def forward(rs):
            ov, gav = o_ref[rs, :], ga_ref[rs, :]
            sig_a = _sigmoid(gav)
            silu_a = gav * sig_a
            cb, gc = cz_ref[rs, conv_w:2 * conv_w], cz_ref[rs, 3 * conv_w:4 * conv_w]
            sig_c = _sigmoid(gc)
            silu_c = gc * sig_c
            bc = cb * cvv[rs]
            mixed = jnp.concatenate([ov * silu_a, bc * silu_c], axis=1).astype(BF16)
            yv = jnp.dot(mixed, w_ref[...], preferred_element_type=F32)
            return ov, gav, sig_a, silu_a, cb, gc, sig_c, silu_c, bc, mixed, yv

        def loss_and_dy(rs, mixed, yv):
            r2 = lax.rsqrt(jnp.mean(yv * yv, axis=-1, keepdims=True) + NORM_EPS)
            yhat = yv * r2
            diff = (x_ref[rs, :] + yhat * gv) - t_ref[rs, :]
            loss_ref[...] += _rowgroup_sum(diff * diff)
            ev = diff * (1.0 / d_model)
            e_ref[rs, :] = ev
            dg_ref[...] += _rowgroup_sum(ev * yhat)
            eg = ev * gv
            dy = (r2 * (eg - yhat * jnp.mean(eg * yhat, axis=-1, keepdims=True))).astype(BF16)
            dw_ref[...] += _tn(mixed, dy)
            return _nt(dy, w_ref[...])

        def backward(rs, ov, gav, sig_a, silu_a, cb, gc, sig_c, silu_c, bc, dm):
            rows = rs.stop - rs.start
            dma, dmc = dm[:, :attn_w], dm[:, attn_w:]
            dov = dma * silu_a
            do_ref[rs, :] = dov.astype(BF16)
            dga_ref[rs, :] = (dma * ov * (sig_a * (1.0 + gav * (1.0 - sig_a)))).astype(BF16)
            prod = dov * ov
            lane = lax.broadcasted_iota(jnp.int32, (rows, LANES), 1)
            lo = lane < HEAD_DIM
            dblk = jnp.zeros((rows, LANES), F32)
            for p, sl in enumerate(groups):
                pr = prod[:, sl]
                dblk = jnp.where(lane == 2 * p, jnp.sum(jnp.where(lo, pr, 0.0), axis=1, keepdims=True), dblk)
                dblk = jnp.where(lane == 2 * p + 1, jnp.sum(jnp.where(lo, 0.0, pr), axis=1, keepdims=True), dblk)
                _stage_put(stage, p, dov[:, sl], rs.start)
            dl_ref[rs, :] = dblk
            _stage_put(stage, len(groups), dblk, rs.start)
            _stage_put(stage, len(groups) + 1, l_ref[rs, :], rs.start)
            dsc = dmc * silu_c
            cv_rows = cvv[rs]
            dcb_ref[rs, :] = (dsc * cv_rows).astype(BF16)
            dgc_ref[rs, :] = (dmc * bc * (sig_c * (1.0 + gc * (1.0 - sig_c)))).astype(BF16)
            dcv = dsc * cb
            dcv_ref[rs, :] = dcv
            dcw_ref[0:SUBLANES, :] += _rowgroup_sum(dcv * u2[rs])
            dcw_ref[SUBLANES:2 * SUBLANES, :] += _rowgroup_sum(dcv * u1[rs])
            dcw_ref[2 * SUBLANES:3 * SUBLANES, :] += _rowgroup_sum(dcv * u[rs])

        halves = [slice(0, tm // 2), slice(tm // 2, tm)]
        fwd = [forward(rs) for rs in halves]
        dms = [loss_and_dy(rs, f[9], f[10]) for rs, f in zip(halves, fwd)]
        for rs, f, dm in zip(halves, fwd, dms):
            backward(rs, *f[:9], dm)
        for p, sl in enumerate(groups):
            _to_perm(stage, p, dop_ref, sl, BF16)
        _to_perm(stage, len(groups), dlp_ref, all_lanes, F32)
        _to_perm(stage, len(groups) + 1, lp_ref, all_lanes, F32)

    row = lambda n: pl.BlockSpec((tm, n), lambda i: (i, 0))
    whole = lambda a, b: pl.BlockSpec((a, b), lambda i: (0, 0))
    return pl.pallas_call(
        body, name="tail", grid=(nt,),
        out_shape=(jax.ShapeDtypeStruct((seq, attn_w), BF16), jax.ShapeDtypeStruct((seq, LANES), F32),
                   jax.ShapeDtypeStruct(_perm_shape(seq, attn_w), BF16), jax.ShapeDtypeStruct(_perm_shape(seq, LANES), F32),
                   jax.ShapeDtypeStruct(_perm_shape(seq, LANES), F32),
                   jax.ShapeDtypeStruct((seq, attn_w), BF16), jax.ShapeDtypeStruct((seq, conv_w), BF16),
                   jax.ShapeDtypeStruct((seq, conv_w), BF16), jax.ShapeDtypeStruct((seq, conv_w), F32),
                   jax.ShapeDtypeStruct((seq, d_model), F32), jax.ShapeDtypeStruct((mix, d_model), F32),
                   jax.ShapeDtypeStruct((SUBLANES, d_model), F32), jax.ShapeDtypeStruct((CONV_K * SUBLANES, conv_w), F32),
                   jax.ShapeDtypeStruct((SUBLANES, d_model), F32)),
        in_specs=[row(attn_w), row(LANES), row(attn_w), row(4 * conv_w),
                  pl.BlockSpec((SUBLANES, 4 * conv_w), lambda i: (jnp.maximum(i * hb - 1, 0), 0)),
                  row(d_model), row(d_model), _const_spec((mix, d_model)), _const_spec((1, d_model)),
                  _const_spec((SUBLANES, conv_w))],
        out_specs=(row(attn_w), row(LANES), _perm_tile_spec(attn_w, tm), _perm_tile_spec(LANES, tm), _perm_tile_spec(LANES, tm),
                   row(attn_w), row(conv_w), row(conv_w), row(conv_w), row(d_model),
                   whole(mix, d_model), whole(SUBLANES, d_model), whole(CONV_K * SUBLANES, conv_w),
                   whole(SUBLANES, d_model)),
        scratch_shapes=[pltpu.VMEM(_stage_shape(len(groups) + 2, tm), F32)],
        compiler_params=_params(("arbitrary",)),
    )(o, lse, ga, cz, cz, x, tgt, w_out, g2, cw)


def dz_dx(nat_grads, perm_grads, dga, dcb, dgc, dcv, cz, tables, x, g1, e, w_full, cw):
    seq, d_model = x.shape
    attn_w = dga.shape[1]
    conv_w = dcv.shape[1]
    width = w_full.shape[2]
    in_w = 4 * attn_w + 4 * conv_w
    groups = _lane_groups(attn_w)
    tm = DZ_ROW_TILE
    nt = seq // tm
    hb = tm // SUBLANES

    def body(dq_ref, dk_ref, dv_ref, dqp_ref, dkp_ref, dvp_ref, dga_ref, dcb_ref, dgc_ref, dcv_ref, nh_ref, cz_ref,
             cos_ref, s1_ref, s2_ref, x_ref, g_ref, e_ref, w_ref, cw_ref, gx_ref, dz_ref, dg_ref, stage):
        i = pl.program_id(0)

        @pl.when(i == 0)
        def _():
            dg_ref[...] = jnp.zeros_like(dg_ref)

        cos, s1, s2 = cos_ref[...], s1_ref[...], s2_ref[...]
        for t, (nat_ref, perm_ref) in enumerate(((dq_ref, dqp_ref), (dk_ref, dkp_ref), (dv_ref, dvp_ref))):
            for g, sl in enumerate(groups):
                _from_perm(perm_ref, sl, stage, g)
            for g, sl in enumerate(groups):
                tot = nat_ref[:, sl].astype(F32) + _stage_get(stage, g)
                if t < 2:
                    tot = _rope_transposed(tot, cos, s1, s2)
                dz_ref[:, t * attn_w + g * LANES:t * attn_w + (g + 1) * LANES] = tot.astype(BF16)
        dz_ref[:, 3 * attn_w:4 * attn_w] = dga_ref[...]
        dcv = dcv_ref[...]
        nh = jnp.where(i < nt - 1, nh_ref[...], 0.0)
        w0, w1, w2 = cw_ref[0:1, :], cw_ref[1:2, :], cw_ref[2:3, :]
        du = dcv * w2 + _shift_up(dcv, nh, 1) * w1 + _shift_up(dcv, nh, 2) * w0
        base = 4 * attn_w
        dz_ref[:, base:base + conv_w] = (du * cz_ref[:, 2 * conv_w:3 * conv_w]).astype(BF16)
        dz_ref[:, base + conv_w:base + 2 * conv_w] = dcb_ref[...]
        dz_ref[:, base + 2 * conv_w:base + 3 * conv_w] = (du * cz_ref[:, 0:conv_w]).astype(BF16)
        dz_ref[:, base + 3 * conv_w:base + 4 * conv_w] = dgc_ref[...]

        dh = _nt(dz_ref[:, 0:width], w_ref[0])
        for j in range(1, N_CHIPS):
            dh = dh + _nt(dz_ref[:, j * width:(j + 1) * width], w_ref[j])
        xv = x_ref[...]
        r1 = lax.rsqrt(jnp.mean(xv * xv, axis=-1, keepdims=True) + NORM_EPS)
        xhat = xv * r1
        dg_ref[...] += _rowgroup_sum(dh * xhat)
        dhg = dh * g_ref[...]
        gx_ref[...] = r1 * (dhg - xhat * jnp.mean(dhg * xhat, axis=-1, keepdims=True)) + e_ref[...]

    row = lambda n: pl.BlockSpec((tm, n), lambda i: (i, 0))
    whole = lambda a, b: pl.BlockSpec((a, b), lambda i: (0, 0))
    pt = _perm_tile_spec(attn_w, tm)
    return pl.pallas_call(
        body, name="dz_dx", grid=(nt,),
        out_shape=(jax.ShapeDtypeStruct((seq, d_model), F32), jax.ShapeDtypeStruct((seq, in_w), BF16),
                   jax.ShapeDtypeStruct((SUBLANES, d_model), F32)),
        in_specs=[row(attn_w), row(attn_w), row(attn_w), pt, pt, pt, row(attn_w), row(conv_w), row(conv_w), row(conv_w),
                  pl.BlockSpec((SUBLANES, conv_w), lambda i: (jnp.minimum((i + 1) * hb, seq // SUBLANES - 1), 0)),
                  row(4 * conv_w), row(LANES), row(LANES), row(LANES), row(d_model), _const_spec((1, d_model)), row(d_model),
                  _const_spec(w_full.shape), _const_spec((SUBLANES, conv_w))],
        out_specs=(row(d_model), row(in_w), whole(SUBLANES, d_model)),
        scratch_shapes=[pltpu.VMEM(_stage_shape(len(groups), tm), F32)],
        compiler_params=_params(("arbitrary",)),
    )(*nat_grads, *perm_grads, dga, dcb, dgc, dcv, dcv, cz, *tables, x, g1, e, w_full, cw)


def dw_in_reduce(ht, dz):
    d_model, seq = ht.shape
    half = dz.shape[1] // N_DEV
    ts = min(2048, seq)
    steps = seq // ts
    x, y, c = lax.axis_index("x"), lax.axis_index("y"), lax.axis_index("c")
    far_first = lambda x, y: [(1 - x, 1 - y), (1 - x, y), (x, 1 - y)]
    chips = jnp.stack([2 * px + py for px, py in far_first(x, y)] + [2 * x + y]).astype(jnp.int32)
    order = jnp.stack([2 * chips + (1 - c), 2 * chips + c], axis=1).reshape(N_DEV)

    def body(order_ref, ht_ref, dz_ref, out_ref, acc, theirs, staged, contrib, resbuf, out_sem, sa, ra, sb, rb, sc, rc):
        del order_ref
        p, s = pl.program_id(0), pl.program_id(1)
        x, y, c = lax.axis_index("x"), lax.axis_index("y"), lax.axis_index("c")
        sib = (x, y, 1 - c)
        peers = far_first(x, y)
        slot = p % 2

        def a_copy(k):
            return pltpu.make_async_remote_copy(src_ref=acc.at[0], dst_ref=theirs.at[k], send_sem=sa.at[k], recv_sem=ra.at[k],
                                                device_id=sib, device_id_type=MESH)

        def b_copy(k):
            px, py = peers[k]
            return pltpu.make_async_remote_copy(src_ref=staged.at[k], dst_ref=contrib.at[k], send_sem=sb.at[k], recv_sem=rb.at[k],
                                                device_id=(px, py, c), device_id_type=MESH)

        def c_copy(which):
            return pltpu.make_async_remote_copy(src_ref=resbuf.at[which], dst_ref=resbuf.at[which], send_sem=sc, recv_sem=rc,
                                                device_id=sib, device_id_type=MESH)

        @pl.when(s == 0)
        def _():
            for k in range(N_CHIPS - 1):
                @pl.when(p == 2 * k + 2)
                def _():
                    a_copy(k).wait_send()
            acc[slot] = jnp.zeros((d_model, half), F32)

        acc[slot] += jnp.dot(ht_ref[...], dz_ref[...], preferred_element_type=F32)

        @pl.when(s == steps - 1)
        def _():
            for k in range(N_CHIPS):
                @pl.when(p == 2 * k)
                def _():
                    a_copy(k).start()
            for k in range(N_CHIPS - 1):
                @pl.when(p == 2 * k + 1)
                def _():
                    a_copy(k).wait_recv()
                    staged[k] = (acc[1] + theirs[k]).astype(BF16)
                    b_copy(k).start()

            @pl.when(p == N_DEV - 1)
            def _():
                a_copy(N_CHIPS - 1).wait_recv()
                tot = acc[1] + theirs[N_CHIPS - 1]
                for k in range(N_CHIPS - 1):
                    b_copy(k).wait_recv()
                    tot = tot + contrib[k].astype(F32)
                resbuf[c] = tot
                c_copy(c).start()
                c_copy(1 - c).wait_recv()
                done = pltpu.make_async_copy(resbuf, out_ref, out_sem)
                done.start()
                a_copy(N_CHIPS - 1).wait_send()
                for k in range(N_CHIPS - 1):
                    b_copy(k).wait_send()
                c_copy(c).wait_send()
                done.wait()

    dma = pltpu.SemaphoreType.DMA
    grid_spec = pltpu.PrefetchScalarGridSpec(
        num_scalar_prefetch=1, grid=(N_DEV, steps),
        in_specs=[pl.BlockSpec((d_model, ts), lambda p, s, order_ref: (0, s)),
                  pl.BlockSpec((ts, half), lambda p, s, order_ref: (s, order_ref[p]))],
        out_specs=pl.BlockSpec(memory_space=pl.ANY),
        scratch_shapes=[pltpu.VMEM((2, d_model, half), F32), pltpu.VMEM((N_CHIPS, d_model, half), F32),
                        pltpu.VMEM((N_CHIPS - 1, d_model, half), BF16), pltpu.VMEM((N_CHIPS - 1, d_model, half), BF16),
                        pltpu.VMEM((2, d_model, half), F32), dma,
                        dma((N_CHIPS,)), dma((N_CHIPS,)), dma((N_CHIPS - 1,)), dma((N_CHIPS - 1,)), dma, dma])
    return pl.pallas_call(
        body, name="dw_in_reduce", grid_spec=grid_spec,
        out_shape=jax.ShapeDtypeStruct((2, d_model, half), F32),
        compiler_params=_params(("arbitrary", "arbitrary")),
    )(order, ht, dz)


def grad_reduce(tensors, small):
    nt = len(tensors)
    split = [g.reshape(N_CHIPS, 2, *g.shape[1:]) for g in tensors]
    shapes = [g.shape[2:] for g in split]

    def body(*refs):
        srcs, sm_ref = refs[:nt], refs[nt]
        res, rs_ref = refs[nt + 1:2 * nt + 1], refs[2 * nt + 1]
        scratch = refs[2 * nt + 2:]
        mine, theirs, staged, contrib = (scratch[k * nt:(k + 1) * nt] for k in range(4))
        sbuf, loc_sems, sa, ra, sb, rb, sc, rc, ss, rs = scratch[4 * nt:]
        x, y, c = lax.axis_index("x"), lax.axis_index("y"), lax.axis_index("c")
        me = 2 * x + y
        sib = (x, y, 1 - c)

        flips = [(fx, fy, fc) for fx in (0, 1) for fy in (0, 1) for fc in (0, 1)][1:]
        my8 = 4 * x + 2 * y + c
        sbuf[my8] = sm_ref[...]

        def small_copy(k, slot, to):
            return pltpu.make_async_remote_copy(src_ref=sm_ref, dst_ref=sbuf.at[slot], send_sem=ss.at[k], recv_sem=rs.at[k],
                                                device_id=to, device_id_type=MESH)

        sends = []
        for k, (fx, fy, fc) in enumerate(flips):
            px, py, pc = _flip(x, fx), _flip(y, fy), _flip(c, fc)
            sends.append(small_copy(k, my8, (px, py, pc)))
            sends[-1].start()

        def a_copy(t, j):
            return pltpu.make_async_remote_copy(src_ref=srcs[t].at[j, 1 - c], dst_ref=theirs[t].at[j], send_sem=sa.at[t, j],
                                                recv_sem=ra.at[t, j], device_id=sib, device_id_type=MESH)

        peers = _chip_peers(x, y)
        order = [2 * px + py for px, py in peers] + [me]
        loads = [[pltpu.make_async_copy(srcs[t].at[j, c], mine[t].at[j], loc_sems.at[t, j]) for t in range(nt)] for j in order]
        for pos, j in enumerate(order):
            for t in range(nt):
                loads[pos][t].start()
                sends.append(a_copy(t, j))
                sends[-1].start()

        def b_copy(k, t, piece, slot, to):
            return pltpu.make_async_remote_copy(src_ref=staged[t].at[piece], dst_ref=contrib[t].at[slot], send_sem=sb.at[k, t],
                                                recv_sem=rb.at[k, t], device_id=to, device_id_type=MESH)

        for k, (px, py) in enumerate(peers):
            j = 2 * px + py
            for t in range(nt):
                loads[k][t].wait()
                a_copy(t, j).wait_recv()
                staged[t][j] = (mine[t][j] + theirs[t][j]).astype(BF16)
                sends.append(b_copy(k, t, j, me, (px, py, c)))
                sends[-1].start()
        for t in range(nt):
            loads[len(peers)][t].wait()
            a_copy(t, me).wait_recv()
            mine[t][me] = mine[t][me] + theirs[t][me]
            contrib[t][me] = mine[t][me].astype(BF16)
        for k, (px, py) in enumerate(peers):
            for t in range(nt):
                b_copy(k, t, me, 2 * px + py, (px, py, c)).wait_recv()

        def c_copy(t, half):
            return pltpu.make_async_remote_copy(src_ref=res[t].at[half], dst_ref=res[t].at[half], send_sem=sc.at[t],
                                                recv_sem=rc.at[t], device_id=sib, device_id_type=MESH)

        for t in range(nt):
            own = mine[t][me]
            term = lambda j: jnp.where(me == j, own, contrib[t][j].astype(F32))
            res[t][c] = ((term(0) + term(1)) + term(2)) + term(3)
            sends.append(c_copy(t, c))
            sends[-1].start()
        for t in range(nt):
            c_copy(t, 1 - c).wait_recv()

        for k, (fx, fy, fc) in enumerate(flips):
            px, py, pc = _flip(x, fx), _flip(y, fy), _flip(c, fc)
            small_copy(k, 4 * px + 2 * py + pc, (px, py, pc)).wait_recv()
        tot = sbuf[0]
        for d in range(1, N_DEV):
            tot = tot + sbuf[d]
        rs_ref[...] = tot
        for cp in sends:
            cp.wait_send()

    vm = pl.BlockSpec(memory_space=pltpu.VMEM)
    anyspace = pl.BlockSpec(memory_space=pl.ANY)
    dma = pltpu.SemaphoreType.DMA
    bufs = [pltpu.VMEM((N_CHIPS, *shp), dt) for dt in (F32, F32, BF16, BF16) for shp in shapes]
    outs = pl.pallas_call(
        body, name="grad_reduce",
        out_shape=(*[jax.ShapeDtypeStruct((2, *shp), F32) for shp in shapes], jax.ShapeDtypeStruct(small.shape, F32)),
        in_specs=[anyspace] * nt + [vm], out_specs=tuple([vm] * (nt + 1)),
        scratch_shapes=[*bufs, pltpu.VMEM((N_DEV, *small.shape), F32),
                        dma((nt, N_CHIPS)), dma((nt, N_CHIPS)), dma((nt, N_CHIPS)), dma((3, nt)), dma((3, nt)), dma((nt,)), dma((nt,)),
                        dma((N_DEV - 1,)), dma((N_DEV - 1,))],
        compiler_params=_params(),
    )(*split, small)
    return outs[:nt], outs[nt]


def _adam_math(w, g, m, v):
    m = ADAM_B1 * m + (1.0 - ADAM_B1) * g
    v = ADAM_B2 * v + (1.0 - ADAM_B2) * (g * g)
    m_hat = m / (1.0 - ADAM_B1 ** ADAM_STEP)
    v_hat = v / (1.0 - ADAM_B2 ** ADAM_STEP)
    delta = -ADAM_LR * (m_hat / (jnp.sqrt(v_hat) + ADAM_EPS) + ADAM_WD * w)
    return delta, m, v


def adam_shard(name, w, g2, m, v, block, grid, w_map, g_map):
    def body(w_ref, g_ref, m_ref, v_ref, go_ref, d_ref, mo_ref, vo_ref):
        g = g_ref[0]
        delta, mn, vn = _adam_math(w_ref[...], g, m_ref[...], v_ref[...])
        go_ref[...] = g
        d_ref[...] = delta
        mo_ref[...] = mn
        vo_ref[...] = vn

    ws = pl.BlockSpec(block, w_map)
    shp = jax.ShapeDtypeStruct(w.shape, F32)
    return pl.pallas_call(
        body, name=name, grid=grid, out_shape=(shp, shp, shp, shp),
        in_specs=[ws, pl.BlockSpec((1, *block), g_map), ws, ws], out_specs=(ws, ws, ws, ws),
        compiler_params=_params(("arbitrary",) * len(grid)),
    )(w, g2, m, v)


def adam_small(ws, gs, ms, vs):
    n = len(ws)

    def body(*refs):
        ins, outs = refs[:4 * n], refs[4 * n:]
        for t in range(n):
            delta, mn, vn = _adam_math(ins[t][...], ins[n + t][...], ins[2 * n + t][...], ins[3 * n + t][...])
            outs[3 * t][...] = delta
            outs[3 * t + 1][...] = mn
            outs[3 * t + 2][...] = vn

    vm = pl.BlockSpec(memory_space=pltpu.VMEM)
    outs = pl.pallas_call(
        body, name="adam_small",
        out_shape=tuple(jax.ShapeDtypeStruct(w.shape, F32) for w in ws for _ in range(3)),
        in_specs=[vm] * (4 * n), out_specs=tuple([vm] * (3 * n)),
        compiler_params=_params(),
    )(*ws, *gs, *ms, *vs)
    return [outs[3 * t:3 * t + 3] for t in range(n)]


def kernel(x, norm_pre_g, w_in, conv_w, w_out, norm_post_g, loss_target, m_norm_pre_g, m_w_in, m_conv_w, m_w_out, m_norm_post_g, v_norm_pre_g, v_w_in, v_conv_w, v_w_out, v_norm_post_g):
    _, seq, d_model = x.shape
    width = w_in.shape[1]
    conv_q = conv_w.shape[1]
    conv_width = N_CHIPS * conv_q
    attn_width = d_model - conv_width
    xs, tg = x[0], loss_target[0]
    g1, g2 = norm_pre_g.reshape(1, d_model), norm_post_g.reshape(1, d_model)

    w_full, wout_full, cw_full, *tables = gather_weights(w_in, w_out, conv_w, seq)
    wout2 = wout_full.reshape(attn_width + conv_width, d_model)
    cw = jnp.zeros((SUBLANES, conv_width), F32).at[:CONV_K].set(
        cw_full[:, :CONV_K, :conv_q].transpose(1, 0, 2).reshape(CONV_K, conv_width))

    ht, q, k, v, qp, kp, vp, ga, cz = inproj(xs, g1, w_full, tables, attn_width, conv_width)
    run = attn_fwd("p4", qp, kp, vp, None)
    run = attn_fwd("p16", qp, kp, vp, run)
    o, lse = attn_fwd("nat", q, k, v, run)
    (d_o, delta, d_op, delta_p, lse_p, dga, dcb, dgc, dcv, e, dwout, dg2, dcw, loss_acc) = tail(
        o, lse, ga, cz, xs, tg, wout2, g2, cw)
    nat_grads = attn_bwd("nat", q, k, v, d_o, lse, delta, None)
    perm_grads = attn_bwd("p4", qp, kp, vp, d_op, lse_p, delta_p, None)
    perm_grads = attn_bwd("p16", qp, kp, vp, d_op, lse_p, delta_p, perm_grads)
    grad_x, dz, dg1 = dz_dx(nat_grads, perm_grads, dga, dcb, dgc, dcv, cz, tables, xs, g1, e, w_full, cw)

    small = jnp.zeros((SUBLANES, d_model), F32)
    small = small.at[0].set(dg1.sum(axis=0)).at[1].set(dg2.sum(axis=0))
    small = small.at[2:2 + CONV_K, :conv_width].set(dcw.reshape(CONV_K, SUBLANES, conv_width).sum(axis=1))
    small = small.at[2 + CONV_K, 0].set(jnp.sum(loss_acc))
    (rout,), rsmall = grad_reduce([dwout.reshape(N_DEV, -1, d_model)], small)
    rin = dw_in_reduce(ht, dz)

    half = width // 2
    tr = 256
    gw_in, d_in, m_in, v_in = adam_shard(
        "adam_w_in", w_in, rin, m_w_in, v_w_in, (tr, half), (2, d_model // tr),
        lambda hf, i: (i, hf), lambda hf, i: (hf, i, 0))
    rq = w_out.shape[0] // 2
    gw_out, d_out, m_out, v_out = adam_shard(
        "adam_w_out", w_out, rout, m_w_out, v_w_out, (rq, d_model), (2,),
        lambda hf: (hf, 0), lambda hf: (hf, 0, 0))

    chip = 2 * lax.axis_index("x") + lax.axis_index("y")
    g_pre, g_post = rsmall[0:1], rsmall[1:2]
    g_conv = lax.dynamic_slice(rsmall[2:2 + CONV_K, :conv_width], (0, chip * conv_q), (CONV_K, conv_q))
    (d_pre, m_pre, v_pre), (d_post, m_post, v_post), (d_cv, m_cv, v_cv) = adam_small(
        [g1, g2, conv_w], [g_pre, g_post, g_conv],
        [m_norm_pre_g.reshape(1, d_model), m_norm_post_g.reshape(1, d_model), m_conv_w],
        [v_norm_pre_g.reshape(1, d_model), v_norm_post_g.reshape(1, d_model), v_conv_w])

    loss = 0.5 * rsmall[2 + CONV_K, 0] / d_model
    vec = lambda a: a.reshape(d_model)
    return (loss, grad_x.reshape(1, seq, d_model),
            vec(g_pre), gw_in, g_conv, gw_out, vec(g_post),
            vec(d_pre), d_in, d_cv, d_out, vec(d_post),
            vec(m_pre), m_in, m_cv, m_out, vec(m_post),
            vec(v_pre), v_in, v_cv, v_out, vec(v_post))
```

```python
import jax
import jax.numpy as jnp
from jax import lax
from jax.experimental import pallas as pl
from jax.experimental.pallas import tpu as pltpu

HEAD_DIM = 64
LANES = 128
SUBLANES = 8
BLOCK = 128
WINDOW_KEYS = 128
PERM = 16
PJ = 4
P4_ROWS = BLOCK // PJ
MAX_QUERY_BLOCKS = 4
ROW_TILE = 512
DZ_ROW_TILE = 512
CONV_K = 3
ROPE_THETA = 10000.0
NORM_EPS = 1e-6
ATTN_SCALE = HEAD_DIM ** -0.5
NEG = -1e30
N_CHIPS = 4
N_DEV = 8
MESH = pl.DeviceIdType.MESH
ADAM_LR = 0.001
ADAM_B1 = 0.9
ADAM_B2 = 0.999
ADAM_EPS = 1e-08
ADAM_WD = 0.01
ADAM_STEP = 10
VMEM_LIMIT = 52 * 1024 * 1024

F32 = jnp.float32
BF16 = jnp.bfloat16


def _params(sem=None, **kw):
    return pltpu.CompilerParams(dimension_semantics=sem, vmem_limit_bytes=VMEM_LIMIT, **kw)


def _const_spec(shape):
    return pl.BlockSpec(shape, lambda *_: (0,) * len(shape), pipeline_mode=pl.Buffered(1))


def _sigmoid(z):
    return 1.0 / (1.0 + jnp.exp(-z))


def _rowgroup_sum(a):
    rows, n = a.shape
    return a.reshape(rows // SUBLANES, SUBLANES, n).sum(axis=0)


def _nt(a, b):
    return lax.dot_general(a, b, (((1,), (1,)), ((), ())), preferred_element_type=F32)


def _tn(a, b):
    return lax.dot_general(a, b, (((0,), (0,)), ((), ())), preferred_element_type=F32)


def _col_pieces(a, b, width):
    out = []
    while a < b:
        j = a // width
        e = min(b, (j + 1) * width)
        out.append((j, a - j * width, e - j * width))
        a = e
    return out


def _lane_groups(width):
    return [slice(g * LANES, (g + 1) * LANES) for g in range(width // LANES)]


def _perm_shape(seq, width):
    return (PJ, PJ, seq // PERM, width)


def _perm_tile_spec(width, tm):
    return pl.BlockSpec((PJ, PJ, tm // PERM, width), lambda i: (0, 0, i, 0))


STAGE_PITCH = 24


def _stage_shape(groups, rows):
    return (groups, rows // PERM * STAGE_PITCH, LANES)


def _stage_put(stage, g, val, row0=0):
    for a in range(val.shape[0] // PERM):
        at = (row0 // PERM + a) * STAGE_PITCH
        stage[g, at:at + PERM, :] = val[a * PERM:(a + 1) * PERM]


def _stage_get(stage, g):
    return jnp.concatenate([stage[g, a * STAGE_PITCH:a * STAGE_PITCH + PERM, :]
                            for a in range(stage.shape[1] // STAGE_PITCH)], axis=0)


def _to_perm(stage, g, dst_ref, sl, dtype):
    rows = stage.shape[1] // STAGE_PITCH
    for b in range(PERM):
        dst_ref[b // PJ, b % PJ, :, sl] = stage[g, pl.ds(b, rows, stride=STAGE_PITCH), :].astype(dtype)


def _from_perm(src_ref, sl, stage, g):
    rows = stage.shape[1] // STAGE_PITCH
    for b in range(PERM):
        stage[g, pl.ds(b, rows, stride=STAGE_PITCH), :] = src_ref[b // PJ, b % PJ, :, sl].astype(F32)


def _flip(a, f):
    return 1 - a if f else a


def _chip_peers(x, y):
    return [(1 - x, y), (x, 1 - y), (1 - x, 1 - y)]


def gather_weights(w_in, w_out, conv_w, seq):
    d_model, width = w_in.shape
    rows = w_out.shape[0]
    cw = jnp.zeros((SUBLANES, LANES), F32).at[:CONV_K, :conv_w.shape[1]].set(conv_w)
    half_dim = HEAD_DIM // 2
    inv_freq = ROPE_THETA ** (-jnp.arange(half_dim, dtype=F32) * 2.0 / HEAD_DIM)
    inv_freq = jnp.tile(inv_freq, LANES // half_dim).reshape(1, LANES)
    chunk = min(ROW_TILE, seq)

    def body(win_ref, wout_ref, cw_ref, freq_ref, winf_ref, woutf_ref, cwf_ref, cos_ref, s1_ref, s2_ref,
             st_in, st_out, ici_send, ici_recv, d2d_send, d2d_recv):
        x, y, c = lax.axis_index("x"), lax.axis_index("y"), lax.axis_index("c")
        me = 2 * x + y
        sib = (x, y, 1 - c)
        st_in[...] = win_ref[...].astype(BF16)
        st_out[...] = wout_ref[...].astype(BF16)
        winf_ref[me] = st_in[...]
        woutf_ref[me] = st_out[...]
        cwf_ref[me] = cw_ref[...]
        stages = (st_in, st_out)
        fulls = (winf_ref, woutf_ref)
        halves = (d_model // 2, rows // 2)

        def half(t, core):
            return pl.ds(pl.multiple_of(core * halves[t], halves[t]), halves[t])

        def ici(k, t, slot, to, core):
            src = stages[t].at[half(t, core)] if t < 2 else cw_ref
            dst = fulls[t].at[slot, half(t, core)] if t < 2 else cwf_ref.at[slot]
            return pltpu.make_async_remote_copy(src_ref=src, dst_ref=dst, send_sem=ici_send.at[k, t], recv_sem=ici_recv.at[k, t],
                                                device_id=to, device_id_type=MESH)

        def d2d(k, t, slot, core):
            ref = fulls[t].at[slot, half(t, core)]
            return pltpu.make_async_remote_copy(src_ref=ref, dst_ref=ref, send_sem=d2d_send.at[k, t], recv_sem=d2d_recv.at[k, t],
                                                device_id=sib, device_id_type=MESH)

        peers = _chip_peers(x, y)
        sends = [ici(k, t, me, (px, py, c), c) for k, (px, py) in enumerate(peers) for t in range(3)]
        for cp in sends:
            cp.start()

        first_half = lax.broadcasted_iota(jnp.int32, (chunk, LANES), 1) % HEAD_DIM < half_dim
        row = lax.broadcasted_iota(jnp.int32, (chunk, LANES), 0)

        def table_rows(i, carry):
            at = pl.multiple_of(i * chunk, chunk)
            ang = (row + at).astype(F32) * freq_ref[...]
            sin = jnp.sin(ang)
            cos_ref[pl.ds(at, chunk), :] = jnp.cos(ang)
            s1_ref[pl.ds(at, chunk), :] = jnp.where(first_half, -sin, 0.0)
            s2_ref[pl.ds(at, chunk), :] = jnp.where(first_half, 0.0, sin)
            return carry

        lax.fori_loop(0, seq // chunk, table_rows, 0)

        for k, (px, py) in enumerate(peers):
            for t in range(2):
                ici(k, t, 2 * px + py, (px, py, c), c).wait_recv()
                fwd = d2d(k, t, 2 * px + py, c)
                fwd.start()
                sends.append(fwd)
            ici(k, 2, 2 * px + py, (px, py, c), c).wait_recv()
        for k, (px, py) in enumerate(peers):
            for t in range(2):
                d2d(k, t, 2 * px + py, 1 - c).wait_recv()
        for cp in sends:
            cp.wait_send()

    vm = pl.BlockSpec(memory_space=pltpu.VMEM)
    dma = pltpu.SemaphoreType.DMA
    return pl.pallas_call(
        body, name="gather_weights",
        out_shape=(jax.ShapeDtypeStruct((N_CHIPS, d_model, width), BF16),
                   jax.ShapeDtypeStruct((N_CHIPS, rows, d_model), BF16),
                   jax.ShapeDtypeStruct((N_CHIPS, SUBLANES, LANES), F32),
                   *[jax.ShapeDtypeStruct((seq, LANES), F32)] * 3),
        in_specs=[vm, vm, vm, vm], out_specs=(vm,) * 6,
        scratch_shapes=[pltpu.VMEM((d_model, width), BF16), pltpu.VMEM((rows, d_model), BF16),
                        dma((3, 3)), dma((3, 3)), dma((3, 2)), dma((3, 2))],
        compiler_params=_params(),
    )(w_in, w_out, cw, inv_freq)


def _rope(t, cos, s1, s2):
    return t * cos + pltpu.roll(t, LANES - HEAD_DIM // 2, 1) * s1 + pltpu.roll(t, HEAD_DIM // 2, 1) * s2


def _rope_transposed(g, cos, s1, s2):
    return g * cos + pltpu.roll(g * s1, HEAD_DIM // 2, 1) + pltpu.roll(g * s2, LANES - HEAD_DIM // 2, 1)


def inproj(x, g1, w_full, tables, attn_w, conv_w):
    seq, d_model = x.shape
    width = w_full.shape[2]
    tm = ROW_TILE
    groups = _lane_groups(attn_w)

    def body(x_ref, g_ref, w_ref, cos_ref, s1_ref, s2_ref,
             ht_ref, q_ref, k_ref, v_ref, qp_ref, kp_ref, vp_ref, ga_ref, cz_ref, stage):
        xv = x_ref[...]
        hb = ((xv * lax.rsqrt(jnp.mean(xv * xv, axis=-1, keepdims=True) + NORM_EPS)) * g_ref[...]).astype(BF16)
        ht_ref[...] = jnp.transpose(hb)
        cos, s1, s2 = cos_ref[...], s1_ref[...], s2_ref[...]

        def proj(a, b):
            parts = [jnp.dot(hb, w_ref[j, :, lo:hi], preferred_element_type=F32) for j, lo, hi in _col_pieces(a, b, width)]
            return parts[0] if len(parts) == 1 else jnp.concatenate(parts, axis=1)

        def emit(z, nat_ref, perm_ref, fn):
            for g, sl in enumerate(groups):
                val = fn(z[:, sl])
                nat_ref[:, sl] = val.astype(BF16)
                _stage_put(stage, g, val)
            for g, sl in enumerate(groups):
                _to_perm(stage, g, perm_ref, sl, BF16)

        emit(proj(0, attn_w), q_ref, qp_ref, lambda t: _rope(t, cos, s1, s2) * ATTN_SCALE)
        emit(proj(attn_w, 2 * attn_w), k_ref, kp_ref, lambda t: _rope(t, cos, s1, s2))
        emit(proj(2 * attn_w, 3 * attn_w), v_ref, vp_ref, lambda t: t)
        ga_ref[...] = proj(3 * attn_w, 4 * attn_w)
        cz_ref[...] = proj(4 * attn_w, 4 * attn_w + 4 * conv_w)

    row = lambda n: pl.BlockSpec((tm, n), lambda i: (i, 0))
    nat = jax.ShapeDtypeStruct((seq, attn_w), BF16)
    perm = jax.ShapeDtypeStruct(_perm_shape(seq, attn_w), BF16)
    return pl.pallas_call(
        body, name="inproj", grid=(seq // tm,),
        out_shape=(jax.ShapeDtypeStruct((d_model, seq), BF16), nat, nat, nat, perm, perm, perm,
                   jax.ShapeDtypeStruct((seq, attn_w), F32), jax.ShapeDtypeStruct((seq, 4 * conv_w), F32)),
        in_specs=[row(d_model), _const_spec((1, d_model)), _const_spec(w_full.shape), row(LANES), row(LANES), row(LANES)],
        out_specs=(pl.BlockSpec((d_model, tm), lambda i: (0, i)), row(attn_w), row(attn_w), row(attn_w),
                   _perm_tile_spec(attn_w, tm), _perm_tile_spec(attn_w, tm), _perm_tile_spec(attn_w, tm),
                   row(attn_w), row(4 * conv_w)),
        scratch_shapes=[pltpu.VMEM(_stage_shape(len(groups), tm), F32)],
        compiler_params=_params(("arbitrary",)),
    )(x, g1, w_full, *tables)


class _Mode:
    def __init__(self, name, seq):
        self.name = name
        if name == "nat":
            self.residues, blocks = 1, seq // BLOCK
        elif name == "p16":
            self.residues, blocks = PERM, seq // PERM // BLOCK
        else:
            self.residues, blocks = PJ, seq // PERM // P4_ROWS
        self.qb = max(d for d in range(1, MAX_QUERY_BLOCKS + 1) if blocks % d == 0)
        self.steps = blocks // self.qb

    def _spec(self, blocks, width, index):
        if self.name == "nat":
            return pl.BlockSpec((blocks * BLOCK, width), lambda r, n: (index(n), 0))
        if self.name == "p16":
            return pl.BlockSpec((1, 1, blocks * BLOCK, width), lambda r, n: (r // PJ, r % PJ, index(n), 0))
        return pl.BlockSpec((PJ, 1, blocks * P4_ROWS, width), lambda r, n: (0, r, index(n), 0))

    def wide(self, width, last=None):
        return self._spec(self.qb, width, (lambda n: n) if last is None else (lambda n: jnp.minimum(n, last)))

    def wide_before(self, width):
        return self._spec(self.qb, width, lambda n: jnp.maximum(n - 1, 0))

    def block_before(self, width, last=None):
        step = (lambda n: n) if last is None else (lambda n: jnp.minimum(n, last))
        return self._spec(1, width, lambda n: jnp.maximum(self.qb * step(n) - 1, 0))

    def get(self, ref, sl, sub=0):
        if self.name == "nat":
            return ref[sub * BLOCK:(sub + 1) * BLOCK, sl]
        if self.name == "p16":
            return ref[0, 0, sub * BLOCK:(sub + 1) * BLOCK, sl]
        return jnp.concatenate([ref[j, 0, sub * P4_ROWS:(sub + 1) * P4_ROWS, sl] for j in range(PJ)], axis=0)

    def put(self, ref, sl, val, sub=0):
        val = val.astype(ref.dtype)
        if self.name == "nat":
            ref[sub * BLOCK:(sub + 1) * BLOCK, sl] = val
        elif self.name == "p16":
            ref[0, 0, sub * BLOCK:(sub + 1) * BLOCK, sl] = val
        else:
            for j in range(PJ):
                ref[j, 0, sub * P4_ROWS:(sub + 1) * P4_ROWS, sl] = val[j * P4_ROWS:(j + 1) * P4_ROWS]

    def keys(self, before_ref, wide_ref, sl, sub):
        older = self.get(before_ref, sl) if sub == 0 else self.get(wide_ref, sl, sub - 1)
        return jnp.concatenate([older, self.get(wide_ref, sl, sub)], axis=0)

    def index(self, idx, is_key):
        if self.name != "p4":
            return idx - BLOCK if is_key else idx
        within = jnp.bitwise_and(idx, BLOCK - 1)
        m = PJ * jnp.bitwise_and(within, P4_ROWS - 1) + jnp.right_shift(within, P4_ROWS.bit_length() - 1)
        return m + BLOCK * (jnp.right_shift(idx, BLOCK.bit_length() - 1) - 1) if is_key else m

    def bias(self, has_before):
        shape = (2 * BLOCK, BLOCK)
        kidx = lax.broadcasted_iota(jnp.int32, shape, 0)
        qidx = lax.broadcasted_iota(jnp.int32, shape, 1)
        rel = self.index(qidx, False) - self.index(kidx, True)
        valid = (rel >= 0) & (rel <= WINDOW_KEYS)
        if has_before is not True:
            valid = valid & ((kidx >= BLOCK) | has_before)
        one = jnp.where(valid, 0.0, NEG)
        return jnp.concatenate([one, one], axis=1)


def _head_masks():
    lane = lax.broadcasted_iota(jnp.int32, (BLOCK, LANES), 1)
    lo = lane < HEAD_DIM
    return lane, lo, jnp.where(lo, 1.0, 0.0).astype(BF16), jnp.where(lo, 0.0, 1.0).astype(BF16)


def _column(blk, lane, h):
    return jnp.sum(jnp.where(lane == h, blk, 0.0), axis=1, keepdims=True)


def attn_fwd(name, q, k, v, run):
    nat = name == "nat"
    seq = q.shape[0] if nat else q.shape[2] * PERM
    attn_w = q.shape[-1]
    mode = _Mode(name, seq)
    groups = _lane_groups(attn_w)
    first = run is None
    all_lanes = slice(0, LANES)

    def body(*refs):
        q_ref, kp_ref, kc_ref, vp_ref, vc_ref = refs[:5]
        if first:
            o_ref, l_ref = refs[5:]
        elif nat:
            oin_ref, lin_ref, ex_ref, o_ref, l_ref, ostage, lstage = refs[5:]
        else:
            oin_ref, lin_ref, ex_ref, o_ref, l_ref = refs[5:]
        n = pl.program_id(1)
        subs = range(mode.qb)
        biases = [mode.bias(n > 0)] + [mode.bias(True)] * (mode.qb - 1)
        _, lo, m_lo, m_hi = _head_masks()
        head_row = lax.broadcasted_iota(jnp.int32, (BLOCK, LANES), 0)
        ones = jnp.ones((2 * BLOCK, LANES), BF16)
        lrows = [jnp.zeros((BLOCK, LANES), F32) for _ in subs]

        def probs(sub, sl):
            q2 = mode.get(q_ref, sl, sub)
            kcat = mode.keys(kp_ref, kc_ref, sl, sub)
            vcat = mode.keys(vp_ref, vc_ref, sl, sub)
            qq = jnp.concatenate([q2 * m_lo, q2 * m_hi], axis=0)
            s_t = _nt(kcat, qq) + biases[sub]
            m = jnp.max(s_t, axis=0, keepdims=True)
            pe = jnp.exp(s_t - m)
            l = jnp.sum(pe, axis=0, keepdims=True)
            return jnp.concatenate([vcat, ones], axis=1), pe.astype(BF16), m + jnp.log(l)

        def output(sub, p, sl, vext, pb, lse):
            o_ext = _tn(pb, vext)
            o_new = o_ext[:, :LANES] / o_ext[:, LANES:]
            mode.put(o_ref, sl, jnp.where(lo, o_new[:BLOCK], o_new[BLOCK:]), sub)
            rows = jnp.where(head_row == 2 * p, lse[:, :BLOCK], lrows[sub])
            lrows[sub] = jnp.where(head_row == 2 * p + 1, lse[:, BLOCK:], rows)

        pending = None
        for sub in subs:
            for p, sl in enumerate(groups):
                nxt = probs(sub, sl)
                if pending is not None:
                    output(*pending)
                pending = (sub, p, sl, *nxt)
        output(*pending)
        if not first and nat:
            for g, sl in enumerate(groups):
                _from_perm(oin_ref, sl, ostage, g)
            _from_perm(lin_ref, all_lanes, lstage, 0)
        for sub in subs:
            rows = slice(sub * BLOCK, (sub + 1) * BLOCK)
            lblk = jnp.transpose(lrows[sub])
            if first:
                mode.put(l_ref, all_lanes, lblk, sub)
                continue
            lin = _stage_get(lstage, 0)[rows] if nat else mode.get(lin_ref, all_lanes, sub)
            mx = jnp.maximum(lin, lblk)
            new = mx + jnp.log(jnp.exp(lin - mx) + jnp.exp(lblk - mx))
            mode.put(l_ref, all_lanes, new, sub)

            def expand(w):
                hi = w.astype(BF16)
                rest = (w - hi.astype(F32)).astype(BF16)
                ex = ex_ref[...]
                return jnp.dot(hi, ex, preferred_element_type=F32) + jnp.dot(rest, ex, preferred_element_type=F32)

            w_prev, w_cur = expand(jnp.exp(lin - new)), expand(jnp.exp(lblk - new))
            for p, sl in enumerate(groups):
                o_prev = _stage_get(ostage, p)[rows] if nat else mode.get(oin_ref, sl, sub)
                mode.put(o_ref, sl, w_prev[:, sl] * o_prev + w_cur[:, sl] * mode.get(o_ref, sl, sub), sub)

    ins = [q, k, k, v, v]
    specs = [mode.wide(attn_w), mode.block_before(attn_w), mode.wide(attn_w), mode.block_before(attn_w), mode.wide(attn_w)]
    scratch = []
    if not first:
        ins += list(run)
        if nat:
            rows_a = mode.qb * BLOCK // PERM
            specs += [pl.BlockSpec((PJ, PJ, rows_a, attn_w), lambda r, n: (0, 0, n, 0)),
                      pl.BlockSpec((PJ, PJ, rows_a, LANES), lambda r, n: (0, 0, n, 0))]
            scratch = [pltpu.VMEM(_stage_shape(len(groups), mode.qb * BLOCK), F32),
                       pltpu.VMEM(_stage_shape(1, mode.qb * BLOCK), F32)]
        else:
            specs += [mode.wide(attn_w), mode.wide(LANES)]
        head_of_lane = jnp.arange(attn_w, dtype=jnp.int32) // HEAD_DIM
        ins.append((jnp.arange(LANES, dtype=jnp.int32)[:, None] == head_of_lane[None, :]).astype(BF16))
        specs.append(_const_spec((LANES, attn_w)))
    if nat:
        out_shape = (jax.ShapeDtypeStruct((seq, attn_w), F32), jax.ShapeDtypeStruct((seq, LANES), F32))
    else:
        out_shape = (jax.ShapeDtypeStruct(_perm_shape(seq, attn_w), F32), jax.ShapeDtypeStruct(_perm_shape(seq, LANES), F32))
    return pl.pallas_call(
        body, name=f"attn_fwd_{name}", grid=(mode.residues, mode.steps),
        out_shape=out_shape, in_specs=specs, out_specs=(mode.wide(attn_w), mode.wide(LANES)),
        scratch_shapes=scratch,
        compiler_params=_params(("arbitrary", "arbitrary")),
    )(*ins)


def attn_bwd(name, q, k, v, d_o, lse, delta, run):
    nat = name == "nat"
    seq = q.shape[0] if nat else q.shape[2] * PERM
    attn_w = q.shape[-1]
    mode = _Mode(name, seq)
    steps, qb = mode.steps, mode.qb
    single = steps == 1
    groups = _lane_groups(attn_w)
    first = run is None
    all_lanes = slice(0, LANES)

    def body(*refs):
        q_ref, kp_ref, kc_ref, vp_ref, vc_ref, do_ref, lse_ref, dl_ref = refs[:8]
        if first:
            dq_ref, dk_ref, dv_ref, ck, cv = refs[8:]
        else:
            dqi_ref, dki_ref, dvi_ref, dq_ref, dk_ref, dv_ref, ck, cv = refs[8:]
        n = pl.program_id(1)
        carries = ((ck, dk_ref, None if first else dki_ref), (cv, dv_ref, None if first else dvi_ref))

        def emit(out_ref, acc_ref, sl, sub, val):
            if acc_ref is not None:
                val = val + mode.get(acc_ref, sl, sub).astype(F32)
            mode.put(out_ref, sl, val, sub)

        if not single:
            @pl.when(n == 0)
            def _():
                ck[...] = jnp.zeros_like(ck)
                cv[...] = jnp.zeros_like(cv)

        @pl.when(n < steps)
        def _():
            biases = [mode.bias(n > 0)] + [mode.bias(True)] * (qb - 1)
            _, lo, m_lo, m_hi = _head_masks()

            def scores(sub, p, sl, lse_t, dl_t):
                q2, do2 = mode.get(q_ref, sl, sub), mode.get(do_ref, sl, sub)
                kcat = mode.keys(kp_ref, kc_ref, sl, sub)
                vcat = mode.keys(vp_ref, vc_ref, sl, sub)
                qq = jnp.concatenate([q2 * m_lo, q2 * m_hi], axis=0)
                dd = jnp.concatenate([do2 * m_lo, do2 * m_hi], axis=0)
                h0 = 2 * p
                lse2 = jnp.concatenate([lse_t[h0:h0 + 1, :], lse_t[h0 + 1:h0 + 2, :]], axis=1)
                dl2 = jnp.concatenate([dl_t[h0:h0 + 1, :], dl_t[h0 + 1:h0 + 2, :]], axis=1)
                p_t = jnp.exp(_nt(kcat, qq) + (biases[sub] - lse2))
                ds_t = p_t * (_nt(vcat, dd) - dl2)
                return qq, dd, kcat, p_t.astype(BF16), ds_t.astype(BF16)

            def grads(sub, sl, qq, dd, kcat, pb, dsb):
                dqb = _tn(dsb, kcat)
                dq2 = jnp.where(lo, dqb[:BLOCK], dqb[BLOCK:]) * ATTN_SCALE
                if not first:
                    dq2 = dq2 + mode.get(dqi_ref, sl, sub).astype(F32)
                mode.put(dq_ref, sl, dq2, sub)
                for (carry, out_ref, acc_ref), lhs, rhs in zip(carries, (dsb, pb), (qq, dd)):
                    both = jnp.dot(lhs, rhs, preferred_element_type=F32)
                    if sub == 0:
                        if not single:
                            for s in range(qb - 1):
                                emit(out_ref, acc_ref, sl, s, carry[s, :, sl])
                            emit(out_ref, acc_ref, sl, qb - 1, carry[qb - 1, :, sl] + both[:BLOCK])
                        carry[0, :, sl] = both[BLOCK:]
                    else:
                        carry[sub - 1, :, sl] += both[:BLOCK]
                        carry[sub, :, sl] = both[BLOCK:]
                    if single and sub == qb - 1:
                        for s in range(qb):
                            emit(out_ref, acc_ref, sl, s, carry[s, :, sl])

            stats = [(jnp.transpose(mode.get(lse_ref, all_lanes, sub)),
                      jnp.transpose(mode.get(dl_ref, all_lanes, sub))) for sub in range(qb)]
            pending = None
            for p, sl in enumerate(groups):
                for sub in range(qb):
                    nxt = scores(sub, p, sl, *stats[sub])
                    if pending is not None:
                        grads(*pending)
                    pending = (sub, sl, *nxt)
            grads(*pending)

        if not single:
            @pl.when(n == steps)
            def _():
                for carry, out_ref, acc_ref in carries:
                    for sl in groups:
                        for s in range(qb):
                            emit(out_ref, acc_ref, sl, s, carry[s, :, sl])

    last = steps - 1
    wide = lambda w: mode.wide(w, last)
    ins = [q, k, k, v, v, d_o, lse, delta]
    specs = [wide(attn_w), mode.block_before(attn_w, last), wide(attn_w), mode.block_before(attn_w, last), wide(attn_w),
             wide(attn_w), wide(LANES), wide(LANES)]
    if not first:
        ins += list(run)
        specs += [wide(attn_w), mode.wide_before(attn_w), mode.wide_before(attn_w)]
    shp = jax.ShapeDtypeStruct((seq, attn_w) if nat else _perm_shape(seq, attn_w), BF16)
    return pl.pallas_call(
        body, name=f"attn_bwd_{name}", grid=(mode.residues, steps if single else steps + 1),
        out_shape=(shp, shp, shp), in_specs=specs,
        out_specs=(wide(attn_w), mode.wide_before(attn_w), mode.wide_before(attn_w)),
        scratch_shapes=[pltpu.VMEM((qb, BLOCK, attn_w), F32), pltpu.VMEM((qb, BLOCK, attn_w), F32)],
        compiler_params=_params(("arbitrary", "arbitrary")),
    )(*ins)


def _shift_down(u, halo, k):
    rolled = pltpu.roll(u, k, 0)
    row = lax.broadcasted_iota(jnp.int32, halo.shape, 0)
    top = jnp.where(row < k, pltpu.roll(halo, k, 0), rolled[:SUBLANES])
    return jnp.concatenate([top, rolled[SUBLANES:]], axis=0)


def _shift_up(u, halo, k):
    rows = u.shape[0]
    rolled = pltpu.roll(u, rows - k, 0)
    row = lax.broadcasted_iota(jnp.int32, halo.shape, 0)
    bot = jnp.where(row >= SUBLANES - k, pltpu.roll(halo, SUBLANES - k, 0), rolled[rows - SUBLANES:])
    return jnp.concatenate([rolled[:rows - SUBLANES], bot], axis=0)


def tail(o, lse, ga, cz, x, tgt, w_out, g2, cw):
    seq, d_model = x.shape
    attn_w = o.shape[1]
    conv_w = cz.shape[1] // 4
    mix = attn_w + conv_w
    groups = _lane_groups(attn_w)
    tm = ROW_TILE
    nt = seq // tm
    hb = tm // SUBLANES

    def body(o_ref, l_ref, ga_ref, cz_ref, hz_ref, x_ref, t_ref, w_ref, g_ref, cw_ref,
             do_ref, dl_ref, dop_ref, dlp_ref, lp_ref, dga_ref, dcb_ref, dgc_ref, dcv_ref, e_ref,
             dw_ref, dg_ref, dcw_ref, loss_ref, stage):
        i = pl.program_id(0)

        @pl.when(i == 0)
        def _():
            dw_ref[...] = jnp.zeros_like(dw_ref)
            dg_ref[...] = jnp.zeros_like(dg_ref)
            dcw_ref[...] = jnp.zeros_like(dcw_ref)
            loss_ref[...] = jnp.zeros_like(loss_ref)

        u = cz_ref[:, 2 * conv_w:3 * conv_w] * cz_ref[:, 0:conv_w]
        uh = hz_ref[:, 2 * conv_w:3 * conv_w] * hz_ref[:, 0:conv_w]
        uh = jnp.where(i > 0, uh, 0.0)
        u1 = _shift_down(u, uh, 1)
        u2 = _shift_down(u, uh, 2)
        w0, w1, w2 = cw_ref[0:1, :], cw_ref[1:2, :], cw_ref[2:3, :]
        cvv = u2 * w0 + u1 * w1 + u * w2
        gv = g_ref[...]
        all_lanes = slice(0, LANES)

        def forward(rs):
            ov, gav = o_ref[rs, :], ga_ref[rs, :]
            sig_a = _sigmoid(gav)
            silu_a = gav * sig_a
            cb, gc = cz_ref[rs, conv_w:2 * conv_w], cz_ref[rs, 3 * conv_w:4 * conv_w]
            sig_c = _sigmoid(gc)
            silu_c = gc * sig_c
            bc = cb * cvv[rs]
            mixed = jnp.concatenate([ov * silu_a, bc * silu_c], axis=1).astype(BF16)
            yv = jnp.dot(mixed, w_ref[...], preferred_element_type=F32)
            return ov, gav, sig_a, silu_a, cb, gc, sig_c, silu_c, bc, mixed, yv

        def loss_and_dy(rs, mixed, yv):
            r2 = lax.rsqrt(jnp.mean(yv * yv, axis=-1, keepdims=True) + NORM_EPS)
            yhat = yv * r2
            diff = (x_ref[rs, :] + yhat * gv) - t_ref[rs, :]
            loss_ref[...] += _rowgroup_sum(diff * diff)
            ev = diff * (1.0 / d_model)
            e_ref[rs, :] = ev
            dg_ref[...] += _rowgroup_sum(ev * yhat)
            eg = ev * gv
            dy = (r2 * (eg - yhat * jnp.mean(eg * yhat, axis=-1, keepdims=True))).astype(BF16)
            dw_ref[...] += _tn(mixed, dy)
            return _nt(dy, w_ref[...])

        def backward(rs, ov, gav, sig_a, silu_a, cb, gc, sig_c, silu_c, bc, dm):
            rows = rs.stop - rs.start
            dma, dmc = dm[:, :attn_w], dm[:, attn_w:]
            dov = dma * silu_a
            do_ref[rs, :] = dov.astype(BF16)
            dga_ref[rs, :] = (dma * ov * (sig_a * (1.0 + gav * (1.0 - sig_a)))).astype(BF16)
            prod = dov * ov
            lane = lax.broadcasted_iota(jnp.int32, (rows, LANES), 1)
            lo = lane < HEAD_DIM
            dblk = jnp.zeros((rows, LANES), F32)
            for p, sl in enumerate(groups):
                pr = prod[:, sl]
                dblk = jnp.where(lane == 2 * p, jnp.sum(jnp.where(lo, pr, 0.0), axis=1, keepdims=True), dblk)
                dblk = jnp.where(lane == 2 * p + 1, jnp.sum(jnp.where(lo, 0.0, pr), axis=1, keepdims=True), dblk)
                _stage_put(stage, p, dov[:, sl], rs.start)
            dl_ref[rs, :] = dblk
            _stage_put(stage, len(groups), dblk, rs.start)
            _stage_put(stage, len(groups) + 1, l_ref[rs, :], rs.start)
            dsc = dmc * silu_c
            cv_rows = cvv[rs]
            dcb_ref[rs, :] = (dsc * cv_rows).astype(BF16)
            dgc_ref[rs, :] = (dmc * bc * (sig_c * (1.0 + gc * (1.0 - sig_c)))).astype(BF16)
            dcv = dsc * cb
            dcv_ref[rs, :] = dcv
            dcw_ref[0:SUBLANES, :] += _rowgroup_sum(dcv * u2[rs])
            dcw_ref[SUBLANES:2 * SUBLANES, :] += _rowgroup_sum(dcv * u1[rs])
            dcw_ref[2 * SUBLANES:3 * SUBLANES, :] += _rowgroup_sum(dcv * u[rs])

        halves = [slice(0, tm // 2), slice(tm // 2, tm)]
        fwd = [forward(rs) for rs in halves]
        dms = [loss_and_dy(rs, f[9], f[10]) for rs, f in zip(halves, fwd)]
        for rs, f, dm in zip(halves, fwd, dms):
            backward(rs, *f[:9], dm)
        for p, sl in enumerate(groups):
            _to_perm(stage, p, dop_ref, sl, BF16)
        _to_perm(stage, len(groups), dlp_ref, all_lanes, F32)
        _to_perm(stage, len(groups) + 1, lp_ref, all_lanes, F32)

    row = lambda n: pl.BlockSpec((tm, n), lambda i: (i, 0))
    whole = lambda a, b: pl.BlockSpec((a, b), lambda i: (0, 0))
    return pl.pallas_call(
        body, name="tail", grid=(nt,),
        out_shape=(jax.ShapeDtypeStruct((seq, attn_w), BF16), jax.ShapeDtypeStruct((seq, LANES), F32),
                   jax.ShapeDtypeStruct(_perm_shape(seq, attn_w), BF16), jax.ShapeDtypeStruct(_perm_shape(seq, LANES), F32),
                   jax.ShapeDtypeStruct(_perm_shape(seq, LANES), F32),
                   jax.ShapeDtypeStruct((seq, attn_w), BF16), jax.ShapeDtypeStruct((seq, conv_w), BF16),
                   jax.ShapeDtypeStruct((seq, conv_w), BF16), jax.ShapeDtypeStruct((seq, conv_w), F32),
                   jax.ShapeDtypeStruct((seq, d_model), F32), jax.ShapeDtypeStruct((mix, d_model), F32),
                   jax.ShapeDtypeStruct((SUBLANES, d_model), F32), jax.ShapeDtypeStruct((CONV_K * SUBLANES, conv_w), F32),
                   jax.ShapeDtypeStruct((SUBLANES, d_model), F32)),
        in_specs=[row(attn_w), row(LANES), row(attn_w), row(4 * conv_w),
                  pl.BlockSpec((SUBLANES, 4 * conv_w), lambda i: (jnp.maximum(i * hb - 1, 0), 0)),
                  row(d_model), row(d_model), _const_spec((mix, d_model)), _const_spec((1, d_model)),
                  _const_spec((SUBLANES, conv_w))],
        out_specs=(row(attn_w), row(LANES), _perm_tile_spec(attn_w, tm), _perm_tile_spec(LANES, tm), _perm_tile_spec(LANES, tm),
                   row(attn_w), row(conv_w), row(conv_w), row(conv_w), row(d_model),
                   whole(mix, d_model), whole(SUBLANES, d_model), whole(CONV_K * SUBLANES, conv_w),
                   whole(SUBLANES, d_model)),
        scratch_shapes=[pltpu.VMEM(_stage_shape(len(groups) + 2, tm), F32)],
        compiler_params=_params(("arbitrary",)),
    )(o, lse, ga, cz, cz, x, tgt, w_out, g2, cw)


def dz_dx(nat_grads, perm_grads, dga, dcb, dgc, dcv, cz, tables, x, g1, e, w_full, cw):
    seq, d_model = x.shape
    attn_w = dga.shape[1]
    conv_w = dcv.shape[1]
    width = w_full.shape[2]
    in_w = 4 * attn_w + 4 * conv_w
    groups = _lane_groups(attn_w)
    tm = DZ_ROW_TILE
    nt = seq // tm
    hb = tm // SUBLANES

    def body(dq_ref, dk_ref, dv_ref, dqp_ref, dkp_ref, dvp_ref, dga_ref, dcb_ref, dgc_ref, dcv_ref, nh_ref, cz_ref,
             cos_ref, s1_ref, s2_ref, x_ref, g_ref, e_ref, w_ref, cw_ref, gx_ref, dz_ref, dg_ref, stage):
        i = pl.program_id(0)

        @pl.when(i == 0)
        def _():
            dg_ref[...] = jnp.zeros_like(dg_ref)

        cos, s1, s2 = cos_ref[...], s1_ref[...], s2_ref[...]
        for t, (nat_ref, perm_ref) in enumerate(((dq_ref, dqp_ref), (dk_ref, dkp_ref), (dv_ref, dvp_ref))):
            for g, sl in enumerate(groups):
                _from_perm(perm_ref, sl, stage, g)
            for g, sl in enumerate(groups):
                tot = nat_ref[:, sl].astype(F32) + _stage_get(stage, g)
                if t < 2:
                    tot = _rope_transposed(tot, cos, s1, s2)
                dz_ref[:, t * attn_w + g * LANES:t * attn_w + (g + 1) * LANES] = tot.astype(BF16)
        dz_ref[:, 3 * attn_w:4 * attn_w] = dga_ref[...]
        dcv = dcv_ref[...]
        nh = jnp.where(i < nt - 1, nh_ref[...], 0.0)
        w0, w1, w2 = cw_ref[0:1, :], cw_ref[1:2, :], cw_ref[2:3, :]
        du = dcv * w2 + _shift_up(dcv, nh, 1) * w1 + _shift_up(dcv, nh, 2) * w0
        base = 4 * attn_w
        dz_ref[:, base:base + conv_w] = (du * cz_ref[:, 2 * conv_w:3 * conv_w]).astype(BF16)
        dz_ref[:, base + conv_w:base + 2 * conv_w] = dcb_ref[...]
        dz_ref[:, base + 2 * conv_w:base + 3 * conv_w] = (du * cz_ref[:, 0:conv_w]).astype(BF16)
        dz_ref[:, base + 3 * conv_w:base + 4 * conv_w] = dgc_ref[...]

        dh = _nt(dz_ref[:, 0:width], w_ref[0])
        for j in range(1, N_CHIPS):
            dh = dh + _nt(dz_ref[:, j * width:(j + 1) * width], w_ref[j])
        xv = x_ref[...]
        r1 = lax.rsqrt(jnp.mean(xv * xv, axis=-1, keepdims=True) + NORM_EPS)
        xhat = xv * r1
        dg_ref[...] += _rowgroup_sum(dh * xhat)
        dhg = dh * g_ref[...]
        gx_ref[...] = r1 * (dhg - xhat * jnp.mean(dhg * xhat, axis=-1, keepdims=True)) + e_ref[...]

    row = lambda n: pl.BlockSpec((tm, n), lambda i: (i, 0))
    whole = lambda a, b: pl.BlockSpec((a, b), lambda i: (0, 0))
    pt = _perm_tile_spec(attn_w, tm)
    return pl.pallas_call(
        body, name="dz_dx", grid=(nt,),
        out_shape=(jax.ShapeDtypeStruct((seq, d_model), F32), jax.ShapeDtypeStruct((seq, in_w), BF16),
                   jax.ShapeDtypeStruct((SUBLANES, d_model), F32)),
        in_specs=[row(attn_w), row(attn_w), row(attn_w), pt, pt, pt, row(attn_w), row(conv_w), row(conv_w), row(conv_w),
                  pl.BlockSpec((SUBLANES, conv_w), lambda i: (jnp.minimum((i + 1) * hb, seq // SUBLANES - 1), 0)),
                  row(4 * conv_w), row(LANES), row(LANES), row(LANES), row(d_model), _const_spec((1, d_model)), row(d_model),
                  _const_spec(w_full.shape), _const_spec((SUBLANES, conv_w))],
        out_specs=(row(d_model), row(in_w), whole(SUBLANES, d_model)),
        scratch_shapes=[pltpu.VMEM(_stage_shape(len(groups), tm), F32)],
        compiler_params=_params(("arbitrary",)),
    )(*nat_grads, *perm_grads, dga, dcb, dgc, dcv, dcv, cz, *tables, x, g1, e, w_full, cw)


def dw_in_reduce(ht, dz):
    d_model, seq = ht.shape
    half = dz.shape[1] // N_DEV
    ts = min(2048, seq)
    steps = seq // ts
    x, y, c = lax.axis_index("x"), lax.axis_index("y"), lax.axis_index("c")
    far_first = lambda x, y: [(1 - x, 1 - y), (1 - x, y), (x, 1 - y)]
    chips = jnp.stack([2 * px + py for px, py in far_first(x, y)] + [2 * x + y]).astype(jnp.int32)
    order = jnp.stack([2 * chips + (1 - c), 2 * chips + c], axis=1).reshape(N_DEV)

    def body(order_ref, ht_ref, dz_ref, out_ref, acc, theirs, staged, contrib, resbuf, out_sem, sa, ra, sb, rb, sc, rc):
        del order_ref
        p, s = pl.program_id(0), pl.program_id(1)
        x, y, c = lax.axis_index("x"), lax.axis_index("y"), lax.axis_index("c")
        sib = (x, y, 1 - c)
        peers = far_first(x, y)
        slot = p % 2

        def a_copy(k):
            return pltpu.make_async_remote_copy(src_ref=acc.at[0], dst_ref=theirs.at[k], send_sem=sa.at[k], recv_sem=ra.at[k],
                                                device_id=sib, device_id_type=MESH)

        def b_copy(k):
            px, py = peers[k]
            return pltpu.make_async_remote_copy(src_ref=staged.at[k], dst_ref=contrib.at[k], send_sem=sb.at[k], recv_sem=rb.at[k],
                                                device_id=(px, py, c), device_id_type=MESH)

        def c_copy(which):
            return pltpu.make_async_remote_copy(src_ref=resbuf.at[which], dst_ref=resbuf.at[which], send_sem=sc, recv_sem=rc,
                                                device_id=sib, device_id_type=MESH)

        @pl.when(s == 0)
        def _():
            for k in range(N_CHIPS - 1):
                @pl.when(p == 2 * k + 2)
                def _():
                    a_copy(k).wait_send()
            acc[slot] = jnp.zeros((d_model, half), F32)

        acc[slot] += jnp.dot(ht_ref[...], dz_ref[...], preferred_element_type=F32)

        @pl.when(s == steps - 1)
        def _():
            for k in range(N_CHIPS):
                @pl.when(p == 2 * k)
                def _():
                    a_copy(k).start()
            for k in range(N_CHIPS - 1):
                @pl.when(p == 2 * k + 1)
                def _():
                    a_copy(k).wait_recv()
                    staged[k] = (acc[1] + theirs[k]).astype(BF16)
                    b_copy(k).start()

            @pl.when(p == N_DEV - 1)
            def _():
                a_copy(N_CHIPS - 1).wait_recv()
                tot = acc[1] + theirs[N_CHIPS - 1]
                for k in range(N_CHIPS - 1):
                    b_copy(k).wait_recv()
                    tot = tot + contrib[k].astype(F32)
                resbuf[c] = tot
                c_copy(c).start()
                c_copy(1 - c).wait_recv()
                done = pltpu.make_async_copy(resbuf, out_ref, out_sem)
                done.start()
                a_copy(N_CHIPS - 1).wait_send()
                for k in range(N_CHIPS - 1):
                    b_copy(k).wait_send()
                c_copy(c).wait_send()
                done.wait()

    dma = pltpu.SemaphoreType.DMA
    grid_spec = pltpu.PrefetchScalarGridSpec(
        num_scalar_prefetch=1, grid=(N_DEV, steps),
        in_specs=[pl.BlockSpec((d_model, ts), lambda p, s, order_ref: (0, s)),
                  pl.BlockSpec((ts, half), lambda p, s, order_ref: (s, order_ref[p]))],
        out_specs=pl.BlockSpec(memory_space=pl.ANY),
        scratch_shapes=[pltpu.VMEM((2, d_model, half), F32), pltpu.VMEM((N_CHIPS, d_model, half), F32),
                        pltpu.VMEM((N_CHIPS - 1, d_model, half), BF16), pltpu.VMEM((N_CHIPS - 1, d_model, half), BF16),
                        pltpu.VMEM((2, d_model, half), F32), dma,
                        dma((N_CHIPS,)), dma((N_CHIPS,)), dma((N_CHIPS - 1,)), dma((N_CHIPS - 1,)), dma, dma])
    return pl.pallas_call(
        body, name="dw_in_reduce", grid_spec=grid_spec,
        out_shape=jax.ShapeDtypeStruct((2, d_model, half), F32),
        compiler_params=_params(("arbitrary", "arbitrary")),
    )(order, ht, dz)


def grad_reduce(tensors, small):
    nt = len(tensors)
    split = [g.reshape(N_CHIPS, 2, *g.shape[1:]) for g in tensors]
    shapes = [g.shape[2:] for g in split]

    def body(*refs):
        srcs, sm_ref = refs[:nt], refs[nt]
        res, rs_ref = refs[nt + 1:2 * nt + 1], refs[2 * nt + 1]
        scratch = refs[2 * nt + 2:]
        mine, theirs, staged, contrib = (scratch[k * nt:(k + 1) * nt] for k in range(4))
        sbuf, loc_sems, sa, ra, sb, rb, sc, rc, ss, rs = scratch[4 * nt:]
        x, y, c = lax.axis_index("x"), lax.axis_index("y"), lax.axis_index("c")
        me = 2 * x + y
        sib = (x, y, 1 - c)

        flips = [(fx, fy, fc) for fx in (0, 1) for fy in (0, 1) for fc in (0, 1)][1:]
        my8 = 4 * x + 2 * y + c
        sbuf[my8] = sm_ref[...]

        def small_copy(k, slot, to):
            return pltpu.make_async_remote_copy(src_ref=sm_ref, dst_ref=sbuf.at[slot], send_sem=ss.at[k], recv_sem=rs.at[k],
                                                device_id=to, device_id_type=MESH)

        sends = []
        for k, (fx, fy, fc) in enumerate(flips):
            px, py, pc = _flip(x, fx), _flip(y, fy), _flip(c, fc)
            sends.append(small_copy(k, my8, (px, py, pc)))
            sends[-1].start()

        def a_copy(t, j):
            return pltpu.make_async_remote_copy(src_ref=srcs[t].at[j, 1 - c], dst_ref=theirs[t].at[j], send_sem=sa.at[t, j],
                                                recv_sem=ra.at[t, j], device_id=sib, device_id_type=MESH)

        peers = _chip_peers(x, y)
        order = [2 * px + py for px, py in peers] + [me]
        loads = [[pltpu.make_async_copy(srcs[t].at[j, c], mine[t].at[j], loc_sems.at[t, j]) for t in range(nt)] for j in order]
        for pos, j in enumerate(order):
            for t in range(nt):
                loads[pos][t].start()
                sends.append(a_copy(t, j))
                sends[-1].start()

        def b_copy(k, t, piece, slot, to):
            return pltpu.make_async_remote_copy(src_ref=staged[t].at[piece], dst_ref=contrib[t].at[slot], send_sem=sb.at[k, t],
                                                recv_sem=rb.at[k, t], device_id=to, device_id_type=MESH)

        for k, (px, py) in enumerate(peers):
            j = 2 * px + py
            for t in range(nt):
                loads[k][t].wait()
                a_copy(t, j).wait_recv()
                staged[t][j] = (mine[t][j] + theirs[t][j]).astype(BF16)
                sends.append(b_copy(k, t, j, me, (px, py, c)))
                sends[-1].start()
        for t in range(nt):
            loads[len(peers)][t].wait()
            a_copy(t, me).wait_recv()
            mine[t][me] = mine[t][me] + theirs[t][me]
            contrib[t][me] = mine[t][me].astype(BF16)
        for k, (px, py) in enumerate(peers):
            for t in range(nt):
                b_copy(k, t, me, 2 * px + py, (px, py, c)).wait_recv()

        def c_copy(t, half):
            return pltpu.make_async_remote_copy(src_ref=res[t].at[half], dst_ref=res[t].at[half], send_sem=sc.at[t],
                                                recv_sem=rc.at[t], device_id=sib, device_id_type=MESH)

        for t in range(nt):
            own = mine[t][me]
            term = lambda j: jnp.where(me == j, own, contrib[t][j].astype(F32))
            res[t][c] = ((term(0) + term(1)) + term(2)) + term(3)
            sends.append(c_copy(t, c))
            sends[-1].start()
        for t in range(nt):
            c_copy(t, 1 - c).wait_recv()

        for k, (fx, fy, fc) in enumerate(flips):
            px, py, pc = _flip(x, fx), _flip(y, fy), _flip(c, fc)
            small_copy(k, 4 * px + 2 * py + pc, (px, py, pc)).wait_recv()
        tot = sbuf[0]
        for d in range(1, N_DEV):
            tot = tot + sbuf[d]
        rs_ref[...] = tot
        for cp in sends:
            cp.wait_send()

    vm = pl.BlockSpec(memory_space=pltpu.VMEM)
    anyspace = pl.BlockSpec(memory_space=pl.ANY)
    dma = pltpu.SemaphoreType.DMA
    bufs = [pltpu.VMEM((N_CHIPS, *shp), dt) for dt in (F32, F32, BF16, BF16) for shp in shapes]
    outs = pl.pallas_call(
        body, name="grad_reduce",
        out_shape=(*[jax.ShapeDtypeStruct((2, *shp), F32) for shp in shapes], jax.ShapeDtypeStruct(small.shape, F32)),
        in_specs=[anyspace] * nt + [vm], out_specs=tuple([vm] * (nt + 1)),
        scratch_shapes=[*bufs, pltpu.VMEM((N_DEV, *small.shape), F32),
                        dma((nt, N_CHIPS)), dma((nt, N_CHIPS)), dma((nt, N_CHIPS)), dma((3, nt)), dma((3, nt)), dma((nt,)), dma((nt,)),
                        dma((N_DEV - 1,)), dma((N_DEV - 1,))],
        compiler_params=_params(),
    )(*split, small)
    return outs[:nt], outs[nt]


def _adam_math(w, g, m, v):
    m = ADAM_B1 * m + (1.0 - ADAM_B1) * g
    v = ADAM_B2 * v + (1.0 - ADAM_B2) * (g * g)
    m_hat = m / (1.0 - ADAM_B1 ** ADAM_STEP)
    v_hat = v / (1.0 - ADAM_B2 ** ADAM_STEP)
    delta = -ADAM_LR * (m_hat / (jnp.sqrt(v_hat) + ADAM_EPS) + ADAM_WD * w)
    return delta, m, v


def adam_shard(name, w, g2, m, v, block, grid, w_map, g_map):
    def body(w_ref, g_ref, m_ref, v_ref, go_ref, d_ref, mo_ref, vo_ref):
        g = g_ref[0]
        delta, mn, vn = _adam_math(w_ref[...], g, m_ref[...], v_ref[...])
        go_ref[...] = g
        d_ref[...] = delta
        mo_ref[...] = mn
        vo_ref[...] = vn

    ws = pl.BlockSpec(block, w_map)
    shp = jax.ShapeDtypeStruct(w.shape, F32)
    return pl.pallas_call(
        body, name=name, grid=grid, out_shape=(shp, shp, shp, shp),
        in_specs=[ws, pl.BlockSpec((1, *block), g_map), ws, ws], out_specs=(ws, ws, ws, ws),
        compiler_params=_params(("arbitrary",) * len(grid)),
    )(w, g2, m, v)


def adam_small(ws, gs, ms, vs):
    n = len(ws)

    def body(*refs):
        ins, outs = refs[:4 * n], refs[4 * n:]
        for t in range(n):
            delta, mn, vn = _adam_math(ins[t][...], ins[n + t][...], ins[2 * n + t][...], ins[3 * n + t][...])
            outs[3 * t][...] = delta
            outs[3 * t + 1][...] = mn
            outs[3 * t + 2][...] = vn

    vm = pl.BlockSpec(memory_space=pltpu.VMEM)
    outs = pl.pallas_call(
        body, name="adam_small",
        out_shape=tuple(jax.ShapeDtypeStruct(w.shape, F32) for w in ws for _ in range(3)),
        in_specs=[vm] * (4 * n), out_specs=tuple([vm] * (3 * n)),
        compiler_params=_params(),
    )(*ws, *gs, *ms, *vs)
    return [outs[3 * t:3 * t + 3] for t in range(n)]


def kernel(x, norm_pre_g, w_in, conv_w, w_out, norm_post_g, loss_target, m_norm_pre_g, m_w_in, m_conv_w, m_w_out, m_norm_post_g, v_norm_pre_g, v_w_in, v_conv_w, v_w_out, v_norm_post_g):
    _, seq, d_model = x.shape
    width = w_in.shape[1]
    conv_q = conv_w.shape[1]
    conv_width = N_CHIPS * conv_q
    attn_width = d_model - conv_width
    xs, tg = x[0], loss_target[0]
    g1, g2 = norm_pre_g.reshape(1, d_model), norm_post_g.reshape(1, d_model)

    w_full, wout_full, cw_full, *tables = gather_weights(w_in, w_out, conv_w, seq)
    wout2 = wout_full.reshape(attn_width + conv_width, d_model)
    cw = jnp.zeros((SUBLANES, conv_width), F32).at[:CONV_K].set(
        cw_full[:, :CONV_K, :conv_q].transpose(1, 0, 2).reshape(CONV_K, conv_width))

    ht, q, k, v, qp, kp, vp, ga, cz = inproj(xs, g1, w_full, tables, attn_width, conv_width)
    run = attn_fwd("p4", qp, kp, vp, None)
    run = attn_fwd("p16", qp, kp, vp, run)
    o, lse = attn_fwd("nat", q, k, v, run)
    (d_o, delta, d_op, delta_p, lse_p, dga, dcb, dgc, dcv, e, dwout, dg2, dcw, loss_acc) = tail(
        o, lse, ga, cz, xs, tg, wout2, g2, cw)
    nat_grads = attn_bwd("nat", q, k, v, d_o, lse, delta, None)
    perm_grads = attn_bwd("p4", qp, kp, vp, d_op, lse_p, delta_p, None)
    perm_grads = attn_bwd("p16", qp, kp, vp, d_op, lse_p, delta_p, perm_grads)
    grad_x, dz, dg1 = dz_dx(nat_grads, perm_grads, dga, dcb, dgc, dcv, cz, tables, xs, g1, e, w_full, cw)

    small = jnp.zeros((SUBLANES, d_model), F32)
    small = small.at[0].set(dg1.sum(axis=0)).at[1].set(dg2.sum(axis=0))
    small = small.at[2:2 + CONV_K, :conv_width].set(dcw.reshape(CONV_K, SUBLANES, conv_width).sum(axis=1))
    small = small.at[2 + CONV_K, 0].set(jnp.sum(loss_acc))
    (rout,), rsmall = grad_reduce([dwout.reshape(N_DEV, -1, d_model)], small)
    rin = dw_in_reduce(ht, dz)

    half = width // 2
    tr = 256
    gw_in, d_in, m_in, v_in = adam_shard(
        "adam_w_in", w_in, rin, m_w_in, v_w_in, (tr, half), (2, d_model // tr),
        lambda hf, i: (i, hf), lambda hf, i: (hf, i, 0))
    rq = w_out.shape[0] // 2
    gw_out, d_out, m_out, v_out = adam_shard(
        "adam_w_out", w_out, rout, m_w_out, v_w_out, (rq, d_model), (2,),
        lambda hf: (hf, 0), lambda hf: (hf, 0, 0))

    chip = 2 * lax.axis_index("x") + lax.axis_index("y")
    g_pre, g_post = rsmall[0:1], rsmall[1:2]
    g_conv = lax.dynamic_slice(rsmall[2:2 + CONV_K, :conv_width], (0, chip * conv_q), (CONV_K, conv_q))
    (d_pre, m_pre, v_pre), (d_post, m_post, v_post), (d_cv, m_cv, v_cv) = adam_small(
        [g1, g2, conv_w], [g_pre, g_post, g_conv],
        [m_norm_pre_g.reshape(1, d_model), m_norm_post_g.reshape(1, d_model), m_conv_w],
        [v_norm_pre_g.reshape(1, d_model), v_norm_post_g.reshape(1, d_model), v_conv_w])

    loss = 0.5 * rsmall[2 + CONV_K, 0] / d_model
    vec = lambda a: a.reshape(d_model)
    return (loss, grad_x.reshape(1, seq, d_model),
            vec(g_pre), gw_in, g_conv, gw_out, vec(g_post),
            vec(d_pre), d_in, d_cv, d_out, vec(d_post),
            vec(m_pre), m_in, m_cv, m_out, vec(m_post),
            vec(v_pre), v_in, v_cv, v_out, vec(v_post))
```

```python
import jax
import jax.numpy as jnp
from jax import lax
from jax.experimental import pallas as pl
from jax.experimental.pallas import tpu as pltpu

HEAD_DIM = 64
LANES = 128
SUBLANES = 8
BLOCK = 128
WINDOW_KEYS = 128
PERM = 16
PJ = 4
P4_ROWS = BLOCK // PJ
MAX_QUERY_BLOCKS = 4
ROW_TILE = 512
DZ_ROW_TILE = 512
CONV_K = 3
ROPE_THETA = 10000.0
NORM_EPS = 1e-6
ATTN_SCALE = HEAD_DIM ** -0.5
NEG = -1e30
N_CHIPS = 4
N_DEV = 8
MESH = pl.DeviceIdType.MESH
ADAM_LR = 0.001
ADAM_B1 = 0.9
ADAM_B2 = 0.999
ADAM_EPS = 1e-08
ADAM_WD = 0.01
ADAM_STEP = 10
VMEM_LIMIT = 52 * 1024 * 1024

F32 = jnp.float32
BF16 = jnp.bfloat16


def _params(sem=None, **kw):
    return pltpu.CompilerParams(dimension_semantics=sem, vmem_limit_bytes=VMEM_LIMIT, **kw)


def _const_spec(shape):
    return pl.BlockSpec(shape, lambda *_: (0,) * len(shape), pipeline_mode=pl.Buffered(1))


def _sigmoid(z):
    return 1.0 / (1.0 + jnp.exp(-z))


def _rowgroup_sum(a):
    rows, n = a.shape
    return a.reshape(rows // SUBLANES, SUBLANES, n).sum(axis=0)


def _nt(a, b):
    return lax.dot_general(a, b, (((1,), (1,)), ((), ())), preferred_element_type=F32)


def _tn(a, b):
    return lax.dot_general(a, b, (((0,), (0,)), ((), ())), preferred_element_type=F32)


def _col_pieces(a, b, width):
    out = []
    while a < b:
        j = a // width
        e = min(b, (j + 1) * width)
        out.append((j, a - j * width, e - j * width))
        a = e
    return out


def _lane_groups(width):
    return [slice(g * LANES, (g + 1) * LANES) for g in range(width // LANES)]


def _perm_shape(seq, width):
    return (PJ, PJ, seq // PERM, width)


def _perm_tile_spec(width, tm):
    return pl.BlockSpec((PJ, PJ, tm // PERM, width), lambda i: (0, 0, i, 0))


STAGE_PITCH = 24


def _stage_shape(groups, rows):
    return (groups, rows // PERM * STAGE_PITCH, LANES)


def _stage_put(stage, g, val, row0=0):
    for a in range(val.shape[0] // PERM):
        at = (row0 // PERM + a) * STAGE_PITCH
        stage[g, at:at + PERM, :] = val[a * PERM:(a + 1) * PERM]


def _stage_get(stage, g):
    return jnp.concatenate([stage[g, a * STAGE_PITCH:a * STAGE_PITCH + PERM, :]
                            for a in range(stage.shape[1] // STAGE_PITCH)], axis=0)


def _to_perm(stage, g, dst_ref, sl, dtype):
    rows = stage.shape[1] // STAGE_PITCH
    for b in range(PERM):
        dst_ref[b // PJ, b % PJ, :, sl] = stage[g, pl.ds(b, rows, stride=STAGE_PITCH), :].astype(dtype)


def _from_perm(src_ref, sl, stage, g):
    rows = stage.shape[1] // STAGE_PITCH
    for b in range(PERM):
        stage[g, pl.ds(b, rows, stride=STAGE_PITCH), :] = src_ref[b // PJ, b % PJ, :, sl].astype(F32)


def _flip(a, f):
    return 1 - a if f else a


def _chip_peers(x, y):
    return [(1 - x, y), (x, 1 - y), (1 - x, 1 - y)]


def gather_weights(w_in, w_out, conv_w, seq):
    d_model, width = w_in.shape
    rows = w_out.shape[0]
    cw = jnp.zeros((SUBLANES, LANES), F32).at[:CONV_K, :conv_w.shape[1]].set(conv_w)
    half_dim = HEAD_DIM // 2
    inv_freq = ROPE_THETA ** (-jnp.arange(half_dim, dtype=F32) * 2.0 / HEAD_DIM)
    inv_freq = jnp.tile(inv_freq, LANES // half_dim).reshape(1, LANES)
    chunk = min(ROW_TILE, seq)

    def body(win_ref, wout_ref, cw_ref, freq_ref, winf_ref, woutf_ref, cwf_ref, cos_ref, s1_ref, s2_ref,
             st_in, st_out, near_send, near_recv, far_send, far_recv, cw_send, cw_recv, d2d_send, d2d_recv):
        x, y, c = lax.axis_index("x"), lax.axis_index("y"), lax.axis_index("c")
        me = 2 * x + y
        sib = (x, y, 1 - c)
        st_in[...] = win_ref[...].astype(BF16)
        st_out[...] = wout_ref[...].astype(BF16)
        winf_ref[me] = st_in[...]
        woutf_ref[me] = st_out[...]
        cwf_ref[me] = cw_ref[...]
        stages = (st_in, st_out)
        fulls = (winf_ref, woutf_ref)
        halves = (d_model // 2, rows // 2)

        def part(t, core, q=None):
            size = halves[t] if q is None else halves[t] // 2
            start = core * halves[t] if q is None else core * halves[t] + q * size
            return pl.ds(pl.multiple_of(start, size), size)

        near = [(1 - x, y), (x, 1 - y)]
        far = (1 - x, 1 - y)
        chip = lambda px, py: 2 * px + py

        def direct(k, t, q, slot, to):
            src = stages[t].at[part(t, c, q)]
            return pltpu.make_async_remote_copy(src_ref=src, dst_ref=fulls[t].at[slot, part(t, c, q)], send_sem=near_send.at[k, t, q],
                                                recv_sem=near_recv.at[k, t, q], device_id=to, device_id_type=MESH)

        def passed_on(k, t, slot, to):
            ref = fulls[t].at[slot, part(t, c, k)]
            return pltpu.make_async_remote_copy(src_ref=ref, dst_ref=ref, send_sem=far_send.at[k, t], recv_sem=far_recv.at[k, t],
                                                device_id=to, device_id_type=MESH)

        def conv_copy(k, slot, to):
            return pltpu.make_async_remote_copy(src_ref=cw_ref, dst_ref=cwf_ref.at[slot], send_sem=cw_send.at[k], recv_sem=cw_recv.at[k],
                                                device_id=to, device_id_type=MESH)

        def d2d(k, t, slot, core):
            ref = fulls[t].at[slot, part(t, core)]
            return pltpu.make_async_remote_copy(src_ref=ref, dst_ref=ref, send_sem=d2d_send.at[k, t], recv_sem=d2d_recv.at[k, t],
                                                device_id=sib, device_id_type=MESH)

        sends = []

        def go(cp):
            cp.start()
            sends.append(cp)

        for q_first in (0, 1):
            for k, (px, py) in enumerate(near):
                for t in range(2):
                    go(direct(k, t, k if q_first == 0 else 1 - k, me, (px, py, c)))
        for k, (px, py) in enumerate(near + [far]):
            go(conv_copy(k, me, (px, py, c)))
        for k, (px, py) in enumerate(near):
            other = near[1 - k]
            for t in range(2):
                direct(k, t, k, chip(px, py), (px, py, c)).wait_recv()
                go(passed_on(k, t, chip(px, py), (*other, c)))

        first_half = lax.broadcasted_iota(jnp.int32, (chunk, LANES), 1) % HEAD_DIM < half_dim
        row = lax.broadcasted_iota(jnp.int32, (chunk, LANES), 0)

        def table_rows(i, carry):
            at = pl.multiple_of(i * chunk, chunk)
            ang = (row + at).astype(F32) * freq_ref[...]
            sin = jnp.sin(ang)
            cos_ref[pl.ds(at, chunk), :] = jnp.cos(ang)
            s1_ref[pl.ds(at, chunk), :] = jnp.where(first_half, -sin, 0.0)
            s2_ref[pl.ds(at, chunk), :] = jnp.where(first_half, 0.0, sin)
            return carry

        lax.fori_loop(0, seq // chunk, table_rows, 0)

        for k, (px, py) in enumerate(near):
            for t in range(2):
                direct(k, t, 1 - k, chip(px, py), (px, py, c)).wait_recv()
                go(d2d(k, t, chip(px, py), c))
        for t in range(2):
            for k, (px, py) in enumerate(near):
                passed_on(k, t, chip(*far), (px, py, c)).wait_recv()
            go(d2d(2, t, chip(*far), c))
        for k, (px, py) in enumerate(near + [far]):
            conv_copy(k, chip(px, py), (px, py, c)).wait_recv()
            for t in range(2):
                d2d(k, t, chip(px, py), 1 - c).wait_recv()
        for cp in sends:
            cp.wait_send()

    vm = pl.BlockSpec(memory_space=pltpu.VMEM)
    dma = pltpu.SemaphoreType.DMA
    return pl.pallas_call(
        body, name="gather_weights",
        out_shape=(jax.ShapeDtypeStruct((N_CHIPS, d_model, width), BF16),
                   jax.ShapeDtypeStruct((N_CHIPS, rows, d_model), BF16),
                   jax.ShapeDtypeStruct((N_CHIPS, SUBLANES, LANES), F32),
                   *[jax.ShapeDtypeStruct((seq, LANES), F32)] * 3),
        in_specs=[vm, vm, vm, vm], out_specs=(vm,) * 6,
        scratch_shapes=[pltpu.VMEM((d_model, width), BF16), pltpu.VMEM((rows, d_model), BF16),
                        dma((2, 2, 2)), dma((2, 2, 2)), dma((2, 2)), dma((2, 2)), dma((3,)), dma((3,)),
                        dma((3, 2)), dma((3, 2))],
        compiler_params=_params(),
    )(w_in, w_out, cw, inv_freq)


def _rope(t, cos, s1, s2):
    return t * cos + pltpu.roll(t, LANES - HEAD_DIM // 2, 1) * s1 + pltpu.roll(t, HEAD_DIM // 2, 1) * s2


def _rope_transposed(g, cos, s1, s2):
    return g * cos + pltpu.roll(g * s1, HEAD_DIM // 2, 1) + pltpu.roll(g * s2, LANES - HEAD_DIM // 2, 1)


def inproj(x, g1, w_full, tables, attn_w, conv_w):
    seq, d_model = x.shape
    width = w_full.shape[2]
    tm = ROW_TILE
    groups = _lane_groups(attn_w)

    def body(x_ref, g_ref, w_ref, cos_ref, s1_ref, s2_ref,
             ht_ref, q_ref, k_ref, v_ref, qp_ref, kp_ref, vp_ref, ga_ref, cz_ref, stage):
        xv = x_ref[...]
        hb = ((xv * lax.rsqrt(jnp.mean(xv * xv, axis=-1, keepdims=True) + NORM_EPS)) * g_ref[...]).astype(BF16)
        ht_ref[...] = jnp.transpose(hb)
        cos, s1, s2 = cos_ref[...], s1_ref[...], s2_ref[...]

        def proj(a, b):
            parts = [jnp.dot(hb, w_ref[j, :, lo:hi], preferred_element_type=F32) for j, lo, hi in _col_pieces(a, b, width)]
            return parts[0] if len(parts) == 1 else jnp.concatenate(parts, axis=1)

        def emit(z, nat_ref, perm_ref, fn):
            for g, sl in enumerate(groups):
                val = fn(z[:, sl])
                nat_ref[:, sl] = val.astype(BF16)
                _stage_put(stage, g, val)
            for g, sl in enumerate(groups):
                _to_perm(stage, g, perm_ref, sl, BF16)

        emit(proj(0, attn_w), q_ref, qp_ref, lambda t: _rope(t, cos, s1, s2) * ATTN_SCALE)
        emit(proj(attn_w, 2 * attn_w), k_ref, kp_ref, lambda t: _rope(t, cos, s1, s2))
        emit(proj(2 * attn_w, 3 * attn_w), v_ref, vp_ref, lambda t: t)
        ga_ref[...] = proj(3 * attn_w, 4 * attn_w)
        cz_ref[...] = proj(4 * attn_w, 4 * attn_w + 4 * conv_w)

    row = lambda n: pl.BlockSpec((tm, n), lambda i: (i, 0))
    nat = jax.ShapeDtypeStruct((seq, attn_w), BF16)
    perm = jax.ShapeDtypeStruct(_perm_shape(seq, attn_w), BF16)
    return pl.pallas_call(
        body, name="inproj", grid=(seq // tm,),
        out_shape=(jax.ShapeDtypeStruct((d_model, seq), BF16), nat, nat, nat, perm, perm, perm,
                   jax.ShapeDtypeStruct((seq, attn_w), F32), jax.ShapeDtypeStruct((seq, 4 * conv_w), F32)),
        in_specs=[row(d_model), _const_spec((1, d_model)), _const_spec(w_full.shape), row(LANES), row(LANES), row(LANES)],
        out_specs=(pl.BlockSpec((d_model, tm), lambda i: (0, i)), row(attn_w), row(attn_w), row(attn_w),
                   _perm_tile_spec(attn_w, tm), _perm_tile_spec(attn_w, tm), _perm_tile_spec(attn_w, tm),
                   row(attn_w), row(4 * conv_w)),
        scratch_shapes=[pltpu.VMEM(_stage_shape(len(groups), tm), F32)],
        compiler_params=_params(("arbitrary",)),
    )(x, g1, w_full, *tables)


class _Mode:
    def __init__(self, name, seq):
        self.name = name
        if name == "nat":
            self.residues, blocks = 1, seq // BLOCK
        elif name == "p16":
            self.residues, blocks = PERM, seq // PERM // BLOCK
        else:
            self.residues, blocks = PJ, seq // PERM // P4_ROWS
        self.qb = max(d for d in range(1, MAX_QUERY_BLOCKS + 1) if blocks % d == 0)
        self.steps = blocks // self.qb

    def _spec(self, blocks, width, index):
        if self.name == "nat":
            return pl.BlockSpec((blocks * BLOCK, width), lambda r, n: (index(n), 0))
        if self.name == "p16":
            return pl.BlockSpec((1, 1, blocks * BLOCK, width), lambda r, n: (r // PJ, r % PJ, index(n), 0))
        return pl.BlockSpec((PJ, 1, blocks * P4_ROWS, width), lambda r, n: (0, r, index(n), 0))

    def wide(self, width, last=None):
        return self._spec(self.qb, width, (lambda n: n) if last is None else (lambda n: jnp.minimum(n, last)))

    def wide_before(self, width):
        return self._spec(self.qb, width, lambda n: jnp.maximum(n - 1, 0))

    def block_before(self, width, last=None):
        step = (lambda n: n) if last is None else (lambda n: jnp.minimum(n, last))
        return self._spec(1, width, lambda n: jnp.maximum(self.qb * step(n) - 1, 0))

    def get(self, ref, sl, sub=0):
        if self.name == "nat":
            return ref[sub * BLOCK:(sub + 1) * BLOCK, sl]
        if self.name == "p16":
            return ref[0, 0, sub * BLOCK:(sub + 1) * BLOCK, sl]
        return jnp.concatenate([ref[j, 0, sub * P4_ROWS:(sub + 1) * P4_ROWS, sl] for j in range(PJ)], axis=0)

    def put(self, ref, sl, val, sub=0):
        val = val.astype(ref.dtype)
        if self.name == "nat":
            ref[sub * BLOCK:(sub + 1) * BLOCK, sl] = val
        elif self.name == "p16":
            ref[0, 0, sub * BLOCK:(sub + 1) * BLOCK, sl] = val
        else:
            for j in range(PJ):
                ref[j, 0, sub * P4_ROWS:(sub + 1) * P4_ROWS, sl] = val[j * P4_ROWS:(j + 1) * P4_ROWS]

    def keys(self, before_ref, wide_ref, sl, sub):
        older = self.get(before_ref, sl) if sub == 0 else self.get(wide_ref, sl, sub - 1)
        return jnp.concatenate([older, self.get(wide_ref, sl, sub)], axis=0)

    def index(self, idx, is_key):
        if self.name != "p4":
            return idx - BLOCK if is_key else idx
        within = jnp.bitwise_and(idx, BLOCK - 1)
        m = PJ * jnp.bitwise_and(within, P4_ROWS - 1) + jnp.right_shift(within, P4_ROWS.bit_length() - 1)
        return m + BLOCK * (jnp.right_shift(idx, BLOCK.bit_length() - 1) - 1) if is_key else m

    def bias(self, has_before):
        shape = (2 * BLOCK, BLOCK)
        kidx = lax.broadcasted_iota(jnp.int32, shape, 0)
        qidx = lax.broadcasted_iota(jnp.int32, shape, 1)
        rel = self.index(qidx, False) - self.index(kidx, True)
        valid = (rel >= 0) & (rel <= WINDOW_KEYS)
        if has_before is not True:
            valid = valid & ((kidx >= BLOCK) | has_before)
        one = jnp.where(valid, 0.0, NEG)
        return jnp.concatenate([one, one], axis=1)


def _head_masks():
    lane = lax.broadcasted_iota(jnp.int32, (BLOCK, LANES), 1)
    lo = lane < HEAD_DIM
    return lane, lo, jnp.where(lo, 1.0, 0.0).astype(BF16), jnp.where(lo, 0.0, 1.0).astype(BF16)


def _column(blk, lane, h):
    return jnp.sum(jnp.where(lane == h, blk, 0.0), axis=1, keepdims=True)


def attn_fwd(name, q, k, v, run):
    nat = name == "nat"
    seq = q.shape[0] if nat else q.shape[2] * PERM
    attn_w = q.shape[-1]
    mode = _Mode(name, seq)
    groups = _lane_groups(attn_w)
    first = run is None
    all_lanes = slice(0, LANES)

    def body(*refs):
        q_ref, kp_ref, kc_ref, vp_ref, vc_ref = refs[:5]
        if first:
            o_ref, l_ref = refs[5:]
        elif nat:
            oin_ref, lin_ref, ex_ref, o_ref, l_ref, ostage, lstage = refs[5:]
        else:
            oin_ref, lin_ref, ex_ref, o_ref, l_ref = refs[5:]
        n = pl.program_id(1)
        subs = range(mode.qb)
        biases = [mode.bias(n > 0)] + [mode.bias(True)] * (mode.qb - 1)
        _, lo, m_lo, m_hi = _head_masks()
        head_row = lax.broadcasted_iota(jnp.int32, (BLOCK, LANES), 0)
        ones = jnp.ones((2 * BLOCK, LANES), BF16)
        lrows = [jnp.zeros((BLOCK, LANES), F32) for _ in subs]

        def probs(sub, sl):
            q2 = mode.get(q_ref, sl, sub)
            kcat = mode.keys(kp_ref, kc_ref, sl, sub)
            vcat = mode.keys(vp_ref, vc_ref, sl, sub)
            qq = jnp.concatenate([q2 * m_lo, q2 * m_hi], axis=0)
            s_t = _nt(kcat, qq) + biases[sub]
            m = jnp.max(s_t, axis=0, keepdims=True)
            pe = jnp.exp(s_t - m)
            l = jnp.sum(pe, axis=0, keepdims=True)
            return jnp.concatenate([vcat, ones], axis=1), pe.astype(BF16), m + jnp.log(l)

        def output(sub, p, sl, vext, pb, lse):
            o_ext = _tn(pb, vext)
            o_new = o_ext[:, :LANES] / o_ext[:, LANES:]
            mode.put(o_ref, sl, jnp.where(lo, o_new[:BLOCK], o_new[BLOCK:]), sub)
            rows = jnp.where(head_row == 2 * p, lse[:, :BLOCK], lrows[sub])
            lrows[sub] = jnp.where(head_row == 2 * p + 1, lse[:, BLOCK:], rows)

        pending = None
        for sub in subs:
            for p, sl in enumerate(groups):
                nxt = probs(sub, sl)
                if pending is not None:
                    output(*pending)
                pending = (sub, p, sl, *nxt)
        output(*pending)
        if not first and nat:
            for g, sl in enumerate(groups):
                _from_perm(oin_ref, sl, ostage, g)
            _from_perm(lin_ref, all_lanes, lstage, 0)
        for sub in subs:
            rows = slice(sub * BLOCK, (sub + 1) * BLOCK)
            lblk = jnp.transpose(lrows[sub])
            if first:
                mode.put(l_ref, all_lanes, lblk, sub)
                continue
            lin = _stage_get(lstage, 0)[rows] if nat else mode.get(lin_ref, all_lanes, sub)
            mx = jnp.maximum(lin, lblk)
            new = mx + jnp.log(jnp.exp(lin - mx) + jnp.exp(lblk - mx))
            mode.put(l_ref, all_lanes, new, sub)

            def expand(w):
                hi = w.astype(BF16)
                rest = (w - hi.astype(F32)).astype(BF16)
                ex = ex_ref[...]
                return jnp.dot(hi, ex, preferred_element_type=F32) + jnp.dot(rest, ex, preferred_element_type=F32)

            w_prev, w_cur = expand(jnp.exp(lin - new)), expand(jnp.exp(lblk - new))
            for p, sl in enumerate(groups):
                o_prev = _stage_get(ostage, p)[rows] if nat else mode.get(oin_ref, sl, sub)
                mode.put(o_ref, sl, w_prev[:, sl] * o_prev + w_cur[:, sl] * mode.get(o_ref, sl, sub), sub)

    ins = [q, k, k, v, v]
    specs = [mode.wide(attn_w), mode.block_before(attn_w), mode.wide(attn_w), mode.block_before(attn_w), mode.wide(attn_w)]
    scratch = []
    if not first:
        ins += list(run)
        if nat:
            rows_a = mode.qb * BLOCK // PERM
            specs += [pl.BlockSpec((PJ, PJ, rows_a, attn_w), lambda r, n: (0, 0, n, 0)),
                      pl.BlockSpec((PJ, PJ, rows_a, LANES), lambda r, n: (0, 0, n, 0))]
            scratch = [pltpu.VMEM(_stage_shape(len(groups), mode.qb * BLOCK), F32),
                       pltpu.VMEM(_stage_shape(1, mode.qb * BLOCK), F32)]
        else:
            specs += [mode.wide(attn_w), mode.wide(LANES)]
        head_of_lane = jnp.arange(attn_w, dtype=jnp.int32) // HEAD_DIM
        ins.append((jnp.arange(LANES, dtype=jnp.int32)[:, None] == head_of_lane[None, :]).astype(BF16))
        specs.append(_const_spec((LANES, attn_w)))
    if nat:
        out_shape = (jax.ShapeDtypeStruct((seq, attn_w), F32), jax.ShapeDtypeStruct((seq, LANES), F32))
    else:
        out_shape = (jax.ShapeDtypeStruct(_perm_shape(seq, attn_w), F32), jax.ShapeDtypeStruct(_perm_shape(seq, LANES), F32))
    return pl.pallas_call(
        body, name=f"attn_fwd_{name}", grid=(mode.residues, mode.steps),
        out_shape=out_shape, in_specs=specs, out_specs=(mode.wide(attn_w), mode.wide(LANES)),
        scratch_shapes=scratch,
        compiler_params=_params(("arbitrary", "arbitrary")),
    )(*ins)


def attn_bwd(name, q, k, v, d_o, lse, delta, run):
    nat = name == "nat"
    seq = q.shape[0] if nat else q.shape[2] * PERM
    attn_w = q.shape[-1]
    mode = _Mode(name, seq)
    steps, qb = mode.steps, mode.qb
    single = steps == 1
    groups = _lane_groups(attn_w)
    first = run is None
    all_lanes = slice(0, LANES)

    def body(*refs):
        q_ref, kp_ref, kc_ref, vp_ref, vc_ref, do_ref, lse_ref, dl_ref = refs[:8]
        if first:
            dq_ref, dk_ref, dv_ref, ck, cv = refs[8:]
        else:
            dqi_ref, dki_ref, dvi_ref, dq_ref, dk_ref, dv_ref, ck, cv = refs[8:]
        n = pl.program_id(1)
        carries = ((ck, dk_ref, None if first else dki_ref), (cv, dv_ref, None if first else dvi_ref))

        def emit(out_ref, acc_ref, sl, sub, val):
            if acc_ref is not None:
                val = val + mode.get(acc_ref, sl, sub).astype(F32)
            mode.put(out_ref, sl, val, sub)

        if not single:
            @pl.when(n == 0)
            def _():
                ck[...] = jnp.zeros_like(ck)
                cv[...] = jnp.zeros_like(cv)

        @pl.when(n < steps)
        def _():
            biases = [mode.bias(n > 0)] + [mode.bias(True)] * (qb - 1)
            _, lo, m_lo, m_hi = _head_masks()

            def scores(sub, p, sl, lse_t, dl_t):
                q2, do2 = mode.get(q_ref, sl, sub), mode.get(do_ref, sl, sub)
                kcat = mode.keys(kp_ref, kc_ref, sl, sub)
                vcat = mode.keys(vp_ref, vc_ref, sl, sub)
                qq = jnp.concatenate([q2 * m_lo, q2 * m_hi], axis=0)
                dd = jnp.concatenate([do2 * m_lo, do2 * m_hi], axis=0)
                h0 = 2 * p
                lse2 = jnp.concatenate([lse_t[h0:h0 + 1, :], lse_t[h0 + 1:h0 + 2, :]], axis=1)
                dl2 = jnp.concatenate([dl_t[h0:h0 + 1, :], dl_t[h0 + 1:h0 + 2, :]], axis=1)
                p_t = jnp.exp(_nt(kcat, qq) + (biases[sub] - lse2))
                ds_t = p_t * (_nt(vcat, dd) - dl2)
                return qq, dd, kcat, p_t.astype(BF16), ds_t.astype(BF16)

            def grads(sub, sl, qq, dd, kcat, pb, dsb):
                dqb = _tn(dsb, kcat)
                dq2 = jnp.where(lo, dqb[:BLOCK], dqb[BLOCK:]) * ATTN_SCALE
                if not first:
                    dq2 = dq2 + mode.get(dqi_ref, sl, sub).astype(F32)
                mode.put(dq_ref, sl, dq2, sub)
                for (carry, out_ref, acc_ref), lhs, rhs in zip(carries, (dsb, pb), (qq, dd)):
                    both = jnp.dot(lhs, rhs, preferred_element_type=F32)
                    if sub == 0:
                        if not single:
                            for s in range(qb - 1):
                                emit(out_ref, acc_ref, sl, s, carry[s, :, sl])
                            emit(out_ref, acc_ref, sl, qb - 1, carry[qb - 1, :, sl] + both[:BLOCK])
                        carry[0, :, sl] = both[BLOCK:]
                    else:
                        carry[sub - 1, :, sl] += both[:BLOCK]
                        carry[sub, :, sl] = both[BLOCK:]
                    if single and sub == qb - 1:
                        for s in range(qb):
                            emit(out_ref, acc_ref, sl, s, carry[s, :, sl])

            stats = [(jnp.transpose(mode.get(lse_ref, all_lanes, sub)),
                      jnp.transpose(mode.get(dl_ref, all_lanes, sub))) for sub in range(qb)]
            pending = None
            for p, sl in enumerate(groups):
                for sub in range(qb):
                    nxt = scores(sub, p, sl, *stats[sub])
                    if pending is not None:
                        grads(*pending)
                    pending = (sub, sl, *nxt)
            grads(*pending)

        if not single:
            @pl.when(n == steps)
            def _():
                for carry, out_ref, acc_ref in carries:
                    for sl in groups:
                        for s in range(qb):
                            emit(out_ref, acc_ref, sl, s, carry[s, :, sl])

    last = steps - 1
    wide = lambda w: mode.wide(w, last)
    ins = [q, k, k, v, v, d_o, lse, delta]
    specs = [wide(attn_w), mode.block_before(attn_w, last), wide(attn_w), mode.block_before(attn_w, last), wide(attn_w),
             wide(attn_w), wide(LANES), wide(LANES)]
    if not first:
        ins += list(run)
        specs += [wide(attn_w), mode.wide_before(attn_w), mode.wide_before(attn_w)]
    shp = jax.ShapeDtypeStruct((seq, attn_w) if nat else _perm_shape(seq, attn_w), BF16)
    return pl.pallas_call(
        body, name=f"attn_bwd_{name}", grid=(mode.residues, steps if single else steps + 1),
        out_shape=(shp, shp, shp), in_specs=specs,
        out_specs=(wide(attn_w), mode.wide_before(attn_w), mode.wide_before(attn_w)),
        scratch_shapes=[pltpu.VMEM((qb, BLOCK, attn_w), F32), pltpu.VMEM((qb, BLOCK, attn_w), F32)],
        compiler_params=_params(("arbitrary", "arbitrary")),
    )(*ins)


def _shift_down(u, halo, k):
    rolled = pltpu.roll(u, k, 0)
    row = lax.broadcasted_iota(jnp.int32, halo.shape, 0)
    top = jnp.where(row < k, pltpu.roll(halo, k, 0), rolled[:SUBLANES])
    return jnp.concatenate([top, rolled[SUBLANES:]], axis=0)


def _shift_up(u, halo, k):
    rows = u.shape[0]
    rolled = pltpu.roll(u, rows - k, 0)
    row = lax.broadcasted_iota(jnp.int32, halo.shape, 0)
    bot = jnp.where(row >= SUBLANES - k, pltpu.roll(halo, SUBLANES - k, 0), rolled[rows - SUBLANES:])
    return jnp.concatenate([rolled[:rows - SUBLANES], bot], axis=0)


def tail(o, lse, ga, cz, x, tgt, w_out, g2, cw):
    seq, d_model = x.shape
    attn_w = o.shape[1]
    conv_w = cz.shape[1] // 4
    mix = attn_w + conv_w
    groups = _lane_groups(attn_w)
    tm = ROW_TILE
    nt = seq // tm
    hb = tm // SUBLANES

    def body(o_ref, l_ref, ga_ref, cz_ref, hz_ref, x_ref, t_ref, w_ref, g_ref, cw_ref,
             do_ref, dl_ref, dop_ref, dlp_ref, lp_ref, dga_ref, dcb_ref, dgc_ref, dcv_ref, e_ref,
             dw_ref, dg_ref, dcw_ref, loss_ref, stage):
        i = pl.program_id(0)

        @pl.when(i == 0)
        def _():
            dw_ref[...] = jnp.zeros_like(dw_ref)
            dg_ref[...] = jnp.zeros_like(dg_ref)
            dcw_ref[...] = jnp.zeros_like(dcw_ref)
            loss_ref[...] = jnp.zeros_like(loss_ref)

        u = cz_ref[:, 2 * conv_w:3 * conv_w] * cz_ref[:, 0:conv_w]
        uh = hz_ref[:, 2 * conv_w:3 * conv_w] * hz_ref[:, 0:conv_w]
        uh = jnp.where(i > 0, uh, 0.0)
        u1 = _shift_down(u, uh, 1)
        u2 = _shift_down(u, uh, 2)
        w0, w1, w2 = cw_ref[0:1, :], cw_ref[1:2, :], cw_ref[2:3, :]
        cvv = u2 * w0 + u1 * w1 + u * w2
        gv = g_ref[...]
        all_lanes = slice(0, LANES)

        def forward(rs):
            ov, gav = o_ref[rs, :], ga_ref[rs, :]
            sig_a = _sigmoid(gav)
            silu_a = gav * sig_a
            cb, gc = cz_ref[rs, conv_w:2 * conv_w], cz_ref[rs, 3 * conv_w:4 * conv_w]
            sig_c = _sigmoid(gc)
            silu_c = gc * sig_c
            bc = cb * cvv[rs]
            mixed = jnp.concatenate([ov * silu_a, bc * silu_c], axis=1).astype(BF16)
            yv = jnp.dot(mixed, w_ref[...], preferred_element_type=F32)
            return ov, gav, sig_a, silu_a, cb, gc, sig_c, silu_c, bc, mixed, yv

        def loss_and_dy(rs, mixed, yv):
            r2 = lax.rsqrt(jnp.mean(yv * yv, axis=-1, keepdims=True) + NORM_EPS)
            yhat = yv * r2
            diff = (x_ref[rs, :] + yhat * gv) - t_ref[rs, :]
            loss_ref[...] += _rowgroup_sum(diff * diff)
            ev = diff * (1.0 / d_model)
            e_ref[rs, :] = ev
            dg_ref[...] += _rowgroup_sum(ev * yhat)
            eg = ev * gv
            dy = (r2 * (eg - yhat * jnp.mean(eg * yhat, axis=-1, keepdims=True))).astype(BF16)
            dw_ref[...] += _tn(mixed, dy)
            return _nt(dy, w_ref[...])

        def backward(rs, ov, gav, sig_a, silu_a, cb, gc, sig_c, silu_c, bc, dm):
            rows = rs.stop - rs.start
            dma, dmc = dm[:, :attn_w], dm[:, attn_w:]
            dov = dma * silu_a
            do_ref[rs, :] = dov.astype(BF16)
            dga_ref[rs, :] = (dma * ov * (sig_a * (1.0 + gav * (1.0 - sig_a)))).astype(BF16)
            prod = dov * ov
            lane = lax.broadcasted_iota(jnp.int32, (rows, LANES), 1)
            lo = lane < HEAD_DIM
            dblk = jnp.zeros((rows, LANES), F32)
            for p, sl in enumerate(groups):
                pr = prod[:, sl]
                dblk = jnp.where(lane == 2 * p, jnp.sum(jnp.where(lo, pr, 0.0), axis=1, keepdims=True), dblk)
                dblk = jnp.where(lane == 2 * p + 1, jnp.sum(jnp.where(lo, 0.0, pr), axis=1, keepdims=True), dblk)
                _stage_put(stage, p, dov[:, sl], rs.start)
            dl_ref[rs, :] = dblk
            _stage_put(stage, len(groups), dblk, rs.start)
            _stage_put(stage, len(groups) + 1, l_ref[rs, :], rs.start)
            dsc = dmc * silu_c
            cv_rows = cvv[rs]
            dcb_ref[rs, :] = (dsc * cv_rows).astype(BF16)
            dgc_ref[rs, :] = (dmc * bc * (sig_c * (1.0 + gc * (1.0 - sig_c)))).astype(BF16)
            dcv = dsc * cb
            dcv_ref[rs, :] = dcv
            dcw_ref[0:SUBLANES, :] += _rowgroup_sum(dcv * u2[rs])
            dcw_ref[SUBLANES:2 * SUBLANES, :] += _rowgroup_sum(dcv * u1[rs])
            dcw_ref[2 * SUBLANES:3 * SUBLANES, :] += _rowgroup_sum(dcv * u[rs])

        halves = [slice(0, tm // 2), slice(tm // 2, tm)]
        fwd = [forward(rs) for rs in halves]
        dms = [loss_and_dy(rs, f[9], f[10]) for rs, f in zip(halves, fwd)]
        for rs, f, dm in zip(halves, fwd, dms):
            backward(rs, *f[:9], dm)
        for p, sl in enumerate(groups):
            _to_perm(stage, p, dop_ref, sl, BF16)
        _to_perm(stage, len(groups), dlp_ref, all_lanes, F32)
        _to_perm(stage, len(groups) + 1, lp_ref, all_lanes, F32)

    row = lambda n: pl.BlockSpec((tm, n), lambda i: (i, 0))
    whole = lambda a, b: pl.BlockSpec((a, b), lambda i: (0, 0))
    return pl.pallas_call(
        body, name="tail", grid=(nt,),
        out_shape=(jax.ShapeDtypeStruct((seq, attn_w), BF16), jax.ShapeDtypeStruct((seq, LANES), F32),
                   jax.ShapeDtypeStruct(_perm_shape(seq, attn_w), BF16), jax.ShapeDtypeStruct(_perm_shape(seq, LANES), F32),
                   jax.ShapeDtypeStruct(_perm_shape(seq, LANES), F32),
                   jax.ShapeDtypeStruct((seq, attn_w), BF16), jax.ShapeDtypeStruct((seq, conv_w), BF16),
                   jax.ShapeDtypeStruct((seq, conv_w), BF16), jax.ShapeDtypeStruct((seq, conv_w), F32),
                   jax.ShapeDtypeStruct((seq, d_model), F32), jax.ShapeDtypeStruct((mix, d_model), F32),
                   jax.ShapeDtypeStruct((SUBLANES, d_model), F32), jax.ShapeDtypeStruct((CONV_K * SUBLANES, conv_w), F32),
                   jax.ShapeDtypeStruct((SUBLANES, d_model), F32)),
        in_specs=[row(attn_w), row(LANES), row(attn_w), row(4 * conv_w),
                  pl.BlockSpec((SUBLANES, 4 * conv_w), lambda i: (jnp.maximum(i * hb - 1, 0), 0)),
                  row(d_model), row(d_model), _const_spec((mix, d_model)), _const_spec((1, d_model)),
                  _const_spec((SUBLANES, conv_w))],
        out_specs=(row(attn_w), row(LANES), _perm_tile_spec(attn_w, tm), _perm_tile_spec(LANES, tm), _perm_tile_spec(LANES, tm),
                   row(attn_w), row(conv_w), row(conv_w), row(conv_w), row(d_model),
                   whole(mix, d_model), whole(SUBLANES, d_model), whole(CONV_K * SUBLANES, conv_w),
                   whole(SUBLANES, d_model)),
        scratch_shapes=[pltpu.VMEM(_stage_shape(len(groups) + 2, tm), F32)],
        compiler_params=_params(("arbitrary",)),
    )(o, lse, ga, cz, cz, x, tgt, w_out, g2, cw)


def dz_dx(nat_grads, perm_grads, dga, dcb, dgc, dcv, cz, tables, x, g1, e, w_full, cw):
    seq, d_model = x.shape
    attn_w = dga.shape[1]
    conv_w = dcv.shape[1]
    width = w_full.shape[2]
    in_w = 4 * attn_w + 4 * conv_w
    groups = _lane_groups(attn_w)
    tm = DZ_ROW_TILE
    nt = seq // tm
    hb = tm // SUBLANES

    def body(dq_ref, dk_ref, dv_ref, dqp_ref, dkp_ref, dvp_ref, dga_ref, dcb_ref, dgc_ref, dcv_ref, nh_ref, cz_ref,
             cos_ref, s1_ref, s2_ref, x_ref, g_ref, e_ref, w_ref, cw_ref, gx_ref, dz_ref, dg_ref, stage):
        i = pl.program_id(0)

        @pl.when(i == 0)
        def _():
            dg_ref[...] = jnp.zeros_like(dg_ref)

        cos, s1, s2 = cos_ref[...], s1_ref[...], s2_ref[...]
        for t, (nat_ref, perm_ref) in enumerate(((dq_ref, dqp_ref), (dk_ref, dkp_ref), (dv_ref, dvp_ref))):
            for g, sl in enumerate(groups):
                _from_perm(perm_ref, sl, stage, g)
            for g, sl in enumerate(groups):
                tot = nat_ref[:, sl].astype(F32) + _stage_get(stage, g)
                if t < 2:
                    tot = _rope_transposed(tot, cos, s1, s2)
                dz_ref[:, t * attn_w + g * LANES:t * attn_w + (g + 1) * LANES] = tot.astype(BF16)
        dz_ref[:, 3 * attn_w:4 * attn_w] = dga_ref[...]
        dcv = dcv_ref[...]
        nh = jnp.where(i < nt - 1, nh_ref[...], 0.0)
        w0, w1, w2 = cw_ref[0:1, :], cw_ref[1:2, :], cw_ref[2:3, :]
        du = dcv * w2 + _shift_up(dcv, nh, 1) * w1 + _shift_up(dcv, nh, 2) * w0
        base = 4 * attn_w
        dz_ref[:, base:base + conv_w] = (du * cz_ref[:, 2 * conv_w:3 * conv_w]).astype(BF16)
        dz_ref[:, base + conv_w:base + 2 * conv_w] = dcb_ref[...]
        dz_ref[:, base + 2 * conv_w:base + 3 * conv_w] = (du * cz_ref[:, 0:conv_w]).astype(BF16)
        dz_ref[:, base + 3 * conv_w:base + 4 * conv_w] = dgc_ref[...]

        dh = _nt(dz_ref[:, 0:width], w_ref[0])
        for j in range(1, N_CHIPS):
            dh = dh + _nt(dz_ref[:, j * width:(j + 1) * width], w_ref[j])
        xv = x_ref[...]
        r1 = lax.rsqrt(jnp.mean(xv * xv, axis=-1, keepdims=True) + NORM_EPS)
        xhat = xv * r1
        dg_ref[...] += _rowgroup_sum(dh * xhat)
        dhg = dh * g_ref[...]
        gx_ref[...] = r1 * (dhg - xhat * jnp.mean(dhg * xhat, axis=-1, keepdims=True)) + e_ref[...]

    row = lambda n: pl.BlockSpec((tm, n), lambda i: (i, 0))
    whole = lambda a, b: pl.BlockSpec((a, b), lambda i: (0, 0))
    pt = _perm_tile_spec(attn_w, tm)
    return pl.pallas_call(
        body, name="dz_dx", grid=(nt,),
        out_shape=(jax.ShapeDtypeStruct((seq, d_model), F32), jax.ShapeDtypeStruct((seq, in_w), BF16),
                   jax.ShapeDtypeStruct((SUBLANES, d_model), F32)),
        in_specs=[row(attn_w), row(attn_w), row(attn_w), pt, pt, pt, row(attn_w), row(conv_w), row(conv_w), row(conv_w),
                  pl.BlockSpec((SUBLANES, conv_w), lambda i: (jnp.minimum((i + 1) * hb, seq // SUBLANES - 1), 0)),
                  row(4 * conv_w), row(LANES), row(LANES), row(LANES), row(d_model), _const_spec((1, d_model)), row(d_model),
                  _const_spec(w_full.shape), _const_spec((SUBLANES, conv_w))],
        out_specs=(row(d_model), row(in_w), whole(SUBLANES, d_model)),
        scratch_shapes=[pltpu.VMEM(_stage_shape(len(groups), tm), F32)],
        compiler_params=_params(("arbitrary",)),
    )(*nat_grads, *perm_grads, dga, dcb, dgc, dcv, dcv, cz, *tables, x, g1, e, w_full, cw)


def dw_in_reduce(ht, dz):
    d_model, seq = ht.shape
    half = dz.shape[1] // N_DEV
    ts = min(2048, seq)
    steps = seq // ts
    x, y, c = lax.axis_index("x"), lax.axis_index("y"), lax.axis_index("c")
    far_first = lambda x, y: [(1 - x, 1 - y), (1 - x, y), (x, 1 - y)]
    chips = jnp.stack([2 * px + py for px, py in far_first(x, y)] + [2 * x + y]).astype(jnp.int32)
    order = jnp.stack([2 * chips + (1 - c), 2 * chips + c], axis=1).reshape(N_DEV)

    def body(order_ref, ht_ref, dz_ref, out_ref, acc, theirs, staged, contrib, resbuf, out_sem, sa, ra, sb, rb, sc, rc):
        del order_ref
        p, s = pl.program_id(0), pl.program_id(1)
        x, y, c = lax.axis_index("x"), lax.axis_index("y"), lax.axis_index("c")
        sib = (x, y, 1 - c)
        peers = far_first(x, y)
        slot = p % 2

        def a_copy(k):
            return pltpu.make_async_remote_copy(src_ref=acc.at[0], dst_ref=theirs.at[k], send_sem=sa.at[k], recv_sem=ra.at[k],
                                                device_id=sib, device_id_type=MESH)

        def b_copy(k):
            px, py = peers[k]
            return pltpu.make_async_remote_copy(src_ref=staged.at[k], dst_ref=contrib.at[k], send_sem=sb.at[k], recv_sem=rb.at[k],
                                                device_id=(px, py, c), device_id_type=MESH)

        def c_copy(which):
            return pltpu.make_async_remote_copy(src_ref=resbuf.at[which], dst_ref=resbuf.at[which], send_sem=sc, recv_sem=rc,
                                                device_id=sib, device_id_type=MESH)

        @pl.when(s == 0)
        def _():
            for k in range(N_CHIPS - 1):
                @pl.when(p == 2 * k + 2)
                def _():
                    a_copy(k).wait_send()
            acc[slot] = jnp.zeros((d_model, half), F32)

        acc[slot] += jnp.dot(ht_ref[...], dz_ref[...], preferred_element_type=F32)

        @pl.when(s == steps - 1)
        def _():
            for k in range(N_CHIPS):
                @pl.when(p == 2 * k)
                def _():
                    a_copy(k).start()
            for k in range(N_CHIPS - 1):
                @pl.when(p == 2 * k + 1)
                def _():
                    a_copy(k).wait_recv()
                    staged[k] = (acc[1] + theirs[k]).astype(BF16)
                    b_copy(k).start()

            @pl.when(p == N_DEV - 1)
            def _():
                a_copy(N_CHIPS - 1).wait_recv()
                tot = acc[1] + theirs[N_CHIPS - 1]
                for k in range(N_CHIPS - 1):
                    b_copy(k).wait_recv()
                    tot = tot + contrib[k].astype(F32)
                resbuf[c] = tot
                c_copy(c).start()
                c_copy(1 - c).wait_recv()
                done = pltpu.make_async_copy(resbuf, out_ref, out_sem)
                done.start()
                a_copy(N_CHIPS - 1).wait_send()
                for k in range(N_CHIPS - 1):
                    b_copy(k).wait_send()
                c_copy(c).wait_send()
                done.wait()

    dma = pltpu.SemaphoreType.DMA
    grid_spec = pltpu.PrefetchScalarGridSpec(
        num_scalar_prefetch=1, grid=(N_DEV, steps),
        in_specs=[pl.BlockSpec((d_model, ts), lambda p, s, order_ref: (0, s)),
                  pl.BlockSpec((ts, half), lambda p, s, order_ref: (s, order_ref[p]))],
        out_specs=pl.BlockSpec(memory_space=pl.ANY),
        scratch_shapes=[pltpu.VMEM((2, d_model, half), F32), pltpu.VMEM((N_CHIPS, d_model, half), F32),
                        pltpu.VMEM((N_CHIPS - 1, d_model, half), BF16), pltpu.VMEM((N_CHIPS - 1, d_model, half), BF16),
                        pltpu.VMEM((2, d_model, half), F32), dma,
                        dma((N_CHIPS,)), dma((N_CHIPS,)), dma((N_CHIPS - 1,)), dma((N_CHIPS - 1,)), dma, dma])
    return pl.pallas_call(
        body, name="dw_in_reduce", grid_spec=grid_spec,
        out_shape=jax.ShapeDtypeStruct((2, d_model, half), F32),
        compiler_params=_params(("arbitrary", "arbitrary")),
    )(order, ht, dz)


def grad_reduce(tensors, small):
    nt = len(tensors)
    split = [g.reshape(N_CHIPS, 2, *g.shape[1:]) for g in tensors]
    shapes = [g.shape[2:] for g in split]

    def body(*refs):
        srcs, sm_ref = refs[:nt], refs[nt]
        res, rs_ref = refs[nt + 1:2 * nt + 1], refs[2 * nt + 1]
        scratch = refs[2 * nt + 2:]
        mine, theirs, staged, contrib = (scratch[k * nt:(k + 1) * nt] for k in range(4))
        sbuf, loc_sems, sa, ra, sb, rb, sc, rc, ss, rs = scratch[4 * nt:]
        x, y, c = lax.axis_index("x"), lax.axis_index("y"), lax.axis_index("c")
        me = 2 * x + y
        sib = (x, y, 1 - c)

        flips = [(fx, fy, fc) for fx in (0, 1) for fy in (0, 1) for fc in (0, 1)][1:]
        my8 = 4 * x + 2 * y + c
        sbuf[my8] = sm_ref[...]

        def small_copy(k, slot, to):
            return pltpu.make_async_remote_copy(src_ref=sm_ref, dst_ref=sbuf.at[slot], send_sem=ss.at[k], recv_sem=rs.at[k],
                                                device_id=to, device_id_type=MESH)

        sends = []
        for k, (fx, fy, fc) in enumerate(flips):
            px, py, pc = _flip(x, fx), _flip(y, fy), _flip(c, fc)
            sends.append(small_copy(k, my8, (px, py, pc)))
            sends[-1].start()

        def a_copy(t, j):
            return pltpu.make_async_remote_copy(src_ref=srcs[t].at[j, 1 - c], dst_ref=theirs[t].at[j], send_sem=sa.at[t, j],
                                                recv_sem=ra.at[t, j], device_id=sib, device_id_type=MESH)

        peers = _chip_peers(x, y)
        order = [2 * px + py for px, py in peers] + [me]
        loads = [[pltpu.make_async_copy(srcs[t].at[j, c], mine[t].at[j], loc_sems.at[t, j]) for t in range(nt)] for j in order]
        for pos, j in enumerate(order):
            for t in range(nt):
                loads[pos][t].start()
                sends.append(a_copy(t, j))
                sends[-1].start()

        def b_copy(k, t, piece, slot, to):
            return pltpu.make_async_remote_copy(src_ref=staged[t].at[piece], dst_ref=contrib[t].at[slot], send_sem=sb.at[k, t],
                                                recv_sem=rb.at[k, t], device_id=to, device_id_type=MESH)

        for k, (px, py) in enumerate(peers):
            j = 2 * px + py
            for t in range(nt):
                loads[k][t].wait()
                a_copy(t, j).wait_recv()
                staged[t][j] = (mine[t][j] + theirs[t][j]).astype(BF16)
                sends.append(b_copy(k, t, j, me, (px, py, c)))
                sends[-1].start()
        for t in range(nt):
            loads[len(peers)][t].wait()
            a_copy(t, me).wait_recv()
            mine[t][me] = mine[t][me] + theirs[t][me]
            contrib[t][me] = mine[t][me].astype(BF16)
        for k, (px, py) in enumerate(peers):
            for t in range(nt):
                b_copy(k, t, me, 2 * px + py, (px, py, c)).wait_recv()

        def c_copy(t, half):
            return pltpu.make_async_remote_copy(src_ref=res[t].at[half], dst_ref=res[t].at[half], send_sem=sc.at[t],
                                                recv_sem=rc.at[t], device_id=sib, device_id_type=MESH)

        for t in range(nt):
            own = mine[t][me]
            term = lambda j: jnp.where(me == j, own, contrib[t][j].astype(F32))
            res[t][c] = ((term(0) + term(1)) + term(2)) + term(3)
            sends.append(c_copy(t, c))
            sends[-1].start()
        for t in range(nt):
            c_copy(t, 1 - c).wait_recv()

        for k, (fx, fy, fc) in enumerate(flips):
            px, py, pc = _flip(x, fx), _flip(y, fy), _flip(c, fc)
            small_copy(k, 4 * px + 2 * py + pc, (px, py, pc)).wait_recv()
        tot = sbuf[0]
        for d in range(1, N_DEV):
            tot = tot + sbuf[d]
        rs_ref[...] = tot
        for cp in sends:
            cp.wait_send()

    vm = pl.BlockSpec(memory_space=pltpu.VMEM)
    anyspace = pl.BlockSpec(memory_space=pl.ANY)
    dma = pltpu.SemaphoreType.DMA
    bufs = [pltpu.VMEM((N_CHIPS, *shp), dt) for dt in (F32, F32, BF16, BF16) for shp in shapes]
    outs = pl.pallas_call(
        body, name="grad_reduce",
        out_shape=(*[jax.ShapeDtypeStruct((2, *shp), F32) for shp in shapes], jax.ShapeDtypeStruct(small.shape, F32)),
        in_specs=[anyspace] * nt + [vm], out_specs=tuple([vm] * (nt + 1)),
        scratch_shapes=[*bufs, pltpu.VMEM((N_DEV, *small.shape), F32),
                        dma((nt, N_CHIPS)), dma((nt, N_CHIPS)), dma((nt, N_CHIPS)), dma((3, nt)), dma((3, nt)), dma((nt,)), dma((nt,)),
                        dma((N_DEV - 1,)), dma((N_DEV - 1,))],
        compiler_params=_params(),
    )(*split, small)
    return outs[:nt], outs[nt]


def _adam_math(w, g, m, v):
    m = ADAM_B1 * m + (1.0 - ADAM_B1) * g
    v = ADAM_B2 * v + (1.0 - ADAM_B2) * (g * g)
    m_hat = m / (1.0 - ADAM_B1 ** ADAM_STEP)
    v_hat = v / (1.0 - ADAM_B2 ** ADAM_STEP)
    delta = -ADAM_LR * (m_hat / (jnp.sqrt(v_hat) + ADAM_EPS) + ADAM_WD * w)
    return delta, m, v


def adam_shard(name, w, g2, m, v, block, grid, w_map, g_map):
    def body(w_ref, g_ref, m_ref, v_ref, go_ref, d_ref, mo_ref, vo_ref):
        g = g_ref[0]
        delta, mn, vn = _adam_math(w_ref[...], g, m_ref[...], v_ref[...])
        go_ref[...] = g
        d_ref[...] = delta
        mo_ref[...] = mn
        vo_ref[...] = vn

    ws = pl.BlockSpec(block, w_map)
    shp = jax.ShapeDtypeStruct(w.shape, F32)
    return pl.pallas_call(
        body, name=name, grid=grid, out_shape=(shp, shp, shp, shp),
        in_specs=[ws, pl.BlockSpec((1, *block), g_map), ws, ws], out_specs=(ws, ws, ws, ws),
        compiler_params=_params(("arbitrary",) * len(grid)),
    )(w, g2, m, v)


def adam_small(ws, gs, ms, vs):
    n = len(ws)

    def body(*refs):
        ins, outs = refs[:4 * n], refs[4 * n:]
        for t in range(n):
            delta, mn, vn = _adam_math(ins[t][...], ins[n + t][...], ins[2 * n + t][...], ins[3 * n + t][...])
            outs[3 * t][...] = delta
            outs[3 * t + 1][...] = mn
            outs[3 * t + 2][...] = vn

    vm = pl.BlockSpec(memory_space=pltpu.VMEM)
    outs = pl.pallas_call(
        body, name="adam_small",
        out_shape=tuple(jax.ShapeDtypeStruct(w.shape, F32) for w in ws for _ in range(3)),
        in_specs=[vm] * (4 * n), out_specs=tuple([vm] * (3 * n)),
        compiler_params=_params(),
    )(*ws, *gs, *ms, *vs)
    return [outs[3 * t:3 * t + 3] for t in range(n)]


def kernel(x, norm_pre_g, w_in, conv_w, w_out, norm_post_g, loss_target, m_norm_pre_g, m_w_in, m_conv_w, m_w_out, m_norm_post_g, v_norm_pre_g, v_w_in, v_conv_w, v_w_out, v_norm_post_g):
    _, seq, d_model = x.shape
    width = w_in.shape[1]
    conv_q = conv_w.shape[1]
    conv_width = N_CHIPS * conv_q
    attn_width = d_model - conv_width
    xs, tg = x[0], loss_target[0]
    g1, g2 = norm_pre_g.reshape(1, d_model), norm_post_g.reshape(1, d_model)

    w_full, wout_full, cw_full, *tables = gather_weights(w_in, w_out, conv_w, seq)
    wout2 = wout_full.reshape(attn_width + conv_width, d_model)
    cw = jnp.zeros((SUBLANES, conv_width), F32).at[:CONV_K].set(
        cw_full[:, :CONV_K, :conv_q].transpose(1, 0, 2).reshape(CONV_K, conv_width))

    ht, q, k, v, qp, kp, vp, ga, cz = inproj(xs, g1, w_full, tables, attn_width, conv_width)
    run = attn_fwd("p4", qp, kp, vp, None)
    run = attn_fwd("p16", qp, kp, vp, run)
    o, lse = attn_fwd("nat", q, k, v, run)
    (d_o, delta, d_op, delta_p, lse_p, dga, dcb, dgc, dcv, e, dwout, dg2, dcw, loss_acc) = tail(
        o, lse, ga, cz, xs, tg, wout2, g2, cw)
    nat_grads = attn_bwd("nat", q, k, v, d_o, lse, delta, None)
    perm_grads = attn_bwd("p4", qp, kp, vp, d_op, lse_p, delta_p, None)
    perm_grads = attn_bwd("p16", qp, kp, vp, d_op, lse_p, delta_p, perm_grads)
    grad_x, dz, dg1 = dz_dx(nat_grads, perm_grads, dga, dcb, dgc, dcv, cz, tables, xs, g1, e, w_full, cw)

    small = jnp.zeros((SUBLANES, d_model), F32)
    small = small.at[0].set(dg1.sum(axis=0)).at[1].set(dg2.sum(axis=0))
    small = small.at[2:2 + CONV_K, :conv_width].set(dcw.reshape(CONV_K, SUBLANES, conv_width).sum(axis=1))
    small = small.at[2 + CONV_K, 0].set(jnp.sum(loss_acc))
    (rout,), rsmall = grad_reduce([dwout.reshape(N_DEV, -1, d_model)], small)
    rin = dw_in_reduce(ht, dz)

    half = width // 2
    tr = 256
    gw_in, d_in, m_in, v_in = adam_shard(
        "adam_w_in", w_in, rin, m_w_in, v_w_in, (tr, half), (2, d_model // tr),
        lambda hf, i: (i, hf), lambda hf, i: (hf, i, 0))
    rq = w_out.shape[0] // 2
    gw_out, d_out, m_out, v_out = adam_shard(
        "adam_w_out", w_out, rout, m_w_out, v_w_out, (rq, d_model), (2,),
        lambda hf: (hf, 0), lambda hf: (hf, 0, 0))

    chip = 2 * lax.axis_index("x") + lax.axis_index("y")
    g_pre, g_post = rsmall[0:1], rsmall[1:2]
    g_conv = lax.dynamic_slice(rsmall[2:2 + CONV_K, :conv_width], (0, chip * conv_q), (CONV_K, conv_q))
    (d_pre, m_pre, v_pre), (d_post, m_post, v_post), (d_cv, m_cv, v_cv) = adam_small(
        [g1, g2, conv_w], [g_pre, g_post, g_conv],
        [m_norm_pre_g.reshape(1, d_model), m_norm_post_g.reshape(1, d_model), m_conv_w],
        [v_norm_pre_g.reshape(1, d_model), v_norm_post_g.reshape(1, d_model), v_conv_w])

    loss = 0.5 * rsmall[2 + CONV_K, 0] / d_model
    vec = lambda a: a.reshape(d_model)
    return (loss, grad_x.reshape(1, seq, d_model),
            vec(g_pre), gw_in, g_conv, gw_out, vec(g_post),
            vec(d_pre), d_in, d_cv, d_out, vec(d_post),
            vec(m_pre), m_in, m_cv, m_out, vec(m_post),
            vec(v_pre), v_in, v_cv, v_out, vec(v_post))
```

```python
import jax
import jax.numpy as jnp
from jax import lax
from jax.experimental import pallas as pl
from jax.experimental.pallas import tpu as pltpu

HEAD_DIM = 64
LANES = 128
SUBLANES = 8
BLOCK = 128
WINDOW_KEYS = 128
PERM = 16
PJ = 4
P4_ROWS = BLOCK // PJ
MAX_QUERY_BLOCKS = 4
ROW_TILE = 512
DZ_ROW_TILE = 512
CONV_K = 3
ROPE_THETA = 10000.0
NORM_EPS = 1e-6
ATTN_SCALE = HEAD_DIM ** -0.5
NEG = -1e30
N_CHIPS = 4
N_DEV = 8
MESH = pl.DeviceIdType.MESH
ADAM_LR = 0.001
ADAM_B1 = 0.9
ADAM_B2 = 0.999
ADAM_EPS = 1e-08
ADAM_WD = 0.01
ADAM_STEP = 10
VMEM_LIMIT = 52 * 1024 * 1024

F32 = jnp.float32
BF16 = jnp.bfloat16


def _params(sem=None, **kw):
    return pltpu.CompilerParams(dimension_semantics=sem, vmem_limit_bytes=VMEM_LIMIT, **kw)


def _const_spec(shape):
    return pl.BlockSpec(shape, lambda *_: (0,) * len(shape), pipeline_mode=pl.Buffered(1))


def _sigmoid(z):
    return 1.0 / (1.0 + jnp.exp(-z))


def _rowgroup_sum(a):
    rows, n = a.shape
    return a.reshape(rows // SUBLANES, SUBLANES, n).sum(axis=0)


def _nt(a, b):
    return lax.dot_general(a, b, (((1,), (1,)), ((), ())), preferred_element_type=F32)


def _tn(a, b):
    return lax.dot_general(a, b, (((0,), (0,)), ((), ())), preferred_element_type=F32)


def _col_pieces(a, b, width):
    out = []
    while a < b:
        j = a // width
        e = min(b, (j + 1) * width)
        out.append((j, a - j * width, e - j * width))
        a = e
    return out


def _lane_groups(width):
    return [slice(g * LANES, (g + 1) * LANES) for g in range(width // LANES)]


def _perm_shape(seq, width):
    return (PJ, PJ, seq // PERM, width)


def _perm_tile_spec(width, tm):
    return pl.BlockSpec((PJ, PJ, tm // PERM, width), lambda i: (0, 0, i, 0))


STAGE_PITCH = 24


def _stage_shape(groups, rows):
    return (groups, rows // PERM * STAGE_PITCH, LANES)


def _stage_put(stage, g, val, row0=0):
    for a in range(val.shape[0] // PERM):
        at = (row0 // PERM + a) * STAGE_PITCH
        stage[g, at:at + PERM, :] = val[a * PERM:(a + 1) * PERM]


def _stage_get(stage, g):
    return jnp.concatenate([stage[g, a * STAGE_PITCH:a * STAGE_PITCH + PERM, :]
                            for a in range(stage.shape[1] // STAGE_PITCH)], axis=0)


def _to_perm(stage, g, dst_ref, sl, dtype):
    rows = stage.shape[1] // STAGE_PITCH
    for b in range(PERM):
        dst_ref[b // PJ, b % PJ, :, sl] = stage[g, pl.ds(b, rows, stride=STAGE_PITCH), :].astype(dtype)


def _from_perm(src_ref, sl, stage, g):
    rows = stage.shape[1] // STAGE_PITCH
    for b in range(PERM):
        stage[g, pl.ds(b, rows, stride=STAGE_PITCH), :] = src_ref[b // PJ, b % PJ, :, sl].astype(F32)


def _flip(a, f):
    return 1 - a if f else a


def _chip_peers(x, y):
    return [(1 - x, y), (x, 1 - y), (1 - x, 1 - y)]


def gather_weights(w_in, w_out, conv_w, seq):
    d_model, width = w_in.shape
    rows = w_out.shape[0]
    cw = jnp.zeros((SUBLANES, LANES), F32).at[:CONV_K, :conv_w.shape[1]].set(conv_w)
    half_dim = HEAD_DIM // 2
    inv_freq = ROPE_THETA ** (-jnp.arange(half_dim, dtype=F32) * 2.0 / HEAD_DIM)
    inv_freq = jnp.tile(inv_freq, LANES // half_dim).reshape(1, LANES)
    chunk = min(ROW_TILE, seq)

    def body(win_ref, wout_ref, cw_ref, freq_ref, winf_ref, woutf_ref, cwf_ref, cos_ref, s1_ref, s2_ref,
             st_in, st_out, near_send, near_recv, far_send, far_recv, cw_send, cw_recv, d2d_send, d2d_recv):
        x, y, c = lax.axis_index("x"), lax.axis_index("y"), lax.axis_index("c")
        me = 2 * x + y
        sib = (x, y, 1 - c)
        st_in[...] = win_ref[...].astype(BF16)
        st_out[...] = wout_ref[...].astype(BF16)
        winf_ref[me] = st_in[...]
        woutf_ref[me] = st_out[...]
        cwf_ref[me] = cw_ref[...]
        stages = (st_in, st_out)
        fulls = (winf_ref, woutf_ref)
        halves = (d_model // 2, rows // 2)

        def part(t, core, q=None):
            size = halves[t] if q is None else halves[t] // 2
            start = core * halves[t] if q is None else core * halves[t] + q * size
            return pl.ds(pl.multiple_of(start, size), size)

        near = [(1 - x, y), (x, 1 - y)]
        far = (1 - x, 1 - y)
        chip = lambda px, py: 2 * px + py

        def direct(k, t, q, slot, to):
            src = stages[t].at[part(t, c, q)]
            return pltpu.make_async_remote_copy(src_ref=src, dst_ref=fulls[t].at[slot, part(t, c, q)], send_sem=near_send.at[k, t, q],
                                                recv_sem=near_recv.at[k, t, q], device_id=to, device_id_type=MESH)

        def passed_on(k, t, slot, to):
            ref = fulls[t].at[slot, part(t, c, k)]
            return pltpu.make_async_remote_copy(src_ref=ref, dst_ref=ref, send_sem=far_send.at[k, t], recv_sem=far_recv.at[k, t],
                                                device_id=to, device_id_type=MESH)

        def conv_copy(k, slot, to):
            return pltpu.make_async_remote_copy(src_ref=cw_ref, dst_ref=cwf_ref.at[slot], send_sem=cw_send.at[k], recv_sem=cw_recv.at[k],
                                                device_id=to, device_id_type=MESH)

        def d2d(k, t, slot, core):
            ref = fulls[t].at[slot, part(t, core)]
            return pltpu.make_async_remote_copy(src_ref=ref, dst_ref=ref, send_sem=d2d_send.at[k, t], recv_sem=d2d_recv.at[k, t],
                                                device_id=sib, device_id_type=MESH)

        sends = []

        def go(cp):
            cp.start()
            sends.append(cp)

        for q_first in (0, 1):
            for k, (px, py) in enumerate(near):
                for t in range(2):
                    go(direct(k, t, k if q_first == 0 else 1 - k, me, (px, py, c)))
        for k, (px, py) in enumerate(near + [far]):
            go(conv_copy(k, me, (px, py, c)))
        for k, (px, py) in enumerate(near):
            other = near[1 - k]
            for t in range(2):
                direct(k, t, k, chip(px, py), (px, py, c)).wait_recv()
                go(passed_on(k, t, chip(px, py), (*other, c)))

        first_half = lax.broadcasted_iota(jnp.int32, (chunk, LANES), 1) % HEAD_DIM < half_dim
        row = lax.broadcasted_iota(jnp.int32, (chunk, LANES), 0)

        def table_rows(i, carry):
            at = pl.multiple_of(i * chunk, chunk)
            ang = (row + at).astype(F32) * freq_ref[...]
            sin = jnp.sin(ang)
            cos_ref[pl.ds(at, chunk), :] = jnp.cos(ang)
            s1_ref[pl.ds(at, chunk), :] = jnp.where(first_half, -sin, 0.0)
            s2_ref[pl.ds(at, chunk), :] = jnp.where(first_half, 0.0, sin)
            return carry

        lax.fori_loop(0, seq // chunk, table_rows, 0)

        for k, (px, py) in enumerate(near):
            for t in range(2):
                direct(k, t, 1 - k, chip(px, py), (px, py, c)).wait_recv()
                go(d2d(k, t, chip(px, py), c))
        for t in range(2):
            for k, (px, py) in enumerate(near):
                passed_on(k, t, chip(*far), (px, py, c)).wait_recv()
            go(d2d(2, t, chip(*far), c))
        for k, (px, py) in enumerate(near + [far]):
            conv_copy(k, chip(px, py), (px, py, c)).wait_recv()
            for t in range(2):
                d2d(k, t, chip(px, py), 1 - c).wait_recv()
        for cp in sends:
            cp.wait_send()

    vm = pl.BlockSpec(memory_space=pltpu.VMEM)
    dma = pltpu.SemaphoreType.DMA
    return pl.pallas_call(
        body, name="gather_weights",
        out_shape=(jax.ShapeDtypeStruct((N_CHIPS, d_model, width), BF16),
                   jax.ShapeDtypeStruct((N_CHIPS, rows, d_model), BF16),
                   jax.ShapeDtypeStruct((N_CHIPS, SUBLANES, LANES), F32),
                   *[jax.ShapeDtypeStruct((seq, LANES), F32)] * 3),
        in_specs=[vm, vm, vm, vm], out_specs=(vm,) * 6,
        scratch_shapes=[pltpu.VMEM((d_model, width), BF16), pltpu.VMEM((rows, d_model), BF16),
                        dma((2, 2, 2)), dma((2, 2, 2)), dma((2, 2)), dma((2, 2)), dma((3,)), dma((3,)),
                        dma((3, 2)), dma((3, 2))],
        compiler_params=_params(),
    )(w_in, w_out, cw, inv_freq)


def _rope(t, cos, s1, s2):
    return t * cos + pltpu.roll(t, LANES - HEAD_DIM // 2, 1) * s1 + pltpu.roll(t, HEAD_DIM // 2, 1) * s2


def _rope_transposed(g, cos, s1, s2):
    return g * cos + pltpu.roll(g * s1, HEAD_DIM // 2, 1) + pltpu.roll(g * s2, LANES - HEAD_DIM // 2, 1)


def inproj(x, g1, w_full, tables, attn_w, conv_w):
    seq, d_model = x.shape
    width = w_full.shape[2]
    tm = ROW_TILE
    groups = _lane_groups(attn_w)

    def body(x_ref, g_ref, w_ref, cos_ref, s1_ref, s2_ref,
             ht_ref, q_ref, k_ref, v_ref, qp_ref, kp_ref, vp_ref, ga_ref, cz_ref, stage):
        xv = x_ref[...]
        hb = ((xv * lax.rsqrt(jnp.mean(xv * xv, axis=-1, keepdims=True) + NORM_EPS)) * g_ref[...]).astype(BF16)
        ht_ref[...] = jnp.transpose(hb)
        cos, s1, s2 = cos_ref[...], s1_ref[...], s2_ref[...]

        def proj(a, b):
            parts = [jnp.dot(hb, w_ref[j, :, lo:hi], preferred_element_type=F32) for j, lo, hi in _col_pieces(a, b, width)]
            return parts[0] if len(parts) == 1 else jnp.concatenate(parts, axis=1)

        def emit(z, nat_ref, perm_ref, fn):
            for g, sl in enumerate(groups):
                val = fn(z[:, sl])
                nat_ref[:, sl] = val.astype(BF16)
                _stage_put(stage, g, val)
            for g, sl in enumerate(groups):
                _to_perm(stage, g, perm_ref, sl, BF16)

        emit(proj(0, attn_w), q_ref, qp_ref, lambda t: _rope(t, cos, s1, s2) * ATTN_SCALE)
        emit(proj(attn_w, 2 * attn_w), k_ref, kp_ref, lambda t: _rope(t, cos, s1, s2))
        emit(proj(2 * attn_w, 3 * attn_w), v_ref, vp_ref, lambda t: t)
        ga_ref[...] = proj(3 * attn_w, 4 * attn_w)
        cz_ref[...] = proj(4 * attn_w, 4 * attn_w + 4 * conv_w)

    row = lambda n: pl.BlockSpec((tm, n), lambda i: (i, 0))
    nat = jax.ShapeDtypeStruct((seq, attn_w), BF16)
    perm = jax.ShapeDtypeStruct(_perm_shape(seq, attn_w), BF16)
    return pl.pallas_call(
        body, name="inproj", grid=(seq // tm,),
        out_shape=(jax.ShapeDtypeStruct((d_model, seq), BF16), nat, nat, nat, perm, perm, perm,
                   jax.ShapeDtypeStruct((seq, attn_w), F32), jax.ShapeDtypeStruct((seq, 4 * conv_w), F32)),
        in_specs=[row(d_model), _const_spec((1, d_model)), _const_spec(w_full.shape), row(LANES), row(LANES), row(LANES)],
        out_specs=(pl.BlockSpec((d_model, tm), lambda i: (0, i)), row(attn_w), row(attn_w), row(attn_w),
                   _perm_tile_spec(attn_w, tm), _perm_tile_spec(attn_w, tm), _perm_tile_spec(attn_w, tm),
                   row(attn_w), row(4 * conv_w)),
        scratch_shapes=[pltpu.VMEM(_stage_shape(len(groups), tm), F32)],
        compiler_params=_params(("arbitrary",)),
    )(x, g1, w_full, *tables)


class _Mode:
    def __init__(self, name, seq):
        self.name = name
        if name == "nat":
            self.residues, blocks = 1, seq // BLOCK
        elif name == "p16":
            self.residues, blocks = PERM, seq // PERM // BLOCK
        else:
            self.residues, blocks = PJ, seq // PERM // P4_ROWS
        self.qb = max(d for d in range(1, MAX_QUERY_BLOCKS + 1) if blocks % d == 0)
        self.steps = blocks // self.qb

    def _spec(self, blocks, width, index):
        if self.name == "nat":
            return pl.BlockSpec((blocks * BLOCK, width), lambda r, n: (index(n), 0))
        if self.name == "p16":
            return pl.BlockSpec((1, 1, blocks * BLOCK, width), lambda r, n: (r // PJ, r % PJ, index(n), 0))
        return pl.BlockSpec((PJ, 1, blocks * P4_ROWS, width), lambda r, n: (0, r, index(n), 0))

    def wide(self, width, last=None):
        return self._spec(self.qb, width, (lambda n: n) if last is None else (lambda n: jnp.minimum(n, last)))

    def wide_before(self, width):
        return self._spec(self.qb, width, lambda n: jnp.maximum(n - 1, 0))

    def block_before(self, width, last=None):
        step = (lambda n: n) if last is None else (lambda n: jnp.minimum(n, last))
        return self._spec(1, width, lambda n: jnp.maximum(self.qb * step(n) - 1, 0))

    def get(self, ref, sl, sub=0):
        if self.name == "nat":
            return ref[sub * BLOCK:(sub + 1) * BLOCK, sl]
        if self.name == "p16":
            return ref[0, 0, sub * BLOCK:(sub + 1) * BLOCK, sl]
        return jnp.concatenate([ref[j, 0, sub * P4_ROWS:(sub + 1) * P4_ROWS, sl] for j in range(PJ)], axis=0)

    def put(self, ref, sl, val, sub=0):
        val = val.astype(ref.dtype)
        if self.name == "nat":
            ref[sub * BLOCK:(sub + 1) * BLOCK, sl] = val
        elif self.name == "p16":
            ref[0, 0, sub * BLOCK:(sub + 1) * BLOCK, sl] = val
        else:
            for j in range(PJ):
                ref[j, 0, sub * P4_ROWS:(sub + 1) * P4_ROWS, sl] = val[j * P4_ROWS:(j + 1) * P4_ROWS]

    def keys(self, before_ref, wide_ref, sl, sub):
        older = self.get(before_ref, sl) if sub == 0 else self.get(wide_ref, sl, sub - 1)
        return jnp.concatenate([older, self.get(wide_ref, sl, sub)], axis=0)

    def index(self, idx, is_key):
        if self.name != "p4":
            return idx - BLOCK if is_key else idx
        within = jnp.bitwise_and(idx, BLOCK - 1)
        m = PJ * jnp.bitwise_and(within, P4_ROWS - 1) + jnp.right_shift(within, P4_ROWS.bit_length() - 1)
        return m + BLOCK * (jnp.right_shift(idx, BLOCK.bit_length() - 1) - 1) if is_key else m

    def bias(self, has_before):
        shape = (2 * BLOCK, BLOCK)
        kidx = lax.broadcasted_iota(jnp.int32, shape, 0)
        qidx = lax.broadcasted_iota(jnp.int32, shape, 1)
        rel = self.index(qidx, False) - self.index(kidx, True)
        valid = (rel >= 0) & (rel <= WINDOW_KEYS)
        if has_before is not True:
            valid = valid & ((kidx >= BLOCK) | has_before)
        one = jnp.where(valid, 0.0, NEG)
        return jnp.concatenate([one, one], axis=1)


def _head_masks():
    lane = lax.broadcasted_iota(jnp.int32, (BLOCK, LANES), 1)
    lo = lane < HEAD_DIM
    return lane, lo, jnp.where(lo, 1.0, 0.0).astype(BF16), jnp.where(lo, 0.0, 1.0).astype(BF16)


def _column(blk, lane, h):
    return jnp.sum(jnp.where(lane == h, blk, 0.0), axis=1, keepdims=True)


def attn_fwd(name, q, k, v, run):
    nat = name == "nat"
    seq = q.shape[0] if nat else q.shape[2] * PERM
    attn_w = q.shape[-1]
    mode = _Mode(name, seq)
    groups = _lane_groups(attn_w)
    first = run is None
    all_lanes = slice(0, LANES)

    def body(*refs):
        q_ref, kp_ref, kc_ref, vp_ref, vc_ref = refs[:5]
        if first:
            o_ref, l_ref = refs[5:]
        elif nat:
            oin_ref, lin_ref, ex_ref, o_ref, l_ref, ostage, lstage = refs[5:]
        else:
            oin_ref, lin_ref, ex_ref, o_ref, l_ref = refs[5:]
        n = pl.program_id(1)
        subs = range(mode.qb)
        biases = [mode.bias(n > 0)] + [mode.bias(True)] * (mode.qb - 1)
        _, lo, m_lo, m_hi = _head_masks()
        head_row = lax.broadcasted_iota(jnp.int32, (BLOCK, LANES), 0)
        ones = jnp.ones((2 * BLOCK, LANES), BF16)
        lrows = [jnp.zeros((BLOCK, LANES), F32) for _ in subs]

        def probs(sub, sl):
            q2 = mode.get(q_ref, sl, sub)
            kcat = mode.keys(kp_ref, kc_ref, sl, sub)
            vcat = mode.keys(vp_ref, vc_ref, sl, sub)
            qq = jnp.concatenate([q2 * m_lo, q2 * m_hi], axis=0)
            s_t = _nt(kcat, qq) + biases[sub]
            m = jnp.max(s_t, axis=0, keepdims=True)
            pe = jnp.exp(s_t - m)
            l = jnp.sum(pe, axis=0, keepdims=True)
            return jnp.concatenate([vcat, ones], axis=1), pe.astype(BF16), m + jnp.log(l)

        def output(sub, p, sl, vext, pb, lse):
            o_ext = _tn(pb, vext)
            o_new = o_ext[:, :LANES] / o_ext[:, LANES:]
            mode.put(o_ref, sl, jnp.where(lo, o_new[:BLOCK], o_new[BLOCK:]), sub)
            rows = jnp.where(head_row == 2 * p, lse[:, :BLOCK], lrows[sub])
            lrows[sub] = jnp.where(head_row == 2 * p + 1, lse[:, BLOCK:], rows)

        pending = None
        for sub in subs:
            for p, sl in enumerate(groups):
                nxt = probs(sub, sl)
                if pending is not None:
                    output(*pending)
                pending = (sub, p, sl, *nxt)
        output(*pending)
        if not first and nat:
            for g, sl in enumerate(groups):
                _from_perm(oin_ref, sl, ostage, g)
            _from_perm(lin_ref, all_lanes, lstage, 0)
        for sub in subs:
            rows = slice(sub * BLOCK, (sub + 1) * BLOCK)
            lblk = jnp.transpose(lrows[sub])
            if first:
                mode.put(l_ref, all_lanes, lblk, sub)
                continue
            lin = _stage_get(lstage, 0)[rows] if nat else mode.get(lin_ref, all_lanes, sub)
            mx = jnp.maximum(lin, lblk)
            new = mx + jnp.log(jnp.exp(lin - mx) + jnp.exp(lblk - mx))
            mode.put(l_ref, all_lanes, new, sub)

            def expand(w):
                hi = w.astype(BF16)
                rest = (w - hi.astype(F32)).astype(BF16)
                ex = ex_ref[...]
                return jnp.dot(hi, ex, preferred_element_type=F32) + jnp.dot(rest, ex, preferred_element_type=F32)

            w_prev, w_cur = expand(jnp.exp(lin - new)), expand(jnp.exp(lblk - new))
            for p, sl in enumerate(groups):
                o_prev = _stage_get(ostage, p)[rows] if nat else mode.get(oin_ref, sl, sub)
                mode.put(o_ref, sl, w_prev[:, sl] * o_prev + w_cur[:, sl] * mode.get(o_ref, sl, sub), sub)

    ins = [q, k, k, v, v]
    specs = [mode.wide(attn_w), mode.block_before(attn_w), mode.wide(attn_w), mode.block_before(attn_w), mode.wide(attn_w)]
    scratch = []
    if not first:
        ins += list(run)
        if nat:
            rows_a = mode.qb * BLOCK // PERM
            specs += [pl.BlockSpec((PJ, PJ, rows_a, attn_w), lambda r, n: (0, 0, n, 0)),
                      pl.BlockSpec((PJ, PJ, rows_a, LANES), lambda r, n: (0, 0, n, 0))]
            scratch = [pltpu.VMEM(_stage_shape(len(groups), mode.qb * BLOCK), F32),
                       pltpu.VMEM(_stage_shape(1, mode.qb * BLOCK), F32)]
        else:
            specs += [mode.wide(attn_w), mode.wide(LANES)]
        head_of_lane = jnp.arange(attn_w, dtype=jnp.int32) // HEAD_DIM
        ins.append((jnp.arange(LANES, dtype=jnp.int32)[:, None] == head_of_lane[None, :]).astype(BF16))
        specs.append(_const_spec((LANES, attn_w)))
    if nat:
        out_shape = (jax.ShapeDtypeStruct((seq, attn_w), F32), jax.ShapeDtypeStruct((seq, LANES), F32))
    else:
        out_shape = (jax.ShapeDtypeStruct(_perm_shape(seq, attn_w), F32), jax.ShapeDtypeStruct(_perm_shape(seq, LANES), F32))
    return pl.pallas_call(
        body, name=f"attn_fwd_{name}", grid=(mode.residues, mode.steps),
        out_shape=out_shape, in_specs=specs, out_specs=(mode.wide(attn_w), mode.wide(LANES)),
        scratch_shapes=scratch,
        compiler_params=_params(("arbitrary", "arbitrary")),
    )(*ins)


def attn_bwd(name, q, k, v, d_o, lse, delta, run):
    nat = name == "nat"
    seq = q.shape[0] if nat else q.shape[2] * PERM
    attn_w = q.shape[-1]
    mode = _Mode(name, seq)
    steps, qb = mode.steps, mode.qb
    single = steps == 1
    groups = _lane_groups(attn_w)
    first = run is None
    all_lanes = slice(0, LANES)

    def body(*refs):
        q_ref, kp_ref, kc_ref, vp_ref, vc_ref, do_ref, lse_ref, dl_ref = refs[:8]
        if first:
            dq_ref, dk_ref, dv_ref, ck, cv = refs[8:]
        else:
            dqi_ref, dki_ref, dvi_ref, dq_ref, dk_ref, dv_ref, ck, cv = refs[8:]
        n = pl.program_id(1)
        carries = ((ck, dk_ref, None if first else dki_ref), (cv, dv_ref, None if first else dvi_ref))

        def emit(out_ref, acc_ref, sl, sub, val):
            if acc_ref is not None:
                val = val + mode.get(acc_ref, sl, sub).astype(F32)
            mode.put(out_ref, sl, val, sub)

        if not single:
            @pl.when(n == 0)
            def _():
                ck[...] = jnp.zeros_like(ck)
                cv[...] = jnp.zeros_like(cv)

        @pl.when(n < steps)
        def _():
            biases = [mode.bias(n > 0)] + [mode.bias(True)] * (qb - 1)
            _, lo, m_lo, m_hi = _head_masks()

            def scores(sub, p, sl, lse_t, dl_t):
                q2, do2 = mode.get(q_ref, sl, sub), mode.get(do_ref, sl, sub)
                kcat = mode.keys(kp_ref, kc_ref, sl, sub)
                vcat = mode.keys(vp_ref, vc_ref, sl, sub)
                qq = jnp.concatenate([q2 * m_lo, q2 * m_hi], axis=0)
                dd = jnp.concatenate([do2 * m_lo, do2 * m_hi], axis=0)
                h0 = 2 * p
                lse2 = jnp.concatenate([lse_t[h0:h0 + 1, :], lse_t[h0 + 1:h0 + 2, :]], axis=1)
                dl2 = jnp.concatenate([dl_t[h0:h0 + 1, :], dl_t[h0 + 1:h0 + 2, :]], axis=1)
                p_t = jnp.exp(_nt(kcat, qq) + (biases[sub] - lse2))
                ds_t = p_t * (_nt(vcat, dd) - dl2)
                return qq, dd, kcat, p_t.astype(BF16), ds_t.astype(BF16)

            def grads(sub, sl, qq, dd, kcat, pb, dsb):
                dqb = _tn(dsb, kcat)
                dq2 = jnp.where(lo, dqb[:BLOCK], dqb[BLOCK:]) * ATTN_SCALE
                if not first:
                    dq2 = dq2 + mode.get(dqi_ref, sl, sub).astype(F32)
                mode.put(dq_ref, sl, dq2, sub)
                for (carry, out_ref, acc_ref), lhs, rhs in zip(carries, (dsb, pb), (qq, dd)):
                    both = jnp.dot(lhs, rhs, preferred_element_type=F32)
                    if sub == 0:
                        if not single:
                            for s in range(qb - 1):
                                emit(out_ref, acc_ref, sl, s, carry[s, :, sl])
                            emit(out_ref, acc_ref, sl, qb - 1, carry[qb - 1, :, sl] + both[:BLOCK])
                        carry[0, :, sl] = both[BLOCK:]
                    else:
                        carry[sub - 1, :, sl] += both[:BLOCK]
                        carry[sub, :, sl] = both[BLOCK:]
                    if single and sub == qb - 1:
                        for s in range(qb):
                            emit(out_ref, acc_ref, sl, s, carry[s, :, sl])

            stats = [(jnp.transpose(mode.get(lse_ref, all_lanes, sub)),
                      jnp.transpose(mode.get(dl_ref, all_lanes, sub))) for sub in range(qb)]
            pending = None
            for p, sl in enumerate(groups):
                for sub in range(qb):
                    nxt = scores(sub, p, sl, *stats[sub])
                    if pending is not None:
                        grads(*pending)
                    pending = (sub, sl, *nxt)
            grads(*pending)

        if not single:
            @pl.when(n == steps)
            def _():
                for carry, out_ref, acc_ref in carries:
                    for sl in groups:
                        for s in range(qb):
                            emit(out_ref, acc_ref, sl, s, carry[s, :, sl])

    last = steps - 1
    wide = lambda w: mode.wide(w, last)
    ins = [q, k, k, v, v, d_o, lse, delta]
    specs = [wide(attn_w), mode.block_before(attn_w, last), wide(attn_w), mode.block_before(attn_w, last), wide(attn_w),
             wide(attn_w), wide(LANES), wide(LANES)]
    if not first:
        ins += list(run)
        specs += [wide(attn_w), mode.wide_before(attn_w), mode.wide_before(attn_w)]
    shp = jax.ShapeDtypeStruct((seq, attn_w) if nat else _perm_shape(seq, attn_w), BF16)
    return pl.pallas_call(
        body, name=f"attn_bwd_{name}", grid=(mode.residues, steps if single else steps + 1),
        out_shape=(shp, shp, shp), in_specs=specs,
        out_specs=(wide(attn_w), mode.wide_before(attn_w), mode.wide_before(attn_w)),
        scratch_shapes=[pltpu.VMEM((qb, BLOCK, attn_w), F32), pltpu.VMEM((qb, BLOCK, attn_w), F32)],
        compiler_params=_params(("arbitrary", "arbitrary")),
    )(*ins)


def _shift_down(u, halo, k):
    rolled = pltpu.roll(u, k, 0)
    row = lax.broadcasted_iota(jnp.int32, halo.shape, 0)
    top = jnp.where(row < k, pltpu.roll(halo, k, 0), rolled[:SUBLANES])
    return jnp.concatenate([top, rolled[SUBLANES:]], axis=0)


def _shift_up(u, halo, k):
    rows = u.shape[0]
    rolled = pltpu.roll(u, rows - k, 0)
    row = lax.broadcasted_iota(jnp.int32, halo.shape, 0)
    bot = jnp.where(row >= SUBLANES - k, pltpu.roll(halo, SUBLANES - k, 0), rolled[rows - SUBLANES:])
    return jnp.concatenate([rolled[:rows - SUBLANES], bot], axis=0)


def tail(o, lse, ga, cz, x, tgt, w_out, g2, cw):
    seq, d_model = x.shape
    attn_w = o.shape[1]
    conv_w = cz.shape[1] // 4
    mix = attn_w + conv_w
    groups = _lane_groups(attn_w)
    tm = ROW_TILE
    nt = seq // tm
    hb = tm // SUBLANES

    def body(o_ref, l_ref, ga_ref, cz_ref, hz_ref, x_ref, t_ref, w_ref, g_ref, cw_ref,
             do_ref, dl_ref, dop_ref, dlp_ref, lp_ref, dga_ref, dcb_ref, dgc_ref, dcv_ref, e_ref,
             dw_ref, dg_ref, dcw_ref, loss_ref, stage):
        i = pl.program_id(0)

        @pl.when(i == 0)
        def _():
            dw_ref[...] = jnp.zeros_like(dw_ref)
            dg_ref[...] = jnp.zeros_like(dg_ref)
            dcw_ref[...] = jnp.zeros_like(dcw_ref)
            loss_ref[...] = jnp.zeros_like(loss_ref)

        u = cz_ref[:, 2 * conv_w:3 * conv_w] * cz_ref[:, 0:conv_w]
        uh = hz_ref[:, 2 * conv_w:3 * conv_w] * hz_ref[:, 0:conv_w]
        uh = jnp.where(i > 0, uh, 0.0)
        u1 = _shift_down(u, uh, 1)
        u2 = _shift_down(u, uh, 2)
        w0, w1, w2 = cw_ref[0:1, :], cw_ref[1:2, :], cw_ref[2:3, :]
        cvv = u2 * w0 + u1 * w1 + u * w2
        gv = g_ref[...]
        all_lanes = slice(0, LANES)

        def forward(rs):
            ov, gav = o_ref[rs, :], ga_ref[rs, :]
            sig_a = _sigmoid(gav)
            silu_a = gav * sig_a
            cb, gc = cz_ref[rs, conv_w:2 * conv_w], cz_ref[rs, 3 * conv_w:4 * conv_w]
            sig_c = _sigmoid(gc)
            silu_c = gc * sig_c
            bc = cb * cvv[rs]
            mixed = jnp.concatenate([ov * silu_a, bc * silu_c], axis=1).astype(BF16)
            yv = jnp.dot(mixed, w_ref[...], preferred_element_type=F32)
            return ov, gav, sig_a, silu_a, cb, gc, sig_c, silu_c, bc, mixed, yv

        def loss_and_dy(rs, mixed, yv):
            r2 = lax.rsqrt(jnp.mean(yv * yv, axis=-1, keepdims=True) + NORM_EPS)
            yhat = yv * r2
            diff = (x_ref[rs, :] + yhat * gv) - t_ref[rs, :]
            loss_ref[...] += _rowgroup_sum(diff * diff)
            ev = diff * (1.0 / d_model)
            e_ref[rs, :] = ev
            dg_ref[...] += _rowgroup_sum(ev * yhat)
            eg = ev * gv
            dy = (r2 * (eg - yhat * jnp.mean(eg * yhat, axis=-1, keepdims=True))).astype(BF16)
            dw_ref[...] += _tn(mixed, dy)
            return _nt(dy, w_ref[...])

        def backward(rs, ov, gav, sig_a, silu_a, cb, gc, sig_c, silu_c, bc, dm):
            rows = rs.stop - rs.start
            dma, dmc = dm[:, :attn_w], dm[:, attn_w:]
            dov = dma * silu_a
            do_ref[rs, :] = dov.astype(BF16)
            dga_ref[rs, :] = (dma * ov * (sig_a * (1.0 + gav * (1.0 - sig_a)))).astype(BF16)
            prod = dov * ov
            lane = lax.broadcasted_iota(jnp.int32, (rows, LANES), 1)
            lo = lane < HEAD_DIM
            dblk = jnp.zeros((rows, LANES), F32)
            for p, sl in enumerate(groups):
                pr = prod[:, sl]
                dblk = jnp.where(lane == 2 * p, jnp.sum(jnp.where(lo, pr, 0.0), axis=1, keepdims=True), dblk)
                dblk = jnp.where(lane == 2 * p + 1, jnp.sum(jnp.where(lo, 0.0, pr), axis=1, keepdims=True), dblk)
                _stage_put(stage, p, dov[:, sl], rs.start)
            dl_ref[rs, :] = dblk
            _stage_put(stage, len(groups), dblk, rs.start)
            _stage_put(stage, len(groups) + 1, l_ref[rs, :], rs.start)
            dsc = dmc * silu_c
            cv_rows = cvv[rs]
            dcb_ref[rs, :] = (dsc * cv_rows).astype(BF16)
            dgc_ref[rs, :] = (dmc * bc * (sig_c * (1.0 + gc * (1.0 - sig_c)))).astype(BF16)
            dcv = dsc * cb
            dcv_ref[rs, :] = dcv
            dcw_ref[0:SUBLANES, :] += _rowgroup_sum(dcv * u2[rs])
            dcw_ref[SUBLANES:2 * SUBLANES, :] += _rowgroup_sum(dcv * u1[rs])
            dcw_ref[2 * SUBLANES:3 * SUBLANES, :] += _rowgroup_sum(dcv * u[rs])

        halves = [slice(0, tm // 2), slice(tm // 2, tm)]
        fwd = [forward(rs) for rs in halves]
        dms = [loss_and_dy(rs, f[9], f[10]) for rs, f in zip(halves, fwd)]
        for rs, f, dm in zip(halves, fwd, dms):
            backward(rs, *f[:9], dm)
        for p, sl in enumerate(groups):
            _to_perm(stage, p, dop_ref, sl, BF16)
        _to_perm(stage, len(groups), dlp_ref, all_lanes, F32)
        _to_perm(stage, len(groups) + 1, lp_ref, all_lanes, F32)

    row = lambda n: pl.BlockSpec((tm, n), lambda i: (i, 0))
    whole = lambda a, b: pl.BlockSpec((a, b), lambda i: (0, 0))
    return pl.pallas_call(
        body, name="tail", grid=(nt,),
        out_shape=(jax.ShapeDtypeStruct((seq, attn_w), BF16), jax.ShapeDtypeStruct((seq, LANES), F32),
                   jax.ShapeDtypeStruct(_perm_shape(seq, attn_w), BF16), jax.ShapeDtypeStruct(_perm_shape(seq, LANES), F32),
                   jax.ShapeDtypeStruct(_perm_shape(seq, LANES), F32),
                   jax.ShapeDtypeStruct((seq, attn_w), BF16), jax.ShapeDtypeStruct((seq, conv_w), BF16),
                   jax.ShapeDtypeStruct((seq, conv_w), BF16), jax.ShapeDtypeStruct((seq, conv_w), F32),
                   jax.ShapeDtypeStruct((seq, d_model), F32), jax.ShapeDtypeStruct((mix, d_model), F32),
                   jax.ShapeDtypeStruct((SUBLANES, d_model), F32), jax.ShapeDtypeStruct((CONV_K * SUBLANES, conv_w), F32),
                   jax.ShapeDtypeStruct((SUBLANES, d_model), F32)),
        in_specs=[row(attn_w), row(LANES), row(attn_w), row(4 * conv_w),
                  pl.BlockSpec((SUBLANES, 4 * conv_w), lambda i: (jnp.maximum(i * hb - 1, 0), 0)),
                  row(d_model), row(d_model), _const_spec((mix, d_model)), _const_spec((1, d_model)),
                  _const_spec((SUBLANES, conv_w))],
        out_specs=(row(attn_w), row(LANES), _perm_tile_spec(attn_w, tm), _perm_tile_spec(LANES, tm), _perm_tile_spec(LANES, tm),
                   row(attn_w), row(conv_w), row(conv_w), row(conv_w), row(d_model),
                   whole(mix, d_model), whole(SUBLANES, d_model), whole(CONV_K * SUBLANES, conv_w),
                   whole(SUBLANES, d_model)),
        scratch_shapes=[pltpu.VMEM(_stage_shape(len(groups) + 2, tm), F32)],
        compiler_params=_params(("arbitrary",)),
    )(o, lse, ga, cz, cz, x, tgt, w_out, g2, cw)


def dz_dx(nat_grads, perm_grads, dga, dcb, dgc, dcv, cz, tables, x, g1, e, w_full, cw):
    seq, d_model = x.shape
    attn_w = dga.shape[1]
    conv_w = dcv.shape[1]
    width = w_full.shape[2]
    in_w = 4 * attn_w + 4 * conv_w
    groups = _lane_groups(attn_w)
    tm = DZ_ROW_TILE
    nt = seq // tm
    hb = tm // SUBLANES

    def body(dq_ref, dk_ref, dv_ref, dqp_ref, dkp_ref, dvp_ref, dga_ref, dcb_ref, dgc_ref, dcv_ref, nh_ref, ch_ref, cc_ref,
             cos_ref, s1_ref, s2_ref, x_ref, g_ref, e_ref, w_ref, cw_ref, gx_ref, dz_ref, dg_ref, stage):
        i = pl.program_id(0)

        @pl.when(i == 0)
        def _():
            dg_ref[...] = jnp.zeros_like(dg_ref)

        cos, s1, s2 = cos_ref[...], s1_ref[...], s2_ref[...]

        def qkv_columns(t, nat_ref, perm_ref):
            for g, sl in enumerate(groups):
                _from_perm(perm_ref, sl, stage, g)
            for g, sl in enumerate(groups):
                tot = nat_ref[:, sl].astype(F32) + _stage_get(stage, g)
                if t < 2:
                    tot = _rope_transposed(tot, cos, s1, s2)
                dz_ref[:, t * attn_w + g * LANES:t * attn_w + (g + 1) * LANES] = tot.astype(BF16)

        def dh_part(j):
            return _nt(dz_ref[:, j * width:(j + 1) * width], w_ref[j])

        dcv = dcv_ref[...]
        nh = jnp.where(i < nt - 1, nh_ref[...], 0.0)
        w0, w1, w2 = cw_ref[0:1, :], cw_ref[1:2, :], cw_ref[2:3, :]
        du = dcv * w2 + _shift_up(dcv, nh, 1) * w1 + _shift_up(dcv, nh, 2) * w0
        base = 4 * attn_w
        dz_ref[:, base:base + conv_w] = (du * cc_ref[...]).astype(BF16)
        dz_ref[:, base + conv_w:base + 2 * conv_w] = dcb_ref[...]
        dz_ref[:, base + 2 * conv_w:base + 3 * conv_w] = (du * ch_ref[...]).astype(BF16)
        dz_ref[:, base + 3 * conv_w:base + 4 * conv_w] = dgc_ref[...]
        dz_ref[:, 3 * attn_w:4 * attn_w] = dga_ref[...]
        ready = in_w
        dh = None
        for t, nat_ref, perm_ref in ((2, dv_ref, dvp_ref), (1, dk_ref, dkp_ref), (0, dq_ref, dqp_ref), (None, None, None)):
            lowest_open = 0 if t is None else (t + 1) * attn_w
            while ready - width >= lowest_open:
                ready -= width
                part = dh_part(ready // width)
                dh = part if dh is None else dh + part
            if t is not None:
                qkv_columns(t, nat_ref, perm_ref)
        xv = x_ref[...]
        r1 = lax.rsqrt(jnp.mean(xv * xv, axis=-1, keepdims=True) + NORM_EPS)
        xhat = xv * r1
        dg_ref[...] += _rowgroup_sum(dh * xhat)
        dhg = dh * g_ref[...]
        gx_ref[...] = r1 * (dhg - xhat * jnp.mean(dhg * xhat, axis=-1, keepdims=True)) + e_ref[...]

    row = lambda n: pl.BlockSpec((tm, n), lambda i: (i, 0))
    whole = lambda a, b: pl.BlockSpec((a, b), lambda i: (0, 0))
    pt = _perm_tile_spec(attn_w, tm)
    return pl.pallas_call(
        body, name="dz_dx", grid=(nt,),
        out_shape=(jax.ShapeDtypeStruct((seq, d_model), F32), jax.ShapeDtypeStruct((seq, in_w), BF16),
                   jax.ShapeDtypeStruct((SUBLANES, d_model), F32)),
        in_specs=[row(attn_w), row(attn_w), row(attn_w), pt, pt, pt, row(attn_w), row(conv_w), row(conv_w), row(conv_w),
                  pl.BlockSpec((SUBLANES, conv_w), lambda i: (jnp.minimum((i + 1) * hb, seq // SUBLANES - 1), 0)),
                  pl.BlockSpec((tm, conv_w), lambda i: (i, 0)), pl.BlockSpec((tm, conv_w), lambda i: (i, 2)),
                  row(LANES), row(LANES), row(LANES), row(d_model), _const_spec((1, d_model)), row(d_model),
                  _const_spec(w_full.shape), _const_spec((SUBLANES, conv_w))],
        out_specs=(row(d_model), row(in_w), whole(SUBLANES, d_model)),
        scratch_shapes=[pltpu.VMEM(_stage_shape(len(groups), tm), F32)],
        compiler_params=_params(("arbitrary",)),
    )(*nat_grads, *perm_grads, dga, dcb, dgc, dcv, dcv, cz, cz, *tables, x, g1, e, w_full, cw)


def dw_in_reduce(ht, dz):
    d_model, seq = ht.shape
    half = dz.shape[1] // N_DEV
    ts = min(2048, seq)
    steps = seq // ts
    x, y, c = lax.axis_index("x"), lax.axis_index("y"), lax.axis_index("c")
    far_first = lambda x, y: [(1 - x, 1 - y), (1 - x, y), (x, 1 - y)]
    chips = jnp.stack([2 * px + py for px, py in far_first(x, y)] + [2 * x + y]).astype(jnp.int32)
    order = jnp.stack([2 * chips + (1 - c), 2 * chips + c], axis=1).reshape(N_DEV)

    def body(order_ref, ht_ref, dz_ref, out_ref, acc, theirs, staged, contrib, resbuf, out_sem, sa, ra, sb, rb, sc, rc):
        del order_ref
        p, s = pl.program_id(0), pl.program_id(1)
        x, y, c = lax.axis_index("x"), lax.axis_index("y"), lax.axis_index("c")
        sib = (x, y, 1 - c)
        peers = far_first(x, y)
        slot = p % 2

        def a_copy(k):
            return pltpu.make_async_remote_copy(src_ref=acc.at[0], dst_ref=theirs.at[k], send_sem=sa.at[k], recv_sem=ra.at[k],
                                                device_id=sib, device_id_type=MESH)

        def b_copy(k):
            px, py = peers[k]
            return pltpu.make_async_remote_copy(src_ref=staged.at[k], dst_ref=contrib.at[k], send_sem=sb.at[k], recv_sem=rb.at[k],
                                                device_id=(px, py, c), device_id_type=MESH)

        def c_copy(which):
            return pltpu.make_async_remote_copy(src_ref=resbuf.at[which], dst_ref=resbuf.at[which], send_sem=sc, recv_sem=rc,
                                                device_id=sib, device_id_type=MESH)

        @pl.when(s == 0)
        def _():
            for k in range(N_CHIPS - 1):
                @pl.when(p == 2 * k + 2)
                def _():
                    a_copy(k).wait_send()
            acc[slot] = jnp.zeros((d_model, half), F32)

        acc[slot] += jnp.dot(ht_ref[...], dz_ref[...], preferred_element_type=F32)

        @pl.when(s == steps - 1)
        def _():
            for k in range(N_CHIPS):
                @pl.when(p == 2 * k)
                def _():
                    a_copy(k).start()
            for k in range(N_CHIPS - 1):
                @pl.when(p == 2 * k + 1)
                def _():
                    a_copy(k).wait_recv()
                    staged[k] = (acc[1] + theirs[k]).astype(BF16)
                    b_copy(k).start()

            @pl.when(p == N_DEV - 1)
            def _():
                a_copy(N_CHIPS - 1).wait_recv()
                tot = acc[1] + theirs[N_CHIPS - 1]
                for k in range(N_CHIPS - 1):
                    b_copy(k).wait_recv()
                    tot = tot + contrib[k].astype(F32)
                resbuf[c] = tot
                c_copy(c).start()
                c_copy(1 - c).wait_recv()
                done = pltpu.make_async_copy(resbuf, out_ref, out_sem)
                done.start()
                a_copy(N_CHIPS - 1).wait_send()
                for k in range(N_CHIPS - 1):
                    b_copy(k).wait_send()
                c_copy(c).wait_send()
                done.wait()

    dma = pltpu.SemaphoreType.DMA
    grid_spec = pltpu.PrefetchScalarGridSpec(
        num_scalar_prefetch=1, grid=(N_DEV, steps),
        in_specs=[pl.BlockSpec((d_model, ts), lambda p, s, order_ref: (0, s)),
                  pl.BlockSpec((ts, half), lambda p, s, order_ref: (s, order_ref[p]))],
        out_specs=pl.BlockSpec(memory_space=pl.ANY),
        scratch_shapes=[pltpu.VMEM((2, d_model, half), F32), pltpu.VMEM((N_CHIPS, d_model, half), F32),
                        pltpu.VMEM((N_CHIPS - 1, d_model, half), BF16), pltpu.VMEM((N_CHIPS - 1, d_model, half), BF16),
                        pltpu.VMEM((2, d_model, half), F32), dma,
                        dma((N_CHIPS,)), dma((N_CHIPS,)), dma((N_CHIPS - 1,)), dma((N_CHIPS - 1,)), dma, dma])
    return pl.pallas_call(
        body, name="dw_in_reduce", grid_spec=grid_spec,
        out_shape=jax.ShapeDtypeStruct((2, d_model, half), F32),
        compiler_params=_params(("arbitrary", "arbitrary")),
    )(order, ht, dz)


def grad_reduce(tensors, small):
    nt = len(tensors)
    split = [g.reshape(N_CHIPS, 2, *g.shape[1:]) for g in tensors]
    shapes = [g.shape[2:] for g in split]

    def body(*refs):
        srcs, sm_ref = refs[:nt], refs[nt]
        res, rs_ref = refs[nt + 1:2 * nt + 1], refs[2 * nt + 1]
        scratch = refs[2 * nt + 2:]
        mine, theirs, staged, contrib = (scratch[k * nt:(k + 1) * nt] for k in range(4))
        sbuf, loc_sems, sa, ra, sb, rb, sc, rc, ss, rs = scratch[4 * nt:]
        x, y, c = lax.axis_index("x"), lax.axis_index("y"), lax.axis_index("c")
        me = 2 * x + y
        sib = (x, y, 1 - c)

        flips = [(fx, fy, fc) for fx in (0, 1) for fy in (0, 1) for fc in (0, 1)][1:]
        my8 = 4 * x + 2 * y + c
        sbuf[my8] = sm_ref[...]

        def small_copy(k, slot, to):
            return pltpu.make_async_remote_copy(src_ref=sm_ref, dst_ref=sbuf.at[slot], send_sem=ss.at[k], recv_sem=rs.at[k],
                                                device_id=to, device_id_type=MESH)

        sends = []
        for k, (fx, fy, fc) in enumerate(flips):
            px, py, pc = _flip(x, fx), _flip(y, fy), _flip(c, fc)
            sends.append(small_copy(k, my8, (px, py, pc)))
            sends[-1].start()

        def a_copy(t, j):
            return pltpu.make_async_remote_copy(src_ref=srcs[t].at[j, 1 - c], dst_ref=theirs[t].at[j], send_sem=sa.at[t, j],
                                                recv_sem=ra.at[t, j], device_id=sib, device_id_type=MESH)

        peers = _chip_peers(x, y)
        order = [2 * px + py for px, py in peers] + [me]
        loads = [[pltpu.make_async_copy(srcs[t].at[j, c], mine[t].at[j], loc_sems.at[t, j]) for t in range(nt)] for j in order]
        for pos, j in enumerate(order):
            for t in range(nt):
                loads[pos][t].start()
                sends.append(a_copy(t, j))
                sends[-1].start()

        def b_copy(k, t, piece, slot, to):
            return pltpu.make_async_remote_copy(src_ref=staged[t].at[piece], dst_ref=contrib[t].at[slot], send_sem=sb.at[k, t],
                                                recv_sem=rb.at[k, t], device_id=to, device_id_type=MESH)

        for k, (px, py) in enumerate(peers):
            j = 2 * px + py
            for t in range(nt):
                loads[k][t].wait()
                a_copy(t, j).wait_recv()
                staged[t][j] = (mine[t][j] + theirs[t][j]).astype(BF16)
                sends.append(b_copy(k, t, j, me, (px, py, c)))
                sends[-1].start()
        for t in range(nt):
            loads[len(peers)][t].wait()
            a_copy(t, me).wait_recv()
            mine[t][me] = mine[t][me] + theirs[t][me]
            contrib[t][me] = mine[t][me].astype(BF16)
        for k, (px, py) in enumerate(peers):
            for t in range(nt):
                b_copy(k, t, me, 2 * px + py, (px, py, c)).wait_recv()

        def c_copy(t, half):
            return pltpu.make_async_remote_copy(src_ref=res[t].at[half], dst_ref=res[t].at[half], send_sem=sc.at[t],
                                                recv_sem=rc.at[t], device_id=sib, device_id_type=MESH)

        for t in range(nt):
            own = mine[t][me]
            term = lambda j: jnp.where(me == j, own, contrib[t][j].astype(F32))
            res[t][c] = ((term(0) + term(1)) + term(2)) + term(3)
            sends.append(c_copy(t, c))
            sends[-1].start()
        for t in range(nt):
            c_copy(t, 1 - c).wait_recv()

        for k, (fx, fy, fc) in enumerate(flips):
            px, py, pc = _flip(x, fx), _flip(y, fy), _flip(c, fc)
            small_copy(k, 4 * px + 2 * py + pc, (px, py, pc)).wait_recv()
        tot = sbuf[0]
        for d in range(1, N_DEV):
            tot = tot + sbuf[d]
        rs_ref[...] = tot
        for cp in sends:
            cp.wait_send()

    vm = pl.BlockSpec(memory_space=pltpu.VMEM)
    anyspace = pl.BlockSpec(memory_space=pl.ANY)
    dma = pltpu.SemaphoreType.DMA
    bufs = [pltpu.VMEM((N_CHIPS, *shp), dt) for dt in (F32, F32, BF16, BF16) for shp in shapes]
    outs = pl.pallas_call(
        body, name="grad_reduce",
        out_shape=(*[jax.ShapeDtypeStruct((2, *shp), F32) for shp in shapes], jax.ShapeDtypeStruct(small.shape, F32)),
        in_specs=[anyspace] * nt + [vm], out_specs=tuple([vm] * (nt + 1)),
        scratch_shapes=[*bufs, pltpu.VMEM((N_DEV, *small.shape), F32),
                        dma((nt, N_CHIPS)), dma((nt, N_CHIPS)), dma((nt, N_CHIPS)), dma((3, nt)), dma((3, nt)), dma((nt,)), dma((nt,)),
                        dma((N_DEV - 1,)), dma((N_DEV - 1,))],
        compiler_params=_params(),
    )(*split, small)
    return outs[:nt], outs[nt]


def _adam_math(w, g, m, v):
    m = ADAM_B1 * m + (1.0 - ADAM_B1) * g
    v = ADAM_B2 * v + (1.0 - ADAM_B2) * (g * g)
    m_hat = m / (1.0 - ADAM_B1 ** ADAM_STEP)
    v_hat = v / (1.0 - ADAM_B2 ** ADAM_STEP)
    delta = -ADAM_LR * (m_hat / (jnp.sqrt(v_hat) + ADAM_EPS) + ADAM_WD * w)
    return delta, m, v


def adam_shard(name, w, g2, m, v, block, grid, w_map, g_map):
    def body(w_ref, g_ref, m_ref, v_ref, go_ref, d_ref, mo_ref, vo_ref):
        g = g_ref[0]
        delta, mn, vn = _adam_math(w_ref[...], g, m_ref[...], v_ref[...])
        go_ref[...] = g
        d_ref[...] = delta
        mo_ref[...] = mn
        vo_ref[...] = vn

    ws = pl.BlockSpec(block, w_map)
    shp = jax.ShapeDtypeStruct(w.shape, F32)
    return pl.pallas_call(
        body, name=name, grid=grid, out_shape=(shp, shp, shp, shp),
        in_specs=[ws, pl.BlockSpec((1, *block), g_map), ws, ws], out_specs=(ws, ws, ws, ws),
        compiler_params=_params(("arbitrary",) * len(grid)),
    )(w, g2, m, v)


def adam_small(ws, gs, ms, vs):
    n = len(ws)

    def body(*refs):
        ins, outs = refs[:4 * n], refs[4 * n:]
        for t in range(n):
            delta, mn, vn = _adam_math(ins[t][...], ins[n + t][...], ins[2 * n + t][...], ins[3 * n + t][...])
            outs[3 * t][...] = delta
            outs[3 * t + 1][...] = mn
            outs[3 * t + 2][...] = vn

    vm = pl.BlockSpec(memory_space=pltpu.VMEM)
    outs = pl.pallas_call(
        body, name="adam_small",
        out_shape=tuple(jax.ShapeDtypeStruct(w.shape, F32) for w in ws for _ in range(3)),
        in_specs=[vm] * (4 * n), out_specs=tuple([vm] * (3 * n)),
        compiler_params=_params(),
    )(*ws, *gs, *ms, *vs)
    return [outs[3 * t:3 * t + 3] for t in range(n)]


def kernel(x, norm_pre_g, w_in, conv_w, w_out, norm_post_g, loss_target, m_norm_pre_g, m_w_in, m_conv_w, m_w_out, m_norm_post_g, v_norm_pre_g, v_w_in, v_conv_w, v_w_out, v_norm_post_g):
    _, seq, d_model = x.shape
    width = w_in.shape[1]
    conv_q = conv_w.shape[1]
    conv_width = N_CHIPS * conv_q
    attn_width = d_model - conv_width
    xs, tg = x[0], loss_target[0]
    g1, g2 = norm_pre_g.reshape(1, d_model), norm_post_g.reshape(1, d_model)

    w_full, wout_full, cw_full, *tables = gather_weights(w_in, w_out, conv_w, seq)
    wout2 = wout_full.reshape(attn_width + conv_width, d_model)
    cw = jnp.zeros((SUBLANES, conv_width), F32).at[:CONV_K].set(
        cw_full[:, :CONV_K, :conv_q].transpose(1, 0, 2).reshape(CONV_K, conv_width))

    ht, q, k, v, qp, kp, vp, ga, cz = inproj(xs, g1, w_full, tables, attn_width, conv_width)
    run = attn_fwd("p4", qp, kp, vp, None)
    run = attn_fwd("p16", qp, kp, vp, run)
    o, lse = attn_fwd("nat", q, k, v, run)
    (d_o, delta, d_op, delta_p, lse_p, dga, dcb, dgc, dcv, e, dwout, dg2, dcw, loss_acc) = tail(
        o, lse, ga, cz, xs, tg, wout2, g2, cw)
    nat_grads = attn_bwd("nat", q, k, v, d_o, lse, delta, None)
    perm_grads = attn_bwd("p4", qp, kp, vp, d_op, lse_p, delta_p, None)
    perm_grads = attn_bwd("p16", qp, kp, vp, d_op, lse_p, delta_p, perm_grads)
    grad_x, dz, dg1 = dz_dx(nat_grads, perm_grads, dga, dcb, dgc, dcv, cz, tables, xs, g1, e, w_full, cw)

    small = jnp.zeros((SUBLANES, d_model), F32)
    small = small.at[0].set(dg1.sum(axis=0)).at[1].set(dg2.sum(axis=0))
    small = small.at[2:2 + CONV_K, :conv_width].set(dcw.reshape(CONV_K, SUBLANES, conv_width).sum(axis=1))
    small = small.at[2 + CONV_K, 0].set(jnp.sum(loss_acc))
    (rout,), rsmall = grad_reduce([dwout.reshape(N_DEV, -1, d_model)], small)
    rin = dw_in_reduce(ht, dz)

    half = width // 2
    tr = 256
    gw_in, d_in, m_in, v_in = adam_shard(
        "adam_w_in", w_in, rin, m_w_in, v_w_in, (tr, half), (2, d_model // tr),
        lambda hf, i: (i, hf), lambda hf, i: (hf, i, 0))
    rq = w_out.shape[0] // 2
    gw_out, d_out, m_out, v_out = adam_shard(
        "adam_w_out", w_out, rout, m_w_out, v_w_out, (rq, d_model), (2,),
        lambda hf: (hf, 0), lambda hf: (hf, 0, 0))

    chip = 2 * lax.axis_index("x") + lax.axis_index("y")
    g_pre, g_post = rsmall[0:1], rsmall[1:2]
    g_conv = lax.dynamic_slice(rsmall[2:2 + CONV_K, :conv_width], (0, chip * conv_q), (CONV_K, conv_q))
    (d_pre, m_pre, v_pre), (d_post, m_post, v_post), (d_cv, m_cv, v_cv) = adam_small(
        [g1, g2, conv_w], [g_pre, g_post, g_conv],
        [m_norm_pre_g.reshape(1, d_model), m_norm_post_g.reshape(1, d_model), m_conv_w],
        [v_norm_pre_g.reshape(1, d_model), v_norm_post_g.reshape(1, d_model), v_conv_w])

    loss = 0.5 * rsmall[2 + CONV_K, 0] / d_model
    vec = lambda a: a.reshape(d_model)
    return (loss, grad_x.reshape(1, seq, d_model),
            vec(g_pre), gw_in, g_conv, gw_out, vec(g_post),
            vec(d_pre), d_in, d_cv, d_out, vec(d_post),
            vec(m_pre), m_in, m_cv, m_out, vec(m_post),
            vec(v_pre), v_in, v_cv, v_out, vec(v_post))
```

```python
import jax
import jax.numpy as jnp
from jax import lax
from jax.experimental import pallas as pl
from jax.experimental.pallas import tpu as pltpu

HEAD_DIM = 64
LANES = 128
SUBLANES = 8
BLOCK = 128
WINDOW_KEYS = 128
PERM = 16
PJ = 4
P4_ROWS = BLOCK // PJ
MAX_QUERY_BLOCKS = 4
ROW_TILE = 512
DZ_ROW_TILE = 512
CONV_K = 3
ROPE_THETA = 10000.0
NORM_EPS = 1e-6
ATTN_SCALE = HEAD_DIM ** -0.5
NEG = -1e30
N_CHIPS = 4
N_DEV = 8
MESH = pl.DeviceIdType.MESH
ADAM_LR = 0.001
ADAM_B1 = 0.9
ADAM_B2 = 0.999
ADAM_EPS = 1e-08
ADAM_WD = 0.01
ADAM_STEP = 10
VMEM_LIMIT = 52 * 1024 * 1024

F32 = jnp.float32
BF16 = jnp.bfloat16


def _params(sem=None, **kw):
    return pltpu.CompilerParams(dimension_semantics=sem, vmem_limit_bytes=VMEM_LIMIT, **kw)


def _const_spec(shape):
    return pl.BlockSpec(shape, lambda *_: (0,) * len(shape), pipeline_mode=pl.Buffered(1))


def _sigmoid(z):
    return 1.0 / (1.0 + jnp.exp(-z))


def _rowgroup_sum(a):
    rows, n = a.shape
    return a.reshape(rows // SUBLANES, SUBLANES, n).sum(axis=0)


def _nt(a, b):
    return lax.dot_general(a, b, (((1,), (1,)), ((), ())), preferred_element_type=F32)


def _tn(a, b):
    return lax.dot_general(a, b, (((0,), (0,)), ((), ())), preferred_element_type=F32)


def _col_pieces(a, b, width):
    out = []
    while a < b:
        j = a // width
        e = min(b, (j + 1) * width)
        out.append((j, a - j * width, e - j * width))
        a = e
    return out


def _lane_groups(width):
    return [slice(g * LANES, (g + 1) * LANES) for g in range(width // LANES)]


def _perm_shape(seq, width):
    return (PJ, PJ, seq // PERM, width)


def _perm_tile_spec(width, tm):
    return pl.BlockSpec((PJ, PJ, tm // PERM, width), lambda i: (0, 0, i, 0))


STAGE_PITCH = 24


def _stage_shape(groups, rows):
    return (groups, rows // PERM * STAGE_PITCH, LANES)


def _stage_put(stage, g, val, row0=0):
    for a in range(val.shape[0] // PERM):
        at = (row0 // PERM + a) * STAGE_PITCH
        stage[g, at:at + PERM, :] = val[a * PERM:(a + 1) * PERM]


def _stage_get(stage, g):
    return jnp.concatenate([stage[g, a * STAGE_PITCH:a * STAGE_PITCH + PERM, :]
                            for a in range(stage.shape[1] // STAGE_PITCH)], axis=0)


def _to_perm(stage, g, dst_ref, sl, dtype):
    rows = stage.shape[1] // STAGE_PITCH
    for b in range(PERM):
        dst_ref[b // PJ, b % PJ, :, sl] = stage[g, pl.ds(b, rows, stride=STAGE_PITCH), :].astype(dtype)


def _from_perm(src_ref, sl, stage, g):
    rows = stage.shape[1] // STAGE_PITCH
    for b in range(PERM):
        stage[g, pl.ds(b, rows, stride=STAGE_PITCH), :] = src_ref[b // PJ, b % PJ, :, sl].astype(F32)


def _flip(a, f):
    return 1 - a if f else a


def gather_weights(w_in, w_out, conv_w, seq):
    d_model, width = w_in.shape
    rows = w_out.shape[0]
    cw = jnp.zeros((SUBLANES, LANES), F32).at[:CONV_K, :conv_w.shape[1]].set(conv_w)
    half_dim = HEAD_DIM // 2
    inv_freq = ROPE_THETA ** (-jnp.arange(half_dim, dtype=F32) * 2.0 / HEAD_DIM)
    inv_freq = jnp.tile(inv_freq, LANES // half_dim).reshape(1, LANES)
    chunk = min(ROW_TILE, seq)

    def body(win_ref, wout_ref, cw_ref, freq_ref, winf_ref, woutf_ref, cwf_ref, cos_ref, s1_ref, s2_ref,
             st_in, st_out, near_send, near_recv, far_send, far_recv, cw_send, cw_recv, d2d_send, d2d_recv):
        x, y, c = lax.axis_index("x"), lax.axis_index("y"), lax.axis_index("c")
        me = 2 * x + y
        sib = (x, y, 1 - c)
        st_in[...] = win_ref[...].astype(BF16)
        st_out[...] = wout_ref[...].astype(BF16)
        winf_ref[me] = st_in[...]
        woutf_ref[me] = st_out[...]
        cwf_ref[me] = cw_ref[...]
        stages = (st_in, st_out)
        fulls = (winf_ref, woutf_ref)
        halves = (d_model // 2, rows // 2)

        def part(t, core, q=None):
            size = halves[t] if q is None else halves[t] // 2
            start = core * halves[t] if q is None else core * halves[t] + q * size
            return pl.ds(pl.multiple_of(start, size), size)

        near = [(1 - x, y), (x, 1 - y)]
        far = (1 - x, 1 - y)
        chip = lambda px, py: 2 * px + py

        def direct(k, t, q, slot, to):
            src = stages[t].at[part(t, c, q)]
            return pltpu.make_async_remote_copy(src_ref=src, dst_ref=fulls[t].at[slot, part(t, c, q)], send_sem=near_send.at[k, t, q],
                                                recv_sem=near_recv.at[k, t, q], device_id=to, device_id_type=MESH)

        def passed_on(k, t, slot, to):
            ref = fulls[t].at[slot, part(t, c, k)]
            return pltpu.make_async_remote_copy(src_ref=ref, dst_ref=ref, send_sem=far_send.at[k, t], recv_sem=far_recv.at[k, t],
                                                device_id=to, device_id_type=MESH)

        def conv_copy(k, slot, to):
            return pltpu.make_async_remote_copy(src_ref=cw_ref, dst_ref=cwf_ref.at[slot], send_sem=cw_send.at[k], recv_sem=cw_recv.at[k],
                                                device_id=to, device_id_type=MESH)

        def d2d(k, t, slot, core):
            ref = fulls[t].at[slot, part(t, core)]
            return pltpu.make_async_remote_copy(src_ref=ref, dst_ref=ref, send_sem=d2d_send.at[k, t], recv_sem=d2d_recv.at[k, t],
                                                device_id=sib, device_id_type=MESH)

        sends = []

        def go(cp):
            cp.start()
            sends.append(cp)

        for q_first in (0, 1):
            for k, (px, py) in enumerate(near):
                for t in range(2):
                    go(direct(k, t, k if q_first == 0 else 1 - k, me, (px, py, c)))
        for k, (px, py) in enumerate(near + [far]):
            go(conv_copy(k, me, (px, py, c)))
        for k, (px, py) in enumerate(near):
            other = near[1 - k]
            for t in range(2):
                direct(k, t, k, chip(px, py), (px, py, c)).wait_recv()
                go(passed_on(k, t, chip(px, py), (*other, c)))

        first_half = lax.broadcasted_iota(jnp.int32, (chunk, LANES), 1) % HEAD_DIM < half_dim
        row = lax.broadcasted_iota(jnp.int32, (chunk, LANES), 0)

        def table_rows(i, carry):
            at = pl.multiple_of(i * chunk, chunk)
            ang = (row + at).astype(F32) * freq_ref[...]
            sin = jnp.sin(ang)
            cos_ref[pl.ds(at, chunk), :] = jnp.cos(ang)
            s1_ref[pl.ds(at, chunk), :] = jnp.where(first_half, -sin, 0.0)
            s2_ref[pl.ds(at, chunk), :] = jnp.where(first_half, 0.0, sin)
            return carry

        lax.fori_loop(0, seq // chunk, table_rows, 0)

        for k, (px, py) in enumerate(near):
            for t in range(2):
                direct(k, t, 1 - k, chip(px, py), (px, py, c)).wait_recv()
                go(d2d(k, t, chip(px, py), c))
        for t in range(2):
            for k, (px, py) in enumerate(near):
                passed_on(k, t, chip(*far), (px, py, c)).wait_recv()
            go(d2d(2, t, chip(*far), c))
        for k, (px, py) in enumerate(near + [far]):
            conv_copy(k, chip(px, py), (px, py, c)).wait_recv()
            for t in range(2):
                d2d(k, t, chip(px, py), 1 - c).wait_recv()
        for cp in sends:
            cp.wait_send()

    vm = pl.BlockSpec(memory_space=pltpu.VMEM)
    dma = pltpu.SemaphoreType.DMA
    return pl.pallas_call(
        body, name="gather_weights",
        out_shape=(jax.ShapeDtypeStruct((N_CHIPS, d_model, width), BF16),
                   jax.ShapeDtypeStruct((N_CHIPS, rows, d_model), BF16),
                   jax.ShapeDtypeStruct((N_CHIPS, SUBLANES, LANES), F32),
                   *[jax.ShapeDtypeStruct((seq, LANES), F32)] * 3),
        in_specs=[vm, vm, vm, vm], out_specs=(vm,) * 6,
        scratch_shapes=[pltpu.VMEM((d_model, width), BF16), pltpu.VMEM((rows, d_model), BF16),
                        dma((2, 2, 2)), dma((2, 2, 2)), dma((2, 2)), dma((2, 2)), dma((3,)), dma((3,)),
                        dma((3, 2)), dma((3, 2))],
        compiler_params=_params(),
    )(w_in, w_out, cw, inv_freq)


def _rope(t, cos, s1, s2):
    return t * cos + pltpu.roll(t, LANES - HEAD_DIM // 2, 1) * s1 + pltpu.roll(t, HEAD_DIM // 2, 1) * s2


def _rope_transposed(g, cos, s1, s2):
    return g * cos + pltpu.roll(g * s1, HEAD_DIM // 2, 1) + pltpu.roll(g * s2, LANES - HEAD_DIM // 2, 1)


def inproj(x, g1, w_full, tables, attn_w, conv_w):
    seq, d_model = x.shape
    width = w_full.shape[2]
    tm = ROW_TILE
    groups = _lane_groups(attn_w)

    def body(x_ref, g_ref, w_ref, cos_ref, s1_ref, s2_ref,
             ht_ref, q_ref, k_ref, v_ref, qp_ref, kp_ref, vp_ref, ga_ref, cz_ref, stage):
        xv = x_ref[...]
        hb = ((xv * lax.rsqrt(jnp.mean(xv * xv, axis=-1, keepdims=True) + NORM_EPS)) * g_ref[...]).astype(BF16)
        ht_ref[...] = jnp.transpose(hb)
        cos, s1, s2 = cos_ref[...], s1_ref[...], s2_ref[...]

        def proj(a, b):
            parts = [jnp.dot(hb, w_ref[j, :, lo:hi], preferred_element_type=F32) for j, lo, hi in _col_pieces(a, b, width)]
            return parts[0] if len(parts) == 1 else jnp.concatenate(parts, axis=1)

        def emit(z, nat_ref, perm_ref, fn):
            for g, sl in enumerate(groups):
                val = fn(z[:, sl])
                nat_ref[:, sl] = val.astype(BF16)
                _stage_put(stage, g, val)
            for g, sl in enumerate(groups):
                _to_perm(stage, g, perm_ref, sl, BF16)

        emit(proj(0, attn_w), q_ref, qp_ref, lambda t: _rope(t, cos, s1, s2) * ATTN_SCALE)
        emit(proj(attn_w, 2 * attn_w), k_ref, kp_ref, lambda t: _rope(t, cos, s1, s2))
        emit(proj(2 * attn_w, 3 * attn_w), v_ref, vp_ref, lambda t: t)
        ga_ref[...] = proj(3 * attn_w, 4 * attn_w)
        cz_ref[...] = proj(4 * attn_w, 4 * attn_w + 4 * conv_w)

    row = lambda n: pl.BlockSpec((tm, n), lambda i: (i, 0))
    nat = jax.ShapeDtypeStruct((seq, attn_w), BF16)
    perm = jax.ShapeDtypeStruct(_perm_shape(seq, attn_w), BF16)
    return pl.pallas_call(
        body, name="inproj", grid=(seq // tm,),
        out_shape=(jax.ShapeDtypeStruct((d_model, seq), BF16), nat, nat, nat, perm, perm, perm,
                   jax.ShapeDtypeStruct((seq, attn_w), F32), jax.ShapeDtypeStruct((seq, 4 * conv_w), F32)),
        in_specs=[row(d_model), _const_spec((1, d_model)), _const_spec(w_full.shape), row(LANES), row(LANES), row(LANES)],
        out_specs=(pl.BlockSpec((d_model, tm), lambda i: (0, i)), row(attn_w), row(attn_w), row(attn_w),
                   _perm_tile_spec(attn_w, tm), _perm_tile_spec(attn_w, tm), _perm_tile_spec(attn_w, tm),
                   row(attn_w), row(4 * conv_w)),
        scratch_shapes=[pltpu.VMEM(_stage_shape(len(groups), tm), F32)],
        compiler_params=_params(("arbitrary",)),
    )(x, g1, w_full, *tables)


class _Mode:
    def __init__(self, name, seq):
        self.name = name
        if name == "nat":
            self.residues, blocks = 1, seq // BLOCK
        elif name == "p16":
            self.residues, blocks = PERM, seq // PERM // BLOCK
        else:
            self.residues, blocks = PJ, seq // PERM // P4_ROWS
        self.qb = max(d for d in range(1, MAX_QUERY_BLOCKS + 1) if blocks % d == 0)
        self.steps = blocks // self.qb

    def _spec(self, blocks, width, index):
        if self.name == "nat":
            return pl.BlockSpec((blocks * BLOCK, width), lambda r, n: (index(n), 0))
        if self.name == "p16":
            return pl.BlockSpec((1, 1, blocks * BLOCK, width), lambda r, n: (r // PJ, r % PJ, index(n), 0))
        return pl.BlockSpec((PJ, 1, blocks * P4_ROWS, width), lambda r, n: (0, r, index(n), 0))

    def wide(self, width, last=None):
        return self._spec(self.qb, width, (lambda n: n) if last is None else (lambda n: jnp.minimum(n, last)))

    def wide_before(self, width):
        return self._spec(self.qb, width, lambda n: jnp.maximum(n - 1, 0))

    def block_before(self, width, last=None):
        step = (lambda n: n) if last is None else (lambda n: jnp.minimum(n, last))
        return self._spec(1, width, lambda n: jnp.maximum(self.qb * step(n) - 1, 0))

    def get(self, ref, sl, sub=0):
        if self.name == "nat":
            return ref[sub * BLOCK:(sub + 1) * BLOCK, sl]
        if self.name == "p16":
            return ref[0, 0, sub * BLOCK:(sub + 1) * BLOCK, sl]
        return jnp.concatenate([ref[j, 0, sub * P4_ROWS:(sub + 1) * P4_ROWS, sl] for j in range(PJ)], axis=0)

    def put(self, ref, sl, val, sub=0):
        val = val.astype(ref.dtype)
        if self.name == "nat":
            ref[sub * BLOCK:(sub + 1) * BLOCK, sl] = val
        elif self.name == "p16":
            ref[0, 0, sub * BLOCK:(sub + 1) * BLOCK, sl] = val
        else:
            for j in range(PJ):
                ref[j, 0, sub * P4_ROWS:(sub + 1) * P4_ROWS, sl] = val[j * P4_ROWS:(j + 1) * P4_ROWS]

    def keys(self, before_ref, wide_ref, sl, sub):
        older = self.get(before_ref, sl) if sub == 0 else self.get(wide_ref, sl, sub - 1)
        return jnp.concatenate([older, self.get(wide_ref, sl, sub)], axis=0)

    def index(self, idx, is_key):
        if self.name != "p4":
            return idx - BLOCK if is_key else idx
        within = jnp.bitwise_and(idx, BLOCK - 1)
        m = PJ * jnp.bitwise_and(within, P4_ROWS - 1) + jnp.right_shift(within, P4_ROWS.bit_length() - 1)
        return m + BLOCK * (jnp.right_shift(idx, BLOCK.bit_length() - 1) - 1) if is_key else m

    def bias(self, has_before):
        shape = (2 * BLOCK, BLOCK)
        kidx = lax.broadcasted_iota(jnp.int32, shape, 0)
        qidx = lax.broadcasted_iota(jnp.int32, shape, 1)
        rel = self.index(qidx, False) - self.index(kidx, True)
        valid = (rel >= 0) & (rel <= WINDOW_KEYS)
        if has_before is not True:
            valid = valid & ((kidx >= BLOCK) | has_before)
        one = jnp.where(valid, 0.0, NEG)
        return jnp.concatenate([one, one], axis=1)


def _head_masks():
    lane = lax.broadcasted_iota(jnp.int32, (BLOCK, LANES), 1)
    lo = lane < HEAD_DIM
    return lane, lo, jnp.where(lo, 1.0, 0.0).astype(BF16), jnp.where(lo, 0.0, 1.0).astype(BF16)


def _column(blk, lane, h):
    return jnp.sum(jnp.where(lane == h, blk, 0.0), axis=1, keepdims=True)


def attn_fwd(name, q, k, v, run):
    nat = name == "nat"
    seq = q.shape[0] if nat else q.shape[2] * PERM
    attn_w = q.shape[-1]
    mode = _Mode(name, seq)
    groups = _lane_groups(attn_w)
    first = run is None
    all_lanes = slice(0, LANES)

    def body(*refs):
        q_ref, kp_ref, kc_ref, vp_ref, vc_ref = refs[:5]
        if first:
            o_ref, l_ref = refs[5:]
        elif nat:
            oin_ref, lin_ref, ex_ref, o_ref, l_ref, ostage, lstage = refs[5:]
        else:
            oin_ref, lin_ref, ex_ref, o_ref, l_ref = refs[5:]
        n = pl.program_id(1)
        subs = range(mode.qb)
        biases = [mode.bias(n > 0)] + [mode.bias(True)] * (mode.qb - 1)
        _, lo, m_lo, m_hi = _head_masks()
        head_row = lax.broadcasted_iota(jnp.int32, (BLOCK, LANES), 0)
        ones = jnp.ones((2 * BLOCK, LANES), BF16)
        lrows = [jnp.zeros((BLOCK, LANES), F32) for _ in subs]

        def probs(sub, sl):
            q2 = mode.get(q_ref, sl, sub)
            kcat = mode.keys(kp_ref, kc_ref, sl, sub)
            vcat = mode.keys(vp_ref, vc_ref, sl, sub)
            qq = jnp.concatenate([q2 * m_lo, q2 * m_hi], axis=0)
            s_t = _nt(kcat, qq) + biases[sub]
            m = jnp.max(s_t, axis=0, keepdims=True)
            pe = jnp.exp(s_t - m)
            l = jnp.sum(pe, axis=0, keepdims=True)
            return jnp.concatenate([vcat, ones], axis=1), pe.astype(BF16), m + jnp.log(l)

        def output(sub, p, sl, vext, pb, lse):
            o_ext = _tn(pb, vext)
            o_new = o_ext[:, :LANES] / o_ext[:, LANES:]
            mode.put(o_ref, sl, jnp.where(lo, o_new[:BLOCK], o_new[BLOCK:]), sub)
            rows = jnp.where(head_row == 2 * p, lse[:, :BLOCK], lrows[sub])
            lrows[sub] = jnp.where(head_row == 2 * p + 1, lse[:, BLOCK:], rows)

        pending = None
        for sub in subs:
            for p, sl in enumerate(groups):
                nxt = probs(sub, sl)
                if pending is not None:
                    output(*pending)
                pending = (sub, p, sl, *nxt)
        output(*pending)
        if not first and nat:
            for g, sl in enumerate(groups):
                _from_perm(oin_ref, sl, ostage, g)
            _from_perm(lin_ref, all_lanes, lstage, 0)
        for sub in subs:
            rows = slice(sub * BLOCK, (sub + 1) * BLOCK)
            lblk = jnp.transpose(lrows[sub])
            if first:
                mode.put(l_ref, all_lanes, lblk, sub)
                continue
            lin = _stage_get(lstage, 0)[rows] if nat else mode.get(lin_ref, all_lanes, sub)
            mx = jnp.maximum(lin, lblk)
            new = mx + jnp.log(jnp.exp(lin - mx) + jnp.exp(lblk - mx))
            mode.put(l_ref, all_lanes, new, sub)

            def expand(w):
                hi = w.astype(BF16)
                rest = (w - hi.astype(F32)).astype(BF16)
                ex = ex_ref[...]
                return jnp.dot(hi, ex, preferred_element_type=F32) + jnp.dot(rest, ex, preferred_element_type=F32)

            w_prev, w_cur = expand(jnp.exp(lin - new)), expand(jnp.exp(lblk - new))
            for p, sl in enumerate(groups):
                o_prev = _stage_get(ostage, p)[rows] if nat else mode.get(oin_ref, sl, sub)
                mode.put(o_ref, sl, w_prev[:, sl] * o_prev + w_cur[:, sl] * mode.get(o_ref, sl, sub), sub)

    ins = [q, k, k, v, v]
    specs = [mode.wide(attn_w), mode.block_before(attn_w), mode.wide(attn_w), mode.block_before(attn_w), mode.wide(attn_w)]
    scratch = []
    if not first:
        ins += list(run)
        if nat:
            rows_a = mode.qb * BLOCK // PERM
            specs += [pl.BlockSpec((PJ, PJ, rows_a, attn_w), lambda r, n: (0, 0, n, 0)),
                      pl.BlockSpec((PJ, PJ, rows_a, LANES), lambda r, n: (0, 0, n, 0))]
            scratch = [pltpu.VMEM(_stage_shape(len(groups), mode.qb * BLOCK), F32),
                       pltpu.VMEM(_stage_shape(1, mode.qb * BLOCK), F32)]
        else:
            specs += [mode.wide(attn_w), mode.wide(LANES)]
        head_of_lane = jnp.arange(attn_w, dtype=jnp.int32) // HEAD_DIM
        ins.append((jnp.arange(LANES, dtype=jnp.int32)[:, None] == head_of_lane[None, :]).astype(BF16))
        specs.append(_const_spec((LANES, attn_w)))
    if nat:
        out_shape = (jax.ShapeDtypeStruct((seq, attn_w), F32), jax.ShapeDtypeStruct((seq, LANES), F32))
    else:
        out_shape = (jax.ShapeDtypeStruct(_perm_shape(seq, attn_w), F32), jax.ShapeDtypeStruct(_perm_shape(seq, LANES), F32))
    return pl.pallas_call(
        body, name=f"attn_fwd_{name}", grid=(mode.residues, mode.steps),
        out_shape=out_shape, in_specs=specs, out_specs=(mode.wide(attn_w), mode.wide(LANES)),
        scratch_shapes=scratch,
        compiler_params=_params(("arbitrary", "arbitrary")),
    )(*ins)


def attn_bwd(name, q, k, v, d_o, lse, delta, run):
    nat = name == "nat"
    seq = q.shape[0] if nat else q.shape[2] * PERM
    attn_w = q.shape[-1]
    mode = _Mode(name, seq)
    steps, qb = mode.steps, mode.qb
    single = steps == 1
    groups = _lane_groups(attn_w)
    first = run is None
    all_lanes = slice(0, LANES)

    def body(*refs):
        q_ref, kp_ref, kc_ref, vp_ref, vc_ref, do_ref, lse_ref, dl_ref = refs[:8]
        if first:
            dq_ref, dk_ref, dv_ref, ck, cv = refs[8:]
        else:
            dqi_ref, dki_ref, dvi_ref, dq_ref, dk_ref, dv_ref, ck, cv = refs[8:]
        n = pl.program_id(1)
        carries = ((ck, dk_ref, None if first else dki_ref), (cv, dv_ref, None if first else dvi_ref))

        def emit(out_ref, acc_ref, sl, sub, val):
            if acc_ref is not None:
                val = val + mode.get(acc_ref, sl, sub).astype(F32)
            mode.put(out_ref, sl, val, sub)

        if not single:
            @pl.when(n == 0)
            def _():
                ck[...] = jnp.zeros_like(ck)
                cv[...] = jnp.zeros_like(cv)

        @pl.when(n < steps)
        def _():
            biases = [mode.bias(n > 0)] + [mode.bias(True)] * (qb - 1)
            _, lo, m_lo, m_hi = _head_masks()

            def scores(sub, p, sl, lse_t, dl_t):
                q2, do2 = mode.get(q_ref, sl, sub), mode.get(do_ref, sl, sub)
                kcat = mode.keys(kp_ref, kc_ref, sl, sub)
                vcat = mode.keys(vp_ref, vc_ref, sl, sub)
                qq = jnp.concatenate([q2 * m_lo, q2 * m_hi], axis=0)
                dd = jnp.concatenate([do2 * m_lo, do2 * m_hi], axis=0)
                h0 = 2 * p
                lse2 = jnp.concatenate([lse_t[h0:h0 + 1, :], lse_t[h0 + 1:h0 + 2, :]], axis=1)
                dl2 = jnp.concatenate([dl_t[h0:h0 + 1, :], dl_t[h0 + 1:h0 + 2, :]], axis=1)
                p_t = jnp.exp(_nt(kcat, qq) + (biases[sub] - lse2))
                ds_t = p_t * (_nt(vcat, dd) - dl2)
                return qq, dd, kcat, p_t.astype(BF16), ds_t.astype(BF16)

            def grads(sub, sl, qq, dd, kcat, pb, dsb):
                dqb = _tn(dsb, kcat)
                dq2 = jnp.where(lo, dqb[:BLOCK], dqb[BLOCK:]) * ATTN_SCALE
                if not first:
                    dq2 = dq2 + mode.get(dqi_ref, sl, sub).astype(F32)
                mode.put(dq_ref, sl, dq2, sub)
                for (carry, out_ref, acc_ref), lhs, rhs in zip(carries, (dsb, pb), (qq, dd)):
                    both = jnp.dot(lhs, rhs, preferred_element_type=F32)
                    if sub == 0:
                        if not single:
                            for s in range(qb - 1):
                                emit(out_ref, acc_ref, sl, s, carry[s, :, sl])
                            emit(out_ref, acc_ref, sl, qb - 1, carry[qb - 1, :, sl] + both[:BLOCK])
                        carry[0, :, sl] = both[BLOCK:]
                    else:
                        carry[sub - 1, :, sl] += both[:BLOCK]
                        carry[sub, :, sl] = both[BLOCK:]
                    if single and sub == qb - 1:
                        for s in range(qb):
                            emit(out_ref, acc_ref, sl, s, carry[s, :, sl])

            stats = [(jnp.transpose(mode.get(lse_ref, all_lanes, sub)),
                      jnp.transpose(mode.get(dl_ref, all_lanes, sub))) for sub in range(qb)]
            pending = None
            for p, sl in enumerate(groups):
                for sub in range(qb):
                    nxt = scores(sub, p, sl, *stats[sub])
                    if pending is not None:
                        grads(*pending)
                    pending = (sub, sl, *nxt)
            grads(*pending)

        if not single:
            @pl.when(n == steps)
            def _():
                for carry, out_ref, acc_ref in carries:
                    for sl in groups:
                        for s in range(qb):
                            emit(out_ref, acc_ref, sl, s, carry[s, :, sl])

    last = steps - 1
    wide = lambda w: mode.wide(w, last)
    ins = [q, k, k, v, v, d_o, lse, delta]
    specs = [wide(attn_w), mode.block_before(attn_w, last), wide(attn_w), mode.block_before(attn_w, last), wide(attn_w),
             wide(attn_w), wide(LANES), wide(LANES)]
    if not first:
        ins += list(run)
        specs += [wide(attn_w), mode.wide_before(attn_w), mode.wide_before(attn_w)]
    shp = jax.ShapeDtypeStruct((seq, attn_w) if nat else _perm_shape(seq, attn_w), BF16)
    return pl.pallas_call(
        body, name=f"attn_bwd_{name}", grid=(mode.residues, steps if single else steps + 1),
        out_shape=(shp, shp, shp), in_specs=specs,
        out_specs=(wide(attn_w), mode.wide_before(attn_w), mode.wide_before(attn_w)),
        scratch_shapes=[pltpu.VMEM((qb, BLOCK, attn_w), F32), pltpu.VMEM((qb, BLOCK, attn_w), F32)],
        compiler_params=_params(("arbitrary", "arbitrary")),
    )(*ins)


def _shift_down(u, halo, k):
    rolled = pltpu.roll(u, k, 0)
    row = lax.broadcasted_iota(jnp.int32, halo.shape, 0)
    top = jnp.where(row < k, pltpu.roll(halo, k, 0), rolled[:SUBLANES])
    return jnp.concatenate([top, rolled[SUBLANES:]], axis=0)


def _shift_up(u, halo, k):
    rows = u.shape[0]
    rolled = pltpu.roll(u, rows - k, 0)
    row = lax.broadcasted_iota(jnp.int32, halo.shape, 0)
    bot = jnp.where(row >= SUBLANES - k, pltpu.roll(halo, SUBLANES - k, 0), rolled[rows - SUBLANES:])
    return jnp.concatenate([rolled[:rows - SUBLANES], bot], axis=0)


def tail(o, lse, ga, cz, x, tgt, w_out, g2, cw):
    seq, d_model = x.shape
    attn_w = o.shape[1]
    conv_w = cz.shape[1] // 4
    mix = attn_w + conv_w
    groups = _lane_groups(attn_w)
    tm = ROW_TILE
    nt = seq // tm
    hb = tm // SUBLANES

    def body(o_ref, l_ref, ga_ref, cz_ref, hz_ref, x_ref, t_ref, w_ref, g_ref, cw_ref,
             do_ref, dl_ref, dop_ref, dlp_ref, lp_ref, dga_ref, dcb_ref, dgc_ref, dcv_ref, e_ref,
             dw_ref, dg_ref, dcw_ref, loss_ref, stage):
        i = pl.program_id(0)

        @pl.when(i == 0)
        def _():
            dw_ref[...] = jnp.zeros_like(dw_ref)
            dg_ref[...] = jnp.zeros_like(dg_ref)
            dcw_ref[...] = jnp.zeros_like(dcw_ref)
            loss_ref[...] = jnp.zeros_like(loss_ref)

        u = cz_ref[:, 2 * conv_w:3 * conv_w] * cz_ref[:, 0:conv_w]
        uh = hz_ref[:, 2 * conv_w:3 * conv_w] * hz_ref[:, 0:conv_w]
        uh = jnp.where(i > 0, uh, 0.0)
        u1 = _shift_down(u, uh, 1)
        u2 = _shift_down(u, uh, 2)
        w0, w1, w2 = cw_ref[0:1, :], cw_ref[1:2, :], cw_ref[2:3, :]
        cvv = u2 * w0 + u1 * w1 + u * w2
        gv = g_ref[...]
        all_lanes = slice(0, LANES)

        def forward(rs):
            ov, gav = o_ref[rs, :], ga_ref[rs, :]
            sig_a = _sigmoid(gav)
            silu_a = gav * sig_a
            cb, gc = cz_ref[rs, conv_w:2 * conv_w], cz_ref[rs, 3 * conv_w:4 * conv_w]
            sig_c = _sigmoid(gc)
            silu_c = gc * sig_c
            bc = cb * cvv[rs]
            mixed = jnp.concatenate([ov * silu_a, bc * silu_c], axis=1).astype(BF16)
            yv = jnp.dot(mixed, w_ref[...], preferred_element_type=F32)
            return ov, gav, sig_a, silu_a, cb, gc, sig_c, silu_c, bc, mixed, yv

        def loss_and_dy(rs, mixed, yv):
            r2 = lax.rsqrt(jnp.mean(yv * yv, axis=-1, keepdims=True) + NORM_EPS)
            yhat = yv * r2
            diff = (x_ref[rs, :] + yhat * gv) - t_ref[rs, :]
            loss_ref[...] += _rowgroup_sum(diff * diff)
            ev = diff * (1.0 / d_model)
            e_ref[rs, :] = ev
            dg_ref[...] += _rowgroup_sum(ev * yhat)
            eg = ev * gv
            dy = (r2 * (eg - yhat * jnp.mean(eg * yhat, axis=-1, keepdims=True))).astype(BF16)
            dw_ref[...] += _tn(mixed, dy)
            return _nt(dy, w_ref[...])

        def backward(rs, ov, gav, sig_a, silu_a, cb, gc, sig_c, silu_c, bc, dm):
            rows = rs.stop - rs.start
            dma, dmc = dm[:, :attn_w], dm[:, attn_w:]
            dov = dma * silu_a
            do_ref[rs, :] = dov.astype(BF16)
            dga_ref[rs, :] = (dma * ov * (sig_a * (1.0 + gav * (1.0 - sig_a)))).astype(BF16)
            prod = dov * ov
            lane = lax.broadcasted_iota(jnp.int32, (rows, LANES), 1)
            lo = lane < HEAD_DIM
            dblk = jnp.zeros((rows, LANES), F32)
            for p, sl in enumerate(groups):
                pr = prod[:, sl]
                dblk = jnp.where(lane == 2 * p, jnp.sum(jnp.where(lo, pr, 0.0), axis=1, keepdims=True), dblk)
                dblk = jnp.where(lane == 2 * p + 1, jnp.sum(jnp.where(lo, 0.0, pr), axis=1, keepdims=True), dblk)
                _stage_put(stage, p, dov[:, sl], rs.start)
            dl_ref[rs, :] = dblk
            _stage_put(stage, len(groups), dblk, rs.start)
            _stage_put(stage, len(groups) + 1, l_ref[rs, :], rs.start)
            dsc = dmc * silu_c
            cv_rows = cvv[rs]
            dcb_ref[rs, :] = (dsc * cv_rows).astype(BF16)
            dgc_ref[rs, :] = (dmc * bc * (sig_c * (1.0 + gc * (1.0 - sig_c)))).astype(BF16)
            dcv = dsc * cb
            dcv_ref[rs, :] = dcv
            dcw_ref[0:SUBLANES, :] += _rowgroup_sum(dcv * u2[rs])
            dcw_ref[SUBLANES:2 * SUBLANES, :] += _rowgroup_sum(dcv * u1[rs])
            dcw_ref[2 * SUBLANES:3 * SUBLANES, :] += _rowgroup_sum(dcv * u[rs])

        halves = [slice(0, tm // 2), slice(tm // 2, tm)]
        fwd = [forward(rs) for rs in halves]
        dms = [loss_and_dy(rs, f[9], f[10]) for rs, f in zip(halves, fwd)]
        for rs, f, dm in zip(halves, fwd, dms):
            backward(rs, *f[:9], dm)
        for p, sl in enumerate(groups):
            _to_perm(stage, p, dop_ref, sl, BF16)
        _to_perm(stage, len(groups), dlp_ref, all_lanes, F32)
        _to_perm(stage, len(groups) + 1, lp_ref, all_lanes, F32)

    row = lambda n: pl.BlockSpec((tm, n), lambda i: (i, 0))
    whole = lambda a, b: pl.BlockSpec((a, b), lambda i: (0, 0))
    return pl.pallas_call(
        body, name="tail", grid=(nt,),
        out_shape=(jax.ShapeDtypeStruct((seq, attn_w), BF16), jax.ShapeDtypeStruct((seq, LANES), F32),
                   jax.ShapeDtypeStruct(_perm_shape(seq, attn_w), BF16), jax.ShapeDtypeStruct(_perm_shape(seq, LANES), F32),
                   jax.ShapeDtypeStruct(_perm_shape(seq, LANES), F32),
                   jax.ShapeDtypeStruct((seq, attn_w), BF16), jax.ShapeDtypeStruct((seq, conv_w), BF16),
                   jax.ShapeDtypeStruct((seq, conv_w), BF16), jax.ShapeDtypeStruct((seq, conv_w), F32),
                   jax.ShapeDtypeStruct((seq, d_model), F32), jax.ShapeDtypeStruct((mix, d_model), F32),
                   jax.ShapeDtypeStruct((SUBLANES, d_model), F32), jax.ShapeDtypeStruct((CONV_K * SUBLANES, conv_w), F32),
                   jax.ShapeDtypeStruct((SUBLANES, d_model), F32)),
        in_specs=[row(attn_w), row(LANES), row(attn_w), row(4 * conv_w),
                  pl.BlockSpec((SUBLANES, 4 * conv_w), lambda i: (jnp.maximum(i * hb - 1, 0), 0)),
                  row(d_model), row(d_model), _const_spec((mix, d_model)), _const_spec((1, d_model)),
                  _const_spec((SUBLANES, conv_w))],
        out_specs=(row(attn_w), row(LANES), _perm_tile_spec(attn_w, tm), _perm_tile_spec(LANES, tm), _perm_tile_spec(LANES, tm),
                   row(attn_w), row(conv_w), row(conv_w), row(conv_w), row(d_model),
                   whole(mix, d_model), whole(SUBLANES, d_model), whole(CONV_K * SUBLANES, conv_w),
                   whole(SUBLANES, d_model)),
        scratch_shapes=[pltpu.VMEM(_stage_shape(len(groups) + 2, tm), F32)],
        compiler_params=_params(("arbitrary",)),
    )(o, lse, ga, cz, cz, x, tgt, w_out, g2, cw)


def dz_dx(nat_grads, perm_grads, dga, dcb, dgc, dcv, cz, tables, x, g1, e, w_full, cw):
    seq, d_model = x.shape
    attn_w = dga.shape[1]
    conv_w = dcv.shape[1]
    width = w_full.shape[2]
    in_w = 4 * attn_w + 4 * conv_w
    groups = _lane_groups(attn_w)
    tm = DZ_ROW_TILE
    nt = seq // tm
    hb = tm // SUBLANES

    def body(dq_ref, dk_ref, dv_ref, dqp_ref, dkp_ref, dvp_ref, dga_ref, dcb_ref, dgc_ref, dcv_ref, nh_ref, ch_ref, cc_ref,
             cos_ref, s1_ref, s2_ref, x_ref, g_ref, e_ref, w_ref, cw_ref, gx_ref, dz_ref, dg_ref, stage):
        i = pl.program_id(0)

        @pl.when(i == 0)
        def _():
            dg_ref[...] = jnp.zeros_like(dg_ref)

        cos, s1, s2 = cos_ref[...], s1_ref[...], s2_ref[...]

        def qkv_columns(t, nat_ref, perm_ref):
            for g, sl in enumerate(groups):
                _from_perm(perm_ref, sl, stage, g)
            for g, sl in enumerate(groups):
                tot = nat_ref[:, sl].astype(F32) + _stage_get(stage, g)
                if t < 2:
                    tot = _rope_transposed(tot, cos, s1, s2)
                dz_ref[:, t * attn_w + g * LANES:t * attn_w + (g + 1) * LANES] = tot.astype(BF16)

        def dh_part(j):
            return _nt(dz_ref[:, j * width:(j + 1) * width], w_ref[j])

        dcv = dcv_ref[...]
        nh = jnp.where(i < nt - 1, nh_ref[...], 0.0)
        w0, w1, w2 = cw_ref[0:1, :], cw_ref[1:2, :], cw_ref[2:3, :]
        du = dcv * w2 + _shift_up(dcv, nh, 1) * w1 + _shift_up(dcv, nh, 2) * w0
        base = 4 * attn_w
        dz_ref[:, base:base + conv_w] = (du * cc_ref[...]).astype(BF16)
        dz_ref[:, base + conv_w:base + 2 * conv_w] = dcb_ref[...]
        dz_ref[:, base + 2 * conv_w:base + 3 * conv_w] = (du * ch_ref[...]).astype(BF16)
        dz_ref[:, base + 3 * conv_w:base + 4 * conv_w] = dgc_ref[...]
        dz_ref[:, 3 * attn_w:4 * attn_w] = dga_ref[...]
        ready = in_w
        dh = None
        for t, nat_ref, perm_ref in ((2, dv_ref, dvp_ref), (1, dk_ref, dkp_ref), (0, dq_ref, dqp_ref), (None, None, None)):
            lowest_open = 0 if t is None else (t + 1) * attn_w
            while ready - width >= lowest_open:
                ready -= width
                part = dh_part(ready // width)
                dh = part if dh is None else dh + part
            if t is not None:
                qkv_columns(t, nat_ref, perm_ref)
        xv = x_ref[...]
        r1 = lax.rsqrt(jnp.mean(xv * xv, axis=-1, keepdims=True) + NORM_EPS)
        xhat = xv * r1
        dg_ref[...] += _rowgroup_sum(dh * xhat)
        dhg = dh * g_ref[...]
        gx_ref[...] = r1 * (dhg - xhat * jnp.mean(dhg * xhat, axis=-1, keepdims=True)) + e_ref[...]

    row = lambda n: pl.BlockSpec((tm, n), lambda i: (i, 0))
    whole = lambda a, b: pl.BlockSpec((a, b), lambda i: (0, 0))
    pt = _perm_tile_spec(attn_w, tm)
    return pl.pallas_call(
        body, name="dz_dx", grid=(nt,),
        out_shape=(jax.ShapeDtypeStruct((seq, d_model), F32), jax.ShapeDtypeStruct((seq, in_w), BF16),
                   jax.ShapeDtypeStruct((SUBLANES, d_model), F32)),
        in_specs=[row(attn_w), row(attn_w), row(attn_w), pt, pt, pt, row(attn_w), row(conv_w), row(conv_w), row(conv_w),
                  pl.BlockSpec((SUBLANES, conv_w), lambda i: (jnp.minimum((i + 1) * hb, seq // SUBLANES - 1), 0)),
                  pl.BlockSpec((tm, conv_w), lambda i: (i, 0)), pl.BlockSpec((tm, conv_w), lambda i: (i, 2)),
                  row(LANES), row(LANES), row(LANES), row(d_model), _const_spec((1, d_model)), row(d_model),
                  _const_spec(w_full.shape), _const_spec((SUBLANES, conv_w))],
        out_specs=(row(d_model), row(in_w), whole(SUBLANES, d_model)),
        scratch_shapes=[pltpu.VMEM(_stage_shape(len(groups), tm), F32)],
        compiler_params=_params(("arbitrary",)),
    )(*nat_grads, *perm_grads, dga, dcb, dgc, dcv, dcv, cz, cz, *tables, x, g1, e, w_full, cw)


def dw_in_reduce(ht, dz, g_out, small):
    d_model, seq = ht.shape
    half = dz.shape[1] // N_DEV
    ts = min(2048, seq)
    steps = seq // ts
    x, y, c = lax.axis_index("x"), lax.axis_index("y"), lax.axis_index("c")
    far_first = lambda x, y: [(1 - x, 1 - y), (1 - x, y), (x, 1 - y)]
    chips = jnp.stack([2 * px + py for px, py in far_first(x, y)] + [2 * x + y]).astype(jnp.int32)
    order = jnp.stack([2 * chips + (1 - c), 2 * chips + c], axis=1).reshape(N_DEV)

    def body(order_ref, ht_ref, dz_ref, go_ref, sm_ref, out_ref, ro_ref, rs_ref,
             acc, theirs, staged, contrib, resbuf, out_sem, sa, ra, sb, rb, sc, rc,
             o_mine, o_theirs, o_staged, o_contrib, o_res, sbuf, o_load, osa, ora, osb, orb, osc, orc, ss, rs):
        del order_ref
        p, s = pl.program_id(0), pl.program_id(1)
        x, y, c = lax.axis_index("x"), lax.axis_index("y"), lax.axis_index("c")
        me = 2 * x + y
        sib = (x, y, 1 - c)
        peers = far_first(x, y)
        slot = p % 2

        flips = [(fx, fy, fc) for fx in (0, 1) for fy in (0, 1) for fc in (0, 1)][1:]
        my8 = 4 * x + 2 * y + c
        chip_ids = [2 * px + py for px, py in peers] + [me]

        def small_copy(k, slot8, to):
            return pltpu.make_async_remote_copy(src_ref=sm_ref, dst_ref=sbuf.at[slot8], send_sem=ss.at[k], recv_sem=rs.at[k],
                                                device_id=to, device_id_type=MESH)

        def small_peer(k):
            fx, fy, fc = flips[k]
            return _flip(x, fx), _flip(y, fy), _flip(c, fc)

        def oa_copy(pos):
            j = chip_ids[pos]
            return pltpu.make_async_remote_copy(src_ref=go_ref.at[j, 1 - c], dst_ref=o_theirs.at[j], send_sem=osa.at[pos],
                                                recv_sem=ora.at[pos], device_id=sib, device_id_type=MESH)

        def o_load_copy(pos):
            j = chip_ids[pos]
            return pltpu.make_async_copy(go_ref.at[j, c], o_mine.at[j], o_load.at[pos])

        def ob_copy(k, piece, slot4):
            px, py = peers[k]
            return pltpu.make_async_remote_copy(src_ref=o_staged.at[piece], dst_ref=o_contrib.at[slot4], send_sem=osb.at[k],
                                                recv_sem=orb.at[k], device_id=(px, py, c), device_id_type=MESH)

        def oc_copy(which):
            return pltpu.make_async_remote_copy(src_ref=o_res.at[which], dst_ref=o_res.at[which], send_sem=osc, recv_sem=orc,
                                                device_id=sib, device_id_type=MESH)

        @pl.when((p == 0) & (s == 0))
        def _():
            sbuf[my8] = sm_ref[...]
            for k in range(N_DEV - 1):
                small_copy(k, my8, small_peer(k)).start()
            for pos in range(N_CHIPS):
                o_load_copy(pos).start()
                oa_copy(pos).start()

        @pl.when((p == 1) & (s == steps - 1))
        def _():
            for pos in range(N_CHIPS):
                j = chip_ids[pos]
                o_load_copy(pos).wait()
                oa_copy(pos).wait_recv()
                if pos < N_CHIPS - 1:
                    o_staged[j] = (o_mine[j] + o_theirs[j]).astype(BF16)
                    ob_copy(pos, j, me).start()
                else:
                    o_mine[j] = o_mine[j] + o_theirs[j]
                    o_contrib[j] = o_mine[j].astype(BF16)

        @pl.when((p == 4) & (s == steps - 1))
        def _():
            for k in range(N_CHIPS - 1):
                ob_copy(k, me, chip_ids[k]).wait_recv()
            own = o_mine[me]
            term = lambda j: jnp.where(me == j, own, o_contrib[j].astype(F32))
            o_res[c] = ((term(0) + term(1)) + term(2)) + term(3)
            oc_copy(c).start()

        def a_copy(k):
            return pltpu.make_async_remote_copy(src_ref=acc.at[0], dst_ref=theirs.at[k], send_sem=sa.at[k], recv_sem=ra.at[k],
                                                device_id=sib, device_id_type=MESH)

        def b_copy(k):
            px, py = peers[k]
            return pltpu.make_async_remote_copy(src_ref=staged.at[k], dst_ref=contrib.at[k], send_sem=sb.at[k], recv_sem=rb.at[k],
                                                device_id=(px, py, c), device_id_type=MESH)

        def c_copy(which):
            return pltpu.make_async_remote_copy(src_ref=resbuf.at[which], dst_ref=resbuf.at[which], send_sem=sc, recv_sem=rc,
                                                device_id=sib, device_id_type=MESH)

        @pl.when(s == 0)
        def _():
            for k in range(N_CHIPS - 1):
                @pl.when(p == 2 * k + 2)
                def _():
                    a_copy(k).wait_send()
            acc[slot] = jnp.zeros((d_model, half), F32)

        acc[slot] += jnp.dot(ht_ref[...], dz_ref[...], preferred_element_type=F32)

        @pl.when(s == steps - 1)
        def _():
            for k in range(N_CHIPS):
                @pl.when(p == 2 * k)
                def _():
                    a_copy(k).start()
            for k in range(N_CHIPS - 1):
                @pl.when(p == 2 * k + 1)
                def _():
                    a_copy(k).wait_recv()
                    staged[k] = (acc[1] + theirs[k]).astype(BF16)
                    b_copy(k).start()

            @pl.when(p == N_DEV - 1)
            def _():
                a_copy(N_CHIPS - 1).wait_recv()
                tot = acc[1] + theirs[N_CHIPS - 1]
                for k in range(N_CHIPS - 1):
                    b_copy(k).wait_recv()
                    tot = tot + contrib[k].astype(F32)
                resbuf[c] = tot
                c_copy(c).start()
                c_copy(1 - c).wait_recv()
                done = pltpu.make_async_copy(resbuf, out_ref, out_sem)
                done.start()
                oc_copy(1 - c).wait_recv()
                ro_ref[...] = o_res[...]
                for k in range(N_DEV - 1):
                    px, py, pc = small_peer(k)
                    small_copy(k, 4 * px + 2 * py + pc, (px, py, pc)).wait_recv()
                tot8 = sbuf[0]
                for d in range(1, N_DEV):
                    tot8 = tot8 + sbuf[d]
                rs_ref[...] = tot8
                a_copy(N_CHIPS - 1).wait_send()
                for k in range(N_CHIPS - 1):
                    b_copy(k).wait_send()
                    ob_copy(k, chip_ids[k], me).wait_send()
                c_copy(c).wait_send()
                oc_copy(c).wait_send()
                for pos in range(N_CHIPS):
                    oa_copy(pos).wait_send()
                for k in range(N_DEV - 1):
                    small_copy(k, my8, small_peer(k)).wait_send()
                done.wait()

    dma = pltpu.SemaphoreType.DMA
    o_shape = g_out.shape[1:]
    go = g_out.reshape(N_CHIPS, 2, *o_shape)
    const = lambda shape: pl.BlockSpec(shape, lambda p, s, order_ref: (0,) * len(shape))
    grid_spec = pltpu.PrefetchScalarGridSpec(
        num_scalar_prefetch=1, grid=(N_DEV, steps),
        in_specs=[pl.BlockSpec((d_model, ts), lambda p, s, order_ref: (0, s)),
                  pl.BlockSpec((ts, half), lambda p, s, order_ref: (s, order_ref[p])),
                  pl.BlockSpec(memory_space=pl.ANY), const(small.shape)],
        out_specs=(pl.BlockSpec(memory_space=pl.ANY), const((2, *o_shape)), const(small.shape)),
        scratch_shapes=[pltpu.VMEM((2, d_model, half), F32), pltpu.VMEM((N_CHIPS, d_model, half), F32),
                        pltpu.VMEM((N_CHIPS - 1, d_model, half), BF16), pltpu.VMEM((N_CHIPS - 1, d_model, half), BF16),
                        pltpu.VMEM((2, d_model, half), F32), dma,
                        dma((N_CHIPS,)), dma((N_CHIPS,)), dma((N_CHIPS - 1,)), dma((N_CHIPS - 1,)), dma, dma,
                        pltpu.VMEM((N_CHIPS, *o_shape), F32), pltpu.VMEM((N_CHIPS, *o_shape), F32),
                        pltpu.VMEM((N_CHIPS, *o_shape), BF16), pltpu.VMEM((N_CHIPS, *o_shape), BF16),
                        pltpu.VMEM((2, *o_shape), F32), pltpu.VMEM((N_DEV, *small.shape), F32),
                        dma((N_CHIPS,)), dma((N_CHIPS,)), dma((N_CHIPS,)), dma((N_CHIPS - 1,)), dma((N_CHIPS - 1,)), dma, dma,
                        dma((N_DEV - 1,)), dma((N_DEV - 1,))])
    return pl.pallas_call(
        body, name="dw_in_reduce", grid_spec=grid_spec,
        out_shape=(jax.ShapeDtypeStruct((2, d_model, half), F32), jax.ShapeDtypeStruct((2, *o_shape), F32),
                   jax.ShapeDtypeStruct(small.shape, F32)),
        compiler_params=_params(("arbitrary", "arbitrary")),
    )(order, ht, dz, go, small)


def _adam_math(w, g, m, v):
    m = ADAM_B1 * m + (1.0 - ADAM_B1) * g
    v = ADAM_B2 * v + (1.0 - ADAM_B2) * (g * g)
    m_hat = m / (1.0 - ADAM_B1 ** ADAM_STEP)
    v_hat = v / (1.0 - ADAM_B2 ** ADAM_STEP)
    delta = -ADAM_LR * (m_hat / (jnp.sqrt(v_hat) + ADAM_EPS) + ADAM_WD * w)
    return delta, m, v


def adam_shard(name, w, g2, m, v, block, grid, w_map, g_map):
    def body(w_ref, g_ref, m_ref, v_ref, go_ref, d_ref, mo_ref, vo_ref):
        g = g_ref[0]
        delta, mn, vn = _adam_math(w_ref[...], g, m_ref[...], v_ref[...])
        go_ref[...] = g
        d_ref[...] = delta
        mo_ref[...] = mn
        vo_ref[...] = vn

    ws = pl.BlockSpec(block, w_map)
    shp = jax.ShapeDtypeStruct(w.shape, F32)
    return pl.pallas_call(
        body, name=name, grid=grid, out_shape=(shp, shp, shp, shp),
        in_specs=[ws, pl.BlockSpec((1, *block), g_map), ws, ws], out_specs=(ws, ws, ws, ws),
        compiler_params=_params(("arbitrary",) * len(grid)),
    )(w, g2, m, v)


def adam_small(ws, gs, ms, vs):
    n = len(ws)

    def body(*refs):
        ins, outs = refs[:4 * n], refs[4 * n:]
        for t in range(n):
            delta, mn, vn = _adam_math(ins[t][...], ins[n + t][...], ins[2 * n + t][...], ins[3 * n + t][...])
            outs[3 * t][...] = delta
            outs[3 * t + 1][...] = mn
            outs[3 * t + 2][...] = vn

    vm = pl.BlockSpec(memory_space=pltpu.VMEM)
    outs = pl.pallas_call(
        body, name="adam_small",
        out_shape=tuple(jax.ShapeDtypeStruct(w.shape, F32) for w in ws for _ in range(3)),
        in_specs=[vm] * (4 * n), out_specs=tuple([vm] * (3 * n)),
        compiler_params=_params(),
    )(*ws, *gs, *ms, *vs)
    return [outs[3 * t:3 * t + 3] for t in range(n)]


def kernel(x, norm_pre_g, w_in, conv_w, w_out, norm_post_g, loss_target, m_norm_pre_g, m_w_in, m_conv_w, m_w_out, m_norm_post_g, v_norm_pre_g, v_w_in, v_conv_w, v_w_out, v_norm_post_g):
    _, seq, d_model = x.shape
    width = w_in.shape[1]
    conv_q = conv_w.shape[1]
    conv_width = N_CHIPS * conv_q
    attn_width = d_model - conv_width
    xs, tg = x[0], loss_target[0]
    g1, g2 = norm_pre_g.reshape(1, d_model), norm_post_g.reshape(1, d_model)

    w_full, wout_full, cw_full, *tables = gather_weights(w_in, w_out, conv_w, seq)
    wout2 = wout_full.reshape(attn_width + conv_width, d_model)
    cw = jnp.zeros((SUBLANES, conv_width), F32).at[:CONV_K].set(
        cw_full[:, :CONV_K, :conv_q].transpose(1, 0, 2).reshape(CONV_K, conv_width))

    ht, q, k, v, qp, kp, vp, ga, cz = inproj(xs, g1, w_full, tables, attn_width, conv_width)
    run = attn_fwd("p4", qp, kp, vp, None)
    run = attn_fwd("p16", qp, kp, vp, run)
    o, lse = attn_fwd("nat", q, k, v, run)
    (d_o, delta, d_op, delta_p, lse_p, dga, dcb, dgc, dcv, e, dwout, dg2, dcw, loss_acc) = tail(
        o, lse, ga, cz, xs, tg, wout2, g2, cw)
    nat_grads = attn_bwd("nat", q, k, v, d_o, lse, delta, None)
    perm_grads = attn_bwd("p4", qp, kp, vp, d_op, lse_p, delta_p, None)
    perm_grads = attn_bwd("p16", qp, kp, vp, d_op, lse_p, delta_p, perm_grads)
    grad_x, dz, dg1 = dz_dx(nat_grads, perm_grads, dga, dcb, dgc, dcv, cz, tables, xs, g1, e, w_full, cw)

    small = jnp.zeros((SUBLANES, d_model), F32)
    small = small.at[0].set(dg1.sum(axis=0)).at[1].set(dg2.sum(axis=0))
    small = small.at[2:2 + CONV_K, :conv_width].set(dcw.reshape(CONV_K, SUBLANES, conv_width).sum(axis=1))
    small = small.at[2 + CONV_K, 0].set(jnp.sum(loss_acc))
    rin, rout, rsmall = dw_in_reduce(ht, dz, dwout.reshape(N_DEV, -1, d_model), small)

    half = width // 2
    tr = 256
    gw_in, d_in, m_in, v_in = adam_shard(
        "adam_w_in", w_in, rin, m_w_in, v_w_in, (tr, half), (2, d_model // tr),
        lambda hf, i: (i, hf), lambda hf, i: (hf, i, 0))
    rq = w_out.shape[0] // 2
    gw_out, d_out, m_out, v_out = adam_shard(
        "adam_w_out", w_out, rout, m_w_out, v_w_out, (rq, d_model), (2,),
        lambda hf: (hf, 0), lambda hf: (hf, 0, 0))

    chip = 2 * lax.axis_index("x") + lax.axis_index("y")
    g_pre, g_post = rsmall[0:1], rsmall[1:2]
    g_conv = lax.dynamic_slice(rsmall[2:2 + CONV_K, :conv_width], (0, chip * conv_q), (CONV_K, conv_q))
    (d_pre, m_pre, v_pre), (d_post, m_post, v_post), (d_cv, m_cv, v_cv) = adam_small(
        [g1, g2, conv_w], [g_pre, g_post, g_conv],
        [m_norm_pre_g.reshape(1, d_model), m_norm_post_g.reshape(1, d_model), m_conv_w],
        [v_norm_pre_g.reshape(1, d_model), v_norm_post_g.reshape(1, d_model), v_conv_w])

    loss = 0.5 * rsmall[2 + CONV_K, 0] / d_model
    vec = lambda a: a.reshape(d_model)
    return (loss, grad_x.reshape(1, seq, d_model),
            vec(g_pre), gw_in, g_conv, gw_out, vec(g_post),
            vec(d_pre), d_in, d_cv, d_out, vec(d_post),
            vec(m_pre), m_in, m_cv, m_out, vec(m_post),
            vec(v_pre), v_in, v_cv, v_out, vec(v_post))
```

```python
import jax
import jax.numpy as jnp
from jax import lax
from jax.experimental import pallas as pl
from jax.experimental.pallas import tpu as pltpu

HEAD_DIM = 64
LANES = 128
SUBLANES = 8
BLOCK = 128
WINDOW_KEYS = 128
PERM = 16
PJ = 4
P4_ROWS = BLOCK // PJ
MAX_QUERY_BLOCKS = 4
ROW_TILE = 512
DZ_ROW_TILE = 512
CONV_K = 3
ROPE_THETA = 10000.0
NORM_EPS = 1e-6
ATTN_SCALE = HEAD_DIM ** -0.5
NEG = -1e30
N_CHIPS = 4
N_DEV = 8
MESH = pl.DeviceIdType.MESH
ADAM_LR = 0.001
ADAM_B1 = 0.9
ADAM_B2 = 0.999
ADAM_EPS = 1e-08
ADAM_WD = 0.01
ADAM_STEP = 10
VMEM_LIMIT = 52 * 1024 * 1024

F32 = jnp.float32
BF16 = jnp.bfloat16


def _params(sem=None, **kw):
    return pltpu.CompilerParams(dimension_semantics=sem, vmem_limit_bytes=VMEM_LIMIT, **kw)


def _const_spec(shape):
    return pl.BlockSpec(shape, lambda *_: (0,) * len(shape), pipeline_mode=pl.Buffered(1))


def _sigmoid(z):
    return 1.0 / (1.0 + jnp.exp(-z))


def _rowgroup_sum(a):
    rows, n = a.shape
    return a.reshape(rows // SUBLANES, SUBLANES, n).sum(axis=0)


def _nt(a, b):
    return lax.dot_general(a, b, (((1,), (1,)), ((), ())), preferred_element_type=F32)


def _tn(a, b):
    return lax.dot_general(a, b, (((0,), (0,)), ((), ())), preferred_element_type=F32)


def _col_pieces(a, b, width):
    out = []
    while a < b:
        j = a // width
        e = min(b, (j + 1) * width)
        out.append((j, a - j * width, e - j * width))
        a = e
    return out


def _lane_groups(width):
    return [slice(g * LANES, (g + 1) * LANES) for g in range(width // LANES)]


def _perm_shape(seq, width):
    return (PJ, PJ, seq // PERM, width)


def _perm_tile_spec(width, tm):
    return pl.BlockSpec((PJ, PJ, tm // PERM, width), lambda i: (0, 0, i, 0))


STAGE_PITCH = 24


def _stage_shape(groups, rows):
    return (groups, rows // PERM * STAGE_PITCH, LANES)


def _stage_put(stage, g, val, row0=0):
    for a in range(val.shape[0] // PERM):
        at = (row0 // PERM + a) * STAGE_PITCH
        stage[g, at:at + PERM, :] = val[a * PERM:(a + 1) * PERM]


def _stage_get(stage, g):
    return jnp.concatenate([stage[g, a * STAGE_PITCH:a * STAGE_PITCH + PERM, :]
                            for a in range(stage.shape[1] // STAGE_PITCH)], axis=0)


def _to_perm(stage, g, dst_ref, sl, dtype):
    rows = stage.shape[1] // STAGE_PITCH
    for b in range(PERM):
        dst_ref[b // PJ, b % PJ, :, sl] = stage[g, pl.ds(b, rows, stride=STAGE_PITCH), :].astype(dtype)


def _from_perm(src_ref, sl, stage, g):
    rows = stage.shape[1] // STAGE_PITCH
    for b in range(PERM):
        stage[g, pl.ds(b, rows, stride=STAGE_PITCH), :] = src_ref[b // PJ, b % PJ, :, sl].astype(F32)


def _flip(a, f):
    return 1 - a if f else a


def gather_weights(w_in, w_out, conv_w, seq):
    d_model, width = w_in.shape
    rows = w_out.shape[0]
    cw = jnp.zeros((SUBLANES, LANES), F32).at[:CONV_K, :conv_w.shape[1]].set(conv_w)
    half_dim = HEAD_DIM // 2
    inv_freq = ROPE_THETA ** (-jnp.arange(half_dim, dtype=F32) * 2.0 / HEAD_DIM)
    inv_freq = jnp.tile(inv_freq, LANES // half_dim).reshape(1, LANES)
    chunk = min(ROW_TILE, seq)

    def body(win_ref, wout_ref, cw_ref, freq_ref, winf_ref, woutf_ref, cwf_ref, cos_ref, s1_ref, s2_ref,
             st_in, st_out, near_send, near_recv, far_send, far_recv, cw_send, cw_recv, d2d_send, d2d_recv):
        x, y, c = lax.axis_index("x"), lax.axis_index("y"), lax.axis_index("c")
        me = 2 * x + y
        sib = (x, y, 1 - c)
        st_in[...] = win_ref[...].astype(BF16)
        st_out[...] = wout_ref[...].astype(BF16)
        winf_ref[me] = st_in[...]
        woutf_ref[me] = st_out[...]
        cwf_ref[me] = cw_ref[...]
        stages = (st_in, st_out)
        fulls = (winf_ref, woutf_ref)
        halves = (d_model // 2, rows // 2)

        def part(t, core, q=None):
            size = halves[t] if q is None else halves[t] // 2
            start = core * halves[t] if q is None else core * halves[t] + q * size
            return pl.ds(pl.multiple_of(start, size), size)

        near = [(1 - x, y), (x, 1 - y)]
        far = (1 - x, 1 - y)
        chip = lambda px, py: 2 * px + py

        def direct(k, t, q, slot, to):
            src = stages[t].at[part(t, c, q)]
            return pltpu.make_async_remote_copy(src_ref=src, dst_ref=fulls[t].at[slot, part(t, c, q)], send_sem=near_send.at[k, t, q],
                                                recv_sem=near_recv.at[k, t, q], device_id=to, device_id_type=MESH)

        def passed_on(k, t, slot, to):
            ref = fulls[t].at[slot, part(t, c, k)]
            return pltpu.make_async_remote_copy(src_ref=ref, dst_ref=ref, send_sem=far_send.at[k, t], recv_sem=far_recv.at[k, t],
                                                device_id=to, device_id_type=MESH)

        def conv_copy(k, slot, to):
            return pltpu.make_async_remote_copy(src_ref=cw_ref, dst_ref=cwf_ref.at[slot], send_sem=cw_send.at[k], recv_sem=cw_recv.at[k],
                                                device_id=to, device_id_type=MESH)

        def d2d(k, t, slot, core):
            ref = fulls[t].at[slot, part(t, core)]
            return pltpu.make_async_remote_copy(src_ref=ref, dst_ref=ref, send_sem=d2d_send.at[k, t], recv_sem=d2d_recv.at[k, t],
                                                device_id=sib, device_id_type=MESH)

        sends = []

        def go(cp):
            cp.start()
            sends.append(cp)

        for q_first in (0, 1):
            for k, (px, py) in enumerate(near):
                for t in range(2):
                    go(direct(k, t, k if q_first == 0 else 1 - k, me, (px, py, c)))
        for k, (px, py) in enumerate(near + [far]):
            go(conv_copy(k, me, (px, py, c)))
        for k, (px, py) in enumerate(near):
            other = near[1 - k]
            for t in range(2):
                direct(k, t, k, chip(px, py), (px, py, c)).wait_recv()
                go(passed_on(k, t, chip(px, py), (*other, c)))

        first_half = lax.broadcasted_iota(jnp.int32, (chunk, LANES), 1) % HEAD_DIM < half_dim
        row = lax.broadcasted_iota(jnp.int32, (chunk, LANES), 0)

        def table_rows(i, carry):
            at = pl.multiple_of(i * chunk, chunk)
            ang = (row + at).astype(F32) * freq_ref[...]
            sin = jnp.sin(ang)
            cos_ref[pl.ds(at, chunk), :] = jnp.cos(ang)
            s1_ref[pl.ds(at, chunk), :] = jnp.where(first_half, -sin, 0.0)
            s2_ref[pl.ds(at, chunk), :] = jnp.where(first_half, 0.0, sin)
            return carry

        lax.fori_loop(0, seq // chunk, table_rows, 0)

        for k, (px, py) in enumerate(near):
            for t in range(2):
                direct(k, t, 1 - k, chip(px, py), (px, py, c)).wait_recv()
                go(d2d(k, t, chip(px, py), c))
        for t in range(2):
            for k, (px, py) in enumerate(near):
                passed_on(k, t, chip(*far), (px, py, c)).wait_recv()
            go(d2d(2, t, chip(*far), c))
        for k, (px, py) in enumerate(near + [far]):
            conv_copy(k, chip(px, py), (px, py, c)).wait_recv()
            for t in range(2):
                d2d(k, t, chip(px, py), 1 - c).wait_recv()
        for cp in sends:
            cp.wait_send()

    vm = pl.BlockSpec(memory_space=pltpu.VMEM)
    dma = pltpu.SemaphoreType.DMA
    return pl.pallas_call(
        body, name="gather_weights",
        out_shape=(jax.ShapeDtypeStruct((N_CHIPS, d_model, width), BF16),
                   jax.ShapeDtypeStruct((N_CHIPS, rows, d_model), BF16),
                   jax.ShapeDtypeStruct((N_CHIPS, SUBLANES, LANES), F32),
                   *[jax.ShapeDtypeStruct((seq, LANES), F32)] * 3),
        in_specs=[vm, vm, vm, vm], out_specs=(vm,) * 6,
        scratch_shapes=[pltpu.VMEM((d_model, width), BF16), pltpu.VMEM((rows, d_model), BF16),
                        dma((2, 2, 2)), dma((2, 2, 2)), dma((2, 2)), dma((2, 2)), dma((3,)), dma((3,)),
                        dma((3, 2)), dma((3, 2))],
        compiler_params=_params(),
    )(w_in, w_out, cw, inv_freq)


def _rope(t, cos, s1, s2):
    return t * cos + pltpu.roll(t, LANES - HEAD_DIM // 2, 1) * s1 + pltpu.roll(t, HEAD_DIM // 2, 1) * s2


def _rope_transposed(g, cos, s1, s2):
    return g * cos + pltpu.roll(g * s1, HEAD_DIM // 2, 1) + pltpu.roll(g * s2, LANES - HEAD_DIM // 2, 1)


def inproj(x, g1, w_full, tables, attn_w, conv_w):
    seq, d_model = x.shape
    width = w_full.shape[2]
    tm = ROW_TILE
    groups = _lane_groups(attn_w)

    def body(x_ref, g_ref, w_ref, cos_ref, s1_ref, s2_ref,
             ht_ref, q_ref, k_ref, v_ref, qp_ref, kp_ref, vp_ref, ga_ref, cz_ref, stage):
        xv = x_ref[...]
        hb = ((xv * lax.rsqrt(jnp.mean(xv * xv, axis=-1, keepdims=True) + NORM_EPS)) * g_ref[...]).astype(BF16)
        ht_ref[...] = jnp.transpose(hb)
        cos, s1, s2 = cos_ref[...], s1_ref[...], s2_ref[...]

        def proj(a, b):
            parts = [jnp.dot(hb, w_ref[j, :, lo:hi], preferred_element_type=F32) for j, lo, hi in _col_pieces(a, b, width)]
            return parts[0] if len(parts) == 1 else jnp.concatenate(parts, axis=1)

        def emit(z, nat_ref, perm_ref, fn):
            for g, sl in enumerate(groups):
                val = fn(z[:, sl])
                nat_ref[:, sl] = val.astype(BF16)
                _stage_put(stage, g, val)
            for g, sl in enumerate(groups):
                _to_perm(stage, g, perm_ref, sl, BF16)

        emit(proj(0, attn_w), q_ref, qp_ref, lambda t: _rope(t, cos, s1, s2) * ATTN_SCALE)
        emit(proj(attn_w, 2 * attn_w), k_ref, kp_ref, lambda t: _rope(t, cos, s1, s2))
        emit(proj(2 * attn_w, 3 * attn_w), v_ref, vp_ref, lambda t: t)
        ga_ref[...] = proj(3 * attn_w, 4 * attn_w)
        cz_ref[...] = proj(4 * attn_w, 4 * attn_w + 4 * conv_w)

    row = lambda n: pl.BlockSpec((tm, n), lambda i: (i, 0))
    nat = jax.ShapeDtypeStruct((seq, attn_w), BF16)
    perm = jax.ShapeDtypeStruct(_perm_shape(seq, attn_w), BF16)
    return pl.pallas_call(
        body, name="inproj", grid=(seq // tm,),
        out_shape=(jax.ShapeDtypeStruct((d_model, seq), BF16), nat, nat, nat, perm, perm, perm,
                   jax.ShapeDtypeStruct((seq, attn_w), F32), jax.ShapeDtypeStruct((seq, 4 * conv_w), F32)),
        in_specs=[row(d_model), _const_spec((1, d_model)), _const_spec(w_full.shape), row(LANES), row(LANES), row(LANES)],
        out_specs=(pl.BlockSpec((d_model, tm), lambda i: (0, i)), row(attn_w), row(attn_w), row(attn_w),
                   _perm_tile_spec(attn_w, tm), _perm_tile_spec(attn_w, tm), _perm_tile_spec(attn_w, tm),
                   row(attn_w), row(4 * conv_w)),
        scratch_shapes=[pltpu.VMEM(_stage_shape(len(groups), tm), F32)],
        compiler_params=_params(("arbitrary",)),
    )(x, g1, w_full, *tables)


class _Mode:
    def __init__(self, name, seq):
        self.name = name
        if name == "nat":
            self.residues, blocks = 1, seq // BLOCK
        elif name == "p16":
            self.residues, blocks = PERM, seq // PERM // BLOCK
        else:
            self.residues, blocks = PJ, seq // PERM // P4_ROWS
        self.qb = max(d for d in range(1, MAX_QUERY_BLOCKS + 1) if blocks % d == 0)
        self.steps = blocks // self.qb

    def _spec(self, blocks, width, index):
        if self.name == "nat":
            return pl.BlockSpec((blocks * BLOCK, width), lambda r, n: (index(n), 0))
        if self.name == "p16":
            return pl.BlockSpec((1, 1, blocks * BLOCK, width), lambda r, n: (r // PJ, r % PJ, index(n), 0))
        return pl.BlockSpec((PJ, 1, blocks * P4_ROWS, width), lambda r, n: (0, r, index(n), 0))

    def wide(self, width, last=None):
        return self._spec(self.qb, width, (lambda n: n) if last is None else (lambda n: jnp.minimum(n, last)))

    def wide_before(self, width):
        return self._spec(self.qb, width, lambda n: jnp.maximum(n - 1, 0))

    def block_before(self, width, last=None):
        step = (lambda n: n) if last is None else (lambda n: jnp.minimum(n, last))
        return self._spec(1, width, lambda n: jnp.maximum(self.qb * step(n) - 1, 0))

    def get(self, ref, sl, sub=0):
        if self.name == "nat":
            return ref[sub * BLOCK:(sub + 1) * BLOCK, sl]
        if self.name == "p16":
            return ref[0, 0, sub * BLOCK:(sub + 1) * BLOCK, sl]
        return jnp.concatenate([ref[j, 0, sub * P4_ROWS:(sub + 1) * P4_ROWS, sl] for j in range(PJ)], axis=0)

    def put(self, ref, sl, val, sub=0):
        val = val.astype(ref.dtype)
        if self.name == "nat":
            ref[sub * BLOCK:(sub + 1) * BLOCK, sl] = val
        elif self.name == "p16":
            ref[0, 0, sub * BLOCK:(sub + 1) * BLOCK, sl] = val
        else:
            for j in range(PJ):
                ref[j, 0, sub * P4_ROWS:(sub + 1) * P4_ROWS, sl] = val[j * P4_ROWS:(j + 1) * P4_ROWS]

    def keys(self, before_ref, wide_ref, sl, sub):
        older = self.get(before_ref, sl) if sub == 0 else self.get(wide_ref, sl, sub - 1)
        return jnp.concatenate([older, self.get(wide_ref, sl, sub)], axis=0)

    def index(self, idx, is_key):
        if self.name != "p4":
            return idx - BLOCK if is_key else idx
        within = jnp.bitwise_and(idx, BLOCK - 1)
        m = PJ * jnp.bitwise_and(within, P4_ROWS - 1) + jnp.right_shift(within, P4_ROWS.bit_length() - 1)
        return m + BLOCK * (jnp.right_shift(idx, BLOCK.bit_length() - 1) - 1) if is_key else m

    def bias(self, has_before):
        shape = (2 * BLOCK, BLOCK)
        kidx = lax.broadcasted_iota(jnp.int32, shape, 0)
        qidx = lax.broadcasted_iota(jnp.int32, shape, 1)
        rel = self.index(qidx, False) - self.index(kidx, True)
        valid = (rel >= 0) & (rel <= WINDOW_KEYS)
        if has_before is not True:
            valid = valid & ((kidx >= BLOCK) | has_before)
        one = jnp.where(valid, 0.0, NEG)
        return jnp.concatenate([one, one], axis=1)


def _head_masks():
    lane = lax.broadcasted_iota(jnp.int32, (BLOCK, LANES), 1)
    lo = lane < HEAD_DIM
    return lane, lo, jnp.where(lo, 1.0, 0.0).astype(BF16), jnp.where(lo, 0.0, 1.0).astype(BF16)


def attn_fwd(name, q, k, v, run):
    nat = name == "nat"
    seq = q.shape[0] if nat else q.shape[2] * PERM
    attn_w = q.shape[-1]
    mode = _Mode(name, seq)
    groups = _lane_groups(attn_w)
    first = run is None
    all_lanes = slice(0, LANES)

    def body(*refs):
        q_ref, kp_ref, kc_ref, vp_ref, vc_ref = refs[:5]
        if first:
            o_ref, l_ref = refs[5:]
        elif nat:
            oin_ref, lin_ref, o_ref, l_ref, ostage, lstage = refs[5:]
        else:
            oin_ref, lin_ref, o_ref, l_ref = refs[5:]
        n = pl.program_id(1)
        subs = range(mode.qb)
        biases = [mode.bias(n > 0)] + [mode.bias(True)] * (mode.qb - 1)
        _, lo, m_lo, m_hi = _head_masks()
        head_row = lax.broadcasted_iota(jnp.int32, (BLOCK, LANES), 0)
        ones = jnp.ones((2 * BLOCK, LANES), BF16)
        lrows = [jnp.zeros((BLOCK, LANES), F32) for _ in subs]
        if not first:
            if nat:
                for g, sl in enumerate(groups):
                    _from_perm(oin_ref, sl, ostage, g)
                _from_perm(lin_ref, all_lanes, lstage, 0)
            wide_rows = lambda a, sub: a[sub * BLOCK:(sub + 1) * BLOCK]
            before = [jnp.transpose(wide_rows(_stage_get(lstage, 0), sub) if nat else mode.get(lin_ref, all_lanes, sub))
                      for sub in subs]

        def probs(sub, p, sl):
            q2 = mode.get(q_ref, sl, sub)
            kcat = mode.keys(kp_ref, kc_ref, sl, sub)
            vcat = mode.keys(vp_ref, vc_ref, sl, sub)
            qq = jnp.concatenate([q2 * m_lo, q2 * m_hi], axis=0)
            s_t = _nt(kcat, qq) + biases[sub]
            m = jnp.max(s_t, axis=0, keepdims=True)
            pe = jnp.exp(s_t - m)
            lse = m + jnp.log(jnp.sum(pe, axis=0, keepdims=True))
            if not first:
                was = jnp.concatenate([before[sub][2 * p:2 * p + 1, :], before[sub][2 * p + 1:2 * p + 2, :]], axis=1)
                top = jnp.maximum(was, lse)
                lse = top + jnp.log(jnp.exp(was - top) + jnp.exp(lse - top))
                pe = pe * jnp.exp(m - lse)
            return jnp.concatenate([vcat, ones], axis=1), pe.astype(BF16), lse

        def output(sub, p, sl, vext, pb, lse):
            o_ext = _tn(pb, vext)
            if first:
                o_new = o_ext[:, :LANES] / o_ext[:, LANES:]
            else:
                o_prev = wide_rows(_stage_get(ostage, p), sub) if nat else mode.get(oin_ref, sl, sub)
                o_new = o_ext[:, :LANES] + jnp.concatenate([o_prev, o_prev], axis=0) * (1.0 - o_ext[:, LANES:])
            mode.put(o_ref, sl, jnp.where(lo, o_new[:BLOCK], o_new[BLOCK:]), sub)
            rows = jnp.where(head_row == 2 * p, lse[:, :BLOCK], lrows[sub])
            lrows[sub] = jnp.where(head_row == 2 * p + 1, lse[:, BLOCK:], rows)

        pending = None
        for sub in subs:
            for p, sl in enumerate(groups):
                nxt = probs(sub, p, sl)
                if pending is not None:
                    output(*pending)
                pending = (sub, p, sl, *nxt)
        output(*pending)
        for sub in subs:
            mode.put(l_ref, all_lanes, jnp.transpose(lrows[sub]), sub)

    ins = [q, k, k, v, v]
    specs = [mode.wide(attn_w), mode.block_before(attn_w), mode.wide(attn_w), mode.block_before(attn_w), mode.wide(attn_w)]
    scratch = []
    if not first:
        ins += list(run)
        if nat:
            rows_a = mode.qb * BLOCK // PERM
            specs += [pl.BlockSpec((PJ, PJ, rows_a, attn_w), lambda r, n: (0, 0, n, 0)),
                      pl.BlockSpec((PJ, PJ, rows_a, LANES), lambda r, n: (0, 0, n, 0))]
            scratch = [pltpu.VMEM(_stage_shape(len(groups), mode.qb * BLOCK), F32),
                       pltpu.VMEM(_stage_shape(1, mode.qb * BLOCK), F32)]
        else:
            specs += [mode.wide(attn_w), mode.wide(LANES)]
    if nat:
        out_shape = (jax.ShapeDtypeStruct((seq, attn_w), F32), jax.ShapeDtypeStruct((seq, LANES), F32))
    else:
        out_shape = (jax.ShapeDtypeStruct(_perm_shape(seq, attn_w), F32), jax.ShapeDtypeStruct(_perm_shape(seq, LANES), F32))
    return pl.pallas_call(
        body, name=f"attn_fwd_{name}", grid=(mode.residues, mode.steps),
        out_shape=out_shape, in_specs=specs, out_specs=(mode.wide(attn_w), mode.wide(LANES)),
        scratch_shapes=scratch,
        compiler_params=_params(("arbitrary", "arbitrary")),
    )(*ins)


def attn_bwd(name, q, k, v, d_o, lse, delta, run):
    nat = name == "nat"
    seq = q.shape[0] if nat else q.shape[2] * PERM
    attn_w = q.shape[-1]
    mode = _Mode(name, seq)
    steps, qb = mode.steps, mode.qb
    single = steps == 1
    groups = _lane_groups(attn_w)
    first = run is None
    all_lanes = slice(0, LANES)

    def body(*refs):
        q_ref, kp_ref, kc_ref, vp_ref, vc_ref, do_ref, lse_ref, dl_ref = refs[:8]
        if first:
            dq_ref, dk_ref, dv_ref, ck, cv = refs[8:]
        else:
            dqi_ref, dki_ref, dvi_ref, dq_ref, dk_ref, dv_ref, ck, cv = refs[8:]
        n = pl.program_id(1)
        carries = ((ck, dk_ref, None if first else dki_ref), (cv, dv_ref, None if first else dvi_ref))

        def emit(out_ref, acc_ref, sl, sub, val):
            if acc_ref is not None:
                val = val + mode.get(acc_ref, sl, sub).astype(F32)
            mode.put(out_ref, sl, val, sub)

        if not single:
            @pl.when(n == 0)
            def _():
                ck[...] = jnp.zeros_like(ck)
                cv[...] = jnp.zeros_like(cv)

        @pl.when(n < steps)
        def _():
            biases = [mode.bias(n > 0)] + [mode.bias(True)] * (qb - 1)
            _, lo, m_lo, m_hi = _head_masks()

            def scores(sub, p, sl, lse_t, dl_t):
                q2, do2 = mode.get(q_ref, sl, sub), mode.get(do_ref, sl, sub)
                kcat = mode.keys(kp_ref, kc_ref, sl, sub)
                vcat = mode.keys(vp_ref, vc_ref, sl, sub)
                qq = jnp.concatenate([q2 * m_lo, q2 * m_hi], axis=0)
                dd = jnp.concatenate([do2 * m_lo, do2 * m_hi], axis=0)
                h0 = 2 * p
                lse2 = jnp.concatenate([lse_t[h0:h0 + 1, :], lse_t[h0 + 1:h0 + 2, :]], axis=1)
                dl2 = jnp.concatenate([dl_t[h0:h0 + 1, :], dl_t[h0 + 1:h0 + 2, :]], axis=1)
                p_t = jnp.exp(_nt(kcat, qq) + (biases[sub] - lse2))
                ds_t = p_t * (_nt(vcat, dd) - dl2)
                return qq, dd, kcat, p_t.astype(BF16), ds_t.astype(BF16)

            def grads(sub, sl, qq, dd, kcat, pb, dsb):
                dqb = _tn(dsb, kcat)
                dq2 = jnp.where(lo, dqb[:BLOCK], dqb[BLOCK:]) * ATTN_SCALE
                if not first:
                    dq2 = dq2 + mode.get(dqi_ref, sl, sub).astype(F32)
                mode.put(dq_ref, sl, dq2, sub)
                for (carry, out_ref, acc_ref), lhs, rhs in zip(carries, (dsb, pb), (qq, dd)):
                    both = jnp.dot(lhs, rhs, preferred_element_type=F32)
                    if sub == 0:
                        if not single:
                            for s in range(qb - 1):
                                emit(out_ref, acc_ref, sl, s, carry[s, :, sl])
                            emit(out_ref, acc_ref, sl, qb - 1, carry[qb - 1, :, sl] + both[:BLOCK])
                        carry[0, :, sl] = both[BLOCK:]
                    else:
                        carry[sub - 1, :, sl] += both[:BLOCK]
                        carry[sub, :, sl] = both[BLOCK:]
                    if single and sub == qb - 1:
                        for s in range(qb):
                            emit(out_ref, acc_ref, sl, s, carry[s, :, sl])

            stats = [(jnp.transpose(mode.get(lse_ref, all_lanes, sub)),
                      jnp.transpose(mode.get(dl_ref, all_lanes, sub))) for sub in range(qb)]
            pending = None
            for p, sl in enumerate(groups):
                for sub in range(qb):
                    nxt = scores(sub, p, sl, *stats[sub])
                    if pending is not None:
                        grads(*pending)
                    pending = (sub, sl, *nxt)
            grads(*pending)

        if not single:
            @pl.when(n == steps)
            def _():
                for carry, out_ref, acc_ref in carries:
                    for sl in groups:
                        for s in range(qb):
                            emit(out_ref, acc_ref, sl, s, carry[s, :, sl])

    last = steps - 1
    wide = lambda w: mode.wide(w, last)
    ins = [q, k, k, v, v, d_o, lse, delta]
    specs = [wide(attn_w), mode.block_before(attn_w, last), wide(attn_w), mode.block_before(attn_w, last), wide(attn_w),
             wide(attn_w), wide(LANES), wide(LANES)]
    if not first:
        ins += list(run)
        specs += [wide(attn_w), mode.wide_before(attn_w), mode.wide_before(attn_w)]
    shp = jax.ShapeDtypeStruct((seq, attn_w) if nat else _perm_shape(seq, attn_w), BF16)
    return pl.pallas_call(
        body, name=f"attn_bwd_{name}", grid=(mode.residues, steps if single else steps + 1),
        out_shape=(shp, shp, shp), in_specs=specs,
        out_specs=(wide(attn_w), mode.wide_before(attn_w), mode.wide_before(attn_w)),
        scratch_shapes=[pltpu.VMEM((qb, BLOCK, attn_w), F32), pltpu.VMEM((qb, BLOCK, attn_w), F32)],
        compiler_params=_params(("arbitrary", "arbitrary")),
    )(*ins)


def _shift_down(u, halo, k):
    rolled = pltpu.roll(u, k, 0)
    row = lax.broadcasted_iota(jnp.int32, halo.shape, 0)
    top = jnp.where(row < k, pltpu.roll(halo, k, 0), rolled[:SUBLANES])
    return jnp.concatenate([top, rolled[SUBLANES:]], axis=0)


def _shift_up(u, halo, k):
    rows = u.shape[0]
    rolled = pltpu.roll(u, rows - k, 0)
    row = lax.broadcasted_iota(jnp.int32, halo.shape, 0)
    bot = jnp.where(row >= SUBLANES - k, pltpu.roll(halo, SUBLANES - k, 0), rolled[rows - SUBLANES:])
    return jnp.concatenate([rolled[:rows - SUBLANES], bot], axis=0)


def tail(o, lse, ga, cz, x, tgt, w_out, g2, cw):
    seq, d_model = x.shape
    attn_w = o.shape[1]
    conv_w = cz.shape[1] // 4
    mix = attn_w + conv_w
    groups = _lane_groups(attn_w)
    tm = ROW_TILE
    nt = seq // tm
    hb = tm // SUBLANES

    def body(o_ref, l_ref, ga_ref, cz_ref, hz_ref, x_ref, t_ref, w_ref, g_ref, cw_ref,
             do_ref, dl_ref, dop_ref, dlp_ref, lp_ref, dga_ref, dcb_ref, dgc_ref, dcv_ref, e_ref,
             dw_ref, dg_ref, dcw_ref, loss_ref, stage):
        i = pl.program_id(0)

        @pl.when(i == 0)
        def _():
            dw_ref[...] = jnp.zeros_like(dw_ref)
            dg_ref[...] = jnp.zeros_like(dg_ref)
            dcw_ref[...] = jnp.zeros_like(dcw_ref)
            loss_ref[...] = jnp.zeros_like(loss_ref)

        u = cz_ref[:, 2 * conv_w:3 * conv_w] * cz_ref[:, 0:conv_w]
        uh = hz_ref[:, 2 * conv_w:3 * conv_w] * hz_ref[:, 0:conv_w]
        uh = jnp.where(i > 0, uh, 0.0)
        u1 = _shift_down(u, uh, 1)
        u2 = _shift_down(u, uh, 2)
        w0, w1, w2 = cw_ref[0:1, :], cw_ref[1:2, :], cw_ref[2:3, :]
        cvv = u2 * w0 + u1 * w1 + u * w2
        gv = g_ref[...]
        all_lanes = slice(0, LANES)

        def forward(rs):
            ov, gav = o_ref[rs, :], ga_ref[rs, :]
            sig_a = _sigmoid(gav)
            silu_a = gav * sig_a
            cb, gc = cz_ref[rs, conv_w:2 * conv_w], cz_ref[rs, 3 * conv_w:4 * conv_w]
            sig_c = _sigmoid(gc)
            silu_c = gc * sig_c
            bc = cb * cvv[rs]
            mixed = jnp.concatenate([ov * silu_a, bc * silu_c], axis=1).astype(BF16)
            yv = jnp.dot(mixed, w_ref[...], preferred_element_type=F32)
            return ov, gav, sig_a, silu_a, cb, gc, sig_c, silu_c, bc, mixed, yv

        def loss_and_dy(rs, mixed, yv):
            r2 = lax.rsqrt(jnp.mean(yv * yv, axis=-1, keepdims=True) + NORM_EPS)
            yhat = yv * r2
            diff = (x_ref[rs, :] + yhat * gv) - t_ref[rs, :]
            loss_ref[...] += _rowgroup_sum(diff * diff)
            ev = diff * (1.0 / d_model)
            e_ref[rs, :] = ev
            dg_ref[...] += _rowgroup_sum(ev * yhat)
            eg = ev * gv
            dy = (r2 * (eg - yhat * jnp.mean(eg * yhat, axis=-1, keepdims=True))).astype(BF16)
            dw_ref[...] += _tn(mixed, dy)
            return _nt(dy, w_ref[...])

        def backward(rs, ov, gav, sig_a, silu_a, cb, gc, sig_c, silu_c, bc, dm):
            rows = rs.stop - rs.start
            dma, dmc = dm[:, :attn_w], dm[:, attn_w:]
            dov = dma * silu_a
            do_ref[rs, :] = dov.astype(BF16)
            dga_ref[rs, :] = (dma * ov * (sig_a * (1.0 + gav * (1.0 - sig_a)))).astype(BF16)
            prod = dov * ov
            lane = lax.broadcasted_iota(jnp.int32, (rows, LANES), 1)
            lo = lane < HEAD_DIM
            dblk = jnp.zeros((rows, LANES), F32)
            for p, sl in enumerate(groups):
                pr = prod[:, sl]
                dblk = jnp.where(lane == 2 * p, jnp.sum(jnp.where(lo, pr, 0.0), axis=1, keepdims=True), dblk)
                dblk = jnp.where(lane == 2 * p + 1, jnp.sum(jnp.where(lo, 0.0, pr), axis=1, keepdims=True), dblk)
                _stage_put(stage, p, dov[:, sl], rs.start)
            dl_ref[rs, :] = dblk
            _stage_put(stage, len(groups), dblk, rs.start)
            _stage_put(stage, len(groups) + 1, l_ref[rs, :], rs.start)
            dsc = dmc * silu_c
            cv_rows = cvv[rs]
            dcb_ref[rs, :] = (dsc * cv_rows).astype(BF16)
            dgc_ref[rs, :] = (dmc * bc * (sig_c * (1.0 + gc * (1.0 - sig_c)))).astype(BF16)
            dcv = dsc * cb
            dcv_ref[rs, :] = dcv
            dcw_ref[0:SUBLANES, :] += _rowgroup_sum(dcv * u2[rs])
            dcw_ref[SUBLANES:2 * SUBLANES, :] += _rowgroup_sum(dcv * u1[rs])
            dcw_ref[2 * SUBLANES:3 * SUBLANES, :] += _rowgroup_sum(dcv * u[rs])

        halves = [slice(0, tm // 2), slice(tm // 2, tm)]
        fwd = [forward(rs) for rs in halves]
        dms = [loss_and_dy(rs, f[9], f[10]) for rs, f in zip(halves, fwd)]
        for rs, f, dm in zip(halves, fwd, dms):
            backward(rs, *f[:9], dm)
        for p, sl in enumerate(groups):
            _to_perm(stage, p, dop_ref, sl, BF16)
        _to_perm(stage, len(groups), dlp_ref, all_lanes, F32)
        _to_perm(stage, len(groups) + 1, lp_ref, all_lanes, F32)

    row = lambda n: pl.BlockSpec((tm, n), lambda i: (i, 0))
    whole = lambda a, b: pl.BlockSpec((a, b), lambda i: (0, 0))
    return pl.pallas_call(
        body, name="tail", grid=(nt,),
        out_shape=(jax.ShapeDtypeStruct((seq, attn_w), BF16), jax.ShapeDtypeStruct((seq, LANES), F32),
                   jax.ShapeDtypeStruct(_perm_shape(seq, attn_w), BF16), jax.ShapeDtypeStruct(_perm_shape(seq, LANES), F32),
                   jax.ShapeDtypeStruct(_perm_shape(seq, LANES), F32),
                   jax.ShapeDtypeStruct((seq, attn_w), BF16), jax.ShapeDtypeStruct((seq, conv_w), BF16),
                   jax.ShapeDtypeStruct((seq, conv_w), BF16), jax.ShapeDtypeStruct((seq, conv_w), F32),
                   jax.ShapeDtypeStruct((seq, d_model), F32), jax.ShapeDtypeStruct((mix, d_model), F32),
                   jax.ShapeDtypeStruct((SUBLANES, d_model), F32), jax.ShapeDtypeStruct((CONV_K * SUBLANES, conv_w), F32),
                   jax.ShapeDtypeStruct((SUBLANES, d_model), F32)),
        in_specs=[row(attn_w), row(LANES), row(attn_w), row(4 * conv_w),
                  pl.BlockSpec((SUBLANES, 4 * conv_w), lambda i: (jnp.maximum(i * hb - 1, 0), 0)),
                  row(d_model), row(d_model), _const_spec((mix, d_model)), _const_spec((1, d_model)),
                  _const_spec((SUBLANES, conv_w))],
        out_specs=(row(attn_w), row(LANES), _perm_tile_spec(attn_w, tm), _perm_tile_spec(LANES, tm), _perm_tile_spec(LANES, tm),
                   row(attn_w), row(conv_w), row(conv_w), row(conv_w), row(d_model),
                   whole(mix, d_model), whole(SUBLANES, d_model), whole(CONV_K * SUBLANES, conv_w),
                   whole(SUBLANES, d_model)),
        scratch_shapes=[pltpu.VMEM(_stage_shape(len(groups) + 2, tm), F32)],
        compiler_params=_params(("arbitrary",)),
    )(o, lse, ga, cz, cz, x, tgt, w_out, g2, cw)


def dz_dx(nat_grads, perm_grads, dga, dcb, dgc, dcv, cz, tables, x, g1, e, w_full, cw):
    seq, d_model = x.shape
    attn_w = dga.shape[1]
    conv_w = dcv.shape[1]
    width = w_full.shape[2]
    in_w = 4 * attn_w + 4 * conv_w
    groups = _lane_groups(attn_w)
    tm = DZ_ROW_TILE
    nt = seq // tm
    hb = tm // SUBLANES

    def body(dq_ref, dk_ref, dv_ref, dqp_ref, dkp_ref, dvp_ref, dga_ref, dcb_ref, dgc_ref, dcv_ref, nh_ref, ch_ref, cc_ref,
             cos_ref, s1_ref, s2_ref, x_ref, g_ref, e_ref, w_ref, cw_ref, gx_ref, dz_ref, dg_ref, stage):
        i = pl.program_id(0)

        @pl.when(i == 0)
        def _():
            dg_ref[...] = jnp.zeros_like(dg_ref)

        cos, s1, s2 = cos_ref[...], s1_ref[...], s2_ref[...]

        def qkv_columns(t, nat_ref, perm_ref):
            for g, sl in enumerate(groups):
                _from_perm(perm_ref, sl, stage, g)
            for g, sl in enumerate(groups):
                tot = nat_ref[:, sl].astype(F32) + _stage_get(stage, g)
                if t < 2:
                    tot = _rope_transposed(tot, cos, s1, s2)
                dz_ref[:, t * attn_w + g * LANES:t * attn_w + (g + 1) * LANES] = tot.astype(BF16)

        def dh_part(j):
            return _nt(dz_ref[:, j * width:(j + 1) * width], w_ref[j])

        dcv = dcv_ref[...]
        nh = jnp.where(i < nt - 1, nh_ref[...], 0.0)
        w0, w1, w2 = cw_ref[0:1, :], cw_ref[1:2, :], cw_ref[2:3, :]
        du = dcv * w2 + _shift_up(dcv, nh, 1) * w1 + _shift_up(dcv, nh, 2) * w0
        base = 4 * attn_w
        dz_ref[:, base:base + conv_w] = (du * cc_ref[...]).astype(BF16)
        dz_ref[:, base + conv_w:base + 2 * conv_w] = dcb_ref[...]
        dz_ref[:, base + 2 * conv_w:base + 3 * conv_w] = (du * ch_ref[...]).astype(BF16)
        dz_ref[:, base + 3 * conv_w:base + 4 * conv_w] = dgc_ref[...]
        dz_ref[:, 3 * attn_w:4 * attn_w] = dga_ref[...]
        ready = in_w
        dh = None
        for t, nat_ref, perm_ref in ((2, dv_ref, dvp_ref), (1, dk_ref, dkp_ref), (0, dq_ref, dqp_ref), (None, None, None)):
            lowest_open = 0 if t is None else (t + 1) * attn_w
            while ready - width >= lowest_open:
                ready -= width
                part = dh_part(ready // width)
                dh = part if dh is None else dh + part
            if t is not None:
                qkv_columns(t, nat_ref, perm_ref)
        xv = x_ref[...]
        r1 = lax.rsqrt(jnp.mean(xv * xv, axis=-1, keepdims=True) + NORM_EPS)
        xhat = xv * r1
        dg_ref[...] += _rowgroup_sum(dh * xhat)
        dhg = dh * g_ref[...]
        gx_ref[...] = r1 * (dhg - xhat * jnp.mean(dhg * xhat, axis=-1, keepdims=True)) + e_ref[...]

    row = lambda n: pl.BlockSpec((tm, n), lambda i: (i, 0))
    whole = lambda a, b: pl.BlockSpec((a, b), lambda i: (0, 0))
    pt = _perm_tile_spec(attn_w, tm)
    return pl.pallas_call(
        body, name="dz_dx", grid=(nt,),
        out_shape=(jax.ShapeDtypeStruct((seq, d_model), F32), jax.ShapeDtypeStruct((seq, in_w), BF16),
                   jax.ShapeDtypeStruct((SUBLANES, d_model), F32)),
        in_specs=[row(attn_w), row(attn_w), row(attn_w), pt, pt, pt, row(attn_w), row(conv_w), row(conv_w), row(conv_w),
                  pl.BlockSpec((SUBLANES, conv_w), lambda i: (jnp.minimum((i + 1) * hb, seq // SUBLANES - 1), 0)),
                  pl.BlockSpec((tm, conv_w), lambda i: (i, 0)), pl.BlockSpec((tm, conv_w), lambda i: (i, 2)),
                  row(LANES), row(LANES), row(LANES), row(d_model), _const_spec((1, d_model)), row(d_model),
                  _const_spec(w_full.shape), _const_spec((SUBLANES, conv_w))],
        out_specs=(row(d_model), row(in_w), whole(SUBLANES, d_model)),
        scratch_shapes=[pltpu.VMEM(_stage_shape(len(groups), tm), F32)],
        compiler_params=_params(("arbitrary",)),
    )(*nat_grads, *perm_grads, dga, dcb, dgc, dcv, dcv, cz, cz, *tables, x, g1, e, w_full, cw)


def dw_in_reduce(ht, dz, g_out, small):
    d_model, seq = ht.shape
    half = dz.shape[1] // N_DEV
    ts = min(2048, seq)
    steps = seq // ts
    x, y, c = lax.axis_index("x"), lax.axis_index("y"), lax.axis_index("c")
    far_first = lambda x, y: [(1 - x, 1 - y), (1 - x, y), (x, 1 - y)]
    chips = jnp.stack([2 * px + py for px, py in far_first(x, y)] + [2 * x + y]).astype(jnp.int32)
    order = jnp.stack([2 * chips + (1 - c), 2 * chips + c], axis=1).reshape(N_DEV)

    def body(order_ref, ht_ref, dz_ref, go_ref, sm_ref, out_ref, ro_ref, rs_ref,
             acc, theirs, staged, contrib, resbuf, out_sem, sa, ra, sb, rb, sc, rc,
             o_mine, o_theirs, o_staged, o_contrib, o_res, sbuf, o_load, osa, ora, osb, orb, osc, orc, ss, rs):
        del order_ref
        p, s = pl.program_id(0), pl.program_id(1)
        x, y, c = lax.axis_index("x"), lax.axis_index("y"), lax.axis_index("c")
        me = 2 * x + y
        sib = (x, y, 1 - c)
        peers = far_first(x, y)
        slot = p % 2

        flips = [(fx, fy, fc) for fx in (0, 1) for fy in (0, 1) for fc in (0, 1)][1:]
        my8 = 4 * x + 2 * y + c
        chip_ids = [2 * px + py for px, py in peers] + [me]

        def small_copy(k, slot8, to):
            return pltpu.make_async_remote_copy(src_ref=sm_ref, dst_ref=sbuf.at[slot8], send_sem=ss.at[k], recv_sem=rs.at[k],
                                                device_id=to, device_id_type=MESH)

        def small_peer(k):
            fx, fy, fc = flips[k]
            return _flip(x, fx), _flip(y, fy), _flip(c, fc)

        def oa_copy(pos):
            j = chip_ids[pos]
            return pltpu.make_async_remote_copy(src_ref=go_ref.at[j, 1 - c], dst_ref=o_theirs.at[j], send_sem=osa.at[pos],
                                                recv_sem=ora.at[pos], device_id=sib, device_id_type=MESH)

        def o_load_copy(pos):
            j = chip_ids[pos]
            return pltpu.make_async_copy(go_ref.at[j, c], o_mine.at[j], o_load.at[pos])

        def ob_copy(k, piece, slot4):
            px, py = peers[k]
            return pltpu.make_async_remote_copy(src_ref=o_staged.at[piece], dst_ref=o_contrib.at[slot4], send_sem=osb.at[k],
                                                recv_sem=orb.at[k], device_id=(px, py, c), device_id_type=MESH)

        def oc_copy(which):
            return pltpu.make_async_remote_copy(src_ref=o_res.at[which], dst_ref=o_res.at[which], send_sem=osc, recv_sem=orc,
                                                device_id=sib, device_id_type=MESH)

        @pl.when((p == 0) & (s == 0))
        def _():
            sbuf[my8] = sm_ref[...]
            for k in range(N_DEV - 1):
                small_copy(k, my8, small_peer(k)).start()
            for pos in range(N_CHIPS):
                o_load_copy(pos).start()
                oa_copy(pos).start()

        @pl.when((p == 1) & (s == steps - 1))
        def _():
            for pos in range(N_CHIPS):
                j = chip_ids[pos]
                o_load_copy(pos).wait()
                oa_copy(pos).wait_recv()
                if pos < N_CHIPS - 1:
                    o_staged[j] = (o_mine[j] + o_theirs[j]).astype(BF16)
                    ob_copy(pos, j, me).start()
                else:
                    o_mine[j] = o_mine[j] + o_theirs[j]
                    o_contrib[j] = o_mine[j].astype(BF16)

        @pl.when((p == 4) & (s == steps - 1))
        def _():
            for k in range(N_CHIPS - 1):
                ob_copy(k, me, chip_ids[k]).wait_recv()
            own = o_mine[me]
            term = lambda j: jnp.where(me == j, own, o_contrib[j].astype(F32))
            o_res[c] = ((term(0) + term(1)) + term(2)) + term(3)
            oc_copy(c).start()

        def a_copy(k):
            return pltpu.make_async_remote_copy(src_ref=acc.at[0], dst_ref=theirs.at[k], send_sem=sa.at[k], recv_sem=ra.at[k],
                                                device_id=sib, device_id_type=MESH)

        def b_copy(k):
            px, py = peers[k]
            return pltpu.make_async_remote_copy(src_ref=staged.at[k], dst_ref=contrib.at[k], send_sem=sb.at[k], recv_sem=rb.at[k],
                                                device_id=(px, py, c), device_id_type=MESH)

        def c_copy(which):
            return pltpu.make_async_remote_copy(src_ref=resbuf.at[which], dst_ref=resbuf.at[which], send_sem=sc, recv_sem=rc,
                                                device_id=sib, device_id_type=MESH)

        @pl.when(s == 0)
        def _():
            for k in range(N_CHIPS - 1):
                @pl.when(p == 2 * k + 2)
                def _():
                    a_copy(k).wait_send()
            acc[slot] = jnp.zeros((d_model, half), F32)

        acc[slot] += jnp.dot(ht_ref[...], dz_ref[...], preferred_element_type=F32)

        @pl.when(s == steps - 1)
        def _():
            for k in range(N_CHIPS):
                @pl.when(p == 2 * k)
                def _():
                    a_copy(k).start()
            for k in range(N_CHIPS - 1):
                @pl.when(p == 2 * k + 1)
                def _():
                    a_copy(k).wait_recv()
                    staged[k] = (acc[1] + theirs[k]).astype(BF16)
                    b_copy(k).start()

            @pl.when(p == N_DEV - 1)
            def _():
                a_copy(N_CHIPS - 1).wait_recv()
                tot = acc[1] + theirs[N_CHIPS - 1]
                for k in range(N_CHIPS - 1):
                    b_copy(k).wait_recv()
                    tot = tot + contrib[k].astype(F32)
                resbuf[c] = tot
                c_copy(c).start()
                c_copy(1 - c).wait_recv()
                done = pltpu.make_async_copy(resbuf, out_ref, out_sem)
                done.start()
                oc_copy(1 - c).wait_recv()
                ro_ref[...] = o_res[...]
                for k in range(N_DEV - 1):
                    px, py, pc = small_peer(k)
                    small_copy(k, 4 * px + 2 * py + pc, (px, py, pc)).wait_recv()
                tot8 = sbuf[0]
                for d in range(1, N_DEV):
                    tot8 = tot8 + sbuf[d]
                rs_ref[...] = tot8
                a_copy(N_CHIPS - 1).wait_send()
                for k in range(N_CHIPS - 1):
                    b_copy(k).wait_send()
                    ob_copy(k, chip_ids[k], me).wait_send()
                c_copy(c).wait_send()
                oc_copy(c).wait_send()
                for pos in range(N_CHIPS):
                    oa_copy(pos).wait_send()
                for k in range(N_DEV - 1):
                    small_copy(k, my8, small_peer(k)).wait_send()
                done.wait()

    dma = pltpu.SemaphoreType.DMA
    o_shape = g_out.shape[1:]
    go = g_out.reshape(N_CHIPS, 2, *o_shape)
    const = lambda shape: pl.BlockSpec(shape, lambda p, s, order_ref: (0,) * len(shape))
    grid_spec = pltpu.PrefetchScalarGridSpec(
        num_scalar_prefetch=1, grid=(N_DEV, steps),
        in_specs=[pl.BlockSpec((d_model, ts), lambda p, s, order_ref: (0, s)),
                  pl.BlockSpec((ts, half), lambda p, s, order_ref: (s, order_ref[p])),
                  pl.BlockSpec(memory_space=pl.ANY), const(small.shape)],
        out_specs=(pl.BlockSpec(memory_space=pl.ANY), const((2, *o_shape)), const(small.shape)),
        scratch_shapes=[pltpu.VMEM((2, d_model, half), F32), pltpu.VMEM((N_CHIPS, d_model, half), F32),
                        pltpu.VMEM((N_CHIPS - 1, d_model, half), BF16), pltpu.VMEM((N_CHIPS - 1, d_model, half), BF16),
                        pltpu.VMEM((2, d_model, half), F32), dma,
                        dma((N_CHIPS,)), dma((N_CHIPS,)), dma((N_CHIPS - 1,)), dma((N_CHIPS - 1,)), dma, dma,
                        pltpu.VMEM((N_CHIPS, *o_shape), F32), pltpu.VMEM((N_CHIPS, *o_shape), F32),
                        pltpu.VMEM((N_CHIPS, *o_shape), BF16), pltpu.VMEM((N_CHIPS, *o_shape), BF16),
                        pltpu.VMEM((2, *o_shape), F32), pltpu.VMEM((N_DEV, *small.shape), F32),
                        dma((N_CHIPS,)), dma((N_CHIPS,)), dma((N_CHIPS,)), dma((N_CHIPS - 1,)), dma((N_CHIPS - 1,)), dma, dma,
                        dma((N_DEV - 1,)), dma((N_DEV - 1,))])
    return pl.pallas_call(
        body, name="dw_in_reduce", grid_spec=grid_spec,
        out_shape=(jax.ShapeDtypeStruct((2, d_model, half), F32), jax.ShapeDtypeStruct((2, *o_shape), F32),
                   jax.ShapeDtypeStruct(small.shape, F32)),
        compiler_params=_params(("arbitrary", "arbitrary")),
    )(order, ht, dz, go, small)


def _adam_math(w, g, m, v):
    m = ADAM_B1 * m + (1.0 - ADAM_B1) * g
    v = ADAM_B2 * v + (1.0 - ADAM_B2) * (g * g)
    m_hat = m / (1.0 - ADAM_B1 ** ADAM_STEP)
    v_hat = v / (1.0 - ADAM_B2 ** ADAM_STEP)
    delta = -ADAM_LR * (m_hat / (jnp.sqrt(v_hat) + ADAM_EPS) + ADAM_WD * w)
    return delta, m, v


def adam_shard(name, w, g2, m, v, block, grid, w_map, g_map):
    def body(w_ref, g_ref, m_ref, v_ref, go_ref, d_ref, mo_ref, vo_ref):
        g = g_ref[0]
        delta, mn, vn = _adam_math(w_ref[...], g, m_ref[...], v_ref[...])
        go_ref[...] = g
        d_ref[...] = delta
        mo_ref[...] = mn
        vo_ref[...] = vn

    ws = pl.BlockSpec(block, w_map)
    shp = jax.ShapeDtypeStruct(w.shape, F32)
    return pl.pallas_call(
        body, name=name, grid=grid, out_shape=(shp, shp, shp, shp),
        in_specs=[ws, pl.BlockSpec((1, *block), g_map), ws, ws], out_specs=(ws, ws, ws, ws),
        compiler_params=_params(("arbitrary",) * len(grid)),
    )(w, g2, m, v)


def adam_small(ws, gs, ms, vs):
    n = len(ws)

    def body(*refs):
        ins, outs = refs[:4 * n], refs[4 * n:]
        for t in range(n):
            delta, mn, vn = _adam_math(ins[t][...], ins[n + t][...], ins[2 * n + t][...], ins[3 * n + t][...])
            outs[3 * t][...] = delta
            outs[3 * t + 1][...] = mn
            outs[3 * t + 2][...] = vn

    vm = pl.BlockSpec(memory_space=pltpu.VMEM)
    outs = pl.pallas_call(
        body, name="adam_small",
        out_shape=tuple(jax.ShapeDtypeStruct(w.shape, F32) for w in ws for _ in range(3)),
        in_specs=[vm] * (4 * n), out_specs=tuple([vm] * (3 * n)),
        compiler_params=_params(),
    )(*ws, *gs, *ms, *vs)
    return [outs[3 * t:3 * t + 3] for t in range(n)]


def kernel(x, norm_pre_g, w_in, conv_w, w_out, norm_post_g, loss_target, m_norm_pre_g, m_w_in, m_conv_w, m_w_out, m_norm_post_g, v_norm_pre_g, v_w_in, v_conv_w, v_w_out, v_norm_post_g):
    _, seq, d_model = x.shape
    width = w_in.shape[1]
    conv_q = conv_w.shape[1]
    conv_width = N_CHIPS * conv_q
    attn_width = d_model - conv_width
    xs, tg = x[0], loss_target[0]
    g1, g2 = norm_pre_g.reshape(1, d_model), norm_post_g.reshape(1, d_model)

    w_full, wout_full, cw_full, *tables = gather_weights(w_in, w_out, conv_w, seq)
    wout2 = wout_full.reshape(attn_width + conv_width, d_model)
    cw = jnp.zeros((SUBLANES, conv_width), F32).at[:CONV_K].set(
        cw_full[:, :CONV_K, :conv_q].transpose(1, 0, 2).reshape(CONV_K, conv_width))

    ht, q, k, v, qp, kp, vp, ga, cz = inproj(xs, g1, w_full, tables, attn_width, conv_width)
    run = attn_fwd("p4", qp, kp, vp, None)
    run = attn_fwd("p16", qp, kp, vp, run)
    o, lse = attn_fwd("nat", q, k, v, run)
    (d_o, delta, d_op, delta_p, lse_p, dga, dcb, dgc, dcv, e, dwout, dg2, dcw, loss_acc) = tail(
        o, lse, ga, cz, xs, tg, wout2, g2, cw)
    nat_grads = attn_bwd("nat", q, k, v, d_o, lse, delta, None)
    perm_grads = attn_bwd("p4", qp, kp, vp, d_op, lse_p, delta_p, None)
    perm_grads = attn_bwd("p16", qp, kp, vp, d_op, lse_p, delta_p, perm_grads)
    grad_x, dz, dg1 = dz_dx(nat_grads, perm_grads, dga, dcb, dgc, dcv, cz, tables, xs, g1, e, w_full, cw)

    small = jnp.zeros((SUBLANES, d_model), F32)
    small = small.at[0].set(dg1.sum(axis=0)).at[1].set(dg2.sum(axis=0))
    small = small.at[2:2 + CONV_K, :conv_width].set(dcw.reshape(CONV_K, SUBLANES, conv_width).sum(axis=1))
    small = small.at[2 + CONV_K, 0].set(jnp.sum(loss_acc))
    rin, rout, rsmall = dw_in_reduce(ht, dz, dwout.reshape(N_DEV, -1, d_model), small)

    half = width // 2
    tr = 256
    gw_in, d_in, m_in, v_in = adam_shard(
        "adam_w_in", w_in, rin, m_w_in, v_w_in, (tr, half), (2, d_model // tr),
        lambda hf, i: (i, hf), lambda hf, i: (hf, i, 0))
    rq = w_out.shape[0] // 2
    gw_out, d_out, m_out, v_out = adam_shard(
        "adam_w_out", w_out, rout, m_w_out, v_w_out, (rq, d_model), (2,),
        lambda hf: (hf, 0), lambda hf: (hf, 0, 0))

    chip = 2 * lax.axis_index("x") + lax.axis_index("y")
    g_pre, g_post = rsmall[0:1], rsmall[1:2]
    g_conv = lax.dynamic_slice(rsmall[2:2 + CONV_K, :conv_width], (0, chip * conv_q), (CONV_K, conv_q))
    (d_pre, m_pre, v_pre), (d_post, m_post, v_post), (d_cv, m_cv, v_cv) = adam_small(
        [g1, g2, conv_w], [g_pre, g_post, g_conv],
        [m_norm_pre_g.reshape(1, d_model), m_norm_post_g.reshape(1, d_model), m_conv_w],
        [v_norm_pre_g.reshape(1, d_model), v_norm_post_g.reshape(1, d_model), v_conv_w])

    loss = 0.5 * rsmall[2 + CONV_K, 0] / d_model
    vec = lambda a: a.reshape(d_model)
    return (loss, grad_x.reshape(1, seq, d_model),
            vec(g_pre), gw_in, g_conv, gw_out, vec(g_post),
            vec(d_pre), d_in, d_cv, d_out, vec(d_post),
            vec(m_pre), m_in, m_cv, m_out, vec(m_post),
            vec(v_pre), v_in, v_cv, v_out, vec(v_post))
```

```python
import jax
import jax.numpy as jnp
from jax import lax
from jax.experimental import pallas as pl
from jax.experimental.pallas import tpu as pltpu

HEAD_DIM = 64
LANES = 128
SUBLANES = 8
BLOCK = 128
WINDOW_KEYS = 128
PERM = 16
PJ = 4
P4_ROWS = BLOCK // PJ
MAX_QUERY_BLOCKS = 4
ROW_TILE = 512
DZ_ROW_TILE = 512
CONV_K = 3
ROPE_THETA = 10000.0
NORM_EPS = 1e-6
ATTN_SCALE = HEAD_DIM ** -0.5
NEG = -1e30
N_CHIPS = 4
N_DEV = 8
MESH = pl.DeviceIdType.MESH
ADAM_LR = 0.001
ADAM_B1 = 0.9
ADAM_B2 = 0.999
ADAM_EPS = 1e-08
ADAM_WD = 0.01
ADAM_STEP = 10
VMEM_LIMIT = 52 * 1024 * 1024

F32 = jnp.float32
BF16 = jnp.bfloat16


def _params(sem=None, **kw):
    return pltpu.CompilerParams(dimension_semantics=sem, vmem_limit_bytes=VMEM_LIMIT, **kw)


def _const_spec(shape):
    return pl.BlockSpec(shape, lambda *_: (0,) * len(shape), pipeline_mode=pl.Buffered(1))


def _sigmoid(z):
    return 1.0 / (1.0 + jnp.exp(-z))


def _rowgroup_sum(a):
    rows, n = a.shape
    return a.reshape(rows // SUBLANES, SUBLANES, n).sum(axis=0)


def _nt(a, b):
    return lax.dot_general(a, b, (((1,), (1,)), ((), ())), preferred_element_type=F32)


def _tn(a, b):
    return lax.dot_general(a, b, (((0,), (0,)), ((), ())), preferred_element_type=F32)


def _col_pieces(a, b, width):
    out = []
    while a < b:
        j = a // width
        e = min(b, (j + 1) * width)
        out.append((j, a - j * width, e - j * width))
        a = e
    return out


def _lane_groups(width):
    return [slice(g * LANES, (g + 1) * LANES) for g in range(width // LANES)]


def _perm_shape(seq, width):
    return (PJ, PJ, seq // PERM, width)


def _perm_tile_spec(width, tm):
    return pl.BlockSpec((PJ, PJ, tm // PERM, width), lambda i: (0, 0, i, 0))


STAGE_PITCH = 24


def _stage_shape(groups, rows):
    return (groups, rows // PERM * STAGE_PITCH, LANES)


def _stage_put(stage, g, val, row0=0):
    for a in range(val.shape[0] // PERM):
        at = (row0 // PERM + a) * STAGE_PITCH
        stage[g, at:at + PERM, :] = val[a * PERM:(a + 1) * PERM]


def _stage_get(stage, g):
    return jnp.concatenate([stage[g, a * STAGE_PITCH:a * STAGE_PITCH + PERM, :]
                            for a in range(stage.shape[1] // STAGE_PITCH)], axis=0)


def _to_perm(stage, g, dst_ref, sl, dtype):
    rows = stage.shape[1] // STAGE_PITCH
    for b in range(PERM):
        dst_ref[b // PJ, b % PJ, :, sl] = stage[g, pl.ds(b, rows, stride=STAGE_PITCH), :].astype(dtype)


def _from_perm(src_ref, sl, stage, g):
    rows = stage.shape[1] // STAGE_PITCH
    for b in range(PERM):
        stage[g, pl.ds(b, rows, stride=STAGE_PITCH), :] = src_ref[b // PJ, b % PJ, :, sl].astype(F32)


def _flip(a, f):
    return 1 - a if f else a


def gather_weights(w_in, w_out, conv_w, seq):
    d_model, width = w_in.shape
    rows = w_out.shape[0]
    cw = jnp.zeros((SUBLANES, LANES), F32).at[:CONV_K, :conv_w.shape[1]].set(conv_w)
    half_dim = HEAD_DIM // 2
    inv_freq = ROPE_THETA ** (-jnp.arange(half_dim, dtype=F32) * 2.0 / HEAD_DIM)
    inv_freq = jnp.tile(inv_freq, LANES // half_dim).reshape(1, LANES)
    chunk = min(ROW_TILE, seq)

    def body(win_ref, wout_ref, cw_ref, freq_ref, winf_ref, woutf_ref, cwf_ref, cos_ref, s1_ref, s2_ref,
             st_in, st_out, near_send, near_recv, far_send, far_recv, cw_send, cw_recv, d2d_send, d2d_recv):
        x, y, c = lax.axis_index("x"), lax.axis_index("y"), lax.axis_index("c")
        me = 2 * x + y
        sib = (x, y, 1 - c)
        st_in[...] = win_ref[...].astype(BF16)
        st_out[...] = wout_ref[...].astype(BF16)
        winf_ref[me] = st_in[...]
        woutf_ref[me] = st_out[...]
        cwf_ref[me] = cw_ref[...]
        stages = (st_in, st_out)
        fulls = (winf_ref, woutf_ref)
        halves = (d_model // 2, rows // 2)

        def part(t, core, q=None):
            size = halves[t] if q is None else halves[t] // 2
            start = core * halves[t] if q is None else core * halves[t] + q * size
            return pl.ds(pl.multiple_of(start, size), size)

        near = [(1 - x, y), (x, 1 - y)]
        far = (1 - x, 1 - y)
        chip = lambda px, py: 2 * px + py

        def direct(k, t, q, slot, to):
            src = stages[t].at[part(t, c, q)]
            return pltpu.make_async_remote_copy(src_ref=src, dst_ref=fulls[t].at[slot, part(t, c, q)], send_sem=near_send.at[k, t, q],
                                                recv_sem=near_recv.at[k, t, q], device_id=to, device_id_type=MESH)

        def passed_on(k, t, slot, to):
            ref = fulls[t].at[slot, part(t, c, k)]
            return pltpu.make_async_remote_copy(src_ref=ref, dst_ref=ref, send_sem=far_send.at[k, t], recv_sem=far_recv.at[k, t],
                                                device_id=to, device_id_type=MESH)

        def conv_copy(k, slot, to):
            return pltpu.make_async_remote_copy(src_ref=cw_ref, dst_ref=cwf_ref.at[slot], send_sem=cw_send.at[k], recv_sem=cw_recv.at[k],
                                                device_id=to, device_id_type=MESH)

        def d2d(k, t, slot, core):
            ref = fulls[t].at[slot, part(t, core)]
            return pltpu.make_async_remote_copy(src_ref=ref, dst_ref=ref, send_sem=d2d_send.at[k, t], recv_sem=d2d_recv.at[k, t],
                                                device_id=sib, device_id_type=MESH)

        sends = []

        def go(cp):
            cp.start()
            sends.append(cp)

        for q_first in (0, 1):
            for k, (px, py) in enumerate(near):
                for t in range(2):
                    go(direct(k, t, k if q_first == 0 else 1 - k, me, (px, py, c)))
        for k, (px, py) in enumerate(near + [far]):
            go(conv_copy(k, me, (px, py, c)))
        for k, (px, py) in enumerate(near):
            other = near[1 - k]
            for t in range(2):
                direct(k, t, k, chip(px, py), (px, py, c)).wait_recv()
                go(passed_on(k, t, chip(px, py), (*other, c)))

        first_half = lax.broadcasted_iota(jnp.int32, (chunk, LANES), 1) % HEAD_DIM < half_dim
        row = lax.broadcasted_iota(jnp.int32, (chunk, LANES), 0)

        def table_rows(i, carry):
            at = pl.multiple_of(i * chunk, chunk)
            ang = (row + at).astype(F32) * freq_ref[...]
            sin = jnp.sin(ang)
            cos_ref[pl.ds(at, chunk), :] = jnp.cos(ang)
            s1_ref[pl.ds(at, chunk), :] = jnp.where(first_half, -sin, 0.0)
            s2_ref[pl.ds(at, chunk), :] = jnp.where(first_half, 0.0, sin)
            return carry

        lax.fori_loop(0, seq // chunk, table_rows, 0)

        for k, (px, py) in enumerate(near):
            for t in range(2):
                direct(k, t, 1 - k, chip(px, py), (px, py, c)).wait_recv()
                go(d2d(k, t, chip(px, py), c))
        for t in range(2):
            for k, (px, py) in enumerate(near):
                passed_on(k, t, chip(*far), (px, py, c)).wait_recv()
            go(d2d(2, t, chip(*far), c))
        for k, (px, py) in enumerate(near + [far]):
            conv_copy(k, chip(px, py), (px, py, c)).wait_recv()
            for t in range(2):
                d2d(k, t, chip(px, py), 1 - c).wait_recv()
        for cp in sends:
            cp.wait_send()

    vm = pl.BlockSpec(memory_space=pltpu.VMEM)
    dma = pltpu.SemaphoreType.DMA
    return pl.pallas_call(
        body, name="gather_weights",
        out_shape=(jax.ShapeDtypeStruct((N_CHIPS, d_model, width), BF16),
                   jax.ShapeDtypeStruct((N_CHIPS, rows, d_model), BF16),
                   jax.ShapeDtypeStruct((N_CHIPS, SUBLANES, LANES), F32),
                   *[jax.ShapeDtypeStruct((seq, LANES), F32)] * 3),
        in_specs=[vm, vm, vm, vm], out_specs=(vm,) * 6,
        scratch_shapes=[pltpu.VMEM((d_model, width), BF16), pltpu.VMEM((rows, d_model), BF16),
                        dma((2, 2, 2)), dma((2, 2, 2)), dma((2, 2)), dma((2, 2)), dma((3,)), dma((3,)),
                        dma((3, 2)), dma((3, 2))],
        compiler_params=_params(),
    )(w_in, w_out, cw, inv_freq)


def _rope(t, cos, s1, s2):
    return t * cos + pltpu.roll(t, LANES - HEAD_DIM // 2, 1) * s1 + pltpu.roll(t, HEAD_DIM // 2, 1) * s2


def _rope_transposed(g, cos, s1, s2):
    return g * cos + pltpu.roll(g * s1, HEAD_DIM // 2, 1) + pltpu.roll(g * s2, LANES - HEAD_DIM // 2, 1)


def inproj(x, g1, w_full, tables, attn_w, conv_w):
    seq, d_model = x.shape
    width = w_full.shape[2]
    tm = ROW_TILE
    groups = _lane_groups(attn_w)

    def body(x_ref, g_ref, w_ref, cos_ref, s1_ref, s2_ref,
             ht_ref, q_ref, k_ref, v_ref, qp_ref, kp_ref, vp_ref, ga_ref, cz_ref, stage):
        xv = x_ref[...]
        hb = ((xv * lax.rsqrt(jnp.mean(xv * xv, axis=-1, keepdims=True) + NORM_EPS)) * g_ref[...]).astype(BF16)
        ht_ref[...] = jnp.transpose(hb)
        cos, s1, s2 = cos_ref[...], s1_ref[...], s2_ref[...]

        def proj(a, b):
            parts = [jnp.dot(hb, w_ref[j, :, lo:hi], preferred_element_type=F32) for j, lo, hi in _col_pieces(a, b, width)]
            return parts[0] if len(parts) == 1 else jnp.concatenate(parts, axis=1)

        def emit(z, nat_ref, perm_ref, fn):
            for g, sl in enumerate(groups):
                val = fn(z[:, sl])
                nat_ref[:, sl] = val.astype(BF16)
                _stage_put(stage, g, val)
            for g, sl in enumerate(groups):
                _to_perm(stage, g, perm_ref, sl, BF16)

        emit(proj(0, attn_w), q_ref, qp_ref, lambda t: _rope(t, cos, s1, s2) * ATTN_SCALE)
        emit(proj(attn_w, 2 * attn_w), k_ref, kp_ref, lambda t: _rope(t, cos, s1, s2))
        emit(proj(2 * attn_w, 3 * attn_w), v_ref, vp_ref, lambda t: t)
        ga_ref[...] = proj(3 * attn_w, 4 * attn_w)
        cz_ref[...] = proj(4 * attn_w, 4 * attn_w + 4 * conv_w)

    row = lambda n: pl.BlockSpec((tm, n), lambda i: (i, 0))
    nat = jax.ShapeDtypeStruct((seq, attn_w), BF16)
    perm = jax.ShapeDtypeStruct(_perm_shape(seq, attn_w), BF16)
    return pl.pallas_call(
        body, name="inproj", grid=(seq // tm,),
        out_shape=(jax.ShapeDtypeStruct((d_model, seq), BF16), nat, nat, nat, perm, perm, perm,
                   jax.ShapeDtypeStruct((seq, attn_w), F32), jax.ShapeDtypeStruct((seq, 4 * conv_w), F32)),
        in_specs=[row(d_model), _const_spec((1, d_model)), _const_spec(w_full.shape), row(LANES), row(LANES), row(LANES)],
        out_specs=(pl.BlockSpec((d_model, tm), lambda i: (0, i)), row(attn_w), row(attn_w), row(attn_w),
                   _perm_tile_spec(attn_w, tm), _perm_tile_spec(attn_w, tm), _perm_tile_spec(attn_w, tm),
                   row(attn_w), row(4 * conv_w)),
        scratch_shapes=[pltpu.VMEM(_stage_shape(len(groups), tm), F32)],
        compiler_params=_params(("arbitrary",)),
    )(x, g1, w_full, *tables)


class _Mode:
    def __init__(self, name, seq):
        self.name = name
        if name == "nat":
            self.residues, blocks = 1, seq // BLOCK
        elif name == "p16":
            self.residues, blocks = PERM, seq // PERM // BLOCK
        else:
            self.residues, blocks = PJ, seq // PERM // P4_ROWS
        self.qb = max(d for d in range(1, MAX_QUERY_BLOCKS + 1) if blocks % d == 0)
        self.steps = blocks // self.qb

    def _spec(self, blocks, width, at):
        if self.name == "nat":
            return pl.BlockSpec((blocks * BLOCK, width), lambda *g: (at(*g)[1], 0))
        if self.name == "p16":
            return pl.BlockSpec((1, 1, blocks * BLOCK, width), lambda *g: (at(*g)[0] // PJ, at(*g)[0] % PJ, at(*g)[1], 0))
        return pl.BlockSpec((PJ, 1, blocks * P4_ROWS, width), lambda *g: (0, at(*g)[0], at(*g)[1], 0))

    def wide(self, width, where=lambda r, n: (r, n)):
        return self._spec(self.qb, width, where)

    def block_before(self, width, where=lambda r, n: (r, n)):
        return self._spec(1, width, lambda *g: (where(*g)[0], jnp.maximum(self.qb * where(*g)[1] - 1, 0)))

    def get(self, ref, sl, sub=0):
        if self.name == "nat":
            return ref[sub * BLOCK:(sub + 1) * BLOCK, sl]
        if self.name == "p16":
            return ref[0, 0, sub * BLOCK:(sub + 1) * BLOCK, sl]
        return jnp.concatenate([ref[j, 0, sub * P4_ROWS:(sub + 1) * P4_ROWS, sl] for j in range(PJ)], axis=0)

    def put(self, ref, sl, val, sub=0):
        val = val.astype(ref.dtype)
        if self.name == "nat":
            ref[sub * BLOCK:(sub + 1) * BLOCK, sl] = val
        elif self.name == "p16":
            ref[0, 0, sub * BLOCK:(sub + 1) * BLOCK, sl] = val
        else:
            for j in range(PJ):
                ref[j, 0, sub * P4_ROWS:(sub + 1) * P4_ROWS, sl] = val[j * P4_ROWS:(j + 1) * P4_ROWS]

    def keys(self, before_ref, wide_ref, sl, sub):
        older = self.get(before_ref, sl) if sub == 0 else self.get(wide_ref, sl, sub - 1)
        return jnp.concatenate([older, self.get(wide_ref, sl, sub)], axis=0)

    def index(self, idx, is_key):
        if self.name != "p4":
            return idx - BLOCK if is_key else idx
        within = jnp.bitwise_and(idx, BLOCK - 1)
        m = PJ * jnp.bitwise_and(within, P4_ROWS - 1) + jnp.right_shift(within, P4_ROWS.bit_length() - 1)
        return m + BLOCK * (jnp.right_shift(idx, BLOCK.bit_length() - 1) - 1) if is_key else m

    def bias(self, has_before):
        shape = (2 * BLOCK, BLOCK)
        kidx = lax.broadcasted_iota(jnp.int32, shape, 0)
        qidx = lax.broadcasted_iota(jnp.int32, shape, 1)
        rel = self.index(qidx, False) - self.index(kidx, True)
        valid = (rel >= 0) & (rel <= WINDOW_KEYS)
        if has_before is not True:
            valid = valid & ((kidx >= BLOCK) | has_before)
        one = jnp.where(valid, 0.0, NEG)
        return jnp.concatenate([one, one], axis=1)


def _head_masks():
    lane = lax.broadcasted_iota(jnp.int32, (BLOCK, LANES), 1)
    lo = lane < HEAD_DIM
    return lane, lo, jnp.where(lo, 1.0, 0.0).astype(BF16), jnp.where(lo, 0.0, 1.0).astype(BF16)


def attn_fwd(name, q, k, v, run):
    nat = name == "nat"
    seq = q.shape[0] if nat else q.shape[2] * PERM
    attn_w = q.shape[-1]
    mode = _Mode(name, seq)
    groups = _lane_groups(attn_w)
    first = run is None
    all_lanes = slice(0, LANES)

    def body(*refs):
        q_ref, kp_ref, kc_ref, vp_ref, vc_ref = refs[:5]
        if first:
            o_ref, l_ref = refs[5:]
        elif nat:
            oin_ref, lin_ref, o_ref, l_ref, ostage, lstage = refs[5:]
        else:
            oin_ref, lin_ref, o_ref, l_ref = refs[5:]
        n = pl.program_id(1)
        subs = range(mode.qb)
        biases = [mode.bias(n > 0)] + [mode.bias(True)] * (mode.qb - 1)
        _, lo, m_lo, m_hi = _head_masks()
        head_row = lax.broadcasted_iota(jnp.int32, (BLOCK, LANES), 0)
        ones = jnp.ones((2 * BLOCK, LANES), BF16)
        lrows = [jnp.zeros((BLOCK, LANES), F32) for _ in subs]
        if not first:
            if nat:
                for g, sl in enumerate(groups):
                    _from_perm(oin_ref, sl, ostage, g)
                _from_perm(lin_ref, all_lanes, lstage, 0)
            wide_rows = lambda a, sub: a[sub * BLOCK:(sub + 1) * BLOCK]
            before = [jnp.transpose(wide_rows(_stage_get(lstage, 0), sub) if nat else mode.get(lin_ref, all_lanes, sub))
                      for sub in subs]

        def probs(sub, p, sl):
            q2 = mode.get(q_ref, sl, sub)
            kcat = mode.keys(kp_ref, kc_ref, sl, sub)
            vcat = mode.keys(vp_ref, vc_ref, sl, sub)
            qq = jnp.concatenate([q2 * m_lo, q2 * m_hi], axis=0)
            s_t = _nt(kcat, qq) + biases[sub]
            m = jnp.max(s_t, axis=0, keepdims=True)
            pe = jnp.exp(s_t - m)
            lse = m + jnp.log(jnp.sum(pe, axis=0, keepdims=True))
            if not first:
                was = jnp.concatenate([before[sub][2 * p:2 * p + 1, :], before[sub][2 * p + 1:2 * p + 2, :]], axis=1)
                top = jnp.maximum(was, lse)
                lse = top + jnp.log(jnp.exp(was - top) + jnp.exp(lse - top))
                pe = pe * jnp.exp(m - lse)
            return jnp.concatenate([vcat, ones], axis=1), pe.astype(BF16), lse

        def output(sub, p, sl, vext, pb, lse):
            o_ext = _tn(pb, vext)
            if first:
                o_new = o_ext[:, :LANES] / o_ext[:, LANES:]
            else:
                o_prev = wide_rows(_stage_get(ostage, p), sub) if nat else mode.get(oin_ref, sl, sub)
                o_new = o_ext[:, :LANES] + jnp.concatenate([o_prev, o_prev], axis=0) * (1.0 - o_ext[:, LANES:])
            mode.put(o_ref, sl, jnp.where(lo, o_new[:BLOCK], o_new[BLOCK:]), sub)
            rows = jnp.where(head_row == 2 * p, lse[:, :BLOCK], lrows[sub])
            lrows[sub] = jnp.where(head_row == 2 * p + 1, lse[:, BLOCK:], rows)

        pending = None
        for sub in subs:
            for p, sl in enumerate(groups):
                nxt = probs(sub, p, sl)
                if pending is not None:
                    output(*pending)
                pending = (sub, p, sl, *nxt)
        output(*pending)
        for sub in subs:
            mode.put(l_ref, all_lanes, jnp.transpose(lrows[sub]), sub)

    ins = [q, k, k, v, v]
    specs = [mode.wide(attn_w), mode.block_before(attn_w), mode.wide(attn_w), mode.block_before(attn_w), mode.wide(attn_w)]
    scratch = []
    if not first:
        ins += list(run)
        if nat:
            rows_a = mode.qb * BLOCK // PERM
            specs += [pl.BlockSpec((PJ, PJ, rows_a, attn_w), lambda r, n: (0, 0, n, 0)),
                      pl.BlockSpec((PJ, PJ, rows_a, LANES), lambda r, n: (0, 0, n, 0))]
            scratch = [pltpu.VMEM(_stage_shape(len(groups), mode.qb * BLOCK), F32),
                       pltpu.VMEM(_stage_shape(1, mode.qb * BLOCK), F32)]
        else:
            specs += [mode.wide(attn_w), mode.wide(LANES)]
    if nat:
        out_shape = (jax.ShapeDtypeStruct((seq, attn_w), F32), jax.ShapeDtypeStruct((seq, LANES), F32))
    else:
        out_shape = (jax.ShapeDtypeStruct(_perm_shape(seq, attn_w), F32), jax.ShapeDtypeStruct(_perm_shape(seq, LANES), F32))
    return pl.pallas_call(
        body, name=f"attn_fwd_{name}", grid=(mode.residues, mode.steps),
        out_shape=out_shape, in_specs=specs, out_specs=(mode.wide(attn_w), mode.wide(LANES)),
        scratch_shapes=scratch,
        compiler_params=_params(("arbitrary", "arbitrary")),
    )(*ins)


def attn_bwd(name, q, k, v, d_o, lse, delta, run):
    nat = name == "nat"
    seq = q.shape[0] if nat else q.shape[2] * PERM
    attn_w = q.shape[-1]
    mode = _Mode(name, seq)
    steps, qb = mode.steps, mode.qb
    single = steps == 1
    groups = _lane_groups(attn_w)
    first = run is None
    all_lanes = slice(0, LANES)

    def body(*refs):
        q_ref, kp_ref, kc_ref, vp_ref, vc_ref, do_ref, lse_ref, dl_ref = refs[:8]
        if first:
            dq_ref, dk_ref, dv_ref, ck, cv = refs[8:]
        else:
            dqi_ref, dki_ref, dvi_ref, dq_ref, dk_ref, dv_ref, ck, cv = refs[8:]
        g = pl.program_id(1) if single else pl.program_id(0)
        n = g if single else lax.rem(g, steps)
        carries = ((ck, dk_ref, None if first else dki_ref), (cv, dv_ref, None if first else dvi_ref))

        def emit(out_ref, acc_ref, sl, sub, val):
            if acc_ref is not None:
                val = val + mode.get(acc_ref, sl, sub).astype(F32)
            mode.put(out_ref, sl, val, sub)

        if not single:
            @pl.when(g == 0)
            def _():
                ck[...] = jnp.zeros_like(ck)
                cv[...] = jnp.zeros_like(cv)

        @pl.when(g < total)
        def _():
            biases = [mode.bias(n > 0)] + [mode.bias(True)] * (qb - 1)
            _, lo, m_lo, m_hi = _head_masks()

            def scores(sub, p, sl, lse_t, dl_t):
                q2, do2 = mode.get(q_ref, sl, sub), mode.get(do_ref, sl, sub)
                kcat = mode.keys(kp_ref, kc_ref, sl, sub)
                vcat = mode.keys(vp_ref, vc_ref, sl, sub)
                qq = jnp.concatenate([q2 * m_lo, q2 * m_hi], axis=0)
                dd = jnp.concatenate([do2 * m_lo, do2 * m_hi], axis=0)
                h0 = 2 * p
                lse2 = jnp.concatenate([lse_t[h0:h0 + 1, :], lse_t[h0 + 1:h0 + 2, :]], axis=1)
                dl2 = jnp.concatenate([dl_t[h0:h0 + 1, :], dl_t[h0 + 1:h0 + 2, :]], axis=1)
                p_t = jnp.exp(_nt(kcat, qq) + (biases[sub] - lse2))
                ds_t = p_t * (_nt(vcat, dd) - dl2)
                return qq, dd, kcat, p_t.astype(BF16), ds_t.astype(BF16)

            def grads(sub, sl, qq, dd, kcat, pb, dsb):
                dqb = _tn(dsb, kcat)
                dq2 = jnp.where(lo, dqb[:BLOCK], dqb[BLOCK:]) * ATTN_SCALE
                if not first:
                    dq2 = dq2 + mode.get(dqi_ref, sl, sub).astype(F32)
                mode.put(dq_ref, sl, dq2, sub)
                for (carry, out_ref, acc_ref), lhs, rhs in zip(carries, (dsb, pb), (qq, dd)):
                    both = jnp.dot(lhs, rhs, preferred_element_type=F32)
                    if sub == 0:
                        if not single:
                            for s in range(qb - 1):
                                emit(out_ref, acc_ref, sl, s, carry[s, :, sl])
                            emit(out_ref, acc_ref, sl, qb - 1, carry[qb - 1, :, sl] + both[:BLOCK])
                        carry[0, :, sl] = both[BLOCK:]
                    else:
                        carry[sub - 1, :, sl] += both[:BLOCK]
                        carry[sub, :, sl] = both[BLOCK:]
                    if single and sub == qb - 1:
                        for s in range(qb):
                            emit(out_ref, acc_ref, sl, s, carry[s, :, sl])

            stats = [(jnp.transpose(mode.get(lse_ref, all_lanes, sub)),
                      jnp.transpose(mode.get(dl_ref, all_lanes, sub))) for sub in range(qb)]
            pending = None
            for p, sl in enumerate(groups):
                for sub in range(qb):
                    nxt = scores(sub, p, sl, *stats[sub])
                    if pending is not None:
                        grads(*pending)
                    pending = (sub, sl, *nxt)
            grads(*pending)

        if not single:
            @pl.when(g == total)
            def _():
                for carry, out_ref, acc_ref in carries:
                    for sl in groups:
                        for s in range(qb):
                            emit(out_ref, acc_ref, sl, s, carry[s, :, sl])

    total = mode.residues * steps
    if single:
        here = before = lambda r, n: (r, n)
    else:
        locate = lambda g: (g // steps, lax.rem(g, steps))
        here = lambda g: locate(jnp.minimum(g, total - 1))
        before = lambda g: locate(jnp.maximum(g - 1, 0))
    wide = lambda w: mode.wide(w, here)
    ins = [q, k, k, v, v, d_o, lse, delta]
    specs = [wide(attn_w), mode.block_before(attn_w, here), wide(attn_w), mode.block_before(attn_w, here), wide(attn_w),
             wide(attn_w), wide(LANES), wide(LANES)]
    if not first:
        ins += list(run)
        specs += [wide(attn_w), mode.wide(attn_w, before), mode.wide(attn_w, before)]
    shp = jax.ShapeDtypeStruct((seq, attn_w) if nat else _perm_shape(seq, attn_w), BF16)
    grid = (mode.residues, 1) if single else (total + 1,)
    return pl.pallas_call(
        body, name=f"attn_bwd_{name}", grid=grid,
        out_shape=(shp, shp, shp), in_specs=specs,
        out_specs=(wide(attn_w), mode.wide(attn_w, before), mode.wide(attn_w, before)),
        scratch_shapes=[pltpu.VMEM((qb, BLOCK, attn_w), F32), pltpu.VMEM((qb, BLOCK, attn_w), F32)],
        compiler_params=_params(("arbitrary",) * len(grid)),
    )(*ins)


def _shift_down(u, halo, k):
    rolled = pltpu.roll(u, k, 0)
    row = lax.broadcasted_iota(jnp.int32, halo.shape, 0)
    top = jnp.where(row < k, pltpu.roll(halo, k, 0), rolled[:SUBLANES])
    return jnp.concatenate([top, rolled[SUBLANES:]], axis=0)


def _shift_up(u, halo, k):
    rows = u.shape[0]
    rolled = pltpu.roll(u, rows - k, 0)
    row = lax.broadcasted_iota(jnp.int32, halo.shape, 0)
    bot = jnp.where(row >= SUBLANES - k, pltpu.roll(halo, SUBLANES - k, 0), rolled[rows - SUBLANES:])
    return jnp.concatenate([rolled[:rows - SUBLANES], bot], axis=0)


def tail(o, lse, ga, cz, x, tgt, w_out, g2, cw):
    seq, d_model = x.shape
    attn_w = o.shape[1]
    conv_w = cz.shape[1] // 4
    mix = attn_w + conv_w
    groups = _lane_groups(attn_w)
    tm = ROW_TILE
    nt = seq // tm
    hb = tm // SUBLANES

    def body(o_ref, l_ref, ga_ref, cz_ref, hz_ref, x_ref, t_ref, w_ref, g_ref, cw_ref,
             do_ref, dl_ref, dop_ref, dlp_ref, lp_ref, dga_ref, dcb_ref, dgc_ref, dcv_ref, e_ref,
             dw_ref, dg_ref, dcw_ref, loss_ref, stage):
        i = pl.program_id(0)

        @pl.when(i == 0)
        def _():
            dw_ref[...] = jnp.zeros_like(dw_ref)
            dg_ref[...] = jnp.zeros_like(dg_ref)
            dcw_ref[...] = jnp.zeros_like(dcw_ref)
            loss_ref[...] = jnp.zeros_like(loss_ref)

        u = cz_ref[:, 2 * conv_w:3 * conv_w] * cz_ref[:, 0:conv_w]
        uh = hz_ref[:, 2 * conv_w:3 * conv_w] * hz_ref[:, 0:conv_w]
        uh = jnp.where(i > 0, uh, 0.0)
        u1 = _shift_down(u, uh, 1)
        u2 = _shift_down(u, uh, 2)
        w0, w1, w2 = cw_ref[0:1, :], cw_ref[1:2, :], cw_ref[2:3, :]
        cvv = u2 * w0 + u1 * w1 + u * w2
        gv = g_ref[...]
        all_lanes = slice(0, LANES)

        def forward(rs):
            ov, gav = o_ref[rs, :], ga_ref[rs, :]
            sig_a = _sigmoid(gav)
            silu_a = gav * sig_a
            cb, gc = cz_ref[rs, conv_w:2 * conv_w], cz_ref[rs, 3 * conv_w:4 * conv_w]
            sig_c = _sigmoid(gc)
            silu_c = gc * sig_c
            bc = cb * cvv[rs]
            mixed = jnp.concatenate([ov * silu_a, bc * silu_c], axis=1).astype(BF16)
            yv = jnp.dot(mixed, w_ref[...], preferred_element_type=F32)
            return ov, gav, sig_a, silu_a, cb, gc, sig_c, silu_c, bc, mixed, yv

        def loss_and_dy(rs, mixed, yv):
            r2 = lax.rsqrt(jnp.mean(yv * yv, axis=-1, keepdims=True) + NORM_EPS)
            yhat = yv * r2
            diff = (x_ref[rs, :] + yhat * gv) - t_ref[rs, :]
            loss_ref[...] += _rowgroup_sum(diff * diff)
            ev = diff * (1.0 / d_model)
            e_ref[rs, :] = ev
            dg_ref[...] += _rowgroup_sum(ev * yhat)
            eg = ev * gv
            dy = (r2 * (eg - yhat * jnp.mean(eg * yhat, axis=-1, keepdims=True))).astype(BF16)
            dw_ref[...] += _tn(mixed, dy)
            return _nt(dy, w_ref[...])

        def backward(rs, ov, gav, sig_a, silu_a, cb, gc, sig_c, silu_c, bc, dm):
            rows = rs.stop - rs.start
            dma, dmc = dm[:, :attn_w], dm[:, attn_w:]
            dov = dma * silu_a
            do_ref[rs, :] = dov.astype(BF16)
            dga_ref[rs, :] = (dma * ov * (sig_a * (1.0 + gav * (1.0 - sig_a)))).astype(BF16)
            prod = dov * ov
            lane = lax.broadcasted_iota(jnp.int32, (rows, LANES), 1)
            lo = lane < HEAD_DIM
            dblk = jnp.zeros((rows, LANES), F32)
            for p, sl in enumerate(groups):
                pr = prod[:, sl]
                dblk = jnp.where(lane == 2 * p, jnp.sum(jnp.where(lo, pr, 0.0), axis=1, keepdims=True), dblk)
                dblk = jnp.where(lane == 2 * p + 1, jnp.sum(jnp.where(lo, 0.0, pr), axis=1, keepdims=True), dblk)
                _stage_put(stage, p, dov[:, sl], rs.start)
            dl_ref[rs, :] = dblk
            _stage_put(stage, len(groups), dblk, rs.start)
            _stage_put(stage, len(groups) + 1, l_ref[rs, :], rs.start)
            dsc = dmc * silu_c
            cv_rows = cvv[rs]
            dcb_ref[rs, :] = (dsc * cv_rows).astype(BF16)
            dgc_ref[rs, :] = (dmc * bc * (sig_c * (1.0 + gc * (1.0 - sig_c)))).astype(BF16)
            dcv = dsc * cb
            dcv_ref[rs, :] = dcv
            dcw_ref[0:SUBLANES, :] += _rowgroup_sum(dcv * u2[rs])
            dcw_ref[SUBLANES:2 * SUBLANES, :] += _rowgroup_sum(dcv * u1[rs])
            dcw_ref[2 * SUBLANES:3 * SUBLANES, :] += _rowgroup_sum(dcv * u[rs])

        halves = [slice(0, tm // 2), slice(tm // 2, tm)]
        fwd = [forward(rs) for rs in halves]
        dms = [loss_and_dy(rs, f[9], f[10]) for rs, f in zip(halves, fwd)]
        for rs, f, dm in zip(halves, fwd, dms):
            backward(rs, *f[:9], dm)
        for p, sl in enumerate(groups):
            _to_perm(stage, p, dop_ref, sl, BF16)
        _to_perm(stage, len(groups), dlp_ref, all_lanes, F32)
        _to_perm(stage, len(groups) + 1, lp_ref, all_lanes, F32)

    row = lambda n: pl.BlockSpec((tm, n), lambda i: (i, 0))
    whole = lambda a, b: pl.BlockSpec((a, b), lambda i: (0, 0))
    return pl.pallas_call(
        body, name="tail", grid=(nt,),
        out_shape=(jax.ShapeDtypeStruct((seq, attn_w), BF16), jax.ShapeDtypeStruct((seq, LANES), F32),
                   jax.ShapeDtypeStruct(_perm_shape(seq, attn_w), BF16), jax.ShapeDtypeStruct(_perm_shape(seq, LANES), F32),
                   jax.ShapeDtypeStruct(_perm_shape(seq, LANES), F32),
                   jax.ShapeDtypeStruct((seq, attn_w), BF16), jax.ShapeDtypeStruct((seq, conv_w), BF16),
                   jax.ShapeDtypeStruct((seq, conv_w), BF16), jax.ShapeDtypeStruct((seq, conv_w), F32),
                   jax.ShapeDtypeStruct((seq, d_model), F32), jax.ShapeDtypeStruct((mix, d_model), F32),
                   jax.ShapeDtypeStruct((SUBLANES, d_model), F32), jax.ShapeDtypeStruct((CONV_K * SUBLANES, conv_w), F32),
                   jax.ShapeDtypeStruct((SUBLANES, d_model), F32)),
        in_specs=[row(attn_w), row(LANES), row(attn_w), row(4 * conv_w),
                  pl.BlockSpec((SUBLANES, 4 * conv_w), lambda i: (jnp.maximum(i * hb - 1, 0), 0)),
                  row(d_model), row(d_model), _const_spec((mix, d_model)), _const_spec((1, d_model)),
                  _const_spec((SUBLANES, conv_w))],
        out_specs=(row(attn_w), row(LANES), _perm_tile_spec(attn_w, tm), _perm_tile_spec(LANES, tm), _perm_tile_spec(LANES, tm),
                   row(attn_w), row(conv_w), row(conv_w), row(conv_w), row(d_model),
                   whole(mix, d_model), whole(SUBLANES, d_model), whole(CONV_K * SUBLANES, conv_w),
                   whole(SUBLANES, d_model)),
        scratch_shapes=[pltpu.VMEM(_stage_shape(len(groups) + 2, tm), F32)],
        compiler_params=_params(("arbitrary",)),
    )(o, lse, ga, cz, cz, x, tgt, w_out, g2, cw)


def dz_dx(nat_grads, perm_grads, dga, dcb, dgc, dcv, cz, tables, x, g1, e, w_full, cw):
    seq, d_model = x.shape
    attn_w = dga.shape[1]
    conv_w = dcv.shape[1]
    width = w_full.shape[2]
    in_w = 4 * attn_w + 4 * conv_w
    groups = _lane_groups(attn_w)
    tm = DZ_ROW_TILE
    nt = seq // tm
    hb = tm // SUBLANES

    def body(dq_ref, dk_ref, dv_ref, dqp_ref, dkp_ref, dvp_ref, dga_ref, dcb_ref, dgc_ref, dcv_ref, nh_ref, ch_ref, cc_ref,
             cos_ref, s1_ref, s2_ref, x_ref, g_ref, e_ref, w_ref, cw_ref, gx_ref, dz_ref, dg_ref, stage):
        i = pl.program_id(0)

        @pl.when(i == 0)
        def _():
            dg_ref[...] = jnp.zeros_like(dg_ref)

        cos, s1, s2 = cos_ref[...], s1_ref[...], s2_ref[...]

        def qkv_columns(t, nat_ref, perm_ref):
            for g, sl in enumerate(groups):
                _from_perm(perm_ref, sl, stage, g)
            for g, sl in enumerate(groups):
                tot = nat_ref[:, sl].astype(F32) + _stage_get(stage, g)
                if t < 2:
                    tot = _rope_transposed(tot, cos, s1, s2)
                dz_ref[:, t * attn_w + g * LANES:t * attn_w + (g + 1) * LANES] = tot.astype(BF16)

        def dh_part(j):
            return _nt(dz_ref[:, j * width:(j + 1) * width], w_ref[j])

        dcv = dcv_ref[...]
        nh = jnp.where(i < nt - 1, nh_ref[...], 0.0)
        w0, w1, w2 = cw_ref[0:1, :], cw_ref[1:2, :], cw_ref[2:3, :]
        du = dcv * w2 + _shift_up(dcv, nh, 1) * w1 + _shift_up(dcv, nh, 2) * w0
        base = 4 * attn_w
        dz_ref[:, base:base + conv_w] = (du * cc_ref[...]).astype(BF16)
        dz_ref[:, base + conv_w:base + 2 * conv_w] = dcb_ref[...]
        dz_ref[:, base + 2 * conv_w:base + 3 * conv_w] = (du * ch_ref[...]).astype(BF16)
        dz_ref[:, base + 3 * conv_w:base + 4 * conv_w] = dgc_ref[...]
        dz_ref[:, 3 * attn_w:4 * attn_w] = dga_ref[...]
        ready = in_w
        dh = None
        for t, nat_ref, perm_ref in ((2, dv_ref, dvp_ref), (1, dk_ref, dkp_ref), (0, dq_ref, dqp_ref), (None, None, None)):
            lowest_open = 0 if t is None else (t + 1) * attn_w
            while ready - width >= lowest_open:
                ready -= width
                part = dh_part(ready // width)
                dh = part if dh is None else dh + part
            if t is not None:
                qkv_columns(t, nat_ref, perm_ref)
        xv = x_ref[...]
        r1 = lax.rsqrt(jnp.mean(xv * xv, axis=-1, keepdims=True) + NORM_EPS)
        xhat = xv * r1
        dg_ref[...] += _rowgroup_sum(dh * xhat)
        dhg = dh * g_ref[...]
        gx_ref[...] = r1 * (dhg - xhat * jnp.mean(dhg * xhat, axis=-1, keepdims=True)) + e_ref[...]

    row = lambda n: pl.BlockSpec((tm, n), lambda i: (i, 0))
    whole = lambda a, b: pl.BlockSpec((a, b), lambda i: (0, 0))
    pt = _perm_tile_spec(attn_w, tm)
    return pl.pallas_call(
        body, name="dz_dx", grid=(nt,),
        out_shape=(jax.ShapeDtypeStruct((seq, d_model), F32), jax.ShapeDtypeStruct((seq, in_w), BF16),
                   jax.ShapeDtypeStruct((SUBLANES, d_model), F32)),
        in_specs=[row(attn_w), row(attn_w), row(attn_w), pt, pt, pt, row(attn_w), row(conv_w), row(conv_w), row(conv_w),
                  pl.BlockSpec((SUBLANES, conv_w), lambda i: (jnp.minimum((i + 1) * hb, seq // SUBLANES - 1), 0)),
                  pl.BlockSpec((tm, conv_w), lambda i: (i, 0)), pl.BlockSpec((tm, conv_w), lambda i: (i, 2)),
                  row(LANES), row(LANES), row(LANES), row(d_model), _const_spec((1, d_model)), row(d_model),
                  _const_spec(w_full.shape), _const_spec((SUBLANES, conv_w))],
        out_specs=(row(d_model), row(in_w), whole(SUBLANES, d_model)),
        scratch_shapes=[pltpu.VMEM(_stage_shape(len(groups), tm), F32)],
        compiler_params=_params(("arbitrary",)),
    )(*nat_grads, *perm_grads, dga, dcb, dgc, dcv, dcv, cz, cz, *tables, x, g1, e, w_full, cw)


def dw_in_reduce(ht, dz, g_out, small):
    d_model, seq = ht.shape
    half = dz.shape[1] // N_DEV
    ts = min(2048, seq)
    steps = seq // ts
    x, y, c = lax.axis_index("x"), lax.axis_index("y"), lax.axis_index("c")
    far_first = lambda x, y: [(1 - x, 1 - y), (1 - x, y), (x, 1 - y)]
    chips = jnp.stack([2 * px + py for px, py in far_first(x, y)] + [2 * x + y]).astype(jnp.int32)
    order = jnp.stack([2 * chips + (1 - c), 2 * chips + c], axis=1).reshape(N_DEV)

    def body(order_ref, ht_ref, dz_ref, go_ref, sm_ref, out_ref, ro_ref, rs_ref,
             acc, theirs, staged, contrib, resbuf, out_sem, sa, ra, sb, rb, sc, rc,
             o_mine, o_theirs, o_staged, o_contrib, o_res, sbuf, o_load, osa, ora, osb, orb, osc, orc, ss, rs):
        del order_ref
        p, s = pl.program_id(0), pl.program_id(1)
        x, y, c = lax.axis_index("x"), lax.axis_index("y"), lax.axis_index("c")
        me = 2 * x + y
        sib = (x, y, 1 - c)
        peers = far_first(x, y)
        slot = p % 2

        flips = [(fx, fy, fc) for fx in (0, 1) for fy in (0, 1) for fc in (0, 1)][1:]
        my8 = 4 * x + 2 * y + c
        chip_ids = [2 * px + py for px, py in peers] + [me]

        def small_copy(k, slot8, to):
            return pltpu.make_async_remote_copy(src_ref=sm_ref, dst_ref=sbuf.at[slot8], send_sem=ss.at[k], recv_sem=rs.at[k],
                                                device_id=to, device_id_type=MESH)

        def small_peer(k):
            fx, fy, fc = flips[k]
            return _flip(x, fx), _flip(y, fy), _flip(c, fc)

        def oa_copy(pos):
            j = chip_ids[pos]
            return pltpu.make_async_remote_copy(src_ref=go_ref.at[j, 1 - c], dst_ref=o_theirs.at[j], send_sem=osa.at[pos],
                                                recv_sem=ora.at[pos], device_id=sib, device_id_type=MESH)

        def o_load_copy(pos):
            j = chip_ids[pos]
            return pltpu.make_async_copy(go_ref.at[j, c], o_mine.at[j], o_load.at[pos])

        def ob_copy(k, piece, slot4):
            px, py = peers[k]
            return pltpu.make_async_remote_copy(src_ref=o_staged.at[piece], dst_ref=o_contrib.at[slot4], send_sem=osb.at[k],
                                                recv_sem=orb.at[k], device_id=(px, py, c), device_id_type=MESH)

        def oc_copy(which):
            return pltpu.make_async_remote_copy(src_ref=o_res.at[which], dst_ref=o_res.at[which], send_sem=osc, recv_sem=orc,
                                                device_id=sib, device_id_type=MESH)

        @pl.when((p == 0) & (s == 0))
        def _():
            sbuf[my8] = sm_ref[...]
            for k in range(N_DEV - 1):
                small_copy(k, my8, small_peer(k)).start()
            for pos in range(N_CHIPS):
                o_load_copy(pos).start()
                oa_copy(pos).start()

        @pl.when((p == 1) & (s == steps - 1))
        def _():
            for pos in range(N_CHIPS):
                j = chip_ids[pos]
                o_load_copy(pos).wait()
                oa_copy(pos).wait_recv()
                if pos < N_CHIPS - 1:
                    o_staged[j] = (o_mine[j] + o_theirs[j]).astype(BF16)
                    ob_copy(pos, j, me).start()
                else:
                    o_mine[j] = o_mine[j] + o_theirs[j]
                    o_contrib[j] = o_mine[j].astype(BF16)

        @pl.when((p == 4) & (s == steps - 1))
        def _():
            for k in range(N_CHIPS - 1):
                ob_copy(k, me, chip_ids[k]).wait_recv()
            own = o_mine[me]
            term = lambda j: jnp.where(me == j, own, o_contrib[j].astype(F32))
            o_res[c] = ((term(0) + term(1)) + term(2)) + term(3)
            oc_copy(c).start()

        def a_copy(k):
            return pltpu.make_async_remote_copy(src_ref=acc.at[0], dst_ref=theirs.at[k], send_sem=sa.at[k], recv_sem=ra.at[k],
                                                device_id=sib, device_id_type=MESH)

        def b_copy(k):
            px, py = peers[k]
            return pltpu.make_async_remote_copy(src_ref=staged.at[k], dst_ref=contrib.at[k], send_sem=sb.at[k], recv_sem=rb.at[k],
                                                device_id=(px, py, c), device_id_type=MESH)

        def c_copy(which):
            return pltpu.make_async_remote_copy(src_ref=resbuf.at[which], dst_ref=resbuf.at[which], send_sem=sc, recv_sem=rc,
                                                device_id=sib, device_id_type=MESH)

        @pl.when(s == 0)
        def _():
            for k in range(N_CHIPS - 1):
                @pl.when(p == 2 * k + 2)
                def _():
                    a_copy(k).wait_send()
            acc[slot] = jnp.zeros((d_model, half), F32)

        acc[slot] += jnp.dot(ht_ref[...], dz_ref[...], preferred_element_type=F32)

        @pl.when(s == steps - 1)
        def _():
            for k in range(N_CHIPS):
                @pl.when(p == 2 * k)
                def _():
                    a_copy(k).start()
            for k in range(N_CHIPS - 1):
                @pl.when(p == 2 * k + 1)
                def _():
                    a_copy(k).wait_recv()
                    staged[k] = (acc[1] + theirs[k]).astype(BF16)
                    b_copy(k).start()

            @pl.when(p == N_DEV - 1)
            def _():
                a_copy(N_CHIPS - 1).wait_recv()
                tot = acc[1] + theirs[N_CHIPS - 1]
                for k in range(N_CHIPS - 1):
                    b_copy(k).wait_recv()
                    tot = tot + contrib[k].astype(F32)
                resbuf[c] = tot
                c_copy(c).start()
                c_copy(1 - c).wait_recv()
                done = pltpu.make_async_copy(resbuf, out_ref, out_sem)
                done.start()
                oc_copy(1 - c).wait_recv()
                ro_ref[...] = o_res[...]
                for k in range(N_DEV - 1):
                    px, py, pc = small_peer(k)
                    small_copy(k, 4 * px + 2 * py + pc, (px, py, pc)).wait_recv()
                tot8 = sbuf[0]
                for d in range(1, N_DEV):
                    tot8 = tot8 + sbuf[d]
                rs_ref[...] = tot8
                a_copy(N_CHIPS - 1).wait_send()
                for k in range(N_CHIPS - 1):
                    b_copy(k).wait_send()
                    ob_copy(k, chip_ids[k], me).wait_send()
                c_copy(c).wait_send()
                oc_copy(c).wait_send()
                for pos in range(N_CHIPS):
                    oa_copy(pos).wait_send()
                for k in range(N_DEV - 1):
                    small_copy(k, my8, small_peer(k)).wait_send()
                done.wait()

    dma = pltpu.SemaphoreType.DMA
    o_shape = g_out.shape[1:]
    go = g_out.reshape(N_CHIPS, 2, *o_shape)
    const = lambda shape: pl.BlockSpec(shape, lambda p, s, order_ref: (0,) * len(shape))
    grid_spec = pltpu.PrefetchScalarGridSpec(
        num_scalar_prefetch=1, grid=(N_DEV, steps),
        in_specs=[pl.BlockSpec((d_model, ts), lambda p, s, order_ref: (0, s)),
                  pl.BlockSpec((ts, half), lambda p, s, order_ref: (s, order_ref[p])),
                  pl.BlockSpec(memory_space=pl.ANY), const(small.shape)],
        out_specs=(pl.BlockSpec(memory_space=pl.ANY), const((2, *o_shape)), const(small.shape)),
        scratch_shapes=[pltpu.VMEM((2, d_model, half), F32), pltpu.VMEM((N_CHIPS, d_model, half), F32),
                        pltpu.VMEM((N_CHIPS - 1, d_model, half), BF16), pltpu.VMEM((N_CHIPS - 1, d_model, half), BF16),
                        pltpu.VMEM((2, d_model, half), F32), dma,
                        dma((N_CHIPS,)), dma((N_CHIPS,)), dma((N_CHIPS - 1,)), dma((N_CHIPS - 1,)), dma, dma,
                        pltpu.VMEM((N_CHIPS, *o_shape), F32), pltpu.VMEM((N_CHIPS, *o_shape), F32),
                        pltpu.VMEM((N_CHIPS, *o_shape), BF16), pltpu.VMEM((N_CHIPS, *o_shape), BF16),
                        pltpu.VMEM((2, *o_shape), F32), pltpu.VMEM((N_DEV, *small.shape), F32),
                        dma((N_CHIPS,)), dma((N_CHIPS,)), dma((N_CHIPS,)), dma((N_CHIPS - 1,)), dma((N_CHIPS - 1,)), dma, dma,
                        dma((N_DEV - 1,)), dma((N_DEV - 1,))])
    return pl.pallas_call(
        body, name="dw_in_reduce", grid_spec=grid_spec,
        out_shape=(jax.ShapeDtypeStruct((2, d_model, half), F32), jax.ShapeDtypeStruct((2, *o_shape), F32),
                   jax.ShapeDtypeStruct(small.shape, F32)),
        compiler_params=_params(("arbitrary", "arbitrary")),
    )(order, ht, dz, go, small)


def _adam_math(w, g, m, v):
    m = ADAM_B1 * m + (1.0 - ADAM_B1) * g
    v = ADAM_B2 * v + (1.0 - ADAM_B2) * (g * g)
    m_hat = m / (1.0 - ADAM_B1 ** ADAM_STEP)
    v_hat = v / (1.0 - ADAM_B2 ** ADAM_STEP)
    delta = -ADAM_LR * (m_hat / (jnp.sqrt(v_hat) + ADAM_EPS) + ADAM_WD * w)
    return delta, m, v


def adam_shard(name, w, g2, m, v, block, grid, w_map, g_map):
    def body(w_ref, g_ref, m_ref, v_ref, go_ref, d_ref, mo_ref, vo_ref):
        g = g_ref[0]
        delta, mn, vn = _adam_math(w_ref[...], g, m_ref[...], v_ref[...])
        go_ref[...] = g
        d_ref[...] = delta
        mo_ref[...] = mn
        vo_ref[...] = vn

    ws = pl.BlockSpec(block, w_map)
    shp = jax.ShapeDtypeStruct(w.shape, F32)
    return pl.pallas_call(
        body, name=name, grid=grid, out_shape=(shp, shp, shp, shp),
        in_specs=[ws, pl.BlockSpec((1, *block), g_map), ws, ws], out_specs=(ws, ws, ws, ws),
        compiler_params=_params(("arbitrary",) * len(grid)),
    )(w, g2, m, v)


def adam_small(ws, gs, ms, vs):
    n = len(ws)

    def body(*refs):
        ins, outs = refs[:4 * n], refs[4 * n:]
        for t in range(n):
            delta, mn, vn = _adam_math(ins[t][...], ins[n + t][...], ins[2 * n + t][...], ins[3 * n + t][...])
            outs[3 * t][...] = delta
            outs[3 * t + 1][...] = mn
            outs[3 * t + 2][...] = vn

    vm = pl.BlockSpec(memory_space=pltpu.VMEM)
    outs = pl.pallas_call(
        body, name="adam_small",
        out_shape=tuple(jax.ShapeDtypeStruct(w.shape, F32) for w in ws for _ in range(3)),
        in_specs=[vm] * (4 * n), out_specs=tuple([vm] * (3 * n)),
        compiler_params=_params(),
    )(*ws, *gs, *ms, *vs)
    return [outs[3 * t:3 * t + 3] for t in range(n)]


def kernel(x, norm_pre_g, w_in, conv_w, w_out, norm_post_g, loss_target, m_norm_pre_g, m_w_in, m_conv_w, m_w_out, m_norm_post_g, v_norm_pre_g, v_w_in, v_conv_w, v_w_out, v_norm_post_g):
    _, seq, d_model = x.shape
    width = w_in.shape[1]
    conv_q = conv_w.shape[1]
    conv_width = N_CHIPS * conv_q
    attn_width = d_model - conv_width
    xs, tg = x[0], loss_target[0]
    g1, g2 = norm_pre_g.reshape(1, d_model), norm_post_g.reshape(1, d_model)

    w_full, wout_full, cw_full, *tables = gather_weights(w_in, w_out, conv_w, seq)
    wout2 = wout_full.reshape(attn_width + conv_width, d_model)
    cw = jnp.zeros((SUBLANES, conv_width), F32).at[:CONV_K].set(
        cw_full[:, :CONV_K, :conv_q].transpose(1, 0, 2).reshape(CONV_K, conv_width))

    ht, q, k, v, qp, kp, vp, ga, cz = inproj(xs, g1, w_full, tables, attn_width, conv_width)
    run = attn_fwd("p4", qp, kp, vp, None)
    run = attn_fwd("p16", qp, kp, vp, run)
    o, lse = attn_fwd("nat", q, k, v, run)
    (d_o, delta, d_op, delta_p, lse_p, dga, dcb, dgc, dcv, e, dwout, dg2, dcw, loss_acc) = tail(
        o, lse, ga, cz, xs, tg, wout2, g2, cw)
    nat_grads = attn_bwd("nat", q, k, v, d_o, lse, delta, None)
    perm_grads = attn_bwd("p4", qp, kp, vp, d_op, lse_p, delta_p, None)
    perm_grads = attn_bwd("p16", qp, kp, vp, d_op, lse_p, delta_p, perm_grads)
    grad_x, dz, dg1 = dz_dx(nat_grads, perm_grads, dga, dcb, dgc, dcv, cz, tables, xs, g1, e, w_full, cw)

    small = jnp.zeros((SUBLANES, d_model), F32)
    small = small.at[0].set(dg1.sum(axis=0)).at[1].set(dg2.sum(axis=0))
    small = small.at[2:2 + CONV_K, :conv_width].set(dcw.reshape(CONV_K, SUBLANES, conv_width).sum(axis=1))
    small = small.at[2 + CONV_K, 0].set(jnp.sum(loss_acc))
    rin, rout, rsmall = dw_in_reduce(ht, dz, dwout.reshape(N_DEV, -1, d_model), small)

    half = width // 2
    tr = 256
    gw_in, d_in, m_in, v_in = adam_shard(
        "adam_w_in", w_in, rin, m_w_in, v_w_in, (tr, half), (2, d_model // tr),
        lambda hf, i: (i, hf), lambda hf, i: (hf, i, 0))
    rq = w_out.shape[0] // 2
    gw_out, d_out, m_out, v_out = adam_shard(
        "adam_w_out", w_out, rout, m_w_out, v_w_out, (rq, d_model), (2,),
        lambda hf: (hf, 0), lambda hf: (hf, 0, 0))

    chip = 2 * lax.axis_index("x") + lax.axis_index("y")
    g_pre, g_post = rsmall[0:1], rsmall[1:2]
    g_conv = lax.dynamic_slice(rsmall[2:2 + CONV_K, :conv_width], (0, chip * conv_q), (CONV_K, conv_q))
    (d_pre, m_pre, v_pre), (d_post, m_post, v_post), (d_cv, m_cv, v_cv) = adam_small(
        [g1, g2, conv_w], [g_pre, g_post, g_conv],
        [m_norm_pre_g.reshape(1, d_model), m_norm_post_g.reshape(1, d_model), m_conv_w],
        [v_norm_pre_g.reshape(1, d_model), v_norm_post_g.reshape(1, d_model), v_conv_w])

    loss = 0.5 * rsmall[2 + CONV_K, 0] / d_model
    vec = lambda a: a.reshape(d_model)
    return (loss, grad_x.reshape(1, seq, d_model),
            vec(g_pre), gw_in, g_conv, gw_out, vec(g_post),
            vec(d_pre), d_in, d_cv, d_out, vec(d_post),
            vec(m_pre), m_in, m_cv, m_out, vec(m_post),
            vec(v_pre), v_in, v_cv, v_out, vec(v_post))
```

```python
import jax
import jax.numpy as jnp
from jax import lax
from jax.experimental import pallas as pl
from jax.experimental.pallas import tpu as pltpu

HEAD_DIM = 64
LANES = 128
SUBLANES = 8
BLOCK = 128
WINDOW_KEYS = 128
PERM = 16
PJ = 4
P4_ROWS = BLOCK // PJ
MAX_QUERY_BLOCKS = 8
ROW_TILE = 512
DZ_ROW_TILE = 512
CONV_K = 3
ROPE_THETA = 10000.0
NORM_EPS = 1e-6
ATTN_SCALE = HEAD_DIM ** -0.5
NEG = -1e30
N_CHIPS = 4
N_DEV = 8
MESH = pl.DeviceIdType.MESH
ADAM_LR = 0.001
ADAM_B1 = 0.9
ADAM_B2 = 0.999
ADAM_EPS = 1e-08
ADAM_WD = 0.01
ADAM_STEP = 10
VMEM_LIMIT = 52 * 1024 * 1024

F32 = jnp.float32
BF16 = jnp.bfloat16


def _params(sem=None, **kw):
    return pltpu.CompilerParams(dimension_semantics=sem, vmem_limit_bytes=VMEM_LIMIT, **kw)


def _const_spec(shape):
    return pl.BlockSpec(shape, lambda *_: (0,) * len(shape), pipeline_mode=pl.Buffered(1))


def _sigmoid(z):
    return 1.0 / (1.0 + jnp.exp(-z))


def _rowgroup_sum(a):
    rows, n = a.shape
    return a.reshape(rows // SUBLANES, SUBLANES, n).sum(axis=0)


def _nt(a, b):
    return lax.dot_general(a, b, (((1,), (1,)), ((), ())), preferred_element_type=F32)


def _tn(a, b):
    return lax.dot_general(a, b, (((0,), (0,)), ((), ())), preferred_element_type=F32)


def _col_pieces(a, b, width):
    out = []
    while a < b:
        j = a // width
        e = min(b, (j + 1) * width)
        out.append((j, a - j * width, e - j * width))
        a = e
    return out


def _lane_groups(width):
    return [slice(g * LANES, (g + 1) * LANES) for g in range(width // LANES)]


def _perm_shape(seq, width):
    return (PJ, PJ, seq // PERM, width)


def _perm_tile_spec(width, tm):
    return pl.BlockSpec((PJ, PJ, tm // PERM, width), lambda i: (0, 0, i, 0))


STAGE_PITCH = 24


def _stage_shape(groups, rows):
    return (groups, rows // PERM * STAGE_PITCH, LANES)


def _stage_put(stage, g, val, row0=0):
    for a in range(val.shape[0] // PERM):
        at = (row0 // PERM + a) * STAGE_PITCH
        stage[g, at:at + PERM, :] = val[a * PERM:(a + 1) * PERM]


def _stage_get(stage, g):
    return jnp.concatenate([stage[g, a * STAGE_PITCH:a * STAGE_PITCH + PERM, :]
                            for a in range(stage.shape[1] // STAGE_PITCH)], axis=0)


def _to_perm(stage, g, dst_ref, sl, dtype):
    rows = stage.shape[1] // STAGE_PITCH
    for b in range(PERM):
        dst_ref[b // PJ, b % PJ, :, sl] = stage[g, pl.ds(b, rows, stride=STAGE_PITCH), :].astype(dtype)


def _from_perm(src_ref, sl, stage, g):
    rows = stage.shape[1] // STAGE_PITCH
    for b in range(PERM):
        stage[g, pl.ds(b, rows, stride=STAGE_PITCH), :] = src_ref[b // PJ, b % PJ, :, sl].astype(F32)


def _flip(a, f):
    return 1 - a if f else a


def gather_weights(w_in, w_out, conv_w, seq):
    d_model, width = w_in.shape
    rows = w_out.shape[0]
    cw = jnp.zeros((SUBLANES, LANES), F32).at[:CONV_K, :conv_w.shape[1]].set(conv_w)
    half_dim = HEAD_DIM // 2
    inv_freq = ROPE_THETA ** (-jnp.arange(half_dim, dtype=F32) * 2.0 / HEAD_DIM)
    inv_freq = jnp.tile(inv_freq, LANES // half_dim).reshape(1, LANES)
    chunk = min(ROW_TILE, seq)

    def body(win_ref, wout_ref, cw_ref, freq_ref, winf_ref, woutf_ref, cwf_ref, cos_ref, s1_ref, s2_ref,
             st_in, st_out, near_send, near_recv, far_send, far_recv, cw_send, cw_recv, d2d_send, d2d_recv):
        x, y, c = lax.axis_index("x"), lax.axis_index("y"), lax.axis_index("c")
        me = 2 * x + y
        sib = (x, y, 1 - c)
        st_in[...] = win_ref[...].astype(BF16)
        st_out[...] = wout_ref[...].astype(BF16)
        winf_ref[me] = st_in[...]
        woutf_ref[me] = st_out[...]
        cwf_ref[me] = cw_ref[...]
        stages = (st_in, st_out)
        fulls = (winf_ref, woutf_ref)
        halves = (d_model // 2, rows // 2)

        def part(t, core, q=None):
            size = halves[t] if q is None else halves[t] // 2
            start = core * halves[t] if q is None else core * halves[t] + q * size
            return pl.ds(pl.multiple_of(start, size), size)

        near = [(1 - x, y), (x, 1 - y)]
        far = (1 - x, 1 - y)
        chip = lambda px, py: 2 * px + py

        def direct(k, t, q, slot, to):
            src = stages[t].at[part(t, c, q)]
            return pltpu.make_async_remote_copy(src_ref=src, dst_ref=fulls[t].at[slot, part(t, c, q)], send_sem=near_send.at[k, t, q],
                                                recv_sem=near_recv.at[k, t, q], device_id=to, device_id_type=MESH)

        def passed_on(k, t, slot, to):
            ref = fulls[t].at[slot, part(t, c, k)]
            return pltpu.make_async_remote_copy(src_ref=ref, dst_ref=ref, send_sem=far_send.at[k, t], recv_sem=far_recv.at[k, t],
                                                device_id=to, device_id_type=MESH)

        def conv_copy(k, slot, to):
            return pltpu.make_async_remote_copy(src_ref=cw_ref, dst_ref=cwf_ref.at[slot], send_sem=cw_send.at[k], recv_sem=cw_recv.at[k],
                                                device_id=to, device_id_type=MESH)

        def d2d(k, t, slot, core):
            ref = fulls[t].at[slot, part(t, core)]
            return pltpu.make_async_remote_copy(src_ref=ref, dst_ref=ref, send_sem=d2d_send.at[k, t], recv_sem=d2d_recv.at[k, t],
                                                device_id=sib, device_id_type=MESH)

        sends = []

        def go(cp):
            cp.start()
            sends.append(cp)

        for q_first in (0, 1):
            for k, (px, py) in enumerate(near):
                for t in range(2):
                    go(direct(k, t, k if q_first == 0 else 1 - k, me, (px, py, c)))
        for k, (px, py) in enumerate(near + [far]):
            go(conv_copy(k, me, (px, py, c)))
        for k, (px, py) in enumerate(near):
            other = near[1 - k]
            for t in range(2):
                direct(k, t, k, chip(px, py), (px, py, c)).wait_recv()
                go(passed_on(k, t, chip(px, py), (*other, c)))

        first_half = lax.broadcasted_iota(jnp.int32, (chunk, LANES), 1) % HEAD_DIM < half_dim
        row = lax.broadcasted_iota(jnp.int32, (chunk, LANES), 0)

        def table_rows(i, carry):
            at = pl.multiple_of(i * chunk, chunk)
            ang = (row + at).astype(F32) * freq_ref[...]
            sin = jnp.sin(ang)
            cos_ref[pl.ds(at, chunk), :] = jnp.cos(ang)
            s1_ref[pl.ds(at, chunk), :] = jnp.where(first_half, -sin, 0.0)
            s2_ref[pl.ds(at, chunk), :] = jnp.where(first_half, 0.0, sin)
            return carry

        lax.fori_loop(0, seq // chunk, table_rows, 0)

        for k, (px, py) in enumerate(near):
            for t in range(2):
                direct(k, t, 1 - k, chip(px, py), (px, py, c)).wait_recv()
                go(d2d(k, t, chip(px, py), c))
        for t in range(2):
            for k, (px, py) in enumerate(near):
                passed_on(k, t, chip(*far), (px, py, c)).wait_recv()
            go(d2d(2, t, chip(*far), c))
        for k, (px, py) in enumerate(near + [far]):
            conv_copy(k, chip(px, py), (px, py, c)).wait_recv()
            for t in range(2):
                d2d(k, t, chip(px, py), 1 - c).wait_recv()
        for cp in sends:
            cp.wait_send()

    vm = pl.BlockSpec(memory_space=pltpu.VMEM)
    dma = pltpu.SemaphoreType.DMA
    return pl.pallas_call(
        body, name="gather_weights",
        out_shape=(jax.ShapeDtypeStruct((N_CHIPS, d_model, width), BF16),
                   jax.ShapeDtypeStruct((N_CHIPS, rows, d_model), BF16),
                   jax.ShapeDtypeStruct((N_CHIPS, SUBLANES, LANES), F32),
                   *[jax.ShapeDtypeStruct((seq, LANES), F32)] * 3),
        in_specs=[vm, vm, vm, vm], out_specs=(vm,) * 6,
        scratch_shapes=[pltpu.VMEM((d_model, width), BF16), pltpu.VMEM((rows, d_model), BF16),
                        dma((2, 2, 2)), dma((2, 2, 2)), dma((2, 2)), dma((2, 2)), dma((3,)), dma((3,)),
                        dma((3, 2)), dma((3, 2))],
        compiler_params=_params(),
    )(w_in, w_out, cw, inv_freq)


def _rope(t, cos, s1, s2):
    return t * cos + pltpu.roll(t, LANES - HEAD_DIM // 2, 1) * s1 + pltpu.roll(t, HEAD_DIM // 2, 1) * s2


def _rope_transposed(g, cos, s1, s2):
    return g * cos + pltpu.roll(g * s1, HEAD_DIM // 2, 1) + pltpu.roll(g * s2, LANES - HEAD_DIM // 2, 1)


def inproj(x, g1, w_full, tables, attn_w, conv_w):
    seq, d_model = x.shape
    width = w_full.shape[2]
    tm = ROW_TILE
    groups = _lane_groups(attn_w)

    def body(x_ref, g_ref, w_ref, cos_ref, s1_ref, s2_ref,
             ht_ref, q_ref, k_ref, v_ref, qp_ref, kp_ref, vp_ref, ga_ref, cz_ref, stage):
        xv = x_ref[...]
        hb = ((xv * lax.rsqrt(jnp.mean(xv * xv, axis=-1, keepdims=True) + NORM_EPS)) * g_ref[...]).astype(BF16)
        ht_ref[...] = jnp.transpose(hb)
        cos, s1, s2 = cos_ref[...], s1_ref[...], s2_ref[...]

        def proj(a, b):
            parts = [jnp.dot(hb, w_ref[j, :, lo:hi], preferred_element_type=F32) for j, lo, hi in _col_pieces(a, b, width)]
            return parts[0] if len(parts) == 1 else jnp.concatenate(parts, axis=1)

        def emit(z, nat_ref, perm_ref, fn):
            for g, sl in enumerate(groups):
                val = fn(z[:, sl])
                nat_ref[:, sl] = val.astype(BF16)
                _stage_put(stage, g, val)
            for g, sl in enumerate(groups):
                _to_perm(stage, g, perm_ref, sl, BF16)

        emit(proj(0, attn_w), q_ref, qp_ref, lambda t: _rope(t, cos, s1, s2) * ATTN_SCALE)
        emit(proj(attn_w, 2 * attn_w), k_ref, kp_ref, lambda t: _rope(t, cos, s1, s2))
        emit(proj(2 * attn_w, 3 * attn_w), v_ref, vp_ref, lambda t: t)
        ga_ref[...] = proj(3 * attn_w, 4 * attn_w)
        cz_ref[...] = proj(4 * attn_w, 4 * attn_w + 4 * conv_w)

    row = lambda n: pl.BlockSpec((tm, n), lambda i: (i, 0))
    nat = jax.ShapeDtypeStruct((seq, attn_w), BF16)
    perm = jax.ShapeDtypeStruct(_perm_shape(seq, attn_w), BF16)
    return pl.pallas_call(
        body, name="inproj", grid=(seq // tm,),
        out_shape=(jax.ShapeDtypeStruct((d_model, seq), BF16), nat, nat, nat, perm, perm, perm,
                   jax.ShapeDtypeStruct((seq, attn_w), F32), jax.ShapeDtypeStruct((seq, 4 * conv_w), F32)),
        in_specs=[row(d_model), _const_spec((1, d_model)), _const_spec(w_full.shape), row(LANES), row(LANES), row(LANES)],
        out_specs=(pl.BlockSpec((d_model, tm), lambda i: (0, i)), row(attn_w), row(attn_w), row(attn_w),
                   _perm_tile_spec(attn_w, tm), _perm_tile_spec(attn_w, tm), _perm_tile_spec(attn_w, tm),
                   row(attn_w), row(4 * conv_w)),
        scratch_shapes=[pltpu.VMEM(_stage_shape(len(groups), tm), F32)],
        compiler_params=_params(("arbitrary",)),
    )(x, g1, w_full, *tables)


class _Mode:
    def __init__(self, name, seq):
        self.name = name
        if name == "nat":
            self.residues, blocks = 1, seq // BLOCK
        elif name == "p16":
            self.residues, blocks = PERM, seq // PERM // BLOCK
        else:
            self.residues, blocks = PJ, seq // PERM // P4_ROWS
        self.qb = max(d for d in range(1, MAX_QUERY_BLOCKS + 1) if blocks % d == 0)
        self.steps = blocks // self.qb

    def _spec(self, blocks, width, at):
        if self.name == "nat":
            return pl.BlockSpec((blocks * BLOCK, width), lambda *g: (at(*g)[1], 0))
        if self.name == "p16":
            return pl.BlockSpec((1, 1, blocks * BLOCK, width), lambda *g: (at(*g)[0] // PJ, at(*g)[0] % PJ, at(*g)[1], 0))
        return pl.BlockSpec((PJ, 1, blocks * P4_ROWS, width), lambda *g: (0, at(*g)[0], at(*g)[1], 0))

    def wide(self, width, where=lambda r, n: (r, n)):
        return self._spec(self.qb, width, where)

    def block_before(self, width, where=lambda r, n: (r, n)):
        return self._spec(1, width, lambda *g: (where(*g)[0], jnp.maximum(self.qb * where(*g)[1] - 1, 0)))

    def get(self, ref, sl, sub=0):
        if self.name == "nat":
            return ref[sub * BLOCK:(sub + 1) * BLOCK, sl]
        if self.name == "p16":
            return ref[0, 0, sub * BLOCK:(sub + 1) * BLOCK, sl]
        return jnp.concatenate([ref[j, 0, sub * P4_ROWS:(sub + 1) * P4_ROWS, sl] for j in range(PJ)], axis=0)

    def put(self, ref, sl, val, sub=0):
        val = val.astype(ref.dtype)
        if self.name == "nat":
            ref[sub * BLOCK:(sub + 1) * BLOCK, sl] = val
        elif self.name == "p16":
            ref[0, 0, sub * BLOCK:(sub + 1) * BLOCK, sl] = val
        else:
            for j in range(PJ):
                ref[j, 0, sub * P4_ROWS:(sub + 1) * P4_ROWS, sl] = val[j * P4_ROWS:(j + 1) * P4_ROWS]

    def keys(self, before_ref, wide_ref, sl, sub):
        older = self.get(before_ref, sl) if sub == 0 else self.get(wide_ref, sl, sub - 1)
        return jnp.concatenate([older, self.get(wide_ref, sl, sub)], axis=0)

    def index(self, idx, is_key):
        if self.name != "p4":
            return idx - BLOCK if is_key else idx
        within = jnp.bitwise_and(idx, BLOCK - 1)
        m = PJ * jnp.bitwise_and(within, P4_ROWS - 1) + jnp.right_shift(within, P4_ROWS.bit_length() - 1)
        return m + BLOCK * (jnp.right_shift(idx, BLOCK.bit_length() - 1) - 1) if is_key else m

    def bias(self, has_before):
        shape = (2 * BLOCK, BLOCK)
        kidx = lax.broadcasted_iota(jnp.int32, shape, 0)
        qidx = lax.broadcasted_iota(jnp.int32, shape, 1)
        rel = self.index(qidx, False) - self.index(kidx, True)
        valid = (rel >= 0) & (rel <= WINDOW_KEYS)
        if has_before is not True:
            valid = valid & ((kidx >= BLOCK) | has_before)
        one = jnp.where(valid, 0.0, NEG)
        return jnp.concatenate([one, one], axis=1)


def _head_masks():
    lane = lax.broadcasted_iota(jnp.int32, (BLOCK, LANES), 1)
    lo = lane < HEAD_DIM
    return lane, lo, jnp.where(lo, 1.0, 0.0).astype(BF16), jnp.where(lo, 0.0, 1.0).astype(BF16)


def attn_fwd(name, q, k, v, run):
    nat = name == "nat"
    seq = q.shape[0] if nat else q.shape[2] * PERM
    attn_w = q.shape[-1]
    mode = _Mode(name, seq)
    groups = _lane_groups(attn_w)
    first = run is None
    all_lanes = slice(0, LANES)

    def body(*refs):
        q_ref, kp_ref, kc_ref, vp_ref, vc_ref = refs[:5]
        if first:
            o_ref, l_ref = refs[5:]
        elif nat:
            oin_ref, lin_ref, o_ref, l_ref, ostage, lstage = refs[5:]
        else:
            oin_ref, lin_ref, o_ref, l_ref = refs[5:]
        n = pl.program_id(1)
        subs = range(mode.qb)
        biases = [mode.bias(n > 0)] + [mode.bias(True)] * (mode.qb - 1)
        _, lo, m_lo, m_hi = _head_masks()
        head_row = lax.broadcasted_iota(jnp.int32, (BLOCK, LANES), 0)
        ones = jnp.ones((2 * BLOCK, LANES), BF16)
        lrows = [jnp.zeros((BLOCK, LANES), F32) for _ in subs]
        if not first:
            if nat:
                for g, sl in enumerate(groups):
                    _from_perm(oin_ref, sl, ostage, g)
                _from_perm(lin_ref, all_lanes, lstage, 0)
            wide_rows = lambda a, sub: a[sub * BLOCK:(sub + 1) * BLOCK]
            before = [jnp.transpose(wide_rows(_stage_get(lstage, 0), sub) if nat else mode.get(lin_ref, all_lanes, sub))
                      for sub in subs]

        def probs(sub, p, sl):
            q2 = mode.get(q_ref, sl, sub)
            kcat = mode.keys(kp_ref, kc_ref, sl, sub)
            vcat = mode.keys(vp_ref, vc_ref, sl, sub)
            qq = jnp.concatenate([q2 * m_lo, q2 * m_hi], axis=0)
            s_t = _nt(kcat, qq) + biases[sub]
            m = jnp.max(s_t, axis=0, keepdims=True)
            pe = jnp.exp(s_t - m)
            lse = m + jnp.log(jnp.sum(pe, axis=0, keepdims=True))
            if not first:
                was = jnp.concatenate([before[sub][2 * p:2 * p + 1, :], before[sub][2 * p + 1:2 * p + 2, :]], axis=1)
                top = jnp.maximum(was, lse)
                lse = top + jnp.log(jnp.exp(was - top) + jnp.exp(lse - top))
                pe = pe * jnp.exp(m - lse)
            return jnp.concatenate([vcat, ones], axis=1), pe.astype(BF16), lse

        def output(sub, p, sl, vext, pb, lse):
            o_ext = _tn(pb, vext)
            if first:
                o_new = o_ext[:, :LANES] / o_ext[:, LANES:]
            else:
                o_prev = wide_rows(_stage_get(ostage, p), sub) if nat else mode.get(oin_ref, sl, sub)
                o_new = o_ext[:, :LANES] + jnp.concatenate([o_prev, o_prev], axis=0) * (1.0 - o_ext[:, LANES:])
            mode.put(o_ref, sl, jnp.where(lo, o_new[:BLOCK], o_new[BLOCK:]), sub)
            rows = jnp.where(head_row == 2 * p, lse[:, :BLOCK], lrows[sub])
            lrows[sub] = jnp.where(head_row == 2 * p + 1, lse[:, BLOCK:], rows)

        pending = None
        for sub in subs:
            for p, sl in enumerate(groups):
                nxt = probs(sub, p, sl)
                if pending is not None:
                    output(*pending)
                pending = (sub, p, sl, *nxt)
        output(*pending)
        for sub in subs:
            mode.put(l_ref, all_lanes, jnp.transpose(lrows[sub]), sub)

    ins = [q, k, k, v, v]
    specs = [mode.wide(attn_w), mode.block_before(attn_w), mode.wide(attn_w), mode.block_before(attn_w), mode.wide(attn_w)]
    scratch = []
    if not first:
        ins += list(run)
        if nat:
            rows_a = mode.qb * BLOCK // PERM
            specs += [pl.BlockSpec((PJ, PJ, rows_a, attn_w), lambda r, n: (0, 0, n, 0)),
                      pl.BlockSpec((PJ, PJ, rows_a, LANES), lambda r, n: (0, 0, n, 0))]
            scratch = [pltpu.VMEM(_stage_shape(len(groups), mode.qb * BLOCK), F32),
                       pltpu.VMEM(_stage_shape(1, mode.qb * BLOCK), F32)]
        else:
            specs += [mode.wide(attn_w), mode.wide(LANES)]
    if nat:
        out_shape = (jax.ShapeDtypeStruct((seq, attn_w), F32), jax.ShapeDtypeStruct((seq, LANES), F32))
    else:
        out_shape = (jax.ShapeDtypeStruct(_perm_shape(seq, attn_w), F32), jax.ShapeDtypeStruct(_perm_shape(seq, LANES), F32))
    return pl.pallas_call(
        body, name=f"attn_fwd_{name}", grid=(mode.residues, mode.steps),
        out_shape=out_shape, in_specs=specs, out_specs=(mode.wide(attn_w), mode.wide(LANES)),
        scratch_shapes=scratch,
        compiler_params=_params(("arbitrary", "arbitrary")),
    )(*ins)


def attn_bwd(name, q, k, v, d_o, lse, delta, run):
    nat = name == "nat"
    seq = q.shape[0] if nat else q.shape[2] * PERM
    attn_w = q.shape[-1]
    mode = _Mode(name, seq)
    steps, qb = mode.steps, mode.qb
    single = steps == 1
    groups = _lane_groups(attn_w)
    first = run is None
    all_lanes = slice(0, LANES)

    def body(*refs):
        q_ref, kp_ref, kc_ref, vp_ref, vc_ref, do_ref, lse_ref, dl_ref = refs[:8]
        if first:
            dq_ref, dk_ref, dv_ref, ck, cv = refs[8:]
        else:
            dqi_ref, dki_ref, dvi_ref, dq_ref, dk_ref, dv_ref, ck, cv = refs[8:]
        g = pl.program_id(1) if single else pl.program_id(0)
        n = g if single else lax.rem(g, steps)
        carries = ((ck, dk_ref, None if first else dki_ref), (cv, dv_ref, None if first else dvi_ref))

        def emit(out_ref, acc_ref, sl, sub, val):
            if acc_ref is not None:
                val = val + mode.get(acc_ref, sl, sub).astype(F32)
            mode.put(out_ref, sl, val, sub)

        if not single:
            @pl.when(g == 0)
            def _():
                ck[...] = jnp.zeros_like(ck)
                cv[...] = jnp.zeros_like(cv)

        @pl.when(g < total)
        def _():
            biases = [mode.bias(n > 0)] + [mode.bias(True)] * (qb - 1)
            _, lo, m_lo, m_hi = _head_masks()

            def scores(sub, p, sl, lse_t, dl_t):
                q2, do2 = mode.get(q_ref, sl, sub), mode.get(do_ref, sl, sub)
                kcat = mode.keys(kp_ref, kc_ref, sl, sub)
                vcat = mode.keys(vp_ref, vc_ref, sl, sub)
                qq = jnp.concatenate([q2 * m_lo, q2 * m_hi], axis=0)
                dd = jnp.concatenate([do2 * m_lo, do2 * m_hi], axis=0)
                h0 = 2 * p
                lse2 = jnp.concatenate([lse_t[h0:h0 + 1, :], lse_t[h0 + 1:h0 + 2, :]], axis=1)
                dl2 = jnp.concatenate([dl_t[h0:h0 + 1, :], dl_t[h0 + 1:h0 + 2, :]], axis=1)
                p_t = jnp.exp(_nt(kcat, qq) + (biases[sub] - lse2))
                ds_t = p_t * (_nt(vcat, dd) - dl2)
                return qq, dd, kcat, p_t.astype(BF16), ds_t.astype(BF16)

            def grads(sub, sl, qq, dd, kcat, pb, dsb):
                dqb = _tn(dsb, kcat)
                dq2 = jnp.where(lo, dqb[:BLOCK], dqb[BLOCK:]) * ATTN_SCALE
                if not first:
                    dq2 = dq2 + mode.get(dqi_ref, sl, sub).astype(F32)
                mode.put(dq_ref, sl, dq2, sub)
                for (carry, out_ref, acc_ref), lhs, rhs in zip(carries, (dsb, pb), (qq, dd)):
                    both = jnp.dot(lhs, rhs, preferred_element_type=F32)
                    if sub == 0:
                        if not single:
                            for s in range(qb - 1):
                                emit(out_ref, acc_ref, sl, s, carry[s, :, sl])
                            emit(out_ref, acc_ref, sl, qb - 1, carry[qb - 1, :, sl] + both[:BLOCK])
                        carry[0, :, sl] = both[BLOCK:]
                    else:
                        carry[sub - 1, :, sl] += both[:BLOCK]
                        carry[sub, :, sl] = both[BLOCK:]
                    if single and sub == qb - 1:
                        for s in range(qb):
                            emit(out_ref, acc_ref, sl, s, carry[s, :, sl])

            stats = [(jnp.transpose(mode.get(lse_ref, all_lanes, sub)),
                      jnp.transpose(mode.get(dl_ref, all_lanes, sub))) for sub in range(qb)]
            pending = None
            for p, sl in enumerate(groups):
                for sub in range(qb):
                    nxt = scores(sub, p, sl, *stats[sub])
                    if pending is not None:
                        grads(*pending)
                    pending = (sub, sl, *nxt)
            grads(*pending)

        if not single:
            @pl.when(g == total)
            def _():
                for carry, out_ref, acc_ref in carries:
                    for sl in groups:
                        for s in range(qb):
                            emit(out_ref, acc_ref, sl, s, carry[s, :, sl])

    total = mode.residues * steps
    if single:
        here = before = lambda r, n: (r, n)
    else:
        locate = lambda g: (g // steps, lax.rem(g, steps))
        here = lambda g: locate(jnp.minimum(g, total - 1))
        before = lambda g: locate(jnp.maximum(g - 1, 0))
    wide = lambda w: mode.wide(w, here)
    ins = [q, k, k, v, v, d_o, lse, delta]
    specs = [wide(attn_w), mode.block_before(attn_w, here), wide(attn_w), mode.block_before(attn_w, here), wide(attn_w),
             wide(attn_w), wide(LANES), wide(LANES)]
    if not first:
        ins += list(run)
        specs += [wide(attn_w), mode.wide(attn_w, before), mode.wide(attn_w, before)]
    shp = jax.ShapeDtypeStruct((seq, attn_w) if nat else _perm_shape(seq, attn_w), BF16)
    grid = (mode.residues, 1) if single else (total + 1,)
    return pl.pallas_call(
        body, name=f"attn_bwd_{name}", grid=grid,
        out_shape=(shp, shp, shp), in_specs=specs,
        out_specs=(wide(attn_w), mode.wide(attn_w, before), mode.wide(attn_w, before)),
        scratch_shapes=[pltpu.VMEM((qb, BLOCK, attn_w), F32), pltpu.VMEM((qb, BLOCK, attn_w), F32)],
        compiler_params=_params(("arbitrary",) * len(grid)),
    )(*ins)


def _shift_down(u, halo, k):
    rolled = pltpu.roll(u, k, 0)
    row = lax.broadcasted_iota(jnp.int32, halo.shape, 0)
    top = jnp.where(row < k, pltpu.roll(halo, k, 0), rolled[:SUBLANES])
    return jnp.concatenate([top, rolled[SUBLANES:]], axis=0)


def _shift_up(u, halo, k):
    rows = u.shape[0]
    rolled = pltpu.roll(u, rows - k, 0)
    row = lax.broadcasted_iota(jnp.int32, halo.shape, 0)
    bot = jnp.where(row >= SUBLANES - k, pltpu.roll(halo, SUBLANES - k, 0), rolled[rows - SUBLANES:])
    return jnp.concatenate([rolled[:rows - SUBLANES], bot], axis=0)


def tail(o, lse, ga, cz, x, tgt, w_out, g2, cw):
    seq, d_model = x.shape
    attn_w = o.shape[1]
    conv_w = cz.shape[1] // 4
    mix = attn_w + conv_w
    groups = _lane_groups(attn_w)
    tm = ROW_TILE
    nt = seq // tm
    hb = tm // SUBLANES

    def body(o_ref, l_ref, ga_ref, cz_ref, hz_ref, x_ref, t_ref, w_ref, g_ref, cw_ref,
             do_ref, dl_ref, dop_ref, dlp_ref, lp_ref, dga_ref, dcb_ref, dgc_ref, dcv_ref, e_ref,
             dw_ref, dg_ref, dcw_ref, loss_ref, stage):
        i = pl.program_id(0)

        @pl.when(i == 0)
        def _():
            dw_ref[...] = jnp.zeros_like(dw_ref)
            dg_ref[...] = jnp.zeros_like(dg_ref)
            dcw_ref[...] = jnp.zeros_like(dcw_ref)
            loss_ref[...] = jnp.zeros_like(loss_ref)

        u = cz_ref[:, 2 * conv_w:3 * conv_w] * cz_ref[:, 0:conv_w]
        uh = hz_ref[:, 2 * conv_w:3 * conv_w] * hz_ref[:, 0:conv_w]
        uh = jnp.where(i > 0, uh, 0.0)
        u1 = _shift_down(u, uh, 1)
        u2 = _shift_down(u, uh, 2)
        w0, w1, w2 = cw_ref[0:1, :], cw_ref[1:2, :], cw_ref[2:3, :]
        cvv = u2 * w0 + u1 * w1 + u * w2
        gv = g_ref[...]
        all_lanes = slice(0, LANES)

        def forward(rs):
            ov, gav = o_ref[rs, :], ga_ref[rs, :]
            sig_a = _sigmoid(gav)
            silu_a = gav * sig_a
            cb, gc = cz_ref[rs, conv_w:2 * conv_w], cz_ref[rs, 3 * conv_w:4 * conv_w]
            sig_c = _sigmoid(gc)
            silu_c = gc * sig_c
            bc = cb * cvv[rs]
            mixed = jnp.concatenate([ov * silu_a, bc * silu_c], axis=1).astype(BF16)
            yv = jnp.dot(mixed, w_ref[...], preferred_element_type=F32)
            return ov, gav, sig_a, silu_a, cb, gc, sig_c, silu_c, bc, mixed, yv

        def loss_and_dy(rs, mixed, yv):
            r2 = lax.rsqrt(jnp.mean(yv * yv, axis=-1, keepdims=True) + NORM_EPS)
            yhat = yv * r2
            diff = (x_ref[rs, :] + yhat * gv) - t_ref[rs, :]
            loss_ref[...] += _rowgroup_sum(diff * diff)
            ev = diff * (1.0 / d_model)
            e_ref[rs, :] = ev
            dg_ref[...] += _rowgroup_sum(ev * yhat)
            eg = ev * gv
            dy = (r2 * (eg - yhat * jnp.mean(eg * yhat, axis=-1, keepdims=True))).astype(BF16)
            dw_ref[...] += _tn(mixed, dy)
            return _nt(dy, w_ref[...])

        def backward(rs, ov, gav, sig_a, silu_a, cb, gc, sig_c, silu_c, bc, dm):
            rows = rs.stop - rs.start
            dma, dmc = dm[:, :attn_w], dm[:, attn_w:]
            dov = dma * silu_a
            do_ref[rs, :] = dov.astype(BF16)
            dga_ref[rs, :] = (dma * ov * (sig_a * (1.0 + gav * (1.0 - sig_a)))).astype(BF16)
            prod = dov * ov
            lane = lax.broadcasted_iota(jnp.int32, (rows, LANES), 1)
            lo = lane < HEAD_DIM
            dblk = jnp.zeros((rows, LANES), F32)
            for p, sl in enumerate(groups):
                pr = prod[:, sl]
                dblk = jnp.where(lane == 2 * p, jnp.sum(jnp.where(lo, pr, 0.0), axis=1, keepdims=True), dblk)
                dblk = jnp.where(lane == 2 * p + 1, jnp.sum(jnp.where(lo, 0.0, pr), axis=1, keepdims=True), dblk)
                _stage_put(stage, p, dov[:, sl], rs.start)
            dl_ref[rs, :] = dblk
            _stage_put(stage, len(groups), dblk, rs.start)
            _stage_put(stage, len(groups) + 1, l_ref[rs, :], rs.start)
            dsc = dmc * silu_c
            cv_rows = cvv[rs]
            dcb_ref[rs, :] = (dsc * cv_rows).astype(BF16)
            dgc_ref[rs, :] = (dmc * bc * (sig_c * (1.0 + gc * (1.0 - sig_c)))).astype(BF16)
            dcv = dsc * cb
            dcv_ref[rs, :] = dcv
            dcw_ref[0:SUBLANES, :] += _rowgroup_sum(dcv * u2[rs])
            dcw_ref[SUBLANES:2 * SUBLANES, :] += _rowgroup_sum(dcv * u1[rs])
            dcw_ref[2 * SUBLANES:3 * SUBLANES, :] += _rowgroup_sum(dcv * u[rs])

        halves = [slice(0, tm // 2), slice(tm // 2, tm)]
        fwd = [forward(rs) for rs in halves]
        dms = [loss_and_dy(rs, f[9], f[10]) for rs, f in zip(halves, fwd)]
        for rs, f, dm in zip(halves, fwd, dms):
            backward(rs, *f[:9], dm)
        for p, sl in enumerate(groups):
            _to_perm(stage, p, dop_ref, sl, BF16)
        _to_perm(stage, len(groups), dlp_ref, all_lanes, F32)
        _to_perm(stage, len(groups) + 1, lp_ref, all_lanes, F32)

    row = lambda n: pl.BlockSpec((tm, n), lambda i: (i, 0))
    whole = lambda a, b: pl.BlockSpec((a, b), lambda i: (0, 0))
    return pl.pallas_call(
        body, name="tail", grid=(nt,),
        out_shape=(jax.ShapeDtypeStruct((seq, attn_w), BF16), jax.ShapeDtypeStruct((seq, LANES), F32),
                   jax.ShapeDtypeStruct(_perm_shape(seq, attn_w), BF16), jax.ShapeDtypeStruct(_perm_shape(seq, LANES), F32),
                   jax.ShapeDtypeStruct(_perm_shape(seq, LANES), F32),
                   jax.ShapeDtypeStruct((seq, attn_w), BF16), jax.ShapeDtypeStruct((seq, conv_w), BF16),
                   jax.ShapeDtypeStruct((seq, conv_w), BF16), jax.ShapeDtypeStruct((seq, conv_w), F32),
                   jax.ShapeDtypeStruct((seq, d_model), F32), jax.ShapeDtypeStruct((mix, d_model), F32),
                   jax.ShapeDtypeStruct((SUBLANES, d_model), F32), jax.ShapeDtypeStruct((CONV_K * SUBLANES, conv_w), F32),
                   jax.ShapeDtypeStruct((SUBLANES, d_model), F32)),
        in_specs=[row(attn_w), row(LANES), row(attn_w), row(4 * conv_w),
                  pl.BlockSpec((SUBLANES, 4 * conv_w), lambda i: (jnp.maximum(i * hb - 1, 0), 0)),
                  row(d_model), row(d_model), _const_spec((mix, d_model)), _const_spec((1, d_model)),
                  _const_spec((SUBLANES, conv_w))],
        out_specs=(row(attn_w), row(LANES), _perm_tile_spec(attn_w, tm), _perm_tile_spec(LANES, tm), _perm_tile_spec(LANES, tm),
                   row(attn_w), row(conv_w), row(conv_w), row(conv_w), row(d_model),
                   whole(mix, d_model), whole(SUBLANES, d_model), whole(CONV_K * SUBLANES, conv_w),
                   whole(SUBLANES, d_model)),
        scratch_shapes=[pltpu.VMEM(_stage_shape(len(groups) + 2, tm), F32)],
        compiler_params=_params(("arbitrary",)),
    )(o, lse, ga, cz, cz, x, tgt, w_out, g2, cw)


def dz_dx(nat_grads, perm_grads, dga, dcb, dgc, dcv, cz, tables, x, g1, e, w_full, cw):
    seq, d_model = x.shape
    attn_w = dga.shape[1]
    conv_w = dcv.shape[1]
    width = w_full.shape[2]
    in_w = 4 * attn_w + 4 * conv_w
    groups = _lane_groups(attn_w)
    tm = DZ_ROW_TILE
    nt = seq // tm
    hb = tm // SUBLANES

    def body(dq_ref, dk_ref, dv_ref, dqp_ref, dkp_ref, dvp_ref, dga_ref, dcb_ref, dgc_ref, dcv_ref, nh_ref, ch_ref, cc_ref,
             cos_ref, s1_ref, s2_ref, x_ref, g_ref, e_ref, w_ref, cw_ref, gx_ref, dz_ref, dg_ref, stage):
        i = pl.program_id(0)

        @pl.when(i == 0)
        def _():
            dg_ref[...] = jnp.zeros_like(dg_ref)

        cos, s1, s2 = cos_ref[...], s1_ref[...], s2_ref[...]

        def qkv_columns(t, nat_ref, perm_ref):
            for g, sl in enumerate(groups):
                _from_perm(perm_ref, sl, stage, g)
            for g, sl in enumerate(groups):
                tot = nat_ref[:, sl].astype(F32) + _stage_get(stage, g)
                if t < 2:
                    tot = _rope_transposed(tot, cos, s1, s2)
                dz_ref[:, t * attn_w + g * LANES:t * attn_w + (g + 1) * LANES] = tot.astype(BF16)

        def dh_part(j):
            return _nt(dz_ref[:, j * width:(j + 1) * width], w_ref[j])

        dcv = dcv_ref[...]
        nh = jnp.where(i < nt - 1, nh_ref[...], 0.0)
        w0, w1, w2 = cw_ref[0:1, :], cw_ref[1:2, :], cw_ref[2:3, :]
        du = dcv * w2 + _shift_up(dcv, nh, 1) * w1 + _shift_up(dcv, nh, 2) * w0
        base = 4 * attn_w
        dz_ref[:, base:base + conv_w] = (du * cc_ref[...]).astype(BF16)
        dz_ref[:, base + conv_w:base + 2 * conv_w] = dcb_ref[...]
        dz_ref[:, base + 2 * conv_w:base + 3 * conv_w] = (du * ch_ref[...]).astype(BF16)
        dz_ref[:, base + 3 * conv_w:base + 4 * conv_w] = dgc_ref[...]
        dz_ref[:, 3 * attn_w:4 * attn_w] = dga_ref[...]
        ready = in_w
        dh = None
        for t, nat_ref, perm_ref in ((2, dv_ref, dvp_ref), (1, dk_ref, dkp_ref), (0, dq_ref, dqp_ref), (None, None, None)):
            lowest_open = 0 if t is None else (t + 1) * attn_w
            while ready - width >= lowest_open:
                ready -= width
                part = dh_part(ready // width)
                dh = part if dh is None else dh + part
            if t is not None:
                qkv_columns(t, nat_ref, perm_ref)
        xv = x_ref[...]
        r1 = lax.rsqrt(jnp.mean(xv * xv, axis=-1, keepdims=True) + NORM_EPS)
        xhat = xv * r1
        dg_ref[...] += _rowgroup_sum(dh * xhat)
        dhg = dh * g_ref[...]
        gx_ref[...] = r1 * (dhg - xhat * jnp.mean(dhg * xhat, axis=-1, keepdims=True)) + e_ref[...]

    row = lambda n: pl.BlockSpec((tm, n), lambda i: (i, 0))
    whole = lambda a, b: pl.BlockSpec((a, b), lambda i: (0, 0))
    pt = _perm_tile_spec(attn_w, tm)
    return pl.pallas_call(
        body, name="dz_dx", grid=(nt,),
        out_shape=(jax.ShapeDtypeStruct((seq, d_model), F32), jax.ShapeDtypeStruct((seq, in_w), BF16),
                   jax.ShapeDtypeStruct((SUBLANES, d_model), F32)),
        in_specs=[row(attn_w), row(attn_w), row(attn_w), pt, pt, pt, row(attn_w), row(conv_w), row(conv_w), row(conv_w),
                  pl.BlockSpec((SUBLANES, conv_w), lambda i: (jnp.minimum((i + 1) * hb, seq // SUBLANES - 1), 0)),
                  pl.BlockSpec((tm, conv_w), lambda i: (i, 0)), pl.BlockSpec((tm, conv_w), lambda i: (i, 2)),
                  row(LANES), row(LANES), row(LANES), row(d_model), _const_spec((1, d_model)), row(d_model),
                  _const_spec(w_full.shape), _const_spec((SUBLANES, conv_w))],
        out_specs=(row(d_model), row(in_w), whole(SUBLANES, d_model)),
        scratch_shapes=[pltpu.VMEM(_stage_shape(len(groups), tm), F32)],
        compiler_params=_params(("arbitrary",)),
    )(*nat_grads, *perm_grads, dga, dcb, dgc, dcv, dcv, cz, cz, *tables, x, g1, e, w_full, cw)


def dw_in_reduce(ht, dz, g_out, small):
    d_model, seq = ht.shape
    half = dz.shape[1] // N_DEV
    ts = min(2048, seq)
    steps = seq // ts
    x, y, c = lax.axis_index("x"), lax.axis_index("y"), lax.axis_index("c")
    far_first = lambda x, y: [(1 - x, 1 - y), (1 - x, y), (x, 1 - y)]
    chips = jnp.stack([2 * px + py for px, py in far_first(x, y)] + [2 * x + y]).astype(jnp.int32)
    order = jnp.stack([2 * chips + (1 - c), 2 * chips + c], axis=1).reshape(N_DEV)

    def body(order_ref, ht_ref, dz_ref, go_ref, sm_ref, out_ref, ro_ref, rs_ref,
             acc, theirs, staged, contrib, resbuf, out_sem, sa, ra, sb, rb, sc, rc,
             o_mine, o_theirs, o_staged, o_contrib, o_res, sbuf, o_load, osa, ora, osb, orb, osc, orc, ss, rs):
        del order_ref
        p, s = pl.program_id(0), pl.program_id(1)
        x, y, c = lax.axis_index("x"), lax.axis_index("y"), lax.axis_index("c")
        me = 2 * x + y
        sib = (x, y, 1 - c)
        peers = far_first(x, y)
        slot = p % 2

        flips = [(fx, fy, fc) for fx in (0, 1) for fy in (0, 1) for fc in (0, 1)][1:]
        my8 = 4 * x + 2 * y + c
        chip_ids = [2 * px + py for px, py in peers] + [me]

        def small_copy(k, slot8, to):
            return pltpu.make_async_remote_copy(src_ref=sm_ref, dst_ref=sbuf.at[slot8], send_sem=ss.at[k], recv_sem=rs.at[k],
                                                device_id=to, device_id_type=MESH)

        def small_peer(k):
            fx, fy, fc = flips[k]
            return _flip(x, fx), _flip(y, fy), _flip(c, fc)

        def oa_copy(pos):
            j = chip_ids[pos]
            return pltpu.make_async_remote_copy(src_ref=go_ref.at[j, 1 - c], dst_ref=o_theirs.at[j], send_sem=osa.at[pos],
                                                recv_sem=ora.at[pos], device_id=sib, device_id_type=MESH)

        def o_load_copy(pos):
            j = chip_ids[pos]
            return pltpu.make_async_copy(go_ref.at[j, c], o_mine.at[j], o_load.at[pos])

        def ob_copy(k, piece, slot4):
            px, py = peers[k]
            return pltpu.make_async_remote_copy(src_ref=o_staged.at[piece], dst_ref=o_contrib.at[slot4], send_sem=osb.at[k],
                                                recv_sem=orb.at[k], device_id=(px, py, c), device_id_type=MESH)

        def oc_copy(which):
            return pltpu.make_async_remote_copy(src_ref=o_res.at[which], dst_ref=o_res.at[which], send_sem=osc, recv_sem=orc,
                                                device_id=sib, device_id_type=MESH)

        @pl.when((p == 0) & (s == 0))
        def _():
            sbuf[my8] = sm_ref[...]
            for k in range(N_DEV - 1):
                small_copy(k, my8, small_peer(k)).start()
            for pos in range(N_CHIPS):
                o_load_copy(pos).start()
                oa_copy(pos).start()

        @pl.when((p == 1) & (s == steps - 1))
        def _():
            for pos in range(N_CHIPS):
                j = chip_ids[pos]
                o_load_copy(pos).wait()
                oa_copy(pos).wait_recv()
                if pos < N_CHIPS - 1:
                    o_staged[j] = (o_mine[j] + o_theirs[j]).astype(BF16)
                    ob_copy(pos, j, me).start()
                else:
                    o_mine[j] = o_mine[j] + o_theirs[j]
                    o_contrib[j] = o_mine[j].astype(BF16)

        @pl.when((p == 4) & (s == steps - 1))
        def _():
            for k in range(N_CHIPS - 1):
                ob_copy(k, me, chip_ids[k]).wait_recv()
            own = o_mine[me]
            term = lambda j: jnp.where(me == j, own, o_contrib[j].astype(F32))
            o_res[c] = ((term(0) + term(1)) + term(2)) + term(3)
            oc_copy(c).start()

        def a_copy(k):
            return pltpu.make_async_remote_copy(src_ref=acc.at[0], dst_ref=theirs.at[k], send_sem=sa.at[k], recv_sem=ra.at[k],
                                                device_id=sib, device_id_type=MESH)

        def b_copy(k):
            px, py = peers[k]
            return pltpu.make_async_remote_copy(src_ref=staged.at[k], dst_ref=contrib.at[k], send_sem=sb.at[k], recv_sem=rb.at[k],
                                                device_id=(px, py, c), device_id_type=MESH)

        def c_copy(which):
            return pltpu.make_async_remote_copy(src_ref=resbuf.at[which], dst_ref=resbuf.at[which], send_sem=sc, recv_sem=rc,
                                                device_id=sib, device_id_type=MESH)

        @pl.when(s == 0)
        def _():
            for k in range(N_CHIPS - 1):
                @pl.when(p == 2 * k + 2)
                def _():
                    a_copy(k).wait_send()
            acc[slot] = jnp.zeros((d_model, half), F32)

        acc[slot] += jnp.dot(ht_ref[...], dz_ref[...], preferred_element_type=F32)

        @pl.when(s == steps - 1)
        def _():
            for k in range(N_CHIPS):
                @pl.when(p == 2 * k)
                def _():
                    a_copy(k).start()
            for k in range(N_CHIPS - 1):
                @pl.when(p == 2 * k + 1)
                def _():
                    a_copy(k).wait_recv()
                    staged[k] = (acc[1] + theirs[k]).astype(BF16)
                    b_copy(k).start()

            @pl.when(p == N_DEV - 1)
            def _():
                a_copy(N_CHIPS - 1).wait_recv()
                tot = acc[1] + theirs[N_CHIPS - 1]
                for k in range(N_CHIPS - 1):
                    b_copy(k).wait_recv()
                    tot = tot + contrib[k].astype(F32)
                resbuf[c] = tot
                c_copy(c).start()
                c_copy(1 - c).wait_recv()
                done = pltpu.make_async_copy(resbuf, out_ref, out_sem)
                done.start()
                oc_copy(1 - c).wait_recv()
                ro_ref[...] = o_res[...]
                for k in range(N_DEV - 1):
                    px, py, pc = small_peer(k)
                    small_copy(k, 4 * px + 2 * py + pc, (px, py, pc)).wait_recv()
                tot8 = sbuf[0]
                for d in range(1, N_DEV):
                    tot8 = tot8 + sbuf[d]
                rs_ref[...] = tot8
                a_copy(N_CHIPS - 1).wait_send()
                for k in range(N_CHIPS - 1):
                    b_copy(k).wait_send()
                    ob_copy(k, chip_ids[k], me).wait_send()
                c_copy(c).wait_send()
                oc_copy(c).wait_send()
                for pos in range(N_CHIPS):
                    oa_copy(pos).wait_send()
                for k in range(N_DEV - 1):
                    small_copy(k, my8, small_peer(k)).wait_send()
                done.wait()

    dma = pltpu.SemaphoreType.DMA
    o_shape = g_out.shape[1:]
    go = g_out.reshape(N_CHIPS, 2, *o_shape)
    const = lambda shape: pl.BlockSpec(shape, lambda p, s, order_ref: (0,) * len(shape))
    grid_spec = pltpu.PrefetchScalarGridSpec(
        num_scalar_prefetch=1, grid=(N_DEV, steps),
        in_specs=[pl.BlockSpec((d_model, ts), lambda p, s, order_ref: (0, s)),
                  pl.BlockSpec((ts, half), lambda p, s, order_ref: (s, order_ref[p])),
                  pl.BlockSpec(memory_space=pl.ANY), const(small.shape)],
        out_specs=(pl.BlockSpec(memory_space=pl.ANY), const((2, *o_shape)), const(small.shape)),
        scratch_shapes=[pltpu.VMEM((2, d_model, half), F32), pltpu.VMEM((N_CHIPS, d_model, half), F32),
                        pltpu.VMEM((N_CHIPS - 1, d_model, half), BF16), pltpu.VMEM((N_CHIPS - 1, d_model, half), BF16),
                        pltpu.VMEM((2, d_model, half), F32), dma,
                        dma((N_CHIPS,)), dma((N_CHIPS,)), dma((N_CHIPS - 1,)), dma((N_CHIPS - 1,)), dma, dma,
                        pltpu.VMEM((N_CHIPS, *o_shape), F32), pltpu.VMEM((N_CHIPS, *o_shape), F32),
                        pltpu.VMEM((N_CHIPS, *o_shape), BF16), pltpu.VMEM((N_CHIPS, *o_shape), BF16),
                        pltpu.VMEM((2, *o_shape), F32), pltpu.VMEM((N_DEV, *small.shape), F32),
                        dma((N_CHIPS,)), dma((N_CHIPS,)), dma((N_CHIPS,)), dma((N_CHIPS - 1,)), dma((N_CHIPS - 1,)), dma, dma,
                        dma((N_DEV - 1,)), dma((N_DEV - 1,))])
    return pl.pallas_call(
        body, name="dw_in_reduce", grid_spec=grid_spec,
        out_shape=(jax.ShapeDtypeStruct((2, d_model, half), F32), jax.ShapeDtypeStruct((2, *o_shape), F32),
                   jax.ShapeDtypeStruct(small.shape, F32)),
        compiler_params=_params(("arbitrary", "arbitrary")),
    )(order, ht, dz, go, small)


def _adam_math(w, g, m, v):
    m = ADAM_B1 * m + (1.0 - ADAM_B1) * g
    v = ADAM_B2 * v + (1.0 - ADAM_B2) * (g * g)
    m_hat = m / (1.0 - ADAM_B1 ** ADAM_STEP)
    v_hat = v / (1.0 - ADAM_B2 ** ADAM_STEP)
    delta = -ADAM_LR * (m_hat / (jnp.sqrt(v_hat) + ADAM_EPS) + ADAM_WD * w)
    return delta, m, v


def adam_shard(name, w, g2, m, v, block, grid, w_map, g_map):
    def body(w_ref, g_ref, m_ref, v_ref, go_ref, d_ref, mo_ref, vo_ref):
        g = g_ref[0]
        delta, mn, vn = _adam_math(w_ref[...], g, m_ref[...], v_ref[...])
        go_ref[...] = g
        d_ref[...] = delta
        mo_ref[...] = mn
        vo_ref[...] = vn

    ws = pl.BlockSpec(block, w_map)
    shp = jax.ShapeDtypeStruct(w.shape, F32)
    return pl.pallas_call(
        body, name=name, grid=grid, out_shape=(shp, shp, shp, shp),
        in_specs=[ws, pl.BlockSpec((1, *block), g_map), ws, ws], out_specs=(ws, ws, ws, ws),
        compiler_params=_params(("arbitrary",) * len(grid)),
    )(w, g2, m, v)


def adam_small(ws, gs, ms, vs):
    n = len(ws)

    def body(*refs):
        ins, outs = refs[:4 * n], refs[4 * n:]
        for t in range(n):
            delta, mn, vn = _adam_math(ins[t][...], ins[n + t][...], ins[2 * n + t][...], ins[3 * n + t][...])
            outs[3 * t][...] = delta
            outs[3 * t + 1][...] = mn
            outs[3 * t + 2][...] = vn

    vm = pl.BlockSpec(memory_space=pltpu.VMEM)
    outs = pl.pallas_call(
        body, name="adam_small",
        out_shape=tuple(jax.ShapeDtypeStruct(w.shape, F32) for w in ws for _ in range(3)),
        in_specs=[vm] * (4 * n), out_specs=tuple([vm] * (3 * n)),
        compiler_params=_params(),
    )(*ws, *gs, *ms, *vs)
    return [outs[3 * t:3 * t + 3] for t in range(n)]


def kernel(x, norm_pre_g, w_in, conv_w, w_out, norm_post_g, loss_target, m_norm_pre_g, m_w_in, m_conv_w, m_w_out, m_norm_post_g, v_norm_pre_g, v_w_in, v_conv_w, v_w_out, v_norm_post_g):
    _, seq, d_model = x.shape
    width = w_in.shape[1]
    conv_q = conv_w.shape[1]
    conv_width = N_CHIPS * conv_q
    attn_width = d_model - conv_width
    xs, tg = x[0], loss_target[0]
    g1, g2 = norm_pre_g.reshape(1, d_model), norm_post_g.reshape(1, d_model)

    w_full, wout_full, cw_full, *tables = gather_weights(w_in, w_out, conv_w, seq)
    wout2 = wout_full.reshape(attn_width + conv_width, d_model)
    cw = jnp.zeros((SUBLANES, conv_width), F32).at[:CONV_K].set(
        cw_full[:, :CONV_K, :conv_q].transpose(1, 0, 2).reshape(CONV_K, conv_width))

    ht, q, k, v, qp, kp, vp, ga, cz = inproj(xs, g1, w_full, tables, attn_width, conv_width)
    run = attn_fwd("p4", qp, kp, vp, None)
    run = attn_fwd("p16", qp, kp, vp, run)
    o, lse = attn_fwd("nat", q, k, v, run)
    (d_o, delta, d_op, delta_p, lse_p, dga, dcb, dgc, dcv, e, dwout, dg2, dcw, loss_acc) = tail(
        o, lse, ga, cz, xs, tg, wout2, g2, cw)
    nat_grads = attn_bwd("nat", q, k, v, d_o, lse, delta, None)
    perm_grads = attn_bwd("p4", qp, kp, vp, d_op, lse_p, delta_p, None)
    perm_grads = attn_bwd("p16", qp, kp, vp, d_op, lse_p, delta_p, perm_grads)
    grad_x, dz, dg1 = dz_dx(nat_grads, perm_grads, dga, dcb, dgc, dcv, cz, tables, xs, g1, e, w_full, cw)

    small = jnp.zeros((SUBLANES, d_model), F32)
    small = small.at[0].set(dg1.sum(axis=0)).at[1].set(dg2.sum(axis=0))
    small = small.at[2:2 + CONV_K, :conv_width].set(dcw.reshape(CONV_K, SUBLANES, conv_width).sum(axis=1))
    small = small.at[2 + CONV_K, 0].set(jnp.sum(loss_acc))
    rin, rout, rsmall = dw_in_reduce(ht, dz, dwout.reshape(N_DEV, -1, d_model), small)

    half = width // 2
    tr = 256
    gw_in, d_in, m_in, v_in = adam_shard(
        "adam_w_in", w_in, rin, m_w_in, v_w_in, (tr, half), (2, d_model // tr),
        lambda hf, i: (i, hf), lambda hf, i: (hf, i, 0))
    rq = w_out.shape[0] // 2
    gw_out, d_out, m_out, v_out = adam_shard(
        "adam_w_out", w_out, rout, m_w_out, v_w_out, (rq, d_model), (2,),
        lambda hf: (hf, 0), lambda hf: (hf, 0, 0))

    chip = 2 * lax.axis_index("x") + lax.axis_index("y")
    g_pre, g_post = rsmall[0:1], rsmall[1:2]
    g_conv = lax.dynamic_slice(rsmall[2:2 + CONV_K, :conv_width], (0, chip * conv_q), (CONV_K, conv_q))
    (d_pre, m_pre, v_pre), (d_post, m_post, v_post), (d_cv, m_cv, v_cv) = adam_small(
        [g1, g2, conv_w], [g_pre, g_post, g_conv],
        [m_norm_pre_g.reshape(1, d_model), m_norm_post_g.reshape(1, d_model), m_conv_w],
        [v_norm_pre_g.reshape(1, d_model), v_norm_post_g.reshape(1, d_model), v_conv_w])

    loss = 0.5 * rsmall[2 + CONV_K, 0] / d_model
    vec = lambda a: a.reshape(d_model)
    return (loss, grad_x.reshape(1, seq, d_model),
            vec(g_pre), gw_in, g_conv, gw_out, vec(g_post),
            vec(d_pre), d_in, d_cv, d_out, vec(d_post),
            vec(m_pre), m_in, m_cv, m_out, vec(m_post),
            vec(v_pre), v_in, v_cv, v_out, vec(v_post))
```

```python
import jax
import jax.numpy as jnp
from jax import lax
from jax.experimental import pallas as pl
from jax.experimental.pallas import tpu as pltpu

HEAD_DIM = 64
LANES = 128
SUBLANES = 8
BLOCK = 128
WINDOW_KEYS = 128
PERM = 16
PJ = 4
P4_ROWS = BLOCK // PJ
MAX_QUERY_BLOCKS = 8
ROW_TILE = 512
DW_ROWS = 2048
ADAM_ROWS = 1024
CONV_K = 3
ROPE_THETA = 10000.0
NORM_EPS = 1e-6
ATTN_SCALE = HEAD_DIM ** -0.5
NEG = -1e30
N_CHIPS = 4
N_DEV = 8
MESH = pl.DeviceIdType.MESH
ADAM_LR = 0.001
ADAM_B1 = 0.9
ADAM_B2 = 0.999
ADAM_EPS = 1e-08
ADAM_WD = 0.01
ADAM_STEP = 10
VMEM_LIMIT = 52 * 1024 * 1024

F32 = jnp.float32
BF16 = jnp.bfloat16


def _params(sem=None, **kw):
    return pltpu.CompilerParams(dimension_semantics=sem, vmem_limit_bytes=VMEM_LIMIT, **kw)


def _const_spec(shape):
    return pl.BlockSpec(shape, lambda *_: (0,) * len(shape), pipeline_mode=pl.Buffered(1))


def _sigmoid(z):
    return 1.0 / (1.0 + jnp.exp(-z))


def _rowgroup_sum(a):
    rows, n = a.shape
    return a.reshape(rows // SUBLANES, SUBLANES, n).sum(axis=0)


def _nt(a, b):
    return lax.dot_general(a, b, (((1,), (1,)), ((), ())), preferred_element_type=F32)


def _tn(a, b):
    return lax.dot_general(a, b, (((0,), (0,)), ((), ())), preferred_element_type=F32)


def _col_pieces(a, b, width):
    out = []
    while a < b:
        j = a // width
        e = min(b, (j + 1) * width)
        out.append((j, a - j * width, e - j * width))
        a = e
    return out


def _lane_groups(width):
    return [slice(g * LANES, (g + 1) * LANES) for g in range(width // LANES)]


def _perm_shape(seq, width):
    return (PJ, PJ, seq // PERM, width)


def _perm_tile_spec(width, tm):
    return pl.BlockSpec((PJ, PJ, tm // PERM, width), lambda i: (0, 0, i, 0))


STAGE_PITCH = 24


def _stage_shape(groups, rows):
    return (groups, rows // PERM * STAGE_PITCH, LANES)


def _stage_put(stage, g, val, row0=0):
    for a in range(val.shape[0] // PERM):
        at = (row0 // PERM + a) * STAGE_PITCH
        stage[g, at:at + PERM, :] = val[a * PERM:(a + 1) * PERM]


def _stage_get(stage, g):
    return jnp.concatenate([stage[g, a * STAGE_PITCH:a * STAGE_PITCH + PERM, :]
                            for a in range(stage.shape[1] // STAGE_PITCH)], axis=0)


def _to_perm(stage, g, dst_ref, sl, dtype):
    rows = stage.shape[1] // STAGE_PITCH
    for b in range(PERM):
        dst_ref[b // PJ, b % PJ, :, sl] = stage[g, pl.ds(b, rows, stride=STAGE_PITCH), :].astype(dtype)


def _from_perm(src_ref, sl, stage, g):
    rows = stage.shape[1] // STAGE_PITCH
    for b in range(PERM):
        stage[g, pl.ds(b, rows, stride=STAGE_PITCH), :] = src_ref[b // PJ, b % PJ, :, sl].astype(F32)


def _flip(a, f):
    return 1 - a if f else a


def gather_weights(w_in, w_out, conv_w, seq):
    d_model, width = w_in.shape
    rows = w_out.shape[0]
    cw = jnp.zeros((SUBLANES, LANES), F32).at[:CONV_K, :conv_w.shape[1]].set(conv_w)
    half_dim = HEAD_DIM // 2
    inv_freq = ROPE_THETA ** (-jnp.arange(half_dim, dtype=F32) * 2.0 / HEAD_DIM)
    inv_freq = jnp.tile(inv_freq, LANES // half_dim).reshape(1, LANES)
    chunk = min(ROW_TILE, seq)

    def body(win_ref, wout_ref, cw_ref, freq_ref, winf_ref, woutf_ref, cwf_ref, cos_ref, s1_ref, s2_ref,
             st_in, st_out, near_send, near_recv, far_send, far_recv, cw_send, cw_recv, d2d_send, d2d_recv):
        x, y, c = lax.axis_index("x"), lax.axis_index("y"), lax.axis_index("c")
        me = 2 * x + y
        sib = (x, y, 1 - c)
        st_in[...] = win_ref[...].astype(BF16)
        st_out[...] = wout_ref[...].astype(BF16)
        winf_ref[me] = st_in[...]
        woutf_ref[me] = st_out[...]
        cwf_ref[me] = cw_ref[...]
        stages = (st_in, st_out)
        fulls = (winf_ref, woutf_ref)
        halves = (d_model // 2, rows // 2)

        def part(t, core, q=None):
            size = halves[t] if q is None else halves[t] // 2
            start = core * halves[t] if q is None else core * halves[t] + q * size
            return pl.ds(pl.multiple_of(start, size), size)

        near = [(1 - x, y), (x, 1 - y)]
        far = (1 - x, 1 - y)
        chip = lambda px, py: 2 * px + py

        def direct(k, t, q, slot, to):
            src = stages[t].at[part(t, c, q)]
            return pltpu.make_async_remote_copy(src_ref=src, dst_ref=fulls[t].at[slot, part(t, c, q)], send_sem=near_send.at[k, t, q],
                                                recv_sem=near_recv.at[k, t, q], device_id=to, device_id_type=MESH)

        def passed_on(k, t, slot, to):
            ref = fulls[t].at[slot, part(t, c, k)]
            return pltpu.make_async_remote_copy(src_ref=ref, dst_ref=ref, send_sem=far_send.at[k, t], recv_sem=far_recv.at[k, t],
                                                device_id=to, device_id_type=MESH)

        def conv_copy(k, slot, to):
            return pltpu.make_async_remote_copy(src_ref=cw_ref, dst_ref=cwf_ref.at[slot], send_sem=cw_send.at[k], recv_sem=cw_recv.at[k],
                                                device_id=to, device_id_type=MESH)

        def d2d(k, t, slot, core):
            ref = fulls[t].at[slot, part(t, core)]
            return pltpu.make_async_remote_copy(src_ref=ref, dst_ref=ref, send_sem=d2d_send.at[k, t], recv_sem=d2d_recv.at[k, t],
                                                device_id=sib, device_id_type=MESH)

        sends = []

        def go(cp):
            cp.start()
            sends.append(cp)

        for q_first in (0, 1):
            for k, (px, py) in enumerate(near):
                for t in range(2):
                    go(direct(k, t, k if q_first == 0 else 1 - k, me, (px, py, c)))
        for k, (px, py) in enumerate(near + [far]):
            go(conv_copy(k, me, (px, py, c)))
        for k, (px, py) in enumerate(near):
            other = near[1 - k]
            for t in range(2):
                direct(k, t, k, chip(px, py), (px, py, c)).wait_recv()
                go(passed_on(k, t, chip(px, py), (*other, c)))

        first_half = lax.broadcasted_iota(jnp.int32, (chunk, LANES), 1) % HEAD_DIM < half_dim
        row = lax.broadcasted_iota(jnp.int32, (chunk, LANES), 0)

        def table_rows(i, carry):
            at = pl.multiple_of(i * chunk, chunk)
            ang = (row + at).astype(F32) * freq_ref[...]
            sin = jnp.sin(ang)
            cos_ref[pl.ds(at, chunk), :] = jnp.cos(ang)
            s1_ref[pl.ds(at, chunk), :] = jnp.where(first_half, -sin, 0.0)
            s2_ref[pl.ds(at, chunk), :] = jnp.where(first_half, 0.0, sin)
            return carry

        lax.fori_loop(0, seq // chunk, table_rows, 0)

        for k, (px, py) in enumerate(near):
            for t in range(2):
                direct(k, t, 1 - k, chip(px, py), (px, py, c)).wait_recv()
                go(d2d(k, t, chip(px, py), c))
        for t in range(2):
            for k, (px, py) in enumerate(near):
                passed_on(k, t, chip(*far), (px, py, c)).wait_recv()
            go(d2d(2, t, chip(*far), c))
        for k, (px, py) in enumerate(near + [far]):
            conv_copy(k, chip(px, py), (px, py, c)).wait_recv()
            for t in range(2):
                d2d(k, t, chip(px, py), 1 - c).wait_recv()
        for cp in sends:
            cp.wait_send()

    vm = pl.BlockSpec(memory_space=pltpu.VMEM)
    dma = pltpu.SemaphoreType.DMA
    return pl.pallas_call(
        body, name="gather_weights",
        out_shape=(jax.ShapeDtypeStruct((N_CHIPS, d_model, width), BF16),
                   jax.ShapeDtypeStruct((N_CHIPS, rows, d_model), BF16),
                   jax.ShapeDtypeStruct((N_CHIPS, SUBLANES, LANES), F32),
                   *[jax.ShapeDtypeStruct((seq, LANES), F32)] * 3),
        in_specs=[vm, vm, vm, vm], out_specs=(vm,) * 6,
        scratch_shapes=[pltpu.VMEM((d_model, width), BF16), pltpu.VMEM((rows, d_model), BF16),
                        dma((2, 2, 2)), dma((2, 2, 2)), dma((2, 2)), dma((2, 2)), dma((3,)), dma((3,)),
                        dma((3, 2)), dma((3, 2))],
        compiler_params=_params(),
    )(w_in, w_out, cw, inv_freq)


def _rope(t, cos, s1, s2):
    return t * cos + pltpu.roll(t, LANES - HEAD_DIM // 2, 1) * s1 + pltpu.roll(t, HEAD_DIM // 2, 1) * s2


def _rope_transposed(g, cos, s1, s2):
    return g * cos + pltpu.roll(g * s1, HEAD_DIM // 2, 1) + pltpu.roll(g * s2, LANES - HEAD_DIM // 2, 1)


def inproj(x, g1, w_full, tables, attn_w, conv_w):
    seq, d_model = x.shape
    width = w_full.shape[2]
    tm = ROW_TILE
    groups = _lane_groups(attn_w)

    def body(x_ref, g_ref, w_ref, cos_ref, s1_ref, s2_ref,
             ht_ref, q_ref, k_ref, v_ref, qp_ref, kp_ref, vp_ref, ga_ref, cz_ref, stage):
        xv = x_ref[...]
        hb = ((xv * lax.rsqrt(jnp.mean(xv * xv, axis=-1, keepdims=True) + NORM_EPS)) * g_ref[...]).astype(BF16)
        ht_ref[...] = jnp.transpose(hb)
        cos, s1, s2 = cos_ref[...], s1_ref[...], s2_ref[...]

        def proj(a, b):
            parts = [jnp.dot(hb, w_ref[j, :, lo:hi], preferred_element_type=F32) for j, lo, hi in _col_pieces(a, b, width)]
            return parts[0] if len(parts) == 1 else jnp.concatenate(parts, axis=1)

        def emit(z, nat_ref, perm_ref, fn):
            for g, sl in enumerate(groups):
                val = fn(z[:, sl])
                nat_ref[:, sl] = val.astype(BF16)
                _stage_put(stage, g, val)
            for g, sl in enumerate(groups):
                _to_perm(stage, g, perm_ref, sl, BF16)

        emit(proj(0, attn_w), q_ref, qp_ref, lambda t: _rope(t, cos, s1, s2) * ATTN_SCALE)
        emit(proj(attn_w, 2 * attn_w), k_ref, kp_ref, lambda t: _rope(t, cos, s1, s2))
        emit(proj(2 * attn_w, 3 * attn_w), v_ref, vp_ref, lambda t: t)
        ga_ref[...] = proj(3 * attn_w, 4 * attn_w)
        cz_ref[...] = proj(4 * attn_w, 4 * attn_w + 4 * conv_w)

    row = lambda n: pl.BlockSpec((tm, n), lambda i: (i, 0))
    nat = jax.ShapeDtypeStruct((seq, attn_w), BF16)
    perm = jax.ShapeDtypeStruct(_perm_shape(seq, attn_w), BF16)
    return pl.pallas_call(
        body, name="inproj", grid=(seq // tm,),
        out_shape=(jax.ShapeDtypeStruct((d_model, seq), BF16), nat, nat, nat, perm, perm, perm,
                   jax.ShapeDtypeStruct((seq, attn_w), F32), jax.ShapeDtypeStruct((seq, 4 * conv_w), F32)),
        in_specs=[row(d_model), _const_spec((1, d_model)), _const_spec(w_full.shape), row(LANES), row(LANES), row(LANES)],
        out_specs=(pl.BlockSpec((d_model, tm), lambda i: (0, i)), row(attn_w), row(attn_w), row(attn_w),
                   _perm_tile_spec(attn_w, tm), _perm_tile_spec(attn_w, tm), _perm_tile_spec(attn_w, tm),
                   row(attn_w), row(4 * conv_w)),
        scratch_shapes=[pltpu.VMEM(_stage_shape(len(groups), tm), F32)],
        compiler_params=_params(("arbitrary",)),
    )(x, g1, w_full, *tables)


class _Mode:
    def __init__(self, name, seq):
        self.name = name
        if name == "nat":
            self.residues, blocks = 1, seq // BLOCK
        elif name == "p16":
            self.residues, blocks = PERM, seq // PERM // BLOCK
        else:
            self.residues, blocks = PJ, seq // PERM // P4_ROWS
        self.qb = max(d for d in range(1, MAX_QUERY_BLOCKS + 1) if blocks % d == 0)
        self.steps = blocks // self.qb

    def _spec(self, blocks, width, at):
        if self.name == "nat":
            return pl.BlockSpec((blocks * BLOCK, width), lambda *g: (at(*g)[1], 0))
        if self.name == "p16":
            return pl.BlockSpec((1, 1, blocks * BLOCK, width), lambda *g: (at(*g)[0] // PJ, at(*g)[0] % PJ, at(*g)[1], 0))
        return pl.BlockSpec((PJ, 1, blocks * P4_ROWS, width), lambda *g: (0, at(*g)[0], at(*g)[1], 0))

    def wide(self, width, where=lambda r, n: (r, n)):
        return self._spec(self.qb, width, where)

    def block_before(self, width, where=lambda r, n: (r, n)):
        return self._spec(1, width, lambda *g: (where(*g)[0], jnp.maximum(self.qb * where(*g)[1] - 1, 0)))

    def get(self, ref, sl, sub=0):
        if self.name == "nat":
            return ref[sub * BLOCK:(sub + 1) * BLOCK, sl]
        if self.name == "p16":
            return ref[0, 0, sub * BLOCK:(sub + 1) * BLOCK, sl]
        return jnp.concatenate([ref[j, 0, sub * P4_ROWS:(sub + 1) * P4_ROWS, sl] for j in range(PJ)], axis=0)

    def put(self, ref, sl, val, sub=0):
        val = val.astype(ref.dtype)
        if self.name == "nat":
            ref[sub * BLOCK:(sub + 1) * BLOCK, sl] = val
        elif self.name == "p16":
            ref[0, 0, sub * BLOCK:(sub + 1) * BLOCK, sl] = val
        else:
            for j in range(PJ):
                ref[j, 0, sub * P4_ROWS:(sub + 1) * P4_ROWS, sl] = val[j * P4_ROWS:(j + 1) * P4_ROWS]

    def keys(self, before_ref, wide_ref, sl, sub):
        older = self.get(before_ref, sl) if sub == 0 else self.get(wide_ref, sl, sub - 1)
        return jnp.concatenate([older, self.get(wide_ref, sl, sub)], axis=0)

    def index(self, idx, is_key):
        if self.name != "p4":
            return idx - BLOCK if is_key else idx
        within = jnp.bitwise_and(idx, BLOCK - 1)
        m = PJ * jnp.bitwise_and(within, P4_ROWS - 1) + jnp.right_shift(within, P4_ROWS.bit_length() - 1)
        return m + BLOCK * (jnp.right_shift(idx, BLOCK.bit_length() - 1) - 1) if is_key else m

    def bias(self, has_before):
        shape = (2 * BLOCK, BLOCK)
        kidx = lax.broadcasted_iota(jnp.int32, shape, 0)
        qidx = lax.broadcasted_iota(jnp.int32, shape, 1)
        rel = self.index(qidx, False) - self.index(kidx, True)
        valid = (rel >= 0) & (rel <= WINDOW_KEYS)
        if has_before is not True:
            valid = valid & ((kidx >= BLOCK) | has_before)
        one = jnp.where(valid, 0.0, NEG)
        return jnp.concatenate([one, one], axis=1)


def _head_masks():
    lane = lax.broadcasted_iota(jnp.int32, (BLOCK, LANES), 1)
    lo = lane < HEAD_DIM
    return lane, lo, jnp.where(lo, 1.0, 0.0).astype(BF16), jnp.where(lo, 0.0, 1.0).astype(BF16)


def attn_fwd(name, q, k, v, run):
    nat = name == "nat"
    seq = q.shape[0] if nat else q.shape[2] * PERM
    attn_w = q.shape[-1]
    mode = _Mode(name, seq)
    groups = _lane_groups(attn_w)
    first = run is None
    all_lanes = slice(0, LANES)

    def body(*refs):
        q_ref, kp_ref, kc_ref, vp_ref, vc_ref = refs[:5]
        if first:
            o_ref, l_ref = refs[5:]
        elif nat:
            oin_ref, lin_ref, o_ref, l_ref, ostage, lstage = refs[5:]
        else:
            oin_ref, lin_ref, o_ref, l_ref = refs[5:]
        n = pl.program_id(1)
        subs = range(mode.qb)
        biases = [mode.bias(n > 0)] + [mode.bias(True)] * (mode.qb - 1)
        _, lo, m_lo, m_hi = _head_masks()
        head_row = lax.broadcasted_iota(jnp.int32, (BLOCK, LANES), 0)
        ones = jnp.ones((2 * BLOCK, LANES), BF16)
        lrows = [jnp.zeros((BLOCK, LANES), F32) for _ in subs]
        if not first:
            if nat:
                for g, sl in enumerate(groups):
                    _from_perm(oin_ref, sl, ostage, g)
                _from_perm(lin_ref, all_lanes, lstage, 0)
            wide_rows = lambda a, sub: a[sub * BLOCK:(sub + 1) * BLOCK]
            before = [jnp.transpose(wide_rows(_stage_get(lstage, 0), sub) if nat else mode.get(lin_ref, all_lanes, sub))
                      for sub in subs]

        def probs(sub, p, sl):
            q2 = mode.get(q_ref, sl, sub)
            kcat = mode.keys(kp_ref, kc_ref, sl, sub)
            vcat = mode.keys(vp_ref, vc_ref, sl, sub)
            qq = jnp.concatenate([q2 * m_lo, q2 * m_hi], axis=0)
            s_t = _nt(kcat, qq) + biases[sub]
            m = jnp.max(s_t, axis=0, keepdims=True)
            pe = jnp.exp(s_t - m)
            lse = m + jnp.log(jnp.sum(pe, axis=0, keepdims=True))
            if not first:
                was = jnp.concatenate([before[sub][2 * p:2 * p + 1, :], before[sub][2 * p + 1:2 * p + 2, :]], axis=1)
                top = jnp.maximum(was, lse)
                lse = top + jnp.log(jnp.exp(was - top) + jnp.exp(lse - top))
                pe = pe * jnp.exp(m - lse)
            return jnp.concatenate([vcat, ones], axis=1), pe.astype(BF16), lse

        def output(sub, p, sl, vext, pb, lse):
            o_ext = _tn(pb, vext)
            if first:
                o_new = o_ext[:, :LANES] / o_ext[:, LANES:]
            else:
                o_prev = wide_rows(_stage_get(ostage, p), sub) if nat else mode.get(oin_ref, sl, sub)
                o_new = o_ext[:, :LANES] + jnp.concatenate([o_prev, o_prev], axis=0) * (1.0 - o_ext[:, LANES:])
            mode.put(o_ref, sl, jnp.where(lo, o_new[:BLOCK], o_new[BLOCK:]), sub)
            rows = jnp.where(head_row == 2 * p, lse[:, :BLOCK], lrows[sub])
            lrows[sub] = jnp.where(head_row == 2 * p + 1, lse[:, BLOCK:], rows)

        pending = None
        for sub in subs:
            for p, sl in enumerate(groups):
                nxt = probs(sub, p, sl)
                if pending is not None:
                    output(*pending)
                pending = (sub, p, sl, *nxt)
        output(*pending)
        for sub in subs:
            mode.put(l_ref, all_lanes, jnp.transpose(lrows[sub]), sub)

    ins = [q, k, k, v, v]
    specs = [mode.wide(attn_w), mode.block_before(attn_w), mode.wide(attn_w), mode.block_before(attn_w), mode.wide(attn_w)]
    scratch = []
    if not first:
        ins += list(run)
        if nat:
            rows_a = mode.qb * BLOCK // PERM
            specs += [pl.BlockSpec((PJ, PJ, rows_a, attn_w), lambda r, n: (0, 0, n, 0)),
                      pl.BlockSpec((PJ, PJ, rows_a, LANES), lambda r, n: (0, 0, n, 0))]
            scratch = [pltpu.VMEM(_stage_shape(len(groups), mode.qb * BLOCK), F32),
                       pltpu.VMEM(_stage_shape(1, mode.qb * BLOCK), F32)]
        else:
            specs += [mode.wide(attn_w), mode.wide(LANES)]
    if nat:
        out_shape = (jax.ShapeDtypeStruct((seq, attn_w), F32), jax.ShapeDtypeStruct((seq, LANES), F32))
    else:
        out_shape = (jax.ShapeDtypeStruct(_perm_shape(seq, attn_w), F32), jax.ShapeDtypeStruct(_perm_shape(seq, LANES), F32))
    return pl.pallas_call(
        body, name=f"attn_fwd_{name}", grid=(mode.residues, mode.steps),
        out_shape=out_shape, in_specs=specs, out_specs=(mode.wide(attn_w), mode.wide(LANES)),
        scratch_shapes=scratch,
        compiler_params=_params(("arbitrary", "arbitrary")),
    )(*ins)


def attn_bwd(name, q, k, v, d_o, lse, delta, run):
    nat = name == "nat"
    seq = q.shape[0] if nat else q.shape[2] * PERM
    attn_w = q.shape[-1]
    mode = _Mode(name, seq)
    steps, qb = mode.steps, mode.qb
    single = steps == 1
    groups = _lane_groups(attn_w)
    first = run is None
    all_lanes = slice(0, LANES)

    def body(*refs):
        q_ref, kp_ref, kc_ref, vp_ref, vc_ref, do_ref, lse_ref, dl_ref = refs[:8]
        if first:
            dq_ref, dk_ref, dv_ref, ck, cv = refs[8:]
        else:
            dqi_ref, dki_ref, dvi_ref, dq_ref, dk_ref, dv_ref, ck, cv = refs[8:]
        g = pl.program_id(1) if single else pl.program_id(0)
        n = g if single else lax.rem(g, steps)
        carries = ((ck, dk_ref, None if first else dki_ref), (cv, dv_ref, None if first else dvi_ref))

        def emit(out_ref, acc_ref, sl, sub, val):
            if acc_ref is not None:
                val = val + mode.get(acc_ref, sl, sub).astype(F32)
            mode.put(out_ref, sl, val, sub)

        if not single:
            @pl.when(g == 0)
            def _():
                ck[...] = jnp.zeros_like(ck)
                cv[...] = jnp.zeros_like(cv)

        @pl.when(g < total)
        def _():
            biases = [mode.bias(n > 0)] + [mode.bias(True)] * (qb - 1)
            _, lo, m_lo, m_hi = _head_masks()

            def scores(sub, p, sl, lse_t, dl_t):
                q2, do2 = mode.get(q_ref, sl, sub), mode.get(do_ref, sl, sub)
                kcat = mode.keys(kp_ref, kc_ref, sl, sub)
                vcat = mode.keys(vp_ref, vc_ref, sl, sub)
                qq = jnp.concatenate([q2 * m_lo, q2 * m_hi], axis=0)
                dd = jnp.concatenate([do2 * m_lo, do2 * m_hi], axis=0)
                h0 = 2 * p
                lse2 = jnp.concatenate([lse_t[h0:h0 + 1, :], lse_t[h0 + 1:h0 + 2, :]], axis=1)
                dl2 = jnp.concatenate([dl_t[h0:h0 + 1, :], dl_t[h0 + 1:h0 + 2, :]], axis=1)
                p_t = jnp.exp(_nt(kcat, qq) + (biases[sub] - lse2))
                ds_t = p_t * (_nt(vcat, dd) - dl2)
                return qq, dd, kcat, p_t.astype(BF16), ds_t.astype(BF16)

            def grads(sub, sl, qq, dd, kcat, pb, dsb):
                dqb = _tn(dsb, kcat)
                dq2 = jnp.where(lo, dqb[:BLOCK], dqb[BLOCK:]) * ATTN_SCALE
                if not first:
                    dq2 = dq2 + mode.get(dqi_ref, sl, sub).astype(F32)
                mode.put(dq_ref, sl, dq2, sub)
                for (carry, out_ref, acc_ref), lhs, rhs in zip(carries, (dsb, pb), (qq, dd)):
                    both = jnp.dot(lhs, rhs, preferred_element_type=F32)
                    if sub == 0:
                        if not single:
                            for s in range(qb - 1):
                                emit(out_ref, acc_ref, sl, s, carry[s, :, sl])
                            emit(out_ref, acc_ref, sl, qb - 1, carry[qb - 1, :, sl] + both[:BLOCK])
                        carry[0, :, sl] = both[BLOCK:]
                    else:
                        carry[sub - 1, :, sl] += both[:BLOCK]
                        carry[sub, :, sl] = both[BLOCK:]
                    if single and sub == qb - 1:
                        for s in range(qb):
                            emit(out_ref, acc_ref, sl, s, carry[s, :, sl])

            stats = [(jnp.transpose(mode.get(lse_ref, all_lanes, sub)),
                      jnp.transpose(mode.get(dl_ref, all_lanes, sub))) for sub in range(qb)]
            pending = None
            for p, sl in enumerate(groups):
                for sub in range(qb):
                    nxt = scores(sub, p, sl, *stats[sub])
                    if pending is not None:
                        grads(*pending)
                    pending = (sub, sl, *nxt)
            grads(*pending)

        if not single:
            @pl.when(g == total)
            def _():
                for carry, out_ref, acc_ref in carries:
                    for sl in groups:
                        for s in range(qb):
                            emit(out_ref, acc_ref, sl, s, carry[s, :, sl])

    total = mode.residues * steps
    if single:
        here = before = lambda r, n: (r, n)
    else:
        locate = lambda g: (g // steps, lax.rem(g, steps))
        here = lambda g: locate(jnp.minimum(g, total - 1))
        before = lambda g: locate(jnp.maximum(g - 1, 0))
    wide = lambda w: mode.wide(w, here)
    ins = [q, k, k, v, v, d_o, lse, delta]
    specs = [wide(attn_w), mode.block_before(attn_w, here), wide(attn_w), mode.block_before(attn_w, here), wide(attn_w),
             wide(attn_w), wide(LANES), wide(LANES)]
    if not first:
        ins += list(run)
        specs += [wide(attn_w), mode.wide(attn_w, before), mode.wide(attn_w, before)]
    shp = jax.ShapeDtypeStruct((seq, attn_w) if nat else _perm_shape(seq, attn_w), BF16)
    grid = (mode.residues, 1) if single else (total + 1,)
    return pl.pallas_call(
        body, name=f"attn_bwd_{name}", grid=grid,
        out_shape=(shp, shp, shp), in_specs=specs,
        out_specs=(wide(attn_w), mode.wide(attn_w, before), mode.wide(attn_w, before)),
        scratch_shapes=[pltpu.VMEM((qb, BLOCK, attn_w), F32), pltpu.VMEM((qb, BLOCK, attn_w), F32)],
        compiler_params=_params(("arbitrary",) * len(grid)),
    )(*ins)


def _shift_down(u, halo, k):
    rolled = pltpu.roll(u, k, 0)
    row = lax.broadcasted_iota(jnp.int32, halo.shape, 0)
    top = jnp.where(row < k, pltpu.roll(halo, k, 0), rolled[:SUBLANES])
    return jnp.concatenate([top, rolled[SUBLANES:]], axis=0)


def _shift_up(u, halo, k):
    rows = u.shape[0]
    rolled = pltpu.roll(u, rows - k, 0)
    row = lax.broadcasted_iota(jnp.int32, halo.shape, 0)
    bot = jnp.where(row >= SUBLANES - k, pltpu.roll(halo, SUBLANES - k, 0), rolled[rows - SUBLANES:])
    return jnp.concatenate([rolled[:rows - SUBLANES], bot], axis=0)


def tail(o, lse, ga, cz, x, tgt, w_out, g2, cw):
    seq, d_model = x.shape
    attn_w = o.shape[1]
    conv_w = cz.shape[1] // 4
    mix = attn_w + conv_w
    groups = _lane_groups(attn_w)
    tm = ROW_TILE
    nt = seq // tm
    hb = tm // SUBLANES

    def body(o_ref, l_ref, ga_ref, cz_ref, hz_ref, x_ref, t_ref, w_ref, g_ref, cw_ref,
             do_ref, dl_ref, dop_ref, dlp_ref, lp_ref, dga_ref, dcb_ref, dgc_ref, dcv_ref, e_ref,
             dw_ref, dg_ref, dcw_ref, loss_ref, stage):
        i = pl.program_id(0)

        @pl.when(i == 0)
        def _():
            dw_ref[...] = jnp.zeros_like(dw_ref)
            dg_ref[...] = jnp.zeros_like(dg_ref)
            dcw_ref[...] = jnp.zeros_like(dcw_ref)
            loss_ref[...] = jnp.zeros_like(loss_ref)

        u = cz_ref[:, 2 * conv_w:3 * conv_w] * cz_ref[:, 0:conv_w]
        uh = hz_ref[:, 2 * conv_w:3 * conv_w] * hz_ref[:, 0:conv_w]
        uh = jnp.where(i > 0, uh, 0.0)
        u1 = _shift_down(u, uh, 1)
        u2 = _shift_down(u, uh, 2)
        w0, w1, w2 = cw_ref[0:1, :], cw_ref[1:2, :], cw_ref[2:3, :]
        cvv = u2 * w0 + u1 * w1 + u * w2
        gv = g_ref[...]
        all_lanes = slice(0, LANES)

        def forward(rs):
            ov, gav = o_ref[rs, :], ga_ref[rs, :]
            sig_a = _sigmoid(gav)
            silu_a = gav * sig_a
            cb, gc = cz_ref[rs, conv_w:2 * conv_w], cz_ref[rs, 3 * conv_w:4 * conv_w]
            sig_c = _sigmoid(gc)
            silu_c = gc * sig_c
            bc = cb * cvv[rs]
            mixed = jnp.concatenate([ov * silu_a, bc * silu_c], axis=1).astype(BF16)
            yv = jnp.dot(mixed, w_ref[...], preferred_element_type=F32)
            return ov, gav, sig_a, silu_a, cb, gc, sig_c, silu_c, bc, mixed, yv

        def loss_and_dy(rs, mixed, yv):
            r2 = lax.rsqrt(jnp.mean(yv * yv, axis=-1, keepdims=True) + NORM_EPS)
            yhat = yv * r2
            diff = (x_ref[rs, :] + yhat * gv) - t_ref[rs, :]
            loss_ref[...] += _rowgroup_sum(diff * diff)
            ev = diff * (1.0 / d_model)
            e_ref[rs, :] = ev
            dg_ref[...] += _rowgroup_sum(ev * yhat)
            eg = ev * gv
            dy = (r2 * (eg - yhat * jnp.mean(eg * yhat, axis=-1, keepdims=True))).astype(BF16)
            dw_ref[...] += _tn(mixed, dy)
            return _nt(dy, w_ref[...])

        def backward(rs, ov, gav, sig_a, silu_a, cb, gc, sig_c, silu_c, bc, dm):
            rows = rs.stop - rs.start
            dma, dmc = dm[:, :attn_w], dm[:, attn_w:]
            dov = dma * silu_a
            do_ref[rs, :] = dov.astype(BF16)
            dga_ref[rs, :] = (dma * ov * (sig_a * (1.0 + gav * (1.0 - sig_a)))).astype(BF16)
            prod = dov * ov
            lane = lax.broadcasted_iota(jnp.int32, (rows, LANES), 1)
            lo = lane < HEAD_DIM
            dblk = jnp.zeros((rows, LANES), F32)
            for p, sl in enumerate(groups):
                pr = prod[:, sl]
                dblk = jnp.where(lane == 2 * p, jnp.sum(jnp.where(lo, pr, 0.0), axis=1, keepdims=True), dblk)
                dblk = jnp.where(lane == 2 * p + 1, jnp.sum(jnp.where(lo, 0.0, pr), axis=1, keepdims=True), dblk)
                _stage_put(stage, p, dov[:, sl], rs.start)
            dl_ref[rs, :] = dblk
            _stage_put(stage, len(groups), dblk, rs.start)
            _stage_put(stage, len(groups) + 1, l_ref[rs, :], rs.start)
            dsc = dmc * silu_c
            cv_rows = cvv[rs]
            dcb_ref[rs, :] = (dsc * cv_rows).astype(BF16)
            dgc_ref[rs, :] = (dmc * bc * (sig_c * (1.0 + gc * (1.0 - sig_c)))).astype(BF16)
            dcv = dsc * cb
            dcv_ref[rs, :] = dcv
            dcw_ref[0:SUBLANES, :] += _rowgroup_sum(dcv * u2[rs])
            dcw_ref[SUBLANES:2 * SUBLANES, :] += _rowgroup_sum(dcv * u1[rs])
            dcw_ref[2 * SUBLANES:3 * SUBLANES, :] += _rowgroup_sum(dcv * u[rs])

        halves = [slice(0, tm // 2), slice(tm // 2, tm)]
        fwd = [forward(rs) for rs in halves]
        dms = [loss_and_dy(rs, f[9], f[10]) for rs, f in zip(halves, fwd)]
        for rs, f, dm in zip(halves, fwd, dms):
            backward(rs, *f[:9], dm)
        for p, sl in enumerate(groups):
            _to_perm(stage, p, dop_ref, sl, BF16)
        _to_perm(stage, len(groups), dlp_ref, all_lanes, F32)
        _to_perm(stage, len(groups) + 1, lp_ref, all_lanes, F32)

    row = lambda n: pl.BlockSpec((tm, n), lambda i: (i, 0))
    whole = lambda a, b: pl.BlockSpec((a, b), lambda i: (0, 0))
    return pl.pallas_call(
        body, name="tail", grid=(nt,),
        out_shape=(jax.ShapeDtypeStruct((seq, attn_w), BF16), jax.ShapeDtypeStruct((seq, LANES), F32),
                   jax.ShapeDtypeStruct(_perm_shape(seq, attn_w), BF16), jax.ShapeDtypeStruct(_perm_shape(seq, LANES), F32),
                   jax.ShapeDtypeStruct(_perm_shape(seq, LANES), F32),
                   jax.ShapeDtypeStruct((seq, attn_w), BF16), jax.ShapeDtypeStruct((seq, conv_w), BF16),
                   jax.ShapeDtypeStruct((seq, conv_w), BF16), jax.ShapeDtypeStruct((seq, conv_w), F32),
                   jax.ShapeDtypeStruct((seq, d_model), F32), jax.ShapeDtypeStruct((mix, d_model), F32),
                   jax.ShapeDtypeStruct((SUBLANES, d_model), F32), jax.ShapeDtypeStruct((CONV_K * SUBLANES, conv_w), F32),
                   jax.ShapeDtypeStruct((SUBLANES, d_model), F32)),
        in_specs=[row(attn_w), row(LANES), row(attn_w), row(4 * conv_w),
                  pl.BlockSpec((SUBLANES, 4 * conv_w), lambda i: (jnp.maximum(i * hb - 1, 0), 0)),
                  row(d_model), row(d_model), _const_spec((mix, d_model)), _const_spec((1, d_model)),
                  _const_spec((SUBLANES, conv_w))],
        out_specs=(row(attn_w), row(LANES), _perm_tile_spec(attn_w, tm), _perm_tile_spec(LANES, tm), _perm_tile_spec(LANES, tm),
                   row(attn_w), row(conv_w), row(conv_w), row(conv_w), row(d_model),
                   whole(mix, d_model), whole(SUBLANES, d_model), whole(CONV_K * SUBLANES, conv_w),
                   whole(SUBLANES, d_model)),
        scratch_shapes=[pltpu.VMEM(_stage_shape(len(groups) + 2, tm), F32)],
        compiler_params=_params(("arbitrary",)),
    )(o, lse, ga, cz, cz, x, tgt, w_out, g2, cw)


def dz_dx(nat_grads, perm_grads, dga, dcb, dgc, dcv, cz, tables, x, g1, e, w_full, cw):
    seq, d_model = x.shape
    attn_w = dga.shape[1]
    conv_w = dcv.shape[1]
    width = w_full.shape[2]
    in_w = 4 * attn_w + 4 * conv_w
    groups = _lane_groups(attn_w)
    tm = ROW_TILE
    nt = seq // tm
    hb = tm // SUBLANES

    def body(dq_ref, dk_ref, dv_ref, dqp_ref, dkp_ref, dvp_ref, dga_ref, dcb_ref, dgc_ref, dcv_ref, nh_ref, ch_ref, cc_ref,
             cos_ref, s1_ref, s2_ref, x_ref, g_ref, e_ref, w_ref, cw_ref, gx_ref, dz_ref, dg_ref, stage):
        i = pl.program_id(0)

        @pl.when(i == 0)
        def _():
            dg_ref[...] = jnp.zeros_like(dg_ref)

        cos, s1, s2 = cos_ref[...], s1_ref[...], s2_ref[...]

        def qkv_columns(t, nat_ref, perm_ref):
            for g, sl in enumerate(groups):
                _from_perm(perm_ref, sl, stage, g)
            for g, sl in enumerate(groups):
                tot = nat_ref[:, sl].astype(F32) + _stage_get(stage, g)
                if t < 2:
                    tot = _rope_transposed(tot, cos, s1, s2)
                dz_ref[:, t * attn_w + g * LANES:t * attn_w + (g + 1) * LANES] = tot.astype(BF16)

        def dh_part(j):
            return _nt(dz_ref[:, j * width:(j + 1) * width], w_ref[j])

        dcv = dcv_ref[...]
        nh = jnp.where(i < nt - 1, nh_ref[...], 0.0)
        w0, w1, w2 = cw_ref[0:1, :], cw_ref[1:2, :], cw_ref[2:3, :]
        du = dcv * w2 + _shift_up(dcv, nh, 1) * w1 + _shift_up(dcv, nh, 2) * w0
        base = 4 * attn_w
        dz_ref[:, base:base + conv_w] = (du * cc_ref[...]).astype(BF16)
        dz_ref[:, base + conv_w:base + 2 * conv_w] = dcb_ref[...]
        dz_ref[:, base + 2 * conv_w:base + 3 * conv_w] = (du * ch_ref[...]).astype(BF16)
        dz_ref[:, base + 3 * conv_w:base + 4 * conv_w] = dgc_ref[...]
        dz_ref[:, 3 * attn_w:4 * attn_w] = dga_ref[...]
        ready = in_w
        dh = None
        for t, nat_ref, perm_ref in ((2, dv_ref, dvp_ref), (1, dk_ref, dkp_ref), (0, dq_ref, dqp_ref), (None, None, None)):
            lowest_open = 0 if t is None else (t + 1) * attn_w
            while ready - width >= lowest_open:
                ready -= width
                part = dh_part(ready // width)
                dh = part if dh is None else dh + part
            if t is not None:
                qkv_columns(t, nat_ref, perm_ref)
        xv = x_ref[...]
        r1 = lax.rsqrt(jnp.mean(xv * xv, axis=-1, keepdims=True) + NORM_EPS)
        xhat = xv * r1
        dg_ref[...] += _rowgroup_sum(dh * xhat)
        dhg = dh * g_ref[...]
        gx_ref[...] = r1 * (dhg - xhat * jnp.mean(dhg * xhat, axis=-1, keepdims=True)) + e_ref[...]

    row = lambda n: pl.BlockSpec((tm, n), lambda i: (i, 0))
    whole = lambda a, b: pl.BlockSpec((a, b), lambda i: (0, 0))
    pt = _perm_tile_spec(attn_w, tm)
    return pl.pallas_call(
        body, name="dz_dx", grid=(nt,),
        out_shape=(jax.ShapeDtypeStruct((seq, d_model), F32), jax.ShapeDtypeStruct((seq, in_w), BF16),
                   jax.ShapeDtypeStruct((SUBLANES, d_model), F32)),
        in_specs=[row(attn_w), row(attn_w), row(attn_w), pt, pt, pt, row(attn_w), row(conv_w), row(conv_w), row(conv_w),
                  pl.BlockSpec((SUBLANES, conv_w), lambda i: (jnp.minimum((i + 1) * hb, seq // SUBLANES - 1), 0)),
                  pl.BlockSpec((tm, conv_w), lambda i: (i, 0)), pl.BlockSpec((tm, conv_w), lambda i: (i, 2)),
                  row(LANES), row(LANES), row(LANES), row(d_model), _const_spec((1, d_model)), row(d_model),
                  _const_spec(w_full.shape), _const_spec((SUBLANES, conv_w))],
        out_specs=(row(d_model), row(in_w), whole(SUBLANES, d_model)),
        scratch_shapes=[pltpu.VMEM(_stage_shape(len(groups), tm), F32)],
        compiler_params=_params(("arbitrary",)),
    )(*nat_grads, *perm_grads, dga, dcb, dgc, dcv, dcv, cz, cz, *tables, x, g1, e, w_full, cw)


def dw_in_reduce(ht, dz, g_out, small):
    d_model, seq = ht.shape
    half = dz.shape[1] // N_DEV
    ts = min(DW_ROWS, seq)
    steps = seq // ts
    x, y, c = lax.axis_index("x"), lax.axis_index("y"), lax.axis_index("c")
    far_first = lambda x, y: [(1 - x, 1 - y), (1 - x, y), (x, 1 - y)]
    chips = jnp.stack([2 * px + py for px, py in far_first(x, y)] + [2 * x + y]).astype(jnp.int32)
    order = jnp.stack([2 * chips + (1 - c), 2 * chips + c], axis=1).reshape(N_DEV)

    def body(order_ref, ht_ref, dz_ref, go_ref, sm_ref, out_ref, ro_ref, rs_ref,
             acc, theirs, staged, contrib, resbuf, out_sem, sa, ra, sb, rb, sc, rc,
             o_mine, o_theirs, o_staged, o_contrib, o_res, sbuf, o_load, osa, ora, osb, orb, osc, orc, ss, rs):
        del order_ref
        p, s = pl.program_id(0), pl.program_id(1)
        x, y, c = lax.axis_index("x"), lax.axis_index("y"), lax.axis_index("c")
        me = 2 * x + y
        sib = (x, y, 1 - c)
        peers = far_first(x, y)
        slot = p % 2

        flips = [(fx, fy, fc) for fx in (0, 1) for fy in (0, 1) for fc in (0, 1)][1:]
        my8 = 4 * x + 2 * y + c
        chip_ids = [2 * px + py for px, py in peers] + [me]

        def small_copy(k, slot8, to):
            return pltpu.make_async_remote_copy(src_ref=sm_ref, dst_ref=sbuf.at[slot8], send_sem=ss.at[k], recv_sem=rs.at[k],
                                                device_id=to, device_id_type=MESH)

        def small_peer(k):
            fx, fy, fc = flips[k]
            return _flip(x, fx), _flip(y, fy), _flip(c, fc)

        def oa_copy(pos):
            j = chip_ids[pos]
            return pltpu.make_async_remote_copy(src_ref=go_ref.at[j, 1 - c], dst_ref=o_theirs.at[j], send_sem=osa.at[pos],
                                                recv_sem=ora.at[pos], device_id=sib, device_id_type=MESH)

        def o_load_copy(pos):
            j = chip_ids[pos]
            return pltpu.make_async_copy(go_ref.at[j, c], o_mine.at[j], o_load.at[pos])

        def ob_copy(k, piece, slot4):
            px, py = peers[k]
            return pltpu.make_async_remote_copy(src_ref=o_staged.at[piece], dst_ref=o_contrib.at[slot4], send_sem=osb.at[k],
                                                recv_sem=orb.at[k], device_id=(px, py, c), device_id_type=MESH)

        def oc_copy(which):
            return pltpu.make_async_remote_copy(src_ref=o_res.at[which], dst_ref=o_res.at[which], send_sem=osc, recv_sem=orc,
                                                device_id=sib, device_id_type=MESH)

        @pl.when((p == 0) & (s == 0))
        def _():
            sbuf[my8] = sm_ref[...]
            for k in range(N_DEV - 1):
                small_copy(k, my8, small_peer(k)).start()
            for pos in range(N_CHIPS):
                o_load_copy(pos).start()
                oa_copy(pos).start()

        @pl.when((p == 1) & (s == steps - 1))
        def _():
            for pos in range(N_CHIPS):
                j = chip_ids[pos]
                o_load_copy(pos).wait()
                oa_copy(pos).wait_recv()
                if pos < N_CHIPS - 1:
                    o_staged[j] = (o_mine[j] + o_theirs[j]).astype(BF16)
                    ob_copy(pos, j, me).start()
                else:
                    o_mine[j] = o_mine[j] + o_theirs[j]
                    o_contrib[j] = o_mine[j].astype(BF16)

        @pl.when((p == 4) & (s == steps - 1))
        def _():
            for k in range(N_CHIPS - 1):
                ob_copy(k, me, chip_ids[k]).wait_recv()
            own = o_mine[me]
            term = lambda j: jnp.where(me == j, own, o_contrib[j].astype(F32))
            o_res[c] = ((term(0) + term(1)) + term(2)) + term(3)
            oc_copy(c).start()

        def a_copy(k):
            return pltpu.make_async_remote_copy(src_ref=acc.at[0], dst_ref=theirs.at[k], send_sem=sa.at[k], recv_sem=ra.at[k],
                                                device_id=sib, device_id_type=MESH)

        def b_copy(k):
            px, py = peers[k]
            return pltpu.make_async_remote_copy(src_ref=staged.at[k], dst_ref=contrib.at[k], send_sem=sb.at[k], recv_sem=rb.at[k],
                                                device_id=(px, py, c), device_id_type=MESH)

        def c_copy(which):
            return pltpu.make_async_remote_copy(src_ref=resbuf.at[which], dst_ref=resbuf.at[which], send_sem=sc, recv_sem=rc,
                                                device_id=sib, device_id_type=MESH)

        @pl.when(s == 0)
        def _():
            for k in range(N_CHIPS - 1):
                @pl.when(p == 2 * k + 2)
                def _():
                    a_copy(k).wait_send()
            acc[slot] = jnp.dot(ht_ref[...], dz_ref[...], preferred_element_type=F32)

        @pl.when(s > 0)
        def _():
            acc[slot] += jnp.dot(ht_ref[...], dz_ref[...], preferred_element_type=F32)

        @pl.when(s == steps - 1)
        def _():
            for k in range(N_CHIPS):
                @pl.when(p == 2 * k)
                def _():
                    a_copy(k).start()
            for k in range(N_CHIPS - 1):
                @pl.when(p == 2 * k + 1)
                def _():
                    a_copy(k).wait_recv()
                    staged[k] = (acc[1] + theirs[k]).astype(BF16)
                    b_copy(k).start()

            @pl.when(p == N_DEV - 1)
            def _():
                a_copy(N_CHIPS - 1).wait_recv()
                tot = acc[1] + theirs[N_CHIPS - 1]
                for k in range(N_CHIPS - 1):
                    b_copy(k).wait_recv()
                    tot = tot + contrib[k].astype(F32)
                resbuf[c] = tot
                c_copy(c).start()
                c_copy(1 - c).wait_recv()
                done = pltpu.make_async_copy(resbuf, out_ref, out_sem)
                done.start()
                oc_copy(1 - c).wait_recv()
                ro_ref[...] = o_res[...]
                for k in range(N_DEV - 1):
                    px, py, pc = small_peer(k)
                    small_copy(k, 4 * px + 2 * py + pc, (px, py, pc)).wait_recv()
                tot8 = sbuf[0]
                for d in range(1, N_DEV):
                    tot8 = tot8 + sbuf[d]
                rs_ref[...] = tot8
                a_copy(N_CHIPS - 1).wait_send()
                for k in range(N_CHIPS - 1):
                    b_copy(k).wait_send()
                    ob_copy(k, chip_ids[k], me).wait_send()
                c_copy(c).wait_send()
                oc_copy(c).wait_send()
                for pos in range(N_CHIPS):
                    oa_copy(pos).wait_send()
                for k in range(N_DEV - 1):
                    small_copy(k, my8, small_peer(k)).wait_send()
                done.wait()

    dma = pltpu.SemaphoreType.DMA
    o_shape = g_out.shape[1:]
    go = g_out.reshape(N_CHIPS, 2, *o_shape)
    const = lambda shape: pl.BlockSpec(shape, lambda p, s, order_ref: (0,) * len(shape))
    grid_spec = pltpu.PrefetchScalarGridSpec(
        num_scalar_prefetch=1, grid=(N_DEV, steps),
        in_specs=[pl.BlockSpec((d_model, ts), lambda p, s, order_ref: (0, s)),
                  pl.BlockSpec((ts, half), lambda p, s, order_ref: (s, order_ref[p])),
                  pl.BlockSpec(memory_space=pl.ANY), const(small.shape)],
        out_specs=(pl.BlockSpec(memory_space=pl.ANY), const((2, *o_shape)), const(small.shape)),
        scratch_shapes=[pltpu.VMEM((2, d_model, half), F32), pltpu.VMEM((N_CHIPS, d_model, half), F32),
                        pltpu.VMEM((N_CHIPS - 1, d_model, half), BF16), pltpu.VMEM((N_CHIPS - 1, d_model, half), BF16),
                        pltpu.VMEM((2, d_model, half), F32), dma,
                        dma((N_CHIPS,)), dma((N_CHIPS,)), dma((N_CHIPS - 1,)), dma((N_CHIPS - 1,)), dma, dma,
                        pltpu.VMEM((N_CHIPS, *o_shape), F32), pltpu.VMEM((N_CHIPS, *o_shape), F32),
                        pltpu.VMEM((N_CHIPS, *o_shape), BF16), pltpu.VMEM((N_CHIPS, *o_shape), BF16),
                        pltpu.VMEM((2, *o_shape), F32), pltpu.VMEM((N_DEV, *small.shape), F32),
                        dma((N_CHIPS,)), dma((N_CHIPS,)), dma((N_CHIPS,)), dma((N_CHIPS - 1,)), dma((N_CHIPS - 1,)), dma, dma,
                        dma((N_DEV - 1,)), dma((N_DEV - 1,))])
    return pl.pallas_call(
        body, name="dw_in_reduce", grid_spec=grid_spec,
        out_shape=(jax.ShapeDtypeStruct((2, d_model, half), F32), jax.ShapeDtypeStruct((2, *o_shape), F32),
                   jax.ShapeDtypeStruct(small.shape, F32)),
        compiler_params=_params(("arbitrary", "arbitrary")),
    )(order, ht, dz, go, small)


def _adam_math(w, g, m, v):
    m = ADAM_B1 * m + (1.0 - ADAM_B1) * g
    v = ADAM_B2 * v + (1.0 - ADAM_B2) * (g * g)
    m_hat = m / (1.0 - ADAM_B1 ** ADAM_STEP)
    v_hat = v / (1.0 - ADAM_B2 ** ADAM_STEP)
    delta = -ADAM_LR * (m_hat / (jnp.sqrt(v_hat) + ADAM_EPS) + ADAM_WD * w)
    return delta, m, v


def adam_shard(name, w, g2, m, v, block, grid, w_map, g_map):
    def body(w_ref, g_ref, m_ref, v_ref, go_ref, d_ref, mo_ref, vo_ref):
        g = g_ref[0]
        delta, mn, vn = _adam_math(w_ref[...], g, m_ref[...], v_ref[...])
        go_ref[...] = g
        d_ref[...] = delta
        mo_ref[...] = mn
        vo_ref[...] = vn

    ws = pl.BlockSpec(block, w_map)
    shp = jax.ShapeDtypeStruct(w.shape, F32)
    return pl.pallas_call(
        body, name=name, grid=grid, out_shape=(shp, shp, shp, shp),
        in_specs=[ws, pl.BlockSpec((1, *block), g_map), ws, ws], out_specs=(ws, ws, ws, ws),
        compiler_params=_params(("arbitrary",) * len(grid)),
    )(w, g2, m, v)


def adam_small(ws, gs, ms, vs):
    n = len(ws)

    def body(*refs):
        ins, outs = refs[:4 * n], refs[4 * n:]
        for t in range(n):
            delta, mn, vn = _adam_math(ins[t][...], ins[n + t][...], ins[2 * n + t][...], ins[3 * n + t][...])
            outs[3 * t][...] = delta
            outs[3 * t + 1][...] = mn
            outs[3 * t + 2][...] = vn

    vm = pl.BlockSpec(memory_space=pltpu.VMEM)
    outs = pl.pallas_call(
        body, name="adam_small",
        out_shape=tuple(jax.ShapeDtypeStruct(w.shape, F32) for w in ws for _ in range(3)),
        in_specs=[vm] * (4 * n), out_specs=tuple([vm] * (3 * n)),
        compiler_params=_params(),
    )(*ws, *gs, *ms, *vs)
    return [outs[3 * t:3 * t + 3] for t in range(n)]


def kernel(x, norm_pre_g, w_in, conv_w, w_out, norm_post_g, loss_target, m_norm_pre_g, m_w_in, m_conv_w, m_w_out, m_norm_post_g, v_norm_pre_g, v_w_in, v_conv_w, v_w_out, v_norm_post_g):
    _, seq, d_model = x.shape
    width = w_in.shape[1]
    conv_q = conv_w.shape[1]
    conv_width = N_CHIPS * conv_q
    attn_width = d_model - conv_width
    xs, tg = x[0], loss_target[0]
    g1, g2 = norm_pre_g.reshape(1, d_model), norm_post_g.reshape(1, d_model)

    w_full, wout_full, cw_full, *tables = gather_weights(w_in, w_out, conv_w, seq)
    wout2 = wout_full.reshape(attn_width + conv_width, d_model)
    cw = jnp.zeros((SUBLANES, conv_width), F32).at[:CONV_K].set(
        cw_full[:, :CONV_K, :conv_q].transpose(1, 0, 2).reshape(CONV_K, conv_width))

    ht, q, k, v, qp, kp, vp, ga, cz = inproj(xs, g1, w_full, tables, attn_width, conv_width)
    run = attn_fwd("p4", qp, kp, vp, None)
    run = attn_fwd("p16", qp, kp, vp, run)
    o, lse = attn_fwd("nat", q, k, v, run)
    (d_o, delta, d_op, delta_p, lse_p, dga, dcb, dgc, dcv, e, dwout, dg2, dcw, loss_acc) = tail(
        o, lse, ga, cz, xs, tg, wout2, g2, cw)
    nat_grads = attn_bwd("nat", q, k, v, d_o, lse, delta, None)
    perm_grads = attn_bwd("p4", qp, kp, vp, d_op, lse_p, delta_p, None)
    perm_grads = attn_bwd("p16", qp, kp, vp, d_op, lse_p, delta_p, perm_grads)
    grad_x, dz, dg1 = dz_dx(nat_grads, perm_grads, dga, dcb, dgc, dcv, cz, tables, xs, g1, e, w_full, cw)

    small = jnp.zeros((SUBLANES, d_model), F32)
    small = small.at[0].set(dg1.sum(axis=0)).at[1].set(dg2.sum(axis=0))
    small = small.at[2:2 + CONV_K, :conv_width].set(dcw.reshape(CONV_K, SUBLANES, conv_width).sum(axis=1))
    small = small.at[2 + CONV_K, 0].set(jnp.sum(loss_acc))
    rin, rout, rsmall = dw_in_reduce(ht, dz, dwout.reshape(N_DEV, -1, d_model), small)

    half = width // 2
    tr = min(ADAM_ROWS, d_model)
    gw_in, d_in, m_in, v_in = adam_shard(
        "adam_w_in", w_in, rin, m_w_in, v_w_in, (tr, half), (2, d_model // tr),
        lambda hf, i: (i, hf), lambda hf, i: (hf, i, 0))
    rq = w_out.shape[0] // 2
    gw_out, d_out, m_out, v_out = adam_shard(
        "adam_w_out", w_out, rout, m_w_out, v_w_out, (rq, d_model), (2,),
        lambda hf: (hf, 0), lambda hf: (hf, 0, 0))

    chip = 2 * lax.axis_index("x") + lax.axis_index("y")
    g_pre, g_post = rsmall[0:1], rsmall[1:2]
    g_conv = lax.dynamic_slice(rsmall[2:2 + CONV_K, :conv_width], (0, chip * conv_q), (CONV_K, conv_q))
    (d_pre, m_pre, v_pre), (d_post, m_post, v_post), (d_cv, m_cv, v_cv) = adam_small(
        [g1, g2, conv_w], [g_pre, g_post, g_conv],
        [m_norm_pre_g.reshape(1, d_model), m_norm_post_g.reshape(1, d_model), m_conv_w],
        [v_norm_pre_g.reshape(1, d_model), v_norm_post_g.reshape(1, d_model), v_conv_w])

    loss = 0.5 * rsmall[2 + CONV_K, 0] / d_model
    vec = lambda a: a.reshape(d_model)
    return (loss, grad_x.reshape(1, seq, d_model),
            vec(g_pre), gw_in, g_conv, gw_out, vec(g_post),
            vec(d_pre), d_in, d_cv, d_out, vec(d_post),
            vec(m_pre), m_in, m_cv, m_out, vec(m_post),
            vec(v_pre), v_in, v_cv, v_out, vec(v_post))
```

```python
import jax
import jax.numpy as jnp
from jax import lax
from jax.experimental import pallas as pl
from jax.experimental.pallas import tpu as pltpu

HEAD_DIM = 64
LANES = 128
SUBLANES = 8
BLOCK = 128
HALF_BLOCK = BLOCK // 2
WINDOW_KEYS = 128
PERM = 16
PJ = 4
P4_ROWS = BLOCK // PJ
MAX_QUERY_BLOCKS = 8
ROW_TILE = 512
DW_ROWS = 2048
ADAM_ROWS = 1024
CONV_K = 3
ROPE_THETA = 10000.0
NORM_EPS = 1e-6
ATTN_SCALE = HEAD_DIM ** -0.5
NEG = -1e30
N_CHIPS = 4
N_DEV = 8
MESH = pl.DeviceIdType.MESH
ADAM_LR = 0.001
ADAM_B1 = 0.9
ADAM_B2 = 0.999
ADAM_EPS = 1e-08
ADAM_WD = 0.01
ADAM_STEP = 10
VMEM_LIMIT = 52 * 1024 * 1024

F32 = jnp.float32
BF16 = jnp.bfloat16


def _params(sem=None, **kw):
    return pltpu.CompilerParams(dimension_semantics=sem, vmem_limit_bytes=VMEM_LIMIT, **kw)


def _const_spec(shape):
    return pl.BlockSpec(shape, lambda *_: (0,) * len(shape), pipeline_mode=pl.Buffered(1))


def _sigmoid(z):
    return 1.0 / (1.0 + jnp.exp(-z))


def _rowgroup_sum(a):
    rows, n = a.shape
    return a.reshape(rows // SUBLANES, SUBLANES, n).sum(axis=0)


def _nt(a, b):
    return lax.dot_general(a, b, (((1,), (1,)), ((), ())), preferred_element_type=F32)


def _tn(a, b):
    return lax.dot_general(a, b, (((0,), (0,)), ((), ())), preferred_element_type=F32)


def _col_pieces(a, b, width):
    out = []
    while a < b:
        j = a // width
        e = min(b, (j + 1) * width)
        out.append((j, a - j * width, e - j * width))
        a = e
    return out


def _lane_groups(width):
    return [slice(g * LANES, (g + 1) * LANES) for g in range(width // LANES)]


def _perm_shape(seq, width):
    return (PJ, PJ, seq // PERM, width)


def _perm_tile_spec(width, tm):
    return pl.BlockSpec((PJ, PJ, tm // PERM, width), lambda i: (0, 0, i, 0))


STAGE_PITCH = 24


def _stage_shape(groups, rows):
    return (groups, rows // PERM * STAGE_PITCH, LANES)


def _stage_put(stage, g, val, row0=0):
    for a in range(val.shape[0] // PERM):
        at = (row0 // PERM + a) * STAGE_PITCH
        stage[g, at:at + PERM, :] = val[a * PERM:(a + 1) * PERM]


def _stage_get(stage, g):
    return jnp.concatenate([stage[g, a * STAGE_PITCH:a * STAGE_PITCH + PERM, :]
                            for a in range(stage.shape[1] // STAGE_PITCH)], axis=0)


def _to_perm(stage, g, dst_ref, sl, dtype):
    rows = stage.shape[1] // STAGE_PITCH
    for b in range(PERM):
        dst_ref[b // PJ, b % PJ, :, sl] = stage[g, pl.ds(b, rows, stride=STAGE_PITCH), :].astype(dtype)


def _from_perm(src_ref, sl, stage, g):
    rows = stage.shape[1] // STAGE_PITCH
    for b in range(PERM):
        stage[g, pl.ds(b, rows, stride=STAGE_PITCH), :] = src_ref[b // PJ, b % PJ, :, sl].astype(F32)


def _flip(a, f):
    return 1 - a if f else a


def gather_weights(w_in, w_out, conv_w, seq):
    d_model, width = w_in.shape
    rows = w_out.shape[0]
    cw = jnp.zeros((SUBLANES, LANES), F32).at[:CONV_K, :conv_w.shape[1]].set(conv_w)
    half_dim = HEAD_DIM // 2
    inv_freq = ROPE_THETA ** (-jnp.arange(half_dim, dtype=F32) * 2.0 / HEAD_DIM)
    inv_freq = jnp.tile(inv_freq, LANES // half_dim).reshape(1, LANES)
    chunk = min(ROW_TILE, seq)

    def body(win_ref, wout_ref, cw_ref, freq_ref, winf_ref, woutf_ref, cwf_ref, cos_ref, s1_ref, s2_ref,
             st_in, st_out, near_send, near_recv, far_send, far_recv, cw_send, cw_recv, d2d_send, d2d_recv):
        x, y, c = lax.axis_index("x"), lax.axis_index("y"), lax.axis_index("c")
        me = 2 * x + y
        sib = (x, y, 1 - c)
        st_in[...] = win_ref[...].astype(BF16)
        st_out[...] = wout_ref[...].astype(BF16)
        winf_ref[me] = st_in[...]
        woutf_ref[me] = st_out[...]
        cwf_ref[me] = cw_ref[...]
        stages = (st_in, st_out)
        fulls = (winf_ref, woutf_ref)
        halves = (d_model // 2, rows // 2)

        def part(t, core, q=None):
            size = halves[t] if q is None else halves[t] // 2
            start = core * halves[t] if q is None else core * halves[t] + q * size
            return pl.ds(pl.multiple_of(start, size), size)

        near = [(1 - x, y), (x, 1 - y)]
        far = (1 - x, 1 - y)
        chip = lambda px, py: 2 * px + py

        def direct(k, t, q, slot, to):
            src = stages[t].at[part(t, c, q)]
            return pltpu.make_async_remote_copy(src_ref=src, dst_ref=fulls[t].at[slot, part(t, c, q)], send_sem=near_send.at[k, t, q],
                                                recv_sem=near_recv.at[k, t, q], device_id=to, device_id_type=MESH)

        def passed_on(k, t, slot, to):
            ref = fulls[t].at[slot, part(t, c, k)]
            return pltpu.make_async_remote_copy(src_ref=ref, dst_ref=ref, send_sem=far_send.at[k, t], recv_sem=far_recv.at[k, t],
                                                device_id=to, device_id_type=MESH)

        def conv_copy(k, slot, to):
            return pltpu.make_async_remote_copy(src_ref=cw_ref, dst_ref=cwf_ref.at[slot], send_sem=cw_send.at[k], recv_sem=cw_recv.at[k],
                                                device_id=to, device_id_type=MESH)

        def d2d(k, t, slot, core):
            ref = fulls[t].at[slot, part(t, core)]
            return pltpu.make_async_remote_copy(src_ref=ref, dst_ref=ref, send_sem=d2d_send.at[k, t], recv_sem=d2d_recv.at[k, t],
                                                device_id=sib, device_id_type=MESH)

        sends = []

        def go(cp):
            cp.start()
            sends.append(cp)

        for q_first in (0, 1):
            for k, (px, py) in enumerate(near):
                for t in range(2):
                    go(direct(k, t, k if q_first == 0 else 1 - k, me, (px, py, c)))
        for k, (px, py) in enumerate(near + [far]):
            go(conv_copy(k, me, (px, py, c)))
        for k, (px, py) in enumerate(near):
            other = near[1 - k]
            for t in range(2):
                direct(k, t, k, chip(px, py), (px, py, c)).wait_recv()
                go(passed_on(k, t, chip(px, py), (*other, c)))

        first_half = lax.broadcasted_iota(jnp.int32, (chunk, LANES), 1) % HEAD_DIM < half_dim
        row = lax.broadcasted_iota(jnp.int32, (chunk, LANES), 0)

        def table_rows(i, carry):
            at = pl.multiple_of(i * chunk, chunk)
            ang = (row + at).astype(F32) * freq_ref[...]
            sin = jnp.sin(ang)
            cos_ref[pl.ds(at, chunk), :] = jnp.cos(ang)
            s1_ref[pl.ds(at, chunk), :] = jnp.where(first_half, -sin, 0.0)
            s2_ref[pl.ds(at, chunk), :] = jnp.where(first_half, 0.0, sin)
            return carry

        lax.fori_loop(0, seq // chunk, table_rows, 0)

        for k, (px, py) in enumerate(near):
            for t in range(2):
                direct(k, t, 1 - k, chip(px, py), (px, py, c)).wait_recv()
                go(d2d(k, t, chip(px, py), c))
        for t in range(2):
            for k, (px, py) in enumerate(near):
                passed_on(k, t, chip(*far), (px, py, c)).wait_recv()
            go(d2d(2, t, chip(*far), c))
        for k, (px, py) in enumerate(near + [far]):
            conv_copy(k, chip(px, py), (px, py, c)).wait_recv()
            for t in range(2):
                d2d(k, t, chip(px, py), 1 - c).wait_recv()
        for cp in sends:
            cp.wait_send()

    vm = pl.BlockSpec(memory_space=pltpu.VMEM)
    dma = pltpu.SemaphoreType.DMA
    return pl.pallas_call(
        body, name="gather_weights",
        out_shape=(jax.ShapeDtypeStruct((N_CHIPS, d_model, width), BF16),
                   jax.ShapeDtypeStruct((N_CHIPS, rows, d_model), BF16),
                   jax.ShapeDtypeStruct((N_CHIPS, SUBLANES, LANES), F32),
                   *[jax.ShapeDtypeStruct((seq, LANES), F32)] * 3),
        in_specs=[vm, vm, vm, vm], out_specs=(vm,) * 6,
        scratch_shapes=[pltpu.VMEM((d_model, width), BF16), pltpu.VMEM((rows, d_model), BF16),
                        dma((2, 2, 2)), dma((2, 2, 2)), dma((2, 2)), dma((2, 2)), dma((3,)), dma((3,)),
                        dma((3, 2)), dma((3, 2))],
        compiler_params=_params(),
    )(w_in, w_out, cw, inv_freq)


def _rope(t, cos, s1, s2):
    return t * cos + pltpu.roll(t, LANES - HEAD_DIM // 2, 1) * s1 + pltpu.roll(t, HEAD_DIM // 2, 1) * s2


def _rope_transposed(g, cos, s1, s2):
    return g * cos + pltpu.roll(g * s1, HEAD_DIM // 2, 1) + pltpu.roll(g * s2, LANES - HEAD_DIM // 2, 1)


def inproj(x, g1, w_full, tables, attn_w, conv_w):
    seq, d_model = x.shape
    width = w_full.shape[2]
    tm = ROW_TILE
    groups = _lane_groups(attn_w)

    def body(x_ref, g_ref, w_ref, cos_ref, s1_ref, s2_ref,
             ht_ref, q_ref, k_ref, v_ref, qp_ref, kp_ref, vp_ref, ga_ref, cz_ref, stage):
        xv = x_ref[...]
        hb = ((xv * lax.rsqrt(jnp.mean(xv * xv, axis=-1, keepdims=True) + NORM_EPS)) * g_ref[...]).astype(BF16)
        ht_ref[...] = jnp.transpose(hb)
        cos, s1, s2 = cos_ref[...], s1_ref[...], s2_ref[...]

        def proj(a, b):
            parts = [jnp.dot(hb, w_ref[j, :, lo:hi], preferred_element_type=F32) for j, lo, hi in _col_pieces(a, b, width)]
            return parts[0] if len(parts) == 1 else jnp.concatenate(parts, axis=1)

        def emit(z, nat_ref, perm_ref, fn):
            for g, sl in enumerate(groups):
                val = fn(z[:, sl])
                nat_ref[:, sl] = val.astype(BF16)
                _stage_put(stage, g, val)
            for g, sl in enumerate(groups):
                _to_perm(stage, g, perm_ref, sl, BF16)

        emit(proj(0, attn_w), q_ref, qp_ref, lambda t: _rope(t, cos, s1, s2) * ATTN_SCALE)
        emit(proj(attn_w, 2 * attn_w), k_ref, kp_ref, lambda t: _rope(t, cos, s1, s2))
        emit(proj(2 * attn_w, 3 * attn_w), v_ref, vp_ref, lambda t: t)
        ga_ref[...] = proj(3 * attn_w, 4 * attn_w)
        cz_ref[...] = proj(4 * attn_w, 4 * attn_w + 4 * conv_w)

    row = lambda n: pl.BlockSpec((tm, n), lambda i: (i, 0))
    nat = jax.ShapeDtypeStruct((seq, attn_w), BF16)
    perm = jax.ShapeDtypeStruct(_perm_shape(seq, attn_w), BF16)
    return pl.pallas_call(
        body, name="inproj", grid=(seq // tm,),
        out_shape=(jax.ShapeDtypeStruct((d_model, seq), BF16), nat, nat, nat, perm, perm, perm,
                   jax.ShapeDtypeStruct((seq, attn_w), F32), jax.ShapeDtypeStruct((seq, 4 * conv_w), F32)),
        in_specs=[row(d_model), _const_spec((1, d_model)), _const_spec(w_full.shape), row(LANES), row(LANES), row(LANES)],
        out_specs=(pl.BlockSpec((d_model, tm), lambda i: (0, i)), row(attn_w), row(attn_w), row(attn_w),
                   _perm_tile_spec(attn_w, tm), _perm_tile_spec(attn_w, tm), _perm_tile_spec(attn_w, tm),
                   row(attn_w), row(4 * conv_w)),
        scratch_shapes=[pltpu.VMEM(_stage_shape(len(groups), tm), F32)],
        compiler_params=_params(("arbitrary",)),
    )(x, g1, w_full, *tables)


class _Mode:
    def __init__(self, name, seq):
        self.name = name
        if name == "nat":
            self.residues, blocks = 1, seq // BLOCK
        elif name == "p16":
            self.residues, blocks = PERM, seq // PERM // BLOCK
        else:
            self.residues, blocks = PJ, seq // PERM // P4_ROWS
        self.qb = max(d for d in range(1, MAX_QUERY_BLOCKS + 1) if blocks % d == 0)
        self.steps = blocks // self.qb

    def _spec(self, blocks, width, at):
        if self.name == "nat":
            return pl.BlockSpec((blocks * BLOCK, width), lambda *g: (at(*g)[1], 0))
        if self.name == "p16":
            return pl.BlockSpec((1, 1, blocks * BLOCK, width), lambda *g: (at(*g)[0] // PJ, at(*g)[0] % PJ, at(*g)[1], 0))
        return pl.BlockSpec((PJ, 1, blocks * P4_ROWS, width), lambda *g: (0, at(*g)[0], at(*g)[1], 0))

    def wide(self, width, where=lambda r, n: (r, n)):
        return self._spec(self.qb, width, where)

    def block_before(self, width, where=lambda r, n: (r, n)):
        return self._spec(1, width, lambda *g: (where(*g)[0], jnp.maximum(self.qb * where(*g)[1] - 1, 0)))

    def get(self, ref, sl, sub=0):
        if self.name == "nat":
            return ref[sub * BLOCK:(sub + 1) * BLOCK, sl]
        if self.name == "p16":
            return ref[0, 0, sub * BLOCK:(sub + 1) * BLOCK, sl]
        return jnp.concatenate([ref[j, 0, at:at + P4_ROWS // 2, sl] for j, at in self._p4_chunks(sub)], axis=0)

    def put(self, ref, sl, val, sub=0):
        val = val.astype(ref.dtype)
        if self.name == "nat":
            ref[sub * BLOCK:(sub + 1) * BLOCK, sl] = val
        elif self.name == "p16":
            ref[0, 0, sub * BLOCK:(sub + 1) * BLOCK, sl] = val
        else:
            for i, (j, at) in enumerate(self._p4_chunks(sub)):
                ref[j, 0, at:at + P4_ROWS // 2, sl] = val[i * (P4_ROWS // 2):(i + 1) * (P4_ROWS // 2)]

    @staticmethod
    def _p4_chunks(sub):
        return [(j, sub * P4_ROWS + half * (P4_ROWS // 2)) for half in (0, 1) for j in range(PJ)]

    def keys(self, before_ref, wide_ref, sl, sub):
        older = self.get(before_ref, sl) if sub == 0 else self.get(wide_ref, sl, sub - 1)
        return jnp.concatenate([older, self.get(wide_ref, sl, sub)], axis=0)

    def index(self, idx, is_key):
        if self.name != "p4":
            return idx - BLOCK if is_key else idx
        within = jnp.bitwise_and(idx, BLOCK - 1)
        chunk = P4_ROWS // 2
        half = jnp.right_shift(within, HALF_BLOCK.bit_length() - 1)
        j = jnp.bitwise_and(jnp.right_shift(within, chunk.bit_length() - 1), PJ - 1)
        m = PJ * (chunk * half + jnp.bitwise_and(within, chunk - 1)) + j
        return m + BLOCK * (jnp.right_shift(idx, BLOCK.bit_length() - 1) - 1) if is_key else m

    def bias(self, has_before):
        shape = (2 * BLOCK, BLOCK)
        kidx = lax.broadcasted_iota(jnp.int32, shape, 0)
        qidx = lax.broadcasted_iota(jnp.int32, shape, 1)
        rel = self.index(qidx, False) - self.index(kidx, True)
        valid = (rel >= 0) & (rel <= WINDOW_KEYS)
        if has_before is not True:
            valid = valid & ((kidx >= BLOCK) | has_before)
        one = jnp.where(valid, 0.0, NEG)
        return jnp.concatenate([one, one], axis=1)

    def live_keys(self, half):
        return (0, 2 * BLOCK - HALF_BLOCK) if half == 0 else (HALF_BLOCK, 2 * BLOCK)

    def half_bias(self, has_before, half):
        r0, r1 = self.live_keys(half)
        shape = (r1 - r0, LANES)
        kidx = lax.broadcasted_iota(jnp.int32, shape, 0) + r0
        qidx = jnp.bitwise_and(lax.broadcasted_iota(jnp.int32, shape, 1), HALF_BLOCK - 1) + half * HALF_BLOCK
        rel = self.index(qidx, False) - self.index(kidx, True)
        valid = (rel >= 0) & (rel <= WINDOW_KEYS)
        if has_before is not True:
            valid = valid & ((kidx >= BLOCK) | has_before)
        return jnp.where(valid, 0.0, NEG)


def _head_masks():
    lane = lax.broadcasted_iota(jnp.int32, (BLOCK, LANES), 1)
    lo = lane < HEAD_DIM
    return lane, lo, jnp.where(lo, 1.0, 0.0).astype(BF16), jnp.where(lo, 0.0, 1.0).astype(BF16)


def attn_fwd(name, q, k, v, run):
    nat = name == "nat"
    seq = q.shape[0] if nat else q.shape[2] * PERM
    attn_w = q.shape[-1]
    mode = _Mode(name, seq)
    groups = _lane_groups(attn_w)
    first = run is None
    all_lanes = slice(0, LANES)

    def body(*refs):
        q_ref, kp_ref, kc_ref, vp_ref, vc_ref = refs[:5]
        if first:
            o_ref, l_ref = refs[5:]
        elif nat:
            oin_ref, lin_ref, o_ref, l_ref, ostage, lstage = refs[5:]
        else:
            oin_ref, lin_ref, o_ref, l_ref = refs[5:]
        n = pl.program_id(1)
        subs = range(mode.qb)
        halves = (0, 1)
        live = [mode.live_keys(x) for x in halves]
        always = [mode.half_bias(True, x) for x in halves]
        biases = [[mode.half_bias(n > 0, x) for x in halves]] + [always] * (mode.qb - 1)
        _, lo, m_lo, m_hi = _head_masks()
        head_row = lax.broadcasted_iota(jnp.int32, (BLOCK, LANES), 0)
        ones = jnp.ones((2 * BLOCK, LANES), BF16)
        hb = HALF_BLOCK
        lrows = [jnp.zeros((BLOCK, LANES), F32) for _ in subs]
        if not first:
            if nat:
                for g, sl in enumerate(groups):
                    _from_perm(oin_ref, sl, ostage, g)
                _from_perm(lin_ref, all_lanes, lstage, 0)
            wide_rows = lambda a, sub: a[sub * BLOCK:(sub + 1) * BLOCK]
            before = [jnp.transpose(wide_rows(_stage_get(lstage, 0), sub) if nat else mode.get(lin_ref, all_lanes, sub))
                      for sub in subs]

        def probs(sub, p, sl):
            q2 = mode.get(q_ref, sl, sub)
            kcat = mode.keys(kp_ref, kc_ref, sl, sub)
            vcat = mode.keys(vp_ref, vc_ref, sl, sub)
            q_lo, q_hi = q2 * m_lo, q2 * m_hi
            qq = jnp.concatenate([q_lo[:hb], q_hi[:hb], q_lo[hb:], q_hi[hb:]], axis=0)
            s_t = _nt(kcat, qq)
            columns, lses = [], []
            for x in halves:
                r0, r1 = live[x]
                s_x = s_t[r0:r1, x * LANES:(x + 1) * LANES] + biases[sub][x]
                m = jnp.max(s_x, axis=0, keepdims=True)
                pe = jnp.exp(s_x - m)
                lse = m + jnp.log(jnp.sum(pe, axis=0, keepdims=True))
                if not first:
                    was = jnp.concatenate([before[sub][2 * p:2 * p + 1, x * hb:(x + 1) * hb],
                                           before[sub][2 * p + 1:2 * p + 2, x * hb:(x + 1) * hb]], axis=1)
                    top = jnp.maximum(was, lse)
                    lse = top + jnp.log(jnp.exp(was - top) + jnp.exp(lse - top))
                    pe = pe * jnp.exp(m - lse)
                pieces = [pe.astype(BF16)]
                if r0 > 0:
                    pieces.insert(0, jnp.zeros((r0, LANES), BF16))
                if r1 < 2 * BLOCK:
                    pieces.append(jnp.zeros((2 * BLOCK - r1, LANES), BF16))
                columns.append(pieces[0] if len(pieces) == 1 else jnp.concatenate(pieces, axis=0))
                lses.append(lse)
            return jnp.concatenate([vcat, ones], axis=1), jnp.concatenate(columns, axis=1), lses

        def output(sub, p, sl, vext, pb, lses):
            o_ext = _tn(pb, vext)
            if first:
                o_new = o_ext[:, :LANES] / o_ext[:, LANES:]
            else:
                o_prev = wide_rows(_stage_get(ostage, p), sub) if nat else mode.get(oin_ref, sl, sub)
                same = jnp.concatenate([o_prev[:hb], o_prev[:hb], o_prev[hb:], o_prev[hb:]], axis=0)
                o_new = o_ext[:, :LANES] + same * (1.0 - o_ext[:, LANES:])
            head_lo = jnp.concatenate([o_new[:hb], o_new[2 * hb:3 * hb]], axis=0)
            head_hi = jnp.concatenate([o_new[hb:2 * hb], o_new[3 * hb:]], axis=0)
            mode.put(o_ref, sl, jnp.where(lo, head_lo, head_hi), sub)
            lse_lo = jnp.concatenate([lses[0][:, :hb], lses[1][:, :hb]], axis=1)
            lse_hi = jnp.concatenate([lses[0][:, hb:], lses[1][:, hb:]], axis=1)
            rows = jnp.where(head_row == 2 * p, lse_lo, lrows[sub])
            lrows[sub] = jnp.where(head_row == 2 * p + 1, lse_hi, rows)

        pending = None
        for sub in subs:
            for p, sl in enumerate(groups):
                nxt = probs(sub, p, sl)
                if pending is not None:
                    output(*pending)
                pending = (sub, p, sl, *nxt)
        output(*pending)
        for sub in subs:
            mode.put(l_ref, all_lanes, jnp.transpose(lrows[sub]), sub)

    ins = [q, k, k, v, v]
    specs = [mode.wide(attn_w), mode.block_before(attn_w), mode.wide(attn_w), mode.block_before(attn_w), mode.wide(attn_w)]
    scratch = []
    if not first:
        ins += list(run)
        if nat:
            rows_a = mode.qb * BLOCK // PERM
            specs += [pl.BlockSpec((PJ, PJ, rows_a, attn_w), lambda r, n: (0, 0, n, 0)),
                      pl.BlockSpec((PJ, PJ, rows_a, LANES), lambda r, n: (0, 0, n, 0))]
            scratch = [pltpu.VMEM(_stage_shape(len(groups), mode.qb * BLOCK), F32),
                       pltpu.VMEM(_stage_shape(1, mode.qb * BLOCK), F32)]
        else:
            specs += [mode.wide(attn_w), mode.wide(LANES)]
    if nat:
        out_shape = (jax.ShapeDtypeStruct((seq, attn_w), F32), jax.ShapeDtypeStruct((seq, LANES), F32))
    else:
        out_shape = (jax.ShapeDtypeStruct(_perm_shape(seq, attn_w), F32), jax.ShapeDtypeStruct(_perm_shape(seq, LANES), F32))
    return pl.pallas_call(
        body, name=f"attn_fwd_{name}", grid=(mode.residues, mode.steps),
        out_shape=out_shape, in_specs=specs, out_specs=(mode.wide(attn_w), mode.wide(LANES)),
        scratch_shapes=scratch,
        compiler_params=_params(("arbitrary", "arbitrary")),
    )(*ins)


def attn_bwd(name, q, k, v, d_o, lse, delta, run):
    nat = name == "nat"
    seq = q.shape[0] if nat else q.shape[2] * PERM
    attn_w = q.shape[-1]
    mode = _Mode(name, seq)
    steps, qb = mode.steps, mode.qb
    single = steps == 1
    groups = _lane_groups(attn_w)
    first = run is None
    all_lanes = slice(0, LANES)

    def body(*refs):
        q_ref, kp_ref, kc_ref, vp_ref, vc_ref, do_ref, lse_ref, dl_ref = refs[:8]
        if first:
            dq_ref, dk_ref, dv_ref, ck, cv = refs[8:]
        else:
            dqi_ref, dki_ref, dvi_ref, dq_ref, dk_ref, dv_ref, ck, cv = refs[8:]
        g = pl.program_id(1) if single else pl.program_id(0)
        n = g if single else lax.rem(g, steps)
        carries = ((ck, dk_ref, None if first else dki_ref), (cv, dv_ref, None if first else dvi_ref))

        def emit(out_ref, acc_ref, sl, sub, val):
            if acc_ref is not None:
                val = val + mode.get(acc_ref, sl, sub).astype(F32)
            mode.put(out_ref, sl, val, sub)

        if not single:
            @pl.when(g == 0)
            def _():
                ck[...] = jnp.zeros_like(ck)
                cv[...] = jnp.zeros_like(cv)

        @pl.when(g < total)
        def _():
            biases = [mode.bias(n > 0)] + [mode.bias(True)] * (qb - 1)
            _, lo, m_lo, m_hi = _head_masks()

            def scores(sub, p, sl, lse_t, dl_t):
                q2, do2 = mode.get(q_ref, sl, sub), mode.get(do_ref, sl, sub)
                kcat = mode.keys(kp_ref, kc_ref, sl, sub)
                vcat = mode.keys(vp_ref, vc_ref, sl, sub)
                qq = jnp.concatenate([q2 * m_lo, q2 * m_hi], axis=0)
                dd = jnp.concatenate([do2 * m_lo, do2 * m_hi], axis=0)
                h0 = 2 * p
                lse2 = jnp.concatenate([lse_t[h0:h0 + 1, :], lse_t[h0 + 1:h0 + 2, :]], axis=1)
                dl2 = jnp.concatenate([dl_t[h0:h0 + 1, :], dl_t[h0 + 1:h0 + 2, :]], axis=1)
                p_t = jnp.exp(_nt(kcat, qq) + (biases[sub] - lse2))
                ds_t = p_t * (_nt(vcat, dd) - dl2)
                return qq, dd, kcat, p_t.astype(BF16), ds_t.astype(BF16)

            def grads(sub, sl, qq, dd, kcat, pb, dsb):
                dqb = _tn(dsb, kcat)
                dq2 = jnp.where(lo, dqb[:BLOCK], dqb[BLOCK:]) * ATTN_SCALE
                if not first:
                    dq2 = dq2 + mode.get(dqi_ref, sl, sub).astype(F32)
                mode.put(dq_ref, sl, dq2, sub)
                for (carry, out_ref, acc_ref), lhs, rhs in zip(carries, (dsb, pb), (qq, dd)):
                    both = jnp.dot(lhs, rhs, preferred_element_type=F32)
                    if sub == 0:
                        if not single:
                            for s in range(qb - 1):
                                emit(out_ref, acc_ref, sl, s, carry[s, :, sl])
                            emit(out_ref, acc_ref, sl, qb - 1, carry[qb - 1, :, sl] + both[:BLOCK])
                        carry[0, :, sl] = both[BLOCK:]
                    else:
                        carry[sub - 1, :, sl] += both[:BLOCK]
                        carry[sub, :, sl] = both[BLOCK:]
                    if single and sub == qb - 1:
                        for s in range(qb):
                            emit(out_ref, acc_ref, sl, s, carry[s, :, sl])

            stats = [(jnp.transpose(mode.get(lse_ref, all_lanes, sub)),
                      jnp.transpose(mode.get(dl_ref, all_lanes, sub))) for sub in range(qb)]
            pending = None
            for p, sl in enumerate(groups):
                for sub in range(qb):
                    nxt = scores(sub, p, sl, *stats[sub])
                    if pending is not None:
                        grads(*pending)
                    pending = (sub, sl, *nxt)
            grads(*pending)

        if not single:
            @pl.when(g == total)
            def _():
                for carry, out_ref, acc_ref in carries:
                    for sl in groups:
                        for s in range(qb):
                            emit(out_ref, acc_ref, sl, s, carry[s, :, sl])

    total = mode.residues * steps
    if single:
        here = before = lambda r, n: (r, n)
    else:
        locate = lambda g: (g // steps, lax.rem(g, steps))
        here = lambda g: locate(jnp.minimum(g, total - 1))
        before = lambda g: locate(jnp.maximum(g - 1, 0))
    wide = lambda w: mode.wide(w, here)
    ins = [q, k, k, v, v, d_o, lse, delta]
    specs = [wide(attn_w), mode.block_before(attn_w, here), wide(attn_w), mode.block_before(attn_w, here), wide(attn_w),
             wide(attn_w), wide(LANES), wide(LANES)]
    if not first:
        ins += list(run)
        specs += [wide(attn_w), mode.wide(attn_w, before), mode.wide(attn_w, before)]
    shp = jax.ShapeDtypeStruct((seq, attn_w) if nat else _perm_shape(seq, attn_w), BF16)
    grid = (mode.residues, 1) if single else (total + 1,)
    return pl.pallas_call(
        body, name=f"attn_bwd_{name}", grid=grid,
        out_shape=(shp, shp, shp), in_specs=specs,
        out_specs=(wide(attn_w), mode.wide(attn_w, before), mode.wide(attn_w, before)),
        scratch_shapes=[pltpu.VMEM((qb, BLOCK, attn_w), F32), pltpu.VMEM((qb, BLOCK, attn_w), F32)],
        compiler_params=_params(("arbitrary",) * len(grid)),
    )(*ins)


def _shift_down(u, halo, k):
    rolled = pltpu.roll(u, k, 0)
    row = lax.broadcasted_iota(jnp.int32, halo.shape, 0)
    top = jnp.where(row < k, pltpu.roll(halo, k, 0), rolled[:SUBLANES])
    return jnp.concatenate([top, rolled[SUBLANES:]], axis=0)


def _shift_up(u, halo, k):
    rows = u.shape[0]
    rolled = pltpu.roll(u, rows - k, 0)
    row = lax.broadcasted_iota(jnp.int32, halo.shape, 0)
    bot = jnp.where(row >= SUBLANES - k, pltpu.roll(halo, SUBLANES - k, 0), rolled[rows - SUBLANES:])
    return jnp.concatenate([rolled[:rows - SUBLANES], bot], axis=0)


def tail(o, lse, ga, cz, x, tgt, w_out, g2, cw):
    seq, d_model = x.shape
    attn_w = o.shape[1]
    conv_w = cz.shape[1] // 4
    mix = attn_w + conv_w
    groups = _lane_groups(attn_w)
    tm = ROW_TILE
    nt = seq // tm
    hb = tm // SUBLANES

    def body(o_ref, l_ref, ga_ref, cz_ref, hz_ref, x_ref, t_ref, w_ref, g_ref, cw_ref,
             do_ref, dl_ref, dop_ref, dlp_ref, lp_ref, dga_ref, dcb_ref, dgc_ref, dcv_ref, e_ref,
             dw_ref, dg_ref, dcw_ref, loss_ref, stage):
        i = pl.program_id(0)

        @pl.when(i == 0)
        def _():
            dw_ref[...] = jnp.zeros_like(dw_ref)
            dg_ref[...] = jnp.zeros_like(dg_ref)
            dcw_ref[...] = jnp.zeros_like(dcw_ref)
            loss_ref[...] = jnp.zeros_like(loss_ref)

        u = cz_ref[:, 2 * conv_w:3 * conv_w] * cz_ref[:, 0:conv_w]
        uh = hz_ref[:, 2 * conv_w:3 * conv_w] * hz_ref[:, 0:conv_w]
        uh = jnp.where(i > 0, uh, 0.0)
        u1 = _shift_down(u, uh, 1)
        u2 = _shift_down(u, uh, 2)
        w0, w1, w2 = cw_ref[0:1, :], cw_ref[1:2, :], cw_ref[2:3, :]
        cvv = u2 * w0 + u1 * w1 + u * w2
        gv = g_ref[...]
        all_lanes = slice(0, LANES)

        def forward(rs):
            ov, gav = o_ref[rs, :], ga_ref[rs, :]
            sig_a = _sigmoid(gav)
            silu_a = gav * sig_a
            cb, gc = cz_ref[rs, conv_w:2 * conv_w], cz_ref[rs, 3 * conv_w:4 * conv_w]
            sig_c = _sigmoid(gc)
            silu_c = gc * sig_c
            bc = cb * cvv[rs]
            mixed = jnp.concatenate([ov * silu_a, bc * silu_c], axis=1).astype(BF16)
            yv = jnp.dot(mixed, w_ref[...], preferred_element_type=F32)
            return ov, gav, sig_a, silu_a, cb, gc, sig_c, silu_c, bc, mixed, yv

        def loss_and_dy(rs, mixed, yv):
            r2 = lax.rsqrt(jnp.mean(yv * yv, axis=-1, keepdims=True) + NORM_EPS)
            yhat = yv * r2
            diff = (x_ref[rs, :] + yhat * gv) - t_ref[rs, :]
            loss_ref[...] += _rowgroup_sum(diff * diff)
            ev = diff * (1.0 / d_model)
            e_ref[rs, :] = ev
            dg_ref[...] += _rowgroup_sum(ev * yhat)
            eg = ev * gv
            dy = (r2 * (eg - yhat * jnp.mean(eg * yhat, axis=-1, keepdims=True))).astype(BF16)
            dw_ref[...] += _tn(mixed, dy)
            return _nt(dy, w_ref[...])

        def backward(rs, ov, gav, sig_a, silu_a, cb, gc, sig_c, silu_c, bc, dm):
            rows = rs.stop - rs.start
            dma, dmc = dm[:, :attn_w], dm[:, attn_w:]
            dov = dma * silu_a
            do_ref[rs, :] = dov.astype(BF16)
            dga_ref[rs, :] = (dma * ov * (sig_a * (1.0 + gav * (1.0 - sig_a)))).astype(BF16)
            prod = dov * ov
            lane = lax.broadcasted_iota(jnp.int32, (rows, LANES), 1)
            lo = lane < HEAD_DIM
            dblk = jnp.zeros((rows, LANES), F32)
            for p, sl in enumerate(groups):
                pr = prod[:, sl]
                dblk = jnp.where(lane == 2 * p, jnp.sum(jnp.where(lo, pr, 0.0), axis=1, keepdims=True), dblk)
                dblk = jnp.where(lane == 2 * p + 1, jnp.sum(jnp.where(lo, 0.0, pr), axis=1, keepdims=True), dblk)
                _stage_put(stage, p, dov[:, sl], rs.start)
            dl_ref[rs, :] = dblk
            _stage_put(stage, len(groups), dblk, rs.start)
            _stage_put(stage, len(groups) + 1, l_ref[rs, :], rs.start)
            dsc = dmc * silu_c
            cv_rows = cvv[rs]
            dcb_ref[rs, :] = (dsc * cv_rows).astype(BF16)
            dgc_ref[rs, :] = (dmc * bc * (sig_c * (1.0 + gc * (1.0 - sig_c)))).astype(BF16)
            dcv = dsc * cb
            dcv_ref[rs, :] = dcv
            dcw_ref[0:SUBLANES, :] += _rowgroup_sum(dcv * u2[rs])
            dcw_ref[SUBLANES:2 * SUBLANES, :] += _rowgroup_sum(dcv * u1[rs])
            dcw_ref[2 * SUBLANES:3 * SUBLANES, :] += _rowgroup_sum(dcv * u[rs])

        halves = [slice(0, tm // 2), slice(tm // 2, tm)]
        fwd = [forward(rs) for rs in halves]
        dms = [loss_and_dy(rs, f[9], f[10]) for rs, f in zip(halves, fwd)]
        for rs, f, dm in zip(halves, fwd, dms):
            backward(rs, *f[:9], dm)
        for p, sl in enumerate(groups):
            _to_perm(stage, p, dop_ref, sl, BF16)
        _to_perm(stage, len(groups), dlp_ref, all_lanes, F32)
        _to_perm(stage, len(groups) + 1, lp_ref, all_lanes, F32)

    row = lambda n: pl.BlockSpec((tm, n), lambda i: (i, 0))
    whole = lambda a, b: pl.BlockSpec((a, b), lambda i: (0, 0))
    return pl.pallas_call(
        body, name="tail", grid=(nt,),
        out_shape=(jax.ShapeDtypeStruct((seq, attn_w), BF16), jax.ShapeDtypeStruct((seq, LANES), F32),
                   jax.ShapeDtypeStruct(_perm_shape(seq, attn_w), BF16), jax.ShapeDtypeStruct(_perm_shape(seq, LANES), F32),
                   jax.ShapeDtypeStruct(_perm_shape(seq, LANES), F32),
                   jax.ShapeDtypeStruct((seq, attn_w), BF16), jax.ShapeDtypeStruct((seq, conv_w), BF16),
                   jax.ShapeDtypeStruct((seq, conv_w), BF16), jax.ShapeDtypeStruct((seq, conv_w), F32),
                   jax.ShapeDtypeStruct((seq, d_model), F32), jax.ShapeDtypeStruct((mix, d_model), F32),
                   jax.ShapeDtypeStruct((SUBLANES, d_model), F32), jax.ShapeDtypeStruct((CONV_K * SUBLANES, conv_w), F32),
                   jax.ShapeDtypeStruct((SUBLANES, d_model), F32)),
        in_specs=[row(attn_w), row(LANES), row(attn_w), row(4 * conv_w),
                  pl.BlockSpec((SUBLANES, 4 * conv_w), lambda i: (jnp.maximum(i * hb - 1, 0), 0)),
                  row(d_model), row(d_model), _const_spec((mix, d_model)), _const_spec((1, d_model)),
                  _const_spec((SUBLANES, conv_w))],
        out_specs=(row(attn_w), row(LANES), _perm_tile_spec(attn_w, tm), _perm_tile_spec(LANES, tm), _perm_tile_spec(LANES, tm),
                   row(attn_w), row(conv_w), row(conv_w), row(conv_w), row(d_model),
                   whole(mix, d_model), whole(SUBLANES, d_model), whole(CONV_K * SUBLANES, conv_w),
                   whole(SUBLANES, d_model)),
        scratch_shapes=[pltpu.VMEM(_stage_shape(len(groups) + 2, tm), F32)],
        compiler_params=_params(("arbitrary",)),
    )(o, lse, ga, cz, cz, x, tgt, w_out, g2, cw)


def dz_dx(nat_grads, perm_grads, dga, dcb, dgc, dcv, cz, tables, x, g1, e, w_full, cw):
    seq, d_model = x.shape
    attn_w = dga.shape[1]
    conv_w = dcv.shape[1]
    width = w_full.shape[2]
    in_w = 4 * attn_w + 4 * conv_w
    groups = _lane_groups(attn_w)
    tm = ROW_TILE
    nt = seq // tm
    hb = tm // SUBLANES

    def body(dq_ref, dk_ref, dv_ref, dqp_ref, dkp_ref, dvp_ref, dga_ref, dcb_ref, dgc_ref, dcv_ref, nh_ref, ch_ref, cc_ref,
             cos_ref, s1_ref, s2_ref, x_ref, g_ref, e_ref, w_ref, cw_ref, gx_ref, dz_ref, dg_ref, stage):
        i = pl.program_id(0)

        @pl.when(i == 0)
        def _():
            dg_ref[...] = jnp.zeros_like(dg_ref)

        cos, s1, s2 = cos_ref[...], s1_ref[...], s2_ref[...]

        def qkv_columns(t, nat_ref, perm_ref):
            for g, sl in enumerate(groups):
                _from_perm(perm_ref, sl, stage, g)
            for g, sl in enumerate(groups):
                tot = nat_ref[:, sl].astype(F32) + _stage_get(stage, g)
                if t < 2:
                    tot = _rope_transposed(tot, cos, s1, s2)
                dz_ref[:, t * attn_w + g * LANES:t * attn_w + (g + 1) * LANES] = tot.astype(BF16)

        def dh_part(j):
            return _nt(dz_ref[:, j * width:(j + 1) * width], w_ref[j])

        dcv = dcv_ref[...]
        nh = jnp.where(i < nt - 1, nh_ref[...], 0.0)
        w0, w1, w2 = cw_ref[0:1, :], cw_ref[1:2, :], cw_ref[2:3, :]
        du = dcv * w2 + _shift_up(dcv, nh, 1) * w1 + _shift_up(dcv, nh, 2) * w0
        base = 4 * attn_w
        dz_ref[:, base:base + conv_w] = (du * cc_ref[...]).astype(BF16)
        dz_ref[:, base + conv_w:base + 2 * conv_w] = dcb_ref[...]
        dz_ref[:, base + 2 * conv_w:base + 3 * conv_w] = (du * ch_ref[...]).astype(BF16)
        dz_ref[:, base + 3 * conv_w:base + 4 * conv_w] = dgc_ref[...]
        dz_ref[:, 3 * attn_w:4 * attn_w] = dga_ref[...]
        ready = in_w
        dh = None
        for t, nat_ref, perm_ref in ((2, dv_ref, dvp_ref), (1, dk_ref, dkp_ref), (0, dq_ref, dqp_ref), (None, None, None)):
            lowest_open = 0 if t is None else (t + 1) * attn_w
            while ready - width >= lowest_open:
                ready -= width
                part = dh_part(ready // width)
                dh = part if dh is None else dh + part
            if t is not None:
                qkv_columns(t, nat_ref, perm_ref)
        xv = x_ref[...]
        r1 = lax.rsqrt(jnp.mean(xv * xv, axis=-1, keepdims=True) + NORM_EPS)
        xhat = xv * r1
        dg_ref[...] += _rowgroup_sum(dh * xhat)
        dhg = dh * g_ref[...]
        gx_ref[...] = r1 * (dhg - xhat * jnp.mean(dhg * xhat, axis=-1, keepdims=True)) + e_ref[...]

    row = lambda n: pl.BlockSpec((tm, n), lambda i: (i, 0))
    whole = lambda a, b: pl.BlockSpec((a, b), lambda i: (0, 0))
    pt = _perm_tile_spec(attn_w, tm)
    return pl.pallas_call(
        body, name="dz_dx", grid=(nt,),
        out_shape=(jax.ShapeDtypeStruct((seq, d_model), F32), jax.ShapeDtypeStruct((seq, in_w), BF16),
                   jax.ShapeDtypeStruct((SUBLANES, d_model), F32)),
        in_specs=[row(attn_w), row(attn_w), row(attn_w), pt, pt, pt, row(attn_w), row(conv_w), row(conv_w), row(conv_w),
                  pl.BlockSpec((SUBLANES, conv_w), lambda i: (jnp.minimum((i + 1) * hb, seq // SUBLANES - 1), 0)),
                  pl.BlockSpec((tm, conv_w), lambda i: (i, 0)), pl.BlockSpec((tm, conv_w), lambda i: (i, 2)),
                  row(LANES), row(LANES), row(LANES), row(d_model), _const_spec((1, d_model)), row(d_model),
                  _const_spec(w_full.shape), _const_spec((SUBLANES, conv_w))],
        out_specs=(row(d_model), row(in_w), whole(SUBLANES, d_model)),
        scratch_shapes=[pltpu.VMEM(_stage_shape(len(groups), tm), F32)],
        compiler_params=_params(("arbitrary",)),
    )(*nat_grads, *perm_grads, dga, dcb, dgc, dcv, dcv, cz, cz, *tables, x, g1, e, w_full, cw)


def dw_in_reduce(ht, dz, g_out, small):
    d_model, seq = ht.shape
    half = dz.shape[1] // N_DEV
    ts = min(DW_ROWS, seq)
    steps = seq // ts
    x, y, c = lax.axis_index("x"), lax.axis_index("y"), lax.axis_index("c")
    far_first = lambda x, y: [(1 - x, 1 - y), (1 - x, y), (x, 1 - y)]
    chips = jnp.stack([2 * px + py for px, py in far_first(x, y)] + [2 * x + y]).astype(jnp.int32)
    order = jnp.stack([2 * chips + (1 - c), 2 * chips + c], axis=1).reshape(N_DEV)

    def body(order_ref, ht_ref, dz_ref, go_ref, sm_ref, out_ref, ro_ref, rs_ref,
             acc, theirs, staged, contrib, resbuf, out_sem, sa, ra, sb, rb, sc, rc,
             o_mine, o_theirs, o_staged, o_contrib, o_res, sbuf, o_load, osa, ora, osb, orb, osc, orc, ss, rs):
        del order_ref
        p, s = pl.program_id(0), pl.program_id(1)
        x, y, c = lax.axis_index("x"), lax.axis_index("y"), lax.axis_index("c")
        me = 2 * x + y
        sib = (x, y, 1 - c)
        peers = far_first(x, y)
        slot = p % 2

        flips = [(fx, fy, fc) for fx in (0, 1) for fy in (0, 1) for fc in (0, 1)][1:]
        my8 = 4 * x + 2 * y + c
        chip_ids = [2 * px + py for px, py in peers] + [me]

        def small_copy(k, slot8, to):
            return pltpu.make_async_remote_copy(src_ref=sm_ref, dst_ref=sbuf.at[slot8], send_sem=ss.at[k], recv_sem=rs.at[k],
                                                device_id=to, device_id_type=MESH)

        def small_peer(k):
            fx, fy, fc = flips[k]
            return _flip(x, fx), _flip(y, fy), _flip(c, fc)

        def oa_copy(pos):
            j = chip_ids[pos]
            return pltpu.make_async_remote_copy(src_ref=go_ref.at[j, 1 - c], dst_ref=o_theirs.at[j], send_sem=osa.at[pos],
                                                recv_sem=ora.at[pos], device_id=sib, device_id_type=MESH)

        def o_load_copy(pos):
            j = chip_ids[pos]
            return pltpu.make_async_copy(go_ref.at[j, c], o_mine.at[j], o_load.at[pos])

        def ob_copy(k, piece, slot4):
            px, py = peers[k]
            return pltpu.make_async_remote_copy(src_ref=o_staged.at[piece], dst_ref=o_contrib.at[slot4], send_sem=osb.at[k],
                                                recv_sem=orb.at[k], device_id=(px, py, c), device_id_type=MESH)

        def oc_copy(which):
            return pltpu.make_async_remote_copy(src_ref=o_res.at[which], dst_ref=o_res.at[which], send_sem=osc, recv_sem=orc,
                                                device_id=sib, device_id_type=MESH)

        @pl.when((p == 0) & (s == 0))
        def _():
            sbuf[my8] = sm_ref[...]
            for k in range(N_DEV - 1):
                small_copy(k, my8, small_peer(k)).start()
            for pos in range(N_CHIPS):
                o_load_copy(pos).start()
                oa_copy(pos).start()

        @pl.when((p == 1) & (s == steps - 1))
        def _():
            for pos in range(N_CHIPS):
                j = chip_ids[pos]
                o_load_copy(pos).wait()
                oa_copy(pos).wait_recv()
                if pos < N_CHIPS - 1:
                    o_staged[j] = (o_mine[j] + o_theirs[j]).astype(BF16)
                    ob_copy(pos, j, me).start()
                else:
                    o_mine[j] = o_mine[j] + o_theirs[j]
                    o_contrib[j] = o_mine[j].astype(BF16)

        @pl.when((p == 4) & (s == steps - 1))
        def _():
            for k in range(N_CHIPS - 1):
                ob_copy(k, me, chip_ids[k]).wait_recv()
            own = o_mine[me]
            term = lambda j: jnp.where(me == j, own, o_contrib[j].astype(F32))
            o_res[c] = ((term(0) + term(1)) + term(2)) + term(3)
            oc_copy(c).start()

        def a_copy(k):
            return pltpu.make_async_remote_copy(src_ref=acc.at[0], dst_ref=theirs.at[k], send_sem=sa.at[k], recv_sem=ra.at[k],
                                                device_id=sib, device_id_type=MESH)

        def b_copy(k):
            px, py = peers[k]
            return pltpu.make_async_remote_copy(src_ref=staged.at[k], dst_ref=contrib.at[k], send_sem=sb.at[k], recv_sem=rb.at[k],
                                                device_id=(px, py, c), device_id_type=MESH)

        def c_copy(which):
            return pltpu.make_async_remote_copy(src_ref=resbuf.at[which], dst_ref=resbuf.at[which], send_sem=sc, recv_sem=rc,
                                                device_id=sib, device_id_type=MESH)

        @pl.when(s == 0)
        def _():
            for k in range(N_CHIPS - 1):
                @pl.when(p == 2 * k + 2)
                def _():
                    a_copy(k).wait_send()
            acc[slot] = jnp.dot(ht_ref[...], dz_ref[...], preferred_element_type=F32)

        @pl.when(s > 0)
        def _():
            acc[slot] += jnp.dot(ht_ref[...], dz_ref[...], preferred_element_type=F32)

        @pl.when(s == steps - 1)
        def _():
            for k in range(N_CHIPS):
                @pl.when(p == 2 * k)
                def _():
                    a_copy(k).start()
            for k in range(N_CHIPS - 1):
                @pl.when(p == 2 * k + 1)
                def _():
                    a_copy(k).wait_recv()
                    staged[k] = (acc[1] + theirs[k]).astype(BF16)
                    b_copy(k).start()

            @pl.when(p == N_DEV - 1)
            def _():
                a_copy(N_CHIPS - 1).wait_recv()
                tot = acc[1] + theirs[N_CHIPS - 1]
                for k in range(N_CHIPS - 1):
                    b_copy(k).wait_recv()
                    tot = tot + contrib[k].astype(F32)
                resbuf[c] = tot
                c_copy(c).start()
                c_copy(1 - c).wait_recv()
                done = pltpu.make_async_copy(resbuf, out_ref, out_sem)
                done.start()
                oc_copy(1 - c).wait_recv()
                ro_ref[...] = o_res[...]
                for k in range(N_DEV - 1):
                    px, py, pc = small_peer(k)
                    small_copy(k, 4 * px + 2 * py + pc, (px, py, pc)).wait_recv()
                tot8 = sbuf[0]
                for d in range(1, N_DEV):
                    tot8 = tot8 + sbuf[d]
                rs_ref[...] = tot8
                a_copy(N_CHIPS - 1).wait_send()
                for k in range(N_CHIPS - 1):
                    b_copy(k).wait_send()
                    ob_copy(k, chip_ids[k], me).wait_send()
                c_copy(c).wait_send()
                oc_copy(c).wait_send()
                for pos in range(N_CHIPS):
                    oa_copy(pos).wait_send()
                for k in range(N_DEV - 1):
                    small_copy(k, my8, small_peer(k)).wait_send()
                done.wait()

    dma = pltpu.SemaphoreType.DMA
    o_shape = g_out.shape[1:]
    go = g_out.reshape(N_CHIPS, 2, *o_shape)
    const = lambda shape: pl.BlockSpec(shape, lambda p, s, order_ref: (0,) * len(shape))
    grid_spec = pltpu.PrefetchScalarGridSpec(
        num_scalar_prefetch=1, grid=(N_DEV, steps),
        in_specs=[pl.BlockSpec((d_model, ts), lambda p, s, order_ref: (0, s)),
                  pl.BlockSpec((ts, half), lambda p, s, order_ref: (s, order_ref[p])),
                  pl.BlockSpec(memory_space=pl.ANY), const(small.shape)],
        out_specs=(pl.BlockSpec(memory_space=pl.ANY), const((2, *o_shape)), const(small.shape)),
        scratch_shapes=[pltpu.VMEM((2, d_model, half), F32), pltpu.VMEM((N_CHIPS, d_model, half), F32),
                        pltpu.VMEM((N_CHIPS - 1, d_model, half), BF16), pltpu.VMEM((N_CHIPS - 1, d_model, half), BF16),
                        pltpu.VMEM((2, d_model, half), F32), dma,
                        dma((N_CHIPS,)), dma((N_CHIPS,)), dma((N_CHIPS - 1,)), dma((N_CHIPS - 1,)), dma, dma,
                        pltpu.VMEM((N_CHIPS, *o_shape), F32), pltpu.VMEM((N_CHIPS, *o_shape), F32),
                        pltpu.VMEM((N_CHIPS, *o_shape), BF16), pltpu.VMEM((N_CHIPS, *o_shape), BF16),
                        pltpu.VMEM((2, *o_shape), F32), pltpu.VMEM((N_DEV, *small.shape), F32),
                        dma((N_CHIPS,)), dma((N_CHIPS,)), dma((N_CHIPS,)), dma((N_CHIPS - 1,)), dma((N_CHIPS - 1,)), dma, dma,
                        dma((N_DEV - 1,)), dma((N_DEV - 1,))])
    return pl.pallas_call(
        body, name="dw_in_reduce", grid_spec=grid_spec,
        out_shape=(jax.ShapeDtypeStruct((2, d_model, half), F32), jax.ShapeDtypeStruct((2, *o_shape), F32),
                   jax.ShapeDtypeStruct(small.shape, F32)),
        compiler_params=_params(("arbitrary", "arbitrary")),
    )(order, ht, dz, go, small)


def _adam_math(w, g, m, v):
    m = ADAM_B1 * m + (1.0 - ADAM_B1) * g
    v = ADAM_B2 * v + (1.0 - ADAM_B2) * (g * g)
    m_hat = m / (1.0 - ADAM_B1 ** ADAM_STEP)
    v_hat = v / (1.0 - ADAM_B2 ** ADAM_STEP)
    delta = -ADAM_LR * (m_hat / (jnp.sqrt(v_hat) + ADAM_EPS) + ADAM_WD * w)
    return delta, m, v


def adam_shard(name, w, g2, m, v, block, grid, w_map, g_map):
    def body(w_ref, g_ref, m_ref, v_ref, go_ref, d_ref, mo_ref, vo_ref):
        g = g_ref[0]
        delta, mn, vn = _adam_math(w_ref[...], g, m_ref[...], v_ref[...])
        go_ref[...] = g
        d_ref[...] = delta
        mo_ref[...] = mn
        vo_ref[...] = vn

    ws = pl.BlockSpec(block, w_map)
    shp = jax.ShapeDtypeStruct(w.shape, F32)
    return pl.pallas_call(
        body, name=name, grid=grid, out_shape=(shp, shp, shp, shp),
        in_specs=[ws, pl.BlockSpec((1, *block), g_map), ws, ws], out_specs=(ws, ws, ws, ws),
        compiler_params=_params(("arbitrary",) * len(grid)),
    )(w, g2, m, v)


def adam_small(ws, gs, ms, vs):
    n = len(ws)

    def body(*refs):
        ins, outs = refs[:4 * n], refs[4 * n:]
        for t in range(n):
            delta, mn, vn = _adam_math(ins[t][...], ins[n + t][...], ins[2 * n + t][...], ins[3 * n + t][...])
            outs[3 * t][...] = delta
            outs[3 * t + 1][...] = mn
            outs[3 * t + 2][...] = vn

    vm = pl.BlockSpec(memory_space=pltpu.VMEM)
    outs = pl.pallas_call(
        body, name="adam_small",
        out_shape=tuple(jax.ShapeDtypeStruct(w.shape, F32) for w in ws for _ in range(3)),
        in_specs=[vm] * (4 * n), out_specs=tuple([vm] * (3 * n)),
        compiler_params=_params(),
    )(*ws, *gs, *ms, *vs)
    return [outs[3 * t:3 * t + 3] for t in range(n)]


def kernel(x, norm_pre_g, w_in, conv_w, w_out, norm_post_g, loss_target, m_norm_pre_g, m_w_in, m_conv_w, m_w_out, m_norm_post_g, v_norm_pre_g, v_w_in, v_conv_w, v_w_out, v_norm_post_g):
    _, seq, d_model = x.shape
    width = w_in.shape[1]
    conv_q = conv_w.shape[1]
    conv_width = N_CHIPS * conv_q
    attn_width = d_model - conv_width
    xs, tg = x[0], loss_target[0]
    g1, g2 = norm_pre_g.reshape(1, d_model), norm_post_g.reshape(1, d_model)

    w_full, wout_full, cw_full, *tables = gather_weights(w_in, w_out, conv_w, seq)
    wout2 = wout_full.reshape(attn_width + conv_width, d_model)
    cw = jnp.zeros((SUBLANES, conv_width), F32).at[:CONV_K].set(
        cw_full[:, :CONV_K, :conv_q].transpose(1, 0, 2).reshape(CONV_K, conv_width))

    ht, q, k, v, qp, kp, vp, ga, cz = inproj(xs, g1, w_full, tables, attn_width, conv_width)
    run = attn_fwd("p4", qp, kp, vp, None)
    run = attn_fwd("p16", qp, kp, vp, run)
    o, lse = attn_fwd("nat", q, k, v, run)
    (d_o, delta, d_op, delta_p, lse_p, dga, dcb, dgc, dcv, e, dwout, dg2, dcw, loss_acc) = tail(
        o, lse, ga, cz, xs, tg, wout2, g2, cw)
    nat_grads = attn_bwd("nat", q, k, v, d_o, lse, delta, None)
    perm_grads = attn_bwd("p4", qp, kp, vp, d_op, lse_p, delta_p, None)
    perm_grads = attn_bwd("p16", qp, kp, vp, d_op, lse_p, delta_p, perm_grads)
    grad_x, dz, dg1 = dz_dx(nat_grads, perm_grads, dga, dcb, dgc, dcv, cz, tables, xs, g1, e, w_full, cw)

    small = jnp.zeros((SUBLANES, d_model), F32)
    small = small.at[0].set(dg1.sum(axis=0)).at[1].set(dg2.sum(axis=0))
    small = small.at[2:2 + CONV_K, :conv_width].set(dcw.reshape(CONV_K, SUBLANES, conv_width).sum(axis=1))
    small = small.at[2 + CONV_K, 0].set(jnp.sum(loss_acc))
    rin, rout, rsmall = dw_in_reduce(ht, dz, dwout.reshape(N_DEV, -1, d_model), small)

    half = width // 2
    tr = min(ADAM_ROWS, d_model)
    gw_in, d_in, m_in, v_in = adam_shard(
        "adam_w_in", w_in, rin, m_w_in, v_w_in, (tr, half), (2, d_model // tr),
        lambda hf, i: (i, hf), lambda hf, i: (hf, i, 0))
    rq = w_out.shape[0] // 2
    gw_out, d_out, m_out, v_out = adam_shard(
        "adam_w_out", w_out, rout, m_w_out, v_w_out, (rq, d_model), (2,),
        lambda hf: (hf, 0), lambda hf: (hf, 0, 0))

    chip = 2 * lax.axis_index("x") + lax.axis_index("y")
    g_pre, g_post = rsmall[0:1], rsmall[1:2]
    g_conv = lax.dynamic_slice(rsmall[2:2 + CONV_K, :conv_width], (0, chip * conv_q), (CONV_K, conv_q))
    (d_pre, m_pre, v_pre), (d_post, m_post, v_post), (d_cv, m_cv, v_cv) = adam_small(
        [g1, g2, conv_w], [g_pre, g_post, g_conv],
        [m_norm_pre_g.reshape(1, d_model), m_norm_post_g.reshape(1, d_model), m_conv_w],
        [v_norm_pre_g.reshape(1, d_model), v_norm_post_g.reshape(1, d_model), v_conv_w])

    loss = 0.5 * rsmall[2 + CONV_K, 0] / d_model
    vec = lambda a: a.reshape(d_model)
    return (loss, grad_x.reshape(1, seq, d_model),
            vec(g_pre), gw_in, g_conv, gw_out, vec(g_post),
            vec(d_pre), d_in, d_cv, d_out, vec(d_post),
            vec(m_pre), m_in, m_cv, m_out, vec(m_post),
            vec(v_pre), v_in, v_cv, v_out, vec(v_post))
```

```python
import jax
import jax.numpy as jnp
from jax import lax
from jax.experimental import pallas as pl
from jax.experimental.pallas import tpu as pltpu

HEAD_DIM = 64
LANES = 128
SUBLANES = 8
BLOCK = 128
HALF_BLOCK = BLOCK // 2
WINDOW_KEYS = 128
PERM = 16
PJ = 4
P4_ROWS = BLOCK // PJ
MAX_QUERY_BLOCKS = 8
ROW_TILE = 512
DW_ROWS = 4096
ADAM_ROWS = 1024
CONV_K = 3
ROPE_THETA = 10000.0
NORM_EPS = 1e-6
ATTN_SCALE = HEAD_DIM ** -0.5
NEG = -1e30
N_CHIPS = 4
N_DEV = 8
MESH = pl.DeviceIdType.MESH
ADAM_LR = 0.001
ADAM_B1 = 0.9
ADAM_B2 = 0.999
ADAM_EPS = 1e-08
ADAM_WD = 0.01
ADAM_STEP = 10
VMEM_LIMIT = 60 * 1024 * 1024

F32 = jnp.float32
BF16 = jnp.bfloat16


def _params(sem=None, **kw):
    return pltpu.CompilerParams(dimension_semantics=sem, vmem_limit_bytes=VMEM_LIMIT, **kw)


def _const_spec(shape):
    return pl.BlockSpec(shape, lambda *_: (0,) * len(shape), pipeline_mode=pl.Buffered(1))


def _sigmoid(z):
    return 1.0 / (1.0 + jnp.exp(-z))


def _rowgroup_sum(a):
    rows, n = a.shape
    return a.reshape(rows // SUBLANES, SUBLANES, n).sum(axis=0)


def _nt(a, b):
    return lax.dot_general(a, b, (((1,), (1,)), ((), ())), preferred_element_type=F32)


def _tn(a, b):
    return lax.dot_general(a, b, (((0,), (0,)), ((), ())), preferred_element_type=F32)


def _col_pieces(a, b, width):
    out = []
    while a < b:
        j = a // width
        e = min(b, (j + 1) * width)
        out.append((j, a - j * width, e - j * width))
        a = e
    return out


def _lane_groups(width):
    return [slice(g * LANES, (g + 1) * LANES) for g in range(width // LANES)]


def _perm_shape(seq, width):
    return (PJ, PJ, seq // PERM, width)


def _perm_tile_spec(width, tm):
    return pl.BlockSpec((PJ, PJ, tm // PERM, width), lambda i: (0, 0, i, 0))


STAGE_PITCH = 24


def _stage_shape(groups, rows):
    return (groups, rows // PERM * STAGE_PITCH, LANES)


def _stage_put(stage, g, val, row0=0):
    for a in range(val.shape[0] // PERM):
        at = (row0 // PERM + a) * STAGE_PITCH
        stage[g, at:at + PERM, :] = val[a * PERM:(a + 1) * PERM]


def _stage_get(stage, g):
    return jnp.concatenate([stage[g, a * STAGE_PITCH:a * STAGE_PITCH + PERM, :]
                            for a in range(stage.shape[1] // STAGE_PITCH)], axis=0)


def _to_perm(stage, g, dst_ref, sl, dtype):
    rows = stage.shape[1] // STAGE_PITCH
    for b in range(PERM):
        dst_ref[b // PJ, b % PJ, :, sl] = stage[g, pl.ds(b, rows, stride=STAGE_PITCH), :].astype(dtype)


def _from_perm(src_ref, sl, stage, g):
    rows = stage.shape[1] // STAGE_PITCH
    for b in range(PERM):
        stage[g, pl.ds(b, rows, stride=STAGE_PITCH), :] = src_ref[b // PJ, b % PJ, :, sl].astype(F32)


def _flip(a, f):
    return 1 - a if f else a


def gather_weights(w_in, w_out, conv_w, seq):
    d_model, width = w_in.shape
    rows = w_out.shape[0]
    cw = jnp.zeros((SUBLANES, LANES), F32).at[:CONV_K, :conv_w.shape[1]].set(conv_w)
    half_dim = HEAD_DIM // 2
    inv_freq = ROPE_THETA ** (-jnp.arange(half_dim, dtype=F32) * 2.0 / HEAD_DIM)
    inv_freq = jnp.tile(inv_freq, LANES // half_dim).reshape(1, LANES)
    chunk = min(ROW_TILE, seq)

    def body(win_ref, wout_ref, cw_ref, freq_ref, winf_ref, woutf_ref, cwf_ref, cos_ref, s1_ref, s2_ref,
             st_in, st_out, near_send, near_recv, far_send, far_recv, cw_send, cw_recv, d2d_send, d2d_recv):
        x, y, c = lax.axis_index("x"), lax.axis_index("y"), lax.axis_index("c")
        me = 2 * x + y
        sib = (x, y, 1 - c)
        st_in[...] = win_ref[...].astype(BF16)
        st_out[...] = wout_ref[...].astype(BF16)
        winf_ref[me] = st_in[...]
        woutf_ref[me] = st_out[...]
        cwf_ref[me] = cw_ref[...]
        stages = (st_in, st_out)
        fulls = (winf_ref, woutf_ref)
        halves = (d_model // 2, rows // 2)

        def part(t, core, q=None):
            size = halves[t] if q is None else halves[t] // 2
            start = core * halves[t] if q is None else core * halves[t] + q * size
            return pl.ds(pl.multiple_of(start, size), size)

        near = [(1 - x, y), (x, 1 - y)]
        far = (1 - x, 1 - y)
        chip = lambda px, py: 2 * px + py

        def direct(k, t, q, slot, to):
            src = stages[t].at[part(t, c, q)]
            return pltpu.make_async_remote_copy(src_ref=src, dst_ref=fulls[t].at[slot, part(t, c, q)], send_sem=near_send.at[k, t, q],
                                                recv_sem=near_recv.at[k, t, q], device_id=to, device_id_type=MESH)

        def passed_on(k, t, slot, to):
            ref = fulls[t].at[slot, part(t, c, k)]
            return pltpu.make_async_remote_copy(src_ref=ref, dst_ref=ref, send_sem=far_send.at[k, t], recv_sem=far_recv.at[k, t],
                                                device_id=to, device_id_type=MESH)

        def conv_copy(k, slot, to):
            return pltpu.make_async_remote_copy(src_ref=cw_ref, dst_ref=cwf_ref.at[slot], send_sem=cw_send.at[k], recv_sem=cw_recv.at[k],
                                                device_id=to, device_id_type=MESH)

        def d2d(k, t, slot, core):
            ref = fulls[t].at[slot, part(t, core)]
            return pltpu.make_async_remote_copy(src_ref=ref, dst_ref=ref, send_sem=d2d_send.at[k, t], recv_sem=d2d_recv.at[k, t],
                                                device_id=sib, device_id_type=MESH)

        sends = []

        def go(cp):
            cp.start()
            sends.append(cp)

        for q_first in (0, 1):
            for k, (px, py) in enumerate(near):
                for t in range(2):
                    go(direct(k, t, k if q_first == 0 else 1 - k, me, (px, py, c)))
        for k, (px, py) in enumerate(near + [far]):
            go(conv_copy(k, me, (px, py, c)))
        for k, (px, py) in enumerate(near):
            other = near[1 - k]
            for t in range(2):
                direct(k, t, k, chip(px, py), (px, py, c)).wait_recv()
                go(passed_on(k, t, chip(px, py), (*other, c)))

        first_half = lax.broadcasted_iota(jnp.int32, (chunk, LANES), 1) % HEAD_DIM < half_dim
        row = lax.broadcasted_iota(jnp.int32, (chunk, LANES), 0)

        def table_rows(i, carry):
            at = pl.multiple_of(i * chunk, chunk)
            ang = (row + at).astype(F32) * freq_ref[...]
            sin = jnp.sin(ang)
            cos_ref[pl.ds(at, chunk), :] = jnp.cos(ang)
            s1_ref[pl.ds(at, chunk), :] = jnp.where(first_half, -sin, 0.0)
            s2_ref[pl.ds(at, chunk), :] = jnp.where(first_half, 0.0, sin)
            return carry

        lax.fori_loop(0, seq // chunk, table_rows, 0)

        for k, (px, py) in enumerate(near):
            for t in range(2):
                direct(k, t, 1 - k, chip(px, py), (px, py, c)).wait_recv()
                go(d2d(k, t, chip(px, py), c))
        for t in range(2):
            for k, (px, py) in enumerate(near):
                passed_on(k, t, chip(*far), (px, py, c)).wait_recv()
            go(d2d(2, t, chip(*far), c))
        for k, (px, py) in enumerate(near + [far]):
            conv_copy(k, chip(px, py), (px, py, c)).wait_recv()
            for t in range(2):
                d2d(k, t, chip(px, py), 1 - c).wait_recv()
        for cp in sends:
            cp.wait_send()

    vm = pl.BlockSpec(memory_space=pltpu.VMEM)
    dma = pltpu.SemaphoreType.DMA
    return pl.pallas_call(
        body, name="gather_weights",
        out_shape=(jax.ShapeDtypeStruct((N_CHIPS, d_model, width), BF16),
                   jax.ShapeDtypeStruct((N_CHIPS, rows, d_model), BF16),
                   jax.ShapeDtypeStruct((N_CHIPS, SUBLANES, LANES), F32),
                   *[jax.ShapeDtypeStruct((seq, LANES), F32)] * 3),
        in_specs=[vm, vm, vm, vm], out_specs=(vm,) * 6,
        scratch_shapes=[pltpu.VMEM((d_model, width), BF16), pltpu.VMEM((rows, d_model), BF16),
                        dma((2, 2, 2)), dma((2, 2, 2)), dma((2, 2)), dma((2, 2)), dma((3,)), dma((3,)),
                        dma((3, 2)), dma((3, 2))],
        compiler_params=_params(),
    )(w_in, w_out, cw, inv_freq)


def _rope(t, cos, s1, s2):
    return t * cos + pltpu.roll(t, LANES - HEAD_DIM // 2, 1) * s1 + pltpu.roll(t, HEAD_DIM // 2, 1) * s2


def _rope_transposed(g, cos, s1, s2):
    return g * cos + pltpu.roll(g * s1, HEAD_DIM // 2, 1) + pltpu.roll(g * s2, LANES - HEAD_DIM // 2, 1)


def inproj(x, g1, w_full, tables, attn_w, conv_w):
    seq, d_model = x.shape
    width = w_full.shape[2]
    tm = ROW_TILE
    groups = _lane_groups(attn_w)

    def body(x_ref, g_ref, w_ref, cos_ref, s1_ref, s2_ref,
             ht_ref, q_ref, k_ref, v_ref, qp_ref, kp_ref, vp_ref, ga_ref, cz_ref, stage):
        xv = x_ref[...]
        hb = ((xv * lax.rsqrt(jnp.mean(xv * xv, axis=-1, keepdims=True) + NORM_EPS)) * g_ref[...]).astype(BF16)
        ht_ref[...] = jnp.transpose(hb)
        cos, s1, s2 = cos_ref[...], s1_ref[...], s2_ref[...]

        def proj(a, b):
            parts = [jnp.dot(hb, w_ref[j, :, lo:hi], preferred_element_type=F32) for j, lo, hi in _col_pieces(a, b, width)]
            return parts[0] if len(parts) == 1 else jnp.concatenate(parts, axis=1)

        def emit(z, nat_ref, perm_ref, fn):
            for g, sl in enumerate(groups):
                val = fn(z[:, sl])
                nat_ref[:, sl] = val.astype(BF16)
                _stage_put(stage, g, val)
            for g, sl in enumerate(groups):
                _to_perm(stage, g, perm_ref, sl, BF16)

        emit(proj(0, attn_w), q_ref, qp_ref, lambda t: _rope(t, cos, s1, s2) * ATTN_SCALE)
        emit(proj(attn_w, 2 * attn_w), k_ref, kp_ref, lambda t: _rope(t, cos, s1, s2))
        emit(proj(2 * attn_w, 3 * attn_w), v_ref, vp_ref, lambda t: t)
        ga_ref[...] = proj(3 * attn_w, 4 * attn_w)
        cz_ref[...] = proj(4 * attn_w, 4 * attn_w + 4 * conv_w)

    row = lambda n: pl.BlockSpec((tm, n), lambda i: (i, 0))
    nat = jax.ShapeDtypeStruct((seq, attn_w), BF16)
    perm = jax.ShapeDtypeStruct(_perm_shape(seq, attn_w), BF16)
    return pl.pallas_call(
        body, name="inproj", grid=(seq // tm,),
        out_shape=(jax.ShapeDtypeStruct((d_model, seq), BF16), nat, nat, nat, perm, perm, perm,
                   jax.ShapeDtypeStruct((seq, attn_w), F32), jax.ShapeDtypeStruct((seq, 4 * conv_w), F32)),
        in_specs=[row(d_model), _const_spec((1, d_model)), _const_spec(w_full.shape), row(LANES), row(LANES), row(LANES)],
        out_specs=(pl.BlockSpec((d_model, tm), lambda i: (0, i)), row(attn_w), row(attn_w), row(attn_w),
                   _perm_tile_spec(attn_w, tm), _perm_tile_spec(attn_w, tm), _perm_tile_spec(attn_w, tm),
                   row(attn_w), row(4 * conv_w)),
        scratch_shapes=[pltpu.VMEM(_stage_shape(len(groups), tm), F32)],
        compiler_params=_params(("arbitrary",)),
    )(x, g1, w_full, *tables)


class _Mode:
    def __init__(self, name, seq):
        self.name = name
        if name == "nat":
            self.residues, blocks = 1, seq // BLOCK
        elif name == "p16":
            self.residues, blocks = PERM, seq // PERM // BLOCK
        else:
            self.residues, blocks = PJ, seq // PERM // P4_ROWS
        self.qb = max(d for d in range(1, MAX_QUERY_BLOCKS + 1) if blocks % d == 0)
        self.steps = blocks // self.qb

    def _spec(self, blocks, width, at):
        if self.name == "nat":
            return pl.BlockSpec((blocks * BLOCK, width), lambda *g: (at(*g)[1], 0))
        if self.name == "p16":
            return pl.BlockSpec((1, 1, blocks * BLOCK, width), lambda *g: (at(*g)[0] // PJ, at(*g)[0] % PJ, at(*g)[1], 0))
        return pl.BlockSpec((PJ, 1, blocks * P4_ROWS, width), lambda *g: (0, at(*g)[0], at(*g)[1], 0))

    def wide(self, width, where=lambda r, n: (r, n)):
        return self._spec(self.qb, width, where)

    def block_before(self, width, where=lambda r, n: (r, n)):
        return self._spec(1, width, lambda *g: (where(*g)[0], jnp.maximum(self.qb * where(*g)[1] - 1, 0)))

    def get(self, ref, sl, sub=0):
        if self.name == "nat":
            return ref[sub * BLOCK:(sub + 1) * BLOCK, sl]
        if self.name == "p16":
            return ref[0, 0, sub * BLOCK:(sub + 1) * BLOCK, sl]
        return jnp.concatenate([ref[j, 0, at:at + P4_ROWS // 2, sl] for j, at in self._p4_chunks(sub)], axis=0)

    def put(self, ref, sl, val, sub=0):
        val = val.astype(ref.dtype)
        if self.name == "nat":
            ref[sub * BLOCK:(sub + 1) * BLOCK, sl] = val
        elif self.name == "p16":
            ref[0, 0, sub * BLOCK:(sub + 1) * BLOCK, sl] = val
        else:
            for i, (j, at) in enumerate(self._p4_chunks(sub)):
                ref[j, 0, at:at + P4_ROWS // 2, sl] = val[i * (P4_ROWS // 2):(i + 1) * (P4_ROWS // 2)]

    @staticmethod
    def _p4_chunks(sub):
        return [(j, sub * P4_ROWS + half * (P4_ROWS // 2)) for half in (0, 1) for j in range(PJ)]

    def keys(self, before_ref, wide_ref, sl, sub):
        older = self.get(before_ref, sl) if sub == 0 else self.get(wide_ref, sl, sub - 1)
        return jnp.concatenate([older, self.get(wide_ref, sl, sub)], axis=0)

    def index(self, idx, is_key):
        if self.name != "p4":
            return idx - BLOCK if is_key else idx
        within = jnp.bitwise_and(idx, BLOCK - 1)
        chunk = P4_ROWS // 2
        half = jnp.right_shift(within, HALF_BLOCK.bit_length() - 1)
        j = jnp.bitwise_and(jnp.right_shift(within, chunk.bit_length() - 1), PJ - 1)
        m = PJ * (chunk * half + jnp.bitwise_and(within, chunk - 1)) + j
        return m + BLOCK * (jnp.right_shift(idx, BLOCK.bit_length() - 1) - 1) if is_key else m

    def bias(self, has_before):
        shape = (2 * BLOCK, BLOCK)
        kidx = lax.broadcasted_iota(jnp.int32, shape, 0)
        qidx = lax.broadcasted_iota(jnp.int32, shape, 1)
        rel = self.index(qidx, False) - self.index(kidx, True)
        valid = (rel >= 0) & (rel <= WINDOW_KEYS)
        if has_before is not True:
            valid = valid & ((kidx >= BLOCK) | has_before)
        one = jnp.where(valid, 0.0, NEG)
        return jnp.concatenate([one, one], axis=1)

    def live_keys(self, half):
        return (0, 2 * BLOCK - HALF_BLOCK) if half == 0 else (HALF_BLOCK, 2 * BLOCK)

    def half_bias(self, has_before, half):
        r0, r1 = self.live_keys(half)
        shape = (r1 - r0, LANES)
        kidx = lax.broadcasted_iota(jnp.int32, shape, 0) + r0
        qidx = jnp.bitwise_and(lax.broadcasted_iota(jnp.int32, shape, 1), HALF_BLOCK - 1) + half * HALF_BLOCK
        rel = self.index(qidx, False) - self.index(kidx, True)
        valid = (rel >= 0) & (rel <= WINDOW_KEYS)
        if has_before is not True:
            valid = valid & ((kidx >= BLOCK) | has_before)
        return jnp.where(valid, 0.0, NEG)


def _head_masks():
    lane = lax.broadcasted_iota(jnp.int32, (BLOCK, LANES), 1)
    lo = lane < HEAD_DIM
    return lane, lo, jnp.where(lo, 1.0, 0.0).astype(BF16), jnp.where(lo, 0.0, 1.0).astype(BF16)


def attn_fwd(name, q, k, v, run):
    nat = name == "nat"
    seq = q.shape[0] if nat else q.shape[2] * PERM
    attn_w = q.shape[-1]
    mode = _Mode(name, seq)
    groups = _lane_groups(attn_w)
    first = run is None
    all_lanes = slice(0, LANES)

    def body(*refs):
        q_ref, kp_ref, kc_ref, vp_ref, vc_ref = refs[:5]
        if first:
            o_ref, l_ref = refs[5:]
        elif nat:
            oin_ref, lin_ref, o_ref, l_ref, ostage, lstage = refs[5:]
        else:
            oin_ref, lin_ref, o_ref, l_ref = refs[5:]
        n = pl.program_id(1)
        subs = range(mode.qb)
        halves = (0, 1)
        live = [mode.live_keys(x) for x in halves]
        always = [mode.half_bias(True, x) for x in halves]
        biases = [[mode.half_bias(n > 0, x) for x in halves]] + [always] * (mode.qb - 1)
        _, lo, m_lo, m_hi = _head_masks()
        head_row = lax.broadcasted_iota(jnp.int32, (BLOCK, LANES), 0)
        ones = jnp.ones((2 * BLOCK, LANES), BF16)
        hb = HALF_BLOCK
        lrows = [jnp.zeros((BLOCK, LANES), F32) for _ in subs]
        if not first:
            if nat:
                for g, sl in enumerate(groups):
                    _from_perm(oin_ref, sl, ostage, g)
                _from_perm(lin_ref, all_lanes, lstage, 0)
            wide_rows = lambda a, sub: a[sub * BLOCK:(sub + 1) * BLOCK]
            before = [jnp.transpose(wide_rows(_stage_get(lstage, 0), sub) if nat else mode.get(lin_ref, all_lanes, sub))
                      for sub in subs]

        def probs(sub, p, sl):
            q2 = mode.get(q_ref, sl, sub)
            kcat = mode.keys(kp_ref, kc_ref, sl, sub)
            vcat = mode.keys(vp_ref, vc_ref, sl, sub)
            q_lo, q_hi = q2 * m_lo, q2 * m_hi
            qq = jnp.concatenate([q_lo[:hb], q_hi[:hb], q_lo[hb:], q_hi[hb:]], axis=0)
            s_t = _nt(kcat, qq)
            columns, lses = [], []
            for x in halves:
                r0, r1 = live[x]
                s_x = s_t[r0:r1, x * LANES:(x + 1) * LANES] + biases[sub][x]
                m = jnp.max(s_x, axis=0, keepdims=True)
                pe = jnp.exp(s_x - m)
                lse = m + jnp.log(jnp.sum(pe, axis=0, keepdims=True))
                if not first:
                    was = jnp.concatenate([before[sub][2 * p:2 * p + 1, x * hb:(x + 1) * hb],
                                           before[sub][2 * p + 1:2 * p + 2, x * hb:(x + 1) * hb]], axis=1)
                    top = jnp.maximum(was, lse)
                    lse = top + jnp.log(jnp.exp(was - top) + jnp.exp(lse - top))
                    pe = pe * jnp.exp(m - lse)
                pieces = [pe.astype(BF16)]
                if r0 > 0:
                    pieces.insert(0, jnp.zeros((r0, LANES), BF16))
                if r1 < 2 * BLOCK:
                    pieces.append(jnp.zeros((2 * BLOCK - r1, LANES), BF16))
                columns.append(pieces[0] if len(pieces) == 1 else jnp.concatenate(pieces, axis=0))
                lses.append(lse)
            return jnp.concatenate([vcat, ones], axis=1), jnp.concatenate(columns, axis=1), lses

        def output(sub, p, sl, vext, pb, lses):
            o_ext = _tn(pb, vext)
            if first:
                o_new = o_ext[:, :LANES] / o_ext[:, LANES:]
            else:
                o_prev = wide_rows(_stage_get(ostage, p), sub) if nat else mode.get(oin_ref, sl, sub)
                same = jnp.concatenate([o_prev[:hb], o_prev[:hb], o_prev[hb:], o_prev[hb:]], axis=0)
                o_new = o_ext[:, :LANES] + same * (1.0 - o_ext[:, LANES:])
            head_lo = jnp.concatenate([o_new[:hb], o_new[2 * hb:3 * hb]], axis=0)
            head_hi = jnp.concatenate([o_new[hb:2 * hb], o_new[3 * hb:]], axis=0)
            mode.put(o_ref, sl, jnp.where(lo, head_lo, head_hi), sub)
            lse_lo = jnp.concatenate([lses[0][:, :hb], lses[1][:, :hb]], axis=1)
            lse_hi = jnp.concatenate([lses[0][:, hb:], lses[1][:, hb:]], axis=1)
            rows = jnp.where(head_row == 2 * p, lse_lo, lrows[sub])
            lrows[sub] = jnp.where(head_row == 2 * p + 1, lse_hi, rows)

        pending = None
        for sub in subs:
            for p, sl in enumerate(groups):
                nxt = probs(sub, p, sl)
                if pending is not None:
                    output(*pending)
                pending = (sub, p, sl, *nxt)
        output(*pending)
        for sub in subs:
            mode.put(l_ref, all_lanes, jnp.transpose(lrows[sub]), sub)

    ins = [q, k, k, v, v]
    specs = [mode.wide(attn_w), mode.block_before(attn_w), mode.wide(attn_w), mode.block_before(attn_w), mode.wide(attn_w)]
    scratch = []
    if not first:
        ins += list(run)
        if nat:
            rows_a = mode.qb * BLOCK // PERM
            specs += [pl.BlockSpec((PJ, PJ, rows_a, attn_w), lambda r, n: (0, 0, n, 0)),
                      pl.BlockSpec((PJ, PJ, rows_a, LANES), lambda r, n: (0, 0, n, 0))]
            scratch = [pltpu.VMEM(_stage_shape(len(groups), mode.qb * BLOCK), F32),
                       pltpu.VMEM(_stage_shape(1, mode.qb * BLOCK), F32)]
        else:
            specs += [mode.wide(attn_w), mode.wide(LANES)]
    if nat:
        out_shape = (jax.ShapeDtypeStruct((seq, attn_w), F32), jax.ShapeDtypeStruct((seq, LANES), F32))
    else:
        out_shape = (jax.ShapeDtypeStruct(_perm_shape(seq, attn_w), F32), jax.ShapeDtypeStruct(_perm_shape(seq, LANES), F32))
    return pl.pallas_call(
        body, name=f"attn_fwd_{name}", grid=(mode.residues, mode.steps),
        out_shape=out_shape, in_specs=specs, out_specs=(mode.wide(attn_w), mode.wide(LANES)),
        scratch_shapes=scratch,
        compiler_params=_params(("arbitrary", "arbitrary")),
    )(*ins)


def attn_bwd(name, q, k, v, d_o, lse, delta, run):
    nat = name == "nat"
    seq = q.shape[0] if nat else q.shape[2] * PERM
    attn_w = q.shape[-1]
    mode = _Mode(name, seq)
    steps, qb = mode.steps, mode.qb
    single = steps == 1
    groups = _lane_groups(attn_w)
    first = run is None
    all_lanes = slice(0, LANES)

    def body(*refs):
        q_ref, kp_ref, kc_ref, vp_ref, vc_ref, do_ref, lse_ref, dl_ref = refs[:8]
        if first:
            dq_ref, dk_ref, dv_ref, ck, cv = refs[8:]
        else:
            dqi_ref, dki_ref, dvi_ref, dq_ref, dk_ref, dv_ref, ck, cv = refs[8:]
        g = pl.program_id(1) if single else pl.program_id(0)
        n = g if single else lax.rem(g, steps)
        carries = ((ck, dk_ref, None if first else dki_ref), (cv, dv_ref, None if first else dvi_ref))

        def emit(out_ref, acc_ref, sl, sub, val):
            if acc_ref is not None:
                val = val + mode.get(acc_ref, sl, sub).astype(F32)
            mode.put(out_ref, sl, val, sub)

        if not single:
            @pl.when(g == 0)
            def _():
                ck[...] = jnp.zeros_like(ck)
                cv[...] = jnp.zeros_like(cv)

        @pl.when(g < total)
        def _():
            biases = [mode.bias(n > 0)] + [mode.bias(True)] * (qb - 1)
            _, lo, m_lo, m_hi = _head_masks()

            def scores(sub, p, sl, lse_t, dl_t):
                q2, do2 = mode.get(q_ref, sl, sub), mode.get(do_ref, sl, sub)
                kcat = mode.keys(kp_ref, kc_ref, sl, sub)
                vcat = mode.keys(vp_ref, vc_ref, sl, sub)
                qq = jnp.concatenate([q2 * m_lo, q2 * m_hi], axis=0)
                dd = jnp.concatenate([do2 * m_lo, do2 * m_hi], axis=0)
                h0 = 2 * p
                lse2 = jnp.concatenate([lse_t[h0:h0 + 1, :], lse_t[h0 + 1:h0 + 2, :]], axis=1)
                dl2 = jnp.concatenate([dl_t[h0:h0 + 1, :], dl_t[h0 + 1:h0 + 2, :]], axis=1)
                p_t = jnp.exp(_nt(kcat, qq) + (biases[sub] - lse2))
                ds_t = p_t * (_nt(vcat, dd) - dl2)
                return qq, dd, kcat, p_t.astype(BF16), ds_t.astype(BF16)

            def grads(sub, sl, qq, dd, kcat, pb, dsb):
                dqb = _tn(dsb, kcat)
                dq2 = jnp.where(lo, dqb[:BLOCK], dqb[BLOCK:]) * ATTN_SCALE
                if not first:
                    dq2 = dq2 + mode.get(dqi_ref, sl, sub).astype(F32)
                mode.put(dq_ref, sl, dq2, sub)
                for (carry, out_ref, acc_ref), lhs, rhs in zip(carries, (dsb, pb), (qq, dd)):
                    both = jnp.dot(lhs, rhs, preferred_element_type=F32)
                    if sub == 0:
                        if not single:
                            for s in range(qb - 1):
                                emit(out_ref, acc_ref, sl, s, carry[s, :, sl])
                            emit(out_ref, acc_ref, sl, qb - 1, carry[qb - 1, :, sl] + both[:BLOCK])
                        carry[0, :, sl] = both[BLOCK:]
                    else:
                        carry[sub - 1, :, sl] += both[:BLOCK]
                        carry[sub, :, sl] = both[BLOCK:]
                    if single and sub == qb - 1:
                        for s in range(qb):
                            emit(out_ref, acc_ref, sl, s, carry[s, :, sl])

            stats = [(jnp.transpose(mode.get(lse_ref, all_lanes, sub)),
                      jnp.transpose(mode.get(dl_ref, all_lanes, sub))) for sub in range(qb)]
            pending = None
            for p, sl in enumerate(groups):
                for sub in range(qb):
                    nxt = scores(sub, p, sl, *stats[sub])
                    if pending is not None:
                        grads(*pending)
                    pending = (sub, sl, *nxt)
            grads(*pending)

        if not single:
            @pl.when(g == total)
            def _():
                for carry, out_ref, acc_ref in carries:
                    for sl in groups:
                        for s in range(qb):
                            emit(out_ref, acc_ref, sl, s, carry[s, :, sl])

    total = mode.residues * steps
    if single:
        here = before = lambda r, n: (r, n)
    else:
        locate = lambda g: (g // steps, lax.rem(g, steps))
        here = lambda g: locate(jnp.minimum(g, total - 1))
        before = lambda g: locate(jnp.maximum(g - 1, 0))
    wide = lambda w: mode.wide(w, here)
    ins = [q, k, k, v, v, d_o, lse, delta]
    specs = [wide(attn_w), mode.block_before(attn_w, here), wide(attn_w), mode.block_before(attn_w, here), wide(attn_w),
             wide(attn_w), wide(LANES), wide(LANES)]
    if not first:
        ins += list(run)
        specs += [wide(attn_w), mode.wide(attn_w, before), mode.wide(attn_w, before)]
    shp = jax.ShapeDtypeStruct((seq, attn_w) if nat else _perm_shape(seq, attn_w), BF16)
    grid = (mode.residues, 1) if single else (total + 1,)
    return pl.pallas_call(
        body, name=f"attn_bwd_{name}", grid=grid,
        out_shape=(shp, shp, shp), in_specs=specs,
        out_specs=(wide(attn_w), mode.wide(attn_w, before), mode.wide(attn_w, before)),
        scratch_shapes=[pltpu.VMEM((qb, BLOCK, attn_w), F32), pltpu.VMEM((qb, BLOCK, attn_w), F32)],
        compiler_params=_params(("arbitrary",) * len(grid)),
    )(*ins)


def _shift_down(u, halo, k):
    rolled = pltpu.roll(u, k, 0)
    row = lax.broadcasted_iota(jnp.int32, halo.shape, 0)
    top = jnp.where(row < k, pltpu.roll(halo, k, 0), rolled[:SUBLANES])
    return jnp.concatenate([top, rolled[SUBLANES:]], axis=0)


def _shift_up(u, halo, k):
    rows = u.shape[0]
    rolled = pltpu.roll(u, rows - k, 0)
    row = lax.broadcasted_iota(jnp.int32, halo.shape, 0)
    bot = jnp.where(row >= SUBLANES - k, pltpu.roll(halo, SUBLANES - k, 0), rolled[rows - SUBLANES:])
    return jnp.concatenate([rolled[:rows - SUBLANES], bot], axis=0)


def tail(o, lse, ga, cz, x, tgt, w_out, g2, cw):
    seq, d_model = x.shape
    attn_w = o.shape[1]
    conv_w = cz.shape[1] // 4
    mix = attn_w + conv_w
    groups = _lane_groups(attn_w)
    tm = ROW_TILE
    nt = seq // tm
    hb = tm // SUBLANES

    def body(o_ref, l_ref, ga_ref, cz_ref, hz_ref, x_ref, t_ref, w_ref, g_ref, cw_ref,
             do_ref, dl_ref, dop_ref, dlp_ref, lp_ref, dga_ref, dcb_ref, dgc_ref, dcv_ref, e_ref,
             dw_ref, dg_ref, dcw_ref, loss_ref, stage):
        i = pl.program_id(0)

        @pl.when(i == 0)
        def _():
            dw_ref[...] = jnp.zeros_like(dw_ref)
            dg_ref[...] = jnp.zeros_like(dg_ref)
            dcw_ref[...] = jnp.zeros_like(dcw_ref)
            loss_ref[...] = jnp.zeros_like(loss_ref)

        u = cz_ref[:, 2 * conv_w:3 * conv_w] * cz_ref[:, 0:conv_w]
        uh = hz_ref[:, 2 * conv_w:3 * conv_w] * hz_ref[:, 0:conv_w]
        uh = jnp.where(i > 0, uh, 0.0)
        u1 = _shift_down(u, uh, 1)
        u2 = _shift_down(u, uh, 2)
        w0, w1, w2 = cw_ref[0:1, :], cw_ref[1:2, :], cw_ref[2:3, :]
        cvv = u2 * w0 + u1 * w1 + u * w2
        gv = g_ref[...]
        all_lanes = slice(0, LANES)

        def forward(rs):
            ov, gav = o_ref[rs, :], ga_ref[rs, :]
            sig_a = _sigmoid(gav)
            silu_a = gav * sig_a
            cb, gc = cz_ref[rs, conv_w:2 * conv_w], cz_ref[rs, 3 * conv_w:4 * conv_w]
            sig_c = _sigmoid(gc)
            silu_c = gc * sig_c
            bc = cb * cvv[rs]
            mixed = jnp.concatenate([ov * silu_a, bc * silu_c], axis=1).astype(BF16)
            yv = jnp.dot(mixed, w_ref[...], preferred_element_type=F32)
            return ov, gav, sig_a, silu_a, cb, gc, sig_c, silu_c, bc, mixed, yv

        def loss_and_dy(rs, mixed, yv):
            r2 = lax.rsqrt(jnp.mean(yv * yv, axis=-1, keepdims=True) + NORM_EPS)
            yhat = yv * r2
            diff = (x_ref[rs, :] + yhat * gv) - t_ref[rs, :]
            loss_ref[...] += _rowgroup_sum(diff * diff)
            ev = diff * (1.0 / d_model)
            e_ref[rs, :] = ev
            dg_ref[...] += _rowgroup_sum(ev * yhat)
            eg = ev * gv
            dy = (r2 * (eg - yhat * jnp.mean(eg * yhat, axis=-1, keepdims=True))).astype(BF16)
            dw_ref[...] += _tn(mixed, dy)
            return _nt(dy, w_ref[...])

        def backward(rs, ov, gav, sig_a, silu_a, cb, gc, sig_c, silu_c, bc, dm):
            rows = rs.stop - rs.start
            dma, dmc = dm[:, :attn_w], dm[:, attn_w:]
            dov = dma * silu_a
            do_ref[rs, :] = dov.astype(BF16)
            dga_ref[rs, :] = (dma * ov * (sig_a * (1.0 + gav * (1.0 - sig_a)))).astype(BF16)
            prod = dov * ov
            lane = lax.broadcasted_iota(jnp.int32, (rows, LANES), 1)
            lo = lane < HEAD_DIM
            dblk = jnp.zeros((rows, LANES), F32)
            for p, sl in enumerate(groups):
                pr = prod[:, sl]
                dblk = jnp.where(lane == 2 * p, jnp.sum(jnp.where(lo, pr, 0.0), axis=1, keepdims=True), dblk)
                dblk = jnp.where(lane == 2 * p + 1, jnp.sum(jnp.where(lo, 0.0, pr), axis=1, keepdims=True), dblk)
                _stage_put(stage, p, dov[:, sl], rs.start)
            dl_ref[rs, :] = dblk
            _stage_put(stage, len(groups), dblk, rs.start)
            _stage_put(stage, len(groups) + 1, l_ref[rs, :], rs.start)
            dsc = dmc * silu_c
            cv_rows = cvv[rs]
            dcb_ref[rs, :] = (dsc * cv_rows).astype(BF16)
            dgc_ref[rs, :] = (dmc * bc * (sig_c * (1.0 + gc * (1.0 - sig_c)))).astype(BF16)
            dcv = dsc * cb
            dcv_ref[rs, :] = dcv
            dcw_ref[0:SUBLANES, :] += _rowgroup_sum(dcv * u2[rs])
            dcw_ref[SUBLANES:2 * SUBLANES, :] += _rowgroup_sum(dcv * u1[rs])
            dcw_ref[2 * SUBLANES:3 * SUBLANES, :] += _rowgroup_sum(dcv * u[rs])

        halves = [slice(0, tm // 2), slice(tm // 2, tm)]
        fwd = [forward(rs) for rs in halves]
        dms = [loss_and_dy(rs, f[9], f[10]) for rs, f in zip(halves, fwd)]
        for rs, f, dm in zip(halves, fwd, dms):
            backward(rs, *f[:9], dm)
        for p, sl in enumerate(groups):
            _to_perm(stage, p, dop_ref, sl, BF16)
        _to_perm(stage, len(groups), dlp_ref, all_lanes, F32)
        _to_perm(stage, len(groups) + 1, lp_ref, all_lanes, F32)

    row = lambda n: pl.BlockSpec((tm, n), lambda i: (i, 0))
    whole = lambda a, b: pl.BlockSpec((a, b), lambda i: (0, 0))
    return pl.pallas_call(
        body, name="tail", grid=(nt,),
        out_shape=(jax.ShapeDtypeStruct((seq, attn_w), BF16), jax.ShapeDtypeStruct((seq, LANES), F32),
                   jax.ShapeDtypeStruct(_perm_shape(seq, attn_w), BF16), jax.ShapeDtypeStruct(_perm_shape(seq, LANES), F32),
                   jax.ShapeDtypeStruct(_perm_shape(seq, LANES), F32),
                   jax.ShapeDtypeStruct((seq, attn_w), BF16), jax.ShapeDtypeStruct((seq, conv_w), BF16),
                   jax.ShapeDtypeStruct((seq, conv_w), BF16), jax.ShapeDtypeStruct((seq, conv_w), F32),
                   jax.ShapeDtypeStruct((seq, d_model), F32), jax.ShapeDtypeStruct((mix, d_model), F32),
                   jax.ShapeDtypeStruct((SUBLANES, d_model), F32), jax.ShapeDtypeStruct((CONV_K * SUBLANES, conv_w), F32),
                   jax.ShapeDtypeStruct((SUBLANES, d_model), F32)),
        in_specs=[row(attn_w), row(LANES), row(attn_w), row(4 * conv_w),
                  pl.BlockSpec((SUBLANES, 4 * conv_w), lambda i: (jnp.maximum(i * hb - 1, 0), 0)),
                  row(d_model), row(d_model), _const_spec((mix, d_model)), _const_spec((1, d_model)),
                  _const_spec((SUBLANES, conv_w))],
        out_specs=(row(attn_w), row(LANES), _perm_tile_spec(attn_w, tm), _perm_tile_spec(LANES, tm), _perm_tile_spec(LANES, tm),
                   row(attn_w), row(conv_w), row(conv_w), row(conv_w), row(d_model),
                   whole(mix, d_model), whole(SUBLANES, d_model), whole(CONV_K * SUBLANES, conv_w),
                   whole(SUBLANES, d_model)),
        scratch_shapes=[pltpu.VMEM(_stage_shape(len(groups) + 2, tm), F32)],
        compiler_params=_params(("arbitrary",)),
    )(o, lse, ga, cz, cz, x, tgt, w_out, g2, cw)


def dz_dx(nat_grads, perm_grads, dga, dcb, dgc, dcv, cz, tables, x, g1, e, w_full, cw):
    seq, d_model = x.shape
    attn_w = dga.shape[1]
    conv_w = dcv.shape[1]
    width = w_full.shape[2]
    in_w = 4 * attn_w + 4 * conv_w
    groups = _lane_groups(attn_w)
    tm = ROW_TILE
    nt = seq // tm
    hb = tm // SUBLANES

    def body(dq_ref, dk_ref, dv_ref, dqp_ref, dkp_ref, dvp_ref, dga_ref, dcb_ref, dgc_ref, dcv_ref, nh_ref, ch_ref, cc_ref,
             cos_ref, s1_ref, s2_ref, x_ref, g_ref, e_ref, w_ref, cw_ref, gx_ref, dz_ref, dg_ref, stage):
        i = pl.program_id(0)

        @pl.when(i == 0)
        def _():
            dg_ref[...] = jnp.zeros_like(dg_ref)

        cos, s1, s2 = cos_ref[...], s1_ref[...], s2_ref[...]

        def qkv_columns(t, nat_ref, perm_ref):
            for g, sl in enumerate(groups):
                _from_perm(perm_ref, sl, stage, g)
            for g, sl in enumerate(groups):
                tot = nat_ref[:, sl].astype(F32) + _stage_get(stage, g)
                if t < 2:
                    tot = _rope_transposed(tot, cos, s1, s2)
                dz_ref[:, t * attn_w + g * LANES:t * attn_w + (g + 1) * LANES] = tot.astype(BF16)

        def dh_part(j):
            return _nt(dz_ref[:, j * width:(j + 1) * width], w_ref[j])

        dcv = dcv_ref[...]
        nh = jnp.where(i < nt - 1, nh_ref[...], 0.0)
        w0, w1, w2 = cw_ref[0:1, :], cw_ref[1:2, :], cw_ref[2:3, :]
        du = dcv * w2 + _shift_up(dcv, nh, 1) * w1 + _shift_up(dcv, nh, 2) * w0
        base = 4 * attn_w
        dz_ref[:, base:base + conv_w] = (du * cc_ref[...]).astype(BF16)
        dz_ref[:, base + conv_w:base + 2 * conv_w] = dcb_ref[...]
        dz_ref[:, base + 2 * conv_w:base + 3 * conv_w] = (du * ch_ref[...]).astype(BF16)
        dz_ref[:, base + 3 * conv_w:base + 4 * conv_w] = dgc_ref[...]
        dz_ref[:, 3 * attn_w:4 * attn_w] = dga_ref[...]
        ready = in_w
        dh = None
        for t, nat_ref, perm_ref in ((2, dv_ref, dvp_ref), (1, dk_ref, dkp_ref), (0, dq_ref, dqp_ref), (None, None, None)):
            lowest_open = 0 if t is None else (t + 1) * attn_w
            while ready - width >= lowest_open:
                ready -= width
                part = dh_part(ready // width)
                dh = part if dh is None else dh + part
            if t is not None:
                qkv_columns(t, nat_ref, perm_ref)
        xv = x_ref[...]
        r1 = lax.rsqrt(jnp.mean(xv * xv, axis=-1, keepdims=True) + NORM_EPS)
        xhat = xv * r1
        dg_ref[...] += _rowgroup_sum(dh * xhat)
        dhg = dh * g_ref[...]
        gx_ref[...] = r1 * (dhg - xhat * jnp.mean(dhg * xhat, axis=-1, keepdims=True)) + e_ref[...]

    row = lambda n: pl.BlockSpec((tm, n), lambda i: (i, 0))
    whole = lambda a, b: pl.BlockSpec((a, b), lambda i: (0, 0))
    pt = _perm_tile_spec(attn_w, tm)
    return pl.pallas_call(
        body, name="dz_dx", grid=(nt,),
        out_shape=(jax.ShapeDtypeStruct((seq, d_model), F32), jax.ShapeDtypeStruct((seq, in_w), BF16),
                   jax.ShapeDtypeStruct((SUBLANES, d_model), F32)),
        in_specs=[row(attn_w), row(attn_w), row(attn_w), pt, pt, pt, row(attn_w), row(conv_w), row(conv_w), row(conv_w),
                  pl.BlockSpec((SUBLANES, conv_w), lambda i: (jnp.minimum((i + 1) * hb, seq // SUBLANES - 1), 0)),
                  pl.BlockSpec((tm, conv_w), lambda i: (i, 0)), pl.BlockSpec((tm, conv_w), lambda i: (i, 2)),
                  row(LANES), row(LANES), row(LANES), row(d_model), _const_spec((1, d_model)), row(d_model),
                  _const_spec(w_full.shape), _const_spec((SUBLANES, conv_w))],
        out_specs=(row(d_model), row(in_w), whole(SUBLANES, d_model)),
        scratch_shapes=[pltpu.VMEM(_stage_shape(len(groups), tm), F32)],
        compiler_params=_params(("arbitrary",)),
    )(*nat_grads, *perm_grads, dga, dcb, dgc, dcv, dcv, cz, cz, *tables, x, g1, e, w_full, cw)


def dw_in_reduce(ht, dz, g_out, small):
    d_model, seq = ht.shape
    half = dz.shape[1] // N_DEV
    ts = min(DW_ROWS, seq)
    steps = seq // ts
    x, y, c = lax.axis_index("x"), lax.axis_index("y"), lax.axis_index("c")
    far_first = lambda x, y: [(1 - x, 1 - y), (1 - x, y), (x, 1 - y)]
    chips = jnp.stack([2 * px + py for px, py in far_first(x, y)] + [2 * x + y]).astype(jnp.int32)
    order = jnp.stack([2 * chips + (1 - c), 2 * chips + c], axis=1).reshape(N_DEV)

    def body(order_ref, ht_ref, dz_ref, go_ref, sm_ref, out_ref, ro_ref, rs_ref,
             acc, theirs, staged, contrib, resbuf, out_sem, sa, ra, sb, rb, sc, rc,
             o_mine, o_theirs, o_staged, o_contrib, o_res, sbuf, o_load, osa, ora, osb, orb, osc, orc, ss, rs):
        del order_ref
        p, s = pl.program_id(0), pl.program_id(1)
        x, y, c = lax.axis_index("x"), lax.axis_index("y"), lax.axis_index("c")
        me = 2 * x + y
        sib = (x, y, 1 - c)
        peers = far_first(x, y)
        slot = p % 2

        flips = [(fx, fy, fc) for fx in (0, 1) for fy in (0, 1) for fc in (0, 1)][1:]
        my8 = 4 * x + 2 * y + c
        chip_ids = [2 * px + py for px, py in peers] + [me]

        def small_copy(k, slot8, to):
            return pltpu.make_async_remote_copy(src_ref=sm_ref, dst_ref=sbuf.at[slot8], send_sem=ss.at[k], recv_sem=rs.at[k],
                                                device_id=to, device_id_type=MESH)

        def small_peer(k):
            fx, fy, fc = flips[k]
            return _flip(x, fx), _flip(y, fy), _flip(c, fc)

        def oa_copy(pos):
            j = chip_ids[pos]
            return pltpu.make_async_remote_copy(src_ref=go_ref.at[j, 1 - c], dst_ref=o_theirs.at[j], send_sem=osa.at[pos],
                                                recv_sem=ora.at[pos], device_id=sib, device_id_type=MESH)

        def o_load_copy(pos):
            j = chip_ids[pos]
            return pltpu.make_async_copy(go_ref.at[j, c], o_mine.at[j], o_load.at[pos])

        def ob_copy(k, piece, slot4):
            px, py = peers[k]
            return pltpu.make_async_remote_copy(src_ref=o_staged.at[piece], dst_ref=o_contrib.at[slot4], send_sem=osb.at[k],
                                                recv_sem=orb.at[k], device_id=(px, py, c), device_id_type=MESH)

        def oc_copy(which):
            return pltpu.make_async_remote_copy(src_ref=o_res.at[which], dst_ref=o_res.at[which], send_sem=osc, recv_sem=orc,
                                                device_id=sib, device_id_type=MESH)

        @pl.when((p == 0) & (s == 0))
        def _():
            sbuf[my8] = sm_ref[...]
            for k in range(N_DEV - 1):
                small_copy(k, my8, small_peer(k)).start()
            for pos in range(N_CHIPS):
                o_load_copy(pos).start()
                oa_copy(pos).start()

        @pl.when((p == 1) & (s == steps - 1))
        def _():
            for pos in range(N_CHIPS):
                j = chip_ids[pos]
                o_load_copy(pos).wait()
                oa_copy(pos).wait_recv()
                if pos < N_CHIPS - 1:
                    o_staged[j] = (o_mine[j] + o_theirs[j]).astype(BF16)
                    ob_copy(pos, j, me).start()
                else:
                    o_mine[j] = o_mine[j] + o_theirs[j]
                    o_contrib[j] = o_mine[j].astype(BF16)

        @pl.when((p == 4) & (s == steps - 1))
        def _():
            for k in range(N_CHIPS - 1):
                ob_copy(k, me, chip_ids[k]).wait_recv()
            own = o_mine[me]
            term = lambda j: jnp.where(me == j, own, o_contrib[j].astype(F32))
            o_res[c] = ((term(0) + term(1)) + term(2)) + term(3)
            oc_copy(c).start()

        def a_copy(k):
            return pltpu.make_async_remote_copy(src_ref=acc.at[0], dst_ref=theirs.at[k], send_sem=sa.at[k], recv_sem=ra.at[k],
                                                device_id=sib, device_id_type=MESH)

        def b_copy(k):
            px, py = peers[k]
            return pltpu.make_async_remote_copy(src_ref=staged.at[k], dst_ref=contrib.at[k], send_sem=sb.at[k], recv_sem=rb.at[k],
                                                device_id=(px, py, c), device_id_type=MESH)

        def c_copy(which):
            return pltpu.make_async_remote_copy(src_ref=resbuf.at[which], dst_ref=resbuf.at[which], send_sem=sc, recv_sem=rc,
                                                device_id=sib, device_id_type=MESH)

        @pl.when(s == 0)
        def _():
            for k in range(N_CHIPS - 1):
                @pl.when(p == 2 * k + 2)
                def _():
                    a_copy(k).wait_send()
            acc[slot] = jnp.dot(ht_ref[...], dz_ref[...], preferred_element_type=F32)

        @pl.when(s > 0)
        def _():
            acc[slot] += jnp.dot(ht_ref[...], dz_ref[...], preferred_element_type=F32)

        @pl.when(s == steps - 1)
        def _():
            for k in range(N_CHIPS):
                @pl.when(p == 2 * k)
                def _():
                    a_copy(k).start()
            for k in range(N_CHIPS - 1):
                @pl.when(p == 2 * k + 1)
                def _():
                    a_copy(k).wait_recv()
                    staged[k] = (acc[1] + theirs[k]).astype(BF16)
                    b_copy(k).start()

            @pl.when(p == N_DEV - 1)
            def _():
                a_copy(N_CHIPS - 1).wait_recv()
                tot = acc[1] + theirs[N_CHIPS - 1]
                for k in range(N_CHIPS - 1):
                    b_copy(k).wait_recv()
                    tot = tot + contrib[k].astype(F32)
                resbuf[c] = tot
                c_copy(c).start()
                c_copy(1 - c).wait_recv()
                done = pltpu.make_async_copy(resbuf, out_ref, out_sem)
                done.start()
                oc_copy(1 - c).wait_recv()
                ro_ref[...] = o_res[...]
                for k in range(N_DEV - 1):
                    px, py, pc = small_peer(k)
                    small_copy(k, 4 * px + 2 * py + pc, (px, py, pc)).wait_recv()
                tot8 = sbuf[0]
                for d in range(1, N_DEV):
                    tot8 = tot8 + sbuf[d]
                rs_ref[...] = tot8
                a_copy(N_CHIPS - 1).wait_send()
                for k in range(N_CHIPS - 1):
                    b_copy(k).wait_send()
                    ob_copy(k, chip_ids[k], me).wait_send()
                c_copy(c).wait_send()
                oc_copy(c).wait_send()
                for pos in range(N_CHIPS):
                    oa_copy(pos).wait_send()
                for k in range(N_DEV - 1):
                    small_copy(k, my8, small_peer(k)).wait_send()
                done.wait()

    dma = pltpu.SemaphoreType.DMA
    o_shape = g_out.shape[1:]
    go = g_out.reshape(N_CHIPS, 2, *o_shape)
    const = lambda shape: pl.BlockSpec(shape, lambda p, s, order_ref: (0,) * len(shape))
    grid_spec = pltpu.PrefetchScalarGridSpec(
        num_scalar_prefetch=1, grid=(N_DEV, steps),
        in_specs=[pl.BlockSpec((d_model, ts), lambda p, s, order_ref: (0, s)),
                  pl.BlockSpec((ts, half), lambda p, s, order_ref: (s, order_ref[p])),
                  pl.BlockSpec(memory_space=pl.ANY), const(small.shape)],
        out_specs=(pl.BlockSpec(memory_space=pl.ANY), const((2, *o_shape)), const(small.shape)),
        scratch_shapes=[pltpu.VMEM((2, d_model, half), F32), pltpu.VMEM((N_CHIPS, d_model, half), F32),
                        pltpu.VMEM((N_CHIPS - 1, d_model, half), BF16), pltpu.VMEM((N_CHIPS - 1, d_model, half), BF16),
                        pltpu.VMEM((2, d_model, half), F32), dma,
                        dma((N_CHIPS,)), dma((N_CHIPS,)), dma((N_CHIPS - 1,)), dma((N_CHIPS - 1,)), dma, dma,
                        pltpu.VMEM((N_CHIPS, *o_shape), F32), pltpu.VMEM((N_CHIPS, *o_shape), F32),
                        pltpu.VMEM((N_CHIPS, *o_shape), BF16), pltpu.VMEM((N_CHIPS, *o_shape), BF16),
                        pltpu.VMEM((2, *o_shape), F32), pltpu.VMEM((N_DEV, *small.shape), F32),
                        dma((N_CHIPS,)), dma((N_CHIPS,)), dma((N_CHIPS,)), dma((N_CHIPS - 1,)), dma((N_CHIPS - 1,)), dma, dma,
                        dma((N_DEV - 1,)), dma((N_DEV - 1,))])
    return pl.pallas_call(
        body, name="dw_in_reduce", grid_spec=grid_spec,
        out_shape=(jax.ShapeDtypeStruct((2, d_model, half), F32), jax.ShapeDtypeStruct((2, *o_shape), F32),
                   jax.ShapeDtypeStruct(small.shape, F32)),
        compiler_params=_params(("arbitrary", "arbitrary")),
    )(order, ht, dz, go, small)


def _adam_math(w, g, m, v):
    m = ADAM_B1 * m + (1.0 - ADAM_B1) * g
    v = ADAM_B2 * v + (1.0 - ADAM_B2) * (g * g)
    m_hat = m / (1.0 - ADAM_B1 ** ADAM_STEP)
    v_hat = v / (1.0 - ADAM_B2 ** ADAM_STEP)
    delta = -ADAM_LR * (m_hat / (jnp.sqrt(v_hat) + ADAM_EPS) + ADAM_WD * w)
    return delta, m, v


def adam_shard(name, w, g2, m, v, block, grid, w_map, g_map):
    def body(w_ref, g_ref, m_ref, v_ref, go_ref, d_ref, mo_ref, vo_ref):
        g = g_ref[0]
        delta, mn, vn = _adam_math(w_ref[...], g, m_ref[...], v_ref[...])
        go_ref[...] = g
        d_ref[...] = delta
        mo_ref[...] = mn
        vo_ref[...] = vn

    ws = pl.BlockSpec(block, w_map)
    shp = jax.ShapeDtypeStruct(w.shape, F32)
    return pl.pallas_call(
        body, name=name, grid=grid, out_shape=(shp, shp, shp, shp),
        in_specs=[ws, pl.BlockSpec((1, *block), g_map), ws, ws], out_specs=(ws, ws, ws, ws),
        compiler_params=_params(("arbitrary",) * len(grid)),
    )(w, g2, m, v)


def adam_small(ws, gs, ms, vs):
    n = len(ws)

    def body(*refs):
        ins, outs = refs[:4 * n], refs[4 * n:]
        for t in range(n):
            delta, mn, vn = _adam_math(ins[t][...], ins[n + t][...], ins[2 * n + t][...], ins[3 * n + t][...])
            outs[3 * t][...] = delta
            outs[3 * t + 1][...] = mn
            outs[3 * t + 2][...] = vn

    vm = pl.BlockSpec(memory_space=pltpu.VMEM)
    outs = pl.pallas_call(
        body, name="adam_small",
        out_shape=tuple(jax.ShapeDtypeStruct(w.shape, F32) for w in ws for _ in range(3)),
        in_specs=[vm] * (4 * n), out_specs=tuple([vm] * (3 * n)),
        compiler_params=_params(),
    )(*ws, *gs, *ms, *vs)
    return [outs[3 * t:3 * t + 3] for t in range(n)]


def kernel(x, norm_pre_g, w_in, conv_w, w_out, norm_post_g, loss_target, m_norm_pre_g, m_w_in, m_conv_w, m_w_out, m_norm_post_g, v_norm_pre_g, v_w_in, v_conv_w, v_w_out, v_norm_post_g):
    _, seq, d_model = x.shape
    width = w_in.shape[1]
    conv_q = conv_w.shape[1]
    conv_width = N_CHIPS * conv_q
    attn_width = d_model - conv_width
    xs, tg = x[0], loss_target[0]
    g1, g2 = norm_pre_g.reshape(1, d_model), norm_post_g.reshape(1, d_model)

    w_full, wout_full, cw_full, *tables = gather_weights(w_in, w_out, conv_w, seq)
    wout2 = wout_full.reshape(attn_width + conv_width, d_model)
    cw = jnp.zeros((SUBLANES, conv_width), F32).at[:CONV_K].set(
        cw_full[:, :CONV_K, :conv_q].transpose(1, 0, 2).reshape(CONV_K, conv_width))

    ht, q, k, v, qp, kp, vp, ga, cz = inproj(xs, g1, w_full, tables, attn_width, conv_width)
    run = attn_fwd("p4", qp, kp, vp, None)
    run = attn_fwd("p16", qp, kp, vp, run)
    o, lse = attn_fwd("nat", q, k, v, run)
    (d_o, delta, d_op, delta_p, lse_p, dga, dcb, dgc, dcv, e, dwout, dg2, dcw, loss_acc) = tail(
        o, lse, ga, cz, xs, tg, wout2, g2, cw)
    nat_grads = attn_bwd("nat", q, k, v, d_o, lse, delta, None)
    perm_grads = attn_bwd("p4", qp, kp, vp, d_op, lse_p, delta_p, None)
    perm_grads = attn_bwd("p16", qp, kp, vp, d_op, lse_p, delta_p, perm_grads)
    grad_x, dz, dg1 = dz_dx(nat_grads, perm_grads, dga, dcb, dgc, dcv, cz, tables, xs, g1, e, w_full, cw)

    small = jnp.zeros((SUBLANES, d_model), F32)
    small = small.at[0].set(dg1.sum(axis=0)).at[1].set(dg2.sum(axis=0))
    small = small.at[2:2 + CONV_K, :conv_width].set(dcw.reshape(CONV_K, SUBLANES, conv_width).sum(axis=1))
    small = small.at[2 + CONV_K, 0].set(jnp.sum(loss_acc))
    rin, rout, rsmall = dw_in_reduce(ht, dz, dwout.reshape(N_DEV, -1, d_model), small)

    half = width // 2
    tr = min(ADAM_ROWS, d_model)
    gw_in, d_in, m_in, v_in = adam_shard(
        "adam_w_in", w_in, rin, m_w_in, v_w_in, (tr, half), (2, d_model // tr),
        lambda hf, i: (i, hf), lambda hf, i: (hf, i, 0))
    rq = w_out.shape[0] // 2
    gw_out, d_out, m_out, v_out = adam_shard(
        "adam_w_out", w_out, rout, m_w_out, v_w_out, (rq, d_model), (2,),
        lambda hf: (hf, 0), lambda hf: (hf, 0, 0))

    chip = 2 * lax.axis_index("x") + lax.axis_index("y")
    g_pre, g_post = rsmall[0:1], rsmall[1:2]
    g_conv = lax.dynamic_slice(rsmall[2:2 + CONV_K, :conv_width], (0, chip * conv_q), (CONV_K, conv_q))
    (d_pre, m_pre, v_pre), (d_post, m_post, v_post), (d_cv, m_cv, v_cv) = adam_small(
        [g1, g2, conv_w], [g_pre, g_post, g_conv],
        [m_norm_pre_g.reshape(1, d_model), m_norm_post_g.reshape(1, d_model), m_conv_w],
        [v_norm_pre_g.reshape(1, d_model), v_norm_post_g.reshape(1, d_model), v_conv_w])

    loss = 0.5 * rsmall[2 + CONV_K, 0] / d_model
    vec = lambda a: a.reshape(d_model)
    return (loss, grad_x.reshape(1, seq, d_model),
            vec(g_pre), gw_in, g_conv, gw_out, vec(g_post),
            vec(d_pre), d_in, d_cv, d_out, vec(d_post),
            vec(m_pre), m_in, m_cv, m_out, vec(m_post),
            vec(v_pre), v_in, v_cv, v_out, vec(v_post))
```

```python
import jax
import jax.numpy as jnp
from jax import lax
from jax.experimental import pallas as pl
from jax.experimental.pallas import tpu as pltpu

HEAD_DIM = 64
LANES = 128
SUBLANES = 8
BLOCK = 128
HALF_BLOCK = BLOCK // 2
WINDOW_KEYS = 128
PERM = 16
PJ = 4
P4_ROWS = BLOCK // PJ
MAX_QUERY_BLOCKS = 8
ROW_TILE = 512
DW_ROWS = 4096
ADAM_ROWS = 1024
CONV_K = 3
ROPE_THETA = 10000.0
NORM_EPS = 1e-6
ATTN_SCALE = HEAD_DIM ** -0.5
NEG = -1e30
N_CHIPS = 4
N_DEV = 8
MESH = pl.DeviceIdType.MESH
ADAM_LR = 0.001
ADAM_B1 = 0.9
ADAM_B2 = 0.999
ADAM_EPS = 1e-08
ADAM_WD = 0.01
ADAM_STEP = 10
VMEM_LIMIT = 60 * 1024 * 1024

F32 = jnp.float32
BF16 = jnp.bfloat16


def _params(sem=None, **kw):
    return pltpu.CompilerParams(dimension_semantics=sem, vmem_limit_bytes=VMEM_LIMIT, **kw)


def _const_spec(shape):
    return pl.BlockSpec(shape, lambda *_: (0,) * len(shape), pipeline_mode=pl.Buffered(1))


def _sigmoid(z):
    return 1.0 / (1.0 + jnp.exp(-z))


def _rowgroup_sum(a):
    rows, n = a.shape
    return a.reshape(rows // SUBLANES, SUBLANES, n).sum(axis=0)


def _nt(a, b):
    return lax.dot_general(a, b, (((1,), (1,)), ((), ())), preferred_element_type=F32)


def _tn(a, b):
    return lax.dot_general(a, b, (((0,), (0,)), ((), ())), preferred_element_type=F32)


def _col_pieces(a, b, width):
    out = []
    while a < b:
        j = a // width
        e = min(b, (j + 1) * width)
        out.append((j, a - j * width, e - j * width))
        a = e
    return out


def _lane_groups(width):
    return [slice(g * LANES, (g + 1) * LANES) for g in range(width // LANES)]


def _perm_shape(seq, width):
    return (PJ, PJ, seq // PERM, width)


def _perm_tile_spec(width, tm):
    return pl.BlockSpec((PJ, PJ, tm // PERM, width), lambda i: (0, 0, i, 0))


STAGE_PITCH = 24


def _stage_shape(groups, rows):
    return (groups, rows // PERM * STAGE_PITCH, LANES)


def _stage_put(stage, g, val, row0=0):
    for a in range(val.shape[0] // PERM):
        at = (row0 // PERM + a) * STAGE_PITCH
        stage[g, at:at + PERM, :] = val[a * PERM:(a + 1) * PERM]


def _stage_get(stage, g):
    return jnp.concatenate([stage[g, a * STAGE_PITCH:a * STAGE_PITCH + PERM, :]
                            for a in range(stage.shape[1] // STAGE_PITCH)], axis=0)


def _to_perm(stage, g, dst_ref, sl, dtype):
    rows = stage.shape[1] // STAGE_PITCH
    for b in range(PERM):
        dst_ref[b // PJ, b % PJ, :, sl] = stage[g, pl.ds(b, rows, stride=STAGE_PITCH), :].astype(dtype)


def _from_perm(src_ref, sl, stage, g):
    rows = stage.shape[1] // STAGE_PITCH
    for b in range(PERM):
        stage[g, pl.ds(b, rows, stride=STAGE_PITCH), :] = src_ref[b // PJ, b % PJ, :, sl].astype(F32)


def _flip(a, f):
    return 1 - a if f else a


def gather_weights(w_in, w_out, conv_w, seq):
    d_model, width = w_in.shape
    rows = w_out.shape[0]
    cw = jnp.zeros((SUBLANES, LANES), F32).at[:CONV_K, :conv_w.shape[1]].set(conv_w)
    half_dim = HEAD_DIM // 2
    inv_freq = ROPE_THETA ** (-jnp.arange(half_dim, dtype=F32) * 2.0 / HEAD_DIM)
    inv_freq = jnp.tile(inv_freq, LANES // half_dim).reshape(1, LANES)
    chunk = min(ROW_TILE, seq)

    def body(win_ref, wout_ref, cw_ref, freq_ref, winf_ref, woutf_ref, cwf_ref, cos_ref, s1_ref, s2_ref,
             st_in, st_out, near_send, near_recv, far_send, far_recv, cw_send, cw_recv, d2d_send, d2d_recv):
        x, y, c = lax.axis_index("x"), lax.axis_index("y"), lax.axis_index("c")
        me = 2 * x + y
        sib = (x, y, 1 - c)
        st_in[...] = win_ref[...].astype(BF16)
        st_out[...] = wout_ref[...].astype(BF16)
        winf_ref[me] = st_in[...]
        woutf_ref[me] = st_out[...]
        cwf_ref[me] = cw_ref[...]
        stages = (st_in, st_out)
        fulls = (winf_ref, woutf_ref)
        halves = (d_model // 2, rows // 2)

        def part(t, core, q=None):
            size = halves[t] if q is None else halves[t] // 2
            start = core * halves[t] if q is None else core * halves[t] + q * size
            return pl.ds(pl.multiple_of(start, size), size)

        near = [(1 - x, y), (x, 1 - y)]
        far = (1 - x, 1 - y)
        chip = lambda px, py: 2 * px + py

        def direct(k, t, q, slot, to):
            src = stages[t].at[part(t, c, q)]
            return pltpu.make_async_remote_copy(src_ref=src, dst_ref=fulls[t].at[slot, part(t, c, q)], send_sem=near_send.at[k, t, q],
                                                recv_sem=near_recv.at[k, t, q], device_id=to, device_id_type=MESH)

        def passed_on(k, t, slot, to):
            ref = fulls[t].at[slot, part(t, c, k)]
            return pltpu.make_async_remote_copy(src_ref=ref, dst_ref=ref, send_sem=far_send.at[k, t], recv_sem=far_recv.at[k, t],
                                                device_id=to, device_id_type=MESH)

        def conv_copy(k, slot, to):
            return pltpu.make_async_remote_copy(src_ref=cw_ref, dst_ref=cwf_ref.at[slot], send_sem=cw_send.at[k], recv_sem=cw_recv.at[k],
                                                device_id=to, device_id_type=MESH)

        def d2d(k, t, slot, core):
            ref = fulls[t].at[slot, part(t, core)]
            return pltpu.make_async_remote_copy(src_ref=ref, dst_ref=ref, send_sem=d2d_send.at[k, t], recv_sem=d2d_recv.at[k, t],
                                                device_id=sib, device_id_type=MESH)

        sends = []

        def go(cp):
            cp.start()
            sends.append(cp)

        for q_first in (0, 1):
            for k, (px, py) in enumerate(near):
                for t in range(2):
                    go(direct(k, t, k if q_first == 0 else 1 - k, me, (px, py, c)))
        for k, (px, py) in enumerate(near + [far]):
            go(conv_copy(k, me, (px, py, c)))
        for k, (px, py) in enumerate(near):
            other = near[1 - k]
            for t in range(2):
                direct(k, t, k, chip(px, py), (px, py, c)).wait_recv()
                go(passed_on(k, t, chip(px, py), (*other, c)))

        first_half = lax.broadcasted_iota(jnp.int32, (chunk, LANES), 1) % HEAD_DIM < half_dim
        row = lax.broadcasted_iota(jnp.int32, (chunk, LANES), 0)

        def table_rows(i, carry):
            at = pl.multiple_of(i * chunk, chunk)
            ang = (row + at).astype(F32) * freq_ref[...]
            sin = jnp.sin(ang)
            cos_ref[pl.ds(at, chunk), :] = jnp.cos(ang)
            s1_ref[pl.ds(at, chunk), :] = jnp.where(first_half, -sin, 0.0)
            s2_ref[pl.ds(at, chunk), :] = jnp.where(first_half, 0.0, sin)
            return carry

        lax.fori_loop(0, seq // chunk, table_rows, 0)

        for k, (px, py) in enumerate(near):
            for t in range(2):
                direct(k, t, 1 - k, chip(px, py), (px, py, c)).wait_recv()
                go(d2d(k, t, chip(px, py), c))
        for t in range(2):
            for k, (px, py) in enumerate(near):
                passed_on(k, t, chip(*far), (px, py, c)).wait_recv()
            go(d2d(2, t, chip(*far), c))
        for k, (px, py) in enumerate(near + [far]):
            conv_copy(k, chip(px, py), (px, py, c)).wait_recv()
            for t in range(2):
                d2d(k, t, chip(px, py), 1 - c).wait_recv()
        for cp in sends:
            cp.wait_send()

    vm = pl.BlockSpec(memory_space=pltpu.VMEM)
    dma = pltpu.SemaphoreType.DMA
    return pl.pallas_call(
        body, name="gather_weights",
        out_shape=(jax.ShapeDtypeStruct((N_CHIPS, d_model, width), BF16),
                   jax.ShapeDtypeStruct((N_CHIPS, rows, d_model), BF16),
                   jax.ShapeDtypeStruct((N_CHIPS, SUBLANES, LANES), F32),
                   *[jax.ShapeDtypeStruct((seq, LANES), F32)] * 3),
        in_specs=[vm, vm, vm, vm], out_specs=(vm,) * 6,
        scratch_shapes=[pltpu.VMEM((d_model, width), BF16), pltpu.VMEM((rows, d_model), BF16),
                        dma((2, 2, 2)), dma((2, 2, 2)), dma((2, 2)), dma((2, 2)), dma((3,)), dma((3,)),
                        dma((3, 2)), dma((3, 2))],
        compiler_params=_params(),
    )(w_in, w_out, cw, inv_freq)


def _rope(t, cos, s1, s2):
    return t * cos + pltpu.roll(t, LANES - HEAD_DIM // 2, 1) * s1 + pltpu.roll(t, HEAD_DIM // 2, 1) * s2


def _rope_transposed(g, cos, s1, s2):
    return g * cos + pltpu.roll(g * s1, HEAD_DIM // 2, 1) + pltpu.roll(g * s2, LANES - HEAD_DIM // 2, 1)


def inproj(x, g1, w_full, tables, attn_w, conv_w):
    seq, d_model = x.shape
    width = w_full.shape[2]
    tm = ROW_TILE
    groups = _lane_groups(attn_w)

    def body(x_ref, g_ref, w_ref, cos_ref, s1_ref, s2_ref,
             ht_ref, q_ref, k_ref, v_ref, qp_ref, kp_ref, vp_ref, ga_ref, cz_ref, stage):
        xv = x_ref[...]
        hb = ((xv * lax.rsqrt(jnp.mean(xv * xv, axis=-1, keepdims=True) + NORM_EPS)) * g_ref[...]).astype(BF16)
        ht_ref[...] = jnp.transpose(hb)
        cos, s1, s2 = cos_ref[...], s1_ref[...], s2_ref[...]

        def proj(a, b):
            parts = [jnp.dot(hb, w_ref[j, :, lo:hi], preferred_element_type=F32) for j, lo, hi in _col_pieces(a, b, width)]
            return parts[0] if len(parts) == 1 else jnp.concatenate(parts, axis=1)

        def emit(z, nat_ref, perm_ref, fn):
            for g, sl in enumerate(groups):
                val = fn(z[:, sl])
                nat_ref[:, sl] = val.astype(BF16)
                _stage_put(stage, g, val)
            for g, sl in enumerate(groups):
                _to_perm(stage, g, perm_ref, sl, BF16)

        emit(proj(0, attn_w), q_ref, qp_ref, lambda t: _rope(t, cos, s1, s2) * ATTN_SCALE)
        emit(proj(attn_w, 2 * attn_w), k_ref, kp_ref, lambda t: _rope(t, cos, s1, s2))
        emit(proj(2 * attn_w, 3 * attn_w), v_ref, vp_ref, lambda t: t)
        ga_ref[...] = proj(3 * attn_w, 4 * attn_w)
        cz_ref[...] = proj(4 * attn_w, 4 * attn_w + 4 * conv_w)

    row = lambda n: pl.BlockSpec((tm, n), lambda i: (i, 0))
    nat = jax.ShapeDtypeStruct((seq, attn_w), BF16)
    perm = jax.ShapeDtypeStruct(_perm_shape(seq, attn_w), BF16)
    return pl.pallas_call(
        body, name="inproj", grid=(seq // tm,),
        out_shape=(jax.ShapeDtypeStruct((d_model, seq), BF16), nat, nat, nat, perm, perm, perm,
                   jax.ShapeDtypeStruct((seq, attn_w), F32), jax.ShapeDtypeStruct((seq, 4 * conv_w), F32)),
        in_specs=[row(d_model), _const_spec((1, d_model)), _const_spec(w_full.shape), row(LANES), row(LANES), row(LANES)],
        out_specs=(pl.BlockSpec((d_model, tm), lambda i: (0, i)), row(attn_w), row(attn_w), row(attn_w),
                   _perm_tile_spec(attn_w, tm), _perm_tile_spec(attn_w, tm), _perm_tile_spec(attn_w, tm),
                   row(attn_w), row(4 * conv_w)),
        scratch_shapes=[pltpu.VMEM(_stage_shape(len(groups), tm), F32)],
        compiler_params=_params(("arbitrary",)),
    )(x, g1, w_full, *tables)


class _Mode:
    def __init__(self, name, seq):
        self.name = name
        if name == "nat":
            self.residues, blocks = 1, seq // BLOCK
        elif name == "p16":
            self.residues, blocks = PERM, seq // PERM // BLOCK
        else:
            self.residues, blocks = PJ, seq // PERM // P4_ROWS
        self.qb = max(d for d in range(1, MAX_QUERY_BLOCKS + 1) if blocks % d == 0)
        self.steps = blocks // self.qb
        self.seq_blocks = self.qb
        self.pack = 1
        if name == "p16" and self.steps == 1:
            self.pack = max(d for d in (1, 2, PJ) if d * blocks <= MAX_QUERY_BLOCKS)
            self.qb, self.residues = self.pack * blocks, PERM // self.pack

    def has_before(self, sub, n):
        if sub % self.seq_blocks:
            return True
        return n > 0 if sub == 0 else False

    def _spec(self, blocks, width, at):
        if self.name == "nat":
            return pl.BlockSpec((blocks * BLOCK, width), lambda *g: (at(*g)[1], 0))
        if self.name == "p16":
            per_j = PJ // self.pack
            rows = min(blocks, self.seq_blocks) * BLOCK
            return pl.BlockSpec((1, self.pack, rows, width), lambda *g: (at(*g)[0] // per_j, at(*g)[0] % per_j, at(*g)[1], 0))
        return pl.BlockSpec((PJ, 1, blocks * P4_ROWS, width), lambda *g: (0, at(*g)[0], at(*g)[1], 0))

    def wide(self, width, where=lambda r, n: (r, n)):
        return self._spec(self.qb, width, where)

    def block_before(self, width, where=lambda r, n: (r, n)):
        return self._spec(1, width, lambda *g: (where(*g)[0], jnp.maximum(self.qb * where(*g)[1] - 1, 0)))

    def get(self, ref, sl, sub=0):
        if self.name == "nat":
            return ref[sub * BLOCK:(sub + 1) * BLOCK, sl]
        if self.name == "p16":
            which, blk = divmod(sub, self.seq_blocks)
            return ref[0, which, blk * BLOCK:(blk + 1) * BLOCK, sl]
        return jnp.concatenate([ref[j, 0, at:at + P4_ROWS // 2, sl] for j, at in self._p4_chunks(sub)], axis=0)

    def put(self, ref, sl, val, sub=0):
        val = val.astype(ref.dtype)
        if self.name == "nat":
            ref[sub * BLOCK:(sub + 1) * BLOCK, sl] = val
        elif self.name == "p16":
            which, blk = divmod(sub, self.seq_blocks)
            ref[0, which, blk * BLOCK:(blk + 1) * BLOCK, sl] = val
        else:
            for i, (j, at) in enumerate(self._p4_chunks(sub)):
                ref[j, 0, at:at + P4_ROWS // 2, sl] = val[i * (P4_ROWS // 2):(i + 1) * (P4_ROWS // 2)]

    @staticmethod
    def _p4_chunks(sub):
        return [(j, sub * P4_ROWS + half * (P4_ROWS // 2)) for half in (0, 1) for j in range(PJ)]

    def keys(self, before_ref, wide_ref, sl, sub):
        older = self.get(before_ref, sl) if sub == 0 else self.get(wide_ref, sl, sub - 1)
        return jnp.concatenate([older, self.get(wide_ref, sl, sub)], axis=0)

    def index(self, idx, is_key):
        if self.name != "p4":
            return idx - BLOCK if is_key else idx
        within = jnp.bitwise_and(idx, BLOCK - 1)
        chunk = P4_ROWS // 2
        half = jnp.right_shift(within, HALF_BLOCK.bit_length() - 1)
        j = jnp.bitwise_and(jnp.right_shift(within, chunk.bit_length() - 1), PJ - 1)
        m = PJ * (chunk * half + jnp.bitwise_and(within, chunk - 1)) + j
        return m + BLOCK * (jnp.right_shift(idx, BLOCK.bit_length() - 1) - 1) if is_key else m

    def bias(self, has_before):
        shape = (2 * BLOCK, BLOCK)
        kidx = lax.broadcasted_iota(jnp.int32, shape, 0)
        qidx = lax.broadcasted_iota(jnp.int32, shape, 1)
        rel = self.index(qidx, False) - self.index(kidx, True)
        valid = (rel >= 0) & (rel <= WINDOW_KEYS)
        if has_before is not True:
            valid = valid & ((kidx >= BLOCK) | has_before)
        one = jnp.where(valid, 0.0, NEG)
        return jnp.concatenate([one, one], axis=1)

    def live_keys(self, half):
        return (0, 2 * BLOCK - HALF_BLOCK) if half == 0 else (HALF_BLOCK, 2 * BLOCK)

    def half_bias(self, has_before, half):
        r0, r1 = self.live_keys(half)
        shape = (r1 - r0, LANES)
        kidx = lax.broadcasted_iota(jnp.int32, shape, 0) + r0
        qidx = jnp.bitwise_and(lax.broadcasted_iota(jnp.int32, shape, 1), HALF_BLOCK - 1) + half * HALF_BLOCK
        rel = self.index(qidx, False) - self.index(kidx, True)
        valid = (rel >= 0) & (rel <= WINDOW_KEYS)
        if has_before is not True:
            valid = valid & ((kidx >= BLOCK) | has_before)
        return jnp.where(valid, 0.0, NEG)


def _head_masks():
    lane = lax.broadcasted_iota(jnp.int32, (BLOCK, LANES), 1)
    lo = lane < HEAD_DIM
    return lane, lo, jnp.where(lo, 1.0, 0.0).astype(BF16), jnp.where(lo, 0.0, 1.0).astype(BF16)


def attn_fwd(name, q, k, v, run):
    nat = name == "nat"
    seq = q.shape[0] if nat else q.shape[2] * PERM
    attn_w = q.shape[-1]
    mode = _Mode(name, seq)
    groups = _lane_groups(attn_w)
    first = run is None
    all_lanes = slice(0, LANES)

    def body(*refs):
        q_ref, kp_ref, kc_ref, vp_ref, vc_ref = refs[:5]
        if first:
            o_ref, l_ref = refs[5:]
        elif nat:
            oin_ref, lin_ref, o_ref, l_ref, ostage, lstage = refs[5:]
        else:
            oin_ref, lin_ref, o_ref, l_ref = refs[5:]
        n = pl.program_id(1)
        subs = range(mode.qb)
        halves = (0, 1)
        live = [mode.live_keys(x) for x in halves]
        always = [mode.half_bias(True, x) for x in halves]
        biases = [always if mode.has_before(sub, n) is True else [mode.half_bias(mode.has_before(sub, n), x) for x in halves]
                  for sub in subs]
        _, lo, m_lo, m_hi = _head_masks()
        head_row = lax.broadcasted_iota(jnp.int32, (BLOCK, LANES), 0)
        ones = jnp.ones((2 * BLOCK, LANES), BF16)
        hb = HALF_BLOCK
        lrows = [jnp.zeros((BLOCK, LANES), F32) for _ in subs]
        if not first:
            if nat:
                for g, sl in enumerate(groups):
                    _from_perm(oin_ref, sl, ostage, g)
                _from_perm(lin_ref, all_lanes, lstage, 0)
            wide_rows = lambda a, sub: a[sub * BLOCK:(sub + 1) * BLOCK]
            before = [jnp.transpose(wide_rows(_stage_get(lstage, 0), sub) if nat else mode.get(lin_ref, all_lanes, sub))
                      for sub in subs]

        def probs(sub, p, sl):
            q2 = mode.get(q_ref, sl, sub)
            kcat = mode.keys(kp_ref, kc_ref, sl, sub)
            vcat = mode.keys(vp_ref, vc_ref, sl, sub)
            q_lo, q_hi = q2 * m_lo, q2 * m_hi
            qq = jnp.concatenate([q_lo[:hb], q_hi[:hb], q_lo[hb:], q_hi[hb:]], axis=0)
            s_t = _nt(kcat, qq)
            columns, lses = [], []
            for x in halves:
                r0, r1 = live[x]
                s_x = s_t[r0:r1, x * LANES:(x + 1) * LANES] + biases[sub][x]
                m = jnp.max(s_x, axis=0, keepdims=True)
                pe = jnp.exp(s_x - m)
                lse = m + jnp.log(jnp.sum(pe, axis=0, keepdims=True))
                if not first:
                    was = jnp.concatenate([before[sub][2 * p:2 * p + 1, x * hb:(x + 1) * hb],
                                           before[sub][2 * p + 1:2 * p + 2, x * hb:(x + 1) * hb]], axis=1)
                    top = jnp.maximum(was, lse)
                    lse = top + jnp.log(jnp.exp(was - top) + jnp.exp(lse - top))
                    pe = pe * jnp.exp(m - lse)
                pieces = [pe.astype(BF16)]
                if r0 > 0:
                    pieces.insert(0, jnp.zeros((r0, LANES), BF16))
                if r1 < 2 * BLOCK:
                    pieces.append(jnp.zeros((2 * BLOCK - r1, LANES), BF16))
                columns.append(pieces[0] if len(pieces) == 1 else jnp.concatenate(pieces, axis=0))
                lses.append(lse)
            return jnp.concatenate([vcat, ones], axis=1), jnp.concatenate(columns, axis=1), lses

        def output(sub, p, sl, vext, pb, lses):
            o_ext = _tn(pb, vext)
            if first:
                o_new = o_ext[:, :LANES] / o_ext[:, LANES:]
            else:
                o_prev = wide_rows(_stage_get(ostage, p), sub) if nat else mode.get(oin_ref, sl, sub)
                same = jnp.concatenate([o_prev[:hb], o_prev[:hb], o_prev[hb:], o_prev[hb:]], axis=0)
                o_new = o_ext[:, :LANES] + same * (1.0 - o_ext[:, LANES:])
            head_lo = jnp.concatenate([o_new[:hb], o_new[2 * hb:3 * hb]], axis=0)
            head_hi = jnp.concatenate([o_new[hb:2 * hb], o_new[3 * hb:]], axis=0)
            mode.put(o_ref, sl, jnp.where(lo, head_lo, head_hi), sub)
            lse_lo = jnp.concatenate([lses[0][:, :hb], lses[1][:, :hb]], axis=1)
            lse_hi = jnp.concatenate([lses[0][:, hb:], lses[1][:, hb:]], axis=1)
            rows = jnp.where(head_row == 2 * p, lse_lo, lrows[sub])
            lrows[sub] = jnp.where(head_row == 2 * p + 1, lse_hi, rows)

        pending = None
        for sub in subs:
            for p, sl in enumerate(groups):
                nxt = probs(sub, p, sl)
                if pending is not None:
                    output(*pending)
                pending = (sub, p, sl, *nxt)
        output(*pending)
        for sub in subs:
            mode.put(l_ref, all_lanes, jnp.transpose(lrows[sub]), sub)

    ins = [q, k, k, v, v]
    specs = [mode.wide(attn_w), mode.block_before(attn_w), mode.wide(attn_w), mode.block_before(attn_w), mode.wide(attn_w)]
    scratch = []
    if not first:
        ins += list(run)
        if nat:
            rows_a = mode.qb * BLOCK // PERM
            specs += [pl.BlockSpec((PJ, PJ, rows_a, attn_w), lambda r, n: (0, 0, n, 0)),
                      pl.BlockSpec((PJ, PJ, rows_a, LANES), lambda r, n: (0, 0, n, 0))]
            scratch = [pltpu.VMEM(_stage_shape(len(groups), mode.qb * BLOCK), F32),
                       pltpu.VMEM(_stage_shape(1, mode.qb * BLOCK), F32)]
        else:
            specs += [mode.wide(attn_w), mode.wide(LANES)]
    if nat:
        out_shape = (jax.ShapeDtypeStruct((seq, attn_w), F32), jax.ShapeDtypeStruct((seq, LANES), F32))
    else:
        out_shape = (jax.ShapeDtypeStruct(_perm_shape(seq, attn_w), F32), jax.ShapeDtypeStruct(_perm_shape(seq, LANES), F32))
    return pl.pallas_call(
        body, name=f"attn_fwd_{name}", grid=(mode.residues, mode.steps),
        out_shape=out_shape, in_specs=specs, out_specs=(mode.wide(attn_w), mode.wide(LANES)),
        scratch_shapes=scratch,
        compiler_params=_params(("arbitrary", "arbitrary")),
    )(*ins)


def attn_bwd(name, q, k, v, d_o, lse, delta, run):
    nat = name == "nat"
    seq = q.shape[0] if nat else q.shape[2] * PERM
    attn_w = q.shape[-1]
    mode = _Mode(name, seq)
    steps, qb = mode.steps, mode.qb
    single = steps == 1
    groups = _lane_groups(attn_w)
    first = run is None
    all_lanes = slice(0, LANES)

    def body(*refs):
        q_ref, kp_ref, kc_ref, vp_ref, vc_ref, do_ref, lse_ref, dl_ref = refs[:8]
        if first:
            dq_ref, dk_ref, dv_ref, ck, cv = refs[8:]
        else:
            dqi_ref, dki_ref, dvi_ref, dq_ref, dk_ref, dv_ref, ck, cv = refs[8:]
        g = pl.program_id(1) if single else pl.program_id(0)
        n = g if single else lax.rem(g, steps)
        carries = ((ck, dk_ref, None if first else dki_ref), (cv, dv_ref, None if first else dvi_ref))

        def emit(out_ref, acc_ref, sl, sub, val):
            if acc_ref is not None:
                val = val + mode.get(acc_ref, sl, sub).astype(F32)
            mode.put(out_ref, sl, val, sub)

        if not single:
            @pl.when(g == 0)
            def _():
                ck[...] = jnp.zeros_like(ck)
                cv[...] = jnp.zeros_like(cv)

        @pl.when(g < total)
        def _():
            always = mode.bias(True)
            biases = [always if mode.has_before(sub, n) is True else mode.bias(mode.has_before(sub, n)) for sub in range(qb)]
            _, lo, m_lo, m_hi = _head_masks()

            def scores(sub, p, sl, lse_t, dl_t):
                q2, do2 = mode.get(q_ref, sl, sub), mode.get(do_ref, sl, sub)
                kcat = mode.keys(kp_ref, kc_ref, sl, sub)
                vcat = mode.keys(vp_ref, vc_ref, sl, sub)
                qq = jnp.concatenate([q2 * m_lo, q2 * m_hi], axis=0)
                dd = jnp.concatenate([do2 * m_lo, do2 * m_hi], axis=0)
                h0 = 2 * p
                lse2 = jnp.concatenate([lse_t[h0:h0 + 1, :], lse_t[h0 + 1:h0 + 2, :]], axis=1)
                dl2 = jnp.concatenate([dl_t[h0:h0 + 1, :], dl_t[h0 + 1:h0 + 2, :]], axis=1)
                p_t = jnp.exp(_nt(kcat, qq) + (biases[sub] - lse2))
                ds_t = p_t * (_nt(vcat, dd) - dl2)
                return qq, dd, kcat, p_t.astype(BF16), ds_t.astype(BF16)

            def grads(sub, sl, qq, dd, kcat, pb, dsb):
                dqb = _tn(dsb, kcat)
                dq2 = jnp.where(lo, dqb[:BLOCK], dqb[BLOCK:]) * ATTN_SCALE
                if not first:
                    dq2 = dq2 + mode.get(dqi_ref, sl, sub).astype(F32)
                mode.put(dq_ref, sl, dq2, sub)
                for (carry, out_ref, acc_ref), lhs, rhs in zip(carries, (dsb, pb), (qq, dd)):
                    both = jnp.dot(lhs, rhs, preferred_element_type=F32)
                    if sub == 0:
                        if not single:
                            for s in range(qb - 1):
                                emit(out_ref, acc_ref, sl, s, carry[s, :, sl])
                            emit(out_ref, acc_ref, sl, qb - 1, carry[qb - 1, :, sl] + both[:BLOCK])
                        carry[0, :, sl] = both[BLOCK:]
                    else:
                        carry[sub - 1, :, sl] += both[:BLOCK]
                        carry[sub, :, sl] = both[BLOCK:]
                    if single and sub == qb - 1:
                        for s in range(qb):
                            emit(out_ref, acc_ref, sl, s, carry[s, :, sl])

            stats = [(jnp.transpose(mode.get(lse_ref, all_lanes, sub)),
                      jnp.transpose(mode.get(dl_ref, all_lanes, sub))) for sub in range(qb)]
            pending = None
            for p, sl in enumerate(groups):
                for sub in range(qb):
                    nxt = scores(sub, p, sl, *stats[sub])
                    if pending is not None:
                        grads(*pending)
                    pending = (sub, sl, *nxt)
            grads(*pending)

        if not single:
            @pl.when(g == total)
            def _():
                for carry, out_ref, acc_ref in carries:
                    for sl in groups:
                        for s in range(qb):
                            emit(out_ref, acc_ref, sl, s, carry[s, :, sl])

    total = mode.residues * steps
    if single:
        here = before = lambda r, n: (r, n)
    else:
        locate = lambda g: (g // steps, lax.rem(g, steps))
        here = lambda g: locate(jnp.minimum(g, total - 1))
        before = lambda g: locate(jnp.maximum(g - 1, 0))
    wide = lambda w: mode.wide(w, here)
    ins = [q, k, k, v, v, d_o, lse, delta]
    specs = [wide(attn_w), mode.block_before(attn_w, here), wide(attn_w), mode.block_before(attn_w, here), wide(attn_w),
             wide(attn_w), wide(LANES), wide(LANES)]
    if not first:
        ins += list(run)
        specs += [wide(attn_w), mode.wide(attn_w, before), mode.wide(attn_w, before)]
    shp = jax.ShapeDtypeStruct((seq, attn_w) if nat else _perm_shape(seq, attn_w), BF16)
    grid = (mode.residues, 1) if single else (total + 1,)
    return pl.pallas_call(
        body, name=f"attn_bwd_{name}", grid=grid,
        out_shape=(shp, shp, shp), in_specs=specs,
        out_specs=(wide(attn_w), mode.wide(attn_w, before), mode.wide(attn_w, before)),
        scratch_shapes=[pltpu.VMEM((qb, BLOCK, attn_w), F32), pltpu.VMEM((qb, BLOCK, attn_w), F32)],
        compiler_params=_params(("arbitrary",) * len(grid)),
    )(*ins)


def _shift_down(u, halo, k):
    rolled = pltpu.roll(u, k, 0)
    row = lax.broadcasted_iota(jnp.int32, halo.shape, 0)
    top = jnp.where(row < k, pltpu.roll(halo, k, 0), rolled[:SUBLANES])
    return jnp.concatenate([top, rolled[SUBLANES:]], axis=0)


def _shift_up(u, halo, k):
    rows = u.shape[0]
    rolled = pltpu.roll(u, rows - k, 0)
    row = lax.broadcasted_iota(jnp.int32, halo.shape, 0)
    bot = jnp.where(row >= SUBLANES - k, pltpu.roll(halo, SUBLANES - k, 0), rolled[rows - SUBLANES:])
    return jnp.concatenate([rolled[:rows - SUBLANES], bot], axis=0)


def tail(o, lse, ga, cz, x, tgt, w_out, g2, cw):
    seq, d_model = x.shape
    attn_w = o.shape[1]
    conv_w = cz.shape[1] // 4
    mix = attn_w + conv_w
    groups = _lane_groups(attn_w)
    tm = ROW_TILE
    nt = seq // tm
    hb = tm // SUBLANES

    def body(o_ref, l_ref, ga_ref, cz_ref, hz_ref, x_ref, t_ref, w_ref, g_ref, cw_ref,
             do_ref, dl_ref, dop_ref, dlp_ref, lp_ref, dga_ref, dcb_ref, dgc_ref, dcv_ref, e_ref,
             dw_ref, dg_ref, dcw_ref, loss_ref, stage):
        i = pl.program_id(0)

        @pl.when(i == 0)
        def _():
            dw_ref[...] = jnp.zeros_like(dw_ref)
            dg_ref[...] = jnp.zeros_like(dg_ref)
            dcw_ref[...] = jnp.zeros_like(dcw_ref)
            loss_ref[...] = jnp.zeros_like(loss_ref)

        u = cz_ref[:, 2 * conv_w:3 * conv_w] * cz_ref[:, 0:conv_w]
        uh = hz_ref[:, 2 * conv_w:3 * conv_w] * hz_ref[:, 0:conv_w]
        uh = jnp.where(i > 0, uh, 0.0)
        u1 = _shift_down(u, uh, 1)
        u2 = _shift_down(u, uh, 2)
        w0, w1, w2 = cw_ref[0:1, :], cw_ref[1:2, :], cw_ref[2:3, :]
        cvv = u2 * w0 + u1 * w1 + u * w2
        gv = g_ref[...]
        all_lanes = slice(0, LANES)

        def forward(rs):
            ov, gav = o_ref[rs, :], ga_ref[rs, :]
            sig_a = _sigmoid(gav)
            silu_a = gav * sig_a
            cb, gc = cz_ref[rs, conv_w:2 * conv_w], cz_ref[rs, 3 * conv_w:4 * conv_w]
            sig_c = _sigmoid(gc)
            silu_c = gc * sig_c
            bc = cb * cvv[rs]
            mixed = jnp.concatenate([ov * silu_a, bc * silu_c], axis=1).astype(BF16)
            yv = jnp.dot(mixed, w_ref[...], preferred_element_type=F32)
            return ov, gav, sig_a, silu_a, cb, gc, sig_c, silu_c, bc, mixed, yv

        def loss_and_dy(rs, mixed, yv):
            r2 = lax.rsqrt(jnp.mean(yv * yv, axis=-1, keepdims=True) + NORM_EPS)
            yhat = yv * r2
            diff = (x_ref[rs, :] + yhat * gv) - t_ref[rs, :]
            loss_ref[...] += _rowgroup_sum(diff * diff)
            ev = diff * (1.0 / d_model)
            e_ref[rs, :] = ev
            dg_ref[...] += _rowgroup_sum(ev * yhat)
            eg = ev * gv
            dy = (r2 * (eg - yhat * jnp.mean(eg * yhat, axis=-1, keepdims=True))).astype(BF16)
            dw_ref[...] += _tn(mixed, dy)
            return _nt(dy, w_ref[...])

        def backward(rs, ov, gav, sig_a, silu_a, cb, gc, sig_c, silu_c, bc, dm):
            rows = rs.stop - rs.start
            dma, dmc = dm[:, :attn_w], dm[:, attn_w:]
            dov = dma * silu_a
            do_ref[rs, :] = dov.astype(BF16)
            dga_ref[rs, :] = (dma * ov * (sig_a * (1.0 + gav * (1.0 - sig_a)))).astype(BF16)
            prod = dov * ov
            lane = lax.broadcasted_iota(jnp.int32, (rows, LANES), 1)
            lo = lane < HEAD_DIM
            dblk = jnp.zeros((rows, LANES), F32)
            for p, sl in enumerate(groups):
                pr = prod[:, sl]
                dblk = jnp.where(lane == 2 * p, jnp.sum(jnp.where(lo, pr, 0.0), axis=1, keepdims=True), dblk)
                dblk = jnp.where(lane == 2 * p + 1, jnp.sum(jnp.where(lo, 0.0, pr), axis=1, keepdims=True), dblk)
                _stage_put(stage, p, dov[:, sl], rs.start)
            dl_ref[rs, :] = dblk
            _stage_put(stage, len(groups), dblk, rs.start)
            _stage_put(stage, len(groups) + 1, l_ref[rs, :], rs.start)
            dsc = dmc * silu_c
            cv_rows = cvv[rs]
            dcb_ref[rs, :] = (dsc * cv_rows).astype(BF16)
            dgc_ref[rs, :] = (dmc * bc * (sig_c * (1.0 + gc * (1.0 - sig_c)))).astype(BF16)
            dcv = dsc * cb
            dcv_ref[rs, :] = dcv
            dcw_ref[0:SUBLANES, :] += _rowgroup_sum(dcv * u2[rs])
            dcw_ref[SUBLANES:2 * SUBLANES, :] += _rowgroup_sum(dcv * u1[rs])
            dcw_ref[2 * SUBLANES:3 * SUBLANES, :] += _rowgroup_sum(dcv * u[rs])

        halves = [slice(0, tm // 2), slice(tm // 2, tm)]
        fwd = [forward(rs) for rs in halves]
        dms = [loss_and_dy(rs, f[9], f[10]) for rs, f in zip(halves, fwd)]
        for rs, f, dm in zip(halves, fwd, dms):
            backward(rs, *f[:9], dm)
        for p, sl in enumerate(groups):
            _to_perm(stage, p, dop_ref, sl, BF16)
        _to_perm(stage, len(groups), dlp_ref, all_lanes, F32)
        _to_perm(stage, len(groups) + 1, lp_ref, all_lanes, F32)

    row = lambda n: pl.BlockSpec((tm, n), lambda i: (i, 0))
    whole = lambda a, b: pl.BlockSpec((a, b), lambda i: (0, 0))
    return pl.pallas_call(
        body, name="tail", grid=(nt,),
        out_shape=(jax.ShapeDtypeStruct((seq, attn_w), BF16), jax.ShapeDtypeStruct((seq, LANES), F32),
                   jax.ShapeDtypeStruct(_perm_shape(seq, attn_w), BF16), jax.ShapeDtypeStruct(_perm_shape(seq, LANES), F32),
                   jax.ShapeDtypeStruct(_perm_shape(seq, LANES), F32),
                   jax.ShapeDtypeStruct((seq, attn_w), BF16), jax.ShapeDtypeStruct((seq, conv_w), BF16),
                   jax.ShapeDtypeStruct((seq, conv_w), BF16), jax.ShapeDtypeStruct((seq, conv_w), F32),
                   jax.ShapeDtypeStruct((seq, d_model), F32), jax.ShapeDtypeStruct((mix, d_model), F32),
                   jax.ShapeDtypeStruct((SUBLANES, d_model), F32), jax.ShapeDtypeStruct((CONV_K * SUBLANES, conv_w), F32),
                   jax.ShapeDtypeStruct((SUBLANES, d_model), F32)),
        in_specs=[row(attn_w), row(LANES), row(attn_w), row(4 * conv_w),
                  pl.BlockSpec((SUBLANES, 4 * conv_w), lambda i: (jnp.maximum(i * hb - 1, 0), 0)),
                  row(d_model), row(d_model), _const_spec((mix, d_model)), _const_spec((1, d_model)),
                  _const_spec((SUBLANES, conv_w))],
        out_specs=(row(attn_w), row(LANES), _perm_tile_spec(attn_w, tm), _perm_tile_spec(LANES, tm), _perm_tile_spec(LANES, tm),
                   row(attn_w), row(conv_w), row(conv_w), row(conv_w), row(d_model),
                   whole(mix, d_model), whole(SUBLANES, d_model), whole(CONV_K * SUBLANES, conv_w),
                   whole(SUBLANES, d_model)),
        scratch_shapes=[pltpu.VMEM(_stage_shape(len(groups) + 2, tm), F32)],
        compiler_params=_params(("arbitrary",)),
    )(o, lse, ga, cz, cz, x, tgt, w_out, g2, cw)


def dz_dx(nat_grads, perm_grads, dga, dcb, dgc, dcv, cz, tables, x, g1, e, w_full, cw):
    seq, d_model = x.shape
    attn_w = dga.shape[1]
    conv_w = dcv.shape[1]
    width = w_full.shape[2]
    in_w = 4 * attn_w + 4 * conv_w
    groups = _lane_groups(attn_w)
    tm = ROW_TILE
    nt = seq // tm
    hb = tm // SUBLANES

    def body(dq_ref, dk_ref, dv_ref, dqp_ref, dkp_ref, dvp_ref, dga_ref, dcb_ref, dgc_ref, dcv_ref, nh_ref, ch_ref, cc_ref,
             cos_ref, s1_ref, s2_ref, x_ref, g_ref, e_ref, w_ref, cw_ref, gx_ref, dz_ref, dg_ref, stage):
        i = pl.program_id(0)

        @pl.when(i == 0)
        def _():
            dg_ref[...] = jnp.zeros_like(dg_ref)

        cos, s1, s2 = cos_ref[...], s1_ref[...], s2_ref[...]

        def qkv_columns(t, nat_ref, perm_ref):
            for g, sl in enumerate(groups):
                _from_perm(perm_ref, sl, stage, g)
            for g, sl in enumerate(groups):
                tot = nat_ref[:, sl].astype(F32) + _stage_get(stage, g)
                if t < 2:
                    tot = _rope_transposed(tot, cos, s1, s2)
                dz_ref[:, t * attn_w + g * LANES:t * attn_w + (g + 1) * LANES] = tot.astype(BF16)

        def dh_part(j):
            return _nt(dz_ref[:, j * width:(j + 1) * width], w_ref[j])

        dcv = dcv_ref[...]
        nh = jnp.where(i < nt - 1, nh_ref[...], 0.0)
        w0, w1, w2 = cw_ref[0:1, :], cw_ref[1:2, :], cw_ref[2:3, :]
        du = dcv * w2 + _shift_up(dcv, nh, 1) * w1 + _shift_up(dcv, nh, 2) * w0
        base = 4 * attn_w
        dz_ref[:, base:base + conv_w] = (du * cc_ref[...]).astype(BF16)
        dz_ref[:, base + conv_w:base + 2 * conv_w] = dcb_ref[...]
        dz_ref[:, base + 2 * conv_w:base + 3 * conv_w] = (du * ch_ref[...]).astype(BF16)
        dz_ref[:, base + 3 * conv_w:base + 4 * conv_w] = dgc_ref[...]
        dz_ref[:, 3 * attn_w:4 * attn_w] = dga_ref[...]
        ready = in_w
        dh = None
        for t, nat_ref, perm_ref in ((2, dv_ref, dvp_ref), (1, dk_ref, dkp_ref), (0, dq_ref, dqp_ref), (None, None, None)):
            lowest_open = 0 if t is None else (t + 1) * attn_w
            while ready - width >= lowest_open:
                ready -= width
                part = dh_part(ready // width)
                dh = part if dh is None else dh + part
            if t is not None:
                qkv_columns(t, nat_ref, perm_ref)
        xv = x_ref[...]
        r1 = lax.rsqrt(jnp.mean(xv * xv, axis=-1, keepdims=True) + NORM_EPS)
        xhat = xv * r1
        dg_ref[...] += _rowgroup_sum(dh * xhat)
        dhg = dh * g_ref[...]
        gx_ref[...] = r1 * (dhg - xhat * jnp.mean(dhg * xhat, axis=-1, keepdims=True)) + e_ref[...]

    row = lambda n: pl.BlockSpec((tm, n), lambda i: (i, 0))
    whole = lambda a, b: pl.BlockSpec((a, b), lambda i: (0, 0))
    pt = _perm_tile_spec(attn_w, tm)
    return pl.pallas_call(
        body, name="dz_dx", grid=(nt,),
        out_shape=(jax.ShapeDtypeStruct((seq, d_model), F32), jax.ShapeDtypeStruct((seq, in_w), BF16),
                   jax.ShapeDtypeStruct((SUBLANES, d_model), F32)),
        in_specs=[row(attn_w), row(attn_w), row(attn_w), pt, pt, pt, row(attn_w), row(conv_w), row(conv_w), row(conv_w),
                  pl.BlockSpec((SUBLANES, conv_w), lambda i: (jnp.minimum((i + 1) * hb, seq // SUBLANES - 1), 0)),
                  pl.BlockSpec((tm, conv_w), lambda i: (i, 0)), pl.BlockSpec((tm, conv_w), lambda i: (i, 2)),
                  row(LANES), row(LANES), row(LANES), row(d_model), _const_spec((1, d_model)), row(d_model),
                  _const_spec(w_full.shape), _const_spec((SUBLANES, conv_w))],
        out_specs=(row(d_model), row(in_w), whole(SUBLANES, d_model)),
        scratch_shapes=[pltpu.VMEM(_stage_shape(len(groups), tm), F32)],
        compiler_params=_params(("arbitrary",)),
    )(*nat_grads, *perm_grads, dga, dcb, dgc, dcv, dcv, cz, cz, *tables, x, g1, e, w_full, cw)


def dw_in_reduce(ht, dz, g_out, small):
    d_model, seq = ht.shape
    half = dz.shape[1] // N_DEV
    ts = min(DW_ROWS, seq)
    steps = seq // ts
    x, y, c = lax.axis_index("x"), lax.axis_index("y"), lax.axis_index("c")
    far_first = lambda x, y: [(1 - x, 1 - y), (1 - x, y), (x, 1 - y)]
    chips = jnp.stack([2 * px + py for px, py in far_first(x, y)] + [2 * x + y]).astype(jnp.int32)
    order = jnp.stack([2 * chips + (1 - c), 2 * chips + c], axis=1).reshape(N_DEV)

    def body(order_ref, ht_ref, dz_ref, go_ref, sm_ref, out_ref, ro_ref, rs_ref,
             acc, theirs, staged, contrib, resbuf, out_sem, sa, ra, sb, rb, sc, rc,
             o_mine, o_theirs, o_staged, o_contrib, o_res, sbuf, o_load, osa, ora, osb, orb, osc, orc, ss, rs):
        del order_ref
        p, s = pl.program_id(0), pl.program_id(1)
        x, y, c = lax.axis_index("x"), lax.axis_index("y"), lax.axis_index("c")
        me = 2 * x + y
        sib = (x, y, 1 - c)
        peers = far_first(x, y)
        slot = p % 2

        flips = [(fx, fy, fc) for fx in (0, 1) for fy in (0, 1) for fc in (0, 1)][1:]
        my8 = 4 * x + 2 * y + c
        chip_ids = [2 * px + py for px, py in peers] + [me]

        def small_copy(k, slot8, to):
            return pltpu.make_async_remote_copy(src_ref=sm_ref, dst_ref=sbuf.at[slot8], send_sem=ss.at[k], recv_sem=rs.at[k],
                                                device_id=to, device_id_type=MESH)

        def small_peer(k):
            fx, fy, fc = flips[k]
            return _flip(x, fx), _flip(y, fy), _flip(c, fc)

        def oa_copy(pos):
            j = chip_ids[pos]
            return pltpu.make_async_remote_copy(src_ref=go_ref.at[j, 1 - c], dst_ref=o_theirs.at[j], send_sem=osa.at[pos],
                                                recv_sem=ora.at[pos], device_id=sib, device_id_type=MESH)

        def o_load_copy(pos):
            j = chip_ids[pos]
            return pltpu.make_async_copy(go_ref.at[j, c], o_mine.at[j], o_load.at[pos])

        def ob_copy(k, piece, slot4):
            px, py = peers[k]
            return pltpu.make_async_remote_copy(src_ref=o_staged.at[piece], dst_ref=o_contrib.at[slot4], send_sem=osb.at[k],
                                                recv_sem=orb.at[k], device_id=(px, py, c), device_id_type=MESH)

        def oc_copy(which):
            return pltpu.make_async_remote_copy(src_ref=o_res.at[which], dst_ref=o_res.at[which], send_sem=osc, recv_sem=orc,
                                                device_id=sib, device_id_type=MESH)

        @pl.when((p == 0) & (s == 0))
        def _():
            sbuf[my8] = sm_ref[...]
            for k in range(N_DEV - 1):
                small_copy(k, my8, small_peer(k)).start()
            for pos in range(N_CHIPS):
                o_load_copy(pos).start()
                oa_copy(pos).start()

        @pl.when((p == 1) & (s == steps - 1))
        def _():
            for pos in range(N_CHIPS):
                j = chip_ids[pos]
                o_load_copy(pos).wait()
                oa_copy(pos).wait_recv()
                if pos < N_CHIPS - 1:
                    o_staged[j] = (o_mine[j] + o_theirs[j]).astype(BF16)
                    ob_copy(pos, j, me).start()
                else:
                    o_mine[j] = o_mine[j] + o_theirs[j]
                    o_contrib[j] = o_mine[j].astype(BF16)

        @pl.when((p == 4) & (s == steps - 1))
        def _():
            for k in range(N_CHIPS - 1):
                ob_copy(k, me, chip_ids[k]).wait_recv()
            own = o_mine[me]
            term = lambda j: jnp.where(me == j, own, o_contrib[j].astype(F32))
            o_res[c] = ((term(0) + term(1)) + term(2)) + term(3)
            oc_copy(c).start()

        def a_copy(k):
            return pltpu.make_async_remote_copy(src_ref=acc.at[0], dst_ref=theirs.at[k], send_sem=sa.at[k], recv_sem=ra.at[k],
                                                device_id=sib, device_id_type=MESH)

        def b_copy(k):
            px, py = peers[k]
            return pltpu.make_async_remote_copy(src_ref=staged.at[k], dst_ref=contrib.at[k], send_sem=sb.at[k], recv_sem=rb.at[k],
                                                device_id=(px, py, c), device_id_type=MESH)

        def c_copy(which):
            return pltpu.make_async_remote_copy(src_ref=resbuf.at[which], dst_ref=resbuf.at[which], send_sem=sc, recv_sem=rc,
                                                device_id=sib, device_id_type=MESH)

        @pl.when(s == 0)
        def _():
            for k in range(N_CHIPS - 1):
                @pl.when(p == 2 * k + 2)
                def _():
                    a_copy(k).wait_send()
            acc[slot] = jnp.dot(ht_ref[...], dz_ref[...], preferred_element_type=F32)

        @pl.when(s > 0)
        def _():
            acc[slot] += jnp.dot(ht_ref[...], dz_ref[...], preferred_element_type=F32)

        @pl.when(s == steps - 1)
        def _():
            for k in range(N_CHIPS):
                @pl.when(p == 2 * k)
                def _():
                    a_copy(k).start()
            for k in range(N_CHIPS - 1):
                @pl.when(p == 2 * k + 1)
                def _():
                    a_copy(k).wait_recv()
                    staged[k] = (acc[1] + theirs[k]).astype(BF16)
                    b_copy(k).start()

            @pl.when(p == N_DEV - 1)
            def _():
                a_copy(N_CHIPS - 1).wait_recv()
                tot = acc[1] + theirs[N_CHIPS - 1]
                for k in range(N_CHIPS - 1):
                    b_copy(k).wait_recv()
                    tot = tot + contrib[k].astype(F32)
                resbuf[c] = tot
                c_copy(c).start()
                c_copy(1 - c).wait_recv()
                done = pltpu.make_async_copy(resbuf, out_ref, out_sem)
                done.start()
                oc_copy(1 - c).wait_recv()
                ro_ref[...] = o_res[...]
                for k in range(N_DEV - 1):
                    px, py, pc = small_peer(k)
                    small_copy(k, 4 * px + 2 * py + pc, (px, py, pc)).wait_recv()
                tot8 = sbuf[0]
                for d in range(1, N_DEV):
                    tot8 = tot8 + sbuf[d]
                rs_ref[...] = tot8
                a_copy(N_CHIPS - 1).wait_send()
                for k in range(N_CHIPS - 1):
                    b_copy(k).wait_send()
                    ob_copy(k, chip_ids[k], me).wait_send()
                c_copy(c).wait_send()
                oc_copy(c).wait_send()
                for pos in range(N_CHIPS):
                    oa_copy(pos).wait_send()
                for k in range(N_DEV - 1):
                    small_copy(k, my8, small_peer(k)).wait_send()
                done.wait()

    dma = pltpu.SemaphoreType.DMA
    o_shape = g_out.shape[1:]
    go = g_out.reshape(N_CHIPS, 2, *o_shape)
    const = lambda shape: pl.BlockSpec(shape, lambda p, s, order_ref: (0,) * len(shape))
    grid_spec = pltpu.PrefetchScalarGridSpec(
        num_scalar_prefetch=1, grid=(N_DEV, steps),
        in_specs=[pl.BlockSpec((d_model, ts), lambda p, s, order_ref: (0, s)),
                  pl.BlockSpec((ts, half), lambda p, s, order_ref: (s, order_ref[p])),
                  pl.BlockSpec(memory_space=pl.ANY), const(small.shape)],
        out_specs=(pl.BlockSpec(memory_space=pl.ANY), const((2, *o_shape)), const(small.shape)),
        scratch_shapes=[pltpu.VMEM((2, d_model, half), F32), pltpu.VMEM((N_CHIPS, d_model, half), F32),
                        pltpu.VMEM((N_CHIPS - 1, d_model, half), BF16), pltpu.VMEM((N_CHIPS - 1, d_model, half), BF16),
                        pltpu.VMEM((2, d_model, half), F32), dma,
                        dma((N_CHIPS,)), dma((N_CHIPS,)), dma((N_CHIPS - 1,)), dma((N_CHIPS - 1,)), dma, dma,
                        pltpu.VMEM((N_CHIPS, *o_shape), F32), pltpu.VMEM((N_CHIPS, *o_shape), F32),
                        pltpu.VMEM((N_CHIPS, *o_shape), BF16), pltpu.VMEM((N_CHIPS, *o_shape), BF16),
                        pltpu.VMEM((2, *o_shape), F32), pltpu.VMEM((N_DEV, *small.shape), F32),
                        dma((N_CHIPS,)), dma((N_CHIPS,)), dma((N_CHIPS,)), dma((N_CHIPS - 1,)), dma((N_CHIPS - 1,)), dma, dma,
                        dma((N_DEV - 1,)), dma((N_DEV - 1,))])
    return pl.pallas_call(
        body, name="dw_in_reduce", grid_spec=grid_spec,
        out_shape=(jax.ShapeDtypeStruct((2, d_model, half), F32), jax.ShapeDtypeStruct((2, *o_shape), F32),
                   jax.ShapeDtypeStruct(small.shape, F32)),
        compiler_params=_params(("arbitrary", "arbitrary")),
    )(order, ht, dz, go, small)


def _adam_math(w, g, m, v):
    m = ADAM_B1 * m + (1.0 - ADAM_B1) * g
    v = ADAM_B2 * v + (1.0 - ADAM_B2) * (g * g)
    m_hat = m / (1.0 - ADAM_B1 ** ADAM_STEP)
    v_hat = v / (1.0 - ADAM_B2 ** ADAM_STEP)
    delta = -ADAM_LR * (m_hat / (jnp.sqrt(v_hat) + ADAM_EPS) + ADAM_WD * w)
    return delta, m, v


def adam_shard(name, w, g2, m, v, block, grid, w_map, g_map):
    def body(w_ref, g_ref, m_ref, v_ref, go_ref, d_ref, mo_ref, vo_ref):
        g = g_ref[0]
        delta, mn, vn = _adam_math(w_ref[...], g, m_ref[...], v_ref[...])
        go_ref[...] = g
        d_ref[...] = delta
        mo_ref[...] = mn
        vo_ref[...] = vn

    ws = pl.BlockSpec(block, w_map)
    shp = jax.ShapeDtypeStruct(w.shape, F32)
    return pl.pallas_call(
        body, name=name, grid=grid, out_shape=(shp, shp, shp, shp),
        in_specs=[ws, pl.BlockSpec((1, *block), g_map), ws, ws], out_specs=(ws, ws, ws, ws),
        compiler_params=_params(("arbitrary",) * len(grid)),
    )(w, g2, m, v)


def adam_small(ws, gs, ms, vs):
    n = len(ws)

    def body(*refs):
        ins, outs = refs[:4 * n], refs[4 * n:]
        for t in range(n):
            delta, mn, vn = _adam_math(ins[t][...], ins[n + t][...], ins[2 * n + t][...], ins[3 * n + t][...])
            outs[3 * t][...] = delta
            outs[3 * t + 1][...] = mn
            outs[3 * t + 2][...] = vn

    vm = pl.BlockSpec(memory_space=pltpu.VMEM)
    outs = pl.pallas_call(
        body, name="adam_small",
        out_shape=tuple(jax.ShapeDtypeStruct(w.shape, F32) for w in ws for _ in range(3)),
        in_specs=[vm] * (4 * n), out_specs=tuple([vm] * (3 * n)),
        compiler_params=_params(),
    )(*ws, *gs, *ms, *vs)
    return [outs[3 * t:3 * t + 3] for t in range(n)]


def kernel(x, norm_pre_g, w_in, conv_w, w_out, norm_post_g, loss_target, m_norm_pre_g, m_w_in, m_conv_w, m_w_out, m_norm_post_g, v_norm_pre_g, v_w_in, v_conv_w, v_w_out, v_norm_post_g):
    _, seq, d_model = x.shape
    width = w_in.shape[1]
    conv_q = conv_w.shape[1]
    conv_width = N_CHIPS * conv_q
    attn_width = d_model - conv_width
    xs, tg = x[0], loss_target[0]
    g1, g2 = norm_pre_g.reshape(1, d_model), norm_post_g.reshape(1, d_model)

    w_full, wout_full, cw_full, *tables = gather_weights(w_in, w_out, conv_w, seq)
    wout2 = wout_full.reshape(attn_width + conv_width, d_model)
    cw = jnp.zeros((SUBLANES, conv_width), F32).at[:CONV_K].set(
        cw_full[:, :CONV_K, :conv_q].transpose(1, 0, 2).reshape(CONV_K, conv_width))

    ht, q, k, v, qp, kp, vp, ga, cz = inproj(xs, g1, w_full, tables, attn_width, conv_width)
    run = attn_fwd("p4", qp, kp, vp, None)
    run = attn_fwd("p16", qp, kp, vp, run)
    o, lse = attn_fwd("nat", q, k, v, run)
    (d_o, delta, d_op, delta_p, lse_p, dga, dcb, dgc, dcv, e, dwout, dg2, dcw, loss_acc) = tail(
        o, lse, ga, cz, xs, tg, wout2, g2, cw)
    nat_grads = attn_bwd("nat", q, k, v, d_o, lse, delta, None)
    perm_grads = attn_bwd("p4", qp, kp, vp, d_op, lse_p, delta_p, None)
    perm_grads = attn_bwd("p16", qp, kp, vp, d_op, lse_p, delta_p, perm_grads)
    grad_x, dz, dg1 = dz_dx(nat_grads, perm_grads, dga, dcb, dgc, dcv, cz, tables, xs, g1, e, w_full, cw)

    small = jnp.zeros((SUBLANES, d_model), F32)
    small = small.at[0].set(dg1.sum(axis=0)).at[1].set(dg2.sum(axis=0))
    small = small.at[2:2 + CONV_K, :conv_width].set(dcw.reshape(CONV_K, SUBLANES, conv_width).sum(axis=1))
    small = small.at[2 + CONV_K, 0].set(jnp.sum(loss_acc))
    rin, rout, rsmall = dw_in_reduce(ht, dz, dwout.reshape(N_DEV, -1, d_model), small)

    half = width // 2
    tr = min(ADAM_ROWS, d_model)
    gw_in, d_in, m_in, v_in = adam_shard(
        "adam_w_in", w_in, rin, m_w_in, v_w_in, (tr, half), (2, d_model // tr),
        lambda hf, i: (i, hf), lambda hf, i: (hf, i, 0))
    rq = w_out.shape[0] // 2
    gw_out, d_out, m_out, v_out = adam_shard(
        "adam_w_out", w_out, rout, m_w_out, v_w_out, (rq, d_model), (2,),
        lambda hf: (hf, 0), lambda hf: (hf, 0, 0))

    chip = 2 * lax.axis_index("x") + lax.axis_index("y")
    g_pre, g_post = rsmall[0:1], rsmall[1:2]
    g_conv = lax.dynamic_slice(rsmall[2:2 + CONV_K, :conv_width], (0, chip * conv_q), (CONV_K, conv_q))
    (d_pre, m_pre, v_pre), (d_post, m_post, v_post), (d_cv, m_cv, v_cv) = adam_small(
        [g1, g2, conv_w], [g_pre, g_post, g_conv],
        [m_norm_pre_g.reshape(1, d_model), m_norm_post_g.reshape(1, d_model), m_conv_w],
        [v_norm_pre_g.reshape(1, d_model), v_norm_post_g.reshape(1, d_model), v_conv_w])

    loss = 0.5 * rsmall[2 + CONV_K, 0] / d_model
    vec = lambda a: a.reshape(d_model)
    return (loss, grad_x.reshape(1, seq, d_model),
            vec(g_pre), gw_in, g_conv, gw_out, vec(g_post),
            vec(d_pre), d_in, d_cv, d_out, vec(d_post),
            vec(m_pre), m_in, m_cv, m_out, vec(m_post),
            vec(v_pre), v_in, v_cv, v_out, vec(v_post))
```

```python
import jax
import jax.numpy as jnp
from jax import lax
from jax.experimental import pallas as pl
from jax.experimental.pallas import tpu as pltpu

HEAD_DIM = 64
LANES = 128
SUBLANES = 8
BLOCK = 128
HALF_BLOCK = BLOCK // 2
WINDOW_KEYS = 128
PERM = 16
PJ = 4
P4_ROWS = BLOCK // PJ
MAX_QUERY_BLOCKS = 8
ROW_TILE = 512
DW_ROWS = 4096
ADAM_ROWS = 1024
CONV_K = 3
ROPE_THETA = 10000.0
NORM_EPS = 1e-6
ATTN_SCALE = HEAD_DIM ** -0.5
NEG = -1e30
N_CHIPS = 4
N_DEV = 8
MESH = pl.DeviceIdType.MESH
ADAM_LR = 0.001
ADAM_B1 = 0.9
ADAM_B2 = 0.999
ADAM_EPS = 1e-08
ADAM_WD = 0.01
ADAM_STEP = 10
VMEM_LIMIT = 63 * 1024 * 1024

F32 = jnp.float32
BF16 = jnp.bfloat16


def _params(sem=None, **kw):
    return pltpu.CompilerParams(dimension_semantics=sem, vmem_limit_bytes=VMEM_LIMIT, **kw)


def _const_spec(shape):
    return pl.BlockSpec(shape, lambda *_: (0,) * len(shape), pipeline_mode=pl.Buffered(1))


def _sigmoid(z):
    return 1.0 / (1.0 + jnp.exp(-z))


def _rowgroup_sum(a):
    rows, n = a.shape
    return a.reshape(rows // SUBLANES, SUBLANES, n).sum(axis=0)


def _nt(a, b):
    return lax.dot_general(a, b, (((1,), (1,)), ((), ())), preferred_element_type=F32)


def _tn(a, b):
    return lax.dot_general(a, b, (((0,), (0,)), ((), ())), preferred_element_type=F32)


def _col_pieces(a, b, width):
    out = []
    while a < b:
        j = a // width
        e = min(b, (j + 1) * width)
        out.append((j, a - j * width, e - j * width))
        a = e
    return out


def _lane_groups(width):
    return [slice(g * LANES, (g + 1) * LANES) for g in range(width // LANES)]


def _perm_shape(seq, width):
    return (PJ, PJ, seq // PERM, width)


def _perm_tile_spec(width, tm):
    return pl.BlockSpec((PJ, PJ, tm // PERM, width), lambda i: (0, 0, i, 0))


STAGE_PITCH = 24


def _stage_shape(groups, rows):
    return (groups, rows // PERM * STAGE_PITCH, LANES)


def _stage_put(stage, g, val, row0=0):
    for a in range(val.shape[0] // PERM):
        at = (row0 // PERM + a) * STAGE_PITCH
        stage[g, at:at + PERM, :] = val[a * PERM:(a + 1) * PERM]


def _stage_get(stage, g):
    return jnp.concatenate([stage[g, a * STAGE_PITCH:a * STAGE_PITCH + PERM, :]
                            for a in range(stage.shape[1] // STAGE_PITCH)], axis=0)


def _to_perm(stage, g, dst_ref, sl, dtype):
    rows = stage.shape[1] // STAGE_PITCH
    for b in range(PERM):
        dst_ref[b // PJ, b % PJ, :, sl] = stage[g, pl.ds(b, rows, stride=STAGE_PITCH), :].astype(dtype)


def _from_perm(src_ref, sl, stage, g):
    rows = stage.shape[1] // STAGE_PITCH
    for b in range(PERM):
        stage[g, pl.ds(b, rows, stride=STAGE_PITCH), :] = src_ref[b // PJ, b % PJ, :, sl].astype(F32)


def _flip(a, f):
    return 1 - a if f else a


def gather_weights(w_in, w_out, conv_w, seq):
    d_model, width = w_in.shape
    rows = w_out.shape[0]
    cw = jnp.zeros((SUBLANES, LANES), F32).at[:CONV_K, :conv_w.shape[1]].set(conv_w)
    half_dim = HEAD_DIM // 2
    inv_freq = ROPE_THETA ** (-jnp.arange(half_dim, dtype=F32) * 2.0 / HEAD_DIM)
    inv_freq = jnp.tile(inv_freq, LANES // half_dim).reshape(1, LANES)
    chunk = min(ROW_TILE, seq)

    def body(win_ref, wout_ref, cw_ref, freq_ref, winf_ref, woutf_ref, cwf_ref, cos_ref, s1_ref, s2_ref,
             st_in, st_out, near_send, near_recv, far_send, far_recv, cw_send, cw_recv, d2d_send, d2d_recv):
        x, y, c = lax.axis_index("x"), lax.axis_index("y"), lax.axis_index("c")
        me = 2 * x + y
        sib = (x, y, 1 - c)
        st_in[...] = win_ref[...].astype(BF16)
        st_out[...] = wout_ref[...].astype(BF16)
        winf_ref[me] = st_in[...]
        woutf_ref[me] = st_out[...]
        cwf_ref[me] = cw_ref[...]
        stages = (st_in, st_out)
        fulls = (winf_ref, woutf_ref)
        halves = (d_model // 2, rows // 2)

        def part(t, core, q=None):
            size = halves[t] if q is None else halves[t] // 2
            start = core * halves[t] if q is None else core * halves[t] + q * size
            return pl.ds(pl.multiple_of(start, size), size)

        near = [(1 - x, y), (x, 1 - y)]
        far = (1 - x, 1 - y)
        chip = lambda px, py: 2 * px + py

        def direct(k, t, q, slot, to):
            src = stages[t].at[part(t, c, q)]
            return pltpu.make_async_remote_copy(src_ref=src, dst_ref=fulls[t].at[slot, part(t, c, q)], send_sem=near_send.at[k, t, q],
                                                recv_sem=near_recv.at[k, t, q], device_id=to, device_id_type=MESH)

        def passed_on(k, t, slot, to):
            ref = fulls[t].at[slot, part(t, c, k)]
            return pltpu.make_async_remote_copy(src_ref=ref, dst_ref=ref, send_sem=far_send.at[k, t], recv_sem=far_recv.at[k, t],
                                                device_id=to, device_id_type=MESH)

        def conv_copy(k, slot, to):
            return pltpu.make_async_remote_copy(src_ref=cw_ref, dst_ref=cwf_ref.at[slot], send_sem=cw_send.at[k], recv_sem=cw_recv.at[k],
                                                device_id=to, device_id_type=MESH)

        def d2d(k, t, slot, core):
            ref = fulls[t].at[slot, part(t, core)]
            return pltpu.make_async_remote_copy(src_ref=ref, dst_ref=ref, send_sem=d2d_send.at[k, t], recv_sem=d2d_recv.at[k, t],
                                                device_id=sib, device_id_type=MESH)

        sends = []

        def go(cp):
            cp.start()
            sends.append(cp)

        for q_first in (0, 1):
            for k, (px, py) in enumerate(near):
                for t in range(2):
                    go(direct(k, t, k if q_first == 0 else 1 - k, me, (px, py, c)))
        for k, (px, py) in enumerate(near + [far]):
            go(conv_copy(k, me, (px, py, c)))
        for k, (px, py) in enumerate(near):
            other = near[1 - k]
            for t in range(2):
                direct(k, t, k, chip(px, py), (px, py, c)).wait_recv()
                go(passed_on(k, t, chip(px, py), (*other, c)))

        first_half = lax.broadcasted_iota(jnp.int32, (chunk, LANES), 1) % HEAD_DIM < half_dim
        row = lax.broadcasted_iota(jnp.int32, (chunk, LANES), 0)

        def table_rows(i, carry):
            at = pl.multiple_of(i * chunk, chunk)
            ang = (row + at).astype(F32) * freq_ref[...]
            sin = jnp.sin(ang)
            cos_ref[pl.ds(at, chunk), :] = jnp.cos(ang)
            s1_ref[pl.ds(at, chunk), :] = jnp.where(first_half, -sin, 0.0)
            s2_ref[pl.ds(at, chunk), :] = jnp.where(first_half, 0.0, sin)
            return carry

        lax.fori_loop(0, seq // chunk, table_rows, 0)

        for k, (px, py) in enumerate(near):
            for t in range(2):
                direct(k, t, 1 - k, chip(px, py), (px, py, c)).wait_recv()
                go(d2d(k, t, chip(px, py), c))
        for t in range(2):
            for k, (px, py) in enumerate(near):
                passed_on(k, t, chip(*far), (px, py, c)).wait_recv()
            go(d2d(2, t, chip(*far), c))
        for k, (px, py) in enumerate(near + [far]):
            conv_copy(k, chip(px, py), (px, py, c)).wait_recv()
            for t in range(2):
                d2d(k, t, chip(px, py), 1 - c).wait_recv()
        for cp in sends:
            cp.wait_send()

    vm = pl.BlockSpec(memory_space=pltpu.VMEM)
    dma = pltpu.SemaphoreType.DMA
    return pl.pallas_call(
        body, name="gather_weights",
        out_shape=(jax.ShapeDtypeStruct((N_CHIPS, d_model, width), BF16),
                   jax.ShapeDtypeStruct((N_CHIPS, rows, d_model), BF16),
                   jax.ShapeDtypeStruct((N_CHIPS, SUBLANES, LANES), F32),
                   *[jax.ShapeDtypeStruct((seq, LANES), F32)] * 3),
        in_specs=[vm, vm, vm, vm], out_specs=(vm,) * 6,
        scratch_shapes=[pltpu.VMEM((d_model, width), BF16), pltpu.VMEM((rows, d_model), BF16),
                        dma((2, 2, 2)), dma((2, 2, 2)), dma((2, 2)), dma((2, 2)), dma((3,)), dma((3,)),
                        dma((3, 2)), dma((3, 2))],
        compiler_params=_params(),
    )(w_in, w_out, cw, inv_freq)


def _rope(t, cos, s1, s2):
    return t * cos + pltpu.roll(t, LANES - HEAD_DIM // 2, 1) * s1 + pltpu.roll(t, HEAD_DIM // 2, 1) * s2


def _rope_transposed(g, cos, s1, s2):
    return g * cos + pltpu.roll(g * s1, HEAD_DIM // 2, 1) + pltpu.roll(g * s2, LANES - HEAD_DIM // 2, 1)


def inproj(x, g1, w_full, tables, attn_w, conv_w):
    seq, d_model = x.shape
    width = w_full.shape[2]
    tm = ROW_TILE
    groups = _lane_groups(attn_w)

    def body(x_ref, g_ref, w_ref, cos_ref, s1_ref, s2_ref,
             ht_ref, q_ref, k_ref, v_ref, qp_ref, kp_ref, vp_ref, ga_ref, cz_ref, stage):
        xv = x_ref[...]
        hb = ((xv * lax.rsqrt(jnp.mean(xv * xv, axis=-1, keepdims=True) + NORM_EPS)) * g_ref[...]).astype(BF16)
        ht_ref[...] = jnp.transpose(hb)
        cos, s1, s2 = cos_ref[...], s1_ref[...], s2_ref[...]

        def proj(a, b):
            parts = [jnp.dot(hb, w_ref[j, :, lo:hi], preferred_element_type=F32) for j, lo, hi in _col_pieces(a, b, width)]
            return parts[0] if len(parts) == 1 else jnp.concatenate(parts, axis=1)

        def emit(z, nat_ref, perm_ref, fn):
            for g, sl in enumerate(groups):
                val = fn(z[:, sl])
                nat_ref[:, sl] = val.astype(BF16)
                _stage_put(stage, g, val)
            for g, sl in enumerate(groups):
                _to_perm(stage, g, perm_ref, sl, BF16)

        emit(proj(0, attn_w), q_ref, qp_ref, lambda t: _rope(t, cos, s1, s2) * ATTN_SCALE)
        emit(proj(attn_w, 2 * attn_w), k_ref, kp_ref, lambda t: _rope(t, cos, s1, s2))
        emit(proj(2 * attn_w, 3 * attn_w), v_ref, vp_ref, lambda t: t)
        ga_ref[...] = proj(3 * attn_w, 4 * attn_w)
        cz_ref[...] = proj(4 * attn_w, 4 * attn_w + 4 * conv_w)

    row = lambda n: pl.BlockSpec((tm, n), lambda i: (i, 0))
    nat = jax.ShapeDtypeStruct((seq, attn_w), BF16)
    perm = jax.ShapeDtypeStruct(_perm_shape(seq, attn_w), BF16)
    return pl.pallas_call(
        body, name="inproj", grid=(seq // tm,),
        out_shape=(jax.ShapeDtypeStruct((d_model, seq), BF16), nat, nat, nat, perm, perm, perm,
                   jax.ShapeDtypeStruct((seq, attn_w), F32), jax.ShapeDtypeStruct((seq, 4 * conv_w), F32)),
        in_specs=[row(d_model), _const_spec((1, d_model)), _const_spec(w_full.shape), row(LANES), row(LANES), row(LANES)],
        out_specs=(pl.BlockSpec((d_model, tm), lambda i: (0, i)), row(attn_w), row(attn_w), row(attn_w),
                   _perm_tile_spec(attn_w, tm), _perm_tile_spec(attn_w, tm), _perm_tile_spec(attn_w, tm),
                   row(attn_w), row(4 * conv_w)),
        scratch_shapes=[pltpu.VMEM(_stage_shape(len(groups), tm), F32)],
        compiler_params=_params(("arbitrary",)),
    )(x, g1, w_full, *tables)


class _Mode:
    def __init__(self, name, seq):
        self.name = name
        if name == "nat":
            self.residues, blocks = 1, seq // BLOCK
        elif name == "p16":
            self.residues, blocks = PERM, seq // PERM // BLOCK
        else:
            self.residues, blocks = PJ, seq // PERM // P4_ROWS
        self.qb = max(d for d in range(1, MAX_QUERY_BLOCKS + 1) if blocks % d == 0)
        self.steps = blocks // self.qb

    def _spec(self, blocks, width, at):
        if self.name == "nat":
            return pl.BlockSpec((blocks * BLOCK, width), lambda *g: (at(*g)[1], 0))
        if self.name == "p16":
            return pl.BlockSpec((1, 1, blocks * BLOCK, width), lambda *g: (at(*g)[0] // PJ, at(*g)[0] % PJ, at(*g)[1], 0))
        return pl.BlockSpec((PJ, 1, blocks * P4_ROWS, width), lambda *g: (0, at(*g)[0], at(*g)[1], 0))

    def wide(self, width, where=lambda r, n: (r, n)):
        return self._spec(self.qb, width, where)

    def block_before(self, width, where=lambda r, n: (r, n)):
        return self._spec(1, width, lambda *g: (where(*g)[0], jnp.maximum(self.qb * where(*g)[1] - 1, 0)))

    def get(self, ref, sl, sub=0):
        if self.name == "nat":
            return ref[sub * BLOCK:(sub + 1) * BLOCK, sl]
        if self.name == "p16":
            return ref[0, 0, sub * BLOCK:(sub + 1) * BLOCK, sl]
        return jnp.concatenate([ref[j, 0, at:at + P4_ROWS // 2, sl] for j, at in self._p4_chunks(sub)], axis=0)

    def put(self, ref, sl, val, sub=0):
        val = val.astype(ref.dtype)
        if self.name == "nat":
            ref[sub * BLOCK:(sub + 1) * BLOCK, sl] = val
        elif self.name == "p16":
            ref[0, 0, sub * BLOCK:(sub + 1) * BLOCK, sl] = val
        else:
            for i, (j, at) in enumerate(self._p4_chunks(sub)):
                ref[j, 0, at:at + P4_ROWS // 2, sl] = val[i * (P4_ROWS // 2):(i + 1) * (P4_ROWS // 2)]

    @staticmethod
    def _p4_chunks(sub):
        return [(j, sub * P4_ROWS + half * (P4_ROWS // 2)) for half in (0, 1) for j in range(PJ)]

    def keys(self, before_ref, wide_ref, sl, sub):
        older = self.get(before_ref, sl) if sub == 0 else self.get(wide_ref, sl, sub - 1)
        return jnp.concatenate([older, self.get(wide_ref, sl, sub)], axis=0)

    def index(self, idx, is_key):
        if self.name != "p4":
            return idx - BLOCK if is_key else idx
        within = jnp.bitwise_and(idx, BLOCK - 1)
        chunk = P4_ROWS // 2
        half = jnp.right_shift(within, HALF_BLOCK.bit_length() - 1)
        j = jnp.bitwise_and(jnp.right_shift(within, chunk.bit_length() - 1), PJ - 1)
        m = PJ * (chunk * half + jnp.bitwise_and(within, chunk - 1)) + j
        return m + BLOCK * (jnp.right_shift(idx, BLOCK.bit_length() - 1) - 1) if is_key else m

    def bias(self, has_before):
        shape = (2 * BLOCK, BLOCK)
        kidx = lax.broadcasted_iota(jnp.int32, shape, 0)
        qidx = lax.broadcasted_iota(jnp.int32, shape, 1)
        rel = self.index(qidx, False) - self.index(kidx, True)
        valid = (rel >= 0) & (rel <= WINDOW_KEYS)
        if has_before is not True:
            valid = valid & ((kidx >= BLOCK) | has_before)
        one = jnp.where(valid, 0.0, NEG)
        return jnp.concatenate([one, one], axis=1)

    def live_keys(self, half):
        return (0, 2 * BLOCK - HALF_BLOCK) if half == 0 else (HALF_BLOCK, 2 * BLOCK)

    def half_bias(self, has_before, half):
        r0, r1 = self.live_keys(half)
        shape = (r1 - r0, LANES)
        kidx = lax.broadcasted_iota(jnp.int32, shape, 0) + r0
        qidx = jnp.bitwise_and(lax.broadcasted_iota(jnp.int32, shape, 1), HALF_BLOCK - 1) + half * HALF_BLOCK
        rel = self.index(qidx, False) - self.index(kidx, True)
        valid = (rel >= 0) & (rel <= WINDOW_KEYS)
        if has_before is not True:
            valid = valid & ((kidx >= BLOCK) | has_before)
        return jnp.where(valid, 0.0, NEG)


def _head_masks():
    lane = lax.broadcasted_iota(jnp.int32, (BLOCK, LANES), 1)
    lo = lane < HEAD_DIM
    return lane, lo, jnp.where(lo, 1.0, 0.0).astype(BF16), jnp.where(lo, 0.0, 1.0).astype(BF16)


def attn_fwd(name, q, k, v, run):
    nat = name == "nat"
    seq = q.shape[0] if nat else q.shape[2] * PERM
    attn_w = q.shape[-1]
    mode = _Mode(name, seq)
    groups = _lane_groups(attn_w)
    first = run is None
    all_lanes = slice(0, LANES)

    def body(*refs):
        q_ref, kp_ref, kc_ref, vp_ref, vc_ref = refs[:5]
        if first:
            o_ref, l_ref = refs[5:]
        elif nat:
            oin_ref, lin_ref, o_ref, l_ref, ostage, lstage = refs[5:]
        else:
            oin_ref, lin_ref, o_ref, l_ref = refs[5:]
        n = pl.program_id(1)
        subs = range(mode.qb)
        halves = (0, 1)
        live = [mode.live_keys(x) for x in halves]
        always = [mode.half_bias(True, x) for x in halves]
        biases = [[mode.half_bias(n > 0, x) for x in halves]] + [always] * (mode.qb - 1)
        _, lo, m_lo, m_hi = _head_masks()
        head_row = lax.broadcasted_iota(jnp.int32, (BLOCK, LANES), 0)
        ones = jnp.ones((2 * BLOCK, LANES), BF16)
        hb = HALF_BLOCK
        lrows = [jnp.zeros((BLOCK, LANES), F32) for _ in subs]
        if not first:
            if nat:
                for g, sl in enumerate(groups):
                    _from_perm(oin_ref, sl, ostage, g)
                _from_perm(lin_ref, all_lanes, lstage, 0)
            wide_rows = lambda a, sub: a[sub * BLOCK:(sub + 1) * BLOCK]
            before = [jnp.transpose(wide_rows(_stage_get(lstage, 0), sub) if nat else mode.get(lin_ref, all_lanes, sub))
                      for sub in subs]

        def probs(sub, p, sl):
            q2 = mode.get(q_ref, sl, sub)
            kcat = mode.keys(kp_ref, kc_ref, sl, sub)
            vcat = mode.keys(vp_ref, vc_ref, sl, sub)
            q_lo, q_hi = q2 * m_lo, q2 * m_hi
            qq = jnp.concatenate([q_lo[:hb], q_hi[:hb], q_lo[hb:], q_hi[hb:]], axis=0)
            s_t = _nt(kcat, qq)
            columns, lses = [], []
            for x in halves:
                r0, r1 = live[x]
                s_x = s_t[r0:r1, x * LANES:(x + 1) * LANES] + biases[sub][x]
                m = jnp.max(s_x, axis=0, keepdims=True)
                pe = jnp.exp(s_x - m)
                lse = m + jnp.log(jnp.sum(pe, axis=0, keepdims=True))
                if not first:
                    was = jnp.concatenate([before[sub][2 * p:2 * p + 1, x * hb:(x + 1) * hb],
                                           before[sub][2 * p + 1:2 * p + 2, x * hb:(x + 1) * hb]], axis=1)
                    top = jnp.maximum(was, lse)
                    lse = top + jnp.log(jnp.exp(was - top) + jnp.exp(lse - top))
                    pe = pe * jnp.exp(m - lse)
                pieces = [pe.astype(BF16)]
                if r0 > 0:
                    pieces.insert(0, jnp.zeros((r0, LANES), BF16))
                if r1 < 2 * BLOCK:
                    pieces.append(jnp.zeros((2 * BLOCK - r1, LANES), BF16))
                columns.append(pieces[0] if len(pieces) == 1 else jnp.concatenate(pieces, axis=0))
                lses.append(lse)
            return jnp.concatenate([vcat, ones], axis=1), jnp.concatenate(columns, axis=1), lses

        def output(sub, p, sl, vext, pb, lses):
            o_ext = _tn(pb, vext)
            if first:
                o_new = o_ext[:, :LANES] / o_ext[:, LANES:]
            else:
                o_prev = wide_rows(_stage_get(ostage, p), sub) if nat else mode.get(oin_ref, sl, sub)
                same = jnp.concatenate([o_prev[:hb], o_prev[:hb], o_prev[hb:], o_prev[hb:]], axis=0)
                o_new = o_ext[:, :LANES] + same * (1.0 - o_ext[:, LANES:])
            head_lo = jnp.concatenate([o_new[:hb], o_new[2 * hb:3 * hb]], axis=0)
            head_hi = jnp.concatenate([o_new[hb:2 * hb], o_new[3 * hb:]], axis=0)
            mode.put(o_ref, sl, jnp.where(lo, head_lo, head_hi), sub)
            lse_lo = jnp.concatenate([lses[0][:, :hb], lses[1][:, :hb]], axis=1)
            lse_hi = jnp.concatenate([lses[0][:, hb:], lses[1][:, hb:]], axis=1)
            rows = jnp.where(head_row == 2 * p, lse_lo, lrows[sub])
            lrows[sub] = jnp.where(head_row == 2 * p + 1, lse_hi, rows)

        pending = None
        for sub in subs:
            for p, sl in enumerate(groups):
                nxt = probs(sub, p, sl)
                if pending is not None:
                    output(*pending)
                pending = (sub, p, sl, *nxt)
        output(*pending)
        for sub in subs:
            mode.put(l_ref, all_lanes, jnp.transpose(lrows[sub]), sub)

    ins = [q, k, k, v, v]
    specs = [mode.wide(attn_w), mode.block_before(attn_w), mode.wide(attn_w), mode.block_before(attn_w), mode.wide(attn_w)]
    scratch = []
    if not first:
        ins += list(run)
        if nat:
            rows_a = mode.qb * BLOCK // PERM
            specs += [pl.BlockSpec((PJ, PJ, rows_a, attn_w), lambda r, n: (0, 0, n, 0)),
                      pl.BlockSpec((PJ, PJ, rows_a, LANES), lambda r, n: (0, 0, n, 0))]
            scratch = [pltpu.VMEM(_stage_shape(len(groups), mode.qb * BLOCK), F32),
                       pltpu.VMEM(_stage_shape(1, mode.qb * BLOCK), F32)]
        else:
            specs += [mode.wide(attn_w), mode.wide(LANES)]
    if nat:
        out_shape = (jax.ShapeDtypeStruct((seq, attn_w), F32), jax.ShapeDtypeStruct((seq, LANES), F32))
    else:
        out_shape = (jax.ShapeDtypeStruct(_perm_shape(seq, attn_w), F32), jax.ShapeDtypeStruct(_perm_shape(seq, LANES), F32))
    return pl.pallas_call(
        body, name=f"attn_fwd_{name}", grid=(mode.residues, mode.steps),
        out_shape=out_shape, in_specs=specs, out_specs=(mode.wide(attn_w), mode.wide(LANES)),
        scratch_shapes=scratch,
        compiler_params=_params(("arbitrary", "arbitrary")),
    )(*ins)


def attn_bwd(name, q, k, v, d_o, lse, delta, run):
    nat = name == "nat"
    seq = q.shape[0] if nat else q.shape[2] * PERM
    attn_w = q.shape[-1]
    mode = _Mode(name, seq)
    steps, qb = mode.steps, mode.qb
    single = steps == 1
    groups = _lane_groups(attn_w)
    first = run is None
    all_lanes = slice(0, LANES)

    def body(*refs):
        q_ref, kp_ref, kc_ref, vp_ref, vc_ref, do_ref, lse_ref, dl_ref = refs[:8]
        if first:
            dq_ref, dk_ref, dv_ref, ck, cv = refs[8:]
        else:
            dqi_ref, dki_ref, dvi_ref, dq_ref, dk_ref, dv_ref, ck, cv = refs[8:]
        g = pl.program_id(1) if single else pl.program_id(0)
        n = g if single else lax.rem(g, steps)
        carries = ((ck, dk_ref, None if first else dki_ref), (cv, dv_ref, None if first else dvi_ref))

        def emit(out_ref, acc_ref, sl, sub, val):
            if acc_ref is not None:
                val = val + mode.get(acc_ref, sl, sub).astype(F32)
            mode.put(out_ref, sl, val, sub)

        if not single:
            @pl.when(g == 0)
            def _():
                ck[...] = jnp.zeros_like(ck)
                cv[...] = jnp.zeros_like(cv)

        @pl.when(g < total)
        def _():
            biases = [mode.bias(n > 0)] + [mode.bias(True)] * (qb - 1)
            _, lo, m_lo, m_hi = _head_masks()

            def scores(sub, p, sl, lse_t, dl_t):
                q2, do2 = mode.get(q_ref, sl, sub), mode.get(do_ref, sl, sub)
                kcat = mode.keys(kp_ref, kc_ref, sl, sub)
                vcat = mode.keys(vp_ref, vc_ref, sl, sub)
                qq = jnp.concatenate([q2 * m_lo, q2 * m_hi], axis=0)
                dd = jnp.concatenate([do2 * m_lo, do2 * m_hi], axis=0)
                h0 = 2 * p
                lse2 = jnp.concatenate([lse_t[h0:h0 + 1, :], lse_t[h0 + 1:h0 + 2, :]], axis=1)
                dl2 = jnp.concatenate([dl_t[h0:h0 + 1, :], dl_t[h0 + 1:h0 + 2, :]], axis=1)
                p_t = jnp.exp(_nt(kcat, qq) + (biases[sub] - lse2))
                ds_t = p_t * (_nt(vcat, dd) - dl2)
                return qq, dd, kcat, p_t.astype(BF16), ds_t.astype(BF16)

            def grads(sub, sl, qq, dd, kcat, pb, dsb):
                dqb = _tn(dsb, kcat)
                dq2 = jnp.where(lo, dqb[:BLOCK], dqb[BLOCK:]) * ATTN_SCALE
                if not first:
                    dq2 = dq2 + mode.get(dqi_ref, sl, sub).astype(F32)
                mode.put(dq_ref, sl, dq2, sub)
                for (carry, out_ref, acc_ref), lhs, rhs in zip(carries, (dsb, pb), (qq, dd)):
                    both = jnp.dot(lhs, rhs, preferred_element_type=F32)
                    if sub == 0:
                        if not single:
                            for s in range(qb - 1):
                                emit(out_ref, acc_ref, sl, s, carry[s, :, sl])
                            emit(out_ref, acc_ref, sl, qb - 1, carry[qb - 1, :, sl] + both[:BLOCK])
                        carry[0, :, sl] = both[BLOCK:]
                    else:
                        carry[sub - 1, :, sl] += both[:BLOCK]
                        carry[sub, :, sl] = both[BLOCK:]
                    if single and sub == qb - 1:
                        for s in range(qb):
                            emit(out_ref, acc_ref, sl, s, carry[s, :, sl])

            stats = [(jnp.transpose(mode.get(lse_ref, all_lanes, sub)),
                      jnp.transpose(mode.get(dl_ref, all_lanes, sub))) for sub in range(qb)]
            pending = None
            for p, sl in enumerate(groups):
                for sub in range(qb):
                    nxt = scores(sub, p, sl, *stats[sub])
                    if pending is not None:
                        grads(*pending)
                    pending = (sub, sl, *nxt)
            grads(*pending)

        if not single:
            @pl.when(g == total)
            def _():
                for carry, out_ref, acc_ref in carries:
                    for sl in groups:
                        for s in range(qb):
                            emit(out_ref, acc_ref, sl, s, carry[s, :, sl])

    total = mode.residues * steps
    if single:
        here = before = lambda r, n: (r, n)
    else:
        locate = lambda g: (g // steps, lax.rem(g, steps))
        here = lambda g: locate(jnp.minimum(g, total - 1))
        before = lambda g: locate(jnp.maximum(g - 1, 0))
    wide = lambda w: mode.wide(w, here)
    ins = [q, k, k, v, v, d_o, lse, delta]
    specs = [wide(attn_w), mode.block_before(attn_w, here), wide(attn_w), mode.block_before(attn_w, here), wide(attn_w),
             wide(attn_w), wide(LANES), wide(LANES)]
    if not first:
        ins += list(run)
        specs += [wide(attn_w), mode.wide(attn_w, before), mode.wide(attn_w, before)]
    shp = jax.ShapeDtypeStruct((seq, attn_w) if nat else _perm_shape(seq, attn_w), BF16)
    grid = (mode.residues, 1) if single else (total + 1,)
    return pl.pallas_call(
        body, name=f"attn_bwd_{name}", grid=grid,
        out_shape=(shp, shp, shp), in_specs=specs,
        out_specs=(wide(attn_w), mode.wide(attn_w, before), mode.wide(attn_w, before)),
        scratch_shapes=[pltpu.VMEM((qb, BLOCK, attn_w), F32), pltpu.VMEM((qb, BLOCK, attn_w), F32)],
        compiler_params=_params(("arbitrary",) * len(grid)),
    )(*ins)


def _shift_down(u, halo, k):
    rolled = pltpu.roll(u, k, 0)
    row = lax.broadcasted_iota(jnp.int32, halo.shape, 0)
    top = jnp.where(row < k, pltpu.roll(halo, k, 0), rolled[:SUBLANES])
    return jnp.concatenate([top, rolled[SUBLANES:]], axis=0)


def _shift_up(u, halo, k):
    rows = u.shape[0]
    rolled = pltpu.roll(u, rows - k, 0)
    row = lax.broadcasted_iota(jnp.int32, halo.shape, 0)
    bot = jnp.where(row >= SUBLANES - k, pltpu.roll(halo, SUBLANES - k, 0), rolled[rows - SUBLANES:])
    return jnp.concatenate([rolled[:rows - SUBLANES], bot], axis=0)


def tail(o, lse, ga, cz, x, tgt, w_out, g2, cw):
    seq, d_model = x.shape
    attn_w = o.shape[1]
    conv_w = cz.shape[1] // 4
    mix = attn_w + conv_w
    groups = _lane_groups(attn_w)
    tm = ROW_TILE
    nt = seq // tm
    hb = tm // SUBLANES

    def body(o_ref, l_ref, ga_ref, cz_ref, hz_ref, x_ref, t_ref, w_ref, g_ref, cw_ref,
             do_ref, dl_ref, dop_ref, dlp_ref, lp_ref, dga_ref, dcb_ref, dgc_ref, dcv_ref, e_ref,
             dw_ref, dg_ref, dcw_ref, loss_ref, stage):
        i = pl.program_id(0)

        @pl.when(i == 0)
        def _():
            dw_ref[...] = jnp.zeros_like(dw_ref)
            dg_ref[...] = jnp.zeros_like(dg_ref)
            dcw_ref[...] = jnp.zeros_like(dcw_ref)
            loss_ref[...] = jnp.zeros_like(loss_ref)

        u = cz_ref[:, 2 * conv_w:3 * conv_w] * cz_ref[:, 0:conv_w]
        uh = hz_ref[:, 2 * conv_w:3 * conv_w] * hz_ref[:, 0:conv_w]
        uh = jnp.where(i > 0, uh, 0.0)
        u1 = _shift_down(u, uh, 1)
        u2 = _shift_down(u, uh, 2)
        w0, w1, w2 = cw_ref[0:1, :], cw_ref[1:2, :], cw_ref[2:3, :]
        cvv = u2 * w0 + u1 * w1 + u * w2
        gv = g_ref[...]
        all_lanes = slice(0, LANES)

        def forward(rs):
            ov, gav = o_ref[rs, :], ga_ref[rs, :]
            sig_a = _sigmoid(gav)
            silu_a = gav * sig_a
            cb, gc = cz_ref[rs, conv_w:2 * conv_w], cz_ref[rs, 3 * conv_w:4 * conv_w]
            sig_c = _sigmoid(gc)
            silu_c = gc * sig_c
            bc = cb * cvv[rs]
            mixed = jnp.concatenate([ov * silu_a, bc * silu_c], axis=1).astype(BF16)
            yv = jnp.dot(mixed, w_ref[...], preferred_element_type=F32)
            return ov, gav, sig_a, silu_a, cb, gc, sig_c, silu_c, bc, mixed, yv

        def loss_and_dy(rs, mixed, yv):
            r2 = lax.rsqrt(jnp.mean(yv * yv, axis=-1, keepdims=True) + NORM_EPS)
            yhat = yv * r2
            diff = (x_ref[rs, :] + yhat * gv) - t_ref[rs, :]
            loss_ref[...] += _rowgroup_sum(diff * diff)
            ev = diff * (1.0 / d_model)
            e_ref[rs, :] = ev
            dg_ref[...] += _rowgroup_sum(ev * yhat)
            eg = ev * gv
            dy = (r2 * (eg - yhat * jnp.mean(eg * yhat, axis=-1, keepdims=True))).astype(BF16)
            dw_ref[...] += _tn(mixed, dy)
            return _nt(dy, w_ref[...])

        def backward(rs, ov, gav, sig_a, silu_a, cb, gc, sig_c, silu_c, bc, dm):
            rows = rs.stop - rs.start
            dma, dmc = dm[:, :attn_w], dm[:, attn_w:]
            dov = dma * silu_a
            do_ref[rs, :] = dov.astype(BF16)
            dga_ref[rs, :] = (dma * ov * (sig_a * (1.0 + gav * (1.0 - sig_a)))).astype(BF16)
            prod = dov * ov
            lane = lax.broadcasted_iota(jnp.int32, (rows, LANES), 1)
            lo = lane < HEAD_DIM
            dblk = jnp.zeros((rows, LANES), F32)
            for p, sl in enumerate(groups):
                pr = prod[:, sl]
                dblk = jnp.where(lane == 2 * p, jnp.sum(jnp.where(lo, pr, 0.0), axis=1, keepdims=True), dblk)
                dblk = jnp.where(lane == 2 * p + 1, jnp.sum(jnp.where(lo, 0.0, pr), axis=1, keepdims=True), dblk)
                _stage_put(stage, p, dov[:, sl], rs.start)
            dl_ref[rs, :] = dblk
            _stage_put(stage, len(groups), dblk, rs.start)
            _stage_put(stage, len(groups) + 1, l_ref[rs, :], rs.start)
            dsc = dmc * silu_c
            cv_rows = cvv[rs]
            dcb_ref[rs, :] = (dsc * cv_rows).astype(BF16)
            dgc_ref[rs, :] = (dmc * bc * (sig_c * (1.0 + gc * (1.0 - sig_c)))).astype(BF16)
            dcv = dsc * cb
            dcv_ref[rs, :] = dcv
            dcw_ref[0:SUBLANES, :] += _rowgroup_sum(dcv * u2[rs])
            dcw_ref[SUBLANES:2 * SUBLANES, :] += _rowgroup_sum(dcv * u1[rs])
            dcw_ref[2 * SUBLANES:3 * SUBLANES, :] += _rowgroup_sum(dcv * u[rs])

        halves = [slice(0, tm // 2), slice(tm // 2, tm)]
        fwd = [forward(rs) for rs in halves]
        dms = [loss_and_dy(rs, f[9], f[10]) for rs, f in zip(halves, fwd)]
        for rs, f, dm in zip(halves, fwd, dms):
            backward(rs, *f[:9], dm)
        for p, sl in enumerate(groups):
            _to_perm(stage, p, dop_ref, sl, BF16)
        _to_perm(stage, len(groups), dlp_ref, all_lanes, F32)
        _to_perm(stage, len(groups) + 1, lp_ref, all_lanes, F32)

    row = lambda n: pl.BlockSpec((tm, n), lambda i: (i, 0))
    whole = lambda a, b: pl.BlockSpec((a, b), lambda i: (0, 0))
    return pl.pallas_call(
        body, name="tail", grid=(nt,),
        out_shape=(jax.ShapeDtypeStruct((seq, attn_w), BF16), jax.ShapeDtypeStruct((seq, LANES), F32),
                   jax.ShapeDtypeStruct(_perm_shape(seq, attn_w), BF16), jax.ShapeDtypeStruct(_perm_shape(seq, LANES), F32),
                   jax.ShapeDtypeStruct(_perm_shape(seq, LANES), F32),
                   jax.ShapeDtypeStruct((seq, attn_w), BF16), jax.ShapeDtypeStruct((seq, conv_w), BF16),
                   jax.ShapeDtypeStruct((seq, conv_w), BF16), jax.ShapeDtypeStruct((seq, conv_w), F32),
                   jax.ShapeDtypeStruct((seq, d_model), F32), jax.ShapeDtypeStruct((mix, d_model), F32),
                   jax.ShapeDtypeStruct((SUBLANES, d_model), F32), jax.ShapeDtypeStruct((CONV_K * SUBLANES, conv_w), F32),
                   jax.ShapeDtypeStruct((SUBLANES, d_model), F32)),
        in_specs=[row(attn_w), row(LANES), row(attn_w), row(4 * conv_w),
                  pl.BlockSpec((SUBLANES, 4 * conv_w), lambda i: (jnp.maximum(i * hb - 1, 0), 0)),
                  row(d_model), row(d_model), _const_spec((mix, d_model)), _const_spec((1, d_model)),
                  _const_spec((SUBLANES, conv_w))],
        out_specs=(row(attn_w), row(LANES), _perm_tile_spec(attn_w, tm), _perm_tile_spec(LANES, tm), _perm_tile_spec(LANES, tm),
                   row(attn_w), row(conv_w), row(conv_w), row(conv_w), row(d_model),
                   whole(mix, d_model), whole(SUBLANES, d_model), whole(CONV_K * SUBLANES, conv_w),
                   whole(SUBLANES, d_model)),
        scratch_shapes=[pltpu.VMEM(_stage_shape(len(groups) + 2, tm), F32)],
        compiler_params=_params(("arbitrary",)),
    )(o, lse, ga, cz, cz, x, tgt, w_out, g2, cw)


def dz_dx(nat_grads, perm_grads, dga, dcb, dgc, dcv, cz, tables, x, g1, e, w_full, cw):
    seq, d_model = x.shape
    attn_w = dga.shape[1]
    conv_w = dcv.shape[1]
    width = w_full.shape[2]
    in_w = 4 * attn_w + 4 * conv_w
    groups = _lane_groups(attn_w)
    tm = ROW_TILE
    nt = seq // tm
    hb = tm // SUBLANES

    def body(dq_ref, dk_ref, dv_ref, dqp_ref, dkp_ref, dvp_ref, dga_ref, dcb_ref, dgc_ref, dcv_ref, nh_ref, ch_ref, cc_ref,
             cos_ref, s1_ref, s2_ref, x_ref, g_ref, e_ref, w_ref, cw_ref, gx_ref, dz_ref, dg_ref, stage):
        i = pl.program_id(0)

        @pl.when(i == 0)
        def _():
            dg_ref[...] = jnp.zeros_like(dg_ref)

        cos, s1, s2 = cos_ref[...], s1_ref[...], s2_ref[...]

        def qkv_columns(t, nat_ref, perm_ref):
            for g, sl in enumerate(groups):
                _from_perm(perm_ref, sl, stage, g)
            for g, sl in enumerate(groups):
                tot = nat_ref[:, sl].astype(F32) + _stage_get(stage, g)
                if t < 2:
                    tot = _rope_transposed(tot, cos, s1, s2)
                dz_ref[:, t * attn_w + g * LANES:t * attn_w + (g + 1) * LANES] = tot.astype(BF16)

        def dh_part(j):
            return _nt(dz_ref[:, j * width:(j + 1) * width], w_ref[j])

        dcv = dcv_ref[...]
        nh = jnp.where(i < nt - 1, nh_ref[...], 0.0)
        w0, w1, w2 = cw_ref[0:1, :], cw_ref[1:2, :], cw_ref[2:3, :]
        du = dcv * w2 + _shift_up(dcv, nh, 1) * w1 + _shift_up(dcv, nh, 2) * w0
        base = 4 * attn_w
        dz_ref[:, base:base + conv_w] = (du * cc_ref[...]).astype(BF16)
        dz_ref[:, base + conv_w:base + 2 * conv_w] = dcb_ref[...]
        dz_ref[:, base + 2 * conv_w:base + 3 * conv_w] = (du * ch_ref[...]).astype(BF16)
        dz_ref[:, base + 3 * conv_w:base + 4 * conv_w] = dgc_ref[...]
        dz_ref[:, 3 * attn_w:4 * attn_w] = dga_ref[...]
        ready = in_w
        dh = None
        for t, nat_ref, perm_ref in ((2, dv_ref, dvp_ref), (1, dk_ref, dkp_ref), (0, dq_ref, dqp_ref), (None, None, None)):
            lowest_open = 0 if t is None else (t + 1) * attn_w
            while ready - width >= lowest_open:
                ready -= width
                part = dh_part(ready // width)
                dh = part if dh is None else dh + part
            if t is not None:
                qkv_columns(t, nat_ref, perm_ref)
        xv = x_ref[...]
        r1 = lax.rsqrt(jnp.mean(xv * xv, axis=-1, keepdims=True) + NORM_EPS)
        xhat = xv * r1
        dg_ref[...] += _rowgroup_sum(dh * xhat)
        dhg = dh * g_ref[...]
        gx_ref[...] = r1 * (dhg - xhat * jnp.mean(dhg * xhat, axis=-1, keepdims=True)) + e_ref[...]

    row = lambda n: pl.BlockSpec((tm, n), lambda i: (i, 0))
    whole = lambda a, b: pl.BlockSpec((a, b), lambda i: (0, 0))
    pt = _perm_tile_spec(attn_w, tm)
    return pl.pallas_call(
        body, name="dz_dx", grid=(nt,),
        out_shape=(jax.ShapeDtypeStruct((seq, d_model), F32), jax.ShapeDtypeStruct((seq, in_w), BF16),
                   jax.ShapeDtypeStruct((SUBLANES, d_model), F32)),
        in_specs=[row(attn_w), row(attn_w), row(attn_w), pt, pt, pt, row(attn_w), row(conv_w), row(conv_w), row(conv_w),
                  pl.BlockSpec((SUBLANES, conv_w), lambda i: (jnp.minimum((i + 1) * hb, seq // SUBLANES - 1), 0)),
                  pl.BlockSpec((tm, conv_w), lambda i: (i, 0)), pl.BlockSpec((tm, conv_w), lambda i: (i, 2)),
                  row(LANES), row(LANES), row(LANES), row(d_model), _const_spec((1, d_model)), row(d_model),
                  _const_spec(w_full.shape), _const_spec((SUBLANES, conv_w))],
        out_specs=(row(d_model), row(in_w), whole(SUBLANES, d_model)),
        scratch_shapes=[pltpu.VMEM(_stage_shape(len(groups), tm), F32)],
        compiler_params=_params(("arbitrary",)),
    )(*nat_grads, *perm_grads, dga, dcb, dgc, dcv, dcv, cz, cz, *tables, x, g1, e, w_full, cw)


def dw_in_reduce(ht, dz, g_out, small):
    d_model, seq = ht.shape
    half = dz.shape[1] // N_DEV
    ts = min(DW_ROWS, seq)
    steps = seq // ts
    x, y, c = lax.axis_index("x"), lax.axis_index("y"), lax.axis_index("c")
    far_first = lambda x, y: [(1 - x, 1 - y), (1 - x, y), (x, 1 - y)]
    chips = jnp.stack([2 * px + py for px, py in far_first(x, y)] + [2 * x + y]).astype(jnp.int32)
    order = jnp.stack([2 * chips + (1 - c), 2 * chips + c], axis=1).reshape(N_DEV)

    def body(order_ref, ht_ref, dz_ref, go_ref, sm_ref, out_ref, ro_ref, rs_ref,
             acc, theirs, staged, contrib, resbuf, out_sem, sa, ra, sb, rb, sc, rc,
             o_mine, o_theirs, o_staged, o_contrib, o_res, sbuf, o_load, osa, ora, osb, orb, osc, orc, ss, rs):
        del order_ref
        p, s = pl.program_id(0), pl.program_id(1)
        x, y, c = lax.axis_index("x"), lax.axis_index("y"), lax.axis_index("c")
        me = 2 * x + y
        sib = (x, y, 1 - c)
        peers = far_first(x, y)
        slot = p % 2

        flips = [(fx, fy, fc) for fx in (0, 1) for fy in (0, 1) for fc in (0, 1)][1:]
        my8 = 4 * x + 2 * y + c
        chip_ids = [2 * px + py for px, py in peers] + [me]

        def small_copy(k, slot8, to):
            return pltpu.make_async_remote_copy(src_ref=sm_ref, dst_ref=sbuf.at[slot8], send_sem=ss.at[k], recv_sem=rs.at[k],
                                                device_id=to, device_id_type=MESH)

        def small_peer(k):
            fx, fy, fc = flips[k]
            return _flip(x, fx), _flip(y, fy), _flip(c, fc)

        def oa_copy(pos):
            j = chip_ids[pos]
            return pltpu.make_async_remote_copy(src_ref=go_ref.at[j, 1 - c], dst_ref=o_theirs.at[j], send_sem=osa.at[pos],
                                                recv_sem=ora.at[pos], device_id=sib, device_id_type=MESH)

        def o_load_copy(pos):
            j = chip_ids[pos]
            return pltpu.make_async_copy(go_ref.at[j, c], o_mine.at[j], o_load.at[pos])

        def ob_copy(k, piece, slot4):
            px, py = peers[k]
            return pltpu.make_async_remote_copy(src_ref=o_staged.at[piece], dst_ref=o_contrib.at[slot4], send_sem=osb.at[k],
                                                recv_sem=orb.at[k], device_id=(px, py, c), device_id_type=MESH)

        def oc_copy(which):
            return pltpu.make_async_remote_copy(src_ref=o_res.at[which], dst_ref=o_res.at[which], send_sem=osc, recv_sem=orc,
                                                device_id=sib, device_id_type=MESH)

        @pl.when((p == 0) & (s == 0))
        def _():
            sbuf[my8] = sm_ref[...]
            for k in range(N_DEV - 1):
                small_copy(k, my8, small_peer(k)).start()
            for pos in range(N_CHIPS):
                o_load_copy(pos).start()
                oa_copy(pos).start()

        @pl.when((p == 1) & (s == steps - 1))
        def _():
            for pos in range(N_CHIPS):
                j = chip_ids[pos]
                o_load_copy(pos).wait()
                oa_copy(pos).wait_recv()
                if pos < N_CHIPS - 1:
                    o_staged[j] = (o_mine[j] + o_theirs[j]).astype(BF16)
                    ob_copy(pos, j, me).start()
                else:
                    o_mine[j] = o_mine[j] + o_theirs[j]
                    o_contrib[j] = o_mine[j].astype(BF16)

        @pl.when((p == 4) & (s == steps - 1))
        def _():
            for k in range(N_CHIPS - 1):
                ob_copy(k, me, chip_ids[k]).wait_recv()
            own = o_mine[me]
            term = lambda j: jnp.where(me == j, own, o_contrib[j].astype(F32))
            o_res[c] = ((term(0) + term(1)) + term(2)) + term(3)
            oc_copy(c).start()

        def a_copy(k):
            return pltpu.make_async_remote_copy(src_ref=acc.at[0], dst_ref=theirs.at[k], send_sem=sa.at[k], recv_sem=ra.at[k],
                                                device_id=sib, device_id_type=MESH)

        def b_copy(k):
            px, py = peers[k]
            return pltpu.make_async_remote_copy(src_ref=staged.at[k], dst_ref=contrib.at[k], send_sem=sb.at[k], recv_sem=rb.at[k],
                                                device_id=(px, py, c), device_id_type=MESH)

        def c_copy(which):
            return pltpu.make_async_remote_copy(src_ref=resbuf.at[which], dst_ref=resbuf.at[which], send_sem=sc, recv_sem=rc,
                                                device_id=sib, device_id_type=MESH)

        @pl.when(s == 0)
        def _():
            for k in range(N_CHIPS - 1):
                @pl.when(p == 2 * k + 2)
                def _():
                    a_copy(k).wait_send()
            acc[slot] = jnp.dot(ht_ref[...], dz_ref[...], preferred_element_type=F32)

        @pl.when(s > 0)
        def _():
            acc[slot] += jnp.dot(ht_ref[...], dz_ref[...], preferred_element_type=F32)

        @pl.when(s == steps - 1)
        def _():
            for k in range(N_CHIPS):
                @pl.when(p == 2 * k)
                def _():
                    a_copy(k).start()
            for k in range(N_CHIPS - 1):
                @pl.when(p == 2 * k + 1)
                def _():
                    a_copy(k).wait_recv()
                    staged[k] = (acc[1] + theirs[k]).astype(BF16)
                    b_copy(k).start()

            @pl.when(p == N_DEV - 1)
            def _():
                a_copy(N_CHIPS - 1).wait_recv()
                tot = acc[1] + theirs[N_CHIPS - 1]
                for k in range(N_CHIPS - 1):
                    b_copy(k).wait_recv()
                    tot = tot + contrib[k].astype(F32)
                resbuf[c] = tot
                c_copy(c).start()
                c_copy(1 - c).wait_recv()
                done = pltpu.make_async_copy(resbuf, out_ref, out_sem)
                done.start()
                oc_copy(1 - c).wait_recv()
                ro_ref[...] = o_res[...]
                for k in range(N_DEV - 1):
                    px, py, pc = small_peer(k)
                    small_copy(k, 4 * px + 2 * py + pc, (px, py, pc)).wait_recv()
                tot8 = sbuf[0]
                for d in range(1, N_DEV):
                    tot8 = tot8 + sbuf[d]
                rs_ref[...] = tot8
                a_copy(N_CHIPS - 1).wait_send()
                for k in range(N_CHIPS - 1):
                    b_copy(k).wait_send()
                    ob_copy(k, chip_ids[k], me).wait_send()
                c_copy(c).wait_send()
                oc_copy(c).wait_send()
                for pos in range(N_CHIPS):
                    oa_copy(pos).wait_send()
                for k in range(N_DEV - 1):
                    small_copy(k, my8, small_peer(k)).wait_send()
                done.wait()

    dma = pltpu.SemaphoreType.DMA
    o_shape = g_out.shape[1:]
    go = g_out.reshape(N_CHIPS, 2, *o_shape)
    const = lambda shape: pl.BlockSpec(shape, lambda p, s, order_ref: (0,) * len(shape))
    grid_spec = pltpu.PrefetchScalarGridSpec(
        num_scalar_prefetch=1, grid=(N_DEV, steps),
        in_specs=[pl.BlockSpec((d_model, ts), lambda p, s, order_ref: (0, s)),
                  pl.BlockSpec((ts, half), lambda p, s, order_ref: (s, order_ref[p])),
                  pl.BlockSpec(memory_space=pl.ANY), const(small.shape)],
        out_specs=(pl.BlockSpec(memory_space=pl.ANY), const((2, *o_shape)), const(small.shape)),
        scratch_shapes=[pltpu.VMEM((2, d_model, half), F32), pltpu.VMEM((N_CHIPS, d_model, half), F32),
                        pltpu.VMEM((N_CHIPS - 1, d_model, half), BF16), pltpu.VMEM((N_CHIPS - 1, d_model, half), BF16),
                        pltpu.VMEM((2, d_model, half), F32), dma,
                        dma((N_CHIPS,)), dma((N_CHIPS,)), dma((N_CHIPS - 1,)), dma((N_CHIPS - 1,)), dma, dma,
                        pltpu.VMEM((N_CHIPS, *o_shape), F32), pltpu.VMEM((N_CHIPS, *o_shape), F32),
                        pltpu.VMEM((N_CHIPS, *o_shape), BF16), pltpu.VMEM((N_CHIPS, *o_shape), BF16),
                        pltpu.VMEM((2, *o_shape), F32), pltpu.VMEM((N_DEV, *small.shape), F32),
                        dma((N_CHIPS,)), dma((N_CHIPS,)), dma((N_CHIPS,)), dma((N_CHIPS - 1,)), dma((N_CHIPS - 1,)), dma, dma,
                        dma((N_DEV - 1,)), dma((N_DEV - 1,))])
    return pl.pallas_call(
        body, name="dw_in_reduce", grid_spec=grid_spec,
        out_shape=(jax.ShapeDtypeStruct((2, d_model, half), F32), jax.ShapeDtypeStruct((2, *o_shape), F32),
                   jax.ShapeDtypeStruct(small.shape, F32)),
        compiler_params=_params(("arbitrary", "arbitrary")),
    )(order, ht, dz, go, small)


def _adam_math(w, g, m, v):
    m = ADAM_B1 * m + (1.0 - ADAM_B1) * g
    v = ADAM_B2 * v + (1.0 - ADAM_B2) * (g * g)
    m_hat = m / (1.0 - ADAM_B1 ** ADAM_STEP)
    v_hat = v / (1.0 - ADAM_B2 ** ADAM_STEP)
    delta = -ADAM_LR * (m_hat / (jnp.sqrt(v_hat) + ADAM_EPS) + ADAM_WD * w)
    return delta, m, v


def adam_shard(name, w, g2, m, v, block, grid, w_map, g_map):
    def body(w_ref, g_ref, m_ref, v_ref, go_ref, d_ref, mo_ref, vo_ref):
        g = g_ref[0]
        delta, mn, vn = _adam_math(w_ref[...], g, m_ref[...], v_ref[...])
        go_ref[...] = g
        d_ref[...] = delta
        mo_ref[...] = mn
        vo_ref[...] = vn

    ws = pl.BlockSpec(block, w_map)
    shp = jax.ShapeDtypeStruct(w.shape, F32)
    return pl.pallas_call(
        body, name=name, grid=grid, out_shape=(shp, shp, shp, shp),
        in_specs=[ws, pl.BlockSpec((1, *block), g_map), ws, ws], out_specs=(ws, ws, ws, ws),
        compiler_params=_params(("arbitrary",) * len(grid)),
    )(w, g2, m, v)


def adam_small(ws, gs, ms, vs):
    n = len(ws)

    def body(*refs):
        ins, outs = refs[:4 * n], refs[4 * n:]
        for t in range(n):
            delta, mn, vn = _adam_math(ins[t][...], ins[n + t][...], ins[2 * n + t][...], ins[3 * n + t][...])
            outs[3 * t][...] = delta
            outs[3 * t + 1][...] = mn
            outs[3 * t + 2][...] = vn

    vm = pl.BlockSpec(memory_space=pltpu.VMEM)
    outs = pl.pallas_call(
        body, name="adam_small",
        out_shape=tuple(jax.ShapeDtypeStruct(w.shape, F32) for w in ws for _ in range(3)),
        in_specs=[vm] * (4 * n), out_specs=tuple([vm] * (3 * n)),
        compiler_params=_params(),
    )(*ws, *gs, *ms, *vs)
    return [outs[3 * t:3 * t + 3] for t in range(n)]


def kernel(x, norm_pre_g, w_in, conv_w, w_out, norm_post_g, loss_target, m_norm_pre_g, m_w_in, m_conv_w, m_w_out, m_norm_post_g, v_norm_pre_g, v_w_in, v_conv_w, v_w_out, v_norm_post_g):
    _, seq, d_model = x.shape
    width = w_in.shape[1]
    conv_q = conv_w.shape[1]
    conv_width = N_CHIPS * conv_q
    attn_width = d_model - conv_width
    xs, tg = x[0], loss_target[0]
    g1, g2 = norm_pre_g.reshape(1, d_model), norm_post_g.reshape(1, d_model)

    w_full, wout_full, cw_full, *tables = gather_weights(w_in, w_out, conv_w, seq)
    wout2 = wout_full.reshape(attn_width + conv_width, d_model)
    cw = jnp.zeros((SUBLANES, conv_width), F32).at[:CONV_K].set(
        cw_full[:, :CONV_K, :conv_q].transpose(1, 0, 2).reshape(CONV_K, conv_width))

    ht, q, k, v, qp, kp, vp, ga, cz = inproj(xs, g1, w_full, tables, attn_width, conv_width)
    run = attn_fwd("p4", qp, kp, vp, None)
    run = attn_fwd("p16", qp, kp, vp, run)
    o, lse = attn_fwd("nat", q, k, v, run)
    (d_o, delta, d_op, delta_p, lse_p, dga, dcb, dgc, dcv, e, dwout, dg2, dcw, loss_acc) = tail(
        o, lse, ga, cz, xs, tg, wout2, g2, cw)
    nat_grads = attn_bwd("nat", q, k, v, d_o, lse, delta, None)
    perm_grads = attn_bwd("p4", qp, kp, vp, d_op, lse_p, delta_p, None)
    perm_grads = attn_bwd("p16", qp, kp, vp, d_op, lse_p, delta_p, perm_grads)
    grad_x, dz, dg1 = dz_dx(nat_grads, perm_grads, dga, dcb, dgc, dcv, cz, tables, xs, g1, e, w_full, cw)

    small = jnp.zeros((SUBLANES, d_model), F32)
    small = small.at[0].set(dg1.sum(axis=0)).at[1].set(dg2.sum(axis=0))
    small = small.at[2:2 + CONV_K, :conv_width].set(dcw.reshape(CONV_K, SUBLANES, conv_width).sum(axis=1))
    small = small.at[2 + CONV_K, 0].set(jnp.sum(loss_acc))
    rin, rout, rsmall = dw_in_reduce(ht, dz, dwout.reshape(N_DEV, -1, d_model), small)

    half = width // 2
    tr = min(ADAM_ROWS, d_model)
    gw_in, d_in, m_in, v_in = adam_shard(
        "adam_w_in", w_in, rin, m_w_in, v_w_in, (tr, half), (2, d_model // tr),
        lambda hf, i: (i, hf), lambda hf, i: (hf, i, 0))
    rq = w_out.shape[0] // 2
    gw_out, d_out, m_out, v_out = adam_shard(
        "adam_w_out", w_out, rout, m_w_out, v_w_out, (rq, d_model), (2,),
        lambda hf: (hf, 0), lambda hf: (hf, 0, 0))

    chip = 2 * lax.axis_index("x") + lax.axis_index("y")
    g_pre, g_post = rsmall[0:1], rsmall[1:2]
    g_conv = lax.dynamic_slice(rsmall[2:2 + CONV_K, :conv_width], (0, chip * conv_q), (CONV_K, conv_q))
    (d_pre, m_pre, v_pre), (d_post, m_post, v_post), (d_cv, m_cv, v_cv) = adam_small(
        [g1, g2, conv_w], [g_pre, g_post, g_conv],
        [m_norm_pre_g.reshape(1, d_model), m_norm_post_g.reshape(1, d_model), m_conv_w],
        [v_norm_pre_g.reshape(1, d_model), v_norm_post_g.reshape(1, d_model), v_conv_w])

    loss = 0.5 * rsmall[2 + CONV_K, 0] / d_model
    vec = lambda a: a.reshape(d_model)
    return (loss, grad_x.reshape(1, seq, d_model),
            vec(g_pre), gw_in, g_conv, gw_out, vec(g_post),
            vec(d_pre), d_in, d_cv, d_out, vec(d_post),
            vec(m_pre), m_in, m_cv, m_out, vec(m_post),
            vec(v_pre), v_in, v_cv, v_out, vec(v_post))
```

```python
import jax
import jax.numpy as jnp
from jax import lax
from jax.experimental import pallas as pl
from jax.experimental.pallas import tpu as pltpu

HEAD_DIM = 64
LANES = 128
SUBLANES = 8
BLOCK = 128
HALF_BLOCK = BLOCK // 2
WINDOW_KEYS = 128
PERM = 16
PJ = 4
P4_ROWS = BLOCK // PJ
MAX_QUERY_BLOCKS = 8
ROW_TILE = 512
DW_ROWS = 4096
ADAM_ROWS = 1024
CONV_K = 3
ROPE_THETA = 10000.0
NORM_EPS = 1e-6
ATTN_SCALE = HEAD_DIM ** -0.5
NEG = -1e30
N_CHIPS = 4
N_DEV = 8
MESH = pl.DeviceIdType.MESH
ADAM_LR = 0.001
ADAM_B1 = 0.9
ADAM_B2 = 0.999
ADAM_EPS = 1e-08
ADAM_WD = 0.01
ADAM_STEP = 10
VMEM_LIMIT = 63 * 1024 * 1024

F32 = jnp.float32
BF16 = jnp.bfloat16


def _params(sem=None, **kw):
    return pltpu.CompilerParams(dimension_semantics=sem, vmem_limit_bytes=VMEM_LIMIT, **kw)


def _const_spec(shape):
    return pl.BlockSpec(shape, lambda *_: (0,) * len(shape), pipeline_mode=pl.Buffered(1))


def _sigmoid(z):
    return 1.0 / (1.0 + jnp.exp(-z))


def _rowgroup_sum(a):
    rows, n = a.shape
    return a.reshape(rows // SUBLANES, SUBLANES, n).sum(axis=0)


def _nt(a, b):
    return lax.dot_general(a, b, (((1,), (1,)), ((), ())), preferred_element_type=F32)


def _tn(a, b):
    return lax.dot_general(a, b, (((0,), (0,)), ((), ())), preferred_element_type=F32)


def _col_pieces(a, b, width):
    out = []
    while a < b:
        j = a // width
        e = min(b, (j + 1) * width)
        out.append((j, a - j * width, e - j * width))
        a = e
    return out


def _lane_groups(width):
    return [slice(g * LANES, (g + 1) * LANES) for g in range(width // LANES)]


def _perm_shape(seq, width):
    return (PJ, PJ, seq // PERM, width)


def _perm_tile_spec(width, tm):
    return pl.BlockSpec((PJ, PJ, tm // PERM, width), lambda i: (0, 0, i, 0))


STAGE_PITCH = 24


def _stage_shape(groups, rows):
    return (groups, rows // PERM * STAGE_PITCH, LANES)


def _stage_put(stage, g, val, row0=0):
    for a in range(val.shape[0] // PERM):
        at = (row0 // PERM + a) * STAGE_PITCH
        stage[g, at:at + PERM, :] = val[a * PERM:(a + 1) * PERM]


def _stage_get(stage, g):
    return jnp.concatenate([stage[g, a * STAGE_PITCH:a * STAGE_PITCH + PERM, :]
                            for a in range(stage.shape[1] // STAGE_PITCH)], axis=0)


def _to_perm(stage, g, dst_ref, sl, dtype):
    rows = stage.shape[1] // STAGE_PITCH
    for b in range(PERM):
        dst_ref[b // PJ, b % PJ, :, sl] = stage[g, pl.ds(b, rows, stride=STAGE_PITCH), :].astype(dtype)


def _from_perm(src_ref, sl, stage, g):
    rows = stage.shape[1] // STAGE_PITCH
    for b in range(PERM):
        stage[g, pl.ds(b, rows, stride=STAGE_PITCH), :] = src_ref[b // PJ, b % PJ, :, sl].astype(F32)


def _flip(a, f):
    return 1 - a if f else a


def gather_weights(w_in, w_out, conv_w, seq):
    d_model, width = w_in.shape
    rows = w_out.shape[0]
    cw = jnp.zeros((SUBLANES, LANES), F32).at[:CONV_K, :conv_w.shape[1]].set(conv_w)
    half_dim = HEAD_DIM // 2
    inv_freq = ROPE_THETA ** (-jnp.arange(half_dim, dtype=F32) * 2.0 / HEAD_DIM)
    inv_freq = jnp.tile(inv_freq, LANES // half_dim).reshape(1, LANES)
    chunk = min(ROW_TILE, seq)

    def body(win_ref, wout_ref, cw_ref, freq_ref, winf_ref, woutf_ref, cwf_ref, cos_ref, sgn_ref,
             st_in, st_out, near_send, near_recv, far_send, far_recv, cw_send, cw_recv, d2d_send, d2d_recv):
        x, y, c = lax.axis_index("x"), lax.axis_index("y"), lax.axis_index("c")
        me = 2 * x + y
        sib = (x, y, 1 - c)
        st_in[...] = win_ref[...].astype(BF16)
        st_out[...] = wout_ref[...].astype(BF16)
        winf_ref[me] = st_in[...]
        woutf_ref[me] = st_out[...]
        cwf_ref[me] = cw_ref[...]
        stages = (st_in, st_out)
        fulls = (winf_ref, woutf_ref)
        halves = (d_model // 2, rows // 2)

        def part(t, core, q=None):
            size = halves[t] if q is None else halves[t] // 2
            start = core * halves[t] if q is None else core * halves[t] + q * size
            return pl.ds(pl.multiple_of(start, size), size)

        near = [(1 - x, y), (x, 1 - y)]
        far = (1 - x, 1 - y)
        chip = lambda px, py: 2 * px + py

        def direct(k, t, q, slot, to):
            src = stages[t].at[part(t, c, q)]
            return pltpu.make_async_remote_copy(src_ref=src, dst_ref=fulls[t].at[slot, part(t, c, q)], send_sem=near_send.at[k, t, q],
                                                recv_sem=near_recv.at[k, t, q], device_id=to, device_id_type=MESH)

        def passed_on(k, t, slot, to):
            ref = fulls[t].at[slot, part(t, c, k)]
            return pltpu.make_async_remote_copy(src_ref=ref, dst_ref=ref, send_sem=far_send.at[k, t], recv_sem=far_recv.at[k, t],
                                                device_id=to, device_id_type=MESH)

        def conv_copy(k, slot, to):
            return pltpu.make_async_remote_copy(src_ref=cw_ref, dst_ref=cwf_ref.at[slot], send_sem=cw_send.at[k], recv_sem=cw_recv.at[k],
                                                device_id=to, device_id_type=MESH)

        def d2d(k, t, slot, core):
            ref = fulls[t].at[slot, part(t, core)]
            return pltpu.make_async_remote_copy(src_ref=ref, dst_ref=ref, send_sem=d2d_send.at[k, t], recv_sem=d2d_recv.at[k, t],
                                                device_id=sib, device_id_type=MESH)

        sends = []

        def go(cp):
            cp.start()
            sends.append(cp)

        for q_first in (0, 1):
            for k, (px, py) in enumerate(near):
                for t in range(2):
                    go(direct(k, t, k if q_first == 0 else 1 - k, me, (px, py, c)))
        for k, (px, py) in enumerate(near + [far]):
            go(conv_copy(k, me, (px, py, c)))
        for k, (px, py) in enumerate(near):
            other = near[1 - k]
            for t in range(2):
                direct(k, t, k, chip(px, py), (px, py, c)).wait_recv()
                go(passed_on(k, t, chip(px, py), (*other, c)))

        first_half = lax.broadcasted_iota(jnp.int32, (chunk, LANES), 1) % HEAD_DIM < half_dim
        row = lax.broadcasted_iota(jnp.int32, (chunk, LANES), 0)

        def table_rows(i, carry):
            at = pl.multiple_of(i * chunk, chunk)
            ang = (row + at).astype(F32) * freq_ref[...]
            sin = jnp.sin(ang)
            cos_ref[pl.ds(at, chunk), :] = jnp.cos(ang)
            sgn_ref[pl.ds(at, chunk), :] = jnp.where(first_half, -sin, sin)
            return carry

        lax.fori_loop(0, seq // chunk, table_rows, 0)

        for k, (px, py) in enumerate(near):
            for t in range(2):
                direct(k, t, 1 - k, chip(px, py), (px, py, c)).wait_recv()
                go(d2d(k, t, chip(px, py), c))
        for t in range(2):
            for k, (px, py) in enumerate(near):
                passed_on(k, t, chip(*far), (px, py, c)).wait_recv()
            go(d2d(2, t, chip(*far), c))
        for k, (px, py) in enumerate(near + [far]):
            conv_copy(k, chip(px, py), (px, py, c)).wait_recv()
            for t in range(2):
                d2d(k, t, chip(px, py), 1 - c).wait_recv()
        for cp in sends:
            cp.wait_send()

    vm = pl.BlockSpec(memory_space=pltpu.VMEM)
    dma = pltpu.SemaphoreType.DMA
    return pl.pallas_call(
        body, name="gather_weights",
        out_shape=(jax.ShapeDtypeStruct((N_CHIPS, d_model, width), BF16),
                   jax.ShapeDtypeStruct((N_CHIPS, rows, d_model), BF16),
                   jax.ShapeDtypeStruct((N_CHIPS, SUBLANES, LANES), F32),
                   *[jax.ShapeDtypeStruct((seq, LANES), F32)] * 2),
        in_specs=[vm, vm, vm, vm], out_specs=(vm,) * 5,
        scratch_shapes=[pltpu.VMEM((d_model, width), BF16), pltpu.VMEM((rows, d_model), BF16),
                        dma((2, 2, 2)), dma((2, 2, 2)), dma((2, 2)), dma((2, 2)), dma((3,)), dma((3,)),
                        dma((3, 2)), dma((3, 2))],
        compiler_params=_params(),
    )(w_in, w_out, cw, inv_freq)


def _split_sin(sgn):
    first_half = lax.broadcasted_iota(jnp.int32, sgn.shape, 1) % HEAD_DIM < HEAD_DIM // 2
    return jnp.where(first_half, sgn, 0.0), jnp.where(first_half, 0.0, sgn)


def _rope(t, cos, s1, s2):
    return t * cos + pltpu.roll(t, LANES - HEAD_DIM // 2, 1) * s1 + pltpu.roll(t, HEAD_DIM // 2, 1) * s2


def _rope_transposed(g, cos, s1, s2):
    return g * cos + pltpu.roll(g * s1, HEAD_DIM // 2, 1) + pltpu.roll(g * s2, LANES - HEAD_DIM // 2, 1)


def inproj(x, g1, w_full, tables, attn_w, conv_w):
    seq, d_model = x.shape
    width = w_full.shape[2]
    tm = ROW_TILE
    groups = _lane_groups(attn_w)

    def body(x_ref, g_ref, w_ref, cos_ref, sgn_ref,
             ht_ref, q_ref, k_ref, v_ref, qp_ref, kp_ref, vp_ref, ga_ref, cz_ref, stage):
        xv = x_ref[...]
        hb = ((xv * lax.rsqrt(jnp.mean(xv * xv, axis=-1, keepdims=True) + NORM_EPS)) * g_ref[...]).astype(BF16)
        ht_ref[...] = jnp.transpose(hb)
        cos, (s1, s2) = cos_ref[...], _split_sin(sgn_ref[...])

        def proj(a, b):
            parts = [jnp.dot(hb, w_ref[j, :, lo:hi], preferred_element_type=F32) for j, lo, hi in _col_pieces(a, b, width)]
            return parts[0] if len(parts) == 1 else jnp.concatenate(parts, axis=1)

        def emit(z, nat_ref, perm_ref, fn):
            for g, sl in enumerate(groups):
                val = fn(z[:, sl])
                nat_ref[:, sl] = val.astype(BF16)
                _stage_put(stage, g, val)
            for g, sl in enumerate(groups):
                _to_perm(stage, g, perm_ref, sl, BF16)

        emit(proj(0, attn_w), q_ref, qp_ref, lambda t: _rope(t, cos, s1, s2) * ATTN_SCALE)
        emit(proj(attn_w, 2 * attn_w), k_ref, kp_ref, lambda t: _rope(t, cos, s1, s2))
        emit(proj(2 * attn_w, 3 * attn_w), v_ref, vp_ref, lambda t: t)
        ga_ref[...] = proj(3 * attn_w, 4 * attn_w)
        cz_ref[...] = proj(4 * attn_w, 4 * attn_w + 4 * conv_w)

    row = lambda n: pl.BlockSpec((tm, n), lambda i: (i, 0))
    nat = jax.ShapeDtypeStruct((seq, attn_w), BF16)
    perm = jax.ShapeDtypeStruct(_perm_shape(seq, attn_w), BF16)
    return pl.pallas_call(
        body, name="inproj", grid=(seq // tm,),
        out_shape=(jax.ShapeDtypeStruct((d_model, seq), BF16), nat, nat, nat, perm, perm, perm,
                   jax.ShapeDtypeStruct((seq, attn_w), F32), jax.ShapeDtypeStruct((seq, 4 * conv_w), F32)),
        in_specs=[row(d_model), _const_spec((1, d_model)), _const_spec(w_full.shape), row(LANES), row(LANES)],
        out_specs=(pl.BlockSpec((d_model, tm), lambda i: (0, i)), row(attn_w), row(attn_w), row(attn_w),
                   _perm_tile_spec(attn_w, tm), _perm_tile_spec(attn_w, tm), _perm_tile_spec(attn_w, tm),
                   row(attn_w), row(4 * conv_w)),
        scratch_shapes=[pltpu.VMEM(_stage_shape(len(groups), tm), F32)],
        compiler_params=_params(("arbitrary",)),
    )(x, g1, w_full, *tables)


class _Mode:
    def __init__(self, name, seq):
        self.name = name
        if name == "nat":
            self.residues, blocks = 1, seq // BLOCK
        elif name == "p16":
            self.residues, blocks = PERM, seq // PERM // BLOCK
        else:
            self.residues, blocks = PJ, seq // PERM // P4_ROWS
        self.qb = max(d for d in range(1, MAX_QUERY_BLOCKS + 1) if blocks % d == 0)
        self.steps = blocks // self.qb

    def _spec(self, blocks, width, at):
        if self.name == "nat":
            return pl.BlockSpec((blocks * BLOCK, width), lambda *g: (at(*g)[1], 0))
        if self.name == "p16":
            return pl.BlockSpec((1, 1, blocks * BLOCK, width), lambda *g: (at(*g)[0] // PJ, at(*g)[0] % PJ, at(*g)[1], 0))
        return pl.BlockSpec((PJ, 1, blocks * P4_ROWS, width), lambda *g: (0, at(*g)[0], at(*g)[1], 0))

    def wide(self, width, where=lambda r, n: (r, n)):
        return self._spec(self.qb, width, where)

    def block_before(self, width, where=lambda r, n: (r, n)):
        return self._spec(1, width, lambda *g: (where(*g)[0], jnp.maximum(self.qb * where(*g)[1] - 1, 0)))

    def get(self, ref, sl, sub=0):
        if self.name == "nat":
            return ref[sub * BLOCK:(sub + 1) * BLOCK, sl]
        if self.name == "p16":
            return ref[0, 0, sub * BLOCK:(sub + 1) * BLOCK, sl]
        return jnp.concatenate([ref[j, 0, at:at + P4_ROWS // 2, sl] for j, at in self._p4_chunks(sub)], axis=0)

    def put(self, ref, sl, val, sub=0):
        val = val.astype(ref.dtype)
        if self.name == "nat":
            ref[sub * BLOCK:(sub + 1) * BLOCK, sl] = val
        elif self.name == "p16":
            ref[0, 0, sub * BLOCK:(sub + 1) * BLOCK, sl] = val
        else:
            for i, (j, at) in enumerate(self._p4_chunks(sub)):
                ref[j, 0, at:at + P4_ROWS // 2, sl] = val[i * (P4_ROWS // 2):(i + 1) * (P4_ROWS // 2)]

    @staticmethod
    def _p4_chunks(sub):
        return [(j, sub * P4_ROWS + half * (P4_ROWS // 2)) for half in (0, 1) for j in range(PJ)]

    def keys(self, before_ref, wide_ref, sl, sub):
        older = self.get(before_ref, sl) if sub == 0 else self.get(wide_ref, sl, sub - 1)
        return jnp.concatenate([older, self.get(wide_ref, sl, sub)], axis=0)

    def index(self, idx, is_key):
        if self.name != "p4":
            return idx - BLOCK if is_key else idx
        within = jnp.bitwise_and(idx, BLOCK - 1)
        chunk = P4_ROWS // 2
        half = jnp.right_shift(within, HALF_BLOCK.bit_length() - 1)
        j = jnp.bitwise_and(jnp.right_shift(within, chunk.bit_length() - 1), PJ - 1)
        m = PJ * (chunk * half + jnp.bitwise_and(within, chunk - 1)) + j
        return m + BLOCK * (jnp.right_shift(idx, BLOCK.bit_length() - 1) - 1) if is_key else m

    def bias(self, has_before):
        shape = (2 * BLOCK, BLOCK)
        kidx = lax.broadcasted_iota(jnp.int32, shape, 0)
        qidx = lax.broadcasted_iota(jnp.int32, shape, 1)
        rel = self.index(qidx, False) - self.index(kidx, True)
        valid = (rel >= 0) & (rel <= WINDOW_KEYS)
        if has_before is not True:
            valid = valid & ((kidx >= BLOCK) | has_before)
        one = jnp.where(valid, 0.0, NEG)
        return jnp.concatenate([one, one], axis=1)

    def live_keys(self, half):
        return (0, 2 * BLOCK - HALF_BLOCK) if half == 0 else (HALF_BLOCK, 2 * BLOCK)

    def half_bias(self, has_before, half):
        r0, r1 = self.live_keys(half)
        shape = (r1 - r0, LANES)
        kidx = lax.broadcasted_iota(jnp.int32, shape, 0) + r0
        qidx = jnp.bitwise_and(lax.broadcasted_iota(jnp.int32, shape, 1), HALF_BLOCK - 1) + half * HALF_BLOCK
        rel = self.index(qidx, False) - self.index(kidx, True)
        valid = (rel >= 0) & (rel <= WINDOW_KEYS)
        if has_before is not True:
            valid = valid & ((kidx >= BLOCK) | has_before)
        return jnp.where(valid, 0.0, NEG)


def _head_masks():
    lane = lax.broadcasted_iota(jnp.int32, (BLOCK, LANES), 1)
    lo = lane < HEAD_DIM
    return lane, lo, jnp.where(lo, 1.0, 0.0).astype(BF16), jnp.where(lo, 0.0, 1.0).astype(BF16)


def attn_fwd(name, q, k, v, run):
    nat = name == "nat"
    seq = q.shape[0] if nat else q.shape[2] * PERM
    attn_w = q.shape[-1]
    mode = _Mode(name, seq)
    groups = _lane_groups(attn_w)
    first = run is None
    all_lanes = slice(0, LANES)

    def body(*refs):
        q_ref, kp_ref, kc_ref, vp_ref, vc_ref = refs[:5]
        if first:
            o_ref, l_ref = refs[5:]
        elif nat:
            oin_ref, lin_ref, o_ref, l_ref, ostage, lstage = refs[5:]
        else:
            oin_ref, lin_ref, o_ref, l_ref = refs[5:]
        n = pl.program_id(1)
        subs = range(mode.qb)
        halves = (0, 1)
        live = [mode.live_keys(x) for x in halves]
        always = [mode.half_bias(True, x) for x in halves]
        biases = [[mode.half_bias(n > 0, x) for x in halves]] + [always] * (mode.qb - 1)
        _, lo, m_lo, m_hi = _head_masks()
        head_row = lax.broadcasted_iota(jnp.int32, (BLOCK, LANES), 0)
        ones = jnp.ones((2 * BLOCK, LANES), BF16)
        hb = HALF_BLOCK
        lrows = [jnp.zeros((BLOCK, LANES), F32) for _ in subs]
        if not first:
            if nat:
                for g, sl in enumerate(groups):
                    _from_perm(oin_ref, sl, ostage, g)
                _from_perm(lin_ref, all_lanes, lstage, 0)
            wide_rows = lambda a, sub: a[sub * BLOCK:(sub + 1) * BLOCK]
            before = [jnp.transpose(wide_rows(_stage_get(lstage, 0), sub) if nat else mode.get(lin_ref, all_lanes, sub))
                      for sub in subs]

        def probs(sub, p, sl):
            q2 = mode.get(q_ref, sl, sub)
            kcat = mode.keys(kp_ref, kc_ref, sl, sub)
            vcat = mode.keys(vp_ref, vc_ref, sl, sub)
            q_lo, q_hi = q2 * m_lo, q2 * m_hi
            qq = jnp.concatenate([q_lo[:hb], q_hi[:hb], q_lo[hb:], q_hi[hb:]], axis=0)
            s_t = _nt(kcat, qq)
            columns, lses = [], []
            for x in halves:
                r0, r1 = live[x]
                s_x = s_t[r0:r1, x * LANES:(x + 1) * LANES] + biases[sub][x]
                m = jnp.max(s_x, axis=0, keepdims=True)
                pe = jnp.exp(s_x - m)
                lse = m + jnp.log(jnp.sum(pe, axis=0, keepdims=True))
                if not first:
                    was = jnp.concatenate([before[sub][2 * p:2 * p + 1, x * hb:(x + 1) * hb],
                                           before[sub][2 * p + 1:2 * p + 2, x * hb:(x + 1) * hb]], axis=1)
                    top = jnp.maximum(was, lse)
                    lse = top + jnp.log(jnp.exp(was - top) + jnp.exp(lse - top))
                    pe = pe * jnp.exp(m - lse)
                pieces = [pe.astype(BF16)]
                if r0 > 0:
                    pieces.insert(0, jnp.zeros((r0, LANES), BF16))
                if r1 < 2 * BLOCK:
                    pieces.append(jnp.zeros((2 * BLOCK - r1, LANES), BF16))
                columns.append(pieces[0] if len(pieces) == 1 else jnp.concatenate(pieces, axis=0))
                lses.append(lse)
            return jnp.concatenate([vcat, ones], axis=1), jnp.concatenate(columns, axis=1), lses

        def output(sub, p, sl, vext, pb, lses):
            o_ext = _tn(pb, vext)
            if first:
                o_new = o_ext[:, :LANES] / o_ext[:, LANES:]
            else:
                o_prev = wide_rows(_stage_get(ostage, p), sub) if nat else mode.get(oin_ref, sl, sub)
                same = jnp.concatenate([o_prev[:hb], o_prev[:hb], o_prev[hb:], o_prev[hb:]], axis=0)
                o_new = o_ext[:, :LANES] + same * (1.0 - o_ext[:, LANES:])
            head_lo = jnp.concatenate([o_new[:hb], o_new[2 * hb:3 * hb]], axis=0)
            head_hi = jnp.concatenate([o_new[hb:2 * hb], o_new[3 * hb:]], axis=0)
            mode.put(o_ref, sl, jnp.where(lo, head_lo, head_hi), sub)
            lse_lo = jnp.concatenate([lses[0][:, :hb], lses[1][:, :hb]], axis=1)
            lse_hi = jnp.concatenate([lses[0][:, hb:], lses[1][:, hb:]], axis=1)
            rows = jnp.where(head_row == 2 * p, lse_lo, lrows[sub])
            lrows[sub] = jnp.where(head_row == 2 * p + 1, lse_hi, rows)

        pending = None
        for sub in subs:
            for p, sl in enumerate(groups):
                nxt = probs(sub, p, sl)
                if pending is not None:
                    output(*pending)
                pending = (sub, p, sl, *nxt)
        output(*pending)
        for sub in subs:
            mode.put(l_ref, all_lanes, jnp.transpose(lrows[sub]), sub)

    ins = [q, k, k, v, v]
    specs = [mode.wide(attn_w), mode.block_before(attn_w), mode.wide(attn_w), mode.block_before(attn_w), mode.wide(attn_w)]
    scratch = []
    if not first:
        ins += list(run)
        if nat:
            rows_a = mode.qb * BLOCK // PERM
            specs += [pl.BlockSpec((PJ, PJ, rows_a, attn_w), lambda r, n: (0, 0, n, 0)),
                      pl.BlockSpec((PJ, PJ, rows_a, LANES), lambda r, n: (0, 0, n, 0))]
            scratch = [pltpu.VMEM(_stage_shape(len(groups), mode.qb * BLOCK), F32),
                       pltpu.VMEM(_stage_shape(1, mode.qb * BLOCK), F32)]
        else:
            specs += [mode.wide(attn_w), mode.wide(LANES)]
    if nat:
        out_shape = (jax.ShapeDtypeStruct((seq, attn_w), F32), jax.ShapeDtypeStruct((seq, LANES), F32))
    else:
        out_shape = (jax.ShapeDtypeStruct(_perm_shape(seq, attn_w), F32), jax.ShapeDtypeStruct(_perm_shape(seq, LANES), F32))
    return pl.pallas_call(
        body, name=f"attn_fwd_{name}", grid=(mode.residues, mode.steps),
        out_shape=out_shape, in_specs=specs, out_specs=(mode.wide(attn_w), mode.wide(LANES)),
        scratch_shapes=scratch,
        compiler_params=_params(("arbitrary", "arbitrary")),
    )(*ins)


def attn_bwd(name, q, k, v, d_o, lse, delta, run):
    nat = name == "nat"
    seq = q.shape[0] if nat else q.shape[2] * PERM
    attn_w = q.shape[-1]
    mode = _Mode(name, seq)
    steps, qb = mode.steps, mode.qb
    single = steps == 1
    groups = _lane_groups(attn_w)
    first = run is None
    all_lanes = slice(0, LANES)

    def body(*refs):
        q_ref, kp_ref, kc_ref, vp_ref, vc_ref, do_ref, lse_ref, dl_ref = refs[:8]
        if first:
            dq_ref, dk_ref, dv_ref, ck, cv = refs[8:]
        else:
            dqi_ref, dki_ref, dvi_ref, dq_ref, dk_ref, dv_ref, ck, cv = refs[8:]
        g = pl.program_id(1) if single else pl.program_id(0)
        n = g if single else lax.rem(g, steps)
        carries = ((ck, dk_ref, None if first else dki_ref), (cv, dv_ref, None if first else dvi_ref))

        def emit(out_ref, acc_ref, sl, sub, val):
            if acc_ref is not None:
                val = val + mode.get(acc_ref, sl, sub).astype(F32)
            mode.put(out_ref, sl, val, sub)

        if not single:
            @pl.when(g == 0)
            def _():
                ck[...] = jnp.zeros_like(ck)
                cv[...] = jnp.zeros_like(cv)

        @pl.when(g < total)
        def _():
            biases = [mode.bias(n > 0)] + [mode.bias(True)] * (qb - 1)
            _, lo, m_lo, m_hi = _head_masks()

            def scores(sub, p, sl, lse_t, dl_t):
                q2, do2 = mode.get(q_ref, sl, sub), mode.get(do_ref, sl, sub)
                kcat = mode.keys(kp_ref, kc_ref, sl, sub)
                vcat = mode.keys(vp_ref, vc_ref, sl, sub)
                qq = jnp.concatenate([q2 * m_lo, q2 * m_hi], axis=0)
                dd = jnp.concatenate([do2 * m_lo, do2 * m_hi], axis=0)
                h0 = 2 * p
                lse2 = jnp.concatenate([lse_t[h0:h0 + 1, :], lse_t[h0 + 1:h0 + 2, :]], axis=1)
                dl2 = jnp.concatenate([dl_t[h0:h0 + 1, :], dl_t[h0 + 1:h0 + 2, :]], axis=1)
                p_t = jnp.exp(_nt(kcat, qq) + (biases[sub] - lse2))
                ds_t = p_t * (_nt(vcat, dd) - dl2)
                return qq, dd, kcat, p_t.astype(BF16), ds_t.astype(BF16)

            def grads(sub, sl, qq, dd, kcat, pb, dsb):
                dqb = _tn(dsb, kcat)
                dq2 = jnp.where(lo, dqb[:BLOCK], dqb[BLOCK:]) * ATTN_SCALE
                if not first:
                    dq2 = dq2 + mode.get(dqi_ref, sl, sub).astype(F32)
                mode.put(dq_ref, sl, dq2, sub)
                for (carry, out_ref, acc_ref), lhs, rhs in zip(carries, (dsb, pb), (qq, dd)):
                    both = jnp.dot(lhs, rhs, preferred_element_type=F32)
                    if sub == 0:
                        if not single:
                            for s in range(qb - 1):
                                emit(out_ref, acc_ref, sl, s, carry[s, :, sl])
                            emit(out_ref, acc_ref, sl, qb - 1, carry[qb - 1, :, sl] + both[:BLOCK])
                        carry[0, :, sl] = both[BLOCK:]
                    else:
                        carry[sub - 1, :, sl] += both[:BLOCK]
                        carry[sub, :, sl] = both[BLOCK:]
                    if single and sub == qb - 1:
                        for s in range(qb):
                            emit(out_ref, acc_ref, sl, s, carry[s, :, sl])

            stats = [(jnp.transpose(mode.get(lse_ref, all_lanes, sub)),
                      jnp.transpose(mode.get(dl_ref, all_lanes, sub))) for sub in range(qb)]
            pending = None
            for p, sl in enumerate(groups):
                for sub in range(qb):
                    nxt = scores(sub, p, sl, *stats[sub])
                    if pending is not None:
                        grads(*pending)
                    pending = (sub, sl, *nxt)
            grads(*pending)

        if not single:
            @pl.when(g == total)
            def _():
                for carry, out_ref, acc_ref in carries:
                    for sl in groups:
                        for s in range(qb):
                            emit(out_ref, acc_ref, sl, s, carry[s, :, sl])

    total = mode.residues * steps
    if single:
        here = before = lambda r, n: (r, n)
    else:
        locate = lambda g: (g // steps, lax.rem(g, steps))
        here = lambda g: locate(jnp.minimum(g, total - 1))
        before = lambda g: locate(jnp.maximum(g - 1, 0))
    wide = lambda w: mode.wide(w, here)
    ins = [q, k, k, v, v, d_o, lse, delta]
    specs = [wide(attn_w), mode.block_before(attn_w, here), wide(attn_w), mode.block_before(attn_w, here), wide(attn_w),
             wide(attn_w), wide(LANES), wide(LANES)]
    if not first:
        ins += list(run)
        specs += [wide(attn_w), mode.wide(attn_w, before), mode.wide(attn_w, before)]
    shp = jax.ShapeDtypeStruct((seq, attn_w) if nat else _perm_shape(seq, attn_w), BF16)
    grid = (mode.residues, 1) if single else (total + 1,)
    return pl.pallas_call(
        body, name=f"attn_bwd_{name}", grid=grid,
        out_shape=(shp, shp, shp), in_specs=specs,
        out_specs=(wide(attn_w), mode.wide(attn_w, before), mode.wide(attn_w, before)),
        scratch_shapes=[pltpu.VMEM((qb, BLOCK, attn_w), F32), pltpu.VMEM((qb, BLOCK, attn_w), F32)],
        compiler_params=_params(("arbitrary",) * len(grid)),
    )(*ins)


def _shift_down(u, halo, k):
    rolled = pltpu.roll(u, k, 0)
    row = lax.broadcasted_iota(jnp.int32, halo.shape, 0)
    top = jnp.where(row < k, pltpu.roll(halo, k, 0), rolled[:SUBLANES])
    return jnp.concatenate([top, rolled[SUBLANES:]], axis=0)


def _shift_up(u, halo, k):
    rows = u.shape[0]
    rolled = pltpu.roll(u, rows - k, 0)
    row = lax.broadcasted_iota(jnp.int32, halo.shape, 0)
    bot = jnp.where(row >= SUBLANES - k, pltpu.roll(halo, SUBLANES - k, 0), rolled[rows - SUBLANES:])
    return jnp.concatenate([rolled[:rows - SUBLANES], bot], axis=0)


def tail(o, lse, ga, cz, x, tgt, w_out, g2, cw):
    seq, d_model = x.shape
    attn_w = o.shape[1]
    conv_w = cz.shape[1] // 4
    mix = attn_w + conv_w
    groups = _lane_groups(attn_w)
    tm = ROW_TILE
    nt = seq // tm
    hb = tm // SUBLANES

    def body(o_ref, l_ref, ga_ref, cz_ref, hz_ref, x_ref, t_ref, w_ref, g_ref, cw_ref,
             do_ref, dl_ref, dop_ref, dlp_ref, lp_ref, dga_ref, dcb_ref, dgc_ref, dcv_ref, e_ref,
             dw_ref, dg_ref, dcw_ref, loss_ref, stage):
        i = pl.program_id(0)

        @pl.when(i == 0)
        def _():
            dw_ref[...] = jnp.zeros_like(dw_ref)
            dg_ref[...] = jnp.zeros_like(dg_ref)
            dcw_ref[...] = jnp.zeros_like(dcw_ref)
            loss_ref[...] = jnp.zeros_like(loss_ref)

        u = cz_ref[:, 2 * conv_w:3 * conv_w] * cz_ref[:, 0:conv_w]
        uh = hz_ref[:, 2 * conv_w:3 * conv_w] * hz_ref[:, 0:conv_w]
        uh = jnp.where(i > 0, uh, 0.0)
        u1 = _shift_down(u, uh, 1)
        u2 = _shift_down(u, uh, 2)
        w0, w1, w2 = cw_ref[0:1, :], cw_ref[1:2, :], cw_ref[2:3, :]
        cvv = u2 * w0 + u1 * w1 + u * w2
        gv = g_ref[...]
        all_lanes = slice(0, LANES)

        def forward(rs):
            ov, gav = o_ref[rs, :], ga_ref[rs, :]
            sig_a = _sigmoid(gav)
            silu_a = gav * sig_a
            cb, gc = cz_ref[rs, conv_w:2 * conv_w], cz_ref[rs, 3 * conv_w:4 * conv_w]
            sig_c = _sigmoid(gc)
            silu_c = gc * sig_c
            bc = cb * cvv[rs]
            mixed = jnp.concatenate([ov * silu_a, bc * silu_c], axis=1).astype(BF16)
            yv = jnp.dot(mixed, w_ref[...], preferred_element_type=F32)
            return ov, gav, sig_a, silu_a, cb, gc, sig_c, silu_c, bc, mixed, yv

        def loss_and_dy(rs, mixed, yv):
            r2 = lax.rsqrt(jnp.mean(yv * yv, axis=-1, keepdims=True) + NORM_EPS)
            yhat = yv * r2
            diff = (x_ref[rs, :] + yhat * gv) - t_ref[rs, :]
            loss_ref[...] += _rowgroup_sum(diff * diff)
            ev = diff * (1.0 / d_model)
            e_ref[rs, :] = ev
            dg_ref[...] += _rowgroup_sum(ev * yhat)
            eg = ev * gv
            dy = (r2 * (eg - yhat * jnp.mean(eg * yhat, axis=-1, keepdims=True))).astype(BF16)
            dw_ref[...] += _tn(mixed, dy)
            return _nt(dy, w_ref[...])

        def backward(rs, ov, gav, sig_a, silu_a, cb, gc, sig_c, silu_c, bc, dm):
            rows = rs.stop - rs.start
            dma, dmc = dm[:, :attn_w], dm[:, attn_w:]
            dov = dma * silu_a
            do_ref[rs, :] = dov.astype(BF16)
            dga_ref[rs, :] = (dma * ov * (sig_a * (1.0 + gav * (1.0 - sig_a)))).astype(BF16)
            prod = dov * ov
            lane = lax.broadcasted_iota(jnp.int32, (rows, LANES), 1)
            lo = lane < HEAD_DIM
            dblk = jnp.zeros((rows, LANES), F32)
            for p, sl in enumerate(groups):
                pr = prod[:, sl]
                dblk = jnp.where(lane == 2 * p, jnp.sum(jnp.where(lo, pr, 0.0), axis=1, keepdims=True), dblk)
                dblk = jnp.where(lane == 2 * p + 1, jnp.sum(jnp.where(lo, 0.0, pr), axis=1, keepdims=True), dblk)
                _stage_put(stage, p, dov[:, sl], rs.start)
            dl_ref[rs, :] = dblk
            _stage_put(stage, len(groups), dblk, rs.start)
            _stage_put(stage, len(groups) + 1, l_ref[rs, :], rs.start)
            dsc = dmc * silu_c
            cv_rows = cvv[rs]
            dcb_ref[rs, :] = (dsc * cv_rows).astype(BF16)
            dgc_ref[rs, :] = (dmc * bc * (sig_c * (1.0 + gc * (1.0 - sig_c)))).astype(BF16)
            dcv = dsc * cb
            dcv_ref[rs, :] = dcv
            dcw_ref[0:SUBLANES, :] += _rowgroup_sum(dcv * u2[rs])
            dcw_ref[SUBLANES:2 * SUBLANES, :] += _rowgroup_sum(dcv * u1[rs])
            dcw_ref[2 * SUBLANES:3 * SUBLANES, :] += _rowgroup_sum(dcv * u[rs])

        halves = [slice(0, tm // 2), slice(tm // 2, tm)]
        fwd = [forward(rs) for rs in halves]
        dms = [loss_and_dy(rs, f[9], f[10]) for rs, f in zip(halves, fwd)]
        for rs, f, dm in zip(halves, fwd, dms):
            backward(rs, *f[:9], dm)
        for p, sl in enumerate(groups):
            _to_perm(stage, p, dop_ref, sl, BF16)
        _to_perm(stage, len(groups), dlp_ref, all_lanes, F32)
        _to_perm(stage, len(groups) + 1, lp_ref, all_lanes, F32)

    row = lambda n: pl.BlockSpec((tm, n), lambda i: (i, 0))
    whole = lambda a, b: pl.BlockSpec((a, b), lambda i: (0, 0))
    return pl.pallas_call(
        body, name="tail", grid=(nt,),
        out_shape=(jax.ShapeDtypeStruct((seq, attn_w), BF16), jax.ShapeDtypeStruct((seq, LANES), F32),
                   jax.ShapeDtypeStruct(_perm_shape(seq, attn_w), BF16), jax.ShapeDtypeStruct(_perm_shape(seq, LANES), F32),
                   jax.ShapeDtypeStruct(_perm_shape(seq, LANES), F32),
                   jax.ShapeDtypeStruct((seq, attn_w), BF16), jax.ShapeDtypeStruct((seq, conv_w), BF16),
                   jax.ShapeDtypeStruct((seq, conv_w), BF16), jax.ShapeDtypeStruct((seq, conv_w), F32),
                   jax.ShapeDtypeStruct((seq, d_model), F32), jax.ShapeDtypeStruct((mix, d_model), F32),
                   jax.ShapeDtypeStruct((SUBLANES, d_model), F32), jax.ShapeDtypeStruct((CONV_K * SUBLANES, conv_w), F32),
                   jax.ShapeDtypeStruct((SUBLANES, d_model), F32)),
        in_specs=[row(attn_w), row(LANES), row(attn_w), row(4 * conv_w),
                  pl.BlockSpec((SUBLANES, 4 * conv_w), lambda i: (jnp.maximum(i * hb - 1, 0), 0)),
                  row(d_model), row(d_model), _const_spec((mix, d_model)), _const_spec((1, d_model)),
                  _const_spec((SUBLANES, conv_w))],
        out_specs=(row(attn_w), row(LANES), _perm_tile_spec(attn_w, tm), _perm_tile_spec(LANES, tm), _perm_tile_spec(LANES, tm),
                   row(attn_w), row(conv_w), row(conv_w), row(conv_w), row(d_model),
                   whole(mix, d_model), whole(SUBLANES, d_model), whole(CONV_K * SUBLANES, conv_w),
                   whole(SUBLANES, d_model)),
        scratch_shapes=[pltpu.VMEM(_stage_shape(len(groups) + 2, tm), F32)],
        compiler_params=_params(("arbitrary",)),
    )(o, lse, ga, cz, cz, x, tgt, w_out, g2, cw)


def dz_dx(nat_grads, perm_grads, dga, dcb, dgc, dcv, cz, tables, x, g1, e, w_full, cw):
    seq, d_model = x.shape
    attn_w = dga.shape[1]
    conv_w = dcv.shape[1]
    width = w_full.shape[2]
    in_w = 4 * attn_w + 4 * conv_w
    groups = _lane_groups(attn_w)
    tm = ROW_TILE
    nt = seq // tm
    hb = tm // SUBLANES

    def body(dq_ref, dk_ref, dv_ref, dqp_ref, dkp_ref, dvp_ref, dga_ref, dcb_ref, dgc_ref, dcv_ref, nh_ref, ch_ref, cc_ref,
             cos_ref, sgn_ref, x_ref, g_ref, e_ref, w_ref, cw_ref, gx_ref, dz_ref, dg_ref, stage):
        i = pl.program_id(0)

        @pl.when(i == 0)
        def _():
            dg_ref[...] = jnp.zeros_like(dg_ref)

        cos, (s1, s2) = cos_ref[...], _split_sin(sgn_ref[...])

        def qkv_columns(t, nat_ref, perm_ref):
            for g, sl in enumerate(groups):
                _from_perm(perm_ref, sl, stage, g)
            for g, sl in enumerate(groups):
                tot = nat_ref[:, sl].astype(F32) + _stage_get(stage, g)
                if t < 2:
                    tot = _rope_transposed(tot, cos, s1, s2)
                dz_ref[:, t * attn_w + g * LANES:t * attn_w + (g + 1) * LANES] = tot.astype(BF16)

        def dh_part(j):
            return _nt(dz_ref[:, j * width:(j + 1) * width], w_ref[j])

        dcv = dcv_ref[...]
        nh = jnp.where(i < nt - 1, nh_ref[...], 0.0)
        w0, w1, w2 = cw_ref[0:1, :], cw_ref[1:2, :], cw_ref[2:3, :]
        du = dcv * w2 + _shift_up(dcv, nh, 1) * w1 + _shift_up(dcv, nh, 2) * w0
        base = 4 * attn_w
        dz_ref[:, base:base + conv_w] = (du * cc_ref[...]).astype(BF16)
        dz_ref[:, base + conv_w:base + 2 * conv_w] = dcb_ref[...]
        dz_ref[:, base + 2 * conv_w:base + 3 * conv_w] = (du * ch_ref[...]).astype(BF16)
        dz_ref[:, base + 3 * conv_w:base + 4 * conv_w] = dgc_ref[...]
        dz_ref[:, 3 * attn_w:4 * attn_w] = dga_ref[...]
        ready = in_w
        dh = None
        for t, nat_ref, perm_ref in ((2, dv_ref, dvp_ref), (1, dk_ref, dkp_ref), (0, dq_ref, dqp_ref), (None, None, None)):
            lowest_open = 0 if t is None else (t + 1) * attn_w
            while ready - width >= lowest_open:
                ready -= width
                part = dh_part(ready // width)
                dh = part if dh is None else dh + part
            if t is not None:
                qkv_columns(t, nat_ref, perm_ref)
        xv = x_ref[...]
        r1 = lax.rsqrt(jnp.mean(xv * xv, axis=-1, keepdims=True) + NORM_EPS)
        xhat = xv * r1
        dg_ref[...] += _rowgroup_sum(dh * xhat)
        dhg = dh * g_ref[...]
        gx_ref[...] = r1 * (dhg - xhat * jnp.mean(dhg * xhat, axis=-1, keepdims=True)) + e_ref[...]

    row = lambda n: pl.BlockSpec((tm, n), lambda i: (i, 0))
    whole = lambda a, b: pl.BlockSpec((a, b), lambda i: (0, 0))
    pt = _perm_tile_spec(attn_w, tm)
    return pl.pallas_call(
        body, name="dz_dx", grid=(nt,),
        out_shape=(jax.ShapeDtypeStruct((seq, d_model), F32), jax.ShapeDtypeStruct((seq, in_w), BF16),
                   jax.ShapeDtypeStruct((SUBLANES, d_model), F32)),
        in_specs=[row(attn_w), row(attn_w), row(attn_w), pt, pt, pt, row(attn_w), row(conv_w), row(conv_w), row(conv_w),
                  pl.BlockSpec((SUBLANES, conv_w), lambda i: (jnp.minimum((i + 1) * hb, seq // SUBLANES - 1), 0)),
                  pl.BlockSpec((tm, conv_w), lambda i: (i, 0)), pl.BlockSpec((tm, conv_w), lambda i: (i, 2)),
                  row(LANES), row(LANES), row(d_model), _const_spec((1, d_model)), row(d_model),
                  _const_spec(w_full.shape), _const_spec((SUBLANES, conv_w))],
        out_specs=(row(d_model), row(in_w), whole(SUBLANES, d_model)),
        scratch_shapes=[pltpu.VMEM(_stage_shape(len(groups), tm), F32)],
        compiler_params=_params(("arbitrary",)),
    )(*nat_grads, *perm_grads, dga, dcb, dgc, dcv, dcv, cz, cz, *tables, x, g1, e, w_full, cw)


def dw_in_reduce(ht, dz, g_out, small):
    d_model, seq = ht.shape
    half = dz.shape[1] // N_DEV
    ts = min(DW_ROWS, seq)
    steps = seq // ts
    x, y, c = lax.axis_index("x"), lax.axis_index("y"), lax.axis_index("c")
    far_first = lambda x, y: [(1 - x, 1 - y), (1 - x, y), (x, 1 - y)]
    chips = jnp.stack([2 * px + py for px, py in far_first(x, y)] + [2 * x + y]).astype(jnp.int32)
    order = jnp.stack([2 * chips + (1 - c), 2 * chips + c], axis=1).reshape(N_DEV)

    def body(order_ref, ht_ref, dz_ref, go_ref, sm_ref, out_ref, ro_ref, rs_ref,
             acc, theirs, staged, contrib, resbuf, out_sem, sa, ra, sb, rb, sc, rc,
             o_mine, o_theirs, o_staged, o_contrib, o_res, sbuf, o_load, osa, ora, osb, orb, osc, orc, ss, rs):
        del order_ref
        p, s = pl.program_id(0), pl.program_id(1)
        x, y, c = lax.axis_index("x"), lax.axis_index("y"), lax.axis_index("c")
        me = 2 * x + y
        sib = (x, y, 1 - c)
        peers = far_first(x, y)
        slot = p % 2

        flips = [(fx, fy, fc) for fx in (0, 1) for fy in (0, 1) for fc in (0, 1)][1:]
        my8 = 4 * x + 2 * y + c
        chip_ids = [2 * px + py for px, py in peers] + [me]

        def small_copy(k, slot8, to):
            return pltpu.make_async_remote_copy(src_ref=sm_ref, dst_ref=sbuf.at[slot8], send_sem=ss.at[k], recv_sem=rs.at[k],
                                                device_id=to, device_id_type=MESH)

        def small_peer(k):
            fx, fy, fc = flips[k]
            return _flip(x, fx), _flip(y, fy), _flip(c, fc)

        def oa_copy(pos):
            j = chip_ids[pos]
            return pltpu.make_async_remote_copy(src_ref=go_ref.at[j, 1 - c], dst_ref=o_theirs.at[j], send_sem=osa.at[pos],
                                                recv_sem=ora.at[pos], device_id=sib, device_id_type=MESH)

        def o_load_copy(pos):
            j = chip_ids[pos]
            return pltpu.make_async_copy(go_ref.at[j, c], o_mine.at[j], o_load.at[pos])

        def ob_copy(k, piece, slot4):
            px, py = peers[k]
            return pltpu.make_async_remote_copy(src_ref=o_staged.at[piece], dst_ref=o_contrib.at[slot4], send_sem=osb.at[k],
                                                recv_sem=orb.at[k], device_id=(px, py, c), device_id_type=MESH)

        def oc_copy(which):
            return pltpu.make_async_remote_copy(src_ref=o_res.at[which], dst_ref=o_res.at[which], send_sem=osc, recv_sem=orc,
                                                device_id=sib, device_id_type=MESH)

        @pl.when((p == 0) & (s == 0))
        def _():
            sbuf[my8] = sm_ref[...]
            for k in range(N_DEV - 1):
                small_copy(k, my8, small_peer(k)).start()
            for pos in range(N_CHIPS):
                o_load_copy(pos).start()
                oa_copy(pos).start()

        @pl.when((p == 1) & (s == steps - 1))
        def _():
            for pos in range(N_CHIPS):
                j = chip_ids[pos]
                o_load_copy(pos).wait()
                oa_copy(pos).wait_recv()
                if pos < N_CHIPS - 1:
                    o_staged[j] = (o_mine[j] + o_theirs[j]).astype(BF16)
                    ob_copy(pos, j, me).start()
                else:
                    o_mine[j] = o_mine[j] + o_theirs[j]
                    o_contrib[j] = o_mine[j].astype(BF16)

        @pl.when((p == 4) & (s == steps - 1))
        def _():
            for k in range(N_CHIPS - 1):
                ob_copy(k, me, chip_ids[k]).wait_recv()
            own = o_mine[me]
            term = lambda j: jnp.where(me == j, own, o_contrib[j].astype(F32))
            o_res[c] = ((term(0) + term(1)) + term(2)) + term(3)
            oc_copy(c).start()

        def a_copy(k):
            return pltpu.make_async_remote_copy(src_ref=acc.at[0], dst_ref=theirs.at[k], send_sem=sa.at[k], recv_sem=ra.at[k],
                                                device_id=sib, device_id_type=MESH)

        def b_copy(k):
            px, py = peers[k]
            return pltpu.make_async_remote_copy(src_ref=staged.at[k], dst_ref=contrib.at[k], send_sem=sb.at[k], recv_sem=rb.at[k],
                                                device_id=(px, py, c), device_id_type=MESH)

        def c_copy(which):
            return pltpu.make_async_remote_copy(src_ref=resbuf.at[which], dst_ref=resbuf.at[which], send_sem=sc, recv_sem=rc,
                                                device_id=sib, device_id_type=MESH)

        @pl.when(s == 0)
        def _():
            for k in range(N_CHIPS - 1):
                @pl.when(p == 2 * k + 2)
                def _():
                    a_copy(k).wait_send()
            acc[slot] = jnp.dot(ht_ref[...], dz_ref[...], preferred_element_type=F32)

        @pl.when(s > 0)
        def _():
            acc[slot] += jnp.dot(ht_ref[...], dz_ref[...], preferred_element_type=F32)

        @pl.when(s == steps - 1)
        def _():
            for k in range(N_CHIPS):
                @pl.when(p == 2 * k)
                def _():
                    a_copy(k).start()
            for k in range(N_CHIPS - 1):
                @pl.when(p == 2 * k + 1)
                def _():
                    a_copy(k).wait_recv()
                    staged[k] = (acc[1] + theirs[k]).astype(BF16)
                    b_copy(k).start()

            @pl.when(p == N_DEV - 1)
            def _():
                a_copy(N_CHIPS - 1).wait_recv()
                tot = acc[1] + theirs[N_CHIPS - 1]
                for k in range(N_CHIPS - 1):
                    b_copy(k).wait_recv()
                    tot = tot + contrib[k].astype(F32)
                resbuf[c] = tot
                c_copy(c).start()
                c_copy(1 - c).wait_recv()
                done = pltpu.make_async_copy(resbuf, out_ref, out_sem)
                done.start()
                oc_copy(1 - c).wait_recv()
                ro_ref[...] = o_res[...]
                for k in range(N_DEV - 1):
                    px, py, pc = small_peer(k)
                    small_copy(k, 4 * px + 2 * py + pc, (px, py, pc)).wait_recv()
                tot8 = sbuf[0]
                for d in range(1, N_DEV):
                    tot8 = tot8 + sbuf[d]
                rs_ref[...] = tot8
                a_copy(N_CHIPS - 1).wait_send()
                for k in range(N_CHIPS - 1):
                    b_copy(k).wait_send()
                    ob_copy(k, chip_ids[k], me).wait_send()
                c_copy(c).wait_send()
                oc_copy(c).wait_send()
                for pos in range(N_CHIPS):
                    oa_copy(pos).wait_send()
                for k in range(N_DEV - 1):
                    small_copy(k, my8, small_peer(k)).wait_send()
                done.wait()

    dma = pltpu.SemaphoreType.DMA
    o_shape = g_out.shape[1:]
    go = g_out.reshape(N_CHIPS, 2, *o_shape)
    const = lambda shape: pl.BlockSpec(shape, lambda p, s, order_ref: (0,) * len(shape))
    grid_spec = pltpu.PrefetchScalarGridSpec(
        num_scalar_prefetch=1, grid=(N_DEV, steps),
        in_specs=[pl.BlockSpec((d_model, ts), lambda p, s, order_ref: (0, s)),
                  pl.BlockSpec((ts, half), lambda p, s, order_ref: (s, order_ref[p])),
                  pl.BlockSpec(memory_space=pl.ANY), const(small.shape)],
        out_specs=(pl.BlockSpec(memory_space=pl.ANY), const((2, *o_shape)), const(small.shape)),
        scratch_shapes=[pltpu.VMEM((2, d_model, half), F32), pltpu.VMEM((N_CHIPS, d_model, half), F32),
                        pltpu.VMEM((N_CHIPS - 1, d_model, half), BF16), pltpu.VMEM((N_CHIPS - 1, d_model, half), BF16),
                        pltpu.VMEM((2, d_model, half), F32), dma,
                        dma((N_CHIPS,)), dma((N_CHIPS,)), dma((N_CHIPS - 1,)), dma((N_CHIPS - 1,)), dma, dma,
                        pltpu.VMEM((N_CHIPS, *o_shape), F32), pltpu.VMEM((N_CHIPS, *o_shape), F32),
                        pltpu.VMEM((N_CHIPS, *o_shape), BF16), pltpu.VMEM((N_CHIPS, *o_shape), BF16),
                        pltpu.VMEM((2, *o_shape), F32), pltpu.VMEM((N_DEV, *small.shape), F32),
                        dma((N_CHIPS,)), dma((N_CHIPS,)), dma((N_CHIPS,)), dma((N_CHIPS - 1,)), dma((N_CHIPS - 1,)), dma, dma,
                        dma((N_DEV - 1,)), dma((N_DEV - 1,))])
    return pl.pallas_call(
        body, name="dw_in_reduce", grid_spec=grid_spec,
        out_shape=(jax.ShapeDtypeStruct((2, d_model, half), F32), jax.ShapeDtypeStruct((2, *o_shape), F32),
                   jax.ShapeDtypeStruct(small.shape, F32)),
        compiler_params=_params(("arbitrary", "arbitrary")),
    )(order, ht, dz, go, small)


def _adam_math(w, g, m, v):
    m = ADAM_B1 * m + (1.0 - ADAM_B1) * g
    v = ADAM_B2 * v + (1.0 - ADAM_B2) * (g * g)
    m_hat = m / (1.0 - ADAM_B1 ** ADAM_STEP)
    v_hat = v / (1.0 - ADAM_B2 ** ADAM_STEP)
    delta = -ADAM_LR * (m_hat / (jnp.sqrt(v_hat) + ADAM_EPS) + ADAM_WD * w)
    return delta, m, v


def adam_shard(name, w, g2, m, v, block, grid, w_map, g_map):
    def body(w_ref, g_ref, m_ref, v_ref, go_ref, d_ref, mo_ref, vo_ref):
        g = g_ref[0]
        delta, mn, vn = _adam_math(w_ref[...], g, m_ref[...], v_ref[...])
        go_ref[...] = g
        d_ref[...] = delta
        mo_ref[...] = mn
        vo_ref[...] = vn

    ws = pl.BlockSpec(block, w_map)
    shp = jax.ShapeDtypeStruct(w.shape, F32)
    return pl.pallas_call(
        body, name=name, grid=grid, out_shape=(shp, shp, shp, shp),
        in_specs=[ws, pl.BlockSpec((1, *block), g_map), ws, ws], out_specs=(ws, ws, ws, ws),
        compiler_params=_params(("arbitrary",) * len(grid)),
    )(w, g2, m, v)


def adam_small(ws, gs, ms, vs):
    n = len(ws)

    def body(*refs):
        ins, outs = refs[:4 * n], refs[4 * n:]
        for t in range(n):
            delta, mn, vn = _adam_math(ins[t][...], ins[n + t][...], ins[2 * n + t][...], ins[3 * n + t][...])
            outs[3 * t][...] = delta
            outs[3 * t + 1][...] = mn
            outs[3 * t + 2][...] = vn

    vm = pl.BlockSpec(memory_space=pltpu.VMEM)
    outs = pl.pallas_call(
        body, name="adam_small",
        out_shape=tuple(jax.ShapeDtypeStruct(w.shape, F32) for w in ws for _ in range(3)),
        in_specs=[vm] * (4 * n), out_specs=tuple([vm] * (3 * n)),
        compiler_params=_params(),
    )(*ws, *gs, *ms, *vs)
    return [outs[3 * t:3 * t + 3] for t in range(n)]


def kernel(x, norm_pre_g, w_in, conv_w, w_out, norm_post_g, loss_target, m_norm_pre_g, m_w_in, m_conv_w, m_w_out, m_norm_post_g, v_norm_pre_g, v_w_in, v_conv_w, v_w_out, v_norm_post_g):
    _, seq, d_model = x.shape
    width = w_in.shape[1]
    conv_q = conv_w.shape[1]
    conv_width = N_CHIPS * conv_q
    attn_width = d_model - conv_width
    xs, tg = x[0], loss_target[0]
    g1, g2 = norm_pre_g.reshape(1, d_model), norm_post_g.reshape(1, d_model)

    w_full, wout_full, cw_full, *tables = gather_weights(w_in, w_out, conv_w, seq)
    wout2 = wout_full.reshape(attn_width + conv_width, d_model)
    cw = jnp.zeros((SUBLANES, conv_width), F32).at[:CONV_K].set(
        cw_full[:, :CONV_K, :conv_q].transpose(1, 0, 2).reshape(CONV_K, conv_width))

    ht, q, k, v, qp, kp, vp, ga, cz = inproj(xs, g1, w_full, tables, attn_width, conv_width)
    run = attn_fwd("p4", qp, kp, vp, None)
    run = attn_fwd("p16", qp, kp, vp, run)
    o, lse = attn_fwd("nat", q, k, v, run)
    (d_o, delta, d_op, delta_p, lse_p, dga, dcb, dgc, dcv, e, dwout, dg2, dcw, loss_acc) = tail(
        o, lse, ga, cz, xs, tg, wout2, g2, cw)
    nat_grads = attn_bwd("nat", q, k, v, d_o, lse, delta, None)
    perm_grads = attn_bwd("p4", qp, kp, vp, d_op, lse_p, delta_p, None)
    perm_grads = attn_bwd("p16", qp, kp, vp, d_op, lse_p, delta_p, perm_grads)
    grad_x, dz, dg1 = dz_dx(nat_grads, perm_grads, dga, dcb, dgc, dcv, cz, tables, xs, g1, e, w_full, cw)

    small = jnp.zeros((SUBLANES, d_model), F32)
    small = small.at[0].set(dg1.sum(axis=0)).at[1].set(dg2.sum(axis=0))
    small = small.at[2:2 + CONV_K, :conv_width].set(dcw.reshape(CONV_K, SUBLANES, conv_width).sum(axis=1))
    small = small.at[2 + CONV_K, 0].set(jnp.sum(loss_acc))
    rin, rout, rsmall = dw_in_reduce(ht, dz, dwout.reshape(N_DEV, -1, d_model), small)

    half = width // 2
    tr = min(ADAM_ROWS, d_model)
    gw_in, d_in, m_in, v_in = adam_shard(
        "adam_w_in", w_in, rin, m_w_in, v_w_in, (tr, half), (2, d_model // tr),
        lambda hf, i: (i, hf), lambda hf, i: (hf, i, 0))
    rq = w_out.shape[0] // 2
    gw_out, d_out, m_out, v_out = adam_shard(
        "adam_w_out", w_out, rout, m_w_out, v_w_out, (rq, d_model), (2,),
        lambda hf: (hf, 0), lambda hf: (hf, 0, 0))

    chip = 2 * lax.axis_index("x") + lax.axis_index("y")
    g_pre, g_post = rsmall[0:1], rsmall[1:2]
    g_conv = lax.dynamic_slice(rsmall[2:2 + CONV_K, :conv_width], (0, chip * conv_q), (CONV_K, conv_q))
    (d_pre, m_pre, v_pre), (d_post, m_post, v_post), (d_cv, m_cv, v_cv) = adam_small(
        [g1, g2, conv_w], [g_pre, g_post, g_conv],
        [m_norm_pre_g.reshape(1, d_model), m_norm_post_g.reshape(1, d_model), m_conv_w],
        [v_norm_pre_g.reshape(1, d_model), v_norm_post_g.reshape(1, d_model), v_conv_w])

    loss = 0.5 * rsmall[2 + CONV_K, 0] / d_model
    vec = lambda a: a.reshape(d_model)
    return (loss, grad_x.reshape(1, seq, d_model),
            vec(g_pre), gw_in, g_conv, gw_out, vec(g_post),
            vec(d_pre), d_in, d_cv, d_out, vec(d_post),
            vec(m_pre), m_in, m_cv, m_out, vec(m_post),
            vec(v_pre), v_in, v_cv, v_out, vec(v_post))
```

```python
import jax
import jax.numpy as jnp
from jax import lax
from jax.experimental import pallas as pl
from jax.experimental.pallas import tpu as pltpu

HEAD_DIM = 64
LANES = 128
SUBLANES = 8
BLOCK = 128
HALF_BLOCK = BLOCK // 2
WINDOW_KEYS = 128
PERM = 16
PJ = 4
P4_ROWS = BLOCK // PJ
MAX_QUERY_BLOCKS = 8
ROW_TILE = 512
DW_ROWS = 4096
ADAM_ROWS = 1024
CONV_K = 3
ROPE_THETA = 10000.0
NORM_EPS = 1e-6
ATTN_SCALE = HEAD_DIM ** -0.5
NEG = -1e30
N_CHIPS = 4
N_DEV = 8
MESH = pl.DeviceIdType.MESH
ADAM_LR = 0.001
ADAM_B1 = 0.9
ADAM_B2 = 0.999
ADAM_EPS = 1e-08
ADAM_WD = 0.01
ADAM_STEP = 10
VMEM_LIMIT = 63 * 1024 * 1024

F32 = jnp.float32
BF16 = jnp.bfloat16


def _params(sem=None, **kw):
    return pltpu.CompilerParams(dimension_semantics=sem, vmem_limit_bytes=VMEM_LIMIT, **kw)


def _const_spec(shape):
    return pl.BlockSpec(shape, lambda *_: (0,) * len(shape), pipeline_mode=pl.Buffered(1))


def _sigmoid(z):
    return 1.0 / (1.0 + jnp.exp(-z))


def _rowgroup_sum(a):
    rows, n = a.shape
    return a.reshape(rows // SUBLANES, SUBLANES, n).sum(axis=0)


def _nt(a, b):
    return lax.dot_general(a, b, (((1,), (1,)), ((), ())), preferred_element_type=F32)


def _tn(a, b):
    return lax.dot_general(a, b, (((0,), (0,)), ((), ())), preferred_element_type=F32)


def _col_pieces(a, b, width):
    out = []
    while a < b:
        j = a // width
        e = min(b, (j + 1) * width)
        out.append((j, a - j * width, e - j * width))
        a = e
    return out


def _lane_groups(width):
    return [slice(g * LANES, (g + 1) * LANES) for g in range(width // LANES)]


def _perm_shape(seq, width):
    return (PJ, PJ, seq // PERM, width)


def _perm_tile_spec(width, tm):
    return pl.BlockSpec((PJ, PJ, tm // PERM, width), lambda i: (0, 0, i, 0))


STAGE_PITCH = 24


def _stage_shape(groups, rows):
    return (groups, rows // PERM * STAGE_PITCH, LANES)


def _stage_put(stage, g, val, row0=0):
    for a in range(val.shape[0] // PERM):
        at = (row0 // PERM + a) * STAGE_PITCH
        stage[g, at:at + PERM, :] = val[a * PERM:(a + 1) * PERM]


def _stage_get(stage, g):
    return jnp.concatenate([stage[g, a * STAGE_PITCH:a * STAGE_PITCH + PERM, :]
                            for a in range(stage.shape[1] // STAGE_PITCH)], axis=0)


def _to_perm(stage, g, dst_ref, sl, dtype):
    rows = stage.shape[1] // STAGE_PITCH
    for b in range(PERM):
        dst_ref[b // PJ, b % PJ, :, sl] = stage[g, pl.ds(b, rows, stride=STAGE_PITCH), :].astype(dtype)


def _from_perm(src_ref, sl, stage, g):
    rows = stage.shape[1] // STAGE_PITCH
    for b in range(PERM):
        stage[g, pl.ds(b, rows, stride=STAGE_PITCH), :] = src_ref[b // PJ, b % PJ, :, sl].astype(F32)


def _flip(a, f):
    return 1 - a if f else a


def gather_weights(w_in, w_out, conv_w, seq):
    d_model, width = w_in.shape
    rows = w_out.shape[0]
    cw = jnp.zeros((SUBLANES, LANES), F32).at[:CONV_K, :conv_w.shape[1]].set(conv_w)
    half_dim = HEAD_DIM // 2
    inv_freq = ROPE_THETA ** (-jnp.arange(half_dim, dtype=F32) * 2.0 / HEAD_DIM)
    inv_freq = jnp.tile(inv_freq, LANES // half_dim).reshape(1, LANES)
    chunk = min(ROW_TILE, seq)

    def body(win_ref, wout_ref, cw_ref, freq_ref, winf_ref, woutf_ref, cwf_ref, cos_ref, s1_ref, s2_ref,
             st_in, st_out, near_send, near_recv, far_send, far_recv, cw_send, cw_recv, d2d_send, d2d_recv):
        x, y, c = lax.axis_index("x"), lax.axis_index("y"), lax.axis_index("c")
        me = 2 * x + y
        sib = (x, y, 1 - c)
        st_in[...] = win_ref[...].astype(BF16)
        st_out[...] = wout_ref[...].astype(BF16)
        winf_ref[me] = st_in[...]
        woutf_ref[me] = st_out[...]
        cwf_ref[me] = cw_ref[...]
        stages = (st_in, st_out)
        fulls = (winf_ref, woutf_ref)
        halves = (d_model // 2, rows // 2)

        def part(t, core, q=None):
            size = halves[t] if q is None else halves[t] // 2
            start = core * halves[t] if q is None else core * halves[t] + q * size
            return pl.ds(pl.multiple_of(start, size), size)

        near = [(1 - x, y), (x, 1 - y)]
        far = (1 - x, 1 - y)
        chip = lambda px, py: 2 * px + py

        def direct(k, t, q, slot, to):
            src = stages[t].at[part(t, c, q)]
            return pltpu.make_async_remote_copy(src_ref=src, dst_ref=fulls[t].at[slot, part(t, c, q)], send_sem=near_send.at[k, t, q],
                                                recv_sem=near_recv.at[k, t, q], device_id=to, device_id_type=MESH)

        def passed_on(k, t, slot, to):
            ref = fulls[t].at[slot, part(t, c, k)]
            return pltpu.make_async_remote_copy(src_ref=ref, dst_ref=ref, send_sem=far_send.at[k, t], recv_sem=far_recv.at[k, t],
                                                device_id=to, device_id_type=MESH)

        def conv_copy(k, slot, to):
            return pltpu.make_async_remote_copy(src_ref=cw_ref, dst_ref=cwf_ref.at[slot], send_sem=cw_send.at[k], recv_sem=cw_recv.at[k],
                                                device_id=to, device_id_type=MESH)

        def d2d(k, t, slot, core):
            ref = fulls[t].at[slot, part(t, core)]
            return pltpu.make_async_remote_copy(src_ref=ref, dst_ref=ref, send_sem=d2d_send.at[k, t], recv_sem=d2d_recv.at[k, t],
                                                device_id=sib, device_id_type=MESH)

        sends = []

        def go(cp):
            cp.start()
            sends.append(cp)

        for q_first in (0, 1):
            for k, (px, py) in enumerate(near):
                for t in range(2):
                    go(direct(k, t, k if q_first == 0 else 1 - k, me, (px, py, c)))
        for k, (px, py) in enumerate(near + [far]):
            go(conv_copy(k, me, (px, py, c)))
        for k, (px, py) in enumerate(near):
            other = near[1 - k]
            for t in range(2):
                direct(k, t, k, chip(px, py), (px, py, c)).wait_recv()
                go(passed_on(k, t, chip(px, py), (*other, c)))

        first_half = lax.broadcasted_iota(jnp.int32, (chunk, LANES), 1) % HEAD_DIM < half_dim
        row = lax.broadcasted_iota(jnp.int32, (chunk, LANES), 0)

        def table_rows(i, carry):
            at = pl.multiple_of(i * chunk, chunk)
            ang = (row + at).astype(F32) * freq_ref[...]
            sin = jnp.sin(ang)
            cos_ref[pl.ds(at, chunk), :] = jnp.cos(ang)
            s1_ref[pl.ds(at, chunk), :] = jnp.where(first_half, -sin, 0.0)
            s2_ref[pl.ds(at, chunk), :] = jnp.where(first_half, 0.0, sin)
            return carry

        lax.fori_loop(0, seq // chunk, table_rows, 0)

        for k, (px, py) in enumerate(near):
            for t in range(2):
                direct(k, t, 1 - k, chip(px, py), (px, py, c)).wait_recv()
                go(d2d(k, t, chip(px, py), c))
        for t in range(2):
            for k, (px, py) in enumerate(near):
                passed_on(k, t, chip(*far), (px, py, c)).wait_recv()
            go(d2d(2, t, chip(*far), c))
        for k, (px, py) in enumerate(near + [far]):
            conv_copy(k, chip(px, py), (px, py, c)).wait_recv()
            for t in range(2):
                d2d(k, t, chip(px, py), 1 - c).wait_recv()
        for cp in sends:
            cp.wait_send()

    vm = pl.BlockSpec(memory_space=pltpu.VMEM)
    dma = pltpu.SemaphoreType.DMA
    return pl.pallas_call(
        body, name="gather_weights",
        out_shape=(jax.ShapeDtypeStruct((N_CHIPS, d_model, width), BF16),
                   jax.ShapeDtypeStruct((N_CHIPS, rows, d_model), BF16),
                   jax.ShapeDtypeStruct((N_CHIPS, SUBLANES, LANES), F32),
                   *[jax.ShapeDtypeStruct((seq, LANES), F32)] * 3),
        in_specs=[vm, vm, vm, vm], out_specs=(vm,) * 6,
        scratch_shapes=[pltpu.VMEM((d_model, width), BF16), pltpu.VMEM((rows, d_model), BF16),
                        dma((2, 2, 2)), dma((2, 2, 2)), dma((2, 2)), dma((2, 2)), dma((3,)), dma((3,)),
                        dma((3, 2)), dma((3, 2))],
        compiler_params=_params(),
    )(w_in, w_out, cw, inv_freq)


def _rope(t, cos, s1, s2):
    return t * cos + pltpu.roll(t, LANES - HEAD_DIM // 2, 1) * s1 + pltpu.roll(t, HEAD_DIM // 2, 1) * s2


def _rope_transposed(g, cos, s1, s2):
    return g * cos + pltpu.roll(g * s1, HEAD_DIM // 2, 1) + pltpu.roll(g * s2, LANES - HEAD_DIM // 2, 1)


def inproj(x, g1, w_full, tables, attn_w, conv_w):
    seq, d_model = x.shape
    width = w_full.shape[2]
    tm = ROW_TILE
    groups = _lane_groups(attn_w)

    def body(x_ref, g_ref, w_ref, cos_ref, s1_ref, s2_ref,
             ht_ref, q_ref, k_ref, v_ref, qp_ref, kp_ref, vp_ref, ga_ref, cz_ref, stage):
        xv = x_ref[...]
        hb = ((xv * lax.rsqrt(jnp.mean(xv * xv, axis=-1, keepdims=True) + NORM_EPS)) * g_ref[...]).astype(BF16)
        ht_ref[...] = jnp.transpose(hb)
        cos, s1, s2 = cos_ref[...], s1_ref[...], s2_ref[...]

        def proj(a, b):
            parts = [jnp.dot(hb, w_ref[j, :, lo:hi], preferred_element_type=F32) for j, lo, hi in _col_pieces(a, b, width)]
            return parts[0] if len(parts) == 1 else jnp.concatenate(parts, axis=1)

        def emit(z, nat_ref, perm_ref, fn):
            for g, sl in enumerate(groups):
                val = fn(z[:, sl])
                nat_ref[:, sl] = val.astype(BF16)
                _stage_put(stage, g, val)
            for g, sl in enumerate(groups):
                _to_perm(stage, g, perm_ref, sl, BF16)

        emit(proj(0, attn_w), q_ref, qp_ref, lambda t: _rope(t, cos, s1, s2) * ATTN_SCALE)
        emit(proj(attn_w, 2 * attn_w), k_ref, kp_ref, lambda t: _rope(t, cos, s1, s2))
        emit(proj(2 * attn_w, 3 * attn_w), v_ref, vp_ref, lambda t: t)
        ga_ref[...] = proj(3 * attn_w, 4 * attn_w)
        cz_ref[...] = proj(4 * attn_w, 4 * attn_w + 4 * conv_w)

    row = lambda n: pl.BlockSpec((tm, n), lambda i: (i, 0))
    nat = jax.ShapeDtypeStruct((seq, attn_w), BF16)
    perm = jax.ShapeDtypeStruct(_perm_shape(seq, attn_w), BF16)
    return pl.pallas_call(
        body, name="inproj", grid=(seq // tm,),
        out_shape=(jax.ShapeDtypeStruct((d_model, seq), BF16), nat, nat, nat, perm, perm, perm,
                   jax.ShapeDtypeStruct((seq, attn_w), F32), jax.ShapeDtypeStruct((seq, 4 * conv_w), F32)),
        in_specs=[row(d_model), _const_spec((1, d_model)), _const_spec(w_full.shape), row(LANES), row(LANES), row(LANES)],
        out_specs=(pl.BlockSpec((d_model, tm), lambda i: (0, i)), row(attn_w), row(attn_w), row(attn_w),
                   _perm_tile_spec(attn_w, tm), _perm_tile_spec(attn_w, tm), _perm_tile_spec(attn_w, tm),
                   row(attn_w), row(4 * conv_w)),
        scratch_shapes=[pltpu.VMEM(_stage_shape(len(groups), tm), F32)],
        compiler_params=_params(("arbitrary",)),
    )(x, g1, w_full, *tables)


class _Mode:
    def __init__(self, name, seq):
        self.name = name
        if name == "nat":
            self.residues, blocks = 1, seq // BLOCK
        elif name == "p16":
            self.residues, blocks = PERM, seq // PERM // BLOCK
        else:
            self.residues, blocks = PJ, seq // PERM // P4_ROWS
        self.qb = max(d for d in range(1, MAX_QUERY_BLOCKS + 1) if blocks % d == 0)
        self.steps = blocks // self.qb

    def _spec(self, blocks, width, at):
        if self.name == "nat":
            return pl.BlockSpec((blocks * BLOCK, width), lambda *g: (at(*g)[1], 0))
        if self.name == "p16":
            return pl.BlockSpec((1, 1, blocks * BLOCK, width), lambda *g: (at(*g)[0] // PJ, at(*g)[0] % PJ, at(*g)[1], 0))
        return pl.BlockSpec((PJ, 1, blocks * P4_ROWS, width), lambda *g: (0, at(*g)[0], at(*g)[1], 0))

    def wide(self, width, where=lambda r, n: (r, n)):
        return self._spec(self.qb, width, where)

    def block_before(self, width, where=lambda r, n: (r, n)):
        return self._spec(1, width, lambda *g: (where(*g)[0], jnp.maximum(self.qb * where(*g)[1] - 1, 0)))

    def get(self, ref, sl, sub=0):
        if self.name == "nat":
            return ref[sub * BLOCK:(sub + 1) * BLOCK, sl]
        if self.name == "p16":
            return ref[0, 0, sub * BLOCK:(sub + 1) * BLOCK, sl]
        return jnp.concatenate([ref[j, 0, at:at + P4_ROWS // 2, sl] for j, at in self._p4_chunks(sub)], axis=0)

    def put(self, ref, sl, val, sub=0):
        val = val.astype(ref.dtype)
        if self.name == "nat":
            ref[sub * BLOCK:(sub + 1) * BLOCK, sl] = val
        elif self.name == "p16":
            ref[0, 0, sub * BLOCK:(sub + 1) * BLOCK, sl] = val
        else:
            for i, (j, at) in enumerate(self._p4_chunks(sub)):
                ref[j, 0, at:at + P4_ROWS // 2, sl] = val[i * (P4_ROWS // 2):(i + 1) * (P4_ROWS // 2)]

    @staticmethod
    def _p4_chunks(sub):
        return [(j, sub * P4_ROWS + half * (P4_ROWS // 2)) for half in (0, 1) for j in range(PJ)]

    def keys(self, before_ref, wide_ref, sl, sub):
        older = self.get(before_ref, sl) if sub == 0 else self.get(wide_ref, sl, sub - 1)
        return jnp.concatenate([older, self.get(wide_ref, sl, sub)], axis=0)

    def index(self, idx, is_key):
        if self.name != "p4":
            return idx - BLOCK if is_key else idx
        within = jnp.bitwise_and(idx, BLOCK - 1)
        chunk = P4_ROWS // 2
        half = jnp.right_shift(within, HALF_BLOCK.bit_length() - 1)
        j = jnp.bitwise_and(jnp.right_shift(within, chunk.bit_length() - 1), PJ - 1)
        m = PJ * (chunk * half + jnp.bitwise_and(within, chunk - 1)) + j
        return m + BLOCK * (jnp.right_shift(idx, BLOCK.bit_length() - 1) - 1) if is_key else m

    def bias(self, has_before):
        shape = (2 * BLOCK, BLOCK)
        kidx = lax.broadcasted_iota(jnp.int32, shape, 0)
        qidx = lax.broadcasted_iota(jnp.int32, shape, 1)
        rel = self.index(qidx, False) - self.index(kidx, True)
        valid = (rel >= 0) & (rel <= WINDOW_KEYS)
        if has_before is not True:
            valid = valid & ((kidx >= BLOCK) | has_before)
        one = jnp.where(valid, 0.0, NEG)
        return jnp.concatenate([one, one], axis=1)

    def live_keys(self, half):
        return (0, 2 * BLOCK - HALF_BLOCK) if half == 0 else (HALF_BLOCK, 2 * BLOCK)

    def half_bias(self, has_before, half):
        r0, r1 = self.live_keys(half)
        shape = (r1 - r0, LANES)
        kidx = lax.broadcasted_iota(jnp.int32, shape, 0) + r0
        qidx = jnp.bitwise_and(lax.broadcasted_iota(jnp.int32, shape, 1), HALF_BLOCK - 1) + half * HALF_BLOCK
        rel = self.index(qidx, False) - self.index(kidx, True)
        valid = (rel >= 0) & (rel <= WINDOW_KEYS)
        if has_before is not True:
            valid = valid & ((kidx >= BLOCK) | has_before)
        return jnp.where(valid, 0.0, NEG)


def _head_masks():
    lane = lax.broadcasted_iota(jnp.int32, (BLOCK, LANES), 1)
    lo = lane < HEAD_DIM
    return lane, lo, jnp.where(lo, 1.0, 0.0).astype(BF16), jnp.where(lo, 0.0, 1.0).astype(BF16)


def attn_fwd(name, q, k, v, run):
    nat = name == "nat"
    seq = q.shape[0] if nat else q.shape[2] * PERM
    attn_w = q.shape[-1]
    mode = _Mode(name, seq)
    groups = _lane_groups(attn_w)
    first = run is None
    all_lanes = slice(0, LANES)

    def body(*refs):
        q_ref, kp_ref, kc_ref, vp_ref, vc_ref = refs[:5]
        if first:
            o_ref, l_ref = refs[5:]
        elif nat:
            oin_ref, lin_ref, o_ref, l_ref, ostage, lstage = refs[5:]
        else:
            oin_ref, lin_ref, o_ref, l_ref = refs[5:]
        n = pl.program_id(1)
        subs = range(mode.qb)
        halves = (0, 1)
        live = [mode.live_keys(x) for x in halves]
        always = [mode.half_bias(True, x) for x in halves]
        biases = [[mode.half_bias(n > 0, x) for x in halves]] + [always] * (mode.qb - 1)
        _, lo, m_lo, m_hi = _head_masks()
        head_row = lax.broadcasted_iota(jnp.int32, (BLOCK, LANES), 0)
        ones = jnp.ones((2 * BLOCK, LANES), BF16)
        hb = HALF_BLOCK
        lrows = [jnp.zeros((BLOCK, LANES), F32) for _ in subs]
        if not first:
            if nat:
                for g, sl in enumerate(groups):
                    _from_perm(oin_ref, sl, ostage, g)
                _from_perm(lin_ref, all_lanes, lstage, 0)
            wide_rows = lambda a, sub: a[sub * BLOCK:(sub + 1) * BLOCK]
            before = [jnp.transpose(wide_rows(_stage_get(lstage, 0), sub) if nat else mode.get(lin_ref, all_lanes, sub))
                      for sub in subs]

        def probs(sub, p, sl):
            q2 = mode.get(q_ref, sl, sub)
            kcat = mode.keys(kp_ref, kc_ref, sl, sub)
            vcat = mode.keys(vp_ref, vc_ref, sl, sub)
            q_lo, q_hi = q2 * m_lo, q2 * m_hi
            qq = jnp.concatenate([q_lo[:hb], q_hi[:hb], q_lo[hb:], q_hi[hb:]], axis=0)
            s_t = _nt(kcat, qq)
            columns, lses = [], []
            for x in halves:
                r0, r1 = live[x]
                s_x = s_t[r0:r1, x * LANES:(x + 1) * LANES] + biases[sub][x]
                m = jnp.max(s_x, axis=0, keepdims=True)
                pe = jnp.exp(s_x - m)
                lse = m + jnp.log(jnp.sum(pe, axis=0, keepdims=True))
                if not first:
                    was = jnp.concatenate([before[sub][2 * p:2 * p + 1, x * hb:(x + 1) * hb],
                                           before[sub][2 * p + 1:2 * p + 2, x * hb:(x + 1) * hb]], axis=1)
                    top = jnp.maximum(was, lse)
                    lse = top + jnp.log(jnp.exp(was - top) + jnp.exp(lse - top))
                    pe = pe * jnp.exp(m - lse)
                pieces = [pe.astype(BF16)]
                if r0 > 0:
                    pieces.insert(0, jnp.zeros((r0, LANES), BF16))
                if r1 < 2 * BLOCK:
                    pieces.append(jnp.zeros((2 * BLOCK - r1, LANES), BF16))
                columns.append(pieces[0] if len(pieces) == 1 else jnp.concatenate(pieces, axis=0))
                lses.append(lse)
            return jnp.concatenate([vcat, ones], axis=1), jnp.concatenate(columns, axis=1), lses

        def output(sub, p, sl, vext, pb, lses):
            o_ext = _tn(pb, vext)
            if first:
                o_new = o_ext[:, :LANES] / o_ext[:, LANES:]
            else:
                o_prev = wide_rows(_stage_get(ostage, p), sub) if nat else mode.get(oin_ref, sl, sub)
                same = jnp.concatenate([o_prev[:hb], o_prev[:hb], o_prev[hb:], o_prev[hb:]], axis=0)
                o_new = o_ext[:, :LANES] + same * (1.0 - o_ext[:, LANES:])
            head_lo = jnp.concatenate([o_new[:hb], o_new[2 * hb:3 * hb]], axis=0)
            head_hi = jnp.concatenate([o_new[hb:2 * hb], o_new[3 * hb:]], axis=0)
            mode.put(o_ref, sl, jnp.where(lo, head_lo, head_hi), sub)
            lse_lo = jnp.concatenate([lses[0][:, :hb], lses[1][:, :hb]], axis=1)
            lse_hi = jnp.concatenate([lses[0][:, hb:], lses[1][:, hb:]], axis=1)
            rows = jnp.where(head_row == 2 * p, lse_lo, lrows[sub])
            lrows[sub] = jnp.where(head_row == 2 * p + 1, lse_hi, rows)

        pending = None
        for sub in subs:
            for p, sl in enumerate(groups):
                nxt = probs(sub, p, sl)
                if pending is not None:
                    output(*pending)
                pending = (sub, p, sl, *nxt)
        output(*pending)
        for sub in subs:
            mode.put(l_ref, all_lanes, jnp.transpose(lrows[sub]), sub)

    ins = [q, k, k, v, v]
    specs = [mode.wide(attn_w), mode.block_before(attn_w), mode.wide(attn_w), mode.block_before(attn_w), mode.wide(attn_w)]
    scratch = []
    if not first:
        ins += list(run)
        if nat:
            rows_a = mode.qb * BLOCK // PERM
            specs += [pl.BlockSpec((PJ, PJ, rows_a, attn_w), lambda r, n: (0, 0, n, 0)),
                      pl.BlockSpec((PJ, PJ, rows_a, LANES), lambda r, n: (0, 0, n, 0))]
            scratch = [pltpu.VMEM(_stage_shape(len(groups), mode.qb * BLOCK), F32),
                       pltpu.VMEM(_stage_shape(1, mode.qb * BLOCK), F32)]
        else:
            specs += [mode.wide(attn_w), mode.wide(LANES)]
    if nat:
        out_shape = (jax.ShapeDtypeStruct((seq, attn_w), F32), jax.ShapeDtypeStruct((seq, LANES), F32))
    else:
        out_shape = (jax.ShapeDtypeStruct(_perm_shape(seq, attn_w), F32), jax.ShapeDtypeStruct(_perm_shape(seq, LANES), F32))
    return pl.pallas_call(
        body, name=f"attn_fwd_{name}", grid=(mode.residues, mode.steps),
        out_shape=out_shape, in_specs=specs, out_specs=(mode.wide(attn_w), mode.wide(LANES)),
        scratch_shapes=scratch,
        compiler_params=_params(("arbitrary", "arbitrary")),
    )(*ins)


def attn_bwd(name, q, k, v, d_o, lse, delta, run):
    nat = name == "nat"
    seq = q.shape[0] if nat else q.shape[2] * PERM
    attn_w = q.shape[-1]
    mode = _Mode(name, seq)
    steps, qb = mode.steps, mode.qb
    single = steps == 1
    groups = _lane_groups(attn_w)
    first = run is None
    all_lanes = slice(0, LANES)

    def body(*refs):
        q_ref, kp_ref, kc_ref, vp_ref, vc_ref, do_ref, lse_ref, dl_ref = refs[:8]
        if first:
            dq_ref, dk_ref, dv_ref, ck, cv = refs[8:]
        else:
            dqi_ref, dki_ref, dvi_ref, dq_ref, dk_ref, dv_ref, ck, cv = refs[8:]
        g = pl.program_id(1) if single else pl.program_id(0)
        n = g if single else lax.rem(g, steps)
        carries = ((ck, dk_ref, None if first else dki_ref), (cv, dv_ref, None if first else dvi_ref))

        def emit(out_ref, acc_ref, sl, sub, val):
            if acc_ref is not None:
                val = val + mode.get(acc_ref, sl, sub).astype(F32)
            mode.put(out_ref, sl, val, sub)

        if not single:
            @pl.when(g == 0)
            def _():
                ck[...] = jnp.zeros_like(ck)
                cv[...] = jnp.zeros_like(cv)

        @pl.when(g < total)
        def _():
            biases = [mode.bias(n > 0)] + [mode.bias(True)] * (qb - 1)
            _, lo, m_lo, m_hi = _head_masks()

            def scores(sub, p, sl, lse_t, dl_t):
                q2, do2 = mode.get(q_ref, sl, sub), mode.get(do_ref, sl, sub)
                kcat = mode.keys(kp_ref, kc_ref, sl, sub)
                vcat = mode.keys(vp_ref, vc_ref, sl, sub)
                qq = jnp.concatenate([q2 * m_lo, q2 * m_hi], axis=0)
                dd = jnp.concatenate([do2 * m_lo, do2 * m_hi], axis=0)
                h0 = 2 * p
                lse2 = jnp.concatenate([lse_t[h0:h0 + 1, :], lse_t[h0 + 1:h0 + 2, :]], axis=1)
                dl2 = jnp.concatenate([dl_t[h0:h0 + 1, :], dl_t[h0 + 1:h0 + 2, :]], axis=1)
                p_t = jnp.exp(_nt(kcat, qq) + (biases[sub] - lse2))
                ds_t = p_t * (_nt(vcat, dd) - dl2)
                return qq, dd, kcat, p_t.astype(BF16), ds_t.astype(BF16)

            def grads(sub, sl, qq, dd, kcat, pb, dsb):
                dqb = _tn(dsb, kcat)
                dq2 = jnp.where(lo, dqb[:BLOCK], dqb[BLOCK:]) * ATTN_SCALE
                if not first:
                    dq2 = dq2 + mode.get(dqi_ref, sl, sub).astype(F32)
                mode.put(dq_ref, sl, dq2, sub)
                for (carry, out_ref, acc_ref), lhs, rhs in zip(carries, (dsb, pb), (qq, dd)):
                    both = jnp.dot(lhs, rhs, preferred_element_type=F32)
                    if sub == 0:
                        if not single:
                            for s in range(qb - 1):
                                emit(out_ref, acc_ref, sl, s, carry[s, :, sl])
                            emit(out_ref, acc_ref, sl, qb - 1, carry[qb - 1, :, sl] + both[:BLOCK])
                        carry[0, :, sl] = both[BLOCK:]
                    else:
                        carry[sub - 1, :, sl] += both[:BLOCK]
                        carry[sub, :, sl] = both[BLOCK:]
                    if single and sub == qb - 1:
                        for s in range(qb):
                            emit(out_ref, acc_ref, sl, s, carry[s, :, sl])

            stats = [(jnp.transpose(mode.get(lse_ref, all_lanes, sub)),
                      jnp.transpose(mode.get(dl_ref, all_lanes, sub))) for sub in range(qb)]
            pending = None
            for p, sl in enumerate(groups):
                for sub in range(qb):
                    nxt = scores(sub, p, sl, *stats[sub])
                    if pending is not None:
                        grads(*pending)
                    pending = (sub, sl, *nxt)
            grads(*pending)

        if not single:
            @pl.when(g == total)
            def _():
                for carry, out_ref, acc_ref in carries:
                    for sl in groups:
                        for s in range(qb):
                            emit(out_ref, acc_ref, sl, s, carry[s, :, sl])

    total = mode.residues * steps
    if single:
        here = before = lambda r, n: (r, n)
    else:
        locate = lambda g: (g // steps, lax.rem(g, steps))
        here = lambda g: locate(jnp.minimum(g, total - 1))
        before = lambda g: locate(jnp.maximum(g - 1, 0))
    wide = lambda w: mode.wide(w, here)
    ins = [q, k, k, v, v, d_o, lse, delta]
    specs = [wide(attn_w), mode.block_before(attn_w, here), wide(attn_w), mode.block_before(attn_w, here), wide(attn_w),
             wide(attn_w), wide(LANES), wide(LANES)]
    if not first:
        ins += list(run)
        specs += [wide(attn_w), mode.wide(attn_w, before), mode.wide(attn_w, before)]
    shp = jax.ShapeDtypeStruct((seq, attn_w) if nat else _perm_shape(seq, attn_w), BF16)
    grid = (mode.residues, 1) if single else (total + 1,)
    return pl.pallas_call(
        body, name=f"attn_bwd_{name}", grid=grid,
        out_shape=(shp, shp, shp), in_specs=specs,
        out_specs=(wide(attn_w), mode.wide(attn_w, before), mode.wide(attn_w, before)),
        scratch_shapes=[pltpu.VMEM((qb, BLOCK, attn_w), F32), pltpu.VMEM((qb, BLOCK, attn_w), F32)],
        compiler_params=_params(("arbitrary",) * len(grid)),
    )(*ins)


def _shift_down(u, halo, k):
    rolled = pltpu.roll(u, k, 0)
    row = lax.broadcasted_iota(jnp.int32, halo.shape, 0)
    top = jnp.where(row < k, pltpu.roll(halo, k, 0), rolled[:SUBLANES])
    return jnp.concatenate([top, rolled[SUBLANES:]], axis=0)


def _shift_up(u, halo, k):
    rows = u.shape[0]
    rolled = pltpu.roll(u, rows - k, 0)
    row = lax.broadcasted_iota(jnp.int32, halo.shape, 0)
    bot = jnp.where(row >= SUBLANES - k, pltpu.roll(halo, SUBLANES - k, 0), rolled[rows - SUBLANES:])
    return jnp.concatenate([rolled[:rows - SUBLANES], bot], axis=0)


def tail(o, lse, ga, cz, x, tgt, w_out, g2, cw):
    seq, d_model = x.shape
    attn_w = o.shape[1]
    conv_w = cz.shape[1] // 4
    mix = attn_w + conv_w
    groups = _lane_groups(attn_w)
    tm = ROW_TILE
    nt = seq // tm
    hb = tm // SUBLANES

    def body(o_ref, l_ref, ga_ref, cz_ref, hz_ref, x_ref, t_ref, w_ref, g_ref, cw_ref,
             do_ref, dl_ref, dop_ref, dlp_ref, lp_ref, dga_ref, dcb_ref, dgc_ref, dcv_ref, e_ref,
             dw_ref, dg_ref, dcw_ref, loss_ref, stage):
        i = pl.program_id(0)

        @pl.when(i == 0)
        def _():
            dw_ref[...] = jnp.zeros_like(dw_ref)
            dg_ref[...] = jnp.zeros_like(dg_ref)
            dcw_ref[...] = jnp.zeros_like(dcw_ref)
            loss_ref[...] = jnp.zeros_like(loss_ref)

        u = cz_ref[:, 2 * conv_w:3 * conv_w] * cz_ref[:, 0:conv_w]
        uh = hz_ref[:, 2 * conv_w:3 * conv_w] * hz_ref[:, 0:conv_w]
        uh = jnp.where(i > 0, uh, 0.0)
        u1 = _shift_down(u, uh, 1)
        u2 = _shift_down(u, uh, 2)
        w0, w1, w2 = cw_ref[0:1, :], cw_ref[1:2, :], cw_ref[2:3, :]
        cvv = u2 * w0 + u1 * w1 + u * w2
        gv = g_ref[...]
        all_lanes = slice(0, LANES)

        def forward(rs):
            ov, gav = o_ref[rs, :], ga_ref[rs, :]
            sig_a = _sigmoid(gav)
            silu_a = gav * sig_a
            cb, gc = cz_ref[rs, conv_w:2 * conv_w], cz_ref[rs, 3 * conv_w:4 * conv_w]
            sig_c = _sigmoid(gc)
            silu_c = gc * sig_c
            bc = cb * cvv[rs]
            mixed = jnp.concatenate([ov * silu_a, bc * silu_c], axis=1).astype(BF16)
            yv = jnp.dot(mixed, w_ref[...], preferred_element_type=F32)
            return ov, gav, sig_a, silu_a, cb, gc, sig_c, silu_c, bc, mixed, yv

        def loss_and_dy(rs, mixed, yv):
            r2 = lax.rsqrt(jnp.mean(yv * yv, axis=-1, keepdims=True) + NORM_EPS)
            yhat = yv * r2
            diff = (x_ref[rs, :] + yhat * gv) - t_ref[rs, :]
            loss_ref[...] += _rowgroup_sum(diff * diff)
            ev = diff * (1.0 / d_model)
            e_ref[rs, :] = ev
            dg_ref[...] += _rowgroup_sum(ev * yhat)
            eg = ev * gv
            dy = (r2 * (eg - yhat * jnp.mean(eg * yhat, axis=-1, keepdims=True))).astype(BF16)
            dw_ref[...] += _tn(mixed, dy)
            return _nt(dy, w_ref[...])

        def backward(rs, ov, gav, sig_a, silu_a, cb, gc, sig_c, silu_c, bc, dm):
            rows = rs.stop - rs.start
            dma, dmc = dm[:, :attn_w], dm[:, attn_w:]
            dov = dma * silu_a
            do_ref[rs, :] = dov.astype(BF16)
            dga_ref[rs, :] = (dma * ov * (sig_a * (1.0 + gav * (1.0 - sig_a)))).astype(BF16)
            prod = dov * ov
            lane = lax.broadcasted_iota(jnp.int32, (rows, LANES), 1)
            lo = lane < HEAD_DIM
            dblk = jnp.zeros((rows, LANES), F32)
            for p, sl in enumerate(groups):
                pr = prod[:, sl]
                dblk = jnp.where(lane == 2 * p, jnp.sum(jnp.where(lo, pr, 0.0), axis=1, keepdims=True), dblk)
                dblk = jnp.where(lane == 2 * p + 1, jnp.sum(jnp.where(lo, 0.0, pr), axis=1, keepdims=True), dblk)
                _stage_put(stage, p, dov[:, sl], rs.start)
            dl_ref[rs, :] = dblk
            _stage_put(stage, len(groups), dblk, rs.start)
            _stage_put(stage, len(groups) + 1, l_ref[rs, :], rs.start)
            dsc = dmc * silu_c
            cv_rows = cvv[rs]
            dcb_ref[rs, :] = (dsc * cv_rows).astype(BF16)
            dgc_ref[rs, :] = (dmc * bc * (sig_c * (1.0 + gc * (1.0 - sig_c)))).astype(BF16)
            dcv = dsc * cb
            dcv_ref[rs, :] = dcv
            dcw_ref[0:SUBLANES, :] += _rowgroup_sum(dcv * u2[rs])
            dcw_ref[SUBLANES:2 * SUBLANES, :] += _rowgroup_sum(dcv * u1[rs])
            dcw_ref[2 * SUBLANES:3 * SUBLANES, :] += _rowgroup_sum(dcv * u[rs])

        halves = [slice(0, tm // 2), slice(tm // 2, tm)]
        fwd = [forward(rs) for rs in halves]
        dms = [loss_and_dy(rs, f[9], f[10]) for rs, f in zip(halves, fwd)]
        for rs, f, dm in zip(halves, fwd, dms):
            backward(rs, *f[:9], dm)
        for p, sl in enumerate(groups):
            _to_perm(stage, p, dop_ref, sl, BF16)
        _to_perm(stage, len(groups), dlp_ref, all_lanes, F32)
        _to_perm(stage, len(groups) + 1, lp_ref, all_lanes, F32)

    row = lambda n: pl.BlockSpec((tm, n), lambda i: (i, 0))
    whole = lambda a, b: pl.BlockSpec((a, b), lambda i: (0, 0))
    return pl.pallas_call(
        body, name="tail", grid=(nt,),
        out_shape=(jax.ShapeDtypeStruct((seq, attn_w), BF16), jax.ShapeDtypeStruct((seq, LANES), F32),
                   jax.ShapeDtypeStruct(_perm_shape(seq, attn_w), BF16), jax.ShapeDtypeStruct(_perm_shape(seq, LANES), F32),
                   jax.ShapeDtypeStruct(_perm_shape(seq, LANES), F32),
                   jax.ShapeDtypeStruct((seq, attn_w), BF16), jax.ShapeDtypeStruct((seq, conv_w), BF16),
                   jax.ShapeDtypeStruct((seq, conv_w), BF16), jax.ShapeDtypeStruct((seq, conv_w), F32),
                   jax.ShapeDtypeStruct((seq, d_model), F32), jax.ShapeDtypeStruct((mix, d_model), F32),
                   jax.ShapeDtypeStruct((SUBLANES, d_model), F32), jax.ShapeDtypeStruct((CONV_K * SUBLANES, conv_w), F32),
                   jax.ShapeDtypeStruct((SUBLANES, d_model), F32)),
        in_specs=[row(attn_w), row(LANES), row(attn_w), row(4 * conv_w),
                  pl.BlockSpec((SUBLANES, 4 * conv_w), lambda i: (jnp.maximum(i * hb - 1, 0), 0)),
                  row(d_model), row(d_model), _const_spec((mix, d_model)), _const_spec((1, d_model)),
                  _const_spec((SUBLANES, conv_w))],
        out_specs=(row(attn_w), row(LANES), _perm_tile_spec(attn_w, tm), _perm_tile_spec(LANES, tm), _perm_tile_spec(LANES, tm),
                   row(attn_w), row(conv_w), row(conv_w), row(conv_w), row(d_model),
                   whole(mix, d_model), whole(SUBLANES, d_model), whole(CONV_K * SUBLANES, conv_w),
                   whole(SUBLANES, d_model)),
        scratch_shapes=[pltpu.VMEM(_stage_shape(len(groups) + 2, tm), F32)],
        compiler_params=_params(("arbitrary",)),
    )(o, lse, ga, cz, cz, x, tgt, w_out, g2, cw)


def dz_dx(nat_grads, perm_grads, dga, dcb, dgc, dcv, cz, tables, x, g1, e, w_full, cw):
    seq, d_model = x.shape
    attn_w = dga.shape[1]
    conv_w = dcv.shape[1]
    width = w_full.shape[2]
    in_w = 4 * attn_w + 4 * conv_w
    groups = _lane_groups(attn_w)
    tm = ROW_TILE
    nt = seq // tm
    hb = tm // SUBLANES

    def body(dq_ref, dk_ref, dv_ref, dqp_ref, dkp_ref, dvp_ref, dga_ref, dcb_ref, dgc_ref, dcv_ref, nh_ref, ch_ref, cc_ref,
             cos_ref, s1_ref, s2_ref, x_ref, g_ref, e_ref, w_ref, cw_ref, gx_ref, dz_ref, dg_ref, stage):
        i = pl.program_id(0)

        @pl.when(i == 0)
        def _():
            dg_ref[...] = jnp.zeros_like(dg_ref)

        cos, s1, s2 = cos_ref[...], s1_ref[...], s2_ref[...]

        def qkv_columns(t, nat_ref, perm_ref):
            for g, sl in enumerate(groups):
                _from_perm(perm_ref, sl, stage, g)
            for g, sl in enumerate(groups):
                tot = nat_ref[:, sl].astype(F32) + _stage_get(stage, g)
                if t < 2:
                    tot = _rope_transposed(tot, cos, s1, s2)
                dz_ref[:, t * attn_w + g * LANES:t * attn_w + (g + 1) * LANES] = tot.astype(BF16)

        def dh_part(j):
            return _nt(dz_ref[:, j * width:(j + 1) * width], w_ref[j])

        dcv = dcv_ref[...]
        nh = jnp.where(i < nt - 1, nh_ref[...], 0.0)
        w0, w1, w2 = cw_ref[0:1, :], cw_ref[1:2, :], cw_ref[2:3, :]
        du = dcv * w2 + _shift_up(dcv, nh, 1) * w1 + _shift_up(dcv, nh, 2) * w0
        base = 4 * attn_w
        dz_ref[:, base:base + conv_w] = (du * cc_ref[...]).astype(BF16)
        dz_ref[:, base + conv_w:base + 2 * conv_w] = dcb_ref[...]
        dz_ref[:, base + 2 * conv_w:base + 3 * conv_w] = (du * ch_ref[...]).astype(BF16)
        dz_ref[:, base + 3 * conv_w:base + 4 * conv_w] = dgc_ref[...]
        dz_ref[:, 3 * attn_w:4 * attn_w] = dga_ref[...]
        ready = in_w
        dh = None
        for t, nat_ref, perm_ref in ((2, dv_ref, dvp_ref), (1, dk_ref, dkp_ref), (0, dq_ref, dqp_ref), (None, None, None)):
            lowest_open = 0 if t is None else (t + 1) * attn_w
            while ready - width >= lowest_open:
                ready -= width
                part = dh_part(ready // width)
                dh = part if dh is None else dh + part
            if t is not None:
                qkv_columns(t, nat_ref, perm_ref)
        xv = x_ref[...]
        r1 = lax.rsqrt(jnp.mean(xv * xv, axis=-1, keepdims=True) + NORM_EPS)
        xhat = xv * r1
        dg_ref[...] += _rowgroup_sum(dh * xhat)
        dhg = dh * g_ref[...]
        gx_ref[...] = r1 * (dhg - xhat * jnp.mean(dhg * xhat, axis=-1, keepdims=True)) + e_ref[...]

    row = lambda n: pl.BlockSpec((tm, n), lambda i: (i, 0))
    whole = lambda a, b: pl.BlockSpec((a, b), lambda i: (0, 0))
    pt = _perm_tile_spec(attn_w, tm)
    return pl.pallas_call(
        body, name="dz_dx", grid=(nt,),
        out_shape=(jax.ShapeDtypeStruct((seq, d_model), F32), jax.ShapeDtypeStruct((seq, in_w), BF16),
                   jax.ShapeDtypeStruct((SUBLANES, d_model), F32)),
        in_specs=[row(attn_w), row(attn_w), row(attn_w), pt, pt, pt, row(attn_w), row(conv_w), row(conv_w), row(conv_w),
                  pl.BlockSpec((SUBLANES, conv_w), lambda i: (jnp.minimum((i + 1) * hb, seq // SUBLANES - 1), 0)),
                  pl.BlockSpec((tm, conv_w), lambda i: (i, 0)), pl.BlockSpec((tm, conv_w), lambda i: (i, 2)),
                  row(LANES), row(LANES), row(LANES), row(d_model), _const_spec((1, d_model)), row(d_model),
                  _const_spec(w_full.shape), _const_spec((SUBLANES, conv_w))],
        out_specs=(row(d_model), row(in_w), whole(SUBLANES, d_model)),
        scratch_shapes=[pltpu.VMEM(_stage_shape(len(groups), tm), F32)],
        compiler_params=_params(("arbitrary",)),
    )(*nat_grads, *perm_grads, dga, dcb, dgc, dcv, dcv, cz, cz, *tables, x, g1, e, w_full, cw)


def dw_in_reduce(ht, dz, g_out, small):
    d_model, seq = ht.shape
    half = dz.shape[1] // N_DEV
    ts = min(DW_ROWS, seq)
    steps = seq // ts
    x, y, c = lax.axis_index("x"), lax.axis_index("y"), lax.axis_index("c")
    far_first = lambda x, y: [(1 - x, 1 - y), (1 - x, y), (x, 1 - y)]
    chips = jnp.stack([2 * px + py for px, py in far_first(x, y)] + [2 * x + y]).astype(jnp.int32)
    order = jnp.stack([2 * chips + (1 - c), 2 * chips + c], axis=1).reshape(N_DEV)

    def body(order_ref, ht_ref, dz_ref, go_ref, sm_ref, out_ref, ro_ref, rs_ref,
             acc, theirs, staged, contrib, resbuf, out_sem, sa, ra, sb, rb, sc, rc,
             o_mine, o_theirs, o_staged, o_contrib, o_res, sbuf, o_load, osa, ora, osb, orb, osc, orc, ss, rs):
        del order_ref
        p, s = pl.program_id(0), pl.program_id(1)
        x, y, c = lax.axis_index("x"), lax.axis_index("y"), lax.axis_index("c")
        me = 2 * x + y
        sib = (x, y, 1 - c)
        peers = far_first(x, y)
        slot = p % 2

        flips = [(fx, fy, fc) for fx in (0, 1) for fy in (0, 1) for fc in (0, 1)][1:]
        my8 = 4 * x + 2 * y + c
        chip_ids = [2 * px + py for px, py in peers] + [me]

        def small_copy(k, slot8, to):
            return pltpu.make_async_remote_copy(src_ref=sm_ref, dst_ref=sbuf.at[slot8], send_sem=ss.at[k], recv_sem=rs.at[k],
                                                device_id=to, device_id_type=MESH)

        def small_peer(k):
            fx, fy, fc = flips[k]
            return _flip(x, fx), _flip(y, fy), _flip(c, fc)

        def oa_copy(pos):
            j = chip_ids[pos]
            return pltpu.make_async_remote_copy(src_ref=go_ref.at[j, 1 - c], dst_ref=o_theirs.at[j], send_sem=osa.at[pos],
                                                recv_sem=ora.at[pos], device_id=sib, device_id_type=MESH)

        def o_load_copy(pos):
            j = chip_ids[pos]
            return pltpu.make_async_copy(go_ref.at[j, c], o_mine.at[j], o_load.at[pos])

        def ob_copy(k, piece, slot4):
            px, py = peers[k]
            return pltpu.make_async_remote_copy(src_ref=o_staged.at[piece], dst_ref=o_contrib.at[slot4], send_sem=osb.at[k],
                                                recv_sem=orb.at[k], device_id=(px, py, c), device_id_type=MESH)

        def oc_copy(which):
            return pltpu.make_async_remote_copy(src_ref=o_res.at[which], dst_ref=o_res.at[which], send_sem=osc, recv_sem=orc,
                                                device_id=sib, device_id_type=MESH)

        @pl.when((p == 0) & (s == 0))
        def _():
            sbuf[my8] = sm_ref[...]
            for k in range(N_DEV - 1):
                small_copy(k, my8, small_peer(k)).start()
            for pos in range(N_CHIPS):
                o_load_copy(pos).start()
                oa_copy(pos).start()

        @pl.when((p == 1) & (s == steps - 1))
        def _():
            for pos in range(N_CHIPS):
                j = chip_ids[pos]
                o_load_copy(pos).wait()
                oa_copy(pos).wait_recv()
                if pos < N_CHIPS - 1:
                    o_staged[j] = (o_mine[j] + o_theirs[j]).astype(BF16)
                    ob_copy(pos, j, me).start()
                else:
                    o_mine[j] = o_mine[j] + o_theirs[j]
                    o_contrib[j] = o_mine[j].astype(BF16)

        @pl.when((p == 4) & (s == steps - 1))
        def _():
            for k in range(N_CHIPS - 1):
                ob_copy(k, me, chip_ids[k]).wait_recv()
            own = o_mine[me]
            term = lambda j: jnp.where(me == j, own, o_contrib[j].astype(F32))
            o_res[c] = ((term(0) + term(1)) + term(2)) + term(3)
            oc_copy(c).start()

        def a_copy(k):
            return pltpu.make_async_remote_copy(src_ref=acc.at[0], dst_ref=theirs.at[k], send_sem=sa.at[k], recv_sem=ra.at[k],
                                                device_id=sib, device_id_type=MESH)

        def b_copy(k):
            px, py = peers[k]
            return pltpu.make_async_remote_copy(src_ref=staged.at[k], dst_ref=contrib.at[k], send_sem=sb.at[k], recv_sem=rb.at[k],
                                                device_id=(px, py, c), device_id_type=MESH)

        def c_copy(which):
            return pltpu.make_async_remote_copy(src_ref=resbuf.at[which], dst_ref=resbuf.at[which], send_sem=sc, recv_sem=rc,
                                                device_id=sib, device_id_type=MESH)

        @pl.when(s == 0)
        def _():
            for k in range(N_CHIPS - 1):
                @pl.when(p == 2 * k + 2)
                def _():
                    a_copy(k).wait_send()
            acc[slot] = jnp.dot(ht_ref[...], dz_ref[...], preferred_element_type=F32)

        @pl.when(s > 0)
        def _():
            acc[slot] += jnp.dot(ht_ref[...], dz_ref[...], preferred_element_type=F32)

        @pl.when(s == steps - 1)
        def _():
            for k in range(N_CHIPS):
                @pl.when(p == 2 * k)
                def _():
                    a_copy(k).start()
            for k in range(N_CHIPS - 1):
                @pl.when(p == 2 * k + 1)
                def _():
                    a_copy(k).wait_recv()
                    staged[k] = (acc[1] + theirs[k]).astype(BF16)
                    b_copy(k).start()

            @pl.when(p == N_DEV - 1)
            def _():
                a_copy(N_CHIPS - 1).wait_recv()
                tot = acc[1] + theirs[N_CHIPS - 1]
                for k in range(N_CHIPS - 1):
                    b_copy(k).wait_recv()
                    tot = tot + contrib[k].astype(F32)
                resbuf[c] = tot
                c_copy(c).start()
                c_copy(1 - c).wait_recv()
                done = pltpu.make_async_copy(resbuf, out_ref, out_sem)
                done.start()
                oc_copy(1 - c).wait_recv()
                ro_ref[...] = o_res[...]
                for k in range(N_DEV - 1):
                    px, py, pc = small_peer(k)
                    small_copy(k, 4 * px + 2 * py + pc, (px, py, pc)).wait_recv()
                tot8 = sbuf[0]
                for d in range(1, N_DEV):
                    tot8 = tot8 + sbuf[d]
                rs_ref[...] = tot8
                a_copy(N_CHIPS - 1).wait_send()
                for k in range(N_CHIPS - 1):
                    b_copy(k).wait_send()
                    ob_copy(k, chip_ids[k], me).wait_send()
                c_copy(c).wait_send()
                oc_copy(c).wait_send()
                for pos in range(N_CHIPS):
                    oa_copy(pos).wait_send()
                for k in range(N_DEV - 1):
                    small_copy(k, my8, small_peer(k)).wait_send()
                done.wait()

    dma = pltpu.SemaphoreType.DMA
    o_shape = g_out.shape[1:]
    go = g_out.reshape(N_CHIPS, 2, *o_shape)
    const = lambda shape: pl.BlockSpec(shape, lambda p, s, order_ref: (0,) * len(shape))
    grid_spec = pltpu.PrefetchScalarGridSpec(
        num_scalar_prefetch=1, grid=(N_DEV, steps),
        in_specs=[pl.BlockSpec((d_model, ts), lambda p, s, order_ref: (0, s)),
                  pl.BlockSpec((ts, half), lambda p, s, order_ref: (s, order_ref[p])),
                  pl.BlockSpec(memory_space=pl.ANY), const(small.shape)],
        out_specs=(pl.BlockSpec(memory_space=pl.ANY), const((2, *o_shape)), const(small.shape)),
        scratch_shapes=[pltpu.VMEM((2, d_model, half), F32), pltpu.VMEM((N_CHIPS, d_model, half), F32),
                        pltpu.VMEM((N_CHIPS - 1, d_model, half), BF16), pltpu.VMEM((N_CHIPS - 1, d_model, half), BF16),
                        pltpu.VMEM((2, d_model, half), F32), dma,
                        dma((N_CHIPS,)), dma((N_CHIPS,)), dma((N_CHIPS - 1,)), dma((N_CHIPS - 1,)), dma, dma,
                        pltpu.VMEM((N_CHIPS, *o_shape), F32), pltpu.VMEM((N_CHIPS, *o_shape), F32),
                        pltpu.VMEM((N_CHIPS, *o_shape), BF16), pltpu.VMEM((N_CHIPS, *o_shape), BF16),
                        pltpu.VMEM((2, *o_shape), F32), pltpu.VMEM((N_DEV, *small.shape), F32),
                        dma((N_CHIPS,)), dma((N_CHIPS,)), dma((N_CHIPS,)), dma((N_CHIPS - 1,)), dma((N_CHIPS - 1,)), dma, dma,
                        dma((N_DEV - 1,)), dma((N_DEV - 1,))])
    return pl.pallas_call(
        body, name="dw_in_reduce", grid_spec=grid_spec,
        out_shape=(jax.ShapeDtypeStruct((2, d_model, half), F32), jax.ShapeDtypeStruct((2, *o_shape), F32),
                   jax.ShapeDtypeStruct(small.shape, F32)),
        compiler_params=_params(("arbitrary", "arbitrary")),
    )(order, ht, dz, go, small)


def _adam_math(w, g, m, v):
    m = ADAM_B1 * m + (1.0 - ADAM_B1) * g
    v = ADAM_B2 * v + (1.0 - ADAM_B2) * (g * g)
    m_hat = m / (1.0 - ADAM_B1 ** ADAM_STEP)
    v_hat = v / (1.0 - ADAM_B2 ** ADAM_STEP)
    delta = -ADAM_LR * (m_hat / (jnp.sqrt(v_hat) + ADAM_EPS) + ADAM_WD * w)
    return delta, m, v


def adam_shard(name, w, g2, m, v, block, grid, w_map, g_map):
    def body(w_ref, g_ref, m_ref, v_ref, go_ref, d_ref, mo_ref, vo_ref):
        g = g_ref[0]
        delta, mn, vn = _adam_math(w_ref[...], g, m_ref[...], v_ref[...])
        go_ref[...] = g
        d_ref[...] = delta
        mo_ref[...] = mn
        vo_ref[...] = vn

    ws = pl.BlockSpec(block, w_map)
    shp = jax.ShapeDtypeStruct(w.shape, F32)
    return pl.pallas_call(
        body, name=name, grid=grid, out_shape=(shp, shp, shp, shp),
        in_specs=[ws, pl.BlockSpec((1, *block), g_map), ws, ws], out_specs=(ws, ws, ws, ws),
        compiler_params=_params(("arbitrary",) * len(grid)),
    )(w, g2, m, v)


def adam_small(ws, gs, ms, vs):
    n = len(ws)

    def body(*refs):
        ins, outs = refs[:4 * n], refs[4 * n:]
        for t in range(n):
            delta, mn, vn = _adam_math(ins[t][...], ins[n + t][...], ins[2 * n + t][...], ins[3 * n + t][...])
            outs[3 * t][...] = delta
            outs[3 * t + 1][...] = mn
            outs[3 * t + 2][...] = vn

    vm = pl.BlockSpec(memory_space=pltpu.VMEM)
    outs = pl.pallas_call(
        body, name="adam_small",
        out_shape=tuple(jax.ShapeDtypeStruct(w.shape, F32) for w in ws for _ in range(3)),
        in_specs=[vm] * (4 * n), out_specs=tuple([vm] * (3 * n)),
        compiler_params=_params(),
    )(*ws, *gs, *ms, *vs)
    return [outs[3 * t:3 * t + 3] for t in range(n)]


def kernel(x, norm_pre_g, w_in, conv_w, w_out, norm_post_g, loss_target, m_norm_pre_g, m_w_in, m_conv_w, m_w_out, m_norm_post_g, v_norm_pre_g, v_w_in, v_conv_w, v_w_out, v_norm_post_g):
    _, seq, d_model = x.shape
    width = w_in.shape[1]
    conv_q = conv_w.shape[1]
    conv_width = N_CHIPS * conv_q
    attn_width = d_model - conv_width
    xs, tg = x[0], loss_target[0]
    g1, g2 = norm_pre_g.reshape(1, d_model), norm_post_g.reshape(1, d_model)

    w_full, wout_full, cw_full, *tables = gather_weights(w_in, w_out, conv_w, seq)
    wout2 = wout_full.reshape(attn_width + conv_width, d_model)
    cw = jnp.zeros((SUBLANES, conv_width), F32).at[:CONV_K].set(
        cw_full[:, :CONV_K, :conv_q].transpose(1, 0, 2).reshape(CONV_K, conv_width))

    ht, q, k, v, qp, kp, vp, ga, cz = inproj(xs, g1, w_full, tables, attn_width, conv_width)
    run = attn_fwd("p4", qp, kp, vp, None)
    run = attn_fwd("p16", qp, kp, vp, run)
    o, lse = attn_fwd("nat", q, k, v, run)
    (d_o, delta, d_op, delta_p, lse_p, dga, dcb, dgc, dcv, e, dwout, dg2, dcw, loss_acc) = tail(
        o, lse, ga, cz, xs, tg, wout2, g2, cw)
    nat_grads = attn_bwd("nat", q, k, v, d_o, lse, delta, None)
    perm_grads = attn_bwd("p4", qp, kp, vp, d_op, lse_p, delta_p, None)
    perm_grads = attn_bwd("p16", qp, kp, vp, d_op, lse_p, delta_p, perm_grads)
    grad_x, dz, dg1 = dz_dx(nat_grads, perm_grads, dga, dcb, dgc, dcv, cz, tables, xs, g1, e, w_full, cw)

    small = jnp.zeros((SUBLANES, d_model), F32)
    small = small.at[0].set(dg1.sum(axis=0)).at[1].set(dg2.sum(axis=0))
    small = small.at[2:2 + CONV_K, :conv_width].set(dcw.reshape(CONV_K, SUBLANES, conv_width).sum(axis=1))
    small = small.at[2 + CONV_K, 0].set(jnp.sum(loss_acc))
    rin, rout, rsmall = dw_in_reduce(ht, dz, dwout.reshape(N_DEV, -1, d_model), small)

    half = width // 2
    tr = min(ADAM_ROWS, d_model)
    gw_in, d_in, m_in, v_in = adam_shard(
        "adam_w_in", w_in, rin, m_w_in, v_w_in, (tr, half), (2, d_model // tr),
        lambda hf, i: (i, hf), lambda hf, i: (hf, i, 0))
    gw_out = rout.reshape(w_out.shape)

    chip = 2 * lax.axis_index("x") + lax.axis_index("y")
    g_pre, g_post = rsmall[0:1], rsmall[1:2]
    g_conv = lax.dynamic_slice(rsmall[2:2 + CONV_K, :conv_width], (0, chip * conv_q), (CONV_K, conv_q))
    (d_pre, m_pre, v_pre), (d_post, m_post, v_post), (d_cv, m_cv, v_cv), (d_out, m_out, v_out) = adam_small(
        [g1, g2, conv_w, w_out], [g_pre, g_post, g_conv, gw_out],
        [m_norm_pre_g.reshape(1, d_model), m_norm_post_g.reshape(1, d_model), m_conv_w, m_w_out],
        [v_norm_pre_g.reshape(1, d_model), v_norm_post_g.reshape(1, d_model), v_conv_w, v_w_out])

    loss = 0.5 * rsmall[2 + CONV_K, 0] / d_model
    vec = lambda a: a.reshape(d_model)
    return (loss, grad_x.reshape(1, seq, d_model),
            vec(g_pre), gw_in, g_conv, gw_out, vec(g_post),
            vec(d_pre), d_in, d_cv, d_out, vec(d_post),
            vec(m_pre), m_in, m_cv, m_out, vec(m_post),
            vec(v_pre), v_in, v_cv, v_out, vec(v_post))
```

```python
import jax
import jax.numpy as jnp
from jax import lax
from jax.experimental import pallas as pl
from jax.experimental.pallas import tpu as pltpu

HEAD_DIM = 64
LANES = 128
SUBLANES = 8
BLOCK = 128
HALF_BLOCK = BLOCK // 2
WINDOW_KEYS = 128
PERM = 16
PJ = 4
P4_ROWS = BLOCK // PJ
MAX_QUERY_BLOCKS = 8
ROW_TILE = 512
DW_ROWS = 4096
ADAM_ROWS = 1024
CONV_K = 3
ROPE_THETA = 10000.0
NORM_EPS = 1e-6
ATTN_SCALE = HEAD_DIM ** -0.5
NEG = -1e30
N_CHIPS = 4
N_DEV = 8
MESH = pl.DeviceIdType.MESH
ADAM_LR = 0.001
ADAM_B1 = 0.9
ADAM_B2 = 0.999
ADAM_EPS = 1e-08
ADAM_WD = 0.01
ADAM_STEP = 10
VMEM_LIMIT = 63 * 1024 * 1024

F32 = jnp.float32
BF16 = jnp.bfloat16


def _params(sem=None, **kw):
    return pltpu.CompilerParams(dimension_semantics=sem, vmem_limit_bytes=VMEM_LIMIT, **kw)


def _const_spec(shape):
    return pl.BlockSpec(shape, lambda *_: (0,) * len(shape), pipeline_mode=pl.Buffered(1))


def _sigmoid(z):
    return 1.0 / (1.0 + jnp.exp(-z))


def _rowgroup_sum(a):
    rows, n = a.shape
    return a.reshape(rows // SUBLANES, SUBLANES, n).sum(axis=0)


def _nt(a, b):
    return lax.dot_general(a, b, (((1,), (1,)), ((), ())), preferred_element_type=F32)


def _tn(a, b):
    return lax.dot_general(a, b, (((0,), (0,)), ((), ())), preferred_element_type=F32)


def _col_pieces(a, b, width):
    out = []
    while a < b:
        j = a // width
        e = min(b, (j + 1) * width)
        out.append((j, a - j * width, e - j * width))
        a = e
    return out


def _lane_groups(width):
    return [slice(g * LANES, (g + 1) * LANES) for g in range(width // LANES)]


def _perm_shape(seq, width):
    return (PJ, PJ, seq // PERM, width)


def _perm_tile_spec(width, tm):
    return pl.BlockSpec((PJ, PJ, tm // PERM, width), lambda i: (0, 0, i, 0))


STAGE_PITCH = 24


def _stage_shape(groups, rows):
    return (groups, rows // PERM * STAGE_PITCH, LANES)


def _stage_put(stage, g, val, row0=0):
    for a in range(val.shape[0] // PERM):
        at = (row0 // PERM + a) * STAGE_PITCH
        stage[g, at:at + PERM, :] = val[a * PERM:(a + 1) * PERM]


def _stage_get(stage, g):
    return jnp.concatenate([stage[g, a * STAGE_PITCH:a * STAGE_PITCH + PERM, :]
                            for a in range(stage.shape[1] // STAGE_PITCH)], axis=0)


def _to_perm(stage, g, dst_ref, sl, dtype):
    rows = stage.shape[1] // STAGE_PITCH
    for b in range(PERM):
        dst_ref[b // PJ, b % PJ, :, sl] = stage[g, pl.ds(b, rows, stride=STAGE_PITCH), :].astype(dtype)


def _from_perm(src_ref, sl, stage, g):
    rows = stage.shape[1] // STAGE_PITCH
    for b in range(PERM):
        stage[g, pl.ds(b, rows, stride=STAGE_PITCH), :] = src_ref[b // PJ, b % PJ, :, sl].astype(F32)


def _flip(a, f):
    return 1 - a if f else a


def gather_weights(w_in, w_out, conv_w, seq):
    d_model, width = w_in.shape
    rows = w_out.shape[0]
    cw = jnp.zeros((SUBLANES, LANES), F32).at[:CONV_K, :conv_w.shape[1]].set(conv_w)
    half_dim = HEAD_DIM // 2
    inv_freq = ROPE_THETA ** (-jnp.arange(half_dim, dtype=F32) * 2.0 / HEAD_DIM)
    inv_freq = jnp.tile(inv_freq, LANES // half_dim).reshape(1, LANES)
    chunk = min(ROW_TILE, seq)

    def body(win_ref, wout_ref, cw_ref, freq_ref, winf_ref, woutf_ref, cwf_ref, cos_ref, s1_ref, s2_ref,
             st_in, st_out, near_send, near_recv, far_send, far_recv, cw_send, cw_recv, d2d_send, d2d_recv):
        x, y, c = lax.axis_index("x"), lax.axis_index("y"), lax.axis_index("c")
        me = 2 * x + y
        sib = (x, y, 1 - c)
        st_in[...] = win_ref[...].astype(BF16)
        st_out[...] = wout_ref[...].astype(BF16)
        winf_ref[me] = st_in[...]
        woutf_ref[me] = st_out[...]
        cwf_ref[me] = cw_ref[...]
        stages = (st_in, st_out)
        fulls = (winf_ref, woutf_ref)
        halves = (d_model // 2, rows // 2)

        def part(t, core, q=None):
            size = halves[t] if q is None else halves[t] // 2
            start = core * halves[t] if q is None else core * halves[t] + q * size
            return pl.ds(pl.multiple_of(start, size), size)

        near = [(1 - x, y), (x, 1 - y)]
        far = (1 - x, 1 - y)
        chip = lambda px, py: 2 * px + py

        def direct(k, t, q, slot, to):
            src = stages[t].at[part(t, c, q)]
            return pltpu.make_async_remote_copy(src_ref=src, dst_ref=fulls[t].at[slot, part(t, c, q)], send_sem=near_send.at[k, t, q],
                                                recv_sem=near_recv.at[k, t, q], device_id=to, device_id_type=MESH)

        def passed_on(k, t, slot, to):
            ref = fulls[t].at[slot, part(t, c, k)]
            return pltpu.make_async_remote_copy(src_ref=ref, dst_ref=ref, send_sem=far_send.at[k, t], recv_sem=far_recv.at[k, t],
                                                device_id=to, device_id_type=MESH)

        def conv_copy(k, slot, to):
            return pltpu.make_async_remote_copy(src_ref=cw_ref, dst_ref=cwf_ref.at[slot], send_sem=cw_send.at[k], recv_sem=cw_recv.at[k],
                                                device_id=to, device_id_type=MESH)

        def d2d(k, t, slot, core):
            ref = fulls[t].at[slot, part(t, core)]
            return pltpu.make_async_remote_copy(src_ref=ref, dst_ref=ref, send_sem=d2d_send.at[k, t], recv_sem=d2d_recv.at[k, t],
                                                device_id=sib, device_id_type=MESH)

        sends = []

        def go(cp):
            cp.start()
            sends.append(cp)

        for q_first in (0, 1):
            for k, (px, py) in enumerate(near):
                for t in range(2):
                    go(direct(k, t, k if q_first == 0 else 1 - k, me, (px, py, c)))
        for k, (px, py) in enumerate(near + [far]):
            go(conv_copy(k, me, (px, py, c)))
        for k, (px, py) in enumerate(near):
            other = near[1 - k]
            for t in range(2):
                direct(k, t, k, chip(px, py), (px, py, c)).wait_recv()
                go(passed_on(k, t, chip(px, py), (*other, c)))

        first_half = lax.broadcasted_iota(jnp.int32, (chunk, LANES), 1) % HEAD_DIM < half_dim
        row = lax.broadcasted_iota(jnp.int32, (chunk, LANES), 0)

        def table_rows(i, carry):
            at = pl.multiple_of(i * chunk, chunk)
            ang = (row + at).astype(F32) * freq_ref[...]
            sin = jnp.sin(ang)
            cos_ref[pl.ds(at, chunk), :] = jnp.cos(ang)
            s1_ref[pl.ds(at, chunk), :] = jnp.where(first_half, -sin, 0.0)
            s2_ref[pl.ds(at, chunk), :] = jnp.where(first_half, 0.0, sin)
            return carry

        lax.fori_loop(0, seq // chunk, table_rows, 0)

        for k, (px, py) in enumerate(near):
            for t in range(2):
                direct(k, t, 1 - k, chip(px, py), (px, py, c)).wait_recv()
                go(d2d(k, t, chip(px, py), c))
        for t in range(2):
            for k, (px, py) in enumerate(near):
                passed_on(k, t, chip(*far), (px, py, c)).wait_recv()
            go(d2d(2, t, chip(*far), c))
        for k, (px, py) in enumerate(near + [far]):
            conv_copy(k, chip(px, py), (px, py, c)).wait_recv()
            for t in range(2):
                d2d(k, t, chip(px, py), 1 - c).wait_recv()
        for cp in sends:
            cp.wait_send()

    vm = pl.BlockSpec(memory_space=pltpu.VMEM)
    dma = pltpu.SemaphoreType.DMA
    return pl.pallas_call(
        body, name="gather_weights",
        out_shape=(jax.ShapeDtypeStruct((N_CHIPS, d_model, width), BF16),
                   jax.ShapeDtypeStruct((N_CHIPS, rows, d_model), BF16),
                   jax.ShapeDtypeStruct((N_CHIPS, SUBLANES, LANES), F32),
                   *[jax.ShapeDtypeStruct((seq, LANES), F32)] * 3),
        in_specs=[vm, vm, vm, vm], out_specs=(vm,) * 6,
        scratch_shapes=[pltpu.VMEM((d_model, width), BF16), pltpu.VMEM((rows, d_model), BF16),
                        dma((2, 2, 2)), dma((2, 2, 2)), dma((2, 2)), dma((2, 2)), dma((3,)), dma((3,)),
                        dma((3, 2)), dma((3, 2))],
        compiler_params=_params(),
    )(w_in, w_out, cw, inv_freq)


def _rope(t, cos, s1, s2):
    return t * cos + pltpu.roll(t, LANES - HEAD_DIM // 2, 1) * s1 + pltpu.roll(t, HEAD_DIM // 2, 1) * s2


def _rope_transposed(g, cos, s1, s2):
    return g * cos + pltpu.roll(g * s1, HEAD_DIM // 2, 1) + pltpu.roll(g * s2, LANES - HEAD_DIM // 2, 1)


def inproj(x, g1, w_full, tables, attn_w, conv_w):
    seq, d_model = x.shape
    width = w_full.shape[2]
    tm = ROW_TILE
    groups = _lane_groups(attn_w)

    def body(x_ref, g_ref, w_ref, cos_ref, s1_ref, s2_ref,
             ht_ref, q_ref, k_ref, v_ref, qp_ref, kp_ref, vp_ref, ga_ref, cz_ref, stage):
        xv = x_ref[...]
        hb = ((xv * lax.rsqrt(jnp.mean(xv * xv, axis=-1, keepdims=True) + NORM_EPS)) * g_ref[...]).astype(BF16)
        ht_ref[...] = jnp.transpose(hb)
        cos, s1, s2 = cos_ref[...], s1_ref[...], s2_ref[...]

        def proj(a, b):
            parts = [jnp.dot(hb, w_ref[j, :, lo:hi], preferred_element_type=F32) for j, lo, hi in _col_pieces(a, b, width)]
            return parts[0] if len(parts) == 1 else jnp.concatenate(parts, axis=1)

        def emit(z, nat_ref, perm_ref, fn):
            for g, sl in enumerate(groups):
                val = fn(z[:, sl])
                nat_ref[:, sl] = val.astype(BF16)
                _stage_put(stage, g, val)
            for g, sl in enumerate(groups):
                _to_perm(stage, g, perm_ref, sl, BF16)

        emit(proj(0, attn_w), q_ref, qp_ref, lambda t: _rope(t, cos, s1, s2) * ATTN_SCALE)
        emit(proj(attn_w, 2 * attn_w), k_ref, kp_ref, lambda t: _rope(t, cos, s1, s2))
        emit(proj(2 * attn_w, 3 * attn_w), v_ref, vp_ref, lambda t: t)
        ga_ref[...] = proj(3 * attn_w, 4 * attn_w)
        cz_ref[...] = proj(4 * attn_w, 4 * attn_w + 4 * conv_w)

    row = lambda n: pl.BlockSpec((tm, n), lambda i: (i, 0))
    nat = jax.ShapeDtypeStruct((seq, attn_w), BF16)
    perm = jax.ShapeDtypeStruct(_perm_shape(seq, attn_w), BF16)
    return pl.pallas_call(
        body, name="inproj", grid=(seq // tm,),
        out_shape=(jax.ShapeDtypeStruct((d_model, seq), BF16), nat, nat, nat, perm, perm, perm,
                   jax.ShapeDtypeStruct((seq, attn_w), F32), jax.ShapeDtypeStruct((seq, 4 * conv_w), F32)),
        in_specs=[row(d_model), _const_spec((1, d_model)), _const_spec(w_full.shape), row(LANES), row(LANES), row(LANES)],
        out_specs=(pl.BlockSpec((d_model, tm), lambda i: (0, i)), row(attn_w), row(attn_w), row(attn_w),
                   _perm_tile_spec(attn_w, tm), _perm_tile_spec(attn_w, tm), _perm_tile_spec(attn_w, tm),
                   row(attn_w), row(4 * conv_w)),
        scratch_shapes=[pltpu.VMEM(_stage_shape(len(groups), tm), F32)],
        compiler_params=_params(("arbitrary",)),
    )(x, g1, w_full, *tables)


class _Mode:
    def __init__(self, name, seq):
        self.name = name
        if name == "nat":
            self.residues, blocks = 1, seq // BLOCK
        elif name == "p16":
            self.residues, blocks = PERM, seq // PERM // BLOCK
        else:
            self.residues, blocks = PJ, seq // PERM // P4_ROWS
        self.qb = max(d for d in range(1, MAX_QUERY_BLOCKS + 1) if blocks % d == 0)
        self.steps = blocks // self.qb

    def _spec(self, blocks, width, at):
        if self.name == "nat":
            return pl.BlockSpec((blocks * BLOCK, width), lambda *g: (at(*g)[1], 0))
        if self.name == "p16":
            return pl.BlockSpec((1, 1, blocks * BLOCK, width), lambda *g: (at(*g)[0] // PJ, at(*g)[0] % PJ, at(*g)[1], 0))
        return pl.BlockSpec((PJ, 1, blocks * P4_ROWS, width), lambda *g: (0, at(*g)[0], at(*g)[1], 0))

    def wide(self, width, where=lambda r, n: (r, n)):
        return self._spec(self.qb, width, where)

    def block_before(self, width, where=lambda r, n: (r, n)):
        return self._spec(1, width, lambda *g: (where(*g)[0], jnp.maximum(self.qb * where(*g)[1] - 1, 0)))

    def get(self, ref, sl, sub=0):
        if self.name == "nat":
            return ref[sub * BLOCK:(sub + 1) * BLOCK, sl]
        if self.name == "p16":
            return ref[0, 0, sub * BLOCK:(sub + 1) * BLOCK, sl]
        return jnp.concatenate([ref[j, 0, at:at + P4_ROWS // 2, sl] for j, at in self._p4_chunks(sub)], axis=0)

    def put(self, ref, sl, val, sub=0):
        val = val.astype(ref.dtype)
        if self.name == "nat":
            ref[sub * BLOCK:(sub + 1) * BLOCK, sl] = val
        elif self.name == "p16":
            ref[0, 0, sub * BLOCK:(sub + 1) * BLOCK, sl] = val
        else:
            for i, (j, at) in enumerate(self._p4_chunks(sub)):
                ref[j, 0, at:at + P4_ROWS // 2, sl] = val[i * (P4_ROWS // 2):(i + 1) * (P4_ROWS // 2)]

    @staticmethod
    def _p4_chunks(sub):
        return [(j, sub * P4_ROWS + half * (P4_ROWS // 2)) for half in (0, 1) for j in range(PJ)]

    def keys(self, before_ref, wide_ref, sl, sub):
        older = self.get(before_ref, sl) if sub == 0 else self.get(wide_ref, sl, sub - 1)
        return jnp.concatenate([older, self.get(wide_ref, sl, sub)], axis=0)

    def index(self, idx, is_key):
        if self.name != "p4":
            return idx - BLOCK if is_key else idx
        within = jnp.bitwise_and(idx, BLOCK - 1)
        chunk = P4_ROWS // 2
        half = jnp.right_shift(within, HALF_BLOCK.bit_length() - 1)
        j = jnp.bitwise_and(jnp.right_shift(within, chunk.bit_length() - 1), PJ - 1)
        m = PJ * (chunk * half + jnp.bitwise_and(within, chunk - 1)) + j
        return m + BLOCK * (jnp.right_shift(idx, BLOCK.bit_length() - 1) - 1) if is_key else m

    def bias(self, has_before):
        shape = (2 * BLOCK, BLOCK)
        kidx = lax.broadcasted_iota(jnp.int32, shape, 0)
        qidx = lax.broadcasted_iota(jnp.int32, shape, 1)
        rel = self.index(qidx, False) - self.index(kidx, True)
        valid = (rel >= 0) & (rel <= WINDOW_KEYS)
        if has_before is not True:
            valid = valid & ((kidx >= BLOCK) | has_before)
        one = jnp.where(valid, 0.0, NEG)
        return jnp.concatenate([one, one], axis=1)

    def live_keys(self, half):
        return (0, 2 * BLOCK - HALF_BLOCK) if half == 0 else (HALF_BLOCK, 2 * BLOCK)

    def half_bias(self, has_before, half):
        r0, r1 = self.live_keys(half)
        shape = (r1 - r0, LANES)
        kidx = lax.broadcasted_iota(jnp.int32, shape, 0) + r0
        qidx = jnp.bitwise_and(lax.broadcasted_iota(jnp.int32, shape, 1), HALF_BLOCK - 1) + half * HALF_BLOCK
        rel = self.index(qidx, False) - self.index(kidx, True)
        valid = (rel >= 0) & (rel <= WINDOW_KEYS)
        if has_before is not True:
            valid = valid & ((kidx >= BLOCK) | has_before)
        return jnp.where(valid, 0.0, NEG)


def _head_masks():
    lane = lax.broadcasted_iota(jnp.int32, (BLOCK, LANES), 1)
    lo = lane < HEAD_DIM
    return lane, lo, jnp.where(lo, 1.0, 0.0).astype(BF16), jnp.where(lo, 0.0, 1.0).astype(BF16)


def attn_fwd(name, q, k, v, run):
    nat = name == "nat"
    seq = q.shape[0] if nat else q.shape[2] * PERM
    attn_w = q.shape[-1]
    mode = _Mode(name, seq)
    groups = _lane_groups(attn_w)
    first = run is None
    all_lanes = slice(0, LANES)

    def body(*refs):
        q_ref, kp_ref, kc_ref, vp_ref, vc_ref = refs[:5]
        if first:
            o_ref, l_ref = refs[5:]
        elif nat:
            oin_ref, lin_ref, o_ref, l_ref, ostage, lstage = refs[5:]
        else:
            oin_ref, lin_ref, o_ref, l_ref = refs[5:]
        n = pl.program_id(1)
        subs = range(mode.qb)
        halves = (0, 1)
        live = [mode.live_keys(x) for x in halves]
        always = [mode.half_bias(True, x) for x in halves]
        biases = [[mode.half_bias(n > 0, x) for x in halves]] + [always] * (mode.qb - 1)
        _, lo, m_lo, m_hi = _head_masks()
        head_row = lax.broadcasted_iota(jnp.int32, (BLOCK, LANES), 0)
        ones = jnp.ones((2 * BLOCK, LANES), BF16)
        hb = HALF_BLOCK
        lrows = [jnp.zeros((BLOCK, LANES), F32) for _ in subs]
        if not first:
            if nat:
                for g, sl in enumerate(groups):
                    _from_perm(oin_ref, sl, ostage, g)
                _from_perm(lin_ref, all_lanes, lstage, 0)
            wide_rows = lambda a, sub: a[sub * BLOCK:(sub + 1) * BLOCK]
            before = [jnp.transpose(wide_rows(_stage_get(lstage, 0), sub) if nat else mode.get(lin_ref, all_lanes, sub))
                      for sub in subs]

        def probs(sub, p, sl):
            q2 = mode.get(q_ref, sl, sub)
            kcat = mode.keys(kp_ref, kc_ref, sl, sub)
            vcat = mode.keys(vp_ref, vc_ref, sl, sub)
            q_lo, q_hi = q2 * m_lo, q2 * m_hi
            qq = jnp.concatenate([q_lo[:hb], q_hi[:hb], q_lo[hb:], q_hi[hb:]], axis=0)
            s_t = _nt(kcat, qq)
            columns, lses = [], []
            for x in halves:
                r0, r1 = live[x]
                s_x = s_t[r0:r1, x * LANES:(x + 1) * LANES] + biases[sub][x]
                m = jnp.max(s_x, axis=0, keepdims=True)
                pe = jnp.exp(s_x - m)
                lse = m + jnp.log(jnp.sum(pe, axis=0, keepdims=True))
                if not first:
                    was = jnp.concatenate([before[sub][2 * p:2 * p + 1, x * hb:(x + 1) * hb],
                                           before[sub][2 * p + 1:2 * p + 2, x * hb:(x + 1) * hb]], axis=1)
                    top = jnp.maximum(was, lse)
                    lse = top + jnp.log(jnp.exp(was - top) + jnp.exp(lse - top))
                    pe = pe * jnp.exp(m - lse)
                pieces = [pe.astype(BF16)]
                if r0 > 0:
                    pieces.insert(0, jnp.zeros((r0, LANES), BF16))
                if r1 < 2 * BLOCK:
                    pieces.append(jnp.zeros((2 * BLOCK - r1, LANES), BF16))
                columns.append(pieces[0] if len(pieces) == 1 else jnp.concatenate(pieces, axis=0))
                lses.append(lse)
            return jnp.concatenate([vcat, ones], axis=1), jnp.concatenate(columns, axis=1), lses

        def output(sub, p, sl, vext, pb, lses):
            o_ext = _tn(pb, vext)
            if first:
                o_new = o_ext[:, :LANES] / o_ext[:, LANES:]
            else:
                o_prev = wide_rows(_stage_get(ostage, p), sub) if nat else mode.get(oin_ref, sl, sub)
                same = jnp.concatenate([o_prev[:hb], o_prev[:hb], o_prev[hb:], o_prev[hb:]], axis=0)
                o_new = o_ext[:, :LANES] + same * (1.0 - o_ext[:, LANES:])
            head_lo = jnp.concatenate([o_new[:hb], o_new[2 * hb:3 * hb]], axis=0)
            head_hi = jnp.concatenate([o_new[hb:2 * hb], o_new[3 * hb:]], axis=0)
            mode.put(o_ref, sl, jnp.where(lo, head_lo, head_hi), sub)
            lse_lo = jnp.concatenate([lses[0][:, :hb], lses[1][:, :hb]], axis=1)
            lse_hi = jnp.concatenate([lses[0][:, hb:], lses[1][:, hb:]], axis=1)
            rows = jnp.where(head_row == 2 * p, lse_lo, lrows[sub])
            lrows[sub] = jnp.where(head_row == 2 * p + 1, lse_hi, rows)

        pending = None
        for sub in subs:
            for p, sl in enumerate(groups):
                nxt = probs(sub, p, sl)
                if pending is not None:
                    output(*pending)
                pending = (sub, p, sl, *nxt)
        output(*pending)
        for sub in subs:
            mode.put(l_ref, all_lanes, jnp.transpose(lrows[sub]), sub)

    ins = [q, k, k, v, v]
    specs = [mode.wide(attn_w), mode.block_before(attn_w), mode.wide(attn_w), mode.block_before(attn_w), mode.wide(attn_w)]
    scratch = []
    if not first:
        ins += list(run)
        if nat:
            rows_a = mode.qb * BLOCK // PERM
            specs += [pl.BlockSpec((PJ, PJ, rows_a, attn_w), lambda r, n: (0, 0, n, 0)),
                      pl.BlockSpec((PJ, PJ, rows_a, LANES), lambda r, n: (0, 0, n, 0))]
            scratch = [pltpu.VMEM(_stage_shape(len(groups), mode.qb * BLOCK), F32),
                       pltpu.VMEM(_stage_shape(1, mode.qb * BLOCK), F32)]
        else:
            specs += [mode.wide(attn_w), mode.wide(LANES)]
    if nat:
        out_shape = (jax.ShapeDtypeStruct((seq, attn_w), F32), jax.ShapeDtypeStruct((seq, LANES), F32))
    else:
        out_shape = (jax.ShapeDtypeStruct(_perm_shape(seq, attn_w), F32), jax.ShapeDtypeStruct(_perm_shape(seq, LANES), F32))
    return pl.pallas_call(
        body, name=f"attn_fwd_{name}", grid=(mode.residues, mode.steps),
        out_shape=out_shape, in_specs=specs, out_specs=(mode.wide(attn_w), mode.wide(LANES)),
        scratch_shapes=scratch,
        compiler_params=_params(("arbitrary", "arbitrary")),
    )(*ins)


def attn_bwd(name, q, k, v, d_o, lse, delta, run):
    nat = name == "nat"
    seq = q.shape[0] if nat else q.shape[2] * PERM
    attn_w = q.shape[-1]
    mode = _Mode(name, seq)
    steps, qb = mode.steps, mode.qb
    single = steps == 1
    groups = _lane_groups(attn_w)
    first = run is None
    all_lanes = slice(0, LANES)

    def body(*refs):
        q_ref, kp_ref, kc_ref, vp_ref, vc_ref, do_ref, lse_ref, dl_ref = refs[:8]
        if first:
            dq_ref, dk_ref, dv_ref, ck, cv = refs[8:]
        else:
            dqi_ref, dki_ref, dvi_ref, dq_ref, dk_ref, dv_ref, ck, cv = refs[8:]
        g = pl.program_id(1) if single else pl.program_id(0)
        n = g if single else lax.rem(g, steps)
        carries = ((ck, dk_ref, None if first else dki_ref), (cv, dv_ref, None if first else dvi_ref))

        def emit(out_ref, acc_ref, sl, sub, val):
            if acc_ref is not None:
                val = val + mode.get(acc_ref, sl, sub).astype(F32)
            mode.put(out_ref, sl, val, sub)

        if not single:
            @pl.when(g == 0)
            def _():
                ck[...] = jnp.zeros_like(ck)
                cv[...] = jnp.zeros_like(cv)

        @pl.when(g < total)
        def _():
            biases = [mode.bias(n > 0)] + [mode.bias(True)] * (qb - 1)
            _, lo, m_lo, m_hi = _head_masks()

            def scores(sub, p, sl, lse_t, dl_t):
                q2, do2 = mode.get(q_ref, sl, sub), mode.get(do_ref, sl, sub)
                kcat = mode.keys(kp_ref, kc_ref, sl, sub)
                vcat = mode.keys(vp_ref, vc_ref, sl, sub)
                qq = jnp.concatenate([q2 * m_lo, q2 * m_hi], axis=0)
                dd = jnp.concatenate([do2 * m_lo, do2 * m_hi], axis=0)
                h0 = 2 * p
                lse2 = jnp.concatenate([lse_t[h0:h0 + 1, :], lse_t[h0 + 1:h0 + 2, :]], axis=1)
                dl2 = jnp.concatenate([dl_t[h0:h0 + 1, :], dl_t[h0 + 1:h0 + 2, :]], axis=1)
                p_t = jnp.exp(_nt(kcat, qq) + (biases[sub] - lse2))
                ds_t = p_t * (_nt(vcat, dd) - dl2)
                return qq, dd, kcat, p_t.astype(BF16), ds_t.astype(BF16)

            def grads(sub, sl, qq, dd, kcat, pb, dsb):
                dqb = _tn(dsb, kcat)
                dq2 = jnp.where(lo, dqb[:BLOCK], dqb[BLOCK:]) * ATTN_SCALE
                if not first:
                    dq2 = dq2 + mode.get(dqi_ref, sl, sub).astype(F32)
                mode.put(dq_ref, sl, dq2, sub)
                for (carry, out_ref, acc_ref), lhs, rhs in zip(carries, (dsb, pb), (qq, dd)):
                    both = jnp.dot(lhs, rhs, preferred_element_type=F32)
                    if sub == 0:
                        if not single:
                            for s in range(qb - 1):
                                emit(out_ref, acc_ref, sl, s, carry[s, :, sl])
                            emit(out_ref, acc_ref, sl, qb - 1, carry[qb - 1, :, sl] + both[:BLOCK])
                        carry[0, :, sl] = both[BLOCK:]
                    else:
                        carry[sub - 1, :, sl] += both[:BLOCK]
                        carry[sub, :, sl] = both[BLOCK:]
                    if single and sub == qb - 1:
                        for s in range(qb):
                            emit(out_ref, acc_ref, sl, s, carry[s, :, sl])

            stats = [(jnp.transpose(mode.get(lse_ref, all_lanes, sub)),
                      jnp.transpose(mode.get(dl_ref, all_lanes, sub))) for sub in range(qb)]
            pending = None
            for p, sl in enumerate(groups):
                for sub in range(qb):
                    nxt = scores(sub, p, sl, *stats[sub])
                    if pending is not None:
                        grads(*pending)
                    pending = (sub, sl, *nxt)
            grads(*pending)

        if not single:
            @pl.when(g == total)
            def _():
                for carry, out_ref, acc_ref in carries:
                    for sl in groups:
                        for s in range(qb):
                            emit(out_ref, acc_ref, sl, s, carry[s, :, sl])

    total = mode.residues * steps
    if single:
        here = before = lambda r, n: (r, n)
    else:
        locate = lambda g: (g // steps, lax.rem(g, steps))
        here = lambda g: locate(jnp.minimum(g, total - 1))
        before = lambda g: locate(jnp.maximum(g - 1, 0))
    wide = lambda w: mode.wide(w, here)
    ins = [q, k, k, v, v, d_o, lse, delta]
    specs = [wide(attn_w), mode.block_before(attn_w, here), wide(attn_w), mode.block_before(attn_w, here), wide(attn_w),
             wide(attn_w), wide(LANES), wide(LANES)]
    if not first:
        ins += list(run)
        specs += [wide(attn_w), mode.wide(attn_w, before), mode.wide(attn_w, before)]
    shp = jax.ShapeDtypeStruct((seq, attn_w) if nat else _perm_shape(seq, attn_w), BF16)
    grid = (mode.residues, 1) if single else (total + 1,)
    return pl.pallas_call(
        body, name=f"attn_bwd_{name}", grid=grid,
        out_shape=(shp, shp, shp), in_specs=specs,
        out_specs=(wide(attn_w), mode.wide(attn_w, before), mode.wide(attn_w, before)),
        scratch_shapes=[pltpu.VMEM((qb, BLOCK, attn_w), F32), pltpu.VMEM((qb, BLOCK, attn_w), F32)],
        compiler_params=_params(("arbitrary",) * len(grid)),
    )(*ins)


def _shift_down(u, halo, k):
    rolled = pltpu.roll(u, k, 0)
    row = lax.broadcasted_iota(jnp.int32, halo.shape, 0)
    top = jnp.where(row < k, pltpu.roll(halo, k, 0), rolled[:SUBLANES])
    return jnp.concatenate([top, rolled[SUBLANES:]], axis=0)


def _shift_up(u, halo, k):
    rows = u.shape[0]
    rolled = pltpu.roll(u, rows - k, 0)
    row = lax.broadcasted_iota(jnp.int32, halo.shape, 0)
    bot = jnp.where(row >= SUBLANES - k, pltpu.roll(halo, SUBLANES - k, 0), rolled[rows - SUBLANES:])
    return jnp.concatenate([rolled[:rows - SUBLANES], bot], axis=0)


def tail(o, lse, ga, cz, x, tgt, w_out, g2, cw):
    seq, d_model = x.shape
    attn_w = o.shape[1]
    conv_w = cz.shape[1] // 4
    mix = attn_w + conv_w
    groups = _lane_groups(attn_w)
    tm = ROW_TILE
    nt = seq // tm
    hb = tm // SUBLANES

    def body(o_ref, l_ref, ga_ref, cz_ref, hz_ref, x_ref, t_ref, w_ref, g_ref, cw_ref,
             do_ref, dl_ref, dop_ref, dlp_ref, lp_ref, dga_ref, dcb_ref, dgc_ref, dcv_ref, e_ref,
             dw_ref, dg_ref, dcw_ref, loss_ref, stage):
        i = pl.program_id(0)

        @pl.when(i == 0)
        def _():
            dw_ref[...] = jnp.zeros_like(dw_ref)
            dg_ref[...] = jnp.zeros_like(dg_ref)
            dcw_ref[...] = jnp.zeros_like(dcw_ref)
            loss_ref[...] = jnp.zeros_like(loss_ref)

        u = cz_ref[:, 2 * conv_w:3 * conv_w] * cz_ref[:, 0:conv_w]
        uh = hz_ref[:, 2 * conv_w:3 * conv_w] * hz_ref[:, 0:conv_w]
        uh = jnp.where(i > 0, uh, 0.0)
        u1 = _shift_down(u, uh, 1)
        u2 = _shift_down(u, uh, 2)
        w0, w1, w2 = cw_ref[0:1, :], cw_ref[1:2, :], cw_ref[2:3, :]
        cvv = u2 * w0 + u1 * w1 + u * w2
        gv = g_ref[...]
        all_lanes = slice(0, LANES)

        def forward(rs):
            ov, gav = o_ref[rs, :], ga_ref[rs, :]
            sig_a = _sigmoid(gav)
            silu_a = gav * sig_a
            cb, gc = cz_ref[rs, conv_w:2 * conv_w], cz_ref[rs, 3 * conv_w:4 * conv_w]
            sig_c = _sigmoid(gc)
            silu_c = gc * sig_c
            bc = cb * cvv[rs]
            mixed = jnp.concatenate([ov * silu_a, bc * silu_c], axis=1).astype(BF16)
            yv = jnp.dot(mixed, w_ref[...], preferred_element_type=F32)
            return ov, gav, sig_a, silu_a, cb, gc, sig_c, silu_c, bc, mixed, yv

        def loss_and_dy(rs, mixed, yv):
            r2 = lax.rsqrt(jnp.mean(yv * yv, axis=-1, keepdims=True) + NORM_EPS)
            yhat = yv * r2
            diff = (x_ref[rs, :] + yhat * gv) - t_ref[rs, :]
            loss_ref[...] += _rowgroup_sum(diff * diff)
            ev = diff * (1.0 / d_model)
            e_ref[rs, :] = ev
            dg_ref[...] += _rowgroup_sum(ev * yhat)
            eg = ev * gv
            dy = (r2 * (eg - yhat * jnp.mean(eg * yhat, axis=-1, keepdims=True))).astype(BF16)
            dw_ref[...] += _tn(mixed, dy)
            return _nt(dy, w_ref[...])

        def backward(rs, ov, gav, sig_a, silu_a, cb, gc, sig_c, silu_c, bc, dm):
            rows = rs.stop - rs.start
            dma, dmc = dm[:, :attn_w], dm[:, attn_w:]
            dov = dma * silu_a
            do_ref[rs, :] = dov.astype(BF16)
            dga_ref[rs, :] = (dma * ov * (sig_a * (1.0 + gav * (1.0 - sig_a)))).astype(BF16)
            prod = dov * ov
            lane = lax.broadcasted_iota(jnp.int32, (rows, LANES), 1)
            lo = lane < HEAD_DIM
            dblk = jnp.zeros((rows, LANES), F32)
            for p, sl in enumerate(groups):
                pr = prod[:, sl]
                dblk = jnp.where(lane == 2 * p, jnp.sum(jnp.where(lo, pr, 0.0), axis=1, keepdims=True), dblk)
                dblk = jnp.where(lane == 2 * p + 1, jnp.sum(jnp.where(lo, 0.0, pr), axis=1, keepdims=True), dblk)
                _stage_put(stage, p, dov[:, sl], rs.start)
            dl_ref[rs, :] = dblk
            _stage_put(stage, len(groups), dblk, rs.start)
            _stage_put(stage, len(groups) + 1, l_ref[rs, :], rs.start)
            dsc = dmc * silu_c
            cv_rows = cvv[rs]
            dcb_ref[rs, :] = (dsc * cv_rows).astype(BF16)
            dgc_ref[rs, :] = (dmc * bc * (sig_c * (1.0 + gc * (1.0 - sig_c)))).astype(BF16)
            dcv = dsc * cb
            dcv_ref[rs, :] = dcv
            dcw_ref[0:SUBLANES, :] += _rowgroup_sum(dcv * u2[rs])
            dcw_ref[SUBLANES:2 * SUBLANES, :] += _rowgroup_sum(dcv * u1[rs])
            dcw_ref[2 * SUBLANES:3 * SUBLANES, :] += _rowgroup_sum(dcv * u[rs])

        halves = [slice(0, tm // 2), slice(tm // 2, tm)]
        fwd = [forward(rs) for rs in halves]
        dms = [loss_and_dy(rs, f[9], f[10]) for rs, f in zip(halves, fwd)]
        for rs, f, dm in zip(halves, fwd, dms):
            backward(rs, *f[:9], dm)
        for p, sl in enumerate(groups):
            _to_perm(stage, p, dop_ref, sl, BF16)
        _to_perm(stage, len(groups), dlp_ref, all_lanes, F32)
        _to_perm(stage, len(groups) + 1, lp_ref, all_lanes, F32)

    row = lambda n: pl.BlockSpec((tm, n), lambda i: (i, 0))
    whole = lambda a, b: pl.BlockSpec((a, b), lambda i: (0, 0))
    return pl.pallas_call(
        body, name="tail", grid=(nt,),
        out_shape=(jax.ShapeDtypeStruct((seq, attn_w), BF16), jax.ShapeDtypeStruct((seq, LANES), F32),
                   jax.ShapeDtypeStruct(_perm_shape(seq, attn_w), BF16), jax.ShapeDtypeStruct(_perm_shape(seq, LANES), F32),
                   jax.ShapeDtypeStruct(_perm_shape(seq, LANES), F32),
                   jax.ShapeDtypeStruct((seq, attn_w), BF16), jax.ShapeDtypeStruct((seq, conv_w), BF16),
                   jax.ShapeDtypeStruct((seq, conv_w), BF16), jax.ShapeDtypeStruct((seq, conv_w), F32),
                   jax.ShapeDtypeStruct((seq, d_model), F32), jax.ShapeDtypeStruct((mix, d_model), F32),
                   jax.ShapeDtypeStruct((SUBLANES, d_model), F32), jax.ShapeDtypeStruct((CONV_K * SUBLANES, conv_w), F32),
                   jax.ShapeDtypeStruct((SUBLANES, d_model), F32)),
        in_specs=[row(attn_w), row(LANES), row(attn_w), row(4 * conv_w),
                  pl.BlockSpec((SUBLANES, 4 * conv_w), lambda i: (jnp.maximum(i * hb - 1, 0), 0)),
                  row(d_model), row(d_model), _const_spec((mix, d_model)), _const_spec((1, d_model)),
                  _const_spec((SUBLANES, conv_w))],
        out_specs=(row(attn_w), row(LANES), _perm_tile_spec(attn_w, tm), _perm_tile_spec(LANES, tm), _perm_tile_spec(LANES, tm),
                   row(attn_w), row(conv_w), row(conv_w), row(conv_w), row(d_model),
                   whole(mix, d_model), whole(SUBLANES, d_model), whole(CONV_K * SUBLANES, conv_w),
                   whole(SUBLANES, d_model)),
        scratch_shapes=[pltpu.VMEM(_stage_shape(len(groups) + 2, tm), F32)],
        compiler_params=_params(("arbitrary",)),
    )(o, lse, ga, cz, cz, x, tgt, w_out, g2, cw)


def dz_dx(nat_grads, perm_grads, dga, dcb, dgc, dcv, cz, tables, x, g1, e, w_full, cw):
    seq, d_model = x.shape
    attn_w = dga.shape[1]
    conv_w = dcv.shape[1]
    width = w_full.shape[2]
    in_w = 4 * attn_w + 4 * conv_w
    groups = _lane_groups(attn_w)
    tm = ROW_TILE
    nt = seq // tm
    hb = tm // SUBLANES

    def body(dq_ref, dk_ref, dv_ref, dqp_ref, dkp_ref, dvp_ref, dga_ref, dcb_ref, dgc_ref, dcv_ref, nh_ref, ch_ref, cc_ref,
             cos_ref, s1_ref, s2_ref, x_ref, g_ref, e_ref, w_ref, cw_ref, gx_ref, dz_ref, dg_ref, stage):
        i = pl.program_id(0)

        @pl.when(i == 0)
        def _():
            dg_ref[...] = jnp.zeros_like(dg_ref)

        cos, s1, s2 = cos_ref[...], s1_ref[...], s2_ref[...]

        def qkv_columns(t, nat_ref, perm_ref):
            for g, sl in enumerate(groups):
                _from_perm(perm_ref, sl, stage, g)
            for g, sl in enumerate(groups):
                tot = nat_ref[:, sl].astype(F32) + _stage_get(stage, g)
                if t < 2:
                    tot = _rope_transposed(tot, cos, s1, s2)
                dz_ref[:, t * attn_w + g * LANES:t * attn_w + (g + 1) * LANES] = tot.astype(BF16)

        def dh_part(j):
            return _nt(dz_ref[:, j * width:(j + 1) * width], w_ref[j])

        dcv = dcv_ref[...]
        nh = jnp.where(i < nt - 1, nh_ref[...], 0.0)
        w0, w1, w2 = cw_ref[0:1, :], cw_ref[1:2, :], cw_ref[2:3, :]
        du = dcv * w2 + _shift_up(dcv, nh, 1) * w1 + _shift_up(dcv, nh, 2) * w0
        base = 4 * attn_w
        dz_ref[:, base:base + conv_w] = (du * cc_ref[...]).astype(BF16)
        dz_ref[:, base + conv_w:base + 2 * conv_w] = dcb_ref[...]
        dz_ref[:, base + 2 * conv_w:base + 3 * conv_w] = (du * ch_ref[...]).astype(BF16)
        dz_ref[:, base + 3 * conv_w:base + 4 * conv_w] = dgc_ref[...]
        dz_ref[:, 3 * attn_w:4 * attn_w] = dga_ref[...]
        ready = in_w
        dh = None
        for t, nat_ref, perm_ref in ((2, dv_ref, dvp_ref), (1, dk_ref, dkp_ref), (0, dq_ref, dqp_ref), (None, None, None)):
            lowest_open = 0 if t is None else (t + 1) * attn_w
            while ready - width >= lowest_open:
                ready -= width
                part = dh_part(ready // width)
                dh = part if dh is None else dh + part
            if t is not None:
                qkv_columns(t, nat_ref, perm_ref)
        xv = x_ref[...]
        r1 = lax.rsqrt(jnp.mean(xv * xv, axis=-1, keepdims=True) + NORM_EPS)
        xhat = xv * r1
        dg_ref[...] += _rowgroup_sum(dh * xhat)
        dhg = dh * g_ref[...]
        gx_ref[...] = r1 * (dhg - xhat * jnp.mean(dhg * xhat, axis=-1, keepdims=True)) + e_ref[...]

    row = lambda n: pl.BlockSpec((tm, n), lambda i: (i, 0))
    whole = lambda a, b: pl.BlockSpec((a, b), lambda i: (0, 0))
    pt = _perm_tile_spec(attn_w, tm)
    return pl.pallas_call(
        body, name="dz_dx", grid=(nt,),
        out_shape=(jax.ShapeDtypeStruct((seq, d_model), F32), jax.ShapeDtypeStruct((seq, in_w), BF16),
                   jax.ShapeDtypeStruct((SUBLANES, d_model), F32)),
        in_specs=[row(attn_w), row(attn_w), row(attn_w), pt, pt, pt, row(attn_w), row(conv_w), row(conv_w), row(conv_w),
                  pl.BlockSpec((SUBLANES, conv_w), lambda i: (jnp.minimum((i + 1) * hb, seq // SUBLANES - 1), 0)),
                  pl.BlockSpec((tm, conv_w), lambda i: (i, 0)), pl.BlockSpec((tm, conv_w), lambda i: (i, 2)),
                  row(LANES), row(LANES), row(LANES), row(d_model), _const_spec((1, d_model)), row(d_model),
                  _const_spec(w_full.shape), _const_spec((SUBLANES, conv_w))],
        out_specs=(row(d_model), row(in_w), whole(SUBLANES, d_model)),
        scratch_shapes=[pltpu.VMEM(_stage_shape(len(groups), tm), F32)],
        compiler_params=_params(("arbitrary",)),
    )(*nat_grads, *perm_grads, dga, dcb, dgc, dcv, dcv, cz, cz, *tables, x, g1, e, w_full, cw)


def dw_in_reduce(ht, dz, g_out, small):
    d_model, seq = ht.shape
    half = dz.shape[1] // N_DEV
    ts = min(DW_ROWS, seq)
    steps = seq // ts
    x, y, c = lax.axis_index("x"), lax.axis_index("y"), lax.axis_index("c")
    far_first = lambda x, y: [(1 - x, 1 - y), (1 - x, y), (x, 1 - y)]
    chips = jnp.stack([2 * px + py for px, py in far_first(x, y)] + [2 * x + y]).astype(jnp.int32)
    order = jnp.stack([2 * chips + (1 - c), 2 * chips + c], axis=1).reshape(N_DEV)

    def body(order_ref, ht_ref, dz_ref, go_ref, sm_ref, out_ref, ro_ref, rs_ref,
             acc, theirs, staged, contrib, resbuf, out_sem, sa, ra, sb, rb, sc, rc,
             o_mine, o_theirs, o_staged, o_contrib, o_res, sbuf, o_load, osa, ora, osb, orb, osc, orc, ss, rs):
        del order_ref
        p, s = pl.program_id(0), pl.program_id(1)
        x, y, c = lax.axis_index("x"), lax.axis_index("y"), lax.axis_index("c")
        me = 2 * x + y
        sib = (x, y, 1 - c)
        peers = far_first(x, y)
        slot = p % 2

        flips = [(fx, fy, fc) for fx in (0, 1) for fy in (0, 1) for fc in (0, 1)][1:]
        my8 = 4 * x + 2 * y + c
        chip_ids = [2 * px + py for px, py in peers] + [me]

        def small_copy(k, slot8, to):
            return pltpu.make_async_remote_copy(src_ref=sm_ref, dst_ref=sbuf.at[slot8], send_sem=ss.at[k], recv_sem=rs.at[k],
                                                device_id=to, device_id_type=MESH)

        def small_peer(k):
            fx, fy, fc = flips[k]
            return _flip(x, fx), _flip(y, fy), _flip(c, fc)

        def oa_copy(pos):
            j = chip_ids[pos]
            return pltpu.make_async_remote_copy(src_ref=go_ref.at[j, 1 - c], dst_ref=o_theirs.at[j], send_sem=osa.at[pos],
                                                recv_sem=ora.at[pos], device_id=sib, device_id_type=MESH)

        def o_load_copy(pos):
            j = chip_ids[pos]
            return pltpu.make_async_copy(go_ref.at[j, c], o_mine.at[j], o_load.at[pos])

        def ob_copy(k, piece, slot4):
            px, py = peers[k]
            return pltpu.make_async_remote_copy(src_ref=o_staged.at[piece], dst_ref=o_contrib.at[slot4], send_sem=osb.at[k],
                                                recv_sem=orb.at[k], device_id=(px, py, c), device_id_type=MESH)

        def oc_copy(which):
            return pltpu.make_async_remote_copy(src_ref=o_res.at[which], dst_ref=o_res.at[which], send_sem=osc, recv_sem=orc,
                                                device_id=sib, device_id_type=MESH)

        @pl.when((p == 0) & (s == 0))
        def _():
            sbuf[my8] = sm_ref[...]
            for k in range(N_DEV - 1):
                small_copy(k, my8, small_peer(k)).start()
            for pos in range(N_CHIPS):
                o_load_copy(pos).start()
                oa_copy(pos).start()

        @pl.when((p == 1) & (s == steps - 1))
        def _():
            for pos in range(N_CHIPS):
                j = chip_ids[pos]
                o_load_copy(pos).wait()
                oa_copy(pos).wait_recv()
                if pos < N_CHIPS - 1:
                    o_staged[j] = (o_mine[j] + o_theirs[j]).astype(BF16)
                    ob_copy(pos, j, me).start()
                else:
                    o_mine[j] = o_mine[j] + o_theirs[j]
                    o_contrib[j] = o_mine[j].astype(BF16)

        @pl.when((p == 4) & (s == steps - 1))
        def _():
            for k in range(N_CHIPS - 1):
                ob_copy(k, me, chip_ids[k]).wait_recv()
            own = o_mine[me]
            term = lambda j: jnp.where(me == j, own, o_contrib[j].astype(F32))
            o_res[c] = ((term(0) + term(1)) + term(2)) + term(3)
            oc_copy(c).start()

        def a_copy(k):
            return pltpu.make_async_remote_copy(src_ref=acc.at[0], dst_ref=theirs.at[k], send_sem=sa.at[k], recv_sem=ra.at[k],
                                                device_id=sib, device_id_type=MESH)

        def b_copy(k):
            px, py = peers[k]
            return pltpu.make_async_remote_copy(src_ref=staged.at[k], dst_ref=contrib.at[k], send_sem=sb.at[k], recv_sem=rb.at[k],
                                                device_id=(px, py, c), device_id_type=MESH)

        def c_copy(which):
            return pltpu.make_async_remote_copy(src_ref=resbuf.at[which], dst_ref=resbuf.at[which], send_sem=sc, recv_sem=rc,
                                                device_id=sib, device_id_type=MESH)

        @pl.when(s == 0)
        def _():
            for k in range(N_CHIPS - 1):
                @pl.when(p == 2 * k + 2)
                def _():
                    a_copy(k).wait_send()
            acc[slot] = jnp.dot(ht_ref[...], dz_ref[...], preferred_element_type=F32)

        @pl.when(s > 0)
        def _():
            acc[slot] += jnp.dot(ht_ref[...], dz_ref[...], preferred_element_type=F32)

        @pl.when(s == steps - 1)
        def _():
            for k in range(N_CHIPS):
                @pl.when(p == 2 * k)
                def _():
                    a_copy(k).start()
            for k in range(N_CHIPS - 1):
                @pl.when(p == 2 * k + 1)
                def _():
                    a_copy(k).wait_recv()
                    staged[k] = (acc[1] + theirs[k]).astype(BF16)
                    b_copy(k).start()

            @pl.when(p == N_DEV - 1)
            def _():
                a_copy(N_CHIPS - 1).wait_recv()
                tot = acc[1] + theirs[N_CHIPS - 1]
                for k in range(N_CHIPS - 1):
                    b_copy(k).wait_recv()
                    tot = tot + contrib[k].astype(F32)
                resbuf[c] = tot
                c_copy(c).start()
                done_mine = pltpu.make_async_copy(resbuf.at[c], out_ref.at[c], out_sem)
                done_mine.start()
                c_copy(1 - c).wait_recv()
                done = pltpu.make_async_copy(resbuf.at[1 - c], out_ref.at[1 - c], out_sem)
                done.start()
                oc_copy(1 - c).wait_recv()
                ro_ref[...] = o_res[...]
                for k in range(N_DEV - 1):
                    px, py, pc = small_peer(k)
                    small_copy(k, 4 * px + 2 * py + pc, (px, py, pc)).wait_recv()
                tot8 = sbuf[0]
                for d in range(1, N_DEV):
                    tot8 = tot8 + sbuf[d]
                rs_ref[...] = tot8
                a_copy(N_CHIPS - 1).wait_send()
                for k in range(N_CHIPS - 1):
                    b_copy(k).wait_send()
                    ob_copy(k, chip_ids[k], me).wait_send()
                c_copy(c).wait_send()
                oc_copy(c).wait_send()
                for pos in range(N_CHIPS):
                    oa_copy(pos).wait_send()
                for k in range(N_DEV - 1):
                    small_copy(k, my8, small_peer(k)).wait_send()
                done_mine.wait()
                done.wait()

    dma = pltpu.SemaphoreType.DMA
    o_shape = g_out.shape[1:]
    go = g_out.reshape(N_CHIPS, 2, *o_shape)
    const = lambda shape: pl.BlockSpec(shape, lambda p, s, order_ref: (0,) * len(shape))
    grid_spec = pltpu.PrefetchScalarGridSpec(
        num_scalar_prefetch=1, grid=(N_DEV, steps),
        in_specs=[pl.BlockSpec((d_model, ts), lambda p, s, order_ref: (0, s)),
                  pl.BlockSpec((ts, half), lambda p, s, order_ref: (s, order_ref[p])),
                  pl.BlockSpec(memory_space=pl.ANY), const(small.shape)],
        out_specs=(pl.BlockSpec(memory_space=pl.ANY), const((2, *o_shape)), const(small.shape)),
        scratch_shapes=[pltpu.VMEM((2, d_model, half), F32), pltpu.VMEM((N_CHIPS, d_model, half), F32),
                        pltpu.VMEM((N_CHIPS - 1, d_model, half), BF16), pltpu.VMEM((N_CHIPS - 1, d_model, half), BF16),
                        pltpu.VMEM((2, d_model, half), F32), dma,
                        dma((N_CHIPS,)), dma((N_CHIPS,)), dma((N_CHIPS - 1,)), dma((N_CHIPS - 1,)), dma, dma,
                        pltpu.VMEM((N_CHIPS, *o_shape), F32), pltpu.VMEM((N_CHIPS, *o_shape), F32),
                        pltpu.VMEM((N_CHIPS, *o_shape), BF16), pltpu.VMEM((N_CHIPS, *o_shape), BF16),
                        pltpu.VMEM((2, *o_shape), F32), pltpu.VMEM((N_DEV, *small.shape), F32),
                        dma((N_CHIPS,)), dma((N_CHIPS,)), dma((N_CHIPS,)), dma((N_CHIPS - 1,)), dma((N_CHIPS - 1,)), dma, dma,
                        dma((N_DEV - 1,)), dma((N_DEV - 1,))])
    return pl.pallas_call(
        body, name="dw_in_reduce", grid_spec=grid_spec,
        out_shape=(jax.ShapeDtypeStruct((2, d_model, half), F32), jax.ShapeDtypeStruct((2, *o_shape), F32),
                   jax.ShapeDtypeStruct(small.shape, F32)),
        compiler_params=_params(("arbitrary", "arbitrary")),
    )(order, ht, dz, go, small)


def _adam_math(w, g, m, v):
    m = ADAM_B1 * m + (1.0 - ADAM_B1) * g
    v = ADAM_B2 * v + (1.0 - ADAM_B2) * (g * g)
    m_hat = m / (1.0 - ADAM_B1 ** ADAM_STEP)
    v_hat = v / (1.0 - ADAM_B2 ** ADAM_STEP)
    delta = -ADAM_LR * (m_hat / (jnp.sqrt(v_hat) + ADAM_EPS) + ADAM_WD * w)
    return delta, m, v


def adam_shard(name, w, g2, m, v, block, grid, w_map, g_map):
    def body(w_ref, g_ref, m_ref, v_ref, go_ref, d_ref, mo_ref, vo_ref):
        g = g_ref[0]
        delta, mn, vn = _adam_math(w_ref[...], g, m_ref[...], v_ref[...])
        go_ref[...] = g
        d_ref[...] = delta
        mo_ref[...] = mn
        vo_ref[...] = vn

    ws = pl.BlockSpec(block, w_map)
    shp = jax.ShapeDtypeStruct(w.shape, F32)
    return pl.pallas_call(
        body, name=name, grid=grid, out_shape=(shp, shp, shp, shp),
        in_specs=[ws, pl.BlockSpec((1, *block), g_map), ws, ws], out_specs=(ws, ws, ws, ws),
        compiler_params=_params(("arbitrary",) * len(grid)),
    )(w, g2, m, v)


def adam_small(ws, gs, ms, vs):
    n = len(ws)

    def body(*refs):
        ins, outs = refs[:4 * n], refs[4 * n:]
        for t in range(n):
            delta, mn, vn = _adam_math(ins[t][...], ins[n + t][...], ins[2 * n + t][...], ins[3 * n + t][...])
            outs[3 * t][...] = delta
            outs[3 * t + 1][...] = mn
            outs[3 * t + 2][...] = vn

    vm = pl.BlockSpec(memory_space=pltpu.VMEM)
    outs = pl.pallas_call(
        body, name="adam_small",
        out_shape=tuple(jax.ShapeDtypeStruct(w.shape, F32) for w in ws for _ in range(3)),
        in_specs=[vm] * (4 * n), out_specs=tuple([vm] * (3 * n)),
        compiler_params=_params(),
    )(*ws, *gs, *ms, *vs)
    return [outs[3 * t:3 * t + 3] for t in range(n)]


def kernel(x, norm_pre_g, w_in, conv_w, w_out, norm_post_g, loss_target, m_norm_pre_g, m_w_in, m_conv_w, m_w_out, m_norm_post_g, v_norm_pre_g, v_w_in, v_conv_w, v_w_out, v_norm_post_g):
    _, seq, d_model = x.shape
    width = w_in.shape[1]
    conv_q = conv_w.shape[1]
    conv_width = N_CHIPS * conv_q
    attn_width = d_model - conv_width
    xs, tg = x[0], loss_target[0]
    g1, g2 = norm_pre_g.reshape(1, d_model), norm_post_g.reshape(1, d_model)

    w_full, wout_full, cw_full, *tables = gather_weights(w_in, w_out, conv_w, seq)
    wout2 = wout_full.reshape(attn_width + conv_width, d_model)
    cw = jnp.zeros((SUBLANES, conv_width), F32).at[:CONV_K].set(
        cw_full[:, :CONV_K, :conv_q].transpose(1, 0, 2).reshape(CONV_K, conv_width))

    ht, q, k, v, qp, kp, vp, ga, cz = inproj(xs, g1, w_full, tables, attn_width, conv_width)
    run = attn_fwd("p4", qp, kp, vp, None)
    run = attn_fwd("p16", qp, kp, vp, run)
    o, lse = attn_fwd("nat", q, k, v, run)
    (d_o, delta, d_op, delta_p, lse_p, dga, dcb, dgc, dcv, e, dwout, dg2, dcw, loss_acc) = tail(
        o, lse, ga, cz, xs, tg, wout2, g2, cw)
    nat_grads = attn_bwd("nat", q, k, v, d_o, lse, delta, None)
    perm_grads = attn_bwd("p4", qp, kp, vp, d_op, lse_p, delta_p, None)
    perm_grads = attn_bwd("p16", qp, kp, vp, d_op, lse_p, delta_p, perm_grads)
    grad_x, dz, dg1 = dz_dx(nat_grads, perm_grads, dga, dcb, dgc, dcv, cz, tables, xs, g1, e, w_full, cw)

    small = jnp.zeros((SUBLANES, d_model), F32)
    small = small.at[0].set(dg1.sum(axis=0)).at[1].set(dg2.sum(axis=0))
    small = small.at[2:2 + CONV_K, :conv_width].set(dcw.reshape(CONV_K, SUBLANES, conv_width).sum(axis=1))
    small = small.at[2 + CONV_K, 0].set(jnp.sum(loss_acc))
    rin, rout, rsmall = dw_in_reduce(ht, dz, dwout.reshape(N_DEV, -1, d_model), small)

    half = width // 2
    tr = min(ADAM_ROWS, d_model)
    gw_in, d_in, m_in, v_in = adam_shard(
        "adam_w_in", w_in, rin, m_w_in, v_w_in, (tr, half), (2, d_model // tr),
        lambda hf, i: (i, hf), lambda hf, i: (hf, i, 0))
    rq = w_out.shape[0] // 2
    gw_out, d_out, m_out, v_out = adam_shard(
        "adam_w_out", w_out, rout, m_w_out, v_w_out, (rq, d_model), (2,),
        lambda hf: (hf, 0), lambda hf: (hf, 0, 0))

    chip = 2 * lax.axis_index("x") + lax.axis_index("y")
    g_pre, g_post = rsmall[0:1], rsmall[1:2]
    g_conv = lax.dynamic_slice(rsmall[2:2 + CONV_K, :conv_width], (0, chip * conv_q), (CONV_K, conv_q))
    (d_pre, m_pre, v_pre), (d_post, m_post, v_post), (d_cv, m_cv, v_cv) = adam_small(
        [g1, g2, conv_w], [g_pre, g_post, g_conv],
        [m_norm_pre_g.reshape(1, d_model), m_norm_post_g.reshape(1, d_model), m_conv_w],
        [v_norm_pre_g.reshape(1, d_model), v_norm_post_g.reshape(1, d_model), v_conv_w])

    loss = 0.5 * rsmall[2 + CONV_K, 0] / d_model
    vec = lambda a: a.reshape(d_model)
    return (loss, grad_x.reshape(1, seq, d_model),
            vec(g_pre), gw_in, g_conv, gw_out, vec(g_post),
            vec(d_pre), d_in, d_cv, d_out, vec(d_post),
            vec(m_pre), m_in, m_cv, m_out, vec(m_post),
            vec(v_pre), v_in, v_cv, v_out, vec(v_post))
```

```python
import jax
import jax.numpy as jnp
from jax import lax
from jax.experimental import pallas as pl
from jax.experimental.pallas import tpu as pltpu

HEAD_DIM = 64
LANES = 128
SUBLANES = 8
BLOCK = 128
HALF_BLOCK = BLOCK // 2
WINDOW_KEYS = 128
PERM = 16
PJ = 4
P4_ROWS = BLOCK // PJ
MAX_QUERY_BLOCKS = 8
ROW_TILE = 512
DW_ROWS = 4096
ADAM_ROWS = 1024
CONV_K = 3
ROPE_THETA = 10000.0
NORM_EPS = 1e-6
ATTN_SCALE = HEAD_DIM ** -0.5
NEG = -1e30
N_CHIPS = 4
N_DEV = 8
MESH = pl.DeviceIdType.MESH
ADAM_LR = 0.001
ADAM_B1 = 0.9
ADAM_B2 = 0.999
ADAM_EPS = 1e-08
ADAM_WD = 0.01
ADAM_STEP = 10
VMEM_LIMIT = 63 * 1024 * 1024

F32 = jnp.float32
BF16 = jnp.bfloat16


def _params(sem=None, **kw):
    return pltpu.CompilerParams(dimension_semantics=sem, vmem_limit_bytes=VMEM_LIMIT, **kw)


def _const_spec(shape):
    return pl.BlockSpec(shape, lambda *_: (0,) * len(shape), pipeline_mode=pl.Buffered(1))


def _sigmoid(z):
    return 1.0 / (1.0 + jnp.exp(-z))


def _rowgroup_sum(a):
    rows, n = a.shape
    return a.reshape(rows // SUBLANES, SUBLANES, n).sum(axis=0)


def _nt(a, b):
    return lax.dot_general(a, b, (((1,), (1,)), ((), ())), preferred_element_type=F32)


def _tn(a, b):
    return lax.dot_general(a, b, (((0,), (0,)), ((), ())), preferred_element_type=F32)


def _col_pieces(a, b, width):
    out = []
    while a < b:
        j = a // width
        e = min(b, (j + 1) * width)
        out.append((j, a - j * width, e - j * width))
        a = e
    return out


def _lane_groups(width):
    return [slice(g * LANES, (g + 1) * LANES) for g in range(width // LANES)]


def _perm_shape(seq, width):
    return (PJ, PJ, seq // PERM, width)


def _perm_tile_spec(width, tm):
    return pl.BlockSpec((PJ, PJ, tm // PERM, width), lambda i: (0, 0, i, 0))


STAGE_PITCH = 24


def _stage_shape(groups, rows):
    return (groups, rows // PERM * STAGE_PITCH, LANES)


def _stage_put(stage, g, val, row0=0):
    for a in range(val.shape[0] // PERM):
        at = (row0 // PERM + a) * STAGE_PITCH
        stage[g, at:at + PERM, :] = val[a * PERM:(a + 1) * PERM]


def _stage_get(stage, g):
    return jnp.concatenate([stage[g, a * STAGE_PITCH:a * STAGE_PITCH + PERM, :]
                            for a in range(stage.shape[1] // STAGE_PITCH)], axis=0)


def _to_perm(stage, g, dst_ref, sl, dtype):
    rows = stage.shape[1] // STAGE_PITCH
    for b in range(PERM):
        dst_ref[b // PJ, b % PJ, :, sl] = stage[g, pl.ds(b, rows, stride=STAGE_PITCH), :].astype(dtype)


def _from_perm(src_ref, sl, stage, g):
    rows = stage.shape[1] // STAGE_PITCH
    for b in range(PERM):
        stage[g, pl.ds(b, rows, stride=STAGE_PITCH), :] = src_ref[b // PJ, b % PJ, :, sl].astype(F32)


def _flip(a, f):
    return 1 - a if f else a


def gather_weights(w_in, w_out, conv_w, seq):
    d_model, width = w_in.shape
    rows = w_out.shape[0]
    cw = jnp.zeros((SUBLANES, LANES), F32).at[:CONV_K, :conv_w.shape[1]].set(conv_w)
    half_dim = HEAD_DIM // 2
    inv_freq = ROPE_THETA ** (-jnp.arange(half_dim, dtype=F32) * 2.0 / HEAD_DIM)
    inv_freq = jnp.tile(inv_freq, LANES // half_dim).reshape(1, LANES)
    chunk = min(ROW_TILE, seq)

    def body(win_ref, wout_ref, cw_ref, freq_ref, winf_ref, woutf_ref, cwf_ref, cos_ref, s1_ref, s2_ref,
             st_in, st_out, near_send, near_recv, far_send, far_recv, cw_send, cw_recv, d2d_send, d2d_recv):
        x, y, c = lax.axis_index("x"), lax.axis_index("y"), lax.axis_index("c")
        me = 2 * x + y
        sib = (x, y, 1 - c)
        st_in[...] = win_ref[...].astype(BF16)
        st_out[...] = wout_ref[...].astype(BF16)
        winf_ref[me] = st_in[...]
        woutf_ref[me] = st_out[...]
        cwf_ref[me] = cw_ref[...]
        stages = (st_in, st_out)
        fulls = (winf_ref, woutf_ref)
        halves = (d_model // 2, rows // 2)

        def part(t, core, q=None):
            size = halves[t] if q is None else halves[t] // 2
            start = core * halves[t] if q is None else core * halves[t] + q * size
            return pl.ds(pl.multiple_of(start, size), size)

        near = [(1 - x, y), (x, 1 - y)]
        far = (1 - x, 1 - y)
        chip = lambda px, py: 2 * px + py

        def direct(k, t, q, slot, to):
            src = stages[t].at[part(t, c, q)]
            return pltpu.make_async_remote_copy(src_ref=src, dst_ref=fulls[t].at[slot, part(t, c, q)], send_sem=near_send.at[k, t, q],
                                                recv_sem=near_recv.at[k, t, q], device_id=to, device_id_type=MESH)

        def passed_on(k, t, slot, to):
            ref = fulls[t].at[slot, part(t, c, k)]
            return pltpu.make_async_remote_copy(src_ref=ref, dst_ref=ref, send_sem=far_send.at[k, t], recv_sem=far_recv.at[k, t],
                                                device_id=to, device_id_type=MESH)

        def conv_copy(k, slot, to):
            return pltpu.make_async_remote_copy(src_ref=cw_ref, dst_ref=cwf_ref.at[slot], send_sem=cw_send.at[k], recv_sem=cw_recv.at[k],
                                                device_id=to, device_id_type=MESH)

        def d2d(k, t, slot, core):
            ref = fulls[t].at[slot, part(t, core)]
            return pltpu.make_async_remote_copy(src_ref=ref, dst_ref=ref, send_sem=d2d_send.at[k, t], recv_sem=d2d_recv.at[k, t],
                                                device_id=sib, device_id_type=MESH)

        sends = []

        def go(cp):
            cp.start()
            sends.append(cp)

        for q_first in (0, 1):
            for k, (px, py) in enumerate(near):
                for t in range(2):
                    go(direct(k, t, k if q_first == 0 else 1 - k, me, (px, py, c)))
        for k, (px, py) in enumerate(near + [far]):
            go(conv_copy(k, me, (px, py, c)))
        for k, (px, py) in enumerate(near):
            other = near[1 - k]
            for t in range(2):
                direct(k, t, k, chip(px, py), (px, py, c)).wait_recv()
                go(passed_on(k, t, chip(px, py), (*other, c)))

        first_half = lax.broadcasted_iota(jnp.int32, (chunk, LANES), 1) % HEAD_DIM < half_dim
        row = lax.broadcasted_iota(jnp.int32, (chunk, LANES), 0)

        def table_rows(i, carry):
            at = pl.multiple_of(i * chunk, chunk)
            ang = (row + at).astype(F32) * freq_ref[...]
            sin = jnp.sin(ang)
            cos_ref[pl.ds(at, chunk), :] = jnp.cos(ang)
            s1_ref[pl.ds(at, chunk), :] = jnp.where(first_half, -sin, 0.0)
            s2_ref[pl.ds(at, chunk), :] = jnp.where(first_half, 0.0, sin)
            return carry

        lax.fori_loop(0, seq // chunk, table_rows, 0)

        for k, (px, py) in enumerate(near):
            for t in range(2):
                direct(k, t, 1 - k, chip(px, py), (px, py, c)).wait_recv()
                go(d2d(k, t, chip(px, py), c))
        for t in range(2):
            for k, (px, py) in enumerate(near):
                passed_on(k, t, chip(*far), (px, py, c)).wait_recv()
            go(d2d(2, t, chip(*far), c))
        for k, (px, py) in enumerate(near + [far]):
            conv_copy(k, chip(px, py), (px, py, c)).wait_recv()
            for t in range(2):
                d2d(k, t, chip(px, py), 1 - c).wait_recv()
        for cp in sends:
            cp.wait_send()

    vm = pl.BlockSpec(memory_space=pltpu.VMEM)
    dma = pltpu.SemaphoreType.DMA
    return pl.pallas_call(
        body, name="gather_weights",
        out_shape=(jax.ShapeDtypeStruct((N_CHIPS, d_model, width), BF16),
                   jax.ShapeDtypeStruct((N_CHIPS, rows, d_model), BF16),
                   jax.ShapeDtypeStruct((N_CHIPS, SUBLANES, LANES), F32),
                   *[jax.ShapeDtypeStruct((seq, LANES), F32)] * 3),
        in_specs=[vm, vm, vm, vm], out_specs=(vm,) * 6,
        scratch_shapes=[pltpu.VMEM((d_model, width), BF16), pltpu.VMEM((rows, d_model), BF16),
                        dma((2, 2, 2)), dma((2, 2, 2)), dma((2, 2)), dma((2, 2)), dma((3,)), dma((3,)),
                        dma((3, 2)), dma((3, 2))],
        compiler_params=_params(),
    )(w_in, w_out, cw, inv_freq)


def _rope(t, cos, s1, s2):
    return t * cos + pltpu.roll(t, LANES - HEAD_DIM // 2, 1) * s1 + pltpu.roll(t, HEAD_DIM // 2, 1) * s2


def _rope_transposed(g, cos, s1, s2):
    return g * cos + pltpu.roll(g * s1, HEAD_DIM // 2, 1) + pltpu.roll(g * s2, LANES - HEAD_DIM // 2, 1)


def inproj(x, g1, w_full, tables, attn_w, conv_w):
    seq, d_model = x.shape
    width = w_full.shape[2]
    tm = ROW_TILE
    groups = _lane_groups(attn_w)

    def body(x_ref, g_ref, w_ref, cos_ref, s1_ref, s2_ref,
             ht_ref, q_ref, k_ref, v_ref, qp_ref, kp_ref, vp_ref, ga_ref, cz_ref, stage):
        xv = x_ref[...]
        hb = ((xv * lax.rsqrt(jnp.mean(xv * xv, axis=-1, keepdims=True) + NORM_EPS)) * g_ref[...]).astype(BF16)
        ht_ref[...] = jnp.transpose(hb)
        cos, s1, s2 = cos_ref[...], s1_ref[...], s2_ref[...]

        def proj(a, b):
            parts = [jnp.dot(hb, w_ref[j, :, lo:hi], preferred_element_type=F32) for j, lo, hi in _col_pieces(a, b, width)]
            return parts[0] if len(parts) == 1 else jnp.concatenate(parts, axis=1)

        def emit(z, nat_ref, perm_ref, fn):
            for g, sl in enumerate(groups):
                val = fn(z[:, sl])
                nat_ref[:, sl] = val.astype(BF16)
                _stage_put(stage, g, val)
            for g, sl in enumerate(groups):
                _to_perm(stage, g, perm_ref, sl, BF16)

        emit(proj(0, attn_w), q_ref, qp_ref, lambda t: _rope(t, cos, s1, s2) * ATTN_SCALE)
        emit(proj(attn_w, 2 * attn_w), k_ref, kp_ref, lambda t: _rope(t, cos, s1, s2))
        emit(proj(2 * attn_w, 3 * attn_w), v_ref, vp_ref, lambda t: t)
        ga_ref[...] = proj(3 * attn_w, 4 * attn_w)
        cz_ref[...] = proj(4 * attn_w, 4 * attn_w + 4 * conv_w)

    row = lambda n: pl.BlockSpec((tm, n), lambda i: (i, 0))
    nat = jax.ShapeDtypeStruct((seq, attn_w), BF16)
    perm = jax.ShapeDtypeStruct(_perm_shape(seq, attn_w), BF16)
    return pl.pallas_call(
        body, name="inproj", grid=(seq // tm,),
        out_shape=(jax.ShapeDtypeStruct((d_model, seq), BF16), nat, nat, nat, perm, perm, perm,
                   jax.ShapeDtypeStruct((seq, attn_w), F32), jax.ShapeDtypeStruct((seq, 4 * conv_w), F32)),
        in_specs=[row(d_model), _const_spec((1, d_model)), _const_spec(w_full.shape), row(LANES), row(LANES), row(LANES)],
        out_specs=(pl.BlockSpec((d_model, tm), lambda i: (0, i)), row(attn_w), row(attn_w), row(attn_w),
                   _perm_tile_spec(attn_w, tm), _perm_tile_spec(attn_w, tm), _perm_tile_spec(attn_w, tm),
                   row(attn_w), row(4 * conv_w)),
        scratch_shapes=[pltpu.VMEM(_stage_shape(len(groups), tm), F32)],
        compiler_params=_params(("arbitrary",)),
    )(x, g1, w_full, *tables)


class _Mode:
    def __init__(self, name, seq):
        self.name = name
        if name == "nat":
            self.residues, blocks = 1, seq // BLOCK
        elif name == "p16":
            self.residues, blocks = PERM, seq // PERM // BLOCK
        else:
            self.residues, blocks = PJ, seq // PERM // P4_ROWS
        self.qb = max(d for d in range(1, MAX_QUERY_BLOCKS + 1) if blocks % d == 0)
        self.steps = blocks // self.qb

    def _spec(self, blocks, width, at):
        if self.name == "nat":
            return pl.BlockSpec((blocks * BLOCK, width), lambda *g: (at(*g)[1], 0))
        if self.name == "p16":
            return pl.BlockSpec((1, 1, blocks * BLOCK, width), lambda *g: (at(*g)[0] // PJ, at(*g)[0] % PJ, at(*g)[1], 0))
        return pl.BlockSpec((PJ, 1, blocks * P4_ROWS, width), lambda *g: (0, at(*g)[0], at(*g)[1], 0))

    def wide(self, width, where=lambda r, n: (r, n)):
        return self._spec(self.qb, width, where)

    def block_before(self, width, where=lambda r, n: (r, n)):
        return self._spec(1, width, lambda *g: (where(*g)[0], jnp.maximum(self.qb * where(*g)[1] - 1, 0)))

    def get(self, ref, sl, sub=0):
        if self.name == "nat":
            return ref[sub * BLOCK:(sub + 1) * BLOCK, sl]
        if self.name == "p16":
            return ref[0, 0, sub * BLOCK:(sub + 1) * BLOCK, sl]
        return jnp.concatenate([ref[j, 0, at:at + P4_ROWS // 2, sl] for j, at in self._p4_chunks(sub)], axis=0)

    def put(self, ref, sl, val, sub=0):
        val = val.astype(ref.dtype)
        if self.name == "nat":
            ref[sub * BLOCK:(sub + 1) * BLOCK, sl] = val
        elif self.name == "p16":
            ref[0, 0, sub * BLOCK:(sub + 1) * BLOCK, sl] = val
        else:
            for i, (j, at) in enumerate(self._p4_chunks(sub)):
                ref[j, 0, at:at + P4_ROWS // 2, sl] = val[i * (P4_ROWS // 2):(i + 1) * (P4_ROWS // 2)]

    @staticmethod
    def _p4_chunks(sub):
        return [(j, sub * P4_ROWS + half * (P4_ROWS // 2)) for half in (0, 1) for j in range(PJ)]

    def keys(self, before_ref, wide_ref, sl, sub):
        older = self.get(before_ref, sl) if sub == 0 else self.get(wide_ref, sl, sub - 1)
        return jnp.concatenate([older, self.get(wide_ref, sl, sub)], axis=0)

    def index(self, idx, is_key):
        if self.name != "p4":
            return idx - BLOCK if is_key else idx
        within = jnp.bitwise_and(idx, BLOCK - 1)
        chunk = P4_ROWS // 2
        half = jnp.right_shift(within, HALF_BLOCK.bit_length() - 1)
        j = jnp.bitwise_and(jnp.right_shift(within, chunk.bit_length() - 1), PJ - 1)
        m = PJ * (chunk * half + jnp.bitwise_and(within, chunk - 1)) + j
        return m + BLOCK * (jnp.right_shift(idx, BLOCK.bit_length() - 1) - 1) if is_key else m

    def bias(self, has_before):
        shape = (2 * BLOCK, BLOCK)
        kidx = lax.broadcasted_iota(jnp.int32, shape, 0)
        qidx = lax.broadcasted_iota(jnp.int32, shape, 1)
        rel = self.index(qidx, False) - self.index(kidx, True)
        valid = (rel >= 0) & (rel <= WINDOW_KEYS)
        if has_before is not True:
            valid = valid & ((kidx >= BLOCK) | has_before)
        one = jnp.where(valid, 0.0, NEG)
        return jnp.concatenate([one, one], axis=1)

    def live_keys(self, half):
        return (0, 2 * BLOCK - HALF_BLOCK) if half == 0 else (HALF_BLOCK, 2 * BLOCK)

    def half_bias(self, has_before, half):
        r0, r1 = self.live_keys(half)
        shape = (r1 - r0, LANES)
        kidx = lax.broadcasted_iota(jnp.int32, shape, 0) + r0
        qidx = jnp.bitwise_and(lax.broadcasted_iota(jnp.int32, shape, 1), HALF_BLOCK - 1) + half * HALF_BLOCK
        rel = self.index(qidx, False) - self.index(kidx, True)
        valid = (rel >= 0) & (rel <= WINDOW_KEYS)
        if has_before is not True:
            valid = valid & ((kidx >= BLOCK) | has_before)
        return jnp.where(valid, 0.0, NEG)


def _head_masks():
    lane = lax.broadcasted_iota(jnp.int32, (BLOCK, LANES), 1)
    lo = lane < HEAD_DIM
    return lane, lo, jnp.where(lo, 1.0, 0.0).astype(BF16), jnp.where(lo, 0.0, 1.0).astype(BF16)


def attn_fwd(name, q, k, v, run):
    nat = name == "nat"
    seq = q.shape[0] if nat else q.shape[2] * PERM
    attn_w = q.shape[-1]
    mode = _Mode(name, seq)
    groups = _lane_groups(attn_w)
    first = run is None
    all_lanes = slice(0, LANES)

    def body(*refs):
        q_ref, kp_ref, kc_ref, vp_ref, vc_ref = refs[:5]
        if first:
            o_ref, l_ref = refs[5:]
        elif nat:
            oin_ref, lin_ref, o_ref, l_ref, ostage, lstage = refs[5:]
        else:
            oin_ref, lin_ref, o_ref, l_ref = refs[5:]
        n = pl.program_id(1)
        subs = range(mode.qb)
        halves = (0, 1)
        live = [mode.live_keys(x) for x in halves]
        always = [mode.half_bias(True, x) for x in halves]
        biases = [[mode.half_bias(n > 0, x) for x in halves]] + [always] * (mode.qb - 1)
        _, lo, m_lo, m_hi = _head_masks()
        head_row = lax.broadcasted_iota(jnp.int32, (BLOCK, LANES), 0)
        ones = jnp.ones((2 * BLOCK, LANES), BF16)
        hb = HALF_BLOCK
        lrows = [jnp.zeros((BLOCK, LANES), F32) for _ in subs]
        if not first:
            if nat:
                for g, sl in enumerate(groups):
                    _from_perm(oin_ref, sl, ostage, g)
                _from_perm(lin_ref, all_lanes, lstage, 0)
            wide_rows = lambda a, sub: a[sub * BLOCK:(sub + 1) * BLOCK]
            before = [jnp.transpose(wide_rows(_stage_get(lstage, 0), sub) if nat else mode.get(lin_ref, all_lanes, sub))
                      for sub in subs]

        def probs(sub, p, sl):
            q2 = mode.get(q_ref, sl, sub)
            kcat = mode.keys(kp_ref, kc_ref, sl, sub)
            vcat = mode.keys(vp_ref, vc_ref, sl, sub)
            q_lo, q_hi = q2 * m_lo, q2 * m_hi
            qq = jnp.concatenate([q_lo[:hb], q_hi[:hb], q_lo[hb:], q_hi[hb:]], axis=0)
            s_t = _nt(kcat, qq)
            columns, lses = [], []
            for x in halves:
                r0, r1 = live[x]
                s_x = s_t[r0:r1, x * LANES:(x + 1) * LANES] + biases[sub][x]
                m = jnp.max(s_x, axis=0, keepdims=True)
                pe = jnp.exp(s_x - m)
                lse = m + jnp.log(jnp.sum(pe, axis=0, keepdims=True))
                if not first:
                    was = jnp.concatenate([before[sub][2 * p:2 * p + 1, x * hb:(x + 1) * hb],
                                           before[sub][2 * p + 1:2 * p + 2, x * hb:(x + 1) * hb]], axis=1)
                    top = jnp.maximum(was, lse)
                    lse = top + jnp.log(jnp.exp(was - top) + jnp.exp(lse - top))
                    pe = pe * jnp.exp(m - lse)
                pieces = [pe.astype(BF16)]
                if r0 > 0:
                    pieces.insert(0, jnp.zeros((r0, LANES), BF16))
                if r1 < 2 * BLOCK:
                    pieces.append(jnp.zeros((2 * BLOCK - r1, LANES), BF16))
                columns.append(pieces[0] if len(pieces) == 1 else jnp.concatenate(pieces, axis=0))
                lses.append(lse)
            return jnp.concatenate([vcat, ones], axis=1), jnp.concatenate(columns, axis=1), lses

        def output(sub, p, sl, vext, pb, lses):
            o_ext = _tn(pb, vext)
            if first:
                o_new = o_ext[:, :LANES] / o_ext[:, LANES:]
            else:
                o_prev = wide_rows(_stage_get(ostage, p), sub) if nat else mode.get(oin_ref, sl, sub)
                same = jnp.concatenate([o_prev[:hb], o_prev[:hb], o_prev[hb:], o_prev[hb:]], axis=0)
                o_new = o_ext[:, :LANES] + same * (1.0 - o_ext[:, LANES:])
            head_lo = jnp.concatenate([o_new[:hb], o_new[2 * hb:3 * hb]], axis=0)
            head_hi = jnp.concatenate([o_new[hb:2 * hb], o_new[3 * hb:]], axis=0)
            mode.put(o_ref, sl, jnp.where(lo, head_lo, head_hi), sub)
            lse_lo = jnp.concatenate([lses[0][:, :hb], lses[1][:, :hb]], axis=1)
            lse_hi = jnp.concatenate([lses[0][:, hb:], lses[1][:, hb:]], axis=1)
            rows = jnp.where(head_row == 2 * p, lse_lo, lrows[sub])
            lrows[sub] = jnp.where(head_row == 2 * p + 1, lse_hi, rows)

        pending = None
        for sub in subs:
            for p, sl in enumerate(groups):
                nxt = probs(sub, p, sl)
                if pending is not None:
                    output(*pending)
                pending = (sub, p, sl, *nxt)
        output(*pending)
        for sub in subs:
            mode.put(l_ref, all_lanes, jnp.transpose(lrows[sub]), sub)

    ins = [q, k, k, v, v]
    specs = [mode.wide(attn_w), mode.block_before(attn_w), mode.wide(attn_w), mode.block_before(attn_w), mode.wide(attn_w)]
    scratch = []
    if not first:
        ins += list(run)
        if nat:
            rows_a = mode.qb * BLOCK // PERM
            specs += [pl.BlockSpec((PJ, PJ, rows_a, attn_w), lambda r, n: (0, 0, n, 0)),
                      pl.BlockSpec((PJ, PJ, rows_a, LANES), lambda r, n: (0, 0, n, 0))]
            scratch = [pltpu.VMEM(_stage_shape(len(groups), mode.qb * BLOCK), F32),
                       pltpu.VMEM(_stage_shape(1, mode.qb * BLOCK), F32)]
        else:
            specs += [mode.wide(attn_w), mode.wide(LANES)]
    if nat:
        out_shape = (jax.ShapeDtypeStruct((seq, attn_w), F32), jax.ShapeDtypeStruct((seq, LANES), F32))
    else:
        out_shape = (jax.ShapeDtypeStruct(_perm_shape(seq, attn_w), F32), jax.ShapeDtypeStruct(_perm_shape(seq, LANES), F32))
    return pl.pallas_call(
        body, name=f"attn_fwd_{name}", grid=(mode.residues, mode.steps),
        out_shape=out_shape, in_specs=specs, out_specs=(mode.wide(attn_w), mode.wide(LANES)),
        scratch_shapes=scratch,
        compiler_params=_params(("arbitrary", "arbitrary")),
    )(*ins)


def attn_bwd(name, q, k, v, d_o, lse, delta, run):
    nat = name == "nat"
    seq = q.shape[0] if nat else q.shape[2] * PERM
    attn_w = q.shape[-1]
    mode = _Mode(name, seq)
    steps, qb = mode.steps, mode.qb
    single = steps == 1
    groups = _lane_groups(attn_w)
    first = run is None
    all_lanes = slice(0, LANES)

    def body(*refs):
        q_ref, kp_ref, kc_ref, vp_ref, vc_ref, do_ref, lse_ref, dl_ref = refs[:8]
        if first:
            dq_ref, dk_ref, dv_ref, ck, cv = refs[8:]
        else:
            dqi_ref, dki_ref, dvi_ref, dq_ref, dk_ref, dv_ref, ck, cv = refs[8:]
        g = pl.program_id(1) if single else pl.program_id(0)
        n = g if single else lax.rem(g, steps)
        carries = ((ck, dk_ref, None if first else dki_ref), (cv, dv_ref, None if first else dvi_ref))

        def emit(out_ref, acc_ref, sl, sub, val):
            if acc_ref is not None:
                val = val + mode.get(acc_ref, sl, sub).astype(F32)
            mode.put(out_ref, sl, val, sub)

        if not single:
            @pl.when(g == 0)
            def _():
                ck[...] = jnp.zeros_like(ck)
                cv[...] = jnp.zeros_like(cv)

        @pl.when(g < total)
        def _():
            biases = [mode.bias(n > 0)] + [mode.bias(True)] * (qb - 1)
            _, lo, m_lo, m_hi = _head_masks()

            def scores(sub, p, sl, lse_t, dl_t):
                q2, do2 = mode.get(q_ref, sl, sub), mode.get(do_ref, sl, sub)
                kcat = mode.keys(kp_ref, kc_ref, sl, sub)
                vcat = mode.keys(vp_ref, vc_ref, sl, sub)
                qq = jnp.concatenate([q2 * m_lo, q2 * m_hi], axis=0)
                dd = jnp.concatenate([do2 * m_lo, do2 * m_hi], axis=0)
                h0 = 2 * p
                lse2 = jnp.concatenate([lse_t[h0:h0 + 1, :], lse_t[h0 + 1:h0 + 2, :]], axis=1)
                dl2 = jnp.concatenate([dl_t[h0:h0 + 1, :], dl_t[h0 + 1:h0 + 2, :]], axis=1)
                p_t = jnp.exp(_nt(kcat, qq) + (biases[sub] - lse2))
                ds_t = p_t * (_nt(vcat, dd) - dl2)
                return qq, dd, kcat, p_t.astype(BF16), ds_t.astype(BF16)

            def grads(sub, sl, qq, dd, kcat, pb, dsb):
                dqb = _tn(dsb, kcat)
                dq2 = jnp.where(lo, dqb[:BLOCK], dqb[BLOCK:]) * ATTN_SCALE
                if not first:
                    dq2 = dq2 + mode.get(dqi_ref, sl, sub).astype(F32)
                mode.put(dq_ref, sl, dq2, sub)
                for (carry, out_ref, acc_ref), lhs, rhs in zip(carries, (dsb, pb), (qq, dd)):
                    both = jnp.dot(lhs, rhs, preferred_element_type=F32)
                    if sub == 0:
                        if not single:
                            for s in range(qb - 1):
                                emit(out_ref, acc_ref, sl, s, carry[s, :, sl])
                            emit(out_ref, acc_ref, sl, qb - 1, carry[qb - 1, :, sl] + both[:BLOCK])
                        carry[0, :, sl] = both[BLOCK:]
                    else:
                        carry[sub - 1, :, sl] += both[:BLOCK]
                        carry[sub, :, sl] = both[BLOCK:]
                    if single and sub == qb - 1:
                        for s in range(qb):
                            emit(out_ref, acc_ref, sl, s, carry[s, :, sl])

            stats = [(jnp.transpose(mode.get(lse_ref, all_lanes, sub)),
                      jnp.transpose(mode.get(dl_ref, all_lanes, sub))) for sub in range(qb)]
            pending = None
            for p, sl in enumerate(groups):
                for sub in range(qb):
                    nxt = scores(sub, p, sl, *stats[sub])
                    if pending is not None:
                        grads(*pending)
                    pending = (sub, sl, *nxt)
            grads(*pending)

        if not single:
            @pl.when(g == total)
            def _():
                for carry, out_ref, acc_ref in carries:
                    for sl in groups:
                        for s in range(qb):
                            emit(out_ref, acc_ref, sl, s, carry[s, :, sl])

    total = mode.residues * steps
    if single:
        here = before = lambda r, n: (r, n)
    else:
        locate = lambda g: (g // steps, lax.rem(g, steps))
        here = lambda g: locate(jnp.minimum(g, total - 1))
        before = lambda g: locate(jnp.maximum(g - 1, 0))
    wide = lambda w: mode.wide(w, here)
    ins = [q, k, k, v, v, d_o, lse, delta]
    specs = [wide(attn_w), mode.block_before(attn_w, here), wide(attn_w), mode.block_before(attn_w, here), wide(attn_w),
             wide(attn_w), wide(LANES), wide(LANES)]
    if not first:
        ins += list(run)
        specs += [wide(attn_w), mode.wide(attn_w, before), mode.wide(attn_w, before)]
    shp = jax.ShapeDtypeStruct((seq, attn_w) if nat else _perm_shape(seq, attn_w), BF16)
    grid = (mode.residues, 1) if single else (total + 1,)
    return pl.pallas_call(
        body, name=f"attn_bwd_{name}", grid=grid,
        out_shape=(shp, shp, shp), in_specs=specs,
        out_specs=(wide(attn_w), mode.wide(attn_w, before), mode.wide(attn_w, before)),
        scratch_shapes=[pltpu.VMEM((qb, BLOCK, attn_w), F32), pltpu.VMEM((qb, BLOCK, attn_w), F32)],
        compiler_params=_params(("arbitrary",) * len(grid)),
    )(*ins)


def _shift_down(u, halo, k):
    rolled = pltpu.roll(u, k, 0)
    row = lax.broadcasted_iota(jnp.int32, halo.shape, 0)
    top = jnp.where(row < k, pltpu.roll(halo, k, 0), rolled[:SUBLANES])
    return jnp.concatenate([top, rolled[SUBLANES:]], axis=0)


def _shift_up(u, halo, k):
    rows = u.shape[0]
    rolled = pltpu.roll(u, rows - k, 0)
    row = lax.broadcasted_iota(jnp.int32, halo.shape, 0)
    bot = jnp.where(row >= SUBLANES - k, pltpu.roll(halo, SUBLANES - k, 0), rolled[rows - SUBLANES:])
    return jnp.concatenate([rolled[:rows - SUBLANES], bot], axis=0)


def tail(o, lse, ga, cz, x, tgt, w_out, g2, cw):
    seq, d_model = x.shape
    attn_w = o.shape[1]
    conv_w = cz.shape[1] // 4
    mix = attn_w + conv_w
    groups = _lane_groups(attn_w)
    tm = ROW_TILE
    nt = seq // tm
    hb = tm // SUBLANES

    def body(o_ref, l_ref, ga_ref, cz_ref, hz_ref, x_ref, t_ref, w_ref, g_ref, cw_ref,
             do_ref, dl_ref, dop_ref, dlp_ref, lp_ref, dga_ref, dcb_ref, dgc_ref, dcv_ref, e_ref,
             dw_ref, dg_ref, dcw_ref, loss_ref, stage):
        i = pl.program_id(0)

        @pl.when(i == 0)
        def _():
            dw_ref[...] = jnp.zeros_like(dw_ref)
            dg_ref[...] = jnp.zeros_like(dg_ref)
            dcw_ref[...] = jnp.zeros_like(dcw_ref)
            loss_ref[...] = jnp.zeros_like(loss_ref)

        u = cz_ref[:, 2 * conv_w:3 * conv_w] * cz_ref[:, 0:conv_w]
        uh = hz_ref[:, 2 * conv_w:3 * conv_w] * hz_ref[:, 0:conv_w]
        uh = jnp.where(i > 0, uh, 0.0)
        u1 = _shift_down(u, uh, 1)
        u2 = _shift_down(u, uh, 2)
        w0, w1, w2 = cw_ref[0:1, :], cw_ref[1:2, :], cw_ref[2:3, :]
        cvv = u2 * w0 + u1 * w1 + u * w2
        gv = g_ref[...]
        all_lanes = slice(0, LANES)

        def forward(rs):
            ov, gav = o_ref[rs, :], ga_ref[rs, :]
            sig_a = _sigmoid(gav)
            silu_a = gav * sig_a
            cb, gc = cz_ref[rs, conv_w:2 * conv_w], cz_ref[rs, 3 * conv_w:4 * conv_w]
            sig_c = _sigmoid(gc)
            silu_c = gc * sig_c
            bc = cb * cvv[rs]
            mixed = jnp.concatenate([ov * silu_a, bc * silu_c], axis=1).astype(BF16)
            yv = jnp.dot(mixed, w_ref[...], preferred_element_type=F32)
            return ov, gav, sig_a, silu_a, cb, gc, sig_c, silu_c, bc, mixed, yv

        def loss_and_dy(rs, mixed, yv):
            r2 = lax.rsqrt(jnp.mean(yv * yv, axis=-1, keepdims=True) + NORM_EPS)
            yhat = yv * r2
            diff = (x_ref[rs, :] + yhat * gv) - t_ref[rs, :]
            loss_ref[...] += _rowgroup_sum(diff * diff)
            ev = diff * (1.0 / d_model)
            e_ref[rs, :] = ev
            dg_ref[...] += _rowgroup_sum(ev * yhat)
            eg = ev * gv
            dy = (r2 * (eg - yhat * jnp.mean(eg * yhat, axis=-1, keepdims=True))).astype(BF16)
            dw_ref[...] += _tn(mixed, dy)
            return _nt(dy, w_ref[...])

        def backward(rs, ov, gav, sig_a, silu_a, cb, gc, sig_c, silu_c, bc, dm):
            rows = rs.stop - rs.start
            dma, dmc = dm[:, :attn_w], dm[:, attn_w:]
            dov = dma * silu_a
            do_ref[rs, :] = dov.astype(BF16)
            dga_ref[rs, :] = (dma * ov * (sig_a * (1.0 + gav * (1.0 - sig_a)))).astype(BF16)
            prod = dov * ov
            lane = lax.broadcasted_iota(jnp.int32, (rows, LANES), 1)
            lo = lane < HEAD_DIM
            dblk = jnp.zeros((rows, LANES), F32)
            for p, sl in enumerate(groups):
                pr = prod[:, sl]
                dblk = jnp.where(lane == 2 * p, jnp.sum(jnp.where(lo, pr, 0.0), axis=1, keepdims=True), dblk)
                dblk = jnp.where(lane == 2 * p + 1, jnp.sum(jnp.where(lo, 0.0, pr), axis=1, keepdims=True), dblk)
                _stage_put(stage, p, dov[:, sl], rs.start)
            dl_ref[rs, :] = dblk
            _stage_put(stage, len(groups), dblk, rs.start)
            _stage_put(stage, len(groups) + 1, l_ref[rs, :], rs.start)
            dsc = dmc * silu_c
            cv_rows = cvv[rs]
            dcb_ref[rs, :] = (dsc * cv_rows).astype(BF16)
            dgc_ref[rs, :] = (dmc * bc * (sig_c * (1.0 + gc * (1.0 - sig_c)))).astype(BF16)
            dcv = dsc * cb
            dcv_ref[rs, :] = dcv
            dcw_ref[0:SUBLANES, :] += _rowgroup_sum(dcv * u2[rs])
            dcw_ref[SUBLANES:2 * SUBLANES, :] += _rowgroup_sum(dcv * u1[rs])
            dcw_ref[2 * SUBLANES:3 * SUBLANES, :] += _rowgroup_sum(dcv * u[rs])

        halves = [slice(0, tm // 2), slice(tm // 2, tm)]
        fwd = [forward(rs) for rs in halves]
        dms = [loss_and_dy(rs, f[9], f[10]) for rs, f in zip(halves, fwd)]
        for rs, f, dm in zip(halves, fwd, dms):
            backward(rs, *f[:9], dm)
        for p, sl in enumerate(groups):
            _to_perm(stage, p, dop_ref, sl, BF16)
        _to_perm(stage, len(groups), dlp_ref, all_lanes, F32)
        _to_perm(stage, len(groups) + 1, lp_ref, all_lanes, F32)

    row = lambda n: pl.BlockSpec((tm, n), lambda i: (i, 0))
    whole = lambda a, b: pl.BlockSpec((a, b), lambda i: (0, 0))
    return pl.pallas_call(
        body, name="tail", grid=(nt,),
        out_shape=(jax.ShapeDtypeStruct((seq, attn_w), BF16), jax.ShapeDtypeStruct((seq, LANES), F32),
                   jax.ShapeDtypeStruct(_perm_shape(seq, attn_w), BF16), jax.ShapeDtypeStruct(_perm_shape(seq, LANES), F32),
                   jax.ShapeDtypeStruct(_perm_shape(seq, LANES), F32),
                   jax.ShapeDtypeStruct((seq, attn_w), BF16), jax.ShapeDtypeStruct((seq, conv_w), BF16),
                   jax.ShapeDtypeStruct((seq, conv_w), BF16), jax.ShapeDtypeStruct((seq, conv_w), F32),
                   jax.ShapeDtypeStruct((seq, d_model), F32), jax.ShapeDtypeStruct((mix, d_model), F32),
                   jax.ShapeDtypeStruct((SUBLANES, d_model), F32), jax.ShapeDtypeStruct((CONV_K * SUBLANES, conv_w), F32),
                   jax.ShapeDtypeStruct((SUBLANES, d_model), F32)),
        in_specs=[row(attn_w), row(LANES), row(attn_w), row(4 * conv_w),
                  pl.BlockSpec((SUBLANES, 4 * conv_w), lambda i: (jnp.maximum(i * hb - 1, 0), 0)),
                  row(d_model), row(d_model), _const_spec((mix, d_model)), _const_spec((1, d_model)),
                  _const_spec((SUBLANES, conv_w))],
        out_specs=(row(attn_w), row(LANES), _perm_tile_spec(attn_w, tm), _perm_tile_spec(LANES, tm), _perm_tile_spec(LANES, tm),
                   row(attn_w), row(conv_w), row(conv_w), row(conv_w), row(d_model),
                   whole(mix, d_model), whole(SUBLANES, d_model), whole(CONV_K * SUBLANES, conv_w),
                   whole(SUBLANES, d_model)),
        scratch_shapes=[pltpu.VMEM(_stage_shape(len(groups) + 2, tm), F32)],
        compiler_params=_params(("arbitrary",)),
    )(o, lse, ga, cz, cz, x, tgt, w_out, g2, cw)


def dz_dx(nat_grads, perm_grads, dga, dcb, dgc, dcv, cz, tables, x, g1, e, w_full, cw):
    seq, d_model = x.shape
    attn_w = dga.shape[1]
    conv_w = dcv.shape[1]
    width = w_full.shape[2]
    in_w = 4 * attn_w + 4 * conv_w
    groups = _lane_groups(attn_w)
    tm = ROW_TILE
    nt = seq // tm
    hb = tm // SUBLANES

    def body(dq_ref, dk_ref, dv_ref, dqp_ref, dkp_ref, dvp_ref, dga_ref, dcb_ref, dgc_ref, dcv_ref, nh_ref, ch_ref, cc_ref,
             cos_ref, s1_ref, s2_ref, x_ref, g_ref, e_ref, w_ref, cw_ref, gx_ref, dz_ref, dg_ref, stage):
        i = pl.program_id(0)

        @pl.when(i == 0)
        def _():
            dg_ref[...] = jnp.zeros_like(dg_ref)

        cos, s1, s2 = cos_ref[...], s1_ref[...], s2_ref[...]

        def qkv_columns(t, nat_ref, perm_ref):
            for g, sl in enumerate(groups):
                _from_perm(perm_ref, sl, stage, g)
            for g, sl in enumerate(groups):
                tot = nat_ref[:, sl].astype(F32) + _stage_get(stage, g)
                if t < 2:
                    tot = _rope_transposed(tot, cos, s1, s2)
                dz_ref[:, t * attn_w + g * LANES:t * attn_w + (g + 1) * LANES] = tot.astype(BF16)

        def dh_part(j):
            return _nt(dz_ref[:, j * width:(j + 1) * width], w_ref[j])

        dcv = dcv_ref[...]
        nh = jnp.where(i < nt - 1, nh_ref[...], 0.0)
        w0, w1, w2 = cw_ref[0:1, :], cw_ref[1:2, :], cw_ref[2:3, :]
        du = dcv * w2 + _shift_up(dcv, nh, 1) * w1 + _shift_up(dcv, nh, 2) * w0
        base = 4 * attn_w
        dz_ref[:, base:base + conv_w] = (du * cc_ref[...]).astype(BF16)
        dz_ref[:, base + conv_w:base + 2 * conv_w] = dcb_ref[...]
        dz_ref[:, base + 2 * conv_w:base + 3 * conv_w] = (du * ch_ref[...]).astype(BF16)
        dz_ref[:, base + 3 * conv_w:base + 4 * conv_w] = dgc_ref[...]
        dz_ref[:, 3 * attn_w:4 * attn_w] = dga_ref[...]
        ready = in_w
        dh = None
        for t, nat_ref, perm_ref in ((2, dv_ref, dvp_ref), (1, dk_ref, dkp_ref), (0, dq_ref, dqp_ref), (None, None, None)):
            lowest_open = 0 if t is None else (t + 1) * attn_w
            while ready - width >= lowest_open:
                ready -= width
                part = dh_part(ready // width)
                dh = part if dh is None else dh + part
            if t is not None:
                qkv_columns(t, nat_ref, perm_ref)
        xv = x_ref[...]
        r1 = lax.rsqrt(jnp.mean(xv * xv, axis=-1, keepdims=True) + NORM_EPS)
        xhat = xv * r1
        dg_ref[...] += _rowgroup_sum(dh * xhat)
        dhg = dh * g_ref[...]
        gx_ref[...] = r1 * (dhg - xhat * jnp.mean(dhg * xhat, axis=-1, keepdims=True)) + e_ref[...]

    row = lambda n: pl.BlockSpec((tm, n), lambda i: (i, 0))
    whole = lambda a, b: pl.BlockSpec((a, b), lambda i: (0, 0))
    pt = _perm_tile_spec(attn_w, tm)
    return pl.pallas_call(
        body, name="dz_dx", grid=(nt,),
        out_shape=(jax.ShapeDtypeStruct((seq, d_model), F32), jax.ShapeDtypeStruct((seq, in_w), BF16),
                   jax.ShapeDtypeStruct((SUBLANES, d_model), F32)),
        in_specs=[row(attn_w), row(attn_w), row(attn_w), pt, pt, pt, row(attn_w), row(conv_w), row(conv_w), row(conv_w),
                  pl.BlockSpec((SUBLANES, conv_w), lambda i: (jnp.minimum((i + 1) * hb, seq // SUBLANES - 1), 0)),
                  pl.BlockSpec((tm, conv_w), lambda i: (i, 0)), pl.BlockSpec((tm, conv_w), lambda i: (i, 2)),
                  row(LANES), row(LANES), row(LANES), row(d_model), _const_spec((1, d_model)), row(d_model),
                  _const_spec(w_full.shape), _const_spec((SUBLANES, conv_w))],
        out_specs=(row(d_model), row(in_w), whole(SUBLANES, d_model)),
        scratch_shapes=[pltpu.VMEM(_stage_shape(len(groups), tm), F32)],
        compiler_params=_params(("arbitrary",)),
    )(*nat_grads, *perm_grads, dga, dcb, dgc, dcv, dcv, cz, cz, *tables, x, g1, e, w_full, cw)


def dw_in_reduce(ht, dz, g_out, small):
    d_model, seq = ht.shape
    half = dz.shape[1] // N_DEV
    ts = min(DW_ROWS, seq)
    steps = seq // ts
    x, y, c = lax.axis_index("x"), lax.axis_index("y"), lax.axis_index("c")
    far_first = lambda x, y: [(1 - x, 1 - y), (1 - x, y), (x, 1 - y)]
    chips = jnp.stack([2 * px + py for px, py in far_first(x, y)] + [2 * x + y]).astype(jnp.int32)
    order = jnp.stack([2 * chips + (1 - c), 2 * chips + c], axis=1).reshape(N_DEV)

    def body(order_ref, ht_ref, dz_ref, go_ref, sm_ref, out_ref, ro_ref, rs_ref,
             acc, theirs, staged, contrib, resbuf, out_sem, sa, ra, sb, rb, sc, rc,
             o_mine, o_theirs, o_staged, o_contrib, o_res, sbuf, o_load, osa, ora, osb, orb, osc, orc, ss, rs):
        del order_ref
        p, s = pl.program_id(0), pl.program_id(1)
        x, y, c = lax.axis_index("x"), lax.axis_index("y"), lax.axis_index("c")
        me = 2 * x + y
        sib = (x, y, 1 - c)
        peers = far_first(x, y)
        slot = p % 2

        flips = [(fx, fy, fc) for fx in (0, 1) for fy in (0, 1) for fc in (0, 1)][1:]
        my8 = 4 * x + 2 * y + c
        chip_ids = [2 * px + py for px, py in peers] + [me]

        def small_copy(k, slot8, to):
            return pltpu.make_async_remote_copy(src_ref=sm_ref, dst_ref=sbuf.at[slot8], send_sem=ss.at[k], recv_sem=rs.at[k],
                                                device_id=to, device_id_type=MESH)

        def small_peer(k):
            fx, fy, fc = flips[k]
            return _flip(x, fx), _flip(y, fy), _flip(c, fc)

        def oa_copy(pos):
            j = chip_ids[pos]
            return pltpu.make_async_remote_copy(src_ref=go_ref.at[j, 1 - c], dst_ref=o_theirs.at[j], send_sem=osa.at[pos],
                                                recv_sem=ora.at[pos], device_id=sib, device_id_type=MESH)

        def o_load_copy(pos):
            j = chip_ids[pos]
            return pltpu.make_async_copy(go_ref.at[j, c], o_mine.at[j], o_load.at[pos])

        def ob_copy(k, piece, slot4):
            px, py = peers[k]
            return pltpu.make_async_remote_copy(src_ref=o_staged.at[piece], dst_ref=o_contrib.at[slot4], send_sem=osb.at[k],
                                                recv_sem=orb.at[k], device_id=(px, py, c), device_id_type=MESH)

        def oc_copy(which):
            return pltpu.make_async_remote_copy(src_ref=o_res.at[which], dst_ref=o_res.at[which], send_sem=osc, recv_sem=orc,
                                                device_id=sib, device_id_type=MESH)

        @pl.when((p == 0) & (s == 0))
        def _():
            sbuf[my8] = sm_ref[...]
            for k in range(N_DEV - 1):
                small_copy(k, my8, small_peer(k)).start()
            for pos in range(N_CHIPS):
                o_load_copy(pos).start()
                oa_copy(pos).start()

        @pl.when((p == 1) & (s == steps - 1))
        def _():
            for pos in range(N_CHIPS):
                j = chip_ids[pos]
                o_load_copy(pos).wait()
                oa_copy(pos).wait_recv()
                if pos < N_CHIPS - 1:
                    o_staged[j] = (o_mine[j] + o_theirs[j]).astype(BF16)
                    ob_copy(pos, j, me).start()
                else:
                    o_mine[j] = o_mine[j] + o_theirs[j]
                    o_contrib[j] = o_mine[j].astype(BF16)

        @pl.when((p == 4) & (s == steps - 1))
        def _():
            for k in range(N_CHIPS - 1):
                ob_copy(k, me, chip_ids[k]).wait_recv()
            own = o_mine[me]
            term = lambda j: jnp.where(me == j, own, o_contrib[j].astype(F32))
            o_res[c] = ((term(0) + term(1)) + term(2)) + term(3)
            oc_copy(c).start()

        def a_copy(k):
            return pltpu.make_async_remote_copy(src_ref=acc.at[0], dst_ref=theirs.at[k], send_sem=sa.at[k], recv_sem=ra.at[k],
                                                device_id=sib, device_id_type=MESH)

        def b_copy(k):
            px, py = peers[k]
            return pltpu.make_async_remote_copy(src_ref=staged.at[k], dst_ref=contrib.at[k], send_sem=sb.at[k], recv_sem=rb.at[k],
                                                device_id=(px, py, c), device_id_type=MESH)

        def c_copy(which):
            return pltpu.make_async_remote_copy(src_ref=resbuf.at[which], dst_ref=resbuf.at[which], send_sem=sc, recv_sem=rc,
                                                device_id=sib, device_id_type=MESH)

        @pl.when(s == 0)
        def _():
            for k in range(N_CHIPS - 1):
                @pl.when(p == 2 * k + 2)
                def _():
                    a_copy(k).wait_send()
            acc[slot] = jnp.dot(ht_ref[...], dz_ref[...], preferred_element_type=F32)

        @pl.when(s > 0)
        def _():
            acc[slot] += jnp.dot(ht_ref[...], dz_ref[...], preferred_element_type=F32)

        @pl.when(s == steps - 1)
        def _():
            for k in range(N_CHIPS):
                @pl.when(p == 2 * k)
                def _():
                    a_copy(k).start()
            for k in range(N_CHIPS - 1):
                @pl.when(p == 2 * k + 1)
                def _():
                    a_copy(k).wait_recv()
                    staged[k] = (acc[1] + theirs[k]).astype(BF16)
                    b_copy(k).start()

            @pl.when(p == N_DEV - 1)
            def _():
                a_copy(N_CHIPS - 1).wait_recv()
                tot = acc[1] + theirs[N_CHIPS - 1]
                for k in range(N_CHIPS - 1):
                    b_copy(k).wait_recv()
                    tot = tot + contrib[k].astype(F32)
                resbuf[c] = tot
                c_copy(c).start()
                done_mine = pltpu.make_async_copy(resbuf.at[c], out_ref.at[c], out_sem)
                done_mine.start()
                oc_copy(1 - c).wait_recv()
                ro_ref[...] = o_res[...]
                for k in range(N_DEV - 1):
                    px, py, pc = small_peer(k)
                    small_copy(k, 4 * px + 2 * py + pc, (px, py, pc)).wait_recv()
                tot8 = sbuf[0]
                for d in range(1, N_DEV):
                    tot8 = tot8 + sbuf[d]
                rs_ref[...] = tot8
                c_copy(1 - c).wait_recv()
                done = pltpu.make_async_copy(resbuf.at[1 - c], out_ref.at[1 - c], out_sem)
                done.start()
                a_copy(N_CHIPS - 1).wait_send()
                for k in range(N_CHIPS - 1):
                    b_copy(k).wait_send()
                    ob_copy(k, chip_ids[k], me).wait_send()
                c_copy(c).wait_send()
                oc_copy(c).wait_send()
                for pos in range(N_CHIPS):
                    oa_copy(pos).wait_send()
                for k in range(N_DEV - 1):
                    small_copy(k, my8, small_peer(k)).wait_send()
                done_mine.wait()
                done.wait()

    dma = pltpu.SemaphoreType.DMA
    o_shape = g_out.shape[1:]
    go = g_out.reshape(N_CHIPS, 2, *o_shape)
    const = lambda shape: pl.BlockSpec(shape, lambda p, s, order_ref: (0,) * len(shape))
    grid_spec = pltpu.PrefetchScalarGridSpec(
        num_scalar_prefetch=1, grid=(N_DEV, steps),
        in_specs=[pl.BlockSpec((d_model, ts), lambda p, s, order_ref: (0, s)),
                  pl.BlockSpec((ts, half), lambda p, s, order_ref: (s, order_ref[p])),
                  pl.BlockSpec(memory_space=pl.ANY), const(small.shape)],
        out_specs=(pl.BlockSpec(memory_space=pl.ANY), const((2, *o_shape)), const(small.shape)),
        scratch_shapes=[pltpu.VMEM((2, d_model, half), F32), pltpu.VMEM((N_CHIPS, d_model, half), F32),
                        pltpu.VMEM((N_CHIPS - 1, d_model, half), BF16), pltpu.VMEM((N_CHIPS - 1, d_model, half), BF16),
                        pltpu.VMEM((2, d_model, half), F32), dma,
                        dma((N_CHIPS,)), dma((N_CHIPS,)), dma((N_CHIPS - 1,)), dma((N_CHIPS - 1,)), dma, dma,
                        pltpu.VMEM((N_CHIPS, *o_shape), F32), pltpu.VMEM((N_CHIPS, *o_shape), F32),
                        pltpu.VMEM((N_CHIPS, *o_shape), BF16), pltpu.VMEM((N_CHIPS, *o_shape), BF16),
                        pltpu.VMEM((2, *o_shape), F32), pltpu.VMEM((N_DEV, *small.shape), F32),
                        dma((N_CHIPS,)), dma((N_CHIPS,)), dma((N_CHIPS,)), dma((N_CHIPS - 1,)), dma((N_CHIPS - 1,)), dma, dma,
                        dma((N_DEV - 1,)), dma((N_DEV - 1,))])
    return pl.pallas_call(
        body, name="dw_in_reduce", grid_spec=grid_spec,
        out_shape=(jax.ShapeDtypeStruct((2, d_model, half), F32), jax.ShapeDtypeStruct((2, *o_shape), F32),
                   jax.ShapeDtypeStruct(small.shape, F32)),
        compiler_params=_params(("arbitrary", "arbitrary")),
    )(order, ht, dz, go, small)


def _adam_math(w, g, m, v):
    m = ADAM_B1 * m + (1.0 - ADAM_B1) * g
    v = ADAM_B2 * v + (1.0 - ADAM_B2) * (g * g)
    m_hat = m / (1.0 - ADAM_B1 ** ADAM_STEP)
    v_hat = v / (1.0 - ADAM_B2 ** ADAM_STEP)
    delta = -ADAM_LR * (m_hat / (jnp.sqrt(v_hat) + ADAM_EPS) + ADAM_WD * w)
    return delta, m, v


def adam_shard(name, w, g2, m, v, block, grid, w_map, g_map):
    def body(w_ref, g_ref, m_ref, v_ref, go_ref, d_ref, mo_ref, vo_ref):
        g = g_ref[0]
        delta, mn, vn = _adam_math(w_ref[...], g, m_ref[...], v_ref[...])
        go_ref[...] = g
        d_ref[...] = delta
        mo_ref[...] = mn
        vo_ref[...] = vn

    ws = pl.BlockSpec(block, w_map)
    shp = jax.ShapeDtypeStruct(w.shape, F32)
    return pl.pallas_call(
        body, name=name, grid=grid, out_shape=(shp, shp, shp, shp),
        in_specs=[ws, pl.BlockSpec((1, *block), g_map), ws, ws], out_specs=(ws, ws, ws, ws),
        compiler_params=_params(("arbitrary",) * len(grid)),
    )(w, g2, m, v)


def adam_small(ws, gs, ms, vs):
    n = len(ws)

    def body(*refs):
        ins, outs = refs[:4 * n], refs[4 * n:]
        for t in range(n):
            delta, mn, vn = _adam_math(ins[t][...], ins[n + t][...], ins[2 * n + t][...], ins[3 * n + t][...])
            outs[3 * t][...] = delta
            outs[3 * t + 1][...] = mn
            outs[3 * t + 2][...] = vn

    vm = pl.BlockSpec(memory_space=pltpu.VMEM)
    outs = pl.pallas_call(
        body, name="adam_small",
        out_shape=tuple(jax.ShapeDtypeStruct(w.shape, F32) for w in ws for _ in range(3)),
        in_specs=[vm] * (4 * n), out_specs=tuple([vm] * (3 * n)),
        compiler_params=_params(),
    )(*ws, *gs, *ms, *vs)
    return [outs[3 * t:3 * t + 3] for t in range(n)]


def kernel(x, norm_pre_g, w_in, conv_w, w_out, norm_post_g, loss_target, m_norm_pre_g, m_w_in, m_conv_w, m_w_out, m_norm_post_g, v_norm_pre_g, v_w_in, v_conv_w, v_w_out, v_norm_post_g):
    _, seq, d_model = x.shape
    width = w_in.shape[1]
    conv_q = conv_w.shape[1]
    conv_width = N_CHIPS * conv_q
    attn_width = d_model - conv_width
    xs, tg = x[0], loss_target[0]
    g1, g2 = norm_pre_g.reshape(1, d_model), norm_post_g.reshape(1, d_model)

    w_full, wout_full, cw_full, *tables = gather_weights(w_in, w_out, conv_w, seq)
    wout2 = wout_full.reshape(attn_width + conv_width, d_model)
    cw = jnp.zeros((SUBLANES, conv_width), F32).at[:CONV_K].set(
        cw_full[:, :CONV_K, :conv_q].transpose(1, 0, 2).reshape(CONV_K, conv_width))

    ht, q, k, v, qp, kp, vp, ga, cz = inproj(xs, g1, w_full, tables, attn_width, conv_width)
    run = attn_fwd("p4", qp, kp, vp, None)
    run = attn_fwd("p16", qp, kp, vp, run)
    o, lse = attn_fwd("nat", q, k, v, run)
    (d_o, delta, d_op, delta_p, lse_p, dga, dcb, dgc, dcv, e, dwout, dg2, dcw, loss_acc) = tail(
        o, lse, ga, cz, xs, tg, wout2, g2, cw)
    nat_grads = attn_bwd("nat", q, k, v, d_o, lse, delta, None)
    perm_grads = attn_bwd("p4", qp, kp, vp, d_op, lse_p, delta_p, None)
    perm_grads = attn_bwd("p16", qp, kp, vp, d_op, lse_p, delta_p, perm_grads)
    grad_x, dz, dg1 = dz_dx(nat_grads, perm_grads, dga, dcb, dgc, dcv, cz, tables, xs, g1, e, w_full, cw)

    small = jnp.zeros((SUBLANES, d_model), F32)
    small = small.at[0].set(dg1.sum(axis=0)).at[1].set(dg2.sum(axis=0))
    small = small.at[2:2 + CONV_K, :conv_width].set(dcw.reshape(CONV_K, SUBLANES, conv_width).sum(axis=1))
    small = small.at[2 + CONV_K, 0].set(jnp.sum(loss_acc))
    rin, rout, rsmall = dw_in_reduce(ht, dz, dwout.reshape(N_DEV, -1, d_model), small)

    half = width // 2
    tr = min(ADAM_ROWS, d_model)
    gw_in, d_in, m_in, v_in = adam_shard(
        "adam_w_in", w_in, rin, m_w_in, v_w_in, (tr, half), (2, d_model // tr),
        lambda hf, i: (i, hf), lambda hf, i: (hf, i, 0))
    rq = w_out.shape[0] // 2
    gw_out, d_out, m_out, v_out = adam_shard(
        "adam_w_out", w_out, rout, m_w_out, v_w_out, (rq, d_model), (2,),
        lambda hf: (hf, 0), lambda hf: (hf, 0, 0))

    chip = 2 * lax.axis_index("x") + lax.axis_index("y")
    g_pre, g_post = rsmall[0:1], rsmall[1:2]
    g_conv = lax.dynamic_slice(rsmall[2:2 + CONV_K, :conv_width], (0, chip * conv_q), (CONV_K, conv_q))
    (d_pre, m_pre, v_pre), (d_post, m_post, v_post), (d_cv, m_cv, v_cv) = adam_small(
        [g1, g2, conv_w], [g_pre, g_post, g_conv],
        [m_norm_pre_g.reshape(1, d_model), m_norm_post_g.reshape(1, d_model), m_conv_w],
        [v_norm_pre_g.reshape(1, d_model), v_norm_post_g.reshape(1, d_model), v_conv_w])

    loss = 0.5 * rsmall[2 + CONV_K, 0] / d_model
    vec = lambda a: a.reshape(d_model)
    return (loss, grad_x.reshape(1, seq, d_model),
            vec(g_pre), gw_in, g_conv, gw_out, vec(g_post),
            vec(d_pre), d_in, d_cv, d_out, vec(d_post),
            vec(m_pre), m_in, m_cv, m_out, vec(m_post),
            vec(v_pre), v_in, v_cv, v_out, vec(v_post))
```
